```python
import math
import jax, jax.numpy as jnp
from jax import lax
import numpy as np

D_MODEL = 1024
BATCH = 8
SEQ = 4096
DEPTH = 1

CHUNK = 64
N_HEADS = 16
N_KV_HEADS = 4
HEAD_DIM = 64
WINDOW = 128
WINDOW_CHUNKS = WINDOW // CHUNK
N_BUCKETS = 32
MAX_DISTANCE = 128
LRU_WIDTH = D_MODEL
LRU_BLOCKS = 16
LRU_BLOCK = LRU_WIDTH // LRU_BLOCKS
CONV_WIDTH = 4
LRU_C = 8.0
D_FF = 2816
RMS_EPS = 1e-6
Q_W = N_HEADS * HEAD_DIM
KV_W = N_KV_HEADS * HEAD_DIM
IN_W = Q_W + 2 * KV_W + 2 * LRU_WIDTH
N_BRANCHES = 2
NEG_INF = -1e30

kernel_name = "hybrid_rglru_swa_sink_macaron"


def rms_norm(x, g):
    xf = x.astype(jnp.float32)
    y = xf * lax.rsqrt(jnp.mean(xf * xf, axis=-1, keepdims=True) + RMS_EPS)
    return (y * g.astype(jnp.float32)).astype(x.dtype)


def swiglu(x, w1, w3, w2):
    return (jax.nn.silu(x @ w1) * (x @ w3)) @ w2


def t5_bucket(rel):
    nb = N_BUCKETS // 2
    max_exact = nb // 2
    ret = jnp.where(rel > 0, nb, 0)
    n = jnp.abs(rel)
    nf = jnp.maximum(n, 1).astype(jnp.float32)
    large = max_exact + (jnp.log(nf / max_exact) / math.log(MAX_DISTANCE / max_exact)
                         * (nb - max_exact)).astype(jnp.int32)
    large = jnp.minimum(large, nb - 1)
    return ret + jnp.where(n < max_exact, n, large)


def band_rel_bias(table):
    kb = (WINDOW_CHUNKS + 1) * CHUNK
    i = jnp.arange(CHUNK)[:, None]
    j = jnp.arange(kb)[None, :]
    buckets = t5_bucket(j - WINDOW_CHUNKS * CHUNK - i)
    return jnp.transpose(table[buckets], (2, 0, 1))


def causal_conv(x, w, b):
    s = x.shape[1]
    xp = jnp.pad(x, ((0, 0), (CONV_WIDTH - 1, 0), (0, 0)))
    y = b
    for tap in range(CONV_WIDTH):
        y = y + xp[:, tap:tap + s] * w[tap]
    return y


def rg_lru(x, wa, ba, wx, bx, lam):
    b, s, _ = x.shape
    xf = x.astype(jnp.float32)
    xb = xf.reshape(b, s, LRU_BLOCKS, LRU_BLOCK)
    r = jax.nn.sigmoid(jnp.einsum('bshi,hij->bshj', xb, wa.astype(jnp.float32)).reshape(b, s, LRU_WIDTH) + ba)
    ig = jax.nn.sigmoid(jnp.einsum('bshi,hij->bshj', xb, wx.astype(jnp.float32)).reshape(b, s, LRU_WIDTH) + bx)
    log_a = -LRU_C * r * jax.nn.softplus(-lam.astype(jnp.float32))
    a = jnp.exp(log_a)
    u = jnp.sqrt(-jnp.expm1(2.0 * log_a)) * (ig * xf)

    def combine(c1, c2):
        a1, b1 = c1
        a2, b2 = c2
        return a1 * a2, a2 * b1 + b2

    _, h = lax.associative_scan(combine, (a, u), axis=1)
    return h.astype(x.dtype)


def swa_sink_attention(q, k, v, sinks, bias):
    b, s, _ = q.shape
    nc = s // CHUNK
    kb = (WINDOW_CHUNKS + 1) * CHUNK
    rep = N_HEADS // N_KV_HEADS
    qc = q.reshape(b, nc, CHUNK, N_KV_HEADS, rep, HEAD_DIM)
    pad = ((0, 0), (WINDOW_CHUNKS * CHUNK, 0), (0, 0))
    kp = jnp.pad(k, pad).reshape(b, nc + WINDOW_CHUNKS, CHUNK, N_KV_HEADS, HEAD_DIM)
    vp = jnp.pad(v, pad).reshape(b, nc + WINDOW_CHUNKS, CHUNK, N_KV_HEADS, HEAD_DIM)
    kband = jnp.concatenate([kp[:, w:w + nc] for w in range(WINDOW_CHUNKS + 1)], axis=2)
    vband = jnp.concatenate([vp[:, w:w + nc] for w in range(WINDOW_CHUNKS + 1)], axis=2)
    sc = jnp.einsum('bnqgrd,bnkgd->bngrqk', qc, kband).astype(jnp.float32) * (HEAD_DIM ** -0.5)
    sc = sc + bias.astype(jnp.float32).reshape(N_KV_HEADS, rep, CHUNK, kb)
    key_chunk = jnp.arange(nc)[:, None] - WINDOW_CHUNKS + jnp.arange(kb)[None, :] // CHUNK
    valid = key_chunk >= 0
    sc = jnp.where(valid[None, :, None, None, None, :], sc, NEG_INF)
    sink = jnp.broadcast_to(sinks.astype(jnp.float32).reshape(1, 1, N_KV_HEADS, rep, 1, 1),
                            sc.shape[:-1] + (1,))
    p = jax.nn.softmax(jnp.concatenate([sc, sink], axis=-1), axis=-1)[..., :-1]
    o = jnp.einsum('bngrqk,bnkgd->bnqgrd', p.astype(vband.dtype), vband)
    return o.reshape(b, s, Q_W)


def _fwd_setup_inputs(seed: int = 0) -> dict:
    key = jax.random.key(seed)
    ks = jax.random.split(key, 32)
    f32 = jnp.float32
    L, D = DEPTH, D_MODEL

    def nrm(k, shape, scale):
        return jax.random.normal(k, shape, f32) * scale

    def gain(k):
        return 1.0 + 0.05 * jax.random.normal(k, (L, D), f32)

    a0 = jax.random.uniform(ks[10], (L, LRU_WIDTH), f32, 0.9, 0.999)
    return {
        "x": jax.random.normal(ks[0], (BATCH, SEQ, D), f32),
        "ffn1_pre_g": gain(ks[1]),
        "ffn1_w1": nrm(ks[2], (L, D, D_FF), D ** -0.5),
        "ffn1_w3": nrm(ks[3], (L, D, D_FF), D ** -0.5),
        "ffn1_w2": nrm(ks[4], (L, D_FF, D), D_FF ** -0.5),
        "ffn1_post_g": gain(ks[5]),
        "mix_pre_g": gain(ks[6]),
        "w_in": nrm(ks[7], (L, D, IN_W), D ** -0.5),
        "conv_w": nrm(ks[8], (L, CONV_WIDTH, LRU_WIDTH), CONV_WIDTH ** -0.5),
        "conv_b": nrm(ks[9], (L, LRU_WIDTH), 0.02),
        "rg_a_w": nrm(ks[11], (L, LRU_BLOCKS, LRU_BLOCK, LRU_BLOCK), LRU_BLOCK ** -0.5),
        "rg_a_b": nrm(ks[12], (L, LRU_WIDTH), 0.1),
        "rg_x_w": nrm(ks[13], (L, LRU_BLOCKS, LRU_BLOCK, LRU_BLOCK), LRU_BLOCK ** -0.5),
        "rg_x_b": nrm(ks[14], (L, LRU_WIDTH), 0.1),
        "lru_lambda": jnp.log(a0) - jnp.log1p(-a0),
        "w_lru_out": nrm(ks[15], (L, LRU_WIDTH, D), LRU_WIDTH ** -0.5),
        "attn_sinks": nrm(ks[16], (L, N_HEADS), 0.5),
        "rel_bias": nrm(ks[17], (N_BUCKETS, N_HEADS), 0.5),
        "w_attn_out": nrm(ks[18], (L, Q_W, D), Q_W ** -0.5),
        "w_gate": nrm(ks[19], (L, D, N_BRANCHES * D), D ** -0.5),
        "b_gate": nrm(ks[20], (L, N_BRANCHES * D), 0.1),
        "w_o": nrm(ks[21], (L, D, D), D ** -0.5),
        "mix_post_g": gain(ks[22]),
        "ffn2_pre_g": gain(ks[23]),
        "ffn2_w1": nrm(ks[24], (L, D, D_FF), D ** -0.5),
        "ffn2_w3": nrm(ks[25], (L, D, D_FF), D ** -0.5),
        "ffn2_w2": nrm(ks[26], (L, D_FF, D), D_FF ** -0.5),
        "ffn2_post_g": gain(ks[27]),
    }


def _fwd_reference(x, ffn1_pre_g, ffn1_w1, ffn1_w3, ffn1_w2, ffn1_post_g, mix_pre_g, w_in, conv_w, conv_b,
              rg_a_w, rg_a_b, rg_x_w, rg_x_b, lru_lambda, w_lru_out, attn_sinks, rel_bias, w_attn_out,
              w_gate, b_gate, w_o, mix_post_g, ffn2_pre_g, ffn2_w1, ffn2_w3, ffn2_w2, ffn2_post_g):
    b, s, d = x.shape
    bias = band_rel_bias(rel_bias)
    splits = [Q_W, Q_W + KV_W, Q_W + 2 * KV_W, Q_W + 2 * KV_W + LRU_WIDTH]
    h = x
    for l in range(DEPTH):
        f = swiglu(rms_norm(h, ffn1_pre_g[l]), ffn1_w1[l], ffn1_w3[l], ffn1_w2[l])
        h = h + 0.5 * rms_norm(f, ffn1_post_g[l])
        u = rms_norm(h, mix_pre_g[l])
        q, k, v, xr, xg = jnp.split(u @ w_in[l], splits, axis=-1)
        xr = causal_conv(xr, conv_w[l], conv_b[l])
        hr = rg_lru(xr, rg_a_w[l], rg_a_b[l], rg_x_w[l], rg_x_b[l], lru_lambda[l])
        y_a = (hr * jax.nn.gelu(xg)) @ w_lru_out[l]
        y_b = swa_sink_attention(q, k, v, attn_sinks[l], bias) @ w_attn_out[l]
        g = jax.nn.sigmoid(u @ w_gate[l] + b_gate[l]).reshape(b, s, N_BRANCHES, d)
        merged = g[:, :, 0] * y_a + g[:, :, 1] * y_b
        h = h + rms_norm(merged @ w_o[l], mix_post_g[l])
        f = swiglu(rms_norm(h, ffn2_pre_g[l]), ffn2_w1[l], ffn2_w3[l], ffn2_w2[l])
        h = h + 0.5 * rms_norm(f, ffn2_post_g[l])
    return h


import jax as _jax
import jax.numpy as _jnp

TWIN_FORMAT = 'train_step'
FWD_PARAMS = ['x', 'ffn1_pre_g', 'ffn1_w1', 'ffn1_w3', 'ffn1_w2', 'ffn1_post_g', 'mix_pre_g', 'w_in', 'conv_w', 'conv_b', 'rg_a_w', 'rg_a_b', 'rg_x_w', 'rg_x_b', 'lru_lambda', 'w_lru_out', 'attn_sinks', 'rel_bias', 'w_attn_out', 'w_gate', 'b_gate', 'w_o', 'mix_post_g', 'ffn2_pre_g', 'ffn2_w1', 'ffn2_w3', 'ffn2_w2', 'ffn2_post_g']
TWIN_WEIGHTS = ['ffn1_pre_g', 'ffn1_w1', 'ffn1_w3', 'ffn1_w2', 'ffn1_post_g', 'mix_pre_g', 'w_in', 'conv_w', 'conv_b', 'rg_a_w', 'rg_a_b', 'rg_x_w', 'rg_x_b', 'lru_lambda', 'w_lru_out', 'attn_sinks', 'rel_bias', 'w_attn_out', 'w_gate', 'b_gate', 'w_o', 'mix_post_g', 'ffn2_pre_g', 'ffn2_w1', 'ffn2_w3', 'ffn2_w2', 'ffn2_post_g']
TWIN_DIFF_INPUT = 'x'
TWIN_INPUTS = ['x', 'ffn1_pre_g', 'ffn1_w1', 'ffn1_w3', 'ffn1_w2', 'ffn1_post_g', 'mix_pre_g', 'w_in', 'conv_w', 'conv_b', 'rg_a_w', 'rg_a_b', 'rg_x_w', 'rg_x_b', 'lru_lambda', 'w_lru_out', 'attn_sinks', 'rel_bias', 'w_attn_out', 'w_gate', 'b_gate', 'w_o', 'mix_post_g', 'ffn2_pre_g', 'ffn2_w1', 'ffn2_w3', 'ffn2_w2', 'ffn2_post_g', 'loss_target', 'm_ffn1_pre_g', 'm_ffn1_w1', 'm_ffn1_w3', 'm_ffn1_w2', 'm_ffn1_post_g', 'm_mix_pre_g', 'm_w_in', 'm_conv_w', 'm_conv_b', 'm_rg_a_w', 'm_rg_a_b', 'm_rg_x_w', 'm_rg_x_b', 'm_lru_lambda', 'm_w_lru_out', 'm_attn_sinks', 'm_rel_bias', 'm_w_attn_out', 'm_w_gate', 'm_b_gate', 'm_w_o', 'm_mix_post_g', 'm_ffn2_pre_g', 'm_ffn2_w1', 'm_ffn2_w3', 'm_ffn2_w2', 'm_ffn2_post_g', 'v_ffn1_pre_g', 'v_ffn1_w1', 'v_ffn1_w3', 'v_ffn1_w2', 'v_ffn1_post_g', 'v_mix_pre_g', 'v_w_in', 'v_conv_w', 'v_conv_b', 'v_rg_a_w', 'v_rg_a_b', 'v_rg_x_w', 'v_rg_x_b', 'v_lru_lambda', 'v_w_lru_out', 'v_attn_sinks', 'v_rel_bias', 'v_w_attn_out', 'v_w_gate', 'v_b_gate', 'v_w_o', 'v_mix_post_g', 'v_ffn2_pre_g', 'v_ffn2_w1', 'v_ffn2_w3', 'v_ffn2_w2', 'v_ffn2_post_g']
TWIN_OUTPUTS = ['loss', 'grad_x', 'grad_ffn1_pre_g', 'grad_ffn1_w1', 'grad_ffn1_w3', 'grad_ffn1_w2', 'grad_ffn1_post_g', 'grad_mix_pre_g', 'grad_w_in', 'grad_conv_w', 'grad_conv_b', 'grad_rg_a_w', 'grad_rg_a_b', 'grad_rg_x_w', 'grad_rg_x_b', 'grad_lru_lambda', 'grad_w_lru_out', 'grad_attn_sinks', 'grad_rel_bias', 'grad_w_attn_out', 'grad_w_gate', 'grad_b_gate', 'grad_w_o', 'grad_mix_post_g', 'grad_ffn2_pre_g', 'grad_ffn2_w1', 'grad_ffn2_w3', 'grad_ffn2_w2', 'grad_ffn2_post_g', 'delta_ffn1_pre_g', 'delta_ffn1_w1', 'delta_ffn1_w3', 'delta_ffn1_w2', 'delta_ffn1_post_g', 'delta_mix_pre_g', 'delta_w_in', 'delta_conv_w', 'delta_conv_b', 'delta_rg_a_w', 'delta_rg_a_b', 'delta_rg_x_w', 'delta_rg_x_b', 'delta_lru_lambda', 'delta_w_lru_out', 'delta_attn_sinks', 'delta_rel_bias', 'delta_w_attn_out', 'delta_w_gate', 'delta_b_gate', 'delta_w_o', 'delta_mix_post_g', 'delta_ffn2_pre_g', 'delta_ffn2_w1', 'delta_ffn2_w3', 'delta_ffn2_w2', 'delta_ffn2_post_g', 'new_m_ffn1_pre_g', 'new_m_ffn1_w1', 'new_m_ffn1_w3', 'new_m_ffn1_w2', 'new_m_ffn1_post_g', 'new_m_mix_pre_g', 'new_m_w_in', 'new_m_conv_w', 'new_m_conv_b', 'new_m_rg_a_w', 'new_m_rg_a_b', 'new_m_rg_x_w', 'new_m_rg_x_b', 'new_m_lru_lambda', 'new_m_w_lru_out', 'new_m_attn_sinks', 'new_m_rel_bias', 'new_m_w_attn_out', 'new_m_w_gate', 'new_m_b_gate', 'new_m_w_o', 'new_m_mix_post_g', 'new_m_ffn2_pre_g', 'new_m_ffn2_w1', 'new_m_ffn2_w3', 'new_m_ffn2_w2', 'new_m_ffn2_post_g', 'new_v_ffn1_pre_g', 'new_v_ffn1_w1', 'new_v_ffn1_w3', 'new_v_ffn1_w2', 'new_v_ffn1_post_g', 'new_v_mix_pre_g', 'new_v_w_in', 'new_v_conv_w', 'new_v_conv_b', 'new_v_rg_a_w', 'new_v_rg_a_b', 'new_v_rg_x_w', 'new_v_rg_x_b', 'new_v_lru_lambda', 'new_v_w_lru_out', 'new_v_attn_sinks', 'new_v_rel_bias', 'new_v_w_attn_out', 'new_v_w_gate', 'new_v_b_gate', 'new_v_w_o', 'new_v_mix_post_g', 'new_v_ffn2_pre_g', 'new_v_ffn2_w1', 'new_v_ffn2_w3', 'new_v_ffn2_w2', 'new_v_ffn2_post_g']
TWIN_LEAF_KINDS = {'loss': 'loss', 'grad_x': 'grad_x', 'grad_ffn1_pre_g': 'grad_w', 'grad_ffn1_w1': 'grad_w', 'grad_ffn1_w3': 'grad_w', 'grad_ffn1_w2': 'grad_w', 'grad_ffn1_post_g': 'grad_w', 'grad_mix_pre_g': 'grad_w', 'grad_w_in': 'grad_w', 'grad_conv_w': 'grad_w', 'grad_conv_b': 'grad_w', 'grad_rg_a_w': 'grad_w', 'grad_rg_a_b': 'grad_w', 'grad_rg_x_w': 'grad_w', 'grad_rg_x_b': 'grad_w', 'grad_lru_lambda': 'grad_w', 'grad_w_lru_out': 'grad_w', 'grad_attn_sinks': 'grad_w', 'grad_rel_bias': 'grad_w', 'grad_w_attn_out': 'grad_w', 'grad_w_gate': 'grad_w', 'grad_b_gate': 'grad_w', 'grad_w_o': 'grad_w', 'grad_mix_post_g': 'grad_w', 'grad_ffn2_pre_g': 'grad_w', 'grad_ffn2_w1': 'grad_w', 'grad_ffn2_w3': 'grad_w', 'grad_ffn2_w2': 'grad_w', 'grad_ffn2_post_g': 'grad_w', 'delta_ffn1_pre_g': 'delta_w', 'delta_ffn1_w1': 'delta_w', 'delta_ffn1_w3': 'delta_w', 'delta_ffn1_w2': 'delta_w', 'delta_ffn1_post_g': 'delta_w', 'delta_mix_pre_g': 'delta_w', 'delta_w_in': 'delta_w', 'delta_conv_w': 'delta_w', 'delta_conv_b': 'delta_w', 'delta_rg_a_w': 'delta_w', 'delta_rg_a_b': 'delta_w', 'delta_rg_x_w': 'delta_w', 'delta_rg_x_b': 'delta_w', 'delta_lru_lambda': 'delta_w', 'delta_w_lru_out': 'delta_w', 'delta_attn_sinks': 'delta_w', 'delta_rel_bias': 'delta_w', 'delta_w_attn_out': 'delta_w', 'delta_w_gate': 'delta_w', 'delta_b_gate': 'delta_w', 'delta_w_o': 'delta_w', 'delta_mix_post_g': 'delta_w', 'delta_ffn2_pre_g': 'delta_w', 'delta_ffn2_w1': 'delta_w', 'delta_ffn2_w3': 'delta_w', 'delta_ffn2_w2': 'delta_w', 'delta_ffn2_post_g': 'delta_w', 'new_m_ffn1_pre_g': 'new_m', 'new_m_ffn1_w1': 'new_m', 'new_m_ffn1_w3': 'new_m', 'new_m_ffn1_w2': 'new_m', 'new_m_ffn1_post_g': 'new_m', 'new_m_mix_pre_g': 'new_m', 'new_m_w_in': 'new_m', 'new_m_conv_w': 'new_m', 'new_m_conv_b': 'new_m', 'new_m_rg_a_w': 'new_m', 'new_m_rg_a_b': 'new_m', 'new_m_rg_x_w': 'new_m', 'new_m_rg_x_b': 'new_m', 'new_m_lru_lambda': 'new_m', 'new_m_w_lru_out': 'new_m', 'new_m_attn_sinks': 'new_m', 'new_m_rel_bias': 'new_m', 'new_m_w_attn_out': 'new_m', 'new_m_w_gate': 'new_m', 'new_m_b_gate': 'new_m', 'new_m_w_o': 'new_m', 'new_m_mix_post_g': 'new_m', 'new_m_ffn2_pre_g': 'new_m', 'new_m_ffn2_w1': 'new_m', 'new_m_ffn2_w3': 'new_m', 'new_m_ffn2_w2': 'new_m', 'new_m_ffn2_post_g': 'new_m', 'new_v_ffn1_pre_g': 'new_v', 'new_v_ffn1_w1': 'new_v', 'new_v_ffn1_w3': 'new_v', 'new_v_ffn1_w2': 'new_v', 'new_v_ffn1_post_g': 'new_v', 'new_v_mix_pre_g': 'new_v', 'new_v_w_in': 'new_v', 'new_v_conv_w': 'new_v', 'new_v_conv_b': 'new_v', 'new_v_rg_a_w': 'new_v', 'new_v_rg_a_b': 'new_v', 'new_v_rg_x_w': 'new_v', 'new_v_rg_x_b': 'new_v', 'new_v_lru_lambda': 'new_v', 'new_v_w_lru_out': 'new_v', 'new_v_attn_sinks': 'new_v', 'new_v_rel_bias': 'new_v', 'new_v_w_attn_out': 'new_v', 'new_v_w_gate': 'new_v', 'new_v_b_gate': 'new_v', 'new_v_w_o': 'new_v', 'new_v_mix_post_g': 'new_v', 'new_v_ffn2_pre_g': 'new_v', 'new_v_ffn2_w1': 'new_v', 'new_v_ffn2_w3': 'new_v', 'new_v_ffn2_w2': 'new_v', 'new_v_ffn2_post_g': 'new_v'}


def _forward(args):
    return _fwd_reference(*[args[k] for k in FWD_PARAMS])


def _output_shape():
    def fwd():
        inp = _fwd_setup_inputs(0)
        return _fwd_reference(*[inp[k] for k in FWD_PARAMS])
    out = _jax.eval_shape(fwd)
    return out.shape, out.dtype

N_MICROBATCH = 1
ADAM_LR = 0.001
ADAM_B1 = 0.9
ADAM_B2 = 0.999
ADAM_EPS = 1e-08
ADAM_WD = 0.01
ADAM_STEP = 10
PER_EXAMPLE_BATCH_AXIS = {'x': 0, 'loss_target': 0}
SHARED_INPUTS = []
_WEIGHT_DTYPES = {'ffn1_pre_g': _jnp.float32, 'ffn1_w1': _jnp.float32, 'ffn1_w3': _jnp.float32, 'ffn1_w2': _jnp.float32, 'ffn1_post_g': _jnp.float32, 'mix_pre_g': _jnp.float32, 'w_in': _jnp.float32, 'conv_w': _jnp.float32, 'conv_b': _jnp.float32, 'rg_a_w': _jnp.float32, 'rg_a_b': _jnp.float32, 'rg_x_w': _jnp.float32, 'rg_x_b': _jnp.float32, 'lru_lambda': _jnp.float32, 'w_lru_out': _jnp.float32, 'attn_sinks': _jnp.float32, 'rel_bias': _jnp.float32, 'w_attn_out': _jnp.float32, 'w_gate': _jnp.float32, 'b_gate': _jnp.float32, 'w_o': _jnp.float32, 'mix_post_g': _jnp.float32, 'ffn2_pre_g': _jnp.float32, 'ffn2_w1': _jnp.float32, 'ffn2_w3': _jnp.float32, 'ffn2_w2': _jnp.float32, 'ffn2_post_g': _jnp.float32}
MOMENT_SCALE = {'ffn1_pre_g': 4.732133e-01, 'ffn1_w1': 1.816923e-01, 'ffn1_w3': 1.910318e-01, 'ffn1_w2': 3.189910e-01, 'ffn1_post_g': 7.985652e+00, 'mix_pre_g': 5.599410e-01, 'w_in': 2.805209e-01, 'conv_w': 3.848132e-01, 'conv_b': 4.171486e+00, 'rg_a_w': 1.340485e-01, 'rg_a_b': 9.267493e-02, 'rg_x_w': 2.474640e-01, 'rg_x_b': 1.443324e-01, 'lru_lambda': 1.769804e-01, 'w_lru_out': 3.924681e-01, 'attn_sinks': 1.084998e-02, 'rel_bias': 1.721954e-01, 'w_attn_out': 1.426126e-01, 'w_gate': 9.133490e-02, 'b_gate': 1.193447e-01, 'w_o': 4.309968e-01, 'mix_post_g': 3.212943e+01, 'ffn2_pre_g': 3.331899e-01, 'ffn2_w1': 1.084233e-01, 'ffn2_w3': 1.787711e-01, 'ffn2_w2': 2.955771e-01, 'ffn2_post_g': 7.985172e+00}


def _to_microbatches(a, axis):
    t = _jnp.moveaxis(a, axis, 0)
    t = t.reshape((N_MICROBATCH, t.shape[0] // N_MICROBATCH) + t.shape[1:])
    return _jnp.moveaxis(t, 1, axis + 1)


def setup_inputs(seed: int = 0) -> dict:
    inp = _fwd_setup_inputs(seed)
    key = _jax.random.fold_in(_jax.random.key(seed), 7919)
    shape, _ = _output_shape()
    out = dict(inp)
    out["loss_target"] = _jax.random.normal(_jax.random.fold_in(key, 0), shape, _jnp.float32)
    for i, name in enumerate(TWIN_WEIGHTS):
        w = inp[name].astype(_jnp.float32)
        if MOMENT_SCALE is None:
            s = _jnp.sqrt(_jnp.mean(_jnp.square(w)) + 1e-30)
        else:
            s = MOMENT_SCALE[name]
        km, kv = _jax.random.split(_jax.random.fold_in(key, i + 1))
        out[name] = w
        out["m_" + name] = s * _jax.random.normal(km, w.shape, _jnp.float32)
        out["v_" + name] = (s * s) * _jax.random.uniform(kv, w.shape, _jnp.float32, 0.5, 1.5)
    if N_MICROBATCH > 1:
        for name, axis in PER_EXAMPLE_BATCH_AXIS.items():
            out[name] = _to_microbatches(out[name], axis)
    return {'x': out['x'], 'ffn1_pre_g': out['ffn1_pre_g'], 'ffn1_w1': out['ffn1_w1'], 'ffn1_w3': out['ffn1_w3'], 'ffn1_w2': out['ffn1_w2'], 'ffn1_post_g': out['ffn1_post_g'], 'mix_pre_g': out['mix_pre_g'], 'w_in': out['w_in'], 'conv_w': out['conv_w'], 'conv_b': out['conv_b'], 'rg_a_w': out['rg_a_w'], 'rg_a_b': out['rg_a_b'], 'rg_x_w': out['rg_x_w'], 'rg_x_b': out['rg_x_b'], 'lru_lambda': out['lru_lambda'], 'w_lru_out': out['w_lru_out'], 'attn_sinks': out['attn_sinks'], 'rel_bias': out['rel_bias'], 'w_attn_out': out['w_attn_out'], 'w_gate': out['w_gate'], 'b_gate': out['b_gate'], 'w_o': out['w_o'], 'mix_post_g': out['mix_post_g'], 'ffn2_pre_g': out['ffn2_pre_g'], 'ffn2_w1': out['ffn2_w1'], 'ffn2_w3': out['ffn2_w3'], 'ffn2_w2': out['ffn2_w2'], 'ffn2_post_g': out['ffn2_post_g'], 'loss_target': out['loss_target'], 'm_ffn1_pre_g': out['m_ffn1_pre_g'], 'm_ffn1_w1': out['m_ffn1_w1'], 'm_ffn1_w3': out['m_ffn1_w3'], 'm_ffn1_w2': out['m_ffn1_w2'], 'm_ffn1_post_g': out['m_ffn1_post_g'], 'm_mix_pre_g': out['m_mix_pre_g'], 'm_w_in': out['m_w_in'], 'm_conv_w': out['m_conv_w'], 'm_conv_b': out['m_conv_b'], 'm_rg_a_w': out['m_rg_a_w'], 'm_rg_a_b': out['m_rg_a_b'], 'm_rg_x_w': out['m_rg_x_w'], 'm_rg_x_b': out['m_rg_x_b'], 'm_lru_lambda': out['m_lru_lambda'], 'm_w_lru_out': out['m_w_lru_out'], 'm_attn_sinks': out['m_attn_sinks'], 'm_rel_bias': out['m_rel_bias'], 'm_w_attn_out': out['m_w_attn_out'], 'm_w_gate': out['m_w_gate'], 'm_b_gate': out['m_b_gate'], 'm_w_o': out['m_w_o'], 'm_mix_post_g': out['m_mix_post_g'], 'm_ffn2_pre_g': out['m_ffn2_pre_g'], 'm_ffn2_w1': out['m_ffn2_w1'], 'm_ffn2_w3': out['m_ffn2_w3'], 'm_ffn2_w2': out['m_ffn2_w2'], 'm_ffn2_post_g': out['m_ffn2_post_g'], 'v_ffn1_pre_g': out['v_ffn1_pre_g'], 'v_ffn1_w1': out['v_ffn1_w1'], 'v_ffn1_w3': out['v_ffn1_w3'], 'v_ffn1_w2': out['v_ffn1_w2'], 'v_ffn1_post_g': out['v_ffn1_post_g'], 'v_mix_pre_g': out['v_mix_pre_g'], 'v_w_in': out['v_w_in'], 'v_conv_w': out['v_conv_w'], 'v_conv_b': out['v_conv_b'], 'v_rg_a_w': out['v_rg_a_w'], 'v_rg_a_b': out['v_rg_a_b'], 'v_rg_x_w': out['v_rg_x_w'], 'v_rg_x_b': out['v_rg_x_b'], 'v_lru_lambda': out['v_lru_lambda'], 'v_w_lru_out': out['v_w_lru_out'], 'v_attn_sinks': out['v_attn_sinks'], 'v_rel_bias': out['v_rel_bias'], 'v_w_attn_out': out['v_w_attn_out'], 'v_w_gate': out['v_w_gate'], 'v_b_gate': out['v_b_gate'], 'v_w_o': out['v_w_o'], 'v_mix_post_g': out['v_mix_post_g'], 'v_ffn2_pre_g': out['v_ffn2_pre_g'], 'v_ffn2_w1': out['v_ffn2_w1'], 'v_ffn2_w3': out['v_ffn2_w3'], 'v_ffn2_w2': out['v_ffn2_w2'], 'v_ffn2_post_g': out['v_ffn2_post_g']}


def _loss(weights, diff, rest, loss_target):
    with _jax.named_scope("forward"):
        args = {**rest, TWIN_DIFF_INPUT: diff, **{k: w.astype(_WEIGHT_DTYPES[k]) for k, w in weights.items()}}
        y = _forward(args)
    with _jax.named_scope("loss_head"):
        err = _jnp.square(y.astype(_jnp.float32) - loss_target)
        return 0.5 * _jnp.sum(_jnp.mean(err, axis=-1)) if err.ndim else 0.5 * err


def _adamw(w, g, m, v):
    m = ADAM_B1 * m + (1.0 - ADAM_B1) * g
    v = ADAM_B2 * v + (1.0 - ADAM_B2) * _jnp.square(g)
    m_hat = m / (1.0 - ADAM_B1 ** ADAM_STEP)
    v_hat = v / (1.0 - ADAM_B2 ** ADAM_STEP)
    delta = -ADAM_LR * (m_hat / (_jnp.sqrt(v_hat) + ADAM_EPS) + ADAM_WD * w)
    return delta, m, v


def reference(x, ffn1_pre_g, ffn1_w1, ffn1_w3, ffn1_w2, ffn1_post_g, mix_pre_g, w_in, conv_w, conv_b, rg_a_w, rg_a_b, rg_x_w, rg_x_b, lru_lambda, w_lru_out, attn_sinks, rel_bias, w_attn_out, w_gate, b_gate, w_o, mix_post_g, ffn2_pre_g, ffn2_w1, ffn2_w3, ffn2_w2, ffn2_post_g, loss_target, m_ffn1_pre_g, m_ffn1_w1, m_ffn1_w3, m_ffn1_w2, m_ffn1_post_g, m_mix_pre_g, m_w_in, m_conv_w, m_conv_b, m_rg_a_w, m_rg_a_b, m_rg_x_w, m_rg_x_b, m_lru_lambda, m_w_lru_out, m_attn_sinks, m_rel_bias, m_w_attn_out, m_w_gate, m_b_gate, m_w_o, m_mix_post_g, m_ffn2_pre_g, m_ffn2_w1, m_ffn2_w3, m_ffn2_w2, m_ffn2_post_g, v_ffn1_pre_g, v_ffn1_w1, v_ffn1_w3, v_ffn1_w2, v_ffn1_post_g, v_mix_pre_g, v_w_in, v_conv_w, v_conv_b, v_rg_a_w, v_rg_a_b, v_rg_x_w, v_rg_x_b, v_lru_lambda, v_w_lru_out, v_attn_sinks, v_rel_bias, v_w_attn_out, v_w_gate, v_b_gate, v_w_o, v_mix_post_g, v_ffn2_pre_g, v_ffn2_w1, v_ffn2_w3, v_ffn2_w2, v_ffn2_post_g):
    given = dict(x=x, ffn1_pre_g=ffn1_pre_g, ffn1_w1=ffn1_w1, ffn1_w3=ffn1_w3, ffn1_w2=ffn1_w2, ffn1_post_g=ffn1_post_g, mix_pre_g=mix_pre_g, w_in=w_in, conv_w=conv_w, conv_b=conv_b, rg_a_w=rg_a_w, rg_a_b=rg_a_b, rg_x_w=rg_x_w, rg_x_b=rg_x_b, lru_lambda=lru_lambda, w_lru_out=w_lru_out, attn_sinks=attn_sinks, rel_bias=rel_bias, w_attn_out=w_attn_out, w_gate=w_gate, b_gate=b_gate, w_o=w_o, mix_post_g=mix_post_g, ffn2_pre_g=ffn2_pre_g, ffn2_w1=ffn2_w1, ffn2_w3=ffn2_w3, ffn2_w2=ffn2_w2, ffn2_post_g=ffn2_post_g, loss_target=loss_target, m_ffn1_pre_g=m_ffn1_pre_g, m_ffn1_w1=m_ffn1_w1, m_ffn1_w3=m_ffn1_w3, m_ffn1_w2=m_ffn1_w2, m_ffn1_post_g=m_ffn1_post_g, m_mix_pre_g=m_mix_pre_g, m_w_in=m_w_in, m_conv_w=m_conv_w, m_conv_b=m_conv_b, m_rg_a_w=m_rg_a_w, m_rg_a_b=m_rg_a_b, m_rg_x_w=m_rg_x_w, m_rg_x_b=m_rg_x_b, m_lru_lambda=m_lru_lambda, m_w_lru_out=m_w_lru_out, m_attn_sinks=m_attn_sinks, m_rel_bias=m_rel_bias, m_w_attn_out=m_w_attn_out, m_w_gate=m_w_gate, m_b_gate=m_b_gate, m_w_o=m_w_o, m_mix_post_g=m_mix_post_g, m_ffn2_pre_g=m_ffn2_pre_g, m_ffn2_w1=m_ffn2_w1, m_ffn2_w3=m_ffn2_w3, m_ffn2_w2=m_ffn2_w2, m_ffn2_post_g=m_ffn2_post_g, v_ffn1_pre_g=v_ffn1_pre_g, v_ffn1_w1=v_ffn1_w1, v_ffn1_w3=v_ffn1_w3, v_ffn1_w2=v_ffn1_w2, v_ffn1_post_g=v_ffn1_post_g, v_mix_pre_g=v_mix_pre_g, v_w_in=v_w_in, v_conv_w=v_conv_w, v_conv_b=v_conv_b, v_rg_a_w=v_rg_a_w, v_rg_a_b=v_rg_a_b, v_rg_x_w=v_rg_x_w, v_rg_x_b=v_rg_x_b, v_lru_lambda=v_lru_lambda, v_w_lru_out=v_w_lru_out, v_attn_sinks=v_attn_sinks, v_rel_bias=v_rel_bias, v_w_attn_out=v_w_attn_out, v_w_gate=v_w_gate, v_b_gate=v_b_gate, v_w_o=v_w_o, v_mix_post_g=v_mix_post_g, v_ffn2_pre_g=v_ffn2_pre_g, v_ffn2_w1=v_ffn2_w1, v_ffn2_w3=v_ffn2_w3, v_ffn2_w2=v_ffn2_w2, v_ffn2_post_g=v_ffn2_post_g)
    weights = {n: given[n] for n in TWIN_WEIGHTS}
    shared = {n: given[n] for n in SHARED_INPUTS}
    per_example = {n: given[n] for n in ['x']}
    grad_fn = _jax.value_and_grad(_loss, argnums=(0, 1))

    def one_microbatch(ex, loss_target):
        ex = dict(ex)
        diff = ex.pop(TWIN_DIFF_INPUT)
        return grad_fn(weights, diff, {**shared, **ex}, loss_target)

    if N_MICROBATCH == 1:
        loss, (grad_w, grad_x) = one_microbatch(per_example, given["loss_target"])
    else:
        def body(carry, xs):
            loss_sum, grad_sum = carry
            l_k, (gw_k, gx_k) = one_microbatch(xs[0], xs[1])
            with _jax.named_scope("update"):
                return (loss_sum + l_k, _jax.tree.map(_jnp.add, grad_sum, gw_k)), gx_k

        init = (_jnp.zeros((), _jnp.float32), _jax.tree.map(_jnp.zeros_like, weights))
        (loss, grad_w), grad_x = _jax.lax.scan(body, init, (per_example, given["loss_target"]))
    with _jax.named_scope("update"):
        delta_w, new_m, new_v = {}, {}, {}
        for n in TWIN_WEIGHTS:
            delta_w[n], new_m[n], new_v[n] = _adamw(weights[n], grad_w[n], given["m_" + n], given["v_" + n])
    return (loss, grad_x, *[grad_w[n] for n in TWIN_WEIGHTS], *[delta_w[n] for n in TWIN_WEIGHTS],
            *[new_m[n] for n in TWIN_WEIGHTS], *[new_v[n] for n in TWIN_WEIGHTS])
```

```python
import math

import numpy as np
import jax
import jax.numpy as jnp
from jax import lax
from jax.experimental import pallas as pl
from jax.experimental.pallas import tpu as pltpu

F32, BF16 = jnp.float32, jnp.bfloat16

NDEV = 8
D = 1024
FF = 2816
N_HEADS, N_KV, HEAD_DIM = 16, 4, 64
CHUNK, WINDOW = 64, 128
N_BUCKETS, MAX_DISTANCE = 32, 128
LRU_BLOCK = 64
CONV_WIDTH = 4
LRU_C = 8.0
RMS_EPS = 1e-6
NEG_INF = -1e30
LANES = 128
QT = 256
KW = QT + WINDOW
LRU_ROWS = 512
IN_W = D + 2 * N_KV * HEAD_DIM + 2 * D
INP_W = D + 2 * N_KV * LANES + 2 * D
VMEM_BIG = 58 * 2 ** 20

ADAM_LR, ADAM_B1, ADAM_B2, ADAM_EPS, ADAM_WD, ADAM_STEP = 0.001, 0.9, 0.999, 1e-08, 0.01, 10

PACK = (("ffn1_w1", FF // NDEV), ("ffn1_w3", FF // NDEV), ("ffn1_w2", FF // NDEV), ("w_in", IN_W // NDEV),
        ("w_gate", 2 * D // NDEV), ("w_lru_out", D // NDEV), ("w_attn_out", D // NDEV), ("w_o", D // NDEV),
        ("ffn2_w1", FF // NDEV), ("ffn2_w3", FF // NDEV), ("ffn2_w2", FF // NDEV))
PACK_OFF = {}
_o = 0
for _n, _r in PACK:
    PACK_OFF[_n] = (_o, _r)
    _o += _r
CONV_OFF = _o
PACK_ROWS = _o + 16
COL_SHARDED = ("ffn1_w1", "ffn1_w3", "w_in", "w_gate", "ffn2_w1", "ffn2_w3")

SMALL = ("ffn1_pre_g", "ffn1_post_g", "mix_pre_g", "conv_w", "conv_b", "rg_a_w", "rg_a_b", "rg_x_w", "rg_x_b",
         "lru_lambda", "attn_sinks", "rel_bias", "b_gate", "mix_post_g", "ffn2_pre_g", "ffn2_post_g")


def _nn(a, b):
    return lax.dot_general(a, b, (((1,), (0,)), ((), ())), preferred_element_type=F32)


def _nt(a, b):
    return lax.dot_general(a, b, (((1,), (1,)), ((), ())), preferred_element_type=F32)


def _tn(a, b):
    return lax.dot_general(a, b, (((0,), (0,)), ((), ())), preferred_element_type=F32)


def _rms_fwd(x, g):
    r = lax.rsqrt(jnp.mean(x * x, axis=-1, keepdims=True) + RMS_EPS)
    xh = x * r
    return xh * g, xh, r


def _rms_bwd(dn, xh, r, g):
    dxh = dn * g
    dx = r * (dxh - xh * jnp.mean(dxh * xh, axis=-1, keepdims=True))
    return dx, jnp.sum(dn * xh, axis=0, keepdims=True)


def _row_spec(tm, c):
    return pl.BlockSpec((tm, c), lambda i: (i, 0))


def _const_spec(shape):
    nd = len(shape)
    return pl.BlockSpec(shape, lambda i: (0,) * nd)


ANY = pl.BlockSpec(memory_space=pl.ANY)


def _params(n_grid, vmem=None):
    return pltpu.CompilerParams(dimension_semantics=("arbitrary",) * n_grid, vmem_limit_bytes=vmem)


def _weight_pieces(name):
    off, rows = PACK_OFF[name]
    return [(d * PACK_ROWS + off, d * rows, rows) for d in range(NDEV)]


def _win_pieces():
    kv = N_KV * HEAD_DIM
    pieces = [(0, 0, D)]
    for part, (g0, d0) in enumerate(((D, D), (D + kv, D + N_KV * LANES))):
        for g in range(N_KV):
            for half in range(2):
                pieces.append((g0 + g * HEAD_DIM, d0 + g * LANES + half * HEAD_DIM, HEAD_DIM))
    pieces.append((D + 2 * kv, D + 2 * N_KV * LANES, D))
    pieces.append((D + 2 * kv + D, D + 2 * N_KV * LANES + D, D))
    off, rows = PACK_OFF["w_in"]
    out = []
    for g0, d0, n in pieces:
        while n > 0:
            dev, loc = divmod(g0, rows)
            m = min(n, rows - loc)
            out.append((dev * PACK_ROWS + off + loc, d0, m))
            g0, d0, n = g0 + m, d0 + m, n - m
    return out


def _start_loads(src_ref, dst_ref, pieces, sems, base):
    cps = []
    for j, (s, d, n) in enumerate(pieces):
        cp = pltpu.make_async_copy(src_ref.at[pl.ds(s, n)], dst_ref.at[pl.ds(d, n)], sems.at[base + j])
        cp.start()
        cps.append(cp)
    return cps


def _flush(acc_ref, out_ref, stage_ref, pieces):
    srows = stage_ref.shape[0]
    for a0, o0, n in pieces:
        done = 0
        while done < n:
            m = min(srows, n - done)
            stage_ref[pl.ds(0, m), :] = acc_ref[pl.ds(a0 + done, m), :].astype(BF16)
            pltpu.sync_copy(stage_ref.at[pl.ds(0, m)], out_ref.at[pl.ds(o0 + done, m)])
            done += m


def _silu_parts(a):
    sig = jax.nn.sigmoid(a)
    return sig, a * sig


FC = FF // 2


def _ffn_fwd(h, gpre, gpost, wg, names, target=None, tm=256):
    t_tok = h.shape[0]
    nt = t_tok // tm
    with_loss = target is not None
    pieces = [_weight_pieces(n) for n in names]

    def body(*refs):
        if with_loss:
            (h_ref, gpre_ref, gpost_ref, tgt_ref, wg_ref, hout_ref, a_ref, b_ref, f_ref, dy_ref, loss_ref,
             w1_s, w3_s, w2_s, sems) = refs
        else:
            (h_ref, gpre_ref, gpost_ref, wg_ref, hout_ref, a_ref, b_ref, f_ref, w1_s, w3_s, w2_s, sems) = refs
        i = pl.program_id(0)

        @pl.when(i == 0)
        def _():
            cps = []
            for k, dst in enumerate((w1_s, w3_s, w2_s)):
                cps += _start_loads(wg_ref, dst, pieces[k], sems, k * NDEV)
            for cp in cps:
                cp.wait()
            if with_loss:
                loss_ref[...] = jnp.zeros_like(loss_ref)

        x = h_ref[...]
        n, _, _ = _rms_fwd(x, gpre_ref[...])
        nb = n.astype(BF16)
        f = jnp.zeros((tm, D), F32)
        for c0 in range(0, FF, FC):
            a = _nt(nb, w1_s[pl.ds(c0, FC), :])
            b = _nt(nb, w3_s[pl.ds(c0, FC), :])
            _, sl = _silu_parts(a)
            a_ref[:, pl.ds(c0, FC)] = a.astype(BF16)
            b_ref[:, pl.ds(c0, FC)] = b.astype(BF16)
            f = f + _nn((sl * b).astype(BF16), w2_s[pl.ds(c0, FC), :])
        f_ref[...] = f
        fn, _, _ = _rms_fwd(f, gpost_ref[...])
        y = x + 0.5 * fn
        hout_ref[...] = y
        if with_loss:
            err = y - tgt_ref[...]
            dy_ref[...] = err * (1.0 / D)
            loss_ref[...] += jnp.sum(jnp.sum(err * err, axis=-1, keepdims=True), axis=0, keepdims=True) * (0.5 / D)

    ins = [h, gpre, gpost] + ([target] if with_loss else []) + [wg]
    in_specs = [_row_spec(tm, D), _const_spec((1, D)), _const_spec((1, D))]
    in_specs += ([_row_spec(tm, D)] if with_loss else []) + [ANY]
    out_shape = [jax.ShapeDtypeStruct((t_tok, D), F32), jax.ShapeDtypeStruct((t_tok, FF), BF16),
                 jax.ShapeDtypeStruct((t_tok, FF), BF16), jax.ShapeDtypeStruct((t_tok, D), F32)]
    out_specs = [_row_spec(tm, D), _row_spec(tm, FF), _row_spec(tm, FF), _row_spec(tm, D)]
    if with_loss:
        out_shape += [jax.ShapeDtypeStruct((t_tok, D), F32), jax.ShapeDtypeStruct((1, LANES), F32)]
        out_specs += [_row_spec(tm, D), _const_spec((1, LANES))]
    return pl.pallas_call(
        body, name="ffn_fwd_" + names[0][:4], grid=(nt,), in_specs=in_specs, out_specs=out_specs, out_shape=out_shape,
        scratch_shapes=[pltpu.VMEM((FF, D), BF16)] * 3 + [pltpu.SemaphoreType.DMA((3 * NDEV,))],
        compiler_params=_params(1, VMEM_BIG))(*ins)


def _ffn_bwd_a(dh, f, a, b, gpost, wg, name_w2, tm=256):
    t_tok = dh.shape[0]
    nt = t_tok // tm
    pieces = _weight_pieces(name_w2)

    def body(dh_ref, f_ref, a_ref, b_ref, gpost_ref, wg_ref, da_ref, db_ref, dgp_ref, dw2_ref, w2_s, acc_s, stage_s, sems):
        i = pl.program_id(0)

        @pl.when(i == 0)
        def _():
            for cp in _start_loads(wg_ref, w2_s, pieces, sems, 0):
                cp.wait()
            acc_s[...] = jnp.zeros_like(acc_s)
            dgp_ref[...] = jnp.zeros_like(dgp_ref)

        fv = f_ref[...]
        _, fh, r = _rms_fwd(fv, gpost_ref[...])
        df, dg = _rms_bwd(0.5 * dh_ref[...], fh, r, gpost_ref[...])
        dgp_ref[...] += dg
        dfb = df.astype(BF16)
        for c0 in range(0, FF, FC):
            ds = _nt(dfb, w2_s[pl.ds(c0, FC), :])
            av = a_ref[:, pl.ds(c0, FC)].astype(F32)
            bv = b_ref[:, pl.ds(c0, FC)].astype(F32)
            sig, sl = _silu_parts(av)
            da_ref[:, pl.ds(c0, FC)] = (ds * bv * (sig * (1.0 + av * (1.0 - sig)))).astype(BF16)
            db_ref[:, pl.ds(c0, FC)] = (ds * sl).astype(BF16)
            acc_s[pl.ds(c0, FC), :] += _tn((sl * bv).astype(BF16), dfb)

        @pl.when(i == nt - 1)
        def _():
            _flush(acc_s, dw2_ref, stage_s, [(0, 0, FF)])

    return pl.pallas_call(
        body, name="ffn_bwd_a_" + name_w2[:4], grid=(nt,),
        in_specs=[_row_spec(tm, D), _row_spec(tm, D), _row_spec(tm, FF), _row_spec(tm, FF), _const_spec((1, D)), ANY],
        out_specs=[_row_spec(tm, FF), _row_spec(tm, FF), _const_spec((1, D)), ANY],
        out_shape=[jax.ShapeDtypeStruct((t_tok, FF), BF16), jax.ShapeDtypeStruct((t_tok, FF), BF16),
                   jax.ShapeDtypeStruct((1, D), F32), jax.ShapeDtypeStruct((FF, D), BF16)],
        scratch_shapes=[pltpu.VMEM((FF, D), BF16), pltpu.VMEM((FF, D), F32), pltpu.VMEM((256, D), BF16),
                        pltpu.SemaphoreType.DMA((NDEV,))],
        compiler_params=_params(1, VMEM_BIG))(dh, f, a, b, gpost, wg)


def _ffn_bwd_b(da, db, h, dh, gpre, wg, name_w1, name_w3, tm=256):
    t_tok = dh.shape[0]
    nt = t_tok // tm
    pieces = [_weight_pieces(name_w1), _weight_pieces(name_w3)]

    def body(da_ref, db_ref, h_ref, dh_ref, gpre_ref, wg_ref, dhin_ref, dgp_ref, dw1_ref, dw3_ref,
             w1_s, w3_s, acc1_s, acc3_s, stage_s, sems):
        i = pl.program_id(0)

        @pl.when(i == 0)
        def _():
            cps = _start_loads(wg_ref, w1_s, pieces[0], sems, 0) + _start_loads(wg_ref, w3_s, pieces[1], sems, NDEV)
            for cp in cps:
                cp.wait()
            acc1_s[...] = jnp.zeros_like(acc1_s)
            acc3_s[...] = jnp.zeros_like(acc3_s)
            dgp_ref[...] = jnp.zeros_like(dgp_ref)

        g = gpre_ref[...]
        n, xh, r = _rms_fwd(h_ref[...], g)
        nb = n.astype(BF16)
        dn = jnp.zeros((tm, D), F32)
        for c0 in range(0, FF, FC):
            dav = da_ref[:, pl.ds(c0, FC)]
            dbv = db_ref[:, pl.ds(c0, FC)]
            dn = dn + _nn(dav, w1_s[pl.ds(c0, FC), :]) + _nn(dbv, w3_s[pl.ds(c0, FC), :])
            acc1_s[pl.ds(c0, FC), :] += _tn(dav, nb)
            acc3_s[pl.ds(c0, FC), :] += _tn(dbv, nb)
        dx, dg = _rms_bwd(dn, xh, r, g)
        dgp_ref[...] += dg
        dhin_ref[...] = dh_ref[...] + dx

        @pl.when(i == nt - 1)
        def _():
            _flush(acc1_s, dw1_ref, stage_s, [(0, 0, FF)])
            _flush(acc3_s, dw3_ref, stage_s, [(0, 0, FF)])

    return pl.pallas_call(
        body, name="ffn_bwd_b_" + name_w1[:4], grid=(nt,),
        in_specs=[_row_spec(tm, FF), _row_spec(tm, FF), _row_spec(tm, D), _row_spec(tm, D), _const_spec((1, D)), ANY],
        out_specs=[_row_spec(tm, D), _const_spec((1, D)), ANY, ANY],
        out_shape=[jax.ShapeDtypeStruct((t_tok, D), F32), jax.ShapeDtypeStruct((1, D), F32),
                   jax.ShapeDtypeStruct((FF, D), BF16), jax.ShapeDtypeStruct((FF, D), BF16)],
        scratch_shapes=[pltpu.VMEM((FF, D), BF16), pltpu.VMEM((FF, D), BF16), pltpu.VMEM((FF, D), F32),
                        pltpu.VMEM((FF, D), F32), pltpu.VMEM((256, D), BF16), pltpu.SemaphoreType.DMA((2 * NDEV,))],
        compiler_params=_params(1, VMEM_BIG))(da, db, h, dh, gpre, wg)


Q0, K0, V0, XR0, XG0 = 0, D, D + N_KV * LANES, D + 2 * N_KV * LANES, 2 * D + 2 * N_KV * LANES


def _mix_proj_fwd(h, g, bgate, wg, tm=256):
    t_tok = h.shape[0]
    nt = t_tok // tm
    win_p = _win_pieces()
    wgate_p = _weight_pieces("w_gate")

    def body(h_ref, g_ref, bg_ref, wg_ref, q_ref, k_ref, v_ref, xr_ref, xg_ref, gs_ref, win_s, wgt_s, sems):
        i = pl.program_id(0)

        @pl.when(i == 0)
        def _():
            cps = _start_loads(wg_ref, win_s, win_p, sems, 0) + _start_loads(wg_ref, wgt_s, wgate_p, sems, len(win_p))
            for cp in cps:
                cp.wait()

        n, _, _ = _rms_fwd(h_ref[...], g_ref[...])
        nb = n.astype(BF16)
        q_ref[...] = _nt(nb, win_s[pl.ds(Q0, D), :]).astype(BF16)
        k_ref[...] = _nt(nb, win_s[pl.ds(K0, N_KV * LANES), :]).astype(BF16)
        v_ref[...] = _nt(nb, win_s[pl.ds(V0, N_KV * LANES), :]).astype(BF16)
        xr_ref[...] = _nt(nb, win_s[pl.ds(XR0, D), :])
        xg_ref[...] = _nt(nb, win_s[pl.ds(XG0, D), :])
        gs_ref[...] = jax.nn.sigmoid(_nt(nb, wgt_s[...]) + bg_ref[...])

    kvw = N_KV * LANES
    return pl.pallas_call(
        body, name="mix_proj_fwd", grid=(nt,),
        in_specs=[_row_spec(tm, D), _const_spec((1, D)), _const_spec((1, 2 * D)), ANY],
        out_specs=[_row_spec(tm, D), _row_spec(tm, kvw), _row_spec(tm, kvw), _row_spec(tm, D), _row_spec(tm, D),
                   _row_spec(tm, 2 * D)],
        out_shape=[jax.ShapeDtypeStruct((t_tok, D), BF16), jax.ShapeDtypeStruct((t_tok, kvw), BF16),
                   jax.ShapeDtypeStruct((t_tok, kvw), BF16), jax.ShapeDtypeStruct((t_tok, D), F32),
                   jax.ShapeDtypeStruct((t_tok, D), F32), jax.ShapeDtypeStruct((t_tok, 2 * D), F32)],
        scratch_shapes=[pltpu.VMEM((INP_W, D), BF16), pltpu.VMEM((2 * D, D), BF16),
                        pltpu.SemaphoreType.DMA((len(win_p) + NDEV,))],
        compiler_params=_params(1, VMEM_BIG))(h, g, bgate, wg)


def _mix_proj_bwd(dq, dk, dv, dxr, dxg, dgpre, h, dh, g, wg, tm=256):
    t_tok = h.shape[0]
    nt = t_tok // tm
    win_p = _win_pieces()
    wgate_p = _weight_pieces("w_gate")
    kv = N_KV * HEAD_DIM
    win_flush = [(Q0, 0, D)]
    win_flush += [(K0 + gi * LANES, D + gi * HEAD_DIM, HEAD_DIM) for gi in range(N_KV)]
    win_flush += [(V0 + gi * LANES, D + kv + gi * HEAD_DIM, HEAD_DIM) for gi in range(N_KV)]
    win_flush += [(XR0, D + 2 * kv, D), (XG0, 2 * D + 2 * kv, D)]

    def body(dq_ref, dk_ref, dv_ref, dxr_ref, dxg_ref, dgp_ref, h_ref, dh_ref, g_ref, wg_ref,
             dhin_ref, dg_ref, dbg_ref, dwin_ref, dwgt_ref, win_s, wgt_s, accin_s, accgt_s, stage_s, sems):
        i = pl.program_id(0)

        @pl.when(i == 0)
        def _():
            cps = _start_loads(wg_ref, win_s, win_p, sems, 0) + _start_loads(wg_ref, wgt_s, wgate_p, sems, len(win_p))
            for cp in cps:
                cp.wait()
            accin_s[...] = jnp.zeros_like(accin_s)
            accgt_s[...] = jnp.zeros_like(accgt_s)
            dg_ref[...] = jnp.zeros_like(dg_ref)
            dbg_ref[...] = jnp.zeros_like(dbg_ref)

        gv = g_ref[...]
        n, xh, r = _rms_fwd(h_ref[...], gv)
        nb = n.astype(BF16)
        dgp = dgp_ref[...]
        dbg_ref[...] += jnp.sum(dgp, axis=0, keepdims=True)
        dgpb = dgp.astype(BF16)
        du = _nn(dgpb, wgt_s[...])
        accgt_s[...] += _tn(dgpb, nb)
        for ref, r0, w in ((dq_ref, Q0, D), (dk_ref, K0, N_KV * LANES), (dv_ref, V0, N_KV * LANES),
                           (dxr_ref, XR0, D), (dxg_ref, XG0, D)):
            piece = ref[...].astype(BF16)
            du = du + _nn(piece, win_s[pl.ds(r0, w), :])
            accin_s[pl.ds(r0, w), :] += _tn(piece, nb)
        dx, dg = _rms_bwd(du, xh, r, gv)
        dg_ref[...] += dg
        dhin_ref[...] = dh_ref[...] + dx

        @pl.when(i == nt - 1)
        def _():
            _flush(accin_s, dwin_ref, stage_s, win_flush)
            _flush(accgt_s, dwgt_ref, stage_s, [(0, 0, 2 * D)])

    kvw = N_KV * LANES
    return pl.pallas_call(
        body, name="mix_proj_bwd", grid=(nt,),
        in_specs=[_row_spec(tm, D), _row_spec(tm, kvw), _row_spec(tm, kvw), _row_spec(tm, D), _row_spec(tm, D),
                  _row_spec(tm, 2 * D), _row_spec(tm, D), _row_spec(tm, D), _const_spec((1, D)), ANY],
        out_specs=[_row_spec(tm, D), _const_spec((1, D)), _const_spec((1, 2 * D)), ANY, ANY],
        out_shape=[jax.ShapeDtypeStruct((t_tok, D), F32), jax.ShapeDtypeStruct((1, D), F32),
                   jax.ShapeDtypeStruct((1, 2 * D), F32), jax.ShapeDtypeStruct((IN_W, D), BF16),
                   jax.ShapeDtypeStruct((2 * D, D), BF16)],
        scratch_shapes=[pltpu.VMEM((INP_W, D), BF16), pltpu.VMEM((2 * D, D), BF16), pltpu.VMEM((INP_W, D), F32),
                        pltpu.VMEM((2 * D, D), F32), pltpu.VMEM((256, D), BF16),
                        pltpu.SemaphoreType.DMA((len(win_p) + NDEV,))],
        compiler_params=_params(1, VMEM_BIG))(dq, dk, dv, dxr, dxg, dgpre, h, dh, g, wg)


def _shift_down(x, d, fill):
    row = lax.broadcasted_iota(jnp.int32, x.shape, 0)
    return jnp.where(row >= d, pltpu.roll(x, d, 0), fill)


def _shift_up(x, d, fill):
    rows = x.shape[0]
    row = lax.broadcasted_iota(jnp.int32, x.shape, 0)
    return jnp.where(row < rows - d, pltpu.roll(x, rows - d, 0), fill)


def _expm1(x):
    small = x * (1.0 + x * (0.5 + x * (1.0 / 6.0 + x * (1.0 / 24.0))))
    return jnp.where(jnp.abs(x) < 0.05, small, jnp.exp(x) - 1.0)


def _softplus(x):
    return jnp.maximum(x, 0.0) + jnp.log(1.0 + jnp.exp(-jnp.abs(x)))


_GELU_C = math.sqrt(2.0 / math.pi)


def _gelu_parts(x):
    th = jnp.tanh(_GELU_C * (x + 0.044715 * x * x * x))
    val = 0.5 * x * (1.0 + th)
    grad = 0.5 * (1.0 + th) + 0.5 * x * (1.0 - th * th) * _GELU_C * (1.0 + 3.0 * 0.044715 * x * x)
    return val, grad


def _lru_pre(x, halo, cw_ref, cb_ref, wa_ref, wx_ref, ba_ref, bx_ref, lam_ref):
    ext = jnp.concatenate([halo, x], axis=0)
    shifted = [x] + [pltpu.roll(ext, k, 0)[8:] for k in (1, 2, 3)]
    xc = cb_ref[...] + cw_ref[pl.ds(CONV_WIDTH - 1, 1), :] * x
    for k in (1, 2, 3):
        xc = xc + cw_ref[pl.ds(CONV_WIDTH - 1 - k, 1), :] * shifted[k]
    xcb = xc.astype(BF16)
    r = jax.nn.sigmoid(_nn(xcb, wa_ref[...]) + ba_ref[...])
    ig = jax.nn.sigmoid(_nn(xcb, wx_ref[...]) + bx_ref[...])
    sp = _softplus(-lam_ref[...])
    log_a = -LRU_C * r * sp
    a = jnp.exp(log_a)
    mult = jnp.sqrt(-_expm1(2.0 * log_a))
    return shifted, xc, xcb, r, ig, sp, a, mult


def _lru_specs(nt, reverse):
    def tt(t):
        return nt - 1 - t if reverse else t
    tile = pl.BlockSpec((LRU_ROWS, LANES), lambda cb, t: (tt(t), cb))
    halo = pl.BlockSpec((8, LANES), lambda cb, t: (jnp.maximum(tt(t) * (LRU_ROWS // 8) - 1, 0), cb))
    vec = pl.BlockSpec((1, LANES), lambda cb, t: (0, cb))
    cw = pl.BlockSpec((CONV_WIDTH, LANES), lambda cb, t: (0, cb))
    mat = pl.BlockSpec((None, LANES, LANES), lambda cb, t: (cb, 0, 0))
    return tile, halo, vec, cw, mat


def _lru_fwd(xr, xg, cw, cb, wa, wx, ba, bx, lam):
    t_tok = xr.shape[0]
    nt = t_tok // LRU_ROWS
    rows = LRU_ROWS

    def body(xr_ref, xg_ref, cw_ref, cb_ref, wa_ref, wx_ref, ba_ref, bx_ref, lam_ref, y_ref, h_ref, tail_s, hc_s):
        t = pl.program_id(1)

        @pl.when(t == 0)
        def _():
            tail_s[...] = jnp.zeros_like(tail_s)
            hc_s[...] = jnp.zeros_like(hc_s)

        x = xr_ref[...]
        _, xc, _, _, ig, _, a, mult = _lru_pre(x, tail_s[...], cw_ref, cb_ref, wa_ref, wx_ref, ba_ref, bx_ref, lam_ref)
        tail_s[...] = xr_ref[pl.ds(rows - 8, 8), :]
        acc_a, acc_b = a, mult * (ig * xc)
        d = 1
        while d < rows:
            acc_b = acc_a * _shift_down(acc_b, d, 0.0) + acc_b
            acc_a = acc_a * _shift_down(acc_a, d, 1.0)
            d *= 2
        hv = acc_b + acc_a * hc_s[...]
        h_ref[...] = hv
        hc_s[...] = h_ref[pl.ds(rows - 1, 1), :]
        gl, _ = _gelu_parts(xg_ref[...])
        y_ref[...] = (hv * gl).astype(BF16)

    tile, _, vec, cws, mat = _lru_specs(nt, False)
    return pl.pallas_call(
        body, name="lru_fwd", grid=(D // LANES, nt),
        in_specs=[tile, tile, cws, vec, mat, mat, vec, vec, vec],
        out_specs=[tile, tile],
        out_shape=[jax.ShapeDtypeStruct((t_tok, D), BF16), jax.ShapeDtypeStruct((t_tok, D), F32)],
        scratch_shapes=[pltpu.VMEM((8, LANES), F32), pltpu.VMEM((1, LANES), F32)],
        compiler_params=_params(2))(xr, xg, cw, cb, wa, wx, ba, bx, lam)


def _lru_bwd(dy, xr, xg, hseq, cw, cb, wa, wx, ba, bx, lam):
    t_tok = xr.shape[0]
    nt = t_tok // LRU_ROWS
    rows = LRU_ROWS

    def body(dy_ref, xr_ref, xrh_ref, xg_ref, h_ref, hh_ref, cw_ref, cb_ref, wa_ref, wx_ref, ba_ref, bx_ref, lam_ref,
             dxr_ref, dxg_ref, dvec_ref, dwa_ref, dwx_ref, gcar_s, acar_s, head_s, tmp_s):
        t = pl.program_id(1)
        first_tile = t == nt - 1

        @pl.when(t == 0)
        def _():
            gcar_s[...] = jnp.zeros_like(gcar_s)
            acar_s[...] = jnp.zeros_like(acar_s)
            head_s[...] = jnp.zeros_like(head_s)
            dvec_ref[...] = jnp.zeros_like(dvec_ref)
            dwa_ref[...] = jnp.zeros_like(dwa_ref)
            dwx_ref[...] = jnp.zeros_like(dwx_ref)

        x = xr_ref[...]
        halo = jnp.where(first_tile, 0.0, xrh_ref[...])
        shifted, xc, xcb, r, ig, sp, a, mult = _lru_pre(x, halo, cw_ref, cb_ref, wa_ref, wx_ref, ba_ref, bx_ref, lam_ref)
        hv = h_ref[...]
        dyv = dy_ref[...]
        gl, glg = _gelu_parts(xg_ref[...])
        dxg_ref[...] = dyv * hv * glg
        acc_a = _shift_up(a, 1, acar_s[...])
        acc_b = dyv * gl
        d = 1
        while d < rows:
            acc_b = acc_a * _shift_up(acc_b, d, 0.0) + acc_b
            acc_a = acc_a * _shift_up(acc_a, d, 1.0)
            d *= 2
        g = acc_b + acc_a * gcar_s[...]
        hhalo = jnp.where(first_tile, 0.0, hh_ref[...])
        hprev = pltpu.roll(jnp.concatenate([hhalo, hv], axis=0), 1, 0)[8:]
        dmult = g * ig * xc
        dlog_a = a * (g * hprev) - dmult * a * a / mult
        dig = g * mult * xc
        dxc = g * mult * ig
        dzr = (dlog_a * (-LRU_C * sp)) * r * (1.0 - r)
        dzx = dig * ig * (1.0 - ig)
        dzrb, dzxb = dzr.astype(BF16), dzx.astype(BF16)
        dxc = dxc + _nt(dzrb, wa_ref[...]) + _nt(dzxb, wx_ref[...])
        dwa_ref[...] += _tn(xcb, dzrb)
        dwx_ref[...] += _tn(xcb, dzxb)
        dsp = jnp.sum(dlog_a * (-LRU_C * r), axis=0, keepdims=True)
        dlam = dsp * (-jax.nn.sigmoid(-lam_ref[...]))
        vrow = lax.broadcasted_iota(jnp.int32, (8, LANES), 0)
        upd = jnp.where(vrow == 4, jnp.sum(dxc, axis=0, keepdims=True), 0.0)
        upd = jnp.where(vrow == 5, jnp.sum(dzr, axis=0, keepdims=True), upd)
        upd = jnp.where(vrow == 6, jnp.sum(dzx, axis=0, keepdims=True), upd)
        upd = jnp.where(vrow == 7, dlam, upd)
        for k in range(CONV_WIDTH):
            upd = jnp.where(vrow == CONV_WIDTH - 1 - k, jnp.sum(dxc * shifted[k], axis=0, keepdims=True), upd)
        dvec_ref[...] += upd
        ext = jnp.concatenate([dxc, head_s[...]], axis=0)
        dxr = cw_ref[pl.ds(CONV_WIDTH - 1, 1), :] * dxc
        for k in (1, 2, 3):
            dxr = dxr + cw_ref[pl.ds(CONV_WIDTH - 1 - k, 1), :] * pltpu.roll(ext, rows + 8 - k, 0)[:rows]
        dxr_ref[...] = dxr
        tmp_s[...] = g
        gcar_s[...] = tmp_s[pl.ds(0, 1), :]
        tmp_s[...] = a
        acar_s[...] = tmp_s[pl.ds(0, 1), :]
        tmp_s[...] = dxc
        head_s[...] = tmp_s[pl.ds(0, 8), :]

    tile, halo, vec, cws, mat = _lru_specs(nt, True)
    return pl.pallas_call(
        body, name="lru_bwd", grid=(D // LANES, nt),
        in_specs=[tile, tile, halo, tile, tile, halo, cws, vec, mat, mat, vec, vec, vec],
        out_specs=[tile, tile, pl.BlockSpec((8, LANES), lambda cb, t: (0, cb)), mat, mat],
        out_shape=[jax.ShapeDtypeStruct((t_tok, D), F32), jax.ShapeDtypeStruct((t_tok, D), F32),
                   jax.ShapeDtypeStruct((8, D), F32), jax.ShapeDtypeStruct((D // LANES, LANES, LANES), F32),
                   jax.ShapeDtypeStruct((D // LANES, LANES, LANES), F32)],
        scratch_shapes=[pltpu.VMEM((1, LANES), F32), pltpu.VMEM((1, LANES), F32), pltpu.VMEM((8, LANES), F32),
                        pltpu.VMEM((rows, LANES), F32)],
        compiler_params=_params(2))(dy, xr, xr, xg, hseq, hseq, cw, cb, wa, wx, ba, bx, lam)


def _t5_bucket_np(rel):
    nb = N_BUCKETS // 2
    max_exact = nb // 2
    ret = np.where(rel > 0, nb, 0)
    n = np.abs(rel)
    nf = np.maximum(n, 1).astype(np.float32)
    large = max_exact + (np.log(nf / np.float32(max_exact)) / np.float32(math.log(MAX_DISTANCE / max_exact))
                         * np.float32(nb - max_exact)).astype(np.int32)
    large = np.minimum(large, nb - 1)
    return ret + np.where(n < max_exact, n, large)


def _bucket_map():
    r = np.arange(QT)[:, None]
    c = np.arange(KW)[None, :]
    j = c - (r // CHUNK) * CHUNK
    band = (j >= 0) & (j < WINDOW + CHUNK)
    return np.where(band, _t5_bucket_np(c - r - WINDOW), -1).astype(np.int32)


def _attn_specs(nt, reverse):
    def tt(i):
        return nt - 1 - i if reverse else i
    kvw = N_KV * LANES
    qs = pl.BlockSpec((QT, D), lambda i: (tt(i), 0))
    cur = pl.BlockSpec((QT, kvw), lambda i: (tt(i), 0))
    prev = pl.BlockSpec((WINDOW, kvw), lambda i: (jnp.maximum(tt(i) * (QT // WINDOW) - 1, 0), 0))
    lse = pl.BlockSpec((QT, LANES), lambda i: (tt(i), 0))
    return qs, cur, prev, lse


def _attn_fwd(q, kd, vd, bias, sinks):
    t_tok = q.shape[0]
    nt = t_tok // QT

    def body(q_ref, kp_ref, kc_ref, vp_ref, vc_ref, bias_ref, sink_ref, o_ref, lse_ref):
        i = pl.program_id(0)
        col = lax.broadcasted_iota(jnp.int32, (1, KW), 1)
        first = jnp.where((i == 0) & (col < WINDOW), NEG_INF, 0.0)
        lane = lax.broadcasted_iota(jnp.int32, (QT, LANES), 1)
        lse_t = jnp.zeros((QT, LANES), F32)
        for g in range(N_KV):
            kwin = jnp.concatenate([kp_ref[:, pl.ds(g * LANES, LANES)], kc_ref[:, pl.ds(g * LANES, LANES)]], axis=0)
            vwin = jnp.concatenate([vp_ref[:, pl.ds(g * LANES, LANES)], vc_ref[:, pl.ds(g * LANES, LANES)]], axis=0)
            for slab in (2 * g, 2 * g + 1):
                qs = q_ref[:, pl.ds(slab * LANES, LANES)]
                o_slab = jnp.zeros((QT, LANES), F32)
                for half in range(2):
                    hd = 2 * slab + half
                    mine = (lane >= half * HEAD_DIM) & (lane < (half + 1) * HEAD_DIM)
                    qh = jnp.where(mine, qs, jnp.zeros_like(qs)) * jnp.asarray(HEAD_DIM ** -0.5, BF16)
                    s = _nt(qh, kwin) + (bias_ref[hd] + first)
                    sk = sink_ref[hd]
                    m = jnp.maximum(jnp.max(s, axis=-1, keepdims=True), sk)
                    e = jnp.exp(s - m)
                    l = jnp.sum(e, axis=-1, keepdims=True) + jnp.exp(sk - m)
                    p = e / l
                    o_slab = jnp.where(mine, _nn(p.astype(BF16), vwin), o_slab)
                    lse_t = jnp.where(lane == hd, m + jnp.log(l), lse_t)
                o_ref[:, pl.ds(slab * LANES, LANES)] = o_slab.astype(BF16)
        lse_ref[...] = lse_t

    qs, cur, prev, lse = _attn_specs(nt, False)
    return pl.pallas_call(
        body, name="attn_fwd", grid=(nt,),
        in_specs=[qs, prev, cur, prev, cur, _const_spec((N_HEADS, QT, KW)), pl.BlockSpec(memory_space=pltpu.SMEM)],
        out_specs=[qs, lse],
        out_shape=[jax.ShapeDtypeStruct((t_tok, D), BF16), jax.ShapeDtypeStruct((t_tok, LANES), F32)],
        compiler_params=_params(1, 48 * 2 ** 20))(q, kd, kd, vd, vd, bias, sinks)


def _attn_bwd(q, kd, vd, o, do, lse, bias, sinks):
    t_tok = q.shape[0]
    nt = t_tok // QT
    kvw = N_KV * LANES

    def body(q_ref, kp_ref, kc_ref, vp_ref, vc_ref, o_ref, do_ref, lse_ref, bias_ref, sink_ref,
             dq_ref, dk_ref, dv_ref, ds_ref, dsink_ref, kcar_s, vcar_s):
        i = pl.program_id(0)
        tile = nt - 1 - i

        @pl.when(i == 0)
        def _():
            kcar_s[...] = jnp.zeros_like(kcar_s)
            vcar_s[...] = jnp.zeros_like(vcar_s)
            ds_ref[...] = jnp.zeros_like(ds_ref)
            dsink_ref[...] = jnp.zeros_like(dsink_ref)

        col = lax.broadcasted_iota(jnp.int32, (1, KW), 1)
        first = jnp.where((tile == 0) & (col < WINDOW), NEG_INF, 0.0)
        lane = lax.broadcasted_iota(jnp.int32, (QT, LANES), 1)
        lane_k = lax.broadcasted_iota(jnp.int32, (KW, LANES), 1)
        lane_1 = lax.broadcasted_iota(jnp.int32, (1, LANES), 1)
        lse_t = lse_ref[...]
        dsink = jnp.zeros((1, LANES), F32)
        for g in range(N_KV):
            kwin = jnp.concatenate([kp_ref[:, pl.ds(g * LANES, LANES)], kc_ref[:, pl.ds(g * LANES, LANES)]], axis=0)
            vwin = jnp.concatenate([vp_ref[:, pl.ds(g * LANES, LANES)], vc_ref[:, pl.ds(g * LANES, LANES)]], axis=0)
            dk_acc = jnp.zeros((KW, LANES), F32)
            dv_acc = jnp.zeros((KW, LANES), F32)
            for slab in (2 * g, 2 * g + 1):
                qs = q_ref[:, pl.ds(slab * LANES, LANES)]
                dos = do_ref[:, pl.ds(slab * LANES, LANES)]
                od = dos.astype(F32) * o_ref[:, pl.ds(slab * LANES, LANES)].astype(F32)
                dq_slab = jnp.zeros((QT, LANES), F32)
                for half in range(2):
                    hd = 2 * slab + half
                    mine = (lane >= half * HEAD_DIM) & (lane < (half + 1) * HEAD_DIM)
                    qh = jnp.where(mine, qs, jnp.zeros_like(qs)) * jnp.asarray(HEAD_DIM ** -0.5, BF16)
                    doh = jnp.where(mine, dos, jnp.zeros_like(dos))
                    s = _nt(qh, kwin) + (bias_ref[hd] + first)
                    lse_h = jnp.sum(jnp.where(lane == hd, lse_t, 0.0), axis=-1, keepdims=True)
                    p = jnp.exp(s - lse_h)
                    dp = _nt(doh, vwin)
                    drow = jnp.sum(jnp.where(mine, od, 0.0), axis=-1, keepdims=True)
                    ds = p * (dp - drow)
                    ds_ref[hd] += ds
                    psink = jnp.exp(sink_ref[hd] - lse_h)
                    dsink = dsink + jnp.where(lane_1 == hd, -jnp.sum(psink * drow, axis=0, keepdims=True), 0.0)
                    dsb = ds.astype(BF16)
                    dq_slab = jnp.where(mine, _nn(dsb, kwin) * (HEAD_DIM ** -0.5), dq_slab)
                    dk_acc = dk_acc + _tn(dsb, qh)
                    dv_acc = dv_acc + _tn(p.astype(BF16), doh)
                dq_ref[:, pl.ds(slab * LANES, LANES)] = dq_slab.astype(BF16)
            dk_f = jnp.where(lane_k < HEAD_DIM, dk_acc + pltpu.roll(dk_acc, HEAD_DIM, 1), 0.0)
            dv_f = jnp.where(lane_k < HEAD_DIM, dv_acc + pltpu.roll(dv_acc, HEAD_DIM, 1), 0.0)
            for acc, out_ref, car in ((dk_f, dk_ref, kcar_s), (dv_f, dv_ref, vcar_s)):
                cs = pl.ds(g * LANES, LANES)
                out_ref[pl.ds(0, QT - WINDOW), cs] = acc[WINDOW:QT].astype(BF16)
                out_ref[pl.ds(QT - WINDOW, WINDOW), cs] = (acc[QT:KW] + car[:, cs]).astype(BF16)
                car[:, cs] = acc[0:WINDOW]
        dsink_ref[...] += dsink

    qs, cur, prev, lse_s = _attn_specs(nt, True)
    return pl.pallas_call(
        body, name="attn_bwd", grid=(nt,),
        in_specs=[qs, prev, cur, prev, cur, qs, qs, lse_s, _const_spec((N_HEADS, QT, KW)),
                  pl.BlockSpec(memory_space=pltpu.SMEM)],
        out_specs=[qs, cur, cur, _const_spec((N_HEADS, QT, KW)), _const_spec((1, LANES))],
        out_shape=[jax.ShapeDtypeStruct((t_tok, D), BF16), jax.ShapeDtypeStruct((t_tok, kvw), BF16),
                   jax.ShapeDtypeStruct((t_tok, kvw), BF16), jax.ShapeDtypeStruct((N_HEADS, QT, KW), F32),
                   jax.ShapeDtypeStruct((1, LANES), F32)],
        scratch_shapes=[pltpu.VMEM((WINDOW, kvw), F32), pltpu.VMEM((WINDOW, kvw), F32)],
        compiler_params=_params(1, VMEM_BIG))(q, kd, kd, vd, vd, o, do, lse, bias, sinks)


def _bias_grad(ds_acc, bmap):
    def body(ds_ref, bm_ref, out_ref):
        row = lax.broadcasted_iota(jnp.int32, (N_BUCKETS, LANES), 0)
        lane = lax.broadcasted_iota(jnp.int32, (N_BUCKETS, LANES), 1)
        bm = bm_ref[...]

        def per_head(hd, res):
            dsv = ds_ref[hd]
            for b in range(N_BUCKETS):
                val = jnp.sum(jnp.sum(jnp.where(bm == b, dsv, 0.0), axis=0, keepdims=True), axis=1, keepdims=True)
                res = jnp.where((row == b) & (lane == hd), val, res)
            return res

        out_ref[...] = lax.fori_loop(0, N_HEADS, per_head, jnp.zeros((N_BUCKETS, LANES), F32))

    return pl.pallas_call(
        body, name="bias_grad", out_shape=jax.ShapeDtypeStruct((N_BUCKETS, LANES), F32),
        in_specs=[pl.BlockSpec(memory_space=pltpu.VMEM), pl.BlockSpec(memory_space=pltpu.VMEM)],
        out_specs=pl.BlockSpec(memory_space=pltpu.VMEM))(ds_acc, bmap)


def _mix_out_fwd(ya_in, o, gs, h, g, wg, tm=256):
    t_tok = h.shape[0]
    nt = t_tok // tm
    pieces = [_weight_pieces(n) for n in ("w_lru_out", "w_attn_out", "w_o")]

    def body(ya_ref, o_ref, gs_ref, h_ref, g_ref, wg_ref, hout_ref, yao_ref, ybo_ref, z_ref, wa_s, wb_s, wo_s, sems):
        i = pl.program_id(0)

        @pl.when(i == 0)
        def _():
            cps = []
            for k, dst in enumerate((wa_s, wb_s, wo_s)):
                cps += _start_loads(wg_ref, dst, pieces[k], sems, k * NDEV)
            for cp in cps:
                cp.wait()

        ya = _nn(ya_ref[...], wa_s[...])
        yb = _nn(o_ref[...], wb_s[...])
        yao_ref[...] = ya
        ybo_ref[...] = yb
        merged = gs_ref[:, pl.ds(0, D)] * ya + gs_ref[:, pl.ds(D, D)] * yb
        z = _nn(merged.astype(BF16), wo_s[...])
        z_ref[...] = z
        zn, _, _ = _rms_fwd(z, g_ref[...])
        hout_ref[...] = h_ref[...] + zn

    return pl.pallas_call(
        body, name="mix_out_fwd", grid=(nt,),
        in_specs=[_row_spec(tm, D), _row_spec(tm, D), _row_spec(tm, 2 * D), _row_spec(tm, D), _const_spec((1, D)), ANY],
        out_specs=[_row_spec(tm, D)] * 4,
        out_shape=[jax.ShapeDtypeStruct((t_tok, D), F32)] * 4,
        scratch_shapes=[pltpu.VMEM((D, D), BF16)] * 3 + [pltpu.SemaphoreType.DMA((3 * NDEV,))],
        compiler_params=_params(1, 48 * 2 ** 20))(ya_in, o, gs, h, g, wg)


def _mix_out_bwd(dh, z, ya, yb, gs, ya_in, o, g, wg, tm=256):
    t_tok = dh.shape[0]
    nt = t_tok // tm
    pieces = [_weight_pieces(n) for n in ("w_lru_out", "w_attn_out", "w_o")]

    def body(dh_ref, z_ref, ya_ref, yb_ref, gs_ref, yain_ref, o_ref, g_ref, wg_ref,
             dyain_ref, do_ref, dgpre_ref, dg_ref, dwa_ref, dwb_ref, dwo_ref,
             wa_s, wb_s, wo_s, acca_s, accb_s, acco_s, stage_s, sems):
        i = pl.program_id(0)

        @pl.when(i == 0)
        def _():
            cps = []
            for k, dst in enumerate((wa_s, wb_s, wo_s)):
                cps += _start_loads(wg_ref, dst, pieces[k], sems, k * NDEV)
            for cp in cps:
                cp.wait()
            for acc in (acca_s, accb_s, acco_s):
                acc[...] = jnp.zeros_like(acc)
            dg_ref[...] = jnp.zeros_like(dg_ref)

        gv = g_ref[...]
        _, zh, r = _rms_fwd(z_ref[...], gv)
        dz, dg = _rms_bwd(dh_ref[...], zh, r, gv)
        dg_ref[...] += dg
        dzb = dz.astype(BF16)
        ga, gb = gs_ref[:, pl.ds(0, D)], gs_ref[:, pl.ds(D, D)]
        ya_v, yb_v = ya_ref[...], yb_ref[...]
        merged = ga * ya_v + gb * yb_v
        acco_s[...] += _tn(merged.astype(BF16), dzb)
        dm = _nt(dzb, wo_s[...])
        dgpre_ref[:, pl.ds(0, D)] = dm * ya_v * ga * (1.0 - ga)
        dgpre_ref[:, pl.ds(D, D)] = dm * yb_v * gb * (1.0 - gb)
        dya = (dm * ga).astype(BF16)
        dyb = (dm * gb).astype(BF16)
        dyain_ref[...] = _nt(dya, wa_s[...])
        do_ref[...] = _nt(dyb, wb_s[...]).astype(BF16)
        acca_s[...] += _tn(yain_ref[...], dya)
        accb_s[...] += _tn(o_ref[...], dyb)

        @pl.when(i == nt - 1)
        def _():
            _flush(acca_s, dwa_ref, stage_s, [(0, 0, D)])
            _flush(accb_s, dwb_ref, stage_s, [(0, 0, D)])
            _flush(acco_s, dwo_ref, stage_s, [(0, 0, D)])

    return pl.pallas_call(
        body, name="mix_out_bwd", grid=(nt,),
        in_specs=[_row_spec(tm, D)] * 4 + [_row_spec(tm, 2 * D), _row_spec(tm, D), _row_spec(tm, D),
                                            _const_spec((1, D)), ANY],
        out_specs=[_row_spec(tm, D), _row_spec(tm, D), _row_spec(tm, 2 * D), _const_spec((1, D)), ANY, ANY, ANY],
        out_shape=[jax.ShapeDtypeStruct((t_tok, D), F32), jax.ShapeDtypeStruct((t_tok, D), BF16),
                   jax.ShapeDtypeStruct((t_tok, 2 * D), F32), jax.ShapeDtypeStruct((1, D), F32)]
        + [jax.ShapeDtypeStruct((D, D), BF16)] * 3,
        scratch_shapes=[pltpu.VMEM((D, D), BF16)] * 3 + [pltpu.VMEM((D, D), F32)] * 3
        + [pltpu.VMEM((256, D), BF16), pltpu.SemaphoreType.DMA((3 * NDEV,))],
        compiler_params=_params(1, VMEM_BIG))(dh, z, ya, yb, gs, ya_in, o, g, wg)


MESH = pl.DeviceIdType.MESH


def _allgather(shard, name):
    m_per, n = shard.shape

    def body(x_ref, out_ref, send_sems, recv_sems, local_sem):
        x, y, c = lax.axis_index("x"), lax.axis_index("y"), lax.axis_index("c")
        me, sibling = (x, y, c), (x, y, 1 - c)
        chips = [(1 - x, y), (x, 1 - y), (1 - x, 1 - y)]

        def rows(px, py, pc):
            return out_ref.at[pl.ds((4 * px + 2 * py + pc) * m_per, m_per), :]

        def copy(k, block, to, src=None):
            return pltpu.make_async_remote_copy(
                src_ref=rows(*block) if src is None else src, dst_ref=rows(*block),
                send_sem=send_sems.at[k], recv_sem=recv_sems.at[k], device_id=to, device_id_type=MESH)

        mine = pltpu.make_async_copy(x_ref, rows(*me), local_sem)
        mine.start()
        first = [copy(0, me, sibling, src=x_ref)]
        first += [copy(1 + j, me, (*chip, c), src=x_ref) for j, chip in enumerate(chips)]
        for cp in first:
            cp.start()
        passed = [copy(4 + j, (*chip, c), sibling) for j, chip in enumerate(chips)]
        for j, chip in enumerate(chips):
            copy(1 + j, (*chip, c), me).wait_recv()
            passed[j].start()
        copy(0, sibling, me).wait_recv()
        for j, chip in enumerate(chips):
            copy(4 + j, (*chip, 1 - c), me).wait_recv()
        for cp in first + passed:
            cp.wait_send()
        mine.wait()

    return pl.pallas_call(
        body, name=name, out_shape=jax.ShapeDtypeStruct((NDEV * m_per, n), shard.dtype),
        in_specs=[ANY], out_specs=ANY,
        scratch_shapes=[pltpu.SemaphoreType.DMA((7,)), pltpu.SemaphoreType.DMA((7,)), pltpu.SemaphoreType.DMA],
    )(shard)


def _reduce_scatter_send(grads):
    nw = len(grads)
    rows = [g.shape[0] // NDEV for g in grads]

    def body(*refs):
        g_refs, r_refs = refs[:nw], refs[nw:2 * nw]
        send_sems, recv_sems, local_sems = refs[2 * nw:]
        x, y, c = lax.axis_index("x"), lax.axis_index("y"), lax.axis_index("c")
        me = 4 * x + 2 * y + c
        locals_ = []
        for w in range(nw):
            cp = pltpu.make_async_copy(g_refs[w].at[pl.ds(me * rows[w], rows[w])], r_refs[w].at[me], local_sems.at[w])
            cp.start()
            locals_.append(cp)
        sends = []
        for k in range(1, NDEV):
            px, py, pc = x ^ (k >> 2), y ^ ((k >> 1) & 1), c ^ (k & 1)
            peer = 4 * px + 2 * py + pc
            for w in range(nw):
                cp = pltpu.make_async_remote_copy(
                    src_ref=g_refs[w].at[pl.ds(peer * rows[w], rows[w])], dst_ref=r_refs[w].at[me],
                    send_sem=send_sems.at[w, k - 1], recv_sem=recv_sems.at[w, k - 1],
                    device_id=(px, py, pc), device_id_type=MESH)
                cp.start()
                sends.append(cp)
        for k in range(1, NDEV):
            px, py, pc = x ^ (k >> 2), y ^ ((k >> 1) & 1), c ^ (k & 1)
            peer = 4 * px + 2 * py + pc
            for w in range(nw):
                pltpu.make_async_remote_copy(
                    src_ref=g_refs[w].at[pl.ds(0, rows[w])], dst_ref=r_refs[w].at[peer],
                    send_sem=send_sems.at[w, k - 1], recv_sem=recv_sems.at[w, k - 1],
                    device_id=(px, py, pc), device_id_type=MESH).wait_recv()
        for cp in sends:
            cp.wait_send()
        for cp in locals_:
            cp.wait()

    return pl.pallas_call(
        body, name="reduce_scatter_send",
        out_shape=[jax.ShapeDtypeStruct((NDEV, r, g.shape[1]), g.dtype) for g, r in zip(grads, rows)],
        in_specs=[ANY] * nw, out_specs=[ANY] * nw,
        scratch_shapes=[pltpu.SemaphoreType.DMA((nw, NDEV - 1)), pltpu.SemaphoreType.DMA((nw, NDEV - 1)),
                        pltpu.SemaphoreType.DMA((nw,))],
    )(*grads)


def _sum_parts(parts_list):
    n = len(parts_list)
    _, r, c = parts_list[0].shape
    tc = 256

    def body(*refs):
        for p_ref, o_ref in zip(refs[:n], refs[n:]):
            acc = p_ref[0].astype(F32)
            for s in range(1, NDEV):
                acc = acc + p_ref[s].astype(F32)
            o_ref[...] = acc

    return pl.pallas_call(
        body, name=f"sum_parts_{r}", grid=(c // tc,),
        in_specs=[pl.BlockSpec((NDEV, r, tc), lambda i: (0, 0, i))] * n,
        out_specs=[pl.BlockSpec((r, tc), lambda i: (0, i))] * n,
        out_shape=[jax.ShapeDtypeStruct((r, c), F32)] * n,
        compiler_params=_params(1, 48 * 2 ** 20))(*parts_list)


def _adamw_math(w, g, m, v):
    m = ADAM_B1 * m + (1.0 - ADAM_B1) * g
    v = ADAM_B2 * v + (1.0 - ADAM_B2) * (g * g)
    m_hat = m / (1.0 - ADAM_B1 ** ADAM_STEP)
    v_hat = v / (1.0 - ADAM_B2 ** ADAM_STEP)
    delta = -ADAM_LR * (m_hat / (jnp.sqrt(v_hat) + ADAM_EPS) + ADAM_WD * w)
    return delta, m, v


def _adamw(items):
    n = len(items)
    r, c = items[0][0].shape
    tr = r if r * c <= 2 ** 18 else max(t for t in range(8, 65, 8) if r % t == 0)

    def body(*refs):
        for k in range(n):
            g_ref, w_ref, m_ref, v_ref = refs[4 * k:4 * k + 4]
            d_ref, nm_ref, nv_ref = refs[4 * n + 3 * k:4 * n + 3 * k + 3]
            d, m, v = _adamw_math(w_ref[...], g_ref[...], m_ref[...], v_ref[...])
            d_ref[...] = d
            nm_ref[...] = m
            nv_ref[...] = v

    spec = pl.BlockSpec((tr, c), lambda i: (i, 0))
    outs = pl.pallas_call(
        body, name=f"adamw_{r}x{c}", grid=(r // tr,),
        in_specs=[spec] * (4 * n), out_specs=[spec] * (3 * n),
        out_shape=[jax.ShapeDtypeStruct((r, c), F32)] * (3 * n),
        compiler_params=_params(1, 40 * 2 ** 20))(*[a for it in items for a in it])
    return [tuple(outs[3 * k:3 * k + 3]) for k in range(n)]


def _pack_small(arrs):
    rows, offs = [], []
    total = 0
    for a in arrs:
        flat = a.reshape(-1).astype(F32)
        nr = -(-flat.shape[0] // LANES)
        flat = jnp.pad(flat, (0, nr * LANES - flat.shape[0]))
        rows.append(flat.reshape(nr, LANES))
        offs.append((total, nr))
        total += nr
    pad = -total % 8
    if pad:
        rows.append(jnp.zeros((pad, LANES), F32))
    return jnp.concatenate(rows, axis=0), offs


def _unpack_small(pack, offs, shapes):
    out = []
    for (o, nr), shp in zip(offs, shapes):
        size = int(np.prod(shp))
        out.append(pack[o:o + nr].reshape(-1)[:size].reshape(shp))
    return out


def _sum_small(gathered, rows):
    def body(p_ref, o_ref):
        acc = p_ref[pl.ds(0, rows), :]
        for s in range(1, NDEV):
            acc = acc + p_ref[pl.ds(s * rows, rows), :]
        o_ref[...] = acc

    return pl.pallas_call(
        body, name="sum_small", out_shape=jax.ShapeDtypeStruct((rows, LANES), F32),
        in_specs=[pl.BlockSpec(memory_space=pltpu.VMEM)], out_specs=pl.BlockSpec(memory_space=pltpu.VMEM))(gathered)


def _block_diag(w):
    w = w.reshape(D // LANES, 2, LRU_BLOCK, LRU_BLOCK)
    z = jnp.zeros((D // LANES, LRU_BLOCK, LRU_BLOCK), w.dtype)
    top = jnp.concatenate([w[:, 0], z], axis=2)
    bot = jnp.concatenate([z, w[:, 1]], axis=2)
    return jnp.concatenate([top, bot], axis=1)


def _block_diag_grad(dw):
    a = dw[:, :LRU_BLOCK, :LRU_BLOCK]
    b = dw[:, LRU_BLOCK:, LRU_BLOCK:]
    return jnp.stack([a, b], axis=1).reshape(D // LRU_BLOCK, LRU_BLOCK, LRU_BLOCK)


def _local_step(x, target, wg, sm):
    vec = lambda n: sm[n].reshape(1, -1)
    bmap = _bucket_map()
    table = sm["rel_bias"]
    bias = jnp.where(jnp.asarray(bmap >= 0)[None], jnp.transpose(table[jnp.asarray(np.maximum(bmap, 0))], (2, 0, 1)),
                     NEG_INF).astype(F32)
    sinks = sm["attn_sinks"].reshape(N_HEADS)
    wa_bd = _block_diag(sm["rg_a_w"].reshape(D // LRU_BLOCK, LRU_BLOCK, LRU_BLOCK)).astype(BF16)
    wx_bd = _block_diag(sm["rg_x_w"].reshape(D // LRU_BLOCK, LRU_BLOCK, LRU_BLOCK)).astype(BF16)
    cw = sm["conv_w"].reshape(CONV_WIDTH, D)
    lru_args = (cw, vec("conv_b"), wa_bd, wx_bd, vec("rg_a_b"), vec("rg_x_b"), vec("lru_lambda"))

    h1, a1, b1, f1 = _ffn_fwd(x, vec("ffn1_pre_g"), vec("ffn1_post_g"), wg, ("ffn1_w1", "ffn1_w3", "ffn1_w2"))
    q, kd, vd, xr, xg, gs = _mix_proj_fwd(h1, vec("mix_pre_g"), vec("b_gate"), wg)
    ya_in, hseq = _lru_fwd(xr, xg, *lru_args)
    o, lse = _attn_fwd(q, kd, vd, bias, sinks)
    h2, ya, yb, z = _mix_out_fwd(ya_in, o, gs, h1, vec("mix_post_g"), wg)
    _, a2, b2, f2, dy, loss = _ffn_fwd(h2, vec("ffn2_pre_g"), vec("ffn2_post_g"), wg,
                                       ("ffn2_w1", "ffn2_w3", "ffn2_w2"), target=target)

    gw, gs_ = {}, {}
    da, db, gs_["ffn2_post_g"], gw["ffn2_w2"] = _ffn_bwd_a(dy, f2, a2, b2, vec("ffn2_post_g"), wg, "ffn2_w2")
    dh2, gs_["ffn2_pre_g"], gw["ffn2_w1"], gw["ffn2_w3"] = _ffn_bwd_b(da, db, h2, dy, vec("ffn2_pre_g"), wg,
                                                                      "ffn2_w1", "ffn2_w3")
    dya_in, do, dgpre, gs_["mix_post_g"], gw["w_lru_out"], gw["w_attn_out"], gw["w_o"] = _mix_out_bwd(
        dh2, z, ya, yb, gs, ya_in, o, vec("mix_post_g"), wg)
    dq, dk, dv, ds_acc, dsink = _attn_bwd(q, kd, vd, o, do, lse, bias, sinks)
    dxr, dxg, dvec, dwa, dwx = _lru_bwd(dya_in, xr, xg, hseq, *lru_args)
    dh1, gs_["mix_pre_g"], gs_["b_gate"], gw["w_in"], gw["w_gate"] = _mix_proj_bwd(
        dq, dk, dv, dxr, dxg, dgpre, h1, dh2, vec("mix_pre_g"), wg)
    da, db, gs_["ffn1_post_g"], gw["ffn1_w2"] = _ffn_bwd_a(dh1, f1, a1, b1, vec("ffn1_post_g"), wg, "ffn1_w2")
    dx, gs_["ffn1_pre_g"], gw["ffn1_w1"], gw["ffn1_w3"] = _ffn_bwd_b(da, db, x, dh1, vec("ffn1_pre_g"), wg,
                                                                     "ffn1_w1", "ffn1_w3")
    gs_["conv_w"] = dvec[0:CONV_WIDTH]
    gs_["conv_b"], gs_["rg_a_b"], gs_["rg_x_b"], gs_["lru_lambda"] = dvec[4], dvec[5], dvec[6], dvec[7]
    gs_["rg_a_w"] = _block_diag_grad(dwa)
    gs_["rg_x_w"] = _block_diag_grad(dwx)
    gs_["attn_sinks"] = dsink[0, :N_HEADS]
    gs_["rel_bias"] = _bias_grad(ds_acc, jnp.asarray(bmap))[:, :N_HEADS]
    return loss[0, 0], dx, gw, gs_


def kernel(x, ffn1_pre_g, ffn1_w1, ffn1_w3, ffn1_w2, ffn1_post_g, mix_pre_g, w_in, conv_w, conv_b, rg_a_w, rg_a_b, rg_x_w, rg_x_b, lru_lambda, w_lru_out, attn_sinks, rel_bias, w_attn_out, w_gate, b_gate, w_o, mix_post_g, ffn2_pre_g, ffn2_w1, ffn2_w3, ffn2_w2, ffn2_post_g, loss_target, m_ffn1_pre_g, m_ffn1_w1, m_ffn1_w3, m_ffn1_w2, m_ffn1_post_g, m_mix_pre_g, m_w_in, m_conv_w, m_conv_b, m_rg_a_w, m_rg_a_b, m_rg_x_w, m_rg_x_b, m_lru_lambda, m_w_lru_out, m_attn_sinks, m_rel_bias, m_w_attn_out, m_w_gate, m_b_gate, m_w_o, m_mix_post_g, m_ffn2_pre_g, m_ffn2_w1, m_ffn2_w3, m_ffn2_w2, m_ffn2_post_g, v_ffn1_pre_g, v_ffn1_w1, v_ffn1_w3, v_ffn1_w2, v_ffn1_post_g, v_mix_pre_g, v_w_in, v_conv_w, v_conv_b, v_rg_a_w, v_rg_a_b, v_rg_x_w, v_rg_x_b, v_lru_lambda, v_w_lru_out, v_attn_sinks, v_rel_bias, v_w_attn_out, v_w_gate, v_b_gate, v_w_o, v_mix_post_g, v_ffn2_pre_g, v_ffn2_w1, v_ffn2_w3, v_ffn2_w2, v_ffn2_post_g):
    names = ["ffn1_pre_g", "ffn1_w1", "ffn1_w3", "ffn1_w2", "ffn1_post_g", "mix_pre_g", "w_in", "conv_w", "conv_b",
             "rg_a_w", "rg_a_b", "rg_x_w", "rg_x_b", "lru_lambda", "w_lru_out", "attn_sinks", "rel_bias", "w_attn_out",
             "w_gate", "b_gate", "w_o", "mix_post_g", "ffn2_pre_g", "ffn2_w1", "ffn2_w3", "ffn2_w2", "ffn2_post_g"]
    ws = dict(zip(names, (ffn1_pre_g, ffn1_w1, ffn1_w3, ffn1_w2, ffn1_post_g, mix_pre_g, w_in, conv_w, conv_b, rg_a_w,
                          rg_a_b, rg_x_w, rg_x_b, lru_lambda, w_lru_out, attn_sinks, rel_bias, w_attn_out, w_gate,
                          b_gate, w_o, mix_post_g, ffn2_pre_g, ffn2_w1, ffn2_w3, ffn2_w2, ffn2_post_g)))
    ms = dict(zip(names, (m_ffn1_pre_g, m_ffn1_w1, m_ffn1_w3, m_ffn1_w2, m_ffn1_post_g, m_mix_pre_g, m_w_in, m_conv_w,
                          m_conv_b, m_rg_a_w, m_rg_a_b, m_rg_x_w, m_rg_x_b, m_lru_lambda, m_w_lru_out, m_attn_sinks,
                          m_rel_bias, m_w_attn_out, m_w_gate, m_b_gate, m_w_o, m_mix_post_g, m_ffn2_pre_g, m_ffn2_w1,
                          m_ffn2_w3, m_ffn2_w2, m_ffn2_post_g)))
    vs = dict(zip(names, (v_ffn1_pre_g, v_ffn1_w1, v_ffn1_w3, v_ffn1_w2, v_ffn1_post_g, v_mix_pre_g, v_w_in, v_conv_w,
                          v_conv_b, v_rg_a_w, v_rg_a_b, v_rg_x_w, v_rg_x_b, v_lru_lambda, v_w_lru_out, v_attn_sinks,
                          v_rel_bias, v_w_attn_out, v_w_gate, v_b_gate, v_w_o, v_mix_post_g, v_ffn2_pre_g, v_ffn2_w1,
                          v_ffn2_w3, v_ffn2_w2, v_ffn2_post_g)))
    me = 4 * lax.axis_index("x") + 2 * lax.axis_index("y") + lax.axis_index("c")

    def shard2d(name, a):
        a = a.reshape(a.shape[-2], a.shape[-1])
        return a.T if name in COL_SHARDED else a

    conv_row = lax.bitcast_convert_type(conv_w.reshape(CONV_WIDTH, LANES), BF16).reshape(1, D)
    pack = jnp.concatenate([shard2d(n, ws[n]).astype(BF16) for n, _ in PACK]
                           + [conv_row, jnp.zeros((PACK_ROWS - CONV_OFF - 1, D), BF16)], axis=0)
    wg = _allgather(pack, "allgather_weights")
    conv_rows = wg.reshape(NDEV, PACK_ROWS, D)[:, CONV_OFF].reshape(NDEV, CONV_WIDTH, LANES, 2)
    conv_full = jnp.transpose(lax.bitcast_convert_type(conv_rows, F32), (1, 0, 2)).reshape(CONV_WIDTH, D)
    sm = {n: ws[n] for n in SMALL}
    sm["conv_w"] = conv_full

    loss_part, grad_x, gw, gsm = _local_step(x[0], loss_target[0], wg, sm)
    loss = lax.psum(loss_part, ("x", "y", "c"))

    parts = dict(zip([n for n, _ in PACK], _reduce_scatter_send([gw[n] for n, _ in PACK])))
    grads = {}
    by_rows = {}
    for n, r in PACK:
        by_rows.setdefault(r, []).append(n)
    for r, group in by_rows.items():
        for n, g in zip(group, _sum_parts([parts[n] for n in group])):
            grads[n] = g.T if n in COL_SHARDED else g
    small_pack, offs = _pack_small([gsm[n] for n in SMALL])
    rows_small = small_pack.shape[0]
    small_sum = _sum_small(_allgather(small_pack, "allgather_small"), rows_small)
    small_shapes = [(CONV_WIDTH, D) if n == "conv_w" else ws[n].shape for n in SMALL]
    for n, g in zip(SMALL, _unpack_small(small_sum, offs, small_shapes)):
        grads[n] = g
    grads["conv_w"] = lax.dynamic_slice(grads["conv_w"], (0, me * LANES), (CONV_WIDTH, LANES)).reshape(conv_w.shape)

    delta, new_m, new_v = {}, {}, {}
    by_shape = {}
    for n, _ in PACK:
        shp = (ws[n].shape[-2], ws[n].shape[-1])
        by_shape.setdefault(shp, []).append(n)
    for shp, group in by_shape.items():
        res = _adamw([(grads[n], ws[n].reshape(shp), ms[n].reshape(shp), vs[n].reshape(shp)) for n in group])
        for n, (d_, m_, v_) in zip(group, res):
            delta[n], new_m[n], new_v[n] = d_, m_, v_
    packs = [_pack_small([src[n] for n in SMALL])[0] for src in (grads, ws, ms, vs)]
    _, offs_p = _pack_small([ws[n] for n in SMALL])
    (d_p, m_p, v_p), = _adamw([tuple(packs)])
    shapes_p = [ws[n].shape for n in SMALL]
    for dst, p in ((delta, d_p), (new_m, m_p), (new_v, v_p)):
        for n, a in zip(SMALL, _unpack_small(p, offs_p, shapes_p)):
            dst[n] = a

    outs = [loss, grad_x.reshape(x.shape)]
    for src in (grads, delta, new_m, new_v):
        outs += [src[n].reshape(ws[n].shape) for n in names]
    return tuple(outs)
```

```python
import math

import numpy as np
import jax
import jax.numpy as jnp
from jax import lax
from jax.experimental import pallas as pl
from jax.experimental.pallas import tpu as pltpu

F32, BF16 = jnp.float32, jnp.bfloat16

NDEV = 8
D = 1024
FF = 2816
N_HEADS, N_KV, HEAD_DIM = 16, 4, 64
CHUNK, WINDOW = 64, 128
N_BUCKETS, MAX_DISTANCE = 32, 128
LRU_BLOCK = 64
CONV_WIDTH = 4
LRU_C = 8.0
RMS_EPS = 1e-6
NEG_INF = -1e30
LANES = 128
QT = 256
KW = QT + WINDOW
LRU_ROWS = 512
IN_W = D + 2 * N_KV * HEAD_DIM + 2 * D
INP_W = D + 2 * N_KV * LANES + 2 * D
VMEM_BIG = 58 * 2 ** 20

ADAM_LR, ADAM_B1, ADAM_B2, ADAM_EPS, ADAM_WD, ADAM_STEP = 0.001, 0.9, 0.999, 1e-08, 0.01, 10

PACK = (("ffn1_w1", FF // NDEV), ("ffn1_w3", FF // NDEV), ("ffn1_w2", FF // NDEV), ("w_in", IN_W // NDEV),
        ("w_gate", 2 * D // NDEV), ("w_lru_out", D // NDEV), ("w_attn_out", D // NDEV), ("w_o", D // NDEV),
        ("ffn2_w1", FF // NDEV), ("ffn2_w3", FF // NDEV), ("ffn2_w2", FF // NDEV))
PACK_OFF = {}
_o = 0
for _n, _r in PACK:
    PACK_OFF[_n] = (_o, _r)
    _o += _r
CONV_OFF = _o
PACK_ROWS = _o + 16
COL_SHARDED = ("ffn1_w1", "ffn1_w3", "w_in", "w_gate", "ffn2_w1", "ffn2_w3")

SMALL = ("ffn1_pre_g", "ffn1_post_g", "mix_pre_g", "conv_w", "conv_b", "rg_a_w", "rg_a_b", "rg_x_w", "rg_x_b",
         "lru_lambda", "attn_sinks", "rel_bias", "b_gate", "mix_post_g", "ffn2_pre_g", "ffn2_post_g")


def _nn(a, b):
    return lax.dot_general(a, b, (((1,), (0,)), ((), ())), preferred_element_type=F32)


def _nt(a, b):
    return lax.dot_general(a, b, (((1,), (1,)), ((), ())), preferred_element_type=F32)


def _tn(a, b):
    return lax.dot_general(a, b, (((0,), (0,)), ((), ())), preferred_element_type=F32)


def _rms_fwd(x, g):
    r = lax.rsqrt(jnp.mean(x * x, axis=-1, keepdims=True) + RMS_EPS)
    xh = x * r
    return xh * g, xh, r


def _rms_bwd(dn, xh, r, g):
    dxh = dn * g
    dx = r * (dxh - xh * jnp.mean(dxh * xh, axis=-1, keepdims=True))
    return dx, jnp.sum(dn * xh, axis=0, keepdims=True)


def _row_spec(tm, c):
    return pl.BlockSpec((tm, c), lambda i: (i, 0))


def _const_spec(shape):
    nd = len(shape)
    return pl.BlockSpec(shape, lambda i: (0,) * nd)


ANY = pl.BlockSpec(memory_space=pl.ANY)


def _params(n_grid, vmem=None):
    return pltpu.CompilerParams(dimension_semantics=("arbitrary",) * n_grid, vmem_limit_bytes=vmem)


def _weight_pieces(name):
    off, rows = PACK_OFF[name]
    return [(d * PACK_ROWS + off, d * rows, rows) for d in range(NDEV)]


def _win_pieces():
    kv = N_KV * HEAD_DIM
    pieces = [(0, 0, D)]
    for part, (g0, d0) in enumerate(((D, D), (D + kv, D + N_KV * LANES))):
        for g in range(N_KV):
            for half in range(2):
                pieces.append((g0 + g * HEAD_DIM, d0 + g * LANES + half * HEAD_DIM, HEAD_DIM))
    pieces.append((D + 2 * kv, D + 2 * N_KV * LANES, D))
    pieces.append((D + 2 * kv + D, D + 2 * N_KV * LANES + D, D))
    off, rows = PACK_OFF["w_in"]
    out = []
    for g0, d0, n in pieces:
        while n > 0:
            dev, loc = divmod(g0, rows)
            m = min(n, rows - loc)
            out.append((dev * PACK_ROWS + off + loc, d0, m))
            g0, d0, n = g0 + m, d0 + m, n - m
    return out


def _start_loads(src_ref, dst_ref, pieces, sems, base):
    cps = []
    for j, (s, d, n) in enumerate(pieces):
        cp = pltpu.make_async_copy(src_ref.at[pl.ds(s, n)], dst_ref.at[pl.ds(d, n)], sems.at[base + j])
        cp.start()
        cps.append(cp)
    return cps


def _flush(acc_ref, out_ref, stage_ref, pieces):
    srows = stage_ref.shape[0]
    for a0, o0, n in pieces:
        done = 0
        while done < n:
            m = min(srows, n - done)
            stage_ref[pl.ds(0, m), :] = acc_ref[pl.ds(a0 + done, m), :].astype(BF16)
            pltpu.sync_copy(stage_ref.at[pl.ds(0, m)], out_ref.at[pl.ds(o0 + done, m)])
            done += m


def _silu_parts(a):
    sig = jax.nn.sigmoid(a)
    return sig, a * sig


FC = FF // 2


def _ffn_fwd(h, gpre, gpost, wg, names, target=None, tm=256):
    t_tok = h.shape[0]
    nt = t_tok // tm
    with_loss = target is not None
    pieces = [_weight_pieces(n) for n in names]

    def body(*refs):
        if with_loss:
            (h_ref, gpre_ref, gpost_ref, tgt_ref, wg_ref, hout_ref, a_ref, b_ref, f_ref, dy_ref, loss_ref,
             w1_s, w3_s, w2_s, sems) = refs
        else:
            (h_ref, gpre_ref, gpost_ref, wg_ref, hout_ref, a_ref, b_ref, f_ref, w1_s, w3_s, w2_s, sems) = refs
        i = pl.program_id(0)

        @pl.when(i == 0)
        def _():
            cps = []
            for k, dst in enumerate((w1_s, w3_s, w2_s)):
                cps += _start_loads(wg_ref, dst, pieces[k], sems, k * NDEV)
            for cp in cps:
                cp.wait()
            if with_loss:
                loss_ref[...] = jnp.zeros_like(loss_ref)

        x = h_ref[...]
        n, _, _ = _rms_fwd(x, gpre_ref[...])
        nb = n.astype(BF16)
        f = jnp.zeros((tm, D), F32)
        for c0 in range(0, FF, FC):
            a = _nt(nb, w1_s[pl.ds(c0, FC), :])
            b = _nt(nb, w3_s[pl.ds(c0, FC), :])
            _, sl = _silu_parts(a)
            a_ref[:, pl.ds(c0, FC)] = a.astype(BF16)
            b_ref[:, pl.ds(c0, FC)] = b.astype(BF16)
            f = f + _nn((sl * b).astype(BF16), w2_s[pl.ds(c0, FC), :])
        f_ref[...] = f
        fn, _, _ = _rms_fwd(f, gpost_ref[...])
        y = x + 0.5 * fn
        hout_ref[...] = y
        if with_loss:
            err = y - tgt_ref[...]
            dy_ref[...] = err * (1.0 / D)
            loss_ref[...] += jnp.sum(jnp.sum(err * err, axis=-1, keepdims=True), axis=0, keepdims=True) * (0.5 / D)

    ins = [h, gpre, gpost] + ([target] if with_loss else []) + [wg]
    in_specs = [_row_spec(tm, D), _const_spec((1, D)), _const_spec((1, D))]
    in_specs += ([_row_spec(tm, D)] if with_loss else []) + [ANY]
    out_shape = [jax.ShapeDtypeStruct((t_tok, D), F32), jax.ShapeDtypeStruct((t_tok, FF), BF16),
                 jax.ShapeDtypeStruct((t_tok, FF), BF16), jax.ShapeDtypeStruct((t_tok, D), F32)]
    out_specs = [_row_spec(tm, D), _row_spec(tm, FF), _row_spec(tm, FF), _row_spec(tm, D)]
    if with_loss:
        out_shape += [jax.ShapeDtypeStruct((t_tok, D), F32), jax.ShapeDtypeStruct((1, LANES), F32)]
        out_specs += [_row_spec(tm, D), _const_spec((1, LANES))]
    return pl.pallas_call(
        body, name="ffn_fwd_" + names[0][:4], grid=(nt,), in_specs=in_specs, out_specs=out_specs, out_shape=out_shape,
        scratch_shapes=[pltpu.VMEM((FF, D), BF16)] * 3 + [pltpu.SemaphoreType.DMA((3 * NDEV,))],
        compiler_params=_params(1, VMEM_BIG))(*ins)


def _ffn_bwd_a(dh, f, a, b, gpost, wg, name_w2, tm=256):
    t_tok = dh.shape[0]
    nt = t_tok // tm
    pieces = _weight_pieces(name_w2)

    def body(dh_ref, f_ref, a_ref, b_ref, gpost_ref, wg_ref, da_ref, db_ref, dgp_ref, dw2_ref, w2_s, acc_s, stage_s, sems):
        i = pl.program_id(0)

        @pl.when(i == 0)
        def _():
            for cp in _start_loads(wg_ref, w2_s, pieces, sems, 0):
                cp.wait()
            acc_s[...] = jnp.zeros_like(acc_s)
            dgp_ref[...] = jnp.zeros_like(dgp_ref)

        fv = f_ref[...]
        _, fh, r = _rms_fwd(fv, gpost_ref[...])
        df, dg = _rms_bwd(0.5 * dh_ref[...], fh, r, gpost_ref[...])
        dgp_ref[...] += dg
        dfb = df.astype(BF16)
        for c0 in range(0, FF, FC):
            ds = _nt(dfb, w2_s[pl.ds(c0, FC), :])
            av = a_ref[:, pl.ds(c0, FC)].astype(F32)
            bv = b_ref[:, pl.ds(c0, FC)].astype(F32)
            sig, sl = _silu_parts(av)
            da_ref[:, pl.ds(c0, FC)] = (ds * bv * (sig * (1.0 + av * (1.0 - sig)))).astype(BF16)
            db_ref[:, pl.ds(c0, FC)] = (ds * sl).astype(BF16)
            acc_s[pl.ds(c0, FC), :] += _tn((sl * bv).astype(BF16), dfb)

        @pl.when(i == nt - 1)
        def _():
            _flush(acc_s, dw2_ref, stage_s, [(0, 0, FF)])

    return pl.pallas_call(
        body, name="ffn_bwd_a_" + name_w2[:4], grid=(nt,),
        in_specs=[_row_spec(tm, D), _row_spec(tm, D), _row_spec(tm, FF), _row_spec(tm, FF), _const_spec((1, D)), ANY],
        out_specs=[_row_spec(tm, FF), _row_spec(tm, FF), _const_spec((1, D)), ANY],
        out_shape=[jax.ShapeDtypeStruct((t_tok, FF), BF16), jax.ShapeDtypeStruct((t_tok, FF), BF16),
                   jax.ShapeDtypeStruct((1, D), F32), jax.ShapeDtypeStruct((FF, D), BF16)],
        scratch_shapes=[pltpu.VMEM((FF, D), BF16), pltpu.VMEM((FF, D), F32), pltpu.VMEM((256, D), BF16),
                        pltpu.SemaphoreType.DMA((NDEV,))],
        compiler_params=_params(1, VMEM_BIG))(dh, f, a, b, gpost, wg)


def _ffn_bwd_b(da, db, h, dh, gpre, wg, name_w1, name_w3, tm=256):
    t_tok = dh.shape[0]
    nt = t_tok // tm
    pieces = [_weight_pieces(name_w1), _weight_pieces(name_w3)]

    def body(da_ref, db_ref, h_ref, dh_ref, gpre_ref, wg_ref, dhin_ref, dgp_ref, dw1_ref, dw3_ref,
             w1_s, w3_s, acc1_s, acc3_s, stage_s, sems):
        i = pl.program_id(0)

        @pl.when(i == 0)
        def _():
            cps = _start_loads(wg_ref, w1_s, pieces[0], sems, 0) + _start_loads(wg_ref, w3_s, pieces[1], sems, NDEV)
            for cp in cps:
                cp.wait()
            acc1_s[...] = jnp.zeros_like(acc1_s)
            acc3_s[...] = jnp.zeros_like(acc3_s)
            dgp_ref[...] = jnp.zeros_like(dgp_ref)

        g = gpre_ref[...]
        n, xh, r = _rms_fwd(h_ref[...], g)
        nb = n.astype(BF16)
        dn = jnp.zeros((tm, D), F32)
        for c0 in range(0, FF, FC):
            dav = da_ref[:, pl.ds(c0, FC)]
            dbv = db_ref[:, pl.ds(c0, FC)]
            dn = dn + _nn(dav, w1_s[pl.ds(c0, FC), :]) + _nn(dbv, w3_s[pl.ds(c0, FC), :])
            acc1_s[pl.ds(c0, FC), :] += _tn(dav, nb)
            acc3_s[pl.ds(c0, FC), :] += _tn(dbv, nb)
        dx, dg = _rms_bwd(dn, xh, r, g)
        dgp_ref[...] += dg
        dhin_ref[...] = dh_ref[...] + dx

        @pl.when(i == nt - 1)
        def _():
            _flush(acc1_s, dw1_ref, stage_s, [(0, 0, FF)])
            _flush(acc3_s, dw3_ref, stage_s, [(0, 0, FF)])

    return pl.pallas_call(
        body, name="ffn_bwd_b_" + name_w1[:4], grid=(nt,),
        in_specs=[_row_spec(tm, FF), _row_spec(tm, FF), _row_spec(tm, D), _row_spec(tm, D), _const_spec((1, D)), ANY],
        out_specs=[_row_spec(tm, D), _const_spec((1, D)), ANY, ANY],
        out_shape=[jax.ShapeDtypeStruct((t_tok, D), F32), jax.ShapeDtypeStruct((1, D), F32),
                   jax.ShapeDtypeStruct((FF, D), BF16), jax.ShapeDtypeStruct((FF, D), BF16)],
        scratch_shapes=[pltpu.VMEM((FF, D), BF16), pltpu.VMEM((FF, D), BF16), pltpu.VMEM((FF, D), F32),
                        pltpu.VMEM((FF, D), F32), pltpu.VMEM((256, D), BF16), pltpu.SemaphoreType.DMA((2 * NDEV,))],
        compiler_params=_params(1, VMEM_BIG))(da, db, h, dh, gpre, wg)


Q0, K0, V0, XR0, XG0 = 0, D, D + N_KV * LANES, D + 2 * N_KV * LANES, 2 * D + 2 * N_KV * LANES


def _mix_proj_fwd(h, g, bgate, wg, tm=256):
    t_tok = h.shape[0]
    nt = t_tok // tm
    win_p = _win_pieces()
    wgate_p = _weight_pieces("w_gate")

    def body(h_ref, g_ref, bg_ref, wg_ref, q_ref, k_ref, v_ref, xr_ref, xg_ref, gs_ref, win_s, wgt_s, sems):
        i = pl.program_id(0)

        @pl.when(i == 0)
        def _():
            cps = _start_loads(wg_ref, win_s, win_p, sems, 0) + _start_loads(wg_ref, wgt_s, wgate_p, sems, len(win_p))
            for cp in cps:
                cp.wait()

        n, _, _ = _rms_fwd(h_ref[...], g_ref[...])
        nb = n.astype(BF16)
        q_ref[...] = _nt(nb, win_s[pl.ds(Q0, D), :]).astype(BF16)
        k_ref[...] = _nt(nb, win_s[pl.ds(K0, N_KV * LANES), :]).astype(BF16)
        v_ref[...] = _nt(nb, win_s[pl.ds(V0, N_KV * LANES), :]).astype(BF16)
        xr_ref[...] = _nt(nb, win_s[pl.ds(XR0, D), :])
        xg_ref[...] = _nt(nb, win_s[pl.ds(XG0, D), :])
        gs_ref[...] = jax.nn.sigmoid(_nt(nb, wgt_s[...]) + bg_ref[...])

    kvw = N_KV * LANES
    return pl.pallas_call(
        body, name="mix_proj_fwd", grid=(nt,),
        in_specs=[_row_spec(tm, D), _const_spec((1, D)), _const_spec((1, 2 * D)), ANY],
        out_specs=[_row_spec(tm, D), _row_spec(tm, kvw), _row_spec(tm, kvw), _row_spec(tm, D), _row_spec(tm, D),
                   _row_spec(tm, 2 * D)],
        out_shape=[jax.ShapeDtypeStruct((t_tok, D), BF16), jax.ShapeDtypeStruct((t_tok, kvw), BF16),
                   jax.ShapeDtypeStruct((t_tok, kvw), BF16), jax.ShapeDtypeStruct((t_tok, D), F32),
                   jax.ShapeDtypeStruct((t_tok, D), F32), jax.ShapeDtypeStruct((t_tok, 2 * D), F32)],
        scratch_shapes=[pltpu.VMEM((INP_W, D), BF16), pltpu.VMEM((2 * D, D), BF16),
                        pltpu.SemaphoreType.DMA((len(win_p) + NDEV,))],
        compiler_params=_params(1, VMEM_BIG))(h, g, bgate, wg)


def _mix_proj_bwd(dq, dk, dv, dxr, dxg, dgpre, h, dh, g, wg, tm=256):
    t_tok = h.shape[0]
    nt = t_tok // tm
    win_p = _win_pieces()
    wgate_p = _weight_pieces("w_gate")
    kv = N_KV * HEAD_DIM
    win_flush = [(Q0, 0, D)]
    win_flush += [(K0 + gi * LANES, D + gi * HEAD_DIM, HEAD_DIM) for gi in range(N_KV)]
    win_flush += [(V0 + gi * LANES, D + kv + gi * HEAD_DIM, HEAD_DIM) for gi in range(N_KV)]
    win_flush += [(XR0, D + 2 * kv, D), (XG0, 2 * D + 2 * kv, D)]

    def body(dq_ref, dk_ref, dv_ref, dxr_ref, dxg_ref, dgp_ref, h_ref, dh_ref, g_ref, wg_ref,
             dhin_ref, dg_ref, dbg_ref, dwin_ref, dwgt_ref, win_s, wgt_s, accin_s, accgt_s, stage_s, sems):
        i = pl.program_id(0)

        @pl.when(i == 0)
        def _():
            cps = _start_loads(wg_ref, win_s, win_p, sems, 0) + _start_loads(wg_ref, wgt_s, wgate_p, sems, len(win_p))
            for cp in cps:
                cp.wait()
            accin_s[...] = jnp.zeros_like(accin_s)
            accgt_s[...] = jnp.zeros_like(accgt_s)
            dg_ref[...] = jnp.zeros_like(dg_ref)
            dbg_ref[...] = jnp.zeros_like(dbg_ref)

        gv = g_ref[...]
        n, xh, r = _rms_fwd(h_ref[...], gv)
        nb = n.astype(BF16)
        dgp = dgp_ref[...]
        dbg_ref[...] += jnp.sum(dgp, axis=0, keepdims=True)
        dgpb = dgp.astype(BF16)
        du = _nn(dgpb, wgt_s[...])
        accgt_s[...] += _tn(dgpb, nb)
        for ref, r0, w in ((dq_ref, Q0, D), (dk_ref, K0, N_KV * LANES), (dv_ref, V0, N_KV * LANES),
                           (dxr_ref, XR0, D), (dxg_ref, XG0, D)):
            piece = ref[...].astype(BF16)
            du = du + _nn(piece, win_s[pl.ds(r0, w), :])
            accin_s[pl.ds(r0, w), :] += _tn(piece, nb)
        dx, dg = _rms_bwd(du, xh, r, gv)
        dg_ref[...] += dg
        dhin_ref[...] = dh_ref[...] + dx

        @pl.when(i == nt - 1)
        def _():
            _flush(accin_s, dwin_ref, stage_s, win_flush)
            _flush(accgt_s, dwgt_ref, stage_s, [(0, 0, 2 * D)])

    kvw = N_KV * LANES
    return pl.pallas_call(
        body, name="mix_proj_bwd", grid=(nt,),
        in_specs=[_row_spec(tm, D), _row_spec(tm, kvw), _row_spec(tm, kvw), _row_spec(tm, D), _row_spec(tm, D),
                  _row_spec(tm, 2 * D), _row_spec(tm, D), _row_spec(tm, D), _const_spec((1, D)), ANY],
        out_specs=[_row_spec(tm, D), _const_spec((1, D)), _const_spec((1, 2 * D)), ANY, ANY],
        out_shape=[jax.ShapeDtypeStruct((t_tok, D), F32), jax.ShapeDtypeStruct((1, D), F32),
                   jax.ShapeDtypeStruct((1, 2 * D), F32), jax.ShapeDtypeStruct((IN_W, D), BF16),
                   jax.ShapeDtypeStruct((2 * D, D), BF16)],
        scratch_shapes=[pltpu.VMEM((INP_W, D), BF16), pltpu.VMEM((2 * D, D), BF16), pltpu.VMEM((INP_W, D), F32),
                        pltpu.VMEM((2 * D, D), F32), pltpu.VMEM((256, D), BF16),
                        pltpu.SemaphoreType.DMA((len(win_p) + NDEV,))],
        compiler_params=_params(1, VMEM_BIG))(dq, dk, dv, dxr, dxg, dgpre, h, dh, g, wg)


def _shift_down(x, d, fill):
    row = lax.broadcasted_iota(jnp.int32, x.shape, 0)
    return jnp.where(row >= d, pltpu.roll(x, d, 0), fill)


def _shift_up(x, d, fill):
    rows = x.shape[0]
    row = lax.broadcasted_iota(jnp.int32, x.shape, 0)
    return jnp.where(row < rows - d, pltpu.roll(x, rows - d, 0), fill)


def _expm1(x):
    small = x * (1.0 + x * (0.5 + x * (1.0 / 6.0 + x * (1.0 / 24.0))))
    return jnp.where(jnp.abs(x) < 0.05, small, jnp.exp(x) - 1.0)


def _softplus(x):
    return jnp.maximum(x, 0.0) + jnp.log(1.0 + jnp.exp(-jnp.abs(x)))


_GELU_C = math.sqrt(2.0 / math.pi)


def _gelu_parts(x):
    th = jnp.tanh(_GELU_C * (x + 0.044715 * x * x * x))
    val = 0.5 * x * (1.0 + th)
    grad = 0.5 * (1.0 + th) + 0.5 * x * (1.0 - th * th) * _GELU_C * (1.0 + 3.0 * 0.044715 * x * x)
    return val, grad


def _lru_pre(x, halo, cw_ref, cb_ref, wa_ref, wx_ref, ba_ref, bx_ref, lam_ref):
    ext = jnp.concatenate([halo, x], axis=0)
    shifted = [x] + [pltpu.roll(ext, k, 0)[8:] for k in (1, 2, 3)]
    xc = cb_ref[...] + cw_ref[pl.ds(CONV_WIDTH - 1, 1), :] * x
    for k in (1, 2, 3):
        xc = xc + cw_ref[pl.ds(CONV_WIDTH - 1 - k, 1), :] * shifted[k]
    xcb = xc.astype(BF16)
    r = jax.nn.sigmoid(_nn(xcb, wa_ref[...]) + ba_ref[...])
    ig = jax.nn.sigmoid(_nn(xcb, wx_ref[...]) + bx_ref[...])
    sp = _softplus(-lam_ref[...])
    log_a = -LRU_C * r * sp
    a = jnp.exp(log_a)
    mult = jnp.sqrt(-_expm1(2.0 * log_a))
    return shifted, xc, xcb, r, ig, sp, a, mult


def _lru_specs(nt, reverse):
    def tt(t):
        return nt - 1 - t if reverse else t
    tile = pl.BlockSpec((LRU_ROWS, LANES), lambda cb, t: (tt(t), cb))
    halo = pl.BlockSpec((8, LANES), lambda cb, t: (jnp.maximum(tt(t) * (LRU_ROWS // 8) - 1, 0), cb))
    vec = pl.BlockSpec((1, LANES), lambda cb, t: (0, cb))
    cw = pl.BlockSpec((CONV_WIDTH, LANES), lambda cb, t: (0, cb))
    mat = pl.BlockSpec((None, LANES, LANES), lambda cb, t: (cb, 0, 0))
    return tile, halo, vec, cw, mat


def _lru_fwd(xr, xg, cw, cb, wa, wx, ba, bx, lam):
    t_tok = xr.shape[0]
    nt = t_tok // LRU_ROWS
    rows = LRU_ROWS

    def body(xr_ref, xg_ref, cw_ref, cb_ref, wa_ref, wx_ref, ba_ref, bx_ref, lam_ref, y_ref, h_ref, tail_s, hc_s):
        t = pl.program_id(1)

        @pl.when(t == 0)
        def _():
            tail_s[...] = jnp.zeros_like(tail_s)
            hc_s[...] = jnp.zeros_like(hc_s)

        x = xr_ref[...]
        _, xc, _, _, ig, _, a, mult = _lru_pre(x, tail_s[...], cw_ref, cb_ref, wa_ref, wx_ref, ba_ref, bx_ref, lam_ref)
        tail_s[...] = xr_ref[pl.ds(rows - 8, 8), :]
        acc_a, acc_b = a, mult * (ig * xc)
        d = 1
        while d < rows:
            acc_b = acc_a * _shift_down(acc_b, d, 0.0) + acc_b
            acc_a = acc_a * _shift_down(acc_a, d, 1.0)
            d *= 2
        hv = acc_b + acc_a * hc_s[...]
        h_ref[...] = hv
        hc_s[...] = h_ref[pl.ds(rows - 1, 1), :]
        gl, _ = _gelu_parts(xg_ref[...])
        y_ref[...] = (hv * gl).astype(BF16)

    tile, _, vec, cws, mat = _lru_specs(nt, False)
    return pl.pallas_call(
        body, name="lru_fwd", grid=(D // LANES, nt),
        in_specs=[tile, tile, cws, vec, mat, mat, vec, vec, vec],
        out_specs=[tile, tile],
        out_shape=[jax.ShapeDtypeStruct((t_tok, D), BF16), jax.ShapeDtypeStruct((t_tok, D), F32)],
        scratch_shapes=[pltpu.VMEM((8, LANES), F32), pltpu.VMEM((1, LANES), F32)],
        compiler_params=_params(2))(xr, xg, cw, cb, wa, wx, ba, bx, lam)


def _lru_bwd(dy, xr, xg, hseq, cw, cb, wa, wx, ba, bx, lam):
    t_tok = xr.shape[0]
    nt = t_tok // LRU_ROWS
    rows = LRU_ROWS

    def body(dy_ref, xr_ref, xrh_ref, xg_ref, h_ref, hh_ref, cw_ref, cb_ref, wa_ref, wx_ref, ba_ref, bx_ref, lam_ref,
             dxr_ref, dxg_ref, dvec_ref, dwa_ref, dwx_ref, gcar_s, acar_s, head_s, tmp_s):
        t = pl.program_id(1)
        first_tile = t == nt - 1

        @pl.when(t == 0)
        def _():
            gcar_s[...] = jnp.zeros_like(gcar_s)
            acar_s[...] = jnp.zeros_like(acar_s)
            head_s[...] = jnp.zeros_like(head_s)
            dvec_ref[...] = jnp.zeros_like(dvec_ref)
            dwa_ref[...] = jnp.zeros_like(dwa_ref)
            dwx_ref[...] = jnp.zeros_like(dwx_ref)

        x = xr_ref[...]
        halo = jnp.where(first_tile, 0.0, xrh_ref[...])
        shifted, xc, xcb, r, ig, sp, a, mult = _lru_pre(x, halo, cw_ref, cb_ref, wa_ref, wx_ref, ba_ref, bx_ref, lam_ref)
        hv = h_ref[...]
        dyv = dy_ref[...]
        gl, glg = _gelu_parts(xg_ref[...])
        dxg_ref[...] = dyv * hv * glg
        acc_a = _shift_up(a, 1, acar_s[...])
        acc_b = dyv * gl
        d = 1
        while d < rows:
            acc_b = acc_a * _shift_up(acc_b, d, 0.0) + acc_b
            acc_a = acc_a * _shift_up(acc_a, d, 1.0)
            d *= 2
        g = acc_b + acc_a * gcar_s[...]
        hhalo = jnp.where(first_tile, 0.0, hh_ref[...])
        hprev = pltpu.roll(jnp.concatenate([hhalo, hv], axis=0), 1, 0)[8:]
        dmult = g * ig * xc
        dlog_a = a * (g * hprev) - dmult * a * a / mult
        dig = g * mult * xc
        dxc = g * mult * ig
        dzr = (dlog_a * (-LRU_C * sp)) * r * (1.0 - r)
        dzx = dig * ig * (1.0 - ig)
        dzrb, dzxb = dzr.astype(BF16), dzx.astype(BF16)
        dxc = dxc + _nt(dzrb, wa_ref[...]) + _nt(dzxb, wx_ref[...])
        dwa_ref[...] += _tn(xcb, dzrb)
        dwx_ref[...] += _tn(xcb, dzxb)
        dsp = jnp.sum(dlog_a * (-LRU_C * r), axis=0, keepdims=True)
        dlam = dsp * (-jax.nn.sigmoid(-lam_ref[...]))
        vrow = lax.broadcasted_iota(jnp.int32, (8, LANES), 0)
        upd = jnp.where(vrow == 4, jnp.sum(dxc, axis=0, keepdims=True), 0.0)
        upd = jnp.where(vrow == 5, jnp.sum(dzr, axis=0, keepdims=True), upd)
        upd = jnp.where(vrow == 6, jnp.sum(dzx, axis=0, keepdims=True), upd)
        upd = jnp.where(vrow == 7, dlam, upd)
        for k in range(CONV_WIDTH):
            upd = jnp.where(vrow == CONV_WIDTH - 1 - k, jnp.sum(dxc * shifted[k], axis=0, keepdims=True), upd)
        dvec_ref[...] += upd
        ext = jnp.concatenate([dxc, head_s[...]], axis=0)
        dxr = cw_ref[pl.ds(CONV_WIDTH - 1, 1), :] * dxc
        for k in (1, 2, 3):
            dxr = dxr + cw_ref[pl.ds(CONV_WIDTH - 1 - k, 1), :] * pltpu.roll(ext, rows + 8 - k, 0)[:rows]
        dxr_ref[...] = dxr
        tmp_s[...] = g
        gcar_s[...] = tmp_s[pl.ds(0, 1), :]
        tmp_s[...] = a
        acar_s[...] = tmp_s[pl.ds(0, 1), :]
        tmp_s[...] = dxc
        head_s[...] = tmp_s[pl.ds(0, 8), :]

    tile, halo, vec, cws, mat = _lru_specs(nt, True)
    return pl.pallas_call(
        body, name="lru_bwd", grid=(D // LANES, nt),
        in_specs=[tile, tile, halo, tile, tile, halo, cws, vec, mat, mat, vec, vec, vec],
        out_specs=[tile, tile, pl.BlockSpec((8, LANES), lambda cb, t: (0, cb)), mat, mat],
        out_shape=[jax.ShapeDtypeStruct((t_tok, D), F32), jax.ShapeDtypeStruct((t_tok, D), F32),
                   jax.ShapeDtypeStruct((8, D), F32), jax.ShapeDtypeStruct((D // LANES, LANES, LANES), F32),
                   jax.ShapeDtypeStruct((D // LANES, LANES, LANES), F32)],
        scratch_shapes=[pltpu.VMEM((1, LANES), F32), pltpu.VMEM((1, LANES), F32), pltpu.VMEM((8, LANES), F32),
                        pltpu.VMEM((rows, LANES), F32)],
        compiler_params=_params(2))(dy, xr, xr, xg, hseq, hseq, cw, cb, wa, wx, ba, bx, lam)


def _t5_bucket_np(rel):
    nb = N_BUCKETS // 2
    max_exact = nb // 2
    ret = np.where(rel > 0, nb, 0)
    n = np.abs(rel)
    nf = np.maximum(n, 1).astype(np.float32)
    large = max_exact + (np.log(nf / np.float32(max_exact)) / np.float32(math.log(MAX_DISTANCE / max_exact))
                         * np.float32(nb - max_exact)).astype(np.int32)
    large = np.minimum(large, nb - 1)
    return ret + np.where(n < max_exact, n, large)


def _bucket_map():
    r = np.arange(QT)[:, None]
    c = np.arange(KW)[None, :]
    j = c - (r // CHUNK) * CHUNK
    band = (j >= 0) & (j < WINDOW + CHUNK)
    return np.where(band, _t5_bucket_np(c - r - WINDOW), -1).astype(np.int32)


def _attn_specs(nt, reverse):
    def tt(i):
        return nt - 1 - i if reverse else i
    kvw = N_KV * LANES
    qs = pl.BlockSpec((QT, D), lambda i: (tt(i), 0))
    cur = pl.BlockSpec((QT, kvw), lambda i: (tt(i), 0))
    prev = pl.BlockSpec((WINDOW, kvw), lambda i: (jnp.maximum(tt(i) * (QT // WINDOW) - 1, 0), 0))
    lse = pl.BlockSpec((QT, LANES), lambda i: (tt(i), 0))
    return qs, cur, prev, lse


def _attn_fwd(q, kd, vd, bias, sinks):
    t_tok = q.shape[0]
    nt = t_tok // QT

    def body(q_ref, kp_ref, kc_ref, vp_ref, vc_ref, bias_ref, sink_ref, o_ref, lse_ref):
        i = pl.program_id(0)
        col = lax.broadcasted_iota(jnp.int32, (1, KW), 1)
        first = jnp.where((i == 0) & (col < WINDOW), NEG_INF, 0.0)
        lane = lax.broadcasted_iota(jnp.int32, (QT, LANES), 1)
        lse_t = jnp.zeros((QT, LANES), F32)
        for g in range(N_KV):
            kwin = jnp.concatenate([kp_ref[:, pl.ds(g * LANES, LANES)], kc_ref[:, pl.ds(g * LANES, LANES)]], axis=0)
            vwin = jnp.concatenate([vp_ref[:, pl.ds(g * LANES, LANES)], vc_ref[:, pl.ds(g * LANES, LANES)]], axis=0)
            for slab in (2 * g, 2 * g + 1):
                qs = q_ref[:, pl.ds(slab * LANES, LANES)]
                o_slab = jnp.zeros((QT, LANES), F32)
                for half in range(2):
                    hd = 2 * slab + half
                    mine = (lane >= half * HEAD_DIM) & (lane < (half + 1) * HEAD_DIM)
                    qh = jnp.where(mine, qs, jnp.zeros_like(qs)) * jnp.asarray(HEAD_DIM ** -0.5, BF16)
                    s = _nt(qh, kwin) + (bias_ref[hd] + first)
                    sk = sink_ref[hd]
                    m = jnp.maximum(jnp.max(s, axis=-1, keepdims=True), sk)
                    e = jnp.exp(s - m)
                    l = jnp.sum(e, axis=-1, keepdims=True) + jnp.exp(sk - m)
                    p = e / l
                    o_slab = jnp.where(mine, _nn(p.astype(BF16), vwin), o_slab)
                    lse_t = jnp.where(lane == hd, m + jnp.log(l), lse_t)
                o_ref[:, pl.ds(slab * LANES, LANES)] = o_slab.astype(BF16)
        lse_ref[...] = lse_t

    qs, cur, prev, lse = _attn_specs(nt, False)
    return pl.pallas_call(
        body, name="attn_fwd", grid=(nt,),
        in_specs=[qs, prev, cur, prev, cur, _const_spec((N_HEADS, QT, KW)), pl.BlockSpec(memory_space=pltpu.SMEM)],
        out_specs=[qs, lse],
        out_shape=[jax.ShapeDtypeStruct((t_tok, D), BF16), jax.ShapeDtypeStruct((t_tok, LANES), F32)],
        compiler_params=_params(1, 48 * 2 ** 20))(q, kd, kd, vd, vd, bias, sinks)


def _attn_bwd(q, kd, vd, o, do, lse, bias, sinks):
    t_tok = q.shape[0]
    nt = t_tok // QT
    kvw = N_KV * LANES

    def body(q_ref, kp_ref, kc_ref, vp_ref, vc_ref, o_ref, do_ref, lse_ref, bias_ref, sink_ref,
             dq_ref, dk_ref, dv_ref, ds_ref, dsink_ref, kcar_s, vcar_s):
        i = pl.program_id(0)
        tile = nt - 1 - i

        @pl.when(i == 0)
        def _():
            kcar_s[...] = jnp.zeros_like(kcar_s)
            vcar_s[...] = jnp.zeros_like(vcar_s)
            ds_ref[...] = jnp.zeros_like(ds_ref)
            dsink_ref[...] = jnp.zeros_like(dsink_ref)

        col = lax.broadcasted_iota(jnp.int32, (1, KW), 1)
        first = jnp.where((tile == 0) & (col < WINDOW), NEG_INF, 0.0)
        lane = lax.broadcasted_iota(jnp.int32, (QT, LANES), 1)
        lane_k = lax.broadcasted_iota(jnp.int32, (KW, LANES), 1)
        lane_1 = lax.broadcasted_iota(jnp.int32, (1, LANES), 1)
        lse_t = lse_ref[...]
        dsink = jnp.zeros((1, LANES), F32)
        for g in range(N_KV):
            kwin = jnp.concatenate([kp_ref[:, pl.ds(g * LANES, LANES)], kc_ref[:, pl.ds(g * LANES, LANES)]], axis=0)
            vwin = jnp.concatenate([vp_ref[:, pl.ds(g * LANES, LANES)], vc_ref[:, pl.ds(g * LANES, LANES)]], axis=0)
            dk_acc = jnp.zeros((KW, LANES), F32)
            dv_acc = jnp.zeros((KW, LANES), F32)
            for slab in (2 * g, 2 * g + 1):
                qs = q_ref[:, pl.ds(slab * LANES, LANES)]
                dos = do_ref[:, pl.ds(slab * LANES, LANES)]
                od = dos.astype(F32) * o_ref[:, pl.ds(slab * LANES, LANES)].astype(F32)
                dq_slab = jnp.zeros((QT, LANES), F32)
                for half in range(2):
                    hd = 2 * slab + half
                    mine = (lane >= half * HEAD_DIM) & (lane < (half + 1) * HEAD_DIM)
                    qh = jnp.where(mine, qs, jnp.zeros_like(qs)) * jnp.asarray(HEAD_DIM ** -0.5, BF16)
                    doh = jnp.where(mine, dos, jnp.zeros_like(dos))
                    s = _nt(qh, kwin) + (bias_ref[hd] + first)
                    lse_h = jnp.sum(jnp.where(lane == hd, lse_t, 0.0), axis=-1, keepdims=True)
                    p = jnp.exp(s - lse_h)
                    dp = _nt(doh, vwin)
                    drow = jnp.sum(jnp.where(mine, od, 0.0), axis=-1, keepdims=True)
                    ds = p * (dp - drow)
                    ds_ref[hd] += ds
                    psink = jnp.exp(sink_ref[hd] - lse_h)
                    dsink = dsink + jnp.where(lane_1 == hd, -jnp.sum(psink * drow, axis=0, keepdims=True), 0.0)
                    dsb = ds.astype(BF16)
                    dq_slab = jnp.where(mine, _nn(dsb, kwin) * (HEAD_DIM ** -0.5), dq_slab)
                    dk_acc = dk_acc + _tn(dsb, qh)
                    dv_acc = dv_acc + _tn(p.astype(BF16), doh)
                dq_ref[:, pl.ds(slab * LANES, LANES)] = dq_slab.astype(BF16)
            dk_f = jnp.where(lane_k < HEAD_DIM, dk_acc + pltpu.roll(dk_acc, HEAD_DIM, 1), 0.0)
            dv_f = jnp.where(lane_k < HEAD_DIM, dv_acc + pltpu.roll(dv_acc, HEAD_DIM, 1), 0.0)
            for acc, out_ref, car in ((dk_f, dk_ref, kcar_s), (dv_f, dv_ref, vcar_s)):
                cs = pl.ds(g * LANES, LANES)
                out_ref[pl.ds(0, QT - WINDOW), cs] = acc[WINDOW:QT].astype(BF16)
                out_ref[pl.ds(QT - WINDOW, WINDOW), cs] = (acc[QT:KW] + car[:, cs]).astype(BF16)
                car[:, cs] = acc[0:WINDOW]
        dsink_ref[...] += dsink

    qs, cur, prev, lse_s = _attn_specs(nt, True)
    return pl.pallas_call(
        body, name="attn_bwd", grid=(nt,),
        in_specs=[qs, prev, cur, prev, cur, qs, qs, lse_s, _const_spec((N_HEADS, QT, KW)),
                  pl.BlockSpec(memory_space=pltpu.SMEM)],
        out_specs=[qs, cur, cur, _const_spec((N_HEADS, QT, KW)), _const_spec((1, LANES))],
        out_shape=[jax.ShapeDtypeStruct((t_tok, D), BF16), jax.ShapeDtypeStruct((t_tok, kvw), BF16),
                   jax.ShapeDtypeStruct((t_tok, kvw), BF16), jax.ShapeDtypeStruct((N_HEADS, QT, KW), F32),
                   jax.ShapeDtypeStruct((1, LANES), F32)],
        scratch_shapes=[pltpu.VMEM((WINDOW, kvw), F32), pltpu.VMEM((WINDOW, kvw), F32)],
        compiler_params=_params(1, VMEM_BIG))(q, kd, kd, vd, vd, o, do, lse, bias, sinks)


def _bias_tile(table, bmap):
    def body(tab_ref, bm_ref, out_ref):
        bm = bm_ref[...]

        def per_head(hd, carry):
            acc = jnp.full((QT, KW), NEG_INF, F32)
            for b in range(N_BUCKETS):
                acc = jnp.where(bm == b, tab_ref[b, hd], acc)
            out_ref[hd] = acc
            return carry

        lax.fori_loop(0, N_HEADS, per_head, 0)

    return pl.pallas_call(
        body, name="bias_tile", out_shape=jax.ShapeDtypeStruct((N_HEADS, QT, KW), F32),
        in_specs=[pl.BlockSpec(memory_space=pltpu.SMEM), pl.BlockSpec(memory_space=pltpu.VMEM)],
        out_specs=pl.BlockSpec(memory_space=pltpu.VMEM))(table, bmap)


def _bias_grad(ds_acc, bmap):
    def body(ds_ref, bm_ref, out_ref):
        row = lax.broadcasted_iota(jnp.int32, (N_BUCKETS, LANES), 0)
        lane = lax.broadcasted_iota(jnp.int32, (N_BUCKETS, LANES), 1)
        bm = bm_ref[...]

        def per_head(hd, res):
            dsv = ds_ref[hd]
            for b in range(N_BUCKETS):
                val = jnp.sum(jnp.sum(jnp.where(bm == b, dsv, 0.0), axis=0, keepdims=True), axis=1, keepdims=True)
                res = jnp.where((row == b) & (lane == hd), val, res)
            return res

        out_ref[...] = lax.fori_loop(0, N_HEADS, per_head, jnp.zeros((N_BUCKETS, LANES), F32))

    return pl.pallas_call(
        body, name="bias_grad", out_shape=jax.ShapeDtypeStruct((N_BUCKETS, LANES), F32),
        in_specs=[pl.BlockSpec(memory_space=pltpu.VMEM), pl.BlockSpec(memory_space=pltpu.VMEM)],
        out_specs=pl.BlockSpec(memory_space=pltpu.VMEM))(ds_acc, bmap)


def _mix_out_fwd(ya_in, o, gs, h, g, wg, tm=256):
    t_tok = h.shape[0]
    nt = t_tok // tm
    pieces = [_weight_pieces(n) for n in ("w_lru_out", "w_attn_out", "w_o")]

    def body(ya_ref, o_ref, gs_ref, h_ref, g_ref, wg_ref, hout_ref, yao_ref, ybo_ref, z_ref, wa_s, wb_s, wo_s, sems):
        i = pl.program_id(0)

        @pl.when(i == 0)
        def _():
            cps = []
            for k, dst in enumerate((wa_s, wb_s, wo_s)):
                cps += _start_loads(wg_ref, dst, pieces[k], sems, k * NDEV)
            for cp in cps:
                cp.wait()

        ya = _nn(ya_ref[...], wa_s[...])
        yb = _nn(o_ref[...], wb_s[...])
        yao_ref[...] = ya
        ybo_ref[...] = yb
        merged = gs_ref[:, pl.ds(0, D)] * ya + gs_ref[:, pl.ds(D, D)] * yb
        z = _nn(merged.astype(BF16), wo_s[...])
        z_ref[...] = z
        zn, _, _ = _rms_fwd(z, g_ref[...])
        hout_ref[...] = h_ref[...] + zn

    return pl.pallas_call(
        body, name="mix_out_fwd", grid=(nt,),
        in_specs=[_row_spec(tm, D), _row_spec(tm, D), _row_spec(tm, 2 * D), _row_spec(tm, D), _const_spec((1, D)), ANY],
        out_specs=[_row_spec(tm, D)] * 4,
        out_shape=[jax.ShapeDtypeStruct((t_tok, D), F32)] * 4,
        scratch_shapes=[pltpu.VMEM((D, D), BF16)] * 3 + [pltpu.SemaphoreType.DMA((3 * NDEV,))],
        compiler_params=_params(1, 48 * 2 ** 20))(ya_in, o, gs, h, g, wg)


def _mix_out_bwd(dh, z, ya, yb, gs, ya_in, o, g, wg, tm=256):
    t_tok = dh.shape[0]
    nt = t_tok // tm
    pieces = [_weight_pieces(n) for n in ("w_lru_out", "w_attn_out", "w_o")]

    def body(dh_ref, z_ref, ya_ref, yb_ref, gs_ref, yain_ref, o_ref, g_ref, wg_ref,
             dyain_ref, do_ref, dgpre_ref, dg_ref, dwa_ref, dwb_ref, dwo_ref,
             wa_s, wb_s, wo_s, acca_s, accb_s, acco_s, stage_s, sems):
        i = pl.program_id(0)

        @pl.when(i == 0)
        def _():
            cps = []
            for k, dst in enumerate((wa_s, wb_s, wo_s)):
                cps += _start_loads(wg_ref, dst, pieces[k], sems, k * NDEV)
            for cp in cps:
                cp.wait()
            for acc in (acca_s, accb_s, acco_s):
                acc[...] = jnp.zeros_like(acc)
            dg_ref[...] = jnp.zeros_like(dg_ref)

        gv = g_ref[...]
        _, zh, r = _rms_fwd(z_ref[...], gv)
        dz, dg = _rms_bwd(dh_ref[...], zh, r, gv)
        dg_ref[...] += dg
        dzb = dz.astype(BF16)
        ga, gb = gs_ref[:, pl.ds(0, D)], gs_ref[:, pl.ds(D, D)]
        ya_v, yb_v = ya_ref[...], yb_ref[...]
        merged = ga * ya_v + gb * yb_v
        acco_s[...] += _tn(merged.astype(BF16), dzb)
        dm = _nt(dzb, wo_s[...])
        dgpre_ref[:, pl.ds(0, D)] = dm * ya_v * ga * (1.0 - ga)
        dgpre_ref[:, pl.ds(D, D)] = dm * yb_v * gb * (1.0 - gb)
        dya = (dm * ga).astype(BF16)
        dyb = (dm * gb).astype(BF16)
        dyain_ref[...] = _nt(dya, wa_s[...])
        do_ref[...] = _nt(dyb, wb_s[...]).astype(BF16)
        acca_s[...] += _tn(yain_ref[...], dya)
        accb_s[...] += _tn(o_ref[...], dyb)

        @pl.when(i == nt - 1)
        def _():
            _flush(acca_s, dwa_ref, stage_s, [(0, 0, D)])
            _flush(accb_s, dwb_ref, stage_s, [(0, 0, D)])
            _flush(acco_s, dwo_ref, stage_s, [(0, 0, D)])

    return pl.pallas_call(
        body, name="mix_out_bwd", grid=(nt,),
        in_specs=[_row_spec(tm, D)] * 4 + [_row_spec(tm, 2 * D), _row_spec(tm, D), _row_spec(tm, D),
                                            _const_spec((1, D)), ANY],
        out_specs=[_row_spec(tm, D), _row_spec(tm, D), _row_spec(tm, 2 * D), _const_spec((1, D)), ANY, ANY, ANY],
        out_shape=[jax.ShapeDtypeStruct((t_tok, D), F32), jax.ShapeDtypeStruct((t_tok, D), BF16),
                   jax.ShapeDtypeStruct((t_tok, 2 * D), F32), jax.ShapeDtypeStruct((1, D), F32)]
        + [jax.ShapeDtypeStruct((D, D), BF16)] * 3,
        scratch_shapes=[pltpu.VMEM((D, D), BF16)] * 3 + [pltpu.VMEM((D, D), F32)] * 3
        + [pltpu.VMEM((256, D), BF16), pltpu.SemaphoreType.DMA((3 * NDEV,))],
        compiler_params=_params(1, VMEM_BIG))(dh, z, ya, yb, gs, ya_in, o, g, wg)


MESH = pl.DeviceIdType.MESH


def _allgather(shard, name):
    m_per, n = shard.shape

    def body(x_ref, out_ref, send_sems, recv_sems, local_sem):
        x, y, c = lax.axis_index("x"), lax.axis_index("y"), lax.axis_index("c")
        me, sibling = (x, y, c), (x, y, 1 - c)
        chips = [(1 - x, y), (x, 1 - y), (1 - x, 1 - y)]

        def rows(px, py, pc):
            return out_ref.at[pl.ds((4 * px + 2 * py + pc) * m_per, m_per), :]

        def copy(k, block, to, src=None):
            return pltpu.make_async_remote_copy(
                src_ref=rows(*block) if src is None else src, dst_ref=rows(*block),
                send_sem=send_sems.at[k], recv_sem=recv_sems.at[k], device_id=to, device_id_type=MESH)

        mine = pltpu.make_async_copy(x_ref, rows(*me), local_sem)
        mine.start()
        first = [copy(0, me, sibling, src=x_ref)]
        first += [copy(1 + j, me, (*chip, c), src=x_ref) for j, chip in enumerate(chips)]
        for cp in first:
            cp.start()
        passed = [copy(4 + j, (*chip, c), sibling) for j, chip in enumerate(chips)]
        for j, chip in enumerate(chips):
            copy(1 + j, (*chip, c), me).wait_recv()
            passed[j].start()
        copy(0, sibling, me).wait_recv()
        for j, chip in enumerate(chips):
            copy(4 + j, (*chip, 1 - c), me).wait_recv()
        for cp in first + passed:
            cp.wait_send()
        mine.wait()

    return pl.pallas_call(
        body, name=name, out_shape=jax.ShapeDtypeStruct((NDEV * m_per, n), shard.dtype),
        in_specs=[ANY], out_specs=ANY,
        scratch_shapes=[pltpu.SemaphoreType.DMA((7,)), pltpu.SemaphoreType.DMA((7,)), pltpu.SemaphoreType.DMA],
    )(shard)


def _reduce_scatter_send(grads):
    nw = len(grads)
    rows = [g.shape[0] // NDEV for g in grads]

    def body(*refs):
        g_refs, r_refs = refs[:nw], refs[nw:2 * nw]
        send_sems, recv_sems, local_sems = refs[2 * nw:]
        x, y, c = lax.axis_index("x"), lax.axis_index("y"), lax.axis_index("c")
        me = 4 * x + 2 * y + c
        locals_ = []
        for w in range(nw):
            cp = pltpu.make_async_copy(g_refs[w].at[pl.ds(me * rows[w], rows[w])], r_refs[w].at[me], local_sems.at[w])
            cp.start()
            locals_.append(cp)
        sends = []
        for k in range(1, NDEV):
            px, py, pc = x ^ (k >> 2), y ^ ((k >> 1) & 1), c ^ (k & 1)
            peer = 4 * px + 2 * py + pc
            for w in range(nw):
                cp = pltpu.make_async_remote_copy(
                    src_ref=g_refs[w].at[pl.ds(peer * rows[w], rows[w])], dst_ref=r_refs[w].at[me],
                    send_sem=send_sems.at[w, k - 1], recv_sem=recv_sems.at[w, k - 1],
                    device_id=(px, py, pc), device_id_type=MESH)
                cp.start()
                sends.append(cp)
        for k in range(1, NDEV):
            px, py, pc = x ^ (k >> 2), y ^ ((k >> 1) & 1), c ^ (k & 1)
            peer = 4 * px + 2 * py + pc
            for w in range(nw):
                pltpu.make_async_remote_copy(
                    src_ref=g_refs[w].at[pl.ds(0, rows[w])], dst_ref=r_refs[w].at[peer],
                    send_sem=send_sems.at[w, k - 1], recv_sem=recv_sems.at[w, k - 1],
                    device_id=(px, py, pc), device_id_type=MESH).wait_recv()
        for cp in sends:
            cp.wait_send()
        for cp in locals_:
            cp.wait()

    return pl.pallas_call(
        body, name="reduce_scatter_send",
        out_shape=[jax.ShapeDtypeStruct((NDEV, r, g.shape[1]), g.dtype) for g, r in zip(grads, rows)],
        in_specs=[ANY] * nw, out_specs=[ANY] * nw,
        scratch_shapes=[pltpu.SemaphoreType.DMA((nw, NDEV - 1)), pltpu.SemaphoreType.DMA((nw, NDEV - 1)),
                        pltpu.SemaphoreType.DMA((nw,))],
    )(*grads)


def _sum_parts(parts_list):
    n = len(parts_list)
    _, r, c = parts_list[0].shape
    tc = 256

    def body(*refs):
        for p_ref, o_ref in zip(refs[:n], refs[n:]):
            acc = p_ref[0].astype(F32)
            for s in range(1, NDEV):
                acc = acc + p_ref[s].astype(F32)
            o_ref[...] = acc

    return pl.pallas_call(
        body, name=f"sum_parts_{r}", grid=(c // tc,),
        in_specs=[pl.BlockSpec((NDEV, r, tc), lambda i: (0, 0, i))] * n,
        out_specs=[pl.BlockSpec((r, tc), lambda i: (0, i))] * n,
        out_shape=[jax.ShapeDtypeStruct((r, c), F32)] * n,
        compiler_params=_params(1, 48 * 2 ** 20))(*parts_list)


def _adamw_math(w, g, m, v):
    m = ADAM_B1 * m + (1.0 - ADAM_B1) * g
    v = ADAM_B2 * v + (1.0 - ADAM_B2) * (g * g)
    m_hat = m / (1.0 - ADAM_B1 ** ADAM_STEP)
    v_hat = v / (1.0 - ADAM_B2 ** ADAM_STEP)
    delta = -ADAM_LR * (m_hat / (jnp.sqrt(v_hat) + ADAM_EPS) + ADAM_WD * w)
    return delta, m, v


def _adamw(items):
    n = len(items)
    r, c = items[0][0].shape
    tr = r if r * c <= 2 ** 18 else max(t for t in range(8, 65, 8) if r % t == 0)

    def body(*refs):
        for k in range(n):
            g_ref, w_ref, m_ref, v_ref = refs[4 * k:4 * k + 4]
            d_ref, nm_ref, nv_ref = refs[4 * n + 3 * k:4 * n + 3 * k + 3]
            d, m, v = _adamw_math(w_ref[...], g_ref[...], m_ref[...], v_ref[...])
            d_ref[...] = d
            nm_ref[...] = m
            nv_ref[...] = v

    spec = pl.BlockSpec((tr, c), lambda i: (i, 0))
    outs = pl.pallas_call(
        body, name=f"adamw_{r}x{c}", grid=(r // tr,),
        in_specs=[spec] * (4 * n), out_specs=[spec] * (3 * n),
        out_shape=[jax.ShapeDtypeStruct((r, c), F32)] * (3 * n),
        compiler_params=_params(1, 40 * 2 ** 20))(*[a for it in items for a in it])
    return [tuple(outs[3 * k:3 * k + 3]) for k in range(n)]


def _adamw_small(items):
    n = len(items)

    def body(*refs):
        for k in range(n):
            g_ref, w_ref, m_ref, v_ref = refs[4 * k:4 * k + 4]
            d_ref, nm_ref, nv_ref = refs[4 * n + 3 * k:4 * n + 3 * k + 3]
            d, m, v = _adamw_math(w_ref[...], g_ref[...], m_ref[...], v_ref[...])
            d_ref[...] = d
            nm_ref[...] = m
            nv_ref[...] = v

    vm = pl.BlockSpec(memory_space=pltpu.VMEM)
    outs = pl.pallas_call(
        body, name="adamw_small", in_specs=[vm] * (4 * n), out_specs=[vm] * (3 * n),
        out_shape=[jax.ShapeDtypeStruct(it[1].shape, F32) for it in items for _ in range(3)],
    )(*[a for it in items for a in it])
    return [tuple(outs[3 * k:3 * k + 3]) for k in range(n)]


def _pack_small(arrs):
    rows, offs = [], []
    total = 0
    for a in arrs:
        flat = a.reshape(-1).astype(F32)
        nr = -(-flat.shape[0] // LANES)
        flat = jnp.pad(flat, (0, nr * LANES - flat.shape[0]))
        rows.append(flat.reshape(nr, LANES))
        offs.append((total, nr))
        total += nr
    pad = -total % 8
    if pad:
        rows.append(jnp.zeros((pad, LANES), F32))
    return jnp.concatenate(rows, axis=0), offs


def _unpack_small(pack, offs, shapes):
    out = []
    for (o, nr), shp in zip(offs, shapes):
        size = int(np.prod(shp))
        out.append(pack[o:o + nr].reshape(-1)[:size].reshape(shp))
    return out


def _sum_small(gathered, rows):
    def body(p_ref, o_ref):
        acc = p_ref[pl.ds(0, rows), :]
        for s in range(1, NDEV):
            acc = acc + p_ref[pl.ds(s * rows, rows), :]
        o_ref[...] = acc

    return pl.pallas_call(
        body, name="sum_small", out_shape=jax.ShapeDtypeStruct((rows, LANES), F32),
        in_specs=[pl.BlockSpec(memory_space=pltpu.VMEM)], out_specs=pl.BlockSpec(memory_space=pltpu.VMEM))(gathered)


def _block_diag(w):
    w = w.reshape(D // LANES, 2, LRU_BLOCK, LRU_BLOCK)
    z = jnp.zeros((D // LANES, LRU_BLOCK, LRU_BLOCK), w.dtype)
    top = jnp.concatenate([w[:, 0], z], axis=2)
    bot = jnp.concatenate([z, w[:, 1]], axis=2)
    return jnp.concatenate([top, bot], axis=1)


def _block_diag_grad(dw):
    a = dw[:, :LRU_BLOCK, :LRU_BLOCK]
    b = dw[:, LRU_BLOCK:, LRU_BLOCK:]
    return jnp.stack([a, b], axis=1).reshape(D // LRU_BLOCK, LRU_BLOCK, LRU_BLOCK)


def _local_step(x, target, wg, sm):
    vec = lambda n: sm[n].reshape(1, -1)
    bmap = jnp.asarray(_bucket_map())
    bias = _bias_tile(sm["rel_bias"], bmap)
    sinks = sm["attn_sinks"].reshape(N_HEADS)
    wa_bd = _block_diag(sm["rg_a_w"].reshape(D // LRU_BLOCK, LRU_BLOCK, LRU_BLOCK)).astype(BF16)
    wx_bd = _block_diag(sm["rg_x_w"].reshape(D // LRU_BLOCK, LRU_BLOCK, LRU_BLOCK)).astype(BF16)
    cw = sm["conv_w"].reshape(CONV_WIDTH, D)
    lru_args = (cw, vec("conv_b"), wa_bd, wx_bd, vec("rg_a_b"), vec("rg_x_b"), vec("lru_lambda"))

    h1, a1, b1, f1 = _ffn_fwd(x, vec("ffn1_pre_g"), vec("ffn1_post_g"), wg, ("ffn1_w1", "ffn1_w3", "ffn1_w2"))
    q, kd, vd, xr, xg, gs = _mix_proj_fwd(h1, vec("mix_pre_g"), vec("b_gate"), wg)
    ya_in, hseq = _lru_fwd(xr, xg, *lru_args)
    o, lse = _attn_fwd(q, kd, vd, bias, sinks)
    h2, ya, yb, z = _mix_out_fwd(ya_in, o, gs, h1, vec("mix_post_g"), wg)
    _, a2, b2, f2, dy, loss = _ffn_fwd(h2, vec("ffn2_pre_g"), vec("ffn2_post_g"), wg,
                                       ("ffn2_w1", "ffn2_w3", "ffn2_w2"), target=target)

    gw, gs_ = {}, {}
    da, db, gs_["ffn2_post_g"], gw["ffn2_w2"] = _ffn_bwd_a(dy, f2, a2, b2, vec("ffn2_post_g"), wg, "ffn2_w2")
    dh2, gs_["ffn2_pre_g"], gw["ffn2_w1"], gw["ffn2_w3"] = _ffn_bwd_b(da, db, h2, dy, vec("ffn2_pre_g"), wg,
                                                                      "ffn2_w1", "ffn2_w3")
    dya_in, do, dgpre, gs_["mix_post_g"], gw["w_lru_out"], gw["w_attn_out"], gw["w_o"] = _mix_out_bwd(
        dh2, z, ya, yb, gs, ya_in, o, vec("mix_post_g"), wg)
    dq, dk, dv, ds_acc, dsink = _attn_bwd(q, kd, vd, o, do, lse, bias, sinks)
    dxr, dxg, dvec, dwa, dwx = _lru_bwd(dya_in, xr, xg, hseq, *lru_args)
    dh1, gs_["mix_pre_g"], gs_["b_gate"], gw["w_in"], gw["w_gate"] = _mix_proj_bwd(
        dq, dk, dv, dxr, dxg, dgpre, h1, dh2, vec("mix_pre_g"), wg)
    da, db, gs_["ffn1_post_g"], gw["ffn1_w2"] = _ffn_bwd_a(dh1, f1, a1, b1, vec("ffn1_post_g"), wg, "ffn1_w2")
    dx, gs_["ffn1_pre_g"], gw["ffn1_w1"], gw["ffn1_w3"] = _ffn_bwd_b(da, db, x, dh1, vec("ffn1_pre_g"), wg,
                                                                     "ffn1_w1", "ffn1_w3")
    gs_["conv_w"] = dvec[0:CONV_WIDTH]
    gs_["conv_b"], gs_["rg_a_b"], gs_["rg_x_b"], gs_["lru_lambda"] = dvec[4], dvec[5], dvec[6], dvec[7]
    gs_["rg_a_w"] = _block_diag_grad(dwa)
    gs_["rg_x_w"] = _block_diag_grad(dwx)
    gs_["attn_sinks"] = dsink[0, :N_HEADS]
    gs_["rel_bias"] = _bias_grad(ds_acc, bmap)[:, :N_HEADS]
    return loss[0, 0], dx, gw, gs_


def kernel(x, ffn1_pre_g, ffn1_w1, ffn1_w3, ffn1_w2, ffn1_post_g, mix_pre_g, w_in, conv_w, conv_b, rg_a_w, rg_a_b, rg_x_w, rg_x_b, lru_lambda, w_lru_out, attn_sinks, rel_bias, w_attn_out, w_gate, b_gate, w_o, mix_post_g, ffn2_pre_g, ffn2_w1, ffn2_w3, ffn2_w2, ffn2_post_g, loss_target, m_ffn1_pre_g, m_ffn1_w1, m_ffn1_w3, m_ffn1_w2, m_ffn1_post_g, m_mix_pre_g, m_w_in, m_conv_w, m_conv_b, m_rg_a_w, m_rg_a_b, m_rg_x_w, m_rg_x_b, m_lru_lambda, m_w_lru_out, m_attn_sinks, m_rel_bias, m_w_attn_out, m_w_gate, m_b_gate, m_w_o, m_mix_post_g, m_ffn2_pre_g, m_ffn2_w1, m_ffn2_w3, m_ffn2_w2, m_ffn2_post_g, v_ffn1_pre_g, v_ffn1_w1, v_ffn1_w3, v_ffn1_w2, v_ffn1_post_g, v_mix_pre_g, v_w_in, v_conv_w, v_conv_b, v_rg_a_w, v_rg_a_b, v_rg_x_w, v_rg_x_b, v_lru_lambda, v_w_lru_out, v_attn_sinks, v_rel_bias, v_w_attn_out, v_w_gate, v_b_gate, v_w_o, v_mix_post_g, v_ffn2_pre_g, v_ffn2_w1, v_ffn2_w3, v_ffn2_w2, v_ffn2_post_g):
    names = ["ffn1_pre_g", "ffn1_w1", "ffn1_w3", "ffn1_w2", "ffn1_post_g", "mix_pre_g", "w_in", "conv_w", "conv_b",
             "rg_a_w", "rg_a_b", "rg_x_w", "rg_x_b", "lru_lambda", "w_lru_out", "attn_sinks", "rel_bias", "w_attn_out",
             "w_gate", "b_gate", "w_o", "mix_post_g", "ffn2_pre_g", "ffn2_w1", "ffn2_w3", "ffn2_w2", "ffn2_post_g"]
    ws = dict(zip(names, (ffn1_pre_g, ffn1_w1, ffn1_w3, ffn1_w2, ffn1_post_g, mix_pre_g, w_in, conv_w, conv_b, rg_a_w,
                          rg_a_b, rg_x_w, rg_x_b, lru_lambda, w_lru_out, attn_sinks, rel_bias, w_attn_out, w_gate,
                          b_gate, w_o, mix_post_g, ffn2_pre_g, ffn2_w1, ffn2_w3, ffn2_w2, ffn2_post_g)))
    ms = dict(zip(names, (m_ffn1_pre_g, m_ffn1_w1, m_ffn1_w3, m_ffn1_w2, m_ffn1_post_g, m_mix_pre_g, m_w_in, m_conv_w,
                          m_conv_b, m_rg_a_w, m_rg_a_b, m_rg_x_w, m_rg_x_b, m_lru_lambda, m_w_lru_out, m_attn_sinks,
                          m_rel_bias, m_w_attn_out, m_w_gate, m_b_gate, m_w_o, m_mix_post_g, m_ffn2_pre_g, m_ffn2_w1,
                          m_ffn2_w3, m_ffn2_w2, m_ffn2_post_g)))
    vs = dict(zip(names, (v_ffn1_pre_g, v_ffn1_w1, v_ffn1_w3, v_ffn1_w2, v_ffn1_post_g, v_mix_pre_g, v_w_in, v_conv_w,
                          v_conv_b, v_rg_a_w, v_rg_a_b, v_rg_x_w, v_rg_x_b, v_lru_lambda, v_w_lru_out, v_attn_sinks,
                          v_rel_bias, v_w_attn_out, v_w_gate, v_b_gate, v_w_o, v_mix_post_g, v_ffn2_pre_g, v_ffn2_w1,
                          v_ffn2_w3, v_ffn2_w2, v_ffn2_post_g)))
    me = 4 * lax.axis_index("x") + 2 * lax.axis_index("y") + lax.axis_index("c")

    def shard2d(name, a):
        a = a.reshape(a.shape[-2], a.shape[-1])
        return a.T if name in COL_SHARDED else a

    conv_row = lax.bitcast_convert_type(conv_w.reshape(CONV_WIDTH, LANES), BF16).reshape(1, D)
    pack = jnp.concatenate([shard2d(n, ws[n]).astype(BF16) for n, _ in PACK]
                           + [conv_row, jnp.zeros((PACK_ROWS - CONV_OFF - 1, D), BF16)], axis=0)
    wg = _allgather(pack, "allgather_weights")
    conv_rows = wg.reshape(NDEV, PACK_ROWS, D)[:, CONV_OFF].reshape(NDEV, CONV_WIDTH, LANES, 2)
    conv_full = jnp.transpose(lax.bitcast_convert_type(conv_rows, F32), (1, 0, 2)).reshape(CONV_WIDTH, D)
    sm = {n: ws[n] for n in SMALL}
    sm["conv_w"] = conv_full

    loss_part, grad_x, gw, gsm = _local_step(x[0], loss_target[0], wg, sm)
    loss = lax.psum(loss_part, ("x", "y", "c"))

    parts = dict(zip([n for n, _ in PACK], _reduce_scatter_send([gw[n] for n, _ in PACK])))
    grads = {}
    by_rows = {}
    for n, r in PACK:
        by_rows.setdefault(r, []).append(n)
    for r, group in by_rows.items():
        for n, g in zip(group, _sum_parts([parts[n] for n in group])):
            grads[n] = g
    small_pack, offs = _pack_small([gsm[n] for n in SMALL])
    rows_small = small_pack.shape[0]
    small_sum = _sum_small(_allgather(small_pack, "allgather_small"), rows_small)
    small_shapes = [(CONV_WIDTH, D) if n == "conv_w" else ws[n].shape for n in SMALL]
    for n, g in zip(SMALL, _unpack_small(small_sum, offs, small_shapes)):
        grads[n] = g
    grads["conv_w"] = lax.dynamic_slice(grads["conv_w"], (0, me * LANES), (CONV_WIDTH, LANES)).reshape(conv_w.shape)

    delta, new_m, new_v = {}, {}, {}
    t_form = [n for n in COL_SHARDED if ws[n].shape[-1] % LANES]
    view = {n: (lambda a: a.reshape(a.shape[-2], a.shape[-1]).T) if n in t_form
            else (lambda a: a.reshape(a.shape[-2], a.shape[-1])) for n, _ in PACK}
    unview = {n: (lambda a: a.T) if n in t_form else (lambda a: a) for n, _ in PACK}
    for n in COL_SHARDED:
        if n not in t_form:
            grads[n] = grads[n].T
    by_shape = {}
    for n, _ in PACK:
        by_shape.setdefault(view[n](ws[n]).shape, []).append(n)
    for shp, group in by_shape.items():
        res = _adamw([(grads[n], view[n](ws[n]), view[n](ms[n]), view[n](vs[n])) for n in group])
        for n, (d_, m_, v_) in zip(group, res):
            delta[n], new_m[n], new_v[n] = unview[n](d_), unview[n](m_), unview[n](v_)
            grads[n] = unview[n](grads[n])
    flat2d = lambda a: a.reshape(-1, a.shape[-1])
    res = _adamw_small([(flat2d(grads[n].reshape(ws[n].shape)), flat2d(ws[n]), flat2d(ms[n]), flat2d(vs[n]))
                        for n in SMALL])
    for n, (d_, m_, v_) in zip(SMALL, res):
        delta[n], new_m[n], new_v[n] = d_, m_, v_

    outs = [loss, grad_x.reshape(x.shape)]
    for src in (grads, delta, new_m, new_v):
        outs += [src[n].reshape(ws[n].shape) for n in names]
    return tuple(outs)
```

```python
import functools
import math
import operator

import numpy as np
import jax
import jax.numpy as jnp
from jax import lax
from jax.experimental import pallas as pl
from jax.experimental.pallas import tpu as pltpu

F32, BF16 = jnp.float32, jnp.bfloat16

NDEV = 8
D = 1024
FF = 2816
N_HEADS, N_KV, HEAD_DIM = 16, 4, 64
CHUNK, WINDOW = 64, 128
N_BUCKETS, MAX_DISTANCE = 32, 128
LRU_BLOCK = 64
CONV_WIDTH = 4
LRU_C = 8.0
RMS_EPS = 1e-6
NEG_INF = -1e30
LANES = 128
QT = 256
KW = QT + WINDOW
LRU_ROWS = 512
IN_W = D + 2 * N_KV * HEAD_DIM + 2 * D
INP_W = D + 2 * N_KV * LANES + 2 * D
VMEM_BIG = 58 * 2 ** 20

ADAM_LR, ADAM_B1, ADAM_B2, ADAM_EPS, ADAM_WD, ADAM_STEP = 0.001, 0.9, 0.999, 1e-08, 0.01, 10

GROUPS = (("ffn1", (("ffn1_w1", FF // NDEV), ("ffn1_w3", FF // NDEV), ("ffn1_w2", FF // NDEV), ("conv", 16))),
          ("mixin", (("w_in", IN_W // NDEV), ("w_gate", 2 * D // NDEV))),
          ("mixout", (("w_lru_out", D // NDEV), ("w_attn_out", D // NDEV), ("w_o", D // NDEV))),
          ("ffn2a", (("ffn2_w1", FF // NDEV), ("ffn2_w3", FF // NDEV))),
          ("ffn2b", (("ffn2_w2", FF // NDEV),)))
LOC, GROUP_ROWS = {}, {}
for _g, _members in GROUPS:
    _o = 0
    for _n, _r in _members:
        LOC[_n] = (_g, _o, _r)
        _o += _r
    GROUP_ROWS[_g] = _o
BIG = tuple(n for _, members in GROUPS for n, _ in members if n != "conv")
COL_SHARDED = ("ffn1_w1", "ffn1_w3", "w_in", "w_gate", "ffn2_w1", "ffn2_w3")

SMALL = ("ffn1_pre_g", "ffn1_post_g", "mix_pre_g", "conv_w", "conv_b", "rg_a_w", "rg_a_b", "rg_x_w", "rg_x_b",
         "lru_lambda", "attn_sinks", "rel_bias", "b_gate", "mix_post_g", "ffn2_pre_g", "ffn2_post_g")

MESH = pl.DeviceIdType.MESH
ANY = pl.BlockSpec(memory_space=pl.ANY)
DMA = pltpu.SemaphoreType.DMA


def _nn(a, b):
    return lax.dot_general(a, b, (((1,), (0,)), ((), ())), preferred_element_type=F32)


def _nt(a, b):
    return lax.dot_general(a, b, (((1,), (1,)), ((), ())), preferred_element_type=F32)


def _tn(a, b):
    return lax.dot_general(a, b, (((0,), (0,)), ((), ())), preferred_element_type=F32)


def _rms_fwd(x, g):
    r = lax.rsqrt(jnp.mean(x * x, axis=-1, keepdims=True) + RMS_EPS)
    xh = x * r
    return xh * g, xh, r


def _rms_bwd(dn, xh, r, g):
    dxh = dn * g
    dx = r * (dxh - xh * jnp.mean(dxh * xh, axis=-1, keepdims=True))
    return dx, jnp.sum(dn * xh, axis=0, keepdims=True)


def _row_spec(tm, c):
    return pl.BlockSpec((tm, c), lambda i: (i, 0))


def _const_spec(shape):
    nd = len(shape)
    return pl.BlockSpec(shape, lambda i: (0,) * nd)


class _AllGather:
    def __init__(self, shard):
        self.m, n = shard.shape
        self.ins = [shard]
        self.out_shape = [jax.ShapeDtypeStruct((NDEV * self.m, n), shard.dtype)]
        self.scratch = [DMA((7,)), DMA((7,)), DMA]

    def _copies(self, ins, outs, scr):
        x_ref, out_ref = ins[0], outs[0]
        send_sems, recv_sems, local_sem = scr
        x, y, c = lax.axis_index("x"), lax.axis_index("y"), lax.axis_index("c")
        me, sibling = (x, y, c), (x, y, 1 - c)
        chips = [(1 - x, y), (x, 1 - y), (1 - x, 1 - y)]
        m = self.m

        def rows(px, py, pc):
            return out_ref.at[pl.ds((4 * px + 2 * py + pc) * m, m), :]

        def copy(k, block, to, src=None):
            return pltpu.make_async_remote_copy(
                src_ref=rows(*block) if src is None else src, dst_ref=rows(*block),
                send_sem=send_sems.at[k], recv_sem=recv_sems.at[k], device_id=to, device_id_type=MESH)

        mine = pltpu.make_async_copy(x_ref, rows(*me), local_sem)
        first = [copy(0, me, sibling, src=x_ref)] + [copy(1 + j, me, (*chip, c), src=x_ref)
                                                     for j, chip in enumerate(chips)]
        passed = [copy(4 + j, (*chip, c), sibling) for j, chip in enumerate(chips)]
        landed = [copy(1 + j, (*chip, c), me) for j, chip in enumerate(chips)]
        from_sibling = [copy(0, sibling, me)] + [copy(4 + j, (*chip, 1 - c), me) for j, chip in enumerate(chips)]
        return mine, first, passed, landed, from_sibling

    def start(self, ins, outs, scr):
        mine, first, _, _, _ = self._copies(ins, outs, scr)
        mine.start()
        for cp in first:
            cp.start()

    def finish(self, ins, outs, scr):
        mine, first, passed, landed, from_sibling = self._copies(ins, outs, scr)
        for cp_in, cp_on in zip(landed, passed):
            cp_in.wait_recv()
            cp_on.start()
        for cp in from_sibling:
            cp.wait_recv()
        for cp in first + passed:
            cp.wait_send()
        mine.wait()


class _ReduceScatterSend:
    def __init__(self, grads):
        self.nw = len(grads)
        self.rows = [g.shape[0] // NDEV for g in grads]
        self.ins = list(grads)
        self.out_shape = [jax.ShapeDtypeStruct((NDEV, r, g.shape[1]), g.dtype) for g, r in zip(grads, self.rows)]
        self.scratch = [DMA((self.nw, NDEV - 1)), DMA((self.nw, NDEV - 1)), DMA((self.nw,))]

    def _copies(self, g_refs, r_refs, scr):
        send_sems, recv_sems, local_sems = scr
        x, y, c = lax.axis_index("x"), lax.axis_index("y"), lax.axis_index("c")
        me = 4 * x + 2 * y + c
        rows = self.rows
        locals_, sends, recvs = [], [], []
        for w in range(self.nw):
            locals_.append(pltpu.make_async_copy(g_refs[w].at[pl.ds(me * rows[w], rows[w])], r_refs[w].at[me],
                                                 local_sems.at[w]))
        for k in range(1, NDEV):
            px, py, pc = x ^ (k >> 2), y ^ ((k >> 1) & 1), c ^ (k & 1)
            peer = 4 * px + 2 * py + pc
            for w in range(self.nw):
                sems = dict(send_sem=send_sems.at[w, k - 1], recv_sem=recv_sems.at[w, k - 1],
                            device_id=(px, py, pc), device_id_type=MESH)
                sends.append(pltpu.make_async_remote_copy(
                    src_ref=g_refs[w].at[pl.ds(peer * rows[w], rows[w])], dst_ref=r_refs[w].at[me], **sems))
                recvs.append(pltpu.make_async_remote_copy(
                    src_ref=g_refs[w].at[pl.ds(0, rows[w])], dst_ref=r_refs[w].at[peer], **sems))
        return locals_, sends, recvs

    def start(self, ins, outs, scr):
        locals_, sends, _ = self._copies(ins, outs, scr)
        for cp in locals_ + sends:
            cp.start()

    def finish(self, ins, outs, scr):
        locals_, sends, recvs = self._copies(ins, outs, scr)
        for cp in recvs:
            cp.wait_recv()
        for cp in sends:
            cp.wait_send()
        for cp in locals_:
            cp.wait()


def _call(body, *, name, grid, ins, in_specs, out_shape, out_specs, scratch=(), vmem=None, carries=()):
    n_in, n_out, n_scr = len(ins), len(out_shape), len(scratch)
    ng = len(grid)

    def split(refs):
        pos = [0]

        def take(k):
            part = refs[pos[0]:pos[0] + k]
            pos[0] += k
            return part

        i_refs = take(n_in)
        c_in = [take(len(c.ins)) for c in carries]
        o_refs = take(n_out)
        c_out = [take(len(c.out_shape)) for c in carries]
        s_refs = take(n_scr)
        c_scr = [take(len(c.scratch)) for c in carries]
        return i_refs, o_refs, s_refs, list(zip(carries, c_in, c_out, c_scr))

    def full(*refs):
        i_refs, o_refs, s_refs, cparts = split(refs)
        if ng == 0:
            for c, a, b, s in cparts:
                c.start(a, b, s)
            for c, a, b, s in cparts:
                c.finish(a, b, s)
            return
        ids = [pl.program_id(a) for a in range(ng)]
        if cparts:
            @pl.when(functools.reduce(operator.and_, [i == 0 for i in ids]))
            def _():
                for c, a, b, s in cparts:
                    c.start(a, b, s)

        body(*i_refs, *o_refs, *s_refs)
        if cparts:
            @pl.when(functools.reduce(operator.and_, [i == g - 1 for i, g in zip(ids, grid)]))
            def _():
                for c, a, b, s in cparts:
                    c.finish(a, b, s)

    all_ins = list(ins) + [a for c in carries for a in c.ins]
    all_in_specs = list(in_specs) + [ANY for c in carries for _ in c.ins]
    all_out_shape = list(out_shape) + [s for c in carries for s in c.out_shape]
    all_out_specs = list(out_specs) + [ANY for c in carries for _ in c.out_shape]
    all_scratch = list(scratch) + [s for c in carries for s in c.scratch]
    kwargs = dict(grid=grid) if ng else {}
    outs = pl.pallas_call(
        full, name=name, in_specs=all_in_specs, out_specs=all_out_specs, out_shape=all_out_shape,
        scratch_shapes=all_scratch,
        compiler_params=pltpu.CompilerParams(dimension_semantics=("arbitrary",) * ng if ng else None,
                                             vmem_limit_bytes=vmem),
        **kwargs)(*all_ins)
    outs = list(outs)
    res, pos = outs[:n_out], n_out
    carried = []
    for c in carries:
        carried.append(outs[pos:pos + len(c.out_shape)])
        pos += len(c.out_shape)
    return res, carried


def _groups_of(names):
    out = []
    for n in names:
        if LOC[n][0] not in out:
            out.append(LOC[n][0])
    return out


def _weight_pieces(name):
    g, off, rows = LOC[name]
    return [(d * GROUP_ROWS[g] + off, d * rows, rows) for d in range(NDEV)]


def _win_pieces():
    kv = N_KV * HEAD_DIM
    pieces = [(0, 0, D)]
    for g0, d0 in ((D, D), (D + kv, D + N_KV * LANES)):
        for g in range(N_KV):
            for half in range(2):
                pieces.append((g0 + g * HEAD_DIM, d0 + g * LANES + half * HEAD_DIM, HEAD_DIM))
    pieces.append((D + 2 * kv, D + 2 * N_KV * LANES, D))
    pieces.append((D + 2 * kv + D, D + 2 * N_KV * LANES + D, D))
    grp, off, rows = LOC["w_in"]
    out = []
    for g0, d0, n in pieces:
        while n > 0:
            dev, loc = divmod(g0, rows)
            m = min(n, rows - loc)
            out.append((dev * GROUP_ROWS[grp] + off + loc, d0, m))
            g0, d0, n = g0 + m, d0 + m, n - m
    return out


def _start_loads(src_ref, dst_ref, pieces, sems, base):
    cps = []
    for j, (s, d, n) in enumerate(pieces):
        cp = pltpu.make_async_copy(src_ref.at[pl.ds(s, n)], dst_ref.at[pl.ds(d, n)], sems.at[base + j])
        cp.start()
        cps.append(cp)
    return cps


def _load_weights(wrefs, targets, sems):
    cps, base = [], 0
    for name, dst in targets:
        pieces = _win_pieces() if name == "w_in" else _weight_pieces(name)
        cps += _start_loads(wrefs[LOC[name][0]], dst, pieces, sems, base)
        base += len(pieces)
    for cp in cps:
        cp.wait()


def _n_pieces(names):
    return sum(len(_win_pieces()) if n == "w_in" else NDEV for n in names)


def _flush(acc_ref, out_ref, stage_ref, pieces):
    srows = stage_ref.shape[0]
    for a0, o0, n in pieces:
        done = 0
        while done < n:
            m = min(srows, n - done)
            stage_ref[pl.ds(0, m), :] = acc_ref[pl.ds(a0 + done, m), :].astype(BF16)
            pltpu.sync_copy(stage_ref.at[pl.ds(0, m)], out_ref.at[pl.ds(o0 + done, m)])
            done += m


def _silu_parts(a):
    sig = jax.nn.sigmoid(a)
    return sig, a * sig


FC = FF // 2


def _ffn_fwd(h, gpre, gpost, wg, names, target=None, tm=256, carries=()):
    t_tok = h.shape[0]
    nt = t_tok // tm
    with_loss = target is not None
    groups = _groups_of(names)

    def body(*refs):
        refs = list(refs)
        h_ref, gpre_ref, gpost_ref = refs[:3]
        del refs[:3]
        tgt_ref = refs.pop(0) if with_loss else None
        wrefs = dict(zip(groups, refs[:len(groups)]))
        del refs[:len(groups)]
        hout_ref, a_ref, b_ref, f_ref = refs[:4]
        del refs[:4]
        if with_loss:
            dy_ref, loss_ref = refs[:2]
            del refs[:2]
        w1_s, w3_s, w2_s, sems = refs
        i = pl.program_id(0)

        @pl.when(i == 0)
        def _():
            _load_weights(wrefs, list(zip(names, (w1_s, w3_s, w2_s))), sems)
            if with_loss:
                loss_ref[...] = jnp.zeros_like(loss_ref)

        x = h_ref[...]
        n, _, _ = _rms_fwd(x, gpre_ref[...])
        nb = n.astype(BF16)
        f = jnp.zeros((tm, D), F32)
        for c0 in range(0, FF, FC):
            a = _nt(nb, w1_s[pl.ds(c0, FC), :])
            b = _nt(nb, w3_s[pl.ds(c0, FC), :])
            _, sl = _silu_parts(a)
            a_ref[:, pl.ds(c0, FC)] = a.astype(BF16)
            b_ref[:, pl.ds(c0, FC)] = b.astype(BF16)
            f = f + _nn((sl * b).astype(BF16), w2_s[pl.ds(c0, FC), :])
        f_ref[...] = f
        fn, _, _ = _rms_fwd(f, gpost_ref[...])
        y = x + 0.5 * fn
        hout_ref[...] = y
        if with_loss:
            err = y - tgt_ref[...]
            dy_ref[...] = err * (1.0 / D)
            loss_ref[...] += jnp.sum(jnp.sum(err * err, axis=-1, keepdims=True), axis=0, keepdims=True) * (0.5 / D)

    ins = [h, gpre, gpost] + ([target] if with_loss else []) + [wg[g] for g in groups]
    in_specs = [_row_spec(tm, D), _const_spec((1, D)), _const_spec((1, D))]
    in_specs += ([_row_spec(tm, D)] if with_loss else []) + [ANY] * len(groups)
    out_shape = [jax.ShapeDtypeStruct((t_tok, D), F32), jax.ShapeDtypeStruct((t_tok, FF), BF16),
                 jax.ShapeDtypeStruct((t_tok, FF), BF16), jax.ShapeDtypeStruct((t_tok, D), F32)]
    out_specs = [_row_spec(tm, D), _row_spec(tm, FF), _row_spec(tm, FF), _row_spec(tm, D)]
    if with_loss:
        out_shape += [jax.ShapeDtypeStruct((t_tok, D), F32), jax.ShapeDtypeStruct((1, LANES), F32)]
        out_specs += [_row_spec(tm, D), _const_spec((1, LANES))]
    return _call(body, name="ffn_fwd_" + names[0][:4], grid=(nt,), ins=ins, in_specs=in_specs,
                 out_shape=out_shape, out_specs=out_specs,
                 scratch=[pltpu.VMEM((FF, D), BF16)] * 3 + [DMA((3 * NDEV,))], vmem=VMEM_BIG, carries=carries)


def _ffn_bwd_a(dh, f, a, b, gpost, wg, name_w2, tm=256, carries=()):
    t_tok = dh.shape[0]
    nt = t_tok // tm
    groups = _groups_of([name_w2])

    def body(dh_ref, f_ref, a_ref, b_ref, gpost_ref, wg_ref, da_ref, db_ref, dgp_ref, dw2_ref, w2_s, acc_s, stage_s, sems):
        i = pl.program_id(0)

        @pl.when(i == 0)
        def _():
            _load_weights({groups[0]: wg_ref}, [(name_w2, w2_s)], sems)
            acc_s[...] = jnp.zeros_like(acc_s)
            dgp_ref[...] = jnp.zeros_like(dgp_ref)

        fv = f_ref[...]
        _, fh, r = _rms_fwd(fv, gpost_ref[...])
        df, dg = _rms_bwd(0.5 * dh_ref[...], fh, r, gpost_ref[...])
        dgp_ref[...] += dg
        dfb = df.astype(BF16)
        for c0 in range(0, FF, FC):
            ds = _nt(dfb, w2_s[pl.ds(c0, FC), :])
            av = a_ref[:, pl.ds(c0, FC)].astype(F32)
            bv = b_ref[:, pl.ds(c0, FC)].astype(F32)
            sig, sl = _silu_parts(av)
            da_ref[:, pl.ds(c0, FC)] = (ds * bv * (sig * (1.0 + av * (1.0 - sig)))).astype(BF16)
            db_ref[:, pl.ds(c0, FC)] = (ds * sl).astype(BF16)
            acc_s[pl.ds(c0, FC), :] += _tn((sl * bv).astype(BF16), dfb)

        @pl.when(i == nt - 1)
        def _():
            _flush(acc_s, dw2_ref, stage_s, [(0, 0, FF)])

    return _call(
        body, name="ffn_bwd_a_" + name_w2[:4], grid=(nt,), ins=[dh, f, a, b, gpost, wg[groups[0]]],
        in_specs=[_row_spec(tm, D), _row_spec(tm, D), _row_spec(tm, FF), _row_spec(tm, FF), _const_spec((1, D)), ANY],
        out_specs=[_row_spec(tm, FF), _row_spec(tm, FF), _const_spec((1, D)), ANY],
        out_shape=[jax.ShapeDtypeStruct((t_tok, FF), BF16), jax.ShapeDtypeStruct((t_tok, FF), BF16),
                   jax.ShapeDtypeStruct((1, D), F32), jax.ShapeDtypeStruct((FF, D), BF16)],
        scratch=[pltpu.VMEM((FF, D), BF16), pltpu.VMEM((FF, D), F32), pltpu.VMEM((256, D), BF16), DMA((NDEV,))],
        vmem=VMEM_BIG, carries=carries)


def _ffn_bwd_dw(da, db, h, gpre, name, tm=512, carries=()):
    t_tok = h.shape[0]
    nt = t_tok // tm

    def body(da_ref, db_ref, h_ref, gpre_ref, dw1_ref, dw3_ref, acc1_s, acc3_s, stage_s):
        i = pl.program_id(0)

        @pl.when(i == 0)
        def _():
            acc1_s[...] = jnp.zeros_like(acc1_s)
            acc3_s[...] = jnp.zeros_like(acc3_s)

        n, _, _ = _rms_fwd(h_ref[...], gpre_ref[...])
        nb = n.astype(BF16)
        for c0 in range(0, FF, FC):
            acc1_s[pl.ds(c0, FC), :] += _tn(da_ref[:, pl.ds(c0, FC)], nb)
            acc3_s[pl.ds(c0, FC), :] += _tn(db_ref[:, pl.ds(c0, FC)], nb)

        @pl.when(i == nt - 1)
        def _():
            _flush(acc1_s, dw1_ref, stage_s, [(0, 0, FF)])
            _flush(acc3_s, dw3_ref, stage_s, [(0, 0, FF)])

    return _call(
        body, name="ffn_bwd_dw_" + name, grid=(nt,), ins=[da, db, h, gpre],
        in_specs=[_row_spec(tm, FF), _row_spec(tm, FF), _row_spec(tm, D), _const_spec((1, D))],
        out_specs=[ANY, ANY],
        out_shape=[jax.ShapeDtypeStruct((FF, D), BF16), jax.ShapeDtypeStruct((FF, D), BF16)],
        scratch=[pltpu.VMEM((FF, D), F32), pltpu.VMEM((FF, D), F32), pltpu.VMEM((256, D), BF16)],
        vmem=VMEM_BIG, carries=carries)


def _ffn_bwd_dx(da, db, h, dh, gpre, wg, name_w1, name_w3, tm=512, carries=()):
    t_tok = dh.shape[0]
    nt = t_tok // tm
    groups = _groups_of([name_w1, name_w3])

    def body(*refs):
        da_ref, db_ref, h_ref, dh_ref, gpre_ref = refs[:5]
        wrefs = dict(zip(groups, refs[5:5 + len(groups)]))
        dhin_ref, dgp_ref, w1_s, w3_s, sems = refs[5 + len(groups):]
        i = pl.program_id(0)

        @pl.when(i == 0)
        def _():
            _load_weights(wrefs, [(name_w1, w1_s), (name_w3, w3_s)], sems)
            dgp_ref[...] = jnp.zeros_like(dgp_ref)

        g = gpre_ref[...]
        _, xh, r = _rms_fwd(h_ref[...], g)
        dn = jnp.zeros((tm, D), F32)
        for c0 in range(0, FF, FC):
            dn = dn + _nn(da_ref[:, pl.ds(c0, FC)], w1_s[pl.ds(c0, FC), :])
            dn = dn + _nn(db_ref[:, pl.ds(c0, FC)], w3_s[pl.ds(c0, FC), :])
        dx, dg = _rms_bwd(dn, xh, r, g)
        dgp_ref[...] += dg
        dhin_ref[...] = dh_ref[...] + dx

    return _call(
        body, name="ffn_bwd_dx_" + name_w1[:4], grid=(nt,), ins=[da, db, h, dh, gpre] + [wg[g] for g in groups],
        in_specs=[_row_spec(tm, FF), _row_spec(tm, FF), _row_spec(tm, D), _row_spec(tm, D), _const_spec((1, D))]
        + [ANY] * len(groups),
        out_specs=[_row_spec(tm, D), _const_spec((1, D))],
        out_shape=[jax.ShapeDtypeStruct((t_tok, D), F32), jax.ShapeDtypeStruct((1, D), F32)],
        scratch=[pltpu.VMEM((FF, D), BF16), pltpu.VMEM((FF, D), BF16), DMA((2 * NDEV,))],
        vmem=VMEM_BIG, carries=carries)


Q0, K0, V0, XR0, XG0 = 0, D, D + N_KV * LANES, D + 2 * N_KV * LANES, 2 * D + 2 * N_KV * LANES


def _mix_proj_fwd(h, g, bgate, wg, tm=256, carries=()):
    t_tok = h.shape[0]
    nt = t_tok // tm
    names = ("w_in", "w_gate")
    groups = _groups_of(names)

    def body(h_ref, g_ref, bg_ref, wg_ref, q_ref, k_ref, v_ref, xr_ref, xg_ref, gs_ref, win_s, wgt_s, sems):
        i = pl.program_id(0)

        @pl.when(i == 0)
        def _():
            _load_weights({groups[0]: wg_ref}, [("w_in", win_s), ("w_gate", wgt_s)], sems)

        n, _, _ = _rms_fwd(h_ref[...], g_ref[...])
        nb = n.astype(BF16)
        q_ref[...] = _nt(nb, win_s[pl.ds(Q0, D), :]).astype(BF16)
        k_ref[...] = _nt(nb, win_s[pl.ds(K0, N_KV * LANES), :]).astype(BF16)
        v_ref[...] = _nt(nb, win_s[pl.ds(V0, N_KV * LANES), :]).astype(BF16)
        xr_ref[...] = _nt(nb, win_s[pl.ds(XR0, D), :])
        xg_ref[...] = _nt(nb, win_s[pl.ds(XG0, D), :])
        gs_ref[...] = jax.nn.sigmoid(_nt(nb, wgt_s[...]) + bg_ref[...])

    kvw = N_KV * LANES
    return _call(
        body, name="mix_proj_fwd", grid=(nt,), ins=[h, g, bgate, wg[groups[0]]],
        in_specs=[_row_spec(tm, D), _const_spec((1, D)), _const_spec((1, 2 * D)), ANY],
        out_specs=[_row_spec(tm, D), _row_spec(tm, kvw), _row_spec(tm, kvw), _row_spec(tm, D), _row_spec(tm, D),
                   _row_spec(tm, 2 * D)],
        out_shape=[jax.ShapeDtypeStruct((t_tok, D), BF16), jax.ShapeDtypeStruct((t_tok, kvw), BF16),
                   jax.ShapeDtypeStruct((t_tok, kvw), BF16), jax.ShapeDtypeStruct((t_tok, D), F32),
                   jax.ShapeDtypeStruct((t_tok, D), F32), jax.ShapeDtypeStruct((t_tok, 2 * D), F32)],
        scratch=[pltpu.VMEM((INP_W, D), BF16), pltpu.VMEM((2 * D, D), BF16), DMA((_n_pieces(names),))],
        vmem=VMEM_BIG, carries=carries)


def _mix_proj_bwd(dq, dk, dv, dxr, dxg, dgpre, h, dh, g, wg, tm=256, carries=()):
    t_tok = h.shape[0]
    nt = t_tok // tm
    names = ("w_in", "w_gate")
    groups = _groups_of(names)
    kv = N_KV * HEAD_DIM
    win_flush = [(Q0, 0, D)]
    win_flush += [(K0 + gi * LANES, D + gi * HEAD_DIM, HEAD_DIM) for gi in range(N_KV)]
    win_flush += [(V0 + gi * LANES, D + kv + gi * HEAD_DIM, HEAD_DIM) for gi in range(N_KV)]
    win_flush += [(XR0, D + 2 * kv, D), (XG0, 2 * D + 2 * kv, D)]

    def body(dq_ref, dk_ref, dv_ref, dxr_ref, dxg_ref, dgp_ref, h_ref, dh_ref, g_ref, wg_ref,
             dhin_ref, dg_ref, dbg_ref, dwin_ref, dwgt_ref, win_s, wgt_s, accin_s, accgt_s, stage_s, sems):
        i = pl.program_id(0)

        @pl.when(i == 0)
        def _():
            _load_weights({groups[0]: wg_ref}, [("w_in", win_s), ("w_gate", wgt_s)], sems)
            accin_s[...] = jnp.zeros_like(accin_s)
            accgt_s[...] = jnp.zeros_like(accgt_s)
            dg_ref[...] = jnp.zeros_like(dg_ref)
            dbg_ref[...] = jnp.zeros_like(dbg_ref)

        gv = g_ref[...]
        n, xh, r = _rms_fwd(h_ref[...], gv)
        nb = n.astype(BF16)
        dgp = dgp_ref[...]
        dbg_ref[...] += jnp.sum(dgp, axis=0, keepdims=True)
        dgpb = dgp.astype(BF16)
        du = _nn(dgpb, wgt_s[...])
        accgt_s[...] += _tn(dgpb, nb)
        for ref, r0, w in ((dq_ref, Q0, D), (dk_ref, K0, N_KV * LANES), (dv_ref, V0, N_KV * LANES),
                           (dxr_ref, XR0, D), (dxg_ref, XG0, D)):
            piece = ref[...].astype(BF16)
            du = du + _nn(piece, win_s[pl.ds(r0, w), :])
            accin_s[pl.ds(r0, w), :] += _tn(piece, nb)
        dx, dg = _rms_bwd(du, xh, r, gv)
        dg_ref[...] += dg
        dhin_ref[...] = dh_ref[...] + dx

        @pl.when(i == nt - 1)
        def _():
            _flush(accin_s, dwin_ref, stage_s, win_flush)
            _flush(accgt_s, dwgt_ref, stage_s, [(0, 0, 2 * D)])

    kvw = N_KV * LANES
    return _call(
        body, name="mix_proj_bwd", grid=(nt,), ins=[dq, dk, dv, dxr, dxg, dgpre, h, dh, g, wg[groups[0]]],
        in_specs=[_row_spec(tm, D), _row_spec(tm, kvw), _row_spec(tm, kvw), _row_spec(tm, D), _row_spec(tm, D),
                  _row_spec(tm, 2 * D), _row_spec(tm, D), _row_spec(tm, D), _const_spec((1, D)), ANY],
        out_specs=[_row_spec(tm, D), _const_spec((1, D)), _const_spec((1, 2 * D)), ANY, ANY],
        out_shape=[jax.ShapeDtypeStruct((t_tok, D), F32), jax.ShapeDtypeStruct((1, D), F32),
                   jax.ShapeDtypeStruct((1, 2 * D), F32), jax.ShapeDtypeStruct((IN_W, D), BF16),
                   jax.ShapeDtypeStruct((2 * D, D), BF16)],
        scratch=[pltpu.VMEM((INP_W, D), BF16), pltpu.VMEM((2 * D, D), BF16), pltpu.VMEM((INP_W, D), F32),
                 pltpu.VMEM((2 * D, D), F32), pltpu.VMEM((256, D), BF16), DMA((_n_pieces(names),))],
        vmem=VMEM_BIG, carries=carries)


def _shift_down(x, d, fill):
    row = lax.broadcasted_iota(jnp.int32, x.shape, 0)
    return jnp.where(row >= d, pltpu.roll(x, d, 0), fill)


def _shift_up(x, d, fill):
    rows = x.shape[0]
    row = lax.broadcasted_iota(jnp.int32, x.shape, 0)
    return jnp.where(row < rows - d, pltpu.roll(x, rows - d, 0), fill)


def _expm1(x):
    small = x * (1.0 + x * (0.5 + x * (1.0 / 6.0 + x * (1.0 / 24.0))))
    return jnp.where(jnp.abs(x) < 0.05, small, jnp.exp(x) - 1.0)


def _softplus(x):
    return jnp.maximum(x, 0.0) + jnp.log(1.0 + jnp.exp(-jnp.abs(x)))


_GELU_C = math.sqrt(2.0 / math.pi)


def _gelu_parts(x):
    th = jnp.tanh(_GELU_C * (x + 0.044715 * x * x * x))
    val = 0.5 * x * (1.0 + th)
    grad = 0.5 * (1.0 + th) + 0.5 * x * (1.0 - th * th) * _GELU_C * (1.0 + 3.0 * 0.044715 * x * x)
    return val, grad


def _lru_pre(x, halo, cw_ref, cb_ref, wa_ref, wx_ref, ba_ref, bx_ref, lam_ref):
    ext = jnp.concatenate([halo, x], axis=0)
    shifted = [x] + [pltpu.roll(ext, k, 0)[8:] for k in (1, 2, 3)]
    xc = cb_ref[...] + cw_ref[pl.ds(CONV_WIDTH - 1, 1), :] * x
    for k in (1, 2, 3):
        xc = xc + cw_ref[pl.ds(CONV_WIDTH - 1 - k, 1), :] * shifted[k]
    xcb = xc.astype(BF16)
    r = jax.nn.sigmoid(_nn(xcb, wa_ref[...]) + ba_ref[...])
    ig = jax.nn.sigmoid(_nn(xcb, wx_ref[...]) + bx_ref[...])
    sp = _softplus(-lam_ref[...])
    log_a = -LRU_C * r * sp
    a = jnp.exp(log_a)
    mult = jnp.sqrt(-_expm1(2.0 * log_a))
    return shifted, xc, xcb, r, ig, sp, a, mult


def _lru_specs(nt, reverse):
    def tt(t):
        return nt - 1 - t if reverse else t
    tile = pl.BlockSpec((LRU_ROWS, LANES), lambda cb, t: (tt(t), cb))
    halo = pl.BlockSpec((8, LANES), lambda cb, t: (jnp.maximum(tt(t) * (LRU_ROWS // 8) - 1, 0), cb))
    vec = pl.BlockSpec((1, LANES), lambda cb, t: (0, cb))
    cw = pl.BlockSpec((CONV_WIDTH, LANES), lambda cb, t: (0, cb))
    mat = pl.BlockSpec((None, LANES, LANES), lambda cb, t: (cb, 0, 0))
    return tile, halo, vec, cw, mat


def _lru_fwd(xr, xg, cw, cb, wa, wx, ba, bx, lam, carries=()):
    t_tok = xr.shape[0]
    nt = t_tok // LRU_ROWS
    rows = LRU_ROWS

    def body(xr_ref, xg_ref, cw_ref, cb_ref, wa_ref, wx_ref, ba_ref, bx_ref, lam_ref, y_ref, h_ref, tail_s, hc_s):
        t = pl.program_id(1)

        @pl.when(t == 0)
        def _():
            tail_s[...] = jnp.zeros_like(tail_s)
            hc_s[...] = jnp.zeros_like(hc_s)

        x = xr_ref[...]
        _, xc, _, _, ig, _, a, mult = _lru_pre(x, tail_s[...], cw_ref, cb_ref, wa_ref, wx_ref, ba_ref, bx_ref, lam_ref)
        tail_s[...] = xr_ref[pl.ds(rows - 8, 8), :]
        acc_a, acc_b = a, mult * (ig * xc)
        d = 1
        while d < rows:
            acc_b = acc_a * _shift_down(acc_b, d, 0.0) + acc_b
            acc_a = acc_a * _shift_down(acc_a, d, 1.0)
            d *= 2
        hv = acc_b + acc_a * hc_s[...]
        h_ref[...] = hv
        hc_s[...] = h_ref[pl.ds(rows - 1, 1), :]
        gl, _ = _gelu_parts(xg_ref[...])
        y_ref[...] = (hv * gl).astype(BF16)

    tile, _, vec, cws, mat = _lru_specs(nt, False)
    return _call(
        body, name="lru_fwd", grid=(D // LANES, nt), ins=[xr, xg, cw, cb, wa, wx, ba, bx, lam],
        in_specs=[tile, tile, cws, vec, mat, mat, vec, vec, vec],
        out_specs=[tile, tile],
        out_shape=[jax.ShapeDtypeStruct((t_tok, D), BF16), jax.ShapeDtypeStruct((t_tok, D), F32)],
        scratch=[pltpu.VMEM((8, LANES), F32), pltpu.VMEM((1, LANES), F32)], carries=carries)


def _lru_bwd(dy, xr, xg, hseq, cw, cb, wa, wx, ba, bx, lam, carries=()):
    t_tok = xr.shape[0]
    nt = t_tok // LRU_ROWS
    rows = LRU_ROWS

    def body(dy_ref, xr_ref, xrh_ref, xg_ref, h_ref, hh_ref, cw_ref, cb_ref, wa_ref, wx_ref, ba_ref, bx_ref, lam_ref,
             dxr_ref, dxg_ref, dvec_ref, dwa_ref, dwx_ref, gcar_s, acar_s, head_s, tmp_s):
        t = pl.program_id(1)
        first_tile = t == nt - 1

        @pl.when(t == 0)
        def _():
            gcar_s[...] = jnp.zeros_like(gcar_s)
            acar_s[...] = jnp.zeros_like(acar_s)
            head_s[...] = jnp.zeros_like(head_s)
            dvec_ref[...] = jnp.zeros_like(dvec_ref)
            dwa_ref[...] = jnp.zeros_like(dwa_ref)
            dwx_ref[...] = jnp.zeros_like(dwx_ref)

        x = xr_ref[...]
        halo = jnp.where(first_tile, 0.0, xrh_ref[...])
        shifted, xc, xcb, r, ig, sp, a, mult = _lru_pre(x, halo, cw_ref, cb_ref, wa_ref, wx_ref, ba_ref, bx_ref, lam_ref)
        hv = h_ref[...]
        dyv = dy_ref[...]
        gl, glg = _gelu_parts(xg_ref[...])
        dxg_ref[...] = dyv * hv * glg
        acc_a = _shift_up(a, 1, acar_s[...])
        acc_b = dyv * gl
        d = 1
        while d < rows:
            acc_b = acc_a * _shift_up(acc_b, d, 0.0) + acc_b
            acc_a = acc_a * _shift_up(acc_a, d, 1.0)
            d *= 2
        g = acc_b + acc_a * gcar_s[...]
        hhalo = jnp.where(first_tile, 0.0, hh_ref[...])
        hprev = pltpu.roll(jnp.concatenate([hhalo, hv], axis=0), 1, 0)[8:]
        dmult = g * ig * xc
        dlog_a = a * (g * hprev) - dmult * a * a / mult
        dig = g * mult * xc
        dxc = g * mult * ig
        dzr = (dlog_a * (-LRU_C * sp)) * r * (1.0 - r)
        dzx = dig * ig * (1.0 - ig)
        dzrb, dzxb = dzr.astype(BF16), dzx.astype(BF16)
        dxc = dxc + _nt(dzrb, wa_ref[...]) + _nt(dzxb, wx_ref[...])
        dwa_ref[...] += _tn(xcb, dzrb)
        dwx_ref[...] += _tn(xcb, dzxb)
        dsp = jnp.sum(dlog_a * (-LRU_C * r), axis=0, keepdims=True)
        dlam = dsp * (-jax.nn.sigmoid(-lam_ref[...]))
        vrow = lax.broadcasted_iota(jnp.int32, (8, LANES), 0)
        upd = jnp.where(vrow == 4, jnp.sum(dxc, axis=0, keepdims=True), 0.0)
        upd = jnp.where(vrow == 5, jnp.sum(dzr, axis=0, keepdims=True), upd)
        upd = jnp.where(vrow == 6, jnp.sum(dzx, axis=0, keepdims=True), upd)
        upd = jnp.where(vrow == 7, dlam, upd)
        for k in range(CONV_WIDTH):
            upd = jnp.where(vrow == CONV_WIDTH - 1 - k, jnp.sum(dxc * shifted[k], axis=0, keepdims=True), upd)
        dvec_ref[...] += upd
        ext = jnp.concatenate([dxc, head_s[...]], axis=0)
        dxr = cw_ref[pl.ds(CONV_WIDTH - 1, 1), :] * dxc
        for k in (1, 2, 3):
            dxr = dxr + cw_ref[pl.ds(CONV_WIDTH - 1 - k, 1), :] * pltpu.roll(ext, rows + 8 - k, 0)[:rows]
        dxr_ref[...] = dxr
        tmp_s[...] = g
        gcar_s[...] = tmp_s[pl.ds(0, 1), :]
        tmp_s[...] = a
        acar_s[...] = tmp_s[pl.ds(0, 1), :]
        tmp_s[...] = dxc
        head_s[...] = tmp_s[pl.ds(0, 8), :]

    tile, halo, vec, cws, mat = _lru_specs(nt, True)
    return _call(
        body, name="lru_bwd", grid=(D // LANES, nt), ins=[dy, xr, xr, xg, hseq, hseq, cw, cb, wa, wx, ba, bx, lam],
        in_specs=[tile, tile, halo, tile, tile, halo, cws, vec, mat, mat, vec, vec, vec],
        out_specs=[tile, tile, pl.BlockSpec((8, LANES), lambda cb, t: (0, cb)), mat, mat],
        out_shape=[jax.ShapeDtypeStruct((t_tok, D), F32), jax.ShapeDtypeStruct((t_tok, D), F32),
                   jax.ShapeDtypeStruct((8, D), F32), jax.ShapeDtypeStruct((D // LANES, LANES, LANES), F32),
                   jax.ShapeDtypeStruct((D // LANES, LANES, LANES), F32)],
        scratch=[pltpu.VMEM((1, LANES), F32), pltpu.VMEM((1, LANES), F32), pltpu.VMEM((8, LANES), F32),
                 pltpu.VMEM((rows, LANES), F32)], carries=carries)


def _t5_bucket_np(rel):
    nb = N_BUCKETS // 2
    max_exact = nb // 2
    ret = np.where(rel > 0, nb, 0)
    n = np.abs(rel)
    nf = np.maximum(n, 1).astype(np.float32)
    large = max_exact + (np.log(nf / np.float32(max_exact)) / np.float32(math.log(MAX_DISTANCE / max_exact))
                         * np.float32(nb - max_exact)).astype(np.int32)
    large = np.minimum(large, nb - 1)
    return ret + np.where(n < max_exact, n, large)


def _bucket_map():
    r = np.arange(QT)[:, None]
    c = np.arange(KW)[None, :]
    j = c - (r // CHUNK) * CHUNK
    band = (j >= 0) & (j < WINDOW + CHUNK)
    return np.where(band, _t5_bucket_np(c - r - WINDOW), -1).astype(np.int32)


def _attn_specs(nt, reverse):
    def tt(i):
        return nt - 1 - i if reverse else i
    kvw = N_KV * LANES
    qs = pl.BlockSpec((QT, D), lambda i: (tt(i), 0))
    cur = pl.BlockSpec((QT, kvw), lambda i: (tt(i), 0))
    prev = pl.BlockSpec((WINDOW, kvw), lambda i: (jnp.maximum(tt(i) * (QT // WINDOW) - 1, 0), 0))
    lse = pl.BlockSpec((QT, LANES), lambda i: (tt(i), 0))
    return qs, cur, prev, lse


def _attn_fwd(q, kd, vd, bias, sinks, carries=()):
    t_tok = q.shape[0]
    nt = t_tok // QT

    def body(q_ref, kp_ref, kc_ref, vp_ref, vc_ref, bias_ref, sink_ref, o_ref, lse_ref):
        i = pl.program_id(0)
        col = lax.broadcasted_iota(jnp.int32, (1, KW), 1)
        first = jnp.where((i == 0) & (col < WINDOW), NEG_INF, 0.0)
        lane = lax.broadcasted_iota(jnp.int32, (QT, LANES), 1)
        lse_t = jnp.zeros((QT, LANES), F32)
        for g in range(N_KV):
            kwin = jnp.concatenate([kp_ref[:, pl.ds(g * LANES, LANES)], kc_ref[:, pl.ds(g * LANES, LANES)]], axis=0)
            vwin = jnp.concatenate([vp_ref[:, pl.ds(g * LANES, LANES)], vc_ref[:, pl.ds(g * LANES, LANES)]], axis=0)
            for slab in (2 * g, 2 * g + 1):
                qs = q_ref[:, pl.ds(slab * LANES, LANES)]
                o_slab = jnp.zeros((QT, LANES), F32)
                for half in range(2):
                    hd = 2 * slab + half
                    mine = (lane >= half * HEAD_DIM) & (lane < (half + 1) * HEAD_DIM)
                    qh = jnp.where(mine, qs, jnp.zeros_like(qs)) * jnp.asarray(HEAD_DIM ** -0.5, BF16)
                    s = _nt(qh, kwin) + (bias_ref[hd] + first)
                    sk = sink_ref[hd]
                    m = jnp.maximum(jnp.max(s, axis=-1, keepdims=True), sk)
                    e = jnp.exp(s - m)
                    l = jnp.sum(e, axis=-1, keepdims=True) + jnp.exp(sk - m)
                    p = e / l
                    o_slab = jnp.where(mine, _nn(p.astype(BF16), vwin), o_slab)
                    lse_t = jnp.where(lane == hd, m + jnp.log(l), lse_t)
                o_ref[:, pl.ds(slab * LANES, LANES)] = o_slab.astype(BF16)
        lse_ref[...] = lse_t

    qs, cur, prev, lse = _attn_specs(nt, False)
    return _call(
        body, name="attn_fwd", grid=(nt,), ins=[q, kd, kd, vd, vd, bias, sinks],
        in_specs=[qs, prev, cur, prev, cur, _const_spec((N_HEADS, QT, KW)), pl.BlockSpec(memory_space=pltpu.SMEM)],
        out_specs=[qs, lse],
        out_shape=[jax.ShapeDtypeStruct((t_tok, D), BF16), jax.ShapeDtypeStruct((t_tok, LANES), F32)],
        vmem=48 * 2 ** 20, carries=carries)


def _attn_bwd(q, kd, vd, o, do, lse, bias, sinks, carries=()):
    t_tok = q.shape[0]
    nt = t_tok // QT
    kvw = N_KV * LANES

    def body(q_ref, kp_ref, kc_ref, vp_ref, vc_ref, o_ref, do_ref, lse_ref, bias_ref, sink_ref,
             dq_ref, dk_ref, dv_ref, ds_ref, dsink_ref, kcar_s, vcar_s):
        i = pl.program_id(0)
        tile = nt - 1 - i

        @pl.when(i == 0)
        def _():
            kcar_s[...] = jnp.zeros_like(kcar_s)
            vcar_s[...] = jnp.zeros_like(vcar_s)
            ds_ref[...] = jnp.zeros_like(ds_ref)
            dsink_ref[...] = jnp.zeros_like(dsink_ref)

        col = lax.broadcasted_iota(jnp.int32, (1, KW), 1)
        first = jnp.where((tile == 0) & (col < WINDOW), NEG_INF, 0.0)
        lane = lax.broadcasted_iota(jnp.int32, (QT, LANES), 1)
        lane_k = lax.broadcasted_iota(jnp.int32, (KW, LANES), 1)
        lane_1 = lax.broadcasted_iota(jnp.int32, (1, LANES), 1)
        lse_t = lse_ref[...]
        dsink = jnp.zeros((1, LANES), F32)
        for g in range(N_KV):
            kwin = jnp.concatenate([kp_ref[:, pl.ds(g * LANES, LANES)], kc_ref[:, pl.ds(g * LANES, LANES)]], axis=0)
            vwin = jnp.concatenate([vp_ref[:, pl.ds(g * LANES, LANES)], vc_ref[:, pl.ds(g * LANES, LANES)]], axis=0)
            dk_acc = jnp.zeros((KW, LANES), F32)
            dv_acc = jnp.zeros((KW, LANES), F32)
            for slab in (2 * g, 2 * g + 1):
                qs = q_ref[:, pl.ds(slab * LANES, LANES)]
                dos = do_ref[:, pl.ds(slab * LANES, LANES)]
                od = dos.astype(F32) * o_ref[:, pl.ds(slab * LANES, LANES)].astype(F32)
                dq_slab = jnp.zeros((QT, LANES), F32)
                for half in range(2):
                    hd = 2 * slab + half
                    mine = (lane >= half * HEAD_DIM) & (lane < (half + 1) * HEAD_DIM)
                    qh = jnp.where(mine, qs, jnp.zeros_like(qs)) * jnp.asarray(HEAD_DIM ** -0.5, BF16)
                    doh = jnp.where(mine, dos, jnp.zeros_like(dos))
                    s = _nt(qh, kwin) + (bias_ref[hd] + first)
                    lse_h = jnp.sum(jnp.where(lane == hd, lse_t, 0.0), axis=-1, keepdims=True)
                    p = jnp.exp(s - lse_h)
                    dp = _nt(doh, vwin)
                    drow = jnp.sum(jnp.where(mine, od, 0.0), axis=-1, keepdims=True)
                    ds = p * (dp - drow)
                    ds_ref[hd] += ds
                    psink = jnp.exp(sink_ref[hd] - lse_h)
                    dsink = dsink + jnp.where(lane_1 == hd, -jnp.sum(psink * drow, axis=0, keepdims=True), 0.0)
                    dsb = ds.astype(BF16)
                    dq_slab = jnp.where(mine, _nn(dsb, kwin) * (HEAD_DIM ** -0.5), dq_slab)
                    dk_acc = dk_acc + _tn(dsb, qh)
                    dv_acc = dv_acc + _tn(p.astype(BF16), doh)
                dq_ref[:, pl.ds(slab * LANES, LANES)] = dq_slab.astype(BF16)
            dk_f = jnp.where(lane_k < HEAD_DIM, dk_acc + pltpu.roll(dk_acc, HEAD_DIM, 1), 0.0)
            dv_f = jnp.where(lane_k < HEAD_DIM, dv_acc + pltpu.roll(dv_acc, HEAD_DIM, 1), 0.0)
            for acc, out_ref, car in ((dk_f, dk_ref, kcar_s), (dv_f, dv_ref, vcar_s)):
                cs = pl.ds(g * LANES, LANES)
                out_ref[pl.ds(0, QT - WINDOW), cs] = acc[WINDOW:QT].astype(BF16)
                out_ref[pl.ds(QT - WINDOW, WINDOW), cs] = (acc[QT:KW] + car[:, cs]).astype(BF16)
                car[:, cs] = acc[0:WINDOW]
        dsink_ref[...] += dsink

    qs, cur, prev, lse_s = _attn_specs(nt, True)
    return _call(
        body, name="attn_bwd", grid=(nt,), ins=[q, kd, kd, vd, vd, o, do, lse, bias, sinks],
        in_specs=[qs, prev, cur, prev, cur, qs, qs, lse_s, _const_spec((N_HEADS, QT, KW)),
                  pl.BlockSpec(memory_space=pltpu.SMEM)],
        out_specs=[qs, cur, cur, _const_spec((N_HEADS, QT, KW)), _const_spec((1, LANES))],
        out_shape=[jax.ShapeDtypeStruct((t_tok, D), BF16), jax.ShapeDtypeStruct((t_tok, kvw), BF16),
                   jax.ShapeDtypeStruct((t_tok, kvw), BF16), jax.ShapeDtypeStruct((N_HEADS, QT, KW), F32),
                   jax.ShapeDtypeStruct((1, LANES), F32)],
        scratch=[pltpu.VMEM((WINDOW, kvw), F32), pltpu.VMEM((WINDOW, kvw), F32)],
        vmem=VMEM_BIG, carries=carries)


def _bias_tile(table, bmap):
    def body(tab_ref, bm_ref, out_ref):
        bm = bm_ref[...]

        def per_head(hd, carry):
            acc = jnp.full((QT, KW), NEG_INF, F32)
            for b in range(N_BUCKETS):
                acc = jnp.where(bm == b, tab_ref[b, hd], acc)
            out_ref[hd] = acc
            return carry

        lax.fori_loop(0, N_HEADS, per_head, 0)

    return pl.pallas_call(
        body, name="bias_tile", out_shape=jax.ShapeDtypeStruct((N_HEADS, QT, KW), F32),
        in_specs=[pl.BlockSpec(memory_space=pltpu.SMEM), pl.BlockSpec(memory_space=pltpu.VMEM)],
        out_specs=pl.BlockSpec(memory_space=pltpu.VMEM))(table, bmap)


def _bias_grad(ds_acc, bmap):
    def body(ds_ref, bm_ref, out_ref):
        row = lax.broadcasted_iota(jnp.int32, (N_BUCKETS, LANES), 0)
        lane = lax.broadcasted_iota(jnp.int32, (N_BUCKETS, LANES), 1)
        bm = bm_ref[...]

        def per_head(hd, res):
            dsv = ds_ref[hd]
            for b in range(N_BUCKETS):
                val = jnp.sum(jnp.sum(jnp.where(bm == b, dsv, 0.0), axis=0, keepdims=True), axis=1, keepdims=True)
                res = jnp.where((row == b) & (lane == hd), val, res)
            return res

        out_ref[...] = lax.fori_loop(0, N_HEADS, per_head, jnp.zeros((N_BUCKETS, LANES), F32))

    return pl.pallas_call(
        body, name="bias_grad", out_shape=jax.ShapeDtypeStruct((N_BUCKETS, LANES), F32),
        in_specs=[pl.BlockSpec(memory_space=pltpu.VMEM), pl.BlockSpec(memory_space=pltpu.VMEM)],
        out_specs=pl.BlockSpec(memory_space=pltpu.VMEM))(ds_acc, bmap)


MIXOUT = ("w_lru_out", "w_attn_out", "w_o")


def _mix_out_fwd(ya_in, o, gs, h, g, wg, tm=256, carries=()):
    t_tok = h.shape[0]
    nt = t_tok // tm
    groups = _groups_of(MIXOUT)

    def body(ya_ref, o_ref, gs_ref, h_ref, g_ref, wg_ref, hout_ref, yao_ref, ybo_ref, z_ref, wa_s, wb_s, wo_s, sems):
        i = pl.program_id(0)

        @pl.when(i == 0)
        def _():
            _load_weights({groups[0]: wg_ref}, list(zip(MIXOUT, (wa_s, wb_s, wo_s))), sems)

        ya = _nn(ya_ref[...], wa_s[...])
        yb = _nn(o_ref[...], wb_s[...])
        yao_ref[...] = ya
        ybo_ref[...] = yb
        merged = gs_ref[:, pl.ds(0, D)] * ya + gs_ref[:, pl.ds(D, D)] * yb
        z = _nn(merged.astype(BF16), wo_s[...])
        z_ref[...] = z
        zn, _, _ = _rms_fwd(z, g_ref[...])
        hout_ref[...] = h_ref[...] + zn

    return _call(
        body, name="mix_out_fwd", grid=(nt,), ins=[ya_in, o, gs, h, g, wg[groups[0]]],
        in_specs=[_row_spec(tm, D), _row_spec(tm, D), _row_spec(tm, 2 * D), _row_spec(tm, D), _const_spec((1, D)), ANY],
        out_specs=[_row_spec(tm, D)] * 4,
        out_shape=[jax.ShapeDtypeStruct((t_tok, D), F32)] * 4,
        scratch=[pltpu.VMEM((D, D), BF16)] * 3 + [DMA((3 * NDEV,))],
        vmem=48 * 2 ** 20, carries=carries)


def _mix_out_bwd(dh, z, ya, yb, gs, ya_in, o, g, wg, tm=256, carries=()):
    t_tok = dh.shape[0]
    nt = t_tok // tm
    groups = _groups_of(MIXOUT)

    def body(dh_ref, z_ref, ya_ref, yb_ref, gs_ref, yain_ref, o_ref, g_ref, wg_ref,
             dyain_ref, do_ref, dgpre_ref, dg_ref, dwa_ref, dwb_ref, dwo_ref,
             wa_s, wb_s, wo_s, acca_s, accb_s, acco_s, stage_s, sems):
        i = pl.program_id(0)

        @pl.when(i == 0)
        def _():
            _load_weights({groups[0]: wg_ref}, list(zip(MIXOUT, (wa_s, wb_s, wo_s))), sems)
            for acc in (acca_s, accb_s, acco_s):
                acc[...] = jnp.zeros_like(acc)
            dg_ref[...] = jnp.zeros_like(dg_ref)

        gv = g_ref[...]
        _, zh, r = _rms_fwd(z_ref[...], gv)
        dz, dg = _rms_bwd(dh_ref[...], zh, r, gv)
        dg_ref[...] += dg
        dzb = dz.astype(BF16)
        ga, gb = gs_ref[:, pl.ds(0, D)], gs_ref[:, pl.ds(D, D)]
        ya_v, yb_v = ya_ref[...], yb_ref[...]
        merged = ga * ya_v + gb * yb_v
        acco_s[...] += _tn(merged.astype(BF16), dzb)
        dm = _nt(dzb, wo_s[...])
        dgpre_ref[:, pl.ds(0, D)] = dm * ya_v * ga * (1.0 - ga)
        dgpre_ref[:, pl.ds(D, D)] = dm * yb_v * gb * (1.0 - gb)
        dya = (dm * ga).astype(BF16)
        dyb = (dm * gb).astype(BF16)
        dyain_ref[...] = _nt(dya, wa_s[...])
        do_ref[...] = _nt(dyb, wb_s[...]).astype(BF16)
        acca_s[...] += _tn(yain_ref[...], dya)
        accb_s[...] += _tn(o_ref[...], dyb)

        @pl.when(i == nt - 1)
        def _():
            _flush(acca_s, dwa_ref, stage_s, [(0, 0, D)])
            _flush(accb_s, dwb_ref, stage_s, [(0, 0, D)])
            _flush(acco_s, dwo_ref, stage_s, [(0, 0, D)])

    return _call(
        body, name="mix_out_bwd", grid=(nt,), ins=[dh, z, ya, yb, gs, ya_in, o, g, wg[groups[0]]],
        in_specs=[_row_spec(tm, D)] * 4 + [_row_spec(tm, 2 * D), _row_spec(tm, D), _row_spec(tm, D),
                                            _const_spec((1, D)), ANY],
        out_specs=[_row_spec(tm, D), _row_spec(tm, D), _row_spec(tm, 2 * D), _const_spec((1, D)), ANY, ANY, ANY],
        out_shape=[jax.ShapeDtypeStruct((t_tok, D), F32), jax.ShapeDtypeStruct((t_tok, D), BF16),
                   jax.ShapeDtypeStruct((t_tok, 2 * D), F32), jax.ShapeDtypeStruct((1, D), F32)]
        + [jax.ShapeDtypeStruct((D, D), BF16)] * 3,
        scratch=[pltpu.VMEM((D, D), BF16)] * 3 + [pltpu.VMEM((D, D), F32)] * 3
        + [pltpu.VMEM((256, D), BF16), DMA((3 * NDEV,))],
        vmem=VMEM_BIG, carries=carries)


def _sum_parts(parts_list):
    n = len(parts_list)
    _, r, c = parts_list[0].shape
    tc = 256

    def body(*refs):
        for p_ref, o_ref in zip(refs[:n], refs[n:]):
            acc = p_ref[0].astype(F32)
            for s in range(1, NDEV):
                acc = acc + p_ref[s].astype(F32)
            o_ref[...] = acc

    return pl.pallas_call(
        body, name=f"sum_parts_{r}", grid=(c // tc,),
        in_specs=[pl.BlockSpec((NDEV, r, tc), lambda i: (0, 0, i))] * n,
        out_specs=[pl.BlockSpec((r, tc), lambda i: (0, i))] * n,
        out_shape=[jax.ShapeDtypeStruct((r, c), F32)] * n,
        compiler_params=pltpu.CompilerParams(dimension_semantics=("arbitrary",), vmem_limit_bytes=48 * 2 ** 20),
    )(*parts_list)


def _adamw_math(w, g, m, v):
    m = ADAM_B1 * m + (1.0 - ADAM_B1) * g
    v = ADAM_B2 * v + (1.0 - ADAM_B2) * (g * g)
    m_hat = m / (1.0 - ADAM_B1 ** ADAM_STEP)
    v_hat = v / (1.0 - ADAM_B2 ** ADAM_STEP)
    delta = -ADAM_LR * (m_hat / (jnp.sqrt(v_hat) + ADAM_EPS) + ADAM_WD * w)
    return delta, m, v


def _adamw_body(n):
    def body(*refs):
        for k in range(n):
            g_ref, w_ref, m_ref, v_ref = refs[4 * k:4 * k + 4]
            d_ref, nm_ref, nv_ref = refs[4 * n + 3 * k:4 * n + 3 * k + 3]
            d, m, v = _adamw_math(w_ref[...], g_ref[...], m_ref[...], v_ref[...])
            d_ref[...] = d
            nm_ref[...] = m
            nv_ref[...] = v
    return body


def _adamw(items):
    n = len(items)
    r, c = items[0][0].shape
    tr = r if r * c <= 2 ** 18 else max(t for t in range(8, 65, 8) if r % t == 0)
    spec = pl.BlockSpec((tr, c), lambda i: (i, 0))
    outs = pl.pallas_call(
        _adamw_body(n), name=f"adamw_{r}x{c}", grid=(r // tr,),
        in_specs=[spec] * (4 * n), out_specs=[spec] * (3 * n),
        out_shape=[jax.ShapeDtypeStruct((r, c), F32)] * (3 * n),
        compiler_params=pltpu.CompilerParams(dimension_semantics=("arbitrary",), vmem_limit_bytes=40 * 2 ** 20),
    )(*[a for it in items for a in it])
    return [tuple(outs[3 * k:3 * k + 3]) for k in range(n)]


def _adamw_small(items):
    n = len(items)
    vm = pl.BlockSpec(memory_space=pltpu.VMEM)
    outs = pl.pallas_call(
        _adamw_body(n), name="adamw_small", in_specs=[vm] * (4 * n), out_specs=[vm] * (3 * n),
        out_shape=[jax.ShapeDtypeStruct(it[1].shape, F32) for it in items for _ in range(3)],
    )(*[a for it in items for a in it])
    return [tuple(outs[3 * k:3 * k + 3]) for k in range(n)]


def _pack_small(arrs):
    rows, offs = [], []
    total = 0
    for a in arrs:
        flat = a.reshape(-1).astype(F32)
        nr = -(-flat.shape[0] // LANES)
        flat = jnp.pad(flat, (0, nr * LANES - flat.shape[0]))
        rows.append(flat.reshape(nr, LANES))
        offs.append((total, nr))
        total += nr
    pad = -total % 8
    if pad:
        rows.append(jnp.zeros((pad, LANES), F32))
    return jnp.concatenate(rows, axis=0), offs


def _unpack_small(pack, offs, shapes):
    out = []
    for (o, nr), shp in zip(offs, shapes):
        size = int(np.prod(shp))
        out.append(pack[o:o + nr].reshape(-1)[:size].reshape(shp))
    return out


def _sum_small(gathered, rows):
    def body(p_ref, o_ref):
        acc = p_ref[pl.ds(0, rows), :]
        for s in range(1, NDEV):
            acc = acc + p_ref[pl.ds(s * rows, rows), :]
        o_ref[...] = acc

    return pl.pallas_call(
        body, name="sum_small", out_shape=jax.ShapeDtypeStruct((rows, LANES), F32),
        in_specs=[pl.BlockSpec(memory_space=pltpu.VMEM)], out_specs=pl.BlockSpec(memory_space=pltpu.VMEM))(gathered)


def _block_diag(w):
    w = w.reshape(D // LANES, 2, LRU_BLOCK, LRU_BLOCK)
    z = jnp.zeros((D // LANES, LRU_BLOCK, LRU_BLOCK), w.dtype)
    top = jnp.concatenate([w[:, 0], z], axis=2)
    bot = jnp.concatenate([z, w[:, 1]], axis=2)
    return jnp.concatenate([top, bot], axis=1)


def _block_diag_grad(dw):
    a = dw[:, :LRU_BLOCK, :LRU_BLOCK]
    b = dw[:, LRU_BLOCK:, LRU_BLOCK:]
    return jnp.stack([a, b], axis=1).reshape(D // LRU_BLOCK, LRU_BLOCK, LRU_BLOCK)


def kernel(x, ffn1_pre_g, ffn1_w1, ffn1_w3, ffn1_w2, ffn1_post_g, mix_pre_g, w_in, conv_w, conv_b, rg_a_w, rg_a_b, rg_x_w, rg_x_b, lru_lambda, w_lru_out, attn_sinks, rel_bias, w_attn_out, w_gate, b_gate, w_o, mix_post_g, ffn2_pre_g, ffn2_w1, ffn2_w3, ffn2_w2, ffn2_post_g, loss_target, m_ffn1_pre_g, m_ffn1_w1, m_ffn1_w3, m_ffn1_w2, m_ffn1_post_g, m_mix_pre_g, m_w_in, m_conv_w, m_conv_b, m_rg_a_w, m_rg_a_b, m_rg_x_w, m_rg_x_b, m_lru_lambda, m_w_lru_out, m_attn_sinks, m_rel_bias, m_w_attn_out, m_w_gate, m_b_gate, m_w_o, m_mix_post_g, m_ffn2_pre_g, m_ffn2_w1, m_ffn2_w3, m_ffn2_w2, m_ffn2_post_g, v_ffn1_pre_g, v_ffn1_w1, v_ffn1_w3, v_ffn1_w2, v_ffn1_post_g, v_mix_pre_g, v_w_in, v_conv_w, v_conv_b, v_rg_a_w, v_rg_a_b, v_rg_x_w, v_rg_x_b, v_lru_lambda, v_w_lru_out, v_attn_sinks, v_rel_bias, v_w_attn_out, v_w_gate, v_b_gate, v_w_o, v_mix_post_g, v_ffn2_pre_g, v_ffn2_w1, v_ffn2_w3, v_ffn2_w2, v_ffn2_post_g):
    names = ["ffn1_pre_g", "ffn1_w1", "ffn1_w3", "ffn1_w2", "ffn1_post_g", "mix_pre_g", "w_in", "conv_w", "conv_b",
             "rg_a_w", "rg_a_b", "rg_x_w", "rg_x_b", "lru_lambda", "w_lru_out", "attn_sinks", "rel_bias", "w_attn_out",
             "w_gate", "b_gate", "w_o", "mix_post_g", "ffn2_pre_g", "ffn2_w1", "ffn2_w3", "ffn2_w2", "ffn2_post_g"]
    ws = dict(zip(names, (ffn1_pre_g, ffn1_w1, ffn1_w3, ffn1_w2, ffn1_post_g, mix_pre_g, w_in, conv_w, conv_b, rg_a_w,
                          rg_a_b, rg_x_w, rg_x_b, lru_lambda, w_lru_out, attn_sinks, rel_bias, w_attn_out, w_gate,
                          b_gate, w_o, mix_post_g, ffn2_pre_g, ffn2_w1, ffn2_w3, ffn2_w2, ffn2_post_g)))
    ms = dict(zip(names, (m_ffn1_pre_g, m_ffn1_w1, m_ffn1_w3, m_ffn1_w2, m_ffn1_post_g, m_mix_pre_g, m_w_in, m_conv_w,
                          m_conv_b, m_rg_a_w, m_rg_a_b, m_rg_x_w, m_rg_x_b, m_lru_lambda, m_w_lru_out, m_attn_sinks,
                          m_rel_bias, m_w_attn_out, m_w_gate, m_b_gate, m_w_o, m_mix_post_g, m_ffn2_pre_g, m_ffn2_w1,
                          m_ffn2_w3, m_ffn2_w2, m_ffn2_post_g)))
    vs = dict(zip(names, (v_ffn1_pre_g, v_ffn1_w1, v_ffn1_w3, v_ffn1_w2, v_ffn1_post_g, v_mix_pre_g, v_w_in, v_conv_w,
                          v_conv_b, v_rg_a_w, v_rg_a_b, v_rg_x_w, v_rg_x_b, v_lru_lambda, v_w_lru_out, v_attn_sinks,
                          v_rel_bias, v_w_attn_out, v_w_gate, v_b_gate, v_w_o, v_mix_post_g, v_ffn2_pre_g, v_ffn2_w1,
                          v_ffn2_w3, v_ffn2_w2, v_ffn2_post_g)))
    me = 4 * lax.axis_index("x") + 2 * lax.axis_index("y") + lax.axis_index("c")
    vec = lambda n: ws[n].reshape(1, -1)

    def shard2d(name):
        if name == "conv":
            row = lax.bitcast_convert_type(conv_w.reshape(CONV_WIDTH, LANES), BF16).reshape(1, D)
            return jnp.concatenate([row, jnp.zeros((LOC["conv"][2] - 1, D), BF16)], axis=0)
        a = ws[name].reshape(ws[name].shape[-2], ws[name].shape[-1])
        return (a.T if name in COL_SHARDED else a).astype(BF16)

    packs = {g: jnp.concatenate([shard2d(n) for n, _ in members], axis=0) for g, members in GROUPS}

    wg = {}
    _, ((wg["ffn1"],),) = _call(None, name="allgather_ffn1", grid=(), ins=[], in_specs=[], out_shape=[], out_specs=[],
                                carries=[_AllGather(packs["ffn1"])])
    conv_rows = wg["ffn1"].reshape(NDEV, GROUP_ROWS["ffn1"], D)[:, LOC["conv"][1]]
    cw = jnp.transpose(lax.bitcast_convert_type(conv_rows.reshape(NDEV, CONV_WIDTH, LANES, 2), F32),
                       (1, 0, 2)).reshape(CONV_WIDTH, D)
    bmap = jnp.asarray(_bucket_map())
    bias = _bias_tile(rel_bias, bmap)
    sinks = attn_sinks.reshape(N_HEADS)
    wa_bd = _block_diag(rg_a_w.reshape(D // LRU_BLOCK, LRU_BLOCK, LRU_BLOCK)).astype(BF16)
    wx_bd = _block_diag(rg_x_w.reshape(D // LRU_BLOCK, LRU_BLOCK, LRU_BLOCK)).astype(BF16)
    lru_args = (cw, vec("conv_b"), wa_bd, wx_bd, vec("rg_a_b"), vec("rg_x_b"), vec("lru_lambda"))
    x2, tgt = x[0], loss_target[0]

    (h1, a1, b1, f1), ((wg["mixin"],),) = _ffn_fwd(
        x2, vec("ffn1_pre_g"), vec("ffn1_post_g"), wg, ("ffn1_w1", "ffn1_w3", "ffn1_w2"),
        carries=[_AllGather(packs["mixin"])])
    (q, kd, vd, xr, xg, gs), ((wg["mixout"],),) = _mix_proj_fwd(
        h1, vec("mix_pre_g"), vec("b_gate"), wg, carries=[_AllGather(packs["mixout"])])
    (ya_in, hseq), ((wg["ffn2a"],),) = _lru_fwd(xr, xg, *lru_args, carries=[_AllGather(packs["ffn2a"])])
    (o, lse), ((wg["ffn2b"],),) = _attn_fwd(q, kd, vd, bias, sinks, carries=[_AllGather(packs["ffn2b"])])
    (h2, ya, yb, z), _ = _mix_out_fwd(ya_in, o, gs, h1, vec("mix_post_g"), wg)
    (_, a2, b2, f2, dy, loss_part), _ = _ffn_fwd(h2, vec("ffn2_pre_g"), vec("ffn2_post_g"), wg,
                                                ("ffn2_w1", "ffn2_w3", "ffn2_w2"), target=tgt)
    loss = lax.psum(loss_part[0, 0], ("x", "y", "c"))

    gsm, parts = {}, {}
    (da, db, gsm["ffn2_post_g"], g_w2), _ = _ffn_bwd_a(dy, f2, a2, b2, vec("ffn2_post_g"), wg, "ffn2_w2")
    (g_w1, g_w3), ((parts["ffn2_w2"],),) = _ffn_bwd_dw(da, db, h2, vec("ffn2_pre_g"), "ffn2",
                                                       carries=[_ReduceScatterSend([g_w2])])
    (dh2, gsm["ffn2_pre_g"]), ((parts["ffn2_w1"],),) = _ffn_bwd_dx(
        da, db, h2, dy, vec("ffn2_pre_g"), wg, "ffn2_w1", "ffn2_w3", carries=[_ReduceScatterSend([g_w1])])
    (dya_in, do, dgpre, gsm["mix_post_g"], g_wa, g_wb, g_wo), ((parts["ffn2_w3"],),) = _mix_out_bwd(
        dh2, z, ya, yb, gs, ya_in, o, vec("mix_post_g"), wg, carries=[_ReduceScatterSend([g_w3])])
    (dq, dk, dv, ds_acc, dsink), (mixout_parts,) = _attn_bwd(
        q, kd, vd, o, do, lse, bias, sinks, carries=[_ReduceScatterSend([g_wa, g_wb, g_wo])])
    parts["w_lru_out"], parts["w_attn_out"], parts["w_o"] = mixout_parts
    (dxr, dxg, dvec, dwa, dwx), _ = _lru_bwd(dya_in, xr, xg, hseq, *lru_args)
    (dh1, gsm["mix_pre_g"], gsm["b_gate"], g_win, g_wgate), _ = _mix_proj_bwd(
        dq, dk, dv, dxr, dxg, dgpre, h1, dh2, vec("mix_pre_g"), wg)
    (da, db, gsm["ffn1_post_g"], g_w2), ((parts["w_in"],),) = _ffn_bwd_a(
        dh1, f1, a1, b1, vec("ffn1_post_g"), wg, "ffn1_w2", carries=[_ReduceScatterSend([g_win])])
    (g_w1, g_w3), ((parts["w_gate"],),) = _ffn_bwd_dw(da, db, x2, vec("ffn1_pre_g"), "ffn1",
                                                      carries=[_ReduceScatterSend([g_wgate])])
    (grad_x, gsm["ffn1_pre_g"]), ((parts["ffn1_w2"],),) = _ffn_bwd_dx(
        da, db, x2, dh1, vec("ffn1_pre_g"), wg, "ffn1_w1", "ffn1_w3", carries=[_ReduceScatterSend([g_w2])])
    gsm["conv_w"] = dvec[0:CONV_WIDTH]
    gsm["conv_b"], gsm["rg_a_b"], gsm["rg_x_b"], gsm["lru_lambda"] = dvec[4], dvec[5], dvec[6], dvec[7]
    gsm["rg_a_w"] = _block_diag_grad(dwa)
    gsm["rg_x_w"] = _block_diag_grad(dwx)
    gsm["attn_sinks"] = dsink[0, :N_HEADS]
    gsm["rel_bias"] = _bias_grad(ds_acc, bmap)[:, :N_HEADS]
    small_pack, offs = _pack_small([gsm[n] for n in SMALL])
    _, ((parts["ffn1_w1"], parts["ffn1_w3"]), (small_all,)) = _call(
        None, name="final_exchange", grid=(), ins=[], in_specs=[], out_shape=[], out_specs=[],
        carries=[_ReduceScatterSend([g_w1, g_w3]), _AllGather(small_pack)])

    grads = {}
    by_rows = {}
    for n in BIG:
        by_rows.setdefault(LOC[n][2], []).append(n)
    for group in by_rows.values():
        for n, g in zip(group, _sum_parts([parts[n] for n in group])):
            grads[n] = g
    small_sum = _sum_small(small_all, small_pack.shape[0])
    small_shapes = [(CONV_WIDTH, D) if n == "conv_w" else ws[n].shape for n in SMALL]
    for n, g in zip(SMALL, _unpack_small(small_sum, offs, small_shapes)):
        grads[n] = g
    grads["conv_w"] = lax.dynamic_slice(grads["conv_w"], (0, me * LANES), (CONV_WIDTH, LANES)).reshape(conv_w.shape)

    delta, new_m, new_v = {}, {}, {}
    t_form = [n for n in COL_SHARDED if ws[n].shape[-1] % LANES]
    view = {n: (lambda a: a.reshape(a.shape[-2], a.shape[-1]).T) if n in t_form
            else (lambda a: a.reshape(a.shape[-2], a.shape[-1])) for n in BIG}
    unview = {n: (lambda a: a.T) if n in t_form else (lambda a: a) for n in BIG}
    for n in COL_SHARDED:
        if n not in t_form:
            grads[n] = grads[n].T
    by_shape = {}
    for n in BIG:
        by_shape.setdefault(view[n](ws[n]).shape, []).append(n)
    for group in by_shape.values():
        res = _adamw([(grads[n], view[n](ws[n]), view[n](ms[n]), view[n](vs[n])) for n in group])
        for n, (d_, m_, v_) in zip(group, res):
            delta[n], new_m[n], new_v[n] = unview[n](d_), unview[n](m_), unview[n](v_)
            grads[n] = unview[n](grads[n])
    flat2d = lambda a: a.reshape(-1, a.shape[-1])
    res = _adamw_small([(flat2d(grads[n].reshape(ws[n].shape)), flat2d(ws[n]), flat2d(ms[n]), flat2d(vs[n]))
                        for n in SMALL])
    for n, (d_, m_, v_) in zip(SMALL, res):
        delta[n], new_m[n], new_v[n] = d_, m_, v_

    outs = [loss, grad_x.reshape(x.shape)]
    for src in (grads, delta, new_m, new_v):
        outs += [src[n].reshape(ws[n].shape) for n in names]
    return tuple(outs)
```

```python
import functools
import math
import operator

import numpy as np
import jax
import jax.numpy as jnp
from jax import lax
from jax.experimental import pallas as pl
from jax.experimental.pallas import tpu as pltpu

F32, BF16 = jnp.float32, jnp.bfloat16

NDEV = 8
D = 1024
FF = 2816
N_HEADS, N_KV, HEAD_DIM = 16, 4, 64
CHUNK, WINDOW = 64, 128
N_BUCKETS, MAX_DISTANCE = 32, 128
LRU_BLOCK = 64
CONV_WIDTH = 4
LRU_C = 8.0
RMS_EPS = 1e-6
NEG_INF = -1e30
LANES = 128
QT = 256
KW = QT + WINDOW
LRU_ROWS = 512
IN_W = D + 2 * N_KV * HEAD_DIM + 2 * D
INP_W = D + 2 * N_KV * LANES + 2 * D
VMEM_BIG = 58 * 2 ** 20

ADAM_LR, ADAM_B1, ADAM_B2, ADAM_EPS, ADAM_WD, ADAM_STEP = 0.001, 0.9, 0.999, 1e-08, 0.01, 10

GROUPS = (("ffn1", (("ffn1_w1", FF // NDEV), ("ffn1_w3", FF // NDEV), ("ffn1_w2", FF // NDEV), ("conv", 16))),
          ("mixin", (("w_in", IN_W // NDEV), ("w_gate", 2 * D // NDEV))),
          ("mixout", (("w_lru_out", D // NDEV), ("w_attn_out", D // NDEV), ("w_o", D // NDEV))),
          ("ffn2a", (("ffn2_w1", FF // NDEV), ("ffn2_w3", FF // NDEV))),
          ("ffn2b", (("ffn2_w2", FF // NDEV),)))
LOC, GROUP_ROWS = {}, {}
for _g, _members in GROUPS:
    _o = 0
    for _n, _r in _members:
        LOC[_n] = (_g, _o, _r)
        _o += _r
    GROUP_ROWS[_g] = _o
BIG = tuple(n for _, members in GROUPS for n, _ in members if n != "conv")
COL_SHARDED = ("ffn1_w1", "ffn1_w3", "w_in", "w_gate", "ffn2_w1", "ffn2_w3")

SMALL = ("ffn1_pre_g", "ffn1_post_g", "mix_pre_g", "conv_w", "conv_b", "rg_a_w", "rg_a_b", "rg_x_w", "rg_x_b",
         "lru_lambda", "attn_sinks", "rel_bias", "b_gate", "mix_post_g", "ffn2_pre_g", "ffn2_post_g")

MESH = pl.DeviceIdType.MESH
ANY = pl.BlockSpec(memory_space=pl.ANY)
DMA = pltpu.SemaphoreType.DMA


def _nn(a, b):
    return lax.dot_general(a, b, (((1,), (0,)), ((), ())), preferred_element_type=F32)


def _nt(a, b):
    return lax.dot_general(a, b, (((1,), (1,)), ((), ())), preferred_element_type=F32)


def _tn(a, b):
    return lax.dot_general(a, b, (((0,), (0,)), ((), ())), preferred_element_type=F32)


def _rms_fwd(x, g):
    r = lax.rsqrt(jnp.mean(x * x, axis=-1, keepdims=True) + RMS_EPS)
    xh = x * r
    return xh * g, xh, r


def _rms_bwd(dn, xh, r, g):
    dxh = dn * g
    dx = r * (dxh - xh * jnp.mean(dxh * xh, axis=-1, keepdims=True))
    return dx, jnp.sum(dn * xh, axis=0, keepdims=True)


def _row_spec(tm, c):
    return pl.BlockSpec((tm, c), lambda i: (i, 0))


def _const_spec(shape):
    nd = len(shape)
    return pl.BlockSpec(shape, lambda i: (0,) * nd)


class _AllGather:
    def __init__(self, shard):
        self.m, n = shard.shape
        self.ins = [shard]
        self.out_shape = [jax.ShapeDtypeStruct((NDEV * self.m, n), shard.dtype)]
        self.scratch = [DMA((7,)), DMA((7,)), DMA]

    def _copies(self, ins, outs, scr, all_of_them):
        x_ref, out_ref = ins[0], outs[0]
        send_sems, recv_sems, local_sem = scr
        x, y, c = lax.axis_index("x"), lax.axis_index("y"), lax.axis_index("c")
        me, sibling = (x, y, c), (x, y, 1 - c)
        chips = [(1 - x, y), (x, 1 - y), (1 - x, 1 - y)]
        m = self.m

        def rows(px, py, pc):
            return out_ref.at[pl.ds((4 * px + 2 * py + pc) * m, m), :]

        def copy(k, block, to, src=None):
            return pltpu.make_async_remote_copy(
                src_ref=rows(*block) if src is None else src, dst_ref=rows(*block),
                send_sem=send_sems.at[k], recv_sem=recv_sems.at[k], device_id=to, device_id_type=MESH)

        mine = pltpu.make_async_copy(x_ref, rows(*me), local_sem)
        first = [copy(0, me, sibling, src=x_ref)] + [copy(1 + j, me, (*chip, c), src=x_ref)
                                                     for j, chip in enumerate(chips)]
        if not all_of_them:
            return mine, first
        passed = [copy(4 + j, (*chip, c), sibling) for j, chip in enumerate(chips)]
        landed = [copy(1 + j, (*chip, c), me) for j, chip in enumerate(chips)]
        from_sibling = [copy(0, sibling, me)] + [copy(4 + j, (*chip, 1 - c), me) for j, chip in enumerate(chips)]
        return mine, first, passed, landed, from_sibling

    def start(self, ins, outs, scr):
        mine, first = self._copies(ins, outs, scr, False)
        mine.start()
        for cp in first:
            cp.start()

    def finish(self, ins, outs, scr):
        mine, first, passed, landed, from_sibling = self._copies(ins, outs, scr, True)
        for cp_in, cp_on in zip(landed, passed):
            cp_in.wait_recv()
            cp_on.start()
        for cp in from_sibling:
            cp.wait_recv()
        for cp in first + passed:
            cp.wait_send()
        mine.wait()


class _ReduceScatterSend:
    def __init__(self, grads):
        grads = [g if isinstance(g, tuple) else (g, 0, g.shape[0] // NDEV) for g in grads]
        self.nw = len(grads)
        self.base = [b for _, b, _ in grads]
        self.rows = [r for _, _, r in grads]
        self.ins = [g for g, _, _ in grads]
        self.out_shape = [jax.ShapeDtypeStruct((NDEV, r, g.shape[1]), g.dtype) for g, _, r in grads]
        self.scratch = [DMA((self.nw, NDEV - 1)), DMA((self.nw, NDEV - 1)), DMA((self.nw,))]

    def _copies(self, g_refs, r_refs, scr, want):
        send_sems, recv_sems, local_sems = scr
        x, y, c = lax.axis_index("x"), lax.axis_index("y"), lax.axis_index("c")
        me = 4 * x + 2 * y + c
        rows, base = self.rows, self.base
        out = []
        if want == "local":
            for w in range(self.nw):
                out.append(pltpu.make_async_copy(g_refs[w].at[pl.ds(base[w] + me * rows[w], rows[w])],
                                                 r_refs[w].at[me], local_sems.at[w]))
            return out
        for k in range(1, NDEV):
            px, py, pc = x ^ (k >> 2), y ^ ((k >> 1) & 1), c ^ (k & 1)
            peer = 4 * px + 2 * py + pc
            for w in range(self.nw):
                sems = dict(send_sem=send_sems.at[w, k - 1], recv_sem=recv_sems.at[w, k - 1],
                            device_id=(px, py, pc), device_id_type=MESH)
                if want == "send":
                    out.append(pltpu.make_async_remote_copy(
                        src_ref=g_refs[w].at[pl.ds(base[w] + peer * rows[w], rows[w])], dst_ref=r_refs[w].at[me],
                        **sems))
                else:
                    out.append(pltpu.make_async_remote_copy(
                        src_ref=g_refs[w].at[pl.ds(0, rows[w])], dst_ref=r_refs[w].at[peer], **sems))
        return out

    def start(self, ins, outs, scr):
        for cp in self._copies(ins, outs, scr, "local") + self._copies(ins, outs, scr, "send"):
            cp.start()

    def finish(self, ins, outs, scr):
        for cp in self._copies(ins, outs, scr, "recv"):
            cp.wait_recv()
        for cp in self._copies(ins, outs, scr, "send"):
            cp.wait_send()
        for cp in self._copies(ins, outs, scr, "local"):
            cp.wait()


def _call(body, *, name, grid, ins, in_specs, out_shape, out_specs, scratch=(), vmem=None, carries=()):
    n_in, n_out, n_scr = len(ins), len(out_shape), len(scratch)
    ng = len(grid)

    def split(refs):
        pos = [0]

        def take(k):
            part = refs[pos[0]:pos[0] + k]
            pos[0] += k
            return part

        i_refs = take(n_in)
        c_in = [take(len(c.ins)) for c in carries]
        o_refs = take(n_out)
        c_out = [take(len(c.out_shape)) for c in carries]
        s_refs = take(n_scr)
        c_scr = [take(len(c.scratch)) for c in carries]
        return i_refs, o_refs, s_refs, list(zip(carries, c_in, c_out, c_scr))

    def full(*refs):
        i_refs, o_refs, s_refs, cparts = split(refs)
        if ng == 0:
            for c, a, b, s in cparts:
                c.start(a, b, s)
            for c, a, b, s in cparts:
                c.finish(a, b, s)
            return
        ids = [pl.program_id(a) for a in range(ng)]
        if cparts:
            @pl.when(functools.reduce(operator.and_, [i == 0 for i in ids]))
            def _():
                for c, a, b, s in cparts:
                    c.start(a, b, s)

        body(*i_refs, *o_refs, *s_refs)
        if cparts:
            @pl.when(functools.reduce(operator.and_, [i == g - 1 for i, g in zip(ids, grid)]))
            def _():
                for c, a, b, s in cparts:
                    c.finish(a, b, s)

    all_ins = list(ins) + [a for c in carries for a in c.ins]
    all_in_specs = list(in_specs) + [ANY for c in carries for _ in c.ins]
    all_out_shape = list(out_shape) + [s for c in carries for s in c.out_shape]
    all_out_specs = list(out_specs) + [ANY for c in carries for _ in c.out_shape]
    all_scratch = list(scratch) + [s for c in carries for s in c.scratch]
    kwargs = dict(grid=grid) if ng else {}
    outs = pl.pallas_call(
        full, name=name, in_specs=all_in_specs, out_specs=all_out_specs, out_shape=all_out_shape,
        scratch_shapes=all_scratch,
        compiler_params=pltpu.CompilerParams(dimension_semantics=("arbitrary",) * ng if ng else None,
                                             vmem_limit_bytes=vmem),
        **kwargs)(*all_ins)
    outs = list(outs)
    res, pos = outs[:n_out], n_out
    carried = []
    for c in carries:
        carried.append(outs[pos:pos + len(c.out_shape)])
        pos += len(c.out_shape)
    return res, carried


def _groups_of(names):
    out = []
    for n in names:
        if LOC[n][0] not in out:
            out.append(LOC[n][0])
    return out


def _weight_pieces(name):
    g, off, rows = LOC[name]
    return [(d * GROUP_ROWS[g] + off, d * rows, rows) for d in range(NDEV)]


def _win_pieces():
    kv = N_KV * HEAD_DIM
    pieces = [(0, 0, D)]
    for g0, d0 in ((D, D), (D + kv, D + N_KV * LANES)):
        for g in range(N_KV):
            for half in range(2):
                pieces.append((g0 + g * HEAD_DIM, d0 + g * LANES + half * HEAD_DIM, HEAD_DIM))
    pieces.append((D + 2 * kv, D + 2 * N_KV * LANES, D))
    pieces.append((D + 2 * kv + D, D + 2 * N_KV * LANES + D, D))
    grp, off, rows = LOC["w_in"]
    out = []
    for g0, d0, n in pieces:
        while n > 0:
            dev, loc = divmod(g0, rows)
            m = min(n, rows - loc)
            out.append((dev * GROUP_ROWS[grp] + off + loc, d0, m))
            g0, d0, n = g0 + m, d0 + m, n - m
    return out


def _start_loads(src_ref, dst_ref, pieces, sems, base):
    cps = []
    for j, (s, d, n) in enumerate(pieces):
        cp = pltpu.make_async_copy(src_ref.at[pl.ds(s, n)], dst_ref.at[pl.ds(d, n)], sems.at[base + j])
        cp.start()
        cps.append(cp)
    return cps


def _load_weights(wrefs, targets, sems):
    cps, base = [], 0
    for name, dst in targets:
        pieces = _win_pieces() if name == "w_in" else _weight_pieces(name)
        cps += _start_loads(wrefs[LOC[name][0]], dst, pieces, sems, base)
        base += len(pieces)
    for cp in cps:
        cp.wait()


def _n_pieces(names):
    return sum(len(_win_pieces()) if n == "w_in" else NDEV for n in names)


def _dw(lhs, rhs, chunk, name, carries=()):
    t_tok, c = lhs.shape

    def body(lhs_ref, rhs_ref, out_ref, rhs_s, sem):
        @pl.when(pl.program_id(0) == 0)
        def _():
            cp = pltpu.make_async_copy(rhs_ref, rhs_s, sem)
            cp.start()
            cp.wait()

        out_ref[...] = _tn(lhs_ref[...], rhs_s[...]).astype(BF16)

    (out,), carried = _call(
        body, name=name, grid=(c // chunk,), ins=[lhs, rhs],
        in_specs=[pl.BlockSpec((t_tok, chunk), lambda i: (0, i)), ANY],
        out_specs=[pl.BlockSpec((chunk, D), lambda i: (i, 0))],
        out_shape=[jax.ShapeDtypeStruct((c, D), BF16)],
        scratch=[pltpu.VMEM((t_tok, D), BF16), DMA], vmem=VMEM_BIG, carries=carries)
    return out, carried


def _silu_parts(a):
    sig = jax.nn.sigmoid(a)
    return sig, a * sig


FC = 256


def _ffn_fwd(h, gpre, gpost, wg, names, target=None, tm=512, carries=()):
    t_tok = h.shape[0]
    nt = t_tok // tm
    with_loss = target is not None
    groups = _groups_of(names)

    def body(*refs):
        refs = list(refs)
        h_ref, gpre_ref, gpost_ref = refs[:3]
        del refs[:3]
        tgt_ref = refs.pop(0) if with_loss else None
        wrefs = dict(zip(groups, refs[:len(groups)]))
        del refs[:len(groups)]
        hout_ref, a_ref, b_ref, f_ref, nb_ref = refs[:5]
        del refs[:5]
        if with_loss:
            dy_ref, loss_ref = refs[:2]
            del refs[:2]
        w1_s, w3_s, w2_s, sems = refs
        i = pl.program_id(0)

        @pl.when(i == 0)
        def _():
            _load_weights(wrefs, list(zip(names, (w1_s, w3_s, w2_s))), sems)
            if with_loss:
                loss_ref[...] = jnp.zeros_like(loss_ref)

        x = h_ref[...]
        n, _, _ = _rms_fwd(x, gpre_ref[...])
        nb = n.astype(BF16)
        nb_ref[...] = nb
        f = jnp.zeros((tm, D), F32)
        for c0 in range(0, FF, FC):
            a = _nt(nb, w1_s[pl.ds(c0, FC), :])
            b = _nt(nb, w3_s[pl.ds(c0, FC), :])
            _, sl = _silu_parts(a)
            a_ref[:, pl.ds(c0, FC)] = a.astype(BF16)
            b_ref[:, pl.ds(c0, FC)] = b.astype(BF16)
            f = f + _nn((sl * b).astype(BF16), w2_s[pl.ds(c0, FC), :])
        f_ref[...] = f
        fn, _, _ = _rms_fwd(f, gpost_ref[...])
        y = x + 0.5 * fn
        hout_ref[...] = y
        if with_loss:
            err = y - tgt_ref[...]
            dy_ref[...] = err * (1.0 / D)
            loss_ref[...] += jnp.sum(jnp.sum(err * err, axis=-1, keepdims=True), axis=0, keepdims=True) * (0.5 / D)

    ins = [h, gpre, gpost] + ([target] if with_loss else []) + [wg[g] for g in groups]
    in_specs = [_row_spec(tm, D), _const_spec((1, D)), _const_spec((1, D))]
    in_specs += ([_row_spec(tm, D)] if with_loss else []) + [ANY] * len(groups)
    out_shape = [jax.ShapeDtypeStruct((t_tok, D), F32), jax.ShapeDtypeStruct((t_tok, FF), BF16),
                 jax.ShapeDtypeStruct((t_tok, FF), BF16), jax.ShapeDtypeStruct((t_tok, D), F32),
                 jax.ShapeDtypeStruct((t_tok, D), BF16)]
    out_specs = [_row_spec(tm, D), _row_spec(tm, FF), _row_spec(tm, FF), _row_spec(tm, D), _row_spec(tm, D)]
    if with_loss:
        out_shape += [jax.ShapeDtypeStruct((t_tok, D), F32), jax.ShapeDtypeStruct((1, LANES), F32)]
        out_specs += [_row_spec(tm, D), _const_spec((1, LANES))]
    return _call(body, name="ffn_fwd_" + names[0][:4], grid=(nt,), ins=ins, in_specs=in_specs,
                 out_shape=out_shape, out_specs=out_specs,
                 scratch=[pltpu.VMEM((FF, D), BF16)] * 3 + [DMA((3 * NDEV,))], vmem=VMEM_BIG, carries=carries)


def _ffn_bwd_a(dh, f, a, b, gpost, wg, name_w2, tm=512, carries=()):
    t_tok = dh.shape[0]
    nt = t_tok // tm
    groups = _groups_of([name_w2])

    def body(dh_ref, f_ref, a_ref, b_ref, gpost_ref, wg_ref, dab_ref, s_ref, df_ref, dgp_ref, w2_s, sems):
        i = pl.program_id(0)

        @pl.when(i == 0)
        def _():
            _load_weights({groups[0]: wg_ref}, [(name_w2, w2_s)], sems)
            dgp_ref[...] = jnp.zeros_like(dgp_ref)

        fv = f_ref[...]
        _, fh, r = _rms_fwd(fv, gpost_ref[...])
        df, dg = _rms_bwd(0.5 * dh_ref[...], fh, r, gpost_ref[...])
        dgp_ref[...] += dg
        dfb = df.astype(BF16)
        df_ref[...] = dfb
        for c0 in range(0, FF, FC):
            ds = _nt(dfb, w2_s[pl.ds(c0, FC), :])
            av = a_ref[:, pl.ds(c0, FC)].astype(F32)
            bv = b_ref[:, pl.ds(c0, FC)].astype(F32)
            sig, sl = _silu_parts(av)
            dab_ref[:, pl.ds(c0, FC)] = (ds * bv * (sig * (1.0 + av * (1.0 - sig)))).astype(BF16)
            dab_ref[:, pl.ds(FF + c0, FC)] = (ds * sl).astype(BF16)
            s_ref[:, pl.ds(c0, FC)] = (sl * bv).astype(BF16)

    return _call(
        body, name="ffn_bwd_a_" + name_w2[:4], grid=(nt,), ins=[dh, f, a, b, gpost, wg[groups[0]]],
        in_specs=[_row_spec(tm, D), _row_spec(tm, D), _row_spec(tm, FF), _row_spec(tm, FF), _const_spec((1, D)), ANY],
        out_specs=[_row_spec(tm, 2 * FF), _row_spec(tm, FF), _row_spec(tm, D), _const_spec((1, D))],
        out_shape=[jax.ShapeDtypeStruct((t_tok, 2 * FF), BF16), jax.ShapeDtypeStruct((t_tok, FF), BF16),
                   jax.ShapeDtypeStruct((t_tok, D), BF16), jax.ShapeDtypeStruct((1, D), F32)],
        scratch=[pltpu.VMEM((FF, D), BF16), DMA((NDEV,))],
        vmem=VMEM_BIG, carries=carries)


def _ffn_bwd_dx(dab, h, dh, gpre, wg, name_w1, name_w3, tm=512, carries=()):
    t_tok = dh.shape[0]
    nt = t_tok // tm
    groups = _groups_of([name_w1, name_w3])

    def body(*refs):
        dab_ref, h_ref, dh_ref, gpre_ref = refs[:4]
        wrefs = dict(zip(groups, refs[4:4 + len(groups)]))
        dhin_ref, dgp_ref, w13_s, sems = refs[4 + len(groups):]
        i = pl.program_id(0)

        @pl.when(i == 0)
        def _():
            _load_weights(wrefs, [(name_w1, w13_s.at[pl.ds(0, FF)]), (name_w3, w13_s.at[pl.ds(FF, FF)])], sems)
            dgp_ref[...] = jnp.zeros_like(dgp_ref)

        g = gpre_ref[...]
        _, xh, r = _rms_fwd(h_ref[...], g)
        dn = _nn(dab_ref[...], w13_s[...])
        dx, dg = _rms_bwd(dn, xh, r, g)
        dgp_ref[...] += dg
        dhin_ref[...] = dh_ref[...] + dx

    return _call(
        body, name="ffn_bwd_dx_" + name_w1[:4], grid=(nt,), ins=[dab, h, dh, gpre] + [wg[g] for g in groups],
        in_specs=[_row_spec(tm, 2 * FF), _row_spec(tm, D), _row_spec(tm, D), _const_spec((1, D))]
        + [ANY] * len(groups),
        out_specs=[_row_spec(tm, D), _const_spec((1, D))],
        out_shape=[jax.ShapeDtypeStruct((t_tok, D), F32), jax.ShapeDtypeStruct((1, D), F32)],
        scratch=[pltpu.VMEM((2 * FF, D), BF16), DMA((2 * NDEV,))],
        vmem=VMEM_BIG, carries=carries)


Q0, K0, V0, XR0, XG0 = 0, D, D + N_KV * LANES, D + 2 * N_KV * LANES, 2 * D + 2 * N_KV * LANES


def _mix_proj_fwd(h, g, bgate, wg, tm=512, carries=()):
    t_tok = h.shape[0]
    nt = t_tok // tm
    names = ("w_in", "w_gate")
    groups = _groups_of(names)

    def body(h_ref, g_ref, bg_ref, wg_ref, q_ref, k_ref, v_ref, xr_ref, xg_ref, gs_ref, ub_ref, win_s, wgt_s, sems):
        i = pl.program_id(0)

        @pl.when(i == 0)
        def _():
            _load_weights({groups[0]: wg_ref}, [("w_in", win_s), ("w_gate", wgt_s)], sems)

        n, _, _ = _rms_fwd(h_ref[...], g_ref[...])
        nb = n.astype(BF16)
        ub_ref[...] = nb
        q_ref[...] = _nt(nb, win_s[pl.ds(Q0, D), :]).astype(BF16)
        k_ref[...] = _nt(nb, win_s[pl.ds(K0, N_KV * LANES), :]).astype(BF16)
        v_ref[...] = _nt(nb, win_s[pl.ds(V0, N_KV * LANES), :]).astype(BF16)
        xr_ref[...] = _nt(nb, win_s[pl.ds(XR0, D), :])
        xg_ref[...] = _nt(nb, win_s[pl.ds(XG0, D), :])
        gs_ref[...] = jax.nn.sigmoid(_nt(nb, wgt_s[...]) + bg_ref[...])

    kvw = N_KV * LANES
    return _call(
        body, name="mix_proj_fwd", grid=(nt,), ins=[h, g, bgate, wg[groups[0]]],
        in_specs=[_row_spec(tm, D), _const_spec((1, D)), _const_spec((1, 2 * D)), ANY],
        out_specs=[_row_spec(tm, D), _row_spec(tm, kvw), _row_spec(tm, kvw), _row_spec(tm, D), _row_spec(tm, D),
                   _row_spec(tm, 2 * D), _row_spec(tm, D)],
        out_shape=[jax.ShapeDtypeStruct((t_tok, D), BF16), jax.ShapeDtypeStruct((t_tok, kvw), BF16),
                   jax.ShapeDtypeStruct((t_tok, kvw), BF16), jax.ShapeDtypeStruct((t_tok, D), F32),
                   jax.ShapeDtypeStruct((t_tok, D), F32), jax.ShapeDtypeStruct((t_tok, 2 * D), F32),
                   jax.ShapeDtypeStruct((t_tok, D), BF16)],
        scratch=[pltpu.VMEM((INP_W, D), BF16), pltpu.VMEM((2 * D, D), BF16), DMA((_n_pieces(names),))],
        vmem=VMEM_BIG, carries=carries)


def _mix_proj_bwd(dqkv, dxr, dxg, dgpre, h, dh, g, wg, tm=512, carries=()):
    t_tok = h.shape[0]
    nt = t_tok // tm
    names = ("w_in", "w_gate")
    groups = _groups_of(names)

    def body(dqkv_ref, dxr_ref, dxg_ref, dgp_ref, h_ref, dh_ref, g_ref, wg_ref, dhin_ref, dg_ref, win_s, wgt_s, sems):
        i = pl.program_id(0)

        @pl.when(i == 0)
        def _():
            _load_weights({groups[0]: wg_ref}, [("w_in", win_s), ("w_gate", wgt_s)], sems)
            dg_ref[...] = jnp.zeros_like(dg_ref)

        gv = g_ref[...]
        _, xh, r = _rms_fwd(h_ref[...], gv)
        du = _nn(dgp_ref[...], wgt_s[...])
        du = du + _nn(dqkv_ref[...], win_s[pl.ds(Q0, XR0), :])
        du = du + _nn(dxr_ref[...], win_s[pl.ds(XR0, D), :])
        du = du + _nn(dxg_ref[...], win_s[pl.ds(XG0, D), :])
        dx, dg = _rms_bwd(du, xh, r, gv)
        dg_ref[...] += dg
        dhin_ref[...] = dh_ref[...] + dx

    return _call(
        body, name="mix_proj_bwd", grid=(nt,), ins=[dqkv, dxr, dxg, dgpre, h, dh, g, wg[groups[0]]],
        in_specs=[_row_spec(tm, XR0), _row_spec(tm, D), _row_spec(tm, D), _row_spec(tm, 2 * D), _row_spec(tm, D),
                  _row_spec(tm, D), _const_spec((1, D)), ANY],
        out_specs=[_row_spec(tm, D), _const_spec((1, D))],
        out_shape=[jax.ShapeDtypeStruct((t_tok, D), F32), jax.ShapeDtypeStruct((1, D), F32)],
        scratch=[pltpu.VMEM((INP_W, D), BF16), pltpu.VMEM((2 * D, D), BF16), DMA((_n_pieces(names),))],
        vmem=VMEM_BIG, carries=carries)


def _shift_down(x, d, fill):
    row = lax.broadcasted_iota(jnp.int32, x.shape, 0)
    return jnp.where(row >= d, pltpu.roll(x, d, 0), fill)


def _shift_up(x, d, fill):
    rows = x.shape[0]
    row = lax.broadcasted_iota(jnp.int32, x.shape, 0)
    return jnp.where(row < rows - d, pltpu.roll(x, rows - d, 0), fill)


def _expm1(x):
    small = x * (1.0 + x * (0.5 + x * (1.0 / 6.0 + x * (1.0 / 24.0))))
    return jnp.where(jnp.abs(x) < 0.05, small, jnp.exp(x) - 1.0)


def _softplus(x):
    return jnp.maximum(x, 0.0) + jnp.log(1.0 + jnp.exp(-jnp.abs(x)))


_GELU_C = math.sqrt(2.0 / math.pi)


def _gelu_parts(x):
    th = jnp.tanh(_GELU_C * (x + 0.044715 * x * x * x))
    val = 0.5 * x * (1.0 + th)
    grad = 0.5 * (1.0 + th) + 0.5 * x * (1.0 - th * th) * _GELU_C * (1.0 + 3.0 * 0.044715 * x * x)
    return val, grad


def _lru_pre(x, halo, cw_ref, cb_ref, wa_ref, wx_ref, ba_ref, bx_ref, lam_ref):
    ext = jnp.concatenate([halo, x], axis=0)
    shifted = [x] + [pltpu.roll(ext, k, 0)[8:] for k in (1, 2, 3)]
    xc = cb_ref[...] + cw_ref[pl.ds(CONV_WIDTH - 1, 1), :] * x
    for k in (1, 2, 3):
        xc = xc + cw_ref[pl.ds(CONV_WIDTH - 1 - k, 1), :] * shifted[k]
    xcb = xc.astype(BF16)
    r = jax.nn.sigmoid(_nn(xcb, wa_ref[...]) + ba_ref[...])
    ig = jax.nn.sigmoid(_nn(xcb, wx_ref[...]) + bx_ref[...])
    sp = _softplus(-lam_ref[...])
    log_a = -LRU_C * r * sp
    a = jnp.exp(log_a)
    mult = jnp.sqrt(-_expm1(2.0 * log_a))
    return shifted, xc, xcb, r, ig, sp, a, mult


def _lru_specs(nt, reverse):
    def tt(t):
        return nt - 1 - t if reverse else t
    tile = pl.BlockSpec((LRU_ROWS, LANES), lambda cb, t: (tt(t), cb))
    halo = pl.BlockSpec((8, LANES), lambda cb, t: (jnp.maximum(tt(t) * (LRU_ROWS // 8) - 1, 0), cb))
    vec = pl.BlockSpec((1, LANES), lambda cb, t: (0, cb))
    cw = pl.BlockSpec((CONV_WIDTH, LANES), lambda cb, t: (0, cb))
    mat = pl.BlockSpec((None, LANES, LANES), lambda cb, t: (cb, 0, 0))
    return tile, halo, vec, cw, mat


def _lru_fwd(xr, xg, cw, cb, wa, wx, ba, bx, lam, carries=()):
    t_tok = xr.shape[0]
    nt = t_tok // LRU_ROWS
    rows = LRU_ROWS

    def body(xr_ref, xg_ref, cw_ref, cb_ref, wa_ref, wx_ref, ba_ref, bx_ref, lam_ref, y_ref, h_ref, tail_s, hc_s):
        t = pl.program_id(1)

        @pl.when(t == 0)
        def _():
            tail_s[...] = jnp.zeros_like(tail_s)
            hc_s[...] = jnp.zeros_like(hc_s)

        x = xr_ref[...]
        _, xc, _, _, ig, _, a, mult = _lru_pre(x, tail_s[...], cw_ref, cb_ref, wa_ref, wx_ref, ba_ref, bx_ref, lam_ref)
        tail_s[...] = xr_ref[pl.ds(rows - 8, 8), :]
        acc_a, acc_b = a, mult * (ig * xc)
        d = 1
        while d < rows:
            acc_b = acc_a * _shift_down(acc_b, d, 0.0) + acc_b
            acc_a = acc_a * _shift_down(acc_a, d, 1.0)
            d *= 2
        hv = acc_b + acc_a * hc_s[...]
        h_ref[...] = hv
        hc_s[...] = h_ref[pl.ds(rows - 1, 1), :]
        gl, _ = _gelu_parts(xg_ref[...])
        y_ref[...] = (hv * gl).astype(BF16)

    tile, _, vec, cws, mat = _lru_specs(nt, False)
    return _call(
        body, name="lru_fwd", grid=(D // LANES, nt), ins=[xr, xg, cw, cb, wa, wx, ba, bx, lam],
        in_specs=[tile, tile, cws, vec, mat, mat, vec, vec, vec],
        out_specs=[tile, tile],
        out_shape=[jax.ShapeDtypeStruct((t_tok, D), BF16), jax.ShapeDtypeStruct((t_tok, D), F32)],
        scratch=[pltpu.VMEM((8, LANES), F32), pltpu.VMEM((1, LANES), F32)], carries=carries)


def _lru_bwd(dy, xr, xg, hseq, cw, cb, wa, wx, ba, bx, lam, carries=()):
    t_tok = xr.shape[0]
    nt = t_tok // LRU_ROWS
    rows = LRU_ROWS

    def body(dy_ref, xr_ref, xrh_ref, xg_ref, h_ref, hh_ref, cw_ref, cb_ref, wa_ref, wx_ref, ba_ref, bx_ref, lam_ref,
             dxr_ref, dxg_ref, dvec_ref, dwa_ref, dwx_ref, gcar_s, acar_s, head_s, tmp_s):
        t = pl.program_id(1)
        first_tile = t == nt - 1

        @pl.when(t == 0)
        def _():
            gcar_s[...] = jnp.zeros_like(gcar_s)
            acar_s[...] = jnp.zeros_like(acar_s)
            head_s[...] = jnp.zeros_like(head_s)
            dvec_ref[...] = jnp.zeros_like(dvec_ref)
            dwa_ref[...] = jnp.zeros_like(dwa_ref)
            dwx_ref[...] = jnp.zeros_like(dwx_ref)

        x = xr_ref[...]
        halo = jnp.where(first_tile, 0.0, xrh_ref[...])
        shifted, xc, xcb, r, ig, sp, a, mult = _lru_pre(x, halo, cw_ref, cb_ref, wa_ref, wx_ref, ba_ref, bx_ref, lam_ref)
        hv = h_ref[...]
        dyv = dy_ref[...]
        gl, glg = _gelu_parts(xg_ref[...])
        dxg_ref[...] = (dyv * hv * glg).astype(BF16)
        acc_a = _shift_up(a, 1, acar_s[...])
        acc_b = dyv * gl
        d = 1
        while d < rows:
            acc_b = acc_a * _shift_up(acc_b, d, 0.0) + acc_b
            acc_a = acc_a * _shift_up(acc_a, d, 1.0)
            d *= 2
        g = acc_b + acc_a * gcar_s[...]
        hhalo = jnp.where(first_tile, 0.0, hh_ref[...])
        hprev = pltpu.roll(jnp.concatenate([hhalo, hv], axis=0), 1, 0)[8:]
        dmult = g * ig * xc
        dlog_a = a * (g * hprev) - dmult * a * a / mult
        dig = g * mult * xc
        dxc = g * mult * ig
        dzr = (dlog_a * (-LRU_C * sp)) * r * (1.0 - r)
        dzx = dig * ig * (1.0 - ig)
        dzrb, dzxb = dzr.astype(BF16), dzx.astype(BF16)
        dxc = dxc + _nt(dzrb, wa_ref[...]) + _nt(dzxb, wx_ref[...])
        dwa_ref[...] += _tn(xcb, dzrb)
        dwx_ref[...] += _tn(xcb, dzxb)
        dsp = jnp.sum(dlog_a * (-LRU_C * r), axis=0, keepdims=True)
        dlam = dsp * (-jax.nn.sigmoid(-lam_ref[...]))
        vrow = lax.broadcasted_iota(jnp.int32, (8, LANES), 0)
        upd = jnp.where(vrow == 4, jnp.sum(dxc, axis=0, keepdims=True), 0.0)
        upd = jnp.where(vrow == 5, jnp.sum(dzr, axis=0, keepdims=True), upd)
        upd = jnp.where(vrow == 6, jnp.sum(dzx, axis=0, keepdims=True), upd)
        upd = jnp.where(vrow == 7, dlam, upd)
        for k in range(CONV_WIDTH):
            upd = jnp.where(vrow == CONV_WIDTH - 1 - k, jnp.sum(dxc * shifted[k], axis=0, keepdims=True), upd)
        dvec_ref[...] += upd
        ext = jnp.concatenate([dxc, head_s[...]], axis=0)
        dxr = cw_ref[pl.ds(CONV_WIDTH - 1, 1), :] * dxc
        for k in (1, 2, 3):
            dxr = dxr + cw_ref[pl.ds(CONV_WIDTH - 1 - k, 1), :] * pltpu.roll(ext, rows + 8 - k, 0)[:rows]
        dxr_ref[...] = dxr.astype(BF16)
        tmp_s[...] = g
        gcar_s[...] = tmp_s[pl.ds(0, 1), :]
        tmp_s[...] = a
        acar_s[...] = tmp_s[pl.ds(0, 1), :]
        tmp_s[...] = dxc
        head_s[...] = tmp_s[pl.ds(0, 8), :]

    tile, halo, vec, cws, mat = _lru_specs(nt, True)
    return _call(
        body, name="lru_bwd", grid=(D // LANES, nt), ins=[dy, xr, xr, xg, hseq, hseq, cw, cb, wa, wx, ba, bx, lam],
        in_specs=[tile, tile, halo, tile, tile, halo, cws, vec, mat, mat, vec, vec, vec],
        out_specs=[tile, tile, pl.BlockSpec((8, LANES), lambda cb, t: (0, cb)), mat, mat],
        out_shape=[jax.ShapeDtypeStruct((t_tok, D), BF16), jax.ShapeDtypeStruct((t_tok, D), BF16),
                   jax.ShapeDtypeStruct((8, D), F32), jax.ShapeDtypeStruct((D // LANES, LANES, LANES), F32),
                   jax.ShapeDtypeStruct((D // LANES, LANES, LANES), F32)],
        scratch=[pltpu.VMEM((1, LANES), F32), pltpu.VMEM((1, LANES), F32), pltpu.VMEM((8, LANES), F32),
                 pltpu.VMEM((rows, LANES), F32)], carries=carries)


def _t5_bucket_np(rel):
    nb = N_BUCKETS // 2
    max_exact = nb // 2
    ret = np.where(rel > 0, nb, 0)
    n = np.abs(rel)
    nf = np.maximum(n, 1).astype(np.float32)
    large = max_exact + (np.log(nf / np.float32(max_exact)) / np.float32(math.log(MAX_DISTANCE / max_exact))
                         * np.float32(nb - max_exact)).astype(np.int32)
    large = np.minimum(large, nb - 1)
    return ret + np.where(n < max_exact, n, large)


def _bucket_map():
    r = np.arange(QT)[:, None]
    c = np.arange(KW)[None, :]
    j = c - (r // CHUNK) * CHUNK
    band = (j >= 0) & (j < WINDOW + CHUNK)
    return np.where(band, _t5_bucket_np(c - r - WINDOW), -1).astype(np.int32)


def _attn_specs(nt, reverse):
    def tt(i):
        return nt - 1 - i if reverse else i
    kvw = N_KV * LANES
    qs = pl.BlockSpec((QT, D), lambda i: (tt(i), 0))
    cur = pl.BlockSpec((QT, kvw), lambda i: (tt(i), 0))
    prev = pl.BlockSpec((WINDOW, kvw), lambda i: (jnp.maximum(tt(i) * (QT // WINDOW) - 1, 0), 0))
    lse = pl.BlockSpec((QT, LANES), lambda i: (tt(i), 0))
    return qs, cur, prev, lse


def _attn_fwd(q, kd, vd, bias, sinks, carries=()):
    t_tok = q.shape[0]
    nt = t_tok // QT

    def body(q_ref, kp_ref, kc_ref, vp_ref, vc_ref, bias_ref, sink_ref, o_ref, lse_ref):
        i = pl.program_id(0)
        col = lax.broadcasted_iota(jnp.int32, (1, KW), 1)
        first = jnp.where((i == 0) & (col < WINDOW), NEG_INF, 0.0)
        lane = lax.broadcasted_iota(jnp.int32, (QT, LANES), 1)
        lse_t = jnp.zeros((QT, LANES), F32)
        for g in range(N_KV):
            kwin = jnp.concatenate([kp_ref[:, pl.ds(g * LANES, LANES)], kc_ref[:, pl.ds(g * LANES, LANES)]], axis=0)
            vwin = jnp.concatenate([vp_ref[:, pl.ds(g * LANES, LANES)], vc_ref[:, pl.ds(g * LANES, LANES)]], axis=0)
            for slab in (2 * g, 2 * g + 1):
                qs = q_ref[:, pl.ds(slab * LANES, LANES)]
                o_slab = jnp.zeros((QT, LANES), F32)
                for half in range(2):
                    hd = 2 * slab + half
                    mine = (lane >= half * HEAD_DIM) & (lane < (half + 1) * HEAD_DIM)
                    qh = jnp.where(mine, qs, jnp.zeros_like(qs)) * jnp.asarray(HEAD_DIM ** -0.5, BF16)
                    s = _nt(qh, kwin) + (bias_ref[hd] + first)
                    sk = sink_ref[hd]
                    m = jnp.maximum(jnp.max(s, axis=-1, keepdims=True), sk)
                    e = jnp.exp(s - m)
                    l = jnp.sum(e, axis=-1, keepdims=True) + jnp.exp(sk - m)
                    p = e / l
                    o_slab = jnp.where(mine, _nn(p.astype(BF16), vwin), o_slab)
                    lse_t = jnp.where(lane == hd, m + jnp.log(l), lse_t)
                o_ref[:, pl.ds(slab * LANES, LANES)] = o_slab.astype(BF16)
        lse_ref[...] = lse_t

    qs, cur, prev, lse = _attn_specs(nt, False)
    return _call(
        body, name="attn_fwd", grid=(nt,), ins=[q, kd, kd, vd, vd, bias, sinks],
        in_specs=[qs, prev, cur, prev, cur, _const_spec((N_HEADS, QT, KW)), pl.BlockSpec(memory_space=pltpu.SMEM)],
        out_specs=[qs, lse],
        out_shape=[jax.ShapeDtypeStruct((t_tok, D), BF16), jax.ShapeDtypeStruct((t_tok, LANES), F32)],
        vmem=48 * 2 ** 20, carries=carries)


def _attn_bwd(q, kd, vd, o, do, lse, bias, sinks, carries=()):
    t_tok = q.shape[0]
    nt = t_tok // QT
    kvw = N_KV * LANES

    def body(q_ref, kp_ref, kc_ref, vp_ref, vc_ref, o_ref, do_ref, lse_ref, bias_ref, sink_ref,
             dqkv_ref, ds_ref, dsink_ref, kcar_s, vcar_s):
        i = pl.program_id(0)
        tile = nt - 1 - i

        @pl.when(i == 0)
        def _():
            kcar_s[...] = jnp.zeros_like(kcar_s)
            vcar_s[...] = jnp.zeros_like(vcar_s)
            ds_ref[...] = jnp.zeros_like(ds_ref)
            dsink_ref[...] = jnp.zeros_like(dsink_ref)

        col = lax.broadcasted_iota(jnp.int32, (1, KW), 1)
        first = jnp.where((tile == 0) & (col < WINDOW), NEG_INF, 0.0)
        lane = lax.broadcasted_iota(jnp.int32, (QT, LANES), 1)
        lane_k = lax.broadcasted_iota(jnp.int32, (KW, LANES), 1)
        lane_1 = lax.broadcasted_iota(jnp.int32, (1, LANES), 1)
        lse_t = lse_ref[...]
        dsink = jnp.zeros((1, LANES), F32)
        for g in range(N_KV):
            kwin = jnp.concatenate([kp_ref[:, pl.ds(g * LANES, LANES)], kc_ref[:, pl.ds(g * LANES, LANES)]], axis=0)
            vwin = jnp.concatenate([vp_ref[:, pl.ds(g * LANES, LANES)], vc_ref[:, pl.ds(g * LANES, LANES)]], axis=0)
            dk_acc = jnp.zeros((KW, LANES), F32)
            dv_acc = jnp.zeros((KW, LANES), F32)
            for slab in (2 * g, 2 * g + 1):
                qs = q_ref[:, pl.ds(slab * LANES, LANES)]
                dos = do_ref[:, pl.ds(slab * LANES, LANES)]
                od = dos.astype(F32) * o_ref[:, pl.ds(slab * LANES, LANES)].astype(F32)
                dq_slab = jnp.zeros((QT, LANES), F32)
                for half in range(2):
                    hd = 2 * slab + half
                    mine = (lane >= half * HEAD_DIM) & (lane < (half + 1) * HEAD_DIM)
                    qh = jnp.where(mine, qs, jnp.zeros_like(qs)) * jnp.asarray(HEAD_DIM ** -0.5, BF16)
                    doh = jnp.where(mine, dos, jnp.zeros_like(dos))
                    s = _nt(qh, kwin) + (bias_ref[hd] + first)
                    lse_h = jnp.sum(jnp.where(lane == hd, lse_t, 0.0), axis=-1, keepdims=True)
                    p = jnp.exp(s - lse_h)
                    dp = _nt(doh, vwin)
                    drow = jnp.sum(jnp.where(mine, od, 0.0), axis=-1, keepdims=True)
                    ds = p * (dp - drow)
                    ds_ref[hd] += ds
                    psink = jnp.exp(sink_ref[hd] - lse_h)
                    dsink = dsink + jnp.where(lane_1 == hd, -jnp.sum(psink * drow, axis=0, keepdims=True), 0.0)
                    dsb = ds.astype(BF16)
                    dq_slab = jnp.where(mine, _nn(dsb, kwin) * (HEAD_DIM ** -0.5), dq_slab)
                    dk_acc = dk_acc + _tn(dsb, qh)
                    dv_acc = dv_acc + _tn(p.astype(BF16), doh)
                dqkv_ref[:, pl.ds(slab * LANES, LANES)] = dq_slab.astype(BF16)
            dk_f = jnp.where(lane_k < HEAD_DIM, dk_acc + pltpu.roll(dk_acc, HEAD_DIM, 1), 0.0)
            dv_f = jnp.where(lane_k < HEAD_DIM, dv_acc + pltpu.roll(dv_acc, HEAD_DIM, 1), 0.0)
            for acc, col0, car in ((dk_f, K0, kcar_s), (dv_f, V0, vcar_s)):
                cs = pl.ds(g * LANES, LANES)
                co = pl.ds(col0 + g * LANES, LANES)
                dqkv_ref[pl.ds(0, QT - WINDOW), co] = acc[WINDOW:QT].astype(BF16)
                dqkv_ref[pl.ds(QT - WINDOW, WINDOW), co] = (acc[QT:KW] + car[:, cs]).astype(BF16)
                car[:, cs] = acc[0:WINDOW]
        dsink_ref[...] += dsink

    qs, cur, prev, lse_s = _attn_specs(nt, True)
    return _call(
        body, name="attn_bwd", grid=(nt,), ins=[q, kd, kd, vd, vd, o, do, lse, bias, sinks],
        in_specs=[qs, prev, cur, prev, cur, qs, qs, lse_s, _const_spec((N_HEADS, QT, KW)),
                  pl.BlockSpec(memory_space=pltpu.SMEM)],
        out_specs=[pl.BlockSpec((QT, XR0), lambda i: (nt - 1 - i, 0)), _const_spec((N_HEADS, QT, KW)),
                   _const_spec((1, LANES))],
        out_shape=[jax.ShapeDtypeStruct((t_tok, XR0), BF16), jax.ShapeDtypeStruct((N_HEADS, QT, KW), F32),
                   jax.ShapeDtypeStruct((1, LANES), F32)],
        scratch=[pltpu.VMEM((WINDOW, kvw), F32), pltpu.VMEM((WINDOW, kvw), F32)],
        vmem=VMEM_BIG, carries=carries)


def _bias_tile(table, bmap):
    def body(tab_ref, bm_ref, out_ref):
        bm = bm_ref[...]

        def per_head(hd, carry):
            acc = jnp.full((QT, KW), NEG_INF, F32)
            for b in range(N_BUCKETS):
                acc = jnp.where(bm == b, tab_ref[b, hd], acc)
            out_ref[hd] = acc
            return carry

        lax.fori_loop(0, N_HEADS, per_head, 0)

    return pl.pallas_call(
        body, name="bias_tile", out_shape=jax.ShapeDtypeStruct((N_HEADS, QT, KW), F32),
        in_specs=[pl.BlockSpec(memory_space=pltpu.SMEM), pl.BlockSpec(memory_space=pltpu.VMEM)],
        out_specs=pl.BlockSpec(memory_space=pltpu.VMEM))(table, bmap)


def _bias_grad(ds_acc, bmap):
    def body(ds_ref, bm_ref, out_ref):
        row = lax.broadcasted_iota(jnp.int32, (N_BUCKETS, LANES), 0)
        lane = lax.broadcasted_iota(jnp.int32, (N_BUCKETS, LANES), 1)
        bm = bm_ref[...]

        def per_head(hd, res):
            dsv = ds_ref[hd]
            for b in range(N_BUCKETS):
                val = jnp.sum(jnp.sum(jnp.where(bm == b, dsv, 0.0), axis=0, keepdims=True), axis=1, keepdims=True)
                res = jnp.where((row == b) & (lane == hd), val, res)
            return res

        out_ref[...] = lax.fori_loop(0, N_HEADS, per_head, jnp.zeros((N_BUCKETS, LANES), F32))

    return pl.pallas_call(
        body, name="bias_grad", out_shape=jax.ShapeDtypeStruct((N_BUCKETS, LANES), F32),
        in_specs=[pl.BlockSpec(memory_space=pltpu.VMEM), pl.BlockSpec(memory_space=pltpu.VMEM)],
        out_specs=pl.BlockSpec(memory_space=pltpu.VMEM))(ds_acc, bmap)


MIXOUT = ("w_lru_out", "w_attn_out", "w_o")


def _mix_out_fwd(ya_in, o, gs, h, g, wg, tm=256, carries=()):
    t_tok = h.shape[0]
    nt = t_tok // tm
    groups = _groups_of(MIXOUT)

    def body(ya_ref, o_ref, gs_ref, h_ref, g_ref, wg_ref, hout_ref, yao_ref, ybo_ref, z_ref, wa_s, wb_s, wo_s, sems):
        i = pl.program_id(0)

        @pl.when(i == 0)
        def _():
            _load_weights({groups[0]: wg_ref}, list(zip(MIXOUT, (wa_s, wb_s, wo_s))), sems)

        ya = _nn(ya_ref[...], wa_s[...])
        yb = _nn(o_ref[...], wb_s[...])
        yao_ref[...] = ya
        ybo_ref[...] = yb
        merged = gs_ref[:, pl.ds(0, D)] * ya + gs_ref[:, pl.ds(D, D)] * yb
        z = _nn(merged.astype(BF16), wo_s[...])
        z_ref[...] = z
        zn, _, _ = _rms_fwd(z, g_ref[...])
        hout_ref[...] = h_ref[...] + zn

    return _call(
        body, name="mix_out_fwd", grid=(nt,), ins=[ya_in, o, gs, h, g, wg[groups[0]]],
        in_specs=[_row_spec(tm, D), _row_spec(tm, D), _row_spec(tm, 2 * D), _row_spec(tm, D), _const_spec((1, D)), ANY],
        out_specs=[_row_spec(tm, D)] * 4,
        out_shape=[jax.ShapeDtypeStruct((t_tok, D), F32)] * 4,
        scratch=[pltpu.VMEM((D, D), BF16)] * 3 + [DMA((3 * NDEV,))],
        vmem=48 * 2 ** 20, carries=carries)


def _mix_out_bwd(dh, z, ya, yb, gs, g, wg, tm=512, carries=()):
    t_tok = dh.shape[0]
    nt = t_tok // tm
    groups = _groups_of(MIXOUT)

    def body(dh_ref, z_ref, ya_ref, yb_ref, gs_ref, g_ref, wg_ref,
             dyain_ref, do_ref, dgpre_ref, dya_ref, dyb_ref, mg_ref, dz_ref, dg_ref, dbg_ref,
             wa_s, wb_s, wo_s, sems):
        i = pl.program_id(0)

        @pl.when(i == 0)
        def _():
            _load_weights({groups[0]: wg_ref}, list(zip(MIXOUT, (wa_s, wb_s, wo_s))), sems)
            dg_ref[...] = jnp.zeros_like(dg_ref)
            dbg_ref[...] = jnp.zeros_like(dbg_ref)

        gv = g_ref[...]
        _, zh, r = _rms_fwd(z_ref[...], gv)
        dz, dg = _rms_bwd(dh_ref[...], zh, r, gv)
        dg_ref[...] += dg
        dzb = dz.astype(BF16)
        dz_ref[...] = dzb
        ga, gb = gs_ref[:, pl.ds(0, D)], gs_ref[:, pl.ds(D, D)]
        ya_v, yb_v = ya_ref[...], yb_ref[...]
        mg_ref[...] = (ga * ya_v + gb * yb_v).astype(BF16)
        dm = _nt(dzb, wo_s[...])
        dga = dm * ya_v * ga * (1.0 - ga)
        dgb = dm * yb_v * gb * (1.0 - gb)
        dgpre_ref[:, pl.ds(0, D)] = dga.astype(BF16)
        dgpre_ref[:, pl.ds(D, D)] = dgb.astype(BF16)
        dbg_ref[:, pl.ds(0, D)] += jnp.sum(dga, axis=0, keepdims=True)
        dbg_ref[:, pl.ds(D, D)] += jnp.sum(dgb, axis=0, keepdims=True)
        dya = (dm * ga).astype(BF16)
        dyb = (dm * gb).astype(BF16)
        dya_ref[...] = dya
        dyb_ref[...] = dyb
        dyain_ref[...] = _nt(dya, wa_s[...])
        do_ref[...] = _nt(dyb, wb_s[...]).astype(BF16)

    bf = jax.ShapeDtypeStruct((t_tok, D), BF16)
    return _call(
        body, name="mix_out_bwd", grid=(nt,), ins=[dh, z, ya, yb, gs, g, wg[groups[0]]],
        in_specs=[_row_spec(tm, D)] * 4 + [_row_spec(tm, 2 * D), _const_spec((1, D)), ANY],
        out_specs=[_row_spec(tm, D), _row_spec(tm, D), _row_spec(tm, 2 * D)] + [_row_spec(tm, D)] * 4
        + [_const_spec((1, D)), _const_spec((1, 2 * D))],
        out_shape=[jax.ShapeDtypeStruct((t_tok, D), F32), bf, jax.ShapeDtypeStruct((t_tok, 2 * D), BF16), bf, bf, bf, bf,
                   jax.ShapeDtypeStruct((1, D), F32), jax.ShapeDtypeStruct((1, 2 * D), F32)],
        scratch=[pltpu.VMEM((D, D), BF16)] * 3 + [DMA((3 * NDEV,))],
        vmem=VMEM_BIG, carries=carries)


def _sum_parts(parts_list):
    n = len(parts_list)
    _, r, c = parts_list[0].shape
    tc = 256

    def body(*refs):
        for p_ref, o_ref in zip(refs[:n], refs[n:]):
            acc = p_ref[0].astype(F32)
            for s in range(1, NDEV):
                acc = acc + p_ref[s].astype(F32)
            o_ref[...] = acc

    return pl.pallas_call(
        body, name=f"sum_parts_{r}", grid=(c // tc,),
        in_specs=[pl.BlockSpec((NDEV, r, tc), lambda i: (0, 0, i))] * n,
        out_specs=[pl.BlockSpec((r, tc), lambda i: (0, i))] * n,
        out_shape=[jax.ShapeDtypeStruct((r, c), F32)] * n,
        compiler_params=pltpu.CompilerParams(dimension_semantics=("arbitrary",), vmem_limit_bytes=48 * 2 ** 20),
    )(*parts_list)


def _adamw_math(w, g, m, v):
    m = ADAM_B1 * m + (1.0 - ADAM_B1) * g
    v = ADAM_B2 * v + (1.0 - ADAM_B2) * (g * g)
    m_hat = m / (1.0 - ADAM_B1 ** ADAM_STEP)
    v_hat = v / (1.0 - ADAM_B2 ** ADAM_STEP)
    delta = -ADAM_LR * (m_hat / (jnp.sqrt(v_hat) + ADAM_EPS) + ADAM_WD * w)
    return delta, m, v


def _adamw_body(n):
    def body(*refs):
        for k in range(n):
            g_ref, w_ref, m_ref, v_ref = refs[4 * k:4 * k + 4]
            d_ref, nm_ref, nv_ref = refs[4 * n + 3 * k:4 * n + 3 * k + 3]
            d, m, v = _adamw_math(w_ref[...], g_ref[...], m_ref[...], v_ref[...])
            d_ref[...] = d
            nm_ref[...] = m
            nv_ref[...] = v
    return body


def _adamw(items):
    n = len(items)
    r, c = items[0][0].shape
    tr = r if r * c <= 2 ** 18 else max(t for t in range(8, 65, 8) if r % t == 0)
    spec = pl.BlockSpec((tr, c), lambda i: (i, 0))
    outs = pl.pallas_call(
        _adamw_body(n), name=f"adamw_{r}x{c}", grid=(r // tr,),
        in_specs=[spec] * (4 * n), out_specs=[spec] * (3 * n),
        out_shape=[jax.ShapeDtypeStruct((r, c), F32)] * (3 * n),
        compiler_params=pltpu.CompilerParams(dimension_semantics=("arbitrary",), vmem_limit_bytes=40 * 2 ** 20),
    )(*[a for it in items for a in it])
    return [tuple(outs[3 * k:3 * k + 3]) for k in range(n)]


def _adamw_small(items):
    n = len(items)
    vm = pl.BlockSpec(memory_space=pltpu.VMEM)
    outs = pl.pallas_call(
        _adamw_body(n), name="adamw_small", in_specs=[vm] * (4 * n), out_specs=[vm] * (3 * n),
        out_shape=[jax.ShapeDtypeStruct(it[1].shape, F32) for it in items for _ in range(3)],
    )(*[a for it in items for a in it])
    return [tuple(outs[3 * k:3 * k + 3]) for k in range(n)]


def _pack_small(arrs):
    rows, offs = [], []
    total = 0
    for a in arrs:
        flat = a.reshape(-1).astype(F32)
        nr = -(-flat.shape[0] // LANES)
        flat = jnp.pad(flat, (0, nr * LANES - flat.shape[0]))
        rows.append(flat.reshape(nr, LANES))
        offs.append((total, nr))
        total += nr
    pad = -total % 8
    if pad:
        rows.append(jnp.zeros((pad, LANES), F32))
    return jnp.concatenate(rows, axis=0), offs


def _unpack_small(pack, offs, shapes):
    out = []
    for (o, nr), shp in zip(offs, shapes):
        size = int(np.prod(shp))
        out.append(pack[o:o + nr].reshape(-1)[:size].reshape(shp))
    return out


def _sum_small(gathered, rows):
    def body(p_ref, o_ref):
        acc = p_ref[pl.ds(0, rows), :]
        for s in range(1, NDEV):
            acc = acc + p_ref[pl.ds(s * rows, rows), :]
        o_ref[...] = acc

    return pl.pallas_call(
        body, name="sum_small", out_shape=jax.ShapeDtypeStruct((rows, LANES), F32),
        in_specs=[pl.BlockSpec(memory_space=pltpu.VMEM)], out_specs=pl.BlockSpec(memory_space=pltpu.VMEM))(gathered)


def _block_diag(w):
    w = w.reshape(D // LANES, 2, LRU_BLOCK, LRU_BLOCK)
    z = jnp.zeros((D // LANES, LRU_BLOCK, LRU_BLOCK), w.dtype)
    top = jnp.concatenate([w[:, 0], z], axis=2)
    bot = jnp.concatenate([z, w[:, 1]], axis=2)
    return jnp.concatenate([top, bot], axis=1)


def _block_diag_grad(dw):
    a = dw[:, :LRU_BLOCK, :LRU_BLOCK]
    b = dw[:, LRU_BLOCK:, LRU_BLOCK:]
    return jnp.stack([a, b], axis=1).reshape(D // LRU_BLOCK, LRU_BLOCK, LRU_BLOCK)


def kernel(x, ffn1_pre_g, ffn1_w1, ffn1_w3, ffn1_w2, ffn1_post_g, mix_pre_g, w_in, conv_w, conv_b, rg_a_w, rg_a_b, rg_x_w, rg_x_b, lru_lambda, w_lru_out, attn_sinks, rel_bias, w_attn_out, w_gate, b_gate, w_o, mix_post_g, ffn2_pre_g, ffn2_w1, ffn2_w3, ffn2_w2, ffn2_post_g, loss_target, m_ffn1_pre_g, m_ffn1_w1, m_ffn1_w3, m_ffn1_w2, m_ffn1_post_g, m_mix_pre_g, m_w_in, m_conv_w, m_conv_b, m_rg_a_w, m_rg_a_b, m_rg_x_w, m_rg_x_b, m_lru_lambda, m_w_lru_out, m_attn_sinks, m_rel_bias, m_w_attn_out, m_w_gate, m_b_gate, m_w_o, m_mix_post_g, m_ffn2_pre_g, m_ffn2_w1, m_ffn2_w3, m_ffn2_w2, m_ffn2_post_g, v_ffn1_pre_g, v_ffn1_w1, v_ffn1_w3, v_ffn1_w2, v_ffn1_post_g, v_mix_pre_g, v_w_in, v_conv_w, v_conv_b, v_rg_a_w, v_rg_a_b, v_rg_x_w, v_rg_x_b, v_lru_lambda, v_w_lru_out, v_attn_sinks, v_rel_bias, v_w_attn_out, v_w_gate, v_b_gate, v_w_o, v_mix_post_g, v_ffn2_pre_g, v_ffn2_w1, v_ffn2_w3, v_ffn2_w2, v_ffn2_post_g):
    names = ["ffn1_pre_g", "ffn1_w1", "ffn1_w3", "ffn1_w2", "ffn1_post_g", "mix_pre_g", "w_in", "conv_w", "conv_b",
             "rg_a_w", "rg_a_b", "rg_x_w", "rg_x_b", "lru_lambda", "w_lru_out", "attn_sinks", "rel_bias", "w_attn_out",
             "w_gate", "b_gate", "w_o", "mix_post_g", "ffn2_pre_g", "ffn2_w1", "ffn2_w3", "ffn2_w2", "ffn2_post_g"]
    ws = dict(zip(names, (ffn1_pre_g, ffn1_w1, ffn1_w3, ffn1_w2, ffn1_post_g, mix_pre_g, w_in, conv_w, conv_b, rg_a_w,
                          rg_a_b, rg_x_w, rg_x_b, lru_lambda, w_lru_out, attn_sinks, rel_bias, w_attn_out, w_gate,
                          b_gate, w_o, mix_post_g, ffn2_pre_g, ffn2_w1, ffn2_w3, ffn2_w2, ffn2_post_g)))
    ms = dict(zip(names, (m_ffn1_pre_g, m_ffn1_w1, m_ffn1_w3, m_ffn1_w2, m_ffn1_post_g, m_mix_pre_g, m_w_in, m_conv_w,
                          m_conv_b, m_rg_a_w, m_rg_a_b, m_rg_x_w, m_rg_x_b, m_lru_lambda, m_w_lru_out, m_attn_sinks,
                          m_rel_bias, m_w_attn_out, m_w_gate, m_b_gate, m_w_o, m_mix_post_g, m_ffn2_pre_g, m_ffn2_w1,
                          m_ffn2_w3, m_ffn2_w2, m_ffn2_post_g)))
    vs = dict(zip(names, (v_ffn1_pre_g, v_ffn1_w1, v_ffn1_w3, v_ffn1_w2, v_ffn1_post_g, v_mix_pre_g, v_w_in, v_conv_w,
                          v_conv_b, v_rg_a_w, v_rg_a_b, v_rg_x_w, v_rg_x_b, v_lru_lambda, v_w_lru_out, v_attn_sinks,
                          v_rel_bias, v_w_attn_out, v_w_gate, v_b_gate, v_w_o, v_mix_post_g, v_ffn2_pre_g, v_ffn2_w1,
                          v_ffn2_w3, v_ffn2_w2, v_ffn2_post_g)))
    me = 4 * lax.axis_index("x") + 2 * lax.axis_index("y") + lax.axis_index("c")
    vec = lambda n: ws[n].reshape(1, -1)

    def shard2d(name):
        if name == "conv":
            row = lax.bitcast_convert_type(conv_w.reshape(CONV_WIDTH, LANES), BF16).reshape(1, D)
            return jnp.concatenate([row, jnp.zeros((LOC["conv"][2] - 1, D), BF16)], axis=0)
        a = ws[name].reshape(ws[name].shape[-2], ws[name].shape[-1])
        return (a.T if name in COL_SHARDED else a).astype(BF16)

    packs = {g: jnp.concatenate([shard2d(n) for n, _ in members], axis=0) for g, members in GROUPS}

    wg = {}
    _, ((wg["ffn1"],),) = _call(None, name="allgather_ffn1", grid=(), ins=[], in_specs=[], out_shape=[], out_specs=[],
                                carries=[_AllGather(packs["ffn1"])])
    conv_rows = wg["ffn1"].reshape(NDEV, GROUP_ROWS["ffn1"], D)[:, LOC["conv"][1]]
    cw = jnp.transpose(lax.bitcast_convert_type(conv_rows.reshape(NDEV, CONV_WIDTH, LANES, 2), F32),
                       (1, 0, 2)).reshape(CONV_WIDTH, D)
    bmap = jnp.asarray(_bucket_map())
    bias = _bias_tile(rel_bias, bmap)
    sinks = attn_sinks.reshape(N_HEADS)
    wa_bd = _block_diag(rg_a_w.reshape(D // LRU_BLOCK, LRU_BLOCK, LRU_BLOCK)).astype(BF16)
    wx_bd = _block_diag(rg_x_w.reshape(D // LRU_BLOCK, LRU_BLOCK, LRU_BLOCK)).astype(BF16)
    lru_args = (cw, vec("conv_b"), wa_bd, wx_bd, vec("rg_a_b"), vec("rg_x_b"), vec("lru_lambda"))
    x2, tgt = x[0], loss_target[0]

    (h1, a1, b1, f1, nb1), ((wg["mixin"],),) = _ffn_fwd(
        x2, vec("ffn1_pre_g"), vec("ffn1_post_g"), wg, ("ffn1_w1", "ffn1_w3", "ffn1_w2"),
        carries=[_AllGather(packs["mixin"])])
    (q, kd, vd, xr, xg, gs, ub), ((wg["mixout"],),) = _mix_proj_fwd(
        h1, vec("mix_pre_g"), vec("b_gate"), wg, carries=[_AllGather(packs["mixout"])])
    (ya_in, hseq), ((wg["ffn2a"],),) = _lru_fwd(xr, xg, *lru_args, carries=[_AllGather(packs["ffn2a"])])
    (o, lse), ((wg["ffn2b"],),) = _attn_fwd(q, kd, vd, bias, sinks, carries=[_AllGather(packs["ffn2b"])])
    (h2, ya, yb, z), _ = _mix_out_fwd(ya_in, o, gs, h1, vec("mix_post_g"), wg)
    (_, a2, b2, f2, nb2, dy, loss_part), _ = _ffn_fwd(h2, vec("ffn2_pre_g"), vec("ffn2_post_g"), wg,
                                                     ("ffn2_w1", "ffn2_w3", "ffn2_w2"), target=tgt)
    loss = lax.psum(loss_part[0, 0], ("x", "y", "c"))

    gsm, parts = {}, {}
    rs = lambda *grads_: [_ReduceScatterSend(list(grads_))]
    ffr = FF // NDEV
    (dab, s_act, dfb, gsm["ffn2_post_g"]), _ = _ffn_bwd_a(dy, f2, a2, b2, vec("ffn2_post_g"), wg, "ffn2_w2")
    g_w2, _ = _dw(s_act, dfb, FF // 2, "dw_ffn2_w2")
    g_w13, ((parts["ffn2_w2"],),) = _dw(dab, nb2, FF // 2, "dw_ffn2_w13", carries=rs(g_w2))
    (dh2, gsm["ffn2_pre_g"]), ((parts["ffn2_w1"],),) = _ffn_bwd_dx(
        dab, h2, dy, vec("ffn2_pre_g"), wg, "ffn2_w1", "ffn2_w3", carries=rs((g_w13, 0, ffr)))
    (dya_in, do, dgpre, dya, dyb, mg, dzb, gsm["mix_post_g"], gsm["b_gate"]), ((parts["ffn2_w3"],),) = _mix_out_bwd(
        dh2, z, ya, yb, gs, vec("mix_post_g"), wg, carries=rs((g_w13, FF, ffr)))
    g_wa, _ = _dw(ya_in, dya, D // 2, "dw_w_lru_out")
    g_wb, _ = _dw(o, dyb, D // 2, "dw_w_attn_out")
    g_wo, _ = _dw(mg, dzb, D // 2, "dw_w_o")
    (dqkv, ds_acc, dsink), (mixout_parts,) = _attn_bwd(
        q, kd, vd, o, do, lse, bias, sinks, carries=rs(g_wa, g_wb, g_wo))
    parts["w_lru_out"], parts["w_attn_out"], parts["w_o"] = mixout_parts
    (dxr, dxg, dvec, dwa, dwx), _ = _lru_bwd(dya_in, xr, xg, hseq, *lru_args)
    g_qkv, _ = _dw(dqkv, ub, D, "dw_w_in_qkv")
    g_xr, _ = _dw(dxr, ub, D // 2, "dw_w_in_xr")
    g_xg, _ = _dw(dxg, ub, D // 2, "dw_w_in_xg")
    g_wgate, _ = _dw(dgpre, ub, D, "dw_w_gate")
    g_win = jnp.concatenate(
        [g_qkv[:D]] + [g_qkv[c0 + gi * LANES:c0 + gi * LANES + HEAD_DIM] for c0 in (K0, V0) for gi in range(N_KV)]
        + [g_xr, g_xg], axis=0)
    (dh1, gsm["mix_pre_g"]), ((parts["w_in"],),) = _mix_proj_bwd(
        dqkv, dxr, dxg, dgpre, h1, dh2, vec("mix_pre_g"), wg, carries=rs(g_win))
    gsm["conv_w"] = dvec[0:CONV_WIDTH]
    gsm["conv_b"], gsm["rg_a_b"], gsm["rg_x_b"], gsm["lru_lambda"] = dvec[4], dvec[5], dvec[6], dvec[7]
    gsm["rg_a_w"] = _block_diag_grad(dwa)
    gsm["rg_x_w"] = _block_diag_grad(dwx)
    gsm["attn_sinks"] = dsink[0, :N_HEADS]
    gsm["rel_bias"] = _bias_grad(ds_acc, bmap)[:, :N_HEADS]
    late = ("ffn1_post_g", "ffn1_pre_g")
    early = tuple(n for n in SMALL if n not in late)
    early_pack, early_offs = _pack_small([gsm[n] for n in early])
    (dab, s_act, dfb, gsm["ffn1_post_g"]), ((parts["w_gate"],), (early_all,)) = _ffn_bwd_a(
        dh1, f1, a1, b1, vec("ffn1_post_g"), wg, "ffn1_w2", carries=rs(g_wgate) + [_AllGather(early_pack)])
    g_w2, _ = _dw(s_act, dfb, FF // 2, "dw_ffn1_w2")
    g_w13, ((parts["ffn1_w2"],),) = _dw(dab, nb1, FF // 2, "dw_ffn1_w13", carries=rs(g_w2))
    (grad_x, gsm["ffn1_pre_g"]), ((parts["ffn1_w1"],),) = _ffn_bwd_dx(
        dab, x2, dh1, vec("ffn1_pre_g"), wg, "ffn1_w1", "ffn1_w3", carries=rs((g_w13, 0, ffr)))
    late_pack, late_offs = _pack_small([gsm[n] for n in late])
    _, ((parts["ffn1_w3"],), (late_all,)) = _call(
        None, name="final_exchange", grid=(), ins=[], in_specs=[], out_shape=[], out_specs=[],
        carries=rs((g_w13, FF, ffr)) + [_AllGather(late_pack)])

    grads = {}
    by_rows = {}
    for n in BIG:
        by_rows.setdefault(LOC[n][2], []).append(n)
    for group in by_rows.values():
        for n, g in zip(group, _sum_parts([parts[n] for n in group])):
            grads[n] = g
    for group, gathered, offs in ((early, early_all, early_offs), (late, late_all, late_offs)):
        total = _sum_small(gathered, gathered.shape[0] // NDEV)
        shapes = [(CONV_WIDTH, D) if n == "conv_w" else ws[n].shape for n in group]
        for n, g in zip(group, _unpack_small(total, offs, shapes)):
            grads[n] = g
    grads["conv_w"] = lax.dynamic_slice(grads["conv_w"], (0, me * LANES), (CONV_WIDTH, LANES)).reshape(conv_w.shape)

    delta, new_m, new_v = {}, {}, {}
    t_form = [n for n in COL_SHARDED if ws[n].shape[-1] % LANES]
    view = {n: (lambda a: a.reshape(a.shape[-2], a.shape[-1]).T) if n in t_form
            else (lambda a: a.reshape(a.shape[-2], a.shape[-1])) for n in BIG}
    unview = {n: (lambda a: a.T) if n in t_form else (lambda a: a) for n in BIG}
    for n in COL_SHARDED:
        if n not in t_form:
            grads[n] = grads[n].T
    by_shape = {}
    for n in BIG:
        by_shape.setdefault(view[n](ws[n]).shape, []).append(n)
    for group in by_shape.values():
        res = _adamw([(grads[n], view[n](ws[n]), view[n](ms[n]), view[n](vs[n])) for n in group])
        for n, (d_, m_, v_) in zip(group, res):
            delta[n], new_m[n], new_v[n] = unview[n](d_), unview[n](m_), unview[n](v_)
            grads[n] = unview[n](grads[n])
    flat2d = lambda a: a.reshape(-1, a.shape[-1])
    res = _adamw_small([(flat2d(grads[n].reshape(ws[n].shape)), flat2d(ws[n]), flat2d(ms[n]), flat2d(vs[n]))
                        for n in SMALL])
    for n, (d_, m_, v_) in zip(SMALL, res):
        delta[n], new_m[n], new_v[n] = d_, m_, v_

    outs = [loss, grad_x.reshape(x.shape)]
    for src in (grads, delta, new_m, new_v):
        outs += [src[n].reshape(ws[n].shape) for n in names]
    return tuple(outs)
```

```python
import functools
import math
import operator

import numpy as np
import jax
import jax.numpy as jnp
from jax import lax
from jax.experimental import pallas as pl
from jax.experimental.pallas import tpu as pltpu

F32, BF16 = jnp.float32, jnp.bfloat16

NDEV = 8
D = 1024
FF = 2816
N_HEADS, N_KV, HEAD_DIM = 16, 4, 64
CHUNK, WINDOW = 64, 128
N_BUCKETS, MAX_DISTANCE = 32, 128
LRU_BLOCK = 64
CONV_WIDTH = 4
LRU_C = 8.0
RMS_EPS = 1e-6
NEG_INF = -1e30
LANES = 128
QT = 128
KW = QT + WINDOW
LRU_ROWS = 512
IN_W = D + 2 * N_KV * HEAD_DIM + 2 * D
INP_W = D + 2 * N_KV * LANES + 2 * D
VMEM_BIG = 58 * 2 ** 20

ADAM_LR, ADAM_B1, ADAM_B2, ADAM_EPS, ADAM_WD, ADAM_STEP = 0.001, 0.9, 0.999, 1e-08, 0.01, 10

GROUPS = (("ffn1", (("ffn1_w1", FF // NDEV), ("ffn1_w3", FF // NDEV), ("ffn1_w2", FF // NDEV), ("conv", 16))),
          ("mixin", (("w_in", IN_W // NDEV), ("w_gate", 2 * D // NDEV))),
          ("mixout", (("w_lru_out", D // NDEV), ("w_attn_out", D // NDEV), ("w_o", D // NDEV))),
          ("ffn2a", (("ffn2_w1", FF // NDEV), ("ffn2_w3", FF // NDEV))),
          ("ffn2b", (("ffn2_w2", FF // NDEV),)))
LOC, GROUP_ROWS = {}, {}
for _g, _members in GROUPS:
    _o = 0
    for _n, _r in _members:
        LOC[_n] = (_g, _o, _r)
        _o += _r
    GROUP_ROWS[_g] = _o
BIG = tuple(n for _, members in GROUPS for n, _ in members if n != "conv")
COL_SHARDED = ("ffn1_w1", "ffn1_w3", "w_in", "w_gate", "ffn2_w1", "ffn2_w3")

SMALL = ("ffn1_pre_g", "ffn1_post_g", "mix_pre_g", "conv_w", "conv_b", "rg_a_w", "rg_a_b", "rg_x_w", "rg_x_b",
         "lru_lambda", "attn_sinks", "rel_bias", "b_gate", "mix_post_g", "ffn2_pre_g", "ffn2_post_g")

MESH = pl.DeviceIdType.MESH
ANY = pl.BlockSpec(memory_space=pl.ANY)
DMA = pltpu.SemaphoreType.DMA


def _nn(a, b):
    return lax.dot_general(a, b, (((1,), (0,)), ((), ())), preferred_element_type=F32)


def _nt(a, b):
    return lax.dot_general(a, b, (((1,), (1,)), ((), ())), preferred_element_type=F32)


def _tn(a, b):
    return lax.dot_general(a, b, (((0,), (0,)), ((), ())), preferred_element_type=F32)


def _rms_fwd(x, g):
    r = lax.rsqrt(jnp.mean(x * x, axis=-1, keepdims=True) + RMS_EPS)
    xh = x * r
    return xh * g, xh, r


def _rms_bwd(dn, xh, r, g):
    dxh = dn * g
    dx = r * (dxh - xh * jnp.mean(dxh * xh, axis=-1, keepdims=True))
    return dx, jnp.sum(dn * xh, axis=0, keepdims=True)


def _row_spec(tm, c):
    return pl.BlockSpec((tm, c), lambda i: (i, 0))


def _const_spec(shape):
    nd = len(shape)
    return pl.BlockSpec(shape, lambda i: (0,) * nd)


class _AllGather:
    def __init__(self, shard):
        self.m, n = shard.shape
        self.ins = [shard]
        self.out_shape = [jax.ShapeDtypeStruct((NDEV * self.m, n), shard.dtype)]
        self.scratch = [DMA((7,)), DMA((7,)), DMA]

    def _copies(self, ins, outs, scr, all_of_them):
        x_ref, out_ref = ins[0], outs[0]
        send_sems, recv_sems, local_sem = scr
        x, y, c = lax.axis_index("x"), lax.axis_index("y"), lax.axis_index("c")
        me, sibling = (x, y, c), (x, y, 1 - c)
        chips = [(1 - x, y), (x, 1 - y), (1 - x, 1 - y)]
        m = self.m

        def rows(px, py, pc):
            return out_ref.at[pl.ds((4 * px + 2 * py + pc) * m, m), :]

        def copy(k, block, to, src=None):
            return pltpu.make_async_remote_copy(
                src_ref=rows(*block) if src is None else src, dst_ref=rows(*block),
                send_sem=send_sems.at[k], recv_sem=recv_sems.at[k], device_id=to, device_id_type=MESH)

        mine = pltpu.make_async_copy(x_ref, rows(*me), local_sem)
        first = [copy(0, me, sibling, src=x_ref)] + [copy(1 + j, me, (*chip, c), src=x_ref)
                                                     for j, chip in enumerate(chips)]
        if not all_of_them:
            return mine, first
        passed = [copy(4 + j, (*chip, c), sibling) for j, chip in enumerate(chips)]
        landed = [copy(1 + j, (*chip, c), me) for j, chip in enumerate(chips)]
        from_sibling = [copy(0, sibling, me)] + [copy(4 + j, (*chip, 1 - c), me) for j, chip in enumerate(chips)]
        return mine, first, passed, landed, from_sibling

    def start(self, ins, outs, scr):
        mine, first = self._copies(ins, outs, scr, False)
        mine.start()
        for cp in first:
            cp.start()

    def finish(self, ins, outs, scr):
        mine, first, passed, landed, from_sibling = self._copies(ins, outs, scr, True)
        for cp_in, cp_on in zip(landed, passed):
            cp_in.wait_recv()
            cp_on.start()
        for cp in from_sibling:
            cp.wait_recv()
        for cp in first + passed:
            cp.wait_send()
        mine.wait()


class _ReduceScatterSend:
    def __init__(self, grads):
        grads = [g if isinstance(g, tuple) else (g, 0, g.shape[0] // NDEV) for g in grads]
        self.nw = len(grads)
        self.base = [b for _, b, _ in grads]
        self.rows = [r for _, _, r in grads]
        self.ins = [g for g, _, _ in grads]
        self.out_shape = [jax.ShapeDtypeStruct((NDEV, r, g.shape[1]), g.dtype) for g, _, r in grads]
        self.scratch = [DMA((self.nw, NDEV - 1)), DMA((self.nw, NDEV - 1)), DMA((self.nw,))]

    def _copies(self, g_refs, r_refs, scr, want):
        send_sems, recv_sems, local_sems = scr
        x, y, c = lax.axis_index("x"), lax.axis_index("y"), lax.axis_index("c")
        me = 4 * x + 2 * y + c
        rows, base = self.rows, self.base
        out = []
        if want == "local":
            for w in range(self.nw):
                out.append(pltpu.make_async_copy(g_refs[w].at[pl.ds(base[w] + me * rows[w], rows[w])],
                                                 r_refs[w].at[me], local_sems.at[w]))
            return out
        for k in range(1, NDEV):
            px, py, pc = x ^ (k >> 2), y ^ ((k >> 1) & 1), c ^ (k & 1)
            peer = 4 * px + 2 * py + pc
            for w in range(self.nw):
                sems = dict(send_sem=send_sems.at[w, k - 1], recv_sem=recv_sems.at[w, k - 1],
                            device_id=(px, py, pc), device_id_type=MESH)
                if want == "send":
                    out.append(pltpu.make_async_remote_copy(
                        src_ref=g_refs[w].at[pl.ds(base[w] + peer * rows[w], rows[w])], dst_ref=r_refs[w].at[me],
                        **sems))
                else:
                    out.append(pltpu.make_async_remote_copy(
                        src_ref=g_refs[w].at[pl.ds(0, rows[w])], dst_ref=r_refs[w].at[peer], **sems))
        return out

    def start(self, ins, outs, scr):
        for cp in self._copies(ins, outs, scr, "local") + self._copies(ins, outs, scr, "send"):
            cp.start()

    def finish(self, ins, outs, scr):
        for cp in self._copies(ins, outs, scr, "recv"):
            cp.wait_recv()
        for cp in self._copies(ins, outs, scr, "send"):
            cp.wait_send()
        for cp in self._copies(ins, outs, scr, "local"):
            cp.wait()


def _call(body, *, name, grid, ins, in_specs, out_shape, out_specs, scratch=(), vmem=None, carries=()):
    n_in, n_out, n_scr = len(ins), len(out_shape), len(scratch)
    ng = len(grid)

    def split(refs):
        pos = [0]

        def take(k):
            part = refs[pos[0]:pos[0] + k]
            pos[0] += k
            return part

        i_refs = take(n_in)
        c_in = [take(len(c.ins)) for c in carries]
        o_refs = take(n_out)
        c_out = [take(len(c.out_shape)) for c in carries]
        s_refs = take(n_scr)
        c_scr = [take(len(c.scratch)) for c in carries]
        return i_refs, o_refs, s_refs, list(zip(carries, c_in, c_out, c_scr))

    def full(*refs):
        i_refs, o_refs, s_refs, cparts = split(refs)
        if ng == 0:
            for c, a, b, s in cparts:
                c.start(a, b, s)
            for c, a, b, s in cparts:
                c.finish(a, b, s)
            return
        ids = [pl.program_id(a) for a in range(ng)]
        if cparts:
            @pl.when(functools.reduce(operator.and_, [i == 0 for i in ids]))
            def _():
                for c, a, b, s in cparts:
                    c.start(a, b, s)

        body(*i_refs, *o_refs, *s_refs)
        if cparts:
            @pl.when(functools.reduce(operator.and_, [i == g - 1 for i, g in zip(ids, grid)]))
            def _():
                for c, a, b, s in cparts:
                    c.finish(a, b, s)

    all_ins = list(ins) + [a for c in carries for a in c.ins]
    all_in_specs = list(in_specs) + [ANY for c in carries for _ in c.ins]
    all_out_shape = list(out_shape) + [s for c in carries for s in c.out_shape]
    all_out_specs = list(out_specs) + [ANY for c in carries for _ in c.out_shape]
    all_scratch = list(scratch) + [s for c in carries for s in c.scratch]
    kwargs = dict(grid=grid) if ng else {}
    outs = pl.pallas_call(
        full, name=name, in_specs=all_in_specs, out_specs=all_out_specs, out_shape=all_out_shape,
        scratch_shapes=all_scratch,
        compiler_params=pltpu.CompilerParams(dimension_semantics=("arbitrary",) * ng if ng else None,
                                             vmem_limit_bytes=vmem),
        **kwargs)(*all_ins)
    outs = list(outs)
    res, pos = outs[:n_out], n_out
    carried = []
    for c in carries:
        carried.append(outs[pos:pos + len(c.out_shape)])
        pos += len(c.out_shape)
    return res, carried


def _groups_of(names):
    out = []
    for n in names:
        if LOC[n][0] not in out:
            out.append(LOC[n][0])
    return out


def _weight_pieces(name):
    g, off, rows = LOC[name]
    return [(d * GROUP_ROWS[g] + off, d * rows, rows) for d in range(NDEV)]


def _win_pieces():
    kv = N_KV * HEAD_DIM
    pieces = [(0, 0, D)]
    for g0, d0 in ((D, D), (D + kv, D + N_KV * LANES)):
        for g in range(N_KV):
            for half in range(2):
                pieces.append((g0 + g * HEAD_DIM, d0 + g * LANES + half * HEAD_DIM, HEAD_DIM))
    pieces.append((D + 2 * kv, D + 2 * N_KV * LANES, D))
    pieces.append((D + 2 * kv + D, D + 2 * N_KV * LANES + D, D))
    grp, off, rows = LOC["w_in"]
    out = []
    for g0, d0, n in pieces:
        while n > 0:
            dev, loc = divmod(g0, rows)
            m = min(n, rows - loc)
            out.append((dev * GROUP_ROWS[grp] + off + loc, d0, m))
            g0, d0, n = g0 + m, d0 + m, n - m
    return out


def _start_loads(src_ref, dst_ref, pieces, sems, base):
    cps = []
    for j, (s, d, n) in enumerate(pieces):
        cp = pltpu.make_async_copy(src_ref.at[pl.ds(s, n)], dst_ref.at[pl.ds(d, n)], sems.at[base + j])
        cp.start()
        cps.append(cp)
    return cps


def _load_weights(wrefs, targets, sems):
    cps, base = [], 0
    for name, dst in targets:
        pieces = _win_pieces() if name == "w_in" else _weight_pieces(name)
        cps += _start_loads(wrefs[LOC[name][0]], dst, pieces, sems, base)
        base += len(pieces)
    for cp in cps:
        cp.wait()


def _n_pieces(names):
    return sum(len(_win_pieces()) if n == "w_in" else NDEV for n in names)


def _dw(lhs, rhs, chunk, name, carries=()):
    t_tok, c = lhs.shape

    def body(lhs_ref, rhs_ref, out_ref, rhs_s, sem):
        @pl.when(pl.program_id(0) == 0)
        def _():
            cp = pltpu.make_async_copy(rhs_ref, rhs_s, sem)
            cp.start()
            cp.wait()

        out_ref[...] = _tn(lhs_ref[...], rhs_s[...]).astype(BF16)

    (out,), carried = _call(
        body, name=name, grid=(c // chunk,), ins=[lhs, rhs],
        in_specs=[pl.BlockSpec((t_tok, chunk), lambda i: (0, i)), ANY],
        out_specs=[pl.BlockSpec((chunk, D), lambda i: (i, 0))],
        out_shape=[jax.ShapeDtypeStruct((c, D), BF16)],
        scratch=[pltpu.VMEM((t_tok, D), BF16), DMA], vmem=VMEM_BIG, carries=carries)
    return out, carried


def _silu_parts(a):
    sig = jax.nn.sigmoid(a)
    return sig, a * sig


FC = 256


def _ffn_fwd(h, gpre, gpost, wg, names, target=None, tm=512, carries=()):
    t_tok = h.shape[0]
    nt = t_tok // tm
    with_loss = target is not None
    groups = _groups_of(names)

    def body(*refs):
        refs = list(refs)
        h_ref, gpre_ref, gpost_ref = refs[:3]
        del refs[:3]
        tgt_ref = refs.pop(0) if with_loss else None
        wrefs = dict(zip(groups, refs[:len(groups)]))
        del refs[:len(groups)]
        hout_ref, a_ref, b_ref, f_ref, nb_ref = refs[:5]
        del refs[:5]
        if with_loss:
            dy_ref, loss_ref = refs[:2]
            del refs[:2]
        w1_s, w3_s, w2_s, sems = refs
        i = pl.program_id(0)

        @pl.when(i == 0)
        def _():
            _load_weights(wrefs, list(zip(names, (w1_s, w3_s, w2_s))), sems)
            if with_loss:
                loss_ref[...] = jnp.zeros_like(loss_ref)

        x = h_ref[...]
        n, _, _ = _rms_fwd(x, gpre_ref[...])
        nb = n.astype(BF16)
        nb_ref[...] = nb
        f = jnp.zeros((tm, D), F32)
        for c0 in range(0, FF, FC):
            a = _nt(nb, w1_s[pl.ds(c0, FC), :])
            b = _nt(nb, w3_s[pl.ds(c0, FC), :])
            _, sl = _silu_parts(a)
            a_ref[:, pl.ds(c0, FC)] = a.astype(BF16)
            b_ref[:, pl.ds(c0, FC)] = b.astype(BF16)
            f = f + _nn((sl * b).astype(BF16), w2_s[pl.ds(c0, FC), :])
        f_ref[...] = f
        fn, _, _ = _rms_fwd(f, gpost_ref[...])
        y = x + 0.5 * fn
        hout_ref[...] = y
        if with_loss:
            err = y - tgt_ref[...]
            dy_ref[...] = err * (1.0 / D)
            loss_ref[...] += jnp.sum(jnp.sum(err * err, axis=-1, keepdims=True), axis=0, keepdims=True) * (0.5 / D)

    ins = [h, gpre, gpost] + ([target] if with_loss else []) + [wg[g] for g in groups]
    in_specs = [_row_spec(tm, D), _const_spec((1, D)), _const_spec((1, D))]
    in_specs += ([_row_spec(tm, D)] if with_loss else []) + [ANY] * len(groups)
    out_shape = [jax.ShapeDtypeStruct((t_tok, D), F32), jax.ShapeDtypeStruct((t_tok, FF), BF16),
                 jax.ShapeDtypeStruct((t_tok, FF), BF16), jax.ShapeDtypeStruct((t_tok, D), F32),
                 jax.ShapeDtypeStruct((t_tok, D), BF16)]
    out_specs = [_row_spec(tm, D), _row_spec(tm, FF), _row_spec(tm, FF), _row_spec(tm, D), _row_spec(tm, D)]
    if with_loss:
        out_shape += [jax.ShapeDtypeStruct((t_tok, D), F32), jax.ShapeDtypeStruct((1, LANES), F32)]
        out_specs += [_row_spec(tm, D), _const_spec((1, LANES))]
    return _call(body, name="ffn_fwd_" + names[0][:4], grid=(nt,), ins=ins, in_specs=in_specs,
                 out_shape=out_shape, out_specs=out_specs,
                 scratch=[pltpu.VMEM((FF, D), BF16)] * 3 + [DMA((3 * NDEV,))], vmem=VMEM_BIG, carries=carries)


def _ffn_bwd_a(dh, f, a, b, gpost, wg, name_w2, tm=512, carries=()):
    t_tok = dh.shape[0]
    nt = t_tok // tm
    groups = _groups_of([name_w2])

    def body(dh_ref, f_ref, a_ref, b_ref, gpost_ref, wg_ref, dab_ref, s_ref, df_ref, dgp_ref, w2_s, sems):
        i = pl.program_id(0)

        @pl.when(i == 0)
        def _():
            _load_weights({groups[0]: wg_ref}, [(name_w2, w2_s)], sems)
            dgp_ref[...] = jnp.zeros_like(dgp_ref)

        fv = f_ref[...]
        _, fh, r = _rms_fwd(fv, gpost_ref[...])
        df, dg = _rms_bwd(0.5 * dh_ref[...], fh, r, gpost_ref[...])
        dgp_ref[...] += dg
        dfb = df.astype(BF16)
        df_ref[...] = dfb
        for c0 in range(0, FF, FC):
            ds = _nt(dfb, w2_s[pl.ds(c0, FC), :])
            av = a_ref[:, pl.ds(c0, FC)].astype(F32)
            bv = b_ref[:, pl.ds(c0, FC)].astype(F32)
            sig, sl = _silu_parts(av)
            dab_ref[:, pl.ds(c0, FC)] = (ds * bv * (sig * (1.0 + av * (1.0 - sig)))).astype(BF16)
            dab_ref[:, pl.ds(FF + c0, FC)] = (ds * sl).astype(BF16)
            s_ref[:, pl.ds(c0, FC)] = (sl * bv).astype(BF16)

    return _call(
        body, name="ffn_bwd_a_" + name_w2[:4], grid=(nt,), ins=[dh, f, a, b, gpost, wg[groups[0]]],
        in_specs=[_row_spec(tm, D), _row_spec(tm, D), _row_spec(tm, FF), _row_spec(tm, FF), _const_spec((1, D)), ANY],
        out_specs=[_row_spec(tm, 2 * FF), _row_spec(tm, FF), _row_spec(tm, D), _const_spec((1, D))],
        out_shape=[jax.ShapeDtypeStruct((t_tok, 2 * FF), BF16), jax.ShapeDtypeStruct((t_tok, FF), BF16),
                   jax.ShapeDtypeStruct((t_tok, D), BF16), jax.ShapeDtypeStruct((1, D), F32)],
        scratch=[pltpu.VMEM((FF, D), BF16), DMA((NDEV,))],
        vmem=VMEM_BIG, carries=carries)


def _ffn_bwd_dx(dab, h, dh, gpre, wg, name_w1, name_w3, tm=512, carries=()):
    t_tok = dh.shape[0]
    nt = t_tok // tm
    groups = _groups_of([name_w1, name_w3])

    def body(*refs):
        dab_ref, h_ref, dh_ref, gpre_ref = refs[:4]
        wrefs = dict(zip(groups, refs[4:4 + len(groups)]))
        dhin_ref, dgp_ref, w13_s, sems = refs[4 + len(groups):]
        i = pl.program_id(0)

        @pl.when(i == 0)
        def _():
            _load_weights(wrefs, [(name_w1, w13_s.at[pl.ds(0, FF)]), (name_w3, w13_s.at[pl.ds(FF, FF)])], sems)
            dgp_ref[...] = jnp.zeros_like(dgp_ref)

        g = gpre_ref[...]
        _, xh, r = _rms_fwd(h_ref[...], g)
        dn = _nn(dab_ref[...], w13_s[...])
        dx, dg = _rms_bwd(dn, xh, r, g)
        dgp_ref[...] += dg
        dhin_ref[...] = dh_ref[...] + dx

    return _call(
        body, name="ffn_bwd_dx_" + name_w1[:4], grid=(nt,), ins=[dab, h, dh, gpre] + [wg[g] for g in groups],
        in_specs=[_row_spec(tm, 2 * FF), _row_spec(tm, D), _row_spec(tm, D), _const_spec((1, D))]
        + [ANY] * len(groups),
        out_specs=[_row_spec(tm, D), _const_spec((1, D))],
        out_shape=[jax.ShapeDtypeStruct((t_tok, D), F32), jax.ShapeDtypeStruct((1, D), F32)],
        scratch=[pltpu.VMEM((2 * FF, D), BF16), DMA((2 * NDEV,))],
        vmem=VMEM_BIG, carries=carries)


Q0, K0, V0, XR0, XG0 = 0, D, D + N_KV * LANES, D + 2 * N_KV * LANES, 2 * D + 2 * N_KV * LANES


def _mix_proj_fwd(h, g, bgate, wg, tm=512, carries=()):
    t_tok = h.shape[0]
    nt = t_tok // tm
    names = ("w_in", "w_gate")
    groups = _groups_of(names)

    def body(h_ref, g_ref, bg_ref, wg_ref, q_ref, k_ref, v_ref, xr_ref, xg_ref, gs_ref, ub_ref, win_s, wgt_s, sems):
        i = pl.program_id(0)

        @pl.when(i == 0)
        def _():
            _load_weights({groups[0]: wg_ref}, [("w_in", win_s), ("w_gate", wgt_s)], sems)

        n, _, _ = _rms_fwd(h_ref[...], g_ref[...])
        nb = n.astype(BF16)
        ub_ref[...] = nb
        q_ref[...] = _nt(nb, win_s[pl.ds(Q0, D), :]).astype(BF16)
        k_ref[...] = _nt(nb, win_s[pl.ds(K0, N_KV * LANES), :]).astype(BF16)
        v_ref[...] = _nt(nb, win_s[pl.ds(V0, N_KV * LANES), :]).astype(BF16)
        xr_ref[...] = _nt(nb, win_s[pl.ds(XR0, D), :])
        xg_ref[...] = _nt(nb, win_s[pl.ds(XG0, D), :])
        gs_ref[...] = jax.nn.sigmoid(_nt(nb, wgt_s[...]) + bg_ref[...])

    kvw = N_KV * LANES
    return _call(
        body, name="mix_proj_fwd", grid=(nt,), ins=[h, g, bgate, wg[groups[0]]],
        in_specs=[_row_spec(tm, D), _const_spec((1, D)), _const_spec((1, 2 * D)), ANY],
        out_specs=[_row_spec(tm, D), _row_spec(tm, kvw), _row_spec(tm, kvw), _row_spec(tm, D), _row_spec(tm, D),
                   _row_spec(tm, 2 * D), _row_spec(tm, D)],
        out_shape=[jax.ShapeDtypeStruct((t_tok, D), BF16), jax.ShapeDtypeStruct((t_tok, kvw), BF16),
                   jax.ShapeDtypeStruct((t_tok, kvw), BF16), jax.ShapeDtypeStruct((t_tok, D), F32),
                   jax.ShapeDtypeStruct((t_tok, D), F32), jax.ShapeDtypeStruct((t_tok, 2 * D), F32),
                   jax.ShapeDtypeStruct((t_tok, D), BF16)],
        scratch=[pltpu.VMEM((INP_W, D), BF16), pltpu.VMEM((2 * D, D), BF16), DMA((_n_pieces(names),))],
        vmem=VMEM_BIG, carries=carries)


def _mix_proj_bwd(dqkv, dxr, dxg, dgpre, h, dh, g, wg, tm=512, carries=()):
    t_tok = h.shape[0]
    nt = t_tok // tm
    names = ("w_in", "w_gate")
    groups = _groups_of(names)

    def body(dqkv_ref, dxr_ref, dxg_ref, dgp_ref, h_ref, dh_ref, g_ref, wg_ref, dhin_ref, dg_ref, win_s, wgt_s, sems):
        i = pl.program_id(0)

        @pl.when(i == 0)
        def _():
            _load_weights({groups[0]: wg_ref}, [("w_in", win_s), ("w_gate", wgt_s)], sems)
            dg_ref[...] = jnp.zeros_like(dg_ref)

        gv = g_ref[...]
        _, xh, r = _rms_fwd(h_ref[...], gv)
        du = _nn(dgp_ref[...], wgt_s[...])
        du = du + _nn(dqkv_ref[...], win_s[pl.ds(Q0, XR0), :])
        du = du + _nn(dxr_ref[...], win_s[pl.ds(XR0, D), :])
        du = du + _nn(dxg_ref[...], win_s[pl.ds(XG0, D), :])
        dx, dg = _rms_bwd(du, xh, r, gv)
        dg_ref[...] += dg
        dhin_ref[...] = dh_ref[...] + dx

    return _call(
        body, name="mix_proj_bwd", grid=(nt,), ins=[dqkv, dxr, dxg, dgpre, h, dh, g, wg[groups[0]]],
        in_specs=[_row_spec(tm, XR0), _row_spec(tm, D), _row_spec(tm, D), _row_spec(tm, 2 * D), _row_spec(tm, D),
                  _row_spec(tm, D), _const_spec((1, D)), ANY],
        out_specs=[_row_spec(tm, D), _const_spec((1, D))],
        out_shape=[jax.ShapeDtypeStruct((t_tok, D), F32), jax.ShapeDtypeStruct((1, D), F32)],
        scratch=[pltpu.VMEM((INP_W, D), BF16), pltpu.VMEM((2 * D, D), BF16), DMA((_n_pieces(names),))],
        vmem=VMEM_BIG, carries=carries)


def _shift_down(x, d, fill):
    row = lax.broadcasted_iota(jnp.int32, x.shape, 0)
    return jnp.where(row >= d, pltpu.roll(x, d, 0), fill)


def _shift_up(x, d, fill):
    rows = x.shape[0]
    row = lax.broadcasted_iota(jnp.int32, x.shape, 0)
    return jnp.where(row < rows - d, pltpu.roll(x, rows - d, 0), fill)


def _scan_rows(a, b, reverse):
    rows = a.shape[0]
    d = 1
    while d < rows:
        if d < 8:
            shift = _shift_up if reverse else _shift_down
            b = a * shift(b, d, 0.0) + b
            a = a * shift(a, d, 1.0)
        elif reverse:
            b = jnp.concatenate([a[:rows - d] * b[d:] + b[:rows - d], b[rows - d:]], axis=0)
            a = jnp.concatenate([a[:rows - d] * a[d:], a[rows - d:]], axis=0)
        else:
            b = jnp.concatenate([b[:d], a[d:] * b[:rows - d] + b[d:]], axis=0)
            a = jnp.concatenate([a[:d], a[d:] * a[:rows - d]], axis=0)
        d *= 2
    return a, b


def _expm1(x):
    small = x * (1.0 + x * (0.5 + x * (1.0 / 6.0 + x * (1.0 / 24.0))))
    return jnp.where(jnp.abs(x) < 0.05, small, jnp.exp(x) - 1.0)


def _softplus(x):
    return jnp.maximum(x, 0.0) + jnp.log(1.0 + jnp.exp(-jnp.abs(x)))


_GELU_C = math.sqrt(2.0 / math.pi)


def _gelu_parts(x):
    th = jnp.tanh(_GELU_C * (x + 0.044715 * x * x * x))
    val = 0.5 * x * (1.0 + th)
    grad = 0.5 * (1.0 + th) + 0.5 * x * (1.0 - th * th) * _GELU_C * (1.0 + 3.0 * 0.044715 * x * x)
    return val, grad


def _lru_pre(x, halo, cw_ref, cb_ref, wa_ref, wx_ref, ba_ref, bx_ref, lam_ref):
    ext = jnp.concatenate([halo, x], axis=0)
    shifted = [x] + [pltpu.roll(ext, k, 0)[8:] for k in (1, 2, 3)]
    xc = cb_ref[...] + cw_ref[pl.ds(CONV_WIDTH - 1, 1), :] * x
    for k in (1, 2, 3):
        xc = xc + cw_ref[pl.ds(CONV_WIDTH - 1 - k, 1), :] * shifted[k]
    xcb = xc.astype(BF16)
    r = jax.nn.sigmoid(_nn(xcb, wa_ref[...]) + ba_ref[...])
    ig = jax.nn.sigmoid(_nn(xcb, wx_ref[...]) + bx_ref[...])
    sp = _softplus(-lam_ref[...])
    log_a = -LRU_C * r * sp
    a = jnp.exp(log_a)
    mult = jnp.sqrt(-_expm1(2.0 * log_a))
    return shifted, xc, xcb, r, ig, sp, a, mult


def _lru_specs(nt, reverse):
    def tt(t):
        return nt - 1 - t if reverse else t
    tile = pl.BlockSpec((LRU_ROWS, LANES), lambda cb, t: (tt(t), cb))
    halo = pl.BlockSpec((8, LANES), lambda cb, t: (jnp.maximum(tt(t) * (LRU_ROWS // 8) - 1, 0), cb))
    vec = pl.BlockSpec((1, LANES), lambda cb, t: (0, cb))
    cw = pl.BlockSpec((CONV_WIDTH, LANES), lambda cb, t: (0, cb))
    mat = pl.BlockSpec((None, LANES, LANES), lambda cb, t: (cb, 0, 0))
    return tile, halo, vec, cw, mat


def _lru_fwd(xr, xg, cw, cb, wa, wx, ba, bx, lam, carries=()):
    t_tok = xr.shape[0]
    nt = t_tok // LRU_ROWS
    rows = LRU_ROWS

    def body(xr_ref, xg_ref, cw_ref, cb_ref, wa_ref, wx_ref, ba_ref, bx_ref, lam_ref, y_ref, h_ref, tail_s, hc_s):
        t = pl.program_id(1)

        @pl.when(t == 0)
        def _():
            tail_s[...] = jnp.zeros_like(tail_s)
            hc_s[...] = jnp.zeros_like(hc_s)

        x = xr_ref[...]
        _, xc, _, _, ig, _, a, mult = _lru_pre(x, tail_s[...], cw_ref, cb_ref, wa_ref, wx_ref, ba_ref, bx_ref, lam_ref)
        tail_s[...] = xr_ref[pl.ds(rows - 8, 8), :]
        acc_a, acc_b = _scan_rows(a, mult * (ig * xc), False)
        hv = acc_b + acc_a * hc_s[...]
        h_ref[...] = hv
        hc_s[...] = h_ref[pl.ds(rows - 1, 1), :]
        gl, _ = _gelu_parts(xg_ref[...])
        y_ref[...] = (hv * gl).astype(BF16)

    tile, _, vec, cws, mat = _lru_specs(nt, False)
    return _call(
        body, name="lru_fwd", grid=(D // LANES, nt), ins=[xr, xg, cw, cb, wa, wx, ba, bx, lam],
        in_specs=[tile, tile, cws, vec, mat, mat, vec, vec, vec],
        out_specs=[tile, tile],
        out_shape=[jax.ShapeDtypeStruct((t_tok, D), BF16), jax.ShapeDtypeStruct((t_tok, D), F32)],
        scratch=[pltpu.VMEM((8, LANES), F32), pltpu.VMEM((1, LANES), F32)], carries=carries)


def _lru_bwd(dy, xr, xg, hseq, cw, cb, wa, wx, ba, bx, lam, carries=()):
    t_tok = xr.shape[0]
    nt = t_tok // LRU_ROWS
    rows = LRU_ROWS

    def body(dy_ref, xr_ref, xrh_ref, xg_ref, h_ref, hh_ref, cw_ref, cb_ref, wa_ref, wx_ref, ba_ref, bx_ref, lam_ref,
             dxr_ref, dxg_ref, dvec_ref, dwa_ref, dwx_ref, gcar_s, acar_s, head_s, tmp_s):
        t = pl.program_id(1)
        first_tile = t == nt - 1

        @pl.when(t == 0)
        def _():
            gcar_s[...] = jnp.zeros_like(gcar_s)
            acar_s[...] = jnp.zeros_like(acar_s)
            head_s[...] = jnp.zeros_like(head_s)
            dvec_ref[...] = jnp.zeros_like(dvec_ref)
            dwa_ref[...] = jnp.zeros_like(dwa_ref)
            dwx_ref[...] = jnp.zeros_like(dwx_ref)

        x = xr_ref[...]
        halo = jnp.where(first_tile, 0.0, xrh_ref[...])
        shifted, xc, xcb, r, ig, sp, a, mult = _lru_pre(x, halo, cw_ref, cb_ref, wa_ref, wx_ref, ba_ref, bx_ref, lam_ref)
        hv = h_ref[...]
        dyv = dy_ref[...]
        gl, glg = _gelu_parts(xg_ref[...])
        dxg_ref[...] = (dyv * hv * glg).astype(BF16)
        acc_a, acc_b = _scan_rows(_shift_up(a, 1, acar_s[...]), dyv * gl, True)
        g = acc_b + acc_a * gcar_s[...]
        hhalo = jnp.where(first_tile, 0.0, hh_ref[...])
        hprev = pltpu.roll(jnp.concatenate([hhalo, hv], axis=0), 1, 0)[8:]
        dmult = g * ig * xc
        dlog_a = a * (g * hprev) - dmult * a * a / mult
        dig = g * mult * xc
        dxc = g * mult * ig
        dzr = (dlog_a * (-LRU_C * sp)) * r * (1.0 - r)
        dzx = dig * ig * (1.0 - ig)
        dzrb, dzxb = dzr.astype(BF16), dzx.astype(BF16)
        dxc = dxc + _nt(dzrb, wa_ref[...]) + _nt(dzxb, wx_ref[...])
        dwa_ref[...] += _tn(xcb, dzrb)
        dwx_ref[...] += _tn(xcb, dzxb)
        dsp = jnp.sum(dlog_a * (-LRU_C * r), axis=0, keepdims=True)
        dlam = dsp * (-jax.nn.sigmoid(-lam_ref[...]))
        vrow = lax.broadcasted_iota(jnp.int32, (8, LANES), 0)
        upd = jnp.where(vrow == 4, jnp.sum(dxc, axis=0, keepdims=True), 0.0)
        upd = jnp.where(vrow == 5, jnp.sum(dzr, axis=0, keepdims=True), upd)
        upd = jnp.where(vrow == 6, jnp.sum(dzx, axis=0, keepdims=True), upd)
        upd = jnp.where(vrow == 7, dlam, upd)
        for k in range(CONV_WIDTH):
            upd = jnp.where(vrow == CONV_WIDTH - 1 - k, jnp.sum(dxc * shifted[k], axis=0, keepdims=True), upd)
        dvec_ref[...] += upd
        ext = jnp.concatenate([dxc, head_s[...]], axis=0)
        dxr = cw_ref[pl.ds(CONV_WIDTH - 1, 1), :] * dxc
        for k in (1, 2, 3):
            dxr = dxr + cw_ref[pl.ds(CONV_WIDTH - 1 - k, 1), :] * pltpu.roll(ext, rows + 8 - k, 0)[:rows]
        dxr_ref[...] = dxr.astype(BF16)
        tmp_s[...] = g
        gcar_s[...] = tmp_s[pl.ds(0, 1), :]
        tmp_s[...] = a
        acar_s[...] = tmp_s[pl.ds(0, 1), :]
        tmp_s[...] = dxc
        head_s[...] = tmp_s[pl.ds(0, 8), :]

    tile, halo, vec, cws, mat = _lru_specs(nt, True)
    return _call(
        body, name="lru_bwd", grid=(D // LANES, nt), ins=[dy, xr, xr, xg, hseq, hseq, cw, cb, wa, wx, ba, bx, lam],
        in_specs=[tile, tile, halo, tile, tile, halo, cws, vec, mat, mat, vec, vec, vec],
        out_specs=[tile, tile, pl.BlockSpec((8, LANES), lambda cb, t: (0, cb)), mat, mat],
        out_shape=[jax.ShapeDtypeStruct((t_tok, D), BF16), jax.ShapeDtypeStruct((t_tok, D), BF16),
                   jax.ShapeDtypeStruct((8, D), F32), jax.ShapeDtypeStruct((D // LANES, LANES, LANES), F32),
                   jax.ShapeDtypeStruct((D // LANES, LANES, LANES), F32)],
        scratch=[pltpu.VMEM((1, LANES), F32), pltpu.VMEM((1, LANES), F32), pltpu.VMEM((8, LANES), F32),
                 pltpu.VMEM((rows, LANES), F32)], carries=carries)


def _t5_bucket_np(rel):
    nb = N_BUCKETS // 2
    max_exact = nb // 2
    ret = np.where(rel > 0, nb, 0)
    n = np.abs(rel)
    nf = np.maximum(n, 1).astype(np.float32)
    large = max_exact + (np.log(nf / np.float32(max_exact)) / np.float32(math.log(MAX_DISTANCE / max_exact))
                         * np.float32(nb - max_exact)).astype(np.int32)
    large = np.minimum(large, nb - 1)
    return ret + np.where(n < max_exact, n, large)


def _bucket_map():
    r = np.arange(QT)[:, None]
    c = np.arange(KW)[None, :]
    j = c - (r // CHUNK) * CHUNK
    band = (j >= 0) & (j < WINDOW + CHUNK)
    return np.where(band, _t5_bucket_np(c - r - WINDOW), -1).astype(np.int32)


def _attn_specs(nt, reverse):
    def tt(i):
        return nt - 1 - i if reverse else i
    kvw = N_KV * LANES
    qs = pl.BlockSpec((QT, D), lambda i: (tt(i), 0))
    cur = pl.BlockSpec((QT, kvw), lambda i: (tt(i), 0))
    prev = pl.BlockSpec((WINDOW, kvw), lambda i: (jnp.maximum(tt(i) * (QT // WINDOW) - 1, 0), 0))
    lse = pl.BlockSpec((QT, LANES), lambda i: (tt(i), 0))
    return qs, cur, prev, lse


REP = N_HEADS // N_KV
SCALE = HEAD_DIM ** -0.5


def _stack_heads(x_ref, g, lo, scale=None):
    parts = []
    for hl in range(REP):
        xs = x_ref[:, pl.ds((2 * g + hl // 2) * LANES, LANES)]
        xs = jnp.where(lo if hl % 2 == 0 else jnp.logical_not(lo), xs, jnp.zeros_like(xs))
        parts.append(xs if scale is None else xs * jnp.asarray(scale, xs.dtype))
    return jnp.concatenate(parts, axis=0)


def _stack_sinks(sink_ref, g, srow):
    sk = jnp.full(srow.shape, sink_ref[REP * g + REP - 1], F32)
    for hl in range(REP - 2, -1, -1):
        sk = jnp.where(srow < (hl + 1) * QT, sink_ref[REP * g + hl], sk)
    return sk


def _attn_fwd(q, kd, vd, bias, sinks, carries=()):
    t_tok = q.shape[0]
    nt = t_tok // QT

    def body(q_ref, kp_ref, kc_ref, vp_ref, vc_ref, bias_ref, sink_ref, o_ref, lse_ref):
        i = pl.program_id(0)
        col = lax.broadcasted_iota(jnp.int32, (1, KW), 1)
        first = jnp.where((i == 0) & (col < WINDOW), NEG_INF, 0.0)
        lane = lax.broadcasted_iota(jnp.int32, (QT, LANES), 1)
        lo = lane < HEAD_DIM
        srow = lax.broadcasted_iota(jnp.int32, (REP * QT, 1), 0)
        lse_t = jnp.zeros((QT, LANES), F32)
        for g in range(N_KV):
            kwin = jnp.concatenate([kp_ref[:, pl.ds(g * LANES, LANES)], kc_ref[:, pl.ds(g * LANES, LANES)]], axis=0)
            vwin = jnp.concatenate([vp_ref[:, pl.ds(g * LANES, LANES)], vc_ref[:, pl.ds(g * LANES, LANES)]], axis=0)
            qst = _stack_heads(q_ref, g, lo, SCALE)
            s = _nt(qst, kwin) + (bias_ref[g] + first)
            sk = _stack_sinks(sink_ref, g, srow)
            m = jnp.maximum(jnp.max(s, axis=-1, keepdims=True), sk)
            e = jnp.exp(s - m)
            l = jnp.sum(e, axis=-1, keepdims=True) + jnp.exp(sk - m)
            p = e / l
            ost = _nn(p.astype(BF16), vwin)
            lse_s = m + jnp.log(l)
            for sl in range(2):
                o_ref[:, pl.ds((2 * g + sl) * LANES, LANES)] = jnp.where(
                    lo, ost[2 * sl * QT:(2 * sl + 1) * QT], ost[(2 * sl + 1) * QT:(2 * sl + 2) * QT]).astype(BF16)
            for hl in range(REP):
                lse_t = jnp.where(lane == REP * g + hl, lse_s[hl * QT:(hl + 1) * QT], lse_t)
        lse_ref[...] = lse_t

    qs, cur, prev, lse = _attn_specs(nt, False)
    return _call(
        body, name="attn_fwd", grid=(nt,), ins=[q, kd, kd, vd, vd, bias, sinks],
        in_specs=[qs, prev, cur, prev, cur, _const_spec((N_KV, REP * QT, KW)), pl.BlockSpec(memory_space=pltpu.SMEM)],
        out_specs=[qs, lse],
        out_shape=[jax.ShapeDtypeStruct((t_tok, D), BF16), jax.ShapeDtypeStruct((t_tok, LANES), F32)],
        vmem=48 * 2 ** 20, carries=carries)


def _attn_bwd(q, kd, vd, o, do, lse, bias, sinks, carries=()):
    t_tok = q.shape[0]
    nt = t_tok // QT
    kvw = N_KV * LANES

    def body(q_ref, kp_ref, kc_ref, vp_ref, vc_ref, o_ref, do_ref, lse_ref, bias_ref, sink_ref,
             dqkv_ref, ds_ref, dsink_ref, kcar_s, vcar_s):
        i = pl.program_id(0)
        tile = nt - 1 - i

        @pl.when(i == 0)
        def _():
            kcar_s[...] = jnp.zeros_like(kcar_s)
            vcar_s[...] = jnp.zeros_like(vcar_s)
            ds_ref[...] = jnp.zeros_like(ds_ref)
            dsink_ref[...] = jnp.zeros_like(dsink_ref)

        col = lax.broadcasted_iota(jnp.int32, (1, KW), 1)
        first = jnp.where((tile == 0) & (col < WINDOW), NEG_INF, 0.0)
        lane = lax.broadcasted_iota(jnp.int32, (QT, LANES), 1)
        lo = lane < HEAD_DIM
        lane_k = lax.broadcasted_iota(jnp.int32, (KW, LANES), 1)
        lane_1 = lax.broadcasted_iota(jnp.int32, (1, LANES), 1)
        srow = lax.broadcasted_iota(jnp.int32, (REP * QT, 1), 0)
        lse_t = lse_ref[...]
        dsink = jnp.zeros((1, LANES), F32)
        for g in range(N_KV):
            kwin = jnp.concatenate([kp_ref[:, pl.ds(g * LANES, LANES)], kc_ref[:, pl.ds(g * LANES, LANES)]], axis=0)
            vwin = jnp.concatenate([vp_ref[:, pl.ds(g * LANES, LANES)], vc_ref[:, pl.ds(g * LANES, LANES)]], axis=0)
            qst = _stack_heads(q_ref, g, lo, SCALE)
            dost = _stack_heads(do_ref, g, lo)
            od = [do_ref[:, pl.ds((2 * g + sl) * LANES, LANES)].astype(F32)
                  * o_ref[:, pl.ds((2 * g + sl) * LANES, LANES)].astype(F32) for sl in range(2)]
            drow = jnp.concatenate([jnp.sum(jnp.where(lo if hl % 2 == 0 else jnp.logical_not(lo), od[hl // 2], 0.0),
                                            axis=-1, keepdims=True) for hl in range(REP)], axis=0)
            lse_s = jnp.concatenate([jnp.sum(jnp.where(lane == REP * g + hl, lse_t, 0.0), axis=-1, keepdims=True)
                                     for hl in range(REP)], axis=0)
            s = _nt(qst, kwin) + (bias_ref[g] + first)
            p = jnp.exp(s - lse_s)
            ds = p * (_nt(dost, vwin) - drow)
            ds_ref[g] += ds
            tsink = -(jnp.exp(_stack_sinks(sink_ref, g, srow) - lse_s) * drow)
            for hl in range(REP):
                dsink = dsink + jnp.where(lane_1 == REP * g + hl,
                                          jnp.sum(tsink[hl * QT:(hl + 1) * QT], axis=0, keepdims=True), 0.0)
            dsb = ds.astype(BF16)
            dqst = _nn(dsb, kwin) * SCALE
            for sl in range(2):
                dqkv_ref[:, pl.ds((2 * g + sl) * LANES, LANES)] = jnp.where(
                    lo, dqst[2 * sl * QT:(2 * sl + 1) * QT], dqst[(2 * sl + 1) * QT:(2 * sl + 2) * QT]).astype(BF16)
            dk_acc = _tn(dsb, qst)
            dv_acc = _tn(p.astype(BF16), dost)
            dk_f = jnp.where(lane_k < HEAD_DIM, dk_acc + pltpu.roll(dk_acc, HEAD_DIM, 1), 0.0)
            dv_f = jnp.where(lane_k < HEAD_DIM, dv_acc + pltpu.roll(dv_acc, HEAD_DIM, 1), 0.0)
            for acc, col0, car in ((dk_f, K0, kcar_s), (dv_f, V0, vcar_s)):
                cs = pl.ds(g * LANES, LANES)
                co = pl.ds(col0 + g * LANES, LANES)
                if QT > WINDOW:
                    dqkv_ref[pl.ds(0, QT - WINDOW), co] = acc[WINDOW:QT].astype(BF16)
                dqkv_ref[pl.ds(QT - WINDOW, WINDOW), co] = (acc[QT:KW] + car[:, cs]).astype(BF16)
                car[:, cs] = acc[0:WINDOW]
        dsink_ref[...] += dsink

    qs, cur, prev, lse_s = _attn_specs(nt, True)
    return _call(
        body, name="attn_bwd", grid=(nt,), ins=[q, kd, kd, vd, vd, o, do, lse, bias, sinks],
        in_specs=[qs, prev, cur, prev, cur, qs, qs, lse_s, _const_spec((N_KV, REP * QT, KW)),
                  pl.BlockSpec(memory_space=pltpu.SMEM)],
        out_specs=[pl.BlockSpec((QT, XR0), lambda i: (nt - 1 - i, 0)), _const_spec((N_KV, REP * QT, KW)),
                   _const_spec((1, LANES))],
        out_shape=[jax.ShapeDtypeStruct((t_tok, XR0), BF16), jax.ShapeDtypeStruct((N_KV, REP * QT, KW), F32),
                   jax.ShapeDtypeStruct((1, LANES), F32)],
        scratch=[pltpu.VMEM((WINDOW, kvw), F32), pltpu.VMEM((WINDOW, kvw), F32)],
        vmem=VMEM_BIG, carries=carries)


def _bias_tile(table, bmap):
    def body(tab_ref, bm_ref, out_ref):
        bm = bm_ref[...]

        def per_head(hd, carry):
            acc = jnp.full((QT, KW), NEG_INF, F32)
            for b in range(N_BUCKETS):
                acc = jnp.where(bm == b, tab_ref[b, hd], acc)
            out_ref[hd] = acc
            return carry

        lax.fori_loop(0, N_HEADS, per_head, 0)

    return pl.pallas_call(
        body, name="bias_tile", out_shape=jax.ShapeDtypeStruct((N_HEADS, QT, KW), F32),
        in_specs=[pl.BlockSpec(memory_space=pltpu.SMEM), pl.BlockSpec(memory_space=pltpu.VMEM)],
        out_specs=pl.BlockSpec(memory_space=pltpu.VMEM))(table, bmap)


def _bias_grad(ds_acc, bmap):
    def body(ds_ref, bm_ref, out_ref):
        row = lax.broadcasted_iota(jnp.int32, (N_BUCKETS, LANES), 0)
        lane = lax.broadcasted_iota(jnp.int32, (N_BUCKETS, LANES), 1)
        bm = bm_ref[...]

        def per_head(hd, res):
            dsv = ds_ref[hd]
            for b in range(N_BUCKETS):
                val = jnp.sum(jnp.sum(jnp.where(bm == b, dsv, 0.0), axis=0, keepdims=True), axis=1, keepdims=True)
                res = jnp.where((row == b) & (lane == hd), val, res)
            return res

        out_ref[...] = lax.fori_loop(0, N_HEADS, per_head, jnp.zeros((N_BUCKETS, LANES), F32))

    return pl.pallas_call(
        body, name="bias_grad", out_shape=jax.ShapeDtypeStruct((N_BUCKETS, LANES), F32),
        in_specs=[pl.BlockSpec(memory_space=pltpu.VMEM), pl.BlockSpec(memory_space=pltpu.VMEM)],
        out_specs=pl.BlockSpec(memory_space=pltpu.VMEM))(ds_acc, bmap)


MIXOUT = ("w_lru_out", "w_attn_out", "w_o")


def _mix_out_fwd(ya_in, o, gs, h, g, wg, tm=256, carries=()):
    t_tok = h.shape[0]
    nt = t_tok // tm
    groups = _groups_of(MIXOUT)

    def body(ya_ref, o_ref, gs_ref, h_ref, g_ref, wg_ref, hout_ref, yao_ref, ybo_ref, z_ref, wa_s, wb_s, wo_s, sems):
        i = pl.program_id(0)

        @pl.when(i == 0)
        def _():
            _load_weights({groups[0]: wg_ref}, list(zip(MIXOUT, (wa_s, wb_s, wo_s))), sems)

        ya = _nn(ya_ref[...], wa_s[...])
        yb = _nn(o_ref[...], wb_s[...])
        yao_ref[...] = ya
        ybo_ref[...] = yb
        merged = gs_ref[:, pl.ds(0, D)] * ya + gs_ref[:, pl.ds(D, D)] * yb
        z = _nn(merged.astype(BF16), wo_s[...])
        z_ref[...] = z
        zn, _, _ = _rms_fwd(z, g_ref[...])
        hout_ref[...] = h_ref[...] + zn

    return _call(
        body, name="mix_out_fwd", grid=(nt,), ins=[ya_in, o, gs, h, g, wg[groups[0]]],
        in_specs=[_row_spec(tm, D), _row_spec(tm, D), _row_spec(tm, 2 * D), _row_spec(tm, D), _const_spec((1, D)), ANY],
        out_specs=[_row_spec(tm, D)] * 4,
        out_shape=[jax.ShapeDtypeStruct((t_tok, D), F32)] * 4,
        scratch=[pltpu.VMEM((D, D), BF16)] * 3 + [DMA((3 * NDEV,))],
        vmem=48 * 2 ** 20, carries=carries)


def _mix_out_bwd(dh, z, ya, yb, gs, g, wg, tm=512, carries=()):
    t_tok = dh.shape[0]
    nt = t_tok // tm
    groups = _groups_of(MIXOUT)

    def body(dh_ref, z_ref, ya_ref, yb_ref, gs_ref, g_ref, wg_ref,
             dyain_ref, do_ref, dgpre_ref, dya_ref, dyb_ref, mg_ref, dz_ref, dg_ref, dbg_ref,
             wa_s, wb_s, wo_s, sems):
        i = pl.program_id(0)

        @pl.when(i == 0)
        def _():
            _load_weights({groups[0]: wg_ref}, list(zip(MIXOUT, (wa_s, wb_s, wo_s))), sems)
            dg_ref[...] = jnp.zeros_like(dg_ref)
            dbg_ref[...] = jnp.zeros_like(dbg_ref)

        gv = g_ref[...]
        _, zh, r = _rms_fwd(z_ref[...], gv)
        dz, dg = _rms_bwd(dh_ref[...], zh, r, gv)
        dg_ref[...] += dg
        dzb = dz.astype(BF16)
        dz_ref[...] = dzb
        ga, gb = gs_ref[:, pl.ds(0, D)], gs_ref[:, pl.ds(D, D)]
        ya_v, yb_v = ya_ref[...], yb_ref[...]
        mg_ref[...] = (ga * ya_v + gb * yb_v).astype(BF16)
        dm = _nt(dzb, wo_s[...])
        dga = dm * ya_v * ga * (1.0 - ga)
        dgb = dm * yb_v * gb * (1.0 - gb)
        dgpre_ref[:, pl.ds(0, D)] = dga.astype(BF16)
        dgpre_ref[:, pl.ds(D, D)] = dgb.astype(BF16)
        dbg_ref[:, pl.ds(0, D)] += jnp.sum(dga, axis=0, keepdims=True)
        dbg_ref[:, pl.ds(D, D)] += jnp.sum(dgb, axis=0, keepdims=True)
        dya = (dm * ga).astype(BF16)
        dyb = (dm * gb).astype(BF16)
        dya_ref[...] = dya
        dyb_ref[...] = dyb
        dyain_ref[...] = _nt(dya, wa_s[...])
        do_ref[...] = _nt(dyb, wb_s[...]).astype(BF16)

    bf = jax.ShapeDtypeStruct((t_tok, D), BF16)
    return _call(
        body, name="mix_out_bwd", grid=(nt,), ins=[dh, z, ya, yb, gs, g, wg[groups[0]]],
        in_specs=[_row_spec(tm, D)] * 4 + [_row_spec(tm, 2 * D), _const_spec((1, D)), ANY],
        out_specs=[_row_spec(tm, D), _row_spec(tm, D), _row_spec(tm, 2 * D)] + [_row_spec(tm, D)] * 4
        + [_const_spec((1, D)), _const_spec((1, 2 * D))],
        out_shape=[jax.ShapeDtypeStruct((t_tok, D), F32), bf, jax.ShapeDtypeStruct((t_tok, 2 * D), BF16), bf, bf, bf, bf,
                   jax.ShapeDtypeStruct((1, D), F32), jax.ShapeDtypeStruct((1, 2 * D), F32)],
        scratch=[pltpu.VMEM((D, D), BF16)] * 3 + [DMA((3 * NDEV,))],
        vmem=VMEM_BIG, carries=carries)


def _sum_parts(parts_list):
    n = len(parts_list)
    _, r, c = parts_list[0].shape
    tc = 256

    def body(*refs):
        for p_ref, o_ref in zip(refs[:n], refs[n:]):
            acc = p_ref[0].astype(F32)
            for s in range(1, NDEV):
                acc = acc + p_ref[s].astype(F32)
            o_ref[...] = acc

    return pl.pallas_call(
        body, name=f"sum_parts_{r}", grid=(c // tc,),
        in_specs=[pl.BlockSpec((NDEV, r, tc), lambda i: (0, 0, i))] * n,
        out_specs=[pl.BlockSpec((r, tc), lambda i: (0, i))] * n,
        out_shape=[jax.ShapeDtypeStruct((r, c), F32)] * n,
        compiler_params=pltpu.CompilerParams(dimension_semantics=("arbitrary",), vmem_limit_bytes=48 * 2 ** 20),
    )(*parts_list)


def _adamw_math(w, g, m, v):
    m = ADAM_B1 * m + (1.0 - ADAM_B1) * g
    v = ADAM_B2 * v + (1.0 - ADAM_B2) * (g * g)
    m_hat = m / (1.0 - ADAM_B1 ** ADAM_STEP)
    v_hat = v / (1.0 - ADAM_B2 ** ADAM_STEP)
    delta = -ADAM_LR * (m_hat / (jnp.sqrt(v_hat) + ADAM_EPS) + ADAM_WD * w)
    return delta, m, v


def _adamw_body(n):
    def body(*refs):
        for k in range(n):
            g_ref, w_ref, m_ref, v_ref = refs[4 * k:4 * k + 4]
            d_ref, nm_ref, nv_ref = refs[4 * n + 3 * k:4 * n + 3 * k + 3]
            d, m, v = _adamw_math(w_ref[...], g_ref[...], m_ref[...], v_ref[...])
            d_ref[...] = d
            nm_ref[...] = m
            nv_ref[...] = v
    return body


def _adamw(items):
    n = len(items)
    r, c = items[0][0].shape
    tr = r if r * c <= 2 ** 18 else max(t for t in range(8, 65, 8) if r % t == 0)
    spec = pl.BlockSpec((tr, c), lambda i: (i, 0))
    outs = pl.pallas_call(
        _adamw_body(n), name=f"adamw_{r}x{c}", grid=(r // tr,),
        in_specs=[spec] * (4 * n), out_specs=[spec] * (3 * n),
        out_shape=[jax.ShapeDtypeStruct((r, c), F32)] * (3 * n),
        compiler_params=pltpu.CompilerParams(dimension_semantics=("arbitrary",), vmem_limit_bytes=40 * 2 ** 20),
    )(*[a for it in items for a in it])
    return [tuple(outs[3 * k:3 * k + 3]) for k in range(n)]


def _adamw_small(items):
    n = len(items)
    vm = pl.BlockSpec(memory_space=pltpu.VMEM)
    outs = pl.pallas_call(
        _adamw_body(n), name="adamw_small", in_specs=[vm] * (4 * n), out_specs=[vm] * (3 * n),
        out_shape=[jax.ShapeDtypeStruct(it[1].shape, F32) for it in items for _ in range(3)],
    )(*[a for it in items for a in it])
    return [tuple(outs[3 * k:3 * k + 3]) for k in range(n)]


def _pack_small(arrs):
    rows, offs = [], []
    total = 0
    for a in arrs:
        flat = a.reshape(-1).astype(F32)
        nr = -(-flat.shape[0] // LANES)
        flat = jnp.pad(flat, (0, nr * LANES - flat.shape[0]))
        rows.append(flat.reshape(nr, LANES))
        offs.append((total, nr))
        total += nr
    pad = -total % 8
    if pad:
        rows.append(jnp.zeros((pad, LANES), F32))
    return jnp.concatenate(rows, axis=0), offs


def _unpack_small(pack, offs, shapes):
    out = []
    for (o, nr), shp in zip(offs, shapes):
        size = int(np.prod(shp))
        out.append(pack[o:o + nr].reshape(-1)[:size].reshape(shp))
    return out


def _sum_small(gathered, rows):
    def body(p_ref, o_ref):
        acc = p_ref[pl.ds(0, rows), :]
        for s in range(1, NDEV):
            acc = acc + p_ref[pl.ds(s * rows, rows), :]
        o_ref[...] = acc

    return pl.pallas_call(
        body, name="sum_small", out_shape=jax.ShapeDtypeStruct((rows, LANES), F32),
        in_specs=[pl.BlockSpec(memory_space=pltpu.VMEM)], out_specs=pl.BlockSpec(memory_space=pltpu.VMEM))(gathered)


def _block_diag(w):
    w = w.reshape(D // LANES, 2, LRU_BLOCK, LRU_BLOCK)
    z = jnp.zeros((D // LANES, LRU_BLOCK, LRU_BLOCK), w.dtype)
    top = jnp.concatenate([w[:, 0], z], axis=2)
    bot = jnp.concatenate([z, w[:, 1]], axis=2)
    return jnp.concatenate([top, bot], axis=1)


def _block_diag_grad(dw):
    a = dw[:, :LRU_BLOCK, :LRU_BLOCK]
    b = dw[:, LRU_BLOCK:, LRU_BLOCK:]
    return jnp.stack([a, b], axis=1).reshape(D // LRU_BLOCK, LRU_BLOCK, LRU_BLOCK)


def kernel(x, ffn1_pre_g, ffn1_w1, ffn1_w3, ffn1_w2, ffn1_post_g, mix_pre_g, w_in, conv_w, conv_b, rg_a_w, rg_a_b, rg_x_w, rg_x_b, lru_lambda, w_lru_out, attn_sinks, rel_bias, w_attn_out, w_gate, b_gate, w_o, mix_post_g, ffn2_pre_g, ffn2_w1, ffn2_w3, ffn2_w2, ffn2_post_g, loss_target, m_ffn1_pre_g, m_ffn1_w1, m_ffn1_w3, m_ffn1_w2, m_ffn1_post_g, m_mix_pre_g, m_w_in, m_conv_w, m_conv_b, m_rg_a_w, m_rg_a_b, m_rg_x_w, m_rg_x_b, m_lru_lambda, m_w_lru_out, m_attn_sinks, m_rel_bias, m_w_attn_out, m_w_gate, m_b_gate, m_w_o, m_mix_post_g, m_ffn2_pre_g, m_ffn2_w1, m_ffn2_w3, m_ffn2_w2, m_ffn2_post_g, v_ffn1_pre_g, v_ffn1_w1, v_ffn1_w3, v_ffn1_w2, v_ffn1_post_g, v_mix_pre_g, v_w_in, v_conv_w, v_conv_b, v_rg_a_w, v_rg_a_b, v_rg_x_w, v_rg_x_b, v_lru_lambda, v_w_lru_out, v_attn_sinks, v_rel_bias, v_w_attn_out, v_w_gate, v_b_gate, v_w_o, v_mix_post_g, v_ffn2_pre_g, v_ffn2_w1, v_ffn2_w3, v_ffn2_w2, v_ffn2_post_g):
    names = ["ffn1_pre_g", "ffn1_w1", "ffn1_w3", "ffn1_w2", "ffn1_post_g", "mix_pre_g", "w_in", "conv_w", "conv_b",
             "rg_a_w", "rg_a_b", "rg_x_w", "rg_x_b", "lru_lambda", "w_lru_out", "attn_sinks", "rel_bias", "w_attn_out",
             "w_gate", "b_gate", "w_o", "mix_post_g", "ffn2_pre_g", "ffn2_w1", "ffn2_w3", "ffn2_w2", "ffn2_post_g"]
    ws = dict(zip(names, (ffn1_pre_g, ffn1_w1, ffn1_w3, ffn1_w2, ffn1_post_g, mix_pre_g, w_in, conv_w, conv_b, rg_a_w,
                          rg_a_b, rg_x_w, rg_x_b, lru_lambda, w_lru_out, attn_sinks, rel_bias, w_attn_out, w_gate,
                          b_gate, w_o, mix_post_g, ffn2_pre_g, ffn2_w1, ffn2_w3, ffn2_w2, ffn2_post_g)))
    ms = dict(zip(names, (m_ffn1_pre_g, m_ffn1_w1, m_ffn1_w3, m_ffn1_w2, m_ffn1_post_g, m_mix_pre_g, m_w_in, m_conv_w,
                          m_conv_b, m_rg_a_w, m_rg_a_b, m_rg_x_w, m_rg_x_b, m_lru_lambda, m_w_lru_out, m_attn_sinks,
                          m_rel_bias, m_w_attn_out, m_w_gate, m_b_gate, m_w_o, m_mix_post_g, m_ffn2_pre_g, m_ffn2_w1,
                          m_ffn2_w3, m_ffn2_w2, m_ffn2_post_g)))
    vs = dict(zip(names, (v_ffn1_pre_g, v_ffn1_w1, v_ffn1_w3, v_ffn1_w2, v_ffn1_post_g, v_mix_pre_g, v_w_in, v_conv_w,
                          v_conv_b, v_rg_a_w, v_rg_a_b, v_rg_x_w, v_rg_x_b, v_lru_lambda, v_w_lru_out, v_attn_sinks,
                          v_rel_bias, v_w_attn_out, v_w_gate, v_b_gate, v_w_o, v_mix_post_g, v_ffn2_pre_g, v_ffn2_w1,
                          v_ffn2_w3, v_ffn2_w2, v_ffn2_post_g)))
    me = 4 * lax.axis_index("x") + 2 * lax.axis_index("y") + lax.axis_index("c")
    vec = lambda n: ws[n].reshape(1, -1)

    def shard2d(name):
        if name == "conv":
            row = lax.bitcast_convert_type(conv_w.reshape(CONV_WIDTH, LANES), BF16).reshape(1, D)
            return jnp.concatenate([row, jnp.zeros((LOC["conv"][2] - 1, D), BF16)], axis=0)
        a = ws[name].reshape(ws[name].shape[-2], ws[name].shape[-1])
        return (a.T if name in COL_SHARDED else a).astype(BF16)

    packs = {g: jnp.concatenate([shard2d(n) for n, _ in members], axis=0) for g, members in GROUPS}

    wg = {}
    _, ((wg["ffn1"],),) = _call(None, name="allgather_ffn1", grid=(), ins=[], in_specs=[], out_shape=[], out_specs=[],
                                carries=[_AllGather(packs["ffn1"])])
    conv_rows = wg["ffn1"].reshape(NDEV, GROUP_ROWS["ffn1"], D)[:, LOC["conv"][1]]
    cw = jnp.transpose(lax.bitcast_convert_type(conv_rows.reshape(NDEV, CONV_WIDTH, LANES, 2), F32),
                       (1, 0, 2)).reshape(CONV_WIDTH, D)
    bmap = jnp.asarray(_bucket_map())
    bias = _bias_tile(rel_bias, bmap).reshape(N_KV, REP * QT, KW)
    sinks = attn_sinks.reshape(N_HEADS)
    wa_bd = _block_diag(rg_a_w.reshape(D // LRU_BLOCK, LRU_BLOCK, LRU_BLOCK)).astype(BF16)
    wx_bd = _block_diag(rg_x_w.reshape(D // LRU_BLOCK, LRU_BLOCK, LRU_BLOCK)).astype(BF16)
    lru_args = (cw, vec("conv_b"), wa_bd, wx_bd, vec("rg_a_b"), vec("rg_x_b"), vec("lru_lambda"))
    x2, tgt = x[0], loss_target[0]

    (h1, a1, b1, f1, nb1), ((wg["mixin"],),) = _ffn_fwd(
        x2, vec("ffn1_pre_g"), vec("ffn1_post_g"), wg, ("ffn1_w1", "ffn1_w3", "ffn1_w2"),
        carries=[_AllGather(packs["mixin"])])
    (q, kd, vd, xr, xg, gs, ub), ((wg["mixout"],),) = _mix_proj_fwd(
        h1, vec("mix_pre_g"), vec("b_gate"), wg, carries=[_AllGather(packs["mixout"])])
    (ya_in, hseq), ((wg["ffn2a"],),) = _lru_fwd(xr, xg, *lru_args, carries=[_AllGather(packs["ffn2a"])])
    (o, lse), ((wg["ffn2b"],),) = _attn_fwd(q, kd, vd, bias, sinks, carries=[_AllGather(packs["ffn2b"])])
    (h2, ya, yb, z), _ = _mix_out_fwd(ya_in, o, gs, h1, vec("mix_post_g"), wg)
    (_, a2, b2, f2, nb2, dy, loss_part), _ = _ffn_fwd(h2, vec("ffn2_pre_g"), vec("ffn2_post_g"), wg,
                                                     ("ffn2_w1", "ffn2_w3", "ffn2_w2"), target=tgt)
    loss = lax.psum(loss_part[0, 0], ("x", "y", "c"))

    gsm, parts = {}, {}
    rs = lambda *grads_: [_ReduceScatterSend(list(grads_))]
    ffr = FF // NDEV
    (dab, s_act, dfb, gsm["ffn2_post_g"]), _ = _ffn_bwd_a(dy, f2, a2, b2, vec("ffn2_post_g"), wg, "ffn2_w2")
    g_w2, _ = _dw(s_act, dfb, FF // 2, "dw_ffn2_w2")
    g_w13, ((parts["ffn2_w2"],),) = _dw(dab, nb2, FF // 2, "dw_ffn2_w13", carries=rs(g_w2))
    (dh2, gsm["ffn2_pre_g"]), ((parts["ffn2_w1"],),) = _ffn_bwd_dx(
        dab, h2, dy, vec("ffn2_pre_g"), wg, "ffn2_w1", "ffn2_w3", carries=rs((g_w13, 0, ffr)))
    (dya_in, do, dgpre, dya, dyb, mg, dzb, gsm["mix_post_g"], gsm["b_gate"]), ((parts["ffn2_w3"],),) = _mix_out_bwd(
        dh2, z, ya, yb, gs, vec("mix_post_g"), wg, carries=rs((g_w13, FF, ffr)))
    g_wa, _ = _dw(ya_in, dya, D // 2, "dw_w_lru_out")
    g_wb, _ = _dw(o, dyb, D // 2, "dw_w_attn_out")
    g_wo, _ = _dw(mg, dzb, D // 2, "dw_w_o")
    (dqkv, ds_acc, dsink), (mixout_parts,) = _attn_bwd(
        q, kd, vd, o, do, lse, bias, sinks, carries=rs(g_wa, g_wb, g_wo))
    parts["w_lru_out"], parts["w_attn_out"], parts["w_o"] = mixout_parts
    (dxr, dxg, dvec, dwa, dwx), _ = _lru_bwd(dya_in, xr, xg, hseq, *lru_args)
    g_qkv, _ = _dw(dqkv, ub, D, "dw_w_in_qkv")
    g_xr, _ = _dw(dxr, ub, D // 2, "dw_w_in_xr")
    g_xg, _ = _dw(dxg, ub, D // 2, "dw_w_in_xg")
    g_wgate, _ = _dw(dgpre, ub, D, "dw_w_gate")
    g_win = jnp.concatenate(
        [g_qkv[:D]] + [g_qkv[c0 + gi * LANES:c0 + gi * LANES + HEAD_DIM] for c0 in (K0, V0) for gi in range(N_KV)]
        + [g_xr, g_xg], axis=0)
    (dh1, gsm["mix_pre_g"]), ((parts["w_in"],),) = _mix_proj_bwd(
        dqkv, dxr, dxg, dgpre, h1, dh2, vec("mix_pre_g"), wg, carries=rs(g_win))
    gsm["conv_w"] = dvec[0:CONV_WIDTH]
    gsm["conv_b"], gsm["rg_a_b"], gsm["rg_x_b"], gsm["lru_lambda"] = dvec[4], dvec[5], dvec[6], dvec[7]
    gsm["rg_a_w"] = _block_diag_grad(dwa)
    gsm["rg_x_w"] = _block_diag_grad(dwx)
    gsm["attn_sinks"] = dsink[0, :N_HEADS]
    gsm["rel_bias"] = _bias_grad(ds_acc.reshape(N_HEADS, QT, KW), bmap)[:, :N_HEADS]
    late = ("ffn1_post_g", "ffn1_pre_g")
    early = tuple(n for n in SMALL if n not in late)
    early_pack, early_offs = _pack_small([gsm[n] for n in early])
    (dab, s_act, dfb, gsm["ffn1_post_g"]), ((parts["w_gate"],), (early_all,)) = _ffn_bwd_a(
        dh1, f1, a1, b1, vec("ffn1_post_g"), wg, "ffn1_w2", carries=rs(g_wgate) + [_AllGather(early_pack)])
    g_w2, _ = _dw(s_act, dfb, FF // 2, "dw_ffn1_w2")
    g_w13, ((parts["ffn1_w2"],),) = _dw(dab, nb1, FF // 2, "dw_ffn1_w13", carries=rs(g_w2))
    (grad_x, gsm["ffn1_pre_g"]), ((parts["ffn1_w1"],),) = _ffn_bwd_dx(
        dab, x2, dh1, vec("ffn1_pre_g"), wg, "ffn1_w1", "ffn1_w3", carries=rs((g_w13, 0, ffr)))
    late_pack, late_offs = _pack_small([gsm[n] for n in late])
    _, ((parts["ffn1_w3"],), (late_all,)) = _call(
        None, name="final_exchange", grid=(), ins=[], in_specs=[], out_shape=[], out_specs=[],
        carries=rs((g_w13, FF, ffr)) + [_AllGather(late_pack)])

    grads = {}
    by_rows = {}
    for n in BIG:
        by_rows.setdefault(LOC[n][2], []).append(n)
    for group in by_rows.values():
        for n, g in zip(group, _sum_parts([parts[n] for n in group])):
            grads[n] = g
    for group, gathered, offs in ((early, early_all, early_offs), (late, late_all, late_offs)):
        total = _sum_small(gathered, gathered.shape[0] // NDEV)
        shapes = [(CONV_WIDTH, D) if n == "conv_w" else ws[n].shape for n in group]
        for n, g in zip(group, _unpack_small(total, offs, shapes)):
            grads[n] = g
    grads["conv_w"] = lax.dynamic_slice(grads["conv_w"], (0, me * LANES), (CONV_WIDTH, LANES)).reshape(conv_w.shape)

    delta, new_m, new_v = {}, {}, {}
    t_form = [n for n in COL_SHARDED if ws[n].shape[-1] % LANES]
    view = {n: (lambda a: a.reshape(a.shape[-2], a.shape[-1]).T) if n in t_form
            else (lambda a: a.reshape(a.shape[-2], a.shape[-1])) for n in BIG}
    unview = {n: (lambda a: a.T) if n in t_form else (lambda a: a) for n in BIG}
    for n in COL_SHARDED:
        if n not in t_form:
            grads[n] = grads[n].T
    by_shape = {}
    for n in BIG:
        by_shape.setdefault(view[n](ws[n]).shape, []).append(n)
    for group in by_shape.values():
        res = _adamw([(grads[n], view[n](ws[n]), view[n](ms[n]), view[n](vs[n])) for n in group])
        for n, (d_, m_, v_) in zip(group, res):
            delta[n], new_m[n], new_v[n] = unview[n](d_), unview[n](m_), unview[n](v_)
            grads[n] = unview[n](grads[n])
    flat2d = lambda a: a.reshape(-1, a.shape[-1])
    res = _adamw_small([(flat2d(grads[n].reshape(ws[n].shape)), flat2d(ws[n]), flat2d(ms[n]), flat2d(vs[n]))
                        for n in SMALL])
    for n, (d_, m_, v_) in zip(SMALL, res):
        delta[n], new_m[n], new_v[n] = d_, m_, v_

    outs = [loss, grad_x.reshape(x.shape)]
    for src in (grads, delta, new_m, new_v):
        outs += [src[n].reshape(ws[n].shape) for n in names]
    return tuple(outs)
```

```python
import functools
import math
import operator

import numpy as np
import jax
import jax.numpy as jnp
from jax import lax
from jax.experimental import pallas as pl
from jax.experimental.pallas import tpu as pltpu

F32, BF16 = jnp.float32, jnp.bfloat16

NDEV = 8
D = 1024
FF = 2816
N_HEADS, N_KV, HEAD_DIM = 16, 4, 64
CHUNK, WINDOW = 64, 128
N_BUCKETS, MAX_DISTANCE = 32, 128
LRU_BLOCK = 64
CONV_WIDTH = 4
LRU_C = 8.0
RMS_EPS = 1e-6
NEG_INF = -1e30
LANES = 128
QT = 128
KW = QT + WINDOW
LRU_ROWS = 512
IN_W = D + 2 * N_KV * HEAD_DIM + 2 * D
INP_W = D + 2 * N_KV * LANES + 2 * D
VMEM_BIG = 58 * 2 ** 20

ADAM_LR, ADAM_B1, ADAM_B2, ADAM_EPS, ADAM_WD, ADAM_STEP = 0.001, 0.9, 0.999, 1e-08, 0.01, 10

GROUPS = (("ffn1", (("ffn1_w1", FF // NDEV), ("ffn1_w3", FF // NDEV), ("ffn1_w2", FF // NDEV), ("conv", 16))),
          ("mixin", (("w_in", IN_W // NDEV), ("w_gate", 2 * D // NDEV))),
          ("mixout", (("w_lru_out", D // NDEV), ("w_attn_out", D // NDEV), ("w_o", D // NDEV))),
          ("ffn2a", (("ffn2_w1", FF // NDEV), ("ffn2_w3", FF // NDEV))),
          ("ffn2b", (("ffn2_w2", FF // NDEV),)))
LOC, GROUP_ROWS = {}, {}
for _g, _members in GROUPS:
    _o = 0
    for _n, _r in _members:
        LOC[_n] = (_g, _o, _r)
        _o += _r
    GROUP_ROWS[_g] = _o
BIG = tuple(n for _, members in GROUPS for n, _ in members if n != "conv")
COL_SHARDED = ("ffn1_w1", "ffn1_w3", "w_in", "w_gate", "ffn2_w1", "ffn2_w3")

SMALL = ("ffn1_pre_g", "ffn1_post_g", "mix_pre_g", "conv_w", "conv_b", "rg_a_w", "rg_a_b", "rg_x_w", "rg_x_b",
         "lru_lambda", "attn_sinks", "rel_bias", "b_gate", "mix_post_g", "ffn2_pre_g", "ffn2_post_g")

MESH = pl.DeviceIdType.MESH
ANY = pl.BlockSpec(memory_space=pl.ANY)
DMA = pltpu.SemaphoreType.DMA


def _nn(a, b):
    return lax.dot_general(a, b, (((1,), (0,)), ((), ())), preferred_element_type=F32)


def _nt(a, b):
    return lax.dot_general(a, b, (((1,), (1,)), ((), ())), preferred_element_type=F32)


def _tn(a, b):
    return lax.dot_general(a, b, (((0,), (0,)), ((), ())), preferred_element_type=F32)


def _rms_fwd(x, g):
    r = lax.rsqrt(jnp.mean(x * x, axis=-1, keepdims=True) + RMS_EPS)
    xh = x * r
    return xh * g, xh, r


def _rms_bwd(dn, xh, r, g):
    dxh = dn * g
    dx = r * (dxh - xh * jnp.mean(dxh * xh, axis=-1, keepdims=True))
    return dx, jnp.sum(dn * xh, axis=0, keepdims=True)


def _row_spec(tm, c):
    return pl.BlockSpec((tm, c), lambda i: (i, 0))


def _const_spec(shape):
    nd = len(shape)
    return pl.BlockSpec(shape, lambda i: (0,) * nd)


class _AllGather:
    def __init__(self, shard):
        self.m, n = shard.shape
        self.ins = [shard]
        self.out_shape = [jax.ShapeDtypeStruct((NDEV * self.m, n), shard.dtype)]
        self.scratch = [DMA((7,)), DMA((7,)), DMA]

    def _copies(self, ins, outs, scr, all_of_them):
        x_ref, out_ref = ins[0], outs[0]
        send_sems, recv_sems, local_sem = scr
        x, y, c = lax.axis_index("x"), lax.axis_index("y"), lax.axis_index("c")
        me, sibling = (x, y, c), (x, y, 1 - c)
        chips = [(1 - x, y), (x, 1 - y), (1 - x, 1 - y)]
        m = self.m

        def rows(px, py, pc):
            return out_ref.at[pl.ds((4 * px + 2 * py + pc) * m, m), :]

        def copy(k, block, to, src=None):
            return pltpu.make_async_remote_copy(
                src_ref=rows(*block) if src is None else src, dst_ref=rows(*block),
                send_sem=send_sems.at[k], recv_sem=recv_sems.at[k], device_id=to, device_id_type=MESH)

        mine = pltpu.make_async_copy(x_ref, rows(*me), local_sem)
        first = [copy(0, me, sibling, src=x_ref)] + [copy(1 + j, me, (*chip, c), src=x_ref)
                                                     for j, chip in enumerate(chips)]
        if not all_of_them:
            return mine, first
        passed = [copy(4 + j, (*chip, c), sibling) for j, chip in enumerate(chips)]
        landed = [copy(1 + j, (*chip, c), me) for j, chip in enumerate(chips)]
        from_sibling = [copy(0, sibling, me)] + [copy(4 + j, (*chip, 1 - c), me) for j, chip in enumerate(chips)]
        return mine, first, passed, landed, from_sibling

    def start(self, ins, outs, scr):
        mine, first = self._copies(ins, outs, scr, False)
        mine.start()
        for cp in first:
            cp.start()

    def finish(self, ins, outs, scr):
        mine, first, passed, landed, from_sibling = self._copies(ins, outs, scr, True)
        for cp_in, cp_on in zip(landed, passed):
            cp_in.wait_recv()
            cp_on.start()
        for cp in from_sibling:
            cp.wait_recv()
        for cp in first + passed:
            cp.wait_send()
        mine.wait()


class _ReduceScatterSend:
    def __init__(self, grads):
        grads = [g if isinstance(g, tuple) else (g, 0, g.shape[0] // NDEV) for g in grads]
        self.nw = len(grads)
        self.base = [b for _, b, _ in grads]
        self.rows = [r for _, _, r in grads]
        self.ins = [g for g, _, _ in grads]
        self.out_shape = [jax.ShapeDtypeStruct((NDEV, r, g.shape[1]), g.dtype) for g, _, r in grads]
        self.scratch = [DMA((self.nw, NDEV - 1)), DMA((self.nw, NDEV - 1)), DMA((self.nw,))]

    def _copies(self, g_refs, r_refs, scr, want):
        send_sems, recv_sems, local_sems = scr
        x, y, c = lax.axis_index("x"), lax.axis_index("y"), lax.axis_index("c")
        me = 4 * x + 2 * y + c
        rows, base = self.rows, self.base
        out = []
        if want == "local":
            for w in range(self.nw):
                out.append(pltpu.make_async_copy(g_refs[w].at[pl.ds(base[w] + me * rows[w], rows[w])],
                                                 r_refs[w].at[me], local_sems.at[w]))
            return out
        for k in range(1, NDEV):
            px, py, pc = x ^ (k >> 2), y ^ ((k >> 1) & 1), c ^ (k & 1)
            peer = 4 * px + 2 * py + pc
            for w in range(self.nw):
                sems = dict(send_sem=send_sems.at[w, k - 1], recv_sem=recv_sems.at[w, k - 1],
                            device_id=(px, py, pc), device_id_type=MESH)
                if want == "send":
                    out.append(pltpu.make_async_remote_copy(
                        src_ref=g_refs[w].at[pl.ds(base[w] + peer * rows[w], rows[w])], dst_ref=r_refs[w].at[me],
                        **sems))
                else:
                    out.append(pltpu.make_async_remote_copy(
                        src_ref=g_refs[w].at[pl.ds(0, rows[w])], dst_ref=r_refs[w].at[peer], **sems))
        return out

    def start(self, ins, outs, scr):
        for cp in self._copies(ins, outs, scr, "local") + self._copies(ins, outs, scr, "send"):
            cp.start()

    def finish(self, ins, outs, scr):
        for cp in self._copies(ins, outs, scr, "recv"):
            cp.wait_recv()
        for cp in self._copies(ins, outs, scr, "send"):
            cp.wait_send()
        for cp in self._copies(ins, outs, scr, "local"):
            cp.wait()


def _call(body, *, name, grid, ins, in_specs, out_shape, out_specs, scratch=(), vmem=None, carries=()):
    n_in, n_out, n_scr = len(ins), len(out_shape), len(scratch)
    ng = len(grid)

    def split(refs):
        pos = [0]

        def take(k):
            part = refs[pos[0]:pos[0] + k]
            pos[0] += k
            return part

        i_refs = take(n_in)
        c_in = [take(len(c.ins)) for c in carries]
        o_refs = take(n_out)
        c_out = [take(len(c.out_shape)) for c in carries]
        s_refs = take(n_scr)
        c_scr = [take(len(c.scratch)) for c in carries]
        return i_refs, o_refs, s_refs, list(zip(carries, c_in, c_out, c_scr))

    def full(*refs):
        i_refs, o_refs, s_refs, cparts = split(refs)
        if ng == 0:
            for c, a, b, s in cparts:
                c.start(a, b, s)
            for c, a, b, s in cparts:
                c.finish(a, b, s)
            return
        ids = [pl.program_id(a) for a in range(ng)]
        if cparts:
            @pl.when(functools.reduce(operator.and_, [i == 0 for i in ids]))
            def _():
                for c, a, b, s in cparts:
                    c.start(a, b, s)

        body(*i_refs, *o_refs, *s_refs)
        if cparts:
            @pl.when(functools.reduce(operator.and_, [i == g - 1 for i, g in zip(ids, grid)]))
            def _():
                for c, a, b, s in cparts:
                    c.finish(a, b, s)

    all_ins = list(ins) + [a for c in carries for a in c.ins]
    all_in_specs = list(in_specs) + [ANY for c in carries for _ in c.ins]
    all_out_shape = list(out_shape) + [s for c in carries for s in c.out_shape]
    all_out_specs = list(out_specs) + [ANY for c in carries for _ in c.out_shape]
    all_scratch = list(scratch) + [s for c in carries for s in c.scratch]
    kwargs = dict(grid=grid) if ng else {}
    outs = pl.pallas_call(
        full, name=name, in_specs=all_in_specs, out_specs=all_out_specs, out_shape=all_out_shape,
        scratch_shapes=all_scratch,
        compiler_params=pltpu.CompilerParams(dimension_semantics=("arbitrary",) * ng if ng else None,
                                             vmem_limit_bytes=vmem),
        **kwargs)(*all_ins)
    outs = list(outs)
    res, pos = outs[:n_out], n_out
    carried = []
    for c in carries:
        carried.append(outs[pos:pos + len(c.out_shape)])
        pos += len(c.out_shape)
    return res, carried


def _groups_of(names):
    out = []
    for n in names:
        if LOC[n][0] not in out:
            out.append(LOC[n][0])
    return out


def _weight_pieces(name):
    g, off, rows = LOC[name]
    return [(d * GROUP_ROWS[g] + off, d * rows, rows) for d in range(NDEV)]


def _win_pieces():
    kv = N_KV * HEAD_DIM
    pieces = [(0, 0, D)]
    for g0, d0 in ((D, D), (D + kv, D + N_KV * LANES)):
        for g in range(N_KV):
            for half in range(2):
                pieces.append((g0 + g * HEAD_DIM, d0 + g * LANES + half * HEAD_DIM, HEAD_DIM))
    pieces.append((D + 2 * kv, D + 2 * N_KV * LANES, D))
    pieces.append((D + 2 * kv + D, D + 2 * N_KV * LANES + D, D))
    grp, off, rows = LOC["w_in"]
    out = []
    for g0, d0, n in pieces:
        while n > 0:
            dev, loc = divmod(g0, rows)
            m = min(n, rows - loc)
            out.append((dev * GROUP_ROWS[grp] + off + loc, d0, m))
            g0, d0, n = g0 + m, d0 + m, n - m
    return out


def _start_loads(src_ref, dst_ref, pieces, sems, base):
    cps = []
    for j, (s, d, n) in enumerate(pieces):
        cp = pltpu.make_async_copy(src_ref.at[pl.ds(s, n)], dst_ref.at[pl.ds(d, n)], sems.at[base + j])
        cp.start()
        cps.append(cp)
    return cps


def _load_weights(wrefs, targets, sems):
    cps, base = [], 0
    for name, dst in targets:
        pieces = _win_pieces() if name == "w_in" else _weight_pieces(name)
        cps += _start_loads(wrefs[LOC[name][0]], dst, pieces, sems, base)
        base += len(pieces)
    for cp in cps:
        cp.wait()


def _n_pieces(names):
    return sum(len(_win_pieces()) if n == "w_in" else NDEV for n in names)


def _dw(lhs, rhs, chunk, name, carries=()):
    t_tok, c = lhs.shape
    nq = 4 if t_tok % (4 * 16) == 0 and chunk <= D // 2 else 1
    tq = t_tok // nq

    def body(lhs_ref, rhs_ref, out_ref, rhs_s, sems):
        def copies():
            return [pltpu.make_async_copy(rhs_ref.at[pl.ds(k * tq, tq)], rhs_s.at[pl.ds(k * tq, tq)], sems.at[k])
                    for k in range(nq)]

        first = pl.program_id(0) == 0
        if nq == 1:
            @pl.when(first)
            def _():
                for cp in copies():
                    cp.start()
                    cp.wait()

            out_ref[...] = _tn(lhs_ref[...], rhs_s[...]).astype(BF16)
            return

        @pl.when(first)
        def _():
            cps = copies()
            for cp in cps:
                cp.start()
            acc = None
            for k, cp in enumerate(cps):
                cp.wait()
                part = _tn(lhs_ref[pl.ds(k * tq, tq), :], rhs_s[pl.ds(k * tq, tq), :])
                acc = part if acc is None else acc + part
            out_ref[...] = acc.astype(BF16)

        @pl.when(jnp.logical_not(first))
        def _():
            out_ref[...] = _tn(lhs_ref[...], rhs_s[...]).astype(BF16)

    (out,), carried = _call(
        body, name=name, grid=(c // chunk,), ins=[lhs, rhs],
        in_specs=[pl.BlockSpec((t_tok, chunk), lambda i: (0, i)), ANY],
        out_specs=[pl.BlockSpec((chunk, D), lambda i: (i, 0))],
        out_shape=[jax.ShapeDtypeStruct((c, D), BF16)],
        scratch=[pltpu.VMEM((t_tok, D), BF16), DMA((nq,))], vmem=VMEM_BIG, carries=carries)
    return out, carried


def _silu_parts(a):
    sig = jax.nn.sigmoid(a)
    return sig, a * sig


FC = 256


def _ffn_fwd(h, gpre, gpost, wg, names, target=None, tm=512, carries=()):
    t_tok = h.shape[0]
    nt = t_tok // tm
    with_loss = target is not None
    groups = _groups_of(names)

    def body(*refs):
        refs = list(refs)
        h_ref, gpre_ref, gpost_ref = refs[:3]
        del refs[:3]
        tgt_ref = refs.pop(0) if with_loss else None
        wrefs = dict(zip(groups, refs[:len(groups)]))
        del refs[:len(groups)]
        hout_ref, a_ref, b_ref, f_ref, nb_ref = refs[:5]
        del refs[:5]
        if with_loss:
            dy_ref, loss_ref = refs[:2]
            del refs[:2]
        w1_s, w3_s, w2_s, sems = refs
        i = pl.program_id(0)

        @pl.when(i == 0)
        def _():
            _load_weights(wrefs, list(zip(names, (w1_s, w3_s, w2_s))), sems)
            if with_loss:
                loss_ref[...] = jnp.zeros_like(loss_ref)

        x = h_ref[...]
        n, _, _ = _rms_fwd(x, gpre_ref[...])
        nb = n.astype(BF16)
        nb_ref[...] = nb
        f = jnp.zeros((tm, D), F32)
        for c0 in range(0, FF, FC):
            a = _nt(nb, w1_s[pl.ds(c0, FC), :])
            b = _nt(nb, w3_s[pl.ds(c0, FC), :])
            _, sl = _silu_parts(a)
            a_ref[:, pl.ds(c0, FC)] = a.astype(BF16)
            b_ref[:, pl.ds(c0, FC)] = b.astype(BF16)
            f = f + _nn((sl * b).astype(BF16), w2_s[pl.ds(c0, FC), :])
        f_ref[...] = f
        fn, _, _ = _rms_fwd(f, gpost_ref[...])
        y = x + 0.5 * fn
        hout_ref[...] = y
        if with_loss:
            err = y - tgt_ref[...]
            dy_ref[...] = err * (1.0 / D)
            loss_ref[...] += jnp.sum(jnp.sum(err * err, axis=-1, keepdims=True), axis=0, keepdims=True) * (0.5 / D)

    ins = [h, gpre, gpost] + ([target] if with_loss else []) + [wg[g] for g in groups]
    in_specs = [_row_spec(tm, D), _const_spec((1, D)), _const_spec((1, D))]
    in_specs += ([_row_spec(tm, D)] if with_loss else []) + [ANY] * len(groups)
    out_shape = [jax.ShapeDtypeStruct((t_tok, D), F32), jax.ShapeDtypeStruct((t_tok, FF), BF16),
                 jax.ShapeDtypeStruct((t_tok, FF), BF16), jax.ShapeDtypeStruct((t_tok, D), F32),
                 jax.ShapeDtypeStruct((t_tok, D), BF16)]
    out_specs = [_row_spec(tm, D), _row_spec(tm, FF), _row_spec(tm, FF), _row_spec(tm, D), _row_spec(tm, D)]
    if with_loss:
        out_shape += [jax.ShapeDtypeStruct((t_tok, D), F32), jax.ShapeDtypeStruct((1, LANES), F32)]
        out_specs += [_row_spec(tm, D), _const_spec((1, LANES))]
    return _call(body, name="ffn_fwd_" + names[0][:4], grid=(nt,), ins=ins, in_specs=in_specs,
                 out_shape=out_shape, out_specs=out_specs,
                 scratch=[pltpu.VMEM((FF, D), BF16)] * 3 + [DMA((3 * NDEV,))], vmem=VMEM_BIG, carries=carries)


def _ffn_bwd_a(dh, f, a, b, gpost, wg, name_w2, tm=512, carries=()):
    t_tok = dh.shape[0]
    nt = t_tok // tm
    groups = _groups_of([name_w2])

    def body(dh_ref, f_ref, a_ref, b_ref, gpost_ref, wg_ref, dab_ref, s_ref, df_ref, dgp_ref, w2_s, sems):
        i = pl.program_id(0)

        @pl.when(i == 0)
        def _():
            _load_weights({groups[0]: wg_ref}, [(name_w2, w2_s)], sems)
            dgp_ref[...] = jnp.zeros_like(dgp_ref)

        fv = f_ref[...]
        _, fh, r = _rms_fwd(fv, gpost_ref[...])
        df, dg = _rms_bwd(0.5 * dh_ref[...], fh, r, gpost_ref[...])
        dgp_ref[...] += dg
        dfb = df.astype(BF16)
        df_ref[...] = dfb
        for c0 in range(0, FF, FC):
            ds = _nt(dfb, w2_s[pl.ds(c0, FC), :])
            av = a_ref[:, pl.ds(c0, FC)].astype(F32)
            bv = b_ref[:, pl.ds(c0, FC)].astype(F32)
            sig, sl = _silu_parts(av)
            dab_ref[:, pl.ds(c0, FC)] = (ds * bv * (sig * (1.0 + av * (1.0 - sig)))).astype(BF16)
            dab_ref[:, pl.ds(FF + c0, FC)] = (ds * sl).astype(BF16)
            s_ref[:, pl.ds(c0, FC)] = (sl * bv).astype(BF16)

    return _call(
        body, name="ffn_bwd_a_" + name_w2[:4], grid=(nt,), ins=[dh, f, a, b, gpost, wg[groups[0]]],
        in_specs=[_row_spec(tm, D), _row_spec(tm, D), _row_spec(tm, FF), _row_spec(tm, FF), _const_spec((1, D)), ANY],
        out_specs=[_row_spec(tm, 2 * FF), _row_spec(tm, FF), _row_spec(tm, D), _const_spec((1, D))],
        out_shape=[jax.ShapeDtypeStruct((t_tok, 2 * FF), BF16), jax.ShapeDtypeStruct((t_tok, FF), BF16),
                   jax.ShapeDtypeStruct((t_tok, D), BF16), jax.ShapeDtypeStruct((1, D), F32)],
        scratch=[pltpu.VMEM((FF, D), BF16), DMA((NDEV,))],
        vmem=VMEM_BIG, carries=carries)


def _ffn_bwd_dx(dab, h, dh, gpre, wg, name_w1, name_w3, tm=512, carries=()):
    t_tok = dh.shape[0]
    nt = t_tok // tm
    groups = _groups_of([name_w1, name_w3])

    def body(*refs):
        dab_ref, h_ref, dh_ref, gpre_ref = refs[:4]
        wrefs = dict(zip(groups, refs[4:4 + len(groups)]))
        dhin_ref, dgp_ref, w13_s, sems = refs[4 + len(groups):]
        i = pl.program_id(0)

        @pl.when(i == 0)
        def _():
            _load_weights(wrefs, [(name_w1, w13_s.at[pl.ds(0, FF)]), (name_w3, w13_s.at[pl.ds(FF, FF)])], sems)
            dgp_ref[...] = jnp.zeros_like(dgp_ref)

        g = gpre_ref[...]
        _, xh, r = _rms_fwd(h_ref[...], g)
        dn = _nn(dab_ref[...], w13_s[...])
        dx, dg = _rms_bwd(dn, xh, r, g)
        dgp_ref[...] += dg
        dhin_ref[...] = dh_ref[...] + dx

    return _call(
        body, name="ffn_bwd_dx_" + name_w1[:4], grid=(nt,), ins=[dab, h, dh, gpre] + [wg[g] for g in groups],
        in_specs=[_row_spec(tm, 2 * FF), _row_spec(tm, D), _row_spec(tm, D), _const_spec((1, D))]
        + [ANY] * len(groups),
        out_specs=[_row_spec(tm, D), _const_spec((1, D))],
        out_shape=[jax.ShapeDtypeStruct((t_tok, D), F32), jax.ShapeDtypeStruct((1, D), F32)],
        scratch=[pltpu.VMEM((2 * FF, D), BF16), DMA((2 * NDEV,))],
        vmem=VMEM_BIG, carries=carries)


Q0, K0, V0, XR0, XG0 = 0, D, D + N_KV * LANES, D + 2 * N_KV * LANES, 2 * D + 2 * N_KV * LANES


def _mix_proj_fwd(h, g, bgate, wg, tm=512, carries=()):
    t_tok = h.shape[0]
    nt = t_tok // tm
    names = ("w_in", "w_gate")
    groups = _groups_of(names)

    def body(h_ref, g_ref, bg_ref, wg_ref, q_ref, k_ref, v_ref, xr_ref, xg_ref, gs_ref, ub_ref, win_s, wgt_s, sems):
        i = pl.program_id(0)

        @pl.when(i == 0)
        def _():
            _load_weights({groups[0]: wg_ref}, [("w_in", win_s), ("w_gate", wgt_s)], sems)

        n, _, _ = _rms_fwd(h_ref[...], g_ref[...])
        nb = n.astype(BF16)
        ub_ref[...] = nb
        q_ref[...] = _nt(nb, win_s[pl.ds(Q0, D), :]).astype(BF16)
        k_ref[...] = _nt(nb, win_s[pl.ds(K0, N_KV * LANES), :]).astype(BF16)
        v_ref[...] = _nt(nb, win_s[pl.ds(V0, N_KV * LANES), :]).astype(BF16)
        xr_ref[...] = _nt(nb, win_s[pl.ds(XR0, D), :])
        xg_ref[...] = _nt(nb, win_s[pl.ds(XG0, D), :])
        gs_ref[...] = jax.nn.sigmoid(_nt(nb, wgt_s[...]) + bg_ref[...])

    kvw = N_KV * LANES
    return _call(
        body, name="mix_proj_fwd", grid=(nt,), ins=[h, g, bgate, wg[groups[0]]],
        in_specs=[_row_spec(tm, D), _const_spec((1, D)), _const_spec((1, 2 * D)), ANY],
        out_specs=[_row_spec(tm, D), _row_spec(tm, kvw), _row_spec(tm, kvw), _row_spec(tm, D), _row_spec(tm, D),
                   _row_spec(tm, 2 * D), _row_spec(tm, D)],
        out_shape=[jax.ShapeDtypeStruct((t_tok, D), BF16), jax.ShapeDtypeStruct((t_tok, kvw), BF16),
                   jax.ShapeDtypeStruct((t_tok, kvw), BF16), jax.ShapeDtypeStruct((t_tok, D), F32),
                   jax.ShapeDtypeStruct((t_tok, D), F32), jax.ShapeDtypeStruct((t_tok, 2 * D), F32),
                   jax.ShapeDtypeStruct((t_tok, D), BF16)],
        scratch=[pltpu.VMEM((INP_W, D), BF16), pltpu.VMEM((2 * D, D), BF16), DMA((_n_pieces(names),))],
        vmem=VMEM_BIG, carries=carries)


def _mix_proj_bwd(dqkv, dxr, dxg, dgpre, h, dh, g, wg, tm=512, carries=()):
    t_tok = h.shape[0]
    nt = t_tok // tm
    names = ("w_in", "w_gate")
    groups = _groups_of(names)

    def body(dqkv_ref, dxr_ref, dxg_ref, dgp_ref, h_ref, dh_ref, g_ref, wg_ref, dhin_ref, dg_ref, win_s, wgt_s, sems):
        i = pl.program_id(0)

        @pl.when(i == 0)
        def _():
            _load_weights({groups[0]: wg_ref}, [("w_in", win_s), ("w_gate", wgt_s)], sems)
            dg_ref[...] = jnp.zeros_like(dg_ref)

        gv = g_ref[...]
        _, xh, r = _rms_fwd(h_ref[...], gv)
        du = _nn(dgp_ref[...], wgt_s[...])
        du = du + _nn(dqkv_ref[...], win_s[pl.ds(Q0, XR0), :])
        du = du + _nn(dxr_ref[...], win_s[pl.ds(XR0, D), :])
        du = du + _nn(dxg_ref[...], win_s[pl.ds(XG0, D), :])
        dx, dg = _rms_bwd(du, xh, r, gv)
        dg_ref[...] += dg
        dhin_ref[...] = dh_ref[...] + dx

    return _call(
        body, name="mix_proj_bwd", grid=(nt,), ins=[dqkv, dxr, dxg, dgpre, h, dh, g, wg[groups[0]]],
        in_specs=[_row_spec(tm, XR0), _row_spec(tm, D), _row_spec(tm, D), _row_spec(tm, 2 * D), _row_spec(tm, D),
                  _row_spec(tm, D), _const_spec((1, D)), ANY],
        out_specs=[_row_spec(tm, D), _const_spec((1, D))],
        out_shape=[jax.ShapeDtypeStruct((t_tok, D), F32), jax.ShapeDtypeStruct((1, D), F32)],
        scratch=[pltpu.VMEM((INP_W, D), BF16), pltpu.VMEM((2 * D, D), BF16), DMA((_n_pieces(names),))],
        vmem=VMEM_BIG, carries=carries)


def _shift_down(x, d, fill):
    row = lax.broadcasted_iota(jnp.int32, x.shape, 0)
    return jnp.where(row >= d, pltpu.roll(x, d, 0), fill)


def _shift_up(x, d, fill):
    rows = x.shape[0]
    row = lax.broadcasted_iota(jnp.int32, x.shape, 0)
    return jnp.where(row < rows - d, pltpu.roll(x, rows - d, 0), fill)


def _scan_rows(a, b, reverse):
    rows = a.shape[0]
    d = 1
    while d < rows:
        if d < 8:
            shift = _shift_up if reverse else _shift_down
            b = a * shift(b, d, 0.0) + b
            a = a * shift(a, d, 1.0)
        elif reverse:
            b = jnp.concatenate([a[:rows - d] * b[d:] + b[:rows - d], b[rows - d:]], axis=0)
            a = jnp.concatenate([a[:rows - d] * a[d:], a[rows - d:]], axis=0)
        else:
            b = jnp.concatenate([b[:d], a[d:] * b[:rows - d] + b[d:]], axis=0)
            a = jnp.concatenate([a[:d], a[d:] * a[:rows - d]], axis=0)
        d *= 2
    return a, b


def _expm1(x):
    small = x * (1.0 + x * (0.5 + x * (1.0 / 6.0 + x * (1.0 / 24.0))))
    return jnp.where(jnp.abs(x) < 0.05, small, jnp.exp(x) - 1.0)


def _softplus(x):
    return jnp.maximum(x, 0.0) + jnp.log(1.0 + jnp.exp(-jnp.abs(x)))


_GELU_C = math.sqrt(2.0 / math.pi)


def _gelu_parts(x):
    th = jnp.tanh(_GELU_C * (x + 0.044715 * x * x * x))
    val = 0.5 * x * (1.0 + th)
    grad = 0.5 * (1.0 + th) + 0.5 * x * (1.0 - th * th) * _GELU_C * (1.0 + 3.0 * 0.044715 * x * x)
    return val, grad


def _lru_pre(x, halo, cw_ref, cb_ref, wa_ref, wx_ref, ba_ref, bx_ref, lam_ref):
    ext = jnp.concatenate([halo, x], axis=0)
    shifted = [x] + [pltpu.roll(ext, k, 0)[8:] for k in (1, 2, 3)]
    xc = cb_ref[...] + cw_ref[pl.ds(CONV_WIDTH - 1, 1), :] * x
    for k in (1, 2, 3):
        xc = xc + cw_ref[pl.ds(CONV_WIDTH - 1 - k, 1), :] * shifted[k]
    xcb = xc.astype(BF16)
    r = jax.nn.sigmoid(_nn(xcb, wa_ref[...]) + ba_ref[...])
    ig = jax.nn.sigmoid(_nn(xcb, wx_ref[...]) + bx_ref[...])
    sp = _softplus(-lam_ref[...])
    log_a = -LRU_C * r * sp
    a = jnp.exp(log_a)
    mult = jnp.sqrt(-_expm1(2.0 * log_a))
    return shifted, xc, xcb, r, ig, sp, a, mult


def _lru_specs(nt, reverse):
    def tt(t):
        return nt - 1 - t if reverse else t
    tile = pl.BlockSpec((LRU_ROWS, LANES), lambda cb, t: (tt(t), cb))
    halo = pl.BlockSpec((8, LANES), lambda cb, t: (jnp.maximum(tt(t) * (LRU_ROWS // 8) - 1, 0), cb))
    vec = pl.BlockSpec((1, LANES), lambda cb, t: (0, cb))
    cw = pl.BlockSpec((CONV_WIDTH, LANES), lambda cb, t: (0, cb))
    mat = pl.BlockSpec((None, LANES, LANES), lambda cb, t: (cb, 0, 0))
    return tile, halo, vec, cw, mat


def _lru_fwd(xr, xg, cw, cb, wa, wx, ba, bx, lam, carries=()):
    t_tok = xr.shape[0]
    nt = t_tok // LRU_ROWS
    rows = LRU_ROWS

    def body(xr_ref, xg_ref, cw_ref, cb_ref, wa_ref, wx_ref, ba_ref, bx_ref, lam_ref, y_ref, h_ref, tail_s, hc_s):
        t = pl.program_id(1)

        @pl.when(t == 0)
        def _():
            tail_s[...] = jnp.zeros_like(tail_s)
            hc_s[...] = jnp.zeros_like(hc_s)

        x = xr_ref[...]
        _, xc, _, _, ig, _, a, mult = _lru_pre(x, tail_s[...], cw_ref, cb_ref, wa_ref, wx_ref, ba_ref, bx_ref, lam_ref)
        tail_s[...] = xr_ref[pl.ds(rows - 8, 8), :]
        acc_a, acc_b = _scan_rows(a, mult * (ig * xc), False)
        hv = acc_b + acc_a * hc_s[...]
        h_ref[...] = hv
        hc_s[...] = h_ref[pl.ds(rows - 1, 1), :]
        gl, _ = _gelu_parts(xg_ref[...])
        y_ref[...] = (hv * gl).astype(BF16)

    tile, _, vec, cws, mat = _lru_specs(nt, False)
    return _call(
        body, name="lru_fwd", grid=(D // LANES, nt), ins=[xr, xg, cw, cb, wa, wx, ba, bx, lam],
        in_specs=[tile, tile, cws, vec, mat, mat, vec, vec, vec],
        out_specs=[tile, tile],
        out_shape=[jax.ShapeDtypeStruct((t_tok, D), BF16), jax.ShapeDtypeStruct((t_tok, D), F32)],
        scratch=[pltpu.VMEM((8, LANES), F32), pltpu.VMEM((1, LANES), F32)], carries=carries)


def _lru_bwd(dy, xr, xg, hseq, cw, cb, wa, wx, ba, bx, lam, carries=()):
    t_tok = xr.shape[0]
    nt = t_tok // LRU_ROWS
    rows = LRU_ROWS

    def body(dy_ref, xr_ref, xrh_ref, xg_ref, h_ref, hh_ref, cw_ref, cb_ref, wa_ref, wx_ref, ba_ref, bx_ref, lam_ref,
             dxr_ref, dxg_ref, dvec_ref, dwa_ref, dwx_ref, gcar_s, acar_s, head_s, tmp_s):
        t = pl.program_id(1)
        first_tile = t == nt - 1

        @pl.when(t == 0)
        def _():
            gcar_s[...] = jnp.zeros_like(gcar_s)
            acar_s[...] = jnp.zeros_like(acar_s)
            head_s[...] = jnp.zeros_like(head_s)
            dvec_ref[...] = jnp.zeros_like(dvec_ref)
            dwa_ref[...] = jnp.zeros_like(dwa_ref)
            dwx_ref[...] = jnp.zeros_like(dwx_ref)

        x = xr_ref[...]
        halo = jnp.where(first_tile, 0.0, xrh_ref[...])
        shifted, xc, xcb, r, ig, sp, a, mult = _lru_pre(x, halo, cw_ref, cb_ref, wa_ref, wx_ref, ba_ref, bx_ref, lam_ref)
        hv = h_ref[...]
        dyv = dy_ref[...]
        gl, glg = _gelu_parts(xg_ref[...])
        dxg_ref[...] = (dyv * hv * glg).astype(BF16)
        acc_a, acc_b = _scan_rows(_shift_up(a, 1, acar_s[...]), dyv * gl, True)
        g = acc_b + acc_a * gcar_s[...]
        hhalo = jnp.where(first_tile, 0.0, hh_ref[...])
        hprev = pltpu.roll(jnp.concatenate([hhalo, hv], axis=0), 1, 0)[8:]
        dmult = g * ig * xc
        dlog_a = a * (g * hprev) - dmult * a * a / mult
        dig = g * mult * xc
        dxc = g * mult * ig
        dzr = (dlog_a * (-LRU_C * sp)) * r * (1.0 - r)
        dzx = dig * ig * (1.0 - ig)
        dzrb, dzxb = dzr.astype(BF16), dzx.astype(BF16)
        dxc = dxc + _nt(dzrb, wa_ref[...]) + _nt(dzxb, wx_ref[...])
        dwa_ref[...] += _tn(xcb, dzrb)
        dwx_ref[...] += _tn(xcb, dzxb)
        dsp = jnp.sum(dlog_a * (-LRU_C * r), axis=0, keepdims=True)
        dlam = dsp * (-jax.nn.sigmoid(-lam_ref[...]))
        vrow = lax.broadcasted_iota(jnp.int32, (8, LANES), 0)
        upd = jnp.where(vrow == 4, jnp.sum(dxc, axis=0, keepdims=True), 0.0)
        upd = jnp.where(vrow == 5, jnp.sum(dzr, axis=0, keepdims=True), upd)
        upd = jnp.where(vrow == 6, jnp.sum(dzx, axis=0, keepdims=True), upd)
        upd = jnp.where(vrow == 7, dlam, upd)
        for k in range(CONV_WIDTH):
            upd = jnp.where(vrow == CONV_WIDTH - 1 - k, jnp.sum(dxc * shifted[k], axis=0, keepdims=True), upd)
        dvec_ref[...] += upd
        ext = jnp.concatenate([dxc, head_s[...]], axis=0)
        dxr = cw_ref[pl.ds(CONV_WIDTH - 1, 1), :] * dxc
        for k in (1, 2, 3):
            dxr = dxr + cw_ref[pl.ds(CONV_WIDTH - 1 - k, 1), :] * pltpu.roll(ext, rows + 8 - k, 0)[:rows]
        dxr_ref[...] = dxr.astype(BF16)
        tmp_s[...] = g
        gcar_s[...] = tmp_s[pl.ds(0, 1), :]
        tmp_s[...] = a
        acar_s[...] = tmp_s[pl.ds(0, 1), :]
        tmp_s[...] = dxc
        head_s[...] = tmp_s[pl.ds(0, 8), :]

    tile, halo, vec, cws, mat = _lru_specs(nt, True)
    return _call(
        body, name="lru_bwd", grid=(D // LANES, nt), ins=[dy, xr, xr, xg, hseq, hseq, cw, cb, wa, wx, ba, bx, lam],
        in_specs=[tile, tile, halo, tile, tile, halo, cws, vec, mat, mat, vec, vec, vec],
        out_specs=[tile, tile, pl.BlockSpec((8, LANES), lambda cb, t: (0, cb)), mat, mat],
        out_shape=[jax.ShapeDtypeStruct((t_tok, D), BF16), jax.ShapeDtypeStruct((t_tok, D), BF16),
                   jax.ShapeDtypeStruct((8, D), F32), jax.ShapeDtypeStruct((D // LANES, LANES, LANES), F32),
                   jax.ShapeDtypeStruct((D // LANES, LANES, LANES), F32)],
        scratch=[pltpu.VMEM((1, LANES), F32), pltpu.VMEM((1, LANES), F32), pltpu.VMEM((8, LANES), F32),
                 pltpu.VMEM((rows, LANES), F32)], carries=carries)


def _t5_bucket_np(rel):
    nb = N_BUCKETS // 2
    max_exact = nb // 2
    ret = np.where(rel > 0, nb, 0)
    n = np.abs(rel)
    nf = np.maximum(n, 1).astype(np.float32)
    large = max_exact + (np.log(nf / np.float32(max_exact)) / np.float32(math.log(MAX_DISTANCE / max_exact))
                         * np.float32(nb - max_exact)).astype(np.int32)
    large = np.minimum(large, nb - 1)
    return ret + np.where(n < max_exact, n, large)


def _bucket_map():
    r = np.arange(QT)[:, None]
    c = np.arange(KW)[None, :]
    j = c - (r // CHUNK) * CHUNK
    band = (j >= 0) & (j < WINDOW + CHUNK)
    return np.where(band, _t5_bucket_np(c - r - WINDOW), -1).astype(np.int32)


def _attn_specs(nt, reverse):
    def tt(i):
        return nt - 1 - i if reverse else i
    kvw = N_KV * LANES
    qs = pl.BlockSpec((QT, D), lambda i: (tt(i), 0))
    cur = pl.BlockSpec((QT, kvw), lambda i: (tt(i), 0))
    prev = pl.BlockSpec((WINDOW, kvw), lambda i: (jnp.maximum(tt(i) * (QT // WINDOW) - 1, 0), 0))
    lse = pl.BlockSpec((QT, LANES), lambda i: (tt(i), 0))
    return qs, cur, prev, lse


REP = N_HEADS // N_KV
SCALE = HEAD_DIM ** -0.5


def _stack_heads(x_ref, g, lo, scale=None):
    parts = []
    for hl in range(REP):
        xs = x_ref[:, pl.ds((2 * g + hl // 2) * LANES, LANES)]
        xs = jnp.where(lo if hl % 2 == 0 else jnp.logical_not(lo), xs, jnp.zeros_like(xs))
        parts.append(xs if scale is None else xs * jnp.asarray(scale, xs.dtype))
    return jnp.concatenate(parts, axis=0)


def _stack_sinks(sink_ref, g, srow):
    sk = jnp.full(srow.shape, sink_ref[REP * g + REP - 1], F32)
    for hl in range(REP - 2, -1, -1):
        sk = jnp.where(srow < (hl + 1) * QT, sink_ref[REP * g + hl], sk)
    return sk


def _attn_fwd(q, kd, vd, bias, sinks, carries=()):
    t_tok = q.shape[0]
    nt = t_tok // QT

    def body(q_ref, kp_ref, kc_ref, vp_ref, vc_ref, bias_ref, sink_ref, o_ref, lse_ref):
        i = pl.program_id(0)
        col = lax.broadcasted_iota(jnp.int32, (1, KW), 1)
        first = jnp.where((i == 0) & (col < WINDOW), NEG_INF, 0.0)
        lane = lax.broadcasted_iota(jnp.int32, (QT, LANES), 1)
        lo = lane < HEAD_DIM
        srow = lax.broadcasted_iota(jnp.int32, (REP * QT, 1), 0)
        lse_t = jnp.zeros((QT, LANES), F32)
        for g in range(N_KV):
            kwin = jnp.concatenate([kp_ref[:, pl.ds(g * LANES, LANES)], kc_ref[:, pl.ds(g * LANES, LANES)]], axis=0)
            vwin = jnp.concatenate([vp_ref[:, pl.ds(g * LANES, LANES)], vc_ref[:, pl.ds(g * LANES, LANES)]], axis=0)
            qst = _stack_heads(q_ref, g, lo, SCALE)
            s = _nt(qst, kwin) + (bias_ref[g] + first)
            sk = _stack_sinks(sink_ref, g, srow)
            m = jnp.maximum(jnp.max(s, axis=-1, keepdims=True), sk)
            e = jnp.exp(s - m)
            l = jnp.sum(e, axis=-1, keepdims=True) + jnp.exp(sk - m)
            p = e / l
            ost = _nn(p.astype(BF16), vwin)
            lse_s = m + jnp.log(l)
            for sl in range(2):
                o_ref[:, pl.ds((2 * g + sl) * LANES, LANES)] = jnp.where(
                    lo, ost[2 * sl * QT:(2 * sl + 1) * QT], ost[(2 * sl + 1) * QT:(2 * sl + 2) * QT]).astype(BF16)
            for hl in range(REP):
                lse_t = jnp.where(lane == REP * g + hl, lse_s[hl * QT:(hl + 1) * QT], lse_t)
        lse_ref[...] = lse_t

    qs, cur, prev, lse = _attn_specs(nt, False)
    return _call(
        body, name="attn_fwd", grid=(nt,), ins=[q, kd, kd, vd, vd, bias, sinks],
        in_specs=[qs, prev, cur, prev, cur, _const_spec((N_KV, REP * QT, KW)), pl.BlockSpec(memory_space=pltpu.SMEM)],
        out_specs=[qs, lse],
        out_shape=[jax.ShapeDtypeStruct((t_tok, D), BF16), jax.ShapeDtypeStruct((t_tok, LANES), F32)],
        vmem=48 * 2 ** 20, carries=carries)


def _attn_bwd(q, kd, vd, o, do, lse, bias, sinks, carries=()):
    t_tok = q.shape[0]
    nt = t_tok // QT
    kvw = N_KV * LANES

    def body(q_ref, kp_ref, kc_ref, vp_ref, vc_ref, o_ref, do_ref, lse_ref, bias_ref, sink_ref,
             dqkv_ref, ds_ref, dsink_ref, kcar_s, vcar_s):
        i = pl.program_id(0)
        tile = nt - 1 - i

        @pl.when(i == 0)
        def _():
            kcar_s[...] = jnp.zeros_like(kcar_s)
            vcar_s[...] = jnp.zeros_like(vcar_s)
            ds_ref[...] = jnp.zeros_like(ds_ref)
            dsink_ref[...] = jnp.zeros_like(dsink_ref)

        col = lax.broadcasted_iota(jnp.int32, (1, KW), 1)
        first = jnp.where((tile == 0) & (col < WINDOW), NEG_INF, 0.0)
        lane = lax.broadcasted_iota(jnp.int32, (QT, LANES), 1)
        lo = lane < HEAD_DIM
        lane_k = lax.broadcasted_iota(jnp.int32, (KW, LANES), 1)
        lane_1 = lax.broadcasted_iota(jnp.int32, (1, LANES), 1)
        srow = lax.broadcasted_iota(jnp.int32, (REP * QT, 1), 0)
        lse_t = lse_ref[...]
        dsink = jnp.zeros((1, LANES), F32)
        for g in range(N_KV):
            kwin = jnp.concatenate([kp_ref[:, pl.ds(g * LANES, LANES)], kc_ref[:, pl.ds(g * LANES, LANES)]], axis=0)
            vwin = jnp.concatenate([vp_ref[:, pl.ds(g * LANES, LANES)], vc_ref[:, pl.ds(g * LANES, LANES)]], axis=0)
            qst = _stack_heads(q_ref, g, lo, SCALE)
            dost = _stack_heads(do_ref, g, lo)
            od = [do_ref[:, pl.ds((2 * g + sl) * LANES, LANES)].astype(F32)
                  * o_ref[:, pl.ds((2 * g + sl) * LANES, LANES)].astype(F32) for sl in range(2)]
            drow = jnp.concatenate([jnp.sum(jnp.where(lo if hl % 2 == 0 else jnp.logical_not(lo), od[hl // 2], 0.0),
                                            axis=-1, keepdims=True) for hl in range(REP)], axis=0)
            lse_s = jnp.concatenate([jnp.sum(jnp.where(lane == REP * g + hl, lse_t, 0.0), axis=-1, keepdims=True)
                                     for hl in range(REP)], axis=0)
            s = _nt(qst, kwin) + (bias_ref[g] + first)
            p = jnp.exp(s - lse_s)
            ds = p * (_nt(dost, vwin) - drow)
            ds_ref[g] += ds
            tsink = -(jnp.exp(_stack_sinks(sink_ref, g, srow) - lse_s) * drow)
            for hl in range(REP):
                dsink = dsink + jnp.where(lane_1 == REP * g + hl,
                                          jnp.sum(tsink[hl * QT:(hl + 1) * QT], axis=0, keepdims=True), 0.0)
            dsb = ds.astype(BF16)
            dqst = _nn(dsb, kwin) * SCALE
            for sl in range(2):
                dqkv_ref[:, pl.ds((2 * g + sl) * LANES, LANES)] = jnp.where(
                    lo, dqst[2 * sl * QT:(2 * sl + 1) * QT], dqst[(2 * sl + 1) * QT:(2 * sl + 2) * QT]).astype(BF16)
            dk_acc = _tn(dsb, qst)
            dv_acc = _tn(p.astype(BF16), dost)
            dk_f = jnp.where(lane_k < HEAD_DIM, dk_acc + pltpu.roll(dk_acc, HEAD_DIM, 1), 0.0)
            dv_f = jnp.where(lane_k < HEAD_DIM, dv_acc + pltpu.roll(dv_acc, HEAD_DIM, 1), 0.0)
            for acc, col0, car in ((dk_f, K0, kcar_s), (dv_f, V0, vcar_s)):
                cs = pl.ds(g * LANES, LANES)
                co = pl.ds(col0 + g * LANES, LANES)
                if QT > WINDOW:
                    dqkv_ref[pl.ds(0, QT - WINDOW), co] = acc[WINDOW:QT].astype(BF16)
                dqkv_ref[pl.ds(QT - WINDOW, WINDOW), co] = (acc[QT:KW] + car[:, cs]).astype(BF16)
                car[:, cs] = acc[0:WINDOW]
        dsink_ref[...] += dsink

    qs, cur, prev, lse_s = _attn_specs(nt, True)
    return _call(
        body, name="attn_bwd", grid=(nt,), ins=[q, kd, kd, vd, vd, o, do, lse, bias, sinks],
        in_specs=[qs, prev, cur, prev, cur, qs, qs, lse_s, _const_spec((N_KV, REP * QT, KW)),
                  pl.BlockSpec(memory_space=pltpu.SMEM)],
        out_specs=[pl.BlockSpec((QT, XR0), lambda i: (nt - 1 - i, 0)), _const_spec((N_KV, REP * QT, KW)),
                   _const_spec((1, LANES))],
        out_shape=[jax.ShapeDtypeStruct((t_tok, XR0), BF16), jax.ShapeDtypeStruct((N_KV, REP * QT, KW), F32),
                   jax.ShapeDtypeStruct((1, LANES), F32)],
        scratch=[pltpu.VMEM((WINDOW, kvw), F32), pltpu.VMEM((WINDOW, kvw), F32)],
        vmem=VMEM_BIG, carries=carries)


def _bias_tile(table, bmap):
    def body(tab_ref, bm_ref, out_ref):
        bm = bm_ref[...]

        def per_head(hd, carry):
            acc = jnp.full((QT, KW), NEG_INF, F32)
            for b in range(N_BUCKETS):
                acc = jnp.where(bm == b, tab_ref[b, hd], acc)
            out_ref[hd] = acc
            return carry

        lax.fori_loop(0, N_HEADS, per_head, 0)

    return pl.pallas_call(
        body, name="bias_tile", out_shape=jax.ShapeDtypeStruct((N_HEADS, QT, KW), F32),
        in_specs=[pl.BlockSpec(memory_space=pltpu.SMEM), pl.BlockSpec(memory_space=pltpu.VMEM)],
        out_specs=pl.BlockSpec(memory_space=pltpu.VMEM))(table, bmap)


def _bias_grad(ds_acc, bmap):
    def body(ds_ref, bm_ref, out_ref):
        row = lax.broadcasted_iota(jnp.int32, (N_BUCKETS, LANES), 0)
        lane = lax.broadcasted_iota(jnp.int32, (N_BUCKETS, LANES), 1)
        bm = bm_ref[...]

        def per_head(hd, res):
            dsv = ds_ref[hd]
            for b in range(N_BUCKETS):
                val = jnp.sum(jnp.sum(jnp.where(bm == b, dsv, 0.0), axis=0, keepdims=True), axis=1, keepdims=True)
                res = jnp.where((row == b) & (lane == hd), val, res)
            return res

        out_ref[...] = lax.fori_loop(0, N_HEADS, per_head, jnp.zeros((N_BUCKETS, LANES), F32))

    return pl.pallas_call(
        body, name="bias_grad", out_shape=jax.ShapeDtypeStruct((N_BUCKETS, LANES), F32),
        in_specs=[pl.BlockSpec(memory_space=pltpu.VMEM), pl.BlockSpec(memory_space=pltpu.VMEM)],
        out_specs=pl.BlockSpec(memory_space=pltpu.VMEM))(ds_acc, bmap)


MIXOUT = ("w_lru_out", "w_attn_out", "w_o")


def _mix_out_fwd(ya_in, o, gs, h, g, wg, tm=256, carries=()):
    t_tok = h.shape[0]
    nt = t_tok // tm
    groups = _groups_of(MIXOUT)

    def body(ya_ref, o_ref, gs_ref, h_ref, g_ref, wg_ref, hout_ref, yao_ref, ybo_ref, z_ref, wa_s, wb_s, wo_s, sems):
        i = pl.program_id(0)

        @pl.when(i == 0)
        def _():
            _load_weights({groups[0]: wg_ref}, list(zip(MIXOUT, (wa_s, wb_s, wo_s))), sems)

        ya = _nn(ya_ref[...], wa_s[...])
        yb = _nn(o_ref[...], wb_s[...])
        yao_ref[...] = ya
        ybo_ref[...] = yb
        merged = gs_ref[:, pl.ds(0, D)] * ya + gs_ref[:, pl.ds(D, D)] * yb
        z = _nn(merged.astype(BF16), wo_s[...])
        z_ref[...] = z
        zn, _, _ = _rms_fwd(z, g_ref[...])
        hout_ref[...] = h_ref[...] + zn

    return _call(
        body, name="mix_out_fwd", grid=(nt,), ins=[ya_in, o, gs, h, g, wg[groups[0]]],
        in_specs=[_row_spec(tm, D), _row_spec(tm, D), _row_spec(tm, 2 * D), _row_spec(tm, D), _const_spec((1, D)), ANY],
        out_specs=[_row_spec(tm, D)] * 4,
        out_shape=[jax.ShapeDtypeStruct((t_tok, D), F32)] * 4,
        scratch=[pltpu.VMEM((D, D), BF16)] * 3 + [DMA((3 * NDEV,))],
        vmem=48 * 2 ** 20, carries=carries)


def _mix_out_bwd(dh, z, ya, yb, gs, g, wg, tm=512, carries=()):
    t_tok = dh.shape[0]
    nt = t_tok // tm
    groups = _groups_of(MIXOUT)

    def body(dh_ref, z_ref, ya_ref, yb_ref, gs_ref, g_ref, wg_ref,
             dyain_ref, do_ref, dgpre_ref, dya_ref, dyb_ref, mg_ref, dz_ref, dg_ref, dbg_ref,
             wa_s, wb_s, wo_s, sems):
        i = pl.program_id(0)

        @pl.when(i == 0)
        def _():
            _load_weights({groups[0]: wg_ref}, list(zip(MIXOUT, (wa_s, wb_s, wo_s))), sems)
            dg_ref[...] = jnp.zeros_like(dg_ref)
            dbg_ref[...] = jnp.zeros_like(dbg_ref)

        gv = g_ref[...]
        _, zh, r = _rms_fwd(z_ref[...], gv)
        dz, dg = _rms_bwd(dh_ref[...], zh, r, gv)
        dg_ref[...] += dg
        dzb = dz.astype(BF16)
        dz_ref[...] = dzb
        ga, gb = gs_ref[:, pl.ds(0, D)], gs_ref[:, pl.ds(D, D)]
        ya_v, yb_v = ya_ref[...], yb_ref[...]
        mg_ref[...] = (ga * ya_v + gb * yb_v).astype(BF16)
        dm = _nt(dzb, wo_s[...])
        dga = dm * ya_v * ga * (1.0 - ga)
        dgb = dm * yb_v * gb * (1.0 - gb)
        dgpre_ref[:, pl.ds(0, D)] = dga.astype(BF16)
        dgpre_ref[:, pl.ds(D, D)] = dgb.astype(BF16)
        dbg_ref[:, pl.ds(0, D)] += jnp.sum(dga, axis=0, keepdims=True)
        dbg_ref[:, pl.ds(D, D)] += jnp.sum(dgb, axis=0, keepdims=True)
        dya = (dm * ga).astype(BF16)
        dyb = (dm * gb).astype(BF16)
        dya_ref[...] = dya
        dyb_ref[...] = dyb
        dyain_ref[...] = _nt(dya, wa_s[...])
        do_ref[...] = _nt(dyb, wb_s[...]).astype(BF16)

    bf = jax.ShapeDtypeStruct((t_tok, D), BF16)
    return _call(
        body, name="mix_out_bwd", grid=(nt,), ins=[dh, z, ya, yb, gs, g, wg[groups[0]]],
        in_specs=[_row_spec(tm, D)] * 4 + [_row_spec(tm, 2 * D), _const_spec((1, D)), ANY],
        out_specs=[_row_spec(tm, D), _row_spec(tm, D), _row_spec(tm, 2 * D)] + [_row_spec(tm, D)] * 4
        + [_const_spec((1, D)), _const_spec((1, 2 * D))],
        out_shape=[jax.ShapeDtypeStruct((t_tok, D), F32), bf, jax.ShapeDtypeStruct((t_tok, 2 * D), BF16), bf, bf, bf, bf,
                   jax.ShapeDtypeStruct((1, D), F32), jax.ShapeDtypeStruct((1, 2 * D), F32)],
        scratch=[pltpu.VMEM((D, D), BF16)] * 3 + [DMA((3 * NDEV,))],
        vmem=VMEM_BIG, carries=carries)


def _sum_parts(parts_list):
    n = len(parts_list)
    _, r, c = parts_list[0].shape
    tc = 256

    def body(*refs):
        for p_ref, o_ref in zip(refs[:n], refs[n:]):
            acc = p_ref[0].astype(F32)
            for s in range(1, NDEV):
                acc = acc + p_ref[s].astype(F32)
            o_ref[...] = acc

    return pl.pallas_call(
        body, name=f"sum_parts_{r}", grid=(c // tc,),
        in_specs=[pl.BlockSpec((NDEV, r, tc), lambda i: (0, 0, i))] * n,
        out_specs=[pl.BlockSpec((r, tc), lambda i: (0, i))] * n,
        out_shape=[jax.ShapeDtypeStruct((r, c), F32)] * n,
        compiler_params=pltpu.CompilerParams(dimension_semantics=("arbitrary",), vmem_limit_bytes=48 * 2 ** 20),
    )(*parts_list)


def _adamw_math(w, g, m, v):
    m = ADAM_B1 * m + (1.0 - ADAM_B1) * g
    v = ADAM_B2 * v + (1.0 - ADAM_B2) * (g * g)
    m_hat = m / (1.0 - ADAM_B1 ** ADAM_STEP)
    v_hat = v / (1.0 - ADAM_B2 ** ADAM_STEP)
    delta = -ADAM_LR * (m_hat / (jnp.sqrt(v_hat) + ADAM_EPS) + ADAM_WD * w)
    return delta, m, v


def _sum_adamw(items, name, carries=()):
    n = len(items)
    c = items[0][1].shape[1]
    tc = LANES

    def body(*refs):
        for k in range(n):
            p_ref, w_ref, m_ref, v_ref = refs[4 * k:4 * k + 4]
            g_ref, d_ref, nm_ref, nv_ref = refs[4 * n + 4 * k:4 * n + 4 * k + 4]
            g = p_ref[0].astype(F32)
            for s in range(1, NDEV):
                g = g + p_ref[s].astype(F32)
            d, m, v = _adamw_math(w_ref[...], g, m_ref[...], v_ref[...])
            g_ref[...] = g
            d_ref[...] = d
            nm_ref[...] = m
            nv_ref[...] = v

    ins, in_specs, out_shape, out_specs = [], [], [], []
    for parts, w, m, v in items:
        r = w.shape[0]
        col = pl.BlockSpec((r, tc), lambda i: (0, i))
        ins += [parts, w, m, v]
        in_specs += [pl.BlockSpec((NDEV, r, tc), lambda i: (0, 0, i)), col, col, col]
        out_shape += [jax.ShapeDtypeStruct((r, c), F32)] * 4
        out_specs += [col] * 4
    outs, carried = _call(body, name=name, grid=(c // tc,), ins=ins, in_specs=in_specs, out_shape=out_shape,
                          out_specs=out_specs, vmem=48 * 2 ** 20, carries=carries)
    return [tuple(outs[4 * k:4 * k + 4]) for k in range(n)], carried


def _adamw_body(n):
    def body(*refs):
        for k in range(n):
            g_ref, w_ref, m_ref, v_ref = refs[4 * k:4 * k + 4]
            d_ref, nm_ref, nv_ref = refs[4 * n + 3 * k:4 * n + 3 * k + 3]
            d, m, v = _adamw_math(w_ref[...], g_ref[...], m_ref[...], v_ref[...])
            d_ref[...] = d
            nm_ref[...] = m
            nv_ref[...] = v
    return body


def _adamw(items):
    n = len(items)
    r, c = items[0][0].shape
    tr = r if r * c <= 2 ** 18 else max(t for t in range(8, 65, 8) if r % t == 0)
    spec = pl.BlockSpec((tr, c), lambda i: (i, 0))
    outs = pl.pallas_call(
        _adamw_body(n), name=f"adamw_{r}x{c}", grid=(r // tr,),
        in_specs=[spec] * (4 * n), out_specs=[spec] * (3 * n),
        out_shape=[jax.ShapeDtypeStruct((r, c), F32)] * (3 * n),
        compiler_params=pltpu.CompilerParams(dimension_semantics=("arbitrary",), vmem_limit_bytes=40 * 2 ** 20),
    )(*[a for it in items for a in it])
    return [tuple(outs[3 * k:3 * k + 3]) for k in range(n)]


def _adamw_small(items):
    n = len(items)
    vm = pl.BlockSpec(memory_space=pltpu.VMEM)
    outs = pl.pallas_call(
        _adamw_body(n), name="adamw_small", in_specs=[vm] * (4 * n), out_specs=[vm] * (3 * n),
        out_shape=[jax.ShapeDtypeStruct(it[1].shape, F32) for it in items for _ in range(3)],
    )(*[a for it in items for a in it])
    return [tuple(outs[3 * k:3 * k + 3]) for k in range(n)]


def _pack_small(arrs):
    rows, offs = [], []
    total = 0
    for a in arrs:
        flat = a.reshape(-1).astype(F32)
        nr = -(-flat.shape[0] // LANES)
        flat = jnp.pad(flat, (0, nr * LANES - flat.shape[0]))
        rows.append(flat.reshape(nr, LANES))
        offs.append((total, nr))
        total += nr
    pad = -total % 8
    if pad:
        rows.append(jnp.zeros((pad, LANES), F32))
    return jnp.concatenate(rows, axis=0), offs


def _unpack_small(pack, offs, shapes):
    out = []
    for (o, nr), shp in zip(offs, shapes):
        size = int(np.prod(shp))
        out.append(pack[o:o + nr].reshape(-1)[:size].reshape(shp))
    return out


def _sum_small(gathered, rows):
    def body(p_ref, o_ref):
        acc = p_ref[pl.ds(0, rows), :]
        for s in range(1, NDEV):
            acc = acc + p_ref[pl.ds(s * rows, rows), :]
        o_ref[...] = acc

    return pl.pallas_call(
        body, name="sum_small", out_shape=jax.ShapeDtypeStruct((rows, LANES), F32),
        in_specs=[pl.BlockSpec(memory_space=pltpu.VMEM)], out_specs=pl.BlockSpec(memory_space=pltpu.VMEM))(gathered)


def _block_diag(w):
    w = w.reshape(D // LANES, 2, LRU_BLOCK, LRU_BLOCK)
    z = jnp.zeros((D // LANES, LRU_BLOCK, LRU_BLOCK), w.dtype)
    top = jnp.concatenate([w[:, 0], z], axis=2)
    bot = jnp.concatenate([z, w[:, 1]], axis=2)
    return jnp.concatenate([top, bot], axis=1)


def _block_diag_grad(dw):
    a = dw[:, :LRU_BLOCK, :LRU_BLOCK]
    b = dw[:, LRU_BLOCK:, LRU_BLOCK:]
    return jnp.stack([a, b], axis=1).reshape(D // LRU_BLOCK, LRU_BLOCK, LRU_BLOCK)


def kernel(x, ffn1_pre_g, ffn1_w1, ffn1_w3, ffn1_w2, ffn1_post_g, mix_pre_g, w_in, conv_w, conv_b, rg_a_w, rg_a_b, rg_x_w, rg_x_b, lru_lambda, w_lru_out, attn_sinks, rel_bias, w_attn_out, w_gate, b_gate, w_o, mix_post_g, ffn2_pre_g, ffn2_w1, ffn2_w3, ffn2_w2, ffn2_post_g, loss_target, m_ffn1_pre_g, m_ffn1_w1, m_ffn1_w3, m_ffn1_w2, m_ffn1_post_g, m_mix_pre_g, m_w_in, m_conv_w, m_conv_b, m_rg_a_w, m_rg_a_b, m_rg_x_w, m_rg_x_b, m_lru_lambda, m_w_lru_out, m_attn_sinks, m_rel_bias, m_w_attn_out, m_w_gate, m_b_gate, m_w_o, m_mix_post_g, m_ffn2_pre_g, m_ffn2_w1, m_ffn2_w3, m_ffn2_w2, m_ffn2_post_g, v_ffn1_pre_g, v_ffn1_w1, v_ffn1_w3, v_ffn1_w2, v_ffn1_post_g, v_mix_pre_g, v_w_in, v_conv_w, v_conv_b, v_rg_a_w, v_rg_a_b, v_rg_x_w, v_rg_x_b, v_lru_lambda, v_w_lru_out, v_attn_sinks, v_rel_bias, v_w_attn_out, v_w_gate, v_b_gate, v_w_o, v_mix_post_g, v_ffn2_pre_g, v_ffn2_w1, v_ffn2_w3, v_ffn2_w2, v_ffn2_post_g):
    names = ["ffn1_pre_g", "ffn1_w1", "ffn1_w3", "ffn1_w2", "ffn1_post_g", "mix_pre_g", "w_in", "conv_w", "conv_b",
             "rg_a_w", "rg_a_b", "rg_x_w", "rg_x_b", "lru_lambda", "w_lru_out", "attn_sinks", "rel_bias", "w_attn_out",
             "w_gate", "b_gate", "w_o", "mix_post_g", "ffn2_pre_g", "ffn2_w1", "ffn2_w3", "ffn2_w2", "ffn2_post_g"]
    ws = dict(zip(names, (ffn1_pre_g, ffn1_w1, ffn1_w3, ffn1_w2, ffn1_post_g, mix_pre_g, w_in, conv_w, conv_b, rg_a_w,
                          rg_a_b, rg_x_w, rg_x_b, lru_lambda, w_lru_out, attn_sinks, rel_bias, w_attn_out, w_gate,
                          b_gate, w_o, mix_post_g, ffn2_pre_g, ffn2_w1, ffn2_w3, ffn2_w2, ffn2_post_g)))
    ms = dict(zip(names, (m_ffn1_pre_g, m_ffn1_w1, m_ffn1_w3, m_ffn1_w2, m_ffn1_post_g, m_mix_pre_g, m_w_in, m_conv_w,
                          m_conv_b, m_rg_a_w, m_rg_a_b, m_rg_x_w, m_rg_x_b, m_lru_lambda, m_w_lru_out, m_attn_sinks,
                          m_rel_bias, m_w_attn_out, m_w_gate, m_b_gate, m_w_o, m_mix_post_g, m_ffn2_pre_g, m_ffn2_w1,
                          m_ffn2_w3, m_ffn2_w2, m_ffn2_post_g)))
    vs = dict(zip(names, (v_ffn1_pre_g, v_ffn1_w1, v_ffn1_w3, v_ffn1_w2, v_ffn1_post_g, v_mix_pre_g, v_w_in, v_conv_w,
                          v_conv_b, v_rg_a_w, v_rg_a_b, v_rg_x_w, v_rg_x_b, v_lru_lambda, v_w_lru_out, v_attn_sinks,
                          v_rel_bias, v_w_attn_out, v_w_gate, v_b_gate, v_w_o, v_mix_post_g, v_ffn2_pre_g, v_ffn2_w1,
                          v_ffn2_w3, v_ffn2_w2, v_ffn2_post_g)))
    me = 4 * lax.axis_index("x") + 2 * lax.axis_index("y") + lax.axis_index("c")
    vec = lambda n: ws[n].reshape(1, -1)

    def shard2d(name):
        if name == "conv":
            row = lax.bitcast_convert_type(conv_w.reshape(CONV_WIDTH, LANES), BF16).reshape(1, D)
            return jnp.concatenate([row, jnp.zeros((LOC["conv"][2] - 1, D), BF16)], axis=0)
        a = ws[name].reshape(ws[name].shape[-2], ws[name].shape[-1])
        return (a.T if name in COL_SHARDED else a).astype(BF16)

    packs = {g: jnp.concatenate([shard2d(n) for n, _ in members], axis=0) for g, members in GROUPS}

    wg = {}
    _, ((wg["ffn1"],),) = _call(None, name="allgather_ffn1", grid=(), ins=[], in_specs=[], out_shape=[], out_specs=[],
                                carries=[_AllGather(packs["ffn1"])])
    conv_rows = wg["ffn1"].reshape(NDEV, GROUP_ROWS["ffn1"], D)[:, LOC["conv"][1]]
    cw = jnp.transpose(lax.bitcast_convert_type(conv_rows.reshape(NDEV, CONV_WIDTH, LANES, 2), F32),
                       (1, 0, 2)).reshape(CONV_WIDTH, D)
    bmap = jnp.asarray(_bucket_map())
    bias = _bias_tile(rel_bias, bmap).reshape(N_KV, REP * QT, KW)
    sinks = attn_sinks.reshape(N_HEADS)
    wa_bd = _block_diag(rg_a_w.reshape(D // LRU_BLOCK, LRU_BLOCK, LRU_BLOCK)).astype(BF16)
    wx_bd = _block_diag(rg_x_w.reshape(D // LRU_BLOCK, LRU_BLOCK, LRU_BLOCK)).astype(BF16)
    lru_args = (cw, vec("conv_b"), wa_bd, wx_bd, vec("rg_a_b"), vec("rg_x_b"), vec("lru_lambda"))
    x2, tgt = x[0], loss_target[0]

    (h1, a1, b1, f1, nb1), ((wg["mixin"],),) = _ffn_fwd(
        x2, vec("ffn1_pre_g"), vec("ffn1_post_g"), wg, ("ffn1_w1", "ffn1_w3", "ffn1_w2"),
        carries=[_AllGather(packs["mixin"])])
    (q, kd, vd, xr, xg, gs, ub), ((wg["mixout"],),) = _mix_proj_fwd(
        h1, vec("mix_pre_g"), vec("b_gate"), wg, carries=[_AllGather(packs["mixout"])])
    (ya_in, hseq), ((wg["ffn2a"],),) = _lru_fwd(xr, xg, *lru_args, carries=[_AllGather(packs["ffn2a"])])
    (o, lse), ((wg["ffn2b"],),) = _attn_fwd(q, kd, vd, bias, sinks, carries=[_AllGather(packs["ffn2b"])])
    (h2, ya, yb, z), _ = _mix_out_fwd(ya_in, o, gs, h1, vec("mix_post_g"), wg)
    (_, a2, b2, f2, nb2, dy, loss_part), _ = _ffn_fwd(h2, vec("ffn2_pre_g"), vec("ffn2_post_g"), wg,
                                                     ("ffn2_w1", "ffn2_w3", "ffn2_w2"), target=tgt)
    loss = lax.psum(loss_part[0, 0], ("x", "y", "c"))

    gsm, parts = {}, {}
    rs = lambda *grads_: [_ReduceScatterSend(list(grads_))]
    ffr = FF // NDEV
    (dab, s_act, dfb, gsm["ffn2_post_g"]), _ = _ffn_bwd_a(dy, f2, a2, b2, vec("ffn2_post_g"), wg, "ffn2_w2")
    g_w2, _ = _dw(s_act, dfb, FF // 2, "dw_ffn2_w2")
    g_w13, ((parts["ffn2_w2"],),) = _dw(dab, nb2, D // 2, "dw_ffn2_w13", carries=rs(g_w2))
    (dh2, gsm["ffn2_pre_g"]), ((parts["ffn2_w1"],),) = _ffn_bwd_dx(
        dab, h2, dy, vec("ffn2_pre_g"), wg, "ffn2_w1", "ffn2_w3", carries=rs((g_w13, 0, ffr)))
    (dya_in, do, dgpre, dya, dyb, mg, dzb, gsm["mix_post_g"], gsm["b_gate"]), ((parts["ffn2_w3"],),) = _mix_out_bwd(
        dh2, z, ya, yb, gs, vec("mix_post_g"), wg, carries=rs((g_w13, FF, ffr)))
    g_wa, _ = _dw(ya_in, dya, D // 2, "dw_w_lru_out")
    g_wb, _ = _dw(o, dyb, D // 2, "dw_w_attn_out")
    g_wo, _ = _dw(mg, dzb, D // 2, "dw_w_o")
    (dqkv, ds_acc, dsink), (mixout_parts,) = _attn_bwd(
        q, kd, vd, o, do, lse, bias, sinks, carries=rs(g_wa, g_wb, g_wo))
    parts["w_lru_out"], parts["w_attn_out"], parts["w_o"] = mixout_parts
    (dxr, dxg, dvec, dwa, dwx), _ = _lru_bwd(dya_in, xr, xg, hseq, *lru_args)
    g_qkv, _ = _dw(dqkv, ub, D // 2, "dw_w_in_qkv")
    g_xr, _ = _dw(dxr, ub, D // 2, "dw_w_in_xr")
    g_xg, _ = _dw(dxg, ub, D // 2, "dw_w_in_xg")
    g_wgate, _ = _dw(dgpre, ub, D // 2, "dw_w_gate")
    g_win = jnp.concatenate(
        [g_qkv[:D]] + [g_qkv[c0 + gi * LANES:c0 + gi * LANES + HEAD_DIM] for c0 in (K0, V0) for gi in range(N_KV)]
        + [g_xr, g_xg], axis=0)
    (dh1, gsm["mix_pre_g"]), ((parts["w_in"],),) = _mix_proj_bwd(
        dqkv, dxr, dxg, dgpre, h1, dh2, vec("mix_pre_g"), wg, carries=rs(g_win))
    gsm["conv_w"] = dvec[0:CONV_WIDTH]
    gsm["conv_b"], gsm["rg_a_b"], gsm["rg_x_b"], gsm["lru_lambda"] = dvec[4], dvec[5], dvec[6], dvec[7]
    gsm["rg_a_w"] = _block_diag_grad(dwa)
    gsm["rg_x_w"] = _block_diag_grad(dwx)
    gsm["attn_sinks"] = dsink[0, :N_HEADS]
    gsm["rel_bias"] = _bias_grad(ds_acc.reshape(N_HEADS, QT, KW), bmap)[:, :N_HEADS]
    late = ("ffn1_post_g", "ffn1_pre_g")
    early = tuple(n for n in SMALL if n not in late)
    early_pack, early_offs = _pack_small([gsm[n] for n in early])
    (dab, s_act, dfb, gsm["ffn1_post_g"]), ((parts["w_gate"],), (early_all,)) = _ffn_bwd_a(
        dh1, f1, a1, b1, vec("ffn1_post_g"), wg, "ffn1_w2", carries=rs(g_wgate) + [_AllGather(early_pack)])
    g_w2, _ = _dw(s_act, dfb, FF // 2, "dw_ffn1_w2")
    g_w13, ((parts["ffn1_w2"],),) = _dw(dab, nb1, D // 2, "dw_ffn1_w13", carries=rs(g_w2))
    (grad_x, gsm["ffn1_pre_g"]), ((parts["ffn1_w1"],),) = _ffn_bwd_dx(
        dab, x2, dh1, vec("ffn1_pre_g"), wg, "ffn1_w1", "ffn1_w3", carries=rs((g_w13, 0, ffr)))
    late_pack, late_offs = _pack_small([gsm[n] for n in late])

    grads, delta, new_m, new_v = {}, {}, {}, {}
    t_form = [n for n in COL_SHARDED if ws[n].shape[-1] % LANES]
    view = {n: (lambda a: a.reshape(a.shape[-2], a.shape[-1]).T) if n in t_form
            else (lambda a: a.reshape(a.shape[-2], a.shape[-1])) for n in BIG}
    unview = {n: (lambda a: a.T) if n in t_form else (lambda a: a) for n in BIG}

    def update(group, name, carries=()):
        res, carried = _sum_adamw([(parts[n], view[n](ws[n]), view[n](ms[n]), view[n](vs[n])) for n in group], name,
                                  carries=carries)
        for n, quad in zip(group, res):
            grads[n], delta[n], new_m[n], new_v[n] = (unview[n](a) for a in quad)
        return carried

    ready = [n for n in BIG if n not in ("ffn1_w3", "w_gate")]
    (parts["ffn1_w3"],), (late_all,) = update(ready, "update_ready", rs((g_w13, FF, ffr)) + [_AllGather(late_pack)])
    update(["ffn1_w3"], "update_last")
    (g_gate_t,) = _sum_parts([parts["w_gate"]])
    grads["w_gate"] = g_gate_t.T
    ((delta["w_gate"], new_m["w_gate"], new_v["w_gate"]),) = _adamw(
        [(grads["w_gate"], view["w_gate"](ws["w_gate"]), view["w_gate"](ms["w_gate"]), view["w_gate"](vs["w_gate"]))])
    for group, gathered, offs in ((early, early_all, early_offs), (late, late_all, late_offs)):
        total = _sum_small(gathered, gathered.shape[0] // NDEV)
        shapes = [(CONV_WIDTH, D) if n == "conv_w" else ws[n].shape for n in group]
        for n, g in zip(group, _unpack_small(total, offs, shapes)):
            grads[n] = g
    grads["conv_w"] = lax.dynamic_slice(grads["conv_w"], (0, me * LANES), (CONV_WIDTH, LANES)).reshape(conv_w.shape)

    flat2d = lambda a: a.reshape(-1, a.shape[-1])
    res = _adamw_small([(flat2d(grads[n].reshape(ws[n].shape)), flat2d(ws[n]), flat2d(ms[n]), flat2d(vs[n]))
                        for n in SMALL])
    for n, (d_, m_, v_) in zip(SMALL, res):
        delta[n], new_m[n], new_v[n] = d_, m_, v_

    outs = [loss, grad_x.reshape(x.shape)]
    for src in (grads, delta, new_m, new_v):
        outs += [src[n].reshape(ws[n].shape) for n in names]
    return tuple(outs)
```

```python
import functools
import math
import operator

import numpy as np
import jax
import jax.numpy as jnp
from jax import lax
from jax.experimental import pallas as pl
from jax.experimental.pallas import tpu as pltpu

F32, BF16 = jnp.float32, jnp.bfloat16

NDEV = 8
D = 1024
FF = 2816
N_HEADS, N_KV, HEAD_DIM = 16, 4, 64
CHUNK, WINDOW = 64, 128
N_BUCKETS, MAX_DISTANCE = 32, 128
LRU_BLOCK = 64
CONV_WIDTH = 4
LRU_C = 8.0
RMS_EPS = 1e-6
NEG_INF = -1e30
LANES = 128
QT = 128
KW = QT + WINDOW
LRU_ROWS = 512
IN_W = D + 2 * N_KV * HEAD_DIM + 2 * D
INP_W = D + 2 * N_KV * LANES + 2 * D
VMEM_BIG = 58 * 2 ** 20

ADAM_LR, ADAM_B1, ADAM_B2, ADAM_EPS, ADAM_WD, ADAM_STEP = 0.001, 0.9, 0.999, 1e-08, 0.01, 10

GROUPS = (("ffn1", (("ffn1_w1", FF // NDEV), ("ffn1_w3", FF // NDEV), ("ffn1_w2", FF // NDEV), ("conv", 16))),
          ("mixin", (("w_in", IN_W // NDEV), ("w_gate", 2 * D // NDEV))),
          ("mixout", (("w_lru_out", D // NDEV), ("w_attn_out", D // NDEV), ("w_o", D // NDEV))),
          ("ffn2a", (("ffn2_w1", FF // NDEV), ("ffn2_w3", FF // NDEV))),
          ("ffn2b", (("ffn2_w2", FF // NDEV),)))
LOC, GROUP_ROWS = {}, {}
for _g, _members in GROUPS:
    _o = 0
    for _n, _r in _members:
        LOC[_n] = (_g, _o, _r)
        _o += _r
    GROUP_ROWS[_g] = _o
BIG = tuple(n for _, members in GROUPS for n, _ in members if n != "conv")
COL_SHARDED = ("ffn1_w1", "ffn1_w3", "w_in", "w_gate", "ffn2_w1", "ffn2_w3")

SMALL = ("ffn1_pre_g", "ffn1_post_g", "mix_pre_g", "conv_w", "conv_b", "rg_a_w", "rg_a_b", "rg_x_w", "rg_x_b",
         "lru_lambda", "attn_sinks", "rel_bias", "b_gate", "mix_post_g", "ffn2_pre_g", "ffn2_post_g")

MESH = pl.DeviceIdType.MESH
ANY = pl.BlockSpec(memory_space=pl.ANY)
DMA = pltpu.SemaphoreType.DMA


def _nn(a, b):
    return lax.dot_general(a, b, (((1,), (0,)), ((), ())), preferred_element_type=F32)


def _nt(a, b):
    return lax.dot_general(a, b, (((1,), (1,)), ((), ())), preferred_element_type=F32)


def _tn(a, b):
    return lax.dot_general(a, b, (((0,), (0,)), ((), ())), preferred_element_type=F32)


def _rms_fwd(x, g):
    r = lax.rsqrt(jnp.mean(x * x, axis=-1, keepdims=True) + RMS_EPS)
    xh = x * r
    return xh * g, xh, r


def _rms_bwd(dn, xh, r, g):
    dxh = dn * g
    dx = r * (dxh - xh * jnp.mean(dxh * xh, axis=-1, keepdims=True))
    return dx, jnp.sum(dn * xh, axis=0, keepdims=True)


def _row_spec(tm, c):
    return pl.BlockSpec((tm, c), lambda i: (i, 0))


def _const_spec(shape):
    nd = len(shape)
    return pl.BlockSpec(shape, lambda i: (0,) * nd)


class _AllGather:
    def __init__(self, shard):
        self.m, n = shard.shape
        self.ins = [shard]
        self.out_shape = [jax.ShapeDtypeStruct((NDEV * self.m, n), shard.dtype)]
        self.scratch = [DMA((7,)), DMA((7,)), DMA]

    def _copies(self, ins, outs, scr, all_of_them):
        x_ref, out_ref = ins[0], outs[0]
        send_sems, recv_sems, local_sem = scr
        x, y, c = lax.axis_index("x"), lax.axis_index("y"), lax.axis_index("c")
        me, sibling = (x, y, c), (x, y, 1 - c)
        chips = [(1 - x, y), (x, 1 - y), (1 - x, 1 - y)]
        m = self.m

        def rows(px, py, pc):
            return out_ref.at[pl.ds((4 * px + 2 * py + pc) * m, m), :]

        def copy(k, block, to, src=None):
            return pltpu.make_async_remote_copy(
                src_ref=rows(*block) if src is None else src, dst_ref=rows(*block),
                send_sem=send_sems.at[k], recv_sem=recv_sems.at[k], device_id=to, device_id_type=MESH)

        mine = pltpu.make_async_copy(x_ref, rows(*me), local_sem)
        first = [copy(0, me, sibling, src=x_ref)] + [copy(1 + j, me, (*chip, c), src=x_ref)
                                                     for j, chip in enumerate(chips)]
        if not all_of_them:
            return mine, first
        passed = [copy(4 + j, (*chip, c), sibling) for j, chip in enumerate(chips)]
        landed = [copy(1 + j, (*chip, c), me) for j, chip in enumerate(chips)]
        from_sibling = [copy(0, sibling, me)] + [copy(4 + j, (*chip, 1 - c), me) for j, chip in enumerate(chips)]
        return mine, first, passed, landed, from_sibling

    def start(self, ins, outs, scr):
        mine, first = self._copies(ins, outs, scr, False)
        mine.start()
        for cp in first:
            cp.start()

    def finish(self, ins, outs, scr):
        mine, first, passed, landed, from_sibling = self._copies(ins, outs, scr, True)
        for cp_in, cp_on in zip(landed, passed):
            cp_in.wait_recv()
            cp_on.start()
        for cp in from_sibling:
            cp.wait_recv()
        for cp in first + passed:
            cp.wait_send()
        mine.wait()


class _ReduceScatterSend:
    def __init__(self, grads):
        grads = [g if isinstance(g, tuple) else (g, 0, g.shape[0] // NDEV, g.shape[0] // NDEV) for g in grads]
        self.nw = len(grads)
        self.base = [b for _, b, _, _ in grads]
        self.stride = [s for _, _, s, _ in grads]
        self.rows = [r for _, _, _, r in grads]
        self.ins = [g for g, _, _, _ in grads]
        self.out_shape = [jax.ShapeDtypeStruct((NDEV, r, g.shape[1]), g.dtype) for g, _, _, r in grads]
        self.scratch = [DMA((self.nw, NDEV - 1)), DMA((self.nw, NDEV - 1)), DMA((self.nw,))]

    def _copies(self, g_refs, r_refs, scr, want):
        send_sems, recv_sems, local_sems = scr
        x, y, c = lax.axis_index("x"), lax.axis_index("y"), lax.axis_index("c")
        me = 4 * x + 2 * y + c
        rows, base, stride = self.rows, self.base, self.stride
        out = []
        if want == "local":
            for w in range(self.nw):
                out.append(pltpu.make_async_copy(g_refs[w].at[pl.ds(base[w] + me * stride[w], rows[w])],
                                                 r_refs[w].at[me], local_sems.at[w]))
            return out
        for k in range(1, NDEV):
            px, py, pc = x ^ (k >> 2), y ^ ((k >> 1) & 1), c ^ (k & 1)
            peer = 4 * px + 2 * py + pc
            for w in range(self.nw):
                sems = dict(send_sem=send_sems.at[w, k - 1], recv_sem=recv_sems.at[w, k - 1],
                            device_id=(px, py, pc), device_id_type=MESH)
                if want == "send":
                    out.append(pltpu.make_async_remote_copy(
                        src_ref=g_refs[w].at[pl.ds(base[w] + peer * stride[w], rows[w])], dst_ref=r_refs[w].at[me],
                        **sems))
                else:
                    out.append(pltpu.make_async_remote_copy(
                        src_ref=g_refs[w].at[pl.ds(0, rows[w])], dst_ref=r_refs[w].at[peer], **sems))
        return out

    def start(self, ins, outs, scr):
        for cp in self._copies(ins, outs, scr, "local") + self._copies(ins, outs, scr, "send"):
            cp.start()

    def finish(self, ins, outs, scr):
        for cp in self._copies(ins, outs, scr, "recv"):
            cp.wait_recv()
        for cp in self._copies(ins, outs, scr, "send"):
            cp.wait_send()
        for cp in self._copies(ins, outs, scr, "local"):
            cp.wait()


def _call(body, *, name, grid, ins, in_specs, out_shape, out_specs, scratch=(), vmem=None, carries=()):
    n_in, n_out, n_scr = len(ins), len(out_shape), len(scratch)
    ng = len(grid)

    def split(refs):
        pos = [0]

        def take(k):
            part = refs[pos[0]:pos[0] + k]
            pos[0] += k
            return part

        i_refs = take(n_in)
        c_in = [take(len(c.ins)) for c in carries]
        o_refs = take(n_out)
        c_out = [take(len(c.out_shape)) for c in carries]
        s_refs = take(n_scr)
        c_scr = [take(len(c.scratch)) for c in carries]
        return i_refs, o_refs, s_refs, list(zip(carries, c_in, c_out, c_scr))

    def full(*refs):
        i_refs, o_refs, s_refs, cparts = split(refs)
        if ng == 0:
            for c, a, b, s in cparts:
                c.start(a, b, s)
            for c, a, b, s in cparts:
                c.finish(a, b, s)
            return
        ids = [pl.program_id(a) for a in range(ng)]
        if cparts:
            @pl.when(functools.reduce(operator.and_, [i == 0 for i in ids]))
            def _():
                for c, a, b, s in cparts:
                    c.start(a, b, s)

        body(*i_refs, *o_refs, *s_refs)
        if cparts:
            @pl.when(functools.reduce(operator.and_, [i == g - 1 for i, g in zip(ids, grid)]))
            def _():
                for c, a, b, s in cparts:
                    c.finish(a, b, s)

    all_ins = list(ins) + [a for c in carries for a in c.ins]
    all_in_specs = list(in_specs) + [ANY for c in carries for _ in c.ins]
    all_out_shape = list(out_shape) + [s for c in carries for s in c.out_shape]
    all_out_specs = list(out_specs) + [ANY for c in carries for _ in c.out_shape]
    all_scratch = list(scratch) + [s for c in carries for s in c.scratch]
    kwargs = dict(grid=grid) if ng else {}
    outs = pl.pallas_call(
        full, name=name, in_specs=all_in_specs, out_specs=all_out_specs, out_shape=all_out_shape,
        scratch_shapes=all_scratch,
        compiler_params=pltpu.CompilerParams(dimension_semantics=("arbitrary",) * ng if ng else None,
                                             vmem_limit_bytes=vmem),
        **kwargs)(*all_ins)
    outs = list(outs)
    res, pos = outs[:n_out], n_out
    carried = []
    for c in carries:
        carried.append(outs[pos:pos + len(c.out_shape)])
        pos += len(c.out_shape)
    return res, carried


def _groups_of(names):
    out = []
    for n in names:
        if LOC[n][0] not in out:
            out.append(LOC[n][0])
    return out


def _weight_pieces(name):
    g, off, rows = LOC[name]
    return [(d * GROUP_ROWS[g] + off, d * rows, rows) for d in range(NDEV)]


def _win_pieces():
    kv = N_KV * HEAD_DIM
    pieces = [(0, 0, D)]
    for g0, d0 in ((D, D), (D + kv, D + N_KV * LANES)):
        for g in range(N_KV):
            for half in range(2):
                pieces.append((g0 + g * HEAD_DIM, d0 + g * LANES + half * HEAD_DIM, HEAD_DIM))
    pieces.append((D + 2 * kv, D + 2 * N_KV * LANES, D))
    pieces.append((D + 2 * kv + D, D + 2 * N_KV * LANES + D, D))
    grp, off, rows = LOC["w_in"]
    out = []
    for g0, d0, n in pieces:
        while n > 0:
            dev, loc = divmod(g0, rows)
            m = min(n, rows - loc)
            out.append((dev * GROUP_ROWS[grp] + off + loc, d0, m))
            g0, d0, n = g0 + m, d0 + m, n - m
    return out


def _start_loads(src_ref, dst_ref, pieces, sems, base):
    cps = []
    for j, (s, d, n) in enumerate(pieces):
        cp = pltpu.make_async_copy(src_ref.at[pl.ds(s, n)], dst_ref.at[pl.ds(d, n)], sems.at[base + j])
        cp.start()
        cps.append(cp)
    return cps


def _load_weights(wrefs, targets, sems):
    cps, base = [], 0
    for name, dst in targets:
        pieces = _win_pieces() if name == "w_in" else _weight_pieces(name)
        cps += _start_loads(wrefs[LOC[name][0]], dst, pieces, sems, base)
        base += len(pieces)
    for cp in cps:
        cp.wait()


def _n_pieces(names):
    return sum(len(_win_pieces()) if n == "w_in" else NDEV for n in names)


def _dw(lhs, rhs, chunk, name, carries=()):
    if lhs.ndim == 3:
        nch, t_tok, chunk = lhs.shape
        c = nch * chunk
        lhs_spec = pl.BlockSpec((None, t_tok, chunk), lambda i: (i, 0, 0))
    else:
        t_tok, c = lhs.shape
        lhs_spec = pl.BlockSpec((t_tok, chunk), lambda i: (0, i))
    nq = 4 if t_tok % (4 * 16) == 0 and chunk <= D // 2 else 1
    tq = t_tok // nq

    def body(lhs_ref, rhs_ref, out_ref, rhs_s, sems):
        def copies():
            return [pltpu.make_async_copy(rhs_ref.at[pl.ds(k * tq, tq)], rhs_s.at[pl.ds(k * tq, tq)], sems.at[k])
                    for k in range(nq)]

        first = pl.program_id(0) == 0
        if nq == 1:
            @pl.when(first)
            def _():
                for cp in copies():
                    cp.start()
                    cp.wait()

            out_ref[...] = _tn(lhs_ref[...], rhs_s[...]).astype(BF16)
            return

        @pl.when(first)
        def _():
            cps = copies()
            for cp in cps:
                cp.start()
            acc = None
            for k, cp in enumerate(cps):
                cp.wait()
                part = _tn(lhs_ref[pl.ds(k * tq, tq), :], rhs_s[pl.ds(k * tq, tq), :])
                acc = part if acc is None else acc + part
            out_ref[...] = acc.astype(BF16)

        @pl.when(jnp.logical_not(first))
        def _():
            out_ref[...] = _tn(lhs_ref[...], rhs_s[...]).astype(BF16)

    (out,), carried = _call(
        body, name=name, grid=(c // chunk,), ins=[lhs, rhs],
        in_specs=[lhs_spec, ANY],
        out_specs=[pl.BlockSpec((chunk, D), lambda i: (i, 0))],
        out_shape=[jax.ShapeDtypeStruct((c, D), BF16)],
        scratch=[pltpu.VMEM((t_tok, D), BF16), DMA((nq,))], vmem=VMEM_BIG, carries=carries)
    return out, carried


def _silu_parts(a):
    sig = jax.nn.sigmoid(a)
    return sig, a * sig


FC = 256


def _ffn_fwd(h, gpre, gpost, wg, names, target=None, tm=512, carries=()):
    t_tok = h.shape[0]
    nt = t_tok // tm
    with_loss = target is not None
    groups = _groups_of(names)

    def body(*refs):
        refs = list(refs)
        h_ref, gpre_ref, gpost_ref = refs[:3]
        del refs[:3]
        tgt_ref = refs.pop(0) if with_loss else None
        wrefs = dict(zip(groups, refs[:len(groups)]))
        del refs[:len(groups)]
        hout_ref, a_ref, b_ref, f_ref, nb_ref = refs[:5]
        del refs[:5]
        if with_loss:
            dy_ref, loss_ref = refs[:2]
            del refs[:2]
        w1_s, w3_s, w2_s, sems = refs
        i = pl.program_id(0)

        @pl.when(i == 0)
        def _():
            _load_weights(wrefs, list(zip(names, (w1_s, w3_s, w2_s))), sems)
            if with_loss:
                loss_ref[...] = jnp.zeros_like(loss_ref)

        x = h_ref[...]
        n, _, _ = _rms_fwd(x, gpre_ref[...])
        nb = n.astype(BF16)
        nb_ref[...] = nb
        f = jnp.zeros((tm, D), F32)
        for c0 in range(0, FF, FC):
            a = _nt(nb, w1_s[pl.ds(c0, FC), :])
            b = _nt(nb, w3_s[pl.ds(c0, FC), :])
            _, sl = _silu_parts(a)
            a_ref[:, pl.ds(c0, FC)] = a.astype(BF16)
            b_ref[:, pl.ds(c0, FC)] = b.astype(BF16)
            f = f + _nn((sl * b).astype(BF16), w2_s[pl.ds(c0, FC), :])
        f_ref[...] = f
        fn, _, _ = _rms_fwd(f, gpost_ref[...])
        y = x + 0.5 * fn
        hout_ref[...] = y
        if with_loss:
            err = y - tgt_ref[...]
            dy_ref[...] = err * (1.0 / D)
            loss_ref[...] += jnp.sum(jnp.sum(err * err, axis=-1, keepdims=True), axis=0, keepdims=True) * (0.5 / D)

    ins = [h, gpre, gpost] + ([target] if with_loss else []) + [wg[g] for g in groups]
    in_specs = [_row_spec(tm, D), _const_spec((1, D)), _const_spec((1, D))]
    in_specs += ([_row_spec(tm, D)] if with_loss else []) + [ANY] * len(groups)
    out_shape = [jax.ShapeDtypeStruct((t_tok, D), F32), jax.ShapeDtypeStruct((t_tok, FF), BF16),
                 jax.ShapeDtypeStruct((t_tok, FF), BF16), jax.ShapeDtypeStruct((t_tok, D), F32),
                 jax.ShapeDtypeStruct((t_tok, D), BF16)]
    out_specs = [_row_spec(tm, D), _row_spec(tm, FF), _row_spec(tm, FF), _row_spec(tm, D), _row_spec(tm, D)]
    if with_loss:
        out_shape += [jax.ShapeDtypeStruct((t_tok, D), F32), jax.ShapeDtypeStruct((1, LANES), F32)]
        out_specs += [_row_spec(tm, D), _const_spec((1, LANES))]
    return _call(body, name="ffn_fwd_" + names[0][:4], grid=(nt,), ins=ins, in_specs=in_specs,
                 out_shape=out_shape, out_specs=out_specs,
                 scratch=[pltpu.VMEM((FF, D), BF16)] * 3 + [DMA((3 * NDEV,))], vmem=VMEM_BIG, carries=carries)


FH = FF // 2
HALF_PIECES = ((0, 256), (256, 256), (512, 256), (768, 256), (1024, 256), (1280, 128))


def _ffn_bwd_a(dh, f, a, b, gpost, wg, name_w2, tm=512, carries=()):
    t_tok = dh.shape[0]
    nt = t_tok // tm
    groups = _groups_of([name_w2])

    def body(dh_ref, f_ref, a_ref, b_ref, gpost_ref, wg_ref, dab_ref, s_ref, df_ref, dgp_ref, w2_s, sems):
        i = pl.program_id(0)

        @pl.when(i == 0)
        def _():
            _load_weights({groups[0]: wg_ref}, [(name_w2, w2_s)], sems)
            dgp_ref[...] = jnp.zeros_like(dgp_ref)

        fv = f_ref[...]
        _, fh, r = _rms_fwd(fv, gpost_ref[...])
        df, dg = _rms_bwd(0.5 * dh_ref[...], fh, r, gpost_ref[...])
        dgp_ref[...] += dg
        dfb = df.astype(BF16)
        df_ref[...] = dfb
        for half in range(2):
            for o, n in HALF_PIECES:
                c0 = half * FH + o
                ds = _nt(dfb, w2_s[pl.ds(c0, n), :])
                av = a_ref[:, pl.ds(c0, n)].astype(F32)
                bv = b_ref[:, pl.ds(c0, n)].astype(F32)
                sig, sl = _silu_parts(av)
                dab_ref[half, :, pl.ds(o, n)] = (ds * bv * (sig * (1.0 + av * (1.0 - sig)))).astype(BF16)
                dab_ref[2 + half, :, pl.ds(o, n)] = (ds * sl).astype(BF16)
                s_ref[half, :, pl.ds(o, n)] = (sl * bv).astype(BF16)

    return _call(
        body, name="ffn_bwd_a_" + name_w2[:4], grid=(nt,), ins=[dh, f, a, b, gpost, wg[groups[0]]],
        in_specs=[_row_spec(tm, D), _row_spec(tm, D), _row_spec(tm, FF), _row_spec(tm, FF), _const_spec((1, D)), ANY],
        out_specs=[pl.BlockSpec((4, tm, FH), lambda i: (0, i, 0)), pl.BlockSpec((2, tm, FH), lambda i: (0, i, 0)),
                   _row_spec(tm, D), _const_spec((1, D))],
        out_shape=[jax.ShapeDtypeStruct((4, t_tok, FH), BF16), jax.ShapeDtypeStruct((2, t_tok, FH), BF16),
                   jax.ShapeDtypeStruct((t_tok, D), BF16), jax.ShapeDtypeStruct((1, D), F32)],
        scratch=[pltpu.VMEM((FF, D), BF16), DMA((NDEV,))],
        vmem=VMEM_BIG, carries=carries)


def _ffn_bwd_dx(dab, h, dh, gpre, wg, name_w1, name_w3, tm=512, carries=()):
    t_tok = dh.shape[0]
    nt = t_tok // tm
    groups = _groups_of([name_w1, name_w3])

    def body(*refs):
        dab_ref, h_ref, dh_ref, gpre_ref = refs[:4]
        wrefs = dict(zip(groups, refs[4:4 + len(groups)]))
        dhin_ref, dgp_ref, w13_s, sems = refs[4 + len(groups):]
        i = pl.program_id(0)

        @pl.when(i == 0)
        def _():
            _load_weights(wrefs, [(name_w1, w13_s.at[pl.ds(0, FF)]), (name_w3, w13_s.at[pl.ds(FF, FF)])], sems)
            dgp_ref[...] = jnp.zeros_like(dgp_ref)

        g = gpre_ref[...]
        _, xh, r = _rms_fwd(h_ref[...], g)
        dn = _nn(dab_ref[0], w13_s[pl.ds(0, FH), :])
        for k in range(1, 4):
            dn = dn + _nn(dab_ref[k], w13_s[pl.ds(k * FH, FH), :])
        dx, dg = _rms_bwd(dn, xh, r, g)
        dgp_ref[...] += dg
        dhin_ref[...] = dh_ref[...] + dx

    return _call(
        body, name="ffn_bwd_dx_" + name_w1[:4], grid=(nt,), ins=[dab, h, dh, gpre] + [wg[g] for g in groups],
        in_specs=[pl.BlockSpec((4, tm, FH), lambda i: (0, i, 0)), _row_spec(tm, D), _row_spec(tm, D),
                  _const_spec((1, D))] + [ANY] * len(groups),
        out_specs=[_row_spec(tm, D), _const_spec((1, D))],
        out_shape=[jax.ShapeDtypeStruct((t_tok, D), F32), jax.ShapeDtypeStruct((1, D), F32)],
        scratch=[pltpu.VMEM((2 * FF, D), BF16), DMA((2 * NDEV,))],
        vmem=VMEM_BIG, carries=carries)


Q0, K0, V0, XR0, XG0 = 0, D, D + N_KV * LANES, D + 2 * N_KV * LANES, 2 * D + 2 * N_KV * LANES


def _mix_proj_fwd(h, g, bgate, wg, tm=512, carries=()):
    t_tok = h.shape[0]
    nt = t_tok // tm
    names = ("w_in", "w_gate")
    groups = _groups_of(names)

    def body(h_ref, g_ref, bg_ref, wg_ref, q_ref, k_ref, v_ref, xr_ref, xg_ref, gs_ref, ub_ref, win_s, wgt_s, sems):
        i = pl.program_id(0)

        @pl.when(i == 0)
        def _():
            _load_weights({groups[0]: wg_ref}, [("w_in", win_s), ("w_gate", wgt_s)], sems)

        n, _, _ = _rms_fwd(h_ref[...], g_ref[...])
        nb = n.astype(BF16)
        ub_ref[...] = nb
        q_ref[...] = _nt(nb, win_s[pl.ds(Q0, D), :]).astype(BF16)
        k_ref[...] = _nt(nb, win_s[pl.ds(K0, N_KV * LANES), :]).astype(BF16)
        v_ref[...] = _nt(nb, win_s[pl.ds(V0, N_KV * LANES), :]).astype(BF16)
        xr_ref[...] = _nt(nb, win_s[pl.ds(XR0, D), :])
        xg_ref[...] = _nt(nb, win_s[pl.ds(XG0, D), :])
        gs_ref[...] = jax.nn.sigmoid(_nt(nb, wgt_s[...]) + bg_ref[...])

    kvw = N_KV * LANES
    return _call(
        body, name="mix_proj_fwd", grid=(nt,), ins=[h, g, bgate, wg[groups[0]]],
        in_specs=[_row_spec(tm, D), _const_spec((1, D)), _const_spec((1, 2 * D)), ANY],
        out_specs=[_row_spec(tm, D), _row_spec(tm, kvw), _row_spec(tm, kvw), _row_spec(tm, D), _row_spec(tm, D),
                   _row_spec(tm, 2 * D), _row_spec(tm, D)],
        out_shape=[jax.ShapeDtypeStruct((t_tok, D), BF16), jax.ShapeDtypeStruct((t_tok, kvw), BF16),
                   jax.ShapeDtypeStruct((t_tok, kvw), BF16), jax.ShapeDtypeStruct((t_tok, D), F32),
                   jax.ShapeDtypeStruct((t_tok, D), F32), jax.ShapeDtypeStruct((t_tok, 2 * D), F32),
                   jax.ShapeDtypeStruct((t_tok, D), BF16)],
        scratch=[pltpu.VMEM((INP_W, D), BF16), pltpu.VMEM((2 * D, D), BF16), DMA((_n_pieces(names),))],
        vmem=VMEM_BIG, carries=carries)


def _mix_proj_bwd(dqkv, dxr, dxg, dgpre, h, dh, g, wg, tm=512, carries=()):
    t_tok = h.shape[0]
    nt = t_tok // tm
    names = ("w_in", "w_gate")
    groups = _groups_of(names)

    def body(dqkv_ref, dxr_ref, dxg_ref, dgp_ref, h_ref, dh_ref, g_ref, wg_ref, dhin_ref, dg_ref, win_s, wgt_s, sems):
        i = pl.program_id(0)

        @pl.when(i == 0)
        def _():
            _load_weights({groups[0]: wg_ref}, [("w_in", win_s), ("w_gate", wgt_s)], sems)
            dg_ref[...] = jnp.zeros_like(dg_ref)

        gv = g_ref[...]
        _, xh, r = _rms_fwd(h_ref[...], gv)
        du = _nn(dgp_ref[...], wgt_s[...])
        du = du + _nn(dqkv_ref[...], win_s[pl.ds(Q0, XR0), :])
        du = du + _nn(dxr_ref[...], win_s[pl.ds(XR0, D), :])
        du = du + _nn(dxg_ref[...], win_s[pl.ds(XG0, D), :])
        dx, dg = _rms_bwd(du, xh, r, gv)
        dg_ref[...] += dg
        dhin_ref[...] = dh_ref[...] + dx

    return _call(
        body, name="mix_proj_bwd", grid=(nt,), ins=[dqkv, dxr, dxg, dgpre, h, dh, g, wg[groups[0]]],
        in_specs=[_row_spec(tm, XR0), _row_spec(tm, D), _row_spec(tm, D), _row_spec(tm, 2 * D), _row_spec(tm, D),
                  _row_spec(tm, D), _const_spec((1, D)), ANY],
        out_specs=[_row_spec(tm, D), _const_spec((1, D))],
        out_shape=[jax.ShapeDtypeStruct((t_tok, D), F32), jax.ShapeDtypeStruct((1, D), F32)],
        scratch=[pltpu.VMEM((INP_W, D), BF16), pltpu.VMEM((2 * D, D), BF16), DMA((_n_pieces(names),))],
        vmem=VMEM_BIG, carries=carries)


def _shift_down(x, d, fill):
    row = lax.broadcasted_iota(jnp.int32, x.shape, 0)
    return jnp.where(row >= d, pltpu.roll(x, d, 0), fill)


def _shift_up(x, d, fill):
    rows = x.shape[0]
    row = lax.broadcasted_iota(jnp.int32, x.shape, 0)
    return jnp.where(row < rows - d, pltpu.roll(x, rows - d, 0), fill)


def _scan_rows(a, b, reverse):
    rows = a.shape[0]
    d = 1
    while d < rows:
        if d < 8:
            shift = _shift_up if reverse else _shift_down
            b = a * shift(b, d, 0.0) + b
            a = a * shift(a, d, 1.0)
        elif reverse:
            b = jnp.concatenate([a[:rows - d] * b[d:] + b[:rows - d], b[rows - d:]], axis=0)
            a = jnp.concatenate([a[:rows - d] * a[d:], a[rows - d:]], axis=0)
        else:
            b = jnp.concatenate([b[:d], a[d:] * b[:rows - d] + b[d:]], axis=0)
            a = jnp.concatenate([a[:d], a[d:] * a[:rows - d]], axis=0)
        d *= 2
    return a, b


def _softplus(x):
    return jnp.maximum(x, 0.0) + jnp.log(1.0 + jnp.exp(-jnp.abs(x)))


_GELU_C = math.sqrt(2.0 / math.pi)


def _gelu_parts(x):
    th = jnp.tanh(_GELU_C * (x + 0.044715 * x * x * x))
    val = 0.5 * x * (1.0 + th)
    grad = 0.5 * (1.0 + th) + 0.5 * x * (1.0 - th * th) * _GELU_C * (1.0 + 3.0 * 0.044715 * x * x)
    return val, grad


def _lru_pre(x, halo, cw_ref, cb_ref, wa_ref, wx_ref, ba_ref, bx_ref, lam_ref):
    ext = jnp.concatenate([halo, x], axis=0)
    shifted = [x] + [pltpu.roll(ext, k, 0)[8:] for k in (1, 2, 3)]
    xc = cb_ref[...] + cw_ref[pl.ds(CONV_WIDTH - 1, 1), :] * x
    for k in (1, 2, 3):
        xc = xc + cw_ref[pl.ds(CONV_WIDTH - 1 - k, 1), :] * shifted[k]
    xcb = xc.astype(BF16)
    r = jax.nn.sigmoid(_nn(xcb, wa_ref[...]) + ba_ref[...])
    ig = jax.nn.sigmoid(_nn(xcb, wx_ref[...]) + bx_ref[...])
    sp = _softplus(-lam_ref[...])
    log_a = -LRU_C * r * sp
    a = jnp.exp(log_a)
    th = jnp.tanh(log_a)
    mult = jnp.sqrt(-2.0 * th / (1.0 - th))
    return shifted, xc, xcb, r, ig, sp, a, mult


def _lru_specs(nt, reverse):
    def tt(t):
        return nt - 1 - t if reverse else t
    tile = pl.BlockSpec((LRU_ROWS, LANES), lambda cb, t: (tt(t), cb))
    halo = pl.BlockSpec((8, LANES), lambda cb, t: (jnp.maximum(tt(t) * (LRU_ROWS // 8) - 1, 0), cb))
    vec = pl.BlockSpec((1, LANES), lambda cb, t: (0, cb))
    cw = pl.BlockSpec((CONV_WIDTH, LANES), lambda cb, t: (0, cb))
    mat = pl.BlockSpec((None, LANES, LANES), lambda cb, t: (cb, 0, 0))
    return tile, halo, vec, cw, mat


def _lru_fwd(xr, xg, cw, cb, wa, wx, ba, bx, lam, carries=()):
    t_tok = xr.shape[0]
    nt = t_tok // LRU_ROWS
    rows = LRU_ROWS

    def body(xr_ref, xg_ref, cw_ref, cb_ref, wa_ref, wx_ref, ba_ref, bx_ref, lam_ref, y_ref, h_ref, tail_s, hc_s):
        t = pl.program_id(1)

        @pl.when(t == 0)
        def _():
            tail_s[...] = jnp.zeros_like(tail_s)
            hc_s[...] = jnp.zeros_like(hc_s)

        x = xr_ref[...]
        _, xc, _, _, ig, _, a, mult = _lru_pre(x, tail_s[...], cw_ref, cb_ref, wa_ref, wx_ref, ba_ref, bx_ref, lam_ref)
        tail_s[...] = xr_ref[pl.ds(rows - 8, 8), :]
        acc_a, acc_b = _scan_rows(a, mult * (ig * xc), False)
        hv = acc_b + acc_a * hc_s[...]
        h_ref[...] = hv
        hc_s[...] = h_ref[pl.ds(rows - 1, 1), :]
        gl, _ = _gelu_parts(xg_ref[...])
        y_ref[...] = (hv * gl).astype(BF16)

    tile, _, vec, cws, mat = _lru_specs(nt, False)
    return _call(
        body, name="lru_fwd", grid=(D // LANES, nt), ins=[xr, xg, cw, cb, wa, wx, ba, bx, lam],
        in_specs=[tile, tile, cws, vec, mat, mat, vec, vec, vec],
        out_specs=[tile, tile],
        out_shape=[jax.ShapeDtypeStruct((t_tok, D), BF16), jax.ShapeDtypeStruct((t_tok, D), F32)],
        scratch=[pltpu.VMEM((8, LANES), F32), pltpu.VMEM((1, LANES), F32)], carries=carries)


def _lru_bwd(dy, xr, xg, hseq, cw, cb, wa, wx, ba, bx, lam, carries=()):
    t_tok = xr.shape[0]
    nt = t_tok // LRU_ROWS
    rows = LRU_ROWS

    def body(dy_ref, xr_ref, xrh_ref, xg_ref, h_ref, hh_ref, cw_ref, cb_ref, wa_ref, wx_ref, ba_ref, bx_ref, lam_ref,
             dxr_ref, dxg_ref, dvec_ref, dwa_ref, dwx_ref, gcar_s, acar_s, head_s, tmp_s):
        t = pl.program_id(1)
        first_tile = t == nt - 1

        @pl.when(t == 0)
        def _():
            gcar_s[...] = jnp.zeros_like(gcar_s)
            acar_s[...] = jnp.zeros_like(acar_s)
            head_s[...] = jnp.zeros_like(head_s)
            dvec_ref[...] = jnp.zeros_like(dvec_ref)
            dwa_ref[...] = jnp.zeros_like(dwa_ref)
            dwx_ref[...] = jnp.zeros_like(dwx_ref)

        x = xr_ref[...]
        halo = jnp.where(first_tile, 0.0, xrh_ref[...])
        shifted, xc, xcb, r, ig, sp, a, mult = _lru_pre(x, halo, cw_ref, cb_ref, wa_ref, wx_ref, ba_ref, bx_ref, lam_ref)
        hv = h_ref[...]
        dyv = dy_ref[...]
        gl, glg = _gelu_parts(xg_ref[...])
        dxg_ref[...] = (dyv * hv * glg).astype(BF16)
        acc_a, acc_b = _scan_rows(_shift_up(a, 1, acar_s[...]), dyv * gl, True)
        g = acc_b + acc_a * gcar_s[...]
        hhalo = jnp.where(first_tile, 0.0, hh_ref[...])
        hprev = pltpu.roll(jnp.concatenate([hhalo, hv], axis=0), 1, 0)[8:]
        dmult = g * ig * xc
        dlog_a = a * (g * hprev) - dmult * a * a / mult
        dig = g * mult * xc
        dxc = g * mult * ig
        dzr = (dlog_a * (-LRU_C * sp)) * r * (1.0 - r)
        dzx = dig * ig * (1.0 - ig)
        dzrb, dzxb = dzr.astype(BF16), dzx.astype(BF16)
        dxc = dxc + _nt(dzrb, wa_ref[...]) + _nt(dzxb, wx_ref[...])
        dwa_ref[...] += _tn(xcb, dzrb)
        dwx_ref[...] += _tn(xcb, dzxb)
        dsp = jnp.sum(dlog_a * (-LRU_C * r), axis=0, keepdims=True)
        dlam = dsp * (-jax.nn.sigmoid(-lam_ref[...]))
        vrow = lax.broadcasted_iota(jnp.int32, (8, LANES), 0)
        upd = jnp.where(vrow == 4, jnp.sum(dxc, axis=0, keepdims=True), 0.0)
        upd = jnp.where(vrow == 5, jnp.sum(dzr, axis=0, keepdims=True), upd)
        upd = jnp.where(vrow == 6, jnp.sum(dzx, axis=0, keepdims=True), upd)
        upd = jnp.where(vrow == 7, dlam, upd)
        for k in range(CONV_WIDTH):
            upd = jnp.where(vrow == CONV_WIDTH - 1 - k, jnp.sum(dxc * shifted[k], axis=0, keepdims=True), upd)
        dvec_ref[...] += upd
        ext = jnp.concatenate([dxc, head_s[...]], axis=0)
        dxr = cw_ref[pl.ds(CONV_WIDTH - 1, 1), :] * dxc
        for k in (1, 2, 3):
            dxr = dxr + cw_ref[pl.ds(CONV_WIDTH - 1 - k, 1), :] * pltpu.roll(ext, rows + 8 - k, 0)[:rows]
        dxr_ref[...] = dxr.astype(BF16)
        tmp_s[...] = g
        gcar_s[...] = tmp_s[pl.ds(0, 1), :]
        tmp_s[...] = a
        acar_s[...] = tmp_s[pl.ds(0, 1), :]
        tmp_s[...] = dxc
        head_s[...] = tmp_s[pl.ds(0, 8), :]

    tile, halo, vec, cws, mat = _lru_specs(nt, True)
    return _call(
        body, name="lru_bwd", grid=(D // LANES, nt), ins=[dy, xr, xr, xg, hseq, hseq, cw, cb, wa, wx, ba, bx, lam],
        in_specs=[tile, tile, halo, tile, tile, halo, cws, vec, mat, mat, vec, vec, vec],
        out_specs=[tile, tile, pl.BlockSpec((8, LANES), lambda cb, t: (0, cb)), mat, mat],
        out_shape=[jax.ShapeDtypeStruct((t_tok, D), BF16), jax.ShapeDtypeStruct((t_tok, D), BF16),
                   jax.ShapeDtypeStruct((8, D), F32), jax.ShapeDtypeStruct((D // LANES, LANES, LANES), F32),
                   jax.ShapeDtypeStruct((D // LANES, LANES, LANES), F32)],
        scratch=[pltpu.VMEM((1, LANES), F32), pltpu.VMEM((1, LANES), F32), pltpu.VMEM((8, LANES), F32),
                 pltpu.VMEM((rows, LANES), F32)], carries=carries)


def _t5_bucket_np(rel):
    nb = N_BUCKETS // 2
    max_exact = nb // 2
    ret = np.where(rel > 0, nb, 0)
    n = np.abs(rel)
    nf = np.maximum(n, 1).astype(np.float32)
    large = max_exact + (np.log(nf / np.float32(max_exact)) / np.float32(math.log(MAX_DISTANCE / max_exact))
                         * np.float32(nb - max_exact)).astype(np.int32)
    large = np.minimum(large, nb - 1)
    return ret + np.where(n < max_exact, n, large)


def _bucket_map():
    r = np.arange(QT)[:, None]
    c = np.arange(KW)[None, :]
    j = c - (r // CHUNK) * CHUNK
    band = (j >= 0) & (j < WINDOW + CHUNK)
    return np.where(band, _t5_bucket_np(c - r - WINDOW), -1).astype(np.int32)


def _attn_specs(nt, reverse):
    def tt(i):
        return nt - 1 - i if reverse else i
    kvw = N_KV * LANES
    qs = pl.BlockSpec((QT, D), lambda i: (tt(i), 0))
    cur = pl.BlockSpec((QT, kvw), lambda i: (tt(i), 0))
    prev = pl.BlockSpec((WINDOW, kvw), lambda i: (jnp.maximum(tt(i) * (QT // WINDOW) - 1, 0), 0))
    lse = pl.BlockSpec((QT, LANES), lambda i: (tt(i), 0))
    return qs, cur, prev, lse


REP = N_HEADS // N_KV
SCALE = HEAD_DIM ** -0.5


def _stack_heads(x_ref, g, lo, scale=None):
    parts = []
    for hl in range(REP):
        xs = x_ref[:, pl.ds((2 * g + hl // 2) * LANES, LANES)]
        xs = jnp.where(lo if hl % 2 == 0 else jnp.logical_not(lo), xs, jnp.zeros_like(xs))
        parts.append(xs if scale is None else xs * jnp.asarray(scale, xs.dtype))
    return jnp.concatenate(parts, axis=0)


def _stack_sinks(sink_ref, g, srow):
    sk = jnp.full(srow.shape, sink_ref[REP * g + REP - 1], F32)
    for hl in range(REP - 2, -1, -1):
        sk = jnp.where(srow < (hl + 1) * QT, sink_ref[REP * g + hl], sk)
    return sk


def _attn_fwd(q, kd, vd, bias, sinks, carries=()):
    t_tok = q.shape[0]
    nt = t_tok // QT

    def body(q_ref, kp_ref, kc_ref, vp_ref, vc_ref, bias_ref, sink_ref, o_ref, lse_ref):
        i = pl.program_id(0)
        col = lax.broadcasted_iota(jnp.int32, (1, KW), 1)
        first = jnp.where((i == 0) & (col < WINDOW), NEG_INF, 0.0)
        lane = lax.broadcasted_iota(jnp.int32, (QT, LANES), 1)
        lo = lane < HEAD_DIM
        srow = lax.broadcasted_iota(jnp.int32, (REP * QT, 1), 0)
        lse_t = jnp.zeros((QT, LANES), F32)
        for g in range(N_KV):
            kwin = jnp.concatenate([kp_ref[:, pl.ds(g * LANES, LANES)], kc_ref[:, pl.ds(g * LANES, LANES)]], axis=0)
            vwin = jnp.concatenate([vp_ref[:, pl.ds(g * LANES, LANES)], vc_ref[:, pl.ds(g * LANES, LANES)]], axis=0)
            qst = _stack_heads(q_ref, g, lo, SCALE)
            s = _nt(qst, kwin) + (bias_ref[g] + first)
            sk = _stack_sinks(sink_ref, g, srow)
            m = jnp.maximum(jnp.max(s, axis=-1, keepdims=True), sk)
            e = jnp.exp(s - m)
            l = jnp.sum(e, axis=-1, keepdims=True) + jnp.exp(sk - m)
            p = e / l
            ost = _nn(p.astype(BF16), vwin)
            lse_s = m + jnp.log(l)
            for sl in range(2):
                o_ref[:, pl.ds((2 * g + sl) * LANES, LANES)] = jnp.where(
                    lo, ost[2 * sl * QT:(2 * sl + 1) * QT], ost[(2 * sl + 1) * QT:(2 * sl + 2) * QT]).astype(BF16)
            for hl in range(REP):
                lse_t = jnp.where(lane == REP * g + hl, lse_s[hl * QT:(hl + 1) * QT], lse_t)
        lse_ref[...] = lse_t

    qs, cur, prev, lse = _attn_specs(nt, False)
    return _call(
        body, name="attn_fwd", grid=(nt,), ins=[q, kd, kd, vd, vd, bias, sinks],
        in_specs=[qs, prev, cur, prev, cur, _const_spec((N_KV, REP * QT, KW)), pl.BlockSpec(memory_space=pltpu.SMEM)],
        out_specs=[qs, lse],
        out_shape=[jax.ShapeDtypeStruct((t_tok, D), BF16), jax.ShapeDtypeStruct((t_tok, LANES), F32)],
        vmem=48 * 2 ** 20, carries=carries)


def _attn_bwd(q, kd, vd, o, do, lse, bias, sinks, carries=()):
    t_tok = q.shape[0]
    nt = t_tok // QT
    kvw = N_KV * LANES

    def body(q_ref, kp_ref, kc_ref, vp_ref, vc_ref, o_ref, do_ref, lse_ref, bias_ref, sink_ref,
             dqkv_ref, ds_ref, dsink_ref, kcar_s, vcar_s):
        i = pl.program_id(0)
        tile = nt - 1 - i

        @pl.when(i == 0)
        def _():
            kcar_s[...] = jnp.zeros_like(kcar_s)
            vcar_s[...] = jnp.zeros_like(vcar_s)
            ds_ref[...] = jnp.zeros_like(ds_ref)
            dsink_ref[...] = jnp.zeros_like(dsink_ref)

        col = lax.broadcasted_iota(jnp.int32, (1, KW), 1)
        first = jnp.where((tile == 0) & (col < WINDOW), NEG_INF, 0.0)
        lane = lax.broadcasted_iota(jnp.int32, (QT, LANES), 1)
        lo = lane < HEAD_DIM
        lane_k = lax.broadcasted_iota(jnp.int32, (KW, LANES), 1)
        lane_1 = lax.broadcasted_iota(jnp.int32, (1, LANES), 1)
        srow = lax.broadcasted_iota(jnp.int32, (REP * QT, 1), 0)
        lse_t = lse_ref[...]
        dsink = jnp.zeros((1, LANES), F32)
        for g in range(N_KV):
            kwin = jnp.concatenate([kp_ref[:, pl.ds(g * LANES, LANES)], kc_ref[:, pl.ds(g * LANES, LANES)]], axis=0)
            vwin = jnp.concatenate([vp_ref[:, pl.ds(g * LANES, LANES)], vc_ref[:, pl.ds(g * LANES, LANES)]], axis=0)
            qst = _stack_heads(q_ref, g, lo, SCALE)
            dost = _stack_heads(do_ref, g, lo)
            od = [do_ref[:, pl.ds((2 * g + sl) * LANES, LANES)].astype(F32)
                  * o_ref[:, pl.ds((2 * g + sl) * LANES, LANES)].astype(F32) for sl in range(2)]
            drow = jnp.concatenate([jnp.sum(jnp.where(lo if hl % 2 == 0 else jnp.logical_not(lo), od[hl // 2], 0.0),
                                            axis=-1, keepdims=True) for hl in range(REP)], axis=0)
            lse_s = jnp.concatenate([jnp.sum(jnp.where(lane == REP * g + hl, lse_t, 0.0), axis=-1, keepdims=True)
                                     for hl in range(REP)], axis=0)
            s = _nt(qst, kwin) + (bias_ref[g] + first)
            p = jnp.exp(s - lse_s)
            ds = p * (_nt(dost, vwin) - drow)
            ds_ref[g] += ds
            tsink = -(jnp.exp(_stack_sinks(sink_ref, g, srow) - lse_s) * drow)
            for hl in range(REP):
                dsink = dsink + jnp.where(lane_1 == REP * g + hl,
                                          jnp.sum(tsink[hl * QT:(hl + 1) * QT], axis=0, keepdims=True), 0.0)
            dsb = ds.astype(BF16)
            dqst = _nn(dsb, kwin) * SCALE
            for sl in range(2):
                dqkv_ref[:, pl.ds((2 * g + sl) * LANES, LANES)] = jnp.where(
                    lo, dqst[2 * sl * QT:(2 * sl + 1) * QT], dqst[(2 * sl + 1) * QT:(2 * sl + 2) * QT]).astype(BF16)
            dk_acc = _tn(dsb, qst)
            dv_acc = _tn(p.astype(BF16), dost)
            dk_f = jnp.where(lane_k < HEAD_DIM, dk_acc + pltpu.roll(dk_acc, HEAD_DIM, 1), 0.0)
            dv_f = jnp.where(lane_k < HEAD_DIM, dv_acc + pltpu.roll(dv_acc, HEAD_DIM, 1), 0.0)
            for acc, col0, car in ((dk_f, K0, kcar_s), (dv_f, V0, vcar_s)):
                cs = pl.ds(g * LANES, LANES)
                co = pl.ds(col0 + g * LANES, LANES)
                if QT > WINDOW:
                    dqkv_ref[pl.ds(0, QT - WINDOW), co] = acc[WINDOW:QT].astype(BF16)
                dqkv_ref[pl.ds(QT - WINDOW, WINDOW), co] = (acc[QT:KW] + car[:, cs]).astype(BF16)
                car[:, cs] = acc[0:WINDOW]
        dsink_ref[...] += dsink

    qs, cur, prev, lse_s = _attn_specs(nt, True)
    return _call(
        body, name="attn_bwd", grid=(nt,), ins=[q, kd, kd, vd, vd, o, do, lse, bias, sinks],
        in_specs=[qs, prev, cur, prev, cur, qs, qs, lse_s, _const_spec((N_KV, REP * QT, KW)),
                  pl.BlockSpec(memory_space=pltpu.SMEM)],
        out_specs=[pl.BlockSpec((QT, XR0), lambda i: (nt - 1 - i, 0)), _const_spec((N_KV, REP * QT, KW)),
                   _const_spec((1, LANES))],
        out_shape=[jax.ShapeDtypeStruct((t_tok, XR0), BF16), jax.ShapeDtypeStruct((N_KV, REP * QT, KW), F32),
                   jax.ShapeDtypeStruct((1, LANES), F32)],
        scratch=[pltpu.VMEM((WINDOW, kvw), F32), pltpu.VMEM((WINDOW, kvw), F32)],
        vmem=VMEM_BIG, carries=carries)


def _bias_tile(table, bmap):
    def body(tab_ref, bm_ref, out_ref):
        bm = bm_ref[...]

        def per_head(hd, carry):
            acc = jnp.full((QT, KW), NEG_INF, F32)
            for b in range(N_BUCKETS):
                acc = jnp.where(bm == b, tab_ref[b, hd], acc)
            out_ref[hd] = acc
            return carry

        lax.fori_loop(0, N_HEADS, per_head, 0)

    return pl.pallas_call(
        body, name="bias_tile", out_shape=jax.ShapeDtypeStruct((N_HEADS, QT, KW), F32),
        in_specs=[pl.BlockSpec(memory_space=pltpu.SMEM), pl.BlockSpec(memory_space=pltpu.VMEM)],
        out_specs=pl.BlockSpec(memory_space=pltpu.VMEM))(table, bmap)


def _bias_grad(ds_acc, bmap):
    def body(ds_ref, bm_ref, out_ref):
        row = lax.broadcasted_iota(jnp.int32, (N_BUCKETS, LANES), 0)
        lane = lax.broadcasted_iota(jnp.int32, (N_BUCKETS, LANES), 1)
        bm = bm_ref[...]

        def per_head(hd, res):
            dsv = ds_ref[hd]
            for b in range(N_BUCKETS):
                val = jnp.sum(jnp.sum(jnp.where(bm == b, dsv, 0.0), axis=0, keepdims=True), axis=1, keepdims=True)
                res = jnp.where((row == b) & (lane == hd), val, res)
            return res

        out_ref[...] = lax.fori_loop(0, N_HEADS, per_head, jnp.zeros((N_BUCKETS, LANES), F32))

    return pl.pallas_call(
        body, name="bias_grad", out_shape=jax.ShapeDtypeStruct((N_BUCKETS, LANES), F32),
        in_specs=[pl.BlockSpec(memory_space=pltpu.VMEM), pl.BlockSpec(memory_space=pltpu.VMEM)],
        out_specs=pl.BlockSpec(memory_space=pltpu.VMEM))(ds_acc, bmap)


MIXOUT = ("w_lru_out", "w_attn_out", "w_o")


def _mix_out_fwd(ya_in, o, gs, h, g, wg, tm=256, carries=()):
    t_tok = h.shape[0]
    nt = t_tok // tm
    groups = _groups_of(MIXOUT)

    def body(ya_ref, o_ref, gs_ref, h_ref, g_ref, wg_ref, hout_ref, yao_ref, ybo_ref, z_ref, wa_s, wb_s, wo_s, sems):
        i = pl.program_id(0)

        @pl.when(i == 0)
        def _():
            _load_weights({groups[0]: wg_ref}, list(zip(MIXOUT, (wa_s, wb_s, wo_s))), sems)

        ya = _nn(ya_ref[...], wa_s[...])
        yb = _nn(o_ref[...], wb_s[...])
        yao_ref[...] = ya
        ybo_ref[...] = yb
        merged = gs_ref[:, pl.ds(0, D)] * ya + gs_ref[:, pl.ds(D, D)] * yb
        z = _nn(merged.astype(BF16), wo_s[...])
        z_ref[...] = z
        zn, _, _ = _rms_fwd(z, g_ref[...])
        hout_ref[...] = h_ref[...] + zn

    return _call(
        body, name="mix_out_fwd", grid=(nt,), ins=[ya_in, o, gs, h, g, wg[groups[0]]],
        in_specs=[_row_spec(tm, D), _row_spec(tm, D), _row_spec(tm, 2 * D), _row_spec(tm, D), _const_spec((1, D)), ANY],
        out_specs=[_row_spec(tm, D)] * 4,
        out_shape=[jax.ShapeDtypeStruct((t_tok, D), F32)] * 4,
        scratch=[pltpu.VMEM((D, D), BF16)] * 3 + [DMA((3 * NDEV,))],
        vmem=48 * 2 ** 20, carries=carries)


def _mix_out_bwd(dh, z, ya, yb, gs, g, wg, tm=512, carries=()):
    t_tok = dh.shape[0]
    nt = t_tok // tm
    groups = _groups_of(MIXOUT)

    def body(dh_ref, z_ref, ya_ref, yb_ref, gs_ref, g_ref, wg_ref,
             dyain_ref, do_ref, dgpre_ref, dya_ref, dyb_ref, mg_ref, dz_ref, dg_ref, dbg_ref,
             wa_s, wb_s, wo_s, sems):
        i = pl.program_id(0)

        @pl.when(i == 0)
        def _():
            _load_weights({groups[0]: wg_ref}, list(zip(MIXOUT, (wa_s, wb_s, wo_s))), sems)
            dg_ref[...] = jnp.zeros_like(dg_ref)
            dbg_ref[...] = jnp.zeros_like(dbg_ref)

        gv = g_ref[...]
        _, zh, r = _rms_fwd(z_ref[...], gv)
        dz, dg = _rms_bwd(dh_ref[...], zh, r, gv)
        dg_ref[...] += dg
        dzb = dz.astype(BF16)
        dz_ref[...] = dzb
        ga, gb = gs_ref[:, pl.ds(0, D)], gs_ref[:, pl.ds(D, D)]
        ya_v, yb_v = ya_ref[...], yb_ref[...]
        mg_ref[...] = (ga * ya_v + gb * yb_v).astype(BF16)
        dm = _nt(dzb, wo_s[...])
        dga = dm * ya_v * ga * (1.0 - ga)
        dgb = dm * yb_v * gb * (1.0 - gb)
        dgpre_ref[:, pl.ds(0, D)] = dga.astype(BF16)
        dgpre_ref[:, pl.ds(D, D)] = dgb.astype(BF16)
        dbg_ref[:, pl.ds(0, D)] += jnp.sum(dga, axis=0, keepdims=True)
        dbg_ref[:, pl.ds(D, D)] += jnp.sum(dgb, axis=0, keepdims=True)
        dya = (dm * ga).astype(BF16)
        dyb = (dm * gb).astype(BF16)
        dya_ref[...] = dya
        dyb_ref[...] = dyb
        dyain_ref[...] = _nt(dya, wa_s[...])
        do_ref[...] = _nt(dyb, wb_s[...]).astype(BF16)

    bf = jax.ShapeDtypeStruct((t_tok, D), BF16)
    return _call(
        body, name="mix_out_bwd", grid=(nt,), ins=[dh, z, ya, yb, gs, g, wg[groups[0]]],
        in_specs=[_row_spec(tm, D)] * 4 + [_row_spec(tm, 2 * D), _const_spec((1, D)), ANY],
        out_specs=[_row_spec(tm, D), _row_spec(tm, D), _row_spec(tm, 2 * D)] + [_row_spec(tm, D)] * 4
        + [_const_spec((1, D)), _const_spec((1, 2 * D))],
        out_shape=[jax.ShapeDtypeStruct((t_tok, D), F32), bf, jax.ShapeDtypeStruct((t_tok, 2 * D), BF16), bf, bf, bf, bf,
                   jax.ShapeDtypeStruct((1, D), F32), jax.ShapeDtypeStruct((1, 2 * D), F32)],
        scratch=[pltpu.VMEM((D, D), BF16)] * 3 + [DMA((3 * NDEV,))],
        vmem=VMEM_BIG, carries=carries)


def _sum_parts(parts_list):
    n = len(parts_list)
    _, r, c = parts_list[0].shape
    tc = 256

    def body(*refs):
        for p_ref, o_ref in zip(refs[:n], refs[n:]):
            acc = p_ref[0].astype(F32)
            for s in range(1, NDEV):
                acc = acc + p_ref[s].astype(F32)
            o_ref[...] = acc

    return pl.pallas_call(
        body, name=f"sum_parts_{r}", grid=(c // tc,),
        in_specs=[pl.BlockSpec((NDEV, r, tc), lambda i: (0, 0, i))] * n,
        out_specs=[pl.BlockSpec((r, tc), lambda i: (0, i))] * n,
        out_shape=[jax.ShapeDtypeStruct((r, c), F32)] * n,
        compiler_params=pltpu.CompilerParams(dimension_semantics=("arbitrary",), vmem_limit_bytes=48 * 2 ** 20),
    )(*parts_list)


def _adamw_math(w, g, m, v):
    m = ADAM_B1 * m + (1.0 - ADAM_B1) * g
    v = ADAM_B2 * v + (1.0 - ADAM_B2) * (g * g)
    m_hat = m / (1.0 - ADAM_B1 ** ADAM_STEP)
    v_hat = v / (1.0 - ADAM_B2 ** ADAM_STEP)
    delta = -ADAM_LR * (m_hat / (jnp.sqrt(v_hat) + ADAM_EPS) + ADAM_WD * w)
    return delta, m, v


def _sum_adamw(items, name, carries=()):
    n = len(items)
    c = items[0][1].shape[1]
    tc = LANES

    def body(*refs):
        for k in range(n):
            p_ref, w_ref, m_ref, v_ref = refs[4 * k:4 * k + 4]
            g_ref, d_ref, nm_ref, nv_ref = refs[4 * n + 4 * k:4 * n + 4 * k + 4]
            g = p_ref[0].astype(F32)
            for s in range(1, NDEV):
                g = g + p_ref[s].astype(F32)
            d, m, v = _adamw_math(w_ref[...], g, m_ref[...], v_ref[...])
            g_ref[...] = g
            d_ref[...] = d
            nm_ref[...] = m
            nv_ref[...] = v

    ins, in_specs, out_shape, out_specs = [], [], [], []
    for parts, w, m, v in items:
        r = w.shape[0]
        col = pl.BlockSpec((r, tc), lambda i: (0, i))
        ins += [parts, w, m, v]
        in_specs += [pl.BlockSpec((NDEV, r, tc), lambda i: (0, 0, i)), col, col, col]
        out_shape += [jax.ShapeDtypeStruct((r, c), F32)] * 4
        out_specs += [col] * 4
    outs, carried = _call(body, name=name, grid=(c // tc,), ins=ins, in_specs=in_specs, out_shape=out_shape,
                          out_specs=out_specs, vmem=48 * 2 ** 20, carries=carries)
    return [tuple(outs[4 * k:4 * k + 4]) for k in range(n)], carried


def _adamw_body(n):
    def body(*refs):
        for k in range(n):
            g_ref, w_ref, m_ref, v_ref = refs[4 * k:4 * k + 4]
            d_ref, nm_ref, nv_ref = refs[4 * n + 3 * k:4 * n + 3 * k + 3]
            d, m, v = _adamw_math(w_ref[...], g_ref[...], m_ref[...], v_ref[...])
            d_ref[...] = d
            nm_ref[...] = m
            nv_ref[...] = v
    return body


def _adamw(items):
    n = len(items)
    r, c = items[0][0].shape
    tr = r if r * c <= 2 ** 18 else max(t for t in range(8, 65, 8) if r % t == 0)
    spec = pl.BlockSpec((tr, c), lambda i: (i, 0))
    outs = pl.pallas_call(
        _adamw_body(n), name=f"adamw_{r}x{c}", grid=(r // tr,),
        in_specs=[spec] * (4 * n), out_specs=[spec] * (3 * n),
        out_shape=[jax.ShapeDtypeStruct((r, c), F32)] * (3 * n),
        compiler_params=pltpu.CompilerParams(dimension_semantics=("arbitrary",), vmem_limit_bytes=40 * 2 ** 20),
    )(*[a for it in items for a in it])
    return [tuple(outs[3 * k:3 * k + 3]) for k in range(n)]


def _adamw_small(items):
    n = len(items)
    vm = pl.BlockSpec(memory_space=pltpu.VMEM)
    outs = pl.pallas_call(
        _adamw_body(n), name="adamw_small", in_specs=[vm] * (4 * n), out_specs=[vm] * (3 * n),
        out_shape=[jax.ShapeDtypeStruct(it[1].shape, F32) for it in items for _ in range(3)],
    )(*[a for it in items for a in it])
    return [tuple(outs[3 * k:3 * k + 3]) for k in range(n)]


def _pack_small(arrs):
    rows, offs = [], []
    total = 0
    for a in arrs:
        flat = a.reshape(-1).astype(F32)
        nr = -(-flat.shape[0] // LANES)
        flat = jnp.pad(flat, (0, nr * LANES - flat.shape[0]))
        rows.append(flat.reshape(nr, LANES))
        offs.append((total, nr))
        total += nr
    pad = -total % 8
    if pad:
        rows.append(jnp.zeros((pad, LANES), F32))
    return jnp.concatenate(rows, axis=0), offs


def _unpack_small(pack, offs, shapes):
    out = []
    for (o, nr), shp in zip(offs, shapes):
        size = int(np.prod(shp))
        out.append(pack[o:o + nr].reshape(-1)[:size].reshape(shp))
    return out


def _sum_small(gathered, rows):
    def body(p_ref, o_ref):
        acc = p_ref[pl.ds(0, rows), :]
        for s in range(1, NDEV):
            acc = acc + p_ref[pl.ds(s * rows, rows), :]
        o_ref[...] = acc

    return pl.pallas_call(
        body, name="sum_small", out_shape=jax.ShapeDtypeStruct((rows, LANES), F32),
        in_specs=[pl.BlockSpec(memory_space=pltpu.VMEM)], out_specs=pl.BlockSpec(memory_space=pltpu.VMEM))(gathered)


def _block_diag(w):
    w = w.reshape(D // LANES, 2, LRU_BLOCK, LRU_BLOCK)
    z = jnp.zeros((D // LANES, LRU_BLOCK, LRU_BLOCK), w.dtype)
    top = jnp.concatenate([w[:, 0], z], axis=2)
    bot = jnp.concatenate([z, w[:, 1]], axis=2)
    return jnp.concatenate([top, bot], axis=1)


def _block_diag_grad(dw):
    a = dw[:, :LRU_BLOCK, :LRU_BLOCK]
    b = dw[:, LRU_BLOCK:, LRU_BLOCK:]
    return jnp.stack([a, b], axis=1).reshape(D // LRU_BLOCK, LRU_BLOCK, LRU_BLOCK)


def kernel(x, ffn1_pre_g, ffn1_w1, ffn1_w3, ffn1_w2, ffn1_post_g, mix_pre_g, w_in, conv_w, conv_b, rg_a_w, rg_a_b, rg_x_w, rg_x_b, lru_lambda, w_lru_out, attn_sinks, rel_bias, w_attn_out, w_gate, b_gate, w_o, mix_post_g, ffn2_pre_g, ffn2_w1, ffn2_w3, ffn2_w2, ffn2_post_g, loss_target, m_ffn1_pre_g, m_ffn1_w1, m_ffn1_w3, m_ffn1_w2, m_ffn1_post_g, m_mix_pre_g, m_w_in, m_conv_w, m_conv_b, m_rg_a_w, m_rg_a_b, m_rg_x_w, m_rg_x_b, m_lru_lambda, m_w_lru_out, m_attn_sinks, m_rel_bias, m_w_attn_out, m_w_gate, m_b_gate, m_w_o, m_mix_post_g, m_ffn2_pre_g, m_ffn2_w1, m_ffn2_w3, m_ffn2_w2, m_ffn2_post_g, v_ffn1_pre_g, v_ffn1_w1, v_ffn1_w3, v_ffn1_w2, v_ffn1_post_g, v_mix_pre_g, v_w_in, v_conv_w, v_conv_b, v_rg_a_w, v_rg_a_b, v_rg_x_w, v_rg_x_b, v_lru_lambda, v_w_lru_out, v_attn_sinks, v_rel_bias, v_w_attn_out, v_w_gate, v_b_gate, v_w_o, v_mix_post_g, v_ffn2_pre_g, v_ffn2_w1, v_ffn2_w3, v_ffn2_w2, v_ffn2_post_g):
    names = ["ffn1_pre_g", "ffn1_w1", "ffn1_w3", "ffn1_w2", "ffn1_post_g", "mix_pre_g", "w_in", "conv_w", "conv_b",
             "rg_a_w", "rg_a_b", "rg_x_w", "rg_x_b", "lru_lambda", "w_lru_out", "attn_sinks", "rel_bias", "w_attn_out",
             "w_gate", "b_gate", "w_o", "mix_post_g", "ffn2_pre_g", "ffn2_w1", "ffn2_w3", "ffn2_w2", "ffn2_post_g"]
    ws = dict(zip(names, (ffn1_pre_g, ffn1_w1, ffn1_w3, ffn1_w2, ffn1_post_g, mix_pre_g, w_in, conv_w, conv_b, rg_a_w,
                          rg_a_b, rg_x_w, rg_x_b, lru_lambda, w_lru_out, attn_sinks, rel_bias, w_attn_out, w_gate,
                          b_gate, w_o, mix_post_g, ffn2_pre_g, ffn2_w1, ffn2_w3, ffn2_w2, ffn2_post_g)))
    ms = dict(zip(names, (m_ffn1_pre_g, m_ffn1_w1, m_ffn1_w3, m_ffn1_w2, m_ffn1_post_g, m_mix_pre_g, m_w_in, m_conv_w,
                          m_conv_b, m_rg_a_w, m_rg_a_b, m_rg_x_w, m_rg_x_b, m_lru_lambda, m_w_lru_out, m_attn_sinks,
                          m_rel_bias, m_w_attn_out, m_w_gate, m_b_gate, m_w_o, m_mix_post_g, m_ffn2_pre_g, m_ffn2_w1,
                          m_ffn2_w3, m_ffn2_w2, m_ffn2_post_g)))
    vs = dict(zip(names, (v_ffn1_pre_g, v_ffn1_w1, v_ffn1_w3, v_ffn1_w2, v_ffn1_post_g, v_mix_pre_g, v_w_in, v_conv_w,
                          v_conv_b, v_rg_a_w, v_rg_a_b, v_rg_x_w, v_rg_x_b, v_lru_lambda, v_w_lru_out, v_attn_sinks,
                          v_rel_bias, v_w_attn_out, v_w_gate, v_b_gate, v_w_o, v_mix_post_g, v_ffn2_pre_g, v_ffn2_w1,
                          v_ffn2_w3, v_ffn2_w2, v_ffn2_post_g)))
    me = 4 * lax.axis_index("x") + 2 * lax.axis_index("y") + lax.axis_index("c")
    vec = lambda n: ws[n].reshape(1, -1)

    def shard2d(name):
        if name == "conv":
            row = lax.bitcast_convert_type(conv_w.reshape(CONV_WIDTH, LANES), BF16).reshape(1, D)
            return jnp.concatenate([row, jnp.zeros((LOC["conv"][2] - 1, D), BF16)], axis=0)
        a = ws[name].reshape(ws[name].shape[-2], ws[name].shape[-1])
        return (a.T if name in COL_SHARDED else a).astype(BF16)

    packs = {g: jnp.concatenate([shard2d(n) for n, _ in members], axis=0) for g, members in GROUPS}

    wg = {}
    _, ((wg["ffn1"],),) = _call(None, name="allgather_ffn1", grid=(), ins=[], in_specs=[], out_shape=[], out_specs=[],
                                carries=[_AllGather(packs["ffn1"])])
    conv_rows = wg["ffn1"].reshape(NDEV, GROUP_ROWS["ffn1"], D)[:, LOC["conv"][1]]
    cw = jnp.transpose(lax.bitcast_convert_type(conv_rows.reshape(NDEV, CONV_WIDTH, LANES, 2), F32),
                       (1, 0, 2)).reshape(CONV_WIDTH, D)
    bmap = jnp.asarray(_bucket_map())
    bias = _bias_tile(rel_bias, bmap).reshape(N_KV, REP * QT, KW)
    sinks = attn_sinks.reshape(N_HEADS)
    wa_bd = _block_diag(rg_a_w.reshape(D // LRU_BLOCK, LRU_BLOCK, LRU_BLOCK)).astype(BF16)
    wx_bd = _block_diag(rg_x_w.reshape(D // LRU_BLOCK, LRU_BLOCK, LRU_BLOCK)).astype(BF16)
    lru_args = (cw, vec("conv_b"), wa_bd, wx_bd, vec("rg_a_b"), vec("rg_x_b"), vec("lru_lambda"))
    x2, tgt = x[0], loss_target[0]

    (h1, a1, b1, f1, nb1), ((wg["mixin"],),) = _ffn_fwd(
        x2, vec("ffn1_pre_g"), vec("ffn1_post_g"), wg, ("ffn1_w1", "ffn1_w3", "ffn1_w2"),
        carries=[_AllGather(packs["mixin"])])
    (q, kd, vd, xr, xg, gs, ub), ((wg["mixout"],),) = _mix_proj_fwd(
        h1, vec("mix_pre_g"), vec("b_gate"), wg, carries=[_AllGather(packs["mixout"])])
    (ya_in, hseq), ((wg["ffn2a"],),) = _lru_fwd(xr, xg, *lru_args, carries=[_AllGather(packs["ffn2a"])])
    (o, lse), ((wg["ffn2b"],),) = _attn_fwd(q, kd, vd, bias, sinks, carries=[_AllGather(packs["ffn2b"])])
    (h2, ya, yb, z), _ = _mix_out_fwd(ya_in, o, gs, h1, vec("mix_post_g"), wg)
    (_, a2, b2, f2, nb2, dy, loss_part), _ = _ffn_fwd(h2, vec("ffn2_pre_g"), vec("ffn2_post_g"), wg,
                                                     ("ffn2_w1", "ffn2_w3", "ffn2_w2"), target=tgt)
    loss = lax.psum(loss_part[0, 0], ("x", "y", "c"))

    gsm, parts = {}, {}
    rs = lambda *grads_: [_ReduceScatterSend(list(grads_))]
    ffr = FF // NDEV
    (dab, s_act, dfb, gsm["ffn2_post_g"]), _ = _ffn_bwd_a(dy, f2, a2, b2, vec("ffn2_post_g"), wg, "ffn2_w2")
    g_w2, _ = _dw(s_act, dfb, FF // 2, "dw_ffn2_w2")
    g_w13, ((parts["ffn2_w2"],),) = _dw(dab, nb2, D // 2, "dw_ffn2_w13", carries=rs(g_w2))
    (dh2, gsm["ffn2_pre_g"]), ((parts["ffn2_w1"],),) = _ffn_bwd_dx(
        dab, h2, dy, vec("ffn2_pre_g"), wg, "ffn2_w1", "ffn2_w3", carries=rs((g_w13, 0, ffr, ffr)))
    (dya_in, do, dgpre, dya, dyb, mg, dzb, gsm["mix_post_g"], gsm["b_gate"]), ((parts["ffn2_w3"],),) = _mix_out_bwd(
        dh2, z, ya, yb, gs, vec("mix_post_g"), wg, carries=rs((g_w13, FF, ffr, ffr)))
    g_wa, _ = _dw(ya_in, dya, D // 2, "dw_w_lru_out")
    g_wb, _ = _dw(o, dyb, D // 2, "dw_w_attn_out")
    g_wo, _ = _dw(mg, dzb, D // 2, "dw_w_o")
    (dqkv, ds_acc, dsink), (mixout_parts,) = _attn_bwd(
        q, kd, vd, o, do, lse, bias, sinks, carries=rs(g_wa, g_wb, g_wo))
    parts["w_lru_out"], parts["w_attn_out"], parts["w_o"] = mixout_parts
    (dxr, dxg, dvec, dwa, dwx), _ = _lru_bwd(dya_in, xr, xg, hseq, *lru_args)
    g_qkv, _ = _dw(dqkv, ub, D // 2, "dw_w_in_qkv")
    g_xr, _ = _dw(dxr, ub, D // 2, "dw_w_in_xr")
    g_xg, _ = _dw(dxg, ub, D // 2, "dw_w_in_xg")
    g_wgate, _ = _dw(dgpre, ub, D // 2, "dw_w_gate")
    g_win = jnp.concatenate(
        [g_qkv[:D]] + [g_qkv[c0 + gi * LANES:c0 + gi * LANES + HEAD_DIM] for c0 in (K0, V0) for gi in range(N_KV)]
        + [g_xr, g_xg], axis=0)
    (dh1, gsm["mix_pre_g"]), ((parts["w_in"],),) = _mix_proj_bwd(
        dqkv, dxr, dxg, dgpre, h1, dh2, vec("mix_pre_g"), wg, carries=rs(g_win))
    gsm["conv_w"] = dvec[0:CONV_WIDTH]
    gsm["conv_b"], gsm["rg_a_b"], gsm["rg_x_b"], gsm["lru_lambda"] = dvec[4], dvec[5], dvec[6], dvec[7]
    gsm["rg_a_w"] = _block_diag_grad(dwa)
    gsm["rg_x_w"] = _block_diag_grad(dwx)
    gsm["attn_sinks"] = dsink[0, :N_HEADS]
    gsm["rel_bias"] = _bias_grad(ds_acc.reshape(N_HEADS, QT, KW), bmap)[:, :N_HEADS]
    late = ("ffn1_post_g", "ffn1_pre_g")
    early = tuple(n for n in SMALL if n not in late)
    early_pack, early_offs = _pack_small([gsm[n] for n in early])
    (dab, s_act, dfb, gsm["ffn1_post_g"]), ((parts["w_gate"],), (early_all,)) = _ffn_bwd_a(
        dh1, f1, a1, b1, vec("ffn1_post_g"), wg, "ffn1_w2", carries=rs(g_wgate) + [_AllGather(early_pack)])
    g_w2, _ = _dw(s_act, dfb, FF // 2, "dw_ffn1_w2")
    g_w13, ((parts["ffn1_w2"],),) = _dw(dab, nb1, D // 2, "dw_ffn1_w13", carries=rs(g_w2))
    head = 64
    (grad_x, gsm["ffn1_pre_g"]), ((parts["ffn1_w1"], w3_head),) = _ffn_bwd_dx(
        dab, x2, dh1, vec("ffn1_pre_g"), wg, "ffn1_w1", "ffn1_w3",
        carries=rs((g_w13, 0, ffr, ffr), (g_w13, FF, ffr, head)))
    late_pack, late_offs = _pack_small([gsm[n] for n in late])

    grads, delta, new_m, new_v = {}, {}, {}, {}
    t_form = [n for n in COL_SHARDED if ws[n].shape[-1] % LANES]
    view = {n: (lambda a: a.reshape(a.shape[-2], a.shape[-1]).T) if n in t_form
            else (lambda a: a.reshape(a.shape[-2], a.shape[-1])) for n in BIG}
    unview = {n: (lambda a: a.T) if n in t_form else (lambda a: a) for n in BIG}

    def update(group, name, carries=()):
        res, carried = _sum_adamw([(parts[n], view[n](ws[n]), view[n](ms[n]), view[n](vs[n])) for n in group], name,
                                  carries=carries)
        for n, quad in zip(group, res):
            grads[n], delta[n], new_m[n], new_v[n] = (unview[n](a) for a in quad)
        return carried

    ready = [n for n in BIG if n not in ("ffn1_w3", "w_gate")]
    (w3_rest,), (late_all,) = update(ready, "update_ready",
                                     rs((g_w13, FF + head, ffr, ffr - head)) + [_AllGather(late_pack)])
    parts["ffn1_w3"] = jnp.concatenate([w3_head, w3_rest], axis=1)
    update(["ffn1_w3"], "update_last")
    (g_gate_t,) = _sum_parts([parts["w_gate"]])
    grads["w_gate"] = g_gate_t.T
    ((delta["w_gate"], new_m["w_gate"], new_v["w_gate"]),) = _adamw(
        [(grads["w_gate"], view["w_gate"](ws["w_gate"]), view["w_gate"](ms["w_gate"]), view["w_gate"](vs["w_gate"]))])
    for group, gathered, offs in ((early, early_all, early_offs), (late, late_all, late_offs)):
        total = _sum_small(gathered, gathered.shape[0] // NDEV)
        shapes = [(CONV_WIDTH, D) if n == "conv_w" else ws[n].shape for n in group]
        for n, g in zip(group, _unpack_small(total, offs, shapes)):
            grads[n] = g
    grads["conv_w"] = lax.dynamic_slice(grads["conv_w"], (0, me * LANES), (CONV_WIDTH, LANES)).reshape(conv_w.shape)

    flat2d = lambda a: a.reshape(-1, a.shape[-1])
    res = _adamw_small([(flat2d(grads[n].reshape(ws[n].shape)), flat2d(ws[n]), flat2d(ms[n]), flat2d(vs[n]))
                        for n in SMALL])
    for n, (d_, m_, v_) in zip(SMALL, res):
        delta[n], new_m[n], new_v[n] = d_, m_, v_

    outs = [loss, grad_x.reshape(x.shape)]
    for src in (grads, delta, new_m, new_v):
        outs += [src[n].reshape(ws[n].shape) for n in names]
    return tuple(outs)
```

```python
import functools
import math
import operator

import numpy as np
import jax
import jax.numpy as jnp
from jax import lax
from jax.experimental import pallas as pl
from jax.experimental.pallas import tpu as pltpu

F32, BF16 = jnp.float32, jnp.bfloat16

NDEV = 8
D = 1024
FF = 2816
N_HEADS, N_KV, HEAD_DIM = 16, 4, 64
CHUNK, WINDOW = 64, 128
N_BUCKETS, MAX_DISTANCE = 32, 128
LRU_BLOCK = 64
CONV_WIDTH = 4
LRU_C = 8.0
RMS_EPS = 1e-6
NEG_INF = -1e30
LANES = 128
QT = 128
KW = QT + WINDOW
LRU_ROWS = 512
IN_W = D + 2 * N_KV * HEAD_DIM + 2 * D
INP_W = D + 2 * N_KV * LANES + 2 * D
VMEM_BIG = 58 * 2 ** 20

ADAM_LR, ADAM_B1, ADAM_B2, ADAM_EPS, ADAM_WD, ADAM_STEP = 0.001, 0.9, 0.999, 1e-08, 0.01, 10

GROUPS = (("ffn1", (("ffn1_w1", FF // NDEV), ("ffn1_w3", FF // NDEV), ("ffn1_w2", FF // NDEV), ("conv", 16))),
          ("mixin", (("w_in", IN_W // NDEV), ("w_gate", 2 * D // NDEV))),
          ("mixout", (("w_lru_out", D // NDEV), ("w_attn_out", D // NDEV), ("w_o", D // NDEV))),
          ("ffn2a", (("ffn2_w1", FF // NDEV), ("ffn2_w3", FF // NDEV))),
          ("ffn2b", (("ffn2_w2", FF // NDEV),)))
LOC, GROUP_ROWS = {}, {}
for _g, _members in GROUPS:
    _o = 0
    for _n, _r in _members:
        LOC[_n] = (_g, _o, _r)
        _o += _r
    GROUP_ROWS[_g] = _o
BIG = tuple(n for _, members in GROUPS for n, _ in members if n != "conv")
COL_SHARDED = ("ffn1_w1", "ffn1_w3", "w_in", "w_gate", "ffn2_w1", "ffn2_w3")

SMALL = ("ffn1_pre_g", "ffn1_post_g", "mix_pre_g", "conv_w", "conv_b", "rg_a_w", "rg_a_b", "rg_x_w", "rg_x_b",
         "lru_lambda", "attn_sinks", "rel_bias", "b_gate", "mix_post_g", "ffn2_pre_g", "ffn2_post_g")

MESH = pl.DeviceIdType.MESH
ANY = pl.BlockSpec(memory_space=pl.ANY)
DMA = pltpu.SemaphoreType.DMA


def _nn(a, b):
    return lax.dot_general(a, b, (((1,), (0,)), ((), ())), preferred_element_type=F32)


def _nt(a, b):
    return lax.dot_general(a, b, (((1,), (1,)), ((), ())), preferred_element_type=F32)


def _tn(a, b):
    return lax.dot_general(a, b, (((0,), (0,)), ((), ())), preferred_element_type=F32)


def _rms_fwd(x, g):
    r = lax.rsqrt(jnp.mean(x * x, axis=-1, keepdims=True) + RMS_EPS)
    xh = x * r
    return xh * g, xh, r


def _rms_bwd(dn, xh, r, g):
    dxh = dn * g
    dx = r * (dxh - xh * jnp.mean(dxh * xh, axis=-1, keepdims=True))
    return dx, jnp.sum(dn * xh, axis=0, keepdims=True)


def _row_spec(tm, c):
    return pl.BlockSpec((tm, c), lambda i: (i, 0))


def _const_spec(shape):
    nd = len(shape)
    return pl.BlockSpec(shape, lambda i: (0,) * nd)


class _AllGather:
    def __init__(self, shard):
        self.m, n = shard.shape
        self.ins = [shard]
        self.out_shape = [jax.ShapeDtypeStruct((NDEV * self.m, n), shard.dtype)]
        self.scratch = [DMA((7,)), DMA((7,)), DMA]

    def _copies(self, ins, outs, scr, all_of_them):
        x_ref, out_ref = ins[0], outs[0]
        send_sems, recv_sems, local_sem = scr
        x, y, c = lax.axis_index("x"), lax.axis_index("y"), lax.axis_index("c")
        me, sibling = (x, y, c), (x, y, 1 - c)
        chips = [(1 - x, y), (x, 1 - y), (1 - x, 1 - y)]
        m = self.m

        def rows(px, py, pc):
            return out_ref.at[pl.ds((4 * px + 2 * py + pc) * m, m), :]

        def copy(k, block, to, src=None):
            return pltpu.make_async_remote_copy(
                src_ref=rows(*block) if src is None else src, dst_ref=rows(*block),
                send_sem=send_sems.at[k], recv_sem=recv_sems.at[k], device_id=to, device_id_type=MESH)

        mine = pltpu.make_async_copy(x_ref, rows(*me), local_sem)
        first = [copy(0, me, sibling, src=x_ref)] + [copy(1 + j, me, (*chip, c), src=x_ref)
                                                     for j, chip in enumerate(chips)]
        if not all_of_them:
            return mine, first
        passed = [copy(4 + j, (*chip, c), sibling) for j, chip in enumerate(chips)]
        landed = [copy(1 + j, (*chip, c), me) for j, chip in enumerate(chips)]
        from_sibling = [copy(0, sibling, me)] + [copy(4 + j, (*chip, 1 - c), me) for j, chip in enumerate(chips)]
        return mine, first, passed, landed, from_sibling

    def start(self, ins, outs, scr):
        mine, first = self._copies(ins, outs, scr, False)
        mine.start()
        for cp in first:
            cp.start()

    def finish(self, ins, outs, scr):
        mine, first, passed, landed, from_sibling = self._copies(ins, outs, scr, True)
        for cp_in, cp_on in zip(landed, passed):
            cp_in.wait_recv()
            cp_on.start()
        for cp in from_sibling:
            cp.wait_recv()
        for cp in first + passed:
            cp.wait_send()
        mine.wait()


class _ReduceScatterSend:
    def __init__(self, grads):
        grads = [g if isinstance(g, tuple) else (g, 0, g.shape[0] // NDEV, g.shape[0] // NDEV) for g in grads]
        self.nw = len(grads)
        self.base = [b for _, b, _, _ in grads]
        self.stride = [s for _, _, s, _ in grads]
        self.rows = [r for _, _, _, r in grads]
        self.ins = [g for g, _, _, _ in grads]
        self.out_shape = [jax.ShapeDtypeStruct((NDEV, r, g.shape[1]), g.dtype) for g, _, _, r in grads]
        self.scratch = [DMA((self.nw, NDEV - 1)), DMA((self.nw, NDEV - 1)), DMA((self.nw,))]

    def _copies(self, g_refs, r_refs, scr, want):
        send_sems, recv_sems, local_sems = scr
        x, y, c = lax.axis_index("x"), lax.axis_index("y"), lax.axis_index("c")
        me = 4 * x + 2 * y + c
        rows, base, stride = self.rows, self.base, self.stride
        out = []
        if want == "local":
            for w in range(self.nw):
                out.append(pltpu.make_async_copy(g_refs[w].at[pl.ds(base[w] + me * stride[w], rows[w])],
                                                 r_refs[w].at[me], local_sems.at[w]))
            return out
        for k in range(1, NDEV):
            px, py, pc = x ^ (k >> 2), y ^ ((k >> 1) & 1), c ^ (k & 1)
            peer = 4 * px + 2 * py + pc
            for w in range(self.nw):
                sems = dict(send_sem=send_sems.at[w, k - 1], recv_sem=recv_sems.at[w, k - 1],
                            device_id=(px, py, pc), device_id_type=MESH)
                if want == "send":
                    out.append(pltpu.make_async_remote_copy(
                        src_ref=g_refs[w].at[pl.ds(base[w] + peer * stride[w], rows[w])], dst_ref=r_refs[w].at[me],
                        **sems))
                else:
                    out.append(pltpu.make_async_remote_copy(
                        src_ref=g_refs[w].at[pl.ds(0, rows[w])], dst_ref=r_refs[w].at[peer], **sems))
        return out

    def start(self, ins, outs, scr):
        for cp in self._copies(ins, outs, scr, "local") + self._copies(ins, outs, scr, "send"):
            cp.start()

    def finish(self, ins, outs, scr):
        for cp in self._copies(ins, outs, scr, "recv"):
            cp.wait_recv()
        for cp in self._copies(ins, outs, scr, "send"):
            cp.wait_send()
        for cp in self._copies(ins, outs, scr, "local"):
            cp.wait()


def _call(body, *, name, grid, ins, in_specs, out_shape, out_specs, scratch=(), vmem=None, carries=()):
    n_in, n_out, n_scr = len(ins), len(out_shape), len(scratch)
    ng = len(grid)

    def split(refs):
        pos = [0]

        def take(k):
            part = refs[pos[0]:pos[0] + k]
            pos[0] += k
            return part

        i_refs = take(n_in)
        c_in = [take(len(c.ins)) for c in carries]
        o_refs = take(n_out)
        c_out = [take(len(c.out_shape)) for c in carries]
        s_refs = take(n_scr)
        c_scr = [take(len(c.scratch)) for c in carries]
        return i_refs, o_refs, s_refs, list(zip(carries, c_in, c_out, c_scr))

    def full(*refs):
        i_refs, o_refs, s_refs, cparts = split(refs)
        if ng == 0:
            for c, a, b, s in cparts:
                c.start(a, b, s)
            for c, a, b, s in cparts:
                c.finish(a, b, s)
            return
        ids = [pl.program_id(a) for a in range(ng)]
        if cparts:
            @pl.when(functools.reduce(operator.and_, [i == 0 for i in ids]))
            def _():
                for c, a, b, s in cparts:
                    c.start(a, b, s)

        body(*i_refs, *o_refs, *s_refs)
        if cparts:
            @pl.when(functools.reduce(operator.and_, [i == g - 1 for i, g in zip(ids, grid)]))
            def _():
                for c, a, b, s in cparts:
                    c.finish(a, b, s)

    all_ins = list(ins) + [a for c in carries for a in c.ins]
    all_in_specs = list(in_specs) + [ANY for c in carries for _ in c.ins]
    all_out_shape = list(out_shape) + [s for c in carries for s in c.out_shape]
    all_out_specs = list(out_specs) + [ANY for c in carries for _ in c.out_shape]
    all_scratch = list(scratch) + [s for c in carries for s in c.scratch]
    kwargs = dict(grid=grid) if ng else {}
    outs = pl.pallas_call(
        full, name=name, in_specs=all_in_specs, out_specs=all_out_specs, out_shape=all_out_shape,
        scratch_shapes=all_scratch,
        compiler_params=pltpu.CompilerParams(dimension_semantics=("arbitrary",) * ng if ng else None,
                                             vmem_limit_bytes=vmem),
        **kwargs)(*all_ins)
    outs = list(outs)
    res, pos = outs[:n_out], n_out
    carried = []
    for c in carries:
        carried.append(outs[pos:pos + len(c.out_shape)])
        pos += len(c.out_shape)
    return res, carried


def _groups_of(names):
    out = []
    for n in names:
        if LOC[n][0] not in out:
            out.append(LOC[n][0])
    return out


def _weight_pieces(name):
    g, off, rows = LOC[name]
    return [(d * GROUP_ROWS[g] + off, d * rows, rows) for d in range(NDEV)]


def _win_pieces():
    kv = N_KV * HEAD_DIM
    pieces = [(0, 0, D)]
    for g0, d0 in ((D, D), (D + kv, D + N_KV * LANES)):
        for g in range(N_KV):
            for half in range(2):
                pieces.append((g0 + g * HEAD_DIM, d0 + g * LANES + half * HEAD_DIM, HEAD_DIM))
    pieces.append((D + 2 * kv, D + 2 * N_KV * LANES, D))
    pieces.append((D + 2 * kv + D, D + 2 * N_KV * LANES + D, D))
    grp, off, rows = LOC["w_in"]
    out = []
    for g0, d0, n in pieces:
        while n > 0:
            dev, loc = divmod(g0, rows)
            m = min(n, rows - loc)
            out.append((dev * GROUP_ROWS[grp] + off + loc, d0, m))
            g0, d0, n = g0 + m, d0 + m, n - m
    return out


def _start_loads(src_ref, dst_ref, pieces, sems, base):
    cps = []
    for j, (s, d, n) in enumerate(pieces):
        cp = pltpu.make_async_copy(src_ref.at[pl.ds(s, n)], dst_ref.at[pl.ds(d, n)], sems.at[base + j])
        cp.start()
        cps.append(cp)
    return cps


def _load_weights(wrefs, targets, sems):
    cps, base = [], 0
    for name, dst in targets:
        pieces = _win_pieces() if name == "w_in" else _weight_pieces(name)
        cps += _start_loads(wrefs[LOC[name][0]], dst, pieces, sems, base)
        base += len(pieces)
    for cp in cps:
        cp.wait()


def _n_pieces(names):
    return sum(len(_win_pieces()) if n == "w_in" else NDEV for n in names)


def _dw(lhs, rhs, chunk, name, carries=()):
    nq = 4
    if lhs.ndim == 3:
        nch, t_tok, chunk = lhs.shape
        c = nch * chunk
        tq = t_tok // nq
        lhs_specs = [pl.BlockSpec((None, tq, chunk), lambda i, k=k: (i, k, 0)) for k in range(nq)]
    else:
        t_tok, c = lhs.shape
        tq = t_tok // nq
        lhs_specs = [pl.BlockSpec((tq, chunk), lambda i, k=k: (k, i)) for k in range(nq)]

    def body(*refs):
        lhs_refs, (rhs_ref, out_ref, rhs_s, sems) = refs[:nq], refs[nq:]
        cps = [pltpu.make_async_copy(rhs_ref.at[pl.ds(k * tq, tq)], rhs_s.at[pl.ds(k * tq, tq)], sems.at[k])
               for k in range(nq)]
        first = pl.program_id(0) == 0

        @pl.when(first)
        def _():
            for cp in cps:
                cp.start()

        for o in range(0, chunk, D // 2):
            n = min(D // 2, chunk - o)
            acc = None
            for k in range(nq):
                if o == 0:
                    @pl.when(first)
                    def _():
                        cps[k].wait()

                part = _tn(lhs_refs[k][:, pl.ds(o, n)], rhs_s[pl.ds(k * tq, tq), :])
                acc = part if acc is None else acc + part
            out_ref[pl.ds(o, n), :] = acc.astype(BF16)

    (out,), carried = _call(
        body, name=name, grid=(c // chunk,), ins=[lhs] * nq + [rhs],
        in_specs=lhs_specs + [ANY],
        out_specs=[pl.BlockSpec((chunk, D), lambda i: (i, 0))],
        out_shape=[jax.ShapeDtypeStruct((c, D), BF16)],
        scratch=[pltpu.VMEM((t_tok, D), BF16), DMA((nq,))], vmem=VMEM_BIG, carries=carries)
    return out, carried


def _silu_parts(a):
    sig = jax.nn.sigmoid(a)
    return sig, a * sig


FC = 256


def _ffn_fwd(h, gpre, gpost, wg, names, target=None, tm=512, carries=()):
    t_tok = h.shape[0]
    nt = t_tok // tm
    with_loss = target is not None
    groups = _groups_of(names)

    def body(*refs):
        refs = list(refs)
        h_ref, gpre_ref, gpost_ref = refs[:3]
        del refs[:3]
        tgt_ref = refs.pop(0) if with_loss else None
        wrefs = dict(zip(groups, refs[:len(groups)]))
        del refs[:len(groups)]
        hout_ref, a_ref, b_ref, f_ref, nb_ref = refs[:5]
        del refs[:5]
        if with_loss:
            dy_ref, loss_ref = refs[:2]
            del refs[:2]
        w1_s, w3_s, w2_s, sems = refs
        i = pl.program_id(0)

        @pl.when(i == 0)
        def _():
            _load_weights(wrefs, list(zip(names, (w1_s, w3_s, w2_s))), sems)
            if with_loss:
                loss_ref[...] = jnp.zeros_like(loss_ref)

        x = h_ref[...]
        n, _, _ = _rms_fwd(x, gpre_ref[...])
        nb = n.astype(BF16)
        nb_ref[...] = nb
        f = jnp.zeros((tm, D), F32)
        for c0 in range(0, FF, FC):
            a = _nt(nb, w1_s[pl.ds(c0, FC), :])
            b = _nt(nb, w3_s[pl.ds(c0, FC), :])
            _, sl = _silu_parts(a)
            a_ref[:, pl.ds(c0, FC)] = a.astype(BF16)
            b_ref[:, pl.ds(c0, FC)] = b.astype(BF16)
            f = f + _nn((sl * b).astype(BF16), w2_s[pl.ds(c0, FC), :])
        f_ref[...] = f
        fn, _, _ = _rms_fwd(f, gpost_ref[...])
        y = x + 0.5 * fn
        hout_ref[...] = y
        if with_loss:
            err = y - tgt_ref[...]
            dy_ref[...] = err * (1.0 / D)
            loss_ref[...] += jnp.sum(jnp.sum(err * err, axis=-1, keepdims=True), axis=0, keepdims=True) * (0.5 / D)

    ins = [h, gpre, gpost] + ([target] if with_loss else []) + [wg[g] for g in groups]
    in_specs = [_row_spec(tm, D), _const_spec((1, D)), _const_spec((1, D))]
    in_specs += ([_row_spec(tm, D)] if with_loss else []) + [ANY] * len(groups)
    out_shape = [jax.ShapeDtypeStruct((t_tok, D), F32), jax.ShapeDtypeStruct((t_tok, FF), BF16),
                 jax.ShapeDtypeStruct((t_tok, FF), BF16), jax.ShapeDtypeStruct((t_tok, D), F32),
                 jax.ShapeDtypeStruct((t_tok, D), BF16)]
    out_specs = [_row_spec(tm, D), _row_spec(tm, FF), _row_spec(tm, FF), _row_spec(tm, D), _row_spec(tm, D)]
    if with_loss:
        out_shape += [jax.ShapeDtypeStruct((t_tok, D), F32), jax.ShapeDtypeStruct((1, LANES), F32)]
        out_specs += [_row_spec(tm, D), _const_spec((1, LANES))]
    return _call(body, name="ffn_fwd_" + names[0][:4], grid=(nt,), ins=ins, in_specs=in_specs,
                 out_shape=out_shape, out_specs=out_specs,
                 scratch=[pltpu.VMEM((FF, D), BF16)] * 3 + [DMA((3 * NDEV,))], vmem=VMEM_BIG, carries=carries)


FH = FF // 2
HALF_PIECES = ((0, 256), (256, 256), (512, 256), (768, 256), (1024, 256), (1280, 128))


def _ffn_bwd_a(dh, f, a, b, gpost, wg, name_w2, tm=512, carries=()):
    t_tok = dh.shape[0]
    nt = t_tok // tm
    groups = _groups_of([name_w2])

    def body(dh_ref, f_ref, a_ref, b_ref, gpost_ref, wg_ref, dab_ref, s_ref, df_ref, dgp_ref, w2_s, sems):
        i = pl.program_id(0)

        @pl.when(i == 0)
        def _():
            _load_weights({groups[0]: wg_ref}, [(name_w2, w2_s)], sems)
            dgp_ref[...] = jnp.zeros_like(dgp_ref)

        fv = f_ref[...]
        _, fh, r = _rms_fwd(fv, gpost_ref[...])
        df, dg = _rms_bwd(0.5 * dh_ref[...], fh, r, gpost_ref[...])
        dgp_ref[...] += dg
        dfb = df.astype(BF16)
        df_ref[...] = dfb
        for half in range(2):
            for o, n in HALF_PIECES:
                c0 = half * FH + o
                ds = _nt(dfb, w2_s[pl.ds(c0, n), :])
                av = a_ref[:, pl.ds(c0, n)].astype(F32)
                bv = b_ref[:, pl.ds(c0, n)].astype(F32)
                sig, sl = _silu_parts(av)
                dab_ref[half, :, pl.ds(o, n)] = (ds * bv * (sig * (1.0 + av * (1.0 - sig)))).astype(BF16)
                dab_ref[2 + half, :, pl.ds(o, n)] = (ds * sl).astype(BF16)
                s_ref[half, :, pl.ds(o, n)] = (sl * bv).astype(BF16)

    return _call(
        body, name="ffn_bwd_a_" + name_w2[:4], grid=(nt,), ins=[dh, f, a, b, gpost, wg[groups[0]]],
        in_specs=[_row_spec(tm, D), _row_spec(tm, D), _row_spec(tm, FF), _row_spec(tm, FF), _const_spec((1, D)), ANY],
        out_specs=[pl.BlockSpec((4, tm, FH), lambda i: (0, i, 0)), pl.BlockSpec((2, tm, FH), lambda i: (0, i, 0)),
                   _row_spec(tm, D), _const_spec((1, D))],
        out_shape=[jax.ShapeDtypeStruct((4, t_tok, FH), BF16), jax.ShapeDtypeStruct((2, t_tok, FH), BF16),
                   jax.ShapeDtypeStruct((t_tok, D), BF16), jax.ShapeDtypeStruct((1, D), F32)],
        scratch=[pltpu.VMEM((FF, D), BF16), DMA((NDEV,))],
        vmem=VMEM_BIG, carries=carries)


def _ffn_bwd_dx(dab, h, dh, gpre, wg, name_w1, name_w3, tm=512, carries=()):
    t_tok = dh.shape[0]
    nt = t_tok // tm
    groups = _groups_of([name_w1, name_w3])

    def body(*refs):
        dab_refs, (h_ref, dh_ref, gpre_ref) = refs[:4], refs[4:7]
        wrefs = dict(zip(groups, refs[7:7 + len(groups)]))
        dhin_ref, dgp_ref, w13_s, sems = refs[7 + len(groups):]
        i = pl.program_id(0)

        @pl.when(i == 0)
        def _():
            _load_weights(wrefs, [(name_w1, w13_s.at[pl.ds(0, FF)]), (name_w3, w13_s.at[pl.ds(FF, FF)])], sems)
            dgp_ref[...] = jnp.zeros_like(dgp_ref)

        g = gpre_ref[...]
        _, xh, r = _rms_fwd(h_ref[...], g)
        dn = _nn(dab_refs[0][...], w13_s[pl.ds(0, FH), :])
        for k in range(1, 4):
            dn = dn + _nn(dab_refs[k][...], w13_s[pl.ds(k * FH, FH), :])
        dx, dg = _rms_bwd(dn, xh, r, g)
        dgp_ref[...] += dg
        dhin_ref[...] = dh_ref[...] + dx

    return _call(
        body, name="ffn_bwd_dx_" + name_w1[:4], grid=(nt,), ins=[dab] * 4 + [h, dh, gpre] + [wg[g] for g in groups],
        in_specs=[pl.BlockSpec((None, tm, FH), lambda i, k=k: (k, i, 0)) for k in range(4)]
        + [_row_spec(tm, D), _row_spec(tm, D), _const_spec((1, D))] + [ANY] * len(groups),
        out_specs=[_row_spec(tm, D), _const_spec((1, D))],
        out_shape=[jax.ShapeDtypeStruct((t_tok, D), F32), jax.ShapeDtypeStruct((1, D), F32)],
        scratch=[pltpu.VMEM((2 * FF, D), BF16), DMA((2 * NDEV,))],
        vmem=VMEM_BIG, carries=carries)


Q0, K0, V0, XR0, XG0 = 0, D, D + N_KV * LANES, D + 2 * N_KV * LANES, 2 * D + 2 * N_KV * LANES


def _mix_proj_fwd(h, g, bgate, wg, tm=512, carries=()):
    t_tok = h.shape[0]
    nt = t_tok // tm
    names = ("w_in", "w_gate")
    groups = _groups_of(names)

    def body(h_ref, g_ref, bg_ref, wg_ref, q_ref, k_ref, v_ref, xr_ref, xg_ref, gs_ref, ub_ref, win_s, wgt_s, sems):
        i = pl.program_id(0)

        @pl.when(i == 0)
        def _():
            _load_weights({groups[0]: wg_ref}, [("w_in", win_s), ("w_gate", wgt_s)], sems)

        n, _, _ = _rms_fwd(h_ref[...], g_ref[...])
        nb = n.astype(BF16)
        ub_ref[...] = nb
        q_ref[...] = _nt(nb, win_s[pl.ds(Q0, D), :]).astype(BF16)
        k_ref[...] = _nt(nb, win_s[pl.ds(K0, N_KV * LANES), :]).astype(BF16)
        v_ref[...] = _nt(nb, win_s[pl.ds(V0, N_KV * LANES), :]).astype(BF16)
        xr_ref[...] = _nt(nb, win_s[pl.ds(XR0, D), :])
        xg_ref[...] = _nt(nb, win_s[pl.ds(XG0, D), :])
        gs_ref[...] = jax.nn.sigmoid(_nt(nb, wgt_s[...]) + bg_ref[...])

    kvw = N_KV * LANES
    return _call(
        body, name="mix_proj_fwd", grid=(nt,), ins=[h, g, bgate, wg[groups[0]]],
        in_specs=[_row_spec(tm, D), _const_spec((1, D)), _const_spec((1, 2 * D)), ANY],
        out_specs=[_row_spec(tm, D), _row_spec(tm, kvw), _row_spec(tm, kvw), _row_spec(tm, D), _row_spec(tm, D),
                   _row_spec(tm, 2 * D), _row_spec(tm, D)],
        out_shape=[jax.ShapeDtypeStruct((t_tok, D), BF16), jax.ShapeDtypeStruct((t_tok, kvw), BF16),
                   jax.ShapeDtypeStruct((t_tok, kvw), BF16), jax.ShapeDtypeStruct((t_tok, D), F32),
                   jax.ShapeDtypeStruct((t_tok, D), F32), jax.ShapeDtypeStruct((t_tok, 2 * D), F32),
                   jax.ShapeDtypeStruct((t_tok, D), BF16)],
        scratch=[pltpu.VMEM((INP_W, D), BF16), pltpu.VMEM((2 * D, D), BF16), DMA((_n_pieces(names),))],
        vmem=VMEM_BIG, carries=carries)


def _mix_proj_bwd(dqkv, dxr, dxg, dgpre, h, dh, g, wg, tm=512, carries=()):
    t_tok = h.shape[0]
    nt = t_tok // tm
    names = ("w_in", "w_gate")
    groups = _groups_of(names)

    def body(dqkv_ref, dxr_ref, dxg_ref, dgp_ref, h_ref, dh_ref, g_ref, wg_ref, dhin_ref, dg_ref, win_s, wgt_s, sems):
        i = pl.program_id(0)

        @pl.when(i == 0)
        def _():
            _load_weights({groups[0]: wg_ref}, [("w_in", win_s), ("w_gate", wgt_s)], sems)
            dg_ref[...] = jnp.zeros_like(dg_ref)

        gv = g_ref[...]
        _, xh, r = _rms_fwd(h_ref[...], gv)
        du = _nn(dgp_ref[...], wgt_s[...])
        du = du + _nn(dqkv_ref[...], win_s[pl.ds(Q0, XR0), :])
        du = du + _nn(dxr_ref[...], win_s[pl.ds(XR0, D), :])
        du = du + _nn(dxg_ref[...], win_s[pl.ds(XG0, D), :])
        dx, dg = _rms_bwd(du, xh, r, gv)
        dg_ref[...] += dg
        dhin_ref[...] = dh_ref[...] + dx

    return _call(
        body, name="mix_proj_bwd", grid=(nt,), ins=[dqkv, dxr, dxg, dgpre, h, dh, g, wg[groups[0]]],
        in_specs=[_row_spec(tm, XR0), _row_spec(tm, D), _row_spec(tm, D), _row_spec(tm, 2 * D), _row_spec(tm, D),
                  _row_spec(tm, D), _const_spec((1, D)), ANY],
        out_specs=[_row_spec(tm, D), _const_spec((1, D))],
        out_shape=[jax.ShapeDtypeStruct((t_tok, D), F32), jax.ShapeDtypeStruct((1, D), F32)],
        scratch=[pltpu.VMEM((INP_W, D), BF16), pltpu.VMEM((2 * D, D), BF16), DMA((_n_pieces(names),))],
        vmem=VMEM_BIG, carries=carries)


def _shift_down(x, d, fill):
    row = lax.broadcasted_iota(jnp.int32, x.shape, 0)
    return jnp.where(row >= d, pltpu.roll(x, d, 0), fill)


def _shift_up(x, d, fill):
    rows = x.shape[0]
    row = lax.broadcasted_iota(jnp.int32, x.shape, 0)
    return jnp.where(row < rows - d, pltpu.roll(x, rows - d, 0), fill)


def _scan_rows(a, b, reverse):
    rows = a.shape[0]
    d = 1
    while d < rows:
        if d < 8:
            shift = _shift_up if reverse else _shift_down
            b = a * shift(b, d, 0.0) + b
            a = a * shift(a, d, 1.0)
        elif reverse:
            b = jnp.concatenate([a[:rows - d] * b[d:] + b[:rows - d], b[rows - d:]], axis=0)
            a = jnp.concatenate([a[:rows - d] * a[d:], a[rows - d:]], axis=0)
        else:
            b = jnp.concatenate([b[:d], a[d:] * b[:rows - d] + b[d:]], axis=0)
            a = jnp.concatenate([a[:d], a[d:] * a[:rows - d]], axis=0)
        d *= 2
    return a, b


def _softplus(x):
    return jnp.maximum(x, 0.0) + jnp.log(1.0 + jnp.exp(-jnp.abs(x)))


_GELU_C = math.sqrt(2.0 / math.pi)


def _gelu_parts(x):
    th = jnp.tanh(_GELU_C * (x + 0.044715 * x * x * x))
    val = 0.5 * x * (1.0 + th)
    grad = 0.5 * (1.0 + th) + 0.5 * x * (1.0 - th * th) * _GELU_C * (1.0 + 3.0 * 0.044715 * x * x)
    return val, grad


def _lru_pre(x, halo, cw_ref, cb_ref, wa_ref, wx_ref, ba_ref, bx_ref, lam_ref):
    ext = jnp.concatenate([halo, x], axis=0)
    shifted = [x] + [pltpu.roll(ext, k, 0)[8:] for k in (1, 2, 3)]
    xc = cb_ref[...] + cw_ref[pl.ds(CONV_WIDTH - 1, 1), :] * x
    for k in (1, 2, 3):
        xc = xc + cw_ref[pl.ds(CONV_WIDTH - 1 - k, 1), :] * shifted[k]
    xcb = xc.astype(BF16)
    r = jax.nn.sigmoid(_nn(xcb, wa_ref[...]) + ba_ref[...])
    ig = jax.nn.sigmoid(_nn(xcb, wx_ref[...]) + bx_ref[...])
    sp = _softplus(-lam_ref[...])
    log_a = -LRU_C * r * sp
    a = jnp.exp(log_a)
    th = jnp.tanh(log_a)
    mult = jnp.sqrt(-2.0 * th / (1.0 - th))
    return shifted, xc, xcb, r, ig, sp, a, mult


def _lru_specs(nt, reverse):
    def tt(t):
        return nt - 1 - t if reverse else t
    tile = pl.BlockSpec((LRU_ROWS, LANES), lambda cb, t: (tt(t), cb))
    halo = pl.BlockSpec((8, LANES), lambda cb, t: (jnp.maximum(tt(t) * (LRU_ROWS // 8) - 1, 0), cb))
    vec = pl.BlockSpec((1, LANES), lambda cb, t: (0, cb))
    cw = pl.BlockSpec((CONV_WIDTH, LANES), lambda cb, t: (0, cb))
    mat = pl.BlockSpec((None, LANES, LANES), lambda cb, t: (cb, 0, 0))
    return tile, halo, vec, cw, mat


def _lru_fwd(xr, xg, cw, cb, wa, wx, ba, bx, lam, carries=()):
    t_tok = xr.shape[0]
    nt = t_tok // LRU_ROWS
    rows = LRU_ROWS

    def body(xr_ref, xg_ref, cw_ref, cb_ref, wa_ref, wx_ref, ba_ref, bx_ref, lam_ref, y_ref, h_ref, tail_s, hc_s):
        t = pl.program_id(1)

        @pl.when(t == 0)
        def _():
            tail_s[...] = jnp.zeros_like(tail_s)
            hc_s[...] = jnp.zeros_like(hc_s)

        x = xr_ref[...]
        _, xc, _, _, ig, _, a, mult = _lru_pre(x, tail_s[...], cw_ref, cb_ref, wa_ref, wx_ref, ba_ref, bx_ref, lam_ref)
        tail_s[...] = xr_ref[pl.ds(rows - 8, 8), :]
        acc_a, acc_b = _scan_rows(a, mult * (ig * xc), False)
        hv = acc_b + acc_a * hc_s[...]
        h_ref[...] = hv
        hc_s[...] = h_ref[pl.ds(rows - 1, 1), :]
        gl, _ = _gelu_parts(xg_ref[...])
        y_ref[...] = (hv * gl).astype(BF16)

    tile, _, vec, cws, mat = _lru_specs(nt, False)
    return _call(
        body, name="lru_fwd", grid=(D // LANES, nt), ins=[xr, xg, cw, cb, wa, wx, ba, bx, lam],
        in_specs=[tile, tile, cws, vec, mat, mat, vec, vec, vec],
        out_specs=[tile, tile],
        out_shape=[jax.ShapeDtypeStruct((t_tok, D), BF16), jax.ShapeDtypeStruct((t_tok, D), F32)],
        scratch=[pltpu.VMEM((8, LANES), F32), pltpu.VMEM((1, LANES), F32)], carries=carries)


def _lru_bwd(dy, xr, xg, hseq, cw, cb, wa, wx, ba, bx, lam, carries=()):
    t_tok = xr.shape[0]
    nt = t_tok // LRU_ROWS
    rows = LRU_ROWS

    def body(dy_ref, xr_ref, xrh_ref, xg_ref, h_ref, hh_ref, cw_ref, cb_ref, wa_ref, wx_ref, ba_ref, bx_ref, lam_ref,
             dxr_ref, dxg_ref, dvec_ref, dwa_ref, dwx_ref, gcar_s, acar_s, head_s, tmp_s):
        t = pl.program_id(1)
        first_tile = t == nt - 1

        @pl.when(t == 0)
        def _():
            gcar_s[...] = jnp.zeros_like(gcar_s)
            acar_s[...] = jnp.zeros_like(acar_s)
            head_s[...] = jnp.zeros_like(head_s)
            dvec_ref[...] = jnp.zeros_like(dvec_ref)
            dwa_ref[...] = jnp.zeros_like(dwa_ref)
            dwx_ref[...] = jnp.zeros_like(dwx_ref)

        x = xr_ref[...]
        halo = jnp.where(first_tile, 0.0, xrh_ref[...])
        shifted, xc, xcb, r, ig, sp, a, mult = _lru_pre(x, halo, cw_ref, cb_ref, wa_ref, wx_ref, ba_ref, bx_ref, lam_ref)
        hv = h_ref[...]
        dyv = dy_ref[...]
        gl, glg = _gelu_parts(xg_ref[...])
        dxg_ref[...] = (dyv * hv * glg).astype(BF16)
        acc_a, acc_b = _scan_rows(_shift_up(a, 1, acar_s[...]), dyv * gl, True)
        g = acc_b + acc_a * gcar_s[...]
        hhalo = jnp.where(first_tile, 0.0, hh_ref[...])
        hprev = pltpu.roll(jnp.concatenate([hhalo, hv], axis=0), 1, 0)[8:]
        dmult = g * ig * xc
        dlog_a = a * (g * hprev) - dmult * a * a / mult
        dig = g * mult * xc
        dxc = g * mult * ig
        dzr = (dlog_a * (-LRU_C * sp)) * r * (1.0 - r)
        dzx = dig * ig * (1.0 - ig)
        dzrb, dzxb = dzr.astype(BF16), dzx.astype(BF16)
        dxc = dxc + _nt(dzrb, wa_ref[...]) + _nt(dzxb, wx_ref[...])
        dwa_ref[...] += _tn(xcb, dzrb)
        dwx_ref[...] += _tn(xcb, dzxb)
        dsp = jnp.sum(dlog_a * (-LRU_C * r), axis=0, keepdims=True)
        dlam = dsp * (-jax.nn.sigmoid(-lam_ref[...]))
        vrow = lax.broadcasted_iota(jnp.int32, (8, LANES), 0)
        upd = jnp.where(vrow == 4, jnp.sum(dxc, axis=0, keepdims=True), 0.0)
        upd = jnp.where(vrow == 5, jnp.sum(dzr, axis=0, keepdims=True), upd)
        upd = jnp.where(vrow == 6, jnp.sum(dzx, axis=0, keepdims=True), upd)
        upd = jnp.where(vrow == 7, dlam, upd)
        for k in range(CONV_WIDTH):
            upd = jnp.where(vrow == CONV_WIDTH - 1 - k, jnp.sum(dxc * shifted[k], axis=0, keepdims=True), upd)
        dvec_ref[...] += upd
        ext = jnp.concatenate([dxc, head_s[...]], axis=0)
        dxr = cw_ref[pl.ds(CONV_WIDTH - 1, 1), :] * dxc
        for k in (1, 2, 3):
            dxr = dxr + cw_ref[pl.ds(CONV_WIDTH - 1 - k, 1), :] * pltpu.roll(ext, rows + 8 - k, 0)[:rows]
        dxr_ref[...] = dxr.astype(BF16)
        tmp_s[...] = g
        gcar_s[...] = tmp_s[pl.ds(0, 1), :]
        tmp_s[...] = a
        acar_s[...] = tmp_s[pl.ds(0, 1), :]
        tmp_s[...] = dxc
        head_s[...] = tmp_s[pl.ds(0, 8), :]

    tile, halo, vec, cws, mat = _lru_specs(nt, True)
    return _call(
        body, name="lru_bwd", grid=(D // LANES, nt), ins=[dy, xr, xr, xg, hseq, hseq, cw, cb, wa, wx, ba, bx, lam],
        in_specs=[tile, tile, halo, tile, tile, halo, cws, vec, mat, mat, vec, vec, vec],
        out_specs=[tile, tile, pl.BlockSpec((8, LANES), lambda cb, t: (0, cb)), mat, mat],
        out_shape=[jax.ShapeDtypeStruct((t_tok, D), BF16), jax.ShapeDtypeStruct((t_tok, D), BF16),
                   jax.ShapeDtypeStruct((8, D), F32), jax.ShapeDtypeStruct((D // LANES, LANES, LANES), F32),
                   jax.ShapeDtypeStruct((D // LANES, LANES, LANES), F32)],
        scratch=[pltpu.VMEM((1, LANES), F32), pltpu.VMEM((1, LANES), F32), pltpu.VMEM((8, LANES), F32),
                 pltpu.VMEM((rows, LANES), F32)], carries=carries)


def _t5_bucket_np(rel):
    nb = N_BUCKETS // 2
    max_exact = nb // 2
    ret = np.where(rel > 0, nb, 0)
    n = np.abs(rel)
    nf = np.maximum(n, 1).astype(np.float32)
    large = max_exact + (np.log(nf / np.float32(max_exact)) / np.float32(math.log(MAX_DISTANCE / max_exact))
                         * np.float32(nb - max_exact)).astype(np.int32)
    large = np.minimum(large, nb - 1)
    return ret + np.where(n < max_exact, n, large)


def _bucket_map():
    r = np.arange(QT)[:, None]
    c = np.arange(KW)[None, :]
    j = c - (r // CHUNK) * CHUNK
    band = (j >= 0) & (j < WINDOW + CHUNK)
    return np.where(band, _t5_bucket_np(c - r - WINDOW), -1).astype(np.int32)


def _attn_specs(nt, reverse):
    def tt(i):
        return nt - 1 - i if reverse else i
    kvw = N_KV * LANES
    qs = pl.BlockSpec((QT, D), lambda i: (tt(i), 0))
    cur = pl.BlockSpec((QT, kvw), lambda i: (tt(i), 0))
    prev = pl.BlockSpec((WINDOW, kvw), lambda i: (jnp.maximum(tt(i) * (QT // WINDOW) - 1, 0), 0))
    lse = pl.BlockSpec((QT, LANES), lambda i: (tt(i), 0))
    return qs, cur, prev, lse


REP = N_HEADS // N_KV
SCALE = HEAD_DIM ** -0.5


def _stack_heads(x_ref, g, lo, scale=None):
    parts = []
    for hl in range(REP):
        xs = x_ref[:, pl.ds((2 * g + hl // 2) * LANES, LANES)]
        xs = jnp.where(lo if hl % 2 == 0 else jnp.logical_not(lo), xs, jnp.zeros_like(xs))
        parts.append(xs if scale is None else xs * jnp.asarray(scale, xs.dtype))
    return jnp.concatenate(parts, axis=0)


def _stack_sinks(sink_ref, g, srow):
    sk = jnp.full(srow.shape, sink_ref[REP * g + REP - 1], F32)
    for hl in range(REP - 2, -1, -1):
        sk = jnp.where(srow < (hl + 1) * QT, sink_ref[REP * g + hl], sk)
    return sk


def _attn_fwd(q, kd, vd, bias, sinks, carries=()):
    t_tok = q.shape[0]
    nt = t_tok // QT

    def body(q_ref, kp_ref, kc_ref, vp_ref, vc_ref, bias_ref, sink_ref, o_ref, lse_ref):
        i = pl.program_id(0)
        col = lax.broadcasted_iota(jnp.int32, (1, KW), 1)
        first = jnp.where((i == 0) & (col < WINDOW), NEG_INF, 0.0)
        lane = lax.broadcasted_iota(jnp.int32, (QT, LANES), 1)
        lo = lane < HEAD_DIM
        srow = lax.broadcasted_iota(jnp.int32, (REP * QT, 1), 0)
        lse_t = jnp.zeros((QT, LANES), F32)
        for g in range(N_KV):
            kwin = jnp.concatenate([kp_ref[:, pl.ds(g * LANES, LANES)], kc_ref[:, pl.ds(g * LANES, LANES)]], axis=0)
            vwin = jnp.concatenate([vp_ref[:, pl.ds(g * LANES, LANES)], vc_ref[:, pl.ds(g * LANES, LANES)]], axis=0)
            qst = _stack_heads(q_ref, g, lo, SCALE)
            s = _nt(qst, kwin) + (bias_ref[g] + first)
            sk = _stack_sinks(sink_ref, g, srow)
            m = jnp.maximum(jnp.max(s, axis=-1, keepdims=True), sk)
            e = jnp.exp(s - m)
            l = jnp.sum(e, axis=-1, keepdims=True) + jnp.exp(sk - m)
            p = e / l
            ost = _nn(p.astype(BF16), vwin)
            lse_s = m + jnp.log(l)
            for sl in range(2):
                o_ref[:, pl.ds((2 * g + sl) * LANES, LANES)] = jnp.where(
                    lo, ost[2 * sl * QT:(2 * sl + 1) * QT], ost[(2 * sl + 1) * QT:(2 * sl + 2) * QT]).astype(BF16)
            for hl in range(REP):
                lse_t = jnp.where(lane == REP * g + hl, lse_s[hl * QT:(hl + 1) * QT], lse_t)
        lse_ref[...] = lse_t

    qs, cur, prev, lse = _attn_specs(nt, False)
    return _call(
        body, name="attn_fwd", grid=(nt,), ins=[q, kd, kd, vd, vd, bias, sinks],
        in_specs=[qs, prev, cur, prev, cur, _const_spec((N_KV, REP * QT, KW)), pl.BlockSpec(memory_space=pltpu.SMEM)],
        out_specs=[qs, lse],
        out_shape=[jax.ShapeDtypeStruct((t_tok, D), BF16), jax.ShapeDtypeStruct((t_tok, LANES), F32)],
        vmem=48 * 2 ** 20, carries=carries)


def _attn_bwd(q, kd, vd, o, do, lse, bias, sinks, carries=()):
    t_tok = q.shape[0]
    nt = t_tok // QT
    kvw = N_KV * LANES

    def body(q_ref, kp_ref, kc_ref, vp_ref, vc_ref, o_ref, do_ref, lse_ref, bias_ref, sink_ref,
             dqkv_ref, ds_ref, dsink_ref, kcar_s, vcar_s):
        i = pl.program_id(0)
        tile = nt - 1 - i

        @pl.when(i == 0)
        def _():
            kcar_s[...] = jnp.zeros_like(kcar_s)
            vcar_s[...] = jnp.zeros_like(vcar_s)
            ds_ref[...] = jnp.zeros_like(ds_ref)
            dsink_ref[...] = jnp.zeros_like(dsink_ref)

        col = lax.broadcasted_iota(jnp.int32, (1, KW), 1)
        first = jnp.where((tile == 0) & (col < WINDOW), NEG_INF, 0.0)
        lane = lax.broadcasted_iota(jnp.int32, (QT, LANES), 1)
        lo = lane < HEAD_DIM
        lane_k = lax.broadcasted_iota(jnp.int32, (KW, LANES), 1)
        lane_1 = lax.broadcasted_iota(jnp.int32, (1, LANES), 1)
        srow = lax.broadcasted_iota(jnp.int32, (REP * QT, 1), 0)
        lse_t = lse_ref[...]
        dsink = jnp.zeros((1, LANES), F32)
        for g in range(N_KV):
            kwin = jnp.concatenate([kp_ref[:, pl.ds(g * LANES, LANES)], kc_ref[:, pl.ds(g * LANES, LANES)]], axis=0)
            vwin = jnp.concatenate([vp_ref[:, pl.ds(g * LANES, LANES)], vc_ref[:, pl.ds(g * LANES, LANES)]], axis=0)
            qst = _stack_heads(q_ref, g, lo, SCALE)
            dost = _stack_heads(do_ref, g, lo)
            od = [do_ref[:, pl.ds((2 * g + sl) * LANES, LANES)].astype(F32)
                  * o_ref[:, pl.ds((2 * g + sl) * LANES, LANES)].astype(F32) for sl in range(2)]
            drow = jnp.concatenate([jnp.sum(jnp.where(lo if hl % 2 == 0 else jnp.logical_not(lo), od[hl // 2], 0.0),
                                            axis=-1, keepdims=True) for hl in range(REP)], axis=0)
            lse_s = jnp.concatenate([jnp.sum(jnp.where(lane == REP * g + hl, lse_t, 0.0), axis=-1, keepdims=True)
                                     for hl in range(REP)], axis=0)
            s = _nt(qst, kwin) + (bias_ref[g] + first)
            p = jnp.exp(s - lse_s)
            ds = p * (_nt(dost, vwin) - drow)
            ds_ref[g] += ds
            tsink = -(jnp.exp(_stack_sinks(sink_ref, g, srow) - lse_s) * drow)
            for hl in range(REP):
                dsink = dsink + jnp.where(lane_1 == REP * g + hl,
                                          jnp.sum(tsink[hl * QT:(hl + 1) * QT], axis=0, keepdims=True), 0.0)
            dsb = ds.astype(BF16)
            dqst = _nn(dsb, kwin) * SCALE
            for sl in range(2):
                dqkv_ref[:, pl.ds((2 * g + sl) * LANES, LANES)] = jnp.where(
                    lo, dqst[2 * sl * QT:(2 * sl + 1) * QT], dqst[(2 * sl + 1) * QT:(2 * sl + 2) * QT]).astype(BF16)
            dk_acc = _tn(dsb, qst)
            dv_acc = _tn(p.astype(BF16), dost)
            dk_f = jnp.where(lane_k < HEAD_DIM, dk_acc + pltpu.roll(dk_acc, HEAD_DIM, 1), 0.0)
            dv_f = jnp.where(lane_k < HEAD_DIM, dv_acc + pltpu.roll(dv_acc, HEAD_DIM, 1), 0.0)
            for acc, col0, car in ((dk_f, K0, kcar_s), (dv_f, V0, vcar_s)):
                cs = pl.ds(g * LANES, LANES)
                co = pl.ds(col0 + g * LANES, LANES)
                if QT > WINDOW:
                    dqkv_ref[pl.ds(0, QT - WINDOW), co] = acc[WINDOW:QT].astype(BF16)
                dqkv_ref[pl.ds(QT - WINDOW, WINDOW), co] = (acc[QT:KW] + car[:, cs]).astype(BF16)
                car[:, cs] = acc[0:WINDOW]
        dsink_ref[...] += dsink

    qs, cur, prev, lse_s = _attn_specs(nt, True)
    return _call(
        body, name="attn_bwd", grid=(nt,), ins=[q, kd, kd, vd, vd, o, do, lse, bias, sinks],
        in_specs=[qs, prev, cur, prev, cur, qs, qs, lse_s, _const_spec((N_KV, REP * QT, KW)),
                  pl.BlockSpec(memory_space=pltpu.SMEM)],
        out_specs=[pl.BlockSpec((QT, XR0), lambda i: (nt - 1 - i, 0)), _const_spec((N_KV, REP * QT, KW)),
                   _const_spec((1, LANES))],
        out_shape=[jax.ShapeDtypeStruct((t_tok, XR0), BF16), jax.ShapeDtypeStruct((N_KV, REP * QT, KW), F32),
                   jax.ShapeDtypeStruct((1, LANES), F32)],
        scratch=[pltpu.VMEM((WINDOW, kvw), F32), pltpu.VMEM((WINDOW, kvw), F32)],
        vmem=VMEM_BIG, carries=carries)


def _bias_tile(table, bmap):
    def body(tab_ref, bm_ref, out_ref):
        bm = bm_ref[...]

        def per_head(hd, carry):
            acc = jnp.full((QT, KW), NEG_INF, F32)
            for b in range(N_BUCKETS):
                acc = jnp.where(bm == b, tab_ref[b, hd], acc)
            out_ref[hd] = acc
            return carry

        lax.fori_loop(0, N_HEADS, per_head, 0)

    return pl.pallas_call(
        body, name="bias_tile", out_shape=jax.ShapeDtypeStruct((N_HEADS, QT, KW), F32),
        in_specs=[pl.BlockSpec(memory_space=pltpu.SMEM), pl.BlockSpec(memory_space=pltpu.VMEM)],
        out_specs=pl.BlockSpec(memory_space=pltpu.VMEM))(table, bmap)


def _bias_grad(ds_acc, bmap):
    def body(ds_ref, bm_ref, out_ref):
        row = lax.broadcasted_iota(jnp.int32, (N_BUCKETS, LANES), 0)
        lane = lax.broadcasted_iota(jnp.int32, (N_BUCKETS, LANES), 1)
        bm = bm_ref[...]

        def per_head(hd, res):
            dsv = ds_ref[hd]
            for b in range(N_BUCKETS):
                val = jnp.sum(jnp.sum(jnp.where(bm == b, dsv, 0.0), axis=0, keepdims=True), axis=1, keepdims=True)
                res = jnp.where((row == b) & (lane == hd), val, res)
            return res

        out_ref[...] = lax.fori_loop(0, N_HEADS, per_head, jnp.zeros((N_BUCKETS, LANES), F32))

    return pl.pallas_call(
        body, name="bias_grad", out_shape=jax.ShapeDtypeStruct((N_BUCKETS, LANES), F32),
        in_specs=[pl.BlockSpec(memory_space=pltpu.VMEM), pl.BlockSpec(memory_space=pltpu.VMEM)],
        out_specs=pl.BlockSpec(memory_space=pltpu.VMEM))(ds_acc, bmap)


MIXOUT = ("w_lru_out", "w_attn_out", "w_o")


def _mix_out_fwd(ya_in, o, gs, h, g, wg, tm=256, carries=()):
    t_tok = h.shape[0]
    nt = t_tok // tm
    groups = _groups_of(MIXOUT)

    def body(ya_ref, o_ref, gs_ref, h_ref, g_ref, wg_ref, hout_ref, yao_ref, ybo_ref, z_ref, wa_s, wb_s, wo_s, sems):
        i = pl.program_id(0)

        @pl.when(i == 0)
        def _():
            _load_weights({groups[0]: wg_ref}, list(zip(MIXOUT, (wa_s, wb_s, wo_s))), sems)

        ya = _nn(ya_ref[...], wa_s[...])
        yb = _nn(o_ref[...], wb_s[...])
        yao_ref[...] = ya
        ybo_ref[...] = yb
        merged = gs_ref[:, pl.ds(0, D)] * ya + gs_ref[:, pl.ds(D, D)] * yb
        z = _nn(merged.astype(BF16), wo_s[...])
        z_ref[...] = z
        zn, _, _ = _rms_fwd(z, g_ref[...])
        hout_ref[...] = h_ref[...] + zn

    return _call(
        body, name="mix_out_fwd", grid=(nt,), ins=[ya_in, o, gs, h, g, wg[groups[0]]],
        in_specs=[_row_spec(tm, D), _row_spec(tm, D), _row_spec(tm, 2 * D), _row_spec(tm, D), _const_spec((1, D)), ANY],
        out_specs=[_row_spec(tm, D)] * 4,
        out_shape=[jax.ShapeDtypeStruct((t_tok, D), F32)] * 4,
        scratch=[pltpu.VMEM((D, D), BF16)] * 3 + [DMA((3 * NDEV,))],
        vmem=48 * 2 ** 20, carries=carries)


def _mix_out_bwd(dh, z, ya, yb, gs, g, wg, tm=512, carries=()):
    t_tok = dh.shape[0]
    nt = t_tok // tm
    groups = _groups_of(MIXOUT)

    def body(dh_ref, z_ref, ya_ref, yb_ref, gs_ref, g_ref, wg_ref,
             dyain_ref, do_ref, dgpre_ref, dya_ref, dyb_ref, mg_ref, dz_ref, dg_ref, dbg_ref,
             wa_s, wb_s, wo_s, sems):
        i = pl.program_id(0)

        @pl.when(i == 0)
        def _():
            _load_weights({groups[0]: wg_ref}, list(zip(MIXOUT, (wa_s, wb_s, wo_s))), sems)
            dg_ref[...] = jnp.zeros_like(dg_ref)
            dbg_ref[...] = jnp.zeros_like(dbg_ref)

        gv = g_ref[...]
        _, zh, r = _rms_fwd(z_ref[...], gv)
        dz, dg = _rms_bwd(dh_ref[...], zh, r, gv)
        dg_ref[...] += dg
        dzb = dz.astype(BF16)
        dz_ref[...] = dzb
        ga, gb = gs_ref[:, pl.ds(0, D)], gs_ref[:, pl.ds(D, D)]
        ya_v, yb_v = ya_ref[...], yb_ref[...]
        mg_ref[...] = (ga * ya_v + gb * yb_v).astype(BF16)
        dm = _nt(dzb, wo_s[...])
        dga = dm * ya_v * ga * (1.0 - ga)
        dgb = dm * yb_v * gb * (1.0 - gb)
        dgpre_ref[:, pl.ds(0, D)] = dga.astype(BF16)
        dgpre_ref[:, pl.ds(D, D)] = dgb.astype(BF16)
        dbg_ref[:, pl.ds(0, D)] += jnp.sum(dga, axis=0, keepdims=True)
        dbg_ref[:, pl.ds(D, D)] += jnp.sum(dgb, axis=0, keepdims=True)
        dya = (dm * ga).astype(BF16)
        dyb = (dm * gb).astype(BF16)
        dya_ref[...] = dya
        dyb_ref[...] = dyb
        dyain_ref[...] = _nt(dya, wa_s[...])
        do_ref[...] = _nt(dyb, wb_s[...]).astype(BF16)

    bf = jax.ShapeDtypeStruct((t_tok, D), BF16)
    return _call(
        body, name="mix_out_bwd", grid=(nt,), ins=[dh, z, ya, yb, gs, g, wg[groups[0]]],
        in_specs=[_row_spec(tm, D)] * 4 + [_row_spec(tm, 2 * D), _const_spec((1, D)), ANY],
        out_specs=[_row_spec(tm, D), _row_spec(tm, D), _row_spec(tm, 2 * D)] + [_row_spec(tm, D)] * 4
        + [_const_spec((1, D)), _const_spec((1, 2 * D))],
        out_shape=[jax.ShapeDtypeStruct((t_tok, D), F32), bf, jax.ShapeDtypeStruct((t_tok, 2 * D), BF16), bf, bf, bf, bf,
                   jax.ShapeDtypeStruct((1, D), F32), jax.ShapeDtypeStruct((1, 2 * D), F32)],
        scratch=[pltpu.VMEM((D, D), BF16)] * 3 + [DMA((3 * NDEV,))],
        vmem=VMEM_BIG, carries=carries)


def _sum_parts(parts_list):
    n = len(parts_list)
    _, r, c = parts_list[0].shape
    tc = 256

    def body(*refs):
        for p_ref, o_ref in zip(refs[:n], refs[n:]):
            acc = p_ref[0].astype(F32)
            for s in range(1, NDEV):
                acc = acc + p_ref[s].astype(F32)
            o_ref[...] = acc

    return pl.pallas_call(
        body, name=f"sum_parts_{r}", grid=(c // tc,),
        in_specs=[pl.BlockSpec((NDEV, r, tc), lambda i: (0, 0, i))] * n,
        out_specs=[pl.BlockSpec((r, tc), lambda i: (0, i))] * n,
        out_shape=[jax.ShapeDtypeStruct((r, c), F32)] * n,
        compiler_params=pltpu.CompilerParams(dimension_semantics=("arbitrary",), vmem_limit_bytes=48 * 2 ** 20),
    )(*parts_list)


def _adamw_math(w, g, m, v):
    m = ADAM_B1 * m + (1.0 - ADAM_B1) * g
    v = ADAM_B2 * v + (1.0 - ADAM_B2) * (g * g)
    m_hat = m / (1.0 - ADAM_B1 ** ADAM_STEP)
    v_hat = v / (1.0 - ADAM_B2 ** ADAM_STEP)
    delta = -ADAM_LR * (m_hat / (jnp.sqrt(v_hat) + ADAM_EPS) + ADAM_WD * w)
    return delta, m, v


def _sum_adamw(items, name, carries=()):
    n = len(items)
    c = items[0][1].shape[1]
    tc = LANES

    def body(*refs):
        for k in range(n):
            p_ref, w_ref, m_ref, v_ref = refs[4 * k:4 * k + 4]
            g_ref, d_ref, nm_ref, nv_ref = refs[4 * n + 4 * k:4 * n + 4 * k + 4]
            g = p_ref[0].astype(F32)
            for s in range(1, NDEV):
                g = g + p_ref[s].astype(F32)
            d, m, v = _adamw_math(w_ref[...], g, m_ref[...], v_ref[...])
            g_ref[...] = g
            d_ref[...] = d
            nm_ref[...] = m
            nv_ref[...] = v

    ins, in_specs, out_shape, out_specs = [], [], [], []
    for parts, w, m, v in items:
        r = w.shape[0]
        col = pl.BlockSpec((r, tc), lambda i: (0, i))
        ins += [parts, w, m, v]
        in_specs += [pl.BlockSpec((NDEV, r, tc), lambda i: (0, 0, i)), col, col, col]
        out_shape += [jax.ShapeDtypeStruct((r, c), F32)] * 4
        out_specs += [col] * 4
    outs, carried = _call(body, name=name, grid=(c // tc,), ins=ins, in_specs=in_specs, out_shape=out_shape,
                          out_specs=out_specs, vmem=48 * 2 ** 20, carries=carries)
    return [tuple(outs[4 * k:4 * k + 4]) for k in range(n)], carried


def _adamw_body(n):
    def body(*refs):
        for k in range(n):
            g_ref, w_ref, m_ref, v_ref = refs[4 * k:4 * k + 4]
            d_ref, nm_ref, nv_ref = refs[4 * n + 3 * k:4 * n + 3 * k + 3]
            d, m, v = _adamw_math(w_ref[...], g_ref[...], m_ref[...], v_ref[...])
            d_ref[...] = d
            nm_ref[...] = m
            nv_ref[...] = v
    return body


def _adamw(items):
    n = len(items)
    r, c = items[0][0].shape
    tr = r if r * c <= 2 ** 18 else max(t for t in range(8, 65, 8) if r % t == 0)
    spec = pl.BlockSpec((tr, c), lambda i: (i, 0))
    outs = pl.pallas_call(
        _adamw_body(n), name=f"adamw_{r}x{c}", grid=(r // tr,),
        in_specs=[spec] * (4 * n), out_specs=[spec] * (3 * n),
        out_shape=[jax.ShapeDtypeStruct((r, c), F32)] * (3 * n),
        compiler_params=pltpu.CompilerParams(dimension_semantics=("arbitrary",), vmem_limit_bytes=40 * 2 ** 20),
    )(*[a for it in items for a in it])
    return [tuple(outs[3 * k:3 * k + 3]) for k in range(n)]


def _adamw_small(items):
    n = len(items)
    vm = pl.BlockSpec(memory_space=pltpu.VMEM)
    outs = pl.pallas_call(
        _adamw_body(n), name="adamw_small", in_specs=[vm] * (4 * n), out_specs=[vm] * (3 * n),
        out_shape=[jax.ShapeDtypeStruct(it[1].shape, F32) for it in items for _ in range(3)],
    )(*[a for it in items for a in it])
    return [tuple(outs[3 * k:3 * k + 3]) for k in range(n)]


def _pack_small(arrs):
    rows, offs = [], []
    total = 0
    for a in arrs:
        flat = a.reshape(-1).astype(F32)
        nr = -(-flat.shape[0] // LANES)
        flat = jnp.pad(flat, (0, nr * LANES - flat.shape[0]))
        rows.append(flat.reshape(nr, LANES))
        offs.append((total, nr))
        total += nr
    pad = -total % 8
    if pad:
        rows.append(jnp.zeros((pad, LANES), F32))
    return jnp.concatenate(rows, axis=0), offs


def _unpack_small(pack, offs, shapes):
    out = []
    for (o, nr), shp in zip(offs, shapes):
        size = int(np.prod(shp))
        out.append(pack[o:o + nr].reshape(-1)[:size].reshape(shp))
    return out


def _sum_small(gathered, rows):
    def body(p_ref, o_ref):
        acc = p_ref[pl.ds(0, rows), :]
        for s in range(1, NDEV):
            acc = acc + p_ref[pl.ds(s * rows, rows), :]
        o_ref[...] = acc

    return pl.pallas_call(
        body, name="sum_small", out_shape=jax.ShapeDtypeStruct((rows, LANES), F32),
        in_specs=[pl.BlockSpec(memory_space=pltpu.VMEM)], out_specs=pl.BlockSpec(memory_space=pltpu.VMEM))(gathered)


def _block_diag(w):
    w = w.reshape(D // LANES, 2, LRU_BLOCK, LRU_BLOCK)
    z = jnp.zeros((D // LANES, LRU_BLOCK, LRU_BLOCK), w.dtype)
    top = jnp.concatenate([w[:, 0], z], axis=2)
    bot = jnp.concatenate([z, w[:, 1]], axis=2)
    return jnp.concatenate([top, bot], axis=1)


def _block_diag_grad(dw):
    a = dw[:, :LRU_BLOCK, :LRU_BLOCK]
    b = dw[:, LRU_BLOCK:, LRU_BLOCK:]
    return jnp.stack([a, b], axis=1).reshape(D // LRU_BLOCK, LRU_BLOCK, LRU_BLOCK)


def kernel(x, ffn1_pre_g, ffn1_w1, ffn1_w3, ffn1_w2, ffn1_post_g, mix_pre_g, w_in, conv_w, conv_b, rg_a_w, rg_a_b, rg_x_w, rg_x_b, lru_lambda, w_lru_out, attn_sinks, rel_bias, w_attn_out, w_gate, b_gate, w_o, mix_post_g, ffn2_pre_g, ffn2_w1, ffn2_w3, ffn2_w2, ffn2_post_g, loss_target, m_ffn1_pre_g, m_ffn1_w1, m_ffn1_w3, m_ffn1_w2, m_ffn1_post_g, m_mix_pre_g, m_w_in, m_conv_w, m_conv_b, m_rg_a_w, m_rg_a_b, m_rg_x_w, m_rg_x_b, m_lru_lambda, m_w_lru_out, m_attn_sinks, m_rel_bias, m_w_attn_out, m_w_gate, m_b_gate, m_w_o, m_mix_post_g, m_ffn2_pre_g, m_ffn2_w1, m_ffn2_w3, m_ffn2_w2, m_ffn2_post_g, v_ffn1_pre_g, v_ffn1_w1, v_ffn1_w3, v_ffn1_w2, v_ffn1_post_g, v_mix_pre_g, v_w_in, v_conv_w, v_conv_b, v_rg_a_w, v_rg_a_b, v_rg_x_w, v_rg_x_b, v_lru_lambda, v_w_lru_out, v_attn_sinks, v_rel_bias, v_w_attn_out, v_w_gate, v_b_gate, v_w_o, v_mix_post_g, v_ffn2_pre_g, v_ffn2_w1, v_ffn2_w3, v_ffn2_w2, v_ffn2_post_g):
    names = ["ffn1_pre_g", "ffn1_w1", "ffn1_w3", "ffn1_w2", "ffn1_post_g", "mix_pre_g", "w_in", "conv_w", "conv_b",
             "rg_a_w", "rg_a_b", "rg_x_w", "rg_x_b", "lru_lambda", "w_lru_out", "attn_sinks", "rel_bias", "w_attn_out",
             "w_gate", "b_gate", "w_o", "mix_post_g", "ffn2_pre_g", "ffn2_w1", "ffn2_w3", "ffn2_w2", "ffn2_post_g"]
    ws = dict(zip(names, (ffn1_pre_g, ffn1_w1, ffn1_w3, ffn1_w2, ffn1_post_g, mix_pre_g, w_in, conv_w, conv_b, rg_a_w,
                          rg_a_b, rg_x_w, rg_x_b, lru_lambda, w_lru_out, attn_sinks, rel_bias, w_attn_out, w_gate,
                          b_gate, w_o, mix_post_g, ffn2_pre_g, ffn2_w1, ffn2_w3, ffn2_w2, ffn2_post_g)))
    ms = dict(zip(names, (m_ffn1_pre_g, m_ffn1_w1, m_ffn1_w3, m_ffn1_w2, m_ffn1_post_g, m_mix_pre_g, m_w_in, m_conv_w,
                          m_conv_b, m_rg_a_w, m_rg_a_b, m_rg_x_w, m_rg_x_b, m_lru_lambda, m_w_lru_out, m_attn_sinks,
                          m_rel_bias, m_w_attn_out, m_w_gate, m_b_gate, m_w_o, m_mix_post_g, m_ffn2_pre_g, m_ffn2_w1,
                          m_ffn2_w3, m_ffn2_w2, m_ffn2_post_g)))
    vs = dict(zip(names, (v_ffn1_pre_g, v_ffn1_w1, v_ffn1_w3, v_ffn1_w2, v_ffn1_post_g, v_mix_pre_g, v_w_in, v_conv_w,
                          v_conv_b, v_rg_a_w, v_rg_a_b, v_rg_x_w, v_rg_x_b, v_lru_lambda, v_w_lru_out, v_attn_sinks,
                          v_rel_bias, v_w_attn_out, v_w_gate, v_b_gate, v_w_o, v_mix_post_g, v_ffn2_pre_g, v_ffn2_w1,
                          v_ffn2_w3, v_ffn2_w2, v_ffn2_post_g)))
    me = 4 * lax.axis_index("x") + 2 * lax.axis_index("y") + lax.axis_index("c")
    vec = lambda n: ws[n].reshape(1, -1)

    def shard2d(name):
        if name == "conv":
            row = lax.bitcast_convert_type(conv_w.reshape(CONV_WIDTH, LANES), BF16).reshape(1, D)
            return jnp.concatenate([row, jnp.zeros((LOC["conv"][2] - 1, D), BF16)], axis=0)
        a = ws[name].reshape(ws[name].shape[-2], ws[name].shape[-1])
        return (a.T if name in COL_SHARDED else a).astype(BF16)

    packs = {g: jnp.concatenate([shard2d(n) for n, _ in members], axis=0) for g, members in GROUPS}

    wg = {}
    _, ((wg["ffn1"],),) = _call(None, name="allgather_ffn1", grid=(), ins=[], in_specs=[], out_shape=[], out_specs=[],
                                carries=[_AllGather(packs["ffn1"])])
    conv_rows = wg["ffn1"].reshape(NDEV, GROUP_ROWS["ffn1"], D)[:, LOC["conv"][1]]
    cw = jnp.transpose(lax.bitcast_convert_type(conv_rows.reshape(NDEV, CONV_WIDTH, LANES, 2), F32),
                       (1, 0, 2)).reshape(CONV_WIDTH, D)
    bmap = jnp.asarray(_bucket_map())
    bias = _bias_tile(rel_bias, bmap).reshape(N_KV, REP * QT, KW)
    sinks = attn_sinks.reshape(N_HEADS)
    wa_bd = _block_diag(rg_a_w.reshape(D // LRU_BLOCK, LRU_BLOCK, LRU_BLOCK)).astype(BF16)
    wx_bd = _block_diag(rg_x_w.reshape(D // LRU_BLOCK, LRU_BLOCK, LRU_BLOCK)).astype(BF16)
    lru_args = (cw, vec("conv_b"), wa_bd, wx_bd, vec("rg_a_b"), vec("rg_x_b"), vec("lru_lambda"))
    x2, tgt = x[0], loss_target[0]

    (h1, a1, b1, f1, nb1), ((wg["mixin"],),) = _ffn_fwd(
        x2, vec("ffn1_pre_g"), vec("ffn1_post_g"), wg, ("ffn1_w1", "ffn1_w3", "ffn1_w2"),
        carries=[_AllGather(packs["mixin"])])
    (q, kd, vd, xr, xg, gs, ub), ((wg["mixout"],),) = _mix_proj_fwd(
        h1, vec("mix_pre_g"), vec("b_gate"), wg, carries=[_AllGather(packs["mixout"])])
    (ya_in, hseq), ((wg["ffn2a"],),) = _lru_fwd(xr, xg, *lru_args, carries=[_AllGather(packs["ffn2a"])])
    (o, lse), ((wg["ffn2b"],),) = _attn_fwd(q, kd, vd, bias, sinks, carries=[_AllGather(packs["ffn2b"])])
    (h2, ya, yb, z), _ = _mix_out_fwd(ya_in, o, gs, h1, vec("mix_post_g"), wg)
    (_, a2, b2, f2, nb2, dy, loss_part), _ = _ffn_fwd(h2, vec("ffn2_pre_g"), vec("ffn2_post_g"), wg,
                                                     ("ffn2_w1", "ffn2_w3", "ffn2_w2"), target=tgt)
    loss = lax.psum(loss_part[0, 0], ("x", "y", "c"))

    gsm, parts = {}, {}
    rs = lambda *grads_: [_ReduceScatterSend(list(grads_))]
    ffr = FF // NDEV
    (dab, s_act, dfb, gsm["ffn2_post_g"]), _ = _ffn_bwd_a(dy, f2, a2, b2, vec("ffn2_post_g"), wg, "ffn2_w2")
    g_w2, _ = _dw(s_act, dfb, FF // 2, "dw_ffn2_w2")
    g_w13, ((parts["ffn2_w2"],),) = _dw(dab, nb2, D // 2, "dw_ffn2_w13", carries=rs(g_w2))
    (dh2, gsm["ffn2_pre_g"]), ((parts["ffn2_w1"],),) = _ffn_bwd_dx(
        dab, h2, dy, vec("ffn2_pre_g"), wg, "ffn2_w1", "ffn2_w3", carries=rs((g_w13, 0, ffr, ffr)))
    (dya_in, do, dgpre, dya, dyb, mg, dzb, gsm["mix_post_g"], gsm["b_gate"]), ((parts["ffn2_w3"],),) = _mix_out_bwd(
        dh2, z, ya, yb, gs, vec("mix_post_g"), wg, carries=rs((g_w13, FF, ffr, ffr)))
    g_wa, _ = _dw(ya_in, dya, D // 2, "dw_w_lru_out")
    g_wb, _ = _dw(o, dyb, D // 2, "dw_w_attn_out")
    g_wo, _ = _dw(mg, dzb, D // 2, "dw_w_o")
    (dqkv, ds_acc, dsink), (mixout_parts,) = _attn_bwd(
        q, kd, vd, o, do, lse, bias, sinks, carries=rs(g_wa, g_wb, g_wo))
    parts["w_lru_out"], parts["w_attn_out"], parts["w_o"] = mixout_parts
    (dxr, dxg, dvec, dwa, dwx), _ = _lru_bwd(dya_in, xr, xg, hseq, *lru_args)
    g_qkv, _ = _dw(dqkv, ub, D // 2, "dw_w_in_qkv")
    g_xr, _ = _dw(dxr, ub, D // 2, "dw_w_in_xr")
    g_xg, _ = _dw(dxg, ub, D // 2, "dw_w_in_xg")
    g_wgate, _ = _dw(dgpre, ub, D // 2, "dw_w_gate")
    g_win = jnp.concatenate(
        [g_qkv[:D]] + [g_qkv[c0 + gi * LANES:c0 + gi * LANES + HEAD_DIM] for c0 in (K0, V0) for gi in range(N_KV)]
        + [g_xr, g_xg], axis=0)
    (dh1, gsm["mix_pre_g"]), ((parts["w_in"],),) = _mix_proj_bwd(
        dqkv, dxr, dxg, dgpre, h1, dh2, vec("mix_pre_g"), wg, carries=rs(g_win))
    gsm["conv_w"] = dvec[0:CONV_WIDTH]
    gsm["conv_b"], gsm["rg_a_b"], gsm["rg_x_b"], gsm["lru_lambda"] = dvec[4], dvec[5], dvec[6], dvec[7]
    gsm["rg_a_w"] = _block_diag_grad(dwa)
    gsm["rg_x_w"] = _block_diag_grad(dwx)
    gsm["attn_sinks"] = dsink[0, :N_HEADS]
    gsm["rel_bias"] = _bias_grad(ds_acc.reshape(N_HEADS, QT, KW), bmap)[:, :N_HEADS]
    late = ("ffn1_post_g", "ffn1_pre_g")
    early = tuple(n for n in SMALL if n not in late)
    early_pack, early_offs = _pack_small([gsm[n] for n in early])
    (dab, s_act, dfb, gsm["ffn1_post_g"]), ((parts["w_gate"],), (early_all,)) = _ffn_bwd_a(
        dh1, f1, a1, b1, vec("ffn1_post_g"), wg, "ffn1_w2", carries=rs(g_wgate) + [_AllGather(early_pack)])
    g_w2, _ = _dw(s_act, dfb, FF // 2, "dw_ffn1_w2")
    g_w13, ((parts["ffn1_w2"],),) = _dw(dab, nb1, D // 2, "dw_ffn1_w13", carries=rs(g_w2))
    head = 64
    (grad_x, gsm["ffn1_pre_g"]), ((parts["ffn1_w1"], w3_head),) = _ffn_bwd_dx(
        dab, x2, dh1, vec("ffn1_pre_g"), wg, "ffn1_w1", "ffn1_w3",
        carries=rs((g_w13, 0, ffr, ffr), (g_w13, FF, ffr, head)))
    late_pack, late_offs = _pack_small([gsm[n] for n in late])

    grads, delta, new_m, new_v = {}, {}, {}, {}
    t_form = [n for n in COL_SHARDED if ws[n].shape[-1] % LANES]
    view = {n: (lambda a: a.reshape(a.shape[-2], a.shape[-1]).T) if n in t_form
            else (lambda a: a.reshape(a.shape[-2], a.shape[-1])) for n in BIG}
    unview = {n: (lambda a: a.T) if n in t_form else (lambda a: a) for n in BIG}

    def update(group, name, carries=()):
        res, carried = _sum_adamw([(parts[n], view[n](ws[n]), view[n](ms[n]), view[n](vs[n])) for n in group], name,
                                  carries=carries)
        for n, quad in zip(group, res):
            grads[n], delta[n], new_m[n], new_v[n] = (unview[n](a) for a in quad)
        return carried

    ready = [n for n in BIG if n not in ("ffn1_w3", "w_gate")]
    (w3_rest,), (late_all,) = update(ready, "update_ready",
                                     rs((g_w13, FF + head, ffr, ffr - head)) + [_AllGather(late_pack)])
    parts["ffn1_w3"] = jnp.concatenate([w3_head, w3_rest], axis=1)
    update(["ffn1_w3"], "update_last")
    (g_gate_t,) = _sum_parts([parts["w_gate"]])
    grads["w_gate"] = g_gate_t.T
    ((delta["w_gate"], new_m["w_gate"], new_v["w_gate"]),) = _adamw(
        [(grads["w_gate"], view["w_gate"](ws["w_gate"]), view["w_gate"](ms["w_gate"]), view["w_gate"](vs["w_gate"]))])
    for group, gathered, offs in ((early, early_all, early_offs), (late, late_all, late_offs)):
        total = _sum_small(gathered, gathered.shape[0] // NDEV)
        shapes = [(CONV_WIDTH, D) if n == "conv_w" else ws[n].shape for n in group]
        for n, g in zip(group, _unpack_small(total, offs, shapes)):
            grads[n] = g
    grads["conv_w"] = lax.dynamic_slice(grads["conv_w"], (0, me * LANES), (CONV_WIDTH, LANES)).reshape(conv_w.shape)

    flat2d = lambda a: a.reshape(-1, a.shape[-1])
    res = _adamw_small([(flat2d(grads[n].reshape(ws[n].shape)), flat2d(ws[n]), flat2d(ms[n]), flat2d(vs[n]))
                        for n in SMALL])
    for n, (d_, m_, v_) in zip(SMALL, res):
        delta[n], new_m[n], new_v[n] = d_, m_, v_

    outs = [loss, grad_x.reshape(x.shape)]
    for src in (grads, delta, new_m, new_v):
        outs += [src[n].reshape(ws[n].shape) for n in names]
    return tuple(outs)
```

```python
import functools
import math
import operator

import numpy as np
import jax
import jax.numpy as jnp
from jax import lax
from jax.experimental import pallas as pl
from jax.experimental.pallas import tpu as pltpu

F32, BF16 = jnp.float32, jnp.bfloat16

NDEV = 8
D = 1024
FF = 2816
N_HEADS, N_KV, HEAD_DIM = 16, 4, 64
CHUNK, WINDOW = 64, 128
N_BUCKETS, MAX_DISTANCE = 32, 128
LRU_BLOCK = 64
CONV_WIDTH = 4
LRU_C = 8.0
RMS_EPS = 1e-6
NEG_INF = -1e30
LANES = 128
QT = 128
KW = QT + WINDOW
LRU_ROWS = 512
IN_W = D + 2 * N_KV * HEAD_DIM + 2 * D
INP_W = D + 2 * N_KV * LANES + 2 * D
VMEM_BIG = 58 * 2 ** 20

ADAM_LR, ADAM_B1, ADAM_B2, ADAM_EPS, ADAM_WD, ADAM_STEP = 0.001, 0.9, 0.999, 1e-08, 0.01, 10

GROUPS = (("ffn1", (("ffn1_w1", FF // NDEV), ("ffn1_w3", FF // NDEV), ("ffn1_w2", FF // NDEV), ("conv", 16))),
          ("mixin", (("w_in", IN_W // NDEV), ("w_gate", 2 * D // NDEV))),
          ("mixout", (("w_lru_out", D // NDEV), ("w_attn_out", D // NDEV), ("w_o", D // NDEV))),
          ("ffn2a", (("ffn2_w1", FF // NDEV), ("ffn2_w3", FF // NDEV))),
          ("ffn2b", (("ffn2_w2", FF // NDEV),)))
LOC, GROUP_ROWS = {}, {}
for _g, _members in GROUPS:
    _o = 0
    for _n, _r in _members:
        LOC[_n] = (_g, _o, _r)
        _o += _r
    GROUP_ROWS[_g] = _o
BIG = tuple(n for _, members in GROUPS for n, _ in members if n != "conv")
COL_SHARDED = ("ffn1_w1", "ffn1_w3", "w_in", "w_gate", "ffn2_w1", "ffn2_w3")

SMALL = ("ffn1_pre_g", "ffn1_post_g", "mix_pre_g", "conv_w", "conv_b", "rg_a_w", "rg_a_b", "rg_x_w", "rg_x_b",
         "lru_lambda", "attn_sinks", "rel_bias", "b_gate", "mix_post_g", "ffn2_pre_g", "ffn2_post_g")

MESH = pl.DeviceIdType.MESH
ANY = pl.BlockSpec(memory_space=pl.ANY)
DMA = pltpu.SemaphoreType.DMA


def _nn(a, b):
    return lax.dot_general(a, b, (((1,), (0,)), ((), ())), preferred_element_type=F32)


def _nt(a, b):
    return lax.dot_general(a, b, (((1,), (1,)), ((), ())), preferred_element_type=F32)


def _tn(a, b):
    return lax.dot_general(a, b, (((0,), (0,)), ((), ())), preferred_element_type=F32)


def _rms_fwd(x, g):
    r = lax.rsqrt(jnp.mean(x * x, axis=-1, keepdims=True) + RMS_EPS)
    xh = x * r
    return xh * g, xh, r


def _rms_bwd(dn, xh, r, g):
    dxh = dn * g
    dx = r * (dxh - xh * jnp.mean(dxh * xh, axis=-1, keepdims=True))
    return dx, jnp.sum(dn * xh, axis=0, keepdims=True)


def _row_spec(tm, c):
    return pl.BlockSpec((tm, c), lambda i: (i, 0))


def _const_spec(shape):
    nd = len(shape)
    return pl.BlockSpec(shape, lambda i: (0,) * nd)


class _AllGather:
    def __init__(self, shard):
        self.m, n = shard.shape
        self.ins = [shard]
        self.out_shape = [jax.ShapeDtypeStruct((NDEV * self.m, n), shard.dtype)]
        self.scratch = [DMA((7,)), DMA((7,)), DMA]

    def _copies(self, ins, outs, scr, all_of_them):
        x_ref, out_ref = ins[0], outs[0]
        send_sems, recv_sems, local_sem = scr
        x, y, c = lax.axis_index("x"), lax.axis_index("y"), lax.axis_index("c")
        me, sibling = (x, y, c), (x, y, 1 - c)
        chips = [(1 - x, y), (x, 1 - y), (1 - x, 1 - y)]
        m = self.m

        def rows(px, py, pc):
            return out_ref.at[pl.ds((4 * px + 2 * py + pc) * m, m), :]

        def copy(k, block, to, src=None):
            return pltpu.make_async_remote_copy(
                src_ref=rows(*block) if src is None else src, dst_ref=rows(*block),
                send_sem=send_sems.at[k], recv_sem=recv_sems.at[k], device_id=to, device_id_type=MESH)

        mine = pltpu.make_async_copy(x_ref, rows(*me), local_sem)
        first = [copy(0, me, sibling, src=x_ref)] + [copy(1 + j, me, (*chip, c), src=x_ref)
                                                     for j, chip in enumerate(chips)]
        if not all_of_them:
            return mine, first
        passed = [copy(4 + j, (*chip, c), sibling) for j, chip in enumerate(chips)]
        landed = [copy(1 + j, (*chip, c), me) for j, chip in enumerate(chips)]
        from_sibling = [copy(0, sibling, me)] + [copy(4 + j, (*chip, 1 - c), me) for j, chip in enumerate(chips)]
        return mine, first, passed, landed, from_sibling

    def start(self, ins, outs, scr):
        mine, first = self._copies(ins, outs, scr, False)
        mine.start()
        for cp in first:
            cp.start()

    def finish(self, ins, outs, scr):
        mine, first, passed, landed, from_sibling = self._copies(ins, outs, scr, True)
        for cp_in, cp_on in zip(landed, passed):
            cp_in.wait_recv()
            cp_on.start()
        for cp in from_sibling:
            cp.wait_recv()
        for cp in first + passed:
            cp.wait_send()
        mine.wait()


class _ReduceScatterSend:
    def __init__(self, grads):
        grads = [g if isinstance(g, tuple) else (g, 0, g.shape[0] // NDEV, g.shape[0] // NDEV) for g in grads]
        self.nw = len(grads)
        self.base = [b for _, b, _, _ in grads]
        self.stride = [s for _, _, s, _ in grads]
        self.rows = [r for _, _, _, r in grads]
        self.ins = [g for g, _, _, _ in grads]
        self.out_shape = [jax.ShapeDtypeStruct((NDEV, r, g.shape[1]), g.dtype) for g, _, _, r in grads]
        self.scratch = [DMA((self.nw, NDEV - 1)), DMA((self.nw, NDEV - 1)), DMA((self.nw,))]

    def _copies(self, g_refs, r_refs, scr, want):
        send_sems, recv_sems, local_sems = scr
        x, y, c = lax.axis_index("x"), lax.axis_index("y"), lax.axis_index("c")
        me = 4 * x + 2 * y + c
        rows, base, stride = self.rows, self.base, self.stride
        out = []
        if want == "local":
            for w in range(self.nw):
                out.append(pltpu.make_async_copy(g_refs[w].at[pl.ds(base[w] + me * stride[w], rows[w])],
                                                 r_refs[w].at[me], local_sems.at[w]))
            return out
        for k in range(1, NDEV):
            px, py, pc = x ^ (k >> 2), y ^ ((k >> 1) & 1), c ^ (k & 1)
            peer = 4 * px + 2 * py + pc
            for w in range(self.nw):
                sems = dict(send_sem=send_sems.at[w, k - 1], recv_sem=recv_sems.at[w, k - 1],
                            device_id=(px, py, pc), device_id_type=MESH)
                if want == "send":
                    out.append(pltpu.make_async_remote_copy(
                        src_ref=g_refs[w].at[pl.ds(base[w] + peer * stride[w], rows[w])], dst_ref=r_refs[w].at[me],
                        **sems))
                else:
                    out.append(pltpu.make_async_remote_copy(
                        src_ref=g_refs[w].at[pl.ds(0, rows[w])], dst_ref=r_refs[w].at[peer], **sems))
        return out

    def start(self, ins, outs, scr):
        for cp in self._copies(ins, outs, scr, "local") + self._copies(ins, outs, scr, "send"):
            cp.start()

    def finish(self, ins, outs, scr):
        for cp in self._copies(ins, outs, scr, "recv"):
            cp.wait_recv()
        for cp in self._copies(ins, outs, scr, "send"):
            cp.wait_send()
        for cp in self._copies(ins, outs, scr, "local"):
            cp.wait()


def _call(body, *, name, grid, ins, in_specs, out_shape, out_specs, scratch=(), vmem=None, carries=()):
    n_in, n_out, n_scr = len(ins), len(out_shape), len(scratch)
    ng = len(grid)

    def split(refs):
        pos = [0]

        def take(k):
            part = refs[pos[0]:pos[0] + k]
            pos[0] += k
            return part

        i_refs = take(n_in)
        c_in = [take(len(c.ins)) for c in carries]
        o_refs = take(n_out)
        c_out = [take(len(c.out_shape)) for c in carries]
        s_refs = take(n_scr)
        c_scr = [take(len(c.scratch)) for c in carries]
        return i_refs, o_refs, s_refs, list(zip(carries, c_in, c_out, c_scr))

    def full(*refs):
        i_refs, o_refs, s_refs, cparts = split(refs)
        if ng == 0:
            for c, a, b, s in cparts:
                c.start(a, b, s)
            for c, a, b, s in cparts:
                c.finish(a, b, s)
            return
        ids = [pl.program_id(a) for a in range(ng)]
        if cparts:
            @pl.when(functools.reduce(operator.and_, [i == 0 for i in ids]))
            def _():
                for c, a, b, s in cparts:
                    c.start(a, b, s)

        body(*i_refs, *o_refs, *s_refs)
        if cparts:
            @pl.when(functools.reduce(operator.and_, [i == g - 1 for i, g in zip(ids, grid)]))
            def _():
                for c, a, b, s in cparts:
                    c.finish(a, b, s)

    all_ins = list(ins) + [a for c in carries for a in c.ins]
    all_in_specs = list(in_specs) + [ANY for c in carries for _ in c.ins]
    all_out_shape = list(out_shape) + [s for c in carries for s in c.out_shape]
    all_out_specs = list(out_specs) + [ANY for c in carries for _ in c.out_shape]
    all_scratch = list(scratch) + [s for c in carries for s in c.scratch]
    kwargs = dict(grid=grid) if ng else {}
    outs = pl.pallas_call(
        full, name=name, in_specs=all_in_specs, out_specs=all_out_specs, out_shape=all_out_shape,
        scratch_shapes=all_scratch,
        compiler_params=pltpu.CompilerParams(dimension_semantics=("arbitrary",) * ng if ng else None,
                                             vmem_limit_bytes=vmem),
        **kwargs)(*all_ins)
    outs = list(outs)
    res, pos = outs[:n_out], n_out
    carried = []
    for c in carries:
        carried.append(outs[pos:pos + len(c.out_shape)])
        pos += len(c.out_shape)
    return res, carried


def _groups_of(names):
    out = []
    for n in names:
        if LOC[n][0] not in out:
            out.append(LOC[n][0])
    return out


def _weight_pieces(name):
    g, off, rows = LOC[name]
    return [(d * GROUP_ROWS[g] + off, d * rows, rows) for d in range(NDEV)]


def _win_pieces():
    kv = N_KV * HEAD_DIM
    pieces = [(0, 0, D)]
    for g0, d0 in ((D, D), (D + kv, D + N_KV * LANES)):
        for g in range(N_KV):
            for half in range(2):
                pieces.append((g0 + g * HEAD_DIM, d0 + g * LANES + half * HEAD_DIM, HEAD_DIM))
    pieces.append((D + 2 * kv, D + 2 * N_KV * LANES, D))
    pieces.append((D + 2 * kv + D, D + 2 * N_KV * LANES + D, D))
    grp, off, rows = LOC["w_in"]
    out = []
    for g0, d0, n in pieces:
        while n > 0:
            dev, loc = divmod(g0, rows)
            m = min(n, rows - loc)
            out.append((dev * GROUP_ROWS[grp] + off + loc, d0, m))
            g0, d0, n = g0 + m, d0 + m, n - m
    return out


def _start_loads(src_ref, dst_ref, pieces, sems, base):
    cps = []
    for j, (s, d, n) in enumerate(pieces):
        cp = pltpu.make_async_copy(src_ref.at[pl.ds(s, n)], dst_ref.at[pl.ds(d, n)], sems.at[base + j])
        cp.start()
        cps.append(cp)
    return cps


def _load_weights(wrefs, targets, sems):
    cps, base = [], 0
    for name, dst in targets:
        pieces = _win_pieces() if name == "w_in" else _weight_pieces(name)
        cps += _start_loads(wrefs[LOC[name][0]], dst, pieces, sems, base)
        base += len(pieces)
    for cp in cps:
        cp.wait()


def _n_pieces(names):
    return sum(len(_win_pieces()) if n == "w_in" else NDEV for n in names)


def _dw(lhs, rhs, chunk, name, carries=()):
    nq = 4
    if lhs.ndim == 3:
        nch, t_tok, chunk = lhs.shape
        c = nch * chunk
        tq = t_tok // nq
        lhs_specs = [pl.BlockSpec((None, tq, chunk), lambda i, k=k: (i, k, 0)) for k in range(nq)]
    else:
        t_tok, c = lhs.shape
        tq = t_tok // nq
        lhs_specs = [pl.BlockSpec((tq, chunk), lambda i, k=k: (k, i)) for k in range(nq)]

    def body(*refs):
        lhs_refs, (rhs_ref, out_ref, rhs_s, sems) = refs[:nq], refs[nq:]
        cps = [pltpu.make_async_copy(rhs_ref.at[pl.ds(k * tq, tq)], rhs_s.at[pl.ds(k * tq, tq)], sems.at[k])
               for k in range(nq)]
        first = pl.program_id(0) == 0

        @pl.when(first)
        def _():
            for cp in cps:
                cp.start()

        for o in range(0, chunk, D // 2):
            n = min(D // 2, chunk - o)
            acc = None
            for k in range(nq):
                if o == 0:
                    @pl.when(first)
                    def _():
                        cps[k].wait()

                part = _tn(lhs_refs[k][:, pl.ds(o, n)], rhs_s[pl.ds(k * tq, tq), :])
                acc = part if acc is None else acc + part
            out_ref[pl.ds(o, n), :] = acc.astype(BF16)

    (out,), carried = _call(
        body, name=name, grid=(c // chunk,), ins=[lhs] * nq + [rhs],
        in_specs=lhs_specs + [ANY],
        out_specs=[pl.BlockSpec((chunk, D), lambda i: (i, 0))],
        out_shape=[jax.ShapeDtypeStruct((c, D), BF16)],
        scratch=[pltpu.VMEM((t_tok, D), BF16), DMA((nq,))], vmem=VMEM_BIG, carries=carries)
    return out, carried


def _silu_parts(a):
    sig = jax.nn.sigmoid(a)
    return sig, a * sig


FC = 256


def _ffn_fwd(h, gpre, gpost, wg, names, target=None, tm=512, carries=()):
    t_tok = h.shape[0]
    nt = t_tok // tm
    with_loss = target is not None
    groups = _groups_of(names)

    def body(*refs):
        refs = list(refs)
        h_ref, gpre_ref, gpost_ref = refs[:3]
        del refs[:3]
        tgt_ref = refs.pop(0) if with_loss else None
        wrefs = dict(zip(groups, refs[:len(groups)]))
        del refs[:len(groups)]
        hout_ref, a_ref, b_ref, f_ref, nb_ref = refs[:5]
        del refs[:5]
        if with_loss:
            dy_ref, loss_ref = refs[:2]
            del refs[:2]
        w1_s, w3_s, w2_s, sems = refs
        i = pl.program_id(0)

        @pl.when(i == 0)
        def _():
            _load_weights(wrefs, list(zip(names, (w1_s, w3_s, w2_s))), sems)
            if with_loss:
                loss_ref[...] = jnp.zeros_like(loss_ref)

        x = h_ref[...]
        n, _, _ = _rms_fwd(x, gpre_ref[...])
        nb = n.astype(BF16)
        nb_ref[...] = nb
        f = jnp.zeros((tm, D), F32)
        for c0 in range(0, FF, FC):
            a = _nt(nb, w1_s[pl.ds(c0, FC), :])
            b = _nt(nb, w3_s[pl.ds(c0, FC), :])
            _, sl = _silu_parts(a)
            a_ref[:, pl.ds(c0, FC)] = a.astype(BF16)
            b_ref[:, pl.ds(c0, FC)] = b.astype(BF16)
            f = f + _nn((sl * b).astype(BF16), w2_s[pl.ds(c0, FC), :])
        f_ref[...] = f
        fn, _, _ = _rms_fwd(f, gpost_ref[...])
        y = x + 0.5 * fn
        hout_ref[...] = y
        if with_loss:
            err = y - tgt_ref[...]
            dy_ref[...] = err * (1.0 / D)
            loss_ref[...] += jnp.sum(jnp.sum(err * err, axis=-1, keepdims=True), axis=0, keepdims=True) * (0.5 / D)

    ins = [h, gpre, gpost] + ([target] if with_loss else []) + [wg[g] for g in groups]
    in_specs = [_row_spec(tm, D), _const_spec((1, D)), _const_spec((1, D))]
    in_specs += ([_row_spec(tm, D)] if with_loss else []) + [ANY] * len(groups)
    out_shape = [jax.ShapeDtypeStruct((t_tok, D), F32), jax.ShapeDtypeStruct((t_tok, FF), BF16),
                 jax.ShapeDtypeStruct((t_tok, FF), BF16), jax.ShapeDtypeStruct((t_tok, D), F32),
                 jax.ShapeDtypeStruct((t_tok, D), BF16)]
    out_specs = [_row_spec(tm, D), _row_spec(tm, FF), _row_spec(tm, FF), _row_spec(tm, D), _row_spec(tm, D)]
    if with_loss:
        out_shape += [jax.ShapeDtypeStruct((t_tok, D), F32), jax.ShapeDtypeStruct((1, LANES), F32)]
        out_specs += [_row_spec(tm, D), _const_spec((1, LANES))]
    return _call(body, name="ffn_fwd_" + names[0][:4], grid=(nt,), ins=ins, in_specs=in_specs,
                 out_shape=out_shape, out_specs=out_specs,
                 scratch=[pltpu.VMEM((FF, D), BF16)] * 3 + [DMA((3 * NDEV,))], vmem=VMEM_BIG, carries=carries)


FH = FF // 2
HALF_PIECES = ((0, 256), (256, 256), (512, 256), (768, 256), (1024, 256), (1280, 128))


def _ffn_bwd_a(dh, f, a, b, gpost, wg, name_w2, tm=512, carries=()):
    t_tok = dh.shape[0]
    nt = t_tok // tm
    groups = _groups_of([name_w2])

    def body(dh_ref, f_ref, a_ref, b_ref, gpost_ref, wg_ref, dab_ref, s_ref, df_ref, dgp_ref, w2_s, sems):
        i = pl.program_id(0)

        @pl.when(i == 0)
        def _():
            _load_weights({groups[0]: wg_ref}, [(name_w2, w2_s)], sems)
            dgp_ref[...] = jnp.zeros_like(dgp_ref)

        fv = f_ref[...]
        _, fh, r = _rms_fwd(fv, gpost_ref[...])
        df, dg = _rms_bwd(0.5 * dh_ref[...], fh, r, gpost_ref[...])
        dgp_ref[...] += dg
        dfb = df.astype(BF16)
        df_ref[...] = dfb
        for half in range(2):
            for o, n in HALF_PIECES:
                c0 = half * FH + o
                ds = _nt(dfb, w2_s[pl.ds(c0, n), :])
                av = a_ref[:, pl.ds(c0, n)].astype(F32)
                bv = b_ref[:, pl.ds(c0, n)].astype(F32)
                sig, sl = _silu_parts(av)
                dab_ref[half, :, pl.ds(o, n)] = (ds * bv * (sig * (1.0 + av * (1.0 - sig)))).astype(BF16)
                dab_ref[2 + half, :, pl.ds(o, n)] = (ds * sl).astype(BF16)
                s_ref[half, :, pl.ds(o, n)] = (sl * bv).astype(BF16)

    return _call(
        body, name="ffn_bwd_a_" + name_w2[:4], grid=(nt,), ins=[dh, f, a, b, gpost, wg[groups[0]]],
        in_specs=[_row_spec(tm, D), _row_spec(tm, D), _row_spec(tm, FF), _row_spec(tm, FF), _const_spec((1, D)), ANY],
        out_specs=[pl.BlockSpec((4, tm, FH), lambda i: (0, i, 0)), pl.BlockSpec((2, tm, FH), lambda i: (0, i, 0)),
                   _row_spec(tm, D), _const_spec((1, D))],
        out_shape=[jax.ShapeDtypeStruct((4, t_tok, FH), BF16), jax.ShapeDtypeStruct((2, t_tok, FH), BF16),
                   jax.ShapeDtypeStruct((t_tok, D), BF16), jax.ShapeDtypeStruct((1, D), F32)],
        scratch=[pltpu.VMEM((FF, D), BF16), DMA((NDEV,))],
        vmem=VMEM_BIG, carries=carries)


def _ffn_bwd_dx(dab, h, dh, gpre, wg, name_w1, name_w3, tm=512, carries=()):
    t_tok = dh.shape[0]
    nt = t_tok // tm
    groups = _groups_of([name_w1, name_w3])

    def body(*refs):
        dab_refs, (h_ref, dh_ref, gpre_ref) = refs[:4], refs[4:7]
        wrefs = dict(zip(groups, refs[7:7 + len(groups)]))
        dhin_ref, dgp_ref, w13_s, sems = refs[7 + len(groups):]
        i = pl.program_id(0)

        @pl.when(i == 0)
        def _():
            _load_weights(wrefs, [(name_w1, w13_s.at[pl.ds(0, FF)]), (name_w3, w13_s.at[pl.ds(FF, FF)])], sems)
            dgp_ref[...] = jnp.zeros_like(dgp_ref)

        g = gpre_ref[...]
        _, xh, r = _rms_fwd(h_ref[...], g)
        dn = _nn(dab_refs[0][...], w13_s[pl.ds(0, FH), :])
        for k in range(1, 4):
            dn = dn + _nn(dab_refs[k][...], w13_s[pl.ds(k * FH, FH), :])
        dx, dg = _rms_bwd(dn, xh, r, g)
        dgp_ref[...] += dg
        dhin_ref[...] = dh_ref[...] + dx

    return _call(
        body, name="ffn_bwd_dx_" + name_w1[:4], grid=(nt,), ins=[dab] * 4 + [h, dh, gpre] + [wg[g] for g in groups],
        in_specs=[pl.BlockSpec((None, tm, FH), lambda i, k=k: (k, i, 0)) for k in range(4)]
        + [_row_spec(tm, D), _row_spec(tm, D), _const_spec((1, D))] + [ANY] * len(groups),
        out_specs=[_row_spec(tm, D), _const_spec((1, D))],
        out_shape=[jax.ShapeDtypeStruct((t_tok, D), F32), jax.ShapeDtypeStruct((1, D), F32)],
        scratch=[pltpu.VMEM((2 * FF, D), BF16), DMA((2 * NDEV,))],
        vmem=VMEM_BIG, carries=carries)


Q0, K0, V0, XR0, XG0 = 0, D, D + N_KV * LANES, D + 2 * N_KV * LANES, 2 * D + 2 * N_KV * LANES


def _mix_proj_fwd(h, g, bgate, wg, tm=512, carries=()):
    t_tok = h.shape[0]
    nt = t_tok // tm
    names = ("w_in", "w_gate")
    groups = _groups_of(names)

    def body(h_ref, g_ref, bg_ref, wg_ref, q_ref, k_ref, v_ref, xr_ref, xg_ref, gs_ref, ub_ref, win_s, wgt_s, sems):
        i = pl.program_id(0)

        @pl.when(i == 0)
        def _():
            _load_weights({groups[0]: wg_ref}, [("w_in", win_s), ("w_gate", wgt_s)], sems)

        n, _, _ = _rms_fwd(h_ref[...], g_ref[...])
        nb = n.astype(BF16)
        ub_ref[...] = nb
        q_ref[...] = _nt(nb, win_s[pl.ds(Q0, D), :]).astype(BF16)
        k_ref[...] = _nt(nb, win_s[pl.ds(K0, N_KV * LANES), :]).astype(BF16)
        v_ref[...] = _nt(nb, win_s[pl.ds(V0, N_KV * LANES), :]).astype(BF16)
        xr_ref[...] = _nt(nb, win_s[pl.ds(XR0, D), :])
        xg_ref[...] = _nt(nb, win_s[pl.ds(XG0, D), :])
        gs_ref[...] = jax.nn.sigmoid(_nt(nb, wgt_s[...]) + bg_ref[...])

    kvw = N_KV * LANES
    return _call(
        body, name="mix_proj_fwd", grid=(nt,), ins=[h, g, bgate, wg[groups[0]]],
        in_specs=[_row_spec(tm, D), _const_spec((1, D)), _const_spec((1, 2 * D)), ANY],
        out_specs=[_row_spec(tm, D), _row_spec(tm, kvw), _row_spec(tm, kvw), _row_spec(tm, D), _row_spec(tm, D),
                   _row_spec(tm, 2 * D), _row_spec(tm, D)],
        out_shape=[jax.ShapeDtypeStruct((t_tok, D), BF16), jax.ShapeDtypeStruct((t_tok, kvw), BF16),
                   jax.ShapeDtypeStruct((t_tok, kvw), BF16), jax.ShapeDtypeStruct((t_tok, D), F32),
                   jax.ShapeDtypeStruct((t_tok, D), F32), jax.ShapeDtypeStruct((t_tok, 2 * D), F32),
                   jax.ShapeDtypeStruct((t_tok, D), BF16)],
        scratch=[pltpu.VMEM((INP_W, D), BF16), pltpu.VMEM((2 * D, D), BF16), DMA((_n_pieces(names),))],
        vmem=VMEM_BIG, carries=carries)


def _mix_proj_bwd(dqkv, dxr, dxg, dgpre, h, dh, g, wg, tm=512, carries=()):
    t_tok = h.shape[0]
    nt = t_tok // tm
    names = ("w_in", "w_gate")
    groups = _groups_of(names)

    def body(dqkv_ref, dxr_ref, dxg_ref, dgp_ref, h_ref, dh_ref, g_ref, wg_ref, dhin_ref, dg_ref, win_s, wgt_s, sems):
        i = pl.program_id(0)

        @pl.when(i == 0)
        def _():
            _load_weights({groups[0]: wg_ref}, [("w_in", win_s), ("w_gate", wgt_s)], sems)
            dg_ref[...] = jnp.zeros_like(dg_ref)

        gv = g_ref[...]
        _, xh, r = _rms_fwd(h_ref[...], gv)
        du = _nn(dgp_ref[...], wgt_s[...])
        du = du + _nn(dqkv_ref[...], win_s[pl.ds(Q0, XR0), :])
        du = du + _nn(dxr_ref[...], win_s[pl.ds(XR0, D), :])
        du = du + _nn(dxg_ref[...], win_s[pl.ds(XG0, D), :])
        dx, dg = _rms_bwd(du, xh, r, gv)
        dg_ref[...] += dg
        dhin_ref[...] = dh_ref[...] + dx

    return _call(
        body, name="mix_proj_bwd", grid=(nt,), ins=[dqkv, dxr, dxg, dgpre, h, dh, g, wg[groups[0]]],
        in_specs=[_row_spec(tm, XR0), _row_spec(tm, D), _row_spec(tm, D), _row_spec(tm, 2 * D), _row_spec(tm, D),
                  _row_spec(tm, D), _const_spec((1, D)), ANY],
        out_specs=[_row_spec(tm, D), _const_spec((1, D))],
        out_shape=[jax.ShapeDtypeStruct((t_tok, D), F32), jax.ShapeDtypeStruct((1, D), F32)],
        scratch=[pltpu.VMEM((INP_W, D), BF16), pltpu.VMEM((2 * D, D), BF16), DMA((_n_pieces(names),))],
        vmem=VMEM_BIG, carries=carries)


def _shift_down(x, d, fill):
    row = lax.broadcasted_iota(jnp.int32, x.shape, 0)
    return jnp.where(row >= d, pltpu.roll(x, d, 0), fill)


def _shift_up(x, d, fill):
    rows = x.shape[0]
    row = lax.broadcasted_iota(jnp.int32, x.shape, 0)
    return jnp.where(row < rows - d, pltpu.roll(x, rows - d, 0), fill)


def _scan_rows(a, b, reverse):
    rows = a.shape[0]
    d = 1
    while d < rows:
        if d < 8:
            shift = _shift_up if reverse else _shift_down
            b = a * shift(b, d, 0.0) + b
            a = a * shift(a, d, 1.0)
        elif reverse:
            b = jnp.concatenate([a[:rows - d] * b[d:] + b[:rows - d], b[rows - d:]], axis=0)
            a = jnp.concatenate([a[:rows - d] * a[d:], a[rows - d:]], axis=0)
        else:
            b = jnp.concatenate([b[:d], a[d:] * b[:rows - d] + b[d:]], axis=0)
            a = jnp.concatenate([a[:d], a[d:] * a[:rows - d]], axis=0)
        d *= 2
    return a, b


def _softplus(x):
    return jnp.maximum(x, 0.0) + jnp.log(1.0 + jnp.exp(-jnp.abs(x)))


_GELU_C = math.sqrt(2.0 / math.pi)


def _gelu_parts(x):
    th = jnp.tanh(_GELU_C * (x + 0.044715 * x * x * x))
    val = 0.5 * x * (1.0 + th)
    grad = 0.5 * (1.0 + th) + 0.5 * x * (1.0 - th * th) * _GELU_C * (1.0 + 3.0 * 0.044715 * x * x)
    return val, grad


def _lru_pre(x, halo, cw_ref, cb_ref, wa_ref, wx_ref, ba_ref, bx_ref, lam_ref):
    ext = jnp.concatenate([halo, x], axis=0)
    shifted = [x] + [pltpu.roll(ext, k, 0)[8:] for k in (1, 2, 3)]
    xc = cb_ref[...] + cw_ref[pl.ds(CONV_WIDTH - 1, 1), :] * x
    for k in (1, 2, 3):
        xc = xc + cw_ref[pl.ds(CONV_WIDTH - 1 - k, 1), :] * shifted[k]
    xcb = xc.astype(BF16)
    r = jax.nn.sigmoid(_nn(xcb, wa_ref[...]) + ba_ref[...])
    ig = jax.nn.sigmoid(_nn(xcb, wx_ref[...]) + bx_ref[...])
    sp = _softplus(-lam_ref[...])
    log_a = -LRU_C * r * sp
    a = jnp.exp(log_a)
    th = jnp.tanh(log_a)
    mult = jnp.sqrt(-2.0 * th / (1.0 - th))
    return shifted, xc, xcb, r, ig, sp, a, mult


def _lru_specs(nt, reverse):
    def tt(t):
        return nt - 1 - t if reverse else t
    tile = pl.BlockSpec((LRU_ROWS, LANES), lambda cb, t: (tt(t), cb))
    halo = pl.BlockSpec((8, LANES), lambda cb, t: (jnp.maximum(tt(t) * (LRU_ROWS // 8) - 1, 0), cb))
    vec = pl.BlockSpec((1, LANES), lambda cb, t: (0, cb))
    cw = pl.BlockSpec((CONV_WIDTH, LANES), lambda cb, t: (0, cb))
    mat = pl.BlockSpec((None, LANES, LANES), lambda cb, t: (cb, 0, 0))
    return tile, halo, vec, cw, mat


def _lru_fwd(xr, xg, cw, cb, wa, wx, ba, bx, lam, carries=()):
    t_tok = xr.shape[0]
    nt = t_tok // LRU_ROWS
    rows = LRU_ROWS

    def body(xr_ref, xg_ref, cw_ref, cb_ref, wa_ref, wx_ref, ba_ref, bx_ref, lam_ref, y_ref, h_ref, tail_s, hc_s):
        t = pl.program_id(1)

        @pl.when(t == 0)
        def _():
            tail_s[...] = jnp.zeros_like(tail_s)
            hc_s[...] = jnp.zeros_like(hc_s)

        x = xr_ref[...]
        _, xc, _, _, ig, _, a, mult = _lru_pre(x, tail_s[...], cw_ref, cb_ref, wa_ref, wx_ref, ba_ref, bx_ref, lam_ref)
        tail_s[...] = xr_ref[pl.ds(rows - 8, 8), :]
        acc_a, acc_b = _scan_rows(a, mult * (ig * xc), False)
        hv = acc_b + acc_a * hc_s[...]
        h_ref[...] = hv
        hc_s[...] = h_ref[pl.ds(rows - 1, 1), :]
        gl, _ = _gelu_parts(xg_ref[...])
        y_ref[...] = (hv * gl).astype(BF16)

    tile, _, vec, cws, mat = _lru_specs(nt, False)
    return _call(
        body, name="lru_fwd", grid=(D // LANES, nt), ins=[xr, xg, cw, cb, wa, wx, ba, bx, lam],
        in_specs=[tile, tile, cws, vec, mat, mat, vec, vec, vec],
        out_specs=[tile, tile],
        out_shape=[jax.ShapeDtypeStruct((t_tok, D), BF16), jax.ShapeDtypeStruct((t_tok, D), F32)],
        scratch=[pltpu.VMEM((8, LANES), F32), pltpu.VMEM((1, LANES), F32)], carries=carries)


def _lru_bwd(dy, xr, xg, hseq, cw, cb, wa, wx, ba, bx, lam, carries=()):
    t_tok = xr.shape[0]
    nt = t_tok // LRU_ROWS
    rows = LRU_ROWS

    def body(dy_ref, xr_ref, xrh_ref, xg_ref, h_ref, hh_ref, cw_ref, cb_ref, wa_ref, wx_ref, ba_ref, bx_ref, lam_ref,
             dxr_ref, dxg_ref, dvec_ref, dwa_ref, dwx_ref, gcar_s, acar_s, head_s, tmp_s):
        t = pl.program_id(1)
        first_tile = t == nt - 1

        @pl.when(t == 0)
        def _():
            gcar_s[...] = jnp.zeros_like(gcar_s)
            acar_s[...] = jnp.zeros_like(acar_s)
            head_s[...] = jnp.zeros_like(head_s)
            dvec_ref[...] = jnp.zeros_like(dvec_ref)
            dwa_ref[...] = jnp.zeros_like(dwa_ref)
            dwx_ref[...] = jnp.zeros_like(dwx_ref)

        x = xr_ref[...]
        halo = jnp.where(first_tile, 0.0, xrh_ref[...])
        shifted, xc, xcb, r, ig, sp, a, mult = _lru_pre(x, halo, cw_ref, cb_ref, wa_ref, wx_ref, ba_ref, bx_ref, lam_ref)
        hv = h_ref[...]
        dyv = dy_ref[...]
        gl, glg = _gelu_parts(xg_ref[...])
        dxg_ref[...] = (dyv * hv * glg).astype(BF16)
        acc_a, acc_b = _scan_rows(_shift_up(a, 1, acar_s[...]), dyv * gl, True)
        g = acc_b + acc_a * gcar_s[...]
        hhalo = jnp.where(first_tile, 0.0, hh_ref[...])
        hprev = pltpu.roll(jnp.concatenate([hhalo, hv], axis=0), 1, 0)[8:]
        dmult = g * ig * xc
        dlog_a = a * (g * hprev) - dmult * a * a / mult
        dig = g * mult * xc
        dxc = g * mult * ig
        dzr = (dlog_a * (-LRU_C * sp)) * r * (1.0 - r)
        dzx = dig * ig * (1.0 - ig)
        dzrb, dzxb = dzr.astype(BF16), dzx.astype(BF16)
        dxc = dxc + _nt(dzrb, wa_ref[...]) + _nt(dzxb, wx_ref[...])
        dwa_ref[...] += _tn(xcb, dzrb)
        dwx_ref[...] += _tn(xcb, dzxb)
        dsp = jnp.sum(dlog_a * (-LRU_C * r), axis=0, keepdims=True)
        dlam = dsp * (-jax.nn.sigmoid(-lam_ref[...]))
        vrow = lax.broadcasted_iota(jnp.int32, (8, LANES), 0)
        upd = jnp.where(vrow == 4, jnp.sum(dxc, axis=0, keepdims=True), 0.0)
        upd = jnp.where(vrow == 5, jnp.sum(dzr, axis=0, keepdims=True), upd)
        upd = jnp.where(vrow == 6, jnp.sum(dzx, axis=0, keepdims=True), upd)
        upd = jnp.where(vrow == 7, dlam, upd)
        for k in range(CONV_WIDTH):
            upd = jnp.where(vrow == CONV_WIDTH - 1 - k, jnp.sum(dxc * shifted[k], axis=0, keepdims=True), upd)
        dvec_ref[...] += upd
        ext = jnp.concatenate([dxc, head_s[...]], axis=0)
        dxr = cw_ref[pl.ds(CONV_WIDTH - 1, 1), :] * dxc
        for k in (1, 2, 3):
            dxr = dxr + cw_ref[pl.ds(CONV_WIDTH - 1 - k, 1), :] * pltpu.roll(ext, rows + 8 - k, 0)[:rows]
        dxr_ref[...] = dxr.astype(BF16)
        tmp_s[...] = g
        gcar_s[...] = tmp_s[pl.ds(0, 1), :]
        tmp_s[...] = a
        acar_s[...] = tmp_s[pl.ds(0, 1), :]
        tmp_s[...] = dxc
        head_s[...] = tmp_s[pl.ds(0, 8), :]

    tile, halo, vec, cws, mat = _lru_specs(nt, True)
    return _call(
        body, name="lru_bwd", grid=(D // LANES, nt), ins=[dy, xr, xr, xg, hseq, hseq, cw, cb, wa, wx, ba, bx, lam],
        in_specs=[tile, tile, halo, tile, tile, halo, cws, vec, mat, mat, vec, vec, vec],
        out_specs=[tile, tile, pl.BlockSpec((8, LANES), lambda cb, t: (0, cb)), mat, mat],
        out_shape=[jax.ShapeDtypeStruct((t_tok, D), BF16), jax.ShapeDtypeStruct((t_tok, D), BF16),
                   jax.ShapeDtypeStruct((8, D), F32), jax.ShapeDtypeStruct((D // LANES, LANES, LANES), F32),
                   jax.ShapeDtypeStruct((D // LANES, LANES, LANES), F32)],
        scratch=[pltpu.VMEM((1, LANES), F32), pltpu.VMEM((1, LANES), F32), pltpu.VMEM((8, LANES), F32),
                 pltpu.VMEM((rows, LANES), F32)], carries=carries)


def _t5_bucket_np(rel):
    nb = N_BUCKETS // 2
    max_exact = nb // 2
    ret = np.where(rel > 0, nb, 0)
    n = np.abs(rel)
    nf = np.maximum(n, 1).astype(np.float32)
    large = max_exact + (np.log(nf / np.float32(max_exact)) / np.float32(math.log(MAX_DISTANCE / max_exact))
                         * np.float32(nb - max_exact)).astype(np.int32)
    large = np.minimum(large, nb - 1)
    return ret + np.where(n < max_exact, n, large)


def _bucket_map():
    r = np.arange(QT)[:, None]
    c = np.arange(KW)[None, :]
    j = c - (r // CHUNK) * CHUNK
    band = (j >= 0) & (j < WINDOW + CHUNK)
    return np.where(band, _t5_bucket_np(c - r - WINDOW), -1).astype(np.int32)


def _attn_specs(nt, reverse):
    def tt(i):
        return nt - 1 - i if reverse else i
    kvw = N_KV * LANES
    qs = pl.BlockSpec((QT, D), lambda i: (tt(i), 0))
    cur = pl.BlockSpec((QT, kvw), lambda i: (tt(i), 0))
    prev = pl.BlockSpec((WINDOW, kvw), lambda i: (jnp.maximum(tt(i) * (QT // WINDOW) - 1, 0), 0))
    lse = pl.BlockSpec((QT, LANES), lambda i: (tt(i), 0))
    return qs, cur, prev, lse


REP = N_HEADS // N_KV
SCALE = HEAD_DIM ** -0.5


def _stack_heads(x_ref, g, lo, scale=None):
    parts = []
    for hl in range(REP):
        xs = x_ref[:, pl.ds((2 * g + hl // 2) * LANES, LANES)]
        xs = jnp.where(lo if hl % 2 == 0 else jnp.logical_not(lo), xs, jnp.zeros_like(xs))
        parts.append(xs if scale is None else xs * jnp.asarray(scale, xs.dtype))
    return jnp.concatenate(parts, axis=0)


def _stack_sinks(sink_ref, g, srow):
    sk = jnp.full(srow.shape, sink_ref[REP * g + REP - 1], F32)
    for hl in range(REP - 2, -1, -1):
        sk = jnp.where(srow < (hl + 1) * QT, sink_ref[REP * g + hl], sk)
    return sk


def _attn_fwd(q, kd, vd, bias, sinks, carries=()):
    t_tok = q.shape[0]
    nt = t_tok // QT

    def body(q_ref, kp_ref, kc_ref, vp_ref, vc_ref, bias_ref, sink_ref, o_ref, lse_ref):
        i = pl.program_id(0)
        col = lax.broadcasted_iota(jnp.int32, (1, KW), 1)
        first = jnp.where((i == 0) & (col < WINDOW), NEG_INF, 0.0)
        lane = lax.broadcasted_iota(jnp.int32, (QT, LANES), 1)
        lo = lane < HEAD_DIM
        srow = lax.broadcasted_iota(jnp.int32, (REP * QT, 1), 0)
        lse_t = jnp.zeros((QT, LANES), F32)
        for g in range(N_KV):
            kwin = jnp.concatenate([kp_ref[:, pl.ds(g * LANES, LANES)], kc_ref[:, pl.ds(g * LANES, LANES)]], axis=0)
            vwin = jnp.concatenate([vp_ref[:, pl.ds(g * LANES, LANES)], vc_ref[:, pl.ds(g * LANES, LANES)]], axis=0)
            qst = _stack_heads(q_ref, g, lo, SCALE)
            s = _nt(qst, kwin) + (bias_ref[g] + first)
            sk = _stack_sinks(sink_ref, g, srow)
            m = jnp.maximum(jnp.max(s, axis=-1, keepdims=True), sk)
            e = jnp.exp(s - m)
            l = jnp.sum(e, axis=-1, keepdims=True) + jnp.exp(sk - m)
            p = e / l
            ost = _nn(p.astype(BF16), vwin)
            lse_s = m + jnp.log(l)
            for sl in range(2):
                o_ref[:, pl.ds((2 * g + sl) * LANES, LANES)] = jnp.where(
                    lo, ost[2 * sl * QT:(2 * sl + 1) * QT], ost[(2 * sl + 1) * QT:(2 * sl + 2) * QT]).astype(BF16)
            for hl in range(REP):
                lse_t = jnp.where(lane == REP * g + hl, lse_s[hl * QT:(hl + 1) * QT], lse_t)
        lse_ref[...] = lse_t

    qs, cur, prev, lse = _attn_specs(nt, False)
    return _call(
        body, name="attn_fwd", grid=(nt,), ins=[q, kd, kd, vd, vd, bias, sinks],
        in_specs=[qs, prev, cur, prev, cur, _const_spec((N_KV, REP * QT, KW)), pl.BlockSpec(memory_space=pltpu.SMEM)],
        out_specs=[qs, lse],
        out_shape=[jax.ShapeDtypeStruct((t_tok, D), BF16), jax.ShapeDtypeStruct((t_tok, LANES), F32)],
        vmem=48 * 2 ** 20, carries=carries)


def _attn_bwd(q, kd, vd, o, do, lse, bias, sinks, carries=()):
    t_tok = q.shape[0]
    nt = t_tok // QT
    kvw = N_KV * LANES

    def body(q_ref, kp_ref, kc_ref, vp_ref, vc_ref, o_ref, do_ref, lse_ref, bias_ref, sink_ref,
             dqkv_ref, ds_ref, dsink_ref, kcar_s, vcar_s):
        i = pl.program_id(0)
        tile = nt - 1 - i

        @pl.when(i == 0)
        def _():
            kcar_s[...] = jnp.zeros_like(kcar_s)
            vcar_s[...] = jnp.zeros_like(vcar_s)
            ds_ref[...] = jnp.zeros_like(ds_ref)
            dsink_ref[...] = jnp.zeros_like(dsink_ref)

        col = lax.broadcasted_iota(jnp.int32, (1, KW), 1)
        first = jnp.where((tile == 0) & (col < WINDOW), NEG_INF, 0.0)
        lane = lax.broadcasted_iota(jnp.int32, (QT, LANES), 1)
        lo = lane < HEAD_DIM
        lane_k = lax.broadcasted_iota(jnp.int32, (KW, LANES), 1)
        lane_1 = lax.broadcasted_iota(jnp.int32, (1, LANES), 1)
        srow = lax.broadcasted_iota(jnp.int32, (REP * QT, 1), 0)
        lse_t = lse_ref[...]
        dsink = jnp.zeros((1, LANES), F32)
        for g in range(N_KV):
            kwin = jnp.concatenate([kp_ref[:, pl.ds(g * LANES, LANES)], kc_ref[:, pl.ds(g * LANES, LANES)]], axis=0)
            vwin = jnp.concatenate([vp_ref[:, pl.ds(g * LANES, LANES)], vc_ref[:, pl.ds(g * LANES, LANES)]], axis=0)
            qst = _stack_heads(q_ref, g, lo, SCALE)
            dost = _stack_heads(do_ref, g, lo)
            od = [do_ref[:, pl.ds((2 * g + sl) * LANES, LANES)].astype(F32)
                  * o_ref[:, pl.ds((2 * g + sl) * LANES, LANES)].astype(F32) for sl in range(2)]
            drow = jnp.concatenate([jnp.sum(jnp.where(lo if hl % 2 == 0 else jnp.logical_not(lo), od[hl // 2], 0.0),
                                            axis=-1, keepdims=True) for hl in range(REP)], axis=0)
            lse_s = jnp.concatenate([jnp.sum(jnp.where(lane == REP * g + hl, lse_t, 0.0), axis=-1, keepdims=True)
                                     for hl in range(REP)], axis=0)
            s = _nt(qst, kwin) + (bias_ref[g] + first)
            p = jnp.exp(s - lse_s)
            ds = p * (_nt(dost, vwin) - drow)
            ds_ref[g] += ds
            tsink = -(jnp.exp(_stack_sinks(sink_ref, g, srow) - lse_s) * drow)
            for hl in range(REP):
                dsink = dsink + jnp.where(lane_1 == REP * g + hl,
                                          jnp.sum(tsink[hl * QT:(hl + 1) * QT], axis=0, keepdims=True), 0.0)
            dsb = ds.astype(BF16)
            dqst = _nn(dsb, kwin) * SCALE
            for sl in range(2):
                dqkv_ref[:, pl.ds((2 * g + sl) * LANES, LANES)] = jnp.where(
                    lo, dqst[2 * sl * QT:(2 * sl + 1) * QT], dqst[(2 * sl + 1) * QT:(2 * sl + 2) * QT]).astype(BF16)
            dk_acc = _tn(dsb, qst)
            dv_acc = _tn(p.astype(BF16), dost)
            dk_f = jnp.where(lane_k < HEAD_DIM, dk_acc + pltpu.roll(dk_acc, HEAD_DIM, 1), 0.0)
            dv_f = jnp.where(lane_k < HEAD_DIM, dv_acc + pltpu.roll(dv_acc, HEAD_DIM, 1), 0.0)
            for acc, col0, car in ((dk_f, K0, kcar_s), (dv_f, V0, vcar_s)):
                cs = pl.ds(g * LANES, LANES)
                co = pl.ds(col0 + g * LANES, LANES)
                if QT > WINDOW:
                    dqkv_ref[pl.ds(0, QT - WINDOW), co] = acc[WINDOW:QT].astype(BF16)
                dqkv_ref[pl.ds(QT - WINDOW, WINDOW), co] = (acc[QT:KW] + car[:, cs]).astype(BF16)
                car[:, cs] = acc[0:WINDOW]
        dsink_ref[...] += dsink

    qs, cur, prev, lse_s = _attn_specs(nt, True)
    return _call(
        body, name="attn_bwd", grid=(nt,), ins=[q, kd, kd, vd, vd, o, do, lse, bias, sinks],
        in_specs=[qs, prev, cur, prev, cur, qs, qs, lse_s, _const_spec((N_KV, REP * QT, KW)),
                  pl.BlockSpec(memory_space=pltpu.SMEM)],
        out_specs=[pl.BlockSpec((QT, XR0), lambda i: (nt - 1 - i, 0)), _const_spec((N_KV, REP * QT, KW)),
                   _const_spec((1, LANES))],
        out_shape=[jax.ShapeDtypeStruct((t_tok, XR0), BF16), jax.ShapeDtypeStruct((N_KV, REP * QT, KW), F32),
                   jax.ShapeDtypeStruct((1, LANES), F32)],
        scratch=[pltpu.VMEM((WINDOW, kvw), F32), pltpu.VMEM((WINDOW, kvw), F32)],
        vmem=VMEM_BIG, carries=carries)


def _bias_tile(table, bmap):
    def body(tab_ref, bm_ref, out_ref):
        bm = bm_ref[...]

        def per_head(hd, carry):
            acc = jnp.full((QT, KW), NEG_INF, F32)
            for b in range(N_BUCKETS):
                acc = jnp.where(bm == b, tab_ref[b, hd], acc)
            out_ref[hd] = acc
            return carry

        lax.fori_loop(0, N_HEADS, per_head, 0)

    return pl.pallas_call(
        body, name="bias_tile", out_shape=jax.ShapeDtypeStruct((N_HEADS, QT, KW), F32),
        in_specs=[pl.BlockSpec(memory_space=pltpu.SMEM), pl.BlockSpec(memory_space=pltpu.VMEM)],
        out_specs=pl.BlockSpec(memory_space=pltpu.VMEM))(table, bmap)


def _bias_grad(ds_acc, bmap):
    def body(ds_ref, bm_ref, out_ref):
        row = lax.broadcasted_iota(jnp.int32, (N_BUCKETS, LANES), 0)
        lane = lax.broadcasted_iota(jnp.int32, (N_BUCKETS, LANES), 1)
        bm = bm_ref[...]

        def per_head(hd, res):
            dsv = ds_ref[hd]
            for b in range(N_BUCKETS):
                val = jnp.sum(jnp.sum(jnp.where(bm == b, dsv, 0.0), axis=0, keepdims=True), axis=1, keepdims=True)
                res = jnp.where((row == b) & (lane == hd), val, res)
            return res

        out_ref[...] = lax.fori_loop(0, N_HEADS, per_head, jnp.zeros((N_BUCKETS, LANES), F32))

    return pl.pallas_call(
        body, name="bias_grad", out_shape=jax.ShapeDtypeStruct((N_BUCKETS, LANES), F32),
        in_specs=[pl.BlockSpec(memory_space=pltpu.VMEM), pl.BlockSpec(memory_space=pltpu.VMEM)],
        out_specs=pl.BlockSpec(memory_space=pltpu.VMEM))(ds_acc, bmap)


MIXOUT = ("w_lru_out", "w_attn_out", "w_o")


def _mix_out_fwd(ya_in, o, gs, h, g, wg, tm=256, carries=()):
    t_tok = h.shape[0]
    nt = t_tok // tm
    groups = _groups_of(MIXOUT)

    def body(ya_ref, o_ref, gs_ref, h_ref, g_ref, wg_ref, hout_ref, yao_ref, ybo_ref, z_ref, wa_s, wb_s, wo_s, sems):
        i = pl.program_id(0)

        @pl.when(i == 0)
        def _():
            _load_weights({groups[0]: wg_ref}, list(zip(MIXOUT, (wa_s, wb_s, wo_s))), sems)

        ya = _nn(ya_ref[...], wa_s[...])
        yb = _nn(o_ref[...], wb_s[...])
        yao_ref[...] = ya
        ybo_ref[...] = yb
        merged = gs_ref[:, pl.ds(0, D)] * ya + gs_ref[:, pl.ds(D, D)] * yb
        z = _nn(merged.astype(BF16), wo_s[...])
        z_ref[...] = z
        zn, _, _ = _rms_fwd(z, g_ref[...])
        hout_ref[...] = h_ref[...] + zn

    return _call(
        body, name="mix_out_fwd", grid=(nt,), ins=[ya_in, o, gs, h, g, wg[groups[0]]],
        in_specs=[_row_spec(tm, D), _row_spec(tm, D), _row_spec(tm, 2 * D), _row_spec(tm, D), _const_spec((1, D)), ANY],
        out_specs=[_row_spec(tm, D)] * 4,
        out_shape=[jax.ShapeDtypeStruct((t_tok, D), F32)] * 4,
        scratch=[pltpu.VMEM((D, D), BF16)] * 3 + [DMA((3 * NDEV,))],
        vmem=48 * 2 ** 20, carries=carries)


def _mix_out_bwd(dh, z, ya, yb, gs, g, wg, tm=512, carries=()):
    t_tok = dh.shape[0]
    nt = t_tok // tm
    groups = _groups_of(MIXOUT)

    def body(dh_ref, z_ref, ya_ref, yb_ref, gs_ref, g_ref, wg_ref,
             dyain_ref, do_ref, dgpre_ref, dya_ref, dyb_ref, mg_ref, dz_ref, dg_ref, dbg_ref,
             wa_s, wb_s, wo_s, sems):
        i = pl.program_id(0)

        @pl.when(i == 0)
        def _():
            _load_weights({groups[0]: wg_ref}, list(zip(MIXOUT, (wa_s, wb_s, wo_s))), sems)
            dg_ref[...] = jnp.zeros_like(dg_ref)
            dbg_ref[...] = jnp.zeros_like(dbg_ref)

        gv = g_ref[...]
        _, zh, r = _rms_fwd(z_ref[...], gv)
        dz, dg = _rms_bwd(dh_ref[...], zh, r, gv)
        dg_ref[...] += dg
        dzb = dz.astype(BF16)
        dz_ref[...] = dzb
        ga, gb = gs_ref[:, pl.ds(0, D)], gs_ref[:, pl.ds(D, D)]
        ya_v, yb_v = ya_ref[...], yb_ref[...]
        mg_ref[...] = (ga * ya_v + gb * yb_v).astype(BF16)
        dm = _nt(dzb, wo_s[...])
        dga = dm * ya_v * ga * (1.0 - ga)
        dgb = dm * yb_v * gb * (1.0 - gb)
        dgpre_ref[:, pl.ds(0, D)] = dga.astype(BF16)
        dgpre_ref[:, pl.ds(D, D)] = dgb.astype(BF16)
        dbg_ref[:, pl.ds(0, D)] += jnp.sum(dga, axis=0, keepdims=True)
        dbg_ref[:, pl.ds(D, D)] += jnp.sum(dgb, axis=0, keepdims=True)
        dya = (dm * ga).astype(BF16)
        dyb = (dm * gb).astype(BF16)
        dya_ref[...] = dya
        dyb_ref[...] = dyb
        dyain_ref[...] = _nt(dya, wa_s[...])
        do_ref[...] = _nt(dyb, wb_s[...]).astype(BF16)

    bf = jax.ShapeDtypeStruct((t_tok, D), BF16)
    return _call(
        body, name="mix_out_bwd", grid=(nt,), ins=[dh, z, ya, yb, gs, g, wg[groups[0]]],
        in_specs=[_row_spec(tm, D)] * 4 + [_row_spec(tm, 2 * D), _const_spec((1, D)), ANY],
        out_specs=[_row_spec(tm, D), _row_spec(tm, D), _row_spec(tm, 2 * D)] + [_row_spec(tm, D)] * 4
        + [_const_spec((1, D)), _const_spec((1, 2 * D))],
        out_shape=[jax.ShapeDtypeStruct((t_tok, D), F32), bf, jax.ShapeDtypeStruct((t_tok, 2 * D), BF16), bf, bf, bf, bf,
                   jax.ShapeDtypeStruct((1, D), F32), jax.ShapeDtypeStruct((1, 2 * D), F32)],
        scratch=[pltpu.VMEM((D, D), BF16)] * 3 + [DMA((3 * NDEV,))],
        vmem=VMEM_BIG, carries=carries)


def _sum_parts(parts_list):
    n = len(parts_list)
    _, r, c = parts_list[0].shape
    tc = 256

    def body(*refs):
        for p_ref, o_ref in zip(refs[:n], refs[n:]):
            acc = p_ref[0].astype(F32)
            for s in range(1, NDEV):
                acc = acc + p_ref[s].astype(F32)
            o_ref[...] = acc

    return pl.pallas_call(
        body, name=f"sum_parts_{r}", grid=(c // tc,),
        in_specs=[pl.BlockSpec((NDEV, r, tc), lambda i: (0, 0, i))] * n,
        out_specs=[pl.BlockSpec((r, tc), lambda i: (0, i))] * n,
        out_shape=[jax.ShapeDtypeStruct((r, c), F32)] * n,
        compiler_params=pltpu.CompilerParams(dimension_semantics=("arbitrary",), vmem_limit_bytes=48 * 2 ** 20),
    )(*parts_list)


def _adamw_math(w, g, m, v):
    m = ADAM_B1 * m + (1.0 - ADAM_B1) * g
    v = ADAM_B2 * v + (1.0 - ADAM_B2) * (g * g)
    m_hat = m / (1.0 - ADAM_B1 ** ADAM_STEP)
    v_hat = v / (1.0 - ADAM_B2 ** ADAM_STEP)
    delta = -ADAM_LR * (m_hat / (jnp.sqrt(v_hat) + ADAM_EPS) + ADAM_WD * w)
    return delta, m, v


def _sum_ready(parts_list, name, carries=()):
    n = len(parts_list)
    c = parts_list[0].shape[2]
    tc = 2 * LANES

    def body(*refs):
        for p_ref, o_ref in zip(refs[:n], refs[n:]):
            g = p_ref[0].astype(F32)
            for s in range(1, NDEV):
                g = g + p_ref[s].astype(F32)
            o_ref[...] = g

    return _call(
        body, name=name, grid=(c // tc,), ins=list(parts_list),
        in_specs=[pl.BlockSpec((NDEV, p.shape[1], tc), lambda i: (0, 0, i)) for p in parts_list],
        out_shape=[jax.ShapeDtypeStruct((p.shape[1], c), F32) for p in parts_list],
        out_specs=[pl.BlockSpec((p.shape[1], tc), lambda i: (0, i)) for p in parts_list],
        vmem=48 * 2 ** 20, carries=carries)


def _adamw_cols(items, name):
    n = len(items)
    c = items[0][1].shape[1]
    tc = LANES

    def body(*refs):
        for k in range(n):
            g_ref, w_ref, m_ref, v_ref = refs[4 * k:4 * k + 4]
            go_ref, d_ref, nm_ref, nv_ref = refs[4 * n + 4 * k:4 * n + 4 * k + 4]
            g = g_ref[...]
            d, m, v = _adamw_math(w_ref[...], g, m_ref[...], v_ref[...])
            go_ref[...] = g
            d_ref[...] = d
            nm_ref[...] = m
            nv_ref[...] = v

    ins, specs, out_shape = [], [], []
    for g, w, m, v in items:
        r = w.shape[0]
        ins += [g, w, m, v]
        specs += [pl.BlockSpec((r, tc), lambda i: (0, i))] * 4
        out_shape += [jax.ShapeDtypeStruct((r, c), F32)] * 4
    outs, _ = _call(body, name=name, grid=(c // tc,), ins=ins, in_specs=specs, out_shape=out_shape,
                    out_specs=specs, vmem=48 * 2 ** 20)
    return [tuple(outs[4 * k:4 * k + 4]) for k in range(n)]


def _adamw_body(n):
    def body(*refs):
        for k in range(n):
            g_ref, w_ref, m_ref, v_ref = refs[4 * k:4 * k + 4]
            d_ref, nm_ref, nv_ref = refs[4 * n + 3 * k:4 * n + 3 * k + 3]
            d, m, v = _adamw_math(w_ref[...], g_ref[...], m_ref[...], v_ref[...])
            d_ref[...] = d
            nm_ref[...] = m
            nv_ref[...] = v
    return body


def _adamw(items):
    n = len(items)
    r, c = items[0][0].shape
    tr = r if r * c <= 2 ** 18 else max(t for t in range(8, 65, 8) if r % t == 0)
    spec = pl.BlockSpec((tr, c), lambda i: (i, 0))
    outs = pl.pallas_call(
        _adamw_body(n), name=f"adamw_{r}x{c}", grid=(r // tr,),
        in_specs=[spec] * (4 * n), out_specs=[spec] * (3 * n),
        out_shape=[jax.ShapeDtypeStruct((r, c), F32)] * (3 * n),
        compiler_params=pltpu.CompilerParams(dimension_semantics=("arbitrary",), vmem_limit_bytes=40 * 2 ** 20),
    )(*[a for it in items for a in it])
    return [tuple(outs[3 * k:3 * k + 3]) for k in range(n)]


def _adamw_small(items):
    n = len(items)
    vm = pl.BlockSpec(memory_space=pltpu.VMEM)
    outs = pl.pallas_call(
        _adamw_body(n), name="adamw_small", in_specs=[vm] * (4 * n), out_specs=[vm] * (3 * n),
        out_shape=[jax.ShapeDtypeStruct(it[1].shape, F32) for it in items for _ in range(3)],
    )(*[a for it in items for a in it])
    return [tuple(outs[3 * k:3 * k + 3]) for k in range(n)]


def _pack_small(arrs):
    rows, offs = [], []
    total = 0
    for a in arrs:
        flat = a.reshape(-1).astype(F32)
        nr = -(-flat.shape[0] // LANES)
        flat = jnp.pad(flat, (0, nr * LANES - flat.shape[0]))
        rows.append(flat.reshape(nr, LANES))
        offs.append((total, nr))
        total += nr
    pad = -total % 8
    if pad:
        rows.append(jnp.zeros((pad, LANES), F32))
    return jnp.concatenate(rows, axis=0), offs


def _unpack_small(pack, offs, shapes):
    out = []
    for (o, nr), shp in zip(offs, shapes):
        size = int(np.prod(shp))
        out.append(pack[o:o + nr].reshape(-1)[:size].reshape(shp))
    return out


def _sum_small(gathered, rows):
    def body(p_ref, o_ref):
        acc = p_ref[pl.ds(0, rows), :]
        for s in range(1, NDEV):
            acc = acc + p_ref[pl.ds(s * rows, rows), :]
        o_ref[...] = acc

    return pl.pallas_call(
        body, name="sum_small", out_shape=jax.ShapeDtypeStruct((rows, LANES), F32),
        in_specs=[pl.BlockSpec(memory_space=pltpu.VMEM)], out_specs=pl.BlockSpec(memory_space=pltpu.VMEM))(gathered)


def _block_diag(w):
    w = w.reshape(D // LANES, 2, LRU_BLOCK, LRU_BLOCK)
    z = jnp.zeros((D // LANES, LRU_BLOCK, LRU_BLOCK), w.dtype)
    top = jnp.concatenate([w[:, 0], z], axis=2)
    bot = jnp.concatenate([z, w[:, 1]], axis=2)
    return jnp.concatenate([top, bot], axis=1)


def _block_diag_grad(dw):
    a = dw[:, :LRU_BLOCK, :LRU_BLOCK]
    b = dw[:, LRU_BLOCK:, LRU_BLOCK:]
    return jnp.stack([a, b], axis=1).reshape(D // LRU_BLOCK, LRU_BLOCK, LRU_BLOCK)


def kernel(x, ffn1_pre_g, ffn1_w1, ffn1_w3, ffn1_w2, ffn1_post_g, mix_pre_g, w_in, conv_w, conv_b, rg_a_w, rg_a_b, rg_x_w, rg_x_b, lru_lambda, w_lru_out, attn_sinks, rel_bias, w_attn_out, w_gate, b_gate, w_o, mix_post_g, ffn2_pre_g, ffn2_w1, ffn2_w3, ffn2_w2, ffn2_post_g, loss_target, m_ffn1_pre_g, m_ffn1_w1, m_ffn1_w3, m_ffn1_w2, m_ffn1_post_g, m_mix_pre_g, m_w_in, m_conv_w, m_conv_b, m_rg_a_w, m_rg_a_b, m_rg_x_w, m_rg_x_b, m_lru_lambda, m_w_lru_out, m_attn_sinks, m_rel_bias, m_w_attn_out, m_w_gate, m_b_gate, m_w_o, m_mix_post_g, m_ffn2_pre_g, m_ffn2_w1, m_ffn2_w3, m_ffn2_w2, m_ffn2_post_g, v_ffn1_pre_g, v_ffn1_w1, v_ffn1_w3, v_ffn1_w2, v_ffn1_post_g, v_mix_pre_g, v_w_in, v_conv_w, v_conv_b, v_rg_a_w, v_rg_a_b, v_rg_x_w, v_rg_x_b, v_lru_lambda, v_w_lru_out, v_attn_sinks, v_rel_bias, v_w_attn_out, v_w_gate, v_b_gate, v_w_o, v_mix_post_g, v_ffn2_pre_g, v_ffn2_w1, v_ffn2_w3, v_ffn2_w2, v_ffn2_post_g):
    names = ["ffn1_pre_g", "ffn1_w1", "ffn1_w3", "ffn1_w2", "ffn1_post_g", "mix_pre_g", "w_in", "conv_w", "conv_b",
             "rg_a_w", "rg_a_b", "rg_x_w", "rg_x_b", "lru_lambda", "w_lru_out", "attn_sinks", "rel_bias", "w_attn_out",
             "w_gate", "b_gate", "w_o", "mix_post_g", "ffn2_pre_g", "ffn2_w1", "ffn2_w3", "ffn2_w2", "ffn2_post_g"]
    ws = dict(zip(names, (ffn1_pre_g, ffn1_w1, ffn1_w3, ffn1_w2, ffn1_post_g, mix_pre_g, w_in, conv_w, conv_b, rg_a_w,
                          rg_a_b, rg_x_w, rg_x_b, lru_lambda, w_lru_out, attn_sinks, rel_bias, w_attn_out, w_gate,
                          b_gate, w_o, mix_post_g, ffn2_pre_g, ffn2_w1, ffn2_w3, ffn2_w2, ffn2_post_g)))
    ms = dict(zip(names, (m_ffn1_pre_g, m_ffn1_w1, m_ffn1_w3, m_ffn1_w2, m_ffn1_post_g, m_mix_pre_g, m_w_in, m_conv_w,
                          m_conv_b, m_rg_a_w, m_rg_a_b, m_rg_x_w, m_rg_x_b, m_lru_lambda, m_w_lru_out, m_attn_sinks,
                          m_rel_bias, m_w_attn_out, m_w_gate, m_b_gate, m_w_o, m_mix_post_g, m_ffn2_pre_g, m_ffn2_w1,
                          m_ffn2_w3, m_ffn2_w2, m_ffn2_post_g)))
    vs = dict(zip(names, (v_ffn1_pre_g, v_ffn1_w1, v_ffn1_w3, v_ffn1_w2, v_ffn1_post_g, v_mix_pre_g, v_w_in, v_conv_w,
                          v_conv_b, v_rg_a_w, v_rg_a_b, v_rg_x_w, v_rg_x_b, v_lru_lambda, v_w_lru_out, v_attn_sinks,
                          v_rel_bias, v_w_attn_out, v_w_gate, v_b_gate, v_w_o, v_mix_post_g, v_ffn2_pre_g, v_ffn2_w1,
                          v_ffn2_w3, v_ffn2_w2, v_ffn2_post_g)))
    me = 4 * lax.axis_index("x") + 2 * lax.axis_index("y") + lax.axis_index("c")
    vec = lambda n: ws[n].reshape(1, -1)

    def shard2d(name):
        if name == "conv":
            row = lax.bitcast_convert_type(conv_w.reshape(CONV_WIDTH, LANES), BF16).reshape(1, D)
            return jnp.concatenate([row, jnp.zeros((LOC["conv"][2] - 1, D), BF16)], axis=0)
        a = ws[name].reshape(ws[name].shape[-2], ws[name].shape[-1])
        return (a.T if name in COL_SHARDED else a).astype(BF16)

    packs = {g: jnp.concatenate([shard2d(n) for n, _ in members], axis=0) for g, members in GROUPS}

    wg = {}
    _, ((wg["ffn1"],),) = _call(None, name="allgather_ffn1", grid=(), ins=[], in_specs=[], out_shape=[], out_specs=[],
                                carries=[_AllGather(packs["ffn1"])])
    conv_rows = wg["ffn1"].reshape(NDEV, GROUP_ROWS["ffn1"], D)[:, LOC["conv"][1]]
    cw = jnp.transpose(lax.bitcast_convert_type(conv_rows.reshape(NDEV, CONV_WIDTH, LANES, 2), F32),
                       (1, 0, 2)).reshape(CONV_WIDTH, D)
    bmap = jnp.asarray(_bucket_map())
    bias = _bias_tile(rel_bias, bmap).reshape(N_KV, REP * QT, KW)
    sinks = attn_sinks.reshape(N_HEADS)
    wa_bd = _block_diag(rg_a_w.reshape(D // LRU_BLOCK, LRU_BLOCK, LRU_BLOCK)).astype(BF16)
    wx_bd = _block_diag(rg_x_w.reshape(D // LRU_BLOCK, LRU_BLOCK, LRU_BLOCK)).astype(BF16)
    lru_args = (cw, vec("conv_b"), wa_bd, wx_bd, vec("rg_a_b"), vec("rg_x_b"), vec("lru_lambda"))
    x2, tgt = x[0], loss_target[0]

    (h1, a1, b1, f1, nb1), ((wg["mixin"],),) = _ffn_fwd(
        x2, vec("ffn1_pre_g"), vec("ffn1_post_g"), wg, ("ffn1_w1", "ffn1_w3", "ffn1_w2"),
        carries=[_AllGather(packs["mixin"])])
    (q, kd, vd, xr, xg, gs, ub), ((wg["mixout"],),) = _mix_proj_fwd(
        h1, vec("mix_pre_g"), vec("b_gate"), wg, carries=[_AllGather(packs["mixout"])])
    (ya_in, hseq), ((wg["ffn2a"],),) = _lru_fwd(xr, xg, *lru_args, carries=[_AllGather(packs["ffn2a"])])
    (o, lse), ((wg["ffn2b"],),) = _attn_fwd(q, kd, vd, bias, sinks, carries=[_AllGather(packs["ffn2b"])])
    (h2, ya, yb, z), _ = _mix_out_fwd(ya_in, o, gs, h1, vec("mix_post_g"), wg)
    (_, a2, b2, f2, nb2, dy, loss_part), _ = _ffn_fwd(h2, vec("ffn2_pre_g"), vec("ffn2_post_g"), wg,
                                                     ("ffn2_w1", "ffn2_w3", "ffn2_w2"), target=tgt)

    gsm, parts = {}, {}
    rs = lambda *grads_: [_ReduceScatterSend(list(grads_))]
    ffr = FF // NDEV
    (dab, s_act, dfb, gsm["ffn2_post_g"]), _ = _ffn_bwd_a(dy, f2, a2, b2, vec("ffn2_post_g"), wg, "ffn2_w2")
    g_w2, _ = _dw(s_act, dfb, FF // 2, "dw_ffn2_w2")
    g_w13, ((parts["ffn2_w2"],),) = _dw(dab, nb2, D // 2, "dw_ffn2_w13", carries=rs(g_w2))
    (dh2, gsm["ffn2_pre_g"]), ((parts["ffn2_w1"],),) = _ffn_bwd_dx(
        dab, h2, dy, vec("ffn2_pre_g"), wg, "ffn2_w1", "ffn2_w3", carries=rs((g_w13, 0, ffr, ffr)))
    (dya_in, do, dgpre, dya, dyb, mg, dzb, gsm["mix_post_g"], gsm["b_gate"]), ((parts["ffn2_w3"],),) = _mix_out_bwd(
        dh2, z, ya, yb, gs, vec("mix_post_g"), wg, carries=rs((g_w13, FF, ffr, ffr)))
    g_wa, _ = _dw(ya_in, dya, D // 2, "dw_w_lru_out")
    g_wb, _ = _dw(o, dyb, D // 2, "dw_w_attn_out")
    g_wo, _ = _dw(mg, dzb, D // 2, "dw_w_o")
    (dqkv, ds_acc, dsink), (mixout_parts,) = _attn_bwd(
        q, kd, vd, o, do, lse, bias, sinks, carries=rs(g_wa, g_wb, g_wo))
    parts["w_lru_out"], parts["w_attn_out"], parts["w_o"] = mixout_parts
    (dxr, dxg, dvec, dwa, dwx), _ = _lru_bwd(dya_in, xr, xg, hseq, *lru_args)
    g_qkv, _ = _dw(dqkv, ub, D // 2, "dw_w_in_qkv")
    g_xr, _ = _dw(dxr, ub, D // 2, "dw_w_in_xr")
    g_xg, _ = _dw(dxg, ub, D // 2, "dw_w_in_xg")
    g_wgate, _ = _dw(dgpre, ub, D // 2, "dw_w_gate")
    g_win = jnp.concatenate(
        [g_qkv[:D]] + [g_qkv[c0 + gi * LANES:c0 + gi * LANES + HEAD_DIM] for c0 in (K0, V0) for gi in range(N_KV)]
        + [g_xr, g_xg], axis=0)
    (dh1, gsm["mix_pre_g"]), ((parts["w_in"],),) = _mix_proj_bwd(
        dqkv, dxr, dxg, dgpre, h1, dh2, vec("mix_pre_g"), wg, carries=rs(g_win))
    gsm["conv_w"] = dvec[0:CONV_WIDTH]
    gsm["conv_b"], gsm["rg_a_b"], gsm["rg_x_b"], gsm["lru_lambda"] = dvec[4], dvec[5], dvec[6], dvec[7]
    gsm["rg_a_w"] = _block_diag_grad(dwa)
    gsm["rg_x_w"] = _block_diag_grad(dwx)
    gsm["attn_sinks"] = dsink[0, :N_HEADS]
    gsm["rel_bias"] = _bias_grad(ds_acc.reshape(N_HEADS, QT, KW), bmap)[:, :N_HEADS]
    late = ("ffn1_post_g", "ffn1_pre_g")
    early = tuple(n for n in SMALL if n not in late)
    early_pack, early_offs = _pack_small([gsm[n] for n in early])
    (dab, s_act, dfb, gsm["ffn1_post_g"]), ((parts["w_gate"],), (early_all,)) = _ffn_bwd_a(
        dh1, f1, a1, b1, vec("ffn1_post_g"), wg, "ffn1_w2", carries=rs(g_wgate) + [_AllGather(early_pack)])
    g_w2, _ = _dw(s_act, dfb, FF // 2, "dw_ffn1_w2")
    g_w13, ((parts["ffn1_w2"],),) = _dw(dab, nb1, D // 2, "dw_ffn1_w13", carries=rs(g_w2))
    head = 64
    (grad_x, gsm["ffn1_pre_g"]), ((parts["ffn1_w1"], w3_head),) = _ffn_bwd_dx(
        dab, x2, dh1, vec("ffn1_pre_g"), wg, "ffn1_w1", "ffn1_w3",
        carries=rs((g_w13, 0, ffr, ffr), (g_w13, FF, ffr, head)))
    late_pack, late_offs = _pack_small([gsm[n] for n in late] + [loss_part])

    grads, delta, new_m, new_v = {}, {}, {}, {}
    t_form = [n for n in COL_SHARDED if ws[n].shape[-1] % LANES]
    view = {n: (lambda a: a.reshape(a.shape[-2], a.shape[-1]).T) if n in t_form
            else (lambda a: a.reshape(a.shape[-2], a.shape[-1])) for n in BIG}
    unview = {n: (lambda a: a.T) if n in t_form else (lambda a: a) for n in BIG}

    ready = [n for n in BIG if n not in ("ffn1_w3", "w_gate")]
    sums, ((w3_rest,), (late_all,)) = _sum_ready(
        [parts[n] for n in ready], "sum_ready", rs((g_w13, FF + head, ffr, ffr - head)) + [_AllGather(late_pack)])
    (sum_w3,), _ = _sum_ready([jnp.concatenate([w3_head, w3_rest], axis=1)], "sum_last")
    group = ready + ["ffn1_w3"]
    res = _adamw_cols([(g, view[n](ws[n]), view[n](ms[n]), view[n](vs[n])) for n, g in zip(group, sums + [sum_w3])],
                      "adamw_big")
    for n, quad in zip(group, res):
        grads[n], delta[n], new_m[n], new_v[n] = (unview[n](a) for a in quad)
    (g_gate_t,) = _sum_parts([parts["w_gate"]])
    grads["w_gate"] = g_gate_t.T
    ((delta["w_gate"], new_m["w_gate"], new_v["w_gate"]),) = _adamw(
        [(grads["w_gate"], view["w_gate"](ws["w_gate"]), view["w_gate"](ms["w_gate"]), view["w_gate"](vs["w_gate"]))])
    for group, gathered, offs in ((early, early_all, early_offs), (late, late_all, late_offs)):
        total = _sum_small(gathered, gathered.shape[0] // NDEV)
        shapes = [(CONV_WIDTH, D) if n == "conv_w" else ws[n].shape for n in group]
        if group is late:
            shapes = shapes + [(1, LANES)]
        unpacked = _unpack_small(total, offs, shapes)
        if group is late:
            loss = unpacked.pop()[0, 0]
        for n, g in zip(group, unpacked):
            grads[n] = g
    grads["conv_w"] = lax.dynamic_slice(grads["conv_w"], (0, me * LANES), (CONV_WIDTH, LANES)).reshape(conv_w.shape)

    flat2d = lambda a: a.reshape(-1, a.shape[-1])
    res = _adamw_small([(flat2d(grads[n].reshape(ws[n].shape)), flat2d(ws[n]), flat2d(ms[n]), flat2d(vs[n]))
                        for n in SMALL])
    for n, (d_, m_, v_) in zip(SMALL, res):
        delta[n], new_m[n], new_v[n] = d_, m_, v_

    outs = [loss, grad_x.reshape(x.shape)]
    for src in (grads, delta, new_m, new_v):
        outs += [src[n].reshape(ws[n].shape) for n in names]
    return tuple(outs)
```

```python
import functools
import math
import operator

import numpy as np
import jax
import jax.numpy as jnp
from jax import lax
from jax.experimental import pallas as pl
from jax.experimental.pallas import tpu as pltpu

F32, BF16 = jnp.float32, jnp.bfloat16

NDEV = 8
D = 1024
FF = 2816
N_HEADS, N_KV, HEAD_DIM = 16, 4, 64
CHUNK, WINDOW = 64, 128
N_BUCKETS, MAX_DISTANCE = 32, 128
LRU_BLOCK = 64
CONV_WIDTH = 4
LRU_C = 8.0
RMS_EPS = 1e-6
NEG_INF = -1e30
LANES = 128
QT = 128
KW = QT + WINDOW
LRU_ROWS = 512
IN_W = D + 2 * N_KV * HEAD_DIM + 2 * D
INP_W = D + 2 * N_KV * LANES + 2 * D
VMEM_BIG = 58 * 2 ** 20

ADAM_LR, ADAM_B1, ADAM_B2, ADAM_EPS, ADAM_WD, ADAM_STEP = 0.001, 0.9, 0.999, 1e-08, 0.01, 10

GROUPS = (("ffn1", (("ffn1_w1", FF // NDEV), ("ffn1_w3", FF // NDEV), ("ffn1_w2", FF // NDEV), ("conv", 16))),
          ("mixin", (("w_in", IN_W // NDEV), ("w_gate", 2 * D // NDEV))),
          ("mixout", (("w_lru_out", D // NDEV), ("w_attn_out", D // NDEV), ("w_o", D // NDEV))),
          ("ffn2a", (("ffn2_w1", FF // NDEV), ("ffn2_w3", FF // NDEV))),
          ("ffn2b", (("ffn2_w2", FF // NDEV),)))
LOC, GROUP_ROWS = {}, {}
for _g, _members in GROUPS:
    _o = 0
    for _n, _r in _members:
        LOC[_n] = (_g, _o, _r)
        _o += _r
    GROUP_ROWS[_g] = _o
BIG = tuple(n for _, members in GROUPS for n, _ in members if n != "conv")
COL_SHARDED = ("ffn1_w1", "ffn1_w3", "w_in", "w_gate", "ffn2_w1", "ffn2_w3")

SMALL = ("ffn1_pre_g", "ffn1_post_g", "mix_pre_g", "conv_w", "conv_b", "rg_a_w", "rg_a_b", "rg_x_w", "rg_x_b",
         "lru_lambda", "attn_sinks", "rel_bias", "b_gate", "mix_post_g", "ffn2_pre_g", "ffn2_post_g")

MESH = pl.DeviceIdType.MESH
ANY = pl.BlockSpec(memory_space=pl.ANY)
DMA = pltpu.SemaphoreType.DMA


def _nn(a, b):
    return lax.dot_general(a, b, (((1,), (0,)), ((), ())), preferred_element_type=F32)


def _nt(a, b):
    return lax.dot_general(a, b, (((1,), (1,)), ((), ())), preferred_element_type=F32)


def _tn(a, b):
    return lax.dot_general(a, b, (((0,), (0,)), ((), ())), preferred_element_type=F32)


def _rms_fwd(x, g):
    r = lax.rsqrt(jnp.mean(x * x, axis=-1, keepdims=True) + RMS_EPS)
    xh = x * r
    return xh * g, xh, r


def _rms_bwd(dn, xh, r, g):
    dxh = dn * g
    dx = r * (dxh - xh * jnp.mean(dxh * xh, axis=-1, keepdims=True))
    return dx, jnp.sum(dn * xh, axis=0, keepdims=True)


def _row_spec(tm, c):
    return pl.BlockSpec((tm, c), lambda i: (i, 0))


def _const_spec(shape):
    nd = len(shape)
    return pl.BlockSpec(shape, lambda i: (0,) * nd)


class _AllGather:
    def __init__(self, shard):
        self.m, n = shard.shape
        self.ins = [shard]
        self.out_shape = [jax.ShapeDtypeStruct((NDEV * self.m, n), shard.dtype)]
        self.scratch = [DMA((7,)), DMA((7,)), DMA]

    def _copies(self, ins, outs, scr, all_of_them):
        x_ref, out_ref = ins[0], outs[0]
        send_sems, recv_sems, local_sem = scr
        x, y, c = lax.axis_index("x"), lax.axis_index("y"), lax.axis_index("c")
        me, sibling = (x, y, c), (x, y, 1 - c)
        chips = [(1 - x, y), (x, 1 - y), (1 - x, 1 - y)]
        m = self.m

        def rows(px, py, pc):
            return out_ref.at[pl.ds((4 * px + 2 * py + pc) * m, m), :]

        def copy(k, block, to, src=None):
            return pltpu.make_async_remote_copy(
                src_ref=rows(*block) if src is None else src, dst_ref=rows(*block),
                send_sem=send_sems.at[k], recv_sem=recv_sems.at[k], device_id=to, device_id_type=MESH)

        mine = pltpu.make_async_copy(x_ref, rows(*me), local_sem)
        first = [copy(0, me, sibling, src=x_ref)] + [copy(1 + j, me, (*chip, c), src=x_ref)
                                                     for j, chip in enumerate(chips)]
        if not all_of_them:
            return mine, first
        passed = [copy(4 + j, (*chip, c), sibling) for j, chip in enumerate(chips)]
        landed = [copy(1 + j, (*chip, c), me) for j, chip in enumerate(chips)]
        from_sibling = [copy(0, sibling, me)] + [copy(4 + j, (*chip, 1 - c), me) for j, chip in enumerate(chips)]
        return mine, first, passed, landed, from_sibling

    def start(self, ins, outs, scr):
        mine, first = self._copies(ins, outs, scr, False)
        mine.start()
        for cp in first:
            cp.start()

    def finish(self, ins, outs, scr):
        mine, first, passed, landed, from_sibling = self._copies(ins, outs, scr, True)
        for cp_in, cp_on in zip(landed, passed):
            cp_in.wait_recv()
            cp_on.start()
        for cp in from_sibling:
            cp.wait_recv()
        for cp in first + passed:
            cp.wait_send()
        mine.wait()


class _ReduceScatterSend:
    def __init__(self, grads):
        grads = [g if isinstance(g, tuple) else (g, 0, g.shape[0] // NDEV, g.shape[0] // NDEV) for g in grads]
        self.nw = len(grads)
        self.base = [b for _, b, _, _ in grads]
        self.stride = [s for _, _, s, _ in grads]
        self.rows = [r for _, _, _, r in grads]
        self.ins = [g for g, _, _, _ in grads]
        self.out_shape = [jax.ShapeDtypeStruct((NDEV, r, g.shape[1]), g.dtype) for g, _, _, r in grads]
        self.scratch = [DMA((self.nw, NDEV - 1)), DMA((self.nw, NDEV - 1)), DMA((self.nw,))]

    def _copies(self, g_refs, r_refs, scr, want):
        send_sems, recv_sems, local_sems = scr
        x, y, c = lax.axis_index("x"), lax.axis_index("y"), lax.axis_index("c")
        me = 4 * x + 2 * y + c
        rows, base, stride = self.rows, self.base, self.stride
        out = []
        if want == "local":
            for w in range(self.nw):
                out.append(pltpu.make_async_copy(g_refs[w].at[pl.ds(base[w] + me * stride[w], rows[w])],
                                                 r_refs[w].at[me], local_sems.at[w]))
            return out
        for k in range(1, NDEV):
            px, py, pc = x ^ (k >> 2), y ^ ((k >> 1) & 1), c ^ (k & 1)
            peer = 4 * px + 2 * py + pc
            for w in range(self.nw):
                sems = dict(send_sem=send_sems.at[w, k - 1], recv_sem=recv_sems.at[w, k - 1],
                            device_id=(px, py, pc), device_id_type=MESH)
                if want == "send":
                    out.append(pltpu.make_async_remote_copy(
                        src_ref=g_refs[w].at[pl.ds(base[w] + peer * stride[w], rows[w])], dst_ref=r_refs[w].at[me],
                        **sems))
                else:
                    out.append(pltpu.make_async_remote_copy(
                        src_ref=g_refs[w].at[pl.ds(0, rows[w])], dst_ref=r_refs[w].at[peer], **sems))
        return out

    def start(self, ins, outs, scr):
        for cp in self._copies(ins, outs, scr, "local") + self._copies(ins, outs, scr, "send"):
            cp.start()

    def finish(self, ins, outs, scr):
        for cp in self._copies(ins, outs, scr, "recv"):
            cp.wait_recv()
        for cp in self._copies(ins, outs, scr, "send"):
            cp.wait_send()
        for cp in self._copies(ins, outs, scr, "local"):
            cp.wait()


def _call(body, *, name, grid, ins, in_specs, out_shape, out_specs, scratch=(), vmem=None, carries=()):
    n_in, n_out, n_scr = len(ins), len(out_shape), len(scratch)
    ng = len(grid)

    def split(refs):
        pos = [0]

        def take(k):
            part = refs[pos[0]:pos[0] + k]
            pos[0] += k
            return part

        i_refs = take(n_in)
        c_in = [take(len(c.ins)) for c in carries]
        o_refs = take(n_out)
        c_out = [take(len(c.out_shape)) for c in carries]
        s_refs = take(n_scr)
        c_scr = [take(len(c.scratch)) for c in carries]
        return i_refs, o_refs, s_refs, list(zip(carries, c_in, c_out, c_scr))

    def full(*refs):
        i_refs, o_refs, s_refs, cparts = split(refs)
        if ng == 0:
            for c, a, b, s in cparts:
                c.start(a, b, s)
            for c, a, b, s in cparts:
                c.finish(a, b, s)
            return
        ids = [pl.program_id(a) for a in range(ng)]
        if cparts:
            @pl.when(functools.reduce(operator.and_, [i == 0 for i in ids]))
            def _():
                for c, a, b, s in cparts:
                    c.start(a, b, s)

        body(*i_refs, *o_refs, *s_refs)
        if cparts:
            @pl.when(functools.reduce(operator.and_, [i == g - 1 for i, g in zip(ids, grid)]))
            def _():
                for c, a, b, s in cparts:
                    c.finish(a, b, s)

    all_ins = list(ins) + [a for c in carries for a in c.ins]
    all_in_specs = list(in_specs) + [ANY for c in carries for _ in c.ins]
    all_out_shape = list(out_shape) + [s for c in carries for s in c.out_shape]
    all_out_specs = list(out_specs) + [ANY for c in carries for _ in c.out_shape]
    all_scratch = list(scratch) + [s for c in carries for s in c.scratch]
    kwargs = dict(grid=grid) if ng else {}
    outs = pl.pallas_call(
        full, name=name, in_specs=all_in_specs, out_specs=all_out_specs, out_shape=all_out_shape,
        scratch_shapes=all_scratch,
        compiler_params=pltpu.CompilerParams(dimension_semantics=("arbitrary",) * ng if ng else None,
                                             vmem_limit_bytes=vmem),
        **kwargs)(*all_ins)
    outs = list(outs)
    res, pos = outs[:n_out], n_out
    carried = []
    for c in carries:
        carried.append(outs[pos:pos + len(c.out_shape)])
        pos += len(c.out_shape)
    return res, carried


def _groups_of(names):
    out = []
    for n in names:
        if LOC[n][0] not in out:
            out.append(LOC[n][0])
    return out


def _weight_pieces(name):
    g, off, rows = LOC[name]
    return [(d * GROUP_ROWS[g] + off, d * rows, rows) for d in range(NDEV)]


def _win_pieces():
    kv = N_KV * HEAD_DIM
    pieces = [(0, 0, D)]
    for g0, d0 in ((D, D), (D + kv, D + N_KV * LANES)):
        for g in range(N_KV):
            for half in range(2):
                pieces.append((g0 + g * HEAD_DIM, d0 + g * LANES + half * HEAD_DIM, HEAD_DIM))
    pieces.append((D + 2 * kv, D + 2 * N_KV * LANES, D))
    pieces.append((D + 2 * kv + D, D + 2 * N_KV * LANES + D, D))
    grp, off, rows = LOC["w_in"]
    out = []
    for g0, d0, n in pieces:
        while n > 0:
            dev, loc = divmod(g0, rows)
            m = min(n, rows - loc)
            out.append((dev * GROUP_ROWS[grp] + off + loc, d0, m))
            g0, d0, n = g0 + m, d0 + m, n - m
    return out


def _start_loads(src_ref, dst_ref, pieces, sems, base):
    cps = []
    for j, (s, d, n) in enumerate(pieces):
        cp = pltpu.make_async_copy(src_ref.at[pl.ds(s, n)], dst_ref.at[pl.ds(d, n)], sems.at[base + j])
        cp.start()
        cps.append(cp)
    return cps


def _load_weights(wrefs, targets, sems):
    cps, base = [], 0
    for name, dst in targets:
        pieces = _win_pieces() if name == "w_in" else _weight_pieces(name)
        cps += _start_loads(wrefs[LOC[name][0]], dst, pieces, sems, base)
        base += len(pieces)
    for cp in cps:
        cp.wait()


def _n_pieces(names):
    return sum(len(_win_pieces()) if n == "w_in" else NDEV for n in names)


def _dw(lhs, rhs, chunk, name, carries=()):
    nq = 4
    if lhs.ndim == 3:
        nch, t_tok, chunk = lhs.shape
        c = nch * chunk
        tq = t_tok // nq
        lhs_specs = [pl.BlockSpec((None, tq, chunk), lambda i, k=k: (i, k, 0)) for k in range(nq)]
    else:
        t_tok, c = lhs.shape
        tq = t_tok // nq
        lhs_specs = [pl.BlockSpec((tq, chunk), lambda i, k=k: (k, i)) for k in range(nq)]

    def body(*refs):
        lhs_refs, (rhs_ref, out_ref, rhs_s, sems) = refs[:nq], refs[nq:]
        cps = [pltpu.make_async_copy(rhs_ref.at[pl.ds(k * tq, tq)], rhs_s.at[pl.ds(k * tq, tq)], sems.at[k])
               for k in range(nq)]
        first = pl.program_id(0) == 0

        @pl.when(first)
        def _():
            for cp in cps:
                cp.start()

        for o in range(0, chunk, D // 2):
            n = min(D // 2, chunk - o)
            acc = None
            for k in range(nq):
                if o == 0:
                    @pl.when(first)
                    def _():
                        cps[k].wait()

                part = _tn(lhs_refs[k][:, pl.ds(o, n)], rhs_s[pl.ds(k * tq, tq), :])
                acc = part if acc is None else acc + part
            out_ref[pl.ds(o, n), :] = acc.astype(BF16)

    (out,), carried = _call(
        body, name=name, grid=(c // chunk,), ins=[lhs] * nq + [rhs],
        in_specs=lhs_specs + [ANY],
        out_specs=[pl.BlockSpec((chunk, D), lambda i: (i, 0))],
        out_shape=[jax.ShapeDtypeStruct((c, D), BF16)],
        scratch=[pltpu.VMEM((t_tok, D), BF16), DMA((nq,))], vmem=VMEM_BIG, carries=carries)
    return out, carried


def _silu_parts(a):
    sig = jax.nn.sigmoid(a)
    return sig, a * sig


FC = 256


def _ffn_fwd(h, gpre, gpost, wg, names, target=None, tm=512, carries=()):
    t_tok = h.shape[0]
    nt = t_tok // tm
    with_loss = target is not None
    groups = _groups_of(names)

    def body(*refs):
        refs = list(refs)
        h_ref, gpre_ref, gpost_ref = refs[:3]
        del refs[:3]
        tgt_ref = refs.pop(0) if with_loss else None
        wrefs = dict(zip(groups, refs[:len(groups)]))
        del refs[:len(groups)]
        hout_ref, a_ref, b_ref, f_ref, nb_ref = refs[:5]
        del refs[:5]
        if with_loss:
            dy_ref, loss_ref = refs[:2]
            del refs[:2]
        w1_s, w3_s, w2_s, sems = refs
        i = pl.program_id(0)

        @pl.when(i == 0)
        def _():
            _load_weights(wrefs, list(zip(names, (w1_s, w3_s, w2_s))), sems)
            if with_loss:
                loss_ref[...] = jnp.zeros_like(loss_ref)

        x = h_ref[...]
        n, _, _ = _rms_fwd(x, gpre_ref[...])
        nb = n.astype(BF16)
        nb_ref[...] = nb
        f = jnp.zeros((tm, D), F32)
        for c0 in range(0, FF, FC):
            a = _nt(nb, w1_s[pl.ds(c0, FC), :])
            b = _nt(nb, w3_s[pl.ds(c0, FC), :])
            _, sl = _silu_parts(a)
            a_ref[:, pl.ds(c0, FC)] = a.astype(BF16)
            b_ref[:, pl.ds(c0, FC)] = b.astype(BF16)
            f = f + _nn((sl * b).astype(BF16), w2_s[pl.ds(c0, FC), :])
        f_ref[...] = f
        fn, _, _ = _rms_fwd(f, gpost_ref[...])
        y = x + 0.5 * fn
        hout_ref[...] = y
        if with_loss:
            err = y - tgt_ref[...]
            dy_ref[...] = err * (1.0 / D)
            loss_ref[...] += jnp.sum(jnp.sum(err * err, axis=-1, keepdims=True), axis=0, keepdims=True) * (0.5 / D)

    ins = [h, gpre, gpost] + ([target] if with_loss else []) + [wg[g] for g in groups]
    in_specs = [_row_spec(tm, D), _const_spec((1, D)), _const_spec((1, D))]
    in_specs += ([_row_spec(tm, D)] if with_loss else []) + [ANY] * len(groups)
    out_shape = [jax.ShapeDtypeStruct((t_tok, D), F32), jax.ShapeDtypeStruct((t_tok, FF), BF16),
                 jax.ShapeDtypeStruct((t_tok, FF), BF16), jax.ShapeDtypeStruct((t_tok, D), F32),
                 jax.ShapeDtypeStruct((t_tok, D), BF16)]
    out_specs = [_row_spec(tm, D), _row_spec(tm, FF), _row_spec(tm, FF), _row_spec(tm, D), _row_spec(tm, D)]
    if with_loss:
        out_shape += [jax.ShapeDtypeStruct((t_tok, D), F32), jax.ShapeDtypeStruct((1, LANES), F32)]
        out_specs += [_row_spec(tm, D), _const_spec((1, LANES))]
    return _call(body, name="ffn_fwd_" + names[0][:4], grid=(nt,), ins=ins, in_specs=in_specs,
                 out_shape=out_shape, out_specs=out_specs,
                 scratch=[pltpu.VMEM((FF, D), BF16)] * 3 + [DMA((3 * NDEV,))], vmem=VMEM_BIG, carries=carries)


FH = FF // 2
HALF_PIECES = ((0, 256), (256, 256), (512, 256), (768, 256), (1024, 256), (1280, 128))


def _ffn_bwd_a(dh, f, a, b, gpost, wg, name_w2, tm=512, carries=()):
    t_tok = dh.shape[0]
    nt = t_tok // tm
    groups = _groups_of([name_w2])

    def body(dh_ref, f_ref, a_ref, b_ref, gpost_ref, wg_ref, dab_ref, s_ref, df_ref, dgp_ref, w2_s, sems):
        i = pl.program_id(0)

        @pl.when(i == 0)
        def _():
            _load_weights({groups[0]: wg_ref}, [(name_w2, w2_s)], sems)
            dgp_ref[...] = jnp.zeros_like(dgp_ref)

        fv = f_ref[...]
        _, fh, r = _rms_fwd(fv, gpost_ref[...])
        df, dg = _rms_bwd(0.5 * dh_ref[...], fh, r, gpost_ref[...])
        dgp_ref[...] += dg
        dfb = df.astype(BF16)
        df_ref[...] = dfb
        for half in range(2):
            for o, n in HALF_PIECES:
                c0 = half * FH + o
                ds = _nt(dfb, w2_s[pl.ds(c0, n), :])
                av = a_ref[:, pl.ds(c0, n)].astype(F32)
                bv = b_ref[:, pl.ds(c0, n)].astype(F32)
                sig, sl = _silu_parts(av)
                dab_ref[half, :, pl.ds(o, n)] = (ds * bv * (sig * (1.0 + av * (1.0 - sig)))).astype(BF16)
                dab_ref[2 + half, :, pl.ds(o, n)] = (ds * sl).astype(BF16)
                s_ref[half, :, pl.ds(o, n)] = (sl * bv).astype(BF16)

    return _call(
        body, name="ffn_bwd_a_" + name_w2[:4], grid=(nt,), ins=[dh, f, a, b, gpost, wg[groups[0]]],
        in_specs=[_row_spec(tm, D), _row_spec(tm, D), _row_spec(tm, FF), _row_spec(tm, FF), _const_spec((1, D)), ANY],
        out_specs=[pl.BlockSpec((4, tm, FH), lambda i: (0, i, 0)), pl.BlockSpec((2, tm, FH), lambda i: (0, i, 0)),
                   _row_spec(tm, D), _const_spec((1, D))],
        out_shape=[jax.ShapeDtypeStruct((4, t_tok, FH), BF16), jax.ShapeDtypeStruct((2, t_tok, FH), BF16),
                   jax.ShapeDtypeStruct((t_tok, D), BF16), jax.ShapeDtypeStruct((1, D), F32)],
        scratch=[pltpu.VMEM((FF, D), BF16), DMA((NDEV,))],
        vmem=VMEM_BIG, carries=carries)


def _ffn_bwd_dx(dab, h, dh, gpre, wg, name_w1, name_w3, tm=512, carries=()):
    t_tok = dh.shape[0]
    nt = t_tok // tm
    groups = _groups_of([name_w1, name_w3])

    def body(*refs):
        dab_refs, (h_ref, dh_ref, gpre_ref) = refs[:4], refs[4:7]
        wrefs = dict(zip(groups, refs[7:7 + len(groups)]))
        dhin_ref, dgp_ref, w13_s, sems = refs[7 + len(groups):]
        i = pl.program_id(0)

        @pl.when(i == 0)
        def _():
            _load_weights(wrefs, [(name_w1, w13_s.at[pl.ds(0, FF)]), (name_w3, w13_s.at[pl.ds(FF, FF)])], sems)
            dgp_ref[...] = jnp.zeros_like(dgp_ref)

        g = gpre_ref[...]
        _, xh, r = _rms_fwd(h_ref[...], g)
        dn = _nn(dab_refs[0][...], w13_s[pl.ds(0, FH), :])
        for k in range(1, 4):
            dn = dn + _nn(dab_refs[k][...], w13_s[pl.ds(k * FH, FH), :])
        dx, dg = _rms_bwd(dn, xh, r, g)
        dgp_ref[...] += dg
        dhin_ref[...] = dh_ref[...] + dx

    return _call(
        body, name="ffn_bwd_dx_" + name_w1[:4], grid=(nt,), ins=[dab] * 4 + [h, dh, gpre] + [wg[g] for g in groups],
        in_specs=[pl.BlockSpec((None, tm, FH), lambda i, k=k: (k, i, 0)) for k in range(4)]
        + [_row_spec(tm, D), _row_spec(tm, D), _const_spec((1, D))] + [ANY] * len(groups),
        out_specs=[_row_spec(tm, D), _const_spec((1, D))],
        out_shape=[jax.ShapeDtypeStruct((t_tok, D), F32), jax.ShapeDtypeStruct((1, D), F32)],
        scratch=[pltpu.VMEM((2 * FF, D), BF16), DMA((2 * NDEV,))],
        vmem=VMEM_BIG, carries=carries)


Q0, K0, V0, XR0, XG0 = 0, D, D + N_KV * LANES, D + 2 * N_KV * LANES, 2 * D + 2 * N_KV * LANES


def _mix_proj_fwd(h, g, bgate, wg, tm=512, carries=()):
    t_tok = h.shape[0]
    nt = t_tok // tm
    names = ("w_in", "w_gate")
    groups = _groups_of(names)

    def body(h_ref, g_ref, bg_ref, wg_ref, q_ref, k_ref, v_ref, xr_ref, xg_ref, gs_ref, ub_ref, win_s, wgt_s, sems):
        i = pl.program_id(0)

        @pl.when(i == 0)
        def _():
            _load_weights({groups[0]: wg_ref}, [("w_in", win_s), ("w_gate", wgt_s)], sems)

        n, _, _ = _rms_fwd(h_ref[...], g_ref[...])
        nb = n.astype(BF16)
        ub_ref[...] = nb
        q_ref[...] = _nt(nb, win_s[pl.ds(Q0, D), :]).astype(BF16)
        k_ref[...] = _nt(nb, win_s[pl.ds(K0, N_KV * LANES), :]).astype(BF16)
        v_ref[...] = _nt(nb, win_s[pl.ds(V0, N_KV * LANES), :]).astype(BF16)
        xr_ref[...] = _nt(nb, win_s[pl.ds(XR0, D), :])
        xg_ref[...] = _nt(nb, win_s[pl.ds(XG0, D), :])
        gs_ref[...] = jax.nn.sigmoid(_nt(nb, wgt_s[...]) + bg_ref[...]).astype(BF16)

    kvw = N_KV * LANES
    return _call(
        body, name="mix_proj_fwd", grid=(nt,), ins=[h, g, bgate, wg[groups[0]]],
        in_specs=[_row_spec(tm, D), _const_spec((1, D)), _const_spec((1, 2 * D)), ANY],
        out_specs=[_row_spec(tm, D), _row_spec(tm, kvw), _row_spec(tm, kvw), _row_spec(tm, D), _row_spec(tm, D),
                   _row_spec(tm, 2 * D), _row_spec(tm, D)],
        out_shape=[jax.ShapeDtypeStruct((t_tok, D), BF16), jax.ShapeDtypeStruct((t_tok, kvw), BF16),
                   jax.ShapeDtypeStruct((t_tok, kvw), BF16), jax.ShapeDtypeStruct((t_tok, D), F32),
                   jax.ShapeDtypeStruct((t_tok, D), F32), jax.ShapeDtypeStruct((t_tok, 2 * D), BF16),
                   jax.ShapeDtypeStruct((t_tok, D), BF16)],
        scratch=[pltpu.VMEM((INP_W, D), BF16), pltpu.VMEM((2 * D, D), BF16), DMA((_n_pieces(names),))],
        vmem=VMEM_BIG, carries=carries)


def _mix_proj_bwd(dqkv, dxr, dxg, dgpre, h, dh, g, wg, tm=512, carries=()):
    t_tok = h.shape[0]
    nt = t_tok // tm
    names = ("w_in", "w_gate")
    groups = _groups_of(names)

    def body(dqkv_ref, dxr_ref, dxg_ref, dgp_ref, h_ref, dh_ref, g_ref, wg_ref, dhin_ref, dg_ref, win_s, wgt_s, sems):
        i = pl.program_id(0)

        @pl.when(i == 0)
        def _():
            _load_weights({groups[0]: wg_ref}, [("w_in", win_s), ("w_gate", wgt_s)], sems)
            dg_ref[...] = jnp.zeros_like(dg_ref)

        gv = g_ref[...]
        _, xh, r = _rms_fwd(h_ref[...], gv)
        du = _nn(dgp_ref[...], wgt_s[...])
        du = du + _nn(dqkv_ref[...], win_s[pl.ds(Q0, XR0), :])
        du = du + _nn(dxr_ref[...], win_s[pl.ds(XR0, D), :])
        du = du + _nn(dxg_ref[...], win_s[pl.ds(XG0, D), :])
        dx, dg = _rms_bwd(du, xh, r, gv)
        dg_ref[...] += dg
        dhin_ref[...] = dh_ref[...] + dx

    return _call(
        body, name="mix_proj_bwd", grid=(nt,), ins=[dqkv, dxr, dxg, dgpre, h, dh, g, wg[groups[0]]],
        in_specs=[_row_spec(tm, XR0), _row_spec(tm, D), _row_spec(tm, D), _row_spec(tm, 2 * D), _row_spec(tm, D),
                  _row_spec(tm, D), _const_spec((1, D)), ANY],
        out_specs=[_row_spec(tm, D), _const_spec((1, D))],
        out_shape=[jax.ShapeDtypeStruct((t_tok, D), F32), jax.ShapeDtypeStruct((1, D), F32)],
        scratch=[pltpu.VMEM((INP_W, D), BF16), pltpu.VMEM((2 * D, D), BF16), DMA((_n_pieces(names),))],
        vmem=VMEM_BIG, carries=carries)


def _shift_down(x, d, fill):
    row = lax.broadcasted_iota(jnp.int32, x.shape, 0)
    return jnp.where(row >= d, pltpu.roll(x, d, 0), fill)


def _shift_up(x, d, fill):
    rows = x.shape[0]
    row = lax.broadcasted_iota(jnp.int32, x.shape, 0)
    return jnp.where(row < rows - d, pltpu.roll(x, rows - d, 0), fill)


def _scan_rows(a, b, reverse):
    rows = a.shape[0]
    d = 1
    while d < rows:
        if d < 8:
            shift = _shift_up if reverse else _shift_down
            b = a * shift(b, d, 0.0) + b
            a = a * shift(a, d, 1.0)
        elif reverse:
            b = jnp.concatenate([a[:rows - d] * b[d:] + b[:rows - d], b[rows - d:]], axis=0)
            a = jnp.concatenate([a[:rows - d] * a[d:], a[rows - d:]], axis=0)
        else:
            b = jnp.concatenate([b[:d], a[d:] * b[:rows - d] + b[d:]], axis=0)
            a = jnp.concatenate([a[:d], a[d:] * a[:rows - d]], axis=0)
        d *= 2
    return a, b


def _softplus(x):
    return jnp.maximum(x, 0.0) + jnp.log(1.0 + jnp.exp(-jnp.abs(x)))


_GELU_C = math.sqrt(2.0 / math.pi)


def _gelu_parts(x):
    th = jnp.tanh(_GELU_C * (x + 0.044715 * x * x * x))
    val = 0.5 * x * (1.0 + th)
    grad = 0.5 * (1.0 + th) + 0.5 * x * (1.0 - th * th) * _GELU_C * (1.0 + 3.0 * 0.044715 * x * x)
    return val, grad


def _lru_pre(x, halo, cw_ref, cb_ref, wa_ref, wx_ref, ba_ref, bx_ref, lam_ref):
    ext = jnp.concatenate([halo, x], axis=0)
    shifted = [x] + [pltpu.roll(ext, k, 0)[8:] for k in (1, 2, 3)]
    xc = cb_ref[...] + cw_ref[pl.ds(CONV_WIDTH - 1, 1), :] * x
    for k in (1, 2, 3):
        xc = xc + cw_ref[pl.ds(CONV_WIDTH - 1 - k, 1), :] * shifted[k]
    xcb = xc.astype(BF16)
    r = jax.nn.sigmoid(_nn(xcb, wa_ref[...]) + ba_ref[...])
    ig = jax.nn.sigmoid(_nn(xcb, wx_ref[...]) + bx_ref[...])
    sp = _softplus(-lam_ref[...])
    log_a = -LRU_C * r * sp
    a = jnp.exp(log_a)
    th = jnp.tanh(log_a)
    mult = jnp.sqrt(-2.0 * th / (1.0 - th))
    return shifted, xc, xcb, r, ig, sp, a, mult


def _lru_specs(nt, reverse):
    def tt(t):
        return nt - 1 - t if reverse else t
    tile = pl.BlockSpec((LRU_ROWS, LANES), lambda cb, t: (tt(t), cb))
    halo = pl.BlockSpec((8, LANES), lambda cb, t: (jnp.maximum(tt(t) * (LRU_ROWS // 8) - 1, 0), cb))
    vec = pl.BlockSpec((1, LANES), lambda cb, t: (0, cb))
    cw = pl.BlockSpec((CONV_WIDTH, LANES), lambda cb, t: (0, cb))
    mat = pl.BlockSpec((None, LANES, LANES), lambda cb, t: (cb, 0, 0))
    return tile, halo, vec, cw, mat


def _lru_fwd(xr, xg, cw, cb, wa, wx, ba, bx, lam, carries=()):
    t_tok = xr.shape[0]
    nt = t_tok // LRU_ROWS
    rows = LRU_ROWS

    def body(xr_ref, xg_ref, cw_ref, cb_ref, wa_ref, wx_ref, ba_ref, bx_ref, lam_ref, y_ref, h_ref, tail_s, hc_s):
        t = pl.program_id(1)

        @pl.when(t == 0)
        def _():
            tail_s[...] = jnp.zeros_like(tail_s)
            hc_s[...] = jnp.zeros_like(hc_s)

        x = xr_ref[...]
        _, xc, _, _, ig, _, a, mult = _lru_pre(x, tail_s[...], cw_ref, cb_ref, wa_ref, wx_ref, ba_ref, bx_ref, lam_ref)
        tail_s[...] = xr_ref[pl.ds(rows - 8, 8), :]
        acc_a, acc_b = _scan_rows(a, mult * (ig * xc), False)
        hv = acc_b + acc_a * hc_s[...]
        h_ref[...] = hv
        hc_s[...] = h_ref[pl.ds(rows - 1, 1), :]
        gl, _ = _gelu_parts(xg_ref[...])
        y_ref[...] = (hv * gl).astype(BF16)

    tile, _, vec, cws, mat = _lru_specs(nt, False)
    return _call(
        body, name="lru_fwd", grid=(D // LANES, nt), ins=[xr, xg, cw, cb, wa, wx, ba, bx, lam],
        in_specs=[tile, tile, cws, vec, mat, mat, vec, vec, vec],
        out_specs=[tile, tile],
        out_shape=[jax.ShapeDtypeStruct((t_tok, D), BF16), jax.ShapeDtypeStruct((t_tok, D), F32)],
        scratch=[pltpu.VMEM((8, LANES), F32), pltpu.VMEM((1, LANES), F32)], carries=carries)


def _lru_bwd(dy, xr, xg, hseq, cw, cb, wa, wx, ba, bx, lam, carries=()):
    t_tok = xr.shape[0]
    nt = t_tok // LRU_ROWS
    rows = LRU_ROWS

    def body(dy_ref, xr_ref, xrh_ref, xg_ref, h_ref, hh_ref, cw_ref, cb_ref, wa_ref, wx_ref, ba_ref, bx_ref, lam_ref,
             dxr_ref, dxg_ref, dvec_ref, dwa_ref, dwx_ref, gcar_s, acar_s, head_s, tmp_s):
        t = pl.program_id(1)
        first_tile = t == nt - 1

        @pl.when(t == 0)
        def _():
            gcar_s[...] = jnp.zeros_like(gcar_s)
            acar_s[...] = jnp.zeros_like(acar_s)
            head_s[...] = jnp.zeros_like(head_s)
            dvec_ref[...] = jnp.zeros_like(dvec_ref)
            dwa_ref[...] = jnp.zeros_like(dwa_ref)
            dwx_ref[...] = jnp.zeros_like(dwx_ref)

        x = xr_ref[...]
        halo = jnp.where(first_tile, 0.0, xrh_ref[...])
        shifted, xc, xcb, r, ig, sp, a, mult = _lru_pre(x, halo, cw_ref, cb_ref, wa_ref, wx_ref, ba_ref, bx_ref, lam_ref)
        hv = h_ref[...]
        dyv = dy_ref[...]
        gl, glg = _gelu_parts(xg_ref[...])
        dxg_ref[...] = (dyv * hv * glg).astype(BF16)
        acc_a, acc_b = _scan_rows(_shift_up(a, 1, acar_s[...]), dyv * gl, True)
        g = acc_b + acc_a * gcar_s[...]
        hhalo = jnp.where(first_tile, 0.0, hh_ref[...])
        hprev = pltpu.roll(jnp.concatenate([hhalo, hv], axis=0), 1, 0)[8:]
        dmult = g * ig * xc
        dlog_a = a * (g * hprev) - dmult * a * a / mult
        dig = g * mult * xc
        dxc = g * mult * ig
        dzr = (dlog_a * (-LRU_C * sp)) * r * (1.0 - r)
        dzx = dig * ig * (1.0 - ig)
        dzrb, dzxb = dzr.astype(BF16), dzx.astype(BF16)
        dxc = dxc + _nt(dzrb, wa_ref[...]) + _nt(dzxb, wx_ref[...])
        dwa_ref[...] += _tn(xcb, dzrb)
        dwx_ref[...] += _tn(xcb, dzxb)
        dsp = jnp.sum(dlog_a * (-LRU_C * r), axis=0, keepdims=True)
        dlam = dsp * (-jax.nn.sigmoid(-lam_ref[...]))
        vrow = lax.broadcasted_iota(jnp.int32, (8, LANES), 0)
        upd = jnp.where(vrow == 4, jnp.sum(dxc, axis=0, keepdims=True), 0.0)
        upd = jnp.where(vrow == 5, jnp.sum(dzr, axis=0, keepdims=True), upd)
        upd = jnp.where(vrow == 6, jnp.sum(dzx, axis=0, keepdims=True), upd)
        upd = jnp.where(vrow == 7, dlam, upd)
        for k in range(CONV_WIDTH):
            upd = jnp.where(vrow == CONV_WIDTH - 1 - k, jnp.sum(dxc * shifted[k], axis=0, keepdims=True), upd)
        dvec_ref[...] += upd
        ext = jnp.concatenate([dxc, head_s[...]], axis=0)
        dxr = cw_ref[pl.ds(CONV_WIDTH - 1, 1), :] * dxc
        for k in (1, 2, 3):
            dxr = dxr + cw_ref[pl.ds(CONV_WIDTH - 1 - k, 1), :] * pltpu.roll(ext, rows + 8 - k, 0)[:rows]
        dxr_ref[...] = dxr.astype(BF16)
        tmp_s[...] = g
        gcar_s[...] = tmp_s[pl.ds(0, 1), :]
        tmp_s[...] = a
        acar_s[...] = tmp_s[pl.ds(0, 1), :]
        tmp_s[...] = dxc
        head_s[...] = tmp_s[pl.ds(0, 8), :]

    tile, halo, vec, cws, mat = _lru_specs(nt, True)
    return _call(
        body, name="lru_bwd", grid=(D // LANES, nt), ins=[dy, xr, xr, xg, hseq, hseq, cw, cb, wa, wx, ba, bx, lam],
        in_specs=[tile, tile, halo, tile, tile, halo, cws, vec, mat, mat, vec, vec, vec],
        out_specs=[tile, tile, pl.BlockSpec((8, LANES), lambda cb, t: (0, cb)), mat, mat],
        out_shape=[jax.ShapeDtypeStruct((t_tok, D), BF16), jax.ShapeDtypeStruct((t_tok, D), BF16),
                   jax.ShapeDtypeStruct((8, D), F32), jax.ShapeDtypeStruct((D // LANES, LANES, LANES), F32),
                   jax.ShapeDtypeStruct((D // LANES, LANES, LANES), F32)],
        scratch=[pltpu.VMEM((1, LANES), F32), pltpu.VMEM((1, LANES), F32), pltpu.VMEM((8, LANES), F32),
                 pltpu.VMEM((rows, LANES), F32)], carries=carries)


def _t5_bucket_np(rel):
    nb = N_BUCKETS // 2
    max_exact = nb // 2
    ret = np.where(rel > 0, nb, 0)
    n = np.abs(rel)
    nf = np.maximum(n, 1).astype(np.float32)
    large = max_exact + (np.log(nf / np.float32(max_exact)) / np.float32(math.log(MAX_DISTANCE / max_exact))
                         * np.float32(nb - max_exact)).astype(np.int32)
    large = np.minimum(large, nb - 1)
    return ret + np.where(n < max_exact, n, large)


def _bucket_map():
    r = np.arange(QT)[:, None]
    c = np.arange(KW)[None, :]
    j = c - (r // CHUNK) * CHUNK
    band = (j >= 0) & (j < WINDOW + CHUNK)
    return np.where(band, _t5_bucket_np(c - r - WINDOW), -1).astype(np.int32)


def _attn_specs(nt, reverse):
    def tt(i):
        return nt - 1 - i if reverse else i
    kvw = N_KV * LANES
    qs = pl.BlockSpec((QT, D), lambda i: (tt(i), 0))
    cur = pl.BlockSpec((QT, kvw), lambda i: (tt(i), 0))
    prev = pl.BlockSpec((WINDOW, kvw), lambda i: (jnp.maximum(tt(i) * (QT // WINDOW) - 1, 0), 0))
    lse = pl.BlockSpec((QT, LANES), lambda i: (tt(i), 0))
    return qs, cur, prev, lse


REP = N_HEADS // N_KV
SCALE = HEAD_DIM ** -0.5


def _stack_heads(x_ref, g, lo, scale=None):
    parts = []
    for hl in range(REP):
        xs = x_ref[:, pl.ds((2 * g + hl // 2) * LANES, LANES)]
        xs = jnp.where(lo if hl % 2 == 0 else jnp.logical_not(lo), xs, jnp.zeros_like(xs))
        parts.append(xs if scale is None else xs * jnp.asarray(scale, xs.dtype))
    return jnp.concatenate(parts, axis=0)


def _stack_sinks(sink_ref, g, srow):
    sk = jnp.full(srow.shape, sink_ref[REP * g + REP - 1], F32)
    for hl in range(REP - 2, -1, -1):
        sk = jnp.where(srow < (hl + 1) * QT, sink_ref[REP * g + hl], sk)
    return sk


def _attn_fwd(q, kd, vd, bias, sinks, carries=()):
    t_tok = q.shape[0]
    nt = t_tok // QT

    def body(q_ref, kp_ref, kc_ref, vp_ref, vc_ref, bias_ref, sink_ref, o_ref, lse_ref):
        i = pl.program_id(0)
        col = lax.broadcasted_iota(jnp.int32, (1, KW), 1)
        first = jnp.where((i == 0) & (col < WINDOW), NEG_INF, 0.0)
        lane = lax.broadcasted_iota(jnp.int32, (QT, LANES), 1)
        lo = lane < HEAD_DIM
        srow = lax.broadcasted_iota(jnp.int32, (REP * QT, 1), 0)
        lse_t = jnp.zeros((QT, LANES), F32)
        for g in range(N_KV):
            kwin = jnp.concatenate([kp_ref[:, pl.ds(g * LANES, LANES)], kc_ref[:, pl.ds(g * LANES, LANES)]], axis=0)
            vwin = jnp.concatenate([vp_ref[:, pl.ds(g * LANES, LANES)], vc_ref[:, pl.ds(g * LANES, LANES)]], axis=0)
            qst = _stack_heads(q_ref, g, lo, SCALE)
            s = _nt(qst, kwin) + (bias_ref[g] + first)
            sk = _stack_sinks(sink_ref, g, srow)
            m = jnp.maximum(jnp.max(s, axis=-1, keepdims=True), sk)
            e = jnp.exp(s - m)
            l = jnp.sum(e, axis=-1, keepdims=True) + jnp.exp(sk - m)
            p = e / l
            ost = _nn(p.astype(BF16), vwin)
            lse_s = m + jnp.log(l)
            for sl in range(2):
                o_ref[:, pl.ds((2 * g + sl) * LANES, LANES)] = jnp.where(
                    lo, ost[2 * sl * QT:(2 * sl + 1) * QT], ost[(2 * sl + 1) * QT:(2 * sl + 2) * QT]).astype(BF16)
            for hl in range(REP):
                lse_t = jnp.where(lane == REP * g + hl, lse_s[hl * QT:(hl + 1) * QT], lse_t)
        lse_ref[...] = lse_t

    qs, cur, prev, lse = _attn_specs(nt, False)
    return _call(
        body, name="attn_fwd", grid=(nt,), ins=[q, kd, kd, vd, vd, bias, sinks],
        in_specs=[qs, prev, cur, prev, cur, _const_spec((N_KV, REP * QT, KW)), pl.BlockSpec(memory_space=pltpu.SMEM)],
        out_specs=[qs, lse],
        out_shape=[jax.ShapeDtypeStruct((t_tok, D), BF16), jax.ShapeDtypeStruct((t_tok, LANES), F32)],
        vmem=48 * 2 ** 20, carries=carries)


def _attn_bwd(q, kd, vd, o, do, lse, bias, sinks, carries=()):
    t_tok = q.shape[0]
    nt = t_tok // QT
    kvw = N_KV * LANES

    def body(q_ref, kp_ref, kc_ref, vp_ref, vc_ref, o_ref, do_ref, lse_ref, bias_ref, sink_ref,
             dqkv_ref, ds_ref, dsink_ref, kcar_s, vcar_s):
        i = pl.program_id(0)
        tile = nt - 1 - i

        @pl.when(i == 0)
        def _():
            kcar_s[...] = jnp.zeros_like(kcar_s)
            vcar_s[...] = jnp.zeros_like(vcar_s)
            ds_ref[...] = jnp.zeros_like(ds_ref)
            dsink_ref[...] = jnp.zeros_like(dsink_ref)

        col = lax.broadcasted_iota(jnp.int32, (1, KW), 1)
        first = jnp.where((tile == 0) & (col < WINDOW), NEG_INF, 0.0)
        lane = lax.broadcasted_iota(jnp.int32, (QT, LANES), 1)
        lo = lane < HEAD_DIM
        lane_k = lax.broadcasted_iota(jnp.int32, (KW, LANES), 1)
        lane_1 = lax.broadcasted_iota(jnp.int32, (1, LANES), 1)
        srow = lax.broadcasted_iota(jnp.int32, (REP * QT, 1), 0)
        lse_t = lse_ref[...]
        dsink = jnp.zeros((1, LANES), F32)
        for g in range(N_KV):
            kwin = jnp.concatenate([kp_ref[:, pl.ds(g * LANES, LANES)], kc_ref[:, pl.ds(g * LANES, LANES)]], axis=0)
            vwin = jnp.concatenate([vp_ref[:, pl.ds(g * LANES, LANES)], vc_ref[:, pl.ds(g * LANES, LANES)]], axis=0)
            qst = _stack_heads(q_ref, g, lo, SCALE)
            dost = _stack_heads(do_ref, g, lo)
            od = [do_ref[:, pl.ds((2 * g + sl) * LANES, LANES)].astype(F32)
                  * o_ref[:, pl.ds((2 * g + sl) * LANES, LANES)].astype(F32) for sl in range(2)]
            drow = jnp.concatenate([jnp.sum(jnp.where(lo if hl % 2 == 0 else jnp.logical_not(lo), od[hl // 2], 0.0),
                                            axis=-1, keepdims=True) for hl in range(REP)], axis=0)
            lse_s = jnp.concatenate([jnp.sum(jnp.where(lane == REP * g + hl, lse_t, 0.0), axis=-1, keepdims=True)
                                     for hl in range(REP)], axis=0)
            s = _nt(qst, kwin) + (bias_ref[g] + first)
            p = jnp.exp(s - lse_s)
            ds = p * (_nt(dost, vwin) - drow)
            ds_ref[g] += ds
            tsink = -(jnp.exp(_stack_sinks(sink_ref, g, srow) - lse_s) * drow)
            for hl in range(REP):
                dsink = dsink + jnp.where(lane_1 == REP * g + hl,
                                          jnp.sum(tsink[hl * QT:(hl + 1) * QT], axis=0, keepdims=True), 0.0)
            dsb = ds.astype(BF16)
            dqst = _nn(dsb, kwin) * SCALE
            for sl in range(2):
                dqkv_ref[:, pl.ds((2 * g + sl) * LANES, LANES)] = jnp.where(
                    lo, dqst[2 * sl * QT:(2 * sl + 1) * QT], dqst[(2 * sl + 1) * QT:(2 * sl + 2) * QT]).astype(BF16)
            dk_acc = _tn(dsb, qst)
            dv_acc = _tn(p.astype(BF16), dost)
            dk_f = jnp.where(lane_k < HEAD_DIM, dk_acc + pltpu.roll(dk_acc, HEAD_DIM, 1), 0.0)
            dv_f = jnp.where(lane_k < HEAD_DIM, dv_acc + pltpu.roll(dv_acc, HEAD_DIM, 1), 0.0)
            for acc, col0, car in ((dk_f, K0, kcar_s), (dv_f, V0, vcar_s)):
                cs = pl.ds(g * LANES, LANES)
                co = pl.ds(col0 + g * LANES, LANES)
                if QT > WINDOW:
                    dqkv_ref[pl.ds(0, QT - WINDOW), co] = acc[WINDOW:QT].astype(BF16)
                dqkv_ref[pl.ds(QT - WINDOW, WINDOW), co] = (acc[QT:KW] + car[:, cs]).astype(BF16)
                car[:, cs] = acc[0:WINDOW]
        dsink_ref[...] += dsink

    qs, cur, prev, lse_s = _attn_specs(nt, True)
    return _call(
        body, name="attn_bwd", grid=(nt,), ins=[q, kd, kd, vd, vd, o, do, lse, bias, sinks],
        in_specs=[qs, prev, cur, prev, cur, qs, qs, lse_s, _const_spec((N_KV, REP * QT, KW)),
                  pl.BlockSpec(memory_space=pltpu.SMEM)],
        out_specs=[pl.BlockSpec((QT, XR0), lambda i: (nt - 1 - i, 0)), _const_spec((N_KV, REP * QT, KW)),
                   _const_spec((1, LANES))],
        out_shape=[jax.ShapeDtypeStruct((t_tok, XR0), BF16), jax.ShapeDtypeStruct((N_KV, REP * QT, KW), F32),
                   jax.ShapeDtypeStruct((1, LANES), F32)],
        scratch=[pltpu.VMEM((WINDOW, kvw), F32), pltpu.VMEM((WINDOW, kvw), F32)],
        vmem=VMEM_BIG, carries=carries)


def _bias_tile(table, bmap):
    def body(tab_ref, bm_ref, out_ref):
        bm = bm_ref[...]

        def per_head(hd, carry):
            acc = jnp.full((QT, KW), NEG_INF, F32)
            for b in range(N_BUCKETS):
                acc = jnp.where(bm == b, tab_ref[b, hd], acc)
            out_ref[hd] = acc
            return carry

        lax.fori_loop(0, N_HEADS, per_head, 0)

    return pl.pallas_call(
        body, name="bias_tile", out_shape=jax.ShapeDtypeStruct((N_HEADS, QT, KW), F32),
        in_specs=[pl.BlockSpec(memory_space=pltpu.SMEM), pl.BlockSpec(memory_space=pltpu.VMEM)],
        out_specs=pl.BlockSpec(memory_space=pltpu.VMEM))(table, bmap)


def _bias_grad(ds_acc, bmap):
    def body(ds_ref, bm_ref, out_ref):
        row = lax.broadcasted_iota(jnp.int32, (N_BUCKETS, LANES), 0)
        lane = lax.broadcasted_iota(jnp.int32, (N_BUCKETS, LANES), 1)
        bm = bm_ref[...]

        def per_head(hd, res):
            dsv = ds_ref[hd]
            for b in range(N_BUCKETS):
                val = jnp.sum(jnp.sum(jnp.where(bm == b, dsv, 0.0), axis=0, keepdims=True), axis=1, keepdims=True)
                res = jnp.where((row == b) & (lane == hd), val, res)
            return res

        out_ref[...] = lax.fori_loop(0, N_HEADS, per_head, jnp.zeros((N_BUCKETS, LANES), F32))

    return pl.pallas_call(
        body, name="bias_grad", out_shape=jax.ShapeDtypeStruct((N_BUCKETS, LANES), F32),
        in_specs=[pl.BlockSpec(memory_space=pltpu.VMEM), pl.BlockSpec(memory_space=pltpu.VMEM)],
        out_specs=pl.BlockSpec(memory_space=pltpu.VMEM))(ds_acc, bmap)


MIXOUT = ("w_lru_out", "w_attn_out", "w_o")


def _mix_out_fwd(ya_in, o, gs, h, g, wg, tm=512, carries=()):
    t_tok = h.shape[0]
    nt = t_tok // tm
    groups = _groups_of(MIXOUT)

    def body(ya_ref, o_ref, gs_ref, h_ref, g_ref, wg_ref, hout_ref, yao_ref, ybo_ref, z_ref, wa_s, wb_s, wo_s, sems):
        i = pl.program_id(0)

        @pl.when(i == 0)
        def _():
            _load_weights({groups[0]: wg_ref}, list(zip(MIXOUT, (wa_s, wb_s, wo_s))), sems)

        ya = _nn(ya_ref[...], wa_s[...])
        yb = _nn(o_ref[...], wb_s[...])
        yao_ref[...] = ya.astype(BF16)
        ybo_ref[...] = yb.astype(BF16)
        merged = gs_ref[:, pl.ds(0, D)].astype(F32) * ya + gs_ref[:, pl.ds(D, D)].astype(F32) * yb
        z = _nn(merged.astype(BF16), wo_s[...])
        z_ref[...] = z
        zn, _, _ = _rms_fwd(z, g_ref[...])
        hout_ref[...] = h_ref[...] + zn

    return _call(
        body, name="mix_out_fwd", grid=(nt,), ins=[ya_in, o, gs, h, g, wg[groups[0]]],
        in_specs=[_row_spec(tm, D), _row_spec(tm, D), _row_spec(tm, 2 * D), _row_spec(tm, D), _const_spec((1, D)), ANY],
        out_specs=[_row_spec(tm, D)] * 4,
        out_shape=[jax.ShapeDtypeStruct((t_tok, D), F32), jax.ShapeDtypeStruct((t_tok, D), BF16),
                   jax.ShapeDtypeStruct((t_tok, D), BF16), jax.ShapeDtypeStruct((t_tok, D), F32)],
        scratch=[pltpu.VMEM((D, D), BF16)] * 3 + [DMA((3 * NDEV,))],
        vmem=48 * 2 ** 20, carries=carries)


def _mix_out_bwd(dh, z, ya, yb, gs, g, wg, tm=512, carries=()):
    t_tok = dh.shape[0]
    nt = t_tok // tm
    groups = _groups_of(MIXOUT)

    def body(dh_ref, z_ref, ya_ref, yb_ref, gs_ref, g_ref, wg_ref,
             dyain_ref, do_ref, dgpre_ref, dya_ref, dyb_ref, mg_ref, dz_ref, dg_ref, dbg_ref,
             wa_s, wb_s, wo_s, sems):
        i = pl.program_id(0)

        @pl.when(i == 0)
        def _():
            _load_weights({groups[0]: wg_ref}, list(zip(MIXOUT, (wa_s, wb_s, wo_s))), sems)
            dg_ref[...] = jnp.zeros_like(dg_ref)
            dbg_ref[...] = jnp.zeros_like(dbg_ref)

        gv = g_ref[...]
        _, zh, r = _rms_fwd(z_ref[...], gv)
        dz, dg = _rms_bwd(dh_ref[...], zh, r, gv)
        dg_ref[...] += dg
        dzb = dz.astype(BF16)
        dz_ref[...] = dzb
        ga, gb = gs_ref[:, pl.ds(0, D)].astype(F32), gs_ref[:, pl.ds(D, D)].astype(F32)
        ya_v, yb_v = ya_ref[...].astype(F32), yb_ref[...].astype(F32)
        mg_ref[...] = (ga * ya_v + gb * yb_v).astype(BF16)
        dm = _nt(dzb, wo_s[...])
        dga = dm * ya_v * ga * (1.0 - ga)
        dgb = dm * yb_v * gb * (1.0 - gb)
        dgpre_ref[:, pl.ds(0, D)] = dga.astype(BF16)
        dgpre_ref[:, pl.ds(D, D)] = dgb.astype(BF16)
        dbg_ref[:, pl.ds(0, D)] += jnp.sum(dga, axis=0, keepdims=True)
        dbg_ref[:, pl.ds(D, D)] += jnp.sum(dgb, axis=0, keepdims=True)
        dya = (dm * ga).astype(BF16)
        dyb = (dm * gb).astype(BF16)
        dya_ref[...] = dya
        dyb_ref[...] = dyb
        dyain_ref[...] = _nt(dya, wa_s[...])
        do_ref[...] = _nt(dyb, wb_s[...]).astype(BF16)

    bf = jax.ShapeDtypeStruct((t_tok, D), BF16)
    return _call(
        body, name="mix_out_bwd", grid=(nt,), ins=[dh, z, ya, yb, gs, g, wg[groups[0]]],
        in_specs=[_row_spec(tm, D)] * 4 + [_row_spec(tm, 2 * D), _const_spec((1, D)), ANY],
        out_specs=[_row_spec(tm, D), _row_spec(tm, D), _row_spec(tm, 2 * D)] + [_row_spec(tm, D)] * 4
        + [_const_spec((1, D)), _const_spec((1, 2 * D))],
        out_shape=[jax.ShapeDtypeStruct((t_tok, D), F32), bf, jax.ShapeDtypeStruct((t_tok, 2 * D), BF16), bf, bf, bf, bf,
                   jax.ShapeDtypeStruct((1, D), F32), jax.ShapeDtypeStruct((1, 2 * D), F32)],
        scratch=[pltpu.VMEM((D, D), BF16)] * 3 + [DMA((3 * NDEV,))],
        vmem=VMEM_BIG, carries=carries)


def _sum_parts(parts_list):
    n = len(parts_list)
    _, r, c = parts_list[0].shape
    tc = 256

    def body(*refs):
        for p_ref, o_ref in zip(refs[:n], refs[n:]):
            acc = p_ref[0].astype(F32)
            for s in range(1, NDEV):
                acc = acc + p_ref[s].astype(F32)
            o_ref[...] = acc

    return pl.pallas_call(
        body, name=f"sum_parts_{r}", grid=(c // tc,),
        in_specs=[pl.BlockSpec((NDEV, r, tc), lambda i: (0, 0, i))] * n,
        out_specs=[pl.BlockSpec((r, tc), lambda i: (0, i))] * n,
        out_shape=[jax.ShapeDtypeStruct((r, c), F32)] * n,
        compiler_params=pltpu.CompilerParams(dimension_semantics=("arbitrary",), vmem_limit_bytes=48 * 2 ** 20),
    )(*parts_list)


def _adamw_math(w, g, m, v):
    m = ADAM_B1 * m + (1.0 - ADAM_B1) * g
    v = ADAM_B2 * v + (1.0 - ADAM_B2) * (g * g)
    m_hat = m / (1.0 - ADAM_B1 ** ADAM_STEP)
    v_hat = v / (1.0 - ADAM_B2 ** ADAM_STEP)
    delta = -ADAM_LR * (m_hat / (jnp.sqrt(v_hat) + ADAM_EPS) + ADAM_WD * w)
    return delta, m, v


def _sum_ready(parts_list, name, carries=()):
    n = len(parts_list)
    c = parts_list[0].shape[2]
    tc = 2 * LANES

    def body(*refs):
        for p_ref, o_ref in zip(refs[:n], refs[n:]):
            g = p_ref[0].astype(F32)
            for s in range(1, NDEV):
                g = g + p_ref[s].astype(F32)
            o_ref[...] = g

    return _call(
        body, name=name, grid=(c // tc,), ins=list(parts_list),
        in_specs=[pl.BlockSpec((NDEV, p.shape[1], tc), lambda i: (0, 0, i)) for p in parts_list],
        out_shape=[jax.ShapeDtypeStruct((p.shape[1], c), F32) for p in parts_list],
        out_specs=[pl.BlockSpec((p.shape[1], tc), lambda i: (0, i)) for p in parts_list],
        vmem=48 * 2 ** 20, carries=carries)


def _adamw_cols(items, name):
    n = len(items)
    c = items[0][1].shape[1]
    tc = LANES

    def body(*refs):
        for k in range(n):
            g_ref, w_ref, m_ref, v_ref = refs[4 * k:4 * k + 4]
            go_ref, d_ref, nm_ref, nv_ref = refs[4 * n + 4 * k:4 * n + 4 * k + 4]
            g = g_ref[...]
            d, m, v = _adamw_math(w_ref[...], g, m_ref[...], v_ref[...])
            go_ref[...] = g
            d_ref[...] = d
            nm_ref[...] = m
            nv_ref[...] = v

    ins, specs, out_shape = [], [], []
    for g, w, m, v in items:
        r = w.shape[0]
        ins += [g, w, m, v]
        specs += [pl.BlockSpec((r, tc), lambda i: (0, i))] * 4
        out_shape += [jax.ShapeDtypeStruct((r, c), F32)] * 4
    outs, _ = _call(body, name=name, grid=(c // tc,), ins=ins, in_specs=specs, out_shape=out_shape,
                    out_specs=specs, vmem=48 * 2 ** 20)
    return [tuple(outs[4 * k:4 * k + 4]) for k in range(n)]


def _adamw_body(n):
    def body(*refs):
        for k in range(n):
            g_ref, w_ref, m_ref, v_ref = refs[4 * k:4 * k + 4]
            d_ref, nm_ref, nv_ref = refs[4 * n + 3 * k:4 * n + 3 * k + 3]
            d, m, v = _adamw_math(w_ref[...], g_ref[...], m_ref[...], v_ref[...])
            d_ref[...] = d
            nm_ref[...] = m
            nv_ref[...] = v
    return body


def _adamw(items):
    n = len(items)
    r, c = items[0][0].shape
    tr = r if r * c <= 2 ** 18 else max(t for t in range(8, 65, 8) if r % t == 0)
    spec = pl.BlockSpec((tr, c), lambda i: (i, 0))
    outs = pl.pallas_call(
        _adamw_body(n), name=f"adamw_{r}x{c}", grid=(r // tr,),
        in_specs=[spec] * (4 * n), out_specs=[spec] * (3 * n),
        out_shape=[jax.ShapeDtypeStruct((r, c), F32)] * (3 * n),
        compiler_params=pltpu.CompilerParams(dimension_semantics=("arbitrary",), vmem_limit_bytes=40 * 2 ** 20),
    )(*[a for it in items for a in it])
    return [tuple(outs[3 * k:3 * k + 3]) for k in range(n)]


def _adamw_small(items):
    n = len(items)
    vm = pl.BlockSpec(memory_space=pltpu.VMEM)
    outs = pl.pallas_call(
        _adamw_body(n), name="adamw_small", in_specs=[vm] * (4 * n), out_specs=[vm] * (3 * n),
        out_shape=[jax.ShapeDtypeStruct(it[1].shape, F32) for it in items for _ in range(3)],
    )(*[a for it in items for a in it])
    return [tuple(outs[3 * k:3 * k + 3]) for k in range(n)]


def _pack_small(arrs):
    rows, offs = [], []
    total = 0
    for a in arrs:
        flat = a.reshape(-1).astype(F32)
        nr = -(-flat.shape[0] // LANES)
        flat = jnp.pad(flat, (0, nr * LANES - flat.shape[0]))
        rows.append(flat.reshape(nr, LANES))
        offs.append((total, nr))
        total += nr
    pad = -total % 8
    if pad:
        rows.append(jnp.zeros((pad, LANES), F32))
    return jnp.concatenate(rows, axis=0), offs


def _unpack_small(pack, offs, shapes):
    out = []
    for (o, nr), shp in zip(offs, shapes):
        size = int(np.prod(shp))
        out.append(pack[o:o + nr].reshape(-1)[:size].reshape(shp))
    return out


def _sum_small(gathered, rows):
    def body(p_ref, o_ref):
        acc = p_ref[pl.ds(0, rows), :]
        for s in range(1, NDEV):
            acc = acc + p_ref[pl.ds(s * rows, rows), :]
        o_ref[...] = acc

    return pl.pallas_call(
        body, name="sum_small", out_shape=jax.ShapeDtypeStruct((rows, LANES), F32),
        in_specs=[pl.BlockSpec(memory_space=pltpu.VMEM)], out_specs=pl.BlockSpec(memory_space=pltpu.VMEM))(gathered)


def _block_diag(w):
    w = w.reshape(D // LANES, 2, LRU_BLOCK, LRU_BLOCK)
    z = jnp.zeros((D // LANES, LRU_BLOCK, LRU_BLOCK), w.dtype)
    top = jnp.concatenate([w[:, 0], z], axis=2)
    bot = jnp.concatenate([z, w[:, 1]], axis=2)
    return jnp.concatenate([top, bot], axis=1)


def _block_diag_grad(dw):
    a = dw[:, :LRU_BLOCK, :LRU_BLOCK]
    b = dw[:, LRU_BLOCK:, LRU_BLOCK:]
    return jnp.stack([a, b], axis=1).reshape(D // LRU_BLOCK, LRU_BLOCK, LRU_BLOCK)


def kernel(x, ffn1_pre_g, ffn1_w1, ffn1_w3, ffn1_w2, ffn1_post_g, mix_pre_g, w_in, conv_w, conv_b, rg_a_w, rg_a_b, rg_x_w, rg_x_b, lru_lambda, w_lru_out, attn_sinks, rel_bias, w_attn_out, w_gate, b_gate, w_o, mix_post_g, ffn2_pre_g, ffn2_w1, ffn2_w3, ffn2_w2, ffn2_post_g, loss_target, m_ffn1_pre_g, m_ffn1_w1, m_ffn1_w3, m_ffn1_w2, m_ffn1_post_g, m_mix_pre_g, m_w_in, m_conv_w, m_conv_b, m_rg_a_w, m_rg_a_b, m_rg_x_w, m_rg_x_b, m_lru_lambda, m_w_lru_out, m_attn_sinks, m_rel_bias, m_w_attn_out, m_w_gate, m_b_gate, m_w_o, m_mix_post_g, m_ffn2_pre_g, m_ffn2_w1, m_ffn2_w3, m_ffn2_w2, m_ffn2_post_g, v_ffn1_pre_g, v_ffn1_w1, v_ffn1_w3, v_ffn1_w2, v_ffn1_post_g, v_mix_pre_g, v_w_in, v_conv_w, v_conv_b, v_rg_a_w, v_rg_a_b, v_rg_x_w, v_rg_x_b, v_lru_lambda, v_w_lru_out, v_attn_sinks, v_rel_bias, v_w_attn_out, v_w_gate, v_b_gate, v_w_o, v_mix_post_g, v_ffn2_pre_g, v_ffn2_w1, v_ffn2_w3, v_ffn2_w2, v_ffn2_post_g):
    names = ["ffn1_pre_g", "ffn1_w1", "ffn1_w3", "ffn1_w2", "ffn1_post_g", "mix_pre_g", "w_in", "conv_w", "conv_b",
             "rg_a_w", "rg_a_b", "rg_x_w", "rg_x_b", "lru_lambda", "w_lru_out", "attn_sinks", "rel_bias", "w_attn_out",
             "w_gate", "b_gate", "w_o", "mix_post_g", "ffn2_pre_g", "ffn2_w1", "ffn2_w3", "ffn2_w2", "ffn2_post_g"]
    ws = dict(zip(names, (ffn1_pre_g, ffn1_w1, ffn1_w3, ffn1_w2, ffn1_post_g, mix_pre_g, w_in, conv_w, conv_b, rg_a_w,
                          rg_a_b, rg_x_w, rg_x_b, lru_lambda, w_lru_out, attn_sinks, rel_bias, w_attn_out, w_gate,
                          b_gate, w_o, mix_post_g, ffn2_pre_g, ffn2_w1, ffn2_w3, ffn2_w2, ffn2_post_g)))
    ms = dict(zip(names, (m_ffn1_pre_g, m_ffn1_w1, m_ffn1_w3, m_ffn1_w2, m_ffn1_post_g, m_mix_pre_g, m_w_in, m_conv_w,
                          m_conv_b, m_rg_a_w, m_rg_a_b, m_rg_x_w, m_rg_x_b, m_lru_lambda, m_w_lru_out, m_attn_sinks,
                          m_rel_bias, m_w_attn_out, m_w_gate, m_b_gate, m_w_o, m_mix_post_g, m_ffn2_pre_g, m_ffn2_w1,
                          m_ffn2_w3, m_ffn2_w2, m_ffn2_post_g)))
    vs = dict(zip(names, (v_ffn1_pre_g, v_ffn1_w1, v_ffn1_w3, v_ffn1_w2, v_ffn1_post_g, v_mix_pre_g, v_w_in, v_conv_w,
                          v_conv_b, v_rg_a_w, v_rg_a_b, v_rg_x_w, v_rg_x_b, v_lru_lambda, v_w_lru_out, v_attn_sinks,
                          v_rel_bias, v_w_attn_out, v_w_gate, v_b_gate, v_w_o, v_mix_post_g, v_ffn2_pre_g, v_ffn2_w1,
                          v_ffn2_w3, v_ffn2_w2, v_ffn2_post_g)))
    me = 4 * lax.axis_index("x") + 2 * lax.axis_index("y") + lax.axis_index("c")
    vec = lambda n: ws[n].reshape(1, -1)

    def shard2d(name):
        if name == "conv":
            row = lax.bitcast_convert_type(conv_w.reshape(CONV_WIDTH, LANES), BF16).reshape(1, D)
            return jnp.concatenate([row, jnp.zeros((LOC["conv"][2] - 1, D), BF16)], axis=0)
        a = ws[name].reshape(ws[name].shape[-2], ws[name].shape[-1])
        return (a.T if name in COL_SHARDED else a).astype(BF16)

    packs = {g: jnp.concatenate([shard2d(n) for n, _ in members], axis=0) for g, members in GROUPS}

    wg = {}
    _, ((wg["ffn1"],),) = _call(None, name="allgather_ffn1", grid=(), ins=[], in_specs=[], out_shape=[], out_specs=[],
                                carries=[_AllGather(packs["ffn1"])])
    conv_rows = wg["ffn1"].reshape(NDEV, GROUP_ROWS["ffn1"], D)[:, LOC["conv"][1]]
    cw = jnp.transpose(lax.bitcast_convert_type(conv_rows.reshape(NDEV, CONV_WIDTH, LANES, 2), F32),
                       (1, 0, 2)).reshape(CONV_WIDTH, D)
    bmap = jnp.asarray(_bucket_map())
    bias = _bias_tile(rel_bias, bmap).reshape(N_KV, REP * QT, KW)
    sinks = attn_sinks.reshape(N_HEADS)
    wa_bd = _block_diag(rg_a_w.reshape(D // LRU_BLOCK, LRU_BLOCK, LRU_BLOCK)).astype(BF16)
    wx_bd = _block_diag(rg_x_w.reshape(D // LRU_BLOCK, LRU_BLOCK, LRU_BLOCK)).astype(BF16)
    lru_args = (cw, vec("conv_b"), wa_bd, wx_bd, vec("rg_a_b"), vec("rg_x_b"), vec("lru_lambda"))
    x2, tgt = x[0], loss_target[0]

    (h1, a1, b1, f1, nb1), ((wg["mixin"],),) = _ffn_fwd(
        x2, vec("ffn1_pre_g"), vec("ffn1_post_g"), wg, ("ffn1_w1", "ffn1_w3", "ffn1_w2"),
        carries=[_AllGather(packs["mixin"])])
    (q, kd, vd, xr, xg, gs, ub), ((wg["mixout"],),) = _mix_proj_fwd(
        h1, vec("mix_pre_g"), vec("b_gate"), wg, carries=[_AllGather(packs["mixout"])])
    (ya_in, hseq), ((wg["ffn2a"],),) = _lru_fwd(xr, xg, *lru_args, carries=[_AllGather(packs["ffn2a"])])
    (o, lse), ((wg["ffn2b"],),) = _attn_fwd(q, kd, vd, bias, sinks, carries=[_AllGather(packs["ffn2b"])])
    (h2, ya, yb, z), _ = _mix_out_fwd(ya_in, o, gs, h1, vec("mix_post_g"), wg)
    (_, a2, b2, f2, nb2, dy, loss_part), _ = _ffn_fwd(h2, vec("ffn2_pre_g"), vec("ffn2_post_g"), wg,
                                                     ("ffn2_w1", "ffn2_w3", "ffn2_w2"), target=tgt)

    gsm, parts = {}, {}
    rs = lambda *grads_: [_ReduceScatterSend(list(grads_))]
    ffr = FF // NDEV
    (dab, s_act, dfb, gsm["ffn2_post_g"]), _ = _ffn_bwd_a(dy, f2, a2, b2, vec("ffn2_post_g"), wg, "ffn2_w2")
    g_w2, _ = _dw(s_act, dfb, FF // 2, "dw_ffn2_w2")
    g_w13, ((parts["ffn2_w2"],),) = _dw(dab, nb2, D // 2, "dw_ffn2_w13", carries=rs(g_w2))
    (dh2, gsm["ffn2_pre_g"]), ((parts["ffn2_w1"],),) = _ffn_bwd_dx(
        dab, h2, dy, vec("ffn2_pre_g"), wg, "ffn2_w1", "ffn2_w3", carries=rs((g_w13, 0, ffr, ffr)))
    (dya_in, do, dgpre, dya, dyb, mg, dzb, gsm["mix_post_g"], gsm["b_gate"]), ((parts["ffn2_w3"],),) = _mix_out_bwd(
        dh2, z, ya, yb, gs, vec("mix_post_g"), wg, carries=rs((g_w13, FF, ffr, ffr)))
    g_wa, _ = _dw(ya_in, dya, D // 2, "dw_w_lru_out")
    g_wb, _ = _dw(o, dyb, D // 2, "dw_w_attn_out")
    g_wo, _ = _dw(mg, dzb, D // 2, "dw_w_o")
    (dqkv, ds_acc, dsink), (mixout_parts,) = _attn_bwd(
        q, kd, vd, o, do, lse, bias, sinks, carries=rs(g_wa, g_wb, g_wo))
    parts["w_lru_out"], parts["w_attn_out"], parts["w_o"] = mixout_parts
    (dxr, dxg, dvec, dwa, dwx), _ = _lru_bwd(dya_in, xr, xg, hseq, *lru_args)
    g_qkv, _ = _dw(dqkv, ub, D // 2, "dw_w_in_qkv")
    g_xr, _ = _dw(dxr, ub, D // 2, "dw_w_in_xr")
    g_xg, _ = _dw(dxg, ub, D // 2, "dw_w_in_xg")
    g_wgate, _ = _dw(dgpre, ub, D // 2, "dw_w_gate")
    g_win = jnp.concatenate(
        [g_qkv[:D]] + [g_qkv[c0 + gi * LANES:c0 + gi * LANES + HEAD_DIM] for c0 in (K0, V0) for gi in range(N_KV)]
        + [g_xr, g_xg], axis=0)
    (dh1, gsm["mix_pre_g"]), ((parts["w_in"],),) = _mix_proj_bwd(
        dqkv, dxr, dxg, dgpre, h1, dh2, vec("mix_pre_g"), wg, carries=rs(g_win))
    gsm["conv_w"] = dvec[0:CONV_WIDTH]
    gsm["conv_b"], gsm["rg_a_b"], gsm["rg_x_b"], gsm["lru_lambda"] = dvec[4], dvec[5], dvec[6], dvec[7]
    gsm["rg_a_w"] = _block_diag_grad(dwa)
    gsm["rg_x_w"] = _block_diag_grad(dwx)
    gsm["attn_sinks"] = dsink[0, :N_HEADS]
    gsm["rel_bias"] = _bias_grad(ds_acc.reshape(N_HEADS, QT, KW), bmap)[:, :N_HEADS]
    late = ("ffn1_post_g", "ffn1_pre_g")
    early = tuple(n for n in SMALL if n not in late)
    early_pack, early_offs = _pack_small([gsm[n] for n in early])
    (dab, s_act, dfb, gsm["ffn1_post_g"]), ((parts["w_gate"],), (early_all,)) = _ffn_bwd_a(
        dh1, f1, a1, b1, vec("ffn1_post_g"), wg, "ffn1_w2", carries=rs(g_wgate) + [_AllGather(early_pack)])
    g_w2, _ = _dw(s_act, dfb, FF // 2, "dw_ffn1_w2")
    g_w13, ((parts["ffn1_w2"],),) = _dw(dab, nb1, D // 2, "dw_ffn1_w13", carries=rs(g_w2))
    head = 64
    (grad_x, gsm["ffn1_pre_g"]), ((parts["ffn1_w1"], w3_head),) = _ffn_bwd_dx(
        dab, x2, dh1, vec("ffn1_pre_g"), wg, "ffn1_w1", "ffn1_w3",
        carries=rs((g_w13, 0, ffr, ffr), (g_w13, FF, ffr, head)))
    late_pack, late_offs = _pack_small([gsm[n] for n in late] + [loss_part])

    grads, delta, new_m, new_v = {}, {}, {}, {}
    t_form = [n for n in COL_SHARDED if ws[n].shape[-1] % LANES]
    view = {n: (lambda a: a.reshape(a.shape[-2], a.shape[-1]).T) if n in t_form
            else (lambda a: a.reshape(a.shape[-2], a.shape[-1])) for n in BIG}
    unview = {n: (lambda a: a.T) if n in t_form else (lambda a: a) for n in BIG}

    ready = [n for n in BIG if n not in ("ffn1_w3", "w_gate")]
    sums, ((w3_rest,), (late_all,)) = _sum_ready(
        [parts[n] for n in ready], "sum_ready", rs((g_w13, FF + head, ffr, ffr - head)) + [_AllGather(late_pack)])
    (sum_w3,), _ = _sum_ready([jnp.concatenate([w3_head, w3_rest], axis=1)], "sum_last")
    group = ready + ["ffn1_w3"]
    res = _adamw_cols([(g, view[n](ws[n]), view[n](ms[n]), view[n](vs[n])) for n, g in zip(group, sums + [sum_w3])],
                      "adamw_big")
    for n, quad in zip(group, res):
        grads[n], delta[n], new_m[n], new_v[n] = (unview[n](a) for a in quad)
    (g_gate_t,) = _sum_parts([parts["w_gate"]])
    grads["w_gate"] = g_gate_t.T
    ((delta["w_gate"], new_m["w_gate"], new_v["w_gate"]),) = _adamw(
        [(grads["w_gate"], view["w_gate"](ws["w_gate"]), view["w_gate"](ms["w_gate"]), view["w_gate"](vs["w_gate"]))])
    for group, gathered, offs in ((early, early_all, early_offs), (late, late_all, late_offs)):
        total = _sum_small(gathered, gathered.shape[0] // NDEV)
        shapes = [(CONV_WIDTH, D) if n == "conv_w" else ws[n].shape for n in group]
        if group is late:
            shapes = shapes + [(1, LANES)]
        unpacked = _unpack_small(total, offs, shapes)
        if group is late:
            loss = unpacked.pop()[0, 0]
        for n, g in zip(group, unpacked):
            grads[n] = g
    grads["conv_w"] = lax.dynamic_slice(grads["conv_w"], (0, me * LANES), (CONV_WIDTH, LANES)).reshape(conv_w.shape)

    flat2d = lambda a: a.reshape(-1, a.shape[-1])
    res = _adamw_small([(flat2d(grads[n].reshape(ws[n].shape)), flat2d(ws[n]), flat2d(ms[n]), flat2d(vs[n]))
                        for n in SMALL])
    for n, (d_, m_, v_) in zip(SMALL, res):
        delta[n], new_m[n], new_v[n] = d_, m_, v_

    outs = [loss, grad_x.reshape(x.shape)]
    for src in (grads, delta, new_m, new_v):
        outs += [src[n].reshape(ws[n].shape) for n in names]
    return tuple(outs)
```

```python
import functools
import math
import operator

import numpy as np
import jax
import jax.numpy as jnp
from jax import lax
from jax.experimental import pallas as pl
from jax.experimental.pallas import tpu as pltpu

F32, BF16 = jnp.float32, jnp.bfloat16

NDEV = 8
D = 1024
FF = 2816
N_HEADS, N_KV, HEAD_DIM = 16, 4, 64
CHUNK, WINDOW = 64, 128
N_BUCKETS, MAX_DISTANCE = 32, 128
LRU_BLOCK = 64
CONV_WIDTH = 4
LRU_C = 8.0
RMS_EPS = 1e-6
NEG_INF = -1e30
LANES = 128
QT = 128
KW = QT + WINDOW
LRU_ROWS = 512
IN_W = D + 2 * N_KV * HEAD_DIM + 2 * D
INP_W = D + 2 * N_KV * LANES + 2 * D
VMEM_BIG = 58 * 2 ** 20

ADAM_LR, ADAM_B1, ADAM_B2, ADAM_EPS, ADAM_WD, ADAM_STEP = 0.001, 0.9, 0.999, 1e-08, 0.01, 10

GROUPS = (("ffn1", (("ffn1_w1", FF // NDEV), ("ffn1_w3", FF // NDEV), ("ffn1_w2", FF // NDEV), ("conv", 16))),
          ("mixin", (("w_in", IN_W // NDEV), ("w_gate", 2 * D // NDEV))),
          ("mixout", (("w_lru_out", D // NDEV), ("w_attn_out", D // NDEV), ("w_o", D // NDEV))),
          ("ffn2a", (("ffn2_w1", FF // NDEV), ("ffn2_w3", FF // NDEV))),
          ("ffn2b", (("ffn2_w2", FF // NDEV),)))
LOC, GROUP_ROWS = {}, {}
for _g, _members in GROUPS:
    _o = 0
    for _n, _r in _members:
        LOC[_n] = (_g, _o, _r)
        _o += _r
    GROUP_ROWS[_g] = _o
BIG = tuple(n for _, members in GROUPS for n, _ in members if n != "conv")
COL_SHARDED = ("ffn1_w1", "ffn1_w3", "w_in", "w_gate", "ffn2_w1", "ffn2_w3")

SMALL = ("ffn1_pre_g", "ffn1_post_g", "mix_pre_g", "conv_w", "conv_b", "rg_a_w", "rg_a_b", "rg_x_w", "rg_x_b",
         "lru_lambda", "attn_sinks", "rel_bias", "b_gate", "mix_post_g", "ffn2_pre_g", "ffn2_post_g")

MESH = pl.DeviceIdType.MESH
ANY = pl.BlockSpec(memory_space=pl.ANY)
DMA = pltpu.SemaphoreType.DMA


def _nn(a, b):
    return lax.dot_general(a, b, (((1,), (0,)), ((), ())), preferred_element_type=F32)


def _nt(a, b):
    return lax.dot_general(a, b, (((1,), (1,)), ((), ())), preferred_element_type=F32)


def _tn(a, b):
    return lax.dot_general(a, b, (((0,), (0,)), ((), ())), preferred_element_type=F32)


def _rms_fwd(x, g):
    r = lax.rsqrt(jnp.mean(x * x, axis=-1, keepdims=True) + RMS_EPS)
    xh = x * r
    return xh * g, xh, r


def _rms_bwd(dn, xh, r, g):
    dxh = dn * g
    dx = r * (dxh - xh * jnp.mean(dxh * xh, axis=-1, keepdims=True))
    return dx, jnp.sum(dn * xh, axis=0, keepdims=True)


def _row_spec(tm, c):
    return pl.BlockSpec((tm, c), lambda i: (i, 0))


def _const_spec(shape):
    nd = len(shape)
    return pl.BlockSpec(shape, lambda i: (0,) * nd)


class _AllGather:
    def __init__(self, shard):
        self.m, n = shard.shape
        self.ins = [shard]
        self.out_shape = [jax.ShapeDtypeStruct((NDEV * self.m, n), shard.dtype)]
        self.scratch = [DMA((7,)), DMA((7,)), DMA]

    def _copies(self, ins, outs, scr, all_of_them):
        x_ref, out_ref = ins[0], outs[0]
        send_sems, recv_sems, local_sem = scr
        x, y, c = lax.axis_index("x"), lax.axis_index("y"), lax.axis_index("c")
        me, sibling = (x, y, c), (x, y, 1 - c)
        chips = [(1 - x, y), (x, 1 - y), (1 - x, 1 - y)]
        m = self.m

        def rows(px, py, pc):
            return out_ref.at[pl.ds((4 * px + 2 * py + pc) * m, m), :]

        def copy(k, block, to, src=None):
            return pltpu.make_async_remote_copy(
                src_ref=rows(*block) if src is None else src, dst_ref=rows(*block),
                send_sem=send_sems.at[k], recv_sem=recv_sems.at[k], device_id=to, device_id_type=MESH)

        mine = pltpu.make_async_copy(x_ref, rows(*me), local_sem)
        first = [copy(0, me, sibling, src=x_ref)] + [copy(1 + j, me, (*chip, c), src=x_ref)
                                                     for j, chip in enumerate(chips)]
        if not all_of_them:
            return mine, first
        passed = [copy(4 + j, (*chip, c), sibling) for j, chip in enumerate(chips)]
        landed = [copy(1 + j, (*chip, c), me) for j, chip in enumerate(chips)]
        from_sibling = [copy(0, sibling, me)] + [copy(4 + j, (*chip, 1 - c), me) for j, chip in enumerate(chips)]
        return mine, first, passed, landed, from_sibling

    def start(self, ins, outs, scr):
        mine, first = self._copies(ins, outs, scr, False)
        mine.start()
        for cp in first:
            cp.start()

    def finish(self, ins, outs, scr):
        mine, first, passed, landed, from_sibling = self._copies(ins, outs, scr, True)
        for cp_in, cp_on in zip(landed, passed):
            cp_in.wait_recv()
            cp_on.start()
        for cp in from_sibling:
            cp.wait_recv()
        for cp in first + passed:
            cp.wait_send()
        mine.wait()


class _ReduceScatterSend:
    def __init__(self, grads):
        grads = [g if isinstance(g, tuple) else (g, 0, g.shape[0] // NDEV, g.shape[0] // NDEV) for g in grads]
        self.nw = len(grads)
        self.base = [b for _, b, _, _ in grads]
        self.stride = [s for _, _, s, _ in grads]
        self.rows = [r for _, _, _, r in grads]
        self.ins = [g for g, _, _, _ in grads]
        self.out_shape = [jax.ShapeDtypeStruct((NDEV, r, g.shape[1]), g.dtype) for g, _, _, r in grads]
        self.scratch = [DMA((self.nw, NDEV - 1)), DMA((self.nw, NDEV - 1)), DMA((self.nw,))]

    def _copies(self, g_refs, r_refs, scr, want):
        send_sems, recv_sems, local_sems = scr
        x, y, c = lax.axis_index("x"), lax.axis_index("y"), lax.axis_index("c")
        me = 4 * x + 2 * y + c
        rows, base, stride = self.rows, self.base, self.stride
        out = []
        if want == "local":
            for w in range(self.nw):
                out.append(pltpu.make_async_copy(g_refs[w].at[pl.ds(base[w] + me * stride[w], rows[w])],
                                                 r_refs[w].at[me], local_sems.at[w]))
            return out
        for k in range(1, NDEV):
            px, py, pc = x ^ (k >> 2), y ^ ((k >> 1) & 1), c ^ (k & 1)
            peer = 4 * px + 2 * py + pc
            for w in range(self.nw):
                sems = dict(send_sem=send_sems.at[w, k - 1], recv_sem=recv_sems.at[w, k - 1],
                            device_id=(px, py, pc), device_id_type=MESH)
                if want == "send":
                    out.append(pltpu.make_async_remote_copy(
                        src_ref=g_refs[w].at[pl.ds(base[w] + peer * stride[w], rows[w])], dst_ref=r_refs[w].at[me],
                        **sems))
                else:
                    out.append(pltpu.make_async_remote_copy(
                        src_ref=g_refs[w].at[pl.ds(0, rows[w])], dst_ref=r_refs[w].at[peer], **sems))
        return out

    def start(self, ins, outs, scr):
        for cp in self._copies(ins, outs, scr, "local") + self._copies(ins, outs, scr, "send"):
            cp.start()

    def finish(self, ins, outs, scr):
        for cp in self._copies(ins, outs, scr, "recv"):
            cp.wait_recv()
        for cp in self._copies(ins, outs, scr, "send"):
            cp.wait_send()
        for cp in self._copies(ins, outs, scr, "local"):
            cp.wait()


def _call(body, *, name, grid, ins, in_specs, out_shape, out_specs, scratch=(), vmem=None, carries=()):
    n_in, n_out, n_scr = len(ins), len(out_shape), len(scratch)
    ng = len(grid)

    def split(refs):
        pos = [0]

        def take(k):
            part = refs[pos[0]:pos[0] + k]
            pos[0] += k
            return part

        i_refs = take(n_in)
        c_in = [take(len(c.ins)) for c in carries]
        o_refs = take(n_out)
        c_out = [take(len(c.out_shape)) for c in carries]
        s_refs = take(n_scr)
        c_scr = [take(len(c.scratch)) for c in carries]
        return i_refs, o_refs, s_refs, list(zip(carries, c_in, c_out, c_scr))

    def full(*refs):
        i_refs, o_refs, s_refs, cparts = split(refs)
        if ng == 0:
            for c, a, b, s in cparts:
                c.start(a, b, s)
            for c, a, b, s in cparts:
                c.finish(a, b, s)
            return
        ids = [pl.program_id(a) for a in range(ng)]
        if cparts:
            @pl.when(functools.reduce(operator.and_, [i == 0 for i in ids]))
            def _():
                for c, a, b, s in cparts:
                    c.start(a, b, s)

        body(*i_refs, *o_refs, *s_refs)
        if cparts:
            @pl.when(functools.reduce(operator.and_, [i == g - 1 for i, g in zip(ids, grid)]))
            def _():
                for c, a, b, s in cparts:
                    c.finish(a, b, s)

    all_ins = list(ins) + [a for c in carries for a in c.ins]
    all_in_specs = list(in_specs) + [ANY for c in carries for _ in c.ins]
    all_out_shape = list(out_shape) + [s for c in carries for s in c.out_shape]
    all_out_specs = list(out_specs) + [ANY for c in carries for _ in c.out_shape]
    all_scratch = list(scratch) + [s for c in carries for s in c.scratch]
    kwargs = dict(grid=grid) if ng else {}
    outs = pl.pallas_call(
        full, name=name, in_specs=all_in_specs, out_specs=all_out_specs, out_shape=all_out_shape,
        scratch_shapes=all_scratch,
        compiler_params=pltpu.CompilerParams(dimension_semantics=("arbitrary",) * ng if ng else None,
                                             vmem_limit_bytes=vmem),
        **kwargs)(*all_ins)
    outs = list(outs)
    res, pos = outs[:n_out], n_out
    carried = []
    for c in carries:
        carried.append(outs[pos:pos + len(c.out_shape)])
        pos += len(c.out_shape)
    return res, carried


def _groups_of(names):
    out = []
    for n in names:
        if LOC[n][0] not in out:
            out.append(LOC[n][0])
    return out


def _weight_pieces(name):
    g, off, rows = LOC[name]
    return [(d * GROUP_ROWS[g] + off, d * rows, rows) for d in range(NDEV)]


def _win_pieces():
    kv = N_KV * HEAD_DIM
    pieces = [(0, 0, D)]
    for g0, d0 in ((D, D), (D + kv, D + N_KV * LANES)):
        for g in range(N_KV):
            for half in range(2):
                pieces.append((g0 + g * HEAD_DIM, d0 + g * LANES + half * HEAD_DIM, HEAD_DIM))
    pieces.append((D + 2 * kv, D + 2 * N_KV * LANES, D))
    pieces.append((D + 2 * kv + D, D + 2 * N_KV * LANES + D, D))
    grp, off, rows = LOC["w_in"]
    out = []
    for g0, d0, n in pieces:
        while n > 0:
            dev, loc = divmod(g0, rows)
            m = min(n, rows - loc)
            out.append((dev * GROUP_ROWS[grp] + off + loc, d0, m))
            g0, d0, n = g0 + m, d0 + m, n - m
    return out


def _start_loads(src_ref, dst_ref, pieces, sems, base):
    cps = []
    for j, (s, d, n) in enumerate(pieces):
        cp = pltpu.make_async_copy(src_ref.at[pl.ds(s, n)], dst_ref.at[pl.ds(d, n)], sems.at[base + j])
        cp.start()
        cps.append(cp)
    return cps


def _load_weights(wrefs, targets, sems):
    cps, base = [], 0
    for name, dst in targets:
        pieces = _win_pieces() if name == "w_in" else _weight_pieces(name)
        cps += _start_loads(wrefs[LOC[name][0]], dst, pieces, sems, base)
        base += len(pieces)
    for cp in cps:
        cp.wait()


def _n_pieces(names):
    return sum(len(_win_pieces()) if n == "w_in" else NDEV for n in names)


def _dw(lhs, rhs, chunk, name, carries=()):
    nq = 4
    if lhs.ndim == 3:
        nch, t_tok, chunk = lhs.shape
        c = nch * chunk
        tq = t_tok // nq
        lhs_specs = [pl.BlockSpec((None, tq, chunk), lambda i, k=k: (i, k, 0)) for k in range(nq)]
    else:
        t_tok, c = lhs.shape
        tq = t_tok // nq
        lhs_specs = [pl.BlockSpec((tq, chunk), lambda i, k=k: (k, i)) for k in range(nq)]

    def body(*refs):
        lhs_refs, (rhs_ref, out_ref, rhs_s, sems) = refs[:nq], refs[nq:]
        cps = [pltpu.make_async_copy(rhs_ref.at[pl.ds(k * tq, tq)], rhs_s.at[pl.ds(k * tq, tq)], sems.at[k])
               for k in range(nq)]
        first = pl.program_id(0) == 0

        @pl.when(first)
        def _():
            for cp in cps:
                cp.start()

        for o in range(0, chunk, D // 2):
            n = min(D // 2, chunk - o)
            acc = None
            for k in range(nq):
                if o == 0:
                    @pl.when(first)
                    def _():
                        cps[k].wait()

                part = _tn(lhs_refs[k][:, pl.ds(o, n)], rhs_s[pl.ds(k * tq, tq), :])
                acc = part if acc is None else acc + part
            out_ref[pl.ds(o, n), :] = acc.astype(BF16)

    (out,), carried = _call(
        body, name=name, grid=(c // chunk,), ins=[lhs] * nq + [rhs],
        in_specs=lhs_specs + [ANY],
        out_specs=[pl.BlockSpec((chunk, D), lambda i: (i, 0))],
        out_shape=[jax.ShapeDtypeStruct((c, D), BF16)],
        scratch=[pltpu.VMEM((t_tok, D), BF16), DMA((nq,))], vmem=VMEM_BIG, carries=carries)
    return out, carried


def _silu_parts(a):
    sig = jax.nn.sigmoid(a)
    return sig, a * sig


FC = 256


def _ffn_fwd(h, gpre, gpost, wg, names, target=None, tm=512, carries=()):
    t_tok = h.shape[0]
    nt = t_tok // tm
    with_loss = target is not None
    groups = _groups_of(names)

    def body(*refs):
        refs = list(refs)
        h_ref, gpre_ref, gpost_ref = refs[:3]
        del refs[:3]
        tgt_ref = refs.pop(0) if with_loss else None
        wrefs = dict(zip(groups, refs[:len(groups)]))
        del refs[:len(groups)]
        hout_ref, a_ref, b_ref, f_ref, nb_ref = refs[:5]
        del refs[:5]
        if with_loss:
            dy_ref, loss_ref = refs[:2]
            del refs[:2]
        w1_s, w3_s, w2_s, sems = refs
        i = pl.program_id(0)

        @pl.when(i == 0)
        def _():
            _load_weights(wrefs, list(zip(names, (w1_s, w3_s, w2_s))), sems)
            if with_loss:
                loss_ref[...] = jnp.zeros_like(loss_ref)

        x = h_ref[...]
        n, _, _ = _rms_fwd(x, gpre_ref[...])
        nb = n.astype(BF16)
        nb_ref[...] = nb
        f = jnp.zeros((tm, D), F32)
        for c0 in range(0, FF, FC):
            a = _nt(nb, w1_s[pl.ds(c0, FC), :])
            b = _nt(nb, w3_s[pl.ds(c0, FC), :])
            _, sl = _silu_parts(a)
            a_ref[:, pl.ds(c0, FC)] = a.astype(BF16)
            b_ref[:, pl.ds(c0, FC)] = b.astype(BF16)
            f = f + _nn((sl * b).astype(BF16), w2_s[pl.ds(c0, FC), :])
        f_ref[...] = f
        fn, _, _ = _rms_fwd(f, gpost_ref[...])
        y = x + 0.5 * fn
        hout_ref[...] = y
        if with_loss:
            err = y - tgt_ref[...]
            dy_ref[...] = err * (1.0 / D)
            loss_ref[...] += jnp.sum(jnp.sum(err * err, axis=-1, keepdims=True), axis=0, keepdims=True) * (0.5 / D)

    ins = [h, gpre, gpost] + ([target] if with_loss else []) + [wg[g] for g in groups]
    in_specs = [_row_spec(tm, D), _const_spec((1, D)), _const_spec((1, D))]
    in_specs += ([_row_spec(tm, D)] if with_loss else []) + [ANY] * len(groups)
    out_shape = [jax.ShapeDtypeStruct((t_tok, D), F32), jax.ShapeDtypeStruct((t_tok, FF), BF16),
                 jax.ShapeDtypeStruct((t_tok, FF), BF16), jax.ShapeDtypeStruct((t_tok, D), F32),
                 jax.ShapeDtypeStruct((t_tok, D), BF16)]
    out_specs = [_row_spec(tm, D), _row_spec(tm, FF), _row_spec(tm, FF), _row_spec(tm, D), _row_spec(tm, D)]
    if with_loss:
        out_shape += [jax.ShapeDtypeStruct((t_tok, D), F32), jax.ShapeDtypeStruct((1, LANES), F32)]
        out_specs += [_row_spec(tm, D), _const_spec((1, LANES))]
    return _call(body, name="ffn_fwd_" + names[0][:4], grid=(nt,), ins=ins, in_specs=in_specs,
                 out_shape=out_shape, out_specs=out_specs,
                 scratch=[pltpu.VMEM((FF, D), BF16)] * 3 + [DMA((3 * NDEV,))], vmem=VMEM_BIG, carries=carries)


FH = FF // 2
HALF_PIECES = ((0, 256), (256, 256), (512, 256), (768, 256), (1024, 256), (1280, 128))


def _ffn_bwd_a(dh, f, a, b, gpost, wg, name_w2, tm=512, carries=()):
    t_tok = dh.shape[0]
    nt = t_tok // tm
    groups = _groups_of([name_w2])

    def body(dh_ref, f_ref, a_ref, b_ref, gpost_ref, wg_ref, dab_ref, s_ref, df_ref, dgp_ref, w2_s, sems):
        i = pl.program_id(0)

        @pl.when(i == 0)
        def _():
            _load_weights({groups[0]: wg_ref}, [(name_w2, w2_s)], sems)
            dgp_ref[...] = jnp.zeros_like(dgp_ref)

        fv = f_ref[...]
        _, fh, r = _rms_fwd(fv, gpost_ref[...])
        df, dg = _rms_bwd(0.5 * dh_ref[...], fh, r, gpost_ref[...])
        dgp_ref[...] += dg
        dfb = df.astype(BF16)
        df_ref[...] = dfb
        for half in range(2):
            for o, n in HALF_PIECES:
                c0 = half * FH + o
                ds = _nt(dfb, w2_s[pl.ds(c0, n), :])
                av = a_ref[:, pl.ds(c0, n)].astype(F32)
                bv = b_ref[:, pl.ds(c0, n)].astype(F32)
                sig, sl = _silu_parts(av)
                dab_ref[half, :, pl.ds(o, n)] = (ds * bv * (sig * (1.0 + av * (1.0 - sig)))).astype(BF16)
                dab_ref[2 + half, :, pl.ds(o, n)] = (ds * sl).astype(BF16)
                s_ref[half, :, pl.ds(o, n)] = (sl * bv).astype(BF16)

    return _call(
        body, name="ffn_bwd_a_" + name_w2[:4], grid=(nt,), ins=[dh, f, a, b, gpost, wg[groups[0]]],
        in_specs=[_row_spec(tm, D), _row_spec(tm, D), _row_spec(tm, FF), _row_spec(tm, FF), _const_spec((1, D)), ANY],
        out_specs=[pl.BlockSpec((4, tm, FH), lambda i: (0, i, 0)), pl.BlockSpec((2, tm, FH), lambda i: (0, i, 0)),
                   _row_spec(tm, D), _const_spec((1, D))],
        out_shape=[jax.ShapeDtypeStruct((4, t_tok, FH), BF16), jax.ShapeDtypeStruct((2, t_tok, FH), BF16),
                   jax.ShapeDtypeStruct((t_tok, D), BF16), jax.ShapeDtypeStruct((1, D), F32)],
        scratch=[pltpu.VMEM((FF, D), BF16), DMA((NDEV,))],
        vmem=VMEM_BIG, carries=carries)


def _ffn_bwd_dx(dab, h, dh, gpre, wg, name_w1, name_w3, tm=512, carries=()):
    t_tok = dh.shape[0]
    nt = t_tok // tm
    groups = _groups_of([name_w1, name_w3])

    def body(*refs):
        dab_refs, (h_ref, dh_ref, gpre_ref) = refs[:4], refs[4:7]
        wrefs = dict(zip(groups, refs[7:7 + len(groups)]))
        dhin_ref, dgp_ref, w13_s, sems = refs[7 + len(groups):]
        i = pl.program_id(0)

        @pl.when(i == 0)
        def _():
            _load_weights(wrefs, [(name_w1, w13_s.at[pl.ds(0, FF)]), (name_w3, w13_s.at[pl.ds(FF, FF)])], sems)
            dgp_ref[...] = jnp.zeros_like(dgp_ref)

        g = gpre_ref[...]
        _, xh, r = _rms_fwd(h_ref[...], g)
        dn = _nn(dab_refs[0][...], w13_s[pl.ds(0, FH), :])
        for k in range(1, 4):
            dn = dn + _nn(dab_refs[k][...], w13_s[pl.ds(k * FH, FH), :])
        dx, dg = _rms_bwd(dn, xh, r, g)
        dgp_ref[...] += dg
        dhin_ref[...] = dh_ref[...] + dx

    return _call(
        body, name="ffn_bwd_dx_" + name_w1[:4], grid=(nt,), ins=[dab] * 4 + [h, dh, gpre] + [wg[g] for g in groups],
        in_specs=[pl.BlockSpec((None, tm, FH), lambda i, k=k: (k, i, 0)) for k in range(4)]
        + [_row_spec(tm, D), _row_spec(tm, D), _const_spec((1, D))] + [ANY] * len(groups),
        out_specs=[_row_spec(tm, D), _const_spec((1, D))],
        out_shape=[jax.ShapeDtypeStruct((t_tok, D), F32), jax.ShapeDtypeStruct((1, D), F32)],
        scratch=[pltpu.VMEM((2 * FF, D), BF16), DMA((2 * NDEV,))],
        vmem=VMEM_BIG, carries=carries)


Q0, K0, V0, XR0, XG0 = 0, D, D + N_KV * LANES, D + 2 * N_KV * LANES, 2 * D + 2 * N_KV * LANES


def _mix_proj_fwd(h, g, bgate, wg, tm=512, carries=()):
    t_tok = h.shape[0]
    nt = t_tok // tm
    names = ("w_in", "w_gate")
    groups = _groups_of(names)

    def body(h_ref, g_ref, bg_ref, wg_ref, q_ref, k_ref, v_ref, xr_ref, xg_ref, gs_ref, ub_ref, win_s, wgt_s, sems):
        i = pl.program_id(0)

        @pl.when(i == 0)
        def _():
            _load_weights({groups[0]: wg_ref}, [("w_in", win_s), ("w_gate", wgt_s)], sems)

        n, _, _ = _rms_fwd(h_ref[...], g_ref[...])
        nb = n.astype(BF16)
        ub_ref[...] = nb
        q_ref[...] = _nt(nb, win_s[pl.ds(Q0, D), :]).astype(BF16)
        k_ref[...] = _nt(nb, win_s[pl.ds(K0, N_KV * LANES), :]).astype(BF16)
        v_ref[...] = _nt(nb, win_s[pl.ds(V0, N_KV * LANES), :]).astype(BF16)
        xr_ref[...] = _nt(nb, win_s[pl.ds(XR0, D), :])
        xg_ref[...] = _nt(nb, win_s[pl.ds(XG0, D), :])
        gs_ref[...] = jax.nn.sigmoid(_nt(nb, wgt_s[...]) + bg_ref[...]).astype(BF16)

    kvw = N_KV * LANES
    return _call(
        body, name="mix_proj_fwd", grid=(nt,), ins=[h, g, bgate, wg[groups[0]]],
        in_specs=[_row_spec(tm, D), _const_spec((1, D)), _const_spec((1, 2 * D)), ANY],
        out_specs=[_row_spec(tm, D), _row_spec(tm, kvw), _row_spec(tm, kvw), _row_spec(tm, D), _row_spec(tm, D),
                   _row_spec(tm, 2 * D), _row_spec(tm, D)],
        out_shape=[jax.ShapeDtypeStruct((t_tok, D), BF16), jax.ShapeDtypeStruct((t_tok, kvw), BF16),
                   jax.ShapeDtypeStruct((t_tok, kvw), BF16), jax.ShapeDtypeStruct((t_tok, D), F32),
                   jax.ShapeDtypeStruct((t_tok, D), F32), jax.ShapeDtypeStruct((t_tok, 2 * D), BF16),
                   jax.ShapeDtypeStruct((t_tok, D), BF16)],
        scratch=[pltpu.VMEM((INP_W, D), BF16), pltpu.VMEM((2 * D, D), BF16), DMA((_n_pieces(names),))],
        vmem=VMEM_BIG, carries=carries)


def _mix_proj_bwd(dqkv, dxr, dxg, dgpre, h, dh, g, wg, tm=512, carries=()):
    t_tok = h.shape[0]
    nt = t_tok // tm
    names = ("w_in", "w_gate")
    groups = _groups_of(names)

    def body(dqkv_ref, dxr_ref, dxg_ref, dgp_ref, h_ref, dh_ref, g_ref, wg_ref, dhin_ref, dg_ref, win_s, wgt_s, sems):
        i = pl.program_id(0)

        @pl.when(i == 0)
        def _():
            _load_weights({groups[0]: wg_ref}, [("w_in", win_s), ("w_gate", wgt_s)], sems)
            dg_ref[...] = jnp.zeros_like(dg_ref)

        gv = g_ref[...]
        _, xh, r = _rms_fwd(h_ref[...], gv)
        du = _nn(dgp_ref[...], wgt_s[...])
        du = du + _nn(dqkv_ref[...], win_s[pl.ds(Q0, XR0), :])
        du = du + _nn(dxr_ref[...], win_s[pl.ds(XR0, D), :])
        du = du + _nn(dxg_ref[...], win_s[pl.ds(XG0, D), :])
        dx, dg = _rms_bwd(du, xh, r, gv)
        dg_ref[...] += dg
        dhin_ref[...] = dh_ref[...] + dx

    return _call(
        body, name="mix_proj_bwd", grid=(nt,), ins=[dqkv, dxr, dxg, dgpre, h, dh, g, wg[groups[0]]],
        in_specs=[_row_spec(tm, XR0), _row_spec(tm, D), _row_spec(tm, D), _row_spec(tm, 2 * D), _row_spec(tm, D),
                  _row_spec(tm, D), _const_spec((1, D)), ANY],
        out_specs=[_row_spec(tm, D), _const_spec((1, D))],
        out_shape=[jax.ShapeDtypeStruct((t_tok, D), F32), jax.ShapeDtypeStruct((1, D), F32)],
        scratch=[pltpu.VMEM((INP_W, D), BF16), pltpu.VMEM((2 * D, D), BF16), DMA((_n_pieces(names),))],
        vmem=VMEM_BIG, carries=carries)


def _shift_down(x, d, fill):
    row = lax.broadcasted_iota(jnp.int32, x.shape, 0)
    return jnp.where(row >= d, pltpu.roll(x, d, 0), fill)


def _shift_up(x, d, fill):
    rows = x.shape[0]
    row = lax.broadcasted_iota(jnp.int32, x.shape, 0)
    return jnp.where(row < rows - d, pltpu.roll(x, rows - d, 0), fill)


def _scan_rows(a, b, reverse):
    rows = a.shape[0]
    d = 1
    while d < rows:
        if d < 8:
            shift = _shift_up if reverse else _shift_down
            b = a * shift(b, d, 0.0) + b
            a = a * shift(a, d, 1.0)
        elif reverse:
            b = jnp.concatenate([a[:rows - d] * b[d:] + b[:rows - d], b[rows - d:]], axis=0)
            a = jnp.concatenate([a[:rows - d] * a[d:], a[rows - d:]], axis=0)
        else:
            b = jnp.concatenate([b[:d], a[d:] * b[:rows - d] + b[d:]], axis=0)
            a = jnp.concatenate([a[:d], a[d:] * a[:rows - d]], axis=0)
        d *= 2
    return a, b


def _softplus(x):
    return jnp.maximum(x, 0.0) + jnp.log(1.0 + jnp.exp(-jnp.abs(x)))


_GELU_C = math.sqrt(2.0 / math.pi)


def _gelu_parts(x):
    th = jnp.tanh(_GELU_C * (x + 0.044715 * x * x * x))
    val = 0.5 * x * (1.0 + th)
    grad = 0.5 * (1.0 + th) + 0.5 * x * (1.0 - th * th) * _GELU_C * (1.0 + 3.0 * 0.044715 * x * x)
    return val, grad


def _lru_pre(x, halo, cw_ref, cb_ref, wa_ref, wx_ref, ba_ref, bx_ref, lam_ref):
    ext = jnp.concatenate([halo, x], axis=0)
    shifted = [x] + [pltpu.roll(ext, k, 0)[8:] for k in (1, 2, 3)]
    xc = cb_ref[...] + cw_ref[pl.ds(CONV_WIDTH - 1, 1), :] * x
    for k in (1, 2, 3):
        xc = xc + cw_ref[pl.ds(CONV_WIDTH - 1 - k, 1), :] * shifted[k]
    xcb = xc.astype(BF16)
    r = jax.nn.sigmoid(_nn(xcb, wa_ref[...]) + ba_ref[...])
    ig = jax.nn.sigmoid(_nn(xcb, wx_ref[...]) + bx_ref[...])
    sp = _softplus(-lam_ref[...])
    log_a = -LRU_C * r * sp
    a = jnp.exp(log_a)
    th = jnp.tanh(log_a)
    mult = jnp.sqrt(-2.0 * th / (1.0 - th))
    return shifted, xc, xcb, r, ig, sp, a, mult


def _lru_specs(nt, reverse):
    def tt(t):
        return nt - 1 - t if reverse else t
    tile = pl.BlockSpec((LRU_ROWS, LANES), lambda cb, t: (tt(t), cb))
    halo = pl.BlockSpec((8, LANES), lambda cb, t: (jnp.maximum(tt(t) * (LRU_ROWS // 8) - 1, 0), cb))
    vec = pl.BlockSpec((1, LANES), lambda cb, t: (0, cb))
    cw = pl.BlockSpec((CONV_WIDTH, LANES), lambda cb, t: (0, cb))
    mat = pl.BlockSpec((None, LANES, LANES), lambda cb, t: (cb, 0, 0))
    return tile, halo, vec, cw, mat


def _lru_fwd(xr, xg, cw, cb, wa, wx, ba, bx, lam, carries=()):
    t_tok = xr.shape[0]
    nt = t_tok // LRU_ROWS
    rows = LRU_ROWS

    def body(xr_ref, xg_ref, cw_ref, cb_ref, wa_ref, wx_ref, ba_ref, bx_ref, lam_ref, y_ref, h_ref, tail_s, hc_s):
        t = pl.program_id(1)

        @pl.when(t == 0)
        def _():
            tail_s[...] = jnp.zeros_like(tail_s)
            hc_s[...] = jnp.zeros_like(hc_s)

        x = xr_ref[...]
        _, xc, _, _, ig, _, a, mult = _lru_pre(x, tail_s[...], cw_ref, cb_ref, wa_ref, wx_ref, ba_ref, bx_ref, lam_ref)
        tail_s[...] = xr_ref[pl.ds(rows - 8, 8), :]
        acc_a, acc_b = _scan_rows(a, mult * (ig * xc), False)
        hv = acc_b + acc_a * hc_s[...]
        h_ref[...] = hv
        hc_s[...] = h_ref[pl.ds(rows - 1, 1), :]
        gl, _ = _gelu_parts(xg_ref[...])
        y_ref[...] = (hv * gl).astype(BF16)

    tile, _, vec, cws, mat = _lru_specs(nt, False)
    return _call(
        body, name="lru_fwd", grid=(D // LANES, nt), ins=[xr, xg, cw, cb, wa, wx, ba, bx, lam],
        in_specs=[tile, tile, cws, vec, mat, mat, vec, vec, vec],
        out_specs=[tile, tile],
        out_shape=[jax.ShapeDtypeStruct((t_tok, D), BF16), jax.ShapeDtypeStruct((t_tok, D), F32)],
        scratch=[pltpu.VMEM((8, LANES), F32), pltpu.VMEM((1, LANES), F32)], carries=carries)


def _lru_bwd(dy, xr, xg, hseq, cw, cb, wa, wx, ba, bx, lam, carries=()):
    t_tok = xr.shape[0]
    nt = t_tok // LRU_ROWS
    rows = LRU_ROWS

    def body(dy_ref, xr_ref, xrh_ref, xg_ref, h_ref, hh_ref, cw_ref, cb_ref, wa_ref, wx_ref, ba_ref, bx_ref, lam_ref,
             dxr_ref, dxg_ref, dvec_ref, dwa_ref, dwx_ref, gcar_s, acar_s, head_s, tmp_s):
        t = pl.program_id(1)
        first_tile = t == nt - 1

        @pl.when(t == 0)
        def _():
            gcar_s[...] = jnp.zeros_like(gcar_s)
            acar_s[...] = jnp.zeros_like(acar_s)
            head_s[...] = jnp.zeros_like(head_s)
            dvec_ref[...] = jnp.zeros_like(dvec_ref)
            dwa_ref[...] = jnp.zeros_like(dwa_ref)
            dwx_ref[...] = jnp.zeros_like(dwx_ref)

        x = xr_ref[...]
        halo = jnp.where(first_tile, 0.0, xrh_ref[...])
        shifted, xc, xcb, r, ig, sp, a, mult = _lru_pre(x, halo, cw_ref, cb_ref, wa_ref, wx_ref, ba_ref, bx_ref, lam_ref)
        hv = h_ref[...]
        dyv = dy_ref[...]
        gl, glg = _gelu_parts(xg_ref[...])
        dxg_ref[...] = (dyv * hv * glg).astype(BF16)
        acc_a, acc_b = _scan_rows(_shift_up(a, 1, acar_s[...]), dyv * gl, True)
        g = acc_b + acc_a * gcar_s[...]
        hhalo = jnp.where(first_tile, 0.0, hh_ref[...])
        hprev = pltpu.roll(jnp.concatenate([hhalo, hv], axis=0), 1, 0)[8:]
        dmult = g * ig * xc
        dlog_a = a * (g * hprev) - dmult * a * a / mult
        dig = g * mult * xc
        dxc = g * mult * ig
        dzr = (dlog_a * (-LRU_C * sp)) * r * (1.0 - r)
        dzx = dig * ig * (1.0 - ig)
        dzrb, dzxb = dzr.astype(BF16), dzx.astype(BF16)
        dxc = dxc + _nt(dzrb, wa_ref[...]) + _nt(dzxb, wx_ref[...])
        dwa_ref[...] += _tn(xcb, dzrb)
        dwx_ref[...] += _tn(xcb, dzxb)
        dsp = jnp.sum(dlog_a * (-LRU_C * r), axis=0, keepdims=True)
        dlam = dsp * (-jax.nn.sigmoid(-lam_ref[...]))
        vrow = lax.broadcasted_iota(jnp.int32, (8, LANES), 0)
        upd = jnp.where(vrow == 4, jnp.sum(dxc, axis=0, keepdims=True), 0.0)
        upd = jnp.where(vrow == 5, jnp.sum(dzr, axis=0, keepdims=True), upd)
        upd = jnp.where(vrow == 6, jnp.sum(dzx, axis=0, keepdims=True), upd)
        upd = jnp.where(vrow == 7, dlam, upd)
        for k in range(CONV_WIDTH):
            upd = jnp.where(vrow == CONV_WIDTH - 1 - k, jnp.sum(dxc * shifted[k], axis=0, keepdims=True), upd)
        dvec_ref[...] += upd
        ext = jnp.concatenate([dxc, head_s[...]], axis=0)
        dxr = cw_ref[pl.ds(CONV_WIDTH - 1, 1), :] * dxc
        for k in (1, 2, 3):
            dxr = dxr + cw_ref[pl.ds(CONV_WIDTH - 1 - k, 1), :] * pltpu.roll(ext, rows + 8 - k, 0)[:rows]
        dxr_ref[...] = dxr.astype(BF16)
        tmp_s[...] = g
        gcar_s[...] = tmp_s[pl.ds(0, 1), :]
        tmp_s[...] = a
        acar_s[...] = tmp_s[pl.ds(0, 1), :]
        tmp_s[...] = dxc
        head_s[...] = tmp_s[pl.ds(0, 8), :]

    tile, halo, vec, cws, mat = _lru_specs(nt, True)
    return _call(
        body, name="lru_bwd", grid=(D // LANES, nt), ins=[dy, xr, xr, xg, hseq, hseq, cw, cb, wa, wx, ba, bx, lam],
        in_specs=[tile, tile, halo, tile, tile, halo, cws, vec, mat, mat, vec, vec, vec],
        out_specs=[tile, tile, pl.BlockSpec((8, LANES), lambda cb, t: (0, cb)), mat, mat],
        out_shape=[jax.ShapeDtypeStruct((t_tok, D), BF16), jax.ShapeDtypeStruct((t_tok, D), BF16),
                   jax.ShapeDtypeStruct((8, D), F32), jax.ShapeDtypeStruct((D // LANES, LANES, LANES), F32),
                   jax.ShapeDtypeStruct((D // LANES, LANES, LANES), F32)],
        scratch=[pltpu.VMEM((1, LANES), F32), pltpu.VMEM((1, LANES), F32), pltpu.VMEM((8, LANES), F32),
                 pltpu.VMEM((rows, LANES), F32)], carries=carries)


def _t5_bucket_np(rel):
    nb = N_BUCKETS // 2
    max_exact = nb // 2
    ret = np.where(rel > 0, nb, 0)
    n = np.abs(rel)
    nf = np.maximum(n, 1).astype(np.float32)
    large = max_exact + (np.log(nf / np.float32(max_exact)) / np.float32(math.log(MAX_DISTANCE / max_exact))
                         * np.float32(nb - max_exact)).astype(np.int32)
    large = np.minimum(large, nb - 1)
    return ret + np.where(n < max_exact, n, large)


def _bucket_map():
    r = np.arange(QT)[:, None]
    c = np.arange(KW)[None, :]
    j = c - (r // CHUNK) * CHUNK
    band = (j >= 0) & (j < WINDOW + CHUNK)
    return np.where(band, _t5_bucket_np(c - r - WINDOW), -1).astype(np.int32)


def _attn_specs(nt, reverse):
    def tt(i):
        return nt - 1 - i if reverse else i
    kvw = N_KV * LANES
    qs = pl.BlockSpec((QT, D), lambda i: (tt(i), 0))
    cur = pl.BlockSpec((QT, kvw), lambda i: (tt(i), 0))
    prev = pl.BlockSpec((WINDOW, kvw), lambda i: (jnp.maximum(tt(i) * (QT // WINDOW) - 1, 0), 0))
    lse = pl.BlockSpec((QT, LANES), lambda i: (tt(i), 0))
    return qs, cur, prev, lse


REP = N_HEADS // N_KV
SCALE = HEAD_DIM ** -0.5


def _stack_heads(x_ref, g, lo, scale=None):
    parts = []
    for hl in range(REP):
        xs = x_ref[:, pl.ds((2 * g + hl // 2) * LANES, LANES)]
        xs = jnp.where(lo if hl % 2 == 0 else jnp.logical_not(lo), xs, jnp.zeros_like(xs))
        parts.append(xs if scale is None else xs * jnp.asarray(scale, xs.dtype))
    return jnp.concatenate(parts, axis=0)


def _stack_sinks(sink_ref, g, srow):
    sk = jnp.full(srow.shape, sink_ref[REP * g + REP - 1], F32)
    for hl in range(REP - 2, -1, -1):
        sk = jnp.where(srow < (hl + 1) * QT, sink_ref[REP * g + hl], sk)
    return sk


def _attn_fwd(q, kd, vd, bias, sinks, carries=()):
    t_tok = q.shape[0]
    nt = t_tok // QT

    def body(q_ref, kp_ref, kc_ref, vp_ref, vc_ref, bias_ref, sink_ref, o_ref, lse_ref):
        i = pl.program_id(0)
        col = lax.broadcasted_iota(jnp.int32, (1, KW), 1)
        first = jnp.where((i == 0) & (col < WINDOW), NEG_INF, 0.0)
        lane = lax.broadcasted_iota(jnp.int32, (QT, LANES), 1)
        lo = lane < HEAD_DIM
        srow = lax.broadcasted_iota(jnp.int32, (REP * QT, 1), 0)
        lse_t = jnp.zeros((QT, LANES), F32)
        for g in range(N_KV):
            kwin = jnp.concatenate([kp_ref[:, pl.ds(g * LANES, LANES)], kc_ref[:, pl.ds(g * LANES, LANES)]], axis=0)
            vwin = jnp.concatenate([vp_ref[:, pl.ds(g * LANES, LANES)], vc_ref[:, pl.ds(g * LANES, LANES)]], axis=0)
            qst = _stack_heads(q_ref, g, lo, SCALE)
            s = _nt(qst, kwin) + (bias_ref[g] + first)
            sk = _stack_sinks(sink_ref, g, srow)
            m = jnp.maximum(jnp.max(s, axis=-1, keepdims=True), sk)
            e = jnp.exp(s - m)
            l = jnp.sum(e, axis=-1, keepdims=True) + jnp.exp(sk - m)
            p = e / l
            ost = _nn(p.astype(BF16), vwin)
            lse_s = m + jnp.log(l)
            for sl in range(2):
                o_ref[:, pl.ds((2 * g + sl) * LANES, LANES)] = jnp.where(
                    lo, ost[2 * sl * QT:(2 * sl + 1) * QT], ost[(2 * sl + 1) * QT:(2 * sl + 2) * QT]).astype(BF16)
            for hl in range(REP):
                lse_t = jnp.where(lane == REP * g + hl, lse_s[hl * QT:(hl + 1) * QT], lse_t)
        lse_ref[...] = lse_t

    qs, cur, prev, lse = _attn_specs(nt, False)
    return _call(
        body, name="attn_fwd", grid=(nt,), ins=[q, kd, kd, vd, vd, bias, sinks],
        in_specs=[qs, prev, cur, prev, cur, _const_spec((N_KV, REP * QT, KW)), pl.BlockSpec(memory_space=pltpu.SMEM)],
        out_specs=[qs, lse],
        out_shape=[jax.ShapeDtypeStruct((t_tok, D), BF16), jax.ShapeDtypeStruct((t_tok, LANES), F32)],
        vmem=48 * 2 ** 20, carries=carries)


def _attn_bwd(q, kd, vd, o, do, lse, bias, sinks, carries=()):
    t_tok = q.shape[0]
    nt = t_tok // QT
    kvw = N_KV * LANES

    def body(q_ref, kp_ref, kc_ref, vp_ref, vc_ref, o_ref, do_ref, lse_ref, bias_ref, sink_ref,
             dqkv_ref, ds_ref, dsink_ref, kcar_s, vcar_s):
        i = pl.program_id(0)
        tile = nt - 1 - i

        @pl.when(i == 0)
        def _():
            kcar_s[...] = jnp.zeros_like(kcar_s)
            vcar_s[...] = jnp.zeros_like(vcar_s)
            ds_ref[...] = jnp.zeros_like(ds_ref)
            dsink_ref[...] = jnp.zeros_like(dsink_ref)

        col = lax.broadcasted_iota(jnp.int32, (1, KW), 1)
        first = jnp.where((tile == 0) & (col < WINDOW), NEG_INF, 0.0)
        lane = lax.broadcasted_iota(jnp.int32, (QT, LANES), 1)
        lo = lane < HEAD_DIM
        lane_k = lax.broadcasted_iota(jnp.int32, (KW, LANES), 1)
        lane_1 = lax.broadcasted_iota(jnp.int32, (1, LANES), 1)
        srow = lax.broadcasted_iota(jnp.int32, (REP * QT, 1), 0)
        lse_t = lse_ref[...]
        dsink = jnp.zeros((1, LANES), F32)
        for g in range(N_KV):
            kwin = jnp.concatenate([kp_ref[:, pl.ds(g * LANES, LANES)], kc_ref[:, pl.ds(g * LANES, LANES)]], axis=0)
            vwin = jnp.concatenate([vp_ref[:, pl.ds(g * LANES, LANES)], vc_ref[:, pl.ds(g * LANES, LANES)]], axis=0)
            qst = _stack_heads(q_ref, g, lo, SCALE)
            dost = _stack_heads(do_ref, g, lo)
            od = [do_ref[:, pl.ds((2 * g + sl) * LANES, LANES)].astype(F32)
                  * o_ref[:, pl.ds((2 * g + sl) * LANES, LANES)].astype(F32) for sl in range(2)]
            drow = jnp.concatenate([jnp.sum(jnp.where(lo if hl % 2 == 0 else jnp.logical_not(lo), od[hl // 2], 0.0),
                                            axis=-1, keepdims=True) for hl in range(REP)], axis=0)
            lse_s = jnp.concatenate([jnp.sum(jnp.where(lane == REP * g + hl, lse_t, 0.0), axis=-1, keepdims=True)
                                     for hl in range(REP)], axis=0)
            s = _nt(qst, kwin) + (bias_ref[g] + first)
            p = jnp.exp(s - lse_s)
            ds = p * (_nt(dost, vwin) - drow)
            ds_ref[g] += ds
            tsink = -(jnp.exp(_stack_sinks(sink_ref, g, srow) - lse_s) * drow)
            for hl in range(REP):
                dsink = dsink + jnp.where(lane_1 == REP * g + hl,
                                          jnp.sum(tsink[hl * QT:(hl + 1) * QT], axis=0, keepdims=True), 0.0)
            dsb = ds.astype(BF16)
            dqst = _nn(dsb, kwin) * SCALE
            for sl in range(2):
                dqkv_ref[:, pl.ds((2 * g + sl) * LANES, LANES)] = jnp.where(
                    lo, dqst[2 * sl * QT:(2 * sl + 1) * QT], dqst[(2 * sl + 1) * QT:(2 * sl + 2) * QT]).astype(BF16)
            dk_acc = _tn(dsb, qst)
            dv_acc = _tn(p.astype(BF16), dost)
            dk_f = jnp.where(lane_k < HEAD_DIM, dk_acc + pltpu.roll(dk_acc, HEAD_DIM, 1), 0.0)
            dv_f = jnp.where(lane_k < HEAD_DIM, dv_acc + pltpu.roll(dv_acc, HEAD_DIM, 1), 0.0)
            for acc, col0, car in ((dk_f, K0, kcar_s), (dv_f, V0, vcar_s)):
                cs = pl.ds(g * LANES, LANES)
                co = pl.ds(col0 + g * LANES, LANES)
                if QT > WINDOW:
                    dqkv_ref[pl.ds(0, QT - WINDOW), co] = acc[WINDOW:QT].astype(BF16)
                dqkv_ref[pl.ds(QT - WINDOW, WINDOW), co] = (acc[QT:KW] + car[:, cs]).astype(BF16)
                car[:, cs] = acc[0:WINDOW]
        dsink_ref[...] += dsink

    qs, cur, prev, lse_s = _attn_specs(nt, True)
    return _call(
        body, name="attn_bwd", grid=(nt,), ins=[q, kd, kd, vd, vd, o, do, lse, bias, sinks],
        in_specs=[qs, prev, cur, prev, cur, qs, qs, lse_s, _const_spec((N_KV, REP * QT, KW)),
                  pl.BlockSpec(memory_space=pltpu.SMEM)],
        out_specs=[pl.BlockSpec((QT, XR0), lambda i: (nt - 1 - i, 0)), _const_spec((N_KV, REP * QT, KW)),
                   _const_spec((1, LANES))],
        out_shape=[jax.ShapeDtypeStruct((t_tok, XR0), BF16), jax.ShapeDtypeStruct((N_KV, REP * QT, KW), F32),
                   jax.ShapeDtypeStruct((1, LANES), F32)],
        scratch=[pltpu.VMEM((WINDOW, kvw), F32), pltpu.VMEM((WINDOW, kvw), F32)],
        vmem=VMEM_BIG, carries=carries)


def _bias_tile(table, bmap):
    def body(tab_ref, bm_ref, out_ref):
        bm = bm_ref[...]

        def per_head(hd, carry):
            acc = jnp.full((QT, KW), NEG_INF, F32)
            for b in range(N_BUCKETS):
                acc = jnp.where(bm == b, tab_ref[b, hd], acc)
            out_ref[hd] = acc
            return carry

        lax.fori_loop(0, N_HEADS, per_head, 0)

    return pl.pallas_call(
        body, name="bias_tile", out_shape=jax.ShapeDtypeStruct((N_HEADS, QT, KW), F32),
        in_specs=[pl.BlockSpec(memory_space=pltpu.SMEM), pl.BlockSpec(memory_space=pltpu.VMEM)],
        out_specs=pl.BlockSpec(memory_space=pltpu.VMEM))(table, bmap)


def _bias_grad(ds_acc, bmap):
    def body(ds_ref, bm_ref, out_ref):
        row = lax.broadcasted_iota(jnp.int32, (N_BUCKETS, LANES), 0)
        lane = lax.broadcasted_iota(jnp.int32, (N_BUCKETS, LANES), 1)
        bm = bm_ref[...]

        def per_head(hd, res):
            dsv = ds_ref[hd]
            for b in range(N_BUCKETS):
                val = jnp.sum(jnp.sum(jnp.where(bm == b, dsv, 0.0), axis=0, keepdims=True), axis=1, keepdims=True)
                res = jnp.where((row == b) & (lane == hd), val, res)
            return res

        out_ref[...] = lax.fori_loop(0, N_HEADS, per_head, jnp.zeros((N_BUCKETS, LANES), F32))

    return pl.pallas_call(
        body, name="bias_grad", out_shape=jax.ShapeDtypeStruct((N_BUCKETS, LANES), F32),
        in_specs=[pl.BlockSpec(memory_space=pltpu.VMEM), pl.BlockSpec(memory_space=pltpu.VMEM)],
        out_specs=pl.BlockSpec(memory_space=pltpu.VMEM))(ds_acc, bmap)


MIXOUT = ("w_lru_out", "w_attn_out", "w_o")


def _mix_out_fwd(ya_in, o, gs, h, g, wg, tm=512, carries=()):
    t_tok = h.shape[0]
    nt = t_tok // tm
    groups = _groups_of(MIXOUT)

    def body(ya_ref, o_ref, gs_ref, h_ref, g_ref, wg_ref, hout_ref, yao_ref, ybo_ref, z_ref, wa_s, wb_s, wo_s, sems):
        i = pl.program_id(0)

        @pl.when(i == 0)
        def _():
            _load_weights({groups[0]: wg_ref}, list(zip(MIXOUT, (wa_s, wb_s, wo_s))), sems)

        ya = _nn(ya_ref[...], wa_s[...])
        yb = _nn(o_ref[...], wb_s[...])
        yao_ref[...] = ya.astype(BF16)
        ybo_ref[...] = yb.astype(BF16)
        merged = gs_ref[:, pl.ds(0, D)].astype(F32) * ya + gs_ref[:, pl.ds(D, D)].astype(F32) * yb
        z = _nn(merged.astype(BF16), wo_s[...])
        z_ref[...] = z
        zn, _, _ = _rms_fwd(z, g_ref[...])
        hout_ref[...] = h_ref[...] + zn

    return _call(
        body, name="mix_out_fwd", grid=(nt,), ins=[ya_in, o, gs, h, g, wg[groups[0]]],
        in_specs=[_row_spec(tm, D), _row_spec(tm, D), _row_spec(tm, 2 * D), _row_spec(tm, D), _const_spec((1, D)), ANY],
        out_specs=[_row_spec(tm, D)] * 4,
        out_shape=[jax.ShapeDtypeStruct((t_tok, D), F32), jax.ShapeDtypeStruct((t_tok, D), BF16),
                   jax.ShapeDtypeStruct((t_tok, D), BF16), jax.ShapeDtypeStruct((t_tok, D), F32)],
        scratch=[pltpu.VMEM((D, D), BF16)] * 3 + [DMA((3 * NDEV,))],
        vmem=48 * 2 ** 20, carries=carries)


def _mix_out_bwd(dh, z, ya, yb, gs, g, wg, tm=512, carries=()):
    t_tok = dh.shape[0]
    nt = t_tok // tm
    groups = _groups_of(MIXOUT)

    def body(dh_ref, z_ref, ya_ref, yb_ref, gs_ref, g_ref, wg_ref,
             dyain_ref, do_ref, dgpre_ref, dya_ref, dyb_ref, mg_ref, dz_ref, dg_ref, dbg_ref,
             wa_s, wb_s, wo_s, sems):
        i = pl.program_id(0)

        @pl.when(i == 0)
        def _():
            _load_weights({groups[0]: wg_ref}, list(zip(MIXOUT, (wa_s, wb_s, wo_s))), sems)
            dg_ref[...] = jnp.zeros_like(dg_ref)
            dbg_ref[...] = jnp.zeros_like(dbg_ref)

        gv = g_ref[...]
        _, zh, r = _rms_fwd(z_ref[...], gv)
        dz, dg = _rms_bwd(dh_ref[...], zh, r, gv)
        dg_ref[...] += dg
        dzb = dz.astype(BF16)
        dz_ref[...] = dzb
        ga, gb = gs_ref[:, pl.ds(0, D)].astype(F32), gs_ref[:, pl.ds(D, D)].astype(F32)
        ya_v, yb_v = ya_ref[...].astype(F32), yb_ref[...].astype(F32)
        mg_ref[...] = (ga * ya_v + gb * yb_v).astype(BF16)
        dm = _nt(dzb, wo_s[...])
        dga = dm * ya_v * ga * (1.0 - ga)
        dgb = dm * yb_v * gb * (1.0 - gb)
        dgpre_ref[:, pl.ds(0, D)] = dga.astype(BF16)
        dgpre_ref[:, pl.ds(D, D)] = dgb.astype(BF16)
        dbg_ref[:, pl.ds(0, D)] += jnp.sum(dga, axis=0, keepdims=True)
        dbg_ref[:, pl.ds(D, D)] += jnp.sum(dgb, axis=0, keepdims=True)
        dya = (dm * ga).astype(BF16)
        dyb = (dm * gb).astype(BF16)
        dya_ref[...] = dya
        dyb_ref[...] = dyb
        dyain_ref[...] = _nt(dya, wa_s[...])
        do_ref[...] = _nt(dyb, wb_s[...]).astype(BF16)

    bf = jax.ShapeDtypeStruct((t_tok, D), BF16)
    return _call(
        body, name="mix_out_bwd", grid=(nt,), ins=[dh, z, ya, yb, gs, g, wg[groups[0]]],
        in_specs=[_row_spec(tm, D)] * 4 + [_row_spec(tm, 2 * D), _const_spec((1, D)), ANY],
        out_specs=[_row_spec(tm, D), _row_spec(tm, D), _row_spec(tm, 2 * D)] + [_row_spec(tm, D)] * 4
        + [_const_spec((1, D)), _const_spec((1, 2 * D))],
        out_shape=[jax.ShapeDtypeStruct((t_tok, D), F32), bf, jax.ShapeDtypeStruct((t_tok, 2 * D), BF16), bf, bf, bf, bf,
                   jax.ShapeDtypeStruct((1, D), F32), jax.ShapeDtypeStruct((1, 2 * D), F32)],
        scratch=[pltpu.VMEM((D, D), BF16)] * 3 + [DMA((3 * NDEV,))],
        vmem=VMEM_BIG, carries=carries)


def _sum_parts(parts_list):
    n = len(parts_list)
    _, r, c = parts_list[0].shape
    tc = 256

    def body(*refs):
        for p_ref, o_ref in zip(refs[:n], refs[n:]):
            acc = p_ref[0].astype(F32)
            for s in range(1, NDEV):
                acc = acc + p_ref[s].astype(F32)
            o_ref[...] = acc

    return pl.pallas_call(
        body, name=f"sum_parts_{r}", grid=(c // tc,),
        in_specs=[pl.BlockSpec((NDEV, r, tc), lambda i: (0, 0, i))] * n,
        out_specs=[pl.BlockSpec((r, tc), lambda i: (0, i))] * n,
        out_shape=[jax.ShapeDtypeStruct((r, c), F32)] * n,
        compiler_params=pltpu.CompilerParams(dimension_semantics=("arbitrary",), vmem_limit_bytes=48 * 2 ** 20),
    )(*parts_list)


def _adamw_math(w, g, m, v):
    m = ADAM_B1 * m + (1.0 - ADAM_B1) * g
    v = ADAM_B2 * v + (1.0 - ADAM_B2) * (g * g)
    m_hat = m / (1.0 - ADAM_B1 ** ADAM_STEP)
    v_hat = v / (1.0 - ADAM_B2 ** ADAM_STEP)
    delta = -ADAM_LR * (m_hat / (jnp.sqrt(v_hat) + ADAM_EPS) + ADAM_WD * w)
    return delta, m, v


def _sum_ready(parts_list, name, carries=()):
    n = len(parts_list)
    c = parts_list[0].shape[2]
    tc = 2 * LANES

    def body(*refs):
        for p_ref, o_ref in zip(refs[:n], refs[n:]):
            g = p_ref[0].astype(F32)
            for s in range(1, NDEV):
                g = g + p_ref[s].astype(F32)
            o_ref[...] = g

    return _call(
        body, name=name, grid=(c // tc,), ins=list(parts_list),
        in_specs=[pl.BlockSpec((NDEV, p.shape[1], tc), lambda i: (0, 0, i)) for p in parts_list],
        out_shape=[jax.ShapeDtypeStruct((p.shape[1], c), F32) for p in parts_list],
        out_specs=[pl.BlockSpec((p.shape[1], tc), lambda i: (0, i)) for p in parts_list],
        vmem=48 * 2 ** 20, carries=carries)


def _adamw_cols(items, name):
    n = len(items)
    c = items[0][1].shape[1]
    tc = LANES

    def body(*refs):
        for k in range(n):
            g_ref, w_ref, m_ref, v_ref = refs[4 * k:4 * k + 4]
            go_ref, d_ref, nm_ref, nv_ref = refs[4 * n + 4 * k:4 * n + 4 * k + 4]
            g = g_ref[...]
            d, m, v = _adamw_math(w_ref[...], g, m_ref[...], v_ref[...])
            go_ref[...] = g
            d_ref[...] = d
            nm_ref[...] = m
            nv_ref[...] = v

    ins, specs, out_shape = [], [], []
    for g, w, m, v in items:
        r = w.shape[0]
        ins += [g, w, m, v]
        specs += [pl.BlockSpec((r, tc), lambda i: (0, i))] * 4
        out_shape += [jax.ShapeDtypeStruct((r, c), F32)] * 4
    outs, _ = _call(body, name=name, grid=(c // tc,), ins=ins, in_specs=specs, out_shape=out_shape,
                    out_specs=specs, vmem=48 * 2 ** 20)
    return [tuple(outs[4 * k:4 * k + 4]) for k in range(n)]


def _adamw_body(n):
    def body(*refs):
        for k in range(n):
            g_ref, w_ref, m_ref, v_ref = refs[4 * k:4 * k + 4]
            d_ref, nm_ref, nv_ref = refs[4 * n + 3 * k:4 * n + 3 * k + 3]
            d, m, v = _adamw_math(w_ref[...], g_ref[...], m_ref[...], v_ref[...])
            d_ref[...] = d
            nm_ref[...] = m
            nv_ref[...] = v
    return body


def _adamw(items):
    n = len(items)
    r, c = items[0][0].shape
    tr = r if r * c <= 2 ** 18 else max(t for t in range(8, 65, 8) if r % t == 0)
    spec = pl.BlockSpec((tr, c), lambda i: (i, 0))
    outs = pl.pallas_call(
        _adamw_body(n), name=f"adamw_{r}x{c}", grid=(r // tr,),
        in_specs=[spec] * (4 * n), out_specs=[spec] * (3 * n),
        out_shape=[jax.ShapeDtypeStruct((r, c), F32)] * (3 * n),
        compiler_params=pltpu.CompilerParams(dimension_semantics=("arbitrary",), vmem_limit_bytes=40 * 2 ** 20),
    )(*[a for it in items for a in it])
    return [tuple(outs[3 * k:3 * k + 3]) for k in range(n)]


def _adamw_small(items):
    n = len(items)
    vm = pl.BlockSpec(memory_space=pltpu.VMEM)
    outs = pl.pallas_call(
        _adamw_body(n), name="adamw_small", in_specs=[vm] * (4 * n), out_specs=[vm] * (3 * n),
        out_shape=[jax.ShapeDtypeStruct(it[1].shape, F32) for it in items for _ in range(3)],
    )(*[a for it in items for a in it])
    return [tuple(outs[3 * k:3 * k + 3]) for k in range(n)]


def _pack_small(arrs):
    rows, offs = [], []
    total = 0
    for a in arrs:
        flat = a.reshape(-1).astype(F32)
        nr = -(-flat.shape[0] // LANES)
        flat = jnp.pad(flat, (0, nr * LANES - flat.shape[0]))
        rows.append(flat.reshape(nr, LANES))
        offs.append((total, nr))
        total += nr
    pad = -total % 8
    if pad:
        rows.append(jnp.zeros((pad, LANES), F32))
    return jnp.concatenate(rows, axis=0), offs


def _unpack_small(pack, offs, shapes):
    out = []
    for (o, nr), shp in zip(offs, shapes):
        size = int(np.prod(shp))
        out.append(pack[o:o + nr].reshape(-1)[:size].reshape(shp))
    return out


def _sum_small(gathered, rows):
    def body(p_ref, o_ref):
        acc = p_ref[pl.ds(0, rows), :]
        for s in range(1, NDEV):
            acc = acc + p_ref[pl.ds(s * rows, rows), :]
        o_ref[...] = acc

    return pl.pallas_call(
        body, name="sum_small", out_shape=jax.ShapeDtypeStruct((rows, LANES), F32),
        in_specs=[pl.BlockSpec(memory_space=pltpu.VMEM)], out_specs=pl.BlockSpec(memory_space=pltpu.VMEM))(gathered)


def _block_diag(w):
    w = w.reshape(D // LANES, 2, LRU_BLOCK, LRU_BLOCK)
    z = jnp.zeros((D // LANES, LRU_BLOCK, LRU_BLOCK), w.dtype)
    top = jnp.concatenate([w[:, 0], z], axis=2)
    bot = jnp.concatenate([z, w[:, 1]], axis=2)
    return jnp.concatenate([top, bot], axis=1)


def _block_diag_grad(dw):
    a = dw[:, :LRU_BLOCK, :LRU_BLOCK]
    b = dw[:, LRU_BLOCK:, LRU_BLOCK:]
    return jnp.stack([a, b], axis=1).reshape(D // LRU_BLOCK, LRU_BLOCK, LRU_BLOCK)


def kernel(x, ffn1_pre_g, ffn1_w1, ffn1_w3, ffn1_w2, ffn1_post_g, mix_pre_g, w_in, conv_w, conv_b, rg_a_w, rg_a_b, rg_x_w, rg_x_b, lru_lambda, w_lru_out, attn_sinks, rel_bias, w_attn_out, w_gate, b_gate, w_o, mix_post_g, ffn2_pre_g, ffn2_w1, ffn2_w3, ffn2_w2, ffn2_post_g, loss_target, m_ffn1_pre_g, m_ffn1_w1, m_ffn1_w3, m_ffn1_w2, m_ffn1_post_g, m_mix_pre_g, m_w_in, m_conv_w, m_conv_b, m_rg_a_w, m_rg_a_b, m_rg_x_w, m_rg_x_b, m_lru_lambda, m_w_lru_out, m_attn_sinks, m_rel_bias, m_w_attn_out, m_w_gate, m_b_gate, m_w_o, m_mix_post_g, m_ffn2_pre_g, m_ffn2_w1, m_ffn2_w3, m_ffn2_w2, m_ffn2_post_g, v_ffn1_pre_g, v_ffn1_w1, v_ffn1_w3, v_ffn1_w2, v_ffn1_post_g, v_mix_pre_g, v_w_in, v_conv_w, v_conv_b, v_rg_a_w, v_rg_a_b, v_rg_x_w, v_rg_x_b, v_lru_lambda, v_w_lru_out, v_attn_sinks, v_rel_bias, v_w_attn_out, v_w_gate, v_b_gate, v_w_o, v_mix_post_g, v_ffn2_pre_g, v_ffn2_w1, v_ffn2_w3, v_ffn2_w2, v_ffn2_post_g):
    names = ["ffn1_pre_g", "ffn1_w1", "ffn1_w3", "ffn1_w2", "ffn1_post_g", "mix_pre_g", "w_in", "conv_w", "conv_b",
             "rg_a_w", "rg_a_b", "rg_x_w", "rg_x_b", "lru_lambda", "w_lru_out", "attn_sinks", "rel_bias", "w_attn_out",
             "w_gate", "b_gate", "w_o", "mix_post_g", "ffn2_pre_g", "ffn2_w1", "ffn2_w3", "ffn2_w2", "ffn2_post_g"]
    ws = dict(zip(names, (ffn1_pre_g, ffn1_w1, ffn1_w3, ffn1_w2, ffn1_post_g, mix_pre_g, w_in, conv_w, conv_b, rg_a_w,
                          rg_a_b, rg_x_w, rg_x_b, lru_lambda, w_lru_out, attn_sinks, rel_bias, w_attn_out, w_gate,
                          b_gate, w_o, mix_post_g, ffn2_pre_g, ffn2_w1, ffn2_w3, ffn2_w2, ffn2_post_g)))
    ms = dict(zip(names, (m_ffn1_pre_g, m_ffn1_w1, m_ffn1_w3, m_ffn1_w2, m_ffn1_post_g, m_mix_pre_g, m_w_in, m_conv_w,
                          m_conv_b, m_rg_a_w, m_rg_a_b, m_rg_x_w, m_rg_x_b, m_lru_lambda, m_w_lru_out, m_attn_sinks,
                          m_rel_bias, m_w_attn_out, m_w_gate, m_b_gate, m_w_o, m_mix_post_g, m_ffn2_pre_g, m_ffn2_w1,
                          m_ffn2_w3, m_ffn2_w2, m_ffn2_post_g)))
    vs = dict(zip(names, (v_ffn1_pre_g, v_ffn1_w1, v_ffn1_w3, v_ffn1_w2, v_ffn1_post_g, v_mix_pre_g, v_w_in, v_conv_w,
                          v_conv_b, v_rg_a_w, v_rg_a_b, v_rg_x_w, v_rg_x_b, v_lru_lambda, v_w_lru_out, v_attn_sinks,
                          v_rel_bias, v_w_attn_out, v_w_gate, v_b_gate, v_w_o, v_mix_post_g, v_ffn2_pre_g, v_ffn2_w1,
                          v_ffn2_w3, v_ffn2_w2, v_ffn2_post_g)))
    me = 4 * lax.axis_index("x") + 2 * lax.axis_index("y") + lax.axis_index("c")
    vec = lambda n: ws[n].reshape(1, -1)

    def shard2d(name):
        if name == "conv":
            row = lax.bitcast_convert_type(conv_w.reshape(CONV_WIDTH, LANES), BF16).reshape(1, D)
            return jnp.concatenate([row, jnp.zeros((LOC["conv"][2] - 1, D), BF16)], axis=0)
        a = ws[name].reshape(ws[name].shape[-2], ws[name].shape[-1])
        return (a.T if name in COL_SHARDED else a).astype(BF16)

    packs = {g: jnp.concatenate([shard2d(n) for n, _ in members], axis=0) for g, members in GROUPS}

    wg = {}
    _, ((wg["ffn1"],),) = _call(None, name="allgather_ffn1", grid=(), ins=[], in_specs=[], out_shape=[], out_specs=[],
                                carries=[_AllGather(packs["ffn1"])])
    conv_rows = wg["ffn1"].reshape(NDEV, GROUP_ROWS["ffn1"], D)[:, LOC["conv"][1]]
    cw = jnp.transpose(lax.bitcast_convert_type(conv_rows.reshape(NDEV, CONV_WIDTH, LANES, 2), F32),
                       (1, 0, 2)).reshape(CONV_WIDTH, D)
    bmap = jnp.asarray(_bucket_map())
    bias = _bias_tile(rel_bias, bmap).reshape(N_KV, REP * QT, KW)
    sinks = attn_sinks.reshape(N_HEADS)
    wa_bd = _block_diag(rg_a_w.reshape(D // LRU_BLOCK, LRU_BLOCK, LRU_BLOCK)).astype(BF16)
    wx_bd = _block_diag(rg_x_w.reshape(D // LRU_BLOCK, LRU_BLOCK, LRU_BLOCK)).astype(BF16)
    lru_args = (cw, vec("conv_b"), wa_bd, wx_bd, vec("rg_a_b"), vec("rg_x_b"), vec("lru_lambda"))
    x2, tgt = x[0], loss_target[0]

    (h1, a1, b1, f1, nb1), ((wg["mixin"],),) = _ffn_fwd(
        x2, vec("ffn1_pre_g"), vec("ffn1_post_g"), wg, ("ffn1_w1", "ffn1_w3", "ffn1_w2"),
        carries=[_AllGather(packs["mixin"])])
    (q, kd, vd, xr, xg, gs, ub), ((wg["mixout"],),) = _mix_proj_fwd(
        h1, vec("mix_pre_g"), vec("b_gate"), wg, carries=[_AllGather(packs["mixout"])])
    (ya_in, hseq), ((wg["ffn2a"],),) = _lru_fwd(xr, xg, *lru_args, carries=[_AllGather(packs["ffn2a"])])
    (o, lse), ((wg["ffn2b"],),) = _attn_fwd(q, kd, vd, bias, sinks, carries=[_AllGather(packs["ffn2b"])])
    (h2, ya, yb, z), _ = _mix_out_fwd(ya_in, o, gs, h1, vec("mix_post_g"), wg)
    (_, a2, b2, f2, nb2, dy, loss_part), _ = _ffn_fwd(h2, vec("ffn2_pre_g"), vec("ffn2_post_g"), wg,
                                                     ("ffn2_w1", "ffn2_w3", "ffn2_w2"), target=tgt)

    gsm, parts = {}, {}
    rs = lambda *grads_: [_ReduceScatterSend(list(grads_))]
    ffr = FF // NDEV
    (dab, s_act, dfb, gsm["ffn2_post_g"]), _ = _ffn_bwd_a(dy, f2, a2, b2, vec("ffn2_post_g"), wg, "ffn2_w2")
    g_w2, _ = _dw(s_act, dfb, FF // 2, "dw_ffn2_w2")
    g_w13, ((parts["ffn2_w2"],),) = _dw(dab, nb2, D // 2, "dw_ffn2_w13", carries=rs(g_w2))
    (dh2, gsm["ffn2_pre_g"]), ((parts["ffn2_w1"],),) = _ffn_bwd_dx(
        dab, h2, dy, vec("ffn2_pre_g"), wg, "ffn2_w1", "ffn2_w3", carries=rs((g_w13, 0, ffr, ffr)))
    (dya_in, do, dgpre, dya, dyb, mg, dzb, gsm["mix_post_g"], gsm["b_gate"]), ((parts["ffn2_w3"],),) = _mix_out_bwd(
        dh2, z, ya, yb, gs, vec("mix_post_g"), wg, carries=rs((g_w13, FF, ffr, ffr)))
    g_wa, _ = _dw(ya_in, dya, D // 2, "dw_w_lru_out")
    g_wb, _ = _dw(o, dyb, D // 2, "dw_w_attn_out")
    g_wo, _ = _dw(mg, dzb, D // 2, "dw_w_o")
    g_wgate, _ = _dw(dgpre, ub, D // 2, "dw_w_gate")
    (dqkv, ds_acc, dsink), (mixout_parts,) = _attn_bwd(
        q, kd, vd, o, do, lse, bias, sinks, carries=rs(g_wa, g_wb, g_wo, g_wgate))
    parts["w_lru_out"], parts["w_attn_out"], parts["w_o"], parts["w_gate"] = mixout_parts
    (dxr, dxg, dvec, dwa, dwx), _ = _lru_bwd(dya_in, xr, xg, hseq, *lru_args)
    g_qkv, _ = _dw(dqkv, ub, D // 2, "dw_w_in_qkv")
    g_xr, _ = _dw(dxr, ub, D // 2, "dw_w_in_xr")
    g_xg, _ = _dw(dxg, ub, D // 2, "dw_w_in_xg")
    g_win = jnp.concatenate(
        [g_qkv[:D]] + [g_qkv[c0 + gi * LANES:c0 + gi * LANES + HEAD_DIM] for c0 in (K0, V0) for gi in range(N_KV)]
        + [g_xr, g_xg], axis=0)
    wir, wir_a = IN_W // NDEV, 336
    (dh1, gsm["mix_pre_g"]), ((win_a,),) = _mix_proj_bwd(
        dqkv, dxr, dxg, dgpre, h1, dh2, vec("mix_pre_g"), wg, carries=rs((g_win, 0, wir, wir_a)))
    gsm["conv_w"] = dvec[0:CONV_WIDTH]
    gsm["conv_b"], gsm["rg_a_b"], gsm["rg_x_b"], gsm["lru_lambda"] = dvec[4], dvec[5], dvec[6], dvec[7]
    gsm["rg_a_w"] = _block_diag_grad(dwa)
    gsm["rg_x_w"] = _block_diag_grad(dwx)
    gsm["attn_sinks"] = dsink[0, :N_HEADS]
    gsm["rel_bias"] = _bias_grad(ds_acc.reshape(N_HEADS, QT, KW), bmap)[:, :N_HEADS]
    late = ("ffn1_post_g", "ffn1_pre_g")
    early = tuple(n for n in SMALL if n not in late)
    early_pack, early_offs = _pack_small([gsm[n] for n in early])
    (dab, s_act, dfb, gsm["ffn1_post_g"]), ((win_b,), (early_all,)) = _ffn_bwd_a(
        dh1, f1, a1, b1, vec("ffn1_post_g"), wg, "ffn1_w2",
        carries=rs((g_win, wir_a, wir, wir - wir_a)) + [_AllGather(early_pack)])
    parts["w_in"] = jnp.concatenate([win_a, win_b], axis=1)
    g_w2, _ = _dw(s_act, dfb, FF // 2, "dw_ffn1_w2")
    g_w13, ((parts["ffn1_w2"],),) = _dw(dab, nb1, D // 2, "dw_ffn1_w13", carries=rs(g_w2))
    head = 64
    (grad_x, gsm["ffn1_pre_g"]), ((parts["ffn1_w1"], w3_head),) = _ffn_bwd_dx(
        dab, x2, dh1, vec("ffn1_pre_g"), wg, "ffn1_w1", "ffn1_w3",
        carries=rs((g_w13, 0, ffr, ffr), (g_w13, FF, ffr, head)))
    late_pack, late_offs = _pack_small([gsm[n] for n in late] + [loss_part])

    grads, delta, new_m, new_v = {}, {}, {}, {}
    t_form = [n for n in COL_SHARDED if ws[n].shape[-1] % LANES]
    view = {n: (lambda a: a.reshape(a.shape[-2], a.shape[-1]).T) if n in t_form
            else (lambda a: a.reshape(a.shape[-2], a.shape[-1])) for n in BIG}
    unview = {n: (lambda a: a.T) if n in t_form else (lambda a: a) for n in BIG}

    ready = [n for n in BIG if n not in ("ffn1_w3", "w_gate")]
    sums, ((w3_rest,), (late_all,)) = _sum_ready(
        [parts[n] for n in ready], "sum_ready", rs((g_w13, FF + head, ffr, ffr - head)) + [_AllGather(late_pack)])
    (sum_w3,), _ = _sum_ready([jnp.concatenate([w3_head, w3_rest], axis=1)], "sum_last")
    group = ready + ["ffn1_w3"]
    res = _adamw_cols([(g, view[n](ws[n]), view[n](ms[n]), view[n](vs[n])) for n, g in zip(group, sums + [sum_w3])],
                      "adamw_big")
    for n, quad in zip(group, res):
        grads[n], delta[n], new_m[n], new_v[n] = (unview[n](a) for a in quad)
    (g_gate_t,) = _sum_parts([parts["w_gate"]])
    grads["w_gate"] = g_gate_t.T
    ((delta["w_gate"], new_m["w_gate"], new_v["w_gate"]),) = _adamw(
        [(grads["w_gate"], view["w_gate"](ws["w_gate"]), view["w_gate"](ms["w_gate"]), view["w_gate"](vs["w_gate"]))])
    for group, gathered, offs in ((early, early_all, early_offs), (late, late_all, late_offs)):
        total = _sum_small(gathered, gathered.shape[0] // NDEV)
        shapes = [(CONV_WIDTH, D) if n == "conv_w" else ws[n].shape for n in group]
        if group is late:
            shapes = shapes + [(1, LANES)]
        unpacked = _unpack_small(total, offs, shapes)
        if group is late:
            loss = unpacked.pop()[0, 0]
        for n, g in zip(group, unpacked):
            grads[n] = g
    grads["conv_w"] = lax.dynamic_slice(grads["conv_w"], (0, me * LANES), (CONV_WIDTH, LANES)).reshape(conv_w.shape)

    flat2d = lambda a: a.reshape(-1, a.shape[-1])
    res = _adamw_small([(flat2d(grads[n].reshape(ws[n].shape)), flat2d(ws[n]), flat2d(ms[n]), flat2d(vs[n]))
                        for n in SMALL])
    for n, (d_, m_, v_) in zip(SMALL, res):
        delta[n], new_m[n], new_v[n] = d_, m_, v_

    outs = [loss, grad_x.reshape(x.shape)]
    for src in (grads, delta, new_m, new_v):
        outs += [src[n].reshape(ws[n].shape) for n in names]
    return tuple(outs)
```

```python
import functools
import math
import operator

import numpy as np
import jax
import jax.numpy as jnp
from jax import lax
from jax.experimental import pallas as pl
from jax.experimental.pallas import tpu as pltpu

F32, BF16 = jnp.float32, jnp.bfloat16

NDEV = 8
D = 1024
FF = 2816
N_HEADS, N_KV, HEAD_DIM = 16, 4, 64
CHUNK, WINDOW = 64, 128
N_BUCKETS, MAX_DISTANCE = 32, 128
LRU_BLOCK = 64
CONV_WIDTH = 4
LRU_C = 8.0
RMS_EPS = 1e-6
NEG_INF = -1e30
LANES = 128
QT = 128
KW = QT + WINDOW
LRU_ROWS = 512
IN_W = D + 2 * N_KV * HEAD_DIM + 2 * D
INP_W = D + 2 * N_KV * LANES + 2 * D
VMEM_BIG = 58 * 2 ** 20

ADAM_LR, ADAM_B1, ADAM_B2, ADAM_EPS, ADAM_WD, ADAM_STEP = 0.001, 0.9, 0.999, 1e-08, 0.01, 10

GROUPS = (("ffn1", (("ffn1_w1", FF // NDEV), ("ffn1_w3", FF // NDEV), ("ffn1_w2", FF // NDEV), ("conv", 16))),
          ("mixin", (("w_in", IN_W // NDEV), ("w_gate", 2 * D // NDEV))),
          ("mixout", (("w_lru_out", D // NDEV), ("w_attn_out", D // NDEV), ("w_o", D // NDEV))),
          ("ffn2a", (("ffn2_w1", FF // NDEV), ("ffn2_w3", FF // NDEV))),
          ("ffn2b", (("ffn2_w2", FF // NDEV),)))
LOC, GROUP_ROWS = {}, {}
for _g, _members in GROUPS:
    _o = 0
    for _n, _r in _members:
        LOC[_n] = (_g, _o, _r)
        _o += _r
    GROUP_ROWS[_g] = _o
BIG = tuple(n for _, members in GROUPS for n, _ in members if n != "conv")
COL_SHARDED = ("ffn1_w1", "ffn1_w3", "w_in", "w_gate", "ffn2_w1", "ffn2_w3")

SMALL = ("ffn1_pre_g", "ffn1_post_g", "mix_pre_g", "conv_w", "conv_b", "rg_a_w", "rg_a_b", "rg_x_w", "rg_x_b",
         "lru_lambda", "attn_sinks", "rel_bias", "b_gate", "mix_post_g", "ffn2_pre_g", "ffn2_post_g")

MESH = pl.DeviceIdType.MESH
ANY = pl.BlockSpec(memory_space=pl.ANY)
DMA = pltpu.SemaphoreType.DMA


def _nn(a, b):
    return lax.dot_general(a, b, (((1,), (0,)), ((), ())), preferred_element_type=F32)


def _nt(a, b):
    return lax.dot_general(a, b, (((1,), (1,)), ((), ())), preferred_element_type=F32)


def _tn(a, b):
    return lax.dot_general(a, b, (((0,), (0,)), ((), ())), preferred_element_type=F32)


def _rms_fwd(x, g):
    r = lax.rsqrt(jnp.mean(x * x, axis=-1, keepdims=True) + RMS_EPS)
    xh = x * r
    return xh * g, xh, r


def _rms_bwd(dn, xh, r, g):
    dxh = dn * g
    dx = r * (dxh - xh * jnp.mean(dxh * xh, axis=-1, keepdims=True))
    return dx, jnp.sum(dn * xh, axis=0, keepdims=True)


def _row_spec(tm, c):
    return pl.BlockSpec((tm, c), lambda i: (i, 0))


def _const_spec(shape):
    nd = len(shape)
    return pl.BlockSpec(shape, lambda i: (0,) * nd)


class _AllGather:
    def __init__(self, shard):
        self.m, n = shard.shape
        self.ins = [shard]
        self.out_shape = [jax.ShapeDtypeStruct((NDEV * self.m, n), shard.dtype)]
        self.scratch = [DMA((7,)), DMA((7,)), DMA]

    def _copies(self, ins, outs, scr, all_of_them):
        x_ref, out_ref = ins[0], outs[0]
        send_sems, recv_sems, local_sem = scr
        x, y, c = lax.axis_index("x"), lax.axis_index("y"), lax.axis_index("c")
        me, sibling = (x, y, c), (x, y, 1 - c)
        chips = [(1 - x, y), (x, 1 - y), (1 - x, 1 - y)]
        m = self.m

        def rows(px, py, pc):
            return out_ref.at[pl.ds((4 * px + 2 * py + pc) * m, m), :]

        def copy(k, block, to, src=None):
            return pltpu.make_async_remote_copy(
                src_ref=rows(*block) if src is None else src, dst_ref=rows(*block),
                send_sem=send_sems.at[k], recv_sem=recv_sems.at[k], device_id=to, device_id_type=MESH)

        mine = pltpu.make_async_copy(x_ref, rows(*me), local_sem)
        first = [copy(0, me, sibling, src=x_ref)] + [copy(1 + j, me, (*chip, c), src=x_ref)
                                                     for j, chip in enumerate(chips)]
        if not all_of_them:
            return mine, first
        passed = [copy(4 + j, (*chip, c), sibling) for j, chip in enumerate(chips)]
        landed = [copy(1 + j, (*chip, c), me) for j, chip in enumerate(chips)]
        from_sibling = [copy(0, sibling, me)] + [copy(4 + j, (*chip, 1 - c), me) for j, chip in enumerate(chips)]
        return mine, first, passed, landed, from_sibling

    def start(self, ins, outs, scr):
        mine, first = self._copies(ins, outs, scr, False)
        mine.start()
        for cp in first:
            cp.start()

    def finish(self, ins, outs, scr):
        mine, first, passed, landed, from_sibling = self._copies(ins, outs, scr, True)
        for cp_in, cp_on in zip(landed, passed):
            cp_in.wait_recv()
            cp_on.start()
        for cp in from_sibling:
            cp.wait_recv()
        for cp in first + passed:
            cp.wait_send()
        mine.wait()


class _ReduceScatterSend:
    def __init__(self, grads, zeros=None):
        grads = [g if isinstance(g, tuple) else (g, 0, g.shape[0] // NDEV, g.shape[0] // NDEV) for g in grads]
        self.chip_sums = zeros is not None
        self.peers = (2, 4, 6) if self.chip_sums else tuple(range(1, NDEV))
        self.nw = len(grads)
        self.base = [b for _, b, _, _ in grads]
        self.stride = [s for _, _, s, _ in grads]
        self.rows = [r for _, _, _, r in grads]
        self.ins = [g for g, _, _, _ in grads] + ([zeros] if self.chip_sums else [])
        self.out_shape = [jax.ShapeDtypeStruct((NDEV, r, g.shape[1]), g.dtype) for g, _, _, r in grads]
        self.scratch = [DMA((self.nw, NDEV - 1)), DMA((self.nw, NDEV - 1)), DMA((self.nw, 1 + NDEV // 2))]

    def _copies(self, g_refs, r_refs, scr, want):
        send_sems, recv_sems, local_sems = scr
        x, y, c = lax.axis_index("x"), lax.axis_index("y"), lax.axis_index("c")
        me = 4 * x + 2 * y + c
        rows, base, stride = self.rows, self.base, self.stride
        out = []
        if want == "local":
            for w in range(self.nw):
                out.append(pltpu.make_async_copy(g_refs[w].at[pl.ds(base[w] + me * stride[w], rows[w])],
                                                 r_refs[w].at[me], local_sems.at[w, 0]))
                if self.chip_sums:
                    for ch in range(NDEV // 2):
                        out.append(pltpu.make_async_copy(g_refs[self.nw].at[pl.ds(0, rows[w])],
                                                         r_refs[w].at[2 * ch + 1 - c], local_sems.at[w, 1 + ch]))
            return out
        for k in self.peers:
            px, py, pc = x ^ (k >> 2), y ^ ((k >> 1) & 1), c ^ (k & 1)
            peer = 4 * px + 2 * py + pc
            for w in range(self.nw):
                sems = dict(send_sem=send_sems.at[w, k - 1], recv_sem=recv_sems.at[w, k - 1],
                            device_id=(px, py, pc), device_id_type=MESH)
                if want == "send":
                    out.append(pltpu.make_async_remote_copy(
                        src_ref=g_refs[w].at[pl.ds(base[w] + peer * stride[w], rows[w])], dst_ref=r_refs[w].at[me],
                        **sems))
                else:
                    out.append(pltpu.make_async_remote_copy(
                        src_ref=g_refs[w].at[pl.ds(0, rows[w])], dst_ref=r_refs[w].at[peer], **sems))
        return out

    def start(self, ins, outs, scr):
        for cp in self._copies(ins, outs, scr, "local") + self._copies(ins, outs, scr, "send"):
            cp.start()

    def finish(self, ins, outs, scr):
        for cp in self._copies(ins, outs, scr, "recv"):
            cp.wait_recv()
        for cp in self._copies(ins, outs, scr, "send"):
            cp.wait_send()
        for cp in self._copies(ins, outs, scr, "local"):
            cp.wait()


def _call(body, *, name, grid, ins, in_specs, out_shape, out_specs, scratch=(), vmem=None, carries=()):
    n_in, n_out, n_scr = len(ins), len(out_shape), len(scratch)
    ng = len(grid)

    def split(refs):
        pos = [0]

        def take(k):
            part = refs[pos[0]:pos[0] + k]
            pos[0] += k
            return part

        i_refs = take(n_in)
        c_in = [take(len(c.ins)) for c in carries]
        o_refs = take(n_out)
        c_out = [take(len(c.out_shape)) for c in carries]
        s_refs = take(n_scr)
        c_scr = [take(len(c.scratch)) for c in carries]
        return i_refs, o_refs, s_refs, list(zip(carries, c_in, c_out, c_scr))

    def full(*refs):
        i_refs, o_refs, s_refs, cparts = split(refs)
        if ng == 0:
            for c, a, b, s in cparts:
                c.start(a, b, s)
            for c, a, b, s in cparts:
                c.finish(a, b, s)
            return
        ids = [pl.program_id(a) for a in range(ng)]
        if cparts:
            @pl.when(functools.reduce(operator.and_, [i == 0 for i in ids]))
            def _():
                for c, a, b, s in cparts:
                    c.start(a, b, s)

        body(*i_refs, *o_refs, *s_refs)
        if cparts:
            @pl.when(functools.reduce(operator.and_, [i == g - 1 for i, g in zip(ids, grid)]))
            def _():
                for c, a, b, s in cparts:
                    c.finish(a, b, s)

    all_ins = list(ins) + [a for c in carries for a in c.ins]
    all_in_specs = list(in_specs) + [ANY for c in carries for _ in c.ins]
    all_out_shape = list(out_shape) + [s for c in carries for s in c.out_shape]
    all_out_specs = list(out_specs) + [ANY for c in carries for _ in c.out_shape]
    all_scratch = list(scratch) + [s for c in carries for s in c.scratch]
    kwargs = dict(grid=grid) if ng else {}
    outs = pl.pallas_call(
        full, name=name, in_specs=all_in_specs, out_specs=all_out_specs, out_shape=all_out_shape,
        scratch_shapes=all_scratch,
        compiler_params=pltpu.CompilerParams(dimension_semantics=("arbitrary",) * ng if ng else None,
                                             vmem_limit_bytes=vmem),
        **kwargs)(*all_ins)
    outs = list(outs)
    res, pos = outs[:n_out], n_out
    carried = []
    for c in carries:
        carried.append(outs[pos:pos + len(c.out_shape)])
        pos += len(c.out_shape)
    return res, carried


def _pair_sum(g, items):
    rows = items[0][1]
    n = len(items) * (NDEV // 2)
    width = g.shape[1]

    def body(g_ref, out_ref, got_ref, mine_v, got_v, send_sems, recv_sems, ld_sems, st_sems):
        x, y, c = lax.axis_index("x"), lax.axis_index("y"), lax.axis_index("c")
        keep = [base + (2 * ch + c) * rows for base, _ in items for ch in range(NDEV // 2)]
        give = [base + (2 * ch + 1 - c) * rows for base, _ in items for ch in range(NDEV // 2)]

        def remote(j, r):
            return pltpu.make_async_remote_copy(
                src_ref=g_ref.at[pl.ds(r, rows)], dst_ref=got_ref.at[pl.ds(r, rows)], send_sem=send_sems.at[j],
                recv_sem=recv_sems.at[j], device_id=(x, y, 1 - c), device_id_type=MESH)

        sends = [remote(j, r) for j, r in enumerate(give)]
        for cp in sends:
            cp.start()
        loads = [pltpu.make_async_copy(g_ref.at[pl.ds(r, rows)], mine_v.at[j], ld_sems.at[0, j])
                 for j, r in enumerate(keep)]
        for cp in loads:
            cp.start()
        stores = []
        for j, r in enumerate(keep):
            remote(j, r).wait_recv()
            cp = pltpu.make_async_copy(got_ref.at[pl.ds(r, rows)], got_v.at[j], ld_sems.at[1, j])
            cp.start()
            loads[j].wait()
            cp.wait()
            mine_v[j] = (mine_v[j].astype(F32) + got_v[j].astype(F32)).astype(BF16)
            st = pltpu.make_async_copy(mine_v.at[j], out_ref.at[pl.ds(r, rows)], st_sems.at[j])
            st.start()
            stores.append(st)
        for cp in sends:
            cp.wait_send()
        for cp in stores:
            cp.wait()

    out, _ = pl.pallas_call(
        body, name="pair_sum", in_specs=[ANY], out_specs=[ANY, ANY],
        out_shape=[jax.ShapeDtypeStruct(g.shape, g.dtype), jax.ShapeDtypeStruct(g.shape, g.dtype)],
        scratch_shapes=[pltpu.VMEM((n, rows, width), g.dtype), pltpu.VMEM((n, rows, width), g.dtype),
                        DMA((n,)), DMA((n,)), DMA((2, n)), DMA((n,))],
        compiler_params=pltpu.CompilerParams(vmem_limit_bytes=40 * 2 ** 20))(g)
    return out


def _groups_of(names):
    out = []
    for n in names:
        if LOC[n][0] not in out:
            out.append(LOC[n][0])
    return out


def _weight_pieces(name):
    g, off, rows = LOC[name]
    return [(d * GROUP_ROWS[g] + off, d * rows, rows) for d in range(NDEV)]


def _win_pieces():
    kv = N_KV * HEAD_DIM
    pieces = [(0, 0, D)]
    for g0, d0 in ((D, D), (D + kv, D + N_KV * LANES)):
        for g in range(N_KV):
            for half in range(2):
                pieces.append((g0 + g * HEAD_DIM, d0 + g * LANES + half * HEAD_DIM, HEAD_DIM))
    pieces.append((D + 2 * kv, D + 2 * N_KV * LANES, D))
    pieces.append((D + 2 * kv + D, D + 2 * N_KV * LANES + D, D))
    grp, off, rows = LOC["w_in"]
    out = []
    for g0, d0, n in pieces:
        while n > 0:
            dev, loc = divmod(g0, rows)
            m = min(n, rows - loc)
            out.append((dev * GROUP_ROWS[grp] + off + loc, d0, m))
            g0, d0, n = g0 + m, d0 + m, n - m
    return out


def _start_loads(src_ref, dst_ref, pieces, sems, base):
    cps = []
    for j, (s, d, n) in enumerate(pieces):
        cp = pltpu.make_async_copy(src_ref.at[pl.ds(s, n)], dst_ref.at[pl.ds(d, n)], sems.at[base + j])
        cp.start()
        cps.append(cp)
    return cps


def _load_weights(wrefs, targets, sems):
    cps, base = [], 0
    for name, dst in targets:
        pieces = _win_pieces() if name == "w_in" else _weight_pieces(name)
        cps += _start_loads(wrefs[LOC[name][0]], dst, pieces, sems, base)
        base += len(pieces)
    for cp in cps:
        cp.wait()


def _n_pieces(names):
    return sum(len(_win_pieces()) if n == "w_in" else NDEV for n in names)


def _dw(lhs, rhs, chunk, name, carries=()):
    nq = 4
    if lhs.ndim == 3:
        nch, t_tok, chunk = lhs.shape
        c = nch * chunk
        tq = t_tok // nq
        lhs_specs = [pl.BlockSpec((None, tq, chunk), lambda i, k=k: (i, k, 0)) for k in range(nq)]
    else:
        t_tok, c = lhs.shape
        tq = t_tok // nq
        lhs_specs = [pl.BlockSpec((tq, chunk), lambda i, k=k: (k, i)) for k in range(nq)]

    def body(*refs):
        lhs_refs, (rhs_ref, out_ref, rhs_s, sems) = refs[:nq], refs[nq:]
        cps = [pltpu.make_async_copy(rhs_ref.at[pl.ds(k * tq, tq)], rhs_s.at[pl.ds(k * tq, tq)], sems.at[k])
               for k in range(nq)]
        first = pl.program_id(0) == 0

        @pl.when(first)
        def _():
            for cp in cps:
                cp.start()

        for o in range(0, chunk, D // 2):
            n = min(D // 2, chunk - o)
            acc = None
            for k in range(nq):
                if o == 0:
                    @pl.when(first)
                    def _():
                        cps[k].wait()

                part = _tn(lhs_refs[k][:, pl.ds(o, n)], rhs_s[pl.ds(k * tq, tq), :])
                acc = part if acc is None else acc + part
            out_ref[pl.ds(o, n), :] = acc.astype(BF16)

    (out,), carried = _call(
        body, name=name, grid=(c // chunk,), ins=[lhs] * nq + [rhs],
        in_specs=lhs_specs + [ANY],
        out_specs=[pl.BlockSpec((chunk, D), lambda i: (i, 0))],
        out_shape=[jax.ShapeDtypeStruct((c, D), BF16)],
        scratch=[pltpu.VMEM((t_tok, D), BF16), DMA((nq,))], vmem=VMEM_BIG, carries=carries)
    return out, carried


def _silu_parts(a):
    sig = jax.nn.sigmoid(a)
    return sig, a * sig


FC = 256


def _ffn_fwd(h, gpre, gpost, wg, names, target=None, tm=512, carries=()):
    t_tok = h.shape[0]
    nt = t_tok // tm
    with_loss = target is not None
    groups = _groups_of(names)

    def body(*refs):
        refs = list(refs)
        h_ref, gpre_ref, gpost_ref = refs[:3]
        del refs[:3]
        tgt_ref = refs.pop(0) if with_loss else None
        wrefs = dict(zip(groups, refs[:len(groups)]))
        del refs[:len(groups)]
        hout_ref, a_ref, b_ref, f_ref, nb_ref = refs[:5]
        del refs[:5]
        if with_loss:
            dy_ref, loss_ref = refs[:2]
            del refs[:2]
        w1_s, w3_s, w2_s, sems = refs
        i = pl.program_id(0)

        @pl.when(i == 0)
        def _():
            _load_weights(wrefs, list(zip(names, (w1_s, w3_s, w2_s))), sems)
            if with_loss:
                loss_ref[...] = jnp.zeros_like(loss_ref)

        x = h_ref[...]
        n, _, _ = _rms_fwd(x, gpre_ref[...])
        nb = n.astype(BF16)
        nb_ref[...] = nb
        f = jnp.zeros((tm, D), F32)
        for c0 in range(0, FF, FC):
            a = _nt(nb, w1_s[pl.ds(c0, FC), :])
            b = _nt(nb, w3_s[pl.ds(c0, FC), :])
            _, sl = _silu_parts(a)
            a_ref[:, pl.ds(c0, FC)] = a.astype(BF16)
            b_ref[:, pl.ds(c0, FC)] = b.astype(BF16)
            f = f + _nn((sl * b).astype(BF16), w2_s[pl.ds(c0, FC), :])
        f_ref[...] = f
        fn, _, _ = _rms_fwd(f, gpost_ref[...])
        y = x + 0.5 * fn
        hout_ref[...] = y
        if with_loss:
            err = y - tgt_ref[...]
            dy_ref[...] = err * (1.0 / D)
            loss_ref[...] += jnp.sum(jnp.sum(err * err, axis=-1, keepdims=True), axis=0, keepdims=True) * (0.5 / D)

    ins = [h, gpre, gpost] + ([target] if with_loss else []) + [wg[g] for g in groups]
    in_specs = [_row_spec(tm, D), _const_spec((1, D)), _const_spec((1, D))]
    in_specs += ([_row_spec(tm, D)] if with_loss else []) + [ANY] * len(groups)
    out_shape = [jax.ShapeDtypeStruct((t_tok, D), F32), jax.ShapeDtypeStruct((t_tok, FF), BF16),
                 jax.ShapeDtypeStruct((t_tok, FF), BF16), jax.ShapeDtypeStruct((t_tok, D), F32),
                 jax.ShapeDtypeStruct((t_tok, D), BF16)]
    out_specs = [_row_spec(tm, D), _row_spec(tm, FF), _row_spec(tm, FF), _row_spec(tm, D), _row_spec(tm, D)]
    if with_loss:
        out_shape += [jax.ShapeDtypeStruct((t_tok, D), F32), jax.ShapeDtypeStruct((1, LANES), F32)]
        out_specs += [_row_spec(tm, D), _const_spec((1, LANES))]
    return _call(body, name="ffn_fwd_" + names[0][:4], grid=(nt,), ins=ins, in_specs=in_specs,
                 out_shape=out_shape, out_specs=out_specs,
                 scratch=[pltpu.VMEM((FF, D), BF16)] * 3 + [DMA((3 * NDEV,))], vmem=VMEM_BIG, carries=carries)


FH = FF // 2
HALF_PIECES = ((0, 256), (256, 256), (512, 256), (768, 256), (1024, 256), (1280, 128))


def _ffn_bwd_a(dh, f, a, b, gpost, wg, name_w2, tm=512, carries=()):
    t_tok = dh.shape[0]
    nt = t_tok // tm
    groups = _groups_of([name_w2])

    def body(dh_ref, f_ref, a_ref, b_ref, gpost_ref, wg_ref, dab_ref, s_ref, df_ref, dgp_ref, w2_s, sems):
        i = pl.program_id(0)

        @pl.when(i == 0)
        def _():
            _load_weights({groups[0]: wg_ref}, [(name_w2, w2_s)], sems)
            dgp_ref[...] = jnp.zeros_like(dgp_ref)

        fv = f_ref[...]
        _, fh, r = _rms_fwd(fv, gpost_ref[...])
        df, dg = _rms_bwd(0.5 * dh_ref[...], fh, r, gpost_ref[...])
        dgp_ref[...] += dg
        dfb = df.astype(BF16)
        df_ref[...] = dfb
        for half in range(2):
            for o, n in HALF_PIECES:
                c0 = half * FH + o
                ds = _nt(dfb, w2_s[pl.ds(c0, n), :])
                av = a_ref[:, pl.ds(c0, n)].astype(F32)
                bv = b_ref[:, pl.ds(c0, n)].astype(F32)
                sig, sl = _silu_parts(av)
                dab_ref[half, :, pl.ds(o, n)] = (ds * bv * (sig * (1.0 + av * (1.0 - sig)))).astype(BF16)
                dab_ref[2 + half, :, pl.ds(o, n)] = (ds * sl).astype(BF16)
                s_ref[half, :, pl.ds(o, n)] = (sl * bv).astype(BF16)

    return _call(
        body, name="ffn_bwd_a_" + name_w2[:4], grid=(nt,), ins=[dh, f, a, b, gpost, wg[groups[0]]],
        in_specs=[_row_spec(tm, D), _row_spec(tm, D), _row_spec(tm, FF), _row_spec(tm, FF), _const_spec((1, D)), ANY],
        out_specs=[pl.BlockSpec((4, tm, FH), lambda i: (0, i, 0)), pl.BlockSpec((2, tm, FH), lambda i: (0, i, 0)),
                   _row_spec(tm, D), _const_spec((1, D))],
        out_shape=[jax.ShapeDtypeStruct((4, t_tok, FH), BF16), jax.ShapeDtypeStruct((2, t_tok, FH), BF16),
                   jax.ShapeDtypeStruct((t_tok, D), BF16), jax.ShapeDtypeStruct((1, D), F32)],
        scratch=[pltpu.VMEM((FF, D), BF16), DMA((NDEV,))],
        vmem=VMEM_BIG, carries=carries)


def _ffn_bwd_dx(dab, h, dh, gpre, wg, name_w1, name_w3, tm=512, carries=()):
    t_tok = dh.shape[0]
    nt = t_tok // tm
    groups = _groups_of([name_w1, name_w3])

    def body(*refs):
        dab_refs, (h_ref, dh_ref, gpre_ref) = refs[:4], refs[4:7]
        wrefs = dict(zip(groups, refs[7:7 + len(groups)]))
        dhin_ref, dgp_ref, w13_s, sems = refs[7 + len(groups):]
        i = pl.program_id(0)

        @pl.when(i == 0)
        def _():
            _load_weights(wrefs, [(name_w1, w13_s.at[pl.ds(0, FF)]), (name_w3, w13_s.at[pl.ds(FF, FF)])], sems)
            dgp_ref[...] = jnp.zeros_like(dgp_ref)

        g = gpre_ref[...]
        _, xh, r = _rms_fwd(h_ref[...], g)
        dn = _nn(dab_refs[0][...], w13_s[pl.ds(0, FH), :])
        for k in range(1, 4):
            dn = dn + _nn(dab_refs[k][...], w13_s[pl.ds(k * FH, FH), :])
        dx, dg = _rms_bwd(dn, xh, r, g)
        dgp_ref[...] += dg
        dhin_ref[...] = dh_ref[...] + dx

    return _call(
        body, name="ffn_bwd_dx_" + name_w1[:4], grid=(nt,), ins=[dab] * 4 + [h, dh, gpre] + [wg[g] for g in groups],
        in_specs=[pl.BlockSpec((None, tm, FH), lambda i, k=k: (k, i, 0)) for k in range(4)]
        + [_row_spec(tm, D), _row_spec(tm, D), _const_spec((1, D))] + [ANY] * len(groups),
        out_specs=[_row_spec(tm, D), _const_spec((1, D))],
        out_shape=[jax.ShapeDtypeStruct((t_tok, D), F32), jax.ShapeDtypeStruct((1, D), F32)],
        scratch=[pltpu.VMEM((2 * FF, D), BF16), DMA((2 * NDEV,))],
        vmem=VMEM_BIG, carries=carries)


Q0, K0, V0, XR0, XG0 = 0, D, D + N_KV * LANES, D + 2 * N_KV * LANES, 2 * D + 2 * N_KV * LANES


def _mix_proj_fwd(h, g, bgate, wg, tm=512, carries=()):
    t_tok = h.shape[0]
    nt = t_tok // tm
    names = ("w_in", "w_gate")
    groups = _groups_of(names)

    def body(h_ref, g_ref, bg_ref, wg_ref, q_ref, k_ref, v_ref, xr_ref, xg_ref, gs_ref, ub_ref, win_s, wgt_s, sems):
        i = pl.program_id(0)

        @pl.when(i == 0)
        def _():
            _load_weights({groups[0]: wg_ref}, [("w_in", win_s), ("w_gate", wgt_s)], sems)

        n, _, _ = _rms_fwd(h_ref[...], g_ref[...])
        nb = n.astype(BF16)
        ub_ref[...] = nb
        q_ref[...] = _nt(nb, win_s[pl.ds(Q0, D), :]).astype(BF16)
        k_ref[...] = _nt(nb, win_s[pl.ds(K0, N_KV * LANES), :]).astype(BF16)
        v_ref[...] = _nt(nb, win_s[pl.ds(V0, N_KV * LANES), :]).astype(BF16)
        xr_ref[...] = _nt(nb, win_s[pl.ds(XR0, D), :])
        xg_ref[...] = _nt(nb, win_s[pl.ds(XG0, D), :])
        gs_ref[...] = jax.nn.sigmoid(_nt(nb, wgt_s[...]) + bg_ref[...]).astype(BF16)

    kvw = N_KV * LANES
    return _call(
        body, name="mix_proj_fwd", grid=(nt,), ins=[h, g, bgate, wg[groups[0]]],
        in_specs=[_row_spec(tm, D), _const_spec((1, D)), _const_spec((1, 2 * D)), ANY],
        out_specs=[_row_spec(tm, D), _row_spec(tm, kvw), _row_spec(tm, kvw), _row_spec(tm, D), _row_spec(tm, D),
                   _row_spec(tm, 2 * D), _row_spec(tm, D)],
        out_shape=[jax.ShapeDtypeStruct((t_tok, D), BF16), jax.ShapeDtypeStruct((t_tok, kvw), BF16),
                   jax.ShapeDtypeStruct((t_tok, kvw), BF16), jax.ShapeDtypeStruct((t_tok, D), F32),
                   jax.ShapeDtypeStruct((t_tok, D), F32), jax.ShapeDtypeStruct((t_tok, 2 * D), BF16),
                   jax.ShapeDtypeStruct((t_tok, D), BF16)],
        scratch=[pltpu.VMEM((INP_W, D), BF16), pltpu.VMEM((2 * D, D), BF16), DMA((_n_pieces(names),))],
        vmem=VMEM_BIG, carries=carries)


def _mix_proj_bwd(dqkv, dxr, dxg, dgpre, h, dh, g, wg, tm=512, carries=()):
    t_tok = h.shape[0]
    nt = t_tok // tm
    names = ("w_in", "w_gate")
    groups = _groups_of(names)

    def body(dqkv_ref, dxr_ref, dxg_ref, dgp_ref, h_ref, dh_ref, g_ref, wg_ref, dhin_ref, dg_ref, win_s, wgt_s, sems):
        i = pl.program_id(0)

        @pl.when(i == 0)
        def _():
            _load_weights({groups[0]: wg_ref}, [("w_in", win_s), ("w_gate", wgt_s)], sems)
            dg_ref[...] = jnp.zeros_like(dg_ref)

        gv = g_ref[...]
        _, xh, r = _rms_fwd(h_ref[...], gv)
        du = _nn(dgp_ref[...], wgt_s[...])
        du = du + _nn(dqkv_ref[...], win_s[pl.ds(Q0, XR0), :])
        du = du + _nn(dxr_ref[...], win_s[pl.ds(XR0, D), :])
        du = du + _nn(dxg_ref[...], win_s[pl.ds(XG0, D), :])
        dx, dg = _rms_bwd(du, xh, r, gv)
        dg_ref[...] += dg
        dhin_ref[...] = dh_ref[...] + dx

    return _call(
        body, name="mix_proj_bwd", grid=(nt,), ins=[dqkv, dxr, dxg, dgpre, h, dh, g, wg[groups[0]]],
        in_specs=[_row_spec(tm, XR0), _row_spec(tm, D), _row_spec(tm, D), _row_spec(tm, 2 * D), _row_spec(tm, D),
                  _row_spec(tm, D), _const_spec((1, D)), ANY],
        out_specs=[_row_spec(tm, D), _const_spec((1, D))],
        out_shape=[jax.ShapeDtypeStruct((t_tok, D), F32), jax.ShapeDtypeStruct((1, D), F32)],
        scratch=[pltpu.VMEM((INP_W, D), BF16), pltpu.VMEM((2 * D, D), BF16), DMA((_n_pieces(names),))],
        vmem=VMEM_BIG, carries=carries)


def _shift_down(x, d, fill):
    row = lax.broadcasted_iota(jnp.int32, x.shape, 0)
    return jnp.where(row >= d, pltpu.roll(x, d, 0), fill)


def _shift_up(x, d, fill):
    rows = x.shape[0]
    row = lax.broadcasted_iota(jnp.int32, x.shape, 0)
    return jnp.where(row < rows - d, pltpu.roll(x, rows - d, 0), fill)


def _scan_rows(a, b, reverse):
    rows = a.shape[0]
    d = 1
    while d < rows:
        if d < 8:
            shift = _shift_up if reverse else _shift_down
            b = a * shift(b, d, 0.0) + b
            a = a * shift(a, d, 1.0)
        elif reverse:
            b = jnp.concatenate([a[:rows - d] * b[d:] + b[:rows - d], b[rows - d:]], axis=0)
            a = jnp.concatenate([a[:rows - d] * a[d:], a[rows - d:]], axis=0)
        else:
            b = jnp.concatenate([b[:d], a[d:] * b[:rows - d] + b[d:]], axis=0)
            a = jnp.concatenate([a[:d], a[d:] * a[:rows - d]], axis=0)
        d *= 2
    return a, b


def _softplus(x):
    return jnp.maximum(x, 0.0) + jnp.log(1.0 + jnp.exp(-jnp.abs(x)))


_GELU_C = math.sqrt(2.0 / math.pi)


def _gelu_parts(x):
    th = jnp.tanh(_GELU_C * (x + 0.044715 * x * x * x))
    val = 0.5 * x * (1.0 + th)
    grad = 0.5 * (1.0 + th) + 0.5 * x * (1.0 - th * th) * _GELU_C * (1.0 + 3.0 * 0.044715 * x * x)
    return val, grad


def _lru_pre(x, halo, cw_ref, cb_ref, wa_ref, wx_ref, ba_ref, bx_ref, lam_ref):
    ext = jnp.concatenate([halo, x], axis=0)
    shifted = [x] + [pltpu.roll(ext, k, 0)[8:] for k in (1, 2, 3)]
    xc = cb_ref[...] + cw_ref[pl.ds(CONV_WIDTH - 1, 1), :] * x
    for k in (1, 2, 3):
        xc = xc + cw_ref[pl.ds(CONV_WIDTH - 1 - k, 1), :] * shifted[k]
    xcb = xc.astype(BF16)
    r = jax.nn.sigmoid(_nn(xcb, wa_ref[...]) + ba_ref[...])
    ig = jax.nn.sigmoid(_nn(xcb, wx_ref[...]) + bx_ref[...])
    sp = _softplus(-lam_ref[...])
    log_a = -LRU_C * r * sp
    a = jnp.exp(log_a)
    th = jnp.tanh(log_a)
    mult = jnp.sqrt(-2.0 * th / (1.0 - th))
    return shifted, xc, xcb, r, ig, sp, a, mult


def _lru_specs(nt, reverse):
    def tt(t):
        return nt - 1 - t if reverse else t
    tile = pl.BlockSpec((LRU_ROWS, LANES), lambda cb, t: (tt(t), cb))
    halo = pl.BlockSpec((8, LANES), lambda cb, t: (jnp.maximum(tt(t) * (LRU_ROWS // 8) - 1, 0), cb))
    vec = pl.BlockSpec((1, LANES), lambda cb, t: (0, cb))
    cw = pl.BlockSpec((CONV_WIDTH, LANES), lambda cb, t: (0, cb))
    mat = pl.BlockSpec((None, LANES, LANES), lambda cb, t: (cb, 0, 0))
    return tile, halo, vec, cw, mat


def _lru_fwd(xr, xg, cw, cb, wa, wx, ba, bx, lam, carries=()):
    t_tok = xr.shape[0]
    nt = t_tok // LRU_ROWS
    rows = LRU_ROWS

    def body(xr_ref, xg_ref, cw_ref, cb_ref, wa_ref, wx_ref, ba_ref, bx_ref, lam_ref, y_ref, h_ref, tail_s, hc_s):
        t = pl.program_id(1)

        @pl.when(t == 0)
        def _():
            tail_s[...] = jnp.zeros_like(tail_s)
            hc_s[...] = jnp.zeros_like(hc_s)

        x = xr_ref[...]
        _, xc, _, _, ig, _, a, mult = _lru_pre(x, tail_s[...], cw_ref, cb_ref, wa_ref, wx_ref, ba_ref, bx_ref, lam_ref)
        tail_s[...] = xr_ref[pl.ds(rows - 8, 8), :]
        acc_a, acc_b = _scan_rows(a, mult * (ig * xc), False)
        hv = acc_b + acc_a * hc_s[...]
        h_ref[...] = hv
        hc_s[...] = h_ref[pl.ds(rows - 1, 1), :]
        gl, _ = _gelu_parts(xg_ref[...])
        y_ref[...] = (hv * gl).astype(BF16)

    tile, _, vec, cws, mat = _lru_specs(nt, False)
    return _call(
        body, name="lru_fwd", grid=(D // LANES, nt), ins=[xr, xg, cw, cb, wa, wx, ba, bx, lam],
        in_specs=[tile, tile, cws, vec, mat, mat, vec, vec, vec],
        out_specs=[tile, tile],
        out_shape=[jax.ShapeDtypeStruct((t_tok, D), BF16), jax.ShapeDtypeStruct((t_tok, D), F32)],
        scratch=[pltpu.VMEM((8, LANES), F32), pltpu.VMEM((1, LANES), F32)], carries=carries)


def _lru_bwd(dy, xr, xg, hseq, cw, cb, wa, wx, ba, bx, lam, carries=()):
    t_tok = xr.shape[0]
    nt = t_tok // LRU_ROWS
    rows = LRU_ROWS

    def body(dy_ref, xr_ref, xrh_ref, xg_ref, h_ref, hh_ref, cw_ref, cb_ref, wa_ref, wx_ref, ba_ref, bx_ref, lam_ref,
             dxr_ref, dxg_ref, dvec_ref, dwa_ref, dwx_ref, gcar_s, acar_s, head_s, tmp_s):
        t = pl.program_id(1)
        first_tile = t == nt - 1

        @pl.when(t == 0)
        def _():
            gcar_s[...] = jnp.zeros_like(gcar_s)
            acar_s[...] = jnp.zeros_like(acar_s)
            head_s[...] = jnp.zeros_like(head_s)
            dvec_ref[...] = jnp.zeros_like(dvec_ref)
            dwa_ref[...] = jnp.zeros_like(dwa_ref)
            dwx_ref[...] = jnp.zeros_like(dwx_ref)

        x = xr_ref[...]
        halo = jnp.where(first_tile, 0.0, xrh_ref[...])
        shifted, xc, xcb, r, ig, sp, a, mult = _lru_pre(x, halo, cw_ref, cb_ref, wa_ref, wx_ref, ba_ref, bx_ref, lam_ref)
        hv = h_ref[...]
        dyv = dy_ref[...]
        gl, glg = _gelu_parts(xg_ref[...])
        dxg_ref[...] = (dyv * hv * glg).astype(BF16)
        acc_a, acc_b = _scan_rows(_shift_up(a, 1, acar_s[...]), dyv * gl, True)
        g = acc_b + acc_a * gcar_s[...]
        hhalo = jnp.where(first_tile, 0.0, hh_ref[...])
        hprev = pltpu.roll(jnp.concatenate([hhalo, hv], axis=0), 1, 0)[8:]
        dmult = g * ig * xc
        dlog_a = a * (g * hprev) - dmult * a * a / mult
        dig = g * mult * xc
        dxc = g * mult * ig
        dzr = (dlog_a * (-LRU_C * sp)) * r * (1.0 - r)
        dzx = dig * ig * (1.0 - ig)
        dzrb, dzxb = dzr.astype(BF16), dzx.astype(BF16)
        dxc = dxc + _nt(dzrb, wa_ref[...]) + _nt(dzxb, wx_ref[...])
        dwa_ref[...] += _tn(xcb, dzrb)
        dwx_ref[...] += _tn(xcb, dzxb)
        dsp = jnp.sum(dlog_a * (-LRU_C * r), axis=0, keepdims=True)
        dlam = dsp * (-jax.nn.sigmoid(-lam_ref[...]))
        vrow = lax.broadcasted_iota(jnp.int32, (8, LANES), 0)
        upd = jnp.where(vrow == 4, jnp.sum(dxc, axis=0, keepdims=True), 0.0)
        upd = jnp.where(vrow == 5, jnp.sum(dzr, axis=0, keepdims=True), upd)
        upd = jnp.where(vrow == 6, jnp.sum(dzx, axis=0, keepdims=True), upd)
        upd = jnp.where(vrow == 7, dlam, upd)
        for k in range(CONV_WIDTH):
            upd = jnp.where(vrow == CONV_WIDTH - 1 - k, jnp.sum(dxc * shifted[k], axis=0, keepdims=True), upd)
        dvec_ref[...] += upd
        ext = jnp.concatenate([dxc, head_s[...]], axis=0)
        dxr = cw_ref[pl.ds(CONV_WIDTH - 1, 1), :] * dxc
        for k in (1, 2, 3):
            dxr = dxr + cw_ref[pl.ds(CONV_WIDTH - 1 - k, 1), :] * pltpu.roll(ext, rows + 8 - k, 0)[:rows]
        dxr_ref[...] = dxr.astype(BF16)
        tmp_s[...] = g
        gcar_s[...] = tmp_s[pl.ds(0, 1), :]
        tmp_s[...] = a
        acar_s[...] = tmp_s[pl.ds(0, 1), :]
        tmp_s[...] = dxc
        head_s[...] = tmp_s[pl.ds(0, 8), :]

    tile, halo, vec, cws, mat = _lru_specs(nt, True)
    return _call(
        body, name="lru_bwd", grid=(D // LANES, nt), ins=[dy, xr, xr, xg, hseq, hseq, cw, cb, wa, wx, ba, bx, lam],
        in_specs=[tile, tile, halo, tile, tile, halo, cws, vec, mat, mat, vec, vec, vec],
        out_specs=[tile, tile, pl.BlockSpec((8, LANES), lambda cb, t: (0, cb)), mat, mat],
        out_shape=[jax.ShapeDtypeStruct((t_tok, D), BF16), jax.ShapeDtypeStruct((t_tok, D), BF16),
                   jax.ShapeDtypeStruct((8, D), F32), jax.ShapeDtypeStruct((D // LANES, LANES, LANES), F32),
                   jax.ShapeDtypeStruct((D // LANES, LANES, LANES), F32)],
        scratch=[pltpu.VMEM((1, LANES), F32), pltpu.VMEM((1, LANES), F32), pltpu.VMEM((8, LANES), F32),
                 pltpu.VMEM((rows, LANES), F32)], carries=carries)


def _t5_bucket_np(rel):
    nb = N_BUCKETS // 2
    max_exact = nb // 2
    ret = np.where(rel > 0, nb, 0)
    n = np.abs(rel)
    nf = np.maximum(n, 1).astype(np.float32)
    large = max_exact + (np.log(nf / np.float32(max_exact)) / np.float32(math.log(MAX_DISTANCE / max_exact))
                         * np.float32(nb - max_exact)).astype(np.int32)
    large = np.minimum(large, nb - 1)
    return ret + np.where(n < max_exact, n, large)


def _bucket_map():
    r = np.arange(QT)[:, None]
    c = np.arange(KW)[None, :]
    j = c - (r // CHUNK) * CHUNK
    band = (j >= 0) & (j < WINDOW + CHUNK)
    return np.where(band, _t5_bucket_np(c - r - WINDOW), -1).astype(np.int32)


def _attn_specs(nt, reverse):
    def tt(i):
        return nt - 1 - i if reverse else i
    kvw = N_KV * LANES
    qs = pl.BlockSpec((QT, D), lambda i: (tt(i), 0))
    cur = pl.BlockSpec((QT, kvw), lambda i: (tt(i), 0))
    prev = pl.BlockSpec((WINDOW, kvw), lambda i: (jnp.maximum(tt(i) * (QT // WINDOW) - 1, 0), 0))
    lse = pl.BlockSpec((QT, LANES), lambda i: (tt(i), 0))
    return qs, cur, prev, lse


REP = N_HEADS // N_KV
SCALE = HEAD_DIM ** -0.5


def _stack_heads(x_ref, g, lo, scale=None):
    parts = []
    for hl in range(REP):
        xs = x_ref[:, pl.ds((2 * g + hl // 2) * LANES, LANES)]
        xs = jnp.where(lo if hl % 2 == 0 else jnp.logical_not(lo), xs, jnp.zeros_like(xs))
        parts.append(xs if scale is None else xs * jnp.asarray(scale, xs.dtype))
    return jnp.concatenate(parts, axis=0)


def _stack_sinks(sink_ref, g, srow):
    sk = jnp.full(srow.shape, sink_ref[REP * g + REP - 1], F32)
    for hl in range(REP - 2, -1, -1):
        sk = jnp.where(srow < (hl + 1) * QT, sink_ref[REP * g + hl], sk)
    return sk


def _attn_fwd(q, kd, vd, bias, sinks, carries=()):
    t_tok = q.shape[0]
    nt = t_tok // QT

    def body(q_ref, kp_ref, kc_ref, vp_ref, vc_ref, bias_ref, sink_ref, o_ref, lse_ref):
        i = pl.program_id(0)
        col = lax.broadcasted_iota(jnp.int32, (1, KW), 1)
        first = jnp.where((i == 0) & (col < WINDOW), NEG_INF, 0.0)
        lane = lax.broadcasted_iota(jnp.int32, (QT, LANES), 1)
        lo = lane < HEAD_DIM
        srow = lax.broadcasted_iota(jnp.int32, (REP * QT, 1), 0)
        lse_t = jnp.zeros((QT, LANES), F32)
        for g in range(N_KV):
            kwin = jnp.concatenate([kp_ref[:, pl.ds(g * LANES, LANES)], kc_ref[:, pl.ds(g * LANES, LANES)]], axis=0)
            vwin = jnp.concatenate([vp_ref[:, pl.ds(g * LANES, LANES)], vc_ref[:, pl.ds(g * LANES, LANES)]], axis=0)
            qst = _stack_heads(q_ref, g, lo, SCALE)
            s = _nt(qst, kwin) + (bias_ref[g] + first)
            sk = _stack_sinks(sink_ref, g, srow)
            m = jnp.maximum(jnp.max(s, axis=-1, keepdims=True), sk)
            e = jnp.exp(s - m)
            l = jnp.sum(e, axis=-1, keepdims=True) + jnp.exp(sk - m)
            p = e / l
            ost = _nn(p.astype(BF16), vwin)
            lse_s = m + jnp.log(l)
            for sl in range(2):
                o_ref[:, pl.ds((2 * g + sl) * LANES, LANES)] = jnp.where(
                    lo, ost[2 * sl * QT:(2 * sl + 1) * QT], ost[(2 * sl + 1) * QT:(2 * sl + 2) * QT]).astype(BF16)
            for hl in range(REP):
                lse_t = jnp.where(lane == REP * g + hl, lse_s[hl * QT:(hl + 1) * QT], lse_t)
        lse_ref[...] = lse_t

    qs, cur, prev, lse = _attn_specs(nt, False)
    return _call(
        body, name="attn_fwd", grid=(nt,), ins=[q, kd, kd, vd, vd, bias, sinks],
        in_specs=[qs, prev, cur, prev, cur, _const_spec((N_KV, REP * QT, KW)), pl.BlockSpec(memory_space=pltpu.SMEM)],
        out_specs=[qs, lse],
        out_shape=[jax.ShapeDtypeStruct((t_tok, D), BF16), jax.ShapeDtypeStruct((t_tok, LANES), F32)],
        vmem=48 * 2 ** 20, carries=carries)


def _attn_bwd(q, kd, vd, o, do, lse, bias, sinks, carries=()):
    t_tok = q.shape[0]
    nt = t_tok // QT
    kvw = N_KV * LANES

    def body(q_ref, kp_ref, kc_ref, vp_ref, vc_ref, o_ref, do_ref, lse_ref, bias_ref, sink_ref,
             dqkv_ref, ds_ref, dsink_ref, kcar_s, vcar_s):
        i = pl.program_id(0)
        tile = nt - 1 - i

        @pl.when(i == 0)
        def _():
            kcar_s[...] = jnp.zeros_like(kcar_s)
            vcar_s[...] = jnp.zeros_like(vcar_s)
            ds_ref[...] = jnp.zeros_like(ds_ref)
            dsink_ref[...] = jnp.zeros_like(dsink_ref)

        col = lax.broadcasted_iota(jnp.int32, (1, KW), 1)
        first = jnp.where((tile == 0) & (col < WINDOW), NEG_INF, 0.0)
        lane = lax.broadcasted_iota(jnp.int32, (QT, LANES), 1)
        lo = lane < HEAD_DIM
        lane_k = lax.broadcasted_iota(jnp.int32, (KW, LANES), 1)
        lane_1 = lax.broadcasted_iota(jnp.int32, (1, LANES), 1)
        srow = lax.broadcasted_iota(jnp.int32, (REP * QT, 1), 0)
        lse_t = lse_ref[...]
        dsink = jnp.zeros((1, LANES), F32)
        for g in range(N_KV):
            kwin = jnp.concatenate([kp_ref[:, pl.ds(g * LANES, LANES)], kc_ref[:, pl.ds(g * LANES, LANES)]], axis=0)
            vwin = jnp.concatenate([vp_ref[:, pl.ds(g * LANES, LANES)], vc_ref[:, pl.ds(g * LANES, LANES)]], axis=0)
            qst = _stack_heads(q_ref, g, lo, SCALE)
            dost = _stack_heads(do_ref, g, lo)
            od = [do_ref[:, pl.ds((2 * g + sl) * LANES, LANES)].astype(F32)
                  * o_ref[:, pl.ds((2 * g + sl) * LANES, LANES)].astype(F32) for sl in range(2)]
            drow = jnp.concatenate([jnp.sum(jnp.where(lo if hl % 2 == 0 else jnp.logical_not(lo), od[hl // 2], 0.0),
                                            axis=-1, keepdims=True) for hl in range(REP)], axis=0)
            lse_s = jnp.concatenate([jnp.sum(jnp.where(lane == REP * g + hl, lse_t, 0.0), axis=-1, keepdims=True)
                                     for hl in range(REP)], axis=0)
            s = _nt(qst, kwin) + (bias_ref[g] + first)
            p = jnp.exp(s - lse_s)
            ds = p * (_nt(dost, vwin) - drow)
            ds_ref[g] += ds
            tsink = -(jnp.exp(_stack_sinks(sink_ref, g, srow) - lse_s) * drow)
            for hl in range(REP):
                dsink = dsink + jnp.where(lane_1 == REP * g + hl,
                                          jnp.sum(tsink[hl * QT:(hl + 1) * QT], axis=0, keepdims=True), 0.0)
            dsb = ds.astype(BF16)
            dqst = _nn(dsb, kwin) * SCALE
            for sl in range(2):
                dqkv_ref[:, pl.ds((2 * g + sl) * LANES, LANES)] = jnp.where(
                    lo, dqst[2 * sl * QT:(2 * sl + 1) * QT], dqst[(2 * sl + 1) * QT:(2 * sl + 2) * QT]).astype(BF16)
            dk_acc = _tn(dsb, qst)
            dv_acc = _tn(p.astype(BF16), dost)
            dk_f = jnp.where(lane_k < HEAD_DIM, dk_acc + pltpu.roll(dk_acc, HEAD_DIM, 1), 0.0)
            dv_f = jnp.where(lane_k < HEAD_DIM, dv_acc + pltpu.roll(dv_acc, HEAD_DIM, 1), 0.0)
            for acc, col0, car in ((dk_f, K0, kcar_s), (dv_f, V0, vcar_s)):
                cs = pl.ds(g * LANES, LANES)
                co = pl.ds(col0 + g * LANES, LANES)
                if QT > WINDOW:
                    dqkv_ref[pl.ds(0, QT - WINDOW), co] = acc[WINDOW:QT].astype(BF16)
                dqkv_ref[pl.ds(QT - WINDOW, WINDOW), co] = (acc[QT:KW] + car[:, cs]).astype(BF16)
                car[:, cs] = acc[0:WINDOW]
        dsink_ref[...] += dsink

    qs, cur, prev, lse_s = _attn_specs(nt, True)
    return _call(
        body, name="attn_bwd", grid=(nt,), ins=[q, kd, kd, vd, vd, o, do, lse, bias, sinks],
        in_specs=[qs, prev, cur, prev, cur, qs, qs, lse_s, _const_spec((N_KV, REP * QT, KW)),
                  pl.BlockSpec(memory_space=pltpu.SMEM)],
        out_specs=[pl.BlockSpec((QT, XR0), lambda i: (nt - 1 - i, 0)), _const_spec((N_KV, REP * QT, KW)),
                   _const_spec((1, LANES))],
        out_shape=[jax.ShapeDtypeStruct((t_tok, XR0), BF16), jax.ShapeDtypeStruct((N_KV, REP * QT, KW), F32),
                   jax.ShapeDtypeStruct((1, LANES), F32)],
        scratch=[pltpu.VMEM((WINDOW, kvw), F32), pltpu.VMEM((WINDOW, kvw), F32)],
        vmem=VMEM_BIG, carries=carries)


def _bias_tile(table, bmap):
    def body(tab_ref, bm_ref, out_ref):
        bm = bm_ref[...]

        def per_head(hd, carry):
            acc = jnp.full((QT, KW), NEG_INF, F32)
            for b in range(N_BUCKETS):
                acc = jnp.where(bm == b, tab_ref[b, hd], acc)
            out_ref[hd] = acc
            return carry

        lax.fori_loop(0, N_HEADS, per_head, 0)

    return pl.pallas_call(
        body, name="bias_tile", out_shape=jax.ShapeDtypeStruct((N_HEADS, QT, KW), F32),
        in_specs=[pl.BlockSpec(memory_space=pltpu.SMEM), pl.BlockSpec(memory_space=pltpu.VMEM)],
        out_specs=pl.BlockSpec(memory_space=pltpu.VMEM))(table, bmap)


def _bias_grad(ds_acc, bmap):
    def body(ds_ref, bm_ref, out_ref):
        row = lax.broadcasted_iota(jnp.int32, (N_BUCKETS, LANES), 0)
        lane = lax.broadcasted_iota(jnp.int32, (N_BUCKETS, LANES), 1)
        bm = bm_ref[...]

        def per_head(hd, res):
            dsv = ds_ref[hd]
            for b in range(N_BUCKETS):
                val = jnp.sum(jnp.sum(jnp.where(bm == b, dsv, 0.0), axis=0, keepdims=True), axis=1, keepdims=True)
                res = jnp.where((row == b) & (lane == hd), val, res)
            return res

        out_ref[...] = lax.fori_loop(0, N_HEADS, per_head, jnp.zeros((N_BUCKETS, LANES), F32))

    return pl.pallas_call(
        body, name="bias_grad", out_shape=jax.ShapeDtypeStruct((N_BUCKETS, LANES), F32),
        in_specs=[pl.BlockSpec(memory_space=pltpu.VMEM), pl.BlockSpec(memory_space=pltpu.VMEM)],
        out_specs=pl.BlockSpec(memory_space=pltpu.VMEM))(ds_acc, bmap)


MIXOUT = ("w_lru_out", "w_attn_out", "w_o")


def _mix_out_fwd(ya_in, o, gs, h, g, wg, tm=512, carries=()):
    t_tok = h.shape[0]
    nt = t_tok // tm
    groups = _groups_of(MIXOUT)

    def body(ya_ref, o_ref, gs_ref, h_ref, g_ref, wg_ref, hout_ref, yao_ref, ybo_ref, z_ref, wa_s, wb_s, wo_s, sems):
        i = pl.program_id(0)

        @pl.when(i == 0)
        def _():
            _load_weights({groups[0]: wg_ref}, list(zip(MIXOUT, (wa_s, wb_s, wo_s))), sems)

        ya = _nn(ya_ref[...], wa_s[...])
        yb = _nn(o_ref[...], wb_s[...])
        yao_ref[...] = ya.astype(BF16)
        ybo_ref[...] = yb.astype(BF16)
        merged = gs_ref[:, pl.ds(0, D)].astype(F32) * ya + gs_ref[:, pl.ds(D, D)].astype(F32) * yb
        z = _nn(merged.astype(BF16), wo_s[...])
        z_ref[...] = z
        zn, _, _ = _rms_fwd(z, g_ref[...])
        hout_ref[...] = h_ref[...] + zn

    return _call(
        body, name="mix_out_fwd", grid=(nt,), ins=[ya_in, o, gs, h, g, wg[groups[0]]],
        in_specs=[_row_spec(tm, D), _row_spec(tm, D), _row_spec(tm, 2 * D), _row_spec(tm, D), _const_spec((1, D)), ANY],
        out_specs=[_row_spec(tm, D)] * 4,
        out_shape=[jax.ShapeDtypeStruct((t_tok, D), F32), jax.ShapeDtypeStruct((t_tok, D), BF16),
                   jax.ShapeDtypeStruct((t_tok, D), BF16), jax.ShapeDtypeStruct((t_tok, D), F32)],
        scratch=[pltpu.VMEM((D, D), BF16)] * 3 + [DMA((3 * NDEV,))],
        vmem=48 * 2 ** 20, carries=carries)


def _mix_out_bwd(dh, z, ya, yb, gs, g, wg, tm=512, carries=()):
    t_tok = dh.shape[0]
    nt = t_tok // tm
    groups = _groups_of(MIXOUT)

    def body(dh_ref, z_ref, ya_ref, yb_ref, gs_ref, g_ref, wg_ref,
             dyain_ref, do_ref, dgpre_ref, dya_ref, dyb_ref, mg_ref, dz_ref, dg_ref, dbg_ref,
             wa_s, wb_s, wo_s, sems):
        i = pl.program_id(0)

        @pl.when(i == 0)
        def _():
            _load_weights({groups[0]: wg_ref}, list(zip(MIXOUT, (wa_s, wb_s, wo_s))), sems)
            dg_ref[...] = jnp.zeros_like(dg_ref)
            dbg_ref[...] = jnp.zeros_like(dbg_ref)

        gv = g_ref[...]
        _, zh, r = _rms_fwd(z_ref[...], gv)
        dz, dg = _rms_bwd(dh_ref[...], zh, r, gv)
        dg_ref[...] += dg
        dzb = dz.astype(BF16)
        dz_ref[...] = dzb
        ga, gb = gs_ref[:, pl.ds(0, D)].astype(F32), gs_ref[:, pl.ds(D, D)].astype(F32)
        ya_v, yb_v = ya_ref[...].astype(F32), yb_ref[...].astype(F32)
        mg_ref[...] = (ga * ya_v + gb * yb_v).astype(BF16)
        dm = _nt(dzb, wo_s[...])
        dga = dm * ya_v * ga * (1.0 - ga)
        dgb = dm * yb_v * gb * (1.0 - gb)
        dgpre_ref[:, pl.ds(0, D)] = dga.astype(BF16)
        dgpre_ref[:, pl.ds(D, D)] = dgb.astype(BF16)
        dbg_ref[:, pl.ds(0, D)] += jnp.sum(dga, axis=0, keepdims=True)
        dbg_ref[:, pl.ds(D, D)] += jnp.sum(dgb, axis=0, keepdims=True)
        dya = (dm * ga).astype(BF16)
        dyb = (dm * gb).astype(BF16)
        dya_ref[...] = dya
        dyb_ref[...] = dyb
        dyain_ref[...] = _nt(dya, wa_s[...])
        do_ref[...] = _nt(dyb, wb_s[...]).astype(BF16)

    bf = jax.ShapeDtypeStruct((t_tok, D), BF16)
    return _call(
        body, name="mix_out_bwd", grid=(nt,), ins=[dh, z, ya, yb, gs, g, wg[groups[0]]],
        in_specs=[_row_spec(tm, D)] * 4 + [_row_spec(tm, 2 * D), _const_spec((1, D)), ANY],
        out_specs=[_row_spec(tm, D), _row_spec(tm, D), _row_spec(tm, 2 * D)] + [_row_spec(tm, D)] * 4
        + [_const_spec((1, D)), _const_spec((1, 2 * D))],
        out_shape=[jax.ShapeDtypeStruct((t_tok, D), F32), bf, jax.ShapeDtypeStruct((t_tok, 2 * D), BF16), bf, bf, bf, bf,
                   jax.ShapeDtypeStruct((1, D), F32), jax.ShapeDtypeStruct((1, 2 * D), F32)],
        scratch=[pltpu.VMEM((D, D), BF16)] * 3 + [DMA((3 * NDEV,))],
        vmem=VMEM_BIG, carries=carries)


def _sum_parts(parts_list):
    n = len(parts_list)
    _, r, c = parts_list[0].shape
    tc = 256

    def body(*refs):
        for p_ref, o_ref in zip(refs[:n], refs[n:]):
            acc = p_ref[0].astype(F32)
            for s in range(1, NDEV):
                acc = acc + p_ref[s].astype(F32)
            o_ref[...] = acc

    return pl.pallas_call(
        body, name=f"sum_parts_{r}", grid=(c // tc,),
        in_specs=[pl.BlockSpec((NDEV, r, tc), lambda i: (0, 0, i))] * n,
        out_specs=[pl.BlockSpec((r, tc), lambda i: (0, i))] * n,
        out_shape=[jax.ShapeDtypeStruct((r, c), F32)] * n,
        compiler_params=pltpu.CompilerParams(dimension_semantics=("arbitrary",), vmem_limit_bytes=48 * 2 ** 20),
    )(*parts_list)


def _adamw_math(w, g, m, v):
    m = ADAM_B1 * m + (1.0 - ADAM_B1) * g
    v = ADAM_B2 * v + (1.0 - ADAM_B2) * (g * g)
    m_hat = m / (1.0 - ADAM_B1 ** ADAM_STEP)
    v_hat = v / (1.0 - ADAM_B2 ** ADAM_STEP)
    delta = -ADAM_LR * (m_hat / (jnp.sqrt(v_hat) + ADAM_EPS) + ADAM_WD * w)
    return delta, m, v


def _sum_ready(parts_list, name, carries=()):
    n = len(parts_list)
    c = parts_list[0].shape[2]
    tc = 2 * LANES

    def body(*refs):
        for p_ref, o_ref in zip(refs[:n], refs[n:]):
            g = p_ref[0].astype(F32)
            for s in range(1, NDEV):
                g = g + p_ref[s].astype(F32)
            o_ref[...] = g

    return _call(
        body, name=name, grid=(c // tc,), ins=list(parts_list),
        in_specs=[pl.BlockSpec((NDEV, p.shape[1], tc), lambda i: (0, 0, i)) for p in parts_list],
        out_shape=[jax.ShapeDtypeStruct((p.shape[1], c), F32) for p in parts_list],
        out_specs=[pl.BlockSpec((p.shape[1], tc), lambda i: (0, i)) for p in parts_list],
        vmem=48 * 2 ** 20, carries=carries)


def _adamw_cols(items, name):
    n = len(items)
    c = items[0][1].shape[1]
    tc = LANES

    def body(*refs):
        for k in range(n):
            g_ref, w_ref, m_ref, v_ref = refs[4 * k:4 * k + 4]
            go_ref, d_ref, nm_ref, nv_ref = refs[4 * n + 4 * k:4 * n + 4 * k + 4]
            g = g_ref[...]
            d, m, v = _adamw_math(w_ref[...], g, m_ref[...], v_ref[...])
            go_ref[...] = g
            d_ref[...] = d
            nm_ref[...] = m
            nv_ref[...] = v

    ins, specs, out_shape = [], [], []
    for g, w, m, v in items:
        r = w.shape[0]
        ins += [g, w, m, v]
        specs += [pl.BlockSpec((r, tc), lambda i: (0, i))] * 4
        out_shape += [jax.ShapeDtypeStruct((r, c), F32)] * 4
    outs, _ = _call(body, name=name, grid=(c // tc,), ins=ins, in_specs=specs, out_shape=out_shape,
                    out_specs=specs, vmem=48 * 2 ** 20)
    return [tuple(outs[4 * k:4 * k + 4]) for k in range(n)]


def _adamw_body(n):
    def body(*refs):
        for k in range(n):
            g_ref, w_ref, m_ref, v_ref = refs[4 * k:4 * k + 4]
            d_ref, nm_ref, nv_ref = refs[4 * n + 3 * k:4 * n + 3 * k + 3]
            d, m, v = _adamw_math(w_ref[...], g_ref[...], m_ref[...], v_ref[...])
            d_ref[...] = d
            nm_ref[...] = m
            nv_ref[...] = v
    return body


def _adamw(items):
    n = len(items)
    r, c = items[0][0].shape
    tr = r if r * c <= 2 ** 18 else max(t for t in range(8, 65, 8) if r % t == 0)
    spec = pl.BlockSpec((tr, c), lambda i: (i, 0))
    outs = pl.pallas_call(
        _adamw_body(n), name=f"adamw_{r}x{c}", grid=(r // tr,),
        in_specs=[spec] * (4 * n), out_specs=[spec] * (3 * n),
        out_shape=[jax.ShapeDtypeStruct((r, c), F32)] * (3 * n),
        compiler_params=pltpu.CompilerParams(dimension_semantics=("arbitrary",), vmem_limit_bytes=40 * 2 ** 20),
    )(*[a for it in items for a in it])
    return [tuple(outs[3 * k:3 * k + 3]) for k in range(n)]


def _adamw_small(items):
    n = len(items)
    vm = pl.BlockSpec(memory_space=pltpu.VMEM)
    outs = pl.pallas_call(
        _adamw_body(n), name="adamw_small", in_specs=[vm] * (4 * n), out_specs=[vm] * (3 * n),
        out_shape=[jax.ShapeDtypeStruct(it[1].shape, F32) for it in items for _ in range(3)],
    )(*[a for it in items for a in it])
    return [tuple(outs[3 * k:3 * k + 3]) for k in range(n)]


def _pack_small(arrs):
    rows, offs = [], []
    total = 0
    for a in arrs:
        flat = a.reshape(-1).astype(F32)
        nr = -(-flat.shape[0] // LANES)
        flat = jnp.pad(flat, (0, nr * LANES - flat.shape[0]))
        rows.append(flat.reshape(nr, LANES))
        offs.append((total, nr))
        total += nr
    pad = -total % 8
    if pad:
        rows.append(jnp.zeros((pad, LANES), F32))
    return jnp.concatenate(rows, axis=0), offs


def _unpack_small(pack, offs, shapes):
    out = []
    for (o, nr), shp in zip(offs, shapes):
        size = int(np.prod(shp))
        out.append(pack[o:o + nr].reshape(-1)[:size].reshape(shp))
    return out


def _sum_small(gathered, rows):
    def body(p_ref, o_ref):
        acc = p_ref[pl.ds(0, rows), :]
        for s in range(1, NDEV):
            acc = acc + p_ref[pl.ds(s * rows, rows), :]
        o_ref[...] = acc

    return pl.pallas_call(
        body, name="sum_small", out_shape=jax.ShapeDtypeStruct((rows, LANES), F32),
        in_specs=[pl.BlockSpec(memory_space=pltpu.VMEM)], out_specs=pl.BlockSpec(memory_space=pltpu.VMEM))(gathered)


def _block_diag(w):
    w = w.reshape(D // LANES, 2, LRU_BLOCK, LRU_BLOCK)
    z = jnp.zeros((D // LANES, LRU_BLOCK, LRU_BLOCK), w.dtype)
    top = jnp.concatenate([w[:, 0], z], axis=2)
    bot = jnp.concatenate([z, w[:, 1]], axis=2)
    return jnp.concatenate([top, bot], axis=1)


def _block_diag_grad(dw):
    a = dw[:, :LRU_BLOCK, :LRU_BLOCK]
    b = dw[:, LRU_BLOCK:, LRU_BLOCK:]
    return jnp.stack([a, b], axis=1).reshape(D // LRU_BLOCK, LRU_BLOCK, LRU_BLOCK)


def kernel(x, ffn1_pre_g, ffn1_w1, ffn1_w3, ffn1_w2, ffn1_post_g, mix_pre_g, w_in, conv_w, conv_b, rg_a_w, rg_a_b, rg_x_w, rg_x_b, lru_lambda, w_lru_out, attn_sinks, rel_bias, w_attn_out, w_gate, b_gate, w_o, mix_post_g, ffn2_pre_g, ffn2_w1, ffn2_w3, ffn2_w2, ffn2_post_g, loss_target, m_ffn1_pre_g, m_ffn1_w1, m_ffn1_w3, m_ffn1_w2, m_ffn1_post_g, m_mix_pre_g, m_w_in, m_conv_w, m_conv_b, m_rg_a_w, m_rg_a_b, m_rg_x_w, m_rg_x_b, m_lru_lambda, m_w_lru_out, m_attn_sinks, m_rel_bias, m_w_attn_out, m_w_gate, m_b_gate, m_w_o, m_mix_post_g, m_ffn2_pre_g, m_ffn2_w1, m_ffn2_w3, m_ffn2_w2, m_ffn2_post_g, v_ffn1_pre_g, v_ffn1_w1, v_ffn1_w3, v_ffn1_w2, v_ffn1_post_g, v_mix_pre_g, v_w_in, v_conv_w, v_conv_b, v_rg_a_w, v_rg_a_b, v_rg_x_w, v_rg_x_b, v_lru_lambda, v_w_lru_out, v_attn_sinks, v_rel_bias, v_w_attn_out, v_w_gate, v_b_gate, v_w_o, v_mix_post_g, v_ffn2_pre_g, v_ffn2_w1, v_ffn2_w3, v_ffn2_w2, v_ffn2_post_g):
    names = ["ffn1_pre_g", "ffn1_w1", "ffn1_w3", "ffn1_w2", "ffn1_post_g", "mix_pre_g", "w_in", "conv_w", "conv_b",
             "rg_a_w", "rg_a_b", "rg_x_w", "rg_x_b", "lru_lambda", "w_lru_out", "attn_sinks", "rel_bias", "w_attn_out",
             "w_gate", "b_gate", "w_o", "mix_post_g", "ffn2_pre_g", "ffn2_w1", "ffn2_w3", "ffn2_w2", "ffn2_post_g"]
    ws = dict(zip(names, (ffn1_pre_g, ffn1_w1, ffn1_w3, ffn1_w2, ffn1_post_g, mix_pre_g, w_in, conv_w, conv_b, rg_a_w,
                          rg_a_b, rg_x_w, rg_x_b, lru_lambda, w_lru_out, attn_sinks, rel_bias, w_attn_out, w_gate,
                          b_gate, w_o, mix_post_g, ffn2_pre_g, ffn2_w1, ffn2_w3, ffn2_w2, ffn2_post_g)))
    ms = dict(zip(names, (m_ffn1_pre_g, m_ffn1_w1, m_ffn1_w3, m_ffn1_w2, m_ffn1_post_g, m_mix_pre_g, m_w_in, m_conv_w,
                          m_conv_b, m_rg_a_w, m_rg_a_b, m_rg_x_w, m_rg_x_b, m_lru_lambda, m_w_lru_out, m_attn_sinks,
                          m_rel_bias, m_w_attn_out, m_w_gate, m_b_gate, m_w_o, m_mix_post_g, m_ffn2_pre_g, m_ffn2_w1,
                          m_ffn2_w3, m_ffn2_w2, m_ffn2_post_g)))
    vs = dict(zip(names, (v_ffn1_pre_g, v_ffn1_w1, v_ffn1_w3, v_ffn1_w2, v_ffn1_post_g, v_mix_pre_g, v_w_in, v_conv_w,
                          v_conv_b, v_rg_a_w, v_rg_a_b, v_rg_x_w, v_rg_x_b, v_lru_lambda, v_w_lru_out, v_attn_sinks,
                          v_rel_bias, v_w_attn_out, v_w_gate, v_b_gate, v_w_o, v_mix_post_g, v_ffn2_pre_g, v_ffn2_w1,
                          v_ffn2_w3, v_ffn2_w2, v_ffn2_post_g)))
    me = 4 * lax.axis_index("x") + 2 * lax.axis_index("y") + lax.axis_index("c")
    vec = lambda n: ws[n].reshape(1, -1)

    def shard2d(name):
        if name == "conv":
            row = lax.bitcast_convert_type(conv_w.reshape(CONV_WIDTH, LANES), BF16).reshape(1, D)
            return jnp.concatenate([row, jnp.zeros((LOC["conv"][2] - 1, D), BF16)], axis=0)
        a = ws[name].reshape(ws[name].shape[-2], ws[name].shape[-1])
        return (a.T if name in COL_SHARDED else a).astype(BF16)

    packs = {g: jnp.concatenate([shard2d(n) for n, _ in members], axis=0) for g, members in GROUPS}

    wg = {}
    _, ((wg["ffn1"],),) = _call(None, name="allgather_ffn1", grid=(), ins=[], in_specs=[], out_shape=[], out_specs=[],
                                carries=[_AllGather(packs["ffn1"])])
    conv_rows = wg["ffn1"].reshape(NDEV, GROUP_ROWS["ffn1"], D)[:, LOC["conv"][1]]
    cw = jnp.transpose(lax.bitcast_convert_type(conv_rows.reshape(NDEV, CONV_WIDTH, LANES, 2), F32),
                       (1, 0, 2)).reshape(CONV_WIDTH, D)
    bmap = jnp.asarray(_bucket_map())
    bias = _bias_tile(rel_bias, bmap).reshape(N_KV, REP * QT, KW)
    sinks = attn_sinks.reshape(N_HEADS)
    wa_bd = _block_diag(rg_a_w.reshape(D // LRU_BLOCK, LRU_BLOCK, LRU_BLOCK)).astype(BF16)
    wx_bd = _block_diag(rg_x_w.reshape(D // LRU_BLOCK, LRU_BLOCK, LRU_BLOCK)).astype(BF16)
    lru_args = (cw, vec("conv_b"), wa_bd, wx_bd, vec("rg_a_b"), vec("rg_x_b"), vec("lru_lambda"))
    x2, tgt = x[0], loss_target[0]

    (h1, a1, b1, f1, nb1), ((wg["mixin"],),) = _ffn_fwd(
        x2, vec("ffn1_pre_g"), vec("ffn1_post_g"), wg, ("ffn1_w1", "ffn1_w3", "ffn1_w2"),
        carries=[_AllGather(packs["mixin"])])
    (q, kd, vd, xr, xg, gs, ub), ((wg["mixout"],),) = _mix_proj_fwd(
        h1, vec("mix_pre_g"), vec("b_gate"), wg, carries=[_AllGather(packs["mixout"])])
    (ya_in, hseq), ((wg["ffn2a"],),) = _lru_fwd(xr, xg, *lru_args, carries=[_AllGather(packs["ffn2a"])])
    (o, lse), ((wg["ffn2b"],),) = _attn_fwd(q, kd, vd, bias, sinks, carries=[_AllGather(packs["ffn2b"])])
    (h2, ya, yb, z), _ = _mix_out_fwd(ya_in, o, gs, h1, vec("mix_post_g"), wg)
    (_, a2, b2, f2, nb2, dy, loss_part), _ = _ffn_fwd(h2, vec("ffn2_pre_g"), vec("ffn2_post_g"), wg,
                                                     ("ffn2_w1", "ffn2_w3", "ffn2_w2"), target=tgt)

    gsm, parts = {}, {}
    rs = lambda *grads_: [_ReduceScatterSend(list(grads_))]
    ffr = FF // NDEV
    (dab, s_act, dfb, gsm["ffn2_post_g"]), _ = _ffn_bwd_a(dy, f2, a2, b2, vec("ffn2_post_g"), wg, "ffn2_w2")
    g_w2, _ = _dw(s_act, dfb, FF // 2, "dw_ffn2_w2")
    g_w13, ((parts["ffn2_w2"],),) = _dw(dab, nb2, D // 2, "dw_ffn2_w13", carries=rs(g_w2))
    (dh2, gsm["ffn2_pre_g"]), ((parts["ffn2_w1"],),) = _ffn_bwd_dx(
        dab, h2, dy, vec("ffn2_pre_g"), wg, "ffn2_w1", "ffn2_w3", carries=rs((g_w13, 0, ffr, ffr)))
    (dya_in, do, dgpre, dya, dyb, mg, dzb, gsm["mix_post_g"], gsm["b_gate"]), ((parts["ffn2_w3"],),) = _mix_out_bwd(
        dh2, z, ya, yb, gs, vec("mix_post_g"), wg, carries=rs((g_w13, FF, ffr, ffr)))
    g_wa, _ = _dw(ya_in, dya, D // 2, "dw_w_lru_out")
    g_wb, _ = _dw(o, dyb, D // 2, "dw_w_attn_out")
    g_wo, _ = _dw(mg, dzb, D // 2, "dw_w_o")
    g_wgate, _ = _dw(dgpre, ub, D // 2, "dw_w_gate")
    (dqkv, ds_acc, dsink), (mixout_parts,) = _attn_bwd(
        q, kd, vd, o, do, lse, bias, sinks, carries=rs(g_wa, g_wb, g_wo, g_wgate))
    parts["w_lru_out"], parts["w_attn_out"], parts["w_o"], parts["w_gate"] = mixout_parts
    (dxr, dxg, dvec, dwa, dwx), _ = _lru_bwd(dya_in, xr, xg, hseq, *lru_args)
    g_qkv, _ = _dw(dqkv, ub, D // 2, "dw_w_in_qkv")
    g_xr, _ = _dw(dxr, ub, D // 2, "dw_w_in_xr")
    g_xg, _ = _dw(dxg, ub, D // 2, "dw_w_in_xg")
    g_win = jnp.concatenate(
        [g_qkv[:D]] + [g_qkv[c0 + gi * LANES:c0 + gi * LANES + HEAD_DIM] for c0 in (K0, V0) for gi in range(N_KV)]
        + [g_xr, g_xg], axis=0)
    wir, wir_a = IN_W // NDEV, 336
    (dh1, gsm["mix_pre_g"]), ((win_a,),) = _mix_proj_bwd(
        dqkv, dxr, dxg, dgpre, h1, dh2, vec("mix_pre_g"), wg, carries=rs((g_win, 0, wir, wir_a)))
    gsm["conv_w"] = dvec[0:CONV_WIDTH]
    gsm["conv_b"], gsm["rg_a_b"], gsm["rg_x_b"], gsm["lru_lambda"] = dvec[4], dvec[5], dvec[6], dvec[7]
    gsm["rg_a_w"] = _block_diag_grad(dwa)
    gsm["rg_x_w"] = _block_diag_grad(dwx)
    gsm["attn_sinks"] = dsink[0, :N_HEADS]
    gsm["rel_bias"] = _bias_grad(ds_acc.reshape(N_HEADS, QT, KW), bmap)[:, :N_HEADS]
    late = ("ffn1_post_g", "ffn1_pre_g")
    early = tuple(n for n in SMALL if n not in late)
    early_pack, early_offs = _pack_small([gsm[n] for n in early])
    (dab, s_act, dfb, gsm["ffn1_post_g"]), ((win_b,), (early_all,)) = _ffn_bwd_a(
        dh1, f1, a1, b1, vec("ffn1_post_g"), wg, "ffn1_w2",
        carries=rs((g_win, wir_a, wir, wir - wir_a)) + [_AllGather(early_pack)])
    parts["w_in"] = jnp.concatenate([win_a, win_b], axis=1)
    g_w2, _ = _dw(s_act, dfb, FF // 2, "dw_ffn1_w2")
    g_w13, ((parts["ffn1_w2"],),) = _dw(dab, nb1, D // 2, "dw_ffn1_w13", carries=rs(g_w2))
    chip_w13 = _pair_sum(g_w13, [(0, ffr), (FF, ffr)])
    (grad_x, gsm["ffn1_pre_g"]), ((parts["ffn1_w1"], parts["ffn1_w3"]),) = _ffn_bwd_dx(
        dab, x2, dh1, vec("ffn1_pre_g"), wg, "ffn1_w1", "ffn1_w3",
        carries=[_ReduceScatterSend([(chip_w13, 0, ffr, ffr), (chip_w13, FF, ffr, ffr)],
                                    zeros=jnp.zeros((ffr, D), BF16))])
    late_pack, late_offs = _pack_small([gsm[n] for n in late] + [loss_part])

    grads, delta, new_m, new_v = {}, {}, {}, {}
    t_form = [n for n in COL_SHARDED if ws[n].shape[-1] % LANES]
    view = {n: (lambda a: a.reshape(a.shape[-2], a.shape[-1]).T) if n in t_form
            else (lambda a: a.reshape(a.shape[-2], a.shape[-1])) for n in BIG}
    unview = {n: (lambda a: a.T) if n in t_form else (lambda a: a) for n in BIG}

    group = [n for n in BIG if n != "w_gate"]
    sums, ((late_all,),) = _sum_ready([parts[n] for n in group], "sum_ready", [_AllGather(late_pack)])
    res = _adamw_cols([(g, view[n](ws[n]), view[n](ms[n]), view[n](vs[n])) for n, g in zip(group, sums)], "adamw_big")
    for n, quad in zip(group, res):
        grads[n], delta[n], new_m[n], new_v[n] = (unview[n](a) for a in quad)
    (g_gate_t,) = _sum_parts([parts["w_gate"]])
    grads["w_gate"] = g_gate_t.T
    ((delta["w_gate"], new_m["w_gate"], new_v["w_gate"]),) = _adamw(
        [(grads["w_gate"], view["w_gate"](ws["w_gate"]), view["w_gate"](ms["w_gate"]), view["w_gate"](vs["w_gate"]))])
    for group, gathered, offs in ((early, early_all, early_offs), (late, late_all, late_offs)):
        total = _sum_small(gathered, gathered.shape[0] // NDEV)
        shapes = [(CONV_WIDTH, D) if n == "conv_w" else ws[n].shape for n in group]
        if group is late:
            shapes = shapes + [(1, LANES)]
        unpacked = _unpack_small(total, offs, shapes)
        if group is late:
            loss = unpacked.pop()[0, 0]
        for n, g in zip(group, unpacked):
            grads[n] = g
    grads["conv_w"] = lax.dynamic_slice(grads["conv_w"], (0, me * LANES), (CONV_WIDTH, LANES)).reshape(conv_w.shape)

    flat2d = lambda a: a.reshape(-1, a.shape[-1])
    res = _adamw_small([(flat2d(grads[n].reshape(ws[n].shape)), flat2d(ws[n]), flat2d(ms[n]), flat2d(vs[n]))
                        for n in SMALL])
    for n, (d_, m_, v_) in zip(SMALL, res):
        delta[n], new_m[n], new_v[n] = d_, m_, v_

    outs = [loss, grad_x.reshape(x.shape)]
    for src in (grads, delta, new_m, new_v):
        outs += [src[n].reshape(ws[n].shape) for n in names]
    return tuple(outs)
```

```python
import functools
import math
import operator

import numpy as np
import jax
import jax.numpy as jnp
from jax import lax
from jax.experimental import pallas as pl
from jax.experimental.pallas import tpu as pltpu

F32, BF16 = jnp.float32, jnp.bfloat16

NDEV = 8
D = 1024
FF = 2816
N_HEADS, N_KV, HEAD_DIM = 16, 4, 64
CHUNK, WINDOW = 64, 128
N_BUCKETS, MAX_DISTANCE = 32, 128
LRU_BLOCK = 64
CONV_WIDTH = 4
LRU_C = 8.0
RMS_EPS = 1e-6
NEG_INF = -1e30
LANES = 128
QT = 128
KW = QT + WINDOW
LRU_ROWS = 512
IN_W = D + 2 * N_KV * HEAD_DIM + 2 * D
INP_W = D + 2 * N_KV * LANES + 2 * D
VMEM_BIG = 58 * 2 ** 20

ADAM_LR, ADAM_B1, ADAM_B2, ADAM_EPS, ADAM_WD, ADAM_STEP = 0.001, 0.9, 0.999, 1e-08, 0.01, 10

GROUPS = (("ffn1", (("ffn1_w1", FF // NDEV), ("ffn1_w3", FF // NDEV), ("ffn1_w2", FF // NDEV), ("conv", 16))),
          ("mixin", (("w_in", IN_W // NDEV), ("w_gate", 2 * D // NDEV))),
          ("mixout", (("w_lru_out", D // NDEV), ("w_attn_out", D // NDEV), ("w_o", D // NDEV))),
          ("ffn2a", (("ffn2_w1", FF // NDEV), ("ffn2_w3", FF // NDEV))),
          ("ffn2b", (("ffn2_w2", FF // NDEV),)))
LOC, GROUP_ROWS = {}, {}
for _g, _members in GROUPS:
    _o = 0
    for _n, _r in _members:
        LOC[_n] = (_g, _o, _r)
        _o += _r
    GROUP_ROWS[_g] = _o
BIG = tuple(n for _, members in GROUPS for n, _ in members if n != "conv")
COL_SHARDED = ("ffn1_w1", "ffn1_w3", "w_in", "w_gate", "ffn2_w1", "ffn2_w3")

SMALL = ("ffn1_pre_g", "ffn1_post_g", "mix_pre_g", "conv_w", "conv_b", "rg_a_w", "rg_a_b", "rg_x_w", "rg_x_b",
         "lru_lambda", "attn_sinks", "rel_bias", "b_gate", "mix_post_g", "ffn2_pre_g", "ffn2_post_g")

MESH = pl.DeviceIdType.MESH
ANY = pl.BlockSpec(memory_space=pl.ANY)
DMA = pltpu.SemaphoreType.DMA


def _nn(a, b):
    return lax.dot_general(a, b, (((1,), (0,)), ((), ())), preferred_element_type=F32)


def _nt(a, b):
    return lax.dot_general(a, b, (((1,), (1,)), ((), ())), preferred_element_type=F32)


def _tn(a, b):
    return lax.dot_general(a, b, (((0,), (0,)), ((), ())), preferred_element_type=F32)


def _rms_fwd(x, g):
    r = lax.rsqrt(jnp.mean(x * x, axis=-1, keepdims=True) + RMS_EPS)
    xh = x * r
    return xh * g, xh, r


def _rms_bwd(dn, xh, r, g):
    dxh = dn * g
    dx = r * (dxh - xh * jnp.mean(dxh * xh, axis=-1, keepdims=True))
    return dx, jnp.sum(dn * xh, axis=0, keepdims=True)


def _row_spec(tm, c):
    return pl.BlockSpec((tm, c), lambda i: (i, 0))


def _const_spec(shape):
    nd = len(shape)
    return pl.BlockSpec(shape, lambda i: (0,) * nd)


class _AllGather:
    def __init__(self, shard):
        self.m, n = shard.shape
        self.ins = [shard]
        self.out_shape = [jax.ShapeDtypeStruct((NDEV * self.m, n), shard.dtype)]
        self.scratch = [DMA((7,)), DMA((7,)), DMA]

    def _copies(self, ins, outs, scr, all_of_them):
        x_ref, out_ref = ins[0], outs[0]
        send_sems, recv_sems, local_sem = scr
        x, y, c = lax.axis_index("x"), lax.axis_index("y"), lax.axis_index("c")
        me, sibling = (x, y, c), (x, y, 1 - c)
        chips = [(1 - x, y), (x, 1 - y), (1 - x, 1 - y)]
        m = self.m

        def rows(px, py, pc):
            return out_ref.at[pl.ds((4 * px + 2 * py + pc) * m, m), :]

        def copy(k, block, to, src=None):
            return pltpu.make_async_remote_copy(
                src_ref=rows(*block) if src is None else src, dst_ref=rows(*block),
                send_sem=send_sems.at[k], recv_sem=recv_sems.at[k], device_id=to, device_id_type=MESH)

        mine = pltpu.make_async_copy(x_ref, rows(*me), local_sem)
        first = [copy(0, me, sibling, src=x_ref)] + [copy(1 + j, me, (*chip, c), src=x_ref)
                                                     for j, chip in enumerate(chips)]
        if not all_of_them:
            return mine, first
        passed = [copy(4 + j, (*chip, c), sibling) for j, chip in enumerate(chips)]
        landed = [copy(1 + j, (*chip, c), me) for j, chip in enumerate(chips)]
        from_sibling = [copy(0, sibling, me)] + [copy(4 + j, (*chip, 1 - c), me) for j, chip in enumerate(chips)]
        return mine, first, passed, landed, from_sibling

    def start(self, ins, outs, scr):
        mine, first = self._copies(ins, outs, scr, False)
        mine.start()
        for cp in first:
            cp.start()

    def finish(self, ins, outs, scr):
        mine, first, passed, landed, from_sibling = self._copies(ins, outs, scr, True)
        for cp_in, cp_on in zip(landed, passed):
            cp_in.wait_recv()
            cp_on.start()
        for cp in from_sibling:
            cp.wait_recv()
        for cp in first + passed:
            cp.wait_send()
        mine.wait()


class _ReduceScatterSend:
    def __init__(self, grads, chip_sums=False):
        grads = [g if isinstance(g, tuple) else (g, 0, g.shape[0] // NDEV, g.shape[0] // NDEV) for g in grads]
        self.peers = (2, 4, 6) if chip_sums else tuple(range(1, NDEV))
        self.nw = len(grads)
        self.base = [b for _, b, _, _ in grads]
        self.stride = [s for _, _, s, _ in grads]
        self.rows = [r for _, _, _, r in grads]
        self.ins = [g for g, _, _, _ in grads]
        self.out_shape = [jax.ShapeDtypeStruct((NDEV, r, g.shape[1]), g.dtype) for g, _, _, r in grads]
        self.scratch = [DMA((self.nw, NDEV - 1)), DMA((self.nw, NDEV - 1)), DMA((self.nw,))]

    def _copies(self, g_refs, r_refs, scr, want):
        send_sems, recv_sems, local_sems = scr
        x, y, c = lax.axis_index("x"), lax.axis_index("y"), lax.axis_index("c")
        me = 4 * x + 2 * y + c
        rows, base, stride = self.rows, self.base, self.stride
        out = []
        if want == "local":
            for w in range(self.nw):
                out.append(pltpu.make_async_copy(g_refs[w].at[pl.ds(base[w] + me * stride[w], rows[w])],
                                                 r_refs[w].at[me], local_sems.at[w]))
            return out
        for k in self.peers:
            px, py, pc = x ^ (k >> 2), y ^ ((k >> 1) & 1), c ^ (k & 1)
            peer = 4 * px + 2 * py + pc
            for w in range(self.nw):
                sems = dict(send_sem=send_sems.at[w, k - 1], recv_sem=recv_sems.at[w, k - 1],
                            device_id=(px, py, pc), device_id_type=MESH)
                if want == "send":
                    out.append(pltpu.make_async_remote_copy(
                        src_ref=g_refs[w].at[pl.ds(base[w] + peer * stride[w], rows[w])], dst_ref=r_refs[w].at[me],
                        **sems))
                else:
                    out.append(pltpu.make_async_remote_copy(
                        src_ref=g_refs[w].at[pl.ds(0, rows[w])], dst_ref=r_refs[w].at[peer], **sems))
        return out

    def start(self, ins, outs, scr):
        for cp in self._copies(ins, outs, scr, "local") + self._copies(ins, outs, scr, "send"):
            cp.start()

    def finish(self, ins, outs, scr):
        for cp in self._copies(ins, outs, scr, "recv"):
            cp.wait_recv()
        for cp in self._copies(ins, outs, scr, "send"):
            cp.wait_send()
        for cp in self._copies(ins, outs, scr, "local"):
            cp.wait()


def _call(body, *, name, grid, ins, in_specs, out_shape, out_specs, scratch=(), vmem=None, carries=()):
    n_in, n_out, n_scr = len(ins), len(out_shape), len(scratch)
    ng = len(grid)

    def split(refs):
        pos = [0]

        def take(k):
            part = refs[pos[0]:pos[0] + k]
            pos[0] += k
            return part

        i_refs = take(n_in)
        c_in = [take(len(c.ins)) for c in carries]
        o_refs = take(n_out)
        c_out = [take(len(c.out_shape)) for c in carries]
        s_refs = take(n_scr)
        c_scr = [take(len(c.scratch)) for c in carries]
        return i_refs, o_refs, s_refs, list(zip(carries, c_in, c_out, c_scr))

    def full(*refs):
        i_refs, o_refs, s_refs, cparts = split(refs)
        if ng == 0:
            for c, a, b, s in cparts:
                c.start(a, b, s)
            for c, a, b, s in cparts:
                c.finish(a, b, s)
            return
        ids = [pl.program_id(a) for a in range(ng)]
        if cparts:
            @pl.when(functools.reduce(operator.and_, [i == 0 for i in ids]))
            def _():
                for c, a, b, s in cparts:
                    c.start(a, b, s)

        body(*i_refs, *o_refs, *s_refs)
        if cparts:
            @pl.when(functools.reduce(operator.and_, [i == g - 1 for i, g in zip(ids, grid)]))
            def _():
                for c, a, b, s in cparts:
                    c.finish(a, b, s)

    all_ins = list(ins) + [a for c in carries for a in c.ins]
    all_in_specs = list(in_specs) + [ANY for c in carries for _ in c.ins]
    all_out_shape = list(out_shape) + [s for c in carries for s in c.out_shape]
    all_out_specs = list(out_specs) + [ANY for c in carries for _ in c.out_shape]
    all_scratch = list(scratch) + [s for c in carries for s in c.scratch]
    kwargs = dict(grid=grid) if ng else {}
    outs = pl.pallas_call(
        full, name=name, in_specs=all_in_specs, out_specs=all_out_specs, out_shape=all_out_shape,
        scratch_shapes=all_scratch,
        compiler_params=pltpu.CompilerParams(dimension_semantics=("arbitrary",) * ng if ng else None,
                                             vmem_limit_bytes=vmem),
        **kwargs)(*all_ins)
    outs = list(outs)
    res, pos = outs[:n_out], n_out
    carried = []
    for c in carries:
        carried.append(outs[pos:pos + len(c.out_shape)])
        pos += len(c.out_shape)
    return res, carried


def _pair_sum(g, items):
    rows = items[0][1]
    n = len(items) * (NDEV // 2)
    width = g.shape[1]

    def body(g_ref, out_ref, got_ref, mine_v, got_v, send_sems, recv_sems, ld_sems, st_sems):
        x, y, c = lax.axis_index("x"), lax.axis_index("y"), lax.axis_index("c")
        keep = [base + (2 * ch + c) * rows for base, _ in items for ch in range(NDEV // 2)]
        give = [base + (2 * ch + 1 - c) * rows for base, _ in items for ch in range(NDEV // 2)]

        def remote(j, r):
            return pltpu.make_async_remote_copy(
                src_ref=g_ref.at[pl.ds(r, rows)], dst_ref=got_ref.at[pl.ds(r, rows)], send_sem=send_sems.at[j],
                recv_sem=recv_sems.at[j], device_id=(x, y, 1 - c), device_id_type=MESH)

        sends = [remote(j, r) for j, r in enumerate(give)]
        for cp in sends:
            cp.start()
        loads = [pltpu.make_async_copy(g_ref.at[pl.ds(r, rows)], mine_v.at[j], ld_sems.at[0, j])
                 for j, r in enumerate(keep)]
        for cp in loads:
            cp.start()
        stores = []
        for j, r in enumerate(keep):
            remote(j, r).wait_recv()
            cp = pltpu.make_async_copy(got_ref.at[pl.ds(r, rows)], got_v.at[j], ld_sems.at[1, j])
            cp.start()
            loads[j].wait()
            cp.wait()
            mine_v[j] = (mine_v[j].astype(F32) + got_v[j].astype(F32)).astype(BF16)
            st = pltpu.make_async_copy(mine_v.at[j], out_ref.at[pl.ds(r, rows)], st_sems.at[j])
            st.start()
            stores.append(st)
        for cp in sends:
            cp.wait_send()
        for cp in stores:
            cp.wait()

    out, _ = pl.pallas_call(
        body, name="pair_sum", in_specs=[ANY], out_specs=[ANY, ANY],
        out_shape=[jax.ShapeDtypeStruct(g.shape, g.dtype), jax.ShapeDtypeStruct(g.shape, g.dtype)],
        scratch_shapes=[pltpu.VMEM((n, rows, width), g.dtype), pltpu.VMEM((n, rows, width), g.dtype),
                        DMA((n,)), DMA((n,)), DMA((2, n)), DMA((n,))],
        compiler_params=pltpu.CompilerParams(vmem_limit_bytes=40 * 2 ** 20))(g)
    return out


def _groups_of(names):
    out = []
    for n in names:
        if LOC[n][0] not in out:
            out.append(LOC[n][0])
    return out


def _weight_pieces(name):
    g, off, rows = LOC[name]
    return [(d * GROUP_ROWS[g] + off, d * rows, rows) for d in range(NDEV)]


def _win_pieces():
    kv = N_KV * HEAD_DIM
    pieces = [(0, 0, D)]
    for g0, d0 in ((D, D), (D + kv, D + N_KV * LANES)):
        for g in range(N_KV):
            for half in range(2):
                pieces.append((g0 + g * HEAD_DIM, d0 + g * LANES + half * HEAD_DIM, HEAD_DIM))
    pieces.append((D + 2 * kv, D + 2 * N_KV * LANES, D))
    pieces.append((D + 2 * kv + D, D + 2 * N_KV * LANES + D, D))
    grp, off, rows = LOC["w_in"]
    out = []
    for g0, d0, n in pieces:
        while n > 0:
            dev, loc = divmod(g0, rows)
            m = min(n, rows - loc)
            out.append((dev * GROUP_ROWS[grp] + off + loc, d0, m))
            g0, d0, n = g0 + m, d0 + m, n - m
    return out


def _start_loads(src_ref, dst_ref, pieces, sems, base):
    cps = []
    for j, (s, d, n) in enumerate(pieces):
        cp = pltpu.make_async_copy(src_ref.at[pl.ds(s, n)], dst_ref.at[pl.ds(d, n)], sems.at[base + j])
        cp.start()
        cps.append(cp)
    return cps


def _load_weights(wrefs, targets, sems):
    cps, base = [], 0
    for name, dst in targets:
        pieces = _win_pieces() if name == "w_in" else _weight_pieces(name)
        cps += _start_loads(wrefs[LOC[name][0]], dst, pieces, sems, base)
        base += len(pieces)
    for cp in cps:
        cp.wait()


def _n_pieces(names):
    return sum(len(_win_pieces()) if n == "w_in" else NDEV for n in names)


def _dw(lhs, rhs, chunk, name, carries=()):
    nq = 4
    if lhs.ndim == 3:
        nch, t_tok, chunk = lhs.shape
        c = nch * chunk
        tq = t_tok // nq
        lhs_specs = [pl.BlockSpec((None, tq, chunk), lambda i, k=k: (i, k, 0)) for k in range(nq)]
    else:
        t_tok, c = lhs.shape
        tq = t_tok // nq
        lhs_specs = [pl.BlockSpec((tq, chunk), lambda i, k=k: (k, i)) for k in range(nq)]

    def body(*refs):
        lhs_refs, (rhs_ref, out_ref, rhs_s, sems) = refs[:nq], refs[nq:]
        cps = [pltpu.make_async_copy(rhs_ref.at[pl.ds(k * tq, tq)], rhs_s.at[pl.ds(k * tq, tq)], sems.at[k])
               for k in range(nq)]
        first = pl.program_id(0) == 0

        @pl.when(first)
        def _():
            for cp in cps:
                cp.start()

        for o in range(0, chunk, D // 2):
            n = min(D // 2, chunk - o)
            acc = None
            for k in range(nq):
                if o == 0:
                    @pl.when(first)
                    def _():
                        cps[k].wait()

                part = _tn(lhs_refs[k][:, pl.ds(o, n)], rhs_s[pl.ds(k * tq, tq), :])
                acc = part if acc is None else acc + part
            out_ref[pl.ds(o, n), :] = acc.astype(BF16)

    (out,), carried = _call(
        body, name=name, grid=(c // chunk,), ins=[lhs] * nq + [rhs],
        in_specs=lhs_specs + [ANY],
        out_specs=[pl.BlockSpec((chunk, D), lambda i: (i, 0))],
        out_shape=[jax.ShapeDtypeStruct((c, D), BF16)],
        scratch=[pltpu.VMEM((t_tok, D), BF16), DMA((nq,))], vmem=VMEM_BIG, carries=carries)
    return out, carried


def _silu_parts(a):
    sig = jax.nn.sigmoid(a)
    return sig, a * sig


FC = 256


def _ffn_fwd(h, gpre, gpost, wg, names, target=None, tm=512, carries=()):
    t_tok = h.shape[0]
    nt = t_tok // tm
    with_loss = target is not None
    groups = _groups_of(names)

    def body(*refs):
        refs = list(refs)
        h_ref, gpre_ref, gpost_ref = refs[:3]
        del refs[:3]
        tgt_ref = refs.pop(0) if with_loss else None
        wrefs = dict(zip(groups, refs[:len(groups)]))
        del refs[:len(groups)]
        hout_ref, a_ref, b_ref, f_ref, nb_ref = refs[:5]
        del refs[:5]
        if with_loss:
            dy_ref, loss_ref = refs[:2]
            del refs[:2]
        w1_s, w3_s, w2_s, sems = refs
        i = pl.program_id(0)

        @pl.when(i == 0)
        def _():
            _load_weights(wrefs, list(zip(names, (w1_s, w3_s, w2_s))), sems)
            if with_loss:
                loss_ref[...] = jnp.zeros_like(loss_ref)

        x = h_ref[...]
        n, _, _ = _rms_fwd(x, gpre_ref[...])
        nb = n.astype(BF16)
        nb_ref[...] = nb
        f = jnp.zeros((tm, D), F32)
        for c0 in range(0, FF, FC):
            a = _nt(nb, w1_s[pl.ds(c0, FC), :])
            b = _nt(nb, w3_s[pl.ds(c0, FC), :])
            _, sl = _silu_parts(a)
            a_ref[:, pl.ds(c0, FC)] = a.astype(BF16)
            b_ref[:, pl.ds(c0, FC)] = b.astype(BF16)
            f = f + _nn((sl * b).astype(BF16), w2_s[pl.ds(c0, FC), :])
        f_ref[...] = f
        fn, _, _ = _rms_fwd(f, gpost_ref[...])
        y = x + 0.5 * fn
        hout_ref[...] = y
        if with_loss:
            err = y - tgt_ref[...]
            dy_ref[...] = err * (1.0 / D)
            loss_ref[...] += jnp.sum(jnp.sum(err * err, axis=-1, keepdims=True), axis=0, keepdims=True) * (0.5 / D)

    ins = [h, gpre, gpost] + ([target] if with_loss else []) + [wg[g] for g in groups]
    in_specs = [_row_spec(tm, D), _const_spec((1, D)), _const_spec((1, D))]
    in_specs += ([_row_spec(tm, D)] if with_loss else []) + [ANY] * len(groups)
    out_shape = [jax.ShapeDtypeStruct((t_tok, D), F32), jax.ShapeDtypeStruct((t_tok, FF), BF16),
                 jax.ShapeDtypeStruct((t_tok, FF), BF16), jax.ShapeDtypeStruct((t_tok, D), F32),
                 jax.ShapeDtypeStruct((t_tok, D), BF16)]
    out_specs = [_row_spec(tm, D), _row_spec(tm, FF), _row_spec(tm, FF), _row_spec(tm, D), _row_spec(tm, D)]
    if with_loss:
        out_shape += [jax.ShapeDtypeStruct((t_tok, D), F32), jax.ShapeDtypeStruct((1, LANES), F32)]
        out_specs += [_row_spec(tm, D), _const_spec((1, LANES))]
    return _call(body, name="ffn_fwd_" + names[0][:4], grid=(nt,), ins=ins, in_specs=in_specs,
                 out_shape=out_shape, out_specs=out_specs,
                 scratch=[pltpu.VMEM((FF, D), BF16)] * 3 + [DMA((3 * NDEV,))], vmem=VMEM_BIG, carries=carries)


FH = FF // 2
HALF_PIECES = ((0, 256), (256, 256), (512, 256), (768, 256), (1024, 256), (1280, 128))


def _ffn_bwd_a(dh, f, a, b, gpost, wg, name_w2, tm=512, carries=()):
    t_tok = dh.shape[0]
    nt = t_tok // tm
    groups = _groups_of([name_w2])

    def body(dh_ref, f_ref, a_ref, b_ref, gpost_ref, wg_ref, dab_ref, s_ref, df_ref, dgp_ref, w2_s, sems):
        i = pl.program_id(0)

        @pl.when(i == 0)
        def _():
            _load_weights({groups[0]: wg_ref}, [(name_w2, w2_s)], sems)
            dgp_ref[...] = jnp.zeros_like(dgp_ref)

        fv = f_ref[...]
        _, fh, r = _rms_fwd(fv, gpost_ref[...])
        df, dg = _rms_bwd(0.5 * dh_ref[...], fh, r, gpost_ref[...])
        dgp_ref[...] += dg
        dfb = df.astype(BF16)
        df_ref[...] = dfb
        for half in range(2):
            for o, n in HALF_PIECES:
                c0 = half * FH + o
                ds = _nt(dfb, w2_s[pl.ds(c0, n), :])
                av = a_ref[:, pl.ds(c0, n)].astype(F32)
                bv = b_ref[:, pl.ds(c0, n)].astype(F32)
                sig, sl = _silu_parts(av)
                dab_ref[half, :, pl.ds(o, n)] = (ds * bv * (sig * (1.0 + av * (1.0 - sig)))).astype(BF16)
                dab_ref[2 + half, :, pl.ds(o, n)] = (ds * sl).astype(BF16)
                s_ref[half, :, pl.ds(o, n)] = (sl * bv).astype(BF16)

    return _call(
        body, name="ffn_bwd_a_" + name_w2[:4], grid=(nt,), ins=[dh, f, a, b, gpost, wg[groups[0]]],
        in_specs=[_row_spec(tm, D), _row_spec(tm, D), _row_spec(tm, FF), _row_spec(tm, FF), _const_spec((1, D)), ANY],
        out_specs=[pl.BlockSpec((4, tm, FH), lambda i: (0, i, 0)), pl.BlockSpec((2, tm, FH), lambda i: (0, i, 0)),
                   _row_spec(tm, D), _const_spec((1, D))],
        out_shape=[jax.ShapeDtypeStruct((4, t_tok, FH), BF16), jax.ShapeDtypeStruct((2, t_tok, FH), BF16),
                   jax.ShapeDtypeStruct((t_tok, D), BF16), jax.ShapeDtypeStruct((1, D), F32)],
        scratch=[pltpu.VMEM((FF, D), BF16), DMA((NDEV,))],
        vmem=VMEM_BIG, carries=carries)


def _ffn_bwd_dx(dab, h, dh, gpre, wg, name_w1, name_w3, tm=512, carries=()):
    t_tok = dh.shape[0]
    nt = t_tok // tm
    groups = _groups_of([name_w1, name_w3])

    def body(*refs):
        dab_refs, (h_ref, dh_ref, gpre_ref) = refs[:4], refs[4:7]
        wrefs = dict(zip(groups, refs[7:7 + len(groups)]))
        dhin_ref, dgp_ref, w13_s, sems = refs[7 + len(groups):]
        i = pl.program_id(0)

        @pl.when(i == 0)
        def _():
            _load_weights(wrefs, [(name_w1, w13_s.at[pl.ds(0, FF)]), (name_w3, w13_s.at[pl.ds(FF, FF)])], sems)
            dgp_ref[...] = jnp.zeros_like(dgp_ref)

        g = gpre_ref[...]
        _, xh, r = _rms_fwd(h_ref[...], g)
        dn = _nn(dab_refs[0][...], w13_s[pl.ds(0, FH), :])
        for k in range(1, 4):
            dn = dn + _nn(dab_refs[k][...], w13_s[pl.ds(k * FH, FH), :])
        dx, dg = _rms_bwd(dn, xh, r, g)
        dgp_ref[...] += dg
        dhin_ref[...] = dh_ref[...] + dx

    return _call(
        body, name="ffn_bwd_dx_" + name_w1[:4], grid=(nt,), ins=[dab] * 4 + [h, dh, gpre] + [wg[g] for g in groups],
        in_specs=[pl.BlockSpec((None, tm, FH), lambda i, k=k: (k, i, 0)) for k in range(4)]
        + [_row_spec(tm, D), _row_spec(tm, D), _const_spec((1, D))] + [ANY] * len(groups),
        out_specs=[_row_spec(tm, D), _const_spec((1, D))],
        out_shape=[jax.ShapeDtypeStruct((t_tok, D), F32), jax.ShapeDtypeStruct((1, D), F32)],
        scratch=[pltpu.VMEM((2 * FF, D), BF16), DMA((2 * NDEV,))],
        vmem=VMEM_BIG, carries=carries)


Q0, K0, V0, XR0, XG0 = 0, D, D + N_KV * LANES, D + 2 * N_KV * LANES, 2 * D + 2 * N_KV * LANES


def _mix_proj_fwd(h, g, bgate, wg, tm=512, carries=()):
    t_tok = h.shape[0]
    nt = t_tok // tm
    names = ("w_in", "w_gate")
    groups = _groups_of(names)

    def body(h_ref, g_ref, bg_ref, wg_ref, q_ref, k_ref, v_ref, xr_ref, xg_ref, gs_ref, ub_ref, win_s, wgt_s, sems):
        i = pl.program_id(0)

        @pl.when(i == 0)
        def _():
            _load_weights({groups[0]: wg_ref}, [("w_in", win_s), ("w_gate", wgt_s)], sems)

        n, _, _ = _rms_fwd(h_ref[...], g_ref[...])
        nb = n.astype(BF16)
        ub_ref[...] = nb
        q_ref[...] = _nt(nb, win_s[pl.ds(Q0, D), :]).astype(BF16)
        k_ref[...] = _nt(nb, win_s[pl.ds(K0, N_KV * LANES), :]).astype(BF16)
        v_ref[...] = _nt(nb, win_s[pl.ds(V0, N_KV * LANES), :]).astype(BF16)
        xr_ref[...] = _nt(nb, win_s[pl.ds(XR0, D), :])
        xg_ref[...] = _nt(nb, win_s[pl.ds(XG0, D), :])
        gs_ref[...] = jax.nn.sigmoid(_nt(nb, wgt_s[...]) + bg_ref[...]).astype(BF16)

    kvw = N_KV * LANES
    return _call(
        body, name="mix_proj_fwd", grid=(nt,), ins=[h, g, bgate, wg[groups[0]]],
        in_specs=[_row_spec(tm, D), _const_spec((1, D)), _const_spec((1, 2 * D)), ANY],
        out_specs=[_row_spec(tm, D), _row_spec(tm, kvw), _row_spec(tm, kvw), _row_spec(tm, D), _row_spec(tm, D),
                   _row_spec(tm, 2 * D), _row_spec(tm, D)],
        out_shape=[jax.ShapeDtypeStruct((t_tok, D), BF16), jax.ShapeDtypeStruct((t_tok, kvw), BF16),
                   jax.ShapeDtypeStruct((t_tok, kvw), BF16), jax.ShapeDtypeStruct((t_tok, D), F32),
                   jax.ShapeDtypeStruct((t_tok, D), F32), jax.ShapeDtypeStruct((t_tok, 2 * D), BF16),
                   jax.ShapeDtypeStruct((t_tok, D), BF16)],
        scratch=[pltpu.VMEM((INP_W, D), BF16), pltpu.VMEM((2 * D, D), BF16), DMA((_n_pieces(names),))],
        vmem=VMEM_BIG, carries=carries)


def _mix_proj_bwd(dqkv, dxr, dxg, dgpre, h, dh, g, wg, tm=512, carries=()):
    t_tok = h.shape[0]
    nt = t_tok // tm
    names = ("w_in", "w_gate")
    groups = _groups_of(names)

    def body(dqkv_ref, dxr_ref, dxg_ref, dgp_ref, h_ref, dh_ref, g_ref, wg_ref, dhin_ref, dg_ref, win_s, wgt_s, sems):
        i = pl.program_id(0)

        @pl.when(i == 0)
        def _():
            _load_weights({groups[0]: wg_ref}, [("w_in", win_s), ("w_gate", wgt_s)], sems)
            dg_ref[...] = jnp.zeros_like(dg_ref)

        gv = g_ref[...]
        _, xh, r = _rms_fwd(h_ref[...], gv)
        du = _nn(dgp_ref[...], wgt_s[...])
        du = du + _nn(dqkv_ref[...], win_s[pl.ds(Q0, XR0), :])
        du = du + _nn(dxr_ref[...], win_s[pl.ds(XR0, D), :])
        du = du + _nn(dxg_ref[...], win_s[pl.ds(XG0, D), :])
        dx, dg = _rms_bwd(du, xh, r, gv)
        dg_ref[...] += dg
        dhin_ref[...] = dh_ref[...] + dx

    return _call(
        body, name="mix_proj_bwd", grid=(nt,), ins=[dqkv, dxr, dxg, dgpre, h, dh, g, wg[groups[0]]],
        in_specs=[_row_spec(tm, XR0), _row_spec(tm, D), _row_spec(tm, D), _row_spec(tm, 2 * D), _row_spec(tm, D),
                  _row_spec(tm, D), _const_spec((1, D)), ANY],
        out_specs=[_row_spec(tm, D), _const_spec((1, D))],
        out_shape=[jax.ShapeDtypeStruct((t_tok, D), F32), jax.ShapeDtypeStruct((1, D), F32)],
        scratch=[pltpu.VMEM((INP_W, D), BF16), pltpu.VMEM((2 * D, D), BF16), DMA((_n_pieces(names),))],
        vmem=VMEM_BIG, carries=carries)


def _shift_down(x, d, fill):
    row = lax.broadcasted_iota(jnp.int32, x.shape, 0)
    return jnp.where(row >= d, pltpu.roll(x, d, 0), fill)


def _shift_up(x, d, fill):
    rows = x.shape[0]
    row = lax.broadcasted_iota(jnp.int32, x.shape, 0)
    return jnp.where(row < rows - d, pltpu.roll(x, rows - d, 0), fill)


def _scan_rows(a, b, reverse):
    rows = a.shape[0]
    d = 1
    while d < rows:
        if d < 8:
            shift = _shift_up if reverse else _shift_down
            b = a * shift(b, d, 0.0) + b
            a = a * shift(a, d, 1.0)
        elif reverse:
            b = jnp.concatenate([a[:rows - d] * b[d:] + b[:rows - d], b[rows - d:]], axis=0)
            a = jnp.concatenate([a[:rows - d] * a[d:], a[rows - d:]], axis=0)
        else:
            b = jnp.concatenate([b[:d], a[d:] * b[:rows - d] + b[d:]], axis=0)
            a = jnp.concatenate([a[:d], a[d:] * a[:rows - d]], axis=0)
        d *= 2
    return a, b


def _softplus(x):
    return jnp.maximum(x, 0.0) + jnp.log(1.0 + jnp.exp(-jnp.abs(x)))


_GELU_C = math.sqrt(2.0 / math.pi)


def _gelu_parts(x):
    th = jnp.tanh(_GELU_C * (x + 0.044715 * x * x * x))
    val = 0.5 * x * (1.0 + th)
    grad = 0.5 * (1.0 + th) + 0.5 * x * (1.0 - th * th) * _GELU_C * (1.0 + 3.0 * 0.044715 * x * x)
    return val, grad


def _lru_pre(x, halo, cw_ref, cb_ref, wa_ref, wx_ref, ba_ref, bx_ref, lam_ref):
    ext = jnp.concatenate([halo, x], axis=0)
    shifted = [x] + [pltpu.roll(ext, k, 0)[8:] for k in (1, 2, 3)]
    xc = cb_ref[...] + cw_ref[pl.ds(CONV_WIDTH - 1, 1), :] * x
    for k in (1, 2, 3):
        xc = xc + cw_ref[pl.ds(CONV_WIDTH - 1 - k, 1), :] * shifted[k]
    xcb = xc.astype(BF16)
    r = jax.nn.sigmoid(_nn(xcb, wa_ref[...]) + ba_ref[...])
    ig = jax.nn.sigmoid(_nn(xcb, wx_ref[...]) + bx_ref[...])
    sp = _softplus(-lam_ref[...])
    log_a = -LRU_C * r * sp
    a = jnp.exp(log_a)
    th = jnp.tanh(log_a)
    mult = jnp.sqrt(-2.0 * th / (1.0 - th))
    return shifted, xc, xcb, r, ig, sp, a, mult


def _lru_specs(nt, reverse):
    def tt(t):
        return nt - 1 - t if reverse else t
    tile = pl.BlockSpec((LRU_ROWS, LANES), lambda cb, t: (tt(t), cb))
    halo = pl.BlockSpec((8, LANES), lambda cb, t: (jnp.maximum(tt(t) * (LRU_ROWS // 8) - 1, 0), cb))
    vec = pl.BlockSpec((1, LANES), lambda cb, t: (0, cb))
    cw = pl.BlockSpec((CONV_WIDTH, LANES), lambda cb, t: (0, cb))
    mat = pl.BlockSpec((None, LANES, LANES), lambda cb, t: (cb, 0, 0))
    return tile, halo, vec, cw, mat


def _lru_fwd(xr, xg, cw, cb, wa, wx, ba, bx, lam, carries=()):
    t_tok = xr.shape[0]
    nt = t_tok // LRU_ROWS
    rows = LRU_ROWS

    def body(xr_ref, xg_ref, cw_ref, cb_ref, wa_ref, wx_ref, ba_ref, bx_ref, lam_ref, y_ref, h_ref, tail_s, hc_s):
        t = pl.program_id(1)

        @pl.when(t == 0)
        def _():
            tail_s[...] = jnp.zeros_like(tail_s)
            hc_s[...] = jnp.zeros_like(hc_s)

        x = xr_ref[...]
        _, xc, _, _, ig, _, a, mult = _lru_pre(x, tail_s[...], cw_ref, cb_ref, wa_ref, wx_ref, ba_ref, bx_ref, lam_ref)
        tail_s[...] = xr_ref[pl.ds(rows - 8, 8), :]
        acc_a, acc_b = _scan_rows(a, mult * (ig * xc), False)
        hv = acc_b + acc_a * hc_s[...]
        h_ref[...] = hv
        hc_s[...] = h_ref[pl.ds(rows - 1, 1), :]
        gl, _ = _gelu_parts(xg_ref[...])
        y_ref[...] = (hv * gl).astype(BF16)

    tile, _, vec, cws, mat = _lru_specs(nt, False)
    return _call(
        body, name="lru_fwd", grid=(D // LANES, nt), ins=[xr, xg, cw, cb, wa, wx, ba, bx, lam],
        in_specs=[tile, tile, cws, vec, mat, mat, vec, vec, vec],
        out_specs=[tile, tile],
        out_shape=[jax.ShapeDtypeStruct((t_tok, D), BF16), jax.ShapeDtypeStruct((t_tok, D), F32)],
        scratch=[pltpu.VMEM((8, LANES), F32), pltpu.VMEM((1, LANES), F32)], carries=carries)


def _lru_bwd(dy, xr, xg, hseq, cw, cb, wa, wx, ba, bx, lam, carries=()):
    t_tok = xr.shape[0]
    nt = t_tok // LRU_ROWS
    rows = LRU_ROWS

    def body(dy_ref, xr_ref, xrh_ref, xg_ref, h_ref, hh_ref, cw_ref, cb_ref, wa_ref, wx_ref, ba_ref, bx_ref, lam_ref,
             dxr_ref, dxg_ref, dvec_ref, dwa_ref, dwx_ref, gcar_s, acar_s, head_s, tmp_s):
        t = pl.program_id(1)
        first_tile = t == nt - 1

        @pl.when(t == 0)
        def _():
            gcar_s[...] = jnp.zeros_like(gcar_s)
            acar_s[...] = jnp.zeros_like(acar_s)
            head_s[...] = jnp.zeros_like(head_s)
            dvec_ref[...] = jnp.zeros_like(dvec_ref)
            dwa_ref[...] = jnp.zeros_like(dwa_ref)
            dwx_ref[...] = jnp.zeros_like(dwx_ref)

        x = xr_ref[...]
        halo = jnp.where(first_tile, 0.0, xrh_ref[...])
        shifted, xc, xcb, r, ig, sp, a, mult = _lru_pre(x, halo, cw_ref, cb_ref, wa_ref, wx_ref, ba_ref, bx_ref, lam_ref)
        hv = h_ref[...]
        dyv = dy_ref[...]
        gl, glg = _gelu_parts(xg_ref[...])
        dxg_ref[...] = (dyv * hv * glg).astype(BF16)
        acc_a, acc_b = _scan_rows(_shift_up(a, 1, acar_s[...]), dyv * gl, True)
        g = acc_b + acc_a * gcar_s[...]
        hhalo = jnp.where(first_tile, 0.0, hh_ref[...])
        hprev = pltpu.roll(jnp.concatenate([hhalo, hv], axis=0), 1, 0)[8:]
        dmult = g * ig * xc
        dlog_a = a * (g * hprev) - dmult * a * a / mult
        dig = g * mult * xc
        dxc = g * mult * ig
        dzr = (dlog_a * (-LRU_C * sp)) * r * (1.0 - r)
        dzx = dig * ig * (1.0 - ig)
        dzrb, dzxb = dzr.astype(BF16), dzx.astype(BF16)
        dxc = dxc + _nt(dzrb, wa_ref[...]) + _nt(dzxb, wx_ref[...])
        dwa_ref[...] += _tn(xcb, dzrb)
        dwx_ref[...] += _tn(xcb, dzxb)
        dsp = jnp.sum(dlog_a * (-LRU_C * r), axis=0, keepdims=True)
        dlam = dsp * (-jax.nn.sigmoid(-lam_ref[...]))
        vrow = lax.broadcasted_iota(jnp.int32, (8, LANES), 0)
        upd = jnp.where(vrow == 4, jnp.sum(dxc, axis=0, keepdims=True), 0.0)
        upd = jnp.where(vrow == 5, jnp.sum(dzr, axis=0, keepdims=True), upd)
        upd = jnp.where(vrow == 6, jnp.sum(dzx, axis=0, keepdims=True), upd)
        upd = jnp.where(vrow == 7, dlam, upd)
        for k in range(CONV_WIDTH):
            upd = jnp.where(vrow == CONV_WIDTH - 1 - k, jnp.sum(dxc * shifted[k], axis=0, keepdims=True), upd)
        dvec_ref[...] += upd
        ext = jnp.concatenate([dxc, head_s[...]], axis=0)
        dxr = cw_ref[pl.ds(CONV_WIDTH - 1, 1), :] * dxc
        for k in (1, 2, 3):
            dxr = dxr + cw_ref[pl.ds(CONV_WIDTH - 1 - k, 1), :] * pltpu.roll(ext, rows + 8 - k, 0)[:rows]
        dxr_ref[...] = dxr.astype(BF16)
        tmp_s[...] = g
        gcar_s[...] = tmp_s[pl.ds(0, 1), :]
        tmp_s[...] = a
        acar_s[...] = tmp_s[pl.ds(0, 1), :]
        tmp_s[...] = dxc
        head_s[...] = tmp_s[pl.ds(0, 8), :]

    tile, halo, vec, cws, mat = _lru_specs(nt, True)
    return _call(
        body, name="lru_bwd", grid=(D // LANES, nt), ins=[dy, xr, xr, xg, hseq, hseq, cw, cb, wa, wx, ba, bx, lam],
        in_specs=[tile, tile, halo, tile, tile, halo, cws, vec, mat, mat, vec, vec, vec],
        out_specs=[tile, tile, pl.BlockSpec((8, LANES), lambda cb, t: (0, cb)), mat, mat],
        out_shape=[jax.ShapeDtypeStruct((t_tok, D), BF16), jax.ShapeDtypeStruct((t_tok, D), BF16),
                   jax.ShapeDtypeStruct((8, D), F32), jax.ShapeDtypeStruct((D // LANES, LANES, LANES), F32),
                   jax.ShapeDtypeStruct((D // LANES, LANES, LANES), F32)],
        scratch=[pltpu.VMEM((1, LANES), F32), pltpu.VMEM((1, LANES), F32), pltpu.VMEM((8, LANES), F32),
                 pltpu.VMEM((rows, LANES), F32)], carries=carries)


def _t5_bucket_np(rel):
    nb = N_BUCKETS // 2
    max_exact = nb // 2
    ret = np.where(rel > 0, nb, 0)
    n = np.abs(rel)
    nf = np.maximum(n, 1).astype(np.float32)
    large = max_exact + (np.log(nf / np.float32(max_exact)) / np.float32(math.log(MAX_DISTANCE / max_exact))
                         * np.float32(nb - max_exact)).astype(np.int32)
    large = np.minimum(large, nb - 1)
    return ret + np.where(n < max_exact, n, large)


def _bucket_map():
    r = np.arange(QT)[:, None]
    c = np.arange(KW)[None, :]
    j = c - (r // CHUNK) * CHUNK
    band = (j >= 0) & (j < WINDOW + CHUNK)
    return np.where(band, _t5_bucket_np(c - r - WINDOW), -1).astype(np.int32)


def _attn_specs(nt, reverse):
    def tt(i):
        return nt - 1 - i if reverse else i
    kvw = N_KV * LANES
    qs = pl.BlockSpec((QT, D), lambda i: (tt(i), 0))
    cur = pl.BlockSpec((QT, kvw), lambda i: (tt(i), 0))
    prev = pl.BlockSpec((WINDOW, kvw), lambda i: (jnp.maximum(tt(i) * (QT // WINDOW) - 1, 0), 0))
    lse = pl.BlockSpec((QT, LANES), lambda i: (tt(i), 0))
    return qs, cur, prev, lse


REP = N_HEADS // N_KV
SCALE = HEAD_DIM ** -0.5


def _stack_heads(x_ref, g, lo, scale=None):
    parts = []
    for hl in range(REP):
        xs = x_ref[:, pl.ds((2 * g + hl // 2) * LANES, LANES)]
        xs = jnp.where(lo if hl % 2 == 0 else jnp.logical_not(lo), xs, jnp.zeros_like(xs))
        parts.append(xs if scale is None else xs * jnp.asarray(scale, xs.dtype))
    return jnp.concatenate(parts, axis=0)


def _stack_sinks(sink_ref, g, srow):
    sk = jnp.full(srow.shape, sink_ref[REP * g + REP - 1], F32)
    for hl in range(REP - 2, -1, -1):
        sk = jnp.where(srow < (hl + 1) * QT, sink_ref[REP * g + hl], sk)
    return sk


def _attn_fwd(q, kd, vd, bias, sinks, carries=()):
    t_tok = q.shape[0]
    nt = t_tok // QT

    def body(q_ref, kp_ref, kc_ref, vp_ref, vc_ref, bias_ref, sink_ref, o_ref, lse_ref):
        i = pl.program_id(0)
        col = lax.broadcasted_iota(jnp.int32, (1, KW), 1)
        first = jnp.where((i == 0) & (col < WINDOW), NEG_INF, 0.0)
        lane = lax.broadcasted_iota(jnp.int32, (QT, LANES), 1)
        lo = lane < HEAD_DIM
        srow = lax.broadcasted_iota(jnp.int32, (REP * QT, 1), 0)
        lse_t = jnp.zeros((QT, LANES), F32)
        for g in range(N_KV):
            kwin = jnp.concatenate([kp_ref[:, pl.ds(g * LANES, LANES)], kc_ref[:, pl.ds(g * LANES, LANES)]], axis=0)
            vwin = jnp.concatenate([vp_ref[:, pl.ds(g * LANES, LANES)], vc_ref[:, pl.ds(g * LANES, LANES)]], axis=0)
            qst = _stack_heads(q_ref, g, lo, SCALE)
            s = _nt(qst, kwin) + (bias_ref[g] + first)
            sk = _stack_sinks(sink_ref, g, srow)
            m = jnp.maximum(jnp.max(s, axis=-1, keepdims=True), sk)
            e = jnp.exp(s - m)
            l = jnp.sum(e, axis=-1, keepdims=True) + jnp.exp(sk - m)
            p = e / l
            ost = _nn(p.astype(BF16), vwin)
            lse_s = m + jnp.log(l)
            for sl in range(2):
                o_ref[:, pl.ds((2 * g + sl) * LANES, LANES)] = jnp.where(
                    lo, ost[2 * sl * QT:(2 * sl + 1) * QT], ost[(2 * sl + 1) * QT:(2 * sl + 2) * QT]).astype(BF16)
            for hl in range(REP):
                lse_t = jnp.where(lane == REP * g + hl, lse_s[hl * QT:(hl + 1) * QT], lse_t)
        lse_ref[...] = lse_t

    qs, cur, prev, lse = _attn_specs(nt, False)
    return _call(
        body, name="attn_fwd", grid=(nt,), ins=[q, kd, kd, vd, vd, bias, sinks],
        in_specs=[qs, prev, cur, prev, cur, _const_spec((N_KV, REP * QT, KW)), pl.BlockSpec(memory_space=pltpu.SMEM)],
        out_specs=[qs, lse],
        out_shape=[jax.ShapeDtypeStruct((t_tok, D), BF16), jax.ShapeDtypeStruct((t_tok, LANES), F32)],
        vmem=48 * 2 ** 20, carries=carries)


def _attn_bwd(q, kd, vd, o, do, lse, bias, sinks, carries=()):
    t_tok = q.shape[0]
    nt = t_tok // QT
    kvw = N_KV * LANES

    def body(q_ref, kp_ref, kc_ref, vp_ref, vc_ref, o_ref, do_ref, lse_ref, bias_ref, sink_ref,
             dqkv_ref, ds_ref, dsink_ref, kcar_s, vcar_s):
        i = pl.program_id(0)
        tile = nt - 1 - i

        @pl.when(i == 0)
        def _():
            kcar_s[...] = jnp.zeros_like(kcar_s)
            vcar_s[...] = jnp.zeros_like(vcar_s)
            ds_ref[...] = jnp.zeros_like(ds_ref)
            dsink_ref[...] = jnp.zeros_like(dsink_ref)

        col = lax.broadcasted_iota(jnp.int32, (1, KW), 1)
        first = jnp.where((tile == 0) & (col < WINDOW), NEG_INF, 0.0)
        lane = lax.broadcasted_iota(jnp.int32, (QT, LANES), 1)
        lo = lane < HEAD_DIM
        lane_k = lax.broadcasted_iota(jnp.int32, (KW, LANES), 1)
        lane_1 = lax.broadcasted_iota(jnp.int32, (1, LANES), 1)
        srow = lax.broadcasted_iota(jnp.int32, (REP * QT, 1), 0)
        lse_t = lse_ref[...]
        dsink = jnp.zeros((1, LANES), F32)
        for g in range(N_KV):
            kwin = jnp.concatenate([kp_ref[:, pl.ds(g * LANES, LANES)], kc_ref[:, pl.ds(g * LANES, LANES)]], axis=0)
            vwin = jnp.concatenate([vp_ref[:, pl.ds(g * LANES, LANES)], vc_ref[:, pl.ds(g * LANES, LANES)]], axis=0)
            qst = _stack_heads(q_ref, g, lo, SCALE)
            dost = _stack_heads(do_ref, g, lo)
            od = [do_ref[:, pl.ds((2 * g + sl) * LANES, LANES)].astype(F32)
                  * o_ref[:, pl.ds((2 * g + sl) * LANES, LANES)].astype(F32) for sl in range(2)]
            drow = jnp.concatenate([jnp.sum(jnp.where(lo if hl % 2 == 0 else jnp.logical_not(lo), od[hl // 2], 0.0),
                                            axis=-1, keepdims=True) for hl in range(REP)], axis=0)
            lse_s = jnp.concatenate([jnp.sum(jnp.where(lane == REP * g + hl, lse_t, 0.0), axis=-1, keepdims=True)
                                     for hl in range(REP)], axis=0)
            s = _nt(qst, kwin) + (bias_ref[g] + first)
            p = jnp.exp(s - lse_s)
            ds = p * (_nt(dost, vwin) - drow)
            ds_ref[g] += ds
            tsink = -(jnp.exp(_stack_sinks(sink_ref, g, srow) - lse_s) * drow)
            for hl in range(REP):
                dsink = dsink + jnp.where(lane_1 == REP * g + hl,
                                          jnp.sum(tsink[hl * QT:(hl + 1) * QT], axis=0, keepdims=True), 0.0)
            dsb = ds.astype(BF16)
            dqst = _nn(dsb, kwin) * SCALE
            for sl in range(2):
                dqkv_ref[:, pl.ds((2 * g + sl) * LANES, LANES)] = jnp.where(
                    lo, dqst[2 * sl * QT:(2 * sl + 1) * QT], dqst[(2 * sl + 1) * QT:(2 * sl + 2) * QT]).astype(BF16)
            dk_acc = _tn(dsb, qst)
            dv_acc = _tn(p.astype(BF16), dost)
            dk_f = jnp.where(lane_k < HEAD_DIM, dk_acc + pltpu.roll(dk_acc, HEAD_DIM, 1), 0.0)
            dv_f = jnp.where(lane_k < HEAD_DIM, dv_acc + pltpu.roll(dv_acc, HEAD_DIM, 1), 0.0)
            for acc, col0, car in ((dk_f, K0, kcar_s), (dv_f, V0, vcar_s)):
                cs = pl.ds(g * LANES, LANES)
                co = pl.ds(col0 + g * LANES, LANES)
                if QT > WINDOW:
                    dqkv_ref[pl.ds(0, QT - WINDOW), co] = acc[WINDOW:QT].astype(BF16)
                dqkv_ref[pl.ds(QT - WINDOW, WINDOW), co] = (acc[QT:KW] + car[:, cs]).astype(BF16)
                car[:, cs] = acc[0:WINDOW]
        dsink_ref[...] += dsink

    qs, cur, prev, lse_s = _attn_specs(nt, True)
    return _call(
        body, name="attn_bwd", grid=(nt,), ins=[q, kd, kd, vd, vd, o, do, lse, bias, sinks],
        in_specs=[qs, prev, cur, prev, cur, qs, qs, lse_s, _const_spec((N_KV, REP * QT, KW)),
                  pl.BlockSpec(memory_space=pltpu.SMEM)],
        out_specs=[pl.BlockSpec((QT, XR0), lambda i: (nt - 1 - i, 0)), _const_spec((N_KV, REP * QT, KW)),
                   _const_spec((1, LANES))],
        out_shape=[jax.ShapeDtypeStruct((t_tok, XR0), BF16), jax.ShapeDtypeStruct((N_KV, REP * QT, KW), F32),
                   jax.ShapeDtypeStruct((1, LANES), F32)],
        scratch=[pltpu.VMEM((WINDOW, kvw), F32), pltpu.VMEM((WINDOW, kvw), F32)],
        vmem=VMEM_BIG, carries=carries)


def _bias_tile(table, bmap):
    def body(tab_ref, bm_ref, out_ref):
        bm = bm_ref[...]

        def per_head(hd, carry):
            acc = jnp.full((QT, KW), NEG_INF, F32)
            for b in range(N_BUCKETS):
                acc = jnp.where(bm == b, tab_ref[b, hd], acc)
            out_ref[hd] = acc
            return carry

        lax.fori_loop(0, N_HEADS, per_head, 0)

    return pl.pallas_call(
        body, name="bias_tile", out_shape=jax.ShapeDtypeStruct((N_HEADS, QT, KW), F32),
        in_specs=[pl.BlockSpec(memory_space=pltpu.SMEM), pl.BlockSpec(memory_space=pltpu.VMEM)],
        out_specs=pl.BlockSpec(memory_space=pltpu.VMEM))(table, bmap)


def _bias_grad(ds_acc, bmap):
    def body(ds_ref, bm_ref, out_ref):
        row = lax.broadcasted_iota(jnp.int32, (N_BUCKETS, LANES), 0)
        lane = lax.broadcasted_iota(jnp.int32, (N_BUCKETS, LANES), 1)
        bm = bm_ref[...]

        def per_head(hd, res):
            dsv = ds_ref[hd]
            for b in range(N_BUCKETS):
                val = jnp.sum(jnp.sum(jnp.where(bm == b, dsv, 0.0), axis=0, keepdims=True), axis=1, keepdims=True)
                res = jnp.where((row == b) & (lane == hd), val, res)
            return res

        out_ref[...] = lax.fori_loop(0, N_HEADS, per_head, jnp.zeros((N_BUCKETS, LANES), F32))

    return pl.pallas_call(
        body, name="bias_grad", out_shape=jax.ShapeDtypeStruct((N_BUCKETS, LANES), F32),
        in_specs=[pl.BlockSpec(memory_space=pltpu.VMEM), pl.BlockSpec(memory_space=pltpu.VMEM)],
        out_specs=pl.BlockSpec(memory_space=pltpu.VMEM))(ds_acc, bmap)


MIXOUT = ("w_lru_out", "w_attn_out", "w_o")


def _mix_out_fwd(ya_in, o, gs, h, g, wg, tm=512, carries=()):
    t_tok = h.shape[0]
    nt = t_tok // tm
    groups = _groups_of(MIXOUT)

    def body(ya_ref, o_ref, gs_ref, h_ref, g_ref, wg_ref, hout_ref, yao_ref, ybo_ref, z_ref, wa_s, wb_s, wo_s, sems):
        i = pl.program_id(0)

        @pl.when(i == 0)
        def _():
            _load_weights({groups[0]: wg_ref}, list(zip(MIXOUT, (wa_s, wb_s, wo_s))), sems)

        ya = _nn(ya_ref[...], wa_s[...])
        yb = _nn(o_ref[...], wb_s[...])
        yao_ref[...] = ya.astype(BF16)
        ybo_ref[...] = yb.astype(BF16)
        merged = gs_ref[:, pl.ds(0, D)].astype(F32) * ya + gs_ref[:, pl.ds(D, D)].astype(F32) * yb
        z = _nn(merged.astype(BF16), wo_s[...])
        z_ref[...] = z
        zn, _, _ = _rms_fwd(z, g_ref[...])
        hout_ref[...] = h_ref[...] + zn

    return _call(
        body, name="mix_out_fwd", grid=(nt,), ins=[ya_in, o, gs, h, g, wg[groups[0]]],
        in_specs=[_row_spec(tm, D), _row_spec(tm, D), _row_spec(tm, 2 * D), _row_spec(tm, D), _const_spec((1, D)), ANY],
        out_specs=[_row_spec(tm, D)] * 4,
        out_shape=[jax.ShapeDtypeStruct((t_tok, D), F32), jax.ShapeDtypeStruct((t_tok, D), BF16),
                   jax.ShapeDtypeStruct((t_tok, D), BF16), jax.ShapeDtypeStruct((t_tok, D), F32)],
        scratch=[pltpu.VMEM((D, D), BF16)] * 3 + [DMA((3 * NDEV,))],
        vmem=48 * 2 ** 20, carries=carries)


def _mix_out_bwd(dh, z, ya, yb, gs, g, wg, tm=512, carries=()):
    t_tok = dh.shape[0]
    nt = t_tok // tm
    groups = _groups_of(MIXOUT)

    def body(dh_ref, z_ref, ya_ref, yb_ref, gs_ref, g_ref, wg_ref,
             dyain_ref, do_ref, dgpre_ref, dya_ref, dyb_ref, mg_ref, dz_ref, dg_ref, dbg_ref,
             wa_s, wb_s, wo_s, sems):
        i = pl.program_id(0)

        @pl.when(i == 0)
        def _():
            _load_weights({groups[0]: wg_ref}, list(zip(MIXOUT, (wa_s, wb_s, wo_s))), sems)
            dg_ref[...] = jnp.zeros_like(dg_ref)
            dbg_ref[...] = jnp.zeros_like(dbg_ref)

        gv = g_ref[...]
        _, zh, r = _rms_fwd(z_ref[...], gv)
        dz, dg = _rms_bwd(dh_ref[...], zh, r, gv)
        dg_ref[...] += dg
        dzb = dz.astype(BF16)
        dz_ref[...] = dzb
        ga, gb = gs_ref[:, pl.ds(0, D)].astype(F32), gs_ref[:, pl.ds(D, D)].astype(F32)
        ya_v, yb_v = ya_ref[...].astype(F32), yb_ref[...].astype(F32)
        mg_ref[...] = (ga * ya_v + gb * yb_v).astype(BF16)
        dm = _nt(dzb, wo_s[...])
        dga = dm * ya_v * ga * (1.0 - ga)
        dgb = dm * yb_v * gb * (1.0 - gb)
        dgpre_ref[:, pl.ds(0, D)] = dga.astype(BF16)
        dgpre_ref[:, pl.ds(D, D)] = dgb.astype(BF16)
        dbg_ref[:, pl.ds(0, D)] += jnp.sum(dga, axis=0, keepdims=True)
        dbg_ref[:, pl.ds(D, D)] += jnp.sum(dgb, axis=0, keepdims=True)
        dya = (dm * ga).astype(BF16)
        dyb = (dm * gb).astype(BF16)
        dya_ref[...] = dya
        dyb_ref[...] = dyb
        dyain_ref[...] = _nt(dya, wa_s[...])
        do_ref[...] = _nt(dyb, wb_s[...]).astype(BF16)

    bf = jax.ShapeDtypeStruct((t_tok, D), BF16)
    return _call(
        body, name="mix_out_bwd", grid=(nt,), ins=[dh, z, ya, yb, gs, g, wg[groups[0]]],
        in_specs=[_row_spec(tm, D)] * 4 + [_row_spec(tm, 2 * D), _const_spec((1, D)), ANY],
        out_specs=[_row_spec(tm, D), _row_spec(tm, D), _row_spec(tm, 2 * D)] + [_row_spec(tm, D)] * 4
        + [_const_spec((1, D)), _const_spec((1, 2 * D))],
        out_shape=[jax.ShapeDtypeStruct((t_tok, D), F32), bf, jax.ShapeDtypeStruct((t_tok, 2 * D), BF16), bf, bf, bf, bf,
                   jax.ShapeDtypeStruct((1, D), F32), jax.ShapeDtypeStruct((1, 2 * D), F32)],
        scratch=[pltpu.VMEM((D, D), BF16)] * 3 + [DMA((3 * NDEV,))],
        vmem=VMEM_BIG, carries=carries)


def _sum_parts(parts_list):
    n = len(parts_list)
    _, r, c = parts_list[0].shape
    tc = 256

    def body(*refs):
        for p_ref, o_ref in zip(refs[:n], refs[n:]):
            acc = p_ref[0].astype(F32)
            for s in range(1, NDEV):
                acc = acc + p_ref[s].astype(F32)
            o_ref[...] = acc

    return pl.pallas_call(
        body, name=f"sum_parts_{r}", grid=(c // tc,),
        in_specs=[pl.BlockSpec((NDEV, r, tc), lambda i: (0, 0, i))] * n,
        out_specs=[pl.BlockSpec((r, tc), lambda i: (0, i))] * n,
        out_shape=[jax.ShapeDtypeStruct((r, c), F32)] * n,
        compiler_params=pltpu.CompilerParams(dimension_semantics=("arbitrary",), vmem_limit_bytes=48 * 2 ** 20),
    )(*parts_list)


def _adamw_math(w, g, m, v):
    m = ADAM_B1 * m + (1.0 - ADAM_B1) * g
    v = ADAM_B2 * v + (1.0 - ADAM_B2) * (g * g)
    m_hat = m / (1.0 - ADAM_B1 ** ADAM_STEP)
    v_hat = v / (1.0 - ADAM_B2 ** ADAM_STEP)
    delta = -ADAM_LR * (m_hat / (jnp.sqrt(v_hat) + ADAM_EPS) + ADAM_WD * w)
    return delta, m, v


def _sum_ready(parts_list, name, carries=(), chip_sums=()):
    n = len(parts_list)
    c = parts_list[0].shape[2]
    tc = 2 * LANES

    def body(*refs):
        for k, (p_ref, o_ref) in enumerate(zip(refs[:n], refs[n:])):
            if k in chip_sums:
                core = lax.axis_index("c")
                slots = [2 * ch + core for ch in range(NDEV // 2)]
            else:
                slots = list(range(NDEV))
            g = p_ref[slots[0]].astype(F32)
            for s in slots[1:]:
                g = g + p_ref[s].astype(F32)
            o_ref[...] = g

    return _call(
        body, name=name, grid=(c // tc,), ins=list(parts_list),
        in_specs=[pl.BlockSpec((NDEV, p.shape[1], tc), lambda i: (0, 0, i)) for p in parts_list],
        out_shape=[jax.ShapeDtypeStruct((p.shape[1], c), F32) for p in parts_list],
        out_specs=[pl.BlockSpec((p.shape[1], tc), lambda i: (0, i)) for p in parts_list],
        vmem=48 * 2 ** 20, carries=carries)


def _adamw_cols(items, name):
    n = len(items)
    c = items[0][1].shape[1]
    tc = LANES

    def body(*refs):
        for k in range(n):
            g_ref, w_ref, m_ref, v_ref = refs[4 * k:4 * k + 4]
            go_ref, d_ref, nm_ref, nv_ref = refs[4 * n + 4 * k:4 * n + 4 * k + 4]
            g = g_ref[...]
            d, m, v = _adamw_math(w_ref[...], g, m_ref[...], v_ref[...])
            go_ref[...] = g
            d_ref[...] = d
            nm_ref[...] = m
            nv_ref[...] = v

    ins, specs, out_shape = [], [], []
    for g, w, m, v in items:
        r = w.shape[0]
        ins += [g, w, m, v]
        specs += [pl.BlockSpec((r, tc), lambda i: (0, i))] * 4
        out_shape += [jax.ShapeDtypeStruct((r, c), F32)] * 4
    outs, _ = _call(body, name=name, grid=(c // tc,), ins=ins, in_specs=specs, out_shape=out_shape,
                    out_specs=specs, vmem=48 * 2 ** 20)
    return [tuple(outs[4 * k:4 * k + 4]) for k in range(n)]


def _adamw_body(n):
    def body(*refs):
        for k in range(n):
            g_ref, w_ref, m_ref, v_ref = refs[4 * k:4 * k + 4]
            d_ref, nm_ref, nv_ref = refs[4 * n + 3 * k:4 * n + 3 * k + 3]
            d, m, v = _adamw_math(w_ref[...], g_ref[...], m_ref[...], v_ref[...])
            d_ref[...] = d
            nm_ref[...] = m
            nv_ref[...] = v
    return body


def _adamw(items):
    n = len(items)
    r, c = items[0][0].shape
    tr = r if r * c <= 2 ** 18 else max(t for t in range(8, 65, 8) if r % t == 0)
    spec = pl.BlockSpec((tr, c), lambda i: (i, 0))
    outs = pl.pallas_call(
        _adamw_body(n), name=f"adamw_{r}x{c}", grid=(r // tr,),
        in_specs=[spec] * (4 * n), out_specs=[spec] * (3 * n),
        out_shape=[jax.ShapeDtypeStruct((r, c), F32)] * (3 * n),
        compiler_params=pltpu.CompilerParams(dimension_semantics=("arbitrary",), vmem_limit_bytes=40 * 2 ** 20),
    )(*[a for it in items for a in it])
    return [tuple(outs[3 * k:3 * k + 3]) for k in range(n)]


def _adamw_small(items):
    n = len(items)
    vm = pl.BlockSpec(memory_space=pltpu.VMEM)
    outs = pl.pallas_call(
        _adamw_body(n), name="adamw_small", in_specs=[vm] * (4 * n), out_specs=[vm] * (3 * n),
        out_shape=[jax.ShapeDtypeStruct(it[1].shape, F32) for it in items for _ in range(3)],
    )(*[a for it in items for a in it])
    return [tuple(outs[3 * k:3 * k + 3]) for k in range(n)]


def _pack_small(arrs):
    rows, offs = [], []
    total = 0
    for a in arrs:
        flat = a.reshape(-1).astype(F32)
        nr = -(-flat.shape[0] // LANES)
        flat = jnp.pad(flat, (0, nr * LANES - flat.shape[0]))
        rows.append(flat.reshape(nr, LANES))
        offs.append((total, nr))
        total += nr
    pad = -total % 8
    if pad:
        rows.append(jnp.zeros((pad, LANES), F32))
    return jnp.concatenate(rows, axis=0), offs


def _unpack_small(pack, offs, shapes):
    out = []
    for (o, nr), shp in zip(offs, shapes):
        size = int(np.prod(shp))
        out.append(pack[o:o + nr].reshape(-1)[:size].reshape(shp))
    return out


def _sum_small(gathered, rows):
    def body(p_ref, o_ref):
        acc = p_ref[pl.ds(0, rows), :]
        for s in range(1, NDEV):
            acc = acc + p_ref[pl.ds(s * rows, rows), :]
        o_ref[...] = acc

    return pl.pallas_call(
        body, name="sum_small", out_shape=jax.ShapeDtypeStruct((rows, LANES), F32),
        in_specs=[pl.BlockSpec(memory_space=pltpu.VMEM)], out_specs=pl.BlockSpec(memory_space=pltpu.VMEM))(gathered)


def _block_diag(w):
    w = w.reshape(D // LANES, 2, LRU_BLOCK, LRU_BLOCK)
    z = jnp.zeros((D // LANES, LRU_BLOCK, LRU_BLOCK), w.dtype)
    top = jnp.concatenate([w[:, 0], z], axis=2)
    bot = jnp.concatenate([z, w[:, 1]], axis=2)
    return jnp.concatenate([top, bot], axis=1)


def _block_diag_grad(dw):
    a = dw[:, :LRU_BLOCK, :LRU_BLOCK]
    b = dw[:, LRU_BLOCK:, LRU_BLOCK:]
    return jnp.stack([a, b], axis=1).reshape(D // LRU_BLOCK, LRU_BLOCK, LRU_BLOCK)


def kernel(x, ffn1_pre_g, ffn1_w1, ffn1_w3, ffn1_w2, ffn1_post_g, mix_pre_g, w_in, conv_w, conv_b, rg_a_w, rg_a_b, rg_x_w, rg_x_b, lru_lambda, w_lru_out, attn_sinks, rel_bias, w_attn_out, w_gate, b_gate, w_o, mix_post_g, ffn2_pre_g, ffn2_w1, ffn2_w3, ffn2_w2, ffn2_post_g, loss_target, m_ffn1_pre_g, m_ffn1_w1, m_ffn1_w3, m_ffn1_w2, m_ffn1_post_g, m_mix_pre_g, m_w_in, m_conv_w, m_conv_b, m_rg_a_w, m_rg_a_b, m_rg_x_w, m_rg_x_b, m_lru_lambda, m_w_lru_out, m_attn_sinks, m_rel_bias, m_w_attn_out, m_w_gate, m_b_gate, m_w_o, m_mix_post_g, m_ffn2_pre_g, m_ffn2_w1, m_ffn2_w3, m_ffn2_w2, m_ffn2_post_g, v_ffn1_pre_g, v_ffn1_w1, v_ffn1_w3, v_ffn1_w2, v_ffn1_post_g, v_mix_pre_g, v_w_in, v_conv_w, v_conv_b, v_rg_a_w, v_rg_a_b, v_rg_x_w, v_rg_x_b, v_lru_lambda, v_w_lru_out, v_attn_sinks, v_rel_bias, v_w_attn_out, v_w_gate, v_b_gate, v_w_o, v_mix_post_g, v_ffn2_pre_g, v_ffn2_w1, v_ffn2_w3, v_ffn2_w2, v_ffn2_post_g):
    names = ["ffn1_pre_g", "ffn1_w1", "ffn1_w3", "ffn1_w2", "ffn1_post_g", "mix_pre_g", "w_in", "conv_w", "conv_b",
             "rg_a_w", "rg_a_b", "rg_x_w", "rg_x_b", "lru_lambda", "w_lru_out", "attn_sinks", "rel_bias", "w_attn_out",
             "w_gate", "b_gate", "w_o", "mix_post_g", "ffn2_pre_g", "ffn2_w1", "ffn2_w3", "ffn2_w2", "ffn2_post_g"]
    ws = dict(zip(names, (ffn1_pre_g, ffn1_w1, ffn1_w3, ffn1_w2, ffn1_post_g, mix_pre_g, w_in, conv_w, conv_b, rg_a_w,
                          rg_a_b, rg_x_w, rg_x_b, lru_lambda, w_lru_out, attn_sinks, rel_bias, w_attn_out, w_gate,
                          b_gate, w_o, mix_post_g, ffn2_pre_g, ffn2_w1, ffn2_w3, ffn2_w2, ffn2_post_g)))
    ms = dict(zip(names, (m_ffn1_pre_g, m_ffn1_w1, m_ffn1_w3, m_ffn1_w2, m_ffn1_post_g, m_mix_pre_g, m_w_in, m_conv_w,
                          m_conv_b, m_rg_a_w, m_rg_a_b, m_rg_x_w, m_rg_x_b, m_lru_lambda, m_w_lru_out, m_attn_sinks,
                          m_rel_bias, m_w_attn_out, m_w_gate, m_b_gate, m_w_o, m_mix_post_g, m_ffn2_pre_g, m_ffn2_w1,
                          m_ffn2_w3, m_ffn2_w2, m_ffn2_post_g)))
    vs = dict(zip(names, (v_ffn1_pre_g, v_ffn1_w1, v_ffn1_w3, v_ffn1_w2, v_ffn1_post_g, v_mix_pre_g, v_w_in, v_conv_w,
                          v_conv_b, v_rg_a_w, v_rg_a_b, v_rg_x_w, v_rg_x_b, v_lru_lambda, v_w_lru_out, v_attn_sinks,
                          v_rel_bias, v_w_attn_out, v_w_gate, v_b_gate, v_w_o, v_mix_post_g, v_ffn2_pre_g, v_ffn2_w1,
                          v_ffn2_w3, v_ffn2_w2, v_ffn2_post_g)))
    me = 4 * lax.axis_index("x") + 2 * lax.axis_index("y") + lax.axis_index("c")
    vec = lambda n: ws[n].reshape(1, -1)

    def shard2d(name):
        if name == "conv":
            row = lax.bitcast_convert_type(conv_w.reshape(CONV_WIDTH, LANES), BF16).reshape(1, D)
            return jnp.concatenate([row, jnp.zeros((LOC["conv"][2] - 1, D), BF16)], axis=0)
        a = ws[name].reshape(ws[name].shape[-2], ws[name].shape[-1])
        return (a.T if name in COL_SHARDED else a).astype(BF16)

    packs = {g: jnp.concatenate([shard2d(n) for n, _ in members], axis=0) for g, members in GROUPS}

    wg = {}
    _, ((wg["ffn1"],),) = _call(None, name="allgather_ffn1", grid=(), ins=[], in_specs=[], out_shape=[], out_specs=[],
                                carries=[_AllGather(packs["ffn1"])])
    conv_rows = wg["ffn1"].reshape(NDEV, GROUP_ROWS["ffn1"], D)[:, LOC["conv"][1]]
    cw = jnp.transpose(lax.bitcast_convert_type(conv_rows.reshape(NDEV, CONV_WIDTH, LANES, 2), F32),
                       (1, 0, 2)).reshape(CONV_WIDTH, D)
    bmap = jnp.asarray(_bucket_map())
    bias = _bias_tile(rel_bias, bmap).reshape(N_KV, REP * QT, KW)
    sinks = attn_sinks.reshape(N_HEADS)
    wa_bd = _block_diag(rg_a_w.reshape(D // LRU_BLOCK, LRU_BLOCK, LRU_BLOCK)).astype(BF16)
    wx_bd = _block_diag(rg_x_w.reshape(D // LRU_BLOCK, LRU_BLOCK, LRU_BLOCK)).astype(BF16)
    lru_args = (cw, vec("conv_b"), wa_bd, wx_bd, vec("rg_a_b"), vec("rg_x_b"), vec("lru_lambda"))
    x2, tgt = x[0], loss_target[0]

    (h1, a1, b1, f1, nb1), ((wg["mixin"],),) = _ffn_fwd(
        x2, vec("ffn1_pre_g"), vec("ffn1_post_g"), wg, ("ffn1_w1", "ffn1_w3", "ffn1_w2"),
        carries=[_AllGather(packs["mixin"])])
    (q, kd, vd, xr, xg, gs, ub), ((wg["mixout"],),) = _mix_proj_fwd(
        h1, vec("mix_pre_g"), vec("b_gate"), wg, carries=[_AllGather(packs["mixout"])])
    (ya_in, hseq), ((wg["ffn2a"],),) = _lru_fwd(xr, xg, *lru_args, carries=[_AllGather(packs["ffn2a"])])
    (o, lse), ((wg["ffn2b"],),) = _attn_fwd(q, kd, vd, bias, sinks, carries=[_AllGather(packs["ffn2b"])])
    (h2, ya, yb, z), _ = _mix_out_fwd(ya_in, o, gs, h1, vec("mix_post_g"), wg)
    (_, a2, b2, f2, nb2, dy, loss_part), _ = _ffn_fwd(h2, vec("ffn2_pre_g"), vec("ffn2_post_g"), wg,
                                                     ("ffn2_w1", "ffn2_w3", "ffn2_w2"), target=tgt)

    gsm, parts = {}, {}
    rs = lambda *grads_: [_ReduceScatterSend(list(grads_))]
    ffr = FF // NDEV
    (dab, s_act, dfb, gsm["ffn2_post_g"]), _ = _ffn_bwd_a(dy, f2, a2, b2, vec("ffn2_post_g"), wg, "ffn2_w2")
    g_w2, _ = _dw(s_act, dfb, FF // 2, "dw_ffn2_w2")
    g_w13, ((parts["ffn2_w2"],),) = _dw(dab, nb2, D // 2, "dw_ffn2_w13", carries=rs(g_w2))
    (dh2, gsm["ffn2_pre_g"]), ((parts["ffn2_w1"],),) = _ffn_bwd_dx(
        dab, h2, dy, vec("ffn2_pre_g"), wg, "ffn2_w1", "ffn2_w3", carries=rs((g_w13, 0, ffr, ffr)))
    (dya_in, do, dgpre, dya, dyb, mg, dzb, gsm["mix_post_g"], gsm["b_gate"]), ((parts["ffn2_w3"],),) = _mix_out_bwd(
        dh2, z, ya, yb, gs, vec("mix_post_g"), wg, carries=rs((g_w13, FF, ffr, ffr)))
    g_wa, _ = _dw(ya_in, dya, D // 2, "dw_w_lru_out")
    g_wb, _ = _dw(o, dyb, D // 2, "dw_w_attn_out")
    g_wo, _ = _dw(mg, dzb, D // 2, "dw_w_o")
    g_wgate, _ = _dw(dgpre, ub, D // 2, "dw_w_gate")
    (dqkv, ds_acc, dsink), (mixout_parts,) = _attn_bwd(
        q, kd, vd, o, do, lse, bias, sinks, carries=rs(g_wa, g_wb, g_wo, g_wgate))
    parts["w_lru_out"], parts["w_attn_out"], parts["w_o"], parts["w_gate"] = mixout_parts
    (dxr, dxg, dvec, dwa, dwx), _ = _lru_bwd(dya_in, xr, xg, hseq, *lru_args)
    g_qkv, _ = _dw(dqkv, ub, D // 2, "dw_w_in_qkv")
    g_xr, _ = _dw(dxr, ub, D // 2, "dw_w_in_xr")
    g_xg, _ = _dw(dxg, ub, D // 2, "dw_w_in_xg")
    g_win = jnp.concatenate(
        [g_qkv[:D]] + [g_qkv[c0 + gi * LANES:c0 + gi * LANES + HEAD_DIM] for c0 in (K0, V0) for gi in range(N_KV)]
        + [g_xr, g_xg], axis=0)
    wir, wir_a = IN_W // NDEV, 336
    (dh1, gsm["mix_pre_g"]), ((win_a,),) = _mix_proj_bwd(
        dqkv, dxr, dxg, dgpre, h1, dh2, vec("mix_pre_g"), wg, carries=rs((g_win, 0, wir, wir_a)))
    gsm["conv_w"] = dvec[0:CONV_WIDTH]
    gsm["conv_b"], gsm["rg_a_b"], gsm["rg_x_b"], gsm["lru_lambda"] = dvec[4], dvec[5], dvec[6], dvec[7]
    gsm["rg_a_w"] = _block_diag_grad(dwa)
    gsm["rg_x_w"] = _block_diag_grad(dwx)
    gsm["attn_sinks"] = dsink[0, :N_HEADS]
    gsm["rel_bias"] = _bias_grad(ds_acc.reshape(N_HEADS, QT, KW), bmap)[:, :N_HEADS]
    late = ("ffn1_post_g", "ffn1_pre_g")
    early = tuple(n for n in SMALL if n not in late)
    early_pack, early_offs = _pack_small([gsm[n] for n in early])
    (dab, s_act, dfb, gsm["ffn1_post_g"]), ((win_b,), (early_all,)) = _ffn_bwd_a(
        dh1, f1, a1, b1, vec("ffn1_post_g"), wg, "ffn1_w2",
        carries=rs((g_win, wir_a, wir, wir - wir_a)) + [_AllGather(early_pack)])
    parts["w_in"] = jnp.concatenate([win_a, win_b], axis=1)
    g_w2, _ = _dw(s_act, dfb, FF // 2, "dw_ffn1_w2")
    g_w13, ((parts["ffn1_w2"],),) = _dw(dab, nb1, D // 2, "dw_ffn1_w13", carries=rs(g_w2))
    chip_w13 = _pair_sum(g_w13, [(0, ffr), (FF, ffr)])
    (grad_x, gsm["ffn1_pre_g"]), ((parts["ffn1_w1"], parts["ffn1_w3"]),) = _ffn_bwd_dx(
        dab, x2, dh1, vec("ffn1_pre_g"), wg, "ffn1_w1", "ffn1_w3",
        carries=[_ReduceScatterSend([(chip_w13, 0, ffr, ffr), (chip_w13, FF, ffr, ffr)], chip_sums=True)])
    late_pack, late_offs = _pack_small([gsm[n] for n in late] + [loss_part])

    grads, delta, new_m, new_v = {}, {}, {}, {}
    t_form = [n for n in COL_SHARDED if ws[n].shape[-1] % LANES]
    view = {n: (lambda a: a.reshape(a.shape[-2], a.shape[-1]).T) if n in t_form
            else (lambda a: a.reshape(a.shape[-2], a.shape[-1])) for n in BIG}
    unview = {n: (lambda a: a.T) if n in t_form else (lambda a: a) for n in BIG}

    group = [n for n in BIG if n != "w_gate"]
    sums, ((late_all,),) = _sum_ready([parts[n] for n in group], "sum_ready", [_AllGather(late_pack)],
                                      chip_sums=(group.index("ffn1_w1"), group.index("ffn1_w3")))
    res = _adamw_cols([(g, view[n](ws[n]), view[n](ms[n]), view[n](vs[n])) for n, g in zip(group, sums)], "adamw_big")
    for n, quad in zip(group, res):
        grads[n], delta[n], new_m[n], new_v[n] = (unview[n](a) for a in quad)
    (g_gate_t,) = _sum_parts([parts["w_gate"]])
    grads["w_gate"] = g_gate_t.T
    ((delta["w_gate"], new_m["w_gate"], new_v["w_gate"]),) = _adamw(
        [(grads["w_gate"], view["w_gate"](ws["w_gate"]), view["w_gate"](ms["w_gate"]), view["w_gate"](vs["w_gate"]))])
    for group, gathered, offs in ((early, early_all, early_offs), (late, late_all, late_offs)):
        total = _sum_small(gathered, gathered.shape[0] // NDEV)
        shapes = [(CONV_WIDTH, D) if n == "conv_w" else ws[n].shape for n in group]
        if group is late:
            shapes = shapes + [(1, LANES)]
        unpacked = _unpack_small(total, offs, shapes)
        if group is late:
            loss = unpacked.pop()[0, 0]
        for n, g in zip(group, unpacked):
            grads[n] = g
    grads["conv_w"] = lax.dynamic_slice(grads["conv_w"], (0, me * LANES), (CONV_WIDTH, LANES)).reshape(conv_w.shape)

    flat2d = lambda a: a.reshape(-1, a.shape[-1])
    res = _adamw_small([(flat2d(grads[n].reshape(ws[n].shape)), flat2d(ws[n]), flat2d(ms[n]), flat2d(vs[n]))
                        for n in SMALL])
    for n, (d_, m_, v_) in zip(SMALL, res):
        delta[n], new_m[n], new_v[n] = d_, m_, v_

    outs = [loss, grad_x.reshape(x.shape)]
    for src in (grads, delta, new_m, new_v):
        outs += [src[n].reshape(ws[n].shape) for n in names]
    return tuple(outs)
```

```python
import functools
import math
import operator

import numpy as np
import jax
import jax.numpy as jnp
from jax import lax
from jax.experimental import pallas as pl
from jax.experimental.pallas import tpu as pltpu

F32, BF16 = jnp.float32, jnp.bfloat16

NDEV = 8
D = 1024
FF = 2816
N_HEADS, N_KV, HEAD_DIM = 16, 4, 64
CHUNK, WINDOW = 64, 128
N_BUCKETS, MAX_DISTANCE = 32, 128
LRU_BLOCK = 64
CONV_WIDTH = 4
LRU_C = 8.0
RMS_EPS = 1e-6
NEG_INF = -1e30
LANES = 128
QT = 128
KW = QT + WINDOW
LRU_ROWS = 512
IN_W = D + 2 * N_KV * HEAD_DIM + 2 * D
INP_W = D + 2 * N_KV * LANES + 2 * D
VMEM_BIG = 58 * 2 ** 20

ADAM_LR, ADAM_B1, ADAM_B2, ADAM_EPS, ADAM_WD, ADAM_STEP = 0.001, 0.9, 0.999, 1e-08, 0.01, 10

GROUPS = (("ffn1", (("ffn1_w1", FF // NDEV), ("ffn1_w3", FF // NDEV), ("ffn1_w2", FF // NDEV), ("conv", 16))),
          ("mixin", (("w_in", IN_W // NDEV), ("w_gate", 2 * D // NDEV))),
          ("mixout", (("w_lru_out", D // NDEV), ("w_attn_out", D // NDEV), ("w_o", D // NDEV))),
          ("ffn2a", (("ffn2_w1", FF // NDEV), ("ffn2_w3", FF // NDEV))),
          ("ffn2b", (("ffn2_w2", FF // NDEV),)))
LOC, GROUP_ROWS = {}, {}
for _g, _members in GROUPS:
    _o = 0
    for _n, _r in _members:
        LOC[_n] = (_g, _o, _r)
        _o += _r
    GROUP_ROWS[_g] = _o
BIG = tuple(n for _, members in GROUPS for n, _ in members if n != "conv")
COL_SHARDED = ("ffn1_w1", "ffn1_w3", "w_in", "w_gate", "ffn2_w1", "ffn2_w3")

SMALL = ("ffn1_pre_g", "ffn1_post_g", "mix_pre_g", "conv_w", "conv_b", "rg_a_w", "rg_a_b", "rg_x_w", "rg_x_b",
         "lru_lambda", "attn_sinks", "rel_bias", "b_gate", "mix_post_g", "ffn2_pre_g", "ffn2_post_g")

MESH = pl.DeviceIdType.MESH
ANY = pl.BlockSpec(memory_space=pl.ANY)
DMA = pltpu.SemaphoreType.DMA


def _nn(a, b):
    return lax.dot_general(a, b, (((1,), (0,)), ((), ())), preferred_element_type=F32)


def _nt(a, b):
    return lax.dot_general(a, b, (((1,), (1,)), ((), ())), preferred_element_type=F32)


def _tn(a, b):
    return lax.dot_general(a, b, (((0,), (0,)), ((), ())), preferred_element_type=F32)


def _rms_fwd(x, g):
    r = lax.rsqrt(jnp.mean(x * x, axis=-1, keepdims=True) + RMS_EPS)
    xh = x * r
    return xh * g, xh, r


def _rms_bwd(dn, xh, r, g):
    dxh = dn * g
    dx = r * (dxh - xh * jnp.mean(dxh * xh, axis=-1, keepdims=True))
    return dx, jnp.sum(dn * xh, axis=0, keepdims=True)


def _row_spec(tm, c):
    return pl.BlockSpec((tm, c), lambda i: (i, 0))


def _const_spec(shape):
    nd = len(shape)
    return pl.BlockSpec(shape, lambda i: (0,) * nd)


class _AllGather:
    def __init__(self, shard):
        self.m, n = shard.shape
        self.ins = [shard]
        self.out_shape = [jax.ShapeDtypeStruct((NDEV * self.m, n), shard.dtype)]
        self.scratch = [DMA((7,)), DMA((7,)), DMA]

    def _copies(self, ins, outs, scr, all_of_them):
        x_ref, out_ref = ins[0], outs[0]
        send_sems, recv_sems, local_sem = scr
        x, y, c = lax.axis_index("x"), lax.axis_index("y"), lax.axis_index("c")
        me, sibling = (x, y, c), (x, y, 1 - c)
        chips = [(1 - x, y), (x, 1 - y), (1 - x, 1 - y)]
        m = self.m

        def rows(px, py, pc):
            return out_ref.at[pl.ds((4 * px + 2 * py + pc) * m, m), :]

        def copy(k, block, to, src=None):
            return pltpu.make_async_remote_copy(
                src_ref=rows(*block) if src is None else src, dst_ref=rows(*block),
                send_sem=send_sems.at[k], recv_sem=recv_sems.at[k], device_id=to, device_id_type=MESH)

        mine = pltpu.make_async_copy(x_ref, rows(*me), local_sem)
        first = [copy(0, me, sibling, src=x_ref)] + [copy(1 + j, me, (*chip, c), src=x_ref)
                                                     for j, chip in enumerate(chips)]
        if not all_of_them:
            return mine, first
        passed = [copy(4 + j, (*chip, c), sibling) for j, chip in enumerate(chips)]
        landed = [copy(1 + j, (*chip, c), me) for j, chip in enumerate(chips)]
        from_sibling = [copy(0, sibling, me)] + [copy(4 + j, (*chip, 1 - c), me) for j, chip in enumerate(chips)]
        return mine, first, passed, landed, from_sibling

    def start(self, ins, outs, scr):
        mine, first = self._copies(ins, outs, scr, False)
        mine.start()
        for cp in first:
            cp.start()

    def finish(self, ins, outs, scr):
        mine, first, passed, landed, from_sibling = self._copies(ins, outs, scr, True)
        for cp_in, cp_on in zip(landed, passed):
            cp_in.wait_recv()
            cp_on.start()
        for cp in from_sibling:
            cp.wait_recv()
        for cp in first + passed:
            cp.wait_send()
        mine.wait()


class _AllGatherTree:
    def __init__(self, shard):
        self.m, n = shard.shape
        self.half = self.m // 32 * 16
        self.ins = [shard]
        self.out_shape = [jax.ShapeDtypeStruct((NDEV * self.m, n), shard.dtype)]
        self.scratch = [DMA((9,)), DMA((9,)), DMA]

    def start(self, ins, outs, scr):
        x_ref, out_ref = ins[0], outs[0]
        send_sems, recv_sems, local_sem = scr
        x, y, c = lax.axis_index("x"), lax.axis_index("y"), lax.axis_index("c")
        m, half = self.m, self.half
        me, sibling, xn, yn, diag = (x, y, c), (x, y, 1 - c), (1 - x, y, c), (x, 1 - y, c), (1 - x, 1 - y, c)
        other = lambda dev: (dev[0], dev[1], 1 - c)

        def rows(dev, lo=0, n=m):
            return out_ref.at[pl.ds((4 * dev[0] + 2 * dev[1] + dev[2]) * m + lo, n), :]

        def copy(k, dev, to, lo=0, n=m, src=None):
            return pltpu.make_async_remote_copy(
                src_ref=rows(dev, lo, n) if src is None else src, dst_ref=rows(dev, lo, n),
                send_sem=send_sems.at[k], recv_sem=recv_sems.at[k], device_id=to, device_id_type=MESH)

        mine = pltpu.make_async_copy(x_ref, rows(me), local_sem)
        own = [copy(0, me, sibling, src=x_ref), copy(1, me, xn, src=x_ref), copy(2, me, yn, src=x_ref)]
        mine.start()
        for cp in own:
            cp.start()
        steps = [(copy(1, xn, me), [copy(3, xn, yn, 0, half), copy(5, xn, sibling)]),
                 (copy(2, yn, me), [copy(4, yn, xn, half, m - half), copy(6, yn, sibling)]),
                 (copy(3, diag, me, 0, half), [copy(7, diag, sibling, 0, half)]),
                 (copy(4, diag, me, half, m - half), [copy(8, diag, sibling, half, m - half)])]
        from_sibling = [copy(0, sibling, me), copy(5, other(xn), me), copy(6, other(yn), me),
                        copy(7, other(diag), me, 0, half), copy(8, other(diag), me, half, m - half)]
        passed = []
        for landing, onward in steps:
            landing.wait_recv()
            for cp in onward:
                cp.start()
            passed += onward
        for cp in from_sibling:
            cp.wait_recv()
        for cp in own + passed:
            cp.wait_send()
        mine.wait()

    def finish(self, ins, outs, scr):
        pass


class _ReduceScatterSend:
    def __init__(self, grads, chip_sums=False):
        grads = [g if isinstance(g, tuple) else (g, 0, g.shape[0] // NDEV, g.shape[0] // NDEV) for g in grads]
        self.peers = (2, 4, 6) if chip_sums else tuple(range(1, NDEV))
        self.nw = len(grads)
        self.base = [b for _, b, _, _ in grads]
        self.stride = [s for _, _, s, _ in grads]
        self.rows = [r for _, _, _, r in grads]
        self.ins = [g for g, _, _, _ in grads]
        self.out_shape = [jax.ShapeDtypeStruct((NDEV, r, g.shape[1]), g.dtype) for g, _, _, r in grads]
        self.scratch = [DMA((self.nw, NDEV - 1)), DMA((self.nw, NDEV - 1)), DMA((self.nw,))]

    def _copies(self, g_refs, r_refs, scr, want):
        send_sems, recv_sems, local_sems = scr
        x, y, c = lax.axis_index("x"), lax.axis_index("y"), lax.axis_index("c")
        me = 4 * x + 2 * y + c
        rows, base, stride = self.rows, self.base, self.stride
        out = []
        if want == "local":
            for w in range(self.nw):
                out.append(pltpu.make_async_copy(g_refs[w].at[pl.ds(base[w] + me * stride[w], rows[w])],
                                                 r_refs[w].at[me], local_sems.at[w]))
            return out
        for k in self.peers:
            px, py, pc = x ^ (k >> 2), y ^ ((k >> 1) & 1), c ^ (k & 1)
            peer = 4 * px + 2 * py + pc
            for w in range(self.nw):
                sems = dict(send_sem=send_sems.at[w, k - 1], recv_sem=recv_sems.at[w, k - 1],
                            device_id=(px, py, pc), device_id_type=MESH)
                if want == "send":
                    out.append(pltpu.make_async_remote_copy(
                        src_ref=g_refs[w].at[pl.ds(base[w] + peer * stride[w], rows[w])], dst_ref=r_refs[w].at[me],
                        **sems))
                else:
                    out.append(pltpu.make_async_remote_copy(
                        src_ref=g_refs[w].at[pl.ds(0, rows[w])], dst_ref=r_refs[w].at[peer], **sems))
        return out

    def start(self, ins, outs, scr):
        for cp in self._copies(ins, outs, scr, "local") + self._copies(ins, outs, scr, "send"):
            cp.start()

    def finish(self, ins, outs, scr):
        for cp in self._copies(ins, outs, scr, "recv"):
            cp.wait_recv()
        for cp in self._copies(ins, outs, scr, "send"):
            cp.wait_send()
        for cp in self._copies(ins, outs, scr, "local"):
            cp.wait()


def _call(body, *, name, grid, ins, in_specs, out_shape, out_specs, scratch=(), vmem=None, carries=()):
    n_in, n_out, n_scr = len(ins), len(out_shape), len(scratch)
    ng = len(grid)

    def split(refs):
        pos = [0]

        def take(k):
            part = refs[pos[0]:pos[0] + k]
            pos[0] += k
            return part

        i_refs = take(n_in)
        c_in = [take(len(c.ins)) for c in carries]
        o_refs = take(n_out)
        c_out = [take(len(c.out_shape)) for c in carries]
        s_refs = take(n_scr)
        c_scr = [take(len(c.scratch)) for c in carries]
        return i_refs, o_refs, s_refs, list(zip(carries, c_in, c_out, c_scr))

    def full(*refs):
        i_refs, o_refs, s_refs, cparts = split(refs)
        if ng == 0:
            for c, a, b, s in cparts:
                c.start(a, b, s)
            for c, a, b, s in cparts:
                c.finish(a, b, s)
            return
        ids = [pl.program_id(a) for a in range(ng)]
        if cparts:
            @pl.when(functools.reduce(operator.and_, [i == 0 for i in ids]))
            def _():
                for c, a, b, s in cparts:
                    c.start(a, b, s)

        body(*i_refs, *o_refs, *s_refs)
        if cparts:
            @pl.when(functools.reduce(operator.and_, [i == g - 1 for i, g in zip(ids, grid)]))
            def _():
                for c, a, b, s in cparts:
                    c.finish(a, b, s)

    all_ins = list(ins) + [a for c in carries for a in c.ins]
    all_in_specs = list(in_specs) + [ANY for c in carries for _ in c.ins]
    all_out_shape = list(out_shape) + [s for c in carries for s in c.out_shape]
    all_out_specs = list(out_specs) + [ANY for c in carries for _ in c.out_shape]
    all_scratch = list(scratch) + [s for c in carries for s in c.scratch]
    kwargs = dict(grid=grid) if ng else {}
    outs = pl.pallas_call(
        full, name=name, in_specs=all_in_specs, out_specs=all_out_specs, out_shape=all_out_shape,
        scratch_shapes=all_scratch,
        compiler_params=pltpu.CompilerParams(dimension_semantics=("arbitrary",) * ng if ng else None,
                                             vmem_limit_bytes=vmem),
        **kwargs)(*all_ins)
    outs = list(outs)
    res, pos = outs[:n_out], n_out
    carried = []
    for c in carries:
        carried.append(outs[pos:pos + len(c.out_shape)])
        pos += len(c.out_shape)
    return res, carried


def _pair_sum(g, items):
    rows = items[0][1]
    n = len(items) * (NDEV // 2)
    width = g.shape[1]

    def body(g_ref, out_ref, got_ref, mine_v, got_v, send_sems, recv_sems, ld_sems, st_sems):
        x, y, c = lax.axis_index("x"), lax.axis_index("y"), lax.axis_index("c")
        keep = [base + (2 * ch + c) * rows for base, _ in items for ch in range(NDEV // 2)]
        give = [base + (2 * ch + 1 - c) * rows for base, _ in items for ch in range(NDEV // 2)]

        def remote(j, r):
            return pltpu.make_async_remote_copy(
                src_ref=g_ref.at[pl.ds(r, rows)], dst_ref=got_ref.at[pl.ds(r, rows)], send_sem=send_sems.at[j],
                recv_sem=recv_sems.at[j], device_id=(x, y, 1 - c), device_id_type=MESH)

        sends = [remote(j, r) for j, r in enumerate(give)]
        for cp in sends:
            cp.start()
        loads = [pltpu.make_async_copy(g_ref.at[pl.ds(r, rows)], mine_v.at[j], ld_sems.at[0, j])
                 for j, r in enumerate(keep)]
        for cp in loads:
            cp.start()
        stores = []
        for j, r in enumerate(keep):
            remote(j, r).wait_recv()
            cp = pltpu.make_async_copy(got_ref.at[pl.ds(r, rows)], got_v.at[j], ld_sems.at[1, j])
            cp.start()
            loads[j].wait()
            cp.wait()
            mine_v[j] = (mine_v[j].astype(F32) + got_v[j].astype(F32)).astype(BF16)
            st = pltpu.make_async_copy(mine_v.at[j], out_ref.at[pl.ds(r, rows)], st_sems.at[j])
            st.start()
            stores.append(st)
        for cp in sends:
            cp.wait_send()
        for cp in stores:
            cp.wait()

    out, _ = pl.pallas_call(
        body, name="pair_sum", in_specs=[ANY], out_specs=[ANY, ANY],
        out_shape=[jax.ShapeDtypeStruct(g.shape, g.dtype), jax.ShapeDtypeStruct(g.shape, g.dtype)],
        scratch_shapes=[pltpu.VMEM((n, rows, width), g.dtype), pltpu.VMEM((n, rows, width), g.dtype),
                        DMA((n,)), DMA((n,)), DMA((2, n)), DMA((n,))],
        compiler_params=pltpu.CompilerParams(vmem_limit_bytes=40 * 2 ** 20))(g)
    return out


def _groups_of(names):
    out = []
    for n in names:
        if LOC[n][0] not in out:
            out.append(LOC[n][0])
    return out


def _weight_pieces(name):
    g, off, rows = LOC[name]
    return [(d * GROUP_ROWS[g] + off, d * rows, rows) for d in range(NDEV)]


def _win_pieces():
    kv = N_KV * HEAD_DIM
    pieces = [(0, 0, D)]
    for g0, d0 in ((D, D), (D + kv, D + N_KV * LANES)):
        for g in range(N_KV):
            for half in range(2):
                pieces.append((g0 + g * HEAD_DIM, d0 + g * LANES + half * HEAD_DIM, HEAD_DIM))
    pieces.append((D + 2 * kv, D + 2 * N_KV * LANES, D))
    pieces.append((D + 2 * kv + D, D + 2 * N_KV * LANES + D, D))
    grp, off, rows = LOC["w_in"]
    out = []
    for g0, d0, n in pieces:
        while n > 0:
            dev, loc = divmod(g0, rows)
            m = min(n, rows - loc)
            out.append((dev * GROUP_ROWS[grp] + off + loc, d0, m))
            g0, d0, n = g0 + m, d0 + m, n - m
    return out


def _start_loads(src_ref, dst_ref, pieces, sems, base):
    cps = []
    for j, (s, d, n) in enumerate(pieces):
        cp = pltpu.make_async_copy(src_ref.at[pl.ds(s, n)], dst_ref.at[pl.ds(d, n)], sems.at[base + j])
        cp.start()
        cps.append(cp)
    return cps


def _load_weights(wrefs, targets, sems):
    cps, base = [], 0
    for name, dst in targets:
        pieces = _win_pieces() if name == "w_in" else _weight_pieces(name)
        cps += _start_loads(wrefs[LOC[name][0]], dst, pieces, sems, base)
        base += len(pieces)
    for cp in cps:
        cp.wait()


def _n_pieces(names):
    return sum(len(_win_pieces()) if n == "w_in" else NDEV for n in names)


def _dw(lhs, rhs, chunk, name, carries=()):
    nq = 4
    if lhs.ndim == 3:
        nch, t_tok, chunk = lhs.shape
        c = nch * chunk
        tq = t_tok // nq
        lhs_specs = [pl.BlockSpec((None, tq, chunk), lambda i, k=k: (i, k, 0)) for k in range(nq)]
    else:
        t_tok, c = lhs.shape
        tq = t_tok // nq
        lhs_specs = [pl.BlockSpec((tq, chunk), lambda i, k=k: (k, i)) for k in range(nq)]

    def body(*refs):
        lhs_refs, (rhs_ref, out_ref, rhs_s, sems) = refs[:nq], refs[nq:]
        cps = [pltpu.make_async_copy(rhs_ref.at[pl.ds(k * tq, tq)], rhs_s.at[pl.ds(k * tq, tq)], sems.at[k])
               for k in range(nq)]
        first = pl.program_id(0) == 0

        @pl.when(first)
        def _():
            for cp in cps:
                cp.start()

        for o in range(0, chunk, D // 2):
            n = min(D // 2, chunk - o)
            acc = None
            for k in range(nq):
                if o == 0:
                    @pl.when(first)
                    def _():
                        cps[k].wait()

                part = _tn(lhs_refs[k][:, pl.ds(o, n)], rhs_s[pl.ds(k * tq, tq), :])
                acc = part if acc is None else acc + part
            out_ref[pl.ds(o, n), :] = acc.astype(BF16)

    (out,), carried = _call(
        body, name=name, grid=(c // chunk,), ins=[lhs] * nq + [rhs],
        in_specs=lhs_specs + [ANY],
        out_specs=[pl.BlockSpec((chunk, D), lambda i: (i, 0))],
        out_shape=[jax.ShapeDtypeStruct((c, D), BF16)],
        scratch=[pltpu.VMEM((t_tok, D), BF16), DMA((nq,))], vmem=VMEM_BIG, carries=carries)
    return out, carried


def _silu_parts(a):
    sig = jax.nn.sigmoid(a)
    return sig, a * sig


FC = 256


def _ffn_fwd(h, gpre, gpost, wg, names, target=None, tm=512, carries=()):
    t_tok = h.shape[0]
    nt = t_tok // tm
    with_loss = target is not None
    groups = _groups_of(names)

    def body(*refs):
        refs = list(refs)
        h_ref, gpre_ref, gpost_ref = refs[:3]
        del refs[:3]
        tgt_ref = refs.pop(0) if with_loss else None
        wrefs = dict(zip(groups, refs[:len(groups)]))
        del refs[:len(groups)]
        hout_ref, a_ref, b_ref, f_ref, nb_ref = refs[:5]
        del refs[:5]
        if with_loss:
            dy_ref, loss_ref = refs[:2]
            del refs[:2]
        w1_s, w3_s, w2_s, sems = refs
        i = pl.program_id(0)

        @pl.when(i == 0)
        def _():
            _load_weights(wrefs, list(zip(names, (w1_s, w3_s, w2_s))), sems)
            if with_loss:
                loss_ref[...] = jnp.zeros_like(loss_ref)

        x = h_ref[...]
        n, _, _ = _rms_fwd(x, gpre_ref[...])
        nb = n.astype(BF16)
        nb_ref[...] = nb
        f = jnp.zeros((tm, D), F32)
        for c0 in range(0, FF, FC):
            a = _nt(nb, w1_s[pl.ds(c0, FC), :])
            b = _nt(nb, w3_s[pl.ds(c0, FC), :])
            _, sl = _silu_parts(a)
            a_ref[:, pl.ds(c0, FC)] = a.astype(BF16)
            b_ref[:, pl.ds(c0, FC)] = b.astype(BF16)
            f = f + _nn((sl * b).astype(BF16), w2_s[pl.ds(c0, FC), :])
        f_ref[...] = f
        fn, _, _ = _rms_fwd(f, gpost_ref[...])
        y = x + 0.5 * fn
        hout_ref[...] = y
        if with_loss:
            err = y - tgt_ref[...]
            dy_ref[...] = err * (1.0 / D)
            loss_ref[...] += jnp.sum(jnp.sum(err * err, axis=-1, keepdims=True), axis=0, keepdims=True) * (0.5 / D)

    ins = [h, gpre, gpost] + ([target] if with_loss else []) + [wg[g] for g in groups]
    in_specs = [_row_spec(tm, D), _const_spec((1, D)), _const_spec((1, D))]
    in_specs += ([_row_spec(tm, D)] if with_loss else []) + [ANY] * len(groups)
    out_shape = [jax.ShapeDtypeStruct((t_tok, D), F32), jax.ShapeDtypeStruct((t_tok, FF), BF16),
                 jax.ShapeDtypeStruct((t_tok, FF), BF16), jax.ShapeDtypeStruct((t_tok, D), F32),
                 jax.ShapeDtypeStruct((t_tok, D), BF16)]
    out_specs = [_row_spec(tm, D), _row_spec(tm, FF), _row_spec(tm, FF), _row_spec(tm, D), _row_spec(tm, D)]
    if with_loss:
        out_shape += [jax.ShapeDtypeStruct((t_tok, D), F32), jax.ShapeDtypeStruct((1, LANES), F32)]
        out_specs += [_row_spec(tm, D), _const_spec((1, LANES))]
    return _call(body, name="ffn_fwd_" + names[0][:4], grid=(nt,), ins=ins, in_specs=in_specs,
                 out_shape=out_shape, out_specs=out_specs,
                 scratch=[pltpu.VMEM((FF, D), BF16)] * 3 + [DMA((3 * NDEV,))], vmem=VMEM_BIG, carries=carries)


FH = FF // 2
HALF_PIECES = ((0, 256), (256, 256), (512, 256), (768, 256), (1024, 256), (1280, 128))


def _ffn_bwd_a(dh, f, a, b, gpost, wg, name_w2, tm=512, carries=()):
    t_tok = dh.shape[0]
    nt = t_tok // tm
    groups = _groups_of([name_w2])

    def body(dh_ref, f_ref, a_ref, b_ref, gpost_ref, wg_ref, dab_ref, s_ref, df_ref, dgp_ref, w2_s, sems):
        i = pl.program_id(0)

        @pl.when(i == 0)
        def _():
            _load_weights({groups[0]: wg_ref}, [(name_w2, w2_s)], sems)
            dgp_ref[...] = jnp.zeros_like(dgp_ref)

        fv = f_ref[...]
        _, fh, r = _rms_fwd(fv, gpost_ref[...])
        df, dg = _rms_bwd(0.5 * dh_ref[...], fh, r, gpost_ref[...])
        dgp_ref[...] += dg
        dfb = df.astype(BF16)
        df_ref[...] = dfb
        for half in range(2):
            for o, n in HALF_PIECES:
                c0 = half * FH + o
                ds = _nt(dfb, w2_s[pl.ds(c0, n), :])
                av = a_ref[:, pl.ds(c0, n)].astype(F32)
                bv = b_ref[:, pl.ds(c0, n)].astype(F32)
                sig, sl = _silu_parts(av)
                dab_ref[half, :, pl.ds(o, n)] = (ds * bv * (sig * (1.0 + av * (1.0 - sig)))).astype(BF16)
                dab_ref[2 + half, :, pl.ds(o, n)] = (ds * sl).astype(BF16)
                s_ref[half, :, pl.ds(o, n)] = (sl * bv).astype(BF16)

    return _call(
        body, name="ffn_bwd_a_" + name_w2[:4], grid=(nt,), ins=[dh, f, a, b, gpost, wg[groups[0]]],
        in_specs=[_row_spec(tm, D), _row_spec(tm, D), _row_spec(tm, FF), _row_spec(tm, FF), _const_spec((1, D)), ANY],
        out_specs=[pl.BlockSpec((4, tm, FH), lambda i: (0, i, 0)), pl.BlockSpec((2, tm, FH), lambda i: (0, i, 0)),
                   _row_spec(tm, D), _const_spec((1, D))],
        out_shape=[jax.ShapeDtypeStruct((4, t_tok, FH), BF16), jax.ShapeDtypeStruct((2, t_tok, FH), BF16),
                   jax.ShapeDtypeStruct((t_tok, D), BF16), jax.ShapeDtypeStruct((1, D), F32)],
        scratch=[pltpu.VMEM((FF, D), BF16), DMA((NDEV,))],
        vmem=VMEM_BIG, carries=carries)


def _ffn_bwd_dx(dab, h, dh, gpre, wg, name_w1, name_w3, tm=512, carries=()):
    t_tok = dh.shape[0]
    nt = t_tok // tm
    groups = _groups_of([name_w1, name_w3])

    def body(*refs):
        dab_refs, (h_ref, dh_ref, gpre_ref) = refs[:4], refs[4:7]
        wrefs = dict(zip(groups, refs[7:7 + len(groups)]))
        dhin_ref, dgp_ref, w13_s, sems = refs[7 + len(groups):]
        i = pl.program_id(0)

        @pl.when(i == 0)
        def _():
            _load_weights(wrefs, [(name_w1, w13_s.at[pl.ds(0, FF)]), (name_w3, w13_s.at[pl.ds(FF, FF)])], sems)
            dgp_ref[...] = jnp.zeros_like(dgp_ref)

        g = gpre_ref[...]
        _, xh, r = _rms_fwd(h_ref[...], g)
        dn = _nn(dab_refs[0][...], w13_s[pl.ds(0, FH), :])
        for k in range(1, 4):
            dn = dn + _nn(dab_refs[k][...], w13_s[pl.ds(k * FH, FH), :])
        dx, dg = _rms_bwd(dn, xh, r, g)
        dgp_ref[...] += dg
        dhin_ref[...] = dh_ref[...] + dx

    return _call(
        body, name="ffn_bwd_dx_" + name_w1[:4], grid=(nt,), ins=[dab] * 4 + [h, dh, gpre] + [wg[g] for g in groups],
        in_specs=[pl.BlockSpec((None, tm, FH), lambda i, k=k: (k, i, 0)) for k in range(4)]
        + [_row_spec(tm, D), _row_spec(tm, D), _const_spec((1, D))] + [ANY] * len(groups),
        out_specs=[_row_spec(tm, D), _const_spec((1, D))],
        out_shape=[jax.ShapeDtypeStruct((t_tok, D), F32), jax.ShapeDtypeStruct((1, D), F32)],
        scratch=[pltpu.VMEM((2 * FF, D), BF16), DMA((2 * NDEV,))],
        vmem=VMEM_BIG, carries=carries)


Q0, K0, V0, XR0, XG0 = 0, D, D + N_KV * LANES, D + 2 * N_KV * LANES, 2 * D + 2 * N_KV * LANES


def _mix_proj_fwd(h, g, bgate, wg, tm=512, carries=()):
    t_tok = h.shape[0]
    nt = t_tok // tm
    names = ("w_in", "w_gate")
    groups = _groups_of(names)

    def body(h_ref, g_ref, bg_ref, wg_ref, q_ref, k_ref, v_ref, xr_ref, xg_ref, gs_ref, ub_ref, win_s, wgt_s, sems):
        i = pl.program_id(0)

        @pl.when(i == 0)
        def _():
            _load_weights({groups[0]: wg_ref}, [("w_in", win_s), ("w_gate", wgt_s)], sems)

        n, _, _ = _rms_fwd(h_ref[...], g_ref[...])
        nb = n.astype(BF16)
        ub_ref[...] = nb
        q_ref[...] = _nt(nb, win_s[pl.ds(Q0, D), :]).astype(BF16)
        k_ref[...] = _nt(nb, win_s[pl.ds(K0, N_KV * LANES), :]).astype(BF16)
        v_ref[...] = _nt(nb, win_s[pl.ds(V0, N_KV * LANES), :]).astype(BF16)
        xr_ref[...] = _nt(nb, win_s[pl.ds(XR0, D), :])
        xg_ref[...] = _nt(nb, win_s[pl.ds(XG0, D), :])
        gs_ref[...] = jax.nn.sigmoid(_nt(nb, wgt_s[...]) + bg_ref[...]).astype(BF16)

    kvw = N_KV * LANES
    return _call(
        body, name="mix_proj_fwd", grid=(nt,), ins=[h, g, bgate, wg[groups[0]]],
        in_specs=[_row_spec(tm, D), _const_spec((1, D)), _const_spec((1, 2 * D)), ANY],
        out_specs=[_row_spec(tm, D), _row_spec(tm, kvw), _row_spec(tm, kvw), _row_spec(tm, D), _row_spec(tm, D),
                   _row_spec(tm, 2 * D), _row_spec(tm, D)],
        out_shape=[jax.ShapeDtypeStruct((t_tok, D), BF16), jax.ShapeDtypeStruct((t_tok, kvw), BF16),
                   jax.ShapeDtypeStruct((t_tok, kvw), BF16), jax.ShapeDtypeStruct((t_tok, D), F32),
                   jax.ShapeDtypeStruct((t_tok, D), F32), jax.ShapeDtypeStruct((t_tok, 2 * D), BF16),
                   jax.ShapeDtypeStruct((t_tok, D), BF16)],
        scratch=[pltpu.VMEM((INP_W, D), BF16), pltpu.VMEM((2 * D, D), BF16), DMA((_n_pieces(names),))],
        vmem=VMEM_BIG, carries=carries)


def _mix_proj_bwd(dqkv, dxr, dxg, dgpre, h, dh, g, wg, tm=512, carries=()):
    t_tok = h.shape[0]
    nt = t_tok // tm
    names = ("w_in", "w_gate")
    groups = _groups_of(names)

    def body(dqkv_ref, dxr_ref, dxg_ref, dgp_ref, h_ref, dh_ref, g_ref, wg_ref, dhin_ref, dg_ref, win_s, wgt_s, sems):
        i = pl.program_id(0)

        @pl.when(i == 0)
        def _():
            _load_weights({groups[0]: wg_ref}, [("w_in", win_s), ("w_gate", wgt_s)], sems)
            dg_ref[...] = jnp.zeros_like(dg_ref)

        gv = g_ref[...]
        _, xh, r = _rms_fwd(h_ref[...], gv)
        du = _nn(dgp_ref[...], wgt_s[...])
        du = du + _nn(dqkv_ref[...], win_s[pl.ds(Q0, XR0), :])
        du = du + _nn(dxr_ref[...], win_s[pl.ds(XR0, D), :])
        du = du + _nn(dxg_ref[...], win_s[pl.ds(XG0, D), :])
        dx, dg = _rms_bwd(du, xh, r, gv)
        dg_ref[...] += dg
        dhin_ref[...] = dh_ref[...] + dx

    return _call(
        body, name="mix_proj_bwd", grid=(nt,), ins=[dqkv, dxr, dxg, dgpre, h, dh, g, wg[groups[0]]],
        in_specs=[_row_spec(tm, XR0), _row_spec(tm, D), _row_spec(tm, D), _row_spec(tm, 2 * D), _row_spec(tm, D),
                  _row_spec(tm, D), _const_spec((1, D)), ANY],
        out_specs=[_row_spec(tm, D), _const_spec((1, D))],
        out_shape=[jax.ShapeDtypeStruct((t_tok, D), F32), jax.ShapeDtypeStruct((1, D), F32)],
        scratch=[pltpu.VMEM((INP_W, D), BF16), pltpu.VMEM((2 * D, D), BF16), DMA((_n_pieces(names),))],
        vmem=VMEM_BIG, carries=carries)


def _shift_down(x, d, fill):
    row = lax.broadcasted_iota(jnp.int32, x.shape, 0)
    return jnp.where(row >= d, pltpu.roll(x, d, 0), fill)


def _shift_up(x, d, fill):
    rows = x.shape[0]
    row = lax.broadcasted_iota(jnp.int32, x.shape, 0)
    return jnp.where(row < rows - d, pltpu.roll(x, rows - d, 0), fill)


def _scan_rows(a, b, reverse):
    rows = a.shape[0]
    d = 1
    while d < rows:
        if d < 8:
            shift = _shift_up if reverse else _shift_down
            b = a * shift(b, d, 0.0) + b
            a = a * shift(a, d, 1.0)
        elif reverse:
            b = jnp.concatenate([a[:rows - d] * b[d:] + b[:rows - d], b[rows - d:]], axis=0)
            a = jnp.concatenate([a[:rows - d] * a[d:], a[rows - d:]], axis=0)
        else:
            b = jnp.concatenate([b[:d], a[d:] * b[:rows - d] + b[d:]], axis=0)
            a = jnp.concatenate([a[:d], a[d:] * a[:rows - d]], axis=0)
        d *= 2
    return a, b


def _softplus(x):
    return jnp.maximum(x, 0.0) + jnp.log(1.0 + jnp.exp(-jnp.abs(x)))


_GELU_C = math.sqrt(2.0 / math.pi)


def _gelu_parts(x):
    th = jnp.tanh(_GELU_C * (x + 0.044715 * x * x * x))
    val = 0.5 * x * (1.0 + th)
    grad = 0.5 * (1.0 + th) + 0.5 * x * (1.0 - th * th) * _GELU_C * (1.0 + 3.0 * 0.044715 * x * x)
    return val, grad


def _lru_pre(x, halo, cw_ref, cb_ref, wa_ref, wx_ref, ba_ref, bx_ref, lam_ref):
    ext = jnp.concatenate([halo, x], axis=0)
    shifted = [x] + [pltpu.roll(ext, k, 0)[8:] for k in (1, 2, 3)]
    xc = cb_ref[...] + cw_ref[pl.ds(CONV_WIDTH - 1, 1), :] * x
    for k in (1, 2, 3):
        xc = xc + cw_ref[pl.ds(CONV_WIDTH - 1 - k, 1), :] * shifted[k]
    xcb = xc.astype(BF16)
    r = jax.nn.sigmoid(_nn(xcb, wa_ref[...]) + ba_ref[...])
    ig = jax.nn.sigmoid(_nn(xcb, wx_ref[...]) + bx_ref[...])
    sp = _softplus(-lam_ref[...])
    log_a = -LRU_C * r * sp
    a = jnp.exp(log_a)
    th = jnp.tanh(log_a)
    mult = jnp.sqrt(-2.0 * th / (1.0 - th))
    return shifted, xc, xcb, r, ig, sp, a, mult


def _lru_specs(nt, reverse):
    def tt(t):
        return nt - 1 - t if reverse else t
    tile = pl.BlockSpec((LRU_ROWS, LANES), lambda cb, t: (tt(t), cb))
    halo = pl.BlockSpec((8, LANES), lambda cb, t: (jnp.maximum(tt(t) * (LRU_ROWS // 8) - 1, 0), cb))
    vec = pl.BlockSpec((1, LANES), lambda cb, t: (0, cb))
    cw = pl.BlockSpec((CONV_WIDTH, LANES), lambda cb, t: (0, cb))
    mat = pl.BlockSpec((None, LANES, LANES), lambda cb, t: (cb, 0, 0))
    return tile, halo, vec, cw, mat


def _lru_fwd(xr, xg, cw, cb, wa, wx, ba, bx, lam, carries=()):
    t_tok = xr.shape[0]
    nt = t_tok // LRU_ROWS
    rows = LRU_ROWS

    def body(xr_ref, xg_ref, cw_ref, cb_ref, wa_ref, wx_ref, ba_ref, bx_ref, lam_ref, y_ref, h_ref, tail_s, hc_s):
        t = pl.program_id(1)

        @pl.when(t == 0)
        def _():
            tail_s[...] = jnp.zeros_like(tail_s)
            hc_s[...] = jnp.zeros_like(hc_s)

        x = xr_ref[...]
        _, xc, _, _, ig, _, a, mult = _lru_pre(x, tail_s[...], cw_ref, cb_ref, wa_ref, wx_ref, ba_ref, bx_ref, lam_ref)
        tail_s[...] = xr_ref[pl.ds(rows - 8, 8), :]
        acc_a, acc_b = _scan_rows(a, mult * (ig * xc), False)
        hv = acc_b + acc_a * hc_s[...]
        h_ref[...] = hv
        hc_s[...] = h_ref[pl.ds(rows - 1, 1), :]
        gl, _ = _gelu_parts(xg_ref[...])
        y_ref[...] = (hv * gl).astype(BF16)

    tile, _, vec, cws, mat = _lru_specs(nt, False)
    return _call(
        body, name="lru_fwd", grid=(D // LANES, nt), ins=[xr, xg, cw, cb, wa, wx, ba, bx, lam],
        in_specs=[tile, tile, cws, vec, mat, mat, vec, vec, vec],
        out_specs=[tile, tile],
        out_shape=[jax.ShapeDtypeStruct((t_tok, D), BF16), jax.ShapeDtypeStruct((t_tok, D), F32)],
        scratch=[pltpu.VMEM((8, LANES), F32), pltpu.VMEM((1, LANES), F32)], carries=carries)


def _lru_bwd(dy, xr, xg, hseq, cw, cb, wa, wx, ba, bx, lam, carries=()):
    t_tok = xr.shape[0]
    nt = t_tok // LRU_ROWS
    rows = LRU_ROWS

    def body(dy_ref, xr_ref, xrh_ref, xg_ref, h_ref, hh_ref, cw_ref, cb_ref, wa_ref, wx_ref, ba_ref, bx_ref, lam_ref,
             dxr_ref, dxg_ref, dvec_ref, dwa_ref, dwx_ref, gcar_s, acar_s, head_s, tmp_s):
        t = pl.program_id(1)
        first_tile = t == nt - 1

        @pl.when(t == 0)
        def _():
            gcar_s[...] = jnp.zeros_like(gcar_s)
            acar_s[...] = jnp.zeros_like(acar_s)
            head_s[...] = jnp.zeros_like(head_s)
            dvec_ref[...] = jnp.zeros_like(dvec_ref)
            dwa_ref[...] = jnp.zeros_like(dwa_ref)
            dwx_ref[...] = jnp.zeros_like(dwx_ref)

        x = xr_ref[...]
        halo = jnp.where(first_tile, 0.0, xrh_ref[...])
        shifted, xc, xcb, r, ig, sp, a, mult = _lru_pre(x, halo, cw_ref, cb_ref, wa_ref, wx_ref, ba_ref, bx_ref, lam_ref)
        hv = h_ref[...]
        dyv = dy_ref[...]
        gl, glg = _gelu_parts(xg_ref[...])
        dxg_ref[...] = (dyv * hv * glg).astype(BF16)
        acc_a, acc_b = _scan_rows(_shift_up(a, 1, acar_s[...]), dyv * gl, True)
        g = acc_b + acc_a * gcar_s[...]
        hhalo = jnp.where(first_tile, 0.0, hh_ref[...])
        hprev = pltpu.roll(jnp.concatenate([hhalo, hv], axis=0), 1, 0)[8:]
        dmult = g * ig * xc
        dlog_a = a * (g * hprev) - dmult * a * a / mult
        dig = g * mult * xc
        dxc = g * mult * ig
        dzr = (dlog_a * (-LRU_C * sp)) * r * (1.0 - r)
        dzx = dig * ig * (1.0 - ig)
        dzrb, dzxb = dzr.astype(BF16), dzx.astype(BF16)
        dxc = dxc + _nt(dzrb, wa_ref[...]) + _nt(dzxb, wx_ref[...])
        dwa_ref[...] += _tn(xcb, dzrb)
        dwx_ref[...] += _tn(xcb, dzxb)
        dsp = jnp.sum(dlog_a * (-LRU_C * r), axis=0, keepdims=True)
        dlam = dsp * (-jax.nn.sigmoid(-lam_ref[...]))
        vrow = lax.broadcasted_iota(jnp.int32, (8, LANES), 0)
        upd = jnp.where(vrow == 4, jnp.sum(dxc, axis=0, keepdims=True), 0.0)
        upd = jnp.where(vrow == 5, jnp.sum(dzr, axis=0, keepdims=True), upd)
        upd = jnp.where(vrow == 6, jnp.sum(dzx, axis=0, keepdims=True), upd)
        upd = jnp.where(vrow == 7, dlam, upd)
        for k in range(CONV_WIDTH):
            upd = jnp.where(vrow == CONV_WIDTH - 1 - k, jnp.sum(dxc * shifted[k], axis=0, keepdims=True), upd)
        dvec_ref[...] += upd
        ext = jnp.concatenate([dxc, head_s[...]], axis=0)
        dxr = cw_ref[pl.ds(CONV_WIDTH - 1, 1), :] * dxc
        for k in (1, 2, 3):
            dxr = dxr + cw_ref[pl.ds(CONV_WIDTH - 1 - k, 1), :] * pltpu.roll(ext, rows + 8 - k, 0)[:rows]
        dxr_ref[...] = dxr.astype(BF16)
        tmp_s[...] = g
        gcar_s[...] = tmp_s[pl.ds(0, 1), :]
        tmp_s[...] = a
        acar_s[...] = tmp_s[pl.ds(0, 1), :]
        tmp_s[...] = dxc
        head_s[...] = tmp_s[pl.ds(0, 8), :]

    tile, halo, vec, cws, mat = _lru_specs(nt, True)
    return _call(
        body, name="lru_bwd", grid=(D // LANES, nt), ins=[dy, xr, xr, xg, hseq, hseq, cw, cb, wa, wx, ba, bx, lam],
        in_specs=[tile, tile, halo, tile, tile, halo, cws, vec, mat, mat, vec, vec, vec],
        out_specs=[tile, tile, pl.BlockSpec((8, LANES), lambda cb, t: (0, cb)), mat, mat],
        out_shape=[jax.ShapeDtypeStruct((t_tok, D), BF16), jax.ShapeDtypeStruct((t_tok, D), BF16),
                   jax.ShapeDtypeStruct((8, D), F32), jax.ShapeDtypeStruct((D // LANES, LANES, LANES), F32),
                   jax.ShapeDtypeStruct((D // LANES, LANES, LANES), F32)],
        scratch=[pltpu.VMEM((1, LANES), F32), pltpu.VMEM((1, LANES), F32), pltpu.VMEM((8, LANES), F32),
                 pltpu.VMEM((rows, LANES), F32)], carries=carries)


def _t5_bucket_np(rel):
    nb = N_BUCKETS // 2
    max_exact = nb // 2
    ret = np.where(rel > 0, nb, 0)
    n = np.abs(rel)
    nf = np.maximum(n, 1).astype(np.float32)
    large = max_exact + (np.log(nf / np.float32(max_exact)) / np.float32(math.log(MAX_DISTANCE / max_exact))
                         * np.float32(nb - max_exact)).astype(np.int32)
    large = np.minimum(large, nb - 1)
    return ret + np.where(n < max_exact, n, large)


def _bucket_map():
    r = np.arange(QT)[:, None]
    c = np.arange(KW)[None, :]
    j = c - (r // CHUNK) * CHUNK
    band = (j >= 0) & (j < WINDOW + CHUNK)
    return np.where(band, _t5_bucket_np(c - r - WINDOW), -1).astype(np.int32)


def _attn_specs(nt, reverse):
    def tt(i):
        return nt - 1 - i if reverse else i
    kvw = N_KV * LANES
    qs = pl.BlockSpec((QT, D), lambda i: (tt(i), 0))
    cur = pl.BlockSpec((QT, kvw), lambda i: (tt(i), 0))
    prev = pl.BlockSpec((WINDOW, kvw), lambda i: (jnp.maximum(tt(i) * (QT // WINDOW) - 1, 0), 0))
    lse = pl.BlockSpec((QT, LANES), lambda i: (tt(i), 0))
    return qs, cur, prev, lse


REP = N_HEADS // N_KV
SCALE = HEAD_DIM ** -0.5


def _stack_heads(x_ref, g, lo, scale=None):
    parts = []
    for hl in range(REP):
        xs = x_ref[:, pl.ds((2 * g + hl // 2) * LANES, LANES)]
        xs = jnp.where(lo if hl % 2 == 0 else jnp.logical_not(lo), xs, jnp.zeros_like(xs))
        parts.append(xs if scale is None else xs * jnp.asarray(scale, xs.dtype))
    return jnp.concatenate(parts, axis=0)


def _stack_sinks(sink_ref, g, srow):
    sk = jnp.full(srow.shape, sink_ref[REP * g + REP - 1], F32)
    for hl in range(REP - 2, -1, -1):
        sk = jnp.where(srow < (hl + 1) * QT, sink_ref[REP * g + hl], sk)
    return sk


def _attn_fwd(q, kd, vd, bias, sinks, carries=()):
    t_tok = q.shape[0]
    nt = t_tok // QT

    def body(q_ref, kp_ref, kc_ref, vp_ref, vc_ref, bias_ref, sink_ref, o_ref, lse_ref):
        i = pl.program_id(0)
        col = lax.broadcasted_iota(jnp.int32, (1, KW), 1)
        first = jnp.where((i == 0) & (col < WINDOW), NEG_INF, 0.0)
        lane = lax.broadcasted_iota(jnp.int32, (QT, LANES), 1)
        lo = lane < HEAD_DIM
        srow = lax.broadcasted_iota(jnp.int32, (REP * QT, 1), 0)
        lse_t = jnp.zeros((QT, LANES), F32)
        for g in range(N_KV):
            kwin = jnp.concatenate([kp_ref[:, pl.ds(g * LANES, LANES)], kc_ref[:, pl.ds(g * LANES, LANES)]], axis=0)
            vwin = jnp.concatenate([vp_ref[:, pl.ds(g * LANES, LANES)], vc_ref[:, pl.ds(g * LANES, LANES)]], axis=0)
            qst = _stack_heads(q_ref, g, lo, SCALE)
            s = _nt(qst, kwin) + (bias_ref[g] + first)
            sk = _stack_sinks(sink_ref, g, srow)
            m = jnp.maximum(jnp.max(s, axis=-1, keepdims=True), sk)
            e = jnp.exp(s - m)
            l = jnp.sum(e, axis=-1, keepdims=True) + jnp.exp(sk - m)
            p = e / l
            ost = _nn(p.astype(BF16), vwin)
            lse_s = m + jnp.log(l)
            for sl in range(2):
                o_ref[:, pl.ds((2 * g + sl) * LANES, LANES)] = jnp.where(
                    lo, ost[2 * sl * QT:(2 * sl + 1) * QT], ost[(2 * sl + 1) * QT:(2 * sl + 2) * QT]).astype(BF16)
            for hl in range(REP):
                lse_t = jnp.where(lane == REP * g + hl, lse_s[hl * QT:(hl + 1) * QT], lse_t)
        lse_ref[...] = lse_t

    qs, cur, prev, lse = _attn_specs(nt, False)
    return _call(
        body, name="attn_fwd", grid=(nt,), ins=[q, kd, kd, vd, vd, bias, sinks],
        in_specs=[qs, prev, cur, prev, cur, _const_spec((N_KV, REP * QT, KW)), pl.BlockSpec(memory_space=pltpu.SMEM)],
        out_specs=[qs, lse],
        out_shape=[jax.ShapeDtypeStruct((t_tok, D), BF16), jax.ShapeDtypeStruct((t_tok, LANES), F32)],
        vmem=48 * 2 ** 20, carries=carries)


def _attn_bwd(q, kd, vd, o, do, lse, bias, sinks, carries=()):
    t_tok = q.shape[0]
    nt = t_tok // QT
    kvw = N_KV * LANES

    def body(q_ref, kp_ref, kc_ref, vp_ref, vc_ref, o_ref, do_ref, lse_ref, bias_ref, sink_ref,
             dqkv_ref, ds_ref, dsink_ref, kcar_s, vcar_s):
        i = pl.program_id(0)
        tile = nt - 1 - i

        @pl.when(i == 0)
        def _():
            kcar_s[...] = jnp.zeros_like(kcar_s)
            vcar_s[...] = jnp.zeros_like(vcar_s)
            ds_ref[...] = jnp.zeros_like(ds_ref)
            dsink_ref[...] = jnp.zeros_like(dsink_ref)

        col = lax.broadcasted_iota(jnp.int32, (1, KW), 1)
        first = jnp.where((tile == 0) & (col < WINDOW), NEG_INF, 0.0)
        lane = lax.broadcasted_iota(jnp.int32, (QT, LANES), 1)
        lo = lane < HEAD_DIM
        lane_k = lax.broadcasted_iota(jnp.int32, (KW, LANES), 1)
        lane_1 = lax.broadcasted_iota(jnp.int32, (1, LANES), 1)
        srow = lax.broadcasted_iota(jnp.int32, (REP * QT, 1), 0)
        lse_t = lse_ref[...]
        dsink = jnp.zeros((1, LANES), F32)
        for g in range(N_KV):
            kwin = jnp.concatenate([kp_ref[:, pl.ds(g * LANES, LANES)], kc_ref[:, pl.ds(g * LANES, LANES)]], axis=0)
            vwin = jnp.concatenate([vp_ref[:, pl.ds(g * LANES, LANES)], vc_ref[:, pl.ds(g * LANES, LANES)]], axis=0)
            qst = _stack_heads(q_ref, g, lo, SCALE)
            dost = _stack_heads(do_ref, g, lo)
            od = [do_ref[:, pl.ds((2 * g + sl) * LANES, LANES)].astype(F32)
                  * o_ref[:, pl.ds((2 * g + sl) * LANES, LANES)].astype(F32) for sl in range(2)]
            drow = jnp.concatenate([jnp.sum(jnp.where(lo if hl % 2 == 0 else jnp.logical_not(lo), od[hl // 2], 0.0),
                                            axis=-1, keepdims=True) for hl in range(REP)], axis=0)
            lse_s = jnp.concatenate([jnp.sum(jnp.where(lane == REP * g + hl, lse_t, 0.0), axis=-1, keepdims=True)
                                     for hl in range(REP)], axis=0)
            s = _nt(qst, kwin) + (bias_ref[g] + first)
            p = jnp.exp(s - lse_s)
            ds = p * (_nt(dost, vwin) - drow)
            ds_ref[g] += ds
            tsink = -(jnp.exp(_stack_sinks(sink_ref, g, srow) - lse_s) * drow)
            for hl in range(REP):
                dsink = dsink + jnp.where(lane_1 == REP * g + hl,
                                          jnp.sum(tsink[hl * QT:(hl + 1) * QT], axis=0, keepdims=True), 0.0)
            dsb = ds.astype(BF16)
            dqst = _nn(dsb, kwin) * SCALE
            for sl in range(2):
                dqkv_ref[:, pl.ds((2 * g + sl) * LANES, LANES)] = jnp.where(
                    lo, dqst[2 * sl * QT:(2 * sl + 1) * QT], dqst[(2 * sl + 1) * QT:(2 * sl + 2) * QT]).astype(BF16)
            dk_acc = _tn(dsb, qst)
            dv_acc = _tn(p.astype(BF16), dost)
            dk_f = jnp.where(lane_k < HEAD_DIM, dk_acc + pltpu.roll(dk_acc, HEAD_DIM, 1), 0.0)
            dv_f = jnp.where(lane_k < HEAD_DIM, dv_acc + pltpu.roll(dv_acc, HEAD_DIM, 1), 0.0)
            for acc, col0, car in ((dk_f, K0, kcar_s), (dv_f, V0, vcar_s)):
                cs = pl.ds(g * LANES, LANES)
                co = pl.ds(col0 + g * LANES, LANES)
                if QT > WINDOW:
                    dqkv_ref[pl.ds(0, QT - WINDOW), co] = acc[WINDOW:QT].astype(BF16)
                dqkv_ref[pl.ds(QT - WINDOW, WINDOW), co] = (acc[QT:KW] + car[:, cs]).astype(BF16)
                car[:, cs] = acc[0:WINDOW]
        dsink_ref[...] += dsink

    qs, cur, prev, lse_s = _attn_specs(nt, True)
    return _call(
        body, name="attn_bwd", grid=(nt,), ins=[q, kd, kd, vd, vd, o, do, lse, bias, sinks],
        in_specs=[qs, prev, cur, prev, cur, qs, qs, lse_s, _const_spec((N_KV, REP * QT, KW)),
                  pl.BlockSpec(memory_space=pltpu.SMEM)],
        out_specs=[pl.BlockSpec((QT, XR0), lambda i: (nt - 1 - i, 0)), _const_spec((N_KV, REP * QT, KW)),
                   _const_spec((1, LANES))],
        out_shape=[jax.ShapeDtypeStruct((t_tok, XR0), BF16), jax.ShapeDtypeStruct((N_KV, REP * QT, KW), F32),
                   jax.ShapeDtypeStruct((1, LANES), F32)],
        scratch=[pltpu.VMEM((WINDOW, kvw), F32), pltpu.VMEM((WINDOW, kvw), F32)],
        vmem=VMEM_BIG, carries=carries)


def _bias_tile(table, bmap):
    def body(tab_ref, bm_ref, out_ref):
        bm = bm_ref[...]

        def per_head(hd, carry):
            acc = jnp.full((QT, KW), NEG_INF, F32)
            for b in range(N_BUCKETS):
                acc = jnp.where(bm == b, tab_ref[b, hd], acc)
            out_ref[hd] = acc
            return carry

        lax.fori_loop(0, N_HEADS, per_head, 0)

    return pl.pallas_call(
        body, name="bias_tile", out_shape=jax.ShapeDtypeStruct((N_HEADS, QT, KW), F32),
        in_specs=[pl.BlockSpec(memory_space=pltpu.SMEM), pl.BlockSpec(memory_space=pltpu.VMEM)],
        out_specs=pl.BlockSpec(memory_space=pltpu.VMEM))(table, bmap)


def _bias_grad(ds_acc, bmap):
    def body(ds_ref, bm_ref, out_ref):
        row = lax.broadcasted_iota(jnp.int32, (N_BUCKETS, LANES), 0)
        lane = lax.broadcasted_iota(jnp.int32, (N_BUCKETS, LANES), 1)
        bm = bm_ref[...]

        def per_head(hd, res):
            dsv = ds_ref[hd]
            for b in range(N_BUCKETS):
                val = jnp.sum(jnp.sum(jnp.where(bm == b, dsv, 0.0), axis=0, keepdims=True), axis=1, keepdims=True)
                res = jnp.where((row == b) & (lane == hd), val, res)
            return res

        out_ref[...] = lax.fori_loop(0, N_HEADS, per_head, jnp.zeros((N_BUCKETS, LANES), F32))

    return pl.pallas_call(
        body, name="bias_grad", out_shape=jax.ShapeDtypeStruct((N_BUCKETS, LANES), F32),
        in_specs=[pl.BlockSpec(memory_space=pltpu.VMEM), pl.BlockSpec(memory_space=pltpu.VMEM)],
        out_specs=pl.BlockSpec(memory_space=pltpu.VMEM))(ds_acc, bmap)


MIXOUT = ("w_lru_out", "w_attn_out", "w_o")


def _mix_out_fwd(ya_in, o, gs, h, g, wg, tm=512, carries=()):
    t_tok = h.shape[0]
    nt = t_tok // tm
    groups = _groups_of(MIXOUT)

    def body(ya_ref, o_ref, gs_ref, h_ref, g_ref, wg_ref, hout_ref, yao_ref, ybo_ref, z_ref, wa_s, wb_s, wo_s, sems):
        i = pl.program_id(0)

        @pl.when(i == 0)
        def _():
            _load_weights({groups[0]: wg_ref}, list(zip(MIXOUT, (wa_s, wb_s, wo_s))), sems)

        ya = _nn(ya_ref[...], wa_s[...])
        yb = _nn(o_ref[...], wb_s[...])
        yao_ref[...] = ya.astype(BF16)
        ybo_ref[...] = yb.astype(BF16)
        merged = gs_ref[:, pl.ds(0, D)].astype(F32) * ya + gs_ref[:, pl.ds(D, D)].astype(F32) * yb
        z = _nn(merged.astype(BF16), wo_s[...])
        z_ref[...] = z
        zn, _, _ = _rms_fwd(z, g_ref[...])
        hout_ref[...] = h_ref[...] + zn

    return _call(
        body, name="mix_out_fwd", grid=(nt,), ins=[ya_in, o, gs, h, g, wg[groups[0]]],
        in_specs=[_row_spec(tm, D), _row_spec(tm, D), _row_spec(tm, 2 * D), _row_spec(tm, D), _const_spec((1, D)), ANY],
        out_specs=[_row_spec(tm, D)] * 4,
        out_shape=[jax.ShapeDtypeStruct((t_tok, D), F32), jax.ShapeDtypeStruct((t_tok, D), BF16),
                   jax.ShapeDtypeStruct((t_tok, D), BF16), jax.ShapeDtypeStruct((t_tok, D), F32)],
        scratch=[pltpu.VMEM((D, D), BF16)] * 3 + [DMA((3 * NDEV,))],
        vmem=48 * 2 ** 20, carries=carries)


def _mix_out_bwd(dh, z, ya, yb, gs, g, wg, tm=512, carries=()):
    t_tok = dh.shape[0]
    nt = t_tok // tm
    groups = _groups_of(MIXOUT)

    def body(dh_ref, z_ref, ya_ref, yb_ref, gs_ref, g_ref, wg_ref,
             dyain_ref, do_ref, dgpre_ref, dya_ref, dyb_ref, mg_ref, dz_ref, dg_ref, dbg_ref,
             wa_s, wb_s, wo_s, sems):
        i = pl.program_id(0)

        @pl.when(i == 0)
        def _():
            _load_weights({groups[0]: wg_ref}, list(zip(MIXOUT, (wa_s, wb_s, wo_s))), sems)
            dg_ref[...] = jnp.zeros_like(dg_ref)
            dbg_ref[...] = jnp.zeros_like(dbg_ref)

        gv = g_ref[...]
        _, zh, r = _rms_fwd(z_ref[...], gv)
        dz, dg = _rms_bwd(dh_ref[...], zh, r, gv)
        dg_ref[...] += dg
        dzb = dz.astype(BF16)
        dz_ref[...] = dzb
        ga, gb = gs_ref[:, pl.ds(0, D)].astype(F32), gs_ref[:, pl.ds(D, D)].astype(F32)
        ya_v, yb_v = ya_ref[...].astype(F32), yb_ref[...].astype(F32)
        mg_ref[...] = (ga * ya_v + gb * yb_v).astype(BF16)
        dm = _nt(dzb, wo_s[...])
        dga = dm * ya_v * ga * (1.0 - ga)
        dgb = dm * yb_v * gb * (1.0 - gb)
        dgpre_ref[:, pl.ds(0, D)] = dga.astype(BF16)
        dgpre_ref[:, pl.ds(D, D)] = dgb.astype(BF16)
        dbg_ref[:, pl.ds(0, D)] += jnp.sum(dga, axis=0, keepdims=True)
        dbg_ref[:, pl.ds(D, D)] += jnp.sum(dgb, axis=0, keepdims=True)
        dya = (dm * ga).astype(BF16)
        dyb = (dm * gb).astype(BF16)
        dya_ref[...] = dya
        dyb_ref[...] = dyb
        dyain_ref[...] = _nt(dya, wa_s[...])
        do_ref[...] = _nt(dyb, wb_s[...]).astype(BF16)

    bf = jax.ShapeDtypeStruct((t_tok, D), BF16)
    return _call(
        body, name="mix_out_bwd", grid=(nt,), ins=[dh, z, ya, yb, gs, g, wg[groups[0]]],
        in_specs=[_row_spec(tm, D)] * 4 + [_row_spec(tm, 2 * D), _const_spec((1, D)), ANY],
        out_specs=[_row_spec(tm, D), _row_spec(tm, D), _row_spec(tm, 2 * D)] + [_row_spec(tm, D)] * 4
        + [_const_spec((1, D)), _const_spec((1, 2 * D))],
        out_shape=[jax.ShapeDtypeStruct((t_tok, D), F32), bf, jax.ShapeDtypeStruct((t_tok, 2 * D), BF16), bf, bf, bf, bf,
                   jax.ShapeDtypeStruct((1, D), F32), jax.ShapeDtypeStruct((1, 2 * D), F32)],
        scratch=[pltpu.VMEM((D, D), BF16)] * 3 + [DMA((3 * NDEV,))],
        vmem=VMEM_BIG, carries=carries)


def _sum_parts(parts_list):
    n = len(parts_list)
    _, r, c = parts_list[0].shape
    tc = 256

    def body(*refs):
        for p_ref, o_ref in zip(refs[:n], refs[n:]):
            acc = p_ref[0].astype(F32)
            for s in range(1, NDEV):
                acc = acc + p_ref[s].astype(F32)
            o_ref[...] = acc

    return pl.pallas_call(
        body, name=f"sum_parts_{r}", grid=(c // tc,),
        in_specs=[pl.BlockSpec((NDEV, r, tc), lambda i: (0, 0, i))] * n,
        out_specs=[pl.BlockSpec((r, tc), lambda i: (0, i))] * n,
        out_shape=[jax.ShapeDtypeStruct((r, c), F32)] * n,
        compiler_params=pltpu.CompilerParams(dimension_semantics=("arbitrary",), vmem_limit_bytes=48 * 2 ** 20),
    )(*parts_list)


def _adamw_math(w, g, m, v):
    m = ADAM_B1 * m + (1.0 - ADAM_B1) * g
    v = ADAM_B2 * v + (1.0 - ADAM_B2) * (g * g)
    m_hat = m / (1.0 - ADAM_B1 ** ADAM_STEP)
    v_hat = v / (1.0 - ADAM_B2 ** ADAM_STEP)
    delta = -ADAM_LR * (m_hat / (jnp.sqrt(v_hat) + ADAM_EPS) + ADAM_WD * w)
    return delta, m, v


def _sum_ready(parts_list, name, carries=(), chip_sums=()):
    n = len(parts_list)
    c = parts_list[0].shape[2]
    tc = 2 * LANES

    def body(*refs):
        for k, (p_ref, o_ref) in enumerate(zip(refs[:n], refs[n:])):
            if k in chip_sums:
                core = lax.axis_index("c")
                slots = [2 * ch + core for ch in range(NDEV // 2)]
            else:
                slots = list(range(NDEV))
            g = p_ref[slots[0]].astype(F32)
            for s in slots[1:]:
                g = g + p_ref[s].astype(F32)
            o_ref[...] = g

    return _call(
        body, name=name, grid=(c // tc,), ins=list(parts_list),
        in_specs=[pl.BlockSpec((NDEV, p.shape[1], tc), lambda i: (0, 0, i)) for p in parts_list],
        out_shape=[jax.ShapeDtypeStruct((p.shape[1], c), F32) for p in parts_list],
        out_specs=[pl.BlockSpec((p.shape[1], tc), lambda i: (0, i)) for p in parts_list],
        vmem=48 * 2 ** 20, carries=carries)


def _adamw_cols(items, name):
    n = len(items)
    c = items[0][1].shape[1]
    tc = LANES

    def body(*refs):
        for k in range(n):
            g_ref, w_ref, m_ref, v_ref = refs[4 * k:4 * k + 4]
            go_ref, d_ref, nm_ref, nv_ref = refs[4 * n + 4 * k:4 * n + 4 * k + 4]
            g = g_ref[...]
            d, m, v = _adamw_math(w_ref[...], g, m_ref[...], v_ref[...])
            go_ref[...] = g
            d_ref[...] = d
            nm_ref[...] = m
            nv_ref[...] = v

    ins, specs, out_shape = [], [], []
    for g, w, m, v in items:
        r = w.shape[0]
        ins += [g, w, m, v]
        specs += [pl.BlockSpec((r, tc), lambda i: (0, i))] * 4
        out_shape += [jax.ShapeDtypeStruct((r, c), F32)] * 4
    outs, _ = _call(body, name=name, grid=(c // tc,), ins=ins, in_specs=specs, out_shape=out_shape,
                    out_specs=specs, vmem=48 * 2 ** 20)
    return [tuple(outs[4 * k:4 * k + 4]) for k in range(n)]


def _adamw_body(n):
    def body(*refs):
        for k in range(n):
            g_ref, w_ref, m_ref, v_ref = refs[4 * k:4 * k + 4]
            d_ref, nm_ref, nv_ref = refs[4 * n + 3 * k:4 * n + 3 * k + 3]
            d, m, v = _adamw_math(w_ref[...], g_ref[...], m_ref[...], v_ref[...])
            d_ref[...] = d
            nm_ref[...] = m
            nv_ref[...] = v
    return body


def _adamw(items):
    n = len(items)
    r, c = items[0][0].shape
    tr = r if r * c <= 2 ** 18 else max(t for t in range(8, 65, 8) if r % t == 0)
    spec = pl.BlockSpec((tr, c), lambda i: (i, 0))
    outs = pl.pallas_call(
        _adamw_body(n), name=f"adamw_{r}x{c}", grid=(r // tr,),
        in_specs=[spec] * (4 * n), out_specs=[spec] * (3 * n),
        out_shape=[jax.ShapeDtypeStruct((r, c), F32)] * (3 * n),
        compiler_params=pltpu.CompilerParams(dimension_semantics=("arbitrary",), vmem_limit_bytes=40 * 2 ** 20),
    )(*[a for it in items for a in it])
    return [tuple(outs[3 * k:3 * k + 3]) for k in range(n)]


def _adamw_small(items):
    n = len(items)
    vm = pl.BlockSpec(memory_space=pltpu.VMEM)
    outs = pl.pallas_call(
        _adamw_body(n), name="adamw_small", in_specs=[vm] * (4 * n), out_specs=[vm] * (3 * n),
        out_shape=[jax.ShapeDtypeStruct(it[1].shape, F32) for it in items for _ in range(3)],
    )(*[a for it in items for a in it])
    return [tuple(outs[3 * k:3 * k + 3]) for k in range(n)]


def _pack_small(arrs):
    rows, offs = [], []
    total = 0
    for a in arrs:
        flat = a.reshape(-1).astype(F32)
        nr = -(-flat.shape[0] // LANES)
        flat = jnp.pad(flat, (0, nr * LANES - flat.shape[0]))
        rows.append(flat.reshape(nr, LANES))
        offs.append((total, nr))
        total += nr
    pad = -total % 8
    if pad:
        rows.append(jnp.zeros((pad, LANES), F32))
    return jnp.concatenate(rows, axis=0), offs


def _unpack_small(pack, offs, shapes):
    out = []
    for (o, nr), shp in zip(offs, shapes):
        size = int(np.prod(shp))
        out.append(pack[o:o + nr].reshape(-1)[:size].reshape(shp))
    return out


def _sum_small(gathered, rows):
    def body(p_ref, o_ref):
        acc = p_ref[pl.ds(0, rows), :]
        for s in range(1, NDEV):
            acc = acc + p_ref[pl.ds(s * rows, rows), :]
        o_ref[...] = acc

    return pl.pallas_call(
        body, name="sum_small", out_shape=jax.ShapeDtypeStruct((rows, LANES), F32),
        in_specs=[pl.BlockSpec(memory_space=pltpu.VMEM)], out_specs=pl.BlockSpec(memory_space=pltpu.VMEM))(gathered)


def _block_diag(w):
    w = w.reshape(D // LANES, 2, LRU_BLOCK, LRU_BLOCK)
    z = jnp.zeros((D // LANES, LRU_BLOCK, LRU_BLOCK), w.dtype)
    top = jnp.concatenate([w[:, 0], z], axis=2)
    bot = jnp.concatenate([z, w[:, 1]], axis=2)
    return jnp.concatenate([top, bot], axis=1)


def _block_diag_grad(dw):
    a = dw[:, :LRU_BLOCK, :LRU_BLOCK]
    b = dw[:, LRU_BLOCK:, LRU_BLOCK:]
    return jnp.stack([a, b], axis=1).reshape(D // LRU_BLOCK, LRU_BLOCK, LRU_BLOCK)


def kernel(x, ffn1_pre_g, ffn1_w1, ffn1_w3, ffn1_w2, ffn1_post_g, mix_pre_g, w_in, conv_w, conv_b, rg_a_w, rg_a_b, rg_x_w, rg_x_b, lru_lambda, w_lru_out, attn_sinks, rel_bias, w_attn_out, w_gate, b_gate, w_o, mix_post_g, ffn2_pre_g, ffn2_w1, ffn2_w3, ffn2_w2, ffn2_post_g, loss_target, m_ffn1_pre_g, m_ffn1_w1, m_ffn1_w3, m_ffn1_w2, m_ffn1_post_g, m_mix_pre_g, m_w_in, m_conv_w, m_conv_b, m_rg_a_w, m_rg_a_b, m_rg_x_w, m_rg_x_b, m_lru_lambda, m_w_lru_out, m_attn_sinks, m_rel_bias, m_w_attn_out, m_w_gate, m_b_gate, m_w_o, m_mix_post_g, m_ffn2_pre_g, m_ffn2_w1, m_ffn2_w3, m_ffn2_w2, m_ffn2_post_g, v_ffn1_pre_g, v_ffn1_w1, v_ffn1_w3, v_ffn1_w2, v_ffn1_post_g, v_mix_pre_g, v_w_in, v_conv_w, v_conv_b, v_rg_a_w, v_rg_a_b, v_rg_x_w, v_rg_x_b, v_lru_lambda, v_w_lru_out, v_attn_sinks, v_rel_bias, v_w_attn_out, v_w_gate, v_b_gate, v_w_o, v_mix_post_g, v_ffn2_pre_g, v_ffn2_w1, v_ffn2_w3, v_ffn2_w2, v_ffn2_post_g):
    names = ["ffn1_pre_g", "ffn1_w1", "ffn1_w3", "ffn1_w2", "ffn1_post_g", "mix_pre_g", "w_in", "conv_w", "conv_b",
             "rg_a_w", "rg_a_b", "rg_x_w", "rg_x_b", "lru_lambda", "w_lru_out", "attn_sinks", "rel_bias", "w_attn_out",
             "w_gate", "b_gate", "w_o", "mix_post_g", "ffn2_pre_g", "ffn2_w1", "ffn2_w3", "ffn2_w2", "ffn2_post_g"]
    ws = dict(zip(names, (ffn1_pre_g, ffn1_w1, ffn1_w3, ffn1_w2, ffn1_post_g, mix_pre_g, w_in, conv_w, conv_b, rg_a_w,
                          rg_a_b, rg_x_w, rg_x_b, lru_lambda, w_lru_out, attn_sinks, rel_bias, w_attn_out, w_gate,
                          b_gate, w_o, mix_post_g, ffn2_pre_g, ffn2_w1, ffn2_w3, ffn2_w2, ffn2_post_g)))
    ms = dict(zip(names, (m_ffn1_pre_g, m_ffn1_w1, m_ffn1_w3, m_ffn1_w2, m_ffn1_post_g, m_mix_pre_g, m_w_in, m_conv_w,
                          m_conv_b, m_rg_a_w, m_rg_a_b, m_rg_x_w, m_rg_x_b, m_lru_lambda, m_w_lru_out, m_attn_sinks,
                          m_rel_bias, m_w_attn_out, m_w_gate, m_b_gate, m_w_o, m_mix_post_g, m_ffn2_pre_g, m_ffn2_w1,
                          m_ffn2_w3, m_ffn2_w2, m_ffn2_post_g)))
    vs = dict(zip(names, (v_ffn1_pre_g, v_ffn1_w1, v_ffn1_w3, v_ffn1_w2, v_ffn1_post_g, v_mix_pre_g, v_w_in, v_conv_w,
                          v_conv_b, v_rg_a_w, v_rg_a_b, v_rg_x_w, v_rg_x_b, v_lru_lambda, v_w_lru_out, v_attn_sinks,
                          v_rel_bias, v_w_attn_out, v_w_gate, v_b_gate, v_w_o, v_mix_post_g, v_ffn2_pre_g, v_ffn2_w1,
                          v_ffn2_w3, v_ffn2_w2, v_ffn2_post_g)))
    me = 4 * lax.axis_index("x") + 2 * lax.axis_index("y") + lax.axis_index("c")
    vec = lambda n: ws[n].reshape(1, -1)

    def shard2d(name):
        if name == "conv":
            row = lax.bitcast_convert_type(conv_w.reshape(CONV_WIDTH, LANES), BF16).reshape(1, D)
            return jnp.concatenate([row, jnp.zeros((LOC["conv"][2] - 1, D), BF16)], axis=0)
        a = ws[name].reshape(ws[name].shape[-2], ws[name].shape[-1])
        return (a.T if name in COL_SHARDED else a).astype(BF16)

    packs = {g: jnp.concatenate([shard2d(n) for n, _ in members], axis=0) for g, members in GROUPS}

    wg = {}
    _, ((wg["ffn1"],),) = _call(None, name="allgather_ffn1", grid=(), ins=[], in_specs=[], out_shape=[], out_specs=[],
                                carries=[_AllGatherTree(packs["ffn1"])])
    conv_rows = wg["ffn1"].reshape(NDEV, GROUP_ROWS["ffn1"], D)[:, LOC["conv"][1]]
    cw = jnp.transpose(lax.bitcast_convert_type(conv_rows.reshape(NDEV, CONV_WIDTH, LANES, 2), F32),
                       (1, 0, 2)).reshape(CONV_WIDTH, D)
    bmap = jnp.asarray(_bucket_map())
    bias = _bias_tile(rel_bias, bmap).reshape(N_KV, REP * QT, KW)
    sinks = attn_sinks.reshape(N_HEADS)
    wa_bd = _block_diag(rg_a_w.reshape(D // LRU_BLOCK, LRU_BLOCK, LRU_BLOCK)).astype(BF16)
    wx_bd = _block_diag(rg_x_w.reshape(D // LRU_BLOCK, LRU_BLOCK, LRU_BLOCK)).astype(BF16)
    lru_args = (cw, vec("conv_b"), wa_bd, wx_bd, vec("rg_a_b"), vec("rg_x_b"), vec("lru_lambda"))
    x2, tgt = x[0], loss_target[0]

    (h1, a1, b1, f1, nb1), ((wg["mixin"],),) = _ffn_fwd(
        x2, vec("ffn1_pre_g"), vec("ffn1_post_g"), wg, ("ffn1_w1", "ffn1_w3", "ffn1_w2"),
        carries=[_AllGather(packs["mixin"])])
    (q, kd, vd, xr, xg, gs, ub), ((wg["mixout"],),) = _mix_proj_fwd(
        h1, vec("mix_pre_g"), vec("b_gate"), wg, carries=[_AllGather(packs["mixout"])])
    (ya_in, hseq), ((wg["ffn2a"],),) = _lru_fwd(xr, xg, *lru_args, carries=[_AllGather(packs["ffn2a"])])
    (o, lse), ((wg["ffn2b"],),) = _attn_fwd(q, kd, vd, bias, sinks, carries=[_AllGather(packs["ffn2b"])])
    (h2, ya, yb, z), _ = _mix_out_fwd(ya_in, o, gs, h1, vec("mix_post_g"), wg)
    (_, a2, b2, f2, nb2, dy, loss_part), _ = _ffn_fwd(h2, vec("ffn2_pre_g"), vec("ffn2_post_g"), wg,
                                                     ("ffn2_w1", "ffn2_w3", "ffn2_w2"), target=tgt)

    gsm, parts = {}, {}
    rs = lambda *grads_: [_ReduceScatterSend(list(grads_))]
    ffr = FF // NDEV
    (dab, s_act, dfb, gsm["ffn2_post_g"]), _ = _ffn_bwd_a(dy, f2, a2, b2, vec("ffn2_post_g"), wg, "ffn2_w2")
    g_w2, _ = _dw(s_act, dfb, FF // 2, "dw_ffn2_w2")
    g_w13, ((parts["ffn2_w2"],),) = _dw(dab, nb2, D // 2, "dw_ffn2_w13", carries=rs(g_w2))
    (dh2, gsm["ffn2_pre_g"]), ((parts["ffn2_w1"],),) = _ffn_bwd_dx(
        dab, h2, dy, vec("ffn2_pre_g"), wg, "ffn2_w1", "ffn2_w3", carries=rs((g_w13, 0, ffr, ffr)))
    (dya_in, do, dgpre, dya, dyb, mg, dzb, gsm["mix_post_g"], gsm["b_gate"]), ((parts["ffn2_w3"],),) = _mix_out_bwd(
        dh2, z, ya, yb, gs, vec("mix_post_g"), wg, carries=rs((g_w13, FF, ffr, ffr)))
    g_wa, _ = _dw(ya_in, dya, D // 2, "dw_w_lru_out")
    g_wb, _ = _dw(o, dyb, D // 2, "dw_w_attn_out")
    g_wo, _ = _dw(mg, dzb, D // 2, "dw_w_o")
    g_wgate, _ = _dw(dgpre, ub, D // 2, "dw_w_gate")
    (dqkv, ds_acc, dsink), (mixout_parts,) = _attn_bwd(
        q, kd, vd, o, do, lse, bias, sinks, carries=rs(g_wa, g_wb, g_wo, g_wgate))
    parts["w_lru_out"], parts["w_attn_out"], parts["w_o"], parts["w_gate"] = mixout_parts
    (dxr, dxg, dvec, dwa, dwx), _ = _lru_bwd(dya_in, xr, xg, hseq, *lru_args)
    g_qkv, _ = _dw(dqkv, ub, D // 2, "dw_w_in_qkv")
    g_xr, _ = _dw(dxr, ub, D // 2, "dw_w_in_xr")
    g_xg, _ = _dw(dxg, ub, D // 2, "dw_w_in_xg")
    g_win = jnp.concatenate(
        [g_qkv[:D]] + [g_qkv[c0 + gi * LANES:c0 + gi * LANES + HEAD_DIM] for c0 in (K0, V0) for gi in range(N_KV)]
        + [g_xr, g_xg], axis=0)
    wir, wir_a = IN_W // NDEV, 336
    (dh1, gsm["mix_pre_g"]), ((win_a,),) = _mix_proj_bwd(
        dqkv, dxr, dxg, dgpre, h1, dh2, vec("mix_pre_g"), wg, carries=rs((g_win, 0, wir, wir_a)))
    gsm["conv_w"] = dvec[0:CONV_WIDTH]
    gsm["conv_b"], gsm["rg_a_b"], gsm["rg_x_b"], gsm["lru_lambda"] = dvec[4], dvec[5], dvec[6], dvec[7]
    gsm["rg_a_w"] = _block_diag_grad(dwa)
    gsm["rg_x_w"] = _block_diag_grad(dwx)
    gsm["attn_sinks"] = dsink[0, :N_HEADS]
    gsm["rel_bias"] = _bias_grad(ds_acc.reshape(N_HEADS, QT, KW), bmap)[:, :N_HEADS]
    late = ("ffn1_post_g", "ffn1_pre_g")
    early = tuple(n for n in SMALL if n not in late)
    early_pack, early_offs = _pack_small([gsm[n] for n in early])
    (dab, s_act, dfb, gsm["ffn1_post_g"]), ((win_b,), (early_all,)) = _ffn_bwd_a(
        dh1, f1, a1, b1, vec("ffn1_post_g"), wg, "ffn1_w2",
        carries=rs((g_win, wir_a, wir, wir - wir_a)) + [_AllGather(early_pack)])
    parts["w_in"] = jnp.concatenate([win_a, win_b], axis=1)
    g_w2, _ = _dw(s_act, dfb, FF // 2, "dw_ffn1_w2")
    g_w13, ((parts["ffn1_w2"],),) = _dw(dab, nb1, D // 2, "dw_ffn1_w13", carries=rs(g_w2))
    chip_w13 = _pair_sum(g_w13, [(0, ffr), (FF, ffr)])
    (grad_x, gsm["ffn1_pre_g"]), ((parts["ffn1_w1"], parts["ffn1_w3"]),) = _ffn_bwd_dx(
        dab, x2, dh1, vec("ffn1_pre_g"), wg, "ffn1_w1", "ffn1_w3",
        carries=[_ReduceScatterSend([(chip_w13, 0, ffr, ffr), (chip_w13, FF, ffr, ffr)], chip_sums=True)])
    late_pack, late_offs = _pack_small([gsm[n] for n in late] + [loss_part])

    grads, delta, new_m, new_v = {}, {}, {}, {}
    t_form = [n for n in COL_SHARDED if ws[n].shape[-1] % LANES]
    view = {n: (lambda a: a.reshape(a.shape[-2], a.shape[-1]).T) if n in t_form
            else (lambda a: a.reshape(a.shape[-2], a.shape[-1])) for n in BIG}
    unview = {n: (lambda a: a.T) if n in t_form else (lambda a: a) for n in BIG}

    group = [n for n in BIG if n != "w_gate"]
    sums, ((late_all,),) = _sum_ready([parts[n] for n in group], "sum_ready", [_AllGather(late_pack)],
                                      chip_sums=(group.index("ffn1_w1"), group.index("ffn1_w3")))
    res = _adamw_cols([(g, view[n](ws[n]), view[n](ms[n]), view[n](vs[n])) for n, g in zip(group, sums)], "adamw_big")
    for n, quad in zip(group, res):
        grads[n], delta[n], new_m[n], new_v[n] = (unview[n](a) for a in quad)
    (g_gate_t,) = _sum_parts([parts["w_gate"]])
    grads["w_gate"] = g_gate_t.T
    ((delta["w_gate"], new_m["w_gate"], new_v["w_gate"]),) = _adamw(
        [(grads["w_gate"], view["w_gate"](ws["w_gate"]), view["w_gate"](ms["w_gate"]), view["w_gate"](vs["w_gate"]))])
    for group, gathered, offs in ((early, early_all, early_offs), (late, late_all, late_offs)):
        total = _sum_small(gathered, gathered.shape[0] // NDEV)
        shapes = [(CONV_WIDTH, D) if n == "conv_w" else ws[n].shape for n in group]
        if group is late:
            shapes = shapes + [(1, LANES)]
        unpacked = _unpack_small(total, offs, shapes)
        if group is late:
            loss = unpacked.pop()[0, 0]
        for n, g in zip(group, unpacked):
            grads[n] = g
    grads["conv_w"] = lax.dynamic_slice(grads["conv_w"], (0, me * LANES), (CONV_WIDTH, LANES)).reshape(conv_w.shape)

    flat2d = lambda a: a.reshape(-1, a.shape[-1])
    res = _adamw_small([(flat2d(grads[n].reshape(ws[n].shape)), flat2d(ws[n]), flat2d(ms[n]), flat2d(vs[n]))
                        for n in SMALL])
    for n, (d_, m_, v_) in zip(SMALL, res):
        delta[n], new_m[n], new_v[n] = d_, m_, v_

    outs = [loss, grad_x.reshape(x.shape)]
    for src in (grads, delta, new_m, new_v):
        outs += [src[n].reshape(ws[n].shape) for n in names]
    return tuple(outs)
```

```python
import functools
import math
import operator

import numpy as np
import jax
import jax.numpy as jnp
from jax import lax
from jax.experimental import pallas as pl
from jax.experimental.pallas import tpu as pltpu

F32, BF16 = jnp.float32, jnp.bfloat16

NDEV = 8
D = 1024
FF = 2816
N_HEADS, N_KV, HEAD_DIM = 16, 4, 64
CHUNK, WINDOW = 64, 128
N_BUCKETS, MAX_DISTANCE = 32, 128
LRU_BLOCK = 64
CONV_WIDTH = 4
LRU_C = 8.0
RMS_EPS = 1e-6
NEG_INF = -1e30
LANES = 128
QT = 128
KW = QT + WINDOW
LRU_ROWS = 512
IN_W = D + 2 * N_KV * HEAD_DIM + 2 * D
INP_W = D + 2 * N_KV * LANES + 2 * D
VMEM_BIG = 58 * 2 ** 20

ADAM_LR, ADAM_B1, ADAM_B2, ADAM_EPS, ADAM_WD, ADAM_STEP = 0.001, 0.9, 0.999, 1e-08, 0.01, 10

GROUPS = (("ffn1", (("ffn1_w1", FF // NDEV), ("ffn1_w3", FF // NDEV), ("ffn1_w2", FF // NDEV), ("conv", 16))),
          ("mixin", (("w_in", IN_W // NDEV), ("w_gate", 2 * D // NDEV))),
          ("mixout", (("w_lru_out", D // NDEV), ("w_attn_out", D // NDEV), ("w_o", D // NDEV))),
          ("ffn2a", (("ffn2_w1", FF // NDEV), ("ffn2_w3", FF // NDEV))),
          ("ffn2b", (("ffn2_w2", FF // NDEV),)))
LOC, GROUP_ROWS = {}, {}
for _g, _members in GROUPS:
    _o = 0
    for _n, _r in _members:
        LOC[_n] = (_g, _o, _r)
        _o += _r
    GROUP_ROWS[_g] = _o
BIG = tuple(n for _, members in GROUPS for n, _ in members if n != "conv")
COL_SHARDED = ("ffn1_w1", "ffn1_w3", "w_in", "w_gate", "ffn2_w1", "ffn2_w3")

SMALL = ("ffn1_pre_g", "ffn1_post_g", "mix_pre_g", "conv_w", "conv_b", "rg_a_w", "rg_a_b", "rg_x_w", "rg_x_b",
         "lru_lambda", "attn_sinks", "rel_bias", "b_gate", "mix_post_g", "ffn2_pre_g", "ffn2_post_g")

MESH = pl.DeviceIdType.MESH
ANY = pl.BlockSpec(memory_space=pl.ANY)
DMA = pltpu.SemaphoreType.DMA


def _nn(a, b):
    return lax.dot_general(a, b, (((1,), (0,)), ((), ())), preferred_element_type=F32)


def _nt(a, b):
    return lax.dot_general(a, b, (((1,), (1,)), ((), ())), preferred_element_type=F32)


def _tn(a, b):
    return lax.dot_general(a, b, (((0,), (0,)), ((), ())), preferred_element_type=F32)


def _rms_fwd(x, g):
    r = lax.rsqrt(jnp.mean(x * x, axis=-1, keepdims=True) + RMS_EPS)
    xh = x * r
    return xh * g, xh, r


def _rms_bwd(dn, xh, r, g):
    dxh = dn * g
    dx = r * (dxh - xh * jnp.mean(dxh * xh, axis=-1, keepdims=True))
    return dx, jnp.sum(dn * xh, axis=0, keepdims=True)


def _row_spec(tm, c):
    return pl.BlockSpec((tm, c), lambda i: (i, 0))


def _const_spec(shape):
    nd = len(shape)
    return pl.BlockSpec(shape, lambda i: (0,) * nd)


class _AllGather:
    def __init__(self, shard):
        self.m, n = shard.shape
        self.ins = [shard]
        self.out_shape = [jax.ShapeDtypeStruct((NDEV * self.m, n), shard.dtype)]
        self.scratch = [DMA((7,)), DMA((7,)), DMA]

    def _copies(self, ins, outs, scr, all_of_them):
        x_ref, out_ref = ins[0], outs[0]
        send_sems, recv_sems, local_sem = scr
        x, y, c = lax.axis_index("x"), lax.axis_index("y"), lax.axis_index("c")
        me, sibling = (x, y, c), (x, y, 1 - c)
        chips = [(1 - x, y), (x, 1 - y), (1 - x, 1 - y)]
        m = self.m

        def rows(px, py, pc):
            return out_ref.at[pl.ds((4 * px + 2 * py + pc) * m, m), :]

        def copy(k, block, to, src=None):
            return pltpu.make_async_remote_copy(
                src_ref=rows(*block) if src is None else src, dst_ref=rows(*block),
                send_sem=send_sems.at[k], recv_sem=recv_sems.at[k], device_id=to, device_id_type=MESH)

        mine = pltpu.make_async_copy(x_ref, rows(*me), local_sem)
        first = [copy(0, me, sibling, src=x_ref)] + [copy(1 + j, me, (*chip, c), src=x_ref)
                                                     for j, chip in enumerate(chips)]
        if not all_of_them:
            return mine, first
        passed = [copy(4 + j, (*chip, c), sibling) for j, chip in enumerate(chips)]
        landed = [copy(1 + j, (*chip, c), me) for j, chip in enumerate(chips)]
        from_sibling = [copy(0, sibling, me)] + [copy(4 + j, (*chip, 1 - c), me) for j, chip in enumerate(chips)]
        return mine, first, passed, landed, from_sibling

    def start(self, ins, outs, scr):
        mine, first = self._copies(ins, outs, scr, False)
        mine.start()
        for cp in first:
            cp.start()

    def finish(self, ins, outs, scr):
        mine, first, passed, landed, from_sibling = self._copies(ins, outs, scr, True)
        for cp_in, cp_on in zip(landed, passed):
            cp_in.wait_recv()
            cp_on.start()
        for cp in from_sibling:
            cp.wait_recv()
        for cp in first + passed:
            cp.wait_send()
        mine.wait()


class _AllGatherTree:
    def __init__(self, shard):
        self.m, n = shard.shape
        self.half = self.m // 32 * 16
        self.ins = [shard]
        self.out_shape = [jax.ShapeDtypeStruct((NDEV * self.m, n), shard.dtype)]
        self.scratch = [DMA((13,)), DMA((13,)), DMA]

    def start(self, ins, outs, scr):
        x_ref, out_ref = ins[0], outs[0]
        send_sems, recv_sems, local_sem = scr
        x, y, c = lax.axis_index("x"), lax.axis_index("y"), lax.axis_index("c")
        m, half = self.m, self.half
        me, sibling, xn, yn, diag = (x, y, c), (x, y, 1 - c), (1 - x, y, c), (x, 1 - y, c), (1 - x, 1 - y, c)
        other = lambda dev: (dev[0], dev[1], 1 - c)

        def rows(dev, lo=0, n=m):
            return out_ref.at[pl.ds((4 * dev[0] + 2 * dev[1] + dev[2]) * m + lo, n), :]

        def copy(k, dev, to, lo=0, n=m, src=None):
            return pltpu.make_async_remote_copy(
                src_ref=rows(dev, lo, n) if src is None else src, dst_ref=rows(dev, lo, n),
                send_sem=send_sems.at[k], recv_sem=recv_sems.at[k], device_id=to, device_id_type=MESH)

        lo, hi = (0, half), (half, m - half)
        mine = pltpu.make_async_copy(x_ref, rows(me), local_sem)
        own = [copy(0, me, sibling, src=x_ref),
               copy(1, me, xn, *lo, src=x_ref.at[pl.ds(*lo)]), copy(2, me, yn, *hi, src=x_ref.at[pl.ds(*hi)]),
               copy(9, me, xn, *hi, src=x_ref.at[pl.ds(*hi)]), copy(10, me, yn, *lo, src=x_ref.at[pl.ds(*lo)])]
        mine.start()
        for cp in own:
            cp.start()
        steps = [(copy(1, xn, me, *lo), [copy(3, xn, yn, *lo), copy(5, xn, sibling, *lo)]),
                 (copy(2, yn, me, *hi), [copy(4, yn, xn, *hi), copy(6, yn, sibling, *hi)]),
                 (copy(9, xn, me, *hi), [copy(11, xn, sibling, *hi)]),
                 (copy(10, yn, me, *lo), [copy(12, yn, sibling, *lo)]),
                 (copy(3, diag, me, *lo), [copy(7, diag, sibling, *lo)]),
                 (copy(4, diag, me, *hi), [copy(8, diag, sibling, *hi)])]
        from_sibling = [copy(0, sibling, me), copy(5, other(xn), me, *lo), copy(11, other(xn), me, *hi),
                        copy(6, other(yn), me, *hi), copy(12, other(yn), me, *lo),
                        copy(7, other(diag), me, *lo), copy(8, other(diag), me, *hi)]
        passed = []
        for landing, onward in steps:
            landing.wait_recv()
            for cp in onward:
                cp.start()
            passed += onward
        for cp in from_sibling:
            cp.wait_recv()
        for cp in own + passed:
            cp.wait_send()
        mine.wait()

    def finish(self, ins, outs, scr):
        pass


class _ReduceScatterSend:
    def __init__(self, grads, chip_sums=False):
        grads = [g if isinstance(g, tuple) else (g, 0, g.shape[0] // NDEV, g.shape[0] // NDEV) for g in grads]
        self.peers = (2, 4, 6) if chip_sums else tuple(range(1, NDEV))
        self.nw = len(grads)
        self.base = [b for _, b, _, _ in grads]
        self.stride = [s for _, _, s, _ in grads]
        self.rows = [r for _, _, _, r in grads]
        self.ins = [g for g, _, _, _ in grads]
        self.out_shape = [jax.ShapeDtypeStruct((NDEV, r, g.shape[1]), g.dtype) for g, _, _, r in grads]
        self.scratch = [DMA((self.nw, NDEV - 1)), DMA((self.nw, NDEV - 1)), DMA((self.nw,))]

    def _copies(self, g_refs, r_refs, scr, want):
        send_sems, recv_sems, local_sems = scr
        x, y, c = lax.axis_index("x"), lax.axis_index("y"), lax.axis_index("c")
        me = 4 * x + 2 * y + c
        rows, base, stride = self.rows, self.base, self.stride
        out = []
        if want == "local":
            for w in range(self.nw):
                out.append(pltpu.make_async_copy(g_refs[w].at[pl.ds(base[w] + me * stride[w], rows[w])],
                                                 r_refs[w].at[me], local_sems.at[w]))
            return out
        for k in self.peers:
            px, py, pc = x ^ (k >> 2), y ^ ((k >> 1) & 1), c ^ (k & 1)
            peer = 4 * px + 2 * py + pc
            for w in range(self.nw):
                sems = dict(send_sem=send_sems.at[w, k - 1], recv_sem=recv_sems.at[w, k - 1],
                            device_id=(px, py, pc), device_id_type=MESH)
                if want == "send":
                    out.append(pltpu.make_async_remote_copy(
                        src_ref=g_refs[w].at[pl.ds(base[w] + peer * stride[w], rows[w])], dst_ref=r_refs[w].at[me],
                        **sems))
                else:
                    out.append(pltpu.make_async_remote_copy(
                        src_ref=g_refs[w].at[pl.ds(0, rows[w])], dst_ref=r_refs[w].at[peer], **sems))
        return out

    def start(self, ins, outs, scr):
        for cp in self._copies(ins, outs, scr, "local") + self._copies(ins, outs, scr, "send"):
            cp.start()

    def finish(self, ins, outs, scr):
        for cp in self._copies(ins, outs, scr, "recv"):
            cp.wait_recv()
        for cp in self._copies(ins, outs, scr, "send"):
            cp.wait_send()
        for cp in self._copies(ins, outs, scr, "local"):
            cp.wait()


def _call(body, *, name, grid, ins, in_specs, out_shape, out_specs, scratch=(), vmem=None, carries=()):
    n_in, n_out, n_scr = len(ins), len(out_shape), len(scratch)
    ng = len(grid)

    def split(refs):
        pos = [0]

        def take(k):
            part = refs[pos[0]:pos[0] + k]
            pos[0] += k
            return part

        i_refs = take(n_in)
        c_in = [take(len(c.ins)) for c in carries]
        o_refs = take(n_out)
        c_out = [take(len(c.out_shape)) for c in carries]
        s_refs = take(n_scr)
        c_scr = [take(len(c.scratch)) for c in carries]
        return i_refs, o_refs, s_refs, list(zip(carries, c_in, c_out, c_scr))

    def full(*refs):
        i_refs, o_refs, s_refs, cparts = split(refs)
        if ng == 0:
            for c, a, b, s in cparts:
                c.start(a, b, s)
            for c, a, b, s in cparts:
                c.finish(a, b, s)
            return
        ids = [pl.program_id(a) for a in range(ng)]
        if cparts:
            @pl.when(functools.reduce(operator.and_, [i == 0 for i in ids]))
            def _():
                for c, a, b, s in cparts:
                    c.start(a, b, s)

        body(*i_refs, *o_refs, *s_refs)
        if cparts:
            @pl.when(functools.reduce(operator.and_, [i == g - 1 for i, g in zip(ids, grid)]))
            def _():
                for c, a, b, s in cparts:
                    c.finish(a, b, s)

    all_ins = list(ins) + [a for c in carries for a in c.ins]
    all_in_specs = list(in_specs) + [ANY for c in carries for _ in c.ins]
    all_out_shape = list(out_shape) + [s for c in carries for s in c.out_shape]
    all_out_specs = list(out_specs) + [ANY for c in carries for _ in c.out_shape]
    all_scratch = list(scratch) + [s for c in carries for s in c.scratch]
    kwargs = dict(grid=grid) if ng else {}
    outs = pl.pallas_call(
        full, name=name, in_specs=all_in_specs, out_specs=all_out_specs, out_shape=all_out_shape,
        scratch_shapes=all_scratch,
        compiler_params=pltpu.CompilerParams(dimension_semantics=("arbitrary",) * ng if ng else None,
                                             vmem_limit_bytes=vmem),
        **kwargs)(*all_ins)
    outs = list(outs)
    res, pos = outs[:n_out], n_out
    carried = []
    for c in carries:
        carried.append(outs[pos:pos + len(c.out_shape)])
        pos += len(c.out_shape)
    return res, carried


def _pair_sum(g, items):
    rows = items[0][1]
    n = len(items) * (NDEV // 2)
    width = g.shape[1]

    def body(g_ref, out_ref, got_ref, mine_v, got_v, send_sems, recv_sems, ld_sems, st_sems):
        x, y, c = lax.axis_index("x"), lax.axis_index("y"), lax.axis_index("c")
        keep = [base + (2 * ch + c) * rows for base, _ in items for ch in range(NDEV // 2)]
        give = [base + (2 * ch + 1 - c) * rows for base, _ in items for ch in range(NDEV // 2)]

        def remote(j, r):
            return pltpu.make_async_remote_copy(
                src_ref=g_ref.at[pl.ds(r, rows)], dst_ref=got_ref.at[pl.ds(r, rows)], send_sem=send_sems.at[j],
                recv_sem=recv_sems.at[j], device_id=(x, y, 1 - c), device_id_type=MESH)

        sends = [remote(j, r) for j, r in enumerate(give)]
        for cp in sends:
            cp.start()
        loads = [pltpu.make_async_copy(g_ref.at[pl.ds(r, rows)], mine_v.at[j], ld_sems.at[0, j])
                 for j, r in enumerate(keep)]
        for cp in loads:
            cp.start()
        stores = []
        for j, r in enumerate(keep):
            remote(j, r).wait_recv()
            cp = pltpu.make_async_copy(got_ref.at[pl.ds(r, rows)], got_v.at[j], ld_sems.at[1, j])
            cp.start()
            loads[j].wait()
            cp.wait()
            mine_v[j] = (mine_v[j].astype(F32) + got_v[j].astype(F32)).astype(BF16)
            st = pltpu.make_async_copy(mine_v.at[j], out_ref.at[pl.ds(r, rows)], st_sems.at[j])
            st.start()
            stores.append(st)
        for cp in sends:
            cp.wait_send()
        for cp in stores:
            cp.wait()

    out, _ = pl.pallas_call(
        body, name="pair_sum", in_specs=[ANY], out_specs=[ANY, ANY],
        out_shape=[jax.ShapeDtypeStruct(g.shape, g.dtype), jax.ShapeDtypeStruct(g.shape, g.dtype)],
        scratch_shapes=[pltpu.VMEM((n, rows, width), g.dtype), pltpu.VMEM((n, rows, width), g.dtype),
                        DMA((n,)), DMA((n,)), DMA((2, n)), DMA((n,))],
        compiler_params=pltpu.CompilerParams(vmem_limit_bytes=40 * 2 ** 20))(g)
    return out


def _groups_of(names):
    out = []
    for n in names:
        if LOC[n][0] not in out:
            out.append(LOC[n][0])
    return out


def _weight_pieces(name):
    g, off, rows = LOC[name]
    return [(d * GROUP_ROWS[g] + off, d * rows, rows) for d in range(NDEV)]


def _win_pieces():
    kv = N_KV * HEAD_DIM
    pieces = [(0, 0, D)]
    for g0, d0 in ((D, D), (D + kv, D + N_KV * LANES)):
        for g in range(N_KV):
            for half in range(2):
                pieces.append((g0 + g * HEAD_DIM, d0 + g * LANES + half * HEAD_DIM, HEAD_DIM))
    pieces.append((D + 2 * kv, D + 2 * N_KV * LANES, D))
    pieces.append((D + 2 * kv + D, D + 2 * N_KV * LANES + D, D))
    grp, off, rows = LOC["w_in"]
    out = []
    for g0, d0, n in pieces:
        while n > 0:
            dev, loc = divmod(g0, rows)
            m = min(n, rows - loc)
            out.append((dev * GROUP_ROWS[grp] + off + loc, d0, m))
            g0, d0, n = g0 + m, d0 + m, n - m
    return out


def _start_loads(src_ref, dst_ref, pieces, sems, base):
    cps = []
    for j, (s, d, n) in enumerate(pieces):
        cp = pltpu.make_async_copy(src_ref.at[pl.ds(s, n)], dst_ref.at[pl.ds(d, n)], sems.at[base + j])
        cp.start()
        cps.append(cp)
    return cps


def _load_weights(wrefs, targets, sems):
    cps, base = [], 0
    for name, dst in targets:
        pieces = _win_pieces() if name == "w_in" else _weight_pieces(name)
        cps += _start_loads(wrefs[LOC[name][0]], dst, pieces, sems, base)
        base += len(pieces)
    for cp in cps:
        cp.wait()


def _n_pieces(names):
    return sum(len(_win_pieces()) if n == "w_in" else NDEV for n in names)


def _dw(lhs, rhs, chunk, name, carries=()):
    nq = 4
    if lhs.ndim == 3:
        nch, t_tok, chunk = lhs.shape
        c = nch * chunk
        tq = t_tok // nq
        lhs_specs = [pl.BlockSpec((None, tq, chunk), lambda i, k=k: (i, k, 0)) for k in range(nq)]
    else:
        t_tok, c = lhs.shape
        tq = t_tok // nq
        lhs_specs = [pl.BlockSpec((tq, chunk), lambda i, k=k: (k, i)) for k in range(nq)]

    def body(*refs):
        lhs_refs, (rhs_ref, out_ref, rhs_s, sems) = refs[:nq], refs[nq:]
        cps = [pltpu.make_async_copy(rhs_ref.at[pl.ds(k * tq, tq)], rhs_s.at[pl.ds(k * tq, tq)], sems.at[k])
               for k in range(nq)]
        first = pl.program_id(0) == 0

        @pl.when(first)
        def _():
            for cp in cps:
                cp.start()

        for o in range(0, chunk, D // 2):
            n = min(D // 2, chunk - o)
            acc = None
            for k in range(nq):
                if o == 0:
                    @pl.when(first)
                    def _():
                        cps[k].wait()

                part = _tn(lhs_refs[k][:, pl.ds(o, n)], rhs_s[pl.ds(k * tq, tq), :])
                acc = part if acc is None else acc + part
            out_ref[pl.ds(o, n), :] = acc.astype(BF16)

    (out,), carried = _call(
        body, name=name, grid=(c // chunk,), ins=[lhs] * nq + [rhs],
        in_specs=lhs_specs + [ANY],
        out_specs=[pl.BlockSpec((chunk, D), lambda i: (i, 0))],
        out_shape=[jax.ShapeDtypeStruct((c, D), BF16)],
        scratch=[pltpu.VMEM((t_tok, D), BF16), DMA((nq,))], vmem=VMEM_BIG, carries=carries)
    return out, carried


def _silu_parts(a):
    sig = jax.nn.sigmoid(a)
    return sig, a * sig


FC = 256


def _ffn_fwd(h, gpre, gpost, wg, names, target=None, tm=512, carries=()):
    t_tok = h.shape[0]
    nt = t_tok // tm
    with_loss = target is not None
    groups = _groups_of(names)

    def body(*refs):
        refs = list(refs)
        h_ref, gpre_ref, gpost_ref = refs[:3]
        del refs[:3]
        tgt_ref = refs.pop(0) if with_loss else None
        wrefs = dict(zip(groups, refs[:len(groups)]))
        del refs[:len(groups)]
        hout_ref, a_ref, b_ref, f_ref, nb_ref = refs[:5]
        del refs[:5]
        if with_loss:
            dy_ref, loss_ref = refs[:2]
            del refs[:2]
        w1_s, w3_s, w2_s, sems = refs
        i = pl.program_id(0)

        @pl.when(i == 0)
        def _():
            _load_weights(wrefs, list(zip(names, (w1_s, w3_s, w2_s))), sems)
            if with_loss:
                loss_ref[...] = jnp.zeros_like(loss_ref)

        x = h_ref[...]
        n, _, _ = _rms_fwd(x, gpre_ref[...])
        nb = n.astype(BF16)
        nb_ref[...] = nb
        f = jnp.zeros((tm, D), F32)
        for c0 in range(0, FF, FC):
            a = _nt(nb, w1_s[pl.ds(c0, FC), :])
            b = _nt(nb, w3_s[pl.ds(c0, FC), :])
            _, sl = _silu_parts(a)
            a_ref[:, pl.ds(c0, FC)] = a.astype(BF16)
            b_ref[:, pl.ds(c0, FC)] = b.astype(BF16)
            f = f + _nn((sl * b).astype(BF16), w2_s[pl.ds(c0, FC), :])
        f_ref[...] = f
        fn, _, _ = _rms_fwd(f, gpost_ref[...])
        y = x + 0.5 * fn
        hout_ref[...] = y
        if with_loss:
            err = y - tgt_ref[...]
            dy_ref[...] = err * (1.0 / D)
            loss_ref[...] += jnp.sum(jnp.sum(err * err, axis=-1, keepdims=True), axis=0, keepdims=True) * (0.5 / D)

    ins = [h, gpre, gpost] + ([target] if with_loss else []) + [wg[g] for g in groups]
    in_specs = [_row_spec(tm, D), _const_spec((1, D)), _const_spec((1, D))]
    in_specs += ([_row_spec(tm, D)] if with_loss else []) + [ANY] * len(groups)
    out_shape = [jax.ShapeDtypeStruct((t_tok, D), F32), jax.ShapeDtypeStruct((t_tok, FF), BF16),
                 jax.ShapeDtypeStruct((t_tok, FF), BF16), jax.ShapeDtypeStruct((t_tok, D), F32),
                 jax.ShapeDtypeStruct((t_tok, D), BF16)]
    out_specs = [_row_spec(tm, D), _row_spec(tm, FF), _row_spec(tm, FF), _row_spec(tm, D), _row_spec(tm, D)]
    if with_loss:
        out_shape += [jax.ShapeDtypeStruct((t_tok, D), F32), jax.ShapeDtypeStruct((1, LANES), F32)]
        out_specs += [_row_spec(tm, D), _const_spec((1, LANES))]
    return _call(body, name="ffn_fwd_" + names[0][:4], grid=(nt,), ins=ins, in_specs=in_specs,
                 out_shape=out_shape, out_specs=out_specs,
                 scratch=[pltpu.VMEM((FF, D), BF16)] * 3 + [DMA((3 * NDEV,))], vmem=VMEM_BIG, carries=carries)


FH = FF // 2
HALF_PIECES = ((0, 256), (256, 256), (512, 256), (768, 256), (1024, 256), (1280, 128))


def _ffn_bwd_a(dh, f, a, b, gpost, wg, name_w2, tm=512, carries=()):
    t_tok = dh.shape[0]
    nt = t_tok // tm
    groups = _groups_of([name_w2])

    def body(dh_ref, f_ref, a_ref, b_ref, gpost_ref, wg_ref, dab_ref, s_ref, df_ref, dgp_ref, w2_s, sems):
        i = pl.program_id(0)

        @pl.when(i == 0)
        def _():
            _load_weights({groups[0]: wg_ref}, [(name_w2, w2_s)], sems)
            dgp_ref[...] = jnp.zeros_like(dgp_ref)

        fv = f_ref[...]
        _, fh, r = _rms_fwd(fv, gpost_ref[...])
        df, dg = _rms_bwd(0.5 * dh_ref[...], fh, r, gpost_ref[...])
        dgp_ref[...] += dg
        dfb = df.astype(BF16)
        df_ref[...] = dfb
        for half in range(2):
            for o, n in HALF_PIECES:
                c0 = half * FH + o
                ds = _nt(dfb, w2_s[pl.ds(c0, n), :])
                av = a_ref[:, pl.ds(c0, n)].astype(F32)
                bv = b_ref[:, pl.ds(c0, n)].astype(F32)
                sig, sl = _silu_parts(av)
                dab_ref[half, :, pl.ds(o, n)] = (ds * bv * (sig * (1.0 + av * (1.0 - sig)))).astype(BF16)
                dab_ref[2 + half, :, pl.ds(o, n)] = (ds * sl).astype(BF16)
                s_ref[half, :, pl.ds(o, n)] = (sl * bv).astype(BF16)

    return _call(
        body, name="ffn_bwd_a_" + name_w2[:4], grid=(nt,), ins=[dh, f, a, b, gpost, wg[groups[0]]],
        in_specs=[_row_spec(tm, D), _row_spec(tm, D), _row_spec(tm, FF), _row_spec(tm, FF), _const_spec((1, D)), ANY],
        out_specs=[pl.BlockSpec((4, tm, FH), lambda i: (0, i, 0)), pl.BlockSpec((2, tm, FH), lambda i: (0, i, 0)),
                   _row_spec(tm, D), _const_spec((1, D))],
        out_shape=[jax.ShapeDtypeStruct((4, t_tok, FH), BF16), jax.ShapeDtypeStruct((2, t_tok, FH), BF16),
                   jax.ShapeDtypeStruct((t_tok, D), BF16), jax.ShapeDtypeStruct((1, D), F32)],
        scratch=[pltpu.VMEM((FF, D), BF16), DMA((NDEV,))],
        vmem=VMEM_BIG, carries=carries)


def _ffn_bwd_dx(dab, h, dh, gpre, wg, name_w1, name_w3, tm=512, carries=()):
    t_tok = dh.shape[0]
    nt = t_tok // tm
    groups = _groups_of([name_w1, name_w3])

    def body(*refs):
        dab_refs, (h_ref, dh_ref, gpre_ref) = refs[:4], refs[4:7]
        wrefs = dict(zip(groups, refs[7:7 + len(groups)]))
        dhin_ref, dgp_ref, w13_s, sems = refs[7 + len(groups):]
        i = pl.program_id(0)

        @pl.when(i == 0)
        def _():
            _load_weights(wrefs, [(name_w1, w13_s.at[pl.ds(0, FF)]), (name_w3, w13_s.at[pl.ds(FF, FF)])], sems)
            dgp_ref[...] = jnp.zeros_like(dgp_ref)

        g = gpre_ref[...]
        _, xh, r = _rms_fwd(h_ref[...], g)
        dn = _nn(dab_refs[0][...], w13_s[pl.ds(0, FH), :])
        for k in range(1, 4):
            dn = dn + _nn(dab_refs[k][...], w13_s[pl.ds(k * FH, FH), :])
        dx, dg = _rms_bwd(dn, xh, r, g)
        dgp_ref[...] += dg
        dhin_ref[...] = dh_ref[...] + dx

    return _call(
        body, name="ffn_bwd_dx_" + name_w1[:4], grid=(nt,), ins=[dab] * 4 + [h, dh, gpre] + [wg[g] for g in groups],
        in_specs=[pl.BlockSpec((None, tm, FH), lambda i, k=k: (k, i, 0)) for k in range(4)]
        + [_row_spec(tm, D), _row_spec(tm, D), _const_spec((1, D))] + [ANY] * len(groups),
        out_specs=[_row_spec(tm, D), _const_spec((1, D))],
        out_shape=[jax.ShapeDtypeStruct((t_tok, D), F32), jax.ShapeDtypeStruct((1, D), F32)],
        scratch=[pltpu.VMEM((2 * FF, D), BF16), DMA((2 * NDEV,))],
        vmem=VMEM_BIG, carries=carries)


Q0, K0, V0, XR0, XG0 = 0, D, D + N_KV * LANES, D + 2 * N_KV * LANES, 2 * D + 2 * N_KV * LANES


def _mix_proj_fwd(h, g, bgate, wg, tm=512, carries=()):
    t_tok = h.shape[0]
    nt = t_tok // tm
    names = ("w_in", "w_gate")
    groups = _groups_of(names)

    def body(h_ref, g_ref, bg_ref, wg_ref, q_ref, k_ref, v_ref, xr_ref, xg_ref, gs_ref, ub_ref, win_s, wgt_s, sems):
        i = pl.program_id(0)

        @pl.when(i == 0)
        def _():
            _load_weights({groups[0]: wg_ref}, [("w_in", win_s), ("w_gate", wgt_s)], sems)

        n, _, _ = _rms_fwd(h_ref[...], g_ref[...])
        nb = n.astype(BF16)
        ub_ref[...] = nb
        q_ref[...] = _nt(nb, win_s[pl.ds(Q0, D), :]).astype(BF16)
        k_ref[...] = _nt(nb, win_s[pl.ds(K0, N_KV * LANES), :]).astype(BF16)
        v_ref[...] = _nt(nb, win_s[pl.ds(V0, N_KV * LANES), :]).astype(BF16)
        xr_ref[...] = _nt(nb, win_s[pl.ds(XR0, D), :])
        xg_ref[...] = _nt(nb, win_s[pl.ds(XG0, D), :])
        gs_ref[...] = jax.nn.sigmoid(_nt(nb, wgt_s[...]) + bg_ref[...]).astype(BF16)

    kvw = N_KV * LANES
    return _call(
        body, name="mix_proj_fwd", grid=(nt,), ins=[h, g, bgate, wg[groups[0]]],
        in_specs=[_row_spec(tm, D), _const_spec((1, D)), _const_spec((1, 2 * D)), ANY],
        out_specs=[_row_spec(tm, D), _row_spec(tm, kvw), _row_spec(tm, kvw), _row_spec(tm, D), _row_spec(tm, D),
                   _row_spec(tm, 2 * D), _row_spec(tm, D)],
        out_shape=[jax.ShapeDtypeStruct((t_tok, D), BF16), jax.ShapeDtypeStruct((t_tok, kvw), BF16),
                   jax.ShapeDtypeStruct((t_tok, kvw), BF16), jax.ShapeDtypeStruct((t_tok, D), F32),
                   jax.ShapeDtypeStruct((t_tok, D), F32), jax.ShapeDtypeStruct((t_tok, 2 * D), BF16),
                   jax.ShapeDtypeStruct((t_tok, D), BF16)],
        scratch=[pltpu.VMEM((INP_W, D), BF16), pltpu.VMEM((2 * D, D), BF16), DMA((_n_pieces(names),))],
        vmem=VMEM_BIG, carries=carries)


def _mix_proj_bwd(dqkv, dxr, dxg, dgpre, h, dh, g, wg, tm=512, carries=()):
    t_tok = h.shape[0]
    nt = t_tok // tm
    names = ("w_in", "w_gate")
    groups = _groups_of(names)

    def body(dqkv_ref, dxr_ref, dxg_ref, dgp_ref, h_ref, dh_ref, g_ref, wg_ref, dhin_ref, dg_ref, win_s, wgt_s, sems):
        i = pl.program_id(0)

        @pl.when(i == 0)
        def _():
            _load_weights({groups[0]: wg_ref}, [("w_in", win_s), ("w_gate", wgt_s)], sems)
            dg_ref[...] = jnp.zeros_like(dg_ref)

        gv = g_ref[...]
        _, xh, r = _rms_fwd(h_ref[...], gv)
        du = _nn(dgp_ref[...], wgt_s[...])
        du = du + _nn(dqkv_ref[...], win_s[pl.ds(Q0, XR0), :])
        du = du + _nn(dxr_ref[...], win_s[pl.ds(XR0, D), :])
        du = du + _nn(dxg_ref[...], win_s[pl.ds(XG0, D), :])
        dx, dg = _rms_bwd(du, xh, r, gv)
        dg_ref[...] += dg
        dhin_ref[...] = dh_ref[...] + dx

    return _call(
        body, name="mix_proj_bwd", grid=(nt,), ins=[dqkv, dxr, dxg, dgpre, h, dh, g, wg[groups[0]]],
        in_specs=[_row_spec(tm, XR0), _row_spec(tm, D), _row_spec(tm, D), _row_spec(tm, 2 * D), _row_spec(tm, D),
                  _row_spec(tm, D), _const_spec((1, D)), ANY],
        out_specs=[_row_spec(tm, D), _const_spec((1, D))],
        out_shape=[jax.ShapeDtypeStruct((t_tok, D), F32), jax.ShapeDtypeStruct((1, D), F32)],
        scratch=[pltpu.VMEM((INP_W, D), BF16), pltpu.VMEM((2 * D, D), BF16), DMA((_n_pieces(names),))],
        vmem=VMEM_BIG, carries=carries)


def _shift_down(x, d, fill):
    row = lax.broadcasted_iota(jnp.int32, x.shape, 0)
    return jnp.where(row >= d, pltpu.roll(x, d, 0), fill)


def _shift_up(x, d, fill):
    rows = x.shape[0]
    row = lax.broadcasted_iota(jnp.int32, x.shape, 0)
    return jnp.where(row < rows - d, pltpu.roll(x, rows - d, 0), fill)


def _scan_rows(a, b, reverse):
    rows = a.shape[0]
    d = 1
    while d < rows:
        if d < 8:
            shift = _shift_up if reverse else _shift_down
            b = a * shift(b, d, 0.0) + b
            a = a * shift(a, d, 1.0)
        elif reverse:
            b = jnp.concatenate([a[:rows - d] * b[d:] + b[:rows - d], b[rows - d:]], axis=0)
            a = jnp.concatenate([a[:rows - d] * a[d:], a[rows - d:]], axis=0)
        else:
            b = jnp.concatenate([b[:d], a[d:] * b[:rows - d] + b[d:]], axis=0)
            a = jnp.concatenate([a[:d], a[d:] * a[:rows - d]], axis=0)
        d *= 2
    return a, b


def _softplus(x):
    return jnp.maximum(x, 0.0) + jnp.log(1.0 + jnp.exp(-jnp.abs(x)))


_GELU_C = math.sqrt(2.0 / math.pi)


def _gelu_parts(x):
    th = jnp.tanh(_GELU_C * (x + 0.044715 * x * x * x))
    val = 0.5 * x * (1.0 + th)
    grad = 0.5 * (1.0 + th) + 0.5 * x * (1.0 - th * th) * _GELU_C * (1.0 + 3.0 * 0.044715 * x * x)
    return val, grad


def _lru_pre(x, halo, cw_ref, cb_ref, wa_ref, wx_ref, ba_ref, bx_ref, lam_ref):
    ext = jnp.concatenate([halo, x], axis=0)
    shifted = [x] + [pltpu.roll(ext, k, 0)[8:] for k in (1, 2, 3)]
    xc = cb_ref[...] + cw_ref[pl.ds(CONV_WIDTH - 1, 1), :] * x
    for k in (1, 2, 3):
        xc = xc + cw_ref[pl.ds(CONV_WIDTH - 1 - k, 1), :] * shifted[k]
    xcb = xc.astype(BF16)
    r = jax.nn.sigmoid(_nn(xcb, wa_ref[...]) + ba_ref[...])
    ig = jax.nn.sigmoid(_nn(xcb, wx_ref[...]) + bx_ref[...])
    sp = _softplus(-lam_ref[...])
    log_a = -LRU_C * r * sp
    a = jnp.exp(log_a)
    th = jnp.tanh(log_a)
    mult = jnp.sqrt(-2.0 * th / (1.0 - th))
    return shifted, xc, xcb, r, ig, sp, a, mult


def _lru_specs(nt, reverse):
    def tt(t):
        return nt - 1 - t if reverse else t
    tile = pl.BlockSpec((LRU_ROWS, LANES), lambda cb, t: (tt(t), cb))
    halo = pl.BlockSpec((8, LANES), lambda cb, t: (jnp.maximum(tt(t) * (LRU_ROWS // 8) - 1, 0), cb))
    vec = pl.BlockSpec((1, LANES), lambda cb, t: (0, cb))
    cw = pl.BlockSpec((CONV_WIDTH, LANES), lambda cb, t: (0, cb))
    mat = pl.BlockSpec((None, LANES, LANES), lambda cb, t: (cb, 0, 0))
    return tile, halo, vec, cw, mat


def _lru_fwd(xr, xg, cw, cb, wa, wx, ba, bx, lam, carries=()):
    t_tok = xr.shape[0]
    nt = t_tok // LRU_ROWS
    rows = LRU_ROWS

    def body(xr_ref, xg_ref, cw_ref, cb_ref, wa_ref, wx_ref, ba_ref, bx_ref, lam_ref, y_ref, h_ref, tail_s, hc_s):
        t = pl.program_id(1)

        @pl.when(t == 0)
        def _():
            tail_s[...] = jnp.zeros_like(tail_s)
            hc_s[...] = jnp.zeros_like(hc_s)

        x = xr_ref[...]
        _, xc, _, _, ig, _, a, mult = _lru_pre(x, tail_s[...], cw_ref, cb_ref, wa_ref, wx_ref, ba_ref, bx_ref, lam_ref)
        tail_s[...] = xr_ref[pl.ds(rows - 8, 8), :]
        acc_a, acc_b = _scan_rows(a, mult * (ig * xc), False)
        hv = acc_b + acc_a * hc_s[...]
        h_ref[...] = hv
        hc_s[...] = h_ref[pl.ds(rows - 1, 1), :]
        gl, _ = _gelu_parts(xg_ref[...])
        y_ref[...] = (hv * gl).astype(BF16)

    tile, _, vec, cws, mat = _lru_specs(nt, False)
    return _call(
        body, name="lru_fwd", grid=(D // LANES, nt), ins=[xr, xg, cw, cb, wa, wx, ba, bx, lam],
        in_specs=[tile, tile, cws, vec, mat, mat, vec, vec, vec],
        out_specs=[tile, tile],
        out_shape=[jax.ShapeDtypeStruct((t_tok, D), BF16), jax.ShapeDtypeStruct((t_tok, D), F32)],
        scratch=[pltpu.VMEM((8, LANES), F32), pltpu.VMEM((1, LANES), F32)], carries=carries)


def _lru_bwd(dy, xr, xg, hseq, cw, cb, wa, wx, ba, bx, lam, carries=()):
    t_tok = xr.shape[0]
    nt = t_tok // LRU_ROWS
    rows = LRU_ROWS

    def body(dy_ref, xr_ref, xrh_ref, xg_ref, h_ref, hh_ref, cw_ref, cb_ref, wa_ref, wx_ref, ba_ref, bx_ref, lam_ref,
             dxr_ref, dxg_ref, dvec_ref, dwa_ref, dwx_ref, gcar_s, acar_s, head_s, tmp_s):
        t = pl.program_id(1)
        first_tile = t == nt - 1

        @pl.when(t == 0)
        def _():
            gcar_s[...] = jnp.zeros_like(gcar_s)
            acar_s[...] = jnp.zeros_like(acar_s)
            head_s[...] = jnp.zeros_like(head_s)
            dvec_ref[...] = jnp.zeros_like(dvec_ref)
            dwa_ref[...] = jnp.zeros_like(dwa_ref)
            dwx_ref[...] = jnp.zeros_like(dwx_ref)

        x = xr_ref[...]
        halo = jnp.where(first_tile, 0.0, xrh_ref[...])
        shifted, xc, xcb, r, ig, sp, a, mult = _lru_pre(x, halo, cw_ref, cb_ref, wa_ref, wx_ref, ba_ref, bx_ref, lam_ref)
        hv = h_ref[...]
        dyv = dy_ref[...]
        gl, glg = _gelu_parts(xg_ref[...])
        dxg_ref[...] = (dyv * hv * glg).astype(BF16)
        acc_a, acc_b = _scan_rows(_shift_up(a, 1, acar_s[...]), dyv * gl, True)
        g = acc_b + acc_a * gcar_s[...]
        hhalo = jnp.where(first_tile, 0.0, hh_ref[...])
        hprev = pltpu.roll(jnp.concatenate([hhalo, hv], axis=0), 1, 0)[8:]
        dmult = g * ig * xc
        dlog_a = a * (g * hprev) - dmult * a * a / mult
        dig = g * mult * xc
        dxc = g * mult * ig
        dzr = (dlog_a * (-LRU_C * sp)) * r * (1.0 - r)
        dzx = dig * ig * (1.0 - ig)
        dzrb, dzxb = dzr.astype(BF16), dzx.astype(BF16)
        dxc = dxc + _nt(dzrb, wa_ref[...]) + _nt(dzxb, wx_ref[...])
        dwa_ref[...] += _tn(xcb, dzrb)
        dwx_ref[...] += _tn(xcb, dzxb)
        dsp = jnp.sum(dlog_a * (-LRU_C * r), axis=0, keepdims=True)
        dlam = dsp * (-jax.nn.sigmoid(-lam_ref[...]))
        vrow = lax.broadcasted_iota(jnp.int32, (8, LANES), 0)
        upd = jnp.where(vrow == 4, jnp.sum(dxc, axis=0, keepdims=True), 0.0)
        upd = jnp.where(vrow == 5, jnp.sum(dzr, axis=0, keepdims=True), upd)
        upd = jnp.where(vrow == 6, jnp.sum(dzx, axis=0, keepdims=True), upd)
        upd = jnp.where(vrow == 7, dlam, upd)
        for k in range(CONV_WIDTH):
            upd = jnp.where(vrow == CONV_WIDTH - 1 - k, jnp.sum(dxc * shifted[k], axis=0, keepdims=True), upd)
        dvec_ref[...] += upd
        ext = jnp.concatenate([dxc, head_s[...]], axis=0)
        dxr = cw_ref[pl.ds(CONV_WIDTH - 1, 1), :] * dxc
        for k in (1, 2, 3):
            dxr = dxr + cw_ref[pl.ds(CONV_WIDTH - 1 - k, 1), :] * pltpu.roll(ext, rows + 8 - k, 0)[:rows]
        dxr_ref[...] = dxr.astype(BF16)
        tmp_s[...] = g
        gcar_s[...] = tmp_s[pl.ds(0, 1), :]
        tmp_s[...] = a
        acar_s[...] = tmp_s[pl.ds(0, 1), :]
        tmp_s[...] = dxc
        head_s[...] = tmp_s[pl.ds(0, 8), :]

    tile, halo, vec, cws, mat = _lru_specs(nt, True)
    return _call(
        body, name="lru_bwd", grid=(D // LANES, nt), ins=[dy, xr, xr, xg, hseq, hseq, cw, cb, wa, wx, ba, bx, lam],
        in_specs=[tile, tile, halo, tile, tile, halo, cws, vec, mat, mat, vec, vec, vec],
        out_specs=[tile, tile, pl.BlockSpec((8, LANES), lambda cb, t: (0, cb)), mat, mat],
        out_shape=[jax.ShapeDtypeStruct((t_tok, D), BF16), jax.ShapeDtypeStruct((t_tok, D), BF16),
                   jax.ShapeDtypeStruct((8, D), F32), jax.ShapeDtypeStruct((D // LANES, LANES, LANES), F32),
                   jax.ShapeDtypeStruct((D // LANES, LANES, LANES), F32)],
        scratch=[pltpu.VMEM((1, LANES), F32), pltpu.VMEM((1, LANES), F32), pltpu.VMEM((8, LANES), F32),
                 pltpu.VMEM((rows, LANES), F32)], carries=carries)


def _t5_bucket_np(rel):
    nb = N_BUCKETS // 2
    max_exact = nb // 2
    ret = np.where(rel > 0, nb, 0)
    n = np.abs(rel)
    nf = np.maximum(n, 1).astype(np.float32)
    large = max_exact + (np.log(nf / np.float32(max_exact)) / np.float32(math.log(MAX_DISTANCE / max_exact))
                         * np.float32(nb - max_exact)).astype(np.int32)
    large = np.minimum(large, nb - 1)
    return ret + np.where(n < max_exact, n, large)


def _bucket_map():
    r = np.arange(QT)[:, None]
    c = np.arange(KW)[None, :]
    j = c - (r // CHUNK) * CHUNK
    band = (j >= 0) & (j < WINDOW + CHUNK)
    return np.where(band, _t5_bucket_np(c - r - WINDOW), -1).astype(np.int32)


def _attn_specs(nt, reverse):
    def tt(i):
        return nt - 1 - i if reverse else i
    kvw = N_KV * LANES
    qs = pl.BlockSpec((QT, D), lambda i: (tt(i), 0))
    cur = pl.BlockSpec((QT, kvw), lambda i: (tt(i), 0))
    prev = pl.BlockSpec((WINDOW, kvw), lambda i: (jnp.maximum(tt(i) * (QT // WINDOW) - 1, 0), 0))
    lse = pl.BlockSpec((QT, LANES), lambda i: (tt(i), 0))
    return qs, cur, prev, lse


REP = N_HEADS // N_KV
SCALE = HEAD_DIM ** -0.5


def _stack_heads(x_ref, g, lo, scale=None):
    parts = []
    for hl in range(REP):
        xs = x_ref[:, pl.ds((2 * g + hl // 2) * LANES, LANES)]
        xs = jnp.where(lo if hl % 2 == 0 else jnp.logical_not(lo), xs, jnp.zeros_like(xs))
        parts.append(xs if scale is None else xs * jnp.asarray(scale, xs.dtype))
    return jnp.concatenate(parts, axis=0)


def _stack_sinks(sink_ref, g, srow):
    sk = jnp.full(srow.shape, sink_ref[REP * g + REP - 1], F32)
    for hl in range(REP - 2, -1, -1):
        sk = jnp.where(srow < (hl + 1) * QT, sink_ref[REP * g + hl], sk)
    return sk


def _attn_fwd(q, kd, vd, bias, sinks, carries=()):
    t_tok = q.shape[0]
    nt = t_tok // QT

    def body(q_ref, kp_ref, kc_ref, vp_ref, vc_ref, bias_ref, sink_ref, o_ref, lse_ref):
        i = pl.program_id(0)
        col = lax.broadcasted_iota(jnp.int32, (1, KW), 1)
        first = jnp.where((i == 0) & (col < WINDOW), NEG_INF, 0.0)
        lane = lax.broadcasted_iota(jnp.int32, (QT, LANES), 1)
        lo = lane < HEAD_DIM
        srow = lax.broadcasted_iota(jnp.int32, (REP * QT, 1), 0)
        lse_t = jnp.zeros((QT, LANES), F32)
        for g in range(N_KV):
            kwin = jnp.concatenate([kp_ref[:, pl.ds(g * LANES, LANES)], kc_ref[:, pl.ds(g * LANES, LANES)]], axis=0)
            vwin = jnp.concatenate([vp_ref[:, pl.ds(g * LANES, LANES)], vc_ref[:, pl.ds(g * LANES, LANES)]], axis=0)
            qst = _stack_heads(q_ref, g, lo, SCALE)
            s = _nt(qst, kwin) + (bias_ref[g] + first)
            sk = _stack_sinks(sink_ref, g, srow)
            m = jnp.maximum(jnp.max(s, axis=-1, keepdims=True), sk)
            e = jnp.exp(s - m)
            l = jnp.sum(e, axis=-1, keepdims=True) + jnp.exp(sk - m)
            p = e / l
            ost = _nn(p.astype(BF16), vwin)
            lse_s = m + jnp.log(l)
            for sl in range(2):
                o_ref[:, pl.ds((2 * g + sl) * LANES, LANES)] = jnp.where(
                    lo, ost[2 * sl * QT:(2 * sl + 1) * QT], ost[(2 * sl + 1) * QT:(2 * sl + 2) * QT]).astype(BF16)
            for hl in range(REP):
                lse_t = jnp.where(lane == REP * g + hl, lse_s[hl * QT:(hl + 1) * QT], lse_t)
        lse_ref[...] = lse_t

    qs, cur, prev, lse = _attn_specs(nt, False)
    return _call(
        body, name="attn_fwd", grid=(nt,), ins=[q, kd, kd, vd, vd, bias, sinks],
        in_specs=[qs, prev, cur, prev, cur, _const_spec((N_KV, REP * QT, KW)), pl.BlockSpec(memory_space=pltpu.SMEM)],
        out_specs=[qs, lse],
        out_shape=[jax.ShapeDtypeStruct((t_tok, D), BF16), jax.ShapeDtypeStruct((t_tok, LANES), F32)],
        vmem=48 * 2 ** 20, carries=carries)


def _attn_bwd(q, kd, vd, o, do, lse, bias, sinks, carries=()):
    t_tok = q.shape[0]
    nt = t_tok // QT
    kvw = N_KV * LANES

    def body(q_ref, kp_ref, kc_ref, vp_ref, vc_ref, o_ref, do_ref, lse_ref, bias_ref, sink_ref,
             dqkv_ref, ds_ref, dsink_ref, kcar_s, vcar_s):
        i = pl.program_id(0)
        tile = nt - 1 - i

        @pl.when(i == 0)
        def _():
            kcar_s[...] = jnp.zeros_like(kcar_s)
            vcar_s[...] = jnp.zeros_like(vcar_s)
            ds_ref[...] = jnp.zeros_like(ds_ref)
            dsink_ref[...] = jnp.zeros_like(dsink_ref)

        col = lax.broadcasted_iota(jnp.int32, (1, KW), 1)
        first = jnp.where((tile == 0) & (col < WINDOW), NEG_INF, 0.0)
        lane = lax.broadcasted_iota(jnp.int32, (QT, LANES), 1)
        lo = lane < HEAD_DIM
        lane_k = lax.broadcasted_iota(jnp.int32, (KW, LANES), 1)
        lane_1 = lax.broadcasted_iota(jnp.int32, (1, LANES), 1)
        srow = lax.broadcasted_iota(jnp.int32, (REP * QT, 1), 0)
        lse_t = lse_ref[...]
        dsink = jnp.zeros((1, LANES), F32)
        for g in range(N_KV):
            kwin = jnp.concatenate([kp_ref[:, pl.ds(g * LANES, LANES)], kc_ref[:, pl.ds(g * LANES, LANES)]], axis=0)
            vwin = jnp.concatenate([vp_ref[:, pl.ds(g * LANES, LANES)], vc_ref[:, pl.ds(g * LANES, LANES)]], axis=0)
            qst = _stack_heads(q_ref, g, lo, SCALE)
            dost = _stack_heads(do_ref, g, lo)
            od = [do_ref[:, pl.ds((2 * g + sl) * LANES, LANES)].astype(F32)
                  * o_ref[:, pl.ds((2 * g + sl) * LANES, LANES)].astype(F32) for sl in range(2)]
            drow = jnp.concatenate([jnp.sum(jnp.where(lo if hl % 2 == 0 else jnp.logical_not(lo), od[hl // 2], 0.0),
                                            axis=-1, keepdims=True) for hl in range(REP)], axis=0)
            lse_s = jnp.concatenate([jnp.sum(jnp.where(lane == REP * g + hl, lse_t, 0.0), axis=-1, keepdims=True)
                                     for hl in range(REP)], axis=0)
            s = _nt(qst, kwin) + (bias_ref[g] + first)
            p = jnp.exp(s - lse_s)
            ds = p * (_nt(dost, vwin) - drow)
            ds_ref[g] += ds
            tsink = -(jnp.exp(_stack_sinks(sink_ref, g, srow) - lse_s) * drow)
            for hl in range(REP):
                dsink = dsink + jnp.where(lane_1 == REP * g + hl,
                                          jnp.sum(tsink[hl * QT:(hl + 1) * QT], axis=0, keepdims=True), 0.0)
            dsb = ds.astype(BF16)
            dqst = _nn(dsb, kwin) * SCALE
            for sl in range(2):
                dqkv_ref[:, pl.ds((2 * g + sl) * LANES, LANES)] = jnp.where(
                    lo, dqst[2 * sl * QT:(2 * sl + 1) * QT], dqst[(2 * sl + 1) * QT:(2 * sl + 2) * QT]).astype(BF16)
            dk_acc = _tn(dsb, qst)
            dv_acc = _tn(p.astype(BF16), dost)
            dk_f = jnp.where(lane_k < HEAD_DIM, dk_acc + pltpu.roll(dk_acc, HEAD_DIM, 1), 0.0)
            dv_f = jnp.where(lane_k < HEAD_DIM, dv_acc + pltpu.roll(dv_acc, HEAD_DIM, 1), 0.0)
            for acc, col0, car in ((dk_f, K0, kcar_s), (dv_f, V0, vcar_s)):
                cs = pl.ds(g * LANES, LANES)
                co = pl.ds(col0 + g * LANES, LANES)
                if QT > WINDOW:
                    dqkv_ref[pl.ds(0, QT - WINDOW), co] = acc[WINDOW:QT].astype(BF16)
                dqkv_ref[pl.ds(QT - WINDOW, WINDOW), co] = (acc[QT:KW] + car[:, cs]).astype(BF16)
                car[:, cs] = acc[0:WINDOW]
        dsink_ref[...] += dsink

    qs, cur, prev, lse_s = _attn_specs(nt, True)
    return _call(
        body, name="attn_bwd", grid=(nt,), ins=[q, kd, kd, vd, vd, o, do, lse, bias, sinks],
        in_specs=[qs, prev, cur, prev, cur, qs, qs, lse_s, _const_spec((N_KV, REP * QT, KW)),
                  pl.BlockSpec(memory_space=pltpu.SMEM)],
        out_specs=[pl.BlockSpec((QT, XR0), lambda i: (nt - 1 - i, 0)), _const_spec((N_KV, REP * QT, KW)),
                   _const_spec((1, LANES))],
        out_shape=[jax.ShapeDtypeStruct((t_tok, XR0), BF16), jax.ShapeDtypeStruct((N_KV, REP * QT, KW), F32),
                   jax.ShapeDtypeStruct((1, LANES), F32)],
        scratch=[pltpu.VMEM((WINDOW, kvw), F32), pltpu.VMEM((WINDOW, kvw), F32)],
        vmem=VMEM_BIG, carries=carries)


def _bias_tile(table, bmap):
    def body(tab_ref, bm_ref, out_ref):
        bm = bm_ref[...]

        def per_head(hd, carry):
            acc = jnp.full((QT, KW), NEG_INF, F32)
            for b in range(N_BUCKETS):
                acc = jnp.where(bm == b, tab_ref[b, hd], acc)
            out_ref[hd] = acc
            return carry

        lax.fori_loop(0, N_HEADS, per_head, 0)

    return pl.pallas_call(
        body, name="bias_tile", out_shape=jax.ShapeDtypeStruct((N_HEADS, QT, KW), F32),
        in_specs=[pl.BlockSpec(memory_space=pltpu.SMEM), pl.BlockSpec(memory_space=pltpu.VMEM)],
        out_specs=pl.BlockSpec(memory_space=pltpu.VMEM))(table, bmap)


def _bias_grad(ds_acc, bmap):
    def body(ds_ref, bm_ref, out_ref):
        row = lax.broadcasted_iota(jnp.int32, (N_BUCKETS, LANES), 0)
        lane = lax.broadcasted_iota(jnp.int32, (N_BUCKETS, LANES), 1)
        bm = bm_ref[...]

        def per_head(hd, res):
            dsv = ds_ref[hd]
            for b in range(N_BUCKETS):
                val = jnp.sum(jnp.sum(jnp.where(bm == b, dsv, 0.0), axis=0, keepdims=True), axis=1, keepdims=True)
                res = jnp.where((row == b) & (lane == hd), val, res)
            return res

        out_ref[...] = lax.fori_loop(0, N_HEADS, per_head, jnp.zeros((N_BUCKETS, LANES), F32))

    return pl.pallas_call(
        body, name="bias_grad", out_shape=jax.ShapeDtypeStruct((N_BUCKETS, LANES), F32),
        in_specs=[pl.BlockSpec(memory_space=pltpu.VMEM), pl.BlockSpec(memory_space=pltpu.VMEM)],
        out_specs=pl.BlockSpec(memory_space=pltpu.VMEM))(ds_acc, bmap)


MIXOUT = ("w_lru_out", "w_attn_out", "w_o")


def _mix_out_fwd(ya_in, o, gs, h, g, wg, tm=512, carries=()):
    t_tok = h.shape[0]
    nt = t_tok // tm
    groups = _groups_of(MIXOUT)

    def body(ya_ref, o_ref, gs_ref, h_ref, g_ref, wg_ref, hout_ref, yao_ref, ybo_ref, z_ref, wa_s, wb_s, wo_s, sems):
        i = pl.program_id(0)

        @pl.when(i == 0)
        def _():
            _load_weights({groups[0]: wg_ref}, list(zip(MIXOUT, (wa_s, wb_s, wo_s))), sems)

        ya = _nn(ya_ref[...], wa_s[...])
        yb = _nn(o_ref[...], wb_s[...])
        yao_ref[...] = ya.astype(BF16)
        ybo_ref[...] = yb.astype(BF16)
        merged = gs_ref[:, pl.ds(0, D)].astype(F32) * ya + gs_ref[:, pl.ds(D, D)].astype(F32) * yb
        z = _nn(merged.astype(BF16), wo_s[...])
        z_ref[...] = z
        zn, _, _ = _rms_fwd(z, g_ref[...])
        hout_ref[...] = h_ref[...] + zn

    return _call(
        body, name="mix_out_fwd", grid=(nt,), ins=[ya_in, o, gs, h, g, wg[groups[0]]],
        in_specs=[_row_spec(tm, D), _row_spec(tm, D), _row_spec(tm, 2 * D), _row_spec(tm, D), _const_spec((1, D)), ANY],
        out_specs=[_row_spec(tm, D)] * 4,
        out_shape=[jax.ShapeDtypeStruct((t_tok, D), F32), jax.ShapeDtypeStruct((t_tok, D), BF16),
                   jax.ShapeDtypeStruct((t_tok, D), BF16), jax.ShapeDtypeStruct((t_tok, D), F32)],
        scratch=[pltpu.VMEM((D, D), BF16)] * 3 + [DMA((3 * NDEV,))],
        vmem=48 * 2 ** 20, carries=carries)


def _mix_out_bwd(dh, z, ya, yb, gs, g, wg, tm=512, carries=()):
    t_tok = dh.shape[0]
    nt = t_tok // tm
    groups = _groups_of(MIXOUT)

    def body(dh_ref, z_ref, ya_ref, yb_ref, gs_ref, g_ref, wg_ref,
             dyain_ref, do_ref, dgpre_ref, dya_ref, dyb_ref, mg_ref, dz_ref, dg_ref, dbg_ref,
             wa_s, wb_s, wo_s, sems):
        i = pl.program_id(0)

        @pl.when(i == 0)
        def _():
            _load_weights({groups[0]: wg_ref}, list(zip(MIXOUT, (wa_s, wb_s, wo_s))), sems)
            dg_ref[...] = jnp.zeros_like(dg_ref)
            dbg_ref[...] = jnp.zeros_like(dbg_ref)

        gv = g_ref[...]
        _, zh, r = _rms_fwd(z_ref[...], gv)
        dz, dg = _rms_bwd(dh_ref[...], zh, r, gv)
        dg_ref[...] += dg
        dzb = dz.astype(BF16)
        dz_ref[...] = dzb
        ga, gb = gs_ref[:, pl.ds(0, D)].astype(F32), gs_ref[:, pl.ds(D, D)].astype(F32)
        ya_v, yb_v = ya_ref[...].astype(F32), yb_ref[...].astype(F32)
        mg_ref[...] = (ga * ya_v + gb * yb_v).astype(BF16)
        dm = _nt(dzb, wo_s[...])
        dga = dm * ya_v * ga * (1.0 - ga)
        dgb = dm * yb_v * gb * (1.0 - gb)
        dgpre_ref[:, pl.ds(0, D)] = dga.astype(BF16)
        dgpre_ref[:, pl.ds(D, D)] = dgb.astype(BF16)
        dbg_ref[:, pl.ds(0, D)] += jnp.sum(dga, axis=0, keepdims=True)
        dbg_ref[:, pl.ds(D, D)] += jnp.sum(dgb, axis=0, keepdims=True)
        dya = (dm * ga).astype(BF16)
        dyb = (dm * gb).astype(BF16)
        dya_ref[...] = dya
        dyb_ref[...] = dyb
        dyain_ref[...] = _nt(dya, wa_s[...])
        do_ref[...] = _nt(dyb, wb_s[...]).astype(BF16)

    bf = jax.ShapeDtypeStruct((t_tok, D), BF16)
    return _call(
        body, name="mix_out_bwd", grid=(nt,), ins=[dh, z, ya, yb, gs, g, wg[groups[0]]],
        in_specs=[_row_spec(tm, D)] * 4 + [_row_spec(tm, 2 * D), _const_spec((1, D)), ANY],
        out_specs=[_row_spec(tm, D), _row_spec(tm, D), _row_spec(tm, 2 * D)] + [_row_spec(tm, D)] * 4
        + [_const_spec((1, D)), _const_spec((1, 2 * D))],
        out_shape=[jax.ShapeDtypeStruct((t_tok, D), F32), bf, jax.ShapeDtypeStruct((t_tok, 2 * D), BF16), bf, bf, bf, bf,
                   jax.ShapeDtypeStruct((1, D), F32), jax.ShapeDtypeStruct((1, 2 * D), F32)],
        scratch=[pltpu.VMEM((D, D), BF16)] * 3 + [DMA((3 * NDEV,))],
        vmem=VMEM_BIG, carries=carries)


def _sum_parts(parts_list):
    n = len(parts_list)
    _, r, c = parts_list[0].shape
    tc = 256

    def body(*refs):
        for p_ref, o_ref in zip(refs[:n], refs[n:]):
            acc = p_ref[0].astype(F32)
            for s in range(1, NDEV):
                acc = acc + p_ref[s].astype(F32)
            o_ref[...] = acc

    return pl.pallas_call(
        body, name=f"sum_parts_{r}", grid=(c // tc,),
        in_specs=[pl.BlockSpec((NDEV, r, tc), lambda i: (0, 0, i))] * n,
        out_specs=[pl.BlockSpec((r, tc), lambda i: (0, i))] * n,
        out_shape=[jax.ShapeDtypeStruct((r, c), F32)] * n,
        compiler_params=pltpu.CompilerParams(dimension_semantics=("arbitrary",), vmem_limit_bytes=48 * 2 ** 20),
    )(*parts_list)


def _adamw_math(w, g, m, v):
    m = ADAM_B1 * m + (1.0 - ADAM_B1) * g
    v = ADAM_B2 * v + (1.0 - ADAM_B2) * (g * g)
    m_hat = m / (1.0 - ADAM_B1 ** ADAM_STEP)
    v_hat = v / (1.0 - ADAM_B2 ** ADAM_STEP)
    delta = -ADAM_LR * (m_hat / (jnp.sqrt(v_hat) + ADAM_EPS) + ADAM_WD * w)
    return delta, m, v


def _sum_ready(parts_list, name, carries=(), chip_sums=()):
    n = len(parts_list)
    c = parts_list[0].shape[2]
    tc = 2 * LANES

    def body(*refs):
        for k, (p_ref, o_ref) in enumerate(zip(refs[:n], refs[n:])):
            if k in chip_sums:
                core = lax.axis_index("c")
                slots = [2 * ch + core for ch in range(NDEV // 2)]
            else:
                slots = list(range(NDEV))
            g = p_ref[slots[0]].astype(F32)
            for s in slots[1:]:
                g = g + p_ref[s].astype(F32)
            o_ref[...] = g

    return _call(
        body, name=name, grid=(c // tc,), ins=list(parts_list),
        in_specs=[pl.BlockSpec((NDEV, p.shape[1], tc), lambda i: (0, 0, i)) for p in parts_list],
        out_shape=[jax.ShapeDtypeStruct((p.shape[1], c), F32) for p in parts_list],
        out_specs=[pl.BlockSpec((p.shape[1], tc), lambda i: (0, i)) for p in parts_list],
        vmem=48 * 2 ** 20, carries=carries)


def _adamw_cols(items, name):
    n = len(items)
    c = items[0][1].shape[1]
    tc = LANES

    def body(*refs):
        for k in range(n):
            g_ref, w_ref, m_ref, v_ref = refs[4 * k:4 * k + 4]
            go_ref, d_ref, nm_ref, nv_ref = refs[4 * n + 4 * k:4 * n + 4 * k + 4]
            g = g_ref[...]
            d, m, v = _adamw_math(w_ref[...], g, m_ref[...], v_ref[...])
            go_ref[...] = g
            d_ref[...] = d
            nm_ref[...] = m
            nv_ref[...] = v

    ins, specs, out_shape = [], [], []
    for g, w, m, v in items:
        r = w.shape[0]
        ins += [g, w, m, v]
        specs += [pl.BlockSpec((r, tc), lambda i: (0, i))] * 4
        out_shape += [jax.ShapeDtypeStruct((r, c), F32)] * 4
    outs, _ = _call(body, name=name, grid=(c // tc,), ins=ins, in_specs=specs, out_shape=out_shape,
                    out_specs=specs, vmem=48 * 2 ** 20)
    return [tuple(outs[4 * k:4 * k + 4]) for k in range(n)]


def _adamw_body(n):
    def body(*refs):
        for k in range(n):
            g_ref, w_ref, m_ref, v_ref = refs[4 * k:4 * k + 4]
            d_ref, nm_ref, nv_ref = refs[4 * n + 3 * k:4 * n + 3 * k + 3]
            d, m, v = _adamw_math(w_ref[...], g_ref[...], m_ref[...], v_ref[...])
            d_ref[...] = d
            nm_ref[...] = m
            nv_ref[...] = v
    return body


def _adamw(items):
    n = len(items)
    r, c = items[0][0].shape
    tr = r if r * c <= 2 ** 18 else max(t for t in range(8, 65, 8) if r % t == 0)
    spec = pl.BlockSpec((tr, c), lambda i: (i, 0))
    outs = pl.pallas_call(
        _adamw_body(n), name=f"adamw_{r}x{c}", grid=(r // tr,),
        in_specs=[spec] * (4 * n), out_specs=[spec] * (3 * n),
        out_shape=[jax.ShapeDtypeStruct((r, c), F32)] * (3 * n),
        compiler_params=pltpu.CompilerParams(dimension_semantics=("arbitrary",), vmem_limit_bytes=40 * 2 ** 20),
    )(*[a for it in items for a in it])
    return [tuple(outs[3 * k:3 * k + 3]) for k in range(n)]


def _adamw_small(items):
    n = len(items)
    vm = pl.BlockSpec(memory_space=pltpu.VMEM)
    outs = pl.pallas_call(
        _adamw_body(n), name="adamw_small", in_specs=[vm] * (4 * n), out_specs=[vm] * (3 * n),
        out_shape=[jax.ShapeDtypeStruct(it[1].shape, F32) for it in items for _ in range(3)],
    )(*[a for it in items for a in it])
    return [tuple(outs[3 * k:3 * k + 3]) for k in range(n)]


def _pack_small(arrs):
    rows, offs = [], []
    total = 0
    for a in arrs:
        flat = a.reshape(-1).astype(F32)
        nr = -(-flat.shape[0] // LANES)
        flat = jnp.pad(flat, (0, nr * LANES - flat.shape[0]))
        rows.append(flat.reshape(nr, LANES))
        offs.append((total, nr))
        total += nr
    pad = -total % 8
    if pad:
        rows.append(jnp.zeros((pad, LANES), F32))
    return jnp.concatenate(rows, axis=0), offs


def _unpack_small(pack, offs, shapes):
    out = []
    for (o, nr), shp in zip(offs, shapes):
        size = int(np.prod(shp))
        out.append(pack[o:o + nr].reshape(-1)[:size].reshape(shp))
    return out


def _sum_small(gathered, rows):
    def body(p_ref, o_ref):
        acc = p_ref[pl.ds(0, rows), :]
        for s in range(1, NDEV):
            acc = acc + p_ref[pl.ds(s * rows, rows), :]
        o_ref[...] = acc

    return pl.pallas_call(
        body, name="sum_small", out_shape=jax.ShapeDtypeStruct((rows, LANES), F32),
        in_specs=[pl.BlockSpec(memory_space=pltpu.VMEM)], out_specs=pl.BlockSpec(memory_space=pltpu.VMEM))(gathered)


def _block_diag(w):
    w = w.reshape(D // LANES, 2, LRU_BLOCK, LRU_BLOCK)
    z = jnp.zeros((D // LANES, LRU_BLOCK, LRU_BLOCK), w.dtype)
    top = jnp.concatenate([w[:, 0], z], axis=2)
    bot = jnp.concatenate([z, w[:, 1]], axis=2)
    return jnp.concatenate([top, bot], axis=1)


def _block_diag_grad(dw):
    a = dw[:, :LRU_BLOCK, :LRU_BLOCK]
    b = dw[:, LRU_BLOCK:, LRU_BLOCK:]
    return jnp.stack([a, b], axis=1).reshape(D // LRU_BLOCK, LRU_BLOCK, LRU_BLOCK)


def kernel(x, ffn1_pre_g, ffn1_w1, ffn1_w3, ffn1_w2, ffn1_post_g, mix_pre_g, w_in, conv_w, conv_b, rg_a_w, rg_a_b, rg_x_w, rg_x_b, lru_lambda, w_lru_out, attn_sinks, rel_bias, w_attn_out, w_gate, b_gate, w_o, mix_post_g, ffn2_pre_g, ffn2_w1, ffn2_w3, ffn2_w2, ffn2_post_g, loss_target, m_ffn1_pre_g, m_ffn1_w1, m_ffn1_w3, m_ffn1_w2, m_ffn1_post_g, m_mix_pre_g, m_w_in, m_conv_w, m_conv_b, m_rg_a_w, m_rg_a_b, m_rg_x_w, m_rg_x_b, m_lru_lambda, m_w_lru_out, m_attn_sinks, m_rel_bias, m_w_attn_out, m_w_gate, m_b_gate, m_w_o, m_mix_post_g, m_ffn2_pre_g, m_ffn2_w1, m_ffn2_w3, m_ffn2_w2, m_ffn2_post_g, v_ffn1_pre_g, v_ffn1_w1, v_ffn1_w3, v_ffn1_w2, v_ffn1_post_g, v_mix_pre_g, v_w_in, v_conv_w, v_conv_b, v_rg_a_w, v_rg_a_b, v_rg_x_w, v_rg_x_b, v_lru_lambda, v_w_lru_out, v_attn_sinks, v_rel_bias, v_w_attn_out, v_w_gate, v_b_gate, v_w_o, v_mix_post_g, v_ffn2_pre_g, v_ffn2_w1, v_ffn2_w3, v_ffn2_w2, v_ffn2_post_g):
    names = ["ffn1_pre_g", "ffn1_w1", "ffn1_w3", "ffn1_w2", "ffn1_post_g", "mix_pre_g", "w_in", "conv_w", "conv_b",
             "rg_a_w", "rg_a_b", "rg_x_w", "rg_x_b", "lru_lambda", "w_lru_out", "attn_sinks", "rel_bias", "w_attn_out",
             "w_gate", "b_gate", "w_o", "mix_post_g", "ffn2_pre_g", "ffn2_w1", "ffn2_w3", "ffn2_w2", "ffn2_post_g"]
    ws = dict(zip(names, (ffn1_pre_g, ffn1_w1, ffn1_w3, ffn1_w2, ffn1_post_g, mix_pre_g, w_in, conv_w, conv_b, rg_a_w,
                          rg_a_b, rg_x_w, rg_x_b, lru_lambda, w_lru_out, attn_sinks, rel_bias, w_attn_out, w_gate,
                          b_gate, w_o, mix_post_g, ffn2_pre_g, ffn2_w1, ffn2_w3, ffn2_w2, ffn2_post_g)))
    ms = dict(zip(names, (m_ffn1_pre_g, m_ffn1_w1, m_ffn1_w3, m_ffn1_w2, m_ffn1_post_g, m_mix_pre_g, m_w_in, m_conv_w,
                          m_conv_b, m_rg_a_w, m_rg_a_b, m_rg_x_w, m_rg_x_b, m_lru_lambda, m_w_lru_out, m_attn_sinks,
                          m_rel_bias, m_w_attn_out, m_w_gate, m_b_gate, m_w_o, m_mix_post_g, m_ffn2_pre_g, m_ffn2_w1,
                          m_ffn2_w3, m_ffn2_w2, m_ffn2_post_g)))
    vs = dict(zip(names, (v_ffn1_pre_g, v_ffn1_w1, v_ffn1_w3, v_ffn1_w2, v_ffn1_post_g, v_mix_pre_g, v_w_in, v_conv_w,
                          v_conv_b, v_rg_a_w, v_rg_a_b, v_rg_x_w, v_rg_x_b, v_lru_lambda, v_w_lru_out, v_attn_sinks,
                          v_rel_bias, v_w_attn_out, v_w_gate, v_b_gate, v_w_o, v_mix_post_g, v_ffn2_pre_g, v_ffn2_w1,
                          v_ffn2_w3, v_ffn2_w2, v_ffn2_post_g)))
    me = 4 * lax.axis_index("x") + 2 * lax.axis_index("y") + lax.axis_index("c")
    vec = lambda n: ws[n].reshape(1, -1)

    def shard2d(name):
        if name == "conv":
            row = lax.bitcast_convert_type(conv_w.reshape(CONV_WIDTH, LANES), BF16).reshape(1, D)
            return jnp.concatenate([row, jnp.zeros((LOC["conv"][2] - 1, D), BF16)], axis=0)
        a = ws[name].reshape(ws[name].shape[-2], ws[name].shape[-1])
        return (a.T if name in COL_SHARDED else a).astype(BF16)

    packs = {g: jnp.concatenate([shard2d(n) for n, _ in members], axis=0) for g, members in GROUPS}

    wg = {}
    _, ((wg["ffn1"],),) = _call(None, name="allgather_ffn1", grid=(), ins=[], in_specs=[], out_shape=[], out_specs=[],
                                carries=[_AllGatherTree(packs["ffn1"])])
    conv_rows = wg["ffn1"].reshape(NDEV, GROUP_ROWS["ffn1"], D)[:, LOC["conv"][1]]
    cw = jnp.transpose(lax.bitcast_convert_type(conv_rows.reshape(NDEV, CONV_WIDTH, LANES, 2), F32),
                       (1, 0, 2)).reshape(CONV_WIDTH, D)
    bmap = jnp.asarray(_bucket_map())
    bias = _bias_tile(rel_bias, bmap).reshape(N_KV, REP * QT, KW)
    sinks = attn_sinks.reshape(N_HEADS)
    wa_bd = _block_diag(rg_a_w.reshape(D // LRU_BLOCK, LRU_BLOCK, LRU_BLOCK)).astype(BF16)
    wx_bd = _block_diag(rg_x_w.reshape(D // LRU_BLOCK, LRU_BLOCK, LRU_BLOCK)).astype(BF16)
    lru_args = (cw, vec("conv_b"), wa_bd, wx_bd, vec("rg_a_b"), vec("rg_x_b"), vec("lru_lambda"))
    x2, tgt = x[0], loss_target[0]

    (h1, a1, b1, f1, nb1), ((wg["mixin"],),) = _ffn_fwd(
        x2, vec("ffn1_pre_g"), vec("ffn1_post_g"), wg, ("ffn1_w1", "ffn1_w3", "ffn1_w2"),
        carries=[_AllGather(packs["mixin"])])
    (q, kd, vd, xr, xg, gs, ub), ((wg["mixout"],),) = _mix_proj_fwd(
        h1, vec("mix_pre_g"), vec("b_gate"), wg, carries=[_AllGather(packs["mixout"])])
    (ya_in, hseq), ((wg["ffn2a"],),) = _lru_fwd(xr, xg, *lru_args, carries=[_AllGather(packs["ffn2a"])])
    (o, lse), ((wg["ffn2b"],),) = _attn_fwd(q, kd, vd, bias, sinks, carries=[_AllGather(packs["ffn2b"])])
    (h2, ya, yb, z), _ = _mix_out_fwd(ya_in, o, gs, h1, vec("mix_post_g"), wg)
    (_, a2, b2, f2, nb2, dy, loss_part), _ = _ffn_fwd(h2, vec("ffn2_pre_g"), vec("ffn2_post_g"), wg,
                                                     ("ffn2_w1", "ffn2_w3", "ffn2_w2"), target=tgt)

    gsm, parts = {}, {}
    rs = lambda *grads_: [_ReduceScatterSend(list(grads_))]
    ffr = FF // NDEV
    (dab, s_act, dfb, gsm["ffn2_post_g"]), _ = _ffn_bwd_a(dy, f2, a2, b2, vec("ffn2_post_g"), wg, "ffn2_w2")
    g_w2, _ = _dw(s_act, dfb, FF // 2, "dw_ffn2_w2")
    g_w13, ((parts["ffn2_w2"],),) = _dw(dab, nb2, D // 2, "dw_ffn2_w13", carries=rs(g_w2))
    (dh2, gsm["ffn2_pre_g"]), ((parts["ffn2_w1"],),) = _ffn_bwd_dx(
        dab, h2, dy, vec("ffn2_pre_g"), wg, "ffn2_w1", "ffn2_w3", carries=rs((g_w13, 0, ffr, ffr)))
    (dya_in, do, dgpre, dya, dyb, mg, dzb, gsm["mix_post_g"], gsm["b_gate"]), ((parts["ffn2_w3"],),) = _mix_out_bwd(
        dh2, z, ya, yb, gs, vec("mix_post_g"), wg, carries=rs((g_w13, FF, ffr, ffr)))
    g_wa, _ = _dw(ya_in, dya, D // 2, "dw_w_lru_out")
    g_wb, _ = _dw(o, dyb, D // 2, "dw_w_attn_out")
    g_wo, _ = _dw(mg, dzb, D // 2, "dw_w_o")
    g_wgate, _ = _dw(dgpre, ub, D // 2, "dw_w_gate")
    (dqkv, ds_acc, dsink), (mixout_parts,) = _attn_bwd(
        q, kd, vd, o, do, lse, bias, sinks, carries=rs(g_wa, g_wb, g_wo, g_wgate))
    parts["w_lru_out"], parts["w_attn_out"], parts["w_o"], parts["w_gate"] = mixout_parts
    (dxr, dxg, dvec, dwa, dwx), _ = _lru_bwd(dya_in, xr, xg, hseq, *lru_args)
    g_qkv, _ = _dw(dqkv, ub, D // 2, "dw_w_in_qkv")
    g_xr, _ = _dw(dxr, ub, D // 2, "dw_w_in_xr")
    g_xg, _ = _dw(dxg, ub, D // 2, "dw_w_in_xg")
    g_win = jnp.concatenate(
        [g_qkv[:D]] + [g_qkv[c0 + gi * LANES:c0 + gi * LANES + HEAD_DIM] for c0 in (K0, V0) for gi in range(N_KV)]
        + [g_xr, g_xg], axis=0)
    wir, wir_a = IN_W // NDEV, 336
    (dh1, gsm["mix_pre_g"]), ((win_a,),) = _mix_proj_bwd(
        dqkv, dxr, dxg, dgpre, h1, dh2, vec("mix_pre_g"), wg, carries=rs((g_win, 0, wir, wir_a)))
    gsm["conv_w"] = dvec[0:CONV_WIDTH]
    gsm["conv_b"], gsm["rg_a_b"], gsm["rg_x_b"], gsm["lru_lambda"] = dvec[4], dvec[5], dvec[6], dvec[7]
    gsm["rg_a_w"] = _block_diag_grad(dwa)
    gsm["rg_x_w"] = _block_diag_grad(dwx)
    gsm["attn_sinks"] = dsink[0, :N_HEADS]
    gsm["rel_bias"] = _bias_grad(ds_acc.reshape(N_HEADS, QT, KW), bmap)[:, :N_HEADS]
    late = ("ffn1_post_g", "ffn1_pre_g")
    early = tuple(n for n in SMALL if n not in late)
    early_pack, early_offs = _pack_small([gsm[n] for n in early])
    (dab, s_act, dfb, gsm["ffn1_post_g"]), ((win_b,), (early_all,)) = _ffn_bwd_a(
        dh1, f1, a1, b1, vec("ffn1_post_g"), wg, "ffn1_w2",
        carries=rs((g_win, wir_a, wir, wir - wir_a)) + [_AllGather(early_pack)])
    parts["w_in"] = jnp.concatenate([win_a, win_b], axis=1)
    g_w2, _ = _dw(s_act, dfb, FF // 2, "dw_ffn1_w2")
    g_w13, ((parts["ffn1_w2"],),) = _dw(dab, nb1, D // 2, "dw_ffn1_w13", carries=rs(g_w2))
    chip_w13 = _pair_sum(g_w13, [(0, ffr), (FF, ffr)])
    (grad_x, gsm["ffn1_pre_g"]), ((parts["ffn1_w1"], parts["ffn1_w3"]),) = _ffn_bwd_dx(
        dab, x2, dh1, vec("ffn1_pre_g"), wg, "ffn1_w1", "ffn1_w3",
        carries=[_ReduceScatterSend([(chip_w13, 0, ffr, ffr), (chip_w13, FF, ffr, ffr)], chip_sums=True)])
    late_pack, late_offs = _pack_small([gsm[n] for n in late] + [loss_part])

    grads, delta, new_m, new_v = {}, {}, {}, {}
    t_form = [n for n in COL_SHARDED if ws[n].shape[-1] % LANES]
    view = {n: (lambda a: a.reshape(a.shape[-2], a.shape[-1]).T) if n in t_form
            else (lambda a: a.reshape(a.shape[-2], a.shape[-1])) for n in BIG}
    unview = {n: (lambda a: a.T) if n in t_form else (lambda a: a) for n in BIG}

    group = [n for n in BIG if n != "w_gate"]
    sums, ((late_all,),) = _sum_ready([parts[n] for n in group], "sum_ready", [_AllGather(late_pack)],
                                      chip_sums=(group.index("ffn1_w1"), group.index("ffn1_w3")))
    res = _adamw_cols([(g, view[n](ws[n]), view[n](ms[n]), view[n](vs[n])) for n, g in zip(group, sums)], "adamw_big")
    for n, quad in zip(group, res):
        grads[n], delta[n], new_m[n], new_v[n] = (unview[n](a) for a in quad)
    (g_gate_t,) = _sum_parts([parts["w_gate"]])
    grads["w_gate"] = g_gate_t.T
    ((delta["w_gate"], new_m["w_gate"], new_v["w_gate"]),) = _adamw(
        [(grads["w_gate"], view["w_gate"](ws["w_gate"]), view["w_gate"](ms["w_gate"]), view["w_gate"](vs["w_gate"]))])
    for group, gathered, offs in ((early, early_all, early_offs), (late, late_all, late_offs)):
        total = _sum_small(gathered, gathered.shape[0] // NDEV)
        shapes = [(CONV_WIDTH, D) if n == "conv_w" else ws[n].shape for n in group]
        if group is late:
            shapes = shapes + [(1, LANES)]
        unpacked = _unpack_small(total, offs, shapes)
        if group is late:
            loss = unpacked.pop()[0, 0]
        for n, g in zip(group, unpacked):
            grads[n] = g
    grads["conv_w"] = lax.dynamic_slice(grads["conv_w"], (0, me * LANES), (CONV_WIDTH, LANES)).reshape(conv_w.shape)

    flat2d = lambda a: a.reshape(-1, a.shape[-1])
    res = _adamw_small([(flat2d(grads[n].reshape(ws[n].shape)), flat2d(ws[n]), flat2d(ms[n]), flat2d(vs[n]))
                        for n in SMALL])
    for n, (d_, m_, v_) in zip(SMALL, res):
        delta[n], new_m[n], new_v[n] = d_, m_, v_

    outs = [loss, grad_x.reshape(x.shape)]
    for src in (grads, delta, new_m, new_v):
        outs += [src[n].reshape(ws[n].shape) for n in names]
    return tuple(outs)
```

```python
import functools
import math
import operator

import numpy as np
import jax
import jax.numpy as jnp
from jax import lax
from jax.experimental import pallas as pl
from jax.experimental.pallas import tpu as pltpu

F32, BF16 = jnp.float32, jnp.bfloat16

NDEV = 8
D = 1024
FF = 2816
N_HEADS, N_KV, HEAD_DIM = 16, 4, 64
CHUNK, WINDOW = 64, 128
N_BUCKETS, MAX_DISTANCE = 32, 128
LRU_BLOCK = 64
CONV_WIDTH = 4
LRU_C = 8.0
RMS_EPS = 1e-6
NEG_INF = -1e30
LANES = 128
QT = 128
KW = QT + WINDOW
LRU_ROWS = 1024
IN_W = D + 2 * N_KV * HEAD_DIM + 2 * D
INP_W = D + 2 * N_KV * LANES + 2 * D
VMEM_BIG = 58 * 2 ** 20

ADAM_LR, ADAM_B1, ADAM_B2, ADAM_EPS, ADAM_WD, ADAM_STEP = 0.001, 0.9, 0.999, 1e-08, 0.01, 10

GROUPS = (("ffn1", (("ffn1_w1", FF // NDEV), ("ffn1_w3", FF // NDEV), ("ffn1_w2", FF // NDEV), ("conv", 16))),
          ("mixin", (("w_in", IN_W // NDEV), ("w_gate", 2 * D // NDEV))),
          ("mixout", (("w_lru_out", D // NDEV), ("w_attn_out", D // NDEV), ("w_o", D // NDEV))),
          ("ffn2a", (("ffn2_w1", FF // NDEV), ("ffn2_w3", FF // NDEV))),
          ("ffn2b", (("ffn2_w2", FF // NDEV),)))
LOC, GROUP_ROWS = {}, {}
for _g, _members in GROUPS:
    _o = 0
    for _n, _r in _members:
        LOC[_n] = (_g, _o, _r)
        _o += _r
    GROUP_ROWS[_g] = _o
BIG = tuple(n for _, members in GROUPS for n, _ in members if n != "conv")
COL_SHARDED = ("ffn1_w1", "ffn1_w3", "w_in", "w_gate", "ffn2_w1", "ffn2_w3")

SMALL = ("ffn1_pre_g", "ffn1_post_g", "mix_pre_g", "conv_w", "conv_b", "rg_a_w", "rg_a_b", "rg_x_w", "rg_x_b",
         "lru_lambda", "attn_sinks", "rel_bias", "b_gate", "mix_post_g", "ffn2_pre_g", "ffn2_post_g")

MESH = pl.DeviceIdType.MESH
ANY = pl.BlockSpec(memory_space=pl.ANY)
DMA = pltpu.SemaphoreType.DMA


def _nn(a, b):
    return lax.dot_general(a, b, (((1,), (0,)), ((), ())), preferred_element_type=F32)


def _nt(a, b):
    return lax.dot_general(a, b, (((1,), (1,)), ((), ())), preferred_element_type=F32)


def _tn(a, b):
    return lax.dot_general(a, b, (((0,), (0,)), ((), ())), preferred_element_type=F32)


def _rms_fwd(x, g):
    r = lax.rsqrt(jnp.mean(x * x, axis=-1, keepdims=True) + RMS_EPS)
    xh = x * r
    return xh * g, xh, r


def _rms_bwd(dn, xh, r, g):
    dxh = dn * g
    dx = r * (dxh - xh * jnp.mean(dxh * xh, axis=-1, keepdims=True))
    return dx, jnp.sum(dn * xh, axis=0, keepdims=True)


def _row_spec(tm, c):
    return pl.BlockSpec((tm, c), lambda i: (i, 0))


def _const_spec(shape):
    nd = len(shape)
    return pl.BlockSpec(shape, lambda i: (0,) * nd)


class _AllGather:
    def __init__(self, shard):
        self.m, n = shard.shape
        self.ins = [shard]
        self.out_shape = [jax.ShapeDtypeStruct((NDEV * self.m, n), shard.dtype)]
        self.scratch = [DMA((7,)), DMA((7,)), DMA]

    def _copies(self, ins, outs, scr, all_of_them):
        x_ref, out_ref = ins[0], outs[0]
        send_sems, recv_sems, local_sem = scr
        x, y, c = lax.axis_index("x"), lax.axis_index("y"), lax.axis_index("c")
        me, sibling = (x, y, c), (x, y, 1 - c)
        chips = [(1 - x, y), (x, 1 - y), (1 - x, 1 - y)]
        m = self.m

        def rows(px, py, pc):
            return out_ref.at[pl.ds((4 * px + 2 * py + pc) * m, m), :]

        def copy(k, block, to, src=None):
            return pltpu.make_async_remote_copy(
                src_ref=rows(*block) if src is None else src, dst_ref=rows(*block),
                send_sem=send_sems.at[k], recv_sem=recv_sems.at[k], device_id=to, device_id_type=MESH)

        mine = pltpu.make_async_copy(x_ref, rows(*me), local_sem)
        first = [copy(0, me, sibling, src=x_ref)] + [copy(1 + j, me, (*chip, c), src=x_ref)
                                                     for j, chip in enumerate(chips)]
        if not all_of_them:
            return mine, first
        passed = [copy(4 + j, (*chip, c), sibling) for j, chip in enumerate(chips)]
        landed = [copy(1 + j, (*chip, c), me) for j, chip in enumerate(chips)]
        from_sibling = [copy(0, sibling, me)] + [copy(4 + j, (*chip, 1 - c), me) for j, chip in enumerate(chips)]
        return mine, first, passed, landed, from_sibling

    def start(self, ins, outs, scr):
        mine, first = self._copies(ins, outs, scr, False)
        mine.start()
        for cp in first:
            cp.start()

    def finish(self, ins, outs, scr):
        mine, first, passed, landed, from_sibling = self._copies(ins, outs, scr, True)
        for cp_in, cp_on in zip(landed, passed):
            cp_in.wait_recv()
            cp_on.start()
        for cp in from_sibling:
            cp.wait_recv()
        for cp in first + passed:
            cp.wait_send()
        mine.wait()


class _AllGatherTree:
    def __init__(self, shard):
        self.m, n = shard.shape
        self.half = self.m // 32 * 16
        self.ins = [shard]
        self.out_shape = [jax.ShapeDtypeStruct((NDEV * self.m, n), shard.dtype)]
        self.scratch = [DMA((9,)), DMA((9,)), DMA]

    def start(self, ins, outs, scr):
        x_ref, out_ref = ins[0], outs[0]
        send_sems, recv_sems, local_sem = scr
        x, y, c = lax.axis_index("x"), lax.axis_index("y"), lax.axis_index("c")
        m, half = self.m, self.half
        me, sibling, xn, yn, diag = (x, y, c), (x, y, 1 - c), (1 - x, y, c), (x, 1 - y, c), (1 - x, 1 - y, c)
        other = lambda dev: (dev[0], dev[1], 1 - c)

        def rows(dev, lo=0, n=m):
            return out_ref.at[pl.ds((4 * dev[0] + 2 * dev[1] + dev[2]) * m + lo, n), :]

        def copy(k, dev, to, lo=0, n=m, src=None):
            return pltpu.make_async_remote_copy(
                src_ref=rows(dev, lo, n) if src is None else src, dst_ref=rows(dev, lo, n),
                send_sem=send_sems.at[k], recv_sem=recv_sems.at[k], device_id=to, device_id_type=MESH)

        mine = pltpu.make_async_copy(x_ref, rows(me), local_sem)
        own = [copy(0, me, sibling, src=x_ref), copy(1, me, xn, src=x_ref), copy(2, me, yn, src=x_ref)]
        mine.start()
        for cp in own:
            cp.start()
        steps = [(copy(1, xn, me), [copy(3, xn, yn, 0, half), copy(5, xn, sibling)]),
                 (copy(2, yn, me), [copy(4, yn, xn, half, m - half), copy(6, yn, sibling)]),
                 (copy(3, diag, me, 0, half), [copy(7, diag, sibling, 0, half)]),
                 (copy(4, diag, me, half, m - half), [copy(8, diag, sibling, half, m - half)])]
        from_sibling = [copy(0, sibling, me), copy(5, other(xn), me), copy(6, other(yn), me),
                        copy(7, other(diag), me, 0, half), copy(8, other(diag), me, half, m - half)]
        passed = []
        for landing, onward in steps:
            landing.wait_recv()
            for cp in onward:
                cp.start()
            passed += onward
        for cp in from_sibling:
            cp.wait_recv()
        for cp in own + passed:
            cp.wait_send()
        mine.wait()

    def finish(self, ins, outs, scr):
        pass


class _ReduceScatterSend:
    def __init__(self, grads, chip_sums=False):
        grads = [g if isinstance(g, tuple) else (g, 0, g.shape[0] // NDEV, g.shape[0] // NDEV) for g in grads]
        self.peers = (2, 4, 6) if chip_sums else tuple(range(1, NDEV))
        self.nw = len(grads)
        self.base = [b for _, b, _, _ in grads]
        self.stride = [s for _, _, s, _ in grads]
        self.rows = [r for _, _, _, r in grads]
        self.ins = [g for g, _, _, _ in grads]
        self.out_shape = [jax.ShapeDtypeStruct((NDEV, r, g.shape[1]), g.dtype) for g, _, _, r in grads]
        self.scratch = [DMA((self.nw, NDEV - 1)), DMA((self.nw, NDEV - 1)), DMA((self.nw,))]

    def _copies(self, g_refs, r_refs, scr, want):
        send_sems, recv_sems, local_sems = scr
        x, y, c = lax.axis_index("x"), lax.axis_index("y"), lax.axis_index("c")
        me = 4 * x + 2 * y + c
        rows, base, stride = self.rows, self.base, self.stride
        out = []
        if want == "local":
            for w in range(self.nw):
                out.append(pltpu.make_async_copy(g_refs[w].at[pl.ds(base[w] + me * stride[w], rows[w])],
                                                 r_refs[w].at[me], local_sems.at[w]))
            return out
        for k in self.peers:
            px, py, pc = x ^ (k >> 2), y ^ ((k >> 1) & 1), c ^ (k & 1)
            peer = 4 * px + 2 * py + pc
            for w in range(self.nw):
                sems = dict(send_sem=send_sems.at[w, k - 1], recv_sem=recv_sems.at[w, k - 1],
                            device_id=(px, py, pc), device_id_type=MESH)
                if want == "send":
                    out.append(pltpu.make_async_remote_copy(
                        src_ref=g_refs[w].at[pl.ds(base[w] + peer * stride[w], rows[w])], dst_ref=r_refs[w].at[me],
                        **sems))
                else:
                    out.append(pltpu.make_async_remote_copy(
                        src_ref=g_refs[w].at[pl.ds(0, rows[w])], dst_ref=r_refs[w].at[peer], **sems))
        return out

    def start(self, ins, outs, scr):
        for cp in self._copies(ins, outs, scr, "local") + self._copies(ins, outs, scr, "send"):
            cp.start()

    def finish(self, ins, outs, scr):
        for cp in self._copies(ins, outs, scr, "recv"):
            cp.wait_recv()
        for cp in self._copies(ins, outs, scr, "send"):
            cp.wait_send()
        for cp in self._copies(ins, outs, scr, "local"):
            cp.wait()


def _call(body, *, name, grid, ins, in_specs, out_shape, out_specs, scratch=(), vmem=None, carries=()):
    n_in, n_out, n_scr = len(ins), len(out_shape), len(scratch)
    ng = len(grid)

    def split(refs):
        pos = [0]

        def take(k):
            part = refs[pos[0]:pos[0] + k]
            pos[0] += k
            return part

        i_refs = take(n_in)
        c_in = [take(len(c.ins)) for c in carries]
        o_refs = take(n_out)
        c_out = [take(len(c.out_shape)) for c in carries]
        s_refs = take(n_scr)
        c_scr = [take(len(c.scratch)) for c in carries]
        return i_refs, o_refs, s_refs, list(zip(carries, c_in, c_out, c_scr))

    def full(*refs):
        i_refs, o_refs, s_refs, cparts = split(refs)
        if ng == 0:
            for c, a, b, s in cparts:
                c.start(a, b, s)
            for c, a, b, s in cparts:
                c.finish(a, b, s)
            return
        ids = [pl.program_id(a) for a in range(ng)]
        if cparts:
            @pl.when(functools.reduce(operator.and_, [i == 0 for i in ids]))
            def _():
                for c, a, b, s in cparts:
                    c.start(a, b, s)

        body(*i_refs, *o_refs, *s_refs)
        if cparts:
            @pl.when(functools.reduce(operator.and_, [i == g - 1 for i, g in zip(ids, grid)]))
            def _():
                for c, a, b, s in cparts:
                    c.finish(a, b, s)

    all_ins = list(ins) + [a for c in carries for a in c.ins]
    all_in_specs = list(in_specs) + [ANY for c in carries for _ in c.ins]
    all_out_shape = list(out_shape) + [s for c in carries for s in c.out_shape]
    all_out_specs = list(out_specs) + [ANY for c in carries for _ in c.out_shape]
    all_scratch = list(scratch) + [s for c in carries for s in c.scratch]
    kwargs = dict(grid=grid) if ng else {}
    outs = pl.pallas_call(
        full, name=name, in_specs=all_in_specs, out_specs=all_out_specs, out_shape=all_out_shape,
        scratch_shapes=all_scratch,
        compiler_params=pltpu.CompilerParams(dimension_semantics=("arbitrary",) * ng if ng else None,
                                             vmem_limit_bytes=vmem),
        **kwargs)(*all_ins)
    outs = list(outs)
    res, pos = outs[:n_out], n_out
    carried = []
    for c in carries:
        carried.append(outs[pos:pos + len(c.out_shape)])
        pos += len(c.out_shape)
    return res, carried


def _pair_sum(g, items):
    rows = items[0][1]
    n = len(items) * (NDEV // 2)
    width = g.shape[1]

    def body(g_ref, out_ref, got_ref, mine_v, got_v, send_sems, recv_sems, ld_sems, st_sems):
        x, y, c = lax.axis_index("x"), lax.axis_index("y"), lax.axis_index("c")
        keep = [base + (2 * ch + c) * rows for base, _ in items for ch in range(NDEV // 2)]
        give = [base + (2 * ch + 1 - c) * rows for base, _ in items for ch in range(NDEV // 2)]

        def remote(j, r):
            return pltpu.make_async_remote_copy(
                src_ref=g_ref.at[pl.ds(r, rows)], dst_ref=got_ref.at[pl.ds(r, rows)], send_sem=send_sems.at[j],
                recv_sem=recv_sems.at[j], device_id=(x, y, 1 - c), device_id_type=MESH)

        sends = [remote(j, r) for j, r in enumerate(give)]
        for cp in sends:
            cp.start()
        loads = [pltpu.make_async_copy(g_ref.at[pl.ds(r, rows)], mine_v.at[j], ld_sems.at[0, j])
                 for j, r in enumerate(keep)]
        for cp in loads:
            cp.start()
        stores = []
        for j, r in enumerate(keep):
            remote(j, r).wait_recv()
            cp = pltpu.make_async_copy(got_ref.at[pl.ds(r, rows)], got_v.at[j], ld_sems.at[1, j])
            cp.start()
            loads[j].wait()
            cp.wait()
            mine_v[j] = (mine_v[j].astype(F32) + got_v[j].astype(F32)).astype(BF16)
            st = pltpu.make_async_copy(mine_v.at[j], out_ref.at[pl.ds(r, rows)], st_sems.at[j])
            st.start()
            stores.append(st)
        for cp in sends:
            cp.wait_send()
        for cp in stores:
            cp.wait()

    out, _ = pl.pallas_call(
        body, name="pair_sum", in_specs=[ANY], out_specs=[ANY, ANY],
        out_shape=[jax.ShapeDtypeStruct(g.shape, g.dtype), jax.ShapeDtypeStruct(g.shape, g.dtype)],
        scratch_shapes=[pltpu.VMEM((n, rows, width), g.dtype), pltpu.VMEM((n, rows, width), g.dtype),
                        DMA((n,)), DMA((n,)), DMA((2, n)), DMA((n,))],
        compiler_params=pltpu.CompilerParams(vmem_limit_bytes=40 * 2 ** 20))(g)
    return out


def _groups_of(names):
    out = []
    for n in names:
        if LOC[n][0] not in out:
            out.append(LOC[n][0])
    return out


def _weight_pieces(name):
    g, off, rows = LOC[name]
    return [(d * GROUP_ROWS[g] + off, d * rows, rows) for d in range(NDEV)]


def _win_pieces():
    kv = N_KV * HEAD_DIM
    pieces = [(0, 0, D)]
    for g0, d0 in ((D, D), (D + kv, D + N_KV * LANES)):
        for g in range(N_KV):
            for half in range(2):
                pieces.append((g0 + g * HEAD_DIM, d0 + g * LANES + half * HEAD_DIM, HEAD_DIM))
    pieces.append((D + 2 * kv, D + 2 * N_KV * LANES, D))
    pieces.append((D + 2 * kv + D, D + 2 * N_KV * LANES + D, D))
    grp, off, rows = LOC["w_in"]
    out = []
    for g0, d0, n in pieces:
        while n > 0:
            dev, loc = divmod(g0, rows)
            m = min(n, rows - loc)
            out.append((dev * GROUP_ROWS[grp] + off + loc, d0, m))
            g0, d0, n = g0 + m, d0 + m, n - m
    return out


def _start_loads(src_ref, dst_ref, pieces, sems, base):
    cps = []
    for j, (s, d, n) in enumerate(pieces):
        cp = pltpu.make_async_copy(src_ref.at[pl.ds(s, n)], dst_ref.at[pl.ds(d, n)], sems.at[base + j])
        cp.start()
        cps.append(cp)
    return cps


def _load_weights(wrefs, targets, sems):
    cps, base = [], 0
    for name, dst in targets:
        pieces = _win_pieces() if name == "w_in" else _weight_pieces(name)
        cps += _start_loads(wrefs[LOC[name][0]], dst, pieces, sems, base)
        base += len(pieces)
    for cp in cps:
        cp.wait()


def _n_pieces(names):
    return sum(len(_win_pieces()) if n == "w_in" else NDEV for n in names)


def _dw(lhs, rhs, chunk, name, carries=()):
    nq = 4
    if lhs.ndim == 3:
        nch, t_tok, chunk = lhs.shape
        c = nch * chunk
        tq = t_tok // nq
        lhs_specs = [pl.BlockSpec((None, tq, chunk), lambda i, k=k: (i, k, 0)) for k in range(nq)]
    else:
        t_tok, c = lhs.shape
        tq = t_tok // nq
        lhs_specs = [pl.BlockSpec((tq, chunk), lambda i, k=k: (k, i)) for k in range(nq)]

    def body(*refs):
        lhs_refs, (rhs_ref, out_ref, rhs_s, sems) = refs[:nq], refs[nq:]
        cps = [pltpu.make_async_copy(rhs_ref.at[pl.ds(k * tq, tq)], rhs_s.at[pl.ds(k * tq, tq)], sems.at[k])
               for k in range(nq)]
        first = pl.program_id(0) == 0

        @pl.when(first)
        def _():
            for cp in cps:
                cp.start()

        for o in range(0, chunk, D // 2):
            n = min(D // 2, chunk - o)
            acc = None
            for k in range(nq):
                if o == 0:
                    @pl.when(first)
                    def _():
                        cps[k].wait()

                part = _tn(lhs_refs[k][:, pl.ds(o, n)], rhs_s[pl.ds(k * tq, tq), :])
                acc = part if acc is None else acc + part
            out_ref[pl.ds(o, n), :] = acc.astype(BF16)

    (out,), carried = _call(
        body, name=name, grid=(c // chunk,), ins=[lhs] * nq + [rhs],
        in_specs=lhs_specs + [ANY],
        out_specs=[pl.BlockSpec((chunk, D), lambda i: (i, 0))],
        out_shape=[jax.ShapeDtypeStruct((c, D), BF16)],
        scratch=[pltpu.VMEM((t_tok, D), BF16), DMA((nq,))], vmem=VMEM_BIG, carries=carries)
    return out, carried


def _silu_parts(a):
    sig = jax.nn.sigmoid(a)
    return sig, a * sig


FC = 256


def _ffn_fwd(h, gpre, gpost, wg, names, target=None, tm=512, carries=()):
    t_tok = h.shape[0]
    nt = t_tok // tm
    with_loss = target is not None
    groups = _groups_of(names)

    def body(*refs):
        refs = list(refs)
        h_ref, gpre_ref, gpost_ref = refs[:3]
        del refs[:3]
        tgt_ref = refs.pop(0) if with_loss else None
        wrefs = dict(zip(groups, refs[:len(groups)]))
        del refs[:len(groups)]
        hout_ref, a_ref, b_ref, f_ref, nb_ref = refs[:5]
        del refs[:5]
        if with_loss:
            dy_ref, loss_ref = refs[:2]
            del refs[:2]
        w1_s, w3_s, w2_s, sems = refs
        i = pl.program_id(0)

        @pl.when(i == 0)
        def _():
            _load_weights(wrefs, list(zip(names, (w1_s, w3_s, w2_s))), sems)
            if with_loss:
                loss_ref[...] = jnp.zeros_like(loss_ref)

        x = h_ref[...]
        n, _, _ = _rms_fwd(x, gpre_ref[...])
        nb = n.astype(BF16)
        nb_ref[...] = nb
        f = jnp.zeros((tm, D), F32)
        for c0 in range(0, FF, FC):
            a = _nt(nb, w1_s[pl.ds(c0, FC), :])
            b = _nt(nb, w3_s[pl.ds(c0, FC), :])
            _, sl = _silu_parts(a)
            a_ref[:, pl.ds(c0, FC)] = a.astype(BF16)
            b_ref[:, pl.ds(c0, FC)] = b.astype(BF16)
            f = f + _nn((sl * b).astype(BF16), w2_s[pl.ds(c0, FC), :])
        f_ref[...] = f
        fn, _, _ = _rms_fwd(f, gpost_ref[...])
        y = x + 0.5 * fn
        hout_ref[...] = y
        if with_loss:
            err = y - tgt_ref[...]
            dy_ref[...] = err * (1.0 / D)
            loss_ref[...] += jnp.sum(jnp.sum(err * err, axis=-1, keepdims=True), axis=0, keepdims=True) * (0.5 / D)

    ins = [h, gpre, gpost] + ([target] if with_loss else []) + [wg[g] for g in groups]
    in_specs = [_row_spec(tm, D), _const_spec((1, D)), _const_spec((1, D))]
    in_specs += ([_row_spec(tm, D)] if with_loss else []) + [ANY] * len(groups)
    out_shape = [jax.ShapeDtypeStruct((t_tok, D), F32), jax.ShapeDtypeStruct((t_tok, FF), BF16),
                 jax.ShapeDtypeStruct((t_tok, FF), BF16), jax.ShapeDtypeStruct((t_tok, D), F32),
                 jax.ShapeDtypeStruct((t_tok, D), BF16)]
    out_specs = [_row_spec(tm, D), _row_spec(tm, FF), _row_spec(tm, FF), _row_spec(tm, D), _row_spec(tm, D)]
    if with_loss:
        out_shape += [jax.ShapeDtypeStruct((t_tok, D), F32), jax.ShapeDtypeStruct((1, LANES), F32)]
        out_specs += [_row_spec(tm, D), _const_spec((1, LANES))]
    return _call(body, name="ffn_fwd_" + names[0][:4], grid=(nt,), ins=ins, in_specs=in_specs,
                 out_shape=out_shape, out_specs=out_specs,
                 scratch=[pltpu.VMEM((FF, D), BF16)] * 3 + [DMA((3 * NDEV,))], vmem=VMEM_BIG, carries=carries)


FH = FF // 2
HALF_PIECES = ((0, 256), (256, 256), (512, 256), (768, 256), (1024, 256), (1280, 128))


def _ffn_bwd_a(dh, f, a, b, gpost, wg, name_w2, tm=512, carries=()):
    t_tok = dh.shape[0]
    nt = t_tok // tm
    groups = _groups_of([name_w2])

    def body(dh_ref, f_ref, a_ref, b_ref, gpost_ref, wg_ref, dab_ref, s_ref, df_ref, dgp_ref, w2_s, sems):
        i = pl.program_id(0)

        @pl.when(i == 0)
        def _():
            _load_weights({groups[0]: wg_ref}, [(name_w2, w2_s)], sems)
            dgp_ref[...] = jnp.zeros_like(dgp_ref)

        fv = f_ref[...]
        _, fh, r = _rms_fwd(fv, gpost_ref[...])
        df, dg = _rms_bwd(0.5 * dh_ref[...], fh, r, gpost_ref[...])
        dgp_ref[...] += dg
        dfb = df.astype(BF16)
        df_ref[...] = dfb
        for half in range(2):
            for o, n in HALF_PIECES:
                c0 = half * FH + o
                ds = _nt(dfb, w2_s[pl.ds(c0, n), :])
                av = a_ref[:, pl.ds(c0, n)].astype(F32)
                bv = b_ref[:, pl.ds(c0, n)].astype(F32)
                sig, sl = _silu_parts(av)
                dab_ref[half, :, pl.ds(o, n)] = (ds * bv * (sig * (1.0 + av * (1.0 - sig)))).astype(BF16)
                dab_ref[2 + half, :, pl.ds(o, n)] = (ds * sl).astype(BF16)
                s_ref[half, :, pl.ds(o, n)] = (sl * bv).astype(BF16)

    return _call(
        body, name="ffn_bwd_a_" + name_w2[:4], grid=(nt,), ins=[dh, f, a, b, gpost, wg[groups[0]]],
        in_specs=[_row_spec(tm, D), _row_spec(tm, D), _row_spec(tm, FF), _row_spec(tm, FF), _const_spec((1, D)), ANY],
        out_specs=[pl.BlockSpec((4, tm, FH), lambda i: (0, i, 0)), pl.BlockSpec((2, tm, FH), lambda i: (0, i, 0)),
                   _row_spec(tm, D), _const_spec((1, D))],
        out_shape=[jax.ShapeDtypeStruct((4, t_tok, FH), BF16), jax.ShapeDtypeStruct((2, t_tok, FH), BF16),
                   jax.ShapeDtypeStruct((t_tok, D), BF16), jax.ShapeDtypeStruct((1, D), F32)],
        scratch=[pltpu.VMEM((FF, D), BF16), DMA((NDEV,))],
        vmem=VMEM_BIG, carries=carries)


def _ffn_bwd_dx(dab, h, dh, gpre, wg, name_w1, name_w3, tm=512, carries=()):
    t_tok = dh.shape[0]
    nt = t_tok // tm
    groups = _groups_of([name_w1, name_w3])

    def body(*refs):
        dab_refs, (h_ref, dh_ref, gpre_ref) = refs[:4], refs[4:7]
        wrefs = dict(zip(groups, refs[7:7 + len(groups)]))
        dhin_ref, dgp_ref, w13_s, sems = refs[7 + len(groups):]
        i = pl.program_id(0)

        @pl.when(i == 0)
        def _():
            _load_weights(wrefs, [(name_w1, w13_s.at[pl.ds(0, FF)]), (name_w3, w13_s.at[pl.ds(FF, FF)])], sems)
            dgp_ref[...] = jnp.zeros_like(dgp_ref)

        g = gpre_ref[...]
        _, xh, r = _rms_fwd(h_ref[...], g)
        dn = _nn(dab_refs[0][...], w13_s[pl.ds(0, FH), :])
        for k in range(1, 4):
            dn = dn + _nn(dab_refs[k][...], w13_s[pl.ds(k * FH, FH), :])
        dx, dg = _rms_bwd(dn, xh, r, g)
        dgp_ref[...] += dg
        dhin_ref[...] = dh_ref[...] + dx

    return _call(
        body, name="ffn_bwd_dx_" + name_w1[:4], grid=(nt,), ins=[dab] * 4 + [h, dh, gpre] + [wg[g] for g in groups],
        in_specs=[pl.BlockSpec((None, tm, FH), lambda i, k=k: (k, i, 0)) for k in range(4)]
        + [_row_spec(tm, D), _row_spec(tm, D), _const_spec((1, D))] + [ANY] * len(groups),
        out_specs=[_row_spec(tm, D), _const_spec((1, D))],
        out_shape=[jax.ShapeDtypeStruct((t_tok, D), F32), jax.ShapeDtypeStruct((1, D), F32)],
        scratch=[pltpu.VMEM((2 * FF, D), BF16), DMA((2 * NDEV,))],
        vmem=VMEM_BIG, carries=carries)


Q0, K0, V0, XR0, XG0 = 0, D, D + N_KV * LANES, D + 2 * N_KV * LANES, 2 * D + 2 * N_KV * LANES


def _mix_proj_fwd(h, g, bgate, wg, tm=512, carries=()):
    t_tok = h.shape[0]
    nt = t_tok // tm
    names = ("w_in", "w_gate")
    groups = _groups_of(names)

    def body(h_ref, g_ref, bg_ref, wg_ref, q_ref, k_ref, v_ref, xr_ref, xg_ref, gs_ref, ub_ref, win_s, wgt_s, sems):
        i = pl.program_id(0)

        @pl.when(i == 0)
        def _():
            _load_weights({groups[0]: wg_ref}, [("w_in", win_s), ("w_gate", wgt_s)], sems)

        n, _, _ = _rms_fwd(h_ref[...], g_ref[...])
        nb = n.astype(BF16)
        ub_ref[...] = nb
        q_ref[...] = _nt(nb, win_s[pl.ds(Q0, D), :]).astype(BF16)
        k_ref[...] = _nt(nb, win_s[pl.ds(K0, N_KV * LANES), :]).astype(BF16)
        v_ref[...] = _nt(nb, win_s[pl.ds(V0, N_KV * LANES), :]).astype(BF16)
        xr_ref[...] = _nt(nb, win_s[pl.ds(XR0, D), :])
        xg_ref[...] = _nt(nb, win_s[pl.ds(XG0, D), :])
        gs_ref[...] = jax.nn.sigmoid(_nt(nb, wgt_s[...]) + bg_ref[...]).astype(BF16)

    kvw = N_KV * LANES
    return _call(
        body, name="mix_proj_fwd", grid=(nt,), ins=[h, g, bgate, wg[groups[0]]],
        in_specs=[_row_spec(tm, D), _const_spec((1, D)), _const_spec((1, 2 * D)), ANY],
        out_specs=[_row_spec(tm, D), _row_spec(tm, kvw), _row_spec(tm, kvw), _row_spec(tm, D), _row_spec(tm, D),
                   _row_spec(tm, 2 * D), _row_spec(tm, D)],
        out_shape=[jax.ShapeDtypeStruct((t_tok, D), BF16), jax.ShapeDtypeStruct((t_tok, kvw), BF16),
                   jax.ShapeDtypeStruct((t_tok, kvw), BF16), jax.ShapeDtypeStruct((t_tok, D), F32),
                   jax.ShapeDtypeStruct((t_tok, D), F32), jax.ShapeDtypeStruct((t_tok, 2 * D), BF16),
                   jax.ShapeDtypeStruct((t_tok, D), BF16)],
        scratch=[pltpu.VMEM((INP_W, D), BF16), pltpu.VMEM((2 * D, D), BF16), DMA((_n_pieces(names),))],
        vmem=VMEM_BIG, carries=carries)


def _mix_proj_bwd(dqkv, dxr, dxg, dgpre, h, dh, g, wg, tm=512, carries=()):
    t_tok = h.shape[0]
    nt = t_tok // tm
    names = ("w_in", "w_gate")
    groups = _groups_of(names)

    def body(dqkv_ref, dxr_ref, dxg_ref, dgp_ref, h_ref, dh_ref, g_ref, wg_ref, dhin_ref, dg_ref, win_s, wgt_s, sems):
        i = pl.program_id(0)

        @pl.when(i == 0)
        def _():
            _load_weights({groups[0]: wg_ref}, [("w_in", win_s), ("w_gate", wgt_s)], sems)
            dg_ref[...] = jnp.zeros_like(dg_ref)

        gv = g_ref[...]
        _, xh, r = _rms_fwd(h_ref[...], gv)
        du = _nn(dgp_ref[...], wgt_s[...])
        du = du + _nn(dqkv_ref[...], win_s[pl.ds(Q0, XR0), :])
        du = du + _nn(dxr_ref[...], win_s[pl.ds(XR0, D), :])
        du = du + _nn(dxg_ref[...], win_s[pl.ds(XG0, D), :])
        dx, dg = _rms_bwd(du, xh, r, gv)
        dg_ref[...] += dg
        dhin_ref[...] = dh_ref[...] + dx

    return _call(
        body, name="mix_proj_bwd", grid=(nt,), ins=[dqkv, dxr, dxg, dgpre, h, dh, g, wg[groups[0]]],
        in_specs=[_row_spec(tm, XR0), _row_spec(tm, D), _row_spec(tm, D), _row_spec(tm, 2 * D), _row_spec(tm, D),
                  _row_spec(tm, D), _const_spec((1, D)), ANY],
        out_specs=[_row_spec(tm, D), _const_spec((1, D))],
        out_shape=[jax.ShapeDtypeStruct((t_tok, D), F32), jax.ShapeDtypeStruct((1, D), F32)],
        scratch=[pltpu.VMEM((INP_W, D), BF16), pltpu.VMEM((2 * D, D), BF16), DMA((_n_pieces(names),))],
        vmem=VMEM_BIG, carries=carries)


def _shift_down(x, d, fill):
    row = lax.broadcasted_iota(jnp.int32, x.shape, 0)
    return jnp.where(row >= d, pltpu.roll(x, d, 0), fill)


def _shift_up(x, d, fill):
    rows = x.shape[0]
    row = lax.broadcasted_iota(jnp.int32, x.shape, 0)
    return jnp.where(row < rows - d, pltpu.roll(x, rows - d, 0), fill)


def _scan_rows(a, b, reverse):
    rows = a.shape[0]
    d = 1
    while d < rows:
        if d < 8:
            shift = _shift_up if reverse else _shift_down
            b = a * shift(b, d, 0.0) + b
            a = a * shift(a, d, 1.0)
        elif reverse:
            b = jnp.concatenate([a[:rows - d] * b[d:] + b[:rows - d], b[rows - d:]], axis=0)
            a = jnp.concatenate([a[:rows - d] * a[d:], a[rows - d:]], axis=0)
        else:
            b = jnp.concatenate([b[:d], a[d:] * b[:rows - d] + b[d:]], axis=0)
            a = jnp.concatenate([a[:d], a[d:] * a[:rows - d]], axis=0)
        d *= 2
    return a, b


def _softplus(x):
    return jnp.maximum(x, 0.0) + jnp.log(1.0 + jnp.exp(-jnp.abs(x)))


_GELU_C = math.sqrt(2.0 / math.pi)


def _gelu_parts(x):
    th = jnp.tanh(_GELU_C * (x + 0.044715 * x * x * x))
    val = 0.5 * x * (1.0 + th)
    grad = 0.5 * (1.0 + th) + 0.5 * x * (1.0 - th * th) * _GELU_C * (1.0 + 3.0 * 0.044715 * x * x)
    return val, grad


def _lru_pre(x, halo, cw_ref, cb_ref, wa_ref, wx_ref, ba_ref, bx_ref, lam_ref):
    ext = jnp.concatenate([halo, x], axis=0)
    shifted = [x] + [pltpu.roll(ext, k, 0)[8:] for k in (1, 2, 3)]
    xc = cb_ref[...] + cw_ref[pl.ds(CONV_WIDTH - 1, 1), :] * x
    for k in (1, 2, 3):
        xc = xc + cw_ref[pl.ds(CONV_WIDTH - 1 - k, 1), :] * shifted[k]
    xcb = xc.astype(BF16)
    r = jax.nn.sigmoid(_nn(xcb, wa_ref[...]) + ba_ref[...])
    ig = jax.nn.sigmoid(_nn(xcb, wx_ref[...]) + bx_ref[...])
    sp = _softplus(-lam_ref[...])
    log_a = -LRU_C * r * sp
    a = jnp.exp(log_a)
    th = jnp.tanh(log_a)
    mult = jnp.sqrt(-2.0 * th / (1.0 - th))
    return shifted, xc, xcb, r, ig, sp, a, mult


def _lru_specs(nt, rows, reverse):
    def tt(t):
        return nt - 1 - t if reverse else t
    tile = pl.BlockSpec((rows, LANES), lambda cb, t: (tt(t), cb))
    halo = pl.BlockSpec((8, LANES), lambda cb, t: (jnp.maximum(tt(t) * (rows // 8) - 1, 0), cb))
    vec = pl.BlockSpec((1, LANES), lambda cb, t: (0, cb))
    cw = pl.BlockSpec((CONV_WIDTH, LANES), lambda cb, t: (0, cb))
    mat = pl.BlockSpec((None, LANES, LANES), lambda cb, t: (cb, 0, 0))
    return tile, halo, vec, cw, mat


def _lru_fwd(xr, xg, cw, cb, wa, wx, ba, bx, lam, carries=()):
    t_tok = xr.shape[0]
    rows = min(LRU_ROWS, t_tok)
    nt = t_tok // rows

    def body(xr_ref, xg_ref, cw_ref, cb_ref, wa_ref, wx_ref, ba_ref, bx_ref, lam_ref, y_ref, h_ref, tail_s, hc_s):
        t = pl.program_id(1)

        @pl.when(t == 0)
        def _():
            tail_s[...] = jnp.zeros_like(tail_s)
            hc_s[...] = jnp.zeros_like(hc_s)

        x = xr_ref[...]
        _, xc, _, _, ig, _, a, mult = _lru_pre(x, tail_s[...], cw_ref, cb_ref, wa_ref, wx_ref, ba_ref, bx_ref, lam_ref)
        tail_s[...] = xr_ref[pl.ds(rows - 8, 8), :]
        acc_a, acc_b = _scan_rows(a, mult * (ig * xc), False)
        hv = acc_b + acc_a * hc_s[...]
        h_ref[...] = hv
        hc_s[...] = h_ref[pl.ds(rows - 1, 1), :]
        gl, _ = _gelu_parts(xg_ref[...])
        y_ref[...] = (hv * gl).astype(BF16)

    tile, _, vec, cws, mat = _lru_specs(nt, rows, False)
    return _call(
        body, name="lru_fwd", grid=(D // LANES, nt), ins=[xr, xg, cw, cb, wa, wx, ba, bx, lam],
        in_specs=[tile, tile, cws, vec, mat, mat, vec, vec, vec],
        out_specs=[tile, tile],
        out_shape=[jax.ShapeDtypeStruct((t_tok, D), BF16), jax.ShapeDtypeStruct((t_tok, D), F32)],
        scratch=[pltpu.VMEM((8, LANES), F32), pltpu.VMEM((1, LANES), F32)], carries=carries)


def _lru_bwd(dy, xr, xg, hseq, cw, cb, wa, wx, ba, bx, lam, carries=()):
    t_tok = xr.shape[0]
    rows = min(LRU_ROWS, t_tok)
    nt = t_tok // rows

    def body(dy_ref, xr_ref, xrh_ref, xg_ref, h_ref, hh_ref, cw_ref, cb_ref, wa_ref, wx_ref, ba_ref, bx_ref, lam_ref,
             dxr_ref, dxg_ref, dvec_ref, dwa_ref, dwx_ref, gcar_s, acar_s, head_s, tmp_s):
        t = pl.program_id(1)
        first_tile = t == nt - 1

        @pl.when(t == 0)
        def _():
            gcar_s[...] = jnp.zeros_like(gcar_s)
            acar_s[...] = jnp.zeros_like(acar_s)
            head_s[...] = jnp.zeros_like(head_s)
            dvec_ref[...] = jnp.zeros_like(dvec_ref)
            dwa_ref[...] = jnp.zeros_like(dwa_ref)
            dwx_ref[...] = jnp.zeros_like(dwx_ref)

        x = xr_ref[...]
        halo = jnp.where(first_tile, 0.0, xrh_ref[...])
        shifted, xc, xcb, r, ig, sp, a, mult = _lru_pre(x, halo, cw_ref, cb_ref, wa_ref, wx_ref, ba_ref, bx_ref, lam_ref)
        hv = h_ref[...]
        dyv = dy_ref[...]
        gl, glg = _gelu_parts(xg_ref[...])
        dxg_ref[...] = (dyv * hv * glg).astype(BF16)
        acc_a, acc_b = _scan_rows(_shift_up(a, 1, acar_s[...]), dyv * gl, True)
        g = acc_b + acc_a * gcar_s[...]
        hhalo = jnp.where(first_tile, 0.0, hh_ref[...])
        hprev = pltpu.roll(jnp.concatenate([hhalo, hv], axis=0), 1, 0)[8:]
        dmult = g * ig * xc
        dlog_a = a * (g * hprev) - dmult * a * a / mult
        dig = g * mult * xc
        dxc = g * mult * ig
        dzr = (dlog_a * (-LRU_C * sp)) * r * (1.0 - r)
        dzx = dig * ig * (1.0 - ig)
        dzrb, dzxb = dzr.astype(BF16), dzx.astype(BF16)
        dxc = dxc + _nt(dzrb, wa_ref[...]) + _nt(dzxb, wx_ref[...])
        dwa_ref[...] += _tn(xcb, dzrb)
        dwx_ref[...] += _tn(xcb, dzxb)
        dsp = jnp.sum(dlog_a * (-LRU_C * r), axis=0, keepdims=True)
        dlam = dsp * (-jax.nn.sigmoid(-lam_ref[...]))
        vrow = lax.broadcasted_iota(jnp.int32, (8, LANES), 0)
        upd = jnp.where(vrow == 4, jnp.sum(dxc, axis=0, keepdims=True), 0.0)
        upd = jnp.where(vrow == 5, jnp.sum(dzr, axis=0, keepdims=True), upd)
        upd = jnp.where(vrow == 6, jnp.sum(dzx, axis=0, keepdims=True), upd)
        upd = jnp.where(vrow == 7, dlam, upd)
        for k in range(CONV_WIDTH):
            upd = jnp.where(vrow == CONV_WIDTH - 1 - k, jnp.sum(dxc * shifted[k], axis=0, keepdims=True), upd)
        dvec_ref[...] += upd
        ext = jnp.concatenate([dxc, head_s[...]], axis=0)
        dxr = cw_ref[pl.ds(CONV_WIDTH - 1, 1), :] * dxc
        for k in (1, 2, 3):
            dxr = dxr + cw_ref[pl.ds(CONV_WIDTH - 1 - k, 1), :] * pltpu.roll(ext, rows + 8 - k, 0)[:rows]
        dxr_ref[...] = dxr.astype(BF16)
        tmp_s[...] = g
        gcar_s[...] = tmp_s[pl.ds(0, 1), :]
        tmp_s[...] = a
        acar_s[...] = tmp_s[pl.ds(0, 1), :]
        tmp_s[...] = dxc
        head_s[...] = tmp_s[pl.ds(0, 8), :]

    tile, halo, vec, cws, mat = _lru_specs(nt, rows, True)
    return _call(
        body, name="lru_bwd", grid=(D // LANES, nt), ins=[dy, xr, xr, xg, hseq, hseq, cw, cb, wa, wx, ba, bx, lam],
        in_specs=[tile, tile, halo, tile, tile, halo, cws, vec, mat, mat, vec, vec, vec],
        out_specs=[tile, tile, pl.BlockSpec((8, LANES), lambda cb, t: (0, cb)), mat, mat],
        out_shape=[jax.ShapeDtypeStruct((t_tok, D), BF16), jax.ShapeDtypeStruct((t_tok, D), BF16),
                   jax.ShapeDtypeStruct((8, D), F32), jax.ShapeDtypeStruct((D // LANES, LANES, LANES), F32),
                   jax.ShapeDtypeStruct((D // LANES, LANES, LANES), F32)],
        scratch=[pltpu.VMEM((1, LANES), F32), pltpu.VMEM((1, LANES), F32), pltpu.VMEM((8, LANES), F32),
                 pltpu.VMEM((rows, LANES), F32)], carries=carries)


def _t5_bucket_np(rel):
    nb = N_BUCKETS // 2
    max_exact = nb // 2
    ret = np.where(rel > 0, nb, 0)
    n = np.abs(rel)
    nf = np.maximum(n, 1).astype(np.float32)
    large = max_exact + (np.log(nf / np.float32(max_exact)) / np.float32(math.log(MAX_DISTANCE / max_exact))
                         * np.float32(nb - max_exact)).astype(np.int32)
    large = np.minimum(large, nb - 1)
    return ret + np.where(n < max_exact, n, large)


def _bucket_map():
    r = np.arange(QT)[:, None]
    c = np.arange(KW)[None, :]
    j = c - (r // CHUNK) * CHUNK
    band = (j >= 0) & (j < WINDOW + CHUNK)
    return np.where(band, _t5_bucket_np(c - r - WINDOW), -1).astype(np.int32)


def _attn_specs(nt, reverse):
    def tt(i):
        return nt - 1 - i if reverse else i
    kvw = N_KV * LANES
    qs = pl.BlockSpec((QT, D), lambda i: (tt(i), 0))
    cur = pl.BlockSpec((QT, kvw), lambda i: (tt(i), 0))
    prev = pl.BlockSpec((WINDOW, kvw), lambda i: (jnp.maximum(tt(i) * (QT // WINDOW) - 1, 0), 0))
    lse = pl.BlockSpec((QT, LANES), lambda i: (tt(i), 0))
    return qs, cur, prev, lse


REP = N_HEADS // N_KV
SCALE = HEAD_DIM ** -0.5


def _stack_heads(x_ref, g, lo, scale=None):
    parts = []
    for hl in range(REP):
        xs = x_ref[:, pl.ds((2 * g + hl // 2) * LANES, LANES)]
        xs = jnp.where(lo if hl % 2 == 0 else jnp.logical_not(lo), xs, jnp.zeros_like(xs))
        parts.append(xs if scale is None else xs * jnp.asarray(scale, xs.dtype))
    return jnp.concatenate(parts, axis=0)


def _stack_sinks(sink_ref, g, srow):
    sk = jnp.full(srow.shape, sink_ref[REP * g + REP - 1], F32)
    for hl in range(REP - 2, -1, -1):
        sk = jnp.where(srow < (hl + 1) * QT, sink_ref[REP * g + hl], sk)
    return sk


def _attn_fwd(q, kd, vd, bias, sinks, carries=()):
    t_tok = q.shape[0]
    nt = t_tok // QT

    def body(q_ref, kp_ref, kc_ref, vp_ref, vc_ref, bias_ref, sink_ref, o_ref, lse_ref):
        i = pl.program_id(0)
        col = lax.broadcasted_iota(jnp.int32, (1, KW), 1)
        first = jnp.where((i == 0) & (col < WINDOW), NEG_INF, 0.0)
        lane = lax.broadcasted_iota(jnp.int32, (QT, LANES), 1)
        lo = lane < HEAD_DIM
        srow = lax.broadcasted_iota(jnp.int32, (REP * QT, 1), 0)
        lse_t = jnp.zeros((QT, LANES), F32)
        for g in range(N_KV):
            kwin = jnp.concatenate([kp_ref[:, pl.ds(g * LANES, LANES)], kc_ref[:, pl.ds(g * LANES, LANES)]], axis=0)
            vwin = jnp.concatenate([vp_ref[:, pl.ds(g * LANES, LANES)], vc_ref[:, pl.ds(g * LANES, LANES)]], axis=0)
            qst = _stack_heads(q_ref, g, lo, SCALE)
            s = _nt(qst, kwin) + (bias_ref[g] + first)
            sk = _stack_sinks(sink_ref, g, srow)
            m = jnp.maximum(jnp.max(s, axis=-1, keepdims=True), sk)
            e = jnp.exp(s - m)
            l = jnp.sum(e, axis=-1, keepdims=True) + jnp.exp(sk - m)
            p = e / l
            ost = _nn(p.astype(BF16), vwin)
            lse_s = m + jnp.log(l)
            for sl in range(2):
                o_ref[:, pl.ds((2 * g + sl) * LANES, LANES)] = jnp.where(
                    lo, ost[2 * sl * QT:(2 * sl + 1) * QT], ost[(2 * sl + 1) * QT:(2 * sl + 2) * QT]).astype(BF16)
            for hl in range(REP):
                lse_t = jnp.where(lane == REP * g + hl, lse_s[hl * QT:(hl + 1) * QT], lse_t)
        lse_ref[...] = lse_t

    qs, cur, prev, lse = _attn_specs(nt, False)
    return _call(
        body, name="attn_fwd", grid=(nt,), ins=[q, kd, kd, vd, vd, bias, sinks],
        in_specs=[qs, prev, cur, prev, cur, _const_spec((N_KV, REP * QT, KW)), pl.BlockSpec(memory_space=pltpu.SMEM)],
        out_specs=[qs, lse],
        out_shape=[jax.ShapeDtypeStruct((t_tok, D), BF16), jax.ShapeDtypeStruct((t_tok, LANES), F32)],
        vmem=48 * 2 ** 20, carries=carries)


def _attn_bwd(q, kd, vd, o, do, lse, bias, sinks, carries=()):
    t_tok = q.shape[0]
    nt = t_tok // QT
    kvw = N_KV * LANES

    def body(q_ref, kp_ref, kc_ref, vp_ref, vc_ref, o_ref, do_ref, lse_ref, bias_ref, sink_ref,
             dqkv_ref, ds_ref, dsink_ref, kcar_s, vcar_s):
        i = pl.program_id(0)
        tile = nt - 1 - i

        @pl.when(i == 0)
        def _():
            kcar_s[...] = jnp.zeros_like(kcar_s)
            vcar_s[...] = jnp.zeros_like(vcar_s)
            ds_ref[...] = jnp.zeros_like(ds_ref)
            dsink_ref[...] = jnp.zeros_like(dsink_ref)

        col = lax.broadcasted_iota(jnp.int32, (1, KW), 1)
        first = jnp.where((tile == 0) & (col < WINDOW), NEG_INF, 0.0)
        lane = lax.broadcasted_iota(jnp.int32, (QT, LANES), 1)
        lo = lane < HEAD_DIM
        lane_k = lax.broadcasted_iota(jnp.int32, (KW, LANES), 1)
        lane_1 = lax.broadcasted_iota(jnp.int32, (1, LANES), 1)
        srow = lax.broadcasted_iota(jnp.int32, (REP * QT, 1), 0)
        lse_t = lse_ref[...]
        dsink = jnp.zeros((1, LANES), F32)
        for g in range(N_KV):
            kwin = jnp.concatenate([kp_ref[:, pl.ds(g * LANES, LANES)], kc_ref[:, pl.ds(g * LANES, LANES)]], axis=0)
            vwin = jnp.concatenate([vp_ref[:, pl.ds(g * LANES, LANES)], vc_ref[:, pl.ds(g * LANES, LANES)]], axis=0)
            qst = _stack_heads(q_ref, g, lo, SCALE)
            dost = _stack_heads(do_ref, g, lo)
            od = [do_ref[:, pl.ds((2 * g + sl) * LANES, LANES)].astype(F32)
                  * o_ref[:, pl.ds((2 * g + sl) * LANES, LANES)].astype(F32) for sl in range(2)]
            drow = jnp.concatenate([jnp.sum(jnp.where(lo if hl % 2 == 0 else jnp.logical_not(lo), od[hl // 2], 0.0),
                                            axis=-1, keepdims=True) for hl in range(REP)], axis=0)
            lse_s = jnp.concatenate([jnp.sum(jnp.where(lane == REP * g + hl, lse_t, 0.0), axis=-1, keepdims=True)
                                     for hl in range(REP)], axis=0)
            s = _nt(qst, kwin) + (bias_ref[g] + first)
            p = jnp.exp(s - lse_s)
            ds = p * (_nt(dost, vwin) - drow)
            ds_ref[g] += ds
            tsink = -(jnp.exp(_stack_sinks(sink_ref, g, srow) - lse_s) * drow)
            for hl in range(REP):
                dsink = dsink + jnp.where(lane_1 == REP * g + hl,
                                          jnp.sum(tsink[hl * QT:(hl + 1) * QT], axis=0, keepdims=True), 0.0)
            dsb = ds.astype(BF16)
            dqst = _nn(dsb, kwin) * SCALE
            for sl in range(2):
                dqkv_ref[:, pl.ds((2 * g + sl) * LANES, LANES)] = jnp.where(
                    lo, dqst[2 * sl * QT:(2 * sl + 1) * QT], dqst[(2 * sl + 1) * QT:(2 * sl + 2) * QT]).astype(BF16)
            dk_acc = _tn(dsb, qst)
            dv_acc = _tn(p.astype(BF16), dost)
            dk_f = jnp.where(lane_k < HEAD_DIM, dk_acc + pltpu.roll(dk_acc, HEAD_DIM, 1), 0.0)
            dv_f = jnp.where(lane_k < HEAD_DIM, dv_acc + pltpu.roll(dv_acc, HEAD_DIM, 1), 0.0)
            for acc, col0, car in ((dk_f, K0, kcar_s), (dv_f, V0, vcar_s)):
                cs = pl.ds(g * LANES, LANES)
                co = pl.ds(col0 + g * LANES, LANES)
                if QT > WINDOW:
                    dqkv_ref[pl.ds(0, QT - WINDOW), co] = acc[WINDOW:QT].astype(BF16)
                dqkv_ref[pl.ds(QT - WINDOW, WINDOW), co] = (acc[QT:KW] + car[:, cs]).astype(BF16)
                car[:, cs] = acc[0:WINDOW]
        dsink_ref[...] += dsink

    qs, cur, prev, lse_s = _attn_specs(nt, True)
    return _call(
        body, name="attn_bwd", grid=(nt,), ins=[q, kd, kd, vd, vd, o, do, lse, bias, sinks],
        in_specs=[qs, prev, cur, prev, cur, qs, qs, lse_s, _const_spec((N_KV, REP * QT, KW)),
                  pl.BlockSpec(memory_space=pltpu.SMEM)],
        out_specs=[pl.BlockSpec((QT, XR0), lambda i: (nt - 1 - i, 0)), _const_spec((N_KV, REP * QT, KW)),
                   _const_spec((1, LANES))],
        out_shape=[jax.ShapeDtypeStruct((t_tok, XR0), BF16), jax.ShapeDtypeStruct((N_KV, REP * QT, KW), F32),
                   jax.ShapeDtypeStruct((1, LANES), F32)],
        scratch=[pltpu.VMEM((WINDOW, kvw), F32), pltpu.VMEM((WINDOW, kvw), F32)],
        vmem=VMEM_BIG, carries=carries)


def _bias_tile(table, bmap):
    def body(tab_ref, bm_ref, out_ref):
        bm = bm_ref[...]

        def per_head(hd, carry):
            acc = jnp.full((QT, KW), NEG_INF, F32)
            for b in range(N_BUCKETS):
                acc = jnp.where(bm == b, tab_ref[b, hd], acc)
            out_ref[hd] = acc
            return carry

        lax.fori_loop(0, N_HEADS, per_head, 0)

    return pl.pallas_call(
        body, name="bias_tile", out_shape=jax.ShapeDtypeStruct((N_HEADS, QT, KW), F32),
        in_specs=[pl.BlockSpec(memory_space=pltpu.SMEM), pl.BlockSpec(memory_space=pltpu.VMEM)],
        out_specs=pl.BlockSpec(memory_space=pltpu.VMEM))(table, bmap)


def _bias_grad(ds_acc, bmap):
    def body(ds_ref, bm_ref, out_ref):
        row = lax.broadcasted_iota(jnp.int32, (N_BUCKETS, LANES), 0)
        lane = lax.broadcasted_iota(jnp.int32, (N_BUCKETS, LANES), 1)
        bm = bm_ref[...]

        def per_head(hd, res):
            dsv = ds_ref[hd]
            for b in range(N_BUCKETS):
                val = jnp.sum(jnp.sum(jnp.where(bm == b, dsv, 0.0), axis=0, keepdims=True), axis=1, keepdims=True)
                res = jnp.where((row == b) & (lane == hd), val, res)
            return res

        out_ref[...] = lax.fori_loop(0, N_HEADS, per_head, jnp.zeros((N_BUCKETS, LANES), F32))

    return pl.pallas_call(
        body, name="bias_grad", out_shape=jax.ShapeDtypeStruct((N_BUCKETS, LANES), F32),
        in_specs=[pl.BlockSpec(memory_space=pltpu.VMEM), pl.BlockSpec(memory_space=pltpu.VMEM)],
        out_specs=pl.BlockSpec(memory_space=pltpu.VMEM))(ds_acc, bmap)


MIXOUT = ("w_lru_out", "w_attn_out", "w_o")


def _mix_out_fwd(ya_in, o, gs, h, g, wg, tm=512, carries=()):
    t_tok = h.shape[0]
    nt = t_tok // tm
    groups = _groups_of(MIXOUT)

    def body(ya_ref, o_ref, gs_ref, h_ref, g_ref, wg_ref, hout_ref, yao_ref, ybo_ref, z_ref, wa_s, wb_s, wo_s, sems):
        i = pl.program_id(0)

        @pl.when(i == 0)
        def _():
            _load_weights({groups[0]: wg_ref}, list(zip(MIXOUT, (wa_s, wb_s, wo_s))), sems)

        ya = _nn(ya_ref[...], wa_s[...])
        yb = _nn(o_ref[...], wb_s[...])
        yao_ref[...] = ya.astype(BF16)
        ybo_ref[...] = yb.astype(BF16)
        merged = gs_ref[:, pl.ds(0, D)].astype(F32) * ya + gs_ref[:, pl.ds(D, D)].astype(F32) * yb
        z = _nn(merged.astype(BF16), wo_s[...])
        z_ref[...] = z
        zn, _, _ = _rms_fwd(z, g_ref[...])
        hout_ref[...] = h_ref[...] + zn

    return _call(
        body, name="mix_out_fwd", grid=(nt,), ins=[ya_in, o, gs, h, g, wg[groups[0]]],
        in_specs=[_row_spec(tm, D), _row_spec(tm, D), _row_spec(tm, 2 * D), _row_spec(tm, D), _const_spec((1, D)), ANY],
        out_specs=[_row_spec(tm, D)] * 4,
        out_shape=[jax.ShapeDtypeStruct((t_tok, D), F32), jax.ShapeDtypeStruct((t_tok, D), BF16),
                   jax.ShapeDtypeStruct((t_tok, D), BF16), jax.ShapeDtypeStruct((t_tok, D), F32)],
        scratch=[pltpu.VMEM((D, D), BF16)] * 3 + [DMA((3 * NDEV,))],
        vmem=48 * 2 ** 20, carries=carries)


def _mix_out_bwd(dh, z, ya, yb, gs, g, wg, tm=512, carries=()):
    t_tok = dh.shape[0]
    nt = t_tok // tm
    groups = _groups_of(MIXOUT)

    def body(dh_ref, z_ref, ya_ref, yb_ref, gs_ref, g_ref, wg_ref,
             dyain_ref, do_ref, dgpre_ref, dya_ref, dyb_ref, mg_ref, dz_ref, dg_ref, dbg_ref,
             wa_s, wb_s, wo_s, sems):
        i = pl.program_id(0)

        @pl.when(i == 0)
        def _():
            _load_weights({groups[0]: wg_ref}, list(zip(MIXOUT, (wa_s, wb_s, wo_s))), sems)
            dg_ref[...] = jnp.zeros_like(dg_ref)
            dbg_ref[...] = jnp.zeros_like(dbg_ref)

        gv = g_ref[...]
        _, zh, r = _rms_fwd(z_ref[...], gv)
        dz, dg = _rms_bwd(dh_ref[...], zh, r, gv)
        dg_ref[...] += dg
        dzb = dz.astype(BF16)
        dz_ref[...] = dzb
        ga, gb = gs_ref[:, pl.ds(0, D)].astype(F32), gs_ref[:, pl.ds(D, D)].astype(F32)
        ya_v, yb_v = ya_ref[...].astype(F32), yb_ref[...].astype(F32)
        mg_ref[...] = (ga * ya_v + gb * yb_v).astype(BF16)
        dm = _nt(dzb, wo_s[...])
        dga = dm * ya_v * ga * (1.0 - ga)
        dgb = dm * yb_v * gb * (1.0 - gb)
        dgpre_ref[:, pl.ds(0, D)] = dga.astype(BF16)
        dgpre_ref[:, pl.ds(D, D)] = dgb.astype(BF16)
        dbg_ref[:, pl.ds(0, D)] += jnp.sum(dga, axis=0, keepdims=True)
        dbg_ref[:, pl.ds(D, D)] += jnp.sum(dgb, axis=0, keepdims=True)
        dya = (dm * ga).astype(BF16)
        dyb = (dm * gb).astype(BF16)
        dya_ref[...] = dya
        dyb_ref[...] = dyb
        dyain_ref[...] = _nt(dya, wa_s[...])
        do_ref[...] = _nt(dyb, wb_s[...]).astype(BF16)

    bf = jax.ShapeDtypeStruct((t_tok, D), BF16)
    return _call(
        body, name="mix_out_bwd", grid=(nt,), ins=[dh, z, ya, yb, gs, g, wg[groups[0]]],
        in_specs=[_row_spec(tm, D)] * 4 + [_row_spec(tm, 2 * D), _const_spec((1, D)), ANY],
        out_specs=[_row_spec(tm, D), _row_spec(tm, D), _row_spec(tm, 2 * D)] + [_row_spec(tm, D)] * 4
        + [_const_spec((1, D)), _const_spec((1, 2 * D))],
        out_shape=[jax.ShapeDtypeStruct((t_tok, D), F32), bf, jax.ShapeDtypeStruct((t_tok, 2 * D), BF16), bf, bf, bf, bf,
                   jax.ShapeDtypeStruct((1, D), F32), jax.ShapeDtypeStruct((1, 2 * D), F32)],
        scratch=[pltpu.VMEM((D, D), BF16)] * 3 + [DMA((3 * NDEV,))],
        vmem=VMEM_BIG, carries=carries)


def _sum_parts(parts_list):
    n = len(parts_list)
    _, r, c = parts_list[0].shape
    tc = 256

    def body(*refs):
        for p_ref, o_ref in zip(refs[:n], refs[n:]):
            acc = p_ref[0].astype(F32)
            for s in range(1, NDEV):
                acc = acc + p_ref[s].astype(F32)
            o_ref[...] = acc

    return pl.pallas_call(
        body, name=f"sum_parts_{r}", grid=(c // tc,),
        in_specs=[pl.BlockSpec((NDEV, r, tc), lambda i: (0, 0, i))] * n,
        out_specs=[pl.BlockSpec((r, tc), lambda i: (0, i))] * n,
        out_shape=[jax.ShapeDtypeStruct((r, c), F32)] * n,
        compiler_params=pltpu.CompilerParams(dimension_semantics=("arbitrary",), vmem_limit_bytes=48 * 2 ** 20),
    )(*parts_list)


def _adamw_math(w, g, m, v):
    m = ADAM_B1 * m + (1.0 - ADAM_B1) * g
    v = ADAM_B2 * v + (1.0 - ADAM_B2) * (g * g)
    m_hat = m / (1.0 - ADAM_B1 ** ADAM_STEP)
    v_hat = v / (1.0 - ADAM_B2 ** ADAM_STEP)
    delta = -ADAM_LR * (m_hat / (jnp.sqrt(v_hat) + ADAM_EPS) + ADAM_WD * w)
    return delta, m, v


def _sum_ready(parts_list, name, carries=(), chip_sums=()):
    n = len(parts_list)
    c = parts_list[0].shape[2]
    tc = 2 * LANES

    def body(*refs):
        for k, (p_ref, o_ref) in enumerate(zip(refs[:n], refs[n:])):
            if k in chip_sums:
                core = lax.axis_index("c")
                slots = [2 * ch + core for ch in range(NDEV // 2)]
            else:
                slots = list(range(NDEV))
            g = p_ref[slots[0]].astype(F32)
            for s in slots[1:]:
                g = g + p_ref[s].astype(F32)
            o_ref[...] = g

    return _call(
        body, name=name, grid=(c // tc,), ins=list(parts_list),
        in_specs=[pl.BlockSpec((NDEV, p.shape[1], tc), lambda i: (0, 0, i)) for p in parts_list],
        out_shape=[jax.ShapeDtypeStruct((p.shape[1], c), F32) for p in parts_list],
        out_specs=[pl.BlockSpec((p.shape[1], tc), lambda i: (0, i)) for p in parts_list],
        vmem=48 * 2 ** 20, carries=carries)


def _adamw_cols(items, name):
    n = len(items)
    c = items[0][1].shape[1]
    tc = LANES

    def body(*refs):
        for k in range(n):
            g_ref, w_ref, m_ref, v_ref = refs[4 * k:4 * k + 4]
            go_ref, d_ref, nm_ref, nv_ref = refs[4 * n + 4 * k:4 * n + 4 * k + 4]
            g = g_ref[...]
            d, m, v = _adamw_math(w_ref[...], g, m_ref[...], v_ref[...])
            go_ref[...] = g
            d_ref[...] = d
            nm_ref[...] = m
            nv_ref[...] = v

    ins, specs, out_shape = [], [], []
    for g, w, m, v in items:
        r = w.shape[0]
        ins += [g, w, m, v]
        specs += [pl.BlockSpec((r, tc), lambda i: (0, i))] * 4
        out_shape += [jax.ShapeDtypeStruct((r, c), F32)] * 4
    outs, _ = _call(body, name=name, grid=(c // tc,), ins=ins, in_specs=specs, out_shape=out_shape,
                    out_specs=specs, vmem=48 * 2 ** 20)
    return [tuple(outs[4 * k:4 * k + 4]) for k in range(n)]


def _adamw_body(n):
    def body(*refs):
        for k in range(n):
            g_ref, w_ref, m_ref, v_ref = refs[4 * k:4 * k + 4]
            d_ref, nm_ref, nv_ref = refs[4 * n + 3 * k:4 * n + 3 * k + 3]
            d, m, v = _adamw_math(w_ref[...], g_ref[...], m_ref[...], v_ref[...])
            d_ref[...] = d
            nm_ref[...] = m
            nv_ref[...] = v
    return body


def _adamw(items):
    n = len(items)
    r, c = items[0][0].shape
    tr = r if r * c <= 2 ** 18 else max(t for t in range(8, 65, 8) if r % t == 0)
    spec = pl.BlockSpec((tr, c), lambda i: (i, 0))
    outs = pl.pallas_call(
        _adamw_body(n), name=f"adamw_{r}x{c}", grid=(r // tr,),
        in_specs=[spec] * (4 * n), out_specs=[spec] * (3 * n),
        out_shape=[jax.ShapeDtypeStruct((r, c), F32)] * (3 * n),
        compiler_params=pltpu.CompilerParams(dimension_semantics=("arbitrary",), vmem_limit_bytes=40 * 2 ** 20),
    )(*[a for it in items for a in it])
    return [tuple(outs[3 * k:3 * k + 3]) for k in range(n)]


def _adamw_small(items):
    n = len(items)
    vm = pl.BlockSpec(memory_space=pltpu.VMEM)
    outs = pl.pallas_call(
        _adamw_body(n), name="adamw_small", in_specs=[vm] * (4 * n), out_specs=[vm] * (3 * n),
        out_shape=[jax.ShapeDtypeStruct(it[1].shape, F32) for it in items for _ in range(3)],
    )(*[a for it in items for a in it])
    return [tuple(outs[3 * k:3 * k + 3]) for k in range(n)]


def _pack_small(arrs):
    rows, offs = [], []
    total = 0
    for a in arrs:
        flat = a.reshape(-1).astype(F32)
        nr = -(-flat.shape[0] // LANES)
        flat = jnp.pad(flat, (0, nr * LANES - flat.shape[0]))
        rows.append(flat.reshape(nr, LANES))
        offs.append((total, nr))
        total += nr
    pad = -total % 8
    if pad:
        rows.append(jnp.zeros((pad, LANES), F32))
    return jnp.concatenate(rows, axis=0), offs


def _unpack_small(pack, offs, shapes):
    out = []
    for (o, nr), shp in zip(offs, shapes):
        size = int(np.prod(shp))
        out.append(pack[o:o + nr].reshape(-1)[:size].reshape(shp))
    return out


def _sum_small(gathered, rows):
    def body(p_ref, o_ref):
        acc = p_ref[pl.ds(0, rows), :]
        for s in range(1, NDEV):
            acc = acc + p_ref[pl.ds(s * rows, rows), :]
        o_ref[...] = acc

    return pl.pallas_call(
        body, name="sum_small", out_shape=jax.ShapeDtypeStruct((rows, LANES), F32),
        in_specs=[pl.BlockSpec(memory_space=pltpu.VMEM)], out_specs=pl.BlockSpec(memory_space=pltpu.VMEM))(gathered)


def _block_diag(w):
    w = w.reshape(D // LANES, 2, LRU_BLOCK, LRU_BLOCK)
    z = jnp.zeros((D // LANES, LRU_BLOCK, LRU_BLOCK), w.dtype)
    top = jnp.concatenate([w[:, 0], z], axis=2)
    bot = jnp.concatenate([z, w[:, 1]], axis=2)
    return jnp.concatenate([top, bot], axis=1)


def _block_diag_grad(dw):
    a = dw[:, :LRU_BLOCK, :LRU_BLOCK]
    b = dw[:, LRU_BLOCK:, LRU_BLOCK:]
    return jnp.stack([a, b], axis=1).reshape(D // LRU_BLOCK, LRU_BLOCK, LRU_BLOCK)


def kernel(x, ffn1_pre_g, ffn1_w1, ffn1_w3, ffn1_w2, ffn1_post_g, mix_pre_g, w_in, conv_w, conv_b, rg_a_w, rg_a_b, rg_x_w, rg_x_b, lru_lambda, w_lru_out, attn_sinks, rel_bias, w_attn_out, w_gate, b_gate, w_o, mix_post_g, ffn2_pre_g, ffn2_w1, ffn2_w3, ffn2_w2, ffn2_post_g, loss_target, m_ffn1_pre_g, m_ffn1_w1, m_ffn1_w3, m_ffn1_w2, m_ffn1_post_g, m_mix_pre_g, m_w_in, m_conv_w, m_conv_b, m_rg_a_w, m_rg_a_b, m_rg_x_w, m_rg_x_b, m_lru_lambda, m_w_lru_out, m_attn_sinks, m_rel_bias, m_w_attn_out, m_w_gate, m_b_gate, m_w_o, m_mix_post_g, m_ffn2_pre_g, m_ffn2_w1, m_ffn2_w3, m_ffn2_w2, m_ffn2_post_g, v_ffn1_pre_g, v_ffn1_w1, v_ffn1_w3, v_ffn1_w2, v_ffn1_post_g, v_mix_pre_g, v_w_in, v_conv_w, v_conv_b, v_rg_a_w, v_rg_a_b, v_rg_x_w, v_rg_x_b, v_lru_lambda, v_w_lru_out, v_attn_sinks, v_rel_bias, v_w_attn_out, v_w_gate, v_b_gate, v_w_o, v_mix_post_g, v_ffn2_pre_g, v_ffn2_w1, v_ffn2_w3, v_ffn2_w2, v_ffn2_post_g):
    names = ["ffn1_pre_g", "ffn1_w1", "ffn1_w3", "ffn1_w2", "ffn1_post_g", "mix_pre_g", "w_in", "conv_w", "conv_b",
             "rg_a_w", "rg_a_b", "rg_x_w", "rg_x_b", "lru_lambda", "w_lru_out", "attn_sinks", "rel_bias", "w_attn_out",
             "w_gate", "b_gate", "w_o", "mix_post_g", "ffn2_pre_g", "ffn2_w1", "ffn2_w3", "ffn2_w2", "ffn2_post_g"]
    ws = dict(zip(names, (ffn1_pre_g, ffn1_w1, ffn1_w3, ffn1_w2, ffn1_post_g, mix_pre_g, w_in, conv_w, conv_b, rg_a_w,
                          rg_a_b, rg_x_w, rg_x_b, lru_lambda, w_lru_out, attn_sinks, rel_bias, w_attn_out, w_gate,
                          b_gate, w_o, mix_post_g, ffn2_pre_g, ffn2_w1, ffn2_w3, ffn2_w2, ffn2_post_g)))
    ms = dict(zip(names, (m_ffn1_pre_g, m_ffn1_w1, m_ffn1_w3, m_ffn1_w2, m_ffn1_post_g, m_mix_pre_g, m_w_in, m_conv_w,
                          m_conv_b, m_rg_a_w, m_rg_a_b, m_rg_x_w, m_rg_x_b, m_lru_lambda, m_w_lru_out, m_attn_sinks,
                          m_rel_bias, m_w_attn_out, m_w_gate, m_b_gate, m_w_o, m_mix_post_g, m_ffn2_pre_g, m_ffn2_w1,
                          m_ffn2_w3, m_ffn2_w2, m_ffn2_post_g)))
    vs = dict(zip(names, (v_ffn1_pre_g, v_ffn1_w1, v_ffn1_w3, v_ffn1_w2, v_ffn1_post_g, v_mix_pre_g, v_w_in, v_conv_w,
                          v_conv_b, v_rg_a_w, v_rg_a_b, v_rg_x_w, v_rg_x_b, v_lru_lambda, v_w_lru_out, v_attn_sinks,
                          v_rel_bias, v_w_attn_out, v_w_gate, v_b_gate, v_w_o, v_mix_post_g, v_ffn2_pre_g, v_ffn2_w1,
                          v_ffn2_w3, v_ffn2_w2, v_ffn2_post_g)))
    me = 4 * lax.axis_index("x") + 2 * lax.axis_index("y") + lax.axis_index("c")
    vec = lambda n: ws[n].reshape(1, -1)

    def shard2d(name):
        if name == "conv":
            row = lax.bitcast_convert_type(conv_w.reshape(CONV_WIDTH, LANES), BF16).reshape(1, D)
            return jnp.concatenate([row, jnp.zeros((LOC["conv"][2] - 1, D), BF16)], axis=0)
        a = ws[name].reshape(ws[name].shape[-2], ws[name].shape[-1])
        return (a.T if name in COL_SHARDED else a).astype(BF16)

    packs = {g: jnp.concatenate([shard2d(n) for n, _ in members], axis=0) for g, members in GROUPS}

    wg = {}
    _, ((wg["ffn1"],),) = _call(None, name="allgather_ffn1", grid=(), ins=[], in_specs=[], out_shape=[], out_specs=[],
                                carries=[_AllGatherTree(packs["ffn1"])])
    conv_rows = wg["ffn1"].reshape(NDEV, GROUP_ROWS["ffn1"], D)[:, LOC["conv"][1]]
    cw = jnp.transpose(lax.bitcast_convert_type(conv_rows.reshape(NDEV, CONV_WIDTH, LANES, 2), F32),
                       (1, 0, 2)).reshape(CONV_WIDTH, D)
    bmap = jnp.asarray(_bucket_map())
    bias = _bias_tile(rel_bias, bmap).reshape(N_KV, REP * QT, KW)
    sinks = attn_sinks.reshape(N_HEADS)
    wa_bd = _block_diag(rg_a_w.reshape(D // LRU_BLOCK, LRU_BLOCK, LRU_BLOCK)).astype(BF16)
    wx_bd = _block_diag(rg_x_w.reshape(D // LRU_BLOCK, LRU_BLOCK, LRU_BLOCK)).astype(BF16)
    lru_args = (cw, vec("conv_b"), wa_bd, wx_bd, vec("rg_a_b"), vec("rg_x_b"), vec("lru_lambda"))
    x2, tgt = x[0], loss_target[0]

    (h1, a1, b1, f1, nb1), ((wg["mixin"],),) = _ffn_fwd(
        x2, vec("ffn1_pre_g"), vec("ffn1_post_g"), wg, ("ffn1_w1", "ffn1_w3", "ffn1_w2"),
        carries=[_AllGather(packs["mixin"])])
    (q, kd, vd, xr, xg, gs, ub), ((wg["mixout"],),) = _mix_proj_fwd(
        h1, vec("mix_pre_g"), vec("b_gate"), wg, carries=[_AllGather(packs["mixout"])])
    (ya_in, hseq), ((wg["ffn2a"],),) = _lru_fwd(xr, xg, *lru_args, carries=[_AllGather(packs["ffn2a"])])
    (o, lse), ((wg["ffn2b"],),) = _attn_fwd(q, kd, vd, bias, sinks, carries=[_AllGather(packs["ffn2b"])])
    (h2, ya, yb, z), _ = _mix_out_fwd(ya_in, o, gs, h1, vec("mix_post_g"), wg)
    (_, a2, b2, f2, nb2, dy, loss_part), _ = _ffn_fwd(h2, vec("ffn2_pre_g"), vec("ffn2_post_g"), wg,
                                                     ("ffn2_w1", "ffn2_w3", "ffn2_w2"), target=tgt)

    gsm, parts = {}, {}
    rs = lambda *grads_: [_ReduceScatterSend(list(grads_))]
    ffr = FF // NDEV
    (dab, s_act, dfb, gsm["ffn2_post_g"]), _ = _ffn_bwd_a(dy, f2, a2, b2, vec("ffn2_post_g"), wg, "ffn2_w2")
    g_w2, _ = _dw(s_act, dfb, FF // 2, "dw_ffn2_w2")
    g_w13, ((parts["ffn2_w2"],),) = _dw(dab, nb2, D // 2, "dw_ffn2_w13", carries=rs(g_w2))
    (dh2, gsm["ffn2_pre_g"]), ((parts["ffn2_w1"],),) = _ffn_bwd_dx(
        dab, h2, dy, vec("ffn2_pre_g"), wg, "ffn2_w1", "ffn2_w3", carries=rs((g_w13, 0, ffr, ffr)))
    (dya_in, do, dgpre, dya, dyb, mg, dzb, gsm["mix_post_g"], gsm["b_gate"]), ((parts["ffn2_w3"],),) = _mix_out_bwd(
        dh2, z, ya, yb, gs, vec("mix_post_g"), wg, carries=rs((g_w13, FF, ffr, ffr)))
    g_wa, _ = _dw(ya_in, dya, D // 2, "dw_w_lru_out")
    g_wb, _ = _dw(o, dyb, D // 2, "dw_w_attn_out")
    g_wo, _ = _dw(mg, dzb, D // 2, "dw_w_o")
    g_wgate, _ = _dw(dgpre, ub, D // 2, "dw_w_gate")
    (dqkv, ds_acc, dsink), (mixout_parts,) = _attn_bwd(
        q, kd, vd, o, do, lse, bias, sinks, carries=rs(g_wa, g_wb, g_wo, g_wgate))
    parts["w_lru_out"], parts["w_attn_out"], parts["w_o"], parts["w_gate"] = mixout_parts
    (dxr, dxg, dvec, dwa, dwx), _ = _lru_bwd(dya_in, xr, xg, hseq, *lru_args)
    g_qkv, _ = _dw(dqkv, ub, D // 2, "dw_w_in_qkv")
    g_xr, _ = _dw(dxr, ub, D // 2, "dw_w_in_xr")
    g_xg, _ = _dw(dxg, ub, D // 2, "dw_w_in_xg")
    g_win = jnp.concatenate(
        [g_qkv[:D]] + [g_qkv[c0 + gi * LANES:c0 + gi * LANES + HEAD_DIM] for c0 in (K0, V0) for gi in range(N_KV)]
        + [g_xr, g_xg], axis=0)
    wir, wir_a = IN_W // NDEV, 336
    (dh1, gsm["mix_pre_g"]), ((win_a,),) = _mix_proj_bwd(
        dqkv, dxr, dxg, dgpre, h1, dh2, vec("mix_pre_g"), wg, carries=rs((g_win, 0, wir, wir_a)))
    gsm["conv_w"] = dvec[0:CONV_WIDTH]
    gsm["conv_b"], gsm["rg_a_b"], gsm["rg_x_b"], gsm["lru_lambda"] = dvec[4], dvec[5], dvec[6], dvec[7]
    gsm["rg_a_w"] = _block_diag_grad(dwa)
    gsm["rg_x_w"] = _block_diag_grad(dwx)
    gsm["attn_sinks"] = dsink[0, :N_HEADS]
    gsm["rel_bias"] = _bias_grad(ds_acc.reshape(N_HEADS, QT, KW), bmap)[:, :N_HEADS]
    late = ("ffn1_post_g", "ffn1_pre_g")
    early = tuple(n for n in SMALL if n not in late)
    early_pack, early_offs = _pack_small([gsm[n] for n in early])
    (dab, s_act, dfb, gsm["ffn1_post_g"]), ((win_b,), (early_all,)) = _ffn_bwd_a(
        dh1, f1, a1, b1, vec("ffn1_post_g"), wg, "ffn1_w2",
        carries=rs((g_win, wir_a, wir, wir - wir_a)) + [_AllGather(early_pack)])
    parts["w_in"] = jnp.concatenate([win_a, win_b], axis=1)
    g_w2, _ = _dw(s_act, dfb, FF // 2, "dw_ffn1_w2")
    g_w13, ((parts["ffn1_w2"],),) = _dw(dab, nb1, D // 2, "dw_ffn1_w13", carries=rs(g_w2))
    chip_w13 = _pair_sum(g_w13, [(0, ffr), (FF, ffr)])
    (grad_x, gsm["ffn1_pre_g"]), ((parts["ffn1_w1"], parts["ffn1_w3"]),) = _ffn_bwd_dx(
        dab, x2, dh1, vec("ffn1_pre_g"), wg, "ffn1_w1", "ffn1_w3",
        carries=[_ReduceScatterSend([(chip_w13, 0, ffr, ffr), (chip_w13, FF, ffr, ffr)], chip_sums=True)])
    late_pack, late_offs = _pack_small([gsm[n] for n in late] + [loss_part])

    grads, delta, new_m, new_v = {}, {}, {}, {}
    t_form = [n for n in COL_SHARDED if ws[n].shape[-1] % LANES]
    view = {n: (lambda a: a.reshape(a.shape[-2], a.shape[-1]).T) if n in t_form
            else (lambda a: a.reshape(a.shape[-2], a.shape[-1])) for n in BIG}
    unview = {n: (lambda a: a.T) if n in t_form else (lambda a: a) for n in BIG}

    group = [n for n in BIG if n != "w_gate"]
    sums, ((late_all,),) = _sum_ready([parts[n] for n in group], "sum_ready", [_AllGather(late_pack)],
                                      chip_sums=(group.index("ffn1_w1"), group.index("ffn1_w3")))
    res = _adamw_cols([(g, view[n](ws[n]), view[n](ms[n]), view[n](vs[n])) for n, g in zip(group, sums)], "adamw_big")
    for n, quad in zip(group, res):
        grads[n], delta[n], new_m[n], new_v[n] = (unview[n](a) for a in quad)
    (g_gate_t,) = _sum_parts([parts["w_gate"]])
    grads["w_gate"] = g_gate_t.T
    ((delta["w_gate"], new_m["w_gate"], new_v["w_gate"]),) = _adamw(
        [(grads["w_gate"], view["w_gate"](ws["w_gate"]), view["w_gate"](ms["w_gate"]), view["w_gate"](vs["w_gate"]))])
    for group, gathered, offs in ((early, early_all, early_offs), (late, late_all, late_offs)):
        total = _sum_small(gathered, gathered.shape[0] // NDEV)
        shapes = [(CONV_WIDTH, D) if n == "conv_w" else ws[n].shape for n in group]
        if group is late:
            shapes = shapes + [(1, LANES)]
        unpacked = _unpack_small(total, offs, shapes)
        if group is late:
            loss = unpacked.pop()[0, 0]
        for n, g in zip(group, unpacked):
            grads[n] = g
    grads["conv_w"] = lax.dynamic_slice(grads["conv_w"], (0, me * LANES), (CONV_WIDTH, LANES)).reshape(conv_w.shape)

    flat2d = lambda a: a.reshape(-1, a.shape[-1])
    res = _adamw_small([(flat2d(grads[n].reshape(ws[n].shape)), flat2d(ws[n]), flat2d(ms[n]), flat2d(vs[n]))
                        for n in SMALL])
    for n, (d_, m_, v_) in zip(SMALL, res):
        delta[n], new_m[n], new_v[n] = d_, m_, v_

    outs = [loss, grad_x.reshape(x.shape)]
    for src in (grads, delta, new_m, new_v):
        outs += [src[n].reshape(ws[n].shape) for n in names]
    return tuple(outs)
```

```python
import functools
import math
import operator

import numpy as np
import jax
import jax.numpy as jnp
from jax import lax
from jax.experimental import pallas as pl
from jax.experimental.pallas import tpu as pltpu

F32, BF16 = jnp.float32, jnp.bfloat16

NDEV = 8
D = 1024
FF = 2816
N_HEADS, N_KV, HEAD_DIM = 16, 4, 64
CHUNK, WINDOW = 64, 128
N_BUCKETS, MAX_DISTANCE = 32, 128
LRU_BLOCK = 64
CONV_WIDTH = 4
LRU_C = 8.0
RMS_EPS = 1e-6
NEG_INF = -1e30
LANES = 128
QT = 128
KW = QT + WINDOW
LRU_ROWS = 1024
IN_W = D + 2 * N_KV * HEAD_DIM + 2 * D
INP_W = D + 2 * N_KV * LANES + 2 * D
VMEM_BIG = 58 * 2 ** 20

ADAM_LR, ADAM_B1, ADAM_B2, ADAM_EPS, ADAM_WD, ADAM_STEP = 0.001, 0.9, 0.999, 1e-08, 0.01, 10

GROUPS = (("ffn1", (("ffn1_w1", FF // NDEV), ("ffn1_w3", FF // NDEV), ("ffn1_w2", FF // NDEV), ("conv", 16))),
          ("mixin", (("w_in", IN_W // NDEV), ("w_gate", 2 * D // NDEV))),
          ("mixout", (("w_lru_out", D // NDEV), ("w_attn_out", D // NDEV), ("w_o", D // NDEV))),
          ("ffn2a", (("ffn2_w1", FF // NDEV), ("ffn2_w3", FF // NDEV))),
          ("ffn2b", (("ffn2_w2", FF // NDEV),)))
LOC, GROUP_ROWS = {}, {}
for _g, _members in GROUPS:
    _o = 0
    for _n, _r in _members:
        LOC[_n] = (_g, _o, _r)
        _o += _r
    GROUP_ROWS[_g] = _o
BIG = tuple(n for _, members in GROUPS for n, _ in members if n != "conv")
COL_SHARDED = ("ffn1_w1", "ffn1_w3", "w_in", "w_gate", "ffn2_w1", "ffn2_w3")

SMALL = ("ffn1_pre_g", "ffn1_post_g", "mix_pre_g", "conv_w", "conv_b", "rg_a_w", "rg_a_b", "rg_x_w", "rg_x_b",
         "lru_lambda", "attn_sinks", "rel_bias", "b_gate", "mix_post_g", "ffn2_pre_g", "ffn2_post_g")

MESH = pl.DeviceIdType.MESH
ANY = pl.BlockSpec(memory_space=pl.ANY)
DMA = pltpu.SemaphoreType.DMA


def _nn(a, b):
    return lax.dot_general(a, b, (((1,), (0,)), ((), ())), preferred_element_type=F32)


def _nt(a, b):
    return lax.dot_general(a, b, (((1,), (1,)), ((), ())), preferred_element_type=F32)


def _tn(a, b):
    return lax.dot_general(a, b, (((0,), (0,)), ((), ())), preferred_element_type=F32)


def _rms_fwd(x, g):
    r = lax.rsqrt(jnp.mean(x * x, axis=-1, keepdims=True) + RMS_EPS)
    xh = x * r
    return xh * g, xh, r


def _rms_bwd(dn, xh, r, g):
    dxh = dn * g
    dx = r * (dxh - xh * jnp.mean(dxh * xh, axis=-1, keepdims=True))
    return dx, jnp.sum(dn * xh, axis=0, keepdims=True)


def _row_spec(tm, c):
    return pl.BlockSpec((tm, c), lambda i: (i, 0))


def _const_spec(shape):
    nd = len(shape)
    return pl.BlockSpec(shape, lambda i: (0,) * nd)


class _AllGather:
    def __init__(self, shard):
        self.m, n = shard.shape
        self.ins = [shard]
        self.out_shape = [jax.ShapeDtypeStruct((NDEV * self.m, n), shard.dtype)]
        self.scratch = [DMA((7,)), DMA((7,)), DMA]

    def _copies(self, ins, outs, scr, all_of_them):
        x_ref, out_ref = ins[0], outs[0]
        send_sems, recv_sems, local_sem = scr
        x, y, c = lax.axis_index("x"), lax.axis_index("y"), lax.axis_index("c")
        me, sibling = (x, y, c), (x, y, 1 - c)
        chips = [(1 - x, y), (x, 1 - y), (1 - x, 1 - y)]
        m = self.m

        def rows(px, py, pc):
            return out_ref.at[pl.ds((4 * px + 2 * py + pc) * m, m), :]

        def copy(k, block, to, src=None):
            return pltpu.make_async_remote_copy(
                src_ref=rows(*block) if src is None else src, dst_ref=rows(*block),
                send_sem=send_sems.at[k], recv_sem=recv_sems.at[k], device_id=to, device_id_type=MESH)

        mine = pltpu.make_async_copy(x_ref, rows(*me), local_sem)
        first = [copy(0, me, sibling, src=x_ref)] + [copy(1 + j, me, (*chip, c), src=x_ref)
                                                     for j, chip in enumerate(chips)]
        if not all_of_them:
            return mine, first
        passed = [copy(4 + j, (*chip, c), sibling) for j, chip in enumerate(chips)]
        landed = [copy(1 + j, (*chip, c), me) for j, chip in enumerate(chips)]
        from_sibling = [copy(0, sibling, me)] + [copy(4 + j, (*chip, 1 - c), me) for j, chip in enumerate(chips)]
        return mine, first, passed, landed, from_sibling

    def start(self, ins, outs, scr):
        mine, first = self._copies(ins, outs, scr, False)
        mine.start()
        for cp in first:
            cp.start()

    def finish(self, ins, outs, scr):
        mine, first, passed, landed, from_sibling = self._copies(ins, outs, scr, True)
        for cp_in, cp_on in zip(landed, passed):
            cp_in.wait_recv()
            cp_on.start()
        for cp in from_sibling:
            cp.wait_recv()
        for cp in first + passed:
            cp.wait_send()
        mine.wait()


class _AllGatherTree:
    def __init__(self, shard):
        self.m, n = shard.shape
        self.half = self.m // 32 * 16
        self.ins = [shard]
        self.out_shape = [jax.ShapeDtypeStruct((NDEV * self.m, n), shard.dtype)]
        self.scratch = [DMA((9,)), DMA((9,)), DMA]

    def start(self, ins, outs, scr):
        x_ref, out_ref = ins[0], outs[0]
        send_sems, recv_sems, local_sem = scr
        x, y, c = lax.axis_index("x"), lax.axis_index("y"), lax.axis_index("c")
        m, half = self.m, self.half
        me, sibling, xn, yn, diag = (x, y, c), (x, y, 1 - c), (1 - x, y, c), (x, 1 - y, c), (1 - x, 1 - y, c)
        other = lambda dev: (dev[0], dev[1], 1 - c)

        def rows(dev, lo=0, n=m):
            return out_ref.at[pl.ds((4 * dev[0] + 2 * dev[1] + dev[2]) * m + lo, n), :]

        def copy(k, dev, to, lo=0, n=m, src=None):
            return pltpu.make_async_remote_copy(
                src_ref=rows(dev, lo, n) if src is None else src, dst_ref=rows(dev, lo, n),
                send_sem=send_sems.at[k], recv_sem=recv_sems.at[k], device_id=to, device_id_type=MESH)

        mine = pltpu.make_async_copy(x_ref, rows(me), local_sem)
        own = [copy(0, me, sibling, src=x_ref), copy(1, me, xn, src=x_ref), copy(2, me, yn, src=x_ref)]
        mine.start()
        for cp in own:
            cp.start()
        steps = [(copy(1, xn, me), [copy(3, xn, yn, 0, half), copy(5, xn, sibling)]),
                 (copy(2, yn, me), [copy(4, yn, xn, half, m - half), copy(6, yn, sibling)]),
                 (copy(3, diag, me, 0, half), [copy(7, diag, sibling, 0, half)]),
                 (copy(4, diag, me, half, m - half), [copy(8, diag, sibling, half, m - half)])]
        from_sibling = [copy(0, sibling, me), copy(5, other(xn), me), copy(6, other(yn), me),
                        copy(7, other(diag), me, 0, half), copy(8, other(diag), me, half, m - half)]
        passed = []
        for landing, onward in steps:
            landing.wait_recv()
            for cp in onward:
                cp.start()
            passed += onward
        for cp in from_sibling:
            cp.wait_recv()
        for cp in own + passed:
            cp.wait_send()
        mine.wait()

    def finish(self, ins, outs, scr):
        pass


class _ReduceScatterSend:
    def __init__(self, grads, chip_sums=False):
        grads = [g if isinstance(g, tuple) else (g, 0, g.shape[0] // NDEV, g.shape[0] // NDEV) for g in grads]
        self.peers = (2, 4, 6) if chip_sums else tuple(range(1, NDEV))
        self.nw = len(grads)
        self.base = [b for _, b, _, _ in grads]
        self.stride = [s for _, _, s, _ in grads]
        self.rows = [r for _, _, _, r in grads]
        self.ins = [g for g, _, _, _ in grads]
        self.out_shape = [jax.ShapeDtypeStruct((NDEV, r, g.shape[1]), g.dtype) for g, _, _, r in grads]
        self.scratch = [DMA((self.nw, NDEV - 1)), DMA((self.nw, NDEV - 1)), DMA((self.nw,))]

    def _copies(self, g_refs, r_refs, scr, want):
        send_sems, recv_sems, local_sems = scr
        x, y, c = lax.axis_index("x"), lax.axis_index("y"), lax.axis_index("c")
        me = 4 * x + 2 * y + c
        rows, base, stride = self.rows, self.base, self.stride
        out = []
        if want == "local":
            for w in range(self.nw):
                out.append(pltpu.make_async_copy(g_refs[w].at[pl.ds(base[w] + me * stride[w], rows[w])],
                                                 r_refs[w].at[me], local_sems.at[w]))
            return out
        for k in self.peers:
            px, py, pc = x ^ (k >> 2), y ^ ((k >> 1) & 1), c ^ (k & 1)
            peer = 4 * px + 2 * py + pc
            for w in range(self.nw):
                sems = dict(send_sem=send_sems.at[w, k - 1], recv_sem=recv_sems.at[w, k - 1],
                            device_id=(px, py, pc), device_id_type=MESH)
                if want == "send":
                    out.append(pltpu.make_async_remote_copy(
                        src_ref=g_refs[w].at[pl.ds(base[w] + peer * stride[w], rows[w])], dst_ref=r_refs[w].at[me],
                        **sems))
                else:
                    out.append(pltpu.make_async_remote_copy(
                        src_ref=g_refs[w].at[pl.ds(0, rows[w])], dst_ref=r_refs[w].at[peer], **sems))
        return out

    def start(self, ins, outs, scr):
        for cp in self._copies(ins, outs, scr, "local") + self._copies(ins, outs, scr, "send"):
            cp.start()

    def finish(self, ins, outs, scr):
        for cp in self._copies(ins, outs, scr, "recv"):
            cp.wait_recv()
        for cp in self._copies(ins, outs, scr, "send"):
            cp.wait_send()
        for cp in self._copies(ins, outs, scr, "local"):
            cp.wait()


def _call(body, *, name, grid, ins, in_specs, out_shape, out_specs, scratch=(), vmem=None, carries=()):
    n_in, n_out, n_scr = len(ins), len(out_shape), len(scratch)
    ng = len(grid)

    def split(refs):
        pos = [0]

        def take(k):
            part = refs[pos[0]:pos[0] + k]
            pos[0] += k
            return part

        i_refs = take(n_in)
        c_in = [take(len(c.ins)) for c in carries]
        o_refs = take(n_out)
        c_out = [take(len(c.out_shape)) for c in carries]
        s_refs = take(n_scr)
        c_scr = [take(len(c.scratch)) for c in carries]
        return i_refs, o_refs, s_refs, list(zip(carries, c_in, c_out, c_scr))

    def full(*refs):
        i_refs, o_refs, s_refs, cparts = split(refs)
        if ng == 0:
            for c, a, b, s in cparts:
                c.start(a, b, s)
            for c, a, b, s in cparts:
                c.finish(a, b, s)
            return
        ids = [pl.program_id(a) for a in range(ng)]
        if cparts:
            @pl.when(functools.reduce(operator.and_, [i == 0 for i in ids]))
            def _():
                for c, a, b, s in cparts:
                    c.start(a, b, s)

        body(*i_refs, *o_refs, *s_refs)
        if cparts:
            @pl.when(functools.reduce(operator.and_, [i == g - 1 for i, g in zip(ids, grid)]))
            def _():
                for c, a, b, s in cparts:
                    c.finish(a, b, s)

    all_ins = list(ins) + [a for c in carries for a in c.ins]
    all_in_specs = list(in_specs) + [ANY for c in carries for _ in c.ins]
    all_out_shape = list(out_shape) + [s for c in carries for s in c.out_shape]
    all_out_specs = list(out_specs) + [ANY for c in carries for _ in c.out_shape]
    all_scratch = list(scratch) + [s for c in carries for s in c.scratch]
    kwargs = dict(grid=grid) if ng else {}
    outs = pl.pallas_call(
        full, name=name, in_specs=all_in_specs, out_specs=all_out_specs, out_shape=all_out_shape,
        scratch_shapes=all_scratch,
        compiler_params=pltpu.CompilerParams(dimension_semantics=("arbitrary",) * ng if ng else None,
                                             vmem_limit_bytes=vmem),
        **kwargs)(*all_ins)
    outs = list(outs)
    res, pos = outs[:n_out], n_out
    carried = []
    for c in carries:
        carried.append(outs[pos:pos + len(c.out_shape)])
        pos += len(c.out_shape)
    return res, carried


def _pair_sum(g, items):
    rows = items[0][1]
    n = len(items) * (NDEV // 2)
    width = g.shape[1]

    def body(g_ref, out_ref, got_ref, mine_v, got_v, send_sems, recv_sems, ld_sems, st_sems):
        x, y, c = lax.axis_index("x"), lax.axis_index("y"), lax.axis_index("c")
        keep = [base + (2 * ch + c) * rows for base, _ in items for ch in range(NDEV // 2)]
        give = [base + (2 * ch + 1 - c) * rows for base, _ in items for ch in range(NDEV // 2)]

        def remote(j, r):
            return pltpu.make_async_remote_copy(
                src_ref=g_ref.at[pl.ds(r, rows)], dst_ref=got_ref.at[pl.ds(r, rows)], send_sem=send_sems.at[j],
                recv_sem=recv_sems.at[j], device_id=(x, y, 1 - c), device_id_type=MESH)

        sends = [remote(j, r) for j, r in enumerate(give)]
        for cp in sends:
            cp.start()
        loads = [pltpu.make_async_copy(g_ref.at[pl.ds(r, rows)], mine_v.at[j], ld_sems.at[0, j])
                 for j, r in enumerate(keep)]
        for cp in loads:
            cp.start()
        stores = []
        for j, r in enumerate(keep):
            remote(j, r).wait_recv()
            cp = pltpu.make_async_copy(got_ref.at[pl.ds(r, rows)], got_v.at[j], ld_sems.at[1, j])
            cp.start()
            loads[j].wait()
            cp.wait()
            mine_v[j] = (mine_v[j].astype(F32) + got_v[j].astype(F32)).astype(BF16)
            st = pltpu.make_async_copy(mine_v.at[j], out_ref.at[pl.ds(r, rows)], st_sems.at[j])
            st.start()
            stores.append(st)
        for cp in sends:
            cp.wait_send()
        for cp in stores:
            cp.wait()

    out, _ = pl.pallas_call(
        body, name="pair_sum", in_specs=[ANY], out_specs=[ANY, ANY],
        out_shape=[jax.ShapeDtypeStruct(g.shape, g.dtype), jax.ShapeDtypeStruct(g.shape, g.dtype)],
        scratch_shapes=[pltpu.VMEM((n, rows, width), g.dtype), pltpu.VMEM((n, rows, width), g.dtype),
                        DMA((n,)), DMA((n,)), DMA((2, n)), DMA((n,))],
        compiler_params=pltpu.CompilerParams(vmem_limit_bytes=40 * 2 ** 20))(g)
    return out


def _groups_of(names):
    out = []
    for n in names:
        if LOC[n][0] not in out:
            out.append(LOC[n][0])
    return out


def _weight_pieces(name):
    g, off, rows = LOC[name]
    return [(d * GROUP_ROWS[g] + off, d * rows, rows) for d in range(NDEV)]


def _win_pieces():
    kv = N_KV * HEAD_DIM
    pieces = [(0, 0, D)]
    for g0, d0 in ((D, D), (D + kv, D + N_KV * LANES)):
        for g in range(N_KV):
            for half in range(2):
                pieces.append((g0 + g * HEAD_DIM, d0 + g * LANES + half * HEAD_DIM, HEAD_DIM))
    pieces.append((D + 2 * kv, D + 2 * N_KV * LANES, D))
    pieces.append((D + 2 * kv + D, D + 2 * N_KV * LANES + D, D))
    grp, off, rows = LOC["w_in"]
    out = []
    for g0, d0, n in pieces:
        while n > 0:
            dev, loc = divmod(g0, rows)
            m = min(n, rows - loc)
            out.append((dev * GROUP_ROWS[grp] + off + loc, d0, m))
            g0, d0, n = g0 + m, d0 + m, n - m
    return out


def _start_loads(src_ref, dst_ref, pieces, sems, base):
    cps = []
    for j, (s, d, n) in enumerate(pieces):
        cp = pltpu.make_async_copy(src_ref.at[pl.ds(s, n)], dst_ref.at[pl.ds(d, n)], sems.at[base + j])
        cp.start()
        cps.append(cp)
    return cps


def _load_weights(wrefs, targets, sems):
    cps, base = [], 0
    for name, dst in targets:
        pieces = _win_pieces() if name == "w_in" else _weight_pieces(name)
        cps += _start_loads(wrefs[LOC[name][0]], dst, pieces, sems, base)
        base += len(pieces)
    for cp in cps:
        cp.wait()


class _WeightLoader:
    def __init__(self, wrefs, targets, sems):
        self.items, base = [], 0
        for name, dst in targets:
            pieces = _win_pieces() if name == "w_in" else _weight_pieces(name)
            for j, (s, d, n) in enumerate(pieces):
                make = functools.partial(pltpu.make_async_copy, wrefs[LOC[name][0]].at[pl.ds(s, n)],
                                         dst.at[pl.ds(d, n)], sems.at[base + j])
                self.items.append([name, d, make, False])
            base += len(pieces)

    def start(self):
        for item in self.items:
            item[2]().start()

    def wait_when(self, cond, name, upto=None):
        todo = [item for item in self.items if not item[3] and item[0] == name and (upto is None or item[1] < upto)]
        if todo:
            @pl.when(cond)
            def _():
                for item in todo:
                    item[2]().wait()

            for item in todo:
                item[3] = True


def _n_pieces(names):
    return sum(len(_win_pieces()) if n == "w_in" else NDEV for n in names)


def _dw(lhs, rhs, chunk, name, carries=()):
    nq = 4
    if lhs.ndim == 3:
        nch, t_tok, chunk = lhs.shape
        c = nch * chunk
        tq = t_tok // nq
        lhs_specs = [pl.BlockSpec((None, tq, chunk), lambda i, k=k: (i, k, 0)) for k in range(nq)]
    else:
        t_tok, c = lhs.shape
        tq = t_tok // nq
        lhs_specs = [pl.BlockSpec((tq, chunk), lambda i, k=k: (k, i)) for k in range(nq)]

    def body(*refs):
        lhs_refs, (rhs_ref, out_ref, rhs_s, sems) = refs[:nq], refs[nq:]
        cps = [pltpu.make_async_copy(rhs_ref.at[pl.ds(k * tq, tq)], rhs_s.at[pl.ds(k * tq, tq)], sems.at[k])
               for k in range(nq)]
        first = pl.program_id(0) == 0

        @pl.when(first)
        def _():
            for cp in cps:
                cp.start()

        for o in range(0, chunk, D // 2):
            n = min(D // 2, chunk - o)
            acc = None
            for k in range(nq):
                if o == 0:
                    @pl.when(first)
                    def _():
                        cps[k].wait()

                part = _tn(lhs_refs[k][:, pl.ds(o, n)], rhs_s[pl.ds(k * tq, tq), :])
                acc = part if acc is None else acc + part
            out_ref[pl.ds(o, n), :] = acc.astype(BF16)

    (out,), carried = _call(
        body, name=name, grid=(c // chunk,), ins=[lhs] * nq + [rhs],
        in_specs=lhs_specs + [ANY],
        out_specs=[pl.BlockSpec((chunk, D), lambda i: (i, 0))],
        out_shape=[jax.ShapeDtypeStruct((c, D), BF16)],
        scratch=[pltpu.VMEM((t_tok, D), BF16), DMA((nq,))], vmem=VMEM_BIG, carries=carries)
    return out, carried


def _silu_parts(a):
    sig = jax.nn.sigmoid(a)
    return sig, a * sig


FC = 256


def _ffn_fwd(h, gpre, gpost, wg, names, target=None, tm=512, carries=()):
    t_tok = h.shape[0]
    nt = t_tok // tm
    with_loss = target is not None
    groups = _groups_of(names)

    def body(*refs):
        refs = list(refs)
        h_ref, gpre_ref, gpost_ref = refs[:3]
        del refs[:3]
        tgt_ref = refs.pop(0) if with_loss else None
        wrefs = dict(zip(groups, refs[:len(groups)]))
        del refs[:len(groups)]
        hout_ref, a_ref, b_ref, f_ref, nb_ref = refs[:5]
        del refs[:5]
        if with_loss:
            dy_ref, loss_ref = refs[:2]
            del refs[:2]
        w1_s, w3_s, w2_s, sems = refs
        i = pl.program_id(0)
        first = i == 0
        loader = _WeightLoader(wrefs, list(zip(names, (w1_s, w3_s, w2_s))), sems)

        @pl.when(first)
        def _():
            loader.start()
            if with_loss:
                loss_ref[...] = jnp.zeros_like(loss_ref)

        if with_loss:
            for name in names:
                loader.wait_when(first, name)
        x = h_ref[...]
        n, _, _ = _rms_fwd(x, gpre_ref[...])
        nb = n.astype(BF16)
        nb_ref[...] = nb
        f = jnp.zeros((tm, D), F32)
        for c0 in range(0, FF, FC):
            for name in names:
                loader.wait_when(first, name, c0 + FC)
            a = _nt(nb, w1_s[pl.ds(c0, FC), :])
            b = _nt(nb, w3_s[pl.ds(c0, FC), :])
            _, sl = _silu_parts(a)
            a_ref[:, pl.ds(c0, FC)] = a.astype(BF16)
            b_ref[:, pl.ds(c0, FC)] = b.astype(BF16)
            f = f + _nn((sl * b).astype(BF16), w2_s[pl.ds(c0, FC), :])
        f_ref[...] = f
        fn, _, _ = _rms_fwd(f, gpost_ref[...])
        y = x + 0.5 * fn
        hout_ref[...] = y
        if with_loss:
            err = y - tgt_ref[...]
            dy_ref[...] = err * (1.0 / D)
            loss_ref[...] += jnp.sum(jnp.sum(err * err, axis=-1, keepdims=True), axis=0, keepdims=True) * (0.5 / D)

    ins = [h, gpre, gpost] + ([target] if with_loss else []) + [wg[g] for g in groups]
    in_specs = [_row_spec(tm, D), _const_spec((1, D)), _const_spec((1, D))]
    in_specs += ([_row_spec(tm, D)] if with_loss else []) + [ANY] * len(groups)
    out_shape = [jax.ShapeDtypeStruct((t_tok, D), F32), jax.ShapeDtypeStruct((t_tok, FF), BF16),
                 jax.ShapeDtypeStruct((t_tok, FF), BF16), jax.ShapeDtypeStruct((t_tok, D), F32),
                 jax.ShapeDtypeStruct((t_tok, D), BF16)]
    out_specs = [_row_spec(tm, D), _row_spec(tm, FF), _row_spec(tm, FF), _row_spec(tm, D), _row_spec(tm, D)]
    if with_loss:
        out_shape += [jax.ShapeDtypeStruct((t_tok, D), F32), jax.ShapeDtypeStruct((1, LANES), F32)]
        out_specs += [_row_spec(tm, D), _const_spec((1, LANES))]
    return _call(body, name="ffn_fwd_" + names[0][:4], grid=(nt,), ins=ins, in_specs=in_specs,
                 out_shape=out_shape, out_specs=out_specs,
                 scratch=[pltpu.VMEM((FF, D), BF16)] * 3 + [DMA((3 * NDEV,))], vmem=VMEM_BIG, carries=carries)


FH = FF // 2
HALF_PIECES = ((0, 256), (256, 256), (512, 256), (768, 256), (1024, 256), (1280, 128))


def _ffn_bwd_a(dh, f, a, b, gpost, wg, name_w2, tm=512, carries=()):
    t_tok = dh.shape[0]
    nt = t_tok // tm
    groups = _groups_of([name_w2])

    def body(dh_ref, f_ref, a_ref, b_ref, gpost_ref, wg_ref, dab_ref, s_ref, df_ref, dgp_ref, w2_s, sems):
        i = pl.program_id(0)
        first = i == 0
        loader = _WeightLoader({groups[0]: wg_ref}, [(name_w2, w2_s)], sems)

        @pl.when(first)
        def _():
            loader.start()
            dgp_ref[...] = jnp.zeros_like(dgp_ref)

        fv = f_ref[...]
        _, fh, r = _rms_fwd(fv, gpost_ref[...])
        df, dg = _rms_bwd(0.5 * dh_ref[...], fh, r, gpost_ref[...])
        dgp_ref[...] += dg
        dfb = df.astype(BF16)
        df_ref[...] = dfb
        for half in range(2):
            for o, n in HALF_PIECES:
                c0 = half * FH + o
                loader.wait_when(first, name_w2, c0 + n)
                ds = _nt(dfb, w2_s[pl.ds(c0, n), :])
                av = a_ref[:, pl.ds(c0, n)].astype(F32)
                bv = b_ref[:, pl.ds(c0, n)].astype(F32)
                sig, sl = _silu_parts(av)
                dab_ref[half, :, pl.ds(o, n)] = (ds * bv * (sig * (1.0 + av * (1.0 - sig)))).astype(BF16)
                dab_ref[2 + half, :, pl.ds(o, n)] = (ds * sl).astype(BF16)
                s_ref[half, :, pl.ds(o, n)] = (sl * bv).astype(BF16)

    return _call(
        body, name="ffn_bwd_a_" + name_w2[:4], grid=(nt,), ins=[dh, f, a, b, gpost, wg[groups[0]]],
        in_specs=[_row_spec(tm, D), _row_spec(tm, D), _row_spec(tm, FF), _row_spec(tm, FF), _const_spec((1, D)), ANY],
        out_specs=[pl.BlockSpec((4, tm, FH), lambda i: (0, i, 0)), pl.BlockSpec((2, tm, FH), lambda i: (0, i, 0)),
                   _row_spec(tm, D), _const_spec((1, D))],
        out_shape=[jax.ShapeDtypeStruct((4, t_tok, FH), BF16), jax.ShapeDtypeStruct((2, t_tok, FH), BF16),
                   jax.ShapeDtypeStruct((t_tok, D), BF16), jax.ShapeDtypeStruct((1, D), F32)],
        scratch=[pltpu.VMEM((FF, D), BF16), DMA((NDEV,))],
        vmem=VMEM_BIG, carries=carries)


def _ffn_bwd_dx(dab, h, dh, gpre, wg, name_w1, name_w3, tm=512, carries=()):
    t_tok = dh.shape[0]
    nt = t_tok // tm
    groups = _groups_of([name_w1, name_w3])

    def body(*refs):
        dab_refs, (h_ref, dh_ref, gpre_ref) = refs[:4], refs[4:7]
        wrefs = dict(zip(groups, refs[7:7 + len(groups)]))
        dhin_ref, dgp_ref, w13_s, sems = refs[7 + len(groups):]
        i = pl.program_id(0)
        first = i == 0
        loader = _WeightLoader(wrefs, [(name_w1, w13_s.at[pl.ds(0, FF)]), (name_w3, w13_s.at[pl.ds(FF, FF)])], sems)

        @pl.when(first)
        def _():
            loader.start()
            dgp_ref[...] = jnp.zeros_like(dgp_ref)

        g = gpre_ref[...]
        _, xh, r = _rms_fwd(h_ref[...], g)
        dn = jnp.zeros((tm, D), F32)
        for k in range(4):
            loader.wait_when(first, (name_w1, name_w3)[k // 2], (k % 2 + 1) * FH)
            dn = dn + _nn(dab_refs[k][...], w13_s[pl.ds(k * FH, FH), :])
        dx, dg = _rms_bwd(dn, xh, r, g)
        dgp_ref[...] += dg
        dhin_ref[...] = dh_ref[...] + dx

    return _call(
        body, name="ffn_bwd_dx_" + name_w1[:4], grid=(nt,), ins=[dab] * 4 + [h, dh, gpre] + [wg[g] for g in groups],
        in_specs=[pl.BlockSpec((None, tm, FH), lambda i, k=k: (k, i, 0)) for k in range(4)]
        + [_row_spec(tm, D), _row_spec(tm, D), _const_spec((1, D))] + [ANY] * len(groups),
        out_specs=[_row_spec(tm, D), _const_spec((1, D))],
        out_shape=[jax.ShapeDtypeStruct((t_tok, D), F32), jax.ShapeDtypeStruct((1, D), F32)],
        scratch=[pltpu.VMEM((2 * FF, D), BF16), DMA((2 * NDEV,))],
        vmem=VMEM_BIG, carries=carries)


Q0, K0, V0, XR0, XG0 = 0, D, D + N_KV * LANES, D + 2 * N_KV * LANES, 2 * D + 2 * N_KV * LANES


def _mix_proj_fwd(h, g, bgate, wg, tm=512, carries=()):
    t_tok = h.shape[0]
    nt = t_tok // tm
    names = ("w_in", "w_gate")
    groups = _groups_of(names)

    def body(h_ref, g_ref, bg_ref, wg_ref, q_ref, k_ref, v_ref, xr_ref, xg_ref, gs_ref, ub_ref, win_s, wgt_s, sems):
        i = pl.program_id(0)
        first = i == 0
        loader = _WeightLoader({groups[0]: wg_ref}, [("w_in", win_s), ("w_gate", wgt_s)], sems)

        @pl.when(first)
        def _():
            loader.start()

        n, _, _ = _rms_fwd(h_ref[...], g_ref[...])
        nb = n.astype(BF16)
        ub_ref[...] = nb
        loader.wait_when(first, "w_in", K0)
        q_ref[...] = _nt(nb, win_s[pl.ds(Q0, D), :]).astype(BF16)
        loader.wait_when(first, "w_in", XR0)
        k_ref[...] = _nt(nb, win_s[pl.ds(K0, N_KV * LANES), :]).astype(BF16)
        v_ref[...] = _nt(nb, win_s[pl.ds(V0, N_KV * LANES), :]).astype(BF16)
        loader.wait_when(first, "w_in", XG0)
        xr_ref[...] = _nt(nb, win_s[pl.ds(XR0, D), :])
        loader.wait_when(first, "w_in")
        xg_ref[...] = _nt(nb, win_s[pl.ds(XG0, D), :])
        loader.wait_when(first, "w_gate")
        gs_ref[...] = jax.nn.sigmoid(_nt(nb, wgt_s[...]) + bg_ref[...]).astype(BF16)

    kvw = N_KV * LANES
    return _call(
        body, name="mix_proj_fwd", grid=(nt,), ins=[h, g, bgate, wg[groups[0]]],
        in_specs=[_row_spec(tm, D), _const_spec((1, D)), _const_spec((1, 2 * D)), ANY],
        out_specs=[_row_spec(tm, D), _row_spec(tm, kvw), _row_spec(tm, kvw), _row_spec(tm, D), _row_spec(tm, D),
                   _row_spec(tm, 2 * D), _row_spec(tm, D)],
        out_shape=[jax.ShapeDtypeStruct((t_tok, D), BF16), jax.ShapeDtypeStruct((t_tok, kvw), BF16),
                   jax.ShapeDtypeStruct((t_tok, kvw), BF16), jax.ShapeDtypeStruct((t_tok, D), F32),
                   jax.ShapeDtypeStruct((t_tok, D), F32), jax.ShapeDtypeStruct((t_tok, 2 * D), BF16),
                   jax.ShapeDtypeStruct((t_tok, D), BF16)],
        scratch=[pltpu.VMEM((INP_W, D), BF16), pltpu.VMEM((2 * D, D), BF16), DMA((_n_pieces(names),))],
        vmem=VMEM_BIG, carries=carries)


def _mix_proj_bwd(dqkv, dxr, dxg, dgpre, h, dh, g, wg, tm=512, carries=()):
    t_tok = h.shape[0]
    nt = t_tok // tm
    names = ("w_in", "w_gate")
    groups = _groups_of(names)

    def body(dqkv_ref, dxr_ref, dxg_ref, dgp_ref, h_ref, dh_ref, g_ref, wg_ref, dhin_ref, dg_ref, win_s, wgt_s, sems):
        i = pl.program_id(0)

        @pl.when(i == 0)
        def _():
            _load_weights({groups[0]: wg_ref}, [("w_in", win_s), ("w_gate", wgt_s)], sems)
            dg_ref[...] = jnp.zeros_like(dg_ref)

        gv = g_ref[...]
        _, xh, r = _rms_fwd(h_ref[...], gv)
        du = _nn(dgp_ref[...], wgt_s[...])
        du = du + _nn(dqkv_ref[...], win_s[pl.ds(Q0, XR0), :])
        du = du + _nn(dxr_ref[...], win_s[pl.ds(XR0, D), :])
        du = du + _nn(dxg_ref[...], win_s[pl.ds(XG0, D), :])
        dx, dg = _rms_bwd(du, xh, r, gv)
        dg_ref[...] += dg
        dhin_ref[...] = dh_ref[...] + dx

    return _call(
        body, name="mix_proj_bwd", grid=(nt,), ins=[dqkv, dxr, dxg, dgpre, h, dh, g, wg[groups[0]]],
        in_specs=[_row_spec(tm, XR0), _row_spec(tm, D), _row_spec(tm, D), _row_spec(tm, 2 * D), _row_spec(tm, D),
                  _row_spec(tm, D), _const_spec((1, D)), ANY],
        out_specs=[_row_spec(tm, D), _const_spec((1, D))],
        out_shape=[jax.ShapeDtypeStruct((t_tok, D), F32), jax.ShapeDtypeStruct((1, D), F32)],
        scratch=[pltpu.VMEM((INP_W, D), BF16), pltpu.VMEM((2 * D, D), BF16), DMA((_n_pieces(names),))],
        vmem=VMEM_BIG, carries=carries)


def _shift_down(x, d, fill):
    row = lax.broadcasted_iota(jnp.int32, x.shape, 0)
    return jnp.where(row >= d, pltpu.roll(x, d, 0), fill)


def _shift_up(x, d, fill):
    rows = x.shape[0]
    row = lax.broadcasted_iota(jnp.int32, x.shape, 0)
    return jnp.where(row < rows - d, pltpu.roll(x, rows - d, 0), fill)


def _scan_rows(a, b, reverse):
    rows = a.shape[0]
    d = 1
    while d < rows:
        if d < 8:
            shift = _shift_up if reverse else _shift_down
            b = a * shift(b, d, 0.0) + b
            a = a * shift(a, d, 1.0)
        elif reverse:
            b = jnp.concatenate([a[:rows - d] * b[d:] + b[:rows - d], b[rows - d:]], axis=0)
            a = jnp.concatenate([a[:rows - d] * a[d:], a[rows - d:]], axis=0)
        else:
            b = jnp.concatenate([b[:d], a[d:] * b[:rows - d] + b[d:]], axis=0)
            a = jnp.concatenate([a[:d], a[d:] * a[:rows - d]], axis=0)
        d *= 2
    return a, b


def _softplus(x):
    return jnp.maximum(x, 0.0) + jnp.log(1.0 + jnp.exp(-jnp.abs(x)))


_GELU_C = math.sqrt(2.0 / math.pi)


def _gelu_parts(x):
    th = jnp.tanh(_GELU_C * (x + 0.044715 * x * x * x))
    val = 0.5 * x * (1.0 + th)
    grad = 0.5 * (1.0 + th) + 0.5 * x * (1.0 - th * th) * _GELU_C * (1.0 + 3.0 * 0.044715 * x * x)
    return val, grad


def _lru_pre(x, halo, cw_ref, cb_ref, wa_ref, wx_ref, ba_ref, bx_ref, lam_ref):
    ext = jnp.concatenate([halo, x], axis=0)
    shifted = [x] + [pltpu.roll(ext, k, 0)[8:] for k in (1, 2, 3)]
    xc = cb_ref[...] + cw_ref[pl.ds(CONV_WIDTH - 1, 1), :] * x
    for k in (1, 2, 3):
        xc = xc + cw_ref[pl.ds(CONV_WIDTH - 1 - k, 1), :] * shifted[k]
    xcb = xc.astype(BF16)
    r = jax.nn.sigmoid(_nn(xcb, wa_ref[...]) + ba_ref[...])
    ig = jax.nn.sigmoid(_nn(xcb, wx_ref[...]) + bx_ref[...])
    sp = _softplus(-lam_ref[...])
    log_a = -LRU_C * r * sp
    a = jnp.exp(log_a)
    th = jnp.tanh(log_a)
    mult = jnp.sqrt(-2.0 * th / (1.0 - th))
    return shifted, xc, xcb, r, ig, sp, a, mult


def _lru_specs(nt, rows, reverse):
    def tt(t):
        return nt - 1 - t if reverse else t
    tile = pl.BlockSpec((rows, LANES), lambda cb, t: (tt(t), cb))
    halo = pl.BlockSpec((8, LANES), lambda cb, t: (jnp.maximum(tt(t) * (rows // 8) - 1, 0), cb))
    vec = pl.BlockSpec((1, LANES), lambda cb, t: (0, cb))
    cw = pl.BlockSpec((CONV_WIDTH, LANES), lambda cb, t: (0, cb))
    mat = pl.BlockSpec((None, LANES, LANES), lambda cb, t: (cb, 0, 0))
    return tile, halo, vec, cw, mat


def _lru_fwd(xr, xg, cw, cb, wa, wx, ba, bx, lam, carries=()):
    t_tok = xr.shape[0]
    rows = min(LRU_ROWS, t_tok)
    nt = t_tok // rows

    def body(xr_ref, xg_ref, cw_ref, cb_ref, wa_ref, wx_ref, ba_ref, bx_ref, lam_ref, y_ref, h_ref, tail_s, hc_s):
        t = pl.program_id(1)

        @pl.when(t == 0)
        def _():
            tail_s[...] = jnp.zeros_like(tail_s)
            hc_s[...] = jnp.zeros_like(hc_s)

        x = xr_ref[...]
        _, xc, _, _, ig, _, a, mult = _lru_pre(x, tail_s[...], cw_ref, cb_ref, wa_ref, wx_ref, ba_ref, bx_ref, lam_ref)
        tail_s[...] = xr_ref[pl.ds(rows - 8, 8), :]
        acc_a, acc_b = _scan_rows(a, mult * (ig * xc), False)
        hv = acc_b + acc_a * hc_s[...]
        h_ref[...] = hv
        hc_s[...] = h_ref[pl.ds(rows - 1, 1), :]
        gl, _ = _gelu_parts(xg_ref[...])
        y_ref[...] = (hv * gl).astype(BF16)

    tile, _, vec, cws, mat = _lru_specs(nt, rows, False)
    return _call(
        body, name="lru_fwd", grid=(D // LANES, nt), ins=[xr, xg, cw, cb, wa, wx, ba, bx, lam],
        in_specs=[tile, tile, cws, vec, mat, mat, vec, vec, vec],
        out_specs=[tile, tile],
        out_shape=[jax.ShapeDtypeStruct((t_tok, D), BF16), jax.ShapeDtypeStruct((t_tok, D), F32)],
        scratch=[pltpu.VMEM((8, LANES), F32), pltpu.VMEM((1, LANES), F32)], carries=carries)


def _lru_bwd(dy, xr, xg, hseq, cw, cb, wa, wx, ba, bx, lam, carries=()):
    t_tok = xr.shape[0]
    rows = min(LRU_ROWS // 2, t_tok)
    nt = t_tok // rows

    def body(dy_ref, xr_ref, xrh_ref, xg_ref, h_ref, hh_ref, cw_ref, cb_ref, wa_ref, wx_ref, ba_ref, bx_ref, lam_ref,
             dxr_ref, dxg_ref, dvec_ref, dwa_ref, dwx_ref, gcar_s, acar_s, head_s, tmp_s):
        t = pl.program_id(1)
        first_tile = t == nt - 1

        @pl.when(t == 0)
        def _():
            gcar_s[...] = jnp.zeros_like(gcar_s)
            acar_s[...] = jnp.zeros_like(acar_s)
            head_s[...] = jnp.zeros_like(head_s)
            dvec_ref[...] = jnp.zeros_like(dvec_ref)
            dwa_ref[...] = jnp.zeros_like(dwa_ref)
            dwx_ref[...] = jnp.zeros_like(dwx_ref)

        x = xr_ref[...]
        halo = jnp.where(first_tile, 0.0, xrh_ref[...])
        shifted, xc, xcb, r, ig, sp, a, mult = _lru_pre(x, halo, cw_ref, cb_ref, wa_ref, wx_ref, ba_ref, bx_ref, lam_ref)
        hv = h_ref[...]
        dyv = dy_ref[...]
        gl, glg = _gelu_parts(xg_ref[...])
        dxg_ref[...] = (dyv * hv * glg).astype(BF16)
        acc_a, acc_b = _scan_rows(_shift_up(a, 1, acar_s[...]), dyv * gl, True)
        g = acc_b + acc_a * gcar_s[...]
        hhalo = jnp.where(first_tile, 0.0, hh_ref[...])
        hprev = pltpu.roll(jnp.concatenate([hhalo, hv], axis=0), 1, 0)[8:]
        dmult = g * ig * xc
        dlog_a = a * (g * hprev) - dmult * a * a / mult
        dig = g * mult * xc
        dxc = g * mult * ig
        dzr = (dlog_a * (-LRU_C * sp)) * r * (1.0 - r)
        dzx = dig * ig * (1.0 - ig)
        dzrb, dzxb = dzr.astype(BF16), dzx.astype(BF16)
        dxc = dxc + _nt(dzrb, wa_ref[...]) + _nt(dzxb, wx_ref[...])
        dwa_ref[...] += _tn(xcb, dzrb)
        dwx_ref[...] += _tn(xcb, dzxb)
        dsp = jnp.sum(dlog_a * (-LRU_C * r), axis=0, keepdims=True)
        dlam = dsp * (-jax.nn.sigmoid(-lam_ref[...]))
        vrow = lax.broadcasted_iota(jnp.int32, (8, LANES), 0)
        upd = jnp.where(vrow == 4, jnp.sum(dxc, axis=0, keepdims=True), 0.0)
        upd = jnp.where(vrow == 5, jnp.sum(dzr, axis=0, keepdims=True), upd)
        upd = jnp.where(vrow == 6, jnp.sum(dzx, axis=0, keepdims=True), upd)
        upd = jnp.where(vrow == 7, dlam, upd)
        for k in range(CONV_WIDTH):
            upd = jnp.where(vrow == CONV_WIDTH - 1 - k, jnp.sum(dxc * shifted[k], axis=0, keepdims=True), upd)
        dvec_ref[...] += upd
        ext = jnp.concatenate([dxc, head_s[...]], axis=0)
        dxr = cw_ref[pl.ds(CONV_WIDTH - 1, 1), :] * dxc
        for k in (1, 2, 3):
            dxr = dxr + cw_ref[pl.ds(CONV_WIDTH - 1 - k, 1), :] * pltpu.roll(ext, rows + 8 - k, 0)[:rows]
        dxr_ref[...] = dxr.astype(BF16)
        tmp_s[...] = g
        gcar_s[...] = tmp_s[pl.ds(0, 1), :]
        tmp_s[...] = a
        acar_s[...] = tmp_s[pl.ds(0, 1), :]
        tmp_s[...] = dxc
        head_s[...] = tmp_s[pl.ds(0, 8), :]

    tile, halo, vec, cws, mat = _lru_specs(nt, rows, True)
    return _call(
        body, name="lru_bwd", grid=(D // LANES, nt), ins=[dy, xr, xr, xg, hseq, hseq, cw, cb, wa, wx, ba, bx, lam],
        in_specs=[tile, tile, halo, tile, tile, halo, cws, vec, mat, mat, vec, vec, vec],
        out_specs=[tile, tile, pl.BlockSpec((8, LANES), lambda cb, t: (0, cb)), mat, mat],
        out_shape=[jax.ShapeDtypeStruct((t_tok, D), BF16), jax.ShapeDtypeStruct((t_tok, D), BF16),
                   jax.ShapeDtypeStruct((8, D), F32), jax.ShapeDtypeStruct((D // LANES, LANES, LANES), F32),
                   jax.ShapeDtypeStruct((D // LANES, LANES, LANES), F32)],
        scratch=[pltpu.VMEM((1, LANES), F32), pltpu.VMEM((1, LANES), F32), pltpu.VMEM((8, LANES), F32),
                 pltpu.VMEM((rows, LANES), F32)], carries=carries)


def _t5_bucket_np(rel):
    nb = N_BUCKETS // 2
    max_exact = nb // 2
    ret = np.where(rel > 0, nb, 0)
    n = np.abs(rel)
    nf = np.maximum(n, 1).astype(np.float32)
    large = max_exact + (np.log(nf / np.float32(max_exact)) / np.float32(math.log(MAX_DISTANCE / max_exact))
                         * np.float32(nb - max_exact)).astype(np.int32)
    large = np.minimum(large, nb - 1)
    return ret + np.where(n < max_exact, n, large)


def _bucket_map():
    r = np.arange(QT)[:, None]
    c = np.arange(KW)[None, :]
    j = c - (r // CHUNK) * CHUNK
    band = (j >= 0) & (j < WINDOW + CHUNK)
    return np.where(band, _t5_bucket_np(c - r - WINDOW), -1).astype(np.int32)


def _attn_specs(nt, reverse):
    def tt(i):
        return nt - 1 - i if reverse else i
    kvw = N_KV * LANES
    qs = pl.BlockSpec((QT, D), lambda i: (tt(i), 0))
    cur = pl.BlockSpec((QT, kvw), lambda i: (tt(i), 0))
    prev = pl.BlockSpec((WINDOW, kvw), lambda i: (jnp.maximum(tt(i) * (QT // WINDOW) - 1, 0), 0))
    lse = pl.BlockSpec((QT, LANES), lambda i: (tt(i), 0))
    return qs, cur, prev, lse


REP = N_HEADS // N_KV
SCALE = HEAD_DIM ** -0.5


def _stack_heads(x_ref, g, lo, scale=None):
    parts = []
    for hl in range(REP):
        xs = x_ref[:, pl.ds((2 * g + hl // 2) * LANES, LANES)]
        xs = jnp.where(lo if hl % 2 == 0 else jnp.logical_not(lo), xs, jnp.zeros_like(xs))
        parts.append(xs if scale is None else xs * jnp.asarray(scale, xs.dtype))
    return jnp.concatenate(parts, axis=0)


def _stack_sinks(sink_ref, g, srow):
    sk = jnp.full(srow.shape, sink_ref[REP * g + REP - 1], F32)
    for hl in range(REP - 2, -1, -1):
        sk = jnp.where(srow < (hl + 1) * QT, sink_ref[REP * g + hl], sk)
    return sk


def _attn_fwd(q, kd, vd, bias, sinks, carries=()):
    t_tok = q.shape[0]
    nt = t_tok // QT

    def body(q_ref, kp_ref, kc_ref, vp_ref, vc_ref, bias_ref, sink_ref, o_ref, lse_ref):
        i = pl.program_id(0)
        col = lax.broadcasted_iota(jnp.int32, (1, KW), 1)
        first = jnp.where((i == 0) & (col < WINDOW), NEG_INF, 0.0)
        lane = lax.broadcasted_iota(jnp.int32, (QT, LANES), 1)
        lo = lane < HEAD_DIM
        srow = lax.broadcasted_iota(jnp.int32, (REP * QT, 1), 0)
        lse_t = jnp.zeros((QT, LANES), F32)
        for g in range(N_KV):
            kwin = jnp.concatenate([kp_ref[:, pl.ds(g * LANES, LANES)], kc_ref[:, pl.ds(g * LANES, LANES)]], axis=0)
            vwin = jnp.concatenate([vp_ref[:, pl.ds(g * LANES, LANES)], vc_ref[:, pl.ds(g * LANES, LANES)]], axis=0)
            qst = _stack_heads(q_ref, g, lo, SCALE)
            s = _nt(qst, kwin) + (bias_ref[g] + first)
            sk = _stack_sinks(sink_ref, g, srow)
            m = jnp.maximum(jnp.max(s, axis=-1, keepdims=True), sk)
            e = jnp.exp(s - m)
            l = jnp.sum(e, axis=-1, keepdims=True) + jnp.exp(sk - m)
            p = e / l
            ost = _nn(p.astype(BF16), vwin)
            lse_s = m + jnp.log(l)
            for sl in range(2):
                o_ref[:, pl.ds((2 * g + sl) * LANES, LANES)] = jnp.where(
                    lo, ost[2 * sl * QT:(2 * sl + 1) * QT], ost[(2 * sl + 1) * QT:(2 * sl + 2) * QT]).astype(BF16)
            for hl in range(REP):
                lse_t = jnp.where(lane == REP * g + hl, lse_s[hl * QT:(hl + 1) * QT], lse_t)
        lse_ref[...] = lse_t

    qs, cur, prev, lse = _attn_specs(nt, False)
    return _call(
        body, name="attn_fwd", grid=(nt,), ins=[q, kd, kd, vd, vd, bias, sinks],
        in_specs=[qs, prev, cur, prev, cur, _const_spec((N_KV, REP * QT, KW)), pl.BlockSpec(memory_space=pltpu.SMEM)],
        out_specs=[qs, lse],
        out_shape=[jax.ShapeDtypeStruct((t_tok, D), BF16), jax.ShapeDtypeStruct((t_tok, LANES), F32)],
        vmem=48 * 2 ** 20, carries=carries)


def _attn_bwd(q, kd, vd, o, do, lse, bias, sinks, carries=()):
    t_tok = q.shape[0]
    nt = t_tok // QT
    kvw = N_KV * LANES

    def body(q_ref, kp_ref, kc_ref, vp_ref, vc_ref, o_ref, do_ref, lse_ref, bias_ref, sink_ref,
             dqkv_ref, ds_ref, dsink_ref, kcar_s, vcar_s):
        i = pl.program_id(0)
        tile = nt - 1 - i

        @pl.when(i == 0)
        def _():
            kcar_s[...] = jnp.zeros_like(kcar_s)
            vcar_s[...] = jnp.zeros_like(vcar_s)
            ds_ref[...] = jnp.zeros_like(ds_ref)
            dsink_ref[...] = jnp.zeros_like(dsink_ref)

        col = lax.broadcasted_iota(jnp.int32, (1, KW), 1)
        first = jnp.where((tile == 0) & (col < WINDOW), NEG_INF, 0.0)
        lane = lax.broadcasted_iota(jnp.int32, (QT, LANES), 1)
        lo = lane < HEAD_DIM
        lane_k = lax.broadcasted_iota(jnp.int32, (KW, LANES), 1)
        lane_1 = lax.broadcasted_iota(jnp.int32, (1, LANES), 1)
        srow = lax.broadcasted_iota(jnp.int32, (REP * QT, 1), 0)
        lse_t = lse_ref[...]
        dsink = jnp.zeros((1, LANES), F32)
        for g in range(N_KV):
            kwin = jnp.concatenate([kp_ref[:, pl.ds(g * LANES, LANES)], kc_ref[:, pl.ds(g * LANES, LANES)]], axis=0)
            vwin = jnp.concatenate([vp_ref[:, pl.ds(g * LANES, LANES)], vc_ref[:, pl.ds(g * LANES, LANES)]], axis=0)
            qst = _stack_heads(q_ref, g, lo, SCALE)
            dost = _stack_heads(do_ref, g, lo)
            od = [do_ref[:, pl.ds((2 * g + sl) * LANES, LANES)].astype(F32)
                  * o_ref[:, pl.ds((2 * g + sl) * LANES, LANES)].astype(F32) for sl in range(2)]
            drow = jnp.concatenate([jnp.sum(jnp.where(lo if hl % 2 == 0 else jnp.logical_not(lo), od[hl // 2], 0.0),
                                            axis=-1, keepdims=True) for hl in range(REP)], axis=0)
            lse_s = jnp.concatenate([jnp.sum(jnp.where(lane == REP * g + hl, lse_t, 0.0), axis=-1, keepdims=True)
                                     for hl in range(REP)], axis=0)
            s = _nt(qst, kwin) + (bias_ref[g] + first)
            p = jnp.exp(s - lse_s)
            ds = p * (_nt(dost, vwin) - drow)
            ds_ref[g] += ds
            tsink = -(jnp.exp(_stack_sinks(sink_ref, g, srow) - lse_s) * drow)
            for hl in range(REP):
                dsink = dsink + jnp.where(lane_1 == REP * g + hl,
                                          jnp.sum(tsink[hl * QT:(hl + 1) * QT], axis=0, keepdims=True), 0.0)
            dsb = ds.astype(BF16)
            dqst = _nn(dsb, kwin) * SCALE
            for sl in range(2):
                dqkv_ref[:, pl.ds((2 * g + sl) * LANES, LANES)] = jnp.where(
                    lo, dqst[2 * sl * QT:(2 * sl + 1) * QT], dqst[(2 * sl + 1) * QT:(2 * sl + 2) * QT]).astype(BF16)
            dk_acc = _tn(dsb, qst)
            dv_acc = _tn(p.astype(BF16), dost)
            dk_f = jnp.where(lane_k < HEAD_DIM, dk_acc + pltpu.roll(dk_acc, HEAD_DIM, 1), 0.0)
            dv_f = jnp.where(lane_k < HEAD_DIM, dv_acc + pltpu.roll(dv_acc, HEAD_DIM, 1), 0.0)
            for acc, col0, car in ((dk_f, K0, kcar_s), (dv_f, V0, vcar_s)):
                cs = pl.ds(g * LANES, LANES)
                co = pl.ds(col0 + g * LANES, LANES)
                if QT > WINDOW:
                    dqkv_ref[pl.ds(0, QT - WINDOW), co] = acc[WINDOW:QT].astype(BF16)
                dqkv_ref[pl.ds(QT - WINDOW, WINDOW), co] = (acc[QT:KW] + car[:, cs]).astype(BF16)
                car[:, cs] = acc[0:WINDOW]
        dsink_ref[...] += dsink

    qs, cur, prev, lse_s = _attn_specs(nt, True)
    return _call(
        body, name="attn_bwd", grid=(nt,), ins=[q, kd, kd, vd, vd, o, do, lse, bias, sinks],
        in_specs=[qs, prev, cur, prev, cur, qs, qs, lse_s, _const_spec((N_KV, REP * QT, KW)),
                  pl.BlockSpec(memory_space=pltpu.SMEM)],
        out_specs=[pl.BlockSpec((QT, XR0), lambda i: (nt - 1 - i, 0)), _const_spec((N_KV, REP * QT, KW)),
                   _const_spec((1, LANES))],
        out_shape=[jax.ShapeDtypeStruct((t_tok, XR0), BF16), jax.ShapeDtypeStruct((N_KV, REP * QT, KW), F32),
                   jax.ShapeDtypeStruct((1, LANES), F32)],
        scratch=[pltpu.VMEM((WINDOW, kvw), F32), pltpu.VMEM((WINDOW, kvw), F32)],
        vmem=VMEM_BIG, carries=carries)


def _bias_tile(table, bmap):
    def body(tab_ref, bm_ref, out_ref):
        bm = bm_ref[...]

        def per_head(hd, carry):
            acc = jnp.full((QT, KW), NEG_INF, F32)
            for b in range(N_BUCKETS):
                acc = jnp.where(bm == b, tab_ref[b, hd], acc)
            out_ref[hd] = acc
            return carry

        lax.fori_loop(0, N_HEADS, per_head, 0)

    return pl.pallas_call(
        body, name="bias_tile", out_shape=jax.ShapeDtypeStruct((N_HEADS, QT, KW), F32),
        in_specs=[pl.BlockSpec(memory_space=pltpu.SMEM), pl.BlockSpec(memory_space=pltpu.VMEM)],
        out_specs=pl.BlockSpec(memory_space=pltpu.VMEM))(table, bmap)


def _bias_grad(ds_acc, bmap):
    def body(ds_ref, bm_ref, out_ref):
        row = lax.broadcasted_iota(jnp.int32, (N_BUCKETS, LANES), 0)
        lane = lax.broadcasted_iota(jnp.int32, (N_BUCKETS, LANES), 1)
        bm = bm_ref[...]

        def per_head(hd, res):
            dsv = ds_ref[hd]
            for b in range(N_BUCKETS):
                val = jnp.sum(jnp.sum(jnp.where(bm == b, dsv, 0.0), axis=0, keepdims=True), axis=1, keepdims=True)
                res = jnp.where((row == b) & (lane == hd), val, res)
            return res

        out_ref[...] = lax.fori_loop(0, N_HEADS, per_head, jnp.zeros((N_BUCKETS, LANES), F32))

    return pl.pallas_call(
        body, name="bias_grad", out_shape=jax.ShapeDtypeStruct((N_BUCKETS, LANES), F32),
        in_specs=[pl.BlockSpec(memory_space=pltpu.VMEM), pl.BlockSpec(memory_space=pltpu.VMEM)],
        out_specs=pl.BlockSpec(memory_space=pltpu.VMEM))(ds_acc, bmap)


MIXOUT = ("w_lru_out", "w_attn_out", "w_o")


def _mix_out_fwd(ya_in, o, gs, h, g, wg, tm=512, carries=()):
    t_tok = h.shape[0]
    nt = t_tok // tm
    groups = _groups_of(MIXOUT)

    def body(ya_ref, o_ref, gs_ref, h_ref, g_ref, wg_ref, hout_ref, yao_ref, ybo_ref, z_ref, wa_s, wb_s, wo_s, sems):
        i = pl.program_id(0)

        @pl.when(i == 0)
        def _():
            _load_weights({groups[0]: wg_ref}, list(zip(MIXOUT, (wa_s, wb_s, wo_s))), sems)

        ya = _nn(ya_ref[...], wa_s[...])
        yb = _nn(o_ref[...], wb_s[...])
        yao_ref[...] = ya.astype(BF16)
        ybo_ref[...] = yb.astype(BF16)
        merged = gs_ref[:, pl.ds(0, D)].astype(F32) * ya + gs_ref[:, pl.ds(D, D)].astype(F32) * yb
        z = _nn(merged.astype(BF16), wo_s[...])
        z_ref[...] = z
        zn, _, _ = _rms_fwd(z, g_ref[...])
        hout_ref[...] = h_ref[...] + zn

    return _call(
        body, name="mix_out_fwd", grid=(nt,), ins=[ya_in, o, gs, h, g, wg[groups[0]]],
        in_specs=[_row_spec(tm, D), _row_spec(tm, D), _row_spec(tm, 2 * D), _row_spec(tm, D), _const_spec((1, D)), ANY],
        out_specs=[_row_spec(tm, D)] * 4,
        out_shape=[jax.ShapeDtypeStruct((t_tok, D), F32), jax.ShapeDtypeStruct((t_tok, D), BF16),
                   jax.ShapeDtypeStruct((t_tok, D), BF16), jax.ShapeDtypeStruct((t_tok, D), F32)],
        scratch=[pltpu.VMEM((D, D), BF16)] * 3 + [DMA((3 * NDEV,))],
        vmem=48 * 2 ** 20, carries=carries)


def _mix_out_bwd(dh, z, ya, yb, gs, g, wg, tm=512, carries=()):
    t_tok = dh.shape[0]
    nt = t_tok // tm
    groups = _groups_of(MIXOUT)

    def body(dh_ref, z_ref, ya_ref, yb_ref, gs_ref, g_ref, wg_ref,
             dyain_ref, do_ref, dgpre_ref, dya_ref, dyb_ref, mg_ref, dz_ref, dg_ref, dbg_ref,
             wa_s, wb_s, wo_s, sems):
        i = pl.program_id(0)

        @pl.when(i == 0)
        def _():
            _load_weights({groups[0]: wg_ref}, list(zip(MIXOUT, (wa_s, wb_s, wo_s))), sems)
            dg_ref[...] = jnp.zeros_like(dg_ref)
            dbg_ref[...] = jnp.zeros_like(dbg_ref)

        gv = g_ref[...]
        _, zh, r = _rms_fwd(z_ref[...], gv)
        dz, dg = _rms_bwd(dh_ref[...], zh, r, gv)
        dg_ref[...] += dg
        dzb = dz.astype(BF16)
        dz_ref[...] = dzb
        ga, gb = gs_ref[:, pl.ds(0, D)].astype(F32), gs_ref[:, pl.ds(D, D)].astype(F32)
        ya_v, yb_v = ya_ref[...].astype(F32), yb_ref[...].astype(F32)
        mg_ref[...] = (ga * ya_v + gb * yb_v).astype(BF16)
        dm = _nt(dzb, wo_s[...])
        dga = dm * ya_v * ga * (1.0 - ga)
        dgb = dm * yb_v * gb * (1.0 - gb)
        dgpre_ref[:, pl.ds(0, D)] = dga.astype(BF16)
        dgpre_ref[:, pl.ds(D, D)] = dgb.astype(BF16)
        dbg_ref[:, pl.ds(0, D)] += jnp.sum(dga, axis=0, keepdims=True)
        dbg_ref[:, pl.ds(D, D)] += jnp.sum(dgb, axis=0, keepdims=True)
        dya = (dm * ga).astype(BF16)
        dyb = (dm * gb).astype(BF16)
        dya_ref[...] = dya
        dyb_ref[...] = dyb
        dyain_ref[...] = _nt(dya, wa_s[...])
        do_ref[...] = _nt(dyb, wb_s[...]).astype(BF16)

    bf = jax.ShapeDtypeStruct((t_tok, D), BF16)
    return _call(
        body, name="mix_out_bwd", grid=(nt,), ins=[dh, z, ya, yb, gs, g, wg[groups[0]]],
        in_specs=[_row_spec(tm, D)] * 4 + [_row_spec(tm, 2 * D), _const_spec((1, D)), ANY],
        out_specs=[_row_spec(tm, D), _row_spec(tm, D), _row_spec(tm, 2 * D)] + [_row_spec(tm, D)] * 4
        + [_const_spec((1, D)), _const_spec((1, 2 * D))],
        out_shape=[jax.ShapeDtypeStruct((t_tok, D), F32), bf, jax.ShapeDtypeStruct((t_tok, 2 * D), BF16), bf, bf, bf, bf,
                   jax.ShapeDtypeStruct((1, D), F32), jax.ShapeDtypeStruct((1, 2 * D), F32)],
        scratch=[pltpu.VMEM((D, D), BF16)] * 3 + [DMA((3 * NDEV,))],
        vmem=VMEM_BIG, carries=carries)


def _sum_parts(parts_list):
    n = len(parts_list)
    _, r, c = parts_list[0].shape
    tc = 256

    def body(*refs):
        for p_ref, o_ref in zip(refs[:n], refs[n:]):
            acc = p_ref[0].astype(F32)
            for s in range(1, NDEV):
                acc = acc + p_ref[s].astype(F32)
            o_ref[...] = acc

    return pl.pallas_call(
        body, name=f"sum_parts_{r}", grid=(c // tc,),
        in_specs=[pl.BlockSpec((NDEV, r, tc), lambda i: (0, 0, i))] * n,
        out_specs=[pl.BlockSpec((r, tc), lambda i: (0, i))] * n,
        out_shape=[jax.ShapeDtypeStruct((r, c), F32)] * n,
        compiler_params=pltpu.CompilerParams(dimension_semantics=("arbitrary",), vmem_limit_bytes=48 * 2 ** 20),
    )(*parts_list)


def _adamw_math(w, g, m, v):
    m = ADAM_B1 * m + (1.0 - ADAM_B1) * g
    v = ADAM_B2 * v + (1.0 - ADAM_B2) * (g * g)
    m_hat = m / (1.0 - ADAM_B1 ** ADAM_STEP)
    v_hat = v / (1.0 - ADAM_B2 ** ADAM_STEP)
    delta = -ADAM_LR * (m_hat / (jnp.sqrt(v_hat) + ADAM_EPS) + ADAM_WD * w)
    return delta, m, v


def _sum_ready(parts_list, name, carries=(), chip_sums=()):
    n = len(parts_list)
    c = parts_list[0].shape[2]
    tc = 2 * LANES

    def body(*refs):
        for k, (p_ref, o_ref) in enumerate(zip(refs[:n], refs[n:])):
            if k in chip_sums:
                core = lax.axis_index("c")
                slots = [2 * ch + core for ch in range(NDEV // 2)]
            else:
                slots = list(range(NDEV))
            g = p_ref[slots[0]].astype(F32)
            for s in slots[1:]:
                g = g + p_ref[s].astype(F32)
            o_ref[...] = g

    return _call(
        body, name=name, grid=(c // tc,), ins=list(parts_list),
        in_specs=[pl.BlockSpec((NDEV, p.shape[1], tc), lambda i: (0, 0, i)) for p in parts_list],
        out_shape=[jax.ShapeDtypeStruct((p.shape[1], c), F32) for p in parts_list],
        out_specs=[pl.BlockSpec((p.shape[1], tc), lambda i: (0, i)) for p in parts_list],
        vmem=48 * 2 ** 20, carries=carries)


def _adamw_cols(items, name):
    n = len(items)
    c = items[0][1].shape[1]
    tc = LANES

    def body(*refs):
        for k in range(n):
            g_ref, w_ref, m_ref, v_ref = refs[4 * k:4 * k + 4]
            go_ref, d_ref, nm_ref, nv_ref = refs[4 * n + 4 * k:4 * n + 4 * k + 4]
            g = g_ref[...]
            d, m, v = _adamw_math(w_ref[...], g, m_ref[...], v_ref[...])
            go_ref[...] = g
            d_ref[...] = d
            nm_ref[...] = m
            nv_ref[...] = v

    ins, specs, out_shape = [], [], []
    for g, w, m, v in items:
        r = w.shape[0]
        ins += [g, w, m, v]
        specs += [pl.BlockSpec((r, tc), lambda i: (0, i))] * 4
        out_shape += [jax.ShapeDtypeStruct((r, c), F32)] * 4
    outs, _ = _call(body, name=name, grid=(c // tc,), ins=ins, in_specs=specs, out_shape=out_shape,
                    out_specs=specs, vmem=48 * 2 ** 20)
    return [tuple(outs[4 * k:4 * k + 4]) for k in range(n)]


def _adamw_body(n):
    def body(*refs):
        for k in range(n):
            g_ref, w_ref, m_ref, v_ref = refs[4 * k:4 * k + 4]
            d_ref, nm_ref, nv_ref = refs[4 * n + 3 * k:4 * n + 3 * k + 3]
            d, m, v = _adamw_math(w_ref[...], g_ref[...], m_ref[...], v_ref[...])
            d_ref[...] = d
            nm_ref[...] = m
            nv_ref[...] = v
    return body


def _adamw(items):
    n = len(items)
    r, c = items[0][0].shape
    tr = r if r * c <= 2 ** 18 else max(t for t in range(8, 65, 8) if r % t == 0)
    spec = pl.BlockSpec((tr, c), lambda i: (i, 0))
    outs = pl.pallas_call(
        _adamw_body(n), name=f"adamw_{r}x{c}", grid=(r // tr,),
        in_specs=[spec] * (4 * n), out_specs=[spec] * (3 * n),
        out_shape=[jax.ShapeDtypeStruct((r, c), F32)] * (3 * n),
        compiler_params=pltpu.CompilerParams(dimension_semantics=("arbitrary",), vmem_limit_bytes=40 * 2 ** 20),
    )(*[a for it in items for a in it])
    return [tuple(outs[3 * k:3 * k + 3]) for k in range(n)]


def _adamw_small(items):
    n = len(items)
    vm = pl.BlockSpec(memory_space=pltpu.VMEM)
    outs = pl.pallas_call(
        _adamw_body(n), name="adamw_small", in_specs=[vm] * (4 * n), out_specs=[vm] * (3 * n),
        out_shape=[jax.ShapeDtypeStruct(it[1].shape, F32) for it in items for _ in range(3)],
    )(*[a for it in items for a in it])
    return [tuple(outs[3 * k:3 * k + 3]) for k in range(n)]


def _pack_small(arrs):
    rows, offs = [], []
    total = 0
    for a in arrs:
        flat = a.reshape(-1).astype(F32)
        nr = -(-flat.shape[0] // LANES)
        flat = jnp.pad(flat, (0, nr * LANES - flat.shape[0]))
        rows.append(flat.reshape(nr, LANES))
        offs.append((total, nr))
        total += nr
    pad = -total % 8
    if pad:
        rows.append(jnp.zeros((pad, LANES), F32))
    return jnp.concatenate(rows, axis=0), offs


def _unpack_small(pack, offs, shapes):
    out = []
    for (o, nr), shp in zip(offs, shapes):
        size = int(np.prod(shp))
        out.append(pack[o:o + nr].reshape(-1)[:size].reshape(shp))
    return out


def _sum_small(gathered, rows):
    def body(p_ref, o_ref):
        acc = p_ref[pl.ds(0, rows), :]
        for s in range(1, NDEV):
            acc = acc + p_ref[pl.ds(s * rows, rows), :]
        o_ref[...] = acc

    return pl.pallas_call(
        body, name="sum_small", out_shape=jax.ShapeDtypeStruct((rows, LANES), F32),
        in_specs=[pl.BlockSpec(memory_space=pltpu.VMEM)], out_specs=pl.BlockSpec(memory_space=pltpu.VMEM))(gathered)


def _block_diag(w):
    w = w.reshape(D // LANES, 2, LRU_BLOCK, LRU_BLOCK)
    z = jnp.zeros((D // LANES, LRU_BLOCK, LRU_BLOCK), w.dtype)
    top = jnp.concatenate([w[:, 0], z], axis=2)
    bot = jnp.concatenate([z, w[:, 1]], axis=2)
    return jnp.concatenate([top, bot], axis=1)


def _block_diag_grad(dw):
    a = dw[:, :LRU_BLOCK, :LRU_BLOCK]
    b = dw[:, LRU_BLOCK:, LRU_BLOCK:]
    return jnp.stack([a, b], axis=1).reshape(D // LRU_BLOCK, LRU_BLOCK, LRU_BLOCK)


def kernel(x, ffn1_pre_g, ffn1_w1, ffn1_w3, ffn1_w2, ffn1_post_g, mix_pre_g, w_in, conv_w, conv_b, rg_a_w, rg_a_b, rg_x_w, rg_x_b, lru_lambda, w_lru_out, attn_sinks, rel_bias, w_attn_out, w_gate, b_gate, w_o, mix_post_g, ffn2_pre_g, ffn2_w1, ffn2_w3, ffn2_w2, ffn2_post_g, loss_target, m_ffn1_pre_g, m_ffn1_w1, m_ffn1_w3, m_ffn1_w2, m_ffn1_post_g, m_mix_pre_g, m_w_in, m_conv_w, m_conv_b, m_rg_a_w, m_rg_a_b, m_rg_x_w, m_rg_x_b, m_lru_lambda, m_w_lru_out, m_attn_sinks, m_rel_bias, m_w_attn_out, m_w_gate, m_b_gate, m_w_o, m_mix_post_g, m_ffn2_pre_g, m_ffn2_w1, m_ffn2_w3, m_ffn2_w2, m_ffn2_post_g, v_ffn1_pre_g, v_ffn1_w1, v_ffn1_w3, v_ffn1_w2, v_ffn1_post_g, v_mix_pre_g, v_w_in, v_conv_w, v_conv_b, v_rg_a_w, v_rg_a_b, v_rg_x_w, v_rg_x_b, v_lru_lambda, v_w_lru_out, v_attn_sinks, v_rel_bias, v_w_attn_out, v_w_gate, v_b_gate, v_w_o, v_mix_post_g, v_ffn2_pre_g, v_ffn2_w1, v_ffn2_w3, v_ffn2_w2, v_ffn2_post_g):
    names = ["ffn1_pre_g", "ffn1_w1", "ffn1_w3", "ffn1_w2", "ffn1_post_g", "mix_pre_g", "w_in", "conv_w", "conv_b",
             "rg_a_w", "rg_a_b", "rg_x_w", "rg_x_b", "lru_lambda", "w_lru_out", "attn_sinks", "rel_bias", "w_attn_out",
             "w_gate", "b_gate", "w_o", "mix_post_g", "ffn2_pre_g", "ffn2_w1", "ffn2_w3", "ffn2_w2", "ffn2_post_g"]
    ws = dict(zip(names, (ffn1_pre_g, ffn1_w1, ffn1_w3, ffn1_w2, ffn1_post_g, mix_pre_g, w_in, conv_w, conv_b, rg_a_w,
                          rg_a_b, rg_x_w, rg_x_b, lru_lambda, w_lru_out, attn_sinks, rel_bias, w_attn_out, w_gate,
                          b_gate, w_o, mix_post_g, ffn2_pre_g, ffn2_w1, ffn2_w3, ffn2_w2, ffn2_post_g)))
    ms = dict(zip(names, (m_ffn1_pre_g, m_ffn1_w1, m_ffn1_w3, m_ffn1_w2, m_ffn1_post_g, m_mix_pre_g, m_w_in, m_conv_w,
                          m_conv_b, m_rg_a_w, m_rg_a_b, m_rg_x_w, m_rg_x_b, m_lru_lambda, m_w_lru_out, m_attn_sinks,
                          m_rel_bias, m_w_attn_out, m_w_gate, m_b_gate, m_w_o, m_mix_post_g, m_ffn2_pre_g, m_ffn2_w1,
                          m_ffn2_w3, m_ffn2_w2, m_ffn2_post_g)))
    vs = dict(zip(names, (v_ffn1_pre_g, v_ffn1_w1, v_ffn1_w3, v_ffn1_w2, v_ffn1_post_g, v_mix_pre_g, v_w_in, v_conv_w,
                          v_conv_b, v_rg_a_w, v_rg_a_b, v_rg_x_w, v_rg_x_b, v_lru_lambda, v_w_lru_out, v_attn_sinks,
                          v_rel_bias, v_w_attn_out, v_w_gate, v_b_gate, v_w_o, v_mix_post_g, v_ffn2_pre_g, v_ffn2_w1,
                          v_ffn2_w3, v_ffn2_w2, v_ffn2_post_g)))
    me = 4 * lax.axis_index("x") + 2 * lax.axis_index("y") + lax.axis_index("c")
    vec = lambda n: ws[n].reshape(1, -1)

    def shard2d(name):
        if name == "conv":
            row = lax.bitcast_convert_type(conv_w.reshape(CONV_WIDTH, LANES), BF16).reshape(1, D)
            return jnp.concatenate([row, jnp.zeros((LOC["conv"][2] - 1, D), BF16)], axis=0)
        a = ws[name].reshape(ws[name].shape[-2], ws[name].shape[-1])
        return (a.T if name in COL_SHARDED else a).astype(BF16)

    packs = {g: jnp.concatenate([shard2d(n) for n, _ in members], axis=0) for g, members in GROUPS}

    wg = {}
    _, ((wg["ffn1"],),) = _call(None, name="allgather_ffn1", grid=(), ins=[], in_specs=[], out_shape=[], out_specs=[],
                                carries=[_AllGatherTree(packs["ffn1"])])
    conv_rows = wg["ffn1"].reshape(NDEV, GROUP_ROWS["ffn1"], D)[:, LOC["conv"][1]]
    cw = jnp.transpose(lax.bitcast_convert_type(conv_rows.reshape(NDEV, CONV_WIDTH, LANES, 2), F32),
                       (1, 0, 2)).reshape(CONV_WIDTH, D)
    bmap = jnp.asarray(_bucket_map())
    bias = _bias_tile(rel_bias, bmap).reshape(N_KV, REP * QT, KW)
    sinks = attn_sinks.reshape(N_HEADS)
    wa_bd = _block_diag(rg_a_w.reshape(D // LRU_BLOCK, LRU_BLOCK, LRU_BLOCK)).astype(BF16)
    wx_bd = _block_diag(rg_x_w.reshape(D // LRU_BLOCK, LRU_BLOCK, LRU_BLOCK)).astype(BF16)
    lru_args = (cw, vec("conv_b"), wa_bd, wx_bd, vec("rg_a_b"), vec("rg_x_b"), vec("lru_lambda"))
    x2, tgt = x[0], loss_target[0]

    (h1, a1, b1, f1, nb1), ((wg["mixin"],),) = _ffn_fwd(
        x2, vec("ffn1_pre_g"), vec("ffn1_post_g"), wg, ("ffn1_w1", "ffn1_w3", "ffn1_w2"),
        carries=[_AllGather(packs["mixin"])])
    (q, kd, vd, xr, xg, gs, ub), ((wg["mixout"],),) = _mix_proj_fwd(
        h1, vec("mix_pre_g"), vec("b_gate"), wg, carries=[_AllGather(packs["mixout"])])
    (ya_in, hseq), ((wg["ffn2a"],),) = _lru_fwd(xr, xg, *lru_args, carries=[_AllGather(packs["ffn2a"])])
    (o, lse), ((wg["ffn2b"],),) = _attn_fwd(q, kd, vd, bias, sinks, carries=[_AllGather(packs["ffn2b"])])
    (h2, ya, yb, z), _ = _mix_out_fwd(ya_in, o, gs, h1, vec("mix_post_g"), wg)
    (_, a2, b2, f2, nb2, dy, loss_part), _ = _ffn_fwd(h2, vec("ffn2_pre_g"), vec("ffn2_post_g"), wg,
                                                     ("ffn2_w1", "ffn2_w3", "ffn2_w2"), target=tgt)

    gsm, parts = {}, {}
    rs = lambda *grads_: [_ReduceScatterSend(list(grads_))]
    ffr = FF // NDEV
    (dab, s_act, dfb, gsm["ffn2_post_g"]), _ = _ffn_bwd_a(dy, f2, a2, b2, vec("ffn2_post_g"), wg, "ffn2_w2")
    g_w2, _ = _dw(s_act, dfb, FF // 2, "dw_ffn2_w2")
    g_w13, ((parts["ffn2_w2"],),) = _dw(dab, nb2, D // 2, "dw_ffn2_w13", carries=rs(g_w2))
    (dh2, gsm["ffn2_pre_g"]), ((parts["ffn2_w1"],),) = _ffn_bwd_dx(
        dab, h2, dy, vec("ffn2_pre_g"), wg, "ffn2_w1", "ffn2_w3", carries=rs((g_w13, 0, ffr, ffr)))
    (dya_in, do, dgpre, dya, dyb, mg, dzb, gsm["mix_post_g"], gsm["b_gate"]), ((parts["ffn2_w3"],),) = _mix_out_bwd(
        dh2, z, ya, yb, gs, vec("mix_post_g"), wg, carries=rs((g_w13, FF, ffr, ffr)))
    g_wa, _ = _dw(ya_in, dya, D // 2, "dw_w_lru_out")
    g_wb, _ = _dw(o, dyb, D // 2, "dw_w_attn_out")
    g_wo, _ = _dw(mg, dzb, D // 2, "dw_w_o")
    g_wgate, _ = _dw(dgpre, ub, D // 2, "dw_w_gate")
    (dqkv, ds_acc, dsink), (mixout_parts,) = _attn_bwd(
        q, kd, vd, o, do, lse, bias, sinks, carries=rs(g_wa, g_wb, g_wo, g_wgate))
    parts["w_lru_out"], parts["w_attn_out"], parts["w_o"], parts["w_gate"] = mixout_parts
    (dxr, dxg, dvec, dwa, dwx), _ = _lru_bwd(dya_in, xr, xg, hseq, *lru_args)
    g_qkv, _ = _dw(dqkv, ub, D // 2, "dw_w_in_qkv")
    g_xr, _ = _dw(dxr, ub, D // 2, "dw_w_in_xr")
    g_xg, _ = _dw(dxg, ub, D // 2, "dw_w_in_xg")
    g_win = jnp.concatenate(
        [g_qkv[:D]] + [g_qkv[c0 + gi * LANES:c0 + gi * LANES + HEAD_DIM] for c0 in (K0, V0) for gi in range(N_KV)]
        + [g_xr, g_xg], axis=0)
    wir, wir_a = IN_W // NDEV, 336
    (dh1, gsm["mix_pre_g"]), ((win_a,),) = _mix_proj_bwd(
        dqkv, dxr, dxg, dgpre, h1, dh2, vec("mix_pre_g"), wg, carries=rs((g_win, 0, wir, wir_a)))
    gsm["conv_w"] = dvec[0:CONV_WIDTH]
    gsm["conv_b"], gsm["rg_a_b"], gsm["rg_x_b"], gsm["lru_lambda"] = dvec[4], dvec[5], dvec[6], dvec[7]
    gsm["rg_a_w"] = _block_diag_grad(dwa)
    gsm["rg_x_w"] = _block_diag_grad(dwx)
    gsm["attn_sinks"] = dsink[0, :N_HEADS]
    gsm["rel_bias"] = _bias_grad(ds_acc.reshape(N_HEADS, QT, KW), bmap)[:, :N_HEADS]
    late = ("ffn1_post_g", "ffn1_pre_g")
    early = tuple(n for n in SMALL if n not in late)
    early_pack, early_offs = _pack_small([gsm[n] for n in early])
    (dab, s_act, dfb, gsm["ffn1_post_g"]), ((win_b,), (early_all,)) = _ffn_bwd_a(
        dh1, f1, a1, b1, vec("ffn1_post_g"), wg, "ffn1_w2",
        carries=rs((g_win, wir_a, wir, wir - wir_a)) + [_AllGather(early_pack)])
    parts["w_in"] = jnp.concatenate([win_a, win_b], axis=1)
    g_w2, _ = _dw(s_act, dfb, FF // 2, "dw_ffn1_w2")
    g_w13, ((parts["ffn1_w2"],),) = _dw(dab, nb1, D // 2, "dw_ffn1_w13", carries=rs(g_w2))
    chip_w13 = _pair_sum(g_w13, [(0, ffr), (FF, ffr)])
    (grad_x, gsm["ffn1_pre_g"]), ((parts["ffn1_w1"], parts["ffn1_w3"]),) = _ffn_bwd_dx(
        dab, x2, dh1, vec("ffn1_pre_g"), wg, "ffn1_w1", "ffn1_w3",
        carries=[_ReduceScatterSend([(chip_w13, 0, ffr, ffr), (chip_w13, FF, ffr, ffr)], chip_sums=True)])
    late_pack, late_offs = _pack_small([gsm[n] for n in late] + [loss_part])

    grads, delta, new_m, new_v = {}, {}, {}, {}
    t_form = [n for n in COL_SHARDED if ws[n].shape[-1] % LANES]
    view = {n: (lambda a: a.reshape(a.shape[-2], a.shape[-1]).T) if n in t_form
            else (lambda a: a.reshape(a.shape[-2], a.shape[-1])) for n in BIG}
    unview = {n: (lambda a: a.T) if n in t_form else (lambda a: a) for n in BIG}

    group = [n for n in BIG if n != "w_gate"]
    sums, ((late_all,),) = _sum_ready([parts[n] for n in group], "sum_ready", [_AllGather(late_pack)],
                                      chip_sums=(group.index("ffn1_w1"), group.index("ffn1_w3")))
    res = _adamw_cols([(g, view[n](ws[n]), view[n](ms[n]), view[n](vs[n])) for n, g in zip(group, sums)], "adamw_big")
    for n, quad in zip(group, res):
        grads[n], delta[n], new_m[n], new_v[n] = (unview[n](a) for a in quad)
    (g_gate_t,) = _sum_parts([parts["w_gate"]])
    grads["w_gate"] = g_gate_t.T
    ((delta["w_gate"], new_m["w_gate"], new_v["w_gate"]),) = _adamw(
        [(grads["w_gate"], view["w_gate"](ws["w_gate"]), view["w_gate"](ms["w_gate"]), view["w_gate"](vs["w_gate"]))])
    for group, gathered, offs in ((early, early_all, early_offs), (late, late_all, late_offs)):
        total = _sum_small(gathered, gathered.shape[0] // NDEV)
        shapes = [(CONV_WIDTH, D) if n == "conv_w" else ws[n].shape for n in group]
        if group is late:
            shapes = shapes + [(1, LANES)]
        unpacked = _unpack_small(total, offs, shapes)
        if group is late:
            loss = unpacked.pop()[0, 0]
        for n, g in zip(group, unpacked):
            grads[n] = g
    grads["conv_w"] = lax.dynamic_slice(grads["conv_w"], (0, me * LANES), (CONV_WIDTH, LANES)).reshape(conv_w.shape)

    flat2d = lambda a: a.reshape(-1, a.shape[-1])
    res = _adamw_small([(flat2d(grads[n].reshape(ws[n].shape)), flat2d(ws[n]), flat2d(ms[n]), flat2d(vs[n]))
                        for n in SMALL])
    for n, (d_, m_, v_) in zip(SMALL, res):
        delta[n], new_m[n], new_v[n] = d_, m_, v_

    outs = [loss, grad_x.reshape(x.shape)]
    for src in (grads, delta, new_m, new_v):
        outs += [src[n].reshape(ws[n].shape) for n in names]
    return tuple(outs)
```

```python
import functools
import math
import operator

import numpy as np
import jax
import jax.numpy as jnp
from jax import lax
from jax.experimental import pallas as pl
from jax.experimental.pallas import tpu as pltpu

F32, BF16 = jnp.float32, jnp.bfloat16

NDEV = 8
D = 1024
FF = 2816
N_HEADS, N_KV, HEAD_DIM = 16, 4, 64
CHUNK, WINDOW = 64, 128
N_BUCKETS, MAX_DISTANCE = 32, 128
LRU_BLOCK = 64
CONV_WIDTH = 4
LRU_C = 8.0
RMS_EPS = 1e-6
NEG_INF = -1e30
LANES = 128
QT = 128
KW = QT + WINDOW
LRU_ROWS = 1024
IN_W = D + 2 * N_KV * HEAD_DIM + 2 * D
INP_W = D + 2 * N_KV * LANES + 2 * D
VMEM_BIG = 58 * 2 ** 20

ADAM_LR, ADAM_B1, ADAM_B2, ADAM_EPS, ADAM_WD, ADAM_STEP = 0.001, 0.9, 0.999, 1e-08, 0.01, 10

GROUPS = (("ffn1", (("ffn1_w1", FF // NDEV), ("ffn1_w3", FF // NDEV), ("ffn1_w2", FF // NDEV), ("conv", 16))),
          ("mixin", (("w_in", IN_W // NDEV), ("w_gate", 2 * D // NDEV))),
          ("mixout", (("w_lru_out", D // NDEV), ("w_attn_out", D // NDEV), ("w_o", D // NDEV))),
          ("ffn2a", (("ffn2_w1", FF // NDEV), ("ffn2_w3", FF // NDEV))),
          ("ffn2b", (("ffn2_w2", FF // NDEV),)))
LOC, GROUP_ROWS = {}, {}
for _g, _members in GROUPS:
    _o = 0
    for _n, _r in _members:
        LOC[_n] = (_g, _o, _r)
        _o += _r
    GROUP_ROWS[_g] = _o
BIG = tuple(n for _, members in GROUPS for n, _ in members if n != "conv")
COL_SHARDED = ("ffn1_w1", "ffn1_w3", "w_in", "w_gate", "ffn2_w1", "ffn2_w3")

SMALL = ("ffn1_pre_g", "ffn1_post_g", "mix_pre_g", "conv_w", "conv_b", "rg_a_w", "rg_a_b", "rg_x_w", "rg_x_b",
         "lru_lambda", "attn_sinks", "rel_bias", "b_gate", "mix_post_g", "ffn2_pre_g", "ffn2_post_g")

MESH = pl.DeviceIdType.MESH
ANY = pl.BlockSpec(memory_space=pl.ANY)
DMA = pltpu.SemaphoreType.DMA


def _nn(a, b):
    return lax.dot_general(a, b, (((1,), (0,)), ((), ())), preferred_element_type=F32)


def _nt(a, b):
    return lax.dot_general(a, b, (((1,), (1,)), ((), ())), preferred_element_type=F32)


def _tn(a, b):
    return lax.dot_general(a, b, (((0,), (0,)), ((), ())), preferred_element_type=F32)


def _rms_fwd(x, g):
    r = lax.rsqrt(jnp.mean(x * x, axis=-1, keepdims=True) + RMS_EPS)
    xh = x * r
    return xh * g, xh, r


def _rms_bwd(dn, xh, r, g):
    dxh = dn * g
    dx = r * (dxh - xh * jnp.mean(dxh * xh, axis=-1, keepdims=True))
    return dx, jnp.sum(dn * xh, axis=0, keepdims=True)


def _row_spec(tm, c):
    return pl.BlockSpec((tm, c), lambda i: (i, 0))


def _const_spec(shape):
    nd = len(shape)
    return pl.BlockSpec(shape, lambda i: (0,) * nd)


class _AllGather:
    def __init__(self, shard):
        self.m, n = shard.shape
        self.ins = [shard]
        self.out_shape = [jax.ShapeDtypeStruct((NDEV * self.m, n), shard.dtype)]
        self.scratch = [DMA((7,)), DMA((7,)), DMA]

    def _copies(self, ins, outs, scr, all_of_them):
        x_ref, out_ref = ins[0], outs[0]
        send_sems, recv_sems, local_sem = scr
        x, y, c = lax.axis_index("x"), lax.axis_index("y"), lax.axis_index("c")
        me, sibling = (x, y, c), (x, y, 1 - c)
        chips = [(1 - x, y), (x, 1 - y), (1 - x, 1 - y)]
        m = self.m

        def rows(px, py, pc):
            return out_ref.at[pl.ds((4 * px + 2 * py + pc) * m, m), :]

        def copy(k, block, to, src=None):
            return pltpu.make_async_remote_copy(
                src_ref=rows(*block) if src is None else src, dst_ref=rows(*block),
                send_sem=send_sems.at[k], recv_sem=recv_sems.at[k], device_id=to, device_id_type=MESH)

        mine = pltpu.make_async_copy(x_ref, rows(*me), local_sem)
        first = [copy(0, me, sibling, src=x_ref)] + [copy(1 + j, me, (*chip, c), src=x_ref)
                                                     for j, chip in enumerate(chips)]
        if not all_of_them:
            return mine, first
        passed = [copy(4 + j, (*chip, c), sibling) for j, chip in enumerate(chips)]
        landed = [copy(1 + j, (*chip, c), me) for j, chip in enumerate(chips)]
        from_sibling = [copy(0, sibling, me)] + [copy(4 + j, (*chip, 1 - c), me) for j, chip in enumerate(chips)]
        return mine, first, passed, landed, from_sibling

    def start(self, ins, outs, scr):
        mine, first = self._copies(ins, outs, scr, False)
        mine.start()
        for cp in first:
            cp.start()

    def finish(self, ins, outs, scr):
        mine, first, passed, landed, from_sibling = self._copies(ins, outs, scr, True)
        for cp_in, cp_on in zip(landed, passed):
            cp_in.wait_recv()
            cp_on.start()
        for cp in from_sibling:
            cp.wait_recv()
        for cp in first + passed:
            cp.wait_send()
        mine.wait()


class _AllGatherTree:
    def __init__(self, shard):
        self.m, n = shard.shape
        self.half = self.m // 32 * 16
        self.ins = [shard]
        self.out_shape = [jax.ShapeDtypeStruct((NDEV * self.m, n), shard.dtype)]
        self.scratch = [DMA((9,)), DMA((9,)), DMA]

    def start(self, ins, outs, scr):
        x_ref, out_ref = ins[0], outs[0]
        send_sems, recv_sems, local_sem = scr
        x, y, c = lax.axis_index("x"), lax.axis_index("y"), lax.axis_index("c")
        m, half = self.m, self.half
        me, sibling, xn, yn, diag = (x, y, c), (x, y, 1 - c), (1 - x, y, c), (x, 1 - y, c), (1 - x, 1 - y, c)
        other = lambda dev: (dev[0], dev[1], 1 - c)

        def rows(dev, lo=0, n=m):
            return out_ref.at[pl.ds((4 * dev[0] + 2 * dev[1] + dev[2]) * m + lo, n), :]

        def copy(k, dev, to, lo=0, n=m, src=None):
            return pltpu.make_async_remote_copy(
                src_ref=rows(dev, lo, n) if src is None else src, dst_ref=rows(dev, lo, n),
                send_sem=send_sems.at[k], recv_sem=recv_sems.at[k], device_id=to, device_id_type=MESH)

        mine = pltpu.make_async_copy(x_ref, rows(me), local_sem)
        own = [copy(0, me, sibling, src=x_ref), copy(1, me, xn, src=x_ref), copy(2, me, yn, src=x_ref)]
        mine.start()
        for cp in own:
            cp.start()
        steps = [(copy(1, xn, me), [copy(3, xn, yn, 0, half), copy(5, xn, sibling)]),
                 (copy(2, yn, me), [copy(4, yn, xn, half, m - half), copy(6, yn, sibling)]),
                 (copy(3, diag, me, 0, half), [copy(7, diag, sibling, 0, half)]),
                 (copy(4, diag, me, half, m - half), [copy(8, diag, sibling, half, m - half)])]
        from_sibling = [copy(0, sibling, me), copy(5, other(xn), me), copy(6, other(yn), me),
                        copy(7, other(diag), me, 0, half), copy(8, other(diag), me, half, m - half)]
        passed = []
        for landing, onward in steps:
            landing.wait_recv()
            for cp in onward:
                cp.start()
            passed += onward
        for cp in from_sibling:
            cp.wait_recv()
        for cp in own + passed:
            cp.wait_send()
        mine.wait()

    def finish(self, ins, outs, scr):
        pass


class _ReduceScatterSend:
    def __init__(self, grads, chip_sums=False):
        grads = [g if isinstance(g, tuple) else (g, 0, g.shape[0] // NDEV, g.shape[0] // NDEV) for g in grads]
        self.peers = (2, 4, 6) if chip_sums else tuple(range(1, NDEV))
        self.nw = len(grads)
        self.base = [b for _, b, _, _ in grads]
        self.stride = [s for _, _, s, _ in grads]
        self.rows = [r for _, _, _, r in grads]
        self.ins = [g for g, _, _, _ in grads]
        self.out_shape = [jax.ShapeDtypeStruct((NDEV, r, g.shape[1]), g.dtype) for g, _, _, r in grads]
        self.scratch = [DMA((self.nw, NDEV - 1)), DMA((self.nw, NDEV - 1)), DMA((self.nw,))]

    def _copies(self, g_refs, r_refs, scr, want):
        send_sems, recv_sems, local_sems = scr
        x, y, c = lax.axis_index("x"), lax.axis_index("y"), lax.axis_index("c")
        me = 4 * x + 2 * y + c
        rows, base, stride = self.rows, self.base, self.stride
        out = []
        if want == "local":
            for w in range(self.nw):
                out.append(pltpu.make_async_copy(g_refs[w].at[pl.ds(base[w] + me * stride[w], rows[w])],
                                                 r_refs[w].at[me], local_sems.at[w]))
            return out
        for k in self.peers:
            px, py, pc = x ^ (k >> 2), y ^ ((k >> 1) & 1), c ^ (k & 1)
            peer = 4 * px + 2 * py + pc
            for w in range(self.nw):
                sems = dict(send_sem=send_sems.at[w, k - 1], recv_sem=recv_sems.at[w, k - 1],
                            device_id=(px, py, pc), device_id_type=MESH)
                if want == "send":
                    out.append(pltpu.make_async_remote_copy(
                        src_ref=g_refs[w].at[pl.ds(base[w] + peer * stride[w], rows[w])], dst_ref=r_refs[w].at[me],
                        **sems))
                else:
                    out.append(pltpu.make_async_remote_copy(
                        src_ref=g_refs[w].at[pl.ds(0, rows[w])], dst_ref=r_refs[w].at[peer], **sems))
        return out

    def start(self, ins, outs, scr):
        for cp in self._copies(ins, outs, scr, "local") + self._copies(ins, outs, scr, "send"):
            cp.start()

    def finish(self, ins, outs, scr):
        for cp in self._copies(ins, outs, scr, "recv"):
            cp.wait_recv()
        for cp in self._copies(ins, outs, scr, "send"):
            cp.wait_send()
        for cp in self._copies(ins, outs, scr, "local"):
            cp.wait()


def _call(body, *, name, grid, ins, in_specs, out_shape, out_specs, scratch=(), vmem=None, carries=()):
    n_in, n_out, n_scr = len(ins), len(out_shape), len(scratch)
    ng = len(grid)

    def split(refs):
        pos = [0]

        def take(k):
            part = refs[pos[0]:pos[0] + k]
            pos[0] += k
            return part

        i_refs = take(n_in)
        c_in = [take(len(c.ins)) for c in carries]
        o_refs = take(n_out)
        c_out = [take(len(c.out_shape)) for c in carries]
        s_refs = take(n_scr)
        c_scr = [take(len(c.scratch)) for c in carries]
        return i_refs, o_refs, s_refs, list(zip(carries, c_in, c_out, c_scr))

    def full(*refs):
        i_refs, o_refs, s_refs, cparts = split(refs)
        if ng == 0:
            for c, a, b, s in cparts:
                c.start(a, b, s)
            for c, a, b, s in cparts:
                c.finish(a, b, s)
            return
        ids = [pl.program_id(a) for a in range(ng)]
        if cparts:
            @pl.when(functools.reduce(operator.and_, [i == 0 for i in ids]))
            def _():
                for c, a, b, s in cparts:
                    c.start(a, b, s)

        body(*i_refs, *o_refs, *s_refs)
        if cparts:
            @pl.when(functools.reduce(operator.and_, [i == g - 1 for i, g in zip(ids, grid)]))
            def _():
                for c, a, b, s in cparts:
                    c.finish(a, b, s)

    all_ins = list(ins) + [a for c in carries for a in c.ins]
    all_in_specs = list(in_specs) + [ANY for c in carries for _ in c.ins]
    all_out_shape = list(out_shape) + [s for c in carries for s in c.out_shape]
    all_out_specs = list(out_specs) + [ANY for c in carries for _ in c.out_shape]
    all_scratch = list(scratch) + [s for c in carries for s in c.scratch]
    kwargs = dict(grid=grid) if ng else {}
    outs = pl.pallas_call(
        full, name=name, in_specs=all_in_specs, out_specs=all_out_specs, out_shape=all_out_shape,
        scratch_shapes=all_scratch,
        compiler_params=pltpu.CompilerParams(dimension_semantics=("arbitrary",) * ng if ng else None,
                                             vmem_limit_bytes=vmem),
        **kwargs)(*all_ins)
    outs = list(outs)
    res, pos = outs[:n_out], n_out
    carried = []
    for c in carries:
        carried.append(outs[pos:pos + len(c.out_shape)])
        pos += len(c.out_shape)
    return res, carried


def _pair_sum(g, items):
    rows = items[0][1]
    n = len(items) * (NDEV // 2)
    width = g.shape[1]

    def body(g_ref, out_ref, got_ref, mine_v, got_v, send_sems, recv_sems, ld_sems, st_sems):
        x, y, c = lax.axis_index("x"), lax.axis_index("y"), lax.axis_index("c")
        keep = [base + (2 * ch + c) * rows for base, _ in items for ch in range(NDEV // 2)]
        give = [base + (2 * ch + 1 - c) * rows for base, _ in items for ch in range(NDEV // 2)]

        def remote(j, r):
            return pltpu.make_async_remote_copy(
                src_ref=g_ref.at[pl.ds(r, rows)], dst_ref=got_ref.at[pl.ds(r, rows)], send_sem=send_sems.at[j],
                recv_sem=recv_sems.at[j], device_id=(x, y, 1 - c), device_id_type=MESH)

        sends = [remote(j, r) for j, r in enumerate(give)]
        for cp in sends:
            cp.start()
        loads = [pltpu.make_async_copy(g_ref.at[pl.ds(r, rows)], mine_v.at[j], ld_sems.at[0, j])
                 for j, r in enumerate(keep)]
        for cp in loads:
            cp.start()
        stores = []
        for j, r in enumerate(keep):
            remote(j, r).wait_recv()
            cp = pltpu.make_async_copy(got_ref.at[pl.ds(r, rows)], got_v.at[j], ld_sems.at[1, j])
            cp.start()
            loads[j].wait()
            cp.wait()
            mine_v[j] = (mine_v[j].astype(F32) + got_v[j].astype(F32)).astype(BF16)
            st = pltpu.make_async_copy(mine_v.at[j], out_ref.at[pl.ds(r, rows)], st_sems.at[j])
            st.start()
            stores.append(st)
        for cp in sends:
            cp.wait_send()
        for cp in stores:
            cp.wait()

    out, _ = pl.pallas_call(
        body, name="pair_sum", in_specs=[ANY], out_specs=[ANY, ANY],
        out_shape=[jax.ShapeDtypeStruct(g.shape, g.dtype), jax.ShapeDtypeStruct(g.shape, g.dtype)],
        scratch_shapes=[pltpu.VMEM((n, rows, width), g.dtype), pltpu.VMEM((n, rows, width), g.dtype),
                        DMA((n,)), DMA((n,)), DMA((2, n)), DMA((n,))],
        compiler_params=pltpu.CompilerParams(vmem_limit_bytes=40 * 2 ** 20))(g)
    return out


def _groups_of(names):
    out = []
    for n in names:
        if LOC[n][0] not in out:
            out.append(LOC[n][0])
    return out


def _weight_pieces(name):
    g, off, rows = LOC[name]
    return [(d * GROUP_ROWS[g] + off, d * rows, rows) for d in range(NDEV)]


def _win_pieces():
    kv = N_KV * HEAD_DIM
    pieces = [(0, 0, D)]
    for g0, d0 in ((D, D), (D + kv, D + N_KV * LANES)):
        for g in range(N_KV):
            for half in range(2):
                pieces.append((g0 + g * HEAD_DIM, d0 + g * LANES + half * HEAD_DIM, HEAD_DIM))
    pieces.append((D + 2 * kv, D + 2 * N_KV * LANES, D))
    pieces.append((D + 2 * kv + D, D + 2 * N_KV * LANES + D, D))
    grp, off, rows = LOC["w_in"]
    out = []
    for g0, d0, n in pieces:
        while n > 0:
            dev, loc = divmod(g0, rows)
            m = min(n, rows - loc)
            out.append((dev * GROUP_ROWS[grp] + off + loc, d0, m))
            g0, d0, n = g0 + m, d0 + m, n - m
    return out


def _start_loads(src_ref, dst_ref, pieces, sems, base):
    cps = []
    for j, (s, d, n) in enumerate(pieces):
        cp = pltpu.make_async_copy(src_ref.at[pl.ds(s, n)], dst_ref.at[pl.ds(d, n)], sems.at[base + j])
        cp.start()
        cps.append(cp)
    return cps


def _load_weights(wrefs, targets, sems):
    cps, base = [], 0
    for name, dst in targets:
        pieces = _win_pieces() if name == "w_in" else _weight_pieces(name)
        cps += _start_loads(wrefs[LOC[name][0]], dst, pieces, sems, base)
        base += len(pieces)
    for cp in cps:
        cp.wait()


def _n_pieces(names):
    return sum(len(_win_pieces()) if n == "w_in" else NDEV for n in names)


def _dw(lhs, rhs, chunk, name, carries=()):
    nq = 4
    if lhs.ndim == 3:
        nch, t_tok, chunk = lhs.shape
        c = nch * chunk
        tq = t_tok // nq
        lhs_specs = [pl.BlockSpec((None, tq, chunk), lambda i, k=k: (i, k, 0)) for k in range(nq)]
    else:
        t_tok, c = lhs.shape
        tq = t_tok // nq
        lhs_specs = [pl.BlockSpec((tq, chunk), lambda i, k=k: (k, i)) for k in range(nq)]

    def body(*refs):
        lhs_refs, (rhs_ref, out_ref, rhs_s, sems) = refs[:nq], refs[nq:]
        cps = [pltpu.make_async_copy(rhs_ref.at[pl.ds(k * tq, tq)], rhs_s.at[pl.ds(k * tq, tq)], sems.at[k])
               for k in range(nq)]
        first = pl.program_id(0) == 0

        @pl.when(first)
        def _():
            for cp in cps:
                cp.start()

        for o in range(0, chunk, D // 2):
            n = min(D // 2, chunk - o)
            acc = None
            for k in range(nq):
                if o == 0:
                    @pl.when(first)
                    def _():
                        cps[k].wait()

                part = _tn(lhs_refs[k][:, pl.ds(o, n)], rhs_s[pl.ds(k * tq, tq), :])
                acc = part if acc is None else acc + part
            out_ref[pl.ds(o, n), :] = acc.astype(BF16)

    (out,), carried = _call(
        body, name=name, grid=(c // chunk,), ins=[lhs] * nq + [rhs],
        in_specs=lhs_specs + [ANY],
        out_specs=[pl.BlockSpec((chunk, D), lambda i: (i, 0))],
        out_shape=[jax.ShapeDtypeStruct((c, D), BF16)],
        scratch=[pltpu.VMEM((t_tok, D), BF16), DMA((nq,))], vmem=VMEM_BIG, carries=carries)
    return out, carried


def _silu_parts(a):
    sig = jax.nn.sigmoid(a)
    return sig, a * sig


FC = 256


def _ffn_fwd(h, gpre, gpost, wg, names, target=None, tm=512, carries=()):
    t_tok = h.shape[0]
    nt = t_tok // tm
    with_loss = target is not None
    groups = _groups_of(names)

    def body(*refs):
        refs = list(refs)
        h_ref, gpre_ref, gpost_ref = refs[:3]
        del refs[:3]
        tgt_ref = refs.pop(0) if with_loss else None
        wrefs = dict(zip(groups, refs[:len(groups)]))
        del refs[:len(groups)]
        hout_ref, a_ref, b_ref, f_ref, nb_ref = refs[:5]
        del refs[:5]
        if with_loss:
            dy_ref, loss_ref = refs[:2]
            del refs[:2]
        w1_s, w3_s, w2_s, sems = refs
        i = pl.program_id(0)

        @pl.when(i == 0)
        def _():
            _load_weights(wrefs, list(zip(names, (w1_s, w3_s, w2_s))), sems)
            if with_loss:
                loss_ref[...] = jnp.zeros_like(loss_ref)

        x = h_ref[...]
        n, _, _ = _rms_fwd(x, gpre_ref[...])
        nb = n.astype(BF16)
        nb_ref[...] = nb
        f = jnp.zeros((tm, D), F32)
        for c0 in range(0, FF, FC):
            a = _nt(nb, w1_s[pl.ds(c0, FC), :])
            b = _nt(nb, w3_s[pl.ds(c0, FC), :])
            _, sl = _silu_parts(a)
            a_ref[:, pl.ds(c0, FC)] = a.astype(BF16)
            b_ref[:, pl.ds(c0, FC)] = b.astype(BF16)
            f = f + _nn((sl * b).astype(BF16), w2_s[pl.ds(c0, FC), :])
        f_ref[...] = f
        fn, _, _ = _rms_fwd(f, gpost_ref[...])
        y = x + 0.5 * fn
        hout_ref[...] = y
        if with_loss:
            err = y - tgt_ref[...]
            dy_ref[...] = err * (1.0 / D)
            loss_ref[...] += jnp.sum(jnp.sum(err * err, axis=-1, keepdims=True), axis=0, keepdims=True) * (0.5 / D)

    ins = [h, gpre, gpost] + ([target] if with_loss else []) + [wg[g] for g in groups]
    in_specs = [_row_spec(tm, D), _const_spec((1, D)), _const_spec((1, D))]
    in_specs += ([_row_spec(tm, D)] if with_loss else []) + [ANY] * len(groups)
    out_shape = [jax.ShapeDtypeStruct((t_tok, D), F32), jax.ShapeDtypeStruct((t_tok, FF), BF16),
                 jax.ShapeDtypeStruct((t_tok, FF), BF16), jax.ShapeDtypeStruct((t_tok, D), F32),
                 jax.ShapeDtypeStruct((t_tok, D), BF16)]
    out_specs = [_row_spec(tm, D), _row_spec(tm, FF), _row_spec(tm, FF), _row_spec(tm, D), _row_spec(tm, D)]
    if with_loss:
        out_shape += [jax.ShapeDtypeStruct((t_tok, D), F32), jax.ShapeDtypeStruct((1, LANES), F32)]
        out_specs += [_row_spec(tm, D), _const_spec((1, LANES))]
    return _call(body, name="ffn_fwd_" + names[0][:4], grid=(nt,), ins=ins, in_specs=in_specs,
                 out_shape=out_shape, out_specs=out_specs,
                 scratch=[pltpu.VMEM((FF, D), BF16)] * 3 + [DMA((3 * NDEV,))], vmem=VMEM_BIG, carries=carries)


FH = FF // 2
HALF_PIECES = ((0, 256), (256, 256), (512, 256), (768, 256), (1024, 256), (1280, 128))


def _ffn_bwd_a(dh, f, a, b, gpost, wg, name_w2, tm=512, carries=()):
    t_tok = dh.shape[0]
    nt = t_tok // tm
    groups = _groups_of([name_w2])

    def body(dh_ref, f_ref, a_ref, b_ref, gpost_ref, wg_ref, dab_ref, s_ref, df_ref, dgp_ref, w2_s, sems):
        i = pl.program_id(0)

        @pl.when(i == 0)
        def _():
            _load_weights({groups[0]: wg_ref}, [(name_w2, w2_s)], sems)
            dgp_ref[...] = jnp.zeros_like(dgp_ref)

        fv = f_ref[...]
        _, fh, r = _rms_fwd(fv, gpost_ref[...])
        df, dg = _rms_bwd(0.5 * dh_ref[...], fh, r, gpost_ref[...])
        dgp_ref[...] += dg
        dfb = df.astype(BF16)
        df_ref[...] = dfb
        for half in range(2):
            for o, n in HALF_PIECES:
                c0 = half * FH + o
                ds = _nt(dfb, w2_s[pl.ds(c0, n), :])
                av = a_ref[:, pl.ds(c0, n)].astype(F32)
                bv = b_ref[:, pl.ds(c0, n)].astype(F32)
                sig, sl = _silu_parts(av)
                dab_ref[half, :, pl.ds(o, n)] = (ds * bv * (sig * (1.0 + av * (1.0 - sig)))).astype(BF16)
                dab_ref[2 + half, :, pl.ds(o, n)] = (ds * sl).astype(BF16)
                s_ref[half, :, pl.ds(o, n)] = (sl * bv).astype(BF16)

    return _call(
        body, name="ffn_bwd_a_" + name_w2[:4], grid=(nt,), ins=[dh, f, a, b, gpost, wg[groups[0]]],
        in_specs=[_row_spec(tm, D), _row_spec(tm, D), _row_spec(tm, FF), _row_spec(tm, FF), _const_spec((1, D)), ANY],
        out_specs=[pl.BlockSpec((4, tm, FH), lambda i: (0, i, 0)), pl.BlockSpec((2, tm, FH), lambda i: (0, i, 0)),
                   _row_spec(tm, D), _const_spec((1, D))],
        out_shape=[jax.ShapeDtypeStruct((4, t_tok, FH), BF16), jax.ShapeDtypeStruct((2, t_tok, FH), BF16),
                   jax.ShapeDtypeStruct((t_tok, D), BF16), jax.ShapeDtypeStruct((1, D), F32)],
        scratch=[pltpu.VMEM((FF, D), BF16), DMA((NDEV,))],
        vmem=VMEM_BIG, carries=carries)


def _ffn_bwd_dx(dab, h, dh, gpre, wg, name_w1, name_w3, tm=512, carries=()):
    t_tok = dh.shape[0]
    nt = t_tok // tm
    groups = _groups_of([name_w1, name_w3])

    def body(*refs):
        dab_refs, (h_ref, dh_ref, gpre_ref) = refs[:4], refs[4:7]
        wrefs = dict(zip(groups, refs[7:7 + len(groups)]))
        dhin_ref, dgp_ref, w13_s, sems = refs[7 + len(groups):]
        i = pl.program_id(0)

        @pl.when(i == 0)
        def _():
            _load_weights(wrefs, [(name_w1, w13_s.at[pl.ds(0, FF)]), (name_w3, w13_s.at[pl.ds(FF, FF)])], sems)
            dgp_ref[...] = jnp.zeros_like(dgp_ref)

        g = gpre_ref[...]
        _, xh, r = _rms_fwd(h_ref[...], g)
        dn = _nn(dab_refs[0][...], w13_s[pl.ds(0, FH), :])
        for k in range(1, 4):
            dn = dn + _nn(dab_refs[k][...], w13_s[pl.ds(k * FH, FH), :])
        dx, dg = _rms_bwd(dn, xh, r, g)
        dgp_ref[...] += dg
        dhin_ref[...] = dh_ref[...] + dx

    return _call(
        body, name="ffn_bwd_dx_" + name_w1[:4], grid=(nt,), ins=[dab] * 4 + [h, dh, gpre] + [wg[g] for g in groups],
        in_specs=[pl.BlockSpec((None, tm, FH), lambda i, k=k: (k, i, 0)) for k in range(4)]
        + [_row_spec(tm, D), _row_spec(tm, D), _const_spec((1, D))] + [ANY] * len(groups),
        out_specs=[_row_spec(tm, D), _const_spec((1, D))],
        out_shape=[jax.ShapeDtypeStruct((t_tok, D), F32), jax.ShapeDtypeStruct((1, D), F32)],
        scratch=[pltpu.VMEM((2 * FF, D), BF16), DMA((2 * NDEV,))],
        vmem=VMEM_BIG, carries=carries)


Q0, K0, V0, XR0, XG0 = 0, D, D + N_KV * LANES, D + 2 * N_KV * LANES, 2 * D + 2 * N_KV * LANES


def _mix_proj_fwd(h, g, bgate, wg, tm=512, carries=()):
    t_tok = h.shape[0]
    nt = t_tok // tm
    names = ("w_in", "w_gate")
    groups = _groups_of(names)

    def body(h_ref, g_ref, bg_ref, wg_ref, q_ref, k_ref, v_ref, xr_ref, xg_ref, gs_ref, ub_ref, win_s, wgt_s, sems):
        i = pl.program_id(0)

        @pl.when(i == 0)
        def _():
            _load_weights({groups[0]: wg_ref}, [("w_in", win_s), ("w_gate", wgt_s)], sems)

        n, _, _ = _rms_fwd(h_ref[...], g_ref[...])
        nb = n.astype(BF16)
        ub_ref[...] = nb
        q_ref[...] = _nt(nb, win_s[pl.ds(Q0, D), :]).astype(BF16)
        k_ref[...] = _nt(nb, win_s[pl.ds(K0, N_KV * LANES), :]).astype(BF16)
        v_ref[...] = _nt(nb, win_s[pl.ds(V0, N_KV * LANES), :]).astype(BF16)
        xr_ref[...] = _nt(nb, win_s[pl.ds(XR0, D), :])
        xg_ref[...] = _nt(nb, win_s[pl.ds(XG0, D), :])
        gs_ref[...] = jax.nn.sigmoid(_nt(nb, wgt_s[...]) + bg_ref[...]).astype(BF16)

    kvw = N_KV * LANES
    return _call(
        body, name="mix_proj_fwd", grid=(nt,), ins=[h, g, bgate, wg[groups[0]]],
        in_specs=[_row_spec(tm, D), _const_spec((1, D)), _const_spec((1, 2 * D)), ANY],
        out_specs=[_row_spec(tm, D), _row_spec(tm, kvw), _row_spec(tm, kvw), _row_spec(tm, D), _row_spec(tm, D),
                   _row_spec(tm, 2 * D), _row_spec(tm, D)],
        out_shape=[jax.ShapeDtypeStruct((t_tok, D), BF16), jax.ShapeDtypeStruct((t_tok, kvw), BF16),
                   jax.ShapeDtypeStruct((t_tok, kvw), BF16), jax.ShapeDtypeStruct((t_tok, D), F32),
                   jax.ShapeDtypeStruct((t_tok, D), F32), jax.ShapeDtypeStruct((t_tok, 2 * D), BF16),
                   jax.ShapeDtypeStruct((t_tok, D), BF16)],
        scratch=[pltpu.VMEM((INP_W, D), BF16), pltpu.VMEM((2 * D, D), BF16), DMA((_n_pieces(names),))],
        vmem=VMEM_BIG, carries=carries)


def _mix_proj_bwd(dqkv, dxr, dxg, dgpre, h, dh, g, wg, tm=512, carries=()):
    t_tok = h.shape[0]
    nt = t_tok // tm
    names = ("w_in", "w_gate")
    groups = _groups_of(names)

    def body(dqkv_ref, dxr_ref, dxg_ref, dgp_ref, h_ref, dh_ref, g_ref, wg_ref, dhin_ref, dg_ref, win_s, wgt_s, sems):
        i = pl.program_id(0)

        @pl.when(i == 0)
        def _():
            _load_weights({groups[0]: wg_ref}, [("w_in", win_s), ("w_gate", wgt_s)], sems)
            dg_ref[...] = jnp.zeros_like(dg_ref)

        gv = g_ref[...]
        _, xh, r = _rms_fwd(h_ref[...], gv)
        du = _nn(dgp_ref[...], wgt_s[...])
        du = du + _nn(dqkv_ref[...], win_s[pl.ds(Q0, XR0), :])
        du = du + _nn(dxr_ref[...], win_s[pl.ds(XR0, D), :])
        du = du + _nn(dxg_ref[...], win_s[pl.ds(XG0, D), :])
        dx, dg = _rms_bwd(du, xh, r, gv)
        dg_ref[...] += dg
        dhin_ref[...] = dh_ref[...] + dx

    return _call(
        body, name="mix_proj_bwd", grid=(nt,), ins=[dqkv, dxr, dxg, dgpre, h, dh, g, wg[groups[0]]],
        in_specs=[_row_spec(tm, XR0), _row_spec(tm, D), _row_spec(tm, D), _row_spec(tm, 2 * D), _row_spec(tm, D),
                  _row_spec(tm, D), _const_spec((1, D)), ANY],
        out_specs=[_row_spec(tm, D), _const_spec((1, D))],
        out_shape=[jax.ShapeDtypeStruct((t_tok, D), F32), jax.ShapeDtypeStruct((1, D), F32)],
        scratch=[pltpu.VMEM((INP_W, D), BF16), pltpu.VMEM((2 * D, D), BF16), DMA((_n_pieces(names),))],
        vmem=VMEM_BIG, carries=carries)


def _shift_down(x, d, fill):
    row = lax.broadcasted_iota(jnp.int32, x.shape, 0)
    return jnp.where(row >= d, pltpu.roll(x, d, 0), fill)


def _shift_up(x, d, fill):
    rows = x.shape[0]
    row = lax.broadcasted_iota(jnp.int32, x.shape, 0)
    return jnp.where(row < rows - d, pltpu.roll(x, rows - d, 0), fill)


def _scan_rows(a, b, reverse):
    rows = a.shape[0]
    d = 1
    while d < rows:
        if d < 8:
            shift = _shift_up if reverse else _shift_down
            b = a * shift(b, d, 0.0) + b
            a = a * shift(a, d, 1.0)
        elif reverse:
            b = jnp.concatenate([a[:rows - d] * b[d:] + b[:rows - d], b[rows - d:]], axis=0)
            a = jnp.concatenate([a[:rows - d] * a[d:], a[rows - d:]], axis=0)
        else:
            b = jnp.concatenate([b[:d], a[d:] * b[:rows - d] + b[d:]], axis=0)
            a = jnp.concatenate([a[:d], a[d:] * a[:rows - d]], axis=0)
        d *= 2
    return a, b


def _softplus(x):
    return jnp.maximum(x, 0.0) + jnp.log(1.0 + jnp.exp(-jnp.abs(x)))


_GELU_C = math.sqrt(2.0 / math.pi)


def _gelu_parts(x):
    th = jnp.tanh(_GELU_C * (x + 0.044715 * x * x * x))
    val = 0.5 * x * (1.0 + th)
    grad = 0.5 * (1.0 + th) + 0.5 * x * (1.0 - th * th) * _GELU_C * (1.0 + 3.0 * 0.044715 * x * x)
    return val, grad


def _lru_pre(x, halo, cw_ref, cb_ref, wa_ref, wx_ref, ba_ref, bx_ref, lam_ref):
    ext = jnp.concatenate([halo, x], axis=0)
    shifted = [x] + [pltpu.roll(ext, k, 0)[8:] for k in (1, 2, 3)]
    xc = cb_ref[...] + cw_ref[pl.ds(CONV_WIDTH - 1, 1), :] * x
    for k in (1, 2, 3):
        xc = xc + cw_ref[pl.ds(CONV_WIDTH - 1 - k, 1), :] * shifted[k]
    xcb = xc.astype(BF16)
    r = jax.nn.sigmoid(_nn(xcb, wa_ref[...]) + ba_ref[...])
    ig = jax.nn.sigmoid(_nn(xcb, wx_ref[...]) + bx_ref[...])
    sp = _softplus(-lam_ref[...])
    log_a = -LRU_C * r * sp
    a = jnp.exp(log_a)
    th = jnp.tanh(log_a)
    mult = jnp.sqrt(-2.0 * th / (1.0 - th))
    return shifted, xc, xcb, r, ig, sp, a, mult


def _lru_specs(nt, rows, reverse):
    def tt(t):
        return nt - 1 - t if reverse else t
    tile = pl.BlockSpec((rows, LANES), lambda cb, t: (tt(t), cb))
    halo = pl.BlockSpec((8, LANES), lambda cb, t: (jnp.maximum(tt(t) * (rows // 8) - 1, 0), cb))
    vec = pl.BlockSpec((1, LANES), lambda cb, t: (0, cb))
    cw = pl.BlockSpec((CONV_WIDTH, LANES), lambda cb, t: (0, cb))
    mat = pl.BlockSpec((None, LANES, LANES), lambda cb, t: (cb, 0, 0))
    return tile, halo, vec, cw, mat


def _lru_fwd(xr, xg, cw, cb, wa, wx, ba, bx, lam, carries=()):
    t_tok = xr.shape[0]
    rows = min(LRU_ROWS, t_tok)
    nt = t_tok // rows

    def body(xr_ref, xg_ref, cw_ref, cb_ref, wa_ref, wx_ref, ba_ref, bx_ref, lam_ref, y_ref, h_ref, tail_s, hc_s):
        t = pl.program_id(1)

        @pl.when(t == 0)
        def _():
            tail_s[...] = jnp.zeros_like(tail_s)
            hc_s[...] = jnp.zeros_like(hc_s)

        x = xr_ref[...]
        _, xc, _, _, ig, _, a, mult = _lru_pre(x, tail_s[...], cw_ref, cb_ref, wa_ref, wx_ref, ba_ref, bx_ref, lam_ref)
        tail_s[...] = xr_ref[pl.ds(rows - 8, 8), :]
        acc_a, acc_b = _scan_rows(a, mult * (ig * xc), False)
        hv = acc_b + acc_a * hc_s[...]
        h_ref[...] = hv
        hc_s[...] = h_ref[pl.ds(rows - 1, 1), :]
        gl, _ = _gelu_parts(xg_ref[...])
        y_ref[...] = (hv * gl).astype(BF16)

    tile, _, vec, cws, mat = _lru_specs(nt, rows, False)
    return _call(
        body, name="lru_fwd", grid=(D // LANES, nt), ins=[xr, xg, cw, cb, wa, wx, ba, bx, lam],
        in_specs=[tile, tile, cws, vec, mat, mat, vec, vec, vec],
        out_specs=[tile, tile],
        out_shape=[jax.ShapeDtypeStruct((t_tok, D), BF16), jax.ShapeDtypeStruct((t_tok, D), F32)],
        scratch=[pltpu.VMEM((8, LANES), F32), pltpu.VMEM((1, LANES), F32)], carries=carries)


def _lru_bwd(dy, xr, xg, hseq, cw, cb, wa, wx, ba, bx, lam, carries=()):
    t_tok = xr.shape[0]
    rows = min(LRU_ROWS // 2, t_tok)
    nt = t_tok // rows

    def body(dy_ref, xr_ref, xrh_ref, xg_ref, h_ref, hh_ref, cw_ref, cb_ref, wa_ref, wx_ref, ba_ref, bx_ref, lam_ref,
             dxr_ref, dxg_ref, dvec_ref, dwa_ref, dwx_ref, gcar_s, acar_s, head_s, tmp_s):
        t = pl.program_id(1)
        first_tile = t == nt - 1

        @pl.when(t == 0)
        def _():
            gcar_s[...] = jnp.zeros_like(gcar_s)
            acar_s[...] = jnp.zeros_like(acar_s)
            head_s[...] = jnp.zeros_like(head_s)
            dvec_ref[...] = jnp.zeros_like(dvec_ref)
            dwa_ref[...] = jnp.zeros_like(dwa_ref)
            dwx_ref[...] = jnp.zeros_like(dwx_ref)

        x = xr_ref[...]
        halo = jnp.where(first_tile, 0.0, xrh_ref[...])
        shifted, xc, xcb, r, ig, sp, a, mult = _lru_pre(x, halo, cw_ref, cb_ref, wa_ref, wx_ref, ba_ref, bx_ref, lam_ref)
        hv = h_ref[...]
        dyv = dy_ref[...]
        gl, glg = _gelu_parts(xg_ref[...])
        dxg_ref[...] = (dyv * hv * glg).astype(BF16)
        acc_a, acc_b = _scan_rows(_shift_up(a, 1, acar_s[...]), dyv * gl, True)
        g = acc_b + acc_a * gcar_s[...]
        hhalo = jnp.where(first_tile, 0.0, hh_ref[...])
        hprev = pltpu.roll(jnp.concatenate([hhalo, hv], axis=0), 1, 0)[8:]
        dmult = g * ig * xc
        dlog_a = a * (g * hprev) - dmult * a * a / mult
        dig = g * mult * xc
        dxc = g * mult * ig
        dzr = (dlog_a * (-LRU_C * sp)) * r * (1.0 - r)
        dzx = dig * ig * (1.0 - ig)
        dzrb, dzxb = dzr.astype(BF16), dzx.astype(BF16)
        dxc = dxc + _nt(dzrb, wa_ref[...]) + _nt(dzxb, wx_ref[...])
        dwa_ref[...] += _tn(xcb, dzrb)
        dwx_ref[...] += _tn(xcb, dzxb)
        dsp = jnp.sum(dlog_a * (-LRU_C * r), axis=0, keepdims=True)
        dlam = dsp * (-jax.nn.sigmoid(-lam_ref[...]))
        vrow = lax.broadcasted_iota(jnp.int32, (8, LANES), 0)
        upd = jnp.where(vrow == 4, jnp.sum(dxc, axis=0, keepdims=True), 0.0)
        upd = jnp.where(vrow == 5, jnp.sum(dzr, axis=0, keepdims=True), upd)
        upd = jnp.where(vrow == 6, jnp.sum(dzx, axis=0, keepdims=True), upd)
        upd = jnp.where(vrow == 7, dlam, upd)
        for k in range(CONV_WIDTH):
            upd = jnp.where(vrow == CONV_WIDTH - 1 - k, jnp.sum(dxc * shifted[k], axis=0, keepdims=True), upd)
        dvec_ref[...] += upd
        ext = jnp.concatenate([dxc, head_s[...]], axis=0)
        dxr = cw_ref[pl.ds(CONV_WIDTH - 1, 1), :] * dxc
        for k in (1, 2, 3):
            dxr = dxr + cw_ref[pl.ds(CONV_WIDTH - 1 - k, 1), :] * pltpu.roll(ext, rows + 8 - k, 0)[:rows]
        dxr_ref[...] = dxr.astype(BF16)
        tmp_s[...] = g
        gcar_s[...] = tmp_s[pl.ds(0, 1), :]
        tmp_s[...] = a
        acar_s[...] = tmp_s[pl.ds(0, 1), :]
        tmp_s[...] = dxc
        head_s[...] = tmp_s[pl.ds(0, 8), :]

    tile, halo, vec, cws, mat = _lru_specs(nt, rows, True)
    return _call(
        body, name="lru_bwd", grid=(D // LANES, nt), ins=[dy, xr, xr, xg, hseq, hseq, cw, cb, wa, wx, ba, bx, lam],
        in_specs=[tile, tile, halo, tile, tile, halo, cws, vec, mat, mat, vec, vec, vec],
        out_specs=[tile, tile, pl.BlockSpec((8, LANES), lambda cb, t: (0, cb)), mat, mat],
        out_shape=[jax.ShapeDtypeStruct((t_tok, D), BF16), jax.ShapeDtypeStruct((t_tok, D), BF16),
                   jax.ShapeDtypeStruct((8, D), F32), jax.ShapeDtypeStruct((D // LANES, LANES, LANES), F32),
                   jax.ShapeDtypeStruct((D // LANES, LANES, LANES), F32)],
        scratch=[pltpu.VMEM((1, LANES), F32), pltpu.VMEM((1, LANES), F32), pltpu.VMEM((8, LANES), F32),
                 pltpu.VMEM((rows, LANES), F32)], carries=carries)


def _t5_bucket_np(rel):
    nb = N_BUCKETS // 2
    max_exact = nb // 2
    ret = np.where(rel > 0, nb, 0)
    n = np.abs(rel)
    nf = np.maximum(n, 1).astype(np.float32)
    large = max_exact + (np.log(nf / np.float32(max_exact)) / np.float32(math.log(MAX_DISTANCE / max_exact))
                         * np.float32(nb - max_exact)).astype(np.int32)
    large = np.minimum(large, nb - 1)
    return ret + np.where(n < max_exact, n, large)


def _bucket_map():
    r = np.arange(QT)[:, None]
    c = np.arange(KW)[None, :]
    j = c - (r // CHUNK) * CHUNK
    band = (j >= 0) & (j < WINDOW + CHUNK)
    return np.where(band, _t5_bucket_np(c - r - WINDOW), -1).astype(np.int32)


def _attn_specs(nt, reverse):
    def tt(i):
        return nt - 1 - i if reverse else i
    kvw = N_KV * LANES
    qs = pl.BlockSpec((QT, D), lambda i: (tt(i), 0))
    cur = pl.BlockSpec((QT, kvw), lambda i: (tt(i), 0))
    prev = pl.BlockSpec((WINDOW, kvw), lambda i: (jnp.maximum(tt(i) * (QT // WINDOW) - 1, 0), 0))
    lse = pl.BlockSpec((QT, LANES), lambda i: (tt(i), 0))
    return qs, cur, prev, lse


REP = N_HEADS // N_KV
SCALE = HEAD_DIM ** -0.5


def _stack_heads(x_ref, g, lo, scale=None):
    parts = []
    for hl in range(REP):
        xs = x_ref[:, pl.ds((2 * g + hl // 2) * LANES, LANES)]
        xs = jnp.where(lo if hl % 2 == 0 else jnp.logical_not(lo), xs, jnp.zeros_like(xs))
        parts.append(xs if scale is None else xs * jnp.asarray(scale, xs.dtype))
    return jnp.concatenate(parts, axis=0)


def _stack_sinks(sink_ref, g, srow):
    sk = jnp.full(srow.shape, sink_ref[REP * g + REP - 1], F32)
    for hl in range(REP - 2, -1, -1):
        sk = jnp.where(srow < (hl + 1) * QT, sink_ref[REP * g + hl], sk)
    return sk


def _attn_fwd(q, kd, vd, bias, sinks, carries=()):
    t_tok = q.shape[0]
    nt = t_tok // QT

    def body(q_ref, kp_ref, kc_ref, vp_ref, vc_ref, bias_ref, sink_ref, o_ref, lse_ref):
        i = pl.program_id(0)
        col = lax.broadcasted_iota(jnp.int32, (1, KW), 1)
        first = jnp.where((i == 0) & (col < WINDOW), NEG_INF, 0.0)
        lane = lax.broadcasted_iota(jnp.int32, (QT, LANES), 1)
        lo = lane < HEAD_DIM
        srow = lax.broadcasted_iota(jnp.int32, (REP * QT, 1), 0)
        lse_t = jnp.zeros((QT, LANES), F32)
        for g in range(N_KV):
            kwin = jnp.concatenate([kp_ref[:, pl.ds(g * LANES, LANES)], kc_ref[:, pl.ds(g * LANES, LANES)]], axis=0)
            vwin = jnp.concatenate([vp_ref[:, pl.ds(g * LANES, LANES)], vc_ref[:, pl.ds(g * LANES, LANES)]], axis=0)
            qst = _stack_heads(q_ref, g, lo, SCALE)
            s = _nt(qst, kwin) + (bias_ref[g] + first)
            sk = _stack_sinks(sink_ref, g, srow)
            m = jnp.maximum(jnp.max(s, axis=-1, keepdims=True), sk)
            e = jnp.exp(s - m)
            l = jnp.sum(e, axis=-1, keepdims=True) + jnp.exp(sk - m)
            p = e / l
            ost = _nn(p.astype(BF16), vwin)
            lse_s = m + jnp.log(l)
            for sl in range(2):
                o_ref[:, pl.ds((2 * g + sl) * LANES, LANES)] = jnp.where(
                    lo, ost[2 * sl * QT:(2 * sl + 1) * QT], ost[(2 * sl + 1) * QT:(2 * sl + 2) * QT]).astype(BF16)
            for hl in range(REP):
                lse_t = jnp.where(lane == REP * g + hl, lse_s[hl * QT:(hl + 1) * QT], lse_t)
        lse_ref[...] = lse_t

    qs, cur, prev, lse = _attn_specs(nt, False)
    return _call(
        body, name="attn_fwd", grid=(nt,), ins=[q, kd, kd, vd, vd, bias, sinks],
        in_specs=[qs, prev, cur, prev, cur, _const_spec((N_KV, REP * QT, KW)), pl.BlockSpec(memory_space=pltpu.SMEM)],
        out_specs=[qs, lse],
        out_shape=[jax.ShapeDtypeStruct((t_tok, D), BF16), jax.ShapeDtypeStruct((t_tok, LANES), F32)],
        vmem=48 * 2 ** 20, carries=carries)


def _attn_bwd(q, kd, vd, o, do, lse, bias, sinks, carries=()):
    t_tok = q.shape[0]
    nt = t_tok // QT
    kvw = N_KV * LANES

    def body(q_ref, kp_ref, kc_ref, vp_ref, vc_ref, o_ref, do_ref, lse_ref, bias_ref, sink_ref,
             dqkv_ref, ds_ref, dsink_ref, kcar_s, vcar_s):
        i = pl.program_id(0)
        tile = nt - 1 - i

        @pl.when(i == 0)
        def _():
            kcar_s[...] = jnp.zeros_like(kcar_s)
            vcar_s[...] = jnp.zeros_like(vcar_s)
            ds_ref[...] = jnp.zeros_like(ds_ref)
            dsink_ref[...] = jnp.zeros_like(dsink_ref)

        col = lax.broadcasted_iota(jnp.int32, (1, KW), 1)
        first = jnp.where((tile == 0) & (col < WINDOW), NEG_INF, 0.0)
        lane = lax.broadcasted_iota(jnp.int32, (QT, LANES), 1)
        lo = lane < HEAD_DIM
        lane_k = lax.broadcasted_iota(jnp.int32, (KW, LANES), 1)
        lane_1 = lax.broadcasted_iota(jnp.int32, (1, LANES), 1)
        srow = lax.broadcasted_iota(jnp.int32, (REP * QT, 1), 0)
        lse_t = lse_ref[...]
        dsink = jnp.zeros((1, LANES), F32)
        for g in range(N_KV):
            kwin = jnp.concatenate([kp_ref[:, pl.ds(g * LANES, LANES)], kc_ref[:, pl.ds(g * LANES, LANES)]], axis=0)
            vwin = jnp.concatenate([vp_ref[:, pl.ds(g * LANES, LANES)], vc_ref[:, pl.ds(g * LANES, LANES)]], axis=0)
            qst = _stack_heads(q_ref, g, lo, SCALE)
            dost = _stack_heads(do_ref, g, lo)
            od = [do_ref[:, pl.ds((2 * g + sl) * LANES, LANES)].astype(F32)
                  * o_ref[:, pl.ds((2 * g + sl) * LANES, LANES)].astype(F32) for sl in range(2)]
            drow = jnp.concatenate([jnp.sum(jnp.where(lo if hl % 2 == 0 else jnp.logical_not(lo), od[hl // 2], 0.0),
                                            axis=-1, keepdims=True) for hl in range(REP)], axis=0)
            lse_s = jnp.concatenate([jnp.sum(jnp.where(lane == REP * g + hl, lse_t, 0.0), axis=-1, keepdims=True)
                                     for hl in range(REP)], axis=0)
            s = _nt(qst, kwin) + (bias_ref[g] + first)
            p = jnp.exp(s - lse_s)
            ds = p * (_nt(dost, vwin) - drow)
            ds_ref[g] += ds
            tsink = -(jnp.exp(_stack_sinks(sink_ref, g, srow) - lse_s) * drow)
            for hl in range(REP):
                dsink = dsink + jnp.where(lane_1 == REP * g + hl,
                                          jnp.sum(tsink[hl * QT:(hl + 1) * QT], axis=0, keepdims=True), 0.0)
            dsb = ds.astype(BF16)
            dqst = _nn(dsb, kwin) * SCALE
            for sl in range(2):
                dqkv_ref[:, pl.ds((2 * g + sl) * LANES, LANES)] = jnp.where(
                    lo, dqst[2 * sl * QT:(2 * sl + 1) * QT], dqst[(2 * sl + 1) * QT:(2 * sl + 2) * QT]).astype(BF16)
            dk_acc = _tn(dsb, qst)
            dv_acc = _tn(p.astype(BF16), dost)
            dk_f = jnp.where(lane_k < HEAD_DIM, dk_acc + pltpu.roll(dk_acc, HEAD_DIM, 1), 0.0)
            dv_f = jnp.where(lane_k < HEAD_DIM, dv_acc + pltpu.roll(dv_acc, HEAD_DIM, 1), 0.0)
            for acc, col0, car in ((dk_f, K0, kcar_s), (dv_f, V0, vcar_s)):
                cs = pl.ds(g * LANES, LANES)
                co = pl.ds(col0 + g * LANES, LANES)
                if QT > WINDOW:
                    dqkv_ref[pl.ds(0, QT - WINDOW), co] = acc[WINDOW:QT].astype(BF16)
                dqkv_ref[pl.ds(QT - WINDOW, WINDOW), co] = (acc[QT:KW] + car[:, cs]).astype(BF16)
                car[:, cs] = acc[0:WINDOW]
        dsink_ref[...] += dsink

    qs, cur, prev, lse_s = _attn_specs(nt, True)
    return _call(
        body, name="attn_bwd", grid=(nt,), ins=[q, kd, kd, vd, vd, o, do, lse, bias, sinks],
        in_specs=[qs, prev, cur, prev, cur, qs, qs, lse_s, _const_spec((N_KV, REP * QT, KW)),
                  pl.BlockSpec(memory_space=pltpu.SMEM)],
        out_specs=[pl.BlockSpec((QT, XR0), lambda i: (nt - 1 - i, 0)), _const_spec((N_KV, REP * QT, KW)),
                   _const_spec((1, LANES))],
        out_shape=[jax.ShapeDtypeStruct((t_tok, XR0), BF16), jax.ShapeDtypeStruct((N_KV, REP * QT, KW), F32),
                   jax.ShapeDtypeStruct((1, LANES), F32)],
        scratch=[pltpu.VMEM((WINDOW, kvw), F32), pltpu.VMEM((WINDOW, kvw), F32)],
        vmem=VMEM_BIG, carries=carries)


def _bias_tile(table, bmap):
    def body(tab_ref, bm_ref, out_ref):
        bm = bm_ref[...]

        def per_head(hd, carry):
            acc = jnp.full((QT, KW), NEG_INF, F32)
            for b in range(N_BUCKETS):
                acc = jnp.where(bm == b, tab_ref[b, hd], acc)
            out_ref[hd] = acc
            return carry

        lax.fori_loop(0, N_HEADS, per_head, 0)

    return pl.pallas_call(
        body, name="bias_tile", out_shape=jax.ShapeDtypeStruct((N_HEADS, QT, KW), F32),
        in_specs=[pl.BlockSpec(memory_space=pltpu.SMEM), pl.BlockSpec(memory_space=pltpu.VMEM)],
        out_specs=pl.BlockSpec(memory_space=pltpu.VMEM))(table, bmap)


def _bias_grad(ds_acc, bmap):
    def body(ds_ref, bm_ref, out_ref):
        row = lax.broadcasted_iota(jnp.int32, (N_BUCKETS, LANES), 0)
        lane = lax.broadcasted_iota(jnp.int32, (N_BUCKETS, LANES), 1)
        bm = bm_ref[...]

        def per_head(hd, res):
            dsv = ds_ref[hd]
            for b in range(N_BUCKETS):
                val = jnp.sum(jnp.sum(jnp.where(bm == b, dsv, 0.0), axis=0, keepdims=True), axis=1, keepdims=True)
                res = jnp.where((row == b) & (lane == hd), val, res)
            return res

        out_ref[...] = lax.fori_loop(0, N_HEADS, per_head, jnp.zeros((N_BUCKETS, LANES), F32))

    return pl.pallas_call(
        body, name="bias_grad", out_shape=jax.ShapeDtypeStruct((N_BUCKETS, LANES), F32),
        in_specs=[pl.BlockSpec(memory_space=pltpu.VMEM), pl.BlockSpec(memory_space=pltpu.VMEM)],
        out_specs=pl.BlockSpec(memory_space=pltpu.VMEM))(ds_acc, bmap)


MIXOUT = ("w_lru_out", "w_attn_out", "w_o")


def _mix_out_fwd(ya_in, o, gs, h, g, wg, tm=512, carries=()):
    t_tok = h.shape[0]
    nt = t_tok // tm
    groups = _groups_of(MIXOUT)

    def body(ya_ref, o_ref, gs_ref, h_ref, g_ref, wg_ref, hout_ref, yao_ref, ybo_ref, z_ref, wa_s, wb_s, wo_s, sems):
        i = pl.program_id(0)

        @pl.when(i == 0)
        def _():
            _load_weights({groups[0]: wg_ref}, list(zip(MIXOUT, (wa_s, wb_s, wo_s))), sems)

        ya = _nn(ya_ref[...], wa_s[...])
        yb = _nn(o_ref[...], wb_s[...])
        yao_ref[...] = ya.astype(BF16)
        ybo_ref[...] = yb.astype(BF16)
        merged = gs_ref[:, pl.ds(0, D)].astype(F32) * ya + gs_ref[:, pl.ds(D, D)].astype(F32) * yb
        z = _nn(merged.astype(BF16), wo_s[...])
        z_ref[...] = z
        zn, _, _ = _rms_fwd(z, g_ref[...])
        hout_ref[...] = h_ref[...] + zn

    return _call(
        body, name="mix_out_fwd", grid=(nt,), ins=[ya_in, o, gs, h, g, wg[groups[0]]],
        in_specs=[_row_spec(tm, D), _row_spec(tm, D), _row_spec(tm, 2 * D), _row_spec(tm, D), _const_spec((1, D)), ANY],
        out_specs=[_row_spec(tm, D)] * 4,
        out_shape=[jax.ShapeDtypeStruct((t_tok, D), F32), jax.ShapeDtypeStruct((t_tok, D), BF16),
                   jax.ShapeDtypeStruct((t_tok, D), BF16), jax.ShapeDtypeStruct((t_tok, D), F32)],
        scratch=[pltpu.VMEM((D, D), BF16)] * 3 + [DMA((3 * NDEV,))],
        vmem=48 * 2 ** 20, carries=carries)


def _mix_out_bwd(dh, z, ya, yb, gs, g, wg, tm=512, carries=()):
    t_tok = dh.shape[0]
    nt = t_tok // tm
    groups = _groups_of(MIXOUT)

    def body(dh_ref, z_ref, ya_ref, yb_ref, gs_ref, g_ref, wg_ref,
             dyain_ref, do_ref, dgpre_ref, dya_ref, dyb_ref, mg_ref, dz_ref, dg_ref, dbg_ref,
             wa_s, wb_s, wo_s, sems):
        i = pl.program_id(0)

        @pl.when(i == 0)
        def _():
            _load_weights({groups[0]: wg_ref}, list(zip(MIXOUT, (wa_s, wb_s, wo_s))), sems)
            dg_ref[...] = jnp.zeros_like(dg_ref)
            dbg_ref[...] = jnp.zeros_like(dbg_ref)

        gv = g_ref[...]
        _, zh, r = _rms_fwd(z_ref[...], gv)
        dz, dg = _rms_bwd(dh_ref[...], zh, r, gv)
        dg_ref[...] += dg
        dzb = dz.astype(BF16)
        dz_ref[...] = dzb
        ga, gb = gs_ref[:, pl.ds(0, D)].astype(F32), gs_ref[:, pl.ds(D, D)].astype(F32)
        ya_v, yb_v = ya_ref[...].astype(F32), yb_ref[...].astype(F32)
        mg_ref[...] = (ga * ya_v + gb * yb_v).astype(BF16)
        dm = _nt(dzb, wo_s[...])
        dga = dm * ya_v * ga * (1.0 - ga)
        dgb = dm * yb_v * gb * (1.0 - gb)
        dgpre_ref[:, pl.ds(0, D)] = dga.astype(BF16)
        dgpre_ref[:, pl.ds(D, D)] = dgb.astype(BF16)
        dbg_ref[:, pl.ds(0, D)] += jnp.sum(dga, axis=0, keepdims=True)
        dbg_ref[:, pl.ds(D, D)] += jnp.sum(dgb, axis=0, keepdims=True)
        dya = (dm * ga).astype(BF16)
        dyb = (dm * gb).astype(BF16)
        dya_ref[...] = dya
        dyb_ref[...] = dyb
        dyain_ref[...] = _nt(dya, wa_s[...])
        do_ref[...] = _nt(dyb, wb_s[...]).astype(BF16)

    bf = jax.ShapeDtypeStruct((t_tok, D), BF16)
    return _call(
        body, name="mix_out_bwd", grid=(nt,), ins=[dh, z, ya, yb, gs, g, wg[groups[0]]],
        in_specs=[_row_spec(tm, D)] * 4 + [_row_spec(tm, 2 * D), _const_spec((1, D)), ANY],
        out_specs=[_row_spec(tm, D), _row_spec(tm, D), _row_spec(tm, 2 * D)] + [_row_spec(tm, D)] * 4
        + [_const_spec((1, D)), _const_spec((1, 2 * D))],
        out_shape=[jax.ShapeDtypeStruct((t_tok, D), F32), bf, jax.ShapeDtypeStruct((t_tok, 2 * D), BF16), bf, bf, bf, bf,
                   jax.ShapeDtypeStruct((1, D), F32), jax.ShapeDtypeStruct((1, 2 * D), F32)],
        scratch=[pltpu.VMEM((D, D), BF16)] * 3 + [DMA((3 * NDEV,))],
        vmem=VMEM_BIG, carries=carries)


def _sum_parts(parts_list):
    n = len(parts_list)
    _, r, c = parts_list[0].shape
    tc = 256

    def body(*refs):
        for p_ref, o_ref in zip(refs[:n], refs[n:]):
            acc = p_ref[0].astype(F32)
            for s in range(1, NDEV):
                acc = acc + p_ref[s].astype(F32)
            o_ref[...] = acc

    return pl.pallas_call(
        body, name=f"sum_parts_{r}", grid=(c // tc,),
        in_specs=[pl.BlockSpec((NDEV, r, tc), lambda i: (0, 0, i))] * n,
        out_specs=[pl.BlockSpec((r, tc), lambda i: (0, i))] * n,
        out_shape=[jax.ShapeDtypeStruct((r, c), F32)] * n,
        compiler_params=pltpu.CompilerParams(dimension_semantics=("arbitrary",), vmem_limit_bytes=48 * 2 ** 20),
    )(*parts_list)


def _adamw_math(w, g, m, v):
    m = ADAM_B1 * m + (1.0 - ADAM_B1) * g
    v = ADAM_B2 * v + (1.0 - ADAM_B2) * (g * g)
    m_hat = m / (1.0 - ADAM_B1 ** ADAM_STEP)
    v_hat = v / (1.0 - ADAM_B2 ** ADAM_STEP)
    delta = -ADAM_LR * (m_hat / (jnp.sqrt(v_hat) + ADAM_EPS) + ADAM_WD * w)
    return delta, m, v


def _sum_ready(parts_list, name, carries=(), chip_sums=()):
    n = len(parts_list)
    c = parts_list[0].shape[2]
    tc = 2 * LANES

    def body(*refs):
        for k, (p_ref, o_ref) in enumerate(zip(refs[:n], refs[n:])):
            if k in chip_sums:
                core = lax.axis_index("c")
                slots = [2 * ch + core for ch in range(NDEV // 2)]
            else:
                slots = list(range(NDEV))
            g = p_ref[slots[0]].astype(F32)
            for s in slots[1:]:
                g = g + p_ref[s].astype(F32)
            o_ref[...] = g

    return _call(
        body, name=name, grid=(c // tc,), ins=list(parts_list),
        in_specs=[pl.BlockSpec((NDEV, p.shape[1], tc), lambda i: (0, 0, i)) for p in parts_list],
        out_shape=[jax.ShapeDtypeStruct((p.shape[1], c), F32) for p in parts_list],
        out_specs=[pl.BlockSpec((p.shape[1], tc), lambda i: (0, i)) for p in parts_list],
        vmem=48 * 2 ** 20, carries=carries)


def _adamw_cols(items, name):
    n = len(items)
    c = items[0][1].shape[1]
    tc = LANES

    def body(*refs):
        for k in range(n):
            g_ref, w_ref, m_ref, v_ref = refs[4 * k:4 * k + 4]
            go_ref, d_ref, nm_ref, nv_ref = refs[4 * n + 4 * k:4 * n + 4 * k + 4]
            g = g_ref[...]
            d, m, v = _adamw_math(w_ref[...], g, m_ref[...], v_ref[...])
            go_ref[...] = g
            d_ref[...] = d
            nm_ref[...] = m
            nv_ref[...] = v

    ins, specs, out_shape = [], [], []
    for g, w, m, v in items:
        r = w.shape[0]
        ins += [g, w, m, v]
        specs += [pl.BlockSpec((r, tc), lambda i: (0, i))] * 4
        out_shape += [jax.ShapeDtypeStruct((r, c), F32)] * 4
    outs, _ = _call(body, name=name, grid=(c // tc,), ins=ins, in_specs=specs, out_shape=out_shape,
                    out_specs=specs, vmem=48 * 2 ** 20)
    return [tuple(outs[4 * k:4 * k + 4]) for k in range(n)]


def _adamw_body(n):
    def body(*refs):
        for k in range(n):
            g_ref, w_ref, m_ref, v_ref = refs[4 * k:4 * k + 4]
            d_ref, nm_ref, nv_ref = refs[4 * n + 3 * k:4 * n + 3 * k + 3]
            d, m, v = _adamw_math(w_ref[...], g_ref[...], m_ref[...], v_ref[...])
            d_ref[...] = d
            nm_ref[...] = m
            nv_ref[...] = v
    return body


def _adamw(items):
    n = len(items)
    r, c = items[0][0].shape
    tr = r if r * c <= 2 ** 18 else max(t for t in range(8, 65, 8) if r % t == 0)
    spec = pl.BlockSpec((tr, c), lambda i: (i, 0))
    outs = pl.pallas_call(
        _adamw_body(n), name=f"adamw_{r}x{c}", grid=(r // tr,),
        in_specs=[spec] * (4 * n), out_specs=[spec] * (3 * n),
        out_shape=[jax.ShapeDtypeStruct((r, c), F32)] * (3 * n),
        compiler_params=pltpu.CompilerParams(dimension_semantics=("arbitrary",), vmem_limit_bytes=40 * 2 ** 20),
    )(*[a for it in items for a in it])
    return [tuple(outs[3 * k:3 * k + 3]) for k in range(n)]


def _adamw_small(items):
    n = len(items)
    vm = pl.BlockSpec(memory_space=pltpu.VMEM)
    outs = pl.pallas_call(
        _adamw_body(n), name="adamw_small", in_specs=[vm] * (4 * n), out_specs=[vm] * (3 * n),
        out_shape=[jax.ShapeDtypeStruct(it[1].shape, F32) for it in items for _ in range(3)],
    )(*[a for it in items for a in it])
    return [tuple(outs[3 * k:3 * k + 3]) for k in range(n)]


def _pack_small(arrs):
    rows, offs = [], []
    total = 0
    for a in arrs:
        flat = a.reshape(-1).astype(F32)
        nr = -(-flat.shape[0] // LANES)
        flat = jnp.pad(flat, (0, nr * LANES - flat.shape[0]))
        rows.append(flat.reshape(nr, LANES))
        offs.append((total, nr))
        total += nr
    pad = -total % 8
    if pad:
        rows.append(jnp.zeros((pad, LANES), F32))
    return jnp.concatenate(rows, axis=0), offs


def _unpack_small(pack, offs, shapes):
    out = []
    for (o, nr), shp in zip(offs, shapes):
        size = int(np.prod(shp))
        out.append(pack[o:o + nr].reshape(-1)[:size].reshape(shp))
    return out


def _sum_small(gathered, rows):
    def body(p_ref, o_ref):
        acc = p_ref[pl.ds(0, rows), :]
        for s in range(1, NDEV):
            acc = acc + p_ref[pl.ds(s * rows, rows), :]
        o_ref[...] = acc

    return pl.pallas_call(
        body, name="sum_small", out_shape=jax.ShapeDtypeStruct((rows, LANES), F32),
        in_specs=[pl.BlockSpec(memory_space=pltpu.VMEM)], out_specs=pl.BlockSpec(memory_space=pltpu.VMEM))(gathered)


def _block_diag(w):
    w = w.reshape(D // LANES, 2, LRU_BLOCK, LRU_BLOCK)
    z = jnp.zeros((D // LANES, LRU_BLOCK, LRU_BLOCK), w.dtype)
    top = jnp.concatenate([w[:, 0], z], axis=2)
    bot = jnp.concatenate([z, w[:, 1]], axis=2)
    return jnp.concatenate([top, bot], axis=1)


def _block_diag_grad(dw):
    a = dw[:, :LRU_BLOCK, :LRU_BLOCK]
    b = dw[:, LRU_BLOCK:, LRU_BLOCK:]
    return jnp.stack([a, b], axis=1).reshape(D // LRU_BLOCK, LRU_BLOCK, LRU_BLOCK)


def kernel(x, ffn1_pre_g, ffn1_w1, ffn1_w3, ffn1_w2, ffn1_post_g, mix_pre_g, w_in, conv_w, conv_b, rg_a_w, rg_a_b, rg_x_w, rg_x_b, lru_lambda, w_lru_out, attn_sinks, rel_bias, w_attn_out, w_gate, b_gate, w_o, mix_post_g, ffn2_pre_g, ffn2_w1, ffn2_w3, ffn2_w2, ffn2_post_g, loss_target, m_ffn1_pre_g, m_ffn1_w1, m_ffn1_w3, m_ffn1_w2, m_ffn1_post_g, m_mix_pre_g, m_w_in, m_conv_w, m_conv_b, m_rg_a_w, m_rg_a_b, m_rg_x_w, m_rg_x_b, m_lru_lambda, m_w_lru_out, m_attn_sinks, m_rel_bias, m_w_attn_out, m_w_gate, m_b_gate, m_w_o, m_mix_post_g, m_ffn2_pre_g, m_ffn2_w1, m_ffn2_w3, m_ffn2_w2, m_ffn2_post_g, v_ffn1_pre_g, v_ffn1_w1, v_ffn1_w3, v_ffn1_w2, v_ffn1_post_g, v_mix_pre_g, v_w_in, v_conv_w, v_conv_b, v_rg_a_w, v_rg_a_b, v_rg_x_w, v_rg_x_b, v_lru_lambda, v_w_lru_out, v_attn_sinks, v_rel_bias, v_w_attn_out, v_w_gate, v_b_gate, v_w_o, v_mix_post_g, v_ffn2_pre_g, v_ffn2_w1, v_ffn2_w3, v_ffn2_w2, v_ffn2_post_g):
    names = ["ffn1_pre_g", "ffn1_w1", "ffn1_w3", "ffn1_w2", "ffn1_post_g", "mix_pre_g", "w_in", "conv_w", "conv_b",
             "rg_a_w", "rg_a_b", "rg_x_w", "rg_x_b", "lru_lambda", "w_lru_out", "attn_sinks", "rel_bias", "w_attn_out",
             "w_gate", "b_gate", "w_o", "mix_post_g", "ffn2_pre_g", "ffn2_w1", "ffn2_w3", "ffn2_w2", "ffn2_post_g"]
    ws = dict(zip(names, (ffn1_pre_g, ffn1_w1, ffn1_w3, ffn1_w2, ffn1_post_g, mix_pre_g, w_in, conv_w, conv_b, rg_a_w,
                          rg_a_b, rg_x_w, rg_x_b, lru_lambda, w_lru_out, attn_sinks, rel_bias, w_attn_out, w_gate,
                          b_gate, w_o, mix_post_g, ffn2_pre_g, ffn2_w1, ffn2_w3, ffn2_w2, ffn2_post_g)))
    ms = dict(zip(names, (m_ffn1_pre_g, m_ffn1_w1, m_ffn1_w3, m_ffn1_w2, m_ffn1_post_g, m_mix_pre_g, m_w_in, m_conv_w,
                          m_conv_b, m_rg_a_w, m_rg_a_b, m_rg_x_w, m_rg_x_b, m_lru_lambda, m_w_lru_out, m_attn_sinks,
                          m_rel_bias, m_w_attn_out, m_w_gate, m_b_gate, m_w_o, m_mix_post_g, m_ffn2_pre_g, m_ffn2_w1,
                          m_ffn2_w3, m_ffn2_w2, m_ffn2_post_g)))
    vs = dict(zip(names, (v_ffn1_pre_g, v_ffn1_w1, v_ffn1_w3, v_ffn1_w2, v_ffn1_post_g, v_mix_pre_g, v_w_in, v_conv_w,
                          v_conv_b, v_rg_a_w, v_rg_a_b, v_rg_x_w, v_rg_x_b, v_lru_lambda, v_w_lru_out, v_attn_sinks,
                          v_rel_bias, v_w_attn_out, v_w_gate, v_b_gate, v_w_o, v_mix_post_g, v_ffn2_pre_g, v_ffn2_w1,
                          v_ffn2_w3, v_ffn2_w2, v_ffn2_post_g)))
    me = 4 * lax.axis_index("x") + 2 * lax.axis_index("y") + lax.axis_index("c")
    vec = lambda n: ws[n].reshape(1, -1)

    def shard2d(name):
        if name == "conv":
            row = lax.bitcast_convert_type(conv_w.reshape(CONV_WIDTH, LANES), BF16).reshape(1, D)
            return jnp.concatenate([row, jnp.zeros((LOC["conv"][2] - 1, D), BF16)], axis=0)
        a = ws[name].reshape(ws[name].shape[-2], ws[name].shape[-1])
        return (a.T if name in COL_SHARDED else a).astype(BF16)

    packs = {g: jnp.concatenate([shard2d(n) for n, _ in members], axis=0) for g, members in GROUPS}

    wg = {}
    _, ((wg["ffn1"],),) = _call(None, name="allgather_ffn1", grid=(), ins=[], in_specs=[], out_shape=[], out_specs=[],
                                carries=[_AllGatherTree(packs["ffn1"])])
    conv_rows = wg["ffn1"].reshape(NDEV, GROUP_ROWS["ffn1"], D)[:, LOC["conv"][1]]
    cw = jnp.transpose(lax.bitcast_convert_type(conv_rows.reshape(NDEV, CONV_WIDTH, LANES, 2), F32),
                       (1, 0, 2)).reshape(CONV_WIDTH, D)
    bmap = jnp.asarray(_bucket_map())
    bias = _bias_tile(rel_bias, bmap).reshape(N_KV, REP * QT, KW)
    sinks = attn_sinks.reshape(N_HEADS)
    wa_bd = _block_diag(rg_a_w.reshape(D // LRU_BLOCK, LRU_BLOCK, LRU_BLOCK)).astype(BF16)
    wx_bd = _block_diag(rg_x_w.reshape(D // LRU_BLOCK, LRU_BLOCK, LRU_BLOCK)).astype(BF16)
    lru_args = (cw, vec("conv_b"), wa_bd, wx_bd, vec("rg_a_b"), vec("rg_x_b"), vec("lru_lambda"))
    x2, tgt = x[0], loss_target[0]

    (h1, a1, b1, f1, nb1), ((wg["mixin"],),) = _ffn_fwd(
        x2, vec("ffn1_pre_g"), vec("ffn1_post_g"), wg, ("ffn1_w1", "ffn1_w3", "ffn1_w2"),
        carries=[_AllGather(packs["mixin"])])
    (q, kd, vd, xr, xg, gs, ub), ((wg["mixout"],),) = _mix_proj_fwd(
        h1, vec("mix_pre_g"), vec("b_gate"), wg, carries=[_AllGather(packs["mixout"])])
    (ya_in, hseq), ((wg["ffn2a"],),) = _lru_fwd(xr, xg, *lru_args, carries=[_AllGather(packs["ffn2a"])])
    (o, lse), ((wg["ffn2b"],),) = _attn_fwd(q, kd, vd, bias, sinks, carries=[_AllGather(packs["ffn2b"])])
    (h2, ya, yb, z), _ = _mix_out_fwd(ya_in, o, gs, h1, vec("mix_post_g"), wg)
    (_, a2, b2, f2, nb2, dy, loss_part), _ = _ffn_fwd(h2, vec("ffn2_pre_g"), vec("ffn2_post_g"), wg,
                                                     ("ffn2_w1", "ffn2_w3", "ffn2_w2"), target=tgt)

    gsm, parts = {}, {}
    rs = lambda *grads_: [_ReduceScatterSend(list(grads_))]
    ffr = FF // NDEV
    (dab, s_act, dfb, gsm["ffn2_post_g"]), _ = _ffn_bwd_a(dy, f2, a2, b2, vec("ffn2_post_g"), wg, "ffn2_w2")
    g_w2, _ = _dw(s_act, dfb, FF // 2, "dw_ffn2_w2")
    g_w13, ((parts["ffn2_w2"],),) = _dw(dab, nb2, D // 2, "dw_ffn2_w13", carries=rs(g_w2))
    (dh2, gsm["ffn2_pre_g"]), ((parts["ffn2_w1"],),) = _ffn_bwd_dx(
        dab, h2, dy, vec("ffn2_pre_g"), wg, "ffn2_w1", "ffn2_w3", carries=rs((g_w13, 0, ffr, ffr)))
    (dya_in, do, dgpre, dya, dyb, mg, dzb, gsm["mix_post_g"], gsm["b_gate"]), ((parts["ffn2_w3"],),) = _mix_out_bwd(
        dh2, z, ya, yb, gs, vec("mix_post_g"), wg, carries=rs((g_w13, FF, ffr, ffr)))
    g_wa, _ = _dw(ya_in, dya, D // 2, "dw_w_lru_out")
    g_wb, _ = _dw(o, dyb, D // 2, "dw_w_attn_out")
    g_wo, _ = _dw(mg, dzb, D // 2, "dw_w_o")
    g_wgate, _ = _dw(dgpre, ub, D // 2, "dw_w_gate")
    (dqkv, ds_acc, dsink), (mixout_parts,) = _attn_bwd(
        q, kd, vd, o, do, lse, bias, sinks, carries=rs(g_wa, g_wb, g_wo, g_wgate))
    parts["w_lru_out"], parts["w_attn_out"], parts["w_o"], parts["w_gate"] = mixout_parts
    (dxr, dxg, dvec, dwa, dwx), _ = _lru_bwd(dya_in, xr, xg, hseq, *lru_args)
    g_qkv, _ = _dw(dqkv, ub, D // 2, "dw_w_in_qkv")
    g_xr, _ = _dw(dxr, ub, D // 2, "dw_w_in_xr")
    g_xg, _ = _dw(dxg, ub, D // 2, "dw_w_in_xg")
    g_win = jnp.concatenate(
        [g_qkv[:D]] + [g_qkv[c0 + gi * LANES:c0 + gi * LANES + HEAD_DIM] for c0 in (K0, V0) for gi in range(N_KV)]
        + [g_xr, g_xg], axis=0)
    wir, wir_a = IN_W // NDEV, 336
    (dh1, gsm["mix_pre_g"]), ((win_a,),) = _mix_proj_bwd(
        dqkv, dxr, dxg, dgpre, h1, dh2, vec("mix_pre_g"), wg, carries=rs((g_win, 0, wir, wir_a)))
    gsm["conv_w"] = dvec[0:CONV_WIDTH]
    gsm["conv_b"], gsm["rg_a_b"], gsm["rg_x_b"], gsm["lru_lambda"] = dvec[4], dvec[5], dvec[6], dvec[7]
    gsm["rg_a_w"] = _block_diag_grad(dwa)
    gsm["rg_x_w"] = _block_diag_grad(dwx)
    gsm["attn_sinks"] = dsink[0, :N_HEADS]
    gsm["rel_bias"] = _bias_grad(ds_acc.reshape(N_HEADS, QT, KW), bmap)[:, :N_HEADS]
    late = ("ffn1_post_g", "ffn1_pre_g")
    early = tuple(n for n in SMALL if n not in late)
    early_pack, early_offs = _pack_small([gsm[n] for n in early])
    (dab, s_act, dfb, gsm["ffn1_post_g"]), ((win_b,), (early_all,)) = _ffn_bwd_a(
        dh1, f1, a1, b1, vec("ffn1_post_g"), wg, "ffn1_w2",
        carries=rs((g_win, wir_a, wir, wir - wir_a)) + [_AllGather(early_pack)])
    parts["w_in"] = jnp.concatenate([win_a, win_b], axis=1)
    g_w2, _ = _dw(s_act, dfb, FF // 2, "dw_ffn1_w2")
    g_w13, ((parts["ffn1_w2"],),) = _dw(dab, nb1, D // 2, "dw_ffn1_w13", carries=rs(g_w2))
    chip_w13 = _pair_sum(g_w13, [(0, ffr), (FF, ffr)])
    (grad_x, gsm["ffn1_pre_g"]), ((parts["ffn1_w1"], parts["ffn1_w3"]),) = _ffn_bwd_dx(
        dab, x2, dh1, vec("ffn1_pre_g"), wg, "ffn1_w1", "ffn1_w3",
        carries=[_ReduceScatterSend([(chip_w13, 0, ffr, ffr), (chip_w13, FF, ffr, ffr)], chip_sums=True)])
    late_pack, late_offs = _pack_small([gsm[n] for n in late] + [loss_part])

    grads, delta, new_m, new_v = {}, {}, {}, {}
    t_form = [n for n in COL_SHARDED if ws[n].shape[-1] % LANES]
    view = {n: (lambda a: a.reshape(a.shape[-2], a.shape[-1]).T) if n in t_form
            else (lambda a: a.reshape(a.shape[-2], a.shape[-1])) for n in BIG}
    unview = {n: (lambda a: a.T) if n in t_form else (lambda a: a) for n in BIG}

    group = [n for n in BIG if n != "w_gate"]
    sums, ((late_all,),) = _sum_ready([parts[n] for n in group], "sum_ready", [_AllGather(late_pack)],
                                      chip_sums=(group.index("ffn1_w1"), group.index("ffn1_w3")))
    res = _adamw_cols([(g, view[n](ws[n]), view[n](ms[n]), view[n](vs[n])) for n, g in zip(group, sums)], "adamw_big")
    for n, quad in zip(group, res):
        grads[n], delta[n], new_m[n], new_v[n] = (unview[n](a) for a in quad)
    (g_gate_t,) = _sum_parts([parts["w_gate"]])
    grads["w_gate"] = g_gate_t.T
    ((delta["w_gate"], new_m["w_gate"], new_v["w_gate"]),) = _adamw(
        [(grads["w_gate"], view["w_gate"](ws["w_gate"]), view["w_gate"](ms["w_gate"]), view["w_gate"](vs["w_gate"]))])
    for group, gathered, offs in ((early, early_all, early_offs), (late, late_all, late_offs)):
        total = _sum_small(gathered, gathered.shape[0] // NDEV)
        shapes = [(CONV_WIDTH, D) if n == "conv_w" else ws[n].shape for n in group]
        if group is late:
            shapes = shapes + [(1, LANES)]
        unpacked = _unpack_small(total, offs, shapes)
        if group is late:
            loss = unpacked.pop()[0, 0]
        for n, g in zip(group, unpacked):
            grads[n] = g
    grads["conv_w"] = lax.dynamic_slice(grads["conv_w"], (0, me * LANES), (CONV_WIDTH, LANES)).reshape(conv_w.shape)

    flat2d = lambda a: a.reshape(-1, a.shape[-1])
    res = _adamw_small([(flat2d(grads[n].reshape(ws[n].shape)), flat2d(ws[n]), flat2d(ms[n]), flat2d(vs[n]))
                        for n in SMALL])
    for n, (d_, m_, v_) in zip(SMALL, res):
        delta[n], new_m[n], new_v[n] = d_, m_, v_

    outs = [loss, grad_x.reshape(x.shape)]
    for src in (grads, delta, new_m, new_v):
        outs += [src[n].reshape(ws[n].shape) for n in names]
    return tuple(outs)
```

```python
import functools
import math
import operator

import numpy as np
import jax
import jax.numpy as jnp
from jax import lax
from jax.experimental import pallas as pl
from jax.experimental.pallas import tpu as pltpu

F32, BF16 = jnp.float32, jnp.bfloat16

NDEV = 8
D = 1024
FF = 2816
N_HEADS, N_KV, HEAD_DIM = 16, 4, 64
CHUNK, WINDOW = 64, 128
N_BUCKETS, MAX_DISTANCE = 32, 128
LRU_BLOCK = 64
CONV_WIDTH = 4
LRU_C = 8.0
RMS_EPS = 1e-6
NEG_INF = -1e30
LANES = 128
QT = 128
KW = QT + WINDOW
LRU_ROWS = 1024
IN_W = D + 2 * N_KV * HEAD_DIM + 2 * D
INP_W = D + 2 * N_KV * LANES + 2 * D
VMEM_BIG = 58 * 2 ** 20

ADAM_LR, ADAM_B1, ADAM_B2, ADAM_EPS, ADAM_WD, ADAM_STEP = 0.001, 0.9, 0.999, 1e-08, 0.01, 10

GROUPS = (("ffn1", (("ffn1_w1", FF // NDEV), ("ffn1_w3", FF // NDEV), ("ffn1_w2", FF // NDEV), ("conv", 16))),
          ("mixin", (("w_in", IN_W // NDEV), ("w_gate", 2 * D // NDEV))),
          ("mixout", (("w_lru_out", D // NDEV), ("w_attn_out", D // NDEV), ("w_o", D // NDEV))),
          ("ffn2a", (("ffn2_w1", FF // NDEV), ("ffn2_w3", FF // NDEV))),
          ("ffn2b", (("ffn2_w2", FF // NDEV),)))
LOC, GROUP_ROWS = {}, {}
for _g, _members in GROUPS:
    _o = 0
    for _n, _r in _members:
        LOC[_n] = (_g, _o, _r)
        _o += _r
    GROUP_ROWS[_g] = _o
BIG = tuple(n for _, members in GROUPS for n, _ in members if n != "conv")
COL_SHARDED = ("ffn1_w1", "ffn1_w3", "w_in", "w_gate", "ffn2_w1", "ffn2_w3")

SMALL = ("ffn1_pre_g", "ffn1_post_g", "mix_pre_g", "conv_w", "conv_b", "rg_a_w", "rg_a_b", "rg_x_w", "rg_x_b",
         "lru_lambda", "attn_sinks", "rel_bias", "b_gate", "mix_post_g", "ffn2_pre_g", "ffn2_post_g")

MESH = pl.DeviceIdType.MESH
ANY = pl.BlockSpec(memory_space=pl.ANY)
DMA = pltpu.SemaphoreType.DMA


def _nn(a, b):
    return lax.dot_general(a, b, (((1,), (0,)), ((), ())), preferred_element_type=F32)


def _nt(a, b):
    return lax.dot_general(a, b, (((1,), (1,)), ((), ())), preferred_element_type=F32)


def _tn(a, b):
    return lax.dot_general(a, b, (((0,), (0,)), ((), ())), preferred_element_type=F32)


def _rms_fwd(x, g):
    r = lax.rsqrt(jnp.mean(x * x, axis=-1, keepdims=True) + RMS_EPS)
    xh = x * r
    return xh * g, xh, r


def _rms_bwd(dn, xh, r, g):
    dxh = dn * g
    dx = r * (dxh - xh * jnp.mean(dxh * xh, axis=-1, keepdims=True))
    return dx, jnp.sum(dn * xh, axis=0, keepdims=True)


def _row_spec(tm, c):
    return pl.BlockSpec((tm, c), lambda i: (i, 0))


def _const_spec(shape):
    nd = len(shape)
    return pl.BlockSpec(shape, lambda i: (0,) * nd)


class _AllGather:
    waits_for_all = True

    def __init__(self, shard):
        self.m, n = shard.shape
        self.ins = [shard]
        self.out_shape = [jax.ShapeDtypeStruct((NDEV * self.m, n), shard.dtype)]
        self.scratch = [DMA((7,)), DMA((7,)), DMA]

    def _copies(self, ins, outs, scr, all_of_them):
        x_ref, out_ref = ins[0], outs[0]
        send_sems, recv_sems, local_sem = scr
        x, y, c = lax.axis_index("x"), lax.axis_index("y"), lax.axis_index("c")
        me, sibling = (x, y, c), (x, y, 1 - c)
        chips = [(1 - x, y), (x, 1 - y), (1 - x, 1 - y)]
        m = self.m

        def rows(px, py, pc):
            return out_ref.at[pl.ds((4 * px + 2 * py + pc) * m, m), :]

        def copy(k, block, to, src=None):
            return pltpu.make_async_remote_copy(
                src_ref=rows(*block) if src is None else src, dst_ref=rows(*block),
                send_sem=send_sems.at[k], recv_sem=recv_sems.at[k], device_id=to, device_id_type=MESH)

        mine = pltpu.make_async_copy(x_ref, rows(*me), local_sem)
        first = [copy(0, me, sibling, src=x_ref)] + [copy(1 + j, me, (*chip, c), src=x_ref)
                                                     for j, chip in enumerate(chips)]
        if not all_of_them:
            return mine, first
        passed = [copy(4 + j, (*chip, c), sibling) for j, chip in enumerate(chips)]
        landed = [copy(1 + j, (*chip, c), me) for j, chip in enumerate(chips)]
        from_sibling = [copy(0, sibling, me)] + [copy(4 + j, (*chip, 1 - c), me) for j, chip in enumerate(chips)]
        return mine, first, passed, landed, from_sibling

    def start(self, ins, outs, scr):
        mine, first = self._copies(ins, outs, scr, False)
        mine.start()
        for cp in first:
            cp.start()

    def finish(self, ins, outs, scr):
        mine, first, passed, landed, from_sibling = self._copies(ins, outs, scr, True)
        for cp_in, cp_on in zip(landed, passed):
            cp_in.wait_recv()
            cp_on.start()
        for cp in from_sibling:
            cp.wait_recv()
        for cp in first + passed:
            cp.wait_send()
        mine.wait()


class _AllGatherTree:
    waits_for_all = True

    def __init__(self, shard):
        self.m, n = shard.shape
        self.half = self.m // 32 * 16
        self.ins = [shard]
        self.out_shape = [jax.ShapeDtypeStruct((NDEV * self.m, n), shard.dtype)]
        self.scratch = [DMA((9,)), DMA((9,)), DMA]

    def start(self, ins, outs, scr):
        x_ref, out_ref = ins[0], outs[0]
        send_sems, recv_sems, local_sem = scr
        x, y, c = lax.axis_index("x"), lax.axis_index("y"), lax.axis_index("c")
        m, half = self.m, self.half
        me, sibling, xn, yn, diag = (x, y, c), (x, y, 1 - c), (1 - x, y, c), (x, 1 - y, c), (1 - x, 1 - y, c)
        other = lambda dev: (dev[0], dev[1], 1 - c)

        def rows(dev, lo=0, n=m):
            return out_ref.at[pl.ds((4 * dev[0] + 2 * dev[1] + dev[2]) * m + lo, n), :]

        def copy(k, dev, to, lo=0, n=m, src=None):
            return pltpu.make_async_remote_copy(
                src_ref=rows(dev, lo, n) if src is None else src, dst_ref=rows(dev, lo, n),
                send_sem=send_sems.at[k], recv_sem=recv_sems.at[k], device_id=to, device_id_type=MESH)

        mine = pltpu.make_async_copy(x_ref, rows(me), local_sem)
        own = [copy(0, me, sibling, src=x_ref), copy(1, me, xn, src=x_ref), copy(2, me, yn, src=x_ref)]
        mine.start()
        for cp in own:
            cp.start()
        steps = [(copy(1, xn, me), [copy(3, xn, yn, 0, half), copy(5, xn, sibling)]),
                 (copy(2, yn, me), [copy(4, yn, xn, half, m - half), copy(6, yn, sibling)]),
                 (copy(3, diag, me, 0, half), [copy(7, diag, sibling, 0, half)]),
                 (copy(4, diag, me, half, m - half), [copy(8, diag, sibling, half, m - half)])]
        from_sibling = [copy(0, sibling, me), copy(5, other(xn), me), copy(6, other(yn), me),
                        copy(7, other(diag), me, 0, half), copy(8, other(diag), me, half, m - half)]
        passed = []
        for landing, onward in steps:
            landing.wait_recv()
            for cp in onward:
                cp.start()
            passed += onward
        for cp in from_sibling:
            cp.wait_recv()
        for cp in own + passed:
            cp.wait_send()
        mine.wait()

    def finish(self, ins, outs, scr):
        pass


class _ReduceScatterSend:
    def __init__(self, grads, chip_sums=False):
        grads = [g if isinstance(g, tuple) else (g, 0, g.shape[0] // NDEV, g.shape[0] // NDEV) for g in grads]
        self.peers = (2, 4, 6) if chip_sums else tuple(range(1, NDEV))
        self.waits_for_all = not chip_sums
        self.nw = len(grads)
        self.base = [b for _, b, _, _ in grads]
        self.stride = [s for _, _, s, _ in grads]
        self.rows = [r for _, _, _, r in grads]
        self.ins = [g for g, _, _, _ in grads]
        self.out_shape = [jax.ShapeDtypeStruct((NDEV, r, g.shape[1]), g.dtype) for g, _, _, r in grads]
        self.scratch = [DMA((self.nw, NDEV - 1)), DMA((self.nw, NDEV - 1)), DMA((self.nw,))]

    def _copies(self, g_refs, r_refs, scr, want):
        send_sems, recv_sems, local_sems = scr
        x, y, c = lax.axis_index("x"), lax.axis_index("y"), lax.axis_index("c")
        me = 4 * x + 2 * y + c
        rows, base, stride = self.rows, self.base, self.stride
        out = []
        if want == "local":
            for w in range(self.nw):
                out.append(pltpu.make_async_copy(g_refs[w].at[pl.ds(base[w] + me * stride[w], rows[w])],
                                                 r_refs[w].at[me], local_sems.at[w]))
            return out
        for k in self.peers:
            px, py, pc = x ^ (k >> 2), y ^ ((k >> 1) & 1), c ^ (k & 1)
            peer = 4 * px + 2 * py + pc
            for w in range(self.nw):
                sems = dict(send_sem=send_sems.at[w, k - 1], recv_sem=recv_sems.at[w, k - 1],
                            device_id=(px, py, pc), device_id_type=MESH)
                if want == "send":
                    out.append(pltpu.make_async_remote_copy(
                        src_ref=g_refs[w].at[pl.ds(base[w] + peer * stride[w], rows[w])], dst_ref=r_refs[w].at[me],
                        **sems))
                else:
                    out.append(pltpu.make_async_remote_copy(
                        src_ref=g_refs[w].at[pl.ds(0, rows[w])], dst_ref=r_refs[w].at[peer], **sems))
        return out

    def start(self, ins, outs, scr):
        for cp in self._copies(ins, outs, scr, "local") + self._copies(ins, outs, scr, "send"):
            cp.start()

    def finish(self, ins, outs, scr):
        for cp in self._copies(ins, outs, scr, "recv"):
            cp.wait_recv()
        for cp in self._copies(ins, outs, scr, "send"):
            cp.wait_send()
        for cp in self._copies(ins, outs, scr, "local"):
            cp.wait()


def _mesh_barrier():
    barrier = pltpu.get_barrier_semaphore()
    x, y, c = lax.axis_index("x"), lax.axis_index("y"), lax.axis_index("c")
    for k in range(1, NDEV):
        pl.semaphore_signal(barrier, inc=1, device_id=(x ^ (k >> 2), y ^ ((k >> 1) & 1), c ^ (k & 1)),
                            device_id_type=MESH)
    pl.semaphore_wait(barrier, NDEV - 1)


def _call(body, *, name, grid, ins, in_specs, out_shape, out_specs, scratch=(), vmem=None, carries=()):
    n_in, n_out, n_scr = len(ins), len(out_shape), len(scratch)
    ng = len(grid)
    own_barrier = any(getattr(c, "waits_for_all", False) for c in carries)

    def split(refs):
        pos = [0]

        def take(k):
            part = refs[pos[0]:pos[0] + k]
            pos[0] += k
            return part

        i_refs = take(n_in)
        c_in = [take(len(c.ins)) for c in carries]
        o_refs = take(n_out)
        c_out = [take(len(c.out_shape)) for c in carries]
        s_refs = take(n_scr)
        c_scr = [take(len(c.scratch)) for c in carries]
        return i_refs, o_refs, s_refs, list(zip(carries, c_in, c_out, c_scr))

    def full(*refs):
        i_refs, o_refs, s_refs, cparts = split(refs)
        if ng == 0:
            if own_barrier:
                _mesh_barrier()
            for c, a, b, s in cparts:
                c.start(a, b, s)
            for c, a, b, s in cparts:
                c.finish(a, b, s)
            return
        ids = [pl.program_id(a) for a in range(ng)]
        if cparts:
            @pl.when(functools.reduce(operator.and_, [i == 0 for i in ids]))
            def _():
                if own_barrier:
                    _mesh_barrier()
                for c, a, b, s in cparts:
                    c.start(a, b, s)

        body(*i_refs, *o_refs, *s_refs)
        if cparts:
            @pl.when(functools.reduce(operator.and_, [i == g - 1 for i, g in zip(ids, grid)]))
            def _():
                for c, a, b, s in cparts:
                    c.finish(a, b, s)

    all_ins = list(ins) + [a for c in carries for a in c.ins]
    all_in_specs = list(in_specs) + [ANY for c in carries for _ in c.ins]
    all_out_shape = list(out_shape) + [s for c in carries for s in c.out_shape]
    all_out_specs = list(out_specs) + [ANY for c in carries for _ in c.out_shape]
    all_scratch = list(scratch) + [s for c in carries for s in c.scratch]
    kwargs = dict(grid=grid) if ng else {}
    outs = pl.pallas_call(
        full, name=name, in_specs=all_in_specs, out_specs=all_out_specs, out_shape=all_out_shape,
        scratch_shapes=all_scratch,
        compiler_params=pltpu.CompilerParams(dimension_semantics=("arbitrary",) * ng if ng else None,
                                             vmem_limit_bytes=vmem, collective_id=0 if own_barrier else None),
        **kwargs)(*all_ins)
    outs = list(outs)
    res, pos = outs[:n_out], n_out
    carried = []
    for c in carries:
        carried.append(outs[pos:pos + len(c.out_shape)])
        pos += len(c.out_shape)
    return res, carried


def _pair_sum(g, items):
    rows = items[0][1]
    n = len(items) * (NDEV // 2)
    width = g.shape[1]

    def body(g_ref, out_ref, got_ref, mine_v, got_v, send_sems, recv_sems, ld_sems, st_sems):
        x, y, c = lax.axis_index("x"), lax.axis_index("y"), lax.axis_index("c")
        keep = [base + (2 * ch + c) * rows for base, _ in items for ch in range(NDEV // 2)]
        give = [base + (2 * ch + 1 - c) * rows for base, _ in items for ch in range(NDEV // 2)]

        def remote(j, r):
            return pltpu.make_async_remote_copy(
                src_ref=g_ref.at[pl.ds(r, rows)], dst_ref=got_ref.at[pl.ds(r, rows)], send_sem=send_sems.at[j],
                recv_sem=recv_sems.at[j], device_id=(x, y, 1 - c), device_id_type=MESH)

        sends = [remote(j, r) for j, r in enumerate(give)]
        for cp in sends:
            cp.start()
        loads = [pltpu.make_async_copy(g_ref.at[pl.ds(r, rows)], mine_v.at[j], ld_sems.at[0, j])
                 for j, r in enumerate(keep)]
        for cp in loads:
            cp.start()
        stores = []
        for j, r in enumerate(keep):
            remote(j, r).wait_recv()
            cp = pltpu.make_async_copy(got_ref.at[pl.ds(r, rows)], got_v.at[j], ld_sems.at[1, j])
            cp.start()
            loads[j].wait()
            cp.wait()
            mine_v[j] = (mine_v[j].astype(F32) + got_v[j].astype(F32)).astype(BF16)
            st = pltpu.make_async_copy(mine_v.at[j], out_ref.at[pl.ds(r, rows)], st_sems.at[j])
            st.start()
            stores.append(st)
        for cp in sends:
            cp.wait_send()
        for cp in stores:
            cp.wait()

    out, _ = pl.pallas_call(
        body, name="pair_sum", in_specs=[ANY], out_specs=[ANY, ANY],
        out_shape=[jax.ShapeDtypeStruct(g.shape, g.dtype), jax.ShapeDtypeStruct(g.shape, g.dtype)],
        scratch_shapes=[pltpu.VMEM((n, rows, width), g.dtype), pltpu.VMEM((n, rows, width), g.dtype),
                        DMA((n,)), DMA((n,)), DMA((2, n)), DMA((n,))],
        compiler_params=pltpu.CompilerParams(vmem_limit_bytes=40 * 2 ** 20))(g)
    return out


def _groups_of(names):
    out = []
    for n in names:
        if LOC[n][0] not in out:
            out.append(LOC[n][0])
    return out


def _weight_pieces(name):
    g, off, rows = LOC[name]
    return [(d * GROUP_ROWS[g] + off, d * rows, rows) for d in range(NDEV)]


def _win_pieces():
    kv = N_KV * HEAD_DIM
    pieces = [(0, 0, D)]
    for g0, d0 in ((D, D), (D + kv, D + N_KV * LANES)):
        for g in range(N_KV):
            for half in range(2):
                pieces.append((g0 + g * HEAD_DIM, d0 + g * LANES + half * HEAD_DIM, HEAD_DIM))
    pieces.append((D + 2 * kv, D + 2 * N_KV * LANES, D))
    pieces.append((D + 2 * kv + D, D + 2 * N_KV * LANES + D, D))
    grp, off, rows = LOC["w_in"]
    out = []
    for g0, d0, n in pieces:
        while n > 0:
            dev, loc = divmod(g0, rows)
            m = min(n, rows - loc)
            out.append((dev * GROUP_ROWS[grp] + off + loc, d0, m))
            g0, d0, n = g0 + m, d0 + m, n - m
    return out


def _start_loads(src_ref, dst_ref, pieces, sems, base):
    cps = []
    for j, (s, d, n) in enumerate(pieces):
        cp = pltpu.make_async_copy(src_ref.at[pl.ds(s, n)], dst_ref.at[pl.ds(d, n)], sems.at[base + j])
        cp.start()
        cps.append(cp)
    return cps


def _load_weights(wrefs, targets, sems):
    cps, base = [], 0
    for name, dst in targets:
        pieces = _win_pieces() if name == "w_in" else _weight_pieces(name)
        cps += _start_loads(wrefs[LOC[name][0]], dst, pieces, sems, base)
        base += len(pieces)
    for cp in cps:
        cp.wait()


def _n_pieces(names):
    return sum(len(_win_pieces()) if n == "w_in" else NDEV for n in names)


def _dw(lhs, rhs, chunk, name, carries=()):
    nq = 4
    if lhs.ndim == 3:
        nch, t_tok, chunk = lhs.shape
        c = nch * chunk
        tq = t_tok // nq
        lhs_specs = [pl.BlockSpec((None, tq, chunk), lambda i, k=k: (i, k, 0)) for k in range(nq)]
    else:
        t_tok, c = lhs.shape
        tq = t_tok // nq
        lhs_specs = [pl.BlockSpec((tq, chunk), lambda i, k=k: (k, i)) for k in range(nq)]

    def body(*refs):
        lhs_refs, (rhs_ref, out_ref, rhs_s, sems) = refs[:nq], refs[nq:]
        cps = [pltpu.make_async_copy(rhs_ref.at[pl.ds(k * tq, tq)], rhs_s.at[pl.ds(k * tq, tq)], sems.at[k])
               for k in range(nq)]
        first = pl.program_id(0) == 0

        @pl.when(first)
        def _():
            for cp in cps:
                cp.start()

        for o in range(0, chunk, D // 2):
            n = min(D // 2, chunk - o)
            acc = None
            for k in range(nq):
                if o == 0:
                    @pl.when(first)
                    def _():
                        cps[k].wait()

                part = _tn(lhs_refs[k][:, pl.ds(o, n)], rhs_s[pl.ds(k * tq, tq), :])
                acc = part if acc is None else acc + part
            out_ref[pl.ds(o, n), :] = acc.astype(BF16)

    (out,), carried = _call(
        body, name=name, grid=(c // chunk,), ins=[lhs] * nq + [rhs],
        in_specs=lhs_specs + [ANY],
        out_specs=[pl.BlockSpec((chunk, D), lambda i: (i, 0))],
        out_shape=[jax.ShapeDtypeStruct((c, D), BF16)],
        scratch=[pltpu.VMEM((t_tok, D), BF16), DMA((nq,))], vmem=VMEM_BIG, carries=carries)
    return out, carried


def _silu_parts(a):
    sig = jax.nn.sigmoid(a)
    return sig, a * sig


FC = 256


def _ffn_fwd(h, gpre, gpost, wg, names, target=None, tm=512, carries=()):
    t_tok = h.shape[0]
    nt = t_tok // tm
    with_loss = target is not None
    groups = _groups_of(names)

    def body(*refs):
        refs = list(refs)
        h_ref, gpre_ref, gpost_ref = refs[:3]
        del refs[:3]
        tgt_ref = refs.pop(0) if with_loss else None
        wrefs = dict(zip(groups, refs[:len(groups)]))
        del refs[:len(groups)]
        hout_ref, a_ref, b_ref, f_ref, nb_ref = refs[:5]
        del refs[:5]
        if with_loss:
            dy_ref, loss_ref = refs[:2]
            del refs[:2]
        w1_s, w3_s, w2_s, sems = refs
        i = pl.program_id(0)

        @pl.when(i == 0)
        def _():
            _load_weights(wrefs, list(zip(names, (w1_s, w3_s, w2_s))), sems)
            if with_loss:
                loss_ref[...] = jnp.zeros_like(loss_ref)

        x = h_ref[...]
        n, _, _ = _rms_fwd(x, gpre_ref[...])
        nb = n.astype(BF16)
        nb_ref[...] = nb
        f = jnp.zeros((tm, D), F32)
        for c0 in range(0, FF, FC):
            a = _nt(nb, w1_s[pl.ds(c0, FC), :])
            b = _nt(nb, w3_s[pl.ds(c0, FC), :])
            _, sl = _silu_parts(a)
            a_ref[:, pl.ds(c0, FC)] = a.astype(BF16)
            b_ref[:, pl.ds(c0, FC)] = b.astype(BF16)
            f = f + _nn((sl * b).astype(BF16), w2_s[pl.ds(c0, FC), :])
        f_ref[...] = f
        fn, _, _ = _rms_fwd(f, gpost_ref[...])
        y = x + 0.5 * fn
        hout_ref[...] = y
        if with_loss:
            err = y - tgt_ref[...]
            dy_ref[...] = err * (1.0 / D)
            loss_ref[...] += jnp.sum(jnp.sum(err * err, axis=-1, keepdims=True), axis=0, keepdims=True) * (0.5 / D)

    ins = [h, gpre, gpost] + ([target] if with_loss else []) + [wg[g] for g in groups]
    in_specs = [_row_spec(tm, D), _const_spec((1, D)), _const_spec((1, D))]
    in_specs += ([_row_spec(tm, D)] if with_loss else []) + [ANY] * len(groups)
    out_shape = [jax.ShapeDtypeStruct((t_tok, D), F32), jax.ShapeDtypeStruct((t_tok, FF), BF16),
                 jax.ShapeDtypeStruct((t_tok, FF), BF16), jax.ShapeDtypeStruct((t_tok, D), F32),
                 jax.ShapeDtypeStruct((t_tok, D), BF16)]
    out_specs = [_row_spec(tm, D), _row_spec(tm, FF), _row_spec(tm, FF), _row_spec(tm, D), _row_spec(tm, D)]
    if with_loss:
        out_shape += [jax.ShapeDtypeStruct((t_tok, D), F32), jax.ShapeDtypeStruct((1, LANES), F32)]
        out_specs += [_row_spec(tm, D), _const_spec((1, LANES))]
    return _call(body, name="ffn_fwd_" + names[0][:4], grid=(nt,), ins=ins, in_specs=in_specs,
                 out_shape=out_shape, out_specs=out_specs,
                 scratch=[pltpu.VMEM((FF, D), BF16)] * 3 + [DMA((3 * NDEV,))], vmem=VMEM_BIG, carries=carries)


FH = FF // 2
HALF_PIECES = ((0, 256), (256, 256), (512, 256), (768, 256), (1024, 256), (1280, 128))


def _ffn_bwd_a(dh, f, a, b, gpost, wg, name_w2, tm=512, carries=()):
    t_tok = dh.shape[0]
    nt = t_tok // tm
    groups = _groups_of([name_w2])

    def body(dh_ref, f_ref, a_ref, b_ref, gpost_ref, wg_ref, dab_ref, s_ref, df_ref, dgp_ref, w2_s, sems):
        i = pl.program_id(0)

        @pl.when(i == 0)
        def _():
            _load_weights({groups[0]: wg_ref}, [(name_w2, w2_s)], sems)
            dgp_ref[...] = jnp.zeros_like(dgp_ref)

        fv = f_ref[...]
        _, fh, r = _rms_fwd(fv, gpost_ref[...])
        df, dg = _rms_bwd(0.5 * dh_ref[...], fh, r, gpost_ref[...])
        dgp_ref[...] += dg
        dfb = df.astype(BF16)
        df_ref[...] = dfb
        for half in range(2):
            for o, n in HALF_PIECES:
                c0 = half * FH + o
                ds = _nt(dfb, w2_s[pl.ds(c0, n), :])
                av = a_ref[:, pl.ds(c0, n)].astype(F32)
                bv = b_ref[:, pl.ds(c0, n)].astype(F32)
                sig, sl = _silu_parts(av)
                dab_ref[half, :, pl.ds(o, n)] = (ds * bv * (sig * (1.0 + av * (1.0 - sig)))).astype(BF16)
                dab_ref[2 + half, :, pl.ds(o, n)] = (ds * sl).astype(BF16)
                s_ref[half, :, pl.ds(o, n)] = (sl * bv).astype(BF16)

    return _call(
        body, name="ffn_bwd_a_" + name_w2[:4], grid=(nt,), ins=[dh, f, a, b, gpost, wg[groups[0]]],
        in_specs=[_row_spec(tm, D), _row_spec(tm, D), _row_spec(tm, FF), _row_spec(tm, FF), _const_spec((1, D)), ANY],
        out_specs=[pl.BlockSpec((4, tm, FH), lambda i: (0, i, 0)), pl.BlockSpec((2, tm, FH), lambda i: (0, i, 0)),
                   _row_spec(tm, D), _const_spec((1, D))],
        out_shape=[jax.ShapeDtypeStruct((4, t_tok, FH), BF16), jax.ShapeDtypeStruct((2, t_tok, FH), BF16),
                   jax.ShapeDtypeStruct((t_tok, D), BF16), jax.ShapeDtypeStruct((1, D), F32)],
        scratch=[pltpu.VMEM((FF, D), BF16), DMA((NDEV,))],
        vmem=VMEM_BIG, carries=carries)


def _ffn_bwd_dx(dab, h, dh, gpre, wg, name_w1, name_w3, tm=512, carries=()):
    t_tok = dh.shape[0]
    nt = t_tok // tm
    groups = _groups_of([name_w1, name_w3])

    def body(*refs):
        dab_refs, (h_ref, dh_ref, gpre_ref) = refs[:4], refs[4:7]
        wrefs = dict(zip(groups, refs[7:7 + len(groups)]))
        dhin_ref, dgp_ref, w13_s, sems = refs[7 + len(groups):]
        i = pl.program_id(0)

        @pl.when(i == 0)
        def _():
            _load_weights(wrefs, [(name_w1, w13_s.at[pl.ds(0, FF)]), (name_w3, w13_s.at[pl.ds(FF, FF)])], sems)
            dgp_ref[...] = jnp.zeros_like(dgp_ref)

        g = gpre_ref[...]
        _, xh, r = _rms_fwd(h_ref[...], g)
        dn = _nn(dab_refs[0][...], w13_s[pl.ds(0, FH), :])
        for k in range(1, 4):
            dn = dn + _nn(dab_refs[k][...], w13_s[pl.ds(k * FH, FH), :])
        dx, dg = _rms_bwd(dn, xh, r, g)
        dgp_ref[...] += dg
        dhin_ref[...] = dh_ref[...] + dx

    return _call(
        body, name="ffn_bwd_dx_" + name_w1[:4], grid=(nt,), ins=[dab] * 4 + [h, dh, gpre] + [wg[g] for g in groups],
        in_specs=[pl.BlockSpec((None, tm, FH), lambda i, k=k: (k, i, 0)) for k in range(4)]
        + [_row_spec(tm, D), _row_spec(tm, D), _const_spec((1, D))] + [ANY] * len(groups),
        out_specs=[_row_spec(tm, D), _const_spec((1, D))],
        out_shape=[jax.ShapeDtypeStruct((t_tok, D), F32), jax.ShapeDtypeStruct((1, D), F32)],
        scratch=[pltpu.VMEM((2 * FF, D), BF16), DMA((2 * NDEV,))],
        vmem=VMEM_BIG, carries=carries)


Q0, K0, V0, XR0, XG0 = 0, D, D + N_KV * LANES, D + 2 * N_KV * LANES, 2 * D + 2 * N_KV * LANES


def _mix_proj_fwd(h, g, bgate, wg, tm=512, carries=()):
    t_tok = h.shape[0]
    nt = t_tok // tm
    names = ("w_in", "w_gate")
    groups = _groups_of(names)

    def body(h_ref, g_ref, bg_ref, wg_ref, q_ref, k_ref, v_ref, xr_ref, xg_ref, gs_ref, ub_ref, win_s, wgt_s, sems):
        i = pl.program_id(0)

        @pl.when(i == 0)
        def _():
            _load_weights({groups[0]: wg_ref}, [("w_in", win_s), ("w_gate", wgt_s)], sems)

        n, _, _ = _rms_fwd(h_ref[...], g_ref[...])
        nb = n.astype(BF16)
        ub_ref[...] = nb
        q_ref[...] = _nt(nb, win_s[pl.ds(Q0, D), :]).astype(BF16)
        k_ref[...] = _nt(nb, win_s[pl.ds(K0, N_KV * LANES), :]).astype(BF16)
        v_ref[...] = _nt(nb, win_s[pl.ds(V0, N_KV * LANES), :]).astype(BF16)
        xr_ref[...] = _nt(nb, win_s[pl.ds(XR0, D), :])
        xg_ref[...] = _nt(nb, win_s[pl.ds(XG0, D), :])
        gs_ref[...] = jax.nn.sigmoid(_nt(nb, wgt_s[...]) + bg_ref[...]).astype(BF16)

    kvw = N_KV * LANES
    return _call(
        body, name="mix_proj_fwd", grid=(nt,), ins=[h, g, bgate, wg[groups[0]]],
        in_specs=[_row_spec(tm, D), _const_spec((1, D)), _const_spec((1, 2 * D)), ANY],
        out_specs=[_row_spec(tm, D), _row_spec(tm, kvw), _row_spec(tm, kvw), _row_spec(tm, D), _row_spec(tm, D),
                   _row_spec(tm, 2 * D), _row_spec(tm, D)],
        out_shape=[jax.ShapeDtypeStruct((t_tok, D), BF16), jax.ShapeDtypeStruct((t_tok, kvw), BF16),
                   jax.ShapeDtypeStruct((t_tok, kvw), BF16), jax.ShapeDtypeStruct((t_tok, D), F32),
                   jax.ShapeDtypeStruct((t_tok, D), F32), jax.ShapeDtypeStruct((t_tok, 2 * D), BF16),
                   jax.ShapeDtypeStruct((t_tok, D), BF16)],
        scratch=[pltpu.VMEM((INP_W, D), BF16), pltpu.VMEM((2 * D, D), BF16), DMA((_n_pieces(names),))],
        vmem=VMEM_BIG, carries=carries)


def _mix_proj_bwd(dqkv, dxr, dxg, dgpre, h, dh, g, wg, tm=512, carries=()):
    t_tok = h.shape[0]
    nt = t_tok // tm
    names = ("w_in", "w_gate")
    groups = _groups_of(names)

    def body(dqkv_ref, dxr_ref, dxg_ref, dgp_ref, h_ref, dh_ref, g_ref, wg_ref, dhin_ref, dg_ref, win_s, wgt_s, sems):
        i = pl.program_id(0)

        @pl.when(i == 0)
        def _():
            _load_weights({groups[0]: wg_ref}, [("w_in", win_s), ("w_gate", wgt_s)], sems)
            dg_ref[...] = jnp.zeros_like(dg_ref)

        gv = g_ref[...]
        _, xh, r = _rms_fwd(h_ref[...], gv)
        du = _nn(dgp_ref[...], wgt_s[...])
        du = du + _nn(dqkv_ref[...], win_s[pl.ds(Q0, XR0), :])
        du = du + _nn(dxr_ref[...], win_s[pl.ds(XR0, D), :])
        du = du + _nn(dxg_ref[...], win_s[pl.ds(XG0, D), :])
        dx, dg = _rms_bwd(du, xh, r, gv)
        dg_ref[...] += dg
        dhin_ref[...] = dh_ref[...] + dx

    return _call(
        body, name="mix_proj_bwd", grid=(nt,), ins=[dqkv, dxr, dxg, dgpre, h, dh, g, wg[groups[0]]],
        in_specs=[_row_spec(tm, XR0), _row_spec(tm, D), _row_spec(tm, D), _row_spec(tm, 2 * D), _row_spec(tm, D),
                  _row_spec(tm, D), _const_spec((1, D)), ANY],
        out_specs=[_row_spec(tm, D), _const_spec((1, D))],
        out_shape=[jax.ShapeDtypeStruct((t_tok, D), F32), jax.ShapeDtypeStruct((1, D), F32)],
        scratch=[pltpu.VMEM((INP_W, D), BF16), pltpu.VMEM((2 * D, D), BF16), DMA((_n_pieces(names),))],
        vmem=VMEM_BIG, carries=carries)


def _shift_down(x, d, fill):
    row = lax.broadcasted_iota(jnp.int32, x.shape, 0)
    return jnp.where(row >= d, pltpu.roll(x, d, 0), fill)


def _shift_up(x, d, fill):
    rows = x.shape[0]
    row = lax.broadcasted_iota(jnp.int32, x.shape, 0)
    return jnp.where(row < rows - d, pltpu.roll(x, rows - d, 0), fill)


def _scan_rows(a, b, reverse):
    rows = a.shape[0]
    d = 1
    while d < rows:
        if d < 8:
            shift = _shift_up if reverse else _shift_down
            b = a * shift(b, d, 0.0) + b
            a = a * shift(a, d, 1.0)
        elif reverse:
            b = jnp.concatenate([a[:rows - d] * b[d:] + b[:rows - d], b[rows - d:]], axis=0)
            a = jnp.concatenate([a[:rows - d] * a[d:], a[rows - d:]], axis=0)
        else:
            b = jnp.concatenate([b[:d], a[d:] * b[:rows - d] + b[d:]], axis=0)
            a = jnp.concatenate([a[:d], a[d:] * a[:rows - d]], axis=0)
        d *= 2
    return a, b


def _softplus(x):
    return jnp.maximum(x, 0.0) + jnp.log(1.0 + jnp.exp(-jnp.abs(x)))


_GELU_C = math.sqrt(2.0 / math.pi)


def _gelu_parts(x):
    th = jnp.tanh(_GELU_C * (x + 0.044715 * x * x * x))
    val = 0.5 * x * (1.0 + th)
    grad = 0.5 * (1.0 + th) + 0.5 * x * (1.0 - th * th) * _GELU_C * (1.0 + 3.0 * 0.044715 * x * x)
    return val, grad


def _lru_pre(x, halo, cw_ref, cb_ref, wa_ref, wx_ref, ba_ref, bx_ref, lam_ref):
    ext = jnp.concatenate([halo, x], axis=0)
    shifted = [x] + [pltpu.roll(ext, k, 0)[8:] for k in (1, 2, 3)]
    xc = cb_ref[...] + cw_ref[pl.ds(CONV_WIDTH - 1, 1), :] * x
    for k in (1, 2, 3):
        xc = xc + cw_ref[pl.ds(CONV_WIDTH - 1 - k, 1), :] * shifted[k]
    xcb = xc.astype(BF16)
    r = jax.nn.sigmoid(_nn(xcb, wa_ref[...]) + ba_ref[...])
    ig = jax.nn.sigmoid(_nn(xcb, wx_ref[...]) + bx_ref[...])
    sp = _softplus(-lam_ref[...])
    log_a = -LRU_C * r * sp
    a = jnp.exp(log_a)
    th = jnp.tanh(log_a)
    mult = jnp.sqrt(-2.0 * th / (1.0 - th))
    return shifted, xc, xcb, r, ig, sp, a, mult


def _lru_specs(nt, rows, reverse):
    def tt(t):
        return nt - 1 - t if reverse else t
    tile = pl.BlockSpec((rows, LANES), lambda cb, t: (tt(t), cb))
    halo = pl.BlockSpec((8, LANES), lambda cb, t: (jnp.maximum(tt(t) * (rows // 8) - 1, 0), cb))
    vec = pl.BlockSpec((1, LANES), lambda cb, t: (0, cb))
    cw = pl.BlockSpec((CONV_WIDTH, LANES), lambda cb, t: (0, cb))
    mat = pl.BlockSpec((None, LANES, LANES), lambda cb, t: (cb, 0, 0))
    return tile, halo, vec, cw, mat


def _lru_fwd(xr, xg, cw, cb, wa, wx, ba, bx, lam, carries=()):
    t_tok = xr.shape[0]
    rows = min(LRU_ROWS, t_tok)
    nt = t_tok // rows

    def body(xr_ref, xg_ref, cw_ref, cb_ref, wa_ref, wx_ref, ba_ref, bx_ref, lam_ref, y_ref, h_ref, tail_s, hc_s):
        t = pl.program_id(1)

        @pl.when(t == 0)
        def _():
            tail_s[...] = jnp.zeros_like(tail_s)
            hc_s[...] = jnp.zeros_like(hc_s)

        x = xr_ref[...]
        _, xc, _, _, ig, _, a, mult = _lru_pre(x, tail_s[...], cw_ref, cb_ref, wa_ref, wx_ref, ba_ref, bx_ref, lam_ref)
        tail_s[...] = xr_ref[pl.ds(rows - 8, 8), :]
        acc_a, acc_b = _scan_rows(a, mult * (ig * xc), False)
        hv = acc_b + acc_a * hc_s[...]
        h_ref[...] = hv
        hc_s[...] = h_ref[pl.ds(rows - 1, 1), :]
        gl, _ = _gelu_parts(xg_ref[...])
        y_ref[...] = (hv * gl).astype(BF16)

    tile, _, vec, cws, mat = _lru_specs(nt, rows, False)
    return _call(
        body, name="lru_fwd", grid=(D // LANES, nt), ins=[xr, xg, cw, cb, wa, wx, ba, bx, lam],
        in_specs=[tile, tile, cws, vec, mat, mat, vec, vec, vec],
        out_specs=[tile, tile],
        out_shape=[jax.ShapeDtypeStruct((t_tok, D), BF16), jax.ShapeDtypeStruct((t_tok, D), F32)],
        scratch=[pltpu.VMEM((8, LANES), F32), pltpu.VMEM((1, LANES), F32)], carries=carries)


def _lru_bwd(dy, xr, xg, hseq, cw, cb, wa, wx, ba, bx, lam, carries=()):
    t_tok = xr.shape[0]
    rows = min(LRU_ROWS // 2, t_tok)
    nt = t_tok // rows

    def body(dy_ref, xr_ref, xrh_ref, xg_ref, h_ref, hh_ref, cw_ref, cb_ref, wa_ref, wx_ref, ba_ref, bx_ref, lam_ref,
             dxr_ref, dxg_ref, dvec_ref, dwa_ref, dwx_ref, gcar_s, acar_s, head_s, tmp_s):
        t = pl.program_id(1)
        first_tile = t == nt - 1

        @pl.when(t == 0)
        def _():
            gcar_s[...] = jnp.zeros_like(gcar_s)
            acar_s[...] = jnp.zeros_like(acar_s)
            head_s[...] = jnp.zeros_like(head_s)
            dvec_ref[...] = jnp.zeros_like(dvec_ref)
            dwa_ref[...] = jnp.zeros_like(dwa_ref)
            dwx_ref[...] = jnp.zeros_like(dwx_ref)

        x = xr_ref[...]
        halo = jnp.where(first_tile, 0.0, xrh_ref[...])
        shifted, xc, xcb, r, ig, sp, a, mult = _lru_pre(x, halo, cw_ref, cb_ref, wa_ref, wx_ref, ba_ref, bx_ref, lam_ref)
        hv = h_ref[...]
        dyv = dy_ref[...]
        gl, glg = _gelu_parts(xg_ref[...])
        dxg_ref[...] = (dyv * hv * glg).astype(BF16)
        acc_a, acc_b = _scan_rows(_shift_up(a, 1, acar_s[...]), dyv * gl, True)
        g = acc_b + acc_a * gcar_s[...]
        hhalo = jnp.where(first_tile, 0.0, hh_ref[...])
        hprev = pltpu.roll(jnp.concatenate([hhalo, hv], axis=0), 1, 0)[8:]
        dmult = g * ig * xc
        dlog_a = a * (g * hprev) - dmult * a * a / mult
        dig = g * mult * xc
        dxc = g * mult * ig
        dzr = (dlog_a * (-LRU_C * sp)) * r * (1.0 - r)
        dzx = dig * ig * (1.0 - ig)
        dzrb, dzxb = dzr.astype(BF16), dzx.astype(BF16)
        dxc = dxc + _nt(dzrb, wa_ref[...]) + _nt(dzxb, wx_ref[...])
        dwa_ref[...] += _tn(xcb, dzrb)
        dwx_ref[...] += _tn(xcb, dzxb)
        dsp = jnp.sum(dlog_a * (-LRU_C * r), axis=0, keepdims=True)
        dlam = dsp * (-jax.nn.sigmoid(-lam_ref[...]))
        vrow = lax.broadcasted_iota(jnp.int32, (8, LANES), 0)
        upd = jnp.where(vrow == 4, jnp.sum(dxc, axis=0, keepdims=True), 0.0)
        upd = jnp.where(vrow == 5, jnp.sum(dzr, axis=0, keepdims=True), upd)
        upd = jnp.where(vrow == 6, jnp.sum(dzx, axis=0, keepdims=True), upd)
        upd = jnp.where(vrow == 7, dlam, upd)
        for k in range(CONV_WIDTH):
            upd = jnp.where(vrow == CONV_WIDTH - 1 - k, jnp.sum(dxc * shifted[k], axis=0, keepdims=True), upd)
        dvec_ref[...] += upd
        ext = jnp.concatenate([dxc, head_s[...]], axis=0)
        dxr = cw_ref[pl.ds(CONV_WIDTH - 1, 1), :] * dxc
        for k in (1, 2, 3):
            dxr = dxr + cw_ref[pl.ds(CONV_WIDTH - 1 - k, 1), :] * pltpu.roll(ext, rows + 8 - k, 0)[:rows]
        dxr_ref[...] = dxr.astype(BF16)
        tmp_s[...] = g
        gcar_s[...] = tmp_s[pl.ds(0, 1), :]
        tmp_s[...] = a
        acar_s[...] = tmp_s[pl.ds(0, 1), :]
        tmp_s[...] = dxc
        head_s[...] = tmp_s[pl.ds(0, 8), :]

    tile, halo, vec, cws, mat = _lru_specs(nt, rows, True)
    return _call(
        body, name="lru_bwd", grid=(D // LANES, nt), ins=[dy, xr, xr, xg, hseq, hseq, cw, cb, wa, wx, ba, bx, lam],
        in_specs=[tile, tile, halo, tile, tile, halo, cws, vec, mat, mat, vec, vec, vec],
        out_specs=[tile, tile, pl.BlockSpec((8, LANES), lambda cb, t: (0, cb)), mat, mat],
        out_shape=[jax.ShapeDtypeStruct((t_tok, D), BF16), jax.ShapeDtypeStruct((t_tok, D), BF16),
                   jax.ShapeDtypeStruct((8, D), F32), jax.ShapeDtypeStruct((D // LANES, LANES, LANES), F32),
                   jax.ShapeDtypeStruct((D // LANES, LANES, LANES), F32)],
        scratch=[pltpu.VMEM((1, LANES), F32), pltpu.VMEM((1, LANES), F32), pltpu.VMEM((8, LANES), F32),
                 pltpu.VMEM((rows, LANES), F32)], carries=carries)


def _t5_bucket_np(rel):
    nb = N_BUCKETS // 2
    max_exact = nb // 2
    ret = np.where(rel > 0, nb, 0)
    n = np.abs(rel)
    nf = np.maximum(n, 1).astype(np.float32)
    large = max_exact + (np.log(nf / np.float32(max_exact)) / np.float32(math.log(MAX_DISTANCE / max_exact))
                         * np.float32(nb - max_exact)).astype(np.int32)
    large = np.minimum(large, nb - 1)
    return ret + np.where(n < max_exact, n, large)


def _bucket_map():
    r = np.arange(QT)[:, None]
    c = np.arange(KW)[None, :]
    j = c - (r // CHUNK) * CHUNK
    band = (j >= 0) & (j < WINDOW + CHUNK)
    return np.where(band, _t5_bucket_np(c - r - WINDOW), -1).astype(np.int32)


def _attn_specs(nt, reverse):
    def tt(i):
        return nt - 1 - i if reverse else i
    kvw = N_KV * LANES
    qs = pl.BlockSpec((QT, D), lambda i: (tt(i), 0))
    cur = pl.BlockSpec((QT, kvw), lambda i: (tt(i), 0))
    prev = pl.BlockSpec((WINDOW, kvw), lambda i: (jnp.maximum(tt(i) * (QT // WINDOW) - 1, 0), 0))
    lse = pl.BlockSpec((QT, LANES), lambda i: (tt(i), 0))
    return qs, cur, prev, lse


REP = N_HEADS // N_KV
SCALE = HEAD_DIM ** -0.5


def _stack_heads(x_ref, g, lo, scale=None):
    parts = []
    for hl in range(REP):
        xs = x_ref[:, pl.ds((2 * g + hl // 2) * LANES, LANES)]
        xs = jnp.where(lo if hl % 2 == 0 else jnp.logical_not(lo), xs, jnp.zeros_like(xs))
        parts.append(xs if scale is None else xs * jnp.asarray(scale, xs.dtype))
    return jnp.concatenate(parts, axis=0)


def _stack_sinks(sink_ref, g, srow):
    sk = jnp.full(srow.shape, sink_ref[REP * g + REP - 1], F32)
    for hl in range(REP - 2, -1, -1):
        sk = jnp.where(srow < (hl + 1) * QT, sink_ref[REP * g + hl], sk)
    return sk


def _attn_fwd(q, kd, vd, bias, sinks, carries=()):
    t_tok = q.shape[0]
    nt = t_tok // QT

    def body(q_ref, kp_ref, kc_ref, vp_ref, vc_ref, bias_ref, sink_ref, o_ref, lse_ref):
        i = pl.program_id(0)
        col = lax.broadcasted_iota(jnp.int32, (1, KW), 1)
        first = jnp.where((i == 0) & (col < WINDOW), NEG_INF, 0.0)
        lane = lax.broadcasted_iota(jnp.int32, (QT, LANES), 1)
        lo = lane < HEAD_DIM
        srow = lax.broadcasted_iota(jnp.int32, (REP * QT, 1), 0)
        lse_t = jnp.zeros((QT, LANES), F32)
        for g in range(N_KV):
            kwin = jnp.concatenate([kp_ref[:, pl.ds(g * LANES, LANES)], kc_ref[:, pl.ds(g * LANES, LANES)]], axis=0)
            vwin = jnp.concatenate([vp_ref[:, pl.ds(g * LANES, LANES)], vc_ref[:, pl.ds(g * LANES, LANES)]], axis=0)
            qst = _stack_heads(q_ref, g, lo, SCALE)
            s = _nt(qst, kwin) + (bias_ref[g] + first)
            sk = _stack_sinks(sink_ref, g, srow)
            m = jnp.maximum(jnp.max(s, axis=-1, keepdims=True), sk)
            e = jnp.exp(s - m)
            l = jnp.sum(e, axis=-1, keepdims=True) + jnp.exp(sk - m)
            p = e / l
            ost = _nn(p.astype(BF16), vwin)
            lse_s = m + jnp.log(l)
            for sl in range(2):
                o_ref[:, pl.ds((2 * g + sl) * LANES, LANES)] = jnp.where(
                    lo, ost[2 * sl * QT:(2 * sl + 1) * QT], ost[(2 * sl + 1) * QT:(2 * sl + 2) * QT]).astype(BF16)
            for hl in range(REP):
                lse_t = jnp.where(lane == REP * g + hl, lse_s[hl * QT:(hl + 1) * QT], lse_t)
        lse_ref[...] = lse_t

    qs, cur, prev, lse = _attn_specs(nt, False)
    return _call(
        body, name="attn_fwd", grid=(nt,), ins=[q, kd, kd, vd, vd, bias, sinks],
        in_specs=[qs, prev, cur, prev, cur, _const_spec((N_KV, REP * QT, KW)), pl.BlockSpec(memory_space=pltpu.SMEM)],
        out_specs=[qs, lse],
        out_shape=[jax.ShapeDtypeStruct((t_tok, D), BF16), jax.ShapeDtypeStruct((t_tok, LANES), F32)],
        vmem=48 * 2 ** 20, carries=carries)


def _attn_bwd(q, kd, vd, o, do, lse, bias, sinks, carries=()):
    t_tok = q.shape[0]
    nt = t_tok // QT
    kvw = N_KV * LANES

    def body(q_ref, kp_ref, kc_ref, vp_ref, vc_ref, o_ref, do_ref, lse_ref, bias_ref, sink_ref,
             dqkv_ref, ds_ref, dsink_ref, kcar_s, vcar_s):
        i = pl.program_id(0)
        tile = nt - 1 - i

        @pl.when(i == 0)
        def _():
            kcar_s[...] = jnp.zeros_like(kcar_s)
            vcar_s[...] = jnp.zeros_like(vcar_s)
            ds_ref[...] = jnp.zeros_like(ds_ref)
            dsink_ref[...] = jnp.zeros_like(dsink_ref)

        col = lax.broadcasted_iota(jnp.int32, (1, KW), 1)
        first = jnp.where((tile == 0) & (col < WINDOW), NEG_INF, 0.0)
        lane = lax.broadcasted_iota(jnp.int32, (QT, LANES), 1)
        lo = lane < HEAD_DIM
        lane_k = lax.broadcasted_iota(jnp.int32, (KW, LANES), 1)
        lane_1 = lax.broadcasted_iota(jnp.int32, (1, LANES), 1)
        srow = lax.broadcasted_iota(jnp.int32, (REP * QT, 1), 0)
        lse_t = lse_ref[...]
        dsink = jnp.zeros((1, LANES), F32)
        for g in range(N_KV):
            kwin = jnp.concatenate([kp_ref[:, pl.ds(g * LANES, LANES)], kc_ref[:, pl.ds(g * LANES, LANES)]], axis=0)
            vwin = jnp.concatenate([vp_ref[:, pl.ds(g * LANES, LANES)], vc_ref[:, pl.ds(g * LANES, LANES)]], axis=0)
            qst = _stack_heads(q_ref, g, lo, SCALE)
            dost = _stack_heads(do_ref, g, lo)
            od = [do_ref[:, pl.ds((2 * g + sl) * LANES, LANES)].astype(F32)
                  * o_ref[:, pl.ds((2 * g + sl) * LANES, LANES)].astype(F32) for sl in range(2)]
            drow = jnp.concatenate([jnp.sum(jnp.where(lo if hl % 2 == 0 else jnp.logical_not(lo), od[hl // 2], 0.0),
                                            axis=-1, keepdims=True) for hl in range(REP)], axis=0)
            lse_s = jnp.concatenate([jnp.sum(jnp.where(lane == REP * g + hl, lse_t, 0.0), axis=-1, keepdims=True)
                                     for hl in range(REP)], axis=0)
            s = _nt(qst, kwin) + (bias_ref[g] + first)
            p = jnp.exp(s - lse_s)
            ds = p * (_nt(dost, vwin) - drow)
            ds_ref[g] += ds
            tsink = -(jnp.exp(_stack_sinks(sink_ref, g, srow) - lse_s) * drow)
            for hl in range(REP):
                dsink = dsink + jnp.where(lane_1 == REP * g + hl,
                                          jnp.sum(tsink[hl * QT:(hl + 1) * QT], axis=0, keepdims=True), 0.0)
            dsb = ds.astype(BF16)
            dqst = _nn(dsb, kwin) * SCALE
            for sl in range(2):
                dqkv_ref[:, pl.ds((2 * g + sl) * LANES, LANES)] = jnp.where(
                    lo, dqst[2 * sl * QT:(2 * sl + 1) * QT], dqst[(2 * sl + 1) * QT:(2 * sl + 2) * QT]).astype(BF16)
            dk_acc = _tn(dsb, qst)
            dv_acc = _tn(p.astype(BF16), dost)
            dk_f = jnp.where(lane_k < HEAD_DIM, dk_acc + pltpu.roll(dk_acc, HEAD_DIM, 1), 0.0)
            dv_f = jnp.where(lane_k < HEAD_DIM, dv_acc + pltpu.roll(dv_acc, HEAD_DIM, 1), 0.0)
            for acc, col0, car in ((dk_f, K0, kcar_s), (dv_f, V0, vcar_s)):
                cs = pl.ds(g * LANES, LANES)
                co = pl.ds(col0 + g * LANES, LANES)
                if QT > WINDOW:
                    dqkv_ref[pl.ds(0, QT - WINDOW), co] = acc[WINDOW:QT].astype(BF16)
                dqkv_ref[pl.ds(QT - WINDOW, WINDOW), co] = (acc[QT:KW] + car[:, cs]).astype(BF16)
                car[:, cs] = acc[0:WINDOW]
        dsink_ref[...] += dsink

    qs, cur, prev, lse_s = _attn_specs(nt, True)
    return _call(
        body, name="attn_bwd", grid=(nt,), ins=[q, kd, kd, vd, vd, o, do, lse, bias, sinks],
        in_specs=[qs, prev, cur, prev, cur, qs, qs, lse_s, _const_spec((N_KV, REP * QT, KW)),
                  pl.BlockSpec(memory_space=pltpu.SMEM)],
        out_specs=[pl.BlockSpec((QT, XR0), lambda i: (nt - 1 - i, 0)), _const_spec((N_KV, REP * QT, KW)),
                   _const_spec((1, LANES))],
        out_shape=[jax.ShapeDtypeStruct((t_tok, XR0), BF16), jax.ShapeDtypeStruct((N_KV, REP * QT, KW), F32),
                   jax.ShapeDtypeStruct((1, LANES), F32)],
        scratch=[pltpu.VMEM((WINDOW, kvw), F32), pltpu.VMEM((WINDOW, kvw), F32)],
        vmem=VMEM_BIG, carries=carries)


def _bias_tile(table, bmap):
    def body(tab_ref, bm_ref, out_ref):
        bm = bm_ref[...]

        def per_head(hd, carry):
            acc = jnp.full((QT, KW), NEG_INF, F32)
            for b in range(N_BUCKETS):
                acc = jnp.where(bm == b, tab_ref[b, hd], acc)
            out_ref[hd] = acc
            return carry

        lax.fori_loop(0, N_HEADS, per_head, 0)

    return pl.pallas_call(
        body, name="bias_tile", out_shape=jax.ShapeDtypeStruct((N_HEADS, QT, KW), F32),
        in_specs=[pl.BlockSpec(memory_space=pltpu.SMEM), pl.BlockSpec(memory_space=pltpu.VMEM)],
        out_specs=pl.BlockSpec(memory_space=pltpu.VMEM))(table, bmap)


def _bias_grad(ds_acc, bmap):
    def body(ds_ref, bm_ref, out_ref):
        row = lax.broadcasted_iota(jnp.int32, (N_BUCKETS, LANES), 0)
        lane = lax.broadcasted_iota(jnp.int32, (N_BUCKETS, LANES), 1)
        bm = bm_ref[...]

        def per_head(hd, res):
            dsv = ds_ref[hd]
            for b in range(N_BUCKETS):
                val = jnp.sum(jnp.sum(jnp.where(bm == b, dsv, 0.0), axis=0, keepdims=True), axis=1, keepdims=True)
                res = jnp.where((row == b) & (lane == hd), val, res)
            return res

        out_ref[...] = lax.fori_loop(0, N_HEADS, per_head, jnp.zeros((N_BUCKETS, LANES), F32))

    return pl.pallas_call(
        body, name="bias_grad", out_shape=jax.ShapeDtypeStruct((N_BUCKETS, LANES), F32),
        in_specs=[pl.BlockSpec(memory_space=pltpu.VMEM), pl.BlockSpec(memory_space=pltpu.VMEM)],
        out_specs=pl.BlockSpec(memory_space=pltpu.VMEM))(ds_acc, bmap)


MIXOUT = ("w_lru_out", "w_attn_out", "w_o")


def _mix_out_fwd(ya_in, o, gs, h, g, wg, tm=512, carries=()):
    t_tok = h.shape[0]
    nt = t_tok // tm
    groups = _groups_of(MIXOUT)

    def body(ya_ref, o_ref, gs_ref, h_ref, g_ref, wg_ref, hout_ref, yao_ref, ybo_ref, z_ref, wa_s, wb_s, wo_s, sems):
        i = pl.program_id(0)

        @pl.when(i == 0)
        def _():
            _load_weights({groups[0]: wg_ref}, list(zip(MIXOUT, (wa_s, wb_s, wo_s))), sems)

        ya = _nn(ya_ref[...], wa_s[...])
        yb = _nn(o_ref[...], wb_s[...])
        yao_ref[...] = ya.astype(BF16)
        ybo_ref[...] = yb.astype(BF16)
        merged = gs_ref[:, pl.ds(0, D)].astype(F32) * ya + gs_ref[:, pl.ds(D, D)].astype(F32) * yb
        z = _nn(merged.astype(BF16), wo_s[...])
        z_ref[...] = z
        zn, _, _ = _rms_fwd(z, g_ref[...])
        hout_ref[...] = h_ref[...] + zn

    return _call(
        body, name="mix_out_fwd", grid=(nt,), ins=[ya_in, o, gs, h, g, wg[groups[0]]],
        in_specs=[_row_spec(tm, D), _row_spec(tm, D), _row_spec(tm, 2 * D), _row_spec(tm, D), _const_spec((1, D)), ANY],
        out_specs=[_row_spec(tm, D)] * 4,
        out_shape=[jax.ShapeDtypeStruct((t_tok, D), F32), jax.ShapeDtypeStruct((t_tok, D), BF16),
                   jax.ShapeDtypeStruct((t_tok, D), BF16), jax.ShapeDtypeStruct((t_tok, D), F32)],
        scratch=[pltpu.VMEM((D, D), BF16)] * 3 + [DMA((3 * NDEV,))],
        vmem=48 * 2 ** 20, carries=carries)


def _mix_out_bwd(dh, z, ya, yb, gs, g, wg, tm=512, carries=()):
    t_tok = dh.shape[0]
    nt = t_tok // tm
    groups = _groups_of(MIXOUT)

    def body(dh_ref, z_ref, ya_ref, yb_ref, gs_ref, g_ref, wg_ref,
             dyain_ref, do_ref, dgpre_ref, dya_ref, dyb_ref, mg_ref, dz_ref, dg_ref, dbg_ref,
             wa_s, wb_s, wo_s, sems):
        i = pl.program_id(0)

        @pl.when(i == 0)
        def _():
            _load_weights({groups[0]: wg_ref}, list(zip(MIXOUT, (wa_s, wb_s, wo_s))), sems)
            dg_ref[...] = jnp.zeros_like(dg_ref)
            dbg_ref[...] = jnp.zeros_like(dbg_ref)

        gv = g_ref[...]
        _, zh, r = _rms_fwd(z_ref[...], gv)
        dz, dg = _rms_bwd(dh_ref[...], zh, r, gv)
        dg_ref[...] += dg
        dzb = dz.astype(BF16)
        dz_ref[...] = dzb
        ga, gb = gs_ref[:, pl.ds(0, D)].astype(F32), gs_ref[:, pl.ds(D, D)].astype(F32)
        ya_v, yb_v = ya_ref[...].astype(F32), yb_ref[...].astype(F32)
        mg_ref[...] = (ga * ya_v + gb * yb_v).astype(BF16)
        dm = _nt(dzb, wo_s[...])
        dga = dm * ya_v * ga * (1.0 - ga)
        dgb = dm * yb_v * gb * (1.0 - gb)
        dgpre_ref[:, pl.ds(0, D)] = dga.astype(BF16)
        dgpre_ref[:, pl.ds(D, D)] = dgb.astype(BF16)
        dbg_ref[:, pl.ds(0, D)] += jnp.sum(dga, axis=0, keepdims=True)
        dbg_ref[:, pl.ds(D, D)] += jnp.sum(dgb, axis=0, keepdims=True)
        dya = (dm * ga).astype(BF16)
        dyb = (dm * gb).astype(BF16)
        dya_ref[...] = dya
        dyb_ref[...] = dyb
        dyain_ref[...] = _nt(dya, wa_s[...])
        do_ref[...] = _nt(dyb, wb_s[...]).astype(BF16)

    bf = jax.ShapeDtypeStruct((t_tok, D), BF16)
    return _call(
        body, name="mix_out_bwd", grid=(nt,), ins=[dh, z, ya, yb, gs, g, wg[groups[0]]],
        in_specs=[_row_spec(tm, D)] * 4 + [_row_spec(tm, 2 * D), _const_spec((1, D)), ANY],
        out_specs=[_row_spec(tm, D), _row_spec(tm, D), _row_spec(tm, 2 * D)] + [_row_spec(tm, D)] * 4
        + [_const_spec((1, D)), _const_spec((1, 2 * D))],
        out_shape=[jax.ShapeDtypeStruct((t_tok, D), F32), bf, jax.ShapeDtypeStruct((t_tok, 2 * D), BF16), bf, bf, bf, bf,
                   jax.ShapeDtypeStruct((1, D), F32), jax.ShapeDtypeStruct((1, 2 * D), F32)],
        scratch=[pltpu.VMEM((D, D), BF16)] * 3 + [DMA((3 * NDEV,))],
        vmem=VMEM_BIG, carries=carries)


def _sum_parts(parts_list):
    n = len(parts_list)
    _, r, c = parts_list[0].shape
    tc = 256

    def body(*refs):
        for p_ref, o_ref in zip(refs[:n], refs[n:]):
            acc = p_ref[0].astype(F32)
            for s in range(1, NDEV):
                acc = acc + p_ref[s].astype(F32)
            o_ref[...] = acc

    return pl.pallas_call(
        body, name=f"sum_parts_{r}", grid=(c // tc,),
        in_specs=[pl.BlockSpec((NDEV, r, tc), lambda i: (0, 0, i))] * n,
        out_specs=[pl.BlockSpec((r, tc), lambda i: (0, i))] * n,
        out_shape=[jax.ShapeDtypeStruct((r, c), F32)] * n,
        compiler_params=pltpu.CompilerParams(dimension_semantics=("arbitrary",), vmem_limit_bytes=48 * 2 ** 20),
    )(*parts_list)


def _adamw_math(w, g, m, v):
    m = ADAM_B1 * m + (1.0 - ADAM_B1) * g
    v = ADAM_B2 * v + (1.0 - ADAM_B2) * (g * g)
    m_hat = m / (1.0 - ADAM_B1 ** ADAM_STEP)
    v_hat = v / (1.0 - ADAM_B2 ** ADAM_STEP)
    delta = -ADAM_LR * (m_hat / (jnp.sqrt(v_hat) + ADAM_EPS) + ADAM_WD * w)
    return delta, m, v


def _sum_ready(parts_list, name, carries=(), chip_sums=()):
    n = len(parts_list)
    c = parts_list[0].shape[2]
    tc = 2 * LANES

    def body(*refs):
        for k, (p_ref, o_ref) in enumerate(zip(refs[:n], refs[n:])):
            if k in chip_sums:
                core = lax.axis_index("c")
                slots = [2 * ch + core for ch in range(NDEV // 2)]
            else:
                slots = list(range(NDEV))
            g = p_ref[slots[0]].astype(F32)
            for s in slots[1:]:
                g = g + p_ref[s].astype(F32)
            o_ref[...] = g

    return _call(
        body, name=name, grid=(c // tc,), ins=list(parts_list),
        in_specs=[pl.BlockSpec((NDEV, p.shape[1], tc), lambda i: (0, 0, i)) for p in parts_list],
        out_shape=[jax.ShapeDtypeStruct((p.shape[1], c), F32) for p in parts_list],
        out_specs=[pl.BlockSpec((p.shape[1], tc), lambda i: (0, i)) for p in parts_list],
        vmem=48 * 2 ** 20, carries=carries)


def _adamw_cols(items, name):
    n = len(items)
    c = items[0][1].shape[1]
    tc = LANES

    def body(*refs):
        for k in range(n):
            g_ref, w_ref, m_ref, v_ref = refs[4 * k:4 * k + 4]
            go_ref, d_ref, nm_ref, nv_ref = refs[4 * n + 4 * k:4 * n + 4 * k + 4]
            g = g_ref[...]
            d, m, v = _adamw_math(w_ref[...], g, m_ref[...], v_ref[...])
            go_ref[...] = g
            d_ref[...] = d
            nm_ref[...] = m
            nv_ref[...] = v

    ins, specs, out_shape = [], [], []
    for g, w, m, v in items:
        r = w.shape[0]
        ins += [g, w, m, v]
        specs += [pl.BlockSpec((r, tc), lambda i: (0, i))] * 4
        out_shape += [jax.ShapeDtypeStruct((r, c), F32)] * 4
    outs, _ = _call(body, name=name, grid=(c // tc,), ins=ins, in_specs=specs, out_shape=out_shape,
                    out_specs=specs, vmem=48 * 2 ** 20)
    return [tuple(outs[4 * k:4 * k + 4]) for k in range(n)]


def _adamw_body(n):
    def body(*refs):
        for k in range(n):
            g_ref, w_ref, m_ref, v_ref = refs[4 * k:4 * k + 4]
            d_ref, nm_ref, nv_ref = refs[4 * n + 3 * k:4 * n + 3 * k + 3]
            d, m, v = _adamw_math(w_ref[...], g_ref[...], m_ref[...], v_ref[...])
            d_ref[...] = d
            nm_ref[...] = m
            nv_ref[...] = v
    return body


def _adamw(items):
    n = len(items)
    r, c = items[0][0].shape
    tr = r if r * c <= 2 ** 18 else max(t for t in range(8, 65, 8) if r % t == 0)
    spec = pl.BlockSpec((tr, c), lambda i: (i, 0))
    outs = pl.pallas_call(
        _adamw_body(n), name=f"adamw_{r}x{c}", grid=(r // tr,),
        in_specs=[spec] * (4 * n), out_specs=[spec] * (3 * n),
        out_shape=[jax.ShapeDtypeStruct((r, c), F32)] * (3 * n),
        compiler_params=pltpu.CompilerParams(dimension_semantics=("arbitrary",), vmem_limit_bytes=40 * 2 ** 20),
    )(*[a for it in items for a in it])
    return [tuple(outs[3 * k:3 * k + 3]) for k in range(n)]


def _adamw_small(items):
    n = len(items)
    vm = pl.BlockSpec(memory_space=pltpu.VMEM)
    outs = pl.pallas_call(
        _adamw_body(n), name="adamw_small", in_specs=[vm] * (4 * n), out_specs=[vm] * (3 * n),
        out_shape=[jax.ShapeDtypeStruct(it[1].shape, F32) for it in items for _ in range(3)],
    )(*[a for it in items for a in it])
    return [tuple(outs[3 * k:3 * k + 3]) for k in range(n)]


def _pack_small(arrs):
    rows, offs = [], []
    total = 0
    for a in arrs:
        flat = a.reshape(-1).astype(F32)
        nr = -(-flat.shape[0] // LANES)
        flat = jnp.pad(flat, (0, nr * LANES - flat.shape[0]))
        rows.append(flat.reshape(nr, LANES))
        offs.append((total, nr))
        total += nr
    pad = -total % 8
    if pad:
        rows.append(jnp.zeros((pad, LANES), F32))
    return jnp.concatenate(rows, axis=0), offs


def _unpack_small(pack, offs, shapes):
    out = []
    for (o, nr), shp in zip(offs, shapes):
        size = int(np.prod(shp))
        out.append(pack[o:o + nr].reshape(-1)[:size].reshape(shp))
    return out


def _sum_small(gathered, rows):
    def body(p_ref, o_ref):
        acc = p_ref[pl.ds(0, rows), :]
        for s in range(1, NDEV):
            acc = acc + p_ref[pl.ds(s * rows, rows), :]
        o_ref[...] = acc

    return pl.pallas_call(
        body, name="sum_small", out_shape=jax.ShapeDtypeStruct((rows, LANES), F32),
        in_specs=[pl.BlockSpec(memory_space=pltpu.VMEM)], out_specs=pl.BlockSpec(memory_space=pltpu.VMEM))(gathered)


def _block_diag(w):
    w = w.reshape(D // LANES, 2, LRU_BLOCK, LRU_BLOCK)
    z = jnp.zeros((D // LANES, LRU_BLOCK, LRU_BLOCK), w.dtype)
    top = jnp.concatenate([w[:, 0], z], axis=2)
    bot = jnp.concatenate([z, w[:, 1]], axis=2)
    return jnp.concatenate([top, bot], axis=1)


def _block_diag_grad(dw):
    a = dw[:, :LRU_BLOCK, :LRU_BLOCK]
    b = dw[:, LRU_BLOCK:, LRU_BLOCK:]
    return jnp.stack([a, b], axis=1).reshape(D // LRU_BLOCK, LRU_BLOCK, LRU_BLOCK)


def kernel(x, ffn1_pre_g, ffn1_w1, ffn1_w3, ffn1_w2, ffn1_post_g, mix_pre_g, w_in, conv_w, conv_b, rg_a_w, rg_a_b, rg_x_w, rg_x_b, lru_lambda, w_lru_out, attn_sinks, rel_bias, w_attn_out, w_gate, b_gate, w_o, mix_post_g, ffn2_pre_g, ffn2_w1, ffn2_w3, ffn2_w2, ffn2_post_g, loss_target, m_ffn1_pre_g, m_ffn1_w1, m_ffn1_w3, m_ffn1_w2, m_ffn1_post_g, m_mix_pre_g, m_w_in, m_conv_w, m_conv_b, m_rg_a_w, m_rg_a_b, m_rg_x_w, m_rg_x_b, m_lru_lambda, m_w_lru_out, m_attn_sinks, m_rel_bias, m_w_attn_out, m_w_gate, m_b_gate, m_w_o, m_mix_post_g, m_ffn2_pre_g, m_ffn2_w1, m_ffn2_w3, m_ffn2_w2, m_ffn2_post_g, v_ffn1_pre_g, v_ffn1_w1, v_ffn1_w3, v_ffn1_w2, v_ffn1_post_g, v_mix_pre_g, v_w_in, v_conv_w, v_conv_b, v_rg_a_w, v_rg_a_b, v_rg_x_w, v_rg_x_b, v_lru_lambda, v_w_lru_out, v_attn_sinks, v_rel_bias, v_w_attn_out, v_w_gate, v_b_gate, v_w_o, v_mix_post_g, v_ffn2_pre_g, v_ffn2_w1, v_ffn2_w3, v_ffn2_w2, v_ffn2_post_g):
    names = ["ffn1_pre_g", "ffn1_w1", "ffn1_w3", "ffn1_w2", "ffn1_post_g", "mix_pre_g", "w_in", "conv_w", "conv_b",
             "rg_a_w", "rg_a_b", "rg_x_w", "rg_x_b", "lru_lambda", "w_lru_out", "attn_sinks", "rel_bias", "w_attn_out",
             "w_gate", "b_gate", "w_o", "mix_post_g", "ffn2_pre_g", "ffn2_w1", "ffn2_w3", "ffn2_w2", "ffn2_post_g"]
    ws = dict(zip(names, (ffn1_pre_g, ffn1_w1, ffn1_w3, ffn1_w2, ffn1_post_g, mix_pre_g, w_in, conv_w, conv_b, rg_a_w,
                          rg_a_b, rg_x_w, rg_x_b, lru_lambda, w_lru_out, attn_sinks, rel_bias, w_attn_out, w_gate,
                          b_gate, w_o, mix_post_g, ffn2_pre_g, ffn2_w1, ffn2_w3, ffn2_w2, ffn2_post_g)))
    ms = dict(zip(names, (m_ffn1_pre_g, m_ffn1_w1, m_ffn1_w3, m_ffn1_w2, m_ffn1_post_g, m_mix_pre_g, m_w_in, m_conv_w,
                          m_conv_b, m_rg_a_w, m_rg_a_b, m_rg_x_w, m_rg_x_b, m_lru_lambda, m_w_lru_out, m_attn_sinks,
                          m_rel_bias, m_w_attn_out, m_w_gate, m_b_gate, m_w_o, m_mix_post_g, m_ffn2_pre_g, m_ffn2_w1,
                          m_ffn2_w3, m_ffn2_w2, m_ffn2_post_g)))
    vs = dict(zip(names, (v_ffn1_pre_g, v_ffn1_w1, v_ffn1_w3, v_ffn1_w2, v_ffn1_post_g, v_mix_pre_g, v_w_in, v_conv_w,
                          v_conv_b, v_rg_a_w, v_rg_a_b, v_rg_x_w, v_rg_x_b, v_lru_lambda, v_w_lru_out, v_attn_sinks,
                          v_rel_bias, v_w_attn_out, v_w_gate, v_b_gate, v_w_o, v_mix_post_g, v_ffn2_pre_g, v_ffn2_w1,
                          v_ffn2_w3, v_ffn2_w2, v_ffn2_post_g)))
    me = 4 * lax.axis_index("x") + 2 * lax.axis_index("y") + lax.axis_index("c")
    vec = lambda n: ws[n].reshape(1, -1)

    def shard2d(name):
        if name == "conv":
            row = lax.bitcast_convert_type(conv_w.reshape(CONV_WIDTH, LANES), BF16).reshape(1, D)
            return jnp.concatenate([row, jnp.zeros((LOC["conv"][2] - 1, D), BF16)], axis=0)
        a = ws[name].reshape(ws[name].shape[-2], ws[name].shape[-1])
        return (a.T if name in COL_SHARDED else a).astype(BF16)

    packs = {g: jnp.concatenate([shard2d(n) for n, _ in members], axis=0) for g, members in GROUPS}

    wg = {}
    _, ((wg["ffn1"],),) = _call(None, name="allgather_ffn1", grid=(), ins=[], in_specs=[], out_shape=[], out_specs=[],
                                carries=[_AllGatherTree(packs["ffn1"])])
    conv_rows = wg["ffn1"].reshape(NDEV, GROUP_ROWS["ffn1"], D)[:, LOC["conv"][1]]
    cw = jnp.transpose(lax.bitcast_convert_type(conv_rows.reshape(NDEV, CONV_WIDTH, LANES, 2), F32),
                       (1, 0, 2)).reshape(CONV_WIDTH, D)
    bmap = jnp.asarray(_bucket_map())
    bias = _bias_tile(rel_bias, bmap).reshape(N_KV, REP * QT, KW)
    sinks = attn_sinks.reshape(N_HEADS)
    wa_bd = _block_diag(rg_a_w.reshape(D // LRU_BLOCK, LRU_BLOCK, LRU_BLOCK)).astype(BF16)
    wx_bd = _block_diag(rg_x_w.reshape(D // LRU_BLOCK, LRU_BLOCK, LRU_BLOCK)).astype(BF16)
    lru_args = (cw, vec("conv_b"), wa_bd, wx_bd, vec("rg_a_b"), vec("rg_x_b"), vec("lru_lambda"))
    x2, tgt = x[0], loss_target[0]

    (h1, a1, b1, f1, nb1), ((wg["mixin"],),) = _ffn_fwd(
        x2, vec("ffn1_pre_g"), vec("ffn1_post_g"), wg, ("ffn1_w1", "ffn1_w3", "ffn1_w2"),
        carries=[_AllGather(packs["mixin"])])
    (q, kd, vd, xr, xg, gs, ub), ((wg["mixout"],),) = _mix_proj_fwd(
        h1, vec("mix_pre_g"), vec("b_gate"), wg, carries=[_AllGather(packs["mixout"])])
    (ya_in, hseq), ((wg["ffn2a"],),) = _lru_fwd(xr, xg, *lru_args, carries=[_AllGather(packs["ffn2a"])])
    (o, lse), ((wg["ffn2b"],),) = _attn_fwd(q, kd, vd, bias, sinks, carries=[_AllGather(packs["ffn2b"])])
    (h2, ya, yb, z), _ = _mix_out_fwd(ya_in, o, gs, h1, vec("mix_post_g"), wg)
    (_, a2, b2, f2, nb2, dy, loss_part), _ = _ffn_fwd(h2, vec("ffn2_pre_g"), vec("ffn2_post_g"), wg,
                                                     ("ffn2_w1", "ffn2_w3", "ffn2_w2"), target=tgt)

    gsm, parts = {}, {}
    rs = lambda *grads_: [_ReduceScatterSend(list(grads_))]
    ffr = FF // NDEV
    (dab, s_act, dfb, gsm["ffn2_post_g"]), _ = _ffn_bwd_a(dy, f2, a2, b2, vec("ffn2_post_g"), wg, "ffn2_w2")
    g_w2, _ = _dw(s_act, dfb, FF // 2, "dw_ffn2_w2")
    g_w13, ((parts["ffn2_w2"],),) = _dw(dab, nb2, D // 2, "dw_ffn2_w13", carries=rs(g_w2))
    (dh2, gsm["ffn2_pre_g"]), ((parts["ffn2_w1"],),) = _ffn_bwd_dx(
        dab, h2, dy, vec("ffn2_pre_g"), wg, "ffn2_w1", "ffn2_w3", carries=rs((g_w13, 0, ffr, ffr)))
    (dya_in, do, dgpre, dya, dyb, mg, dzb, gsm["mix_post_g"], gsm["b_gate"]), ((parts["ffn2_w3"],),) = _mix_out_bwd(
        dh2, z, ya, yb, gs, vec("mix_post_g"), wg, carries=rs((g_w13, FF, ffr, ffr)))
    g_wa, _ = _dw(ya_in, dya, D // 2, "dw_w_lru_out")
    g_wb, _ = _dw(o, dyb, D // 2, "dw_w_attn_out")
    g_wo, _ = _dw(mg, dzb, D // 2, "dw_w_o")
    g_wgate, _ = _dw(dgpre, ub, D // 2, "dw_w_gate")
    (dqkv, ds_acc, dsink), (mixout_parts,) = _attn_bwd(
        q, kd, vd, o, do, lse, bias, sinks, carries=rs(g_wa, g_wb, g_wo, g_wgate))
    parts["w_lru_out"], parts["w_attn_out"], parts["w_o"], parts["w_gate"] = mixout_parts
    (dxr, dxg, dvec, dwa, dwx), _ = _lru_bwd(dya_in, xr, xg, hseq, *lru_args)
    g_qkv, _ = _dw(dqkv, ub, D // 2, "dw_w_in_qkv")
    g_xr, _ = _dw(dxr, ub, D // 2, "dw_w_in_xr")
    g_xg, _ = _dw(dxg, ub, D // 2, "dw_w_in_xg")
    g_win = jnp.concatenate(
        [g_qkv[:D]] + [g_qkv[c0 + gi * LANES:c0 + gi * LANES + HEAD_DIM] for c0 in (K0, V0) for gi in range(N_KV)]
        + [g_xr, g_xg], axis=0)
    wir, wir_a = IN_W // NDEV, 336
    (dh1, gsm["mix_pre_g"]), ((win_a,),) = _mix_proj_bwd(
        dqkv, dxr, dxg, dgpre, h1, dh2, vec("mix_pre_g"), wg, carries=rs((g_win, 0, wir, wir_a)))
    gsm["conv_w"] = dvec[0:CONV_WIDTH]
    gsm["conv_b"], gsm["rg_a_b"], gsm["rg_x_b"], gsm["lru_lambda"] = dvec[4], dvec[5], dvec[6], dvec[7]
    gsm["rg_a_w"] = _block_diag_grad(dwa)
    gsm["rg_x_w"] = _block_diag_grad(dwx)
    gsm["attn_sinks"] = dsink[0, :N_HEADS]
    gsm["rel_bias"] = _bias_grad(ds_acc.reshape(N_HEADS, QT, KW), bmap)[:, :N_HEADS]
    late = ("ffn1_post_g", "ffn1_pre_g")
    early = tuple(n for n in SMALL if n not in late)
    early_pack, early_offs = _pack_small([gsm[n] for n in early])
    (dab, s_act, dfb, gsm["ffn1_post_g"]), ((win_b,), (early_all,)) = _ffn_bwd_a(
        dh1, f1, a1, b1, vec("ffn1_post_g"), wg, "ffn1_w2",
        carries=rs((g_win, wir_a, wir, wir - wir_a)) + [_AllGather(early_pack)])
    parts["w_in"] = jnp.concatenate([win_a, win_b], axis=1)
    g_w2, _ = _dw(s_act, dfb, FF // 2, "dw_ffn1_w2")
    g_w13, ((parts["ffn1_w2"],),) = _dw(dab, nb1, D // 2, "dw_ffn1_w13", carries=rs(g_w2))
    chip_w13 = _pair_sum(g_w13, [(0, ffr), (FF, ffr)])
    (grad_x, gsm["ffn1_pre_g"]), ((parts["ffn1_w1"], parts["ffn1_w3"]),) = _ffn_bwd_dx(
        dab, x2, dh1, vec("ffn1_pre_g"), wg, "ffn1_w1", "ffn1_w3",
        carries=[_ReduceScatterSend([(chip_w13, 0, ffr, ffr), (chip_w13, FF, ffr, ffr)], chip_sums=True)])
    late_pack, late_offs = _pack_small([gsm[n] for n in late] + [loss_part])

    grads, delta, new_m, new_v = {}, {}, {}, {}
    t_form = [n for n in COL_SHARDED if ws[n].shape[-1] % LANES]
    view = {n: (lambda a: a.reshape(a.shape[-2], a.shape[-1]).T) if n in t_form
            else (lambda a: a.reshape(a.shape[-2], a.shape[-1])) for n in BIG}
    unview = {n: (lambda a: a.T) if n in t_form else (lambda a: a) for n in BIG}

    group = [n for n in BIG if n != "w_gate"]
    sums, ((late_all,),) = _sum_ready([parts[n] for n in group], "sum_ready", [_AllGather(late_pack)],
                                      chip_sums=(group.index("ffn1_w1"), group.index("ffn1_w3")))
    res = _adamw_cols([(g, view[n](ws[n]), view[n](ms[n]), view[n](vs[n])) for n, g in zip(group, sums)], "adamw_big")
    for n, quad in zip(group, res):
        grads[n], delta[n], new_m[n], new_v[n] = (unview[n](a) for a in quad)
    (g_gate_t,) = _sum_parts([parts["w_gate"]])
    grads["w_gate"] = g_gate_t.T
    ((delta["w_gate"], new_m["w_gate"], new_v["w_gate"]),) = _adamw(
        [(grads["w_gate"], view["w_gate"](ws["w_gate"]), view["w_gate"](ms["w_gate"]), view["w_gate"](vs["w_gate"]))])
    for group, gathered, offs in ((early, early_all, early_offs), (late, late_all, late_offs)):
        total = _sum_small(gathered, gathered.shape[0] // NDEV)
        shapes = [(CONV_WIDTH, D) if n == "conv_w" else ws[n].shape for n in group]
        if group is late:
            shapes = shapes + [(1, LANES)]
        unpacked = _unpack_small(total, offs, shapes)
        if group is late:
            loss = unpacked.pop()[0, 0]
        for n, g in zip(group, unpacked):
            grads[n] = g
    grads["conv_w"] = lax.dynamic_slice(grads["conv_w"], (0, me * LANES), (CONV_WIDTH, LANES)).reshape(conv_w.shape)

    flat2d = lambda a: a.reshape(-1, a.shape[-1])
    res = _adamw_small([(flat2d(grads[n].reshape(ws[n].shape)), flat2d(ws[n]), flat2d(ms[n]), flat2d(vs[n]))
                        for n in SMALL])
    for n, (d_, m_, v_) in zip(SMALL, res):
        delta[n], new_m[n], new_v[n] = d_, m_, v_

    outs = [loss, grad_x.reshape(x.shape)]
    for src in (grads, delta, new_m, new_v):
        outs += [src[n].reshape(ws[n].shape) for n in names]
    return tuple(outs)
```

```python
import functools
import math
import operator

import numpy as np
import jax
import jax.numpy as jnp
from jax import lax
from jax.experimental import pallas as pl
from jax.experimental.pallas import tpu as pltpu

F32, BF16 = jnp.float32, jnp.bfloat16

NDEV = 8
D = 1024
FF = 2816
N_HEADS, N_KV, HEAD_DIM = 16, 4, 64
CHUNK, WINDOW = 64, 128
N_BUCKETS, MAX_DISTANCE = 32, 128
LRU_BLOCK = 64
CONV_WIDTH = 4
LRU_C = 8.0
RMS_EPS = 1e-6
NEG_INF = -1e30
LANES = 128
QT = 128
KW = QT + WINDOW
LRU_ROWS = 1024
IN_W = D + 2 * N_KV * HEAD_DIM + 2 * D
INP_W = D + 2 * N_KV * LANES + 2 * D
VMEM_BIG = 58 * 2 ** 20

ADAM_LR, ADAM_B1, ADAM_B2, ADAM_EPS, ADAM_WD, ADAM_STEP = 0.001, 0.9, 0.999, 1e-08, 0.01, 10

GROUPS = (("ffn1", (("ffn1_w1", FF // NDEV), ("ffn1_w3", FF // NDEV), ("ffn1_w2", FF // NDEV), ("conv", 16))),
          ("mixin", (("w_in", IN_W // NDEV), ("w_gate", 2 * D // NDEV))),
          ("mixout", (("w_lru_out", D // NDEV), ("w_attn_out", D // NDEV), ("w_o", D // NDEV))),
          ("ffn2a", (("ffn2_w1", FF // NDEV), ("ffn2_w3", FF // NDEV))),
          ("ffn2b", (("ffn2_w2", FF // NDEV),)))
LOC, GROUP_ROWS = {}, {}
for _g, _members in GROUPS:
    _o = 0
    for _n, _r in _members:
        LOC[_n] = (_g, _o, _r)
        _o += _r
    GROUP_ROWS[_g] = _o
BIG = tuple(n for _, members in GROUPS for n, _ in members if n != "conv")
COL_SHARDED = ("ffn1_w1", "ffn1_w3", "w_in", "w_gate", "ffn2_w1", "ffn2_w3")

SMALL = ("ffn1_pre_g", "ffn1_post_g", "mix_pre_g", "conv_w", "conv_b", "rg_a_w", "rg_a_b", "rg_x_w", "rg_x_b",
         "lru_lambda", "attn_sinks", "rel_bias", "b_gate", "mix_post_g", "ffn2_pre_g", "ffn2_post_g")

MESH = pl.DeviceIdType.MESH
ANY = pl.BlockSpec(memory_space=pl.ANY)
DMA = pltpu.SemaphoreType.DMA


def _nn(a, b):
    return lax.dot_general(a, b, (((1,), (0,)), ((), ())), preferred_element_type=F32)


def _nt(a, b):
    return lax.dot_general(a, b, (((1,), (1,)), ((), ())), preferred_element_type=F32)


def _tn(a, b):
    return lax.dot_general(a, b, (((0,), (0,)), ((), ())), preferred_element_type=F32)


def _rms_fwd(x, g):
    r = lax.rsqrt(jnp.mean(x * x, axis=-1, keepdims=True) + RMS_EPS)
    xh = x * r
    return xh * g, xh, r


def _rms_bwd(dn, xh, r, g):
    dxh = dn * g
    dx = r * (dxh - xh * jnp.mean(dxh * xh, axis=-1, keepdims=True))
    return dx, jnp.sum(dn * xh, axis=0, keepdims=True)


def _row_spec(tm, c):
    return pl.BlockSpec((tm, c), lambda i: (i, 0))


def _const_spec(shape):
    nd = len(shape)
    return pl.BlockSpec(shape, lambda i: (0,) * nd)


class _AllGather:
    waits_for_all = True

    def __init__(self, shard):
        self.m, n = shard.shape
        self.ins = [shard]
        self.out_shape = [jax.ShapeDtypeStruct((NDEV * self.m, n), shard.dtype)]
        self.scratch = [DMA((7,)), DMA((7,)), DMA]

    def _copies(self, ins, outs, scr, all_of_them):
        x_ref, out_ref = ins[0], outs[0]
        send_sems, recv_sems, local_sem = scr
        x, y, c = lax.axis_index("x"), lax.axis_index("y"), lax.axis_index("c")
        me, sibling = (x, y, c), (x, y, 1 - c)
        chips = [(1 - x, y), (x, 1 - y), (1 - x, 1 - y)]
        m = self.m

        def rows(px, py, pc):
            return out_ref.at[pl.ds((4 * px + 2 * py + pc) * m, m), :]

        def copy(k, block, to, src=None):
            return pltpu.make_async_remote_copy(
                src_ref=rows(*block) if src is None else src, dst_ref=rows(*block),
                send_sem=send_sems.at[k], recv_sem=recv_sems.at[k], device_id=to, device_id_type=MESH)

        mine = pltpu.make_async_copy(x_ref, rows(*me), local_sem)
        first = [copy(0, me, sibling, src=x_ref)] + [copy(1 + j, me, (*chip, c), src=x_ref)
                                                     for j, chip in enumerate(chips)]
        if not all_of_them:
            return mine, first
        passed = [copy(4 + j, (*chip, c), sibling) for j, chip in enumerate(chips)]
        landed = [copy(1 + j, (*chip, c), me) for j, chip in enumerate(chips)]
        from_sibling = [copy(0, sibling, me)] + [copy(4 + j, (*chip, 1 - c), me) for j, chip in enumerate(chips)]
        return mine, first, passed, landed, from_sibling

    def start(self, ins, outs, scr):
        mine, first = self._copies(ins, outs, scr, False)
        mine.start()
        for cp in first:
            cp.start()

    def pass_on(self, ins, outs, scr):
        _, _, passed, landed, _ = self._copies(ins, outs, scr, True)
        for cp_in, cp_on in zip(landed, passed):
            cp_in.wait_recv()
            cp_on.start()

    def finish(self, ins, outs, scr):
        mine, first, passed, landed, from_sibling = self._copies(ins, outs, scr, True)
        for cp in from_sibling:
            cp.wait_recv()
        for cp in first + passed:
            cp.wait_send()
        mine.wait()


class _AllGatherTree:
    waits_for_all = True

    def __init__(self, shard):
        self.m, n = shard.shape
        self.half = self.m // 32 * 16
        self.ins = [shard]
        self.out_shape = [jax.ShapeDtypeStruct((NDEV * self.m, n), shard.dtype)]
        self.scratch = [DMA((9,)), DMA((9,)), DMA]

    def start(self, ins, outs, scr):
        x_ref, out_ref = ins[0], outs[0]
        send_sems, recv_sems, local_sem = scr
        x, y, c = lax.axis_index("x"), lax.axis_index("y"), lax.axis_index("c")
        m, half = self.m, self.half
        me, sibling, xn, yn, diag = (x, y, c), (x, y, 1 - c), (1 - x, y, c), (x, 1 - y, c), (1 - x, 1 - y, c)
        other = lambda dev: (dev[0], dev[1], 1 - c)

        def rows(dev, lo=0, n=m):
            return out_ref.at[pl.ds((4 * dev[0] + 2 * dev[1] + dev[2]) * m + lo, n), :]

        def copy(k, dev, to, lo=0, n=m, src=None):
            return pltpu.make_async_remote_copy(
                src_ref=rows(dev, lo, n) if src is None else src, dst_ref=rows(dev, lo, n),
                send_sem=send_sems.at[k], recv_sem=recv_sems.at[k], device_id=to, device_id_type=MESH)

        mine = pltpu.make_async_copy(x_ref, rows(me), local_sem)
        own = [copy(0, me, sibling, src=x_ref), copy(1, me, xn, src=x_ref), copy(2, me, yn, src=x_ref)]
        mine.start()
        for cp in own:
            cp.start()
        steps = [(copy(1, xn, me), [copy(3, xn, yn, 0, half), copy(5, xn, sibling)]),
                 (copy(2, yn, me), [copy(4, yn, xn, half, m - half), copy(6, yn, sibling)]),
                 (copy(3, diag, me, 0, half), [copy(7, diag, sibling, 0, half)]),
                 (copy(4, diag, me, half, m - half), [copy(8, diag, sibling, half, m - half)])]
        from_sibling = [copy(0, sibling, me), copy(5, other(xn), me), copy(6, other(yn), me),
                        copy(7, other(diag), me, 0, half), copy(8, other(diag), me, half, m - half)]
        passed = []
        for landing, onward in steps:
            landing.wait_recv()
            for cp in onward:
                cp.start()
            passed += onward
        for cp in from_sibling:
            cp.wait_recv()
        for cp in own + passed:
            cp.wait_send()
        mine.wait()

    def finish(self, ins, outs, scr):
        pass


class _ReduceScatterSend:
    def __init__(self, grads, chip_sums=False):
        grads = [g if isinstance(g, tuple) else (g, 0, g.shape[0] // NDEV, g.shape[0] // NDEV) for g in grads]
        self.peers = (2, 4, 6) if chip_sums else tuple(range(1, NDEV))
        self.waits_for_all = not chip_sums
        self.nw = len(grads)
        self.base = [b for _, b, _, _ in grads]
        self.stride = [s for _, _, s, _ in grads]
        self.rows = [r for _, _, _, r in grads]
        self.ins = [g for g, _, _, _ in grads]
        self.out_shape = [jax.ShapeDtypeStruct((NDEV, r, g.shape[1]), g.dtype) for g, _, _, r in grads]
        self.scratch = [DMA((self.nw, NDEV - 1)), DMA((self.nw, NDEV - 1)), DMA((self.nw,))]

    def _copies(self, g_refs, r_refs, scr, want):
        send_sems, recv_sems, local_sems = scr
        x, y, c = lax.axis_index("x"), lax.axis_index("y"), lax.axis_index("c")
        me = 4 * x + 2 * y + c
        rows, base, stride = self.rows, self.base, self.stride
        out = []
        if want == "local":
            for w in range(self.nw):
                out.append(pltpu.make_async_copy(g_refs[w].at[pl.ds(base[w] + me * stride[w], rows[w])],
                                                 r_refs[w].at[me], local_sems.at[w]))
            return out
        for k in self.peers:
            px, py, pc = x ^ (k >> 2), y ^ ((k >> 1) & 1), c ^ (k & 1)
            peer = 4 * px + 2 * py + pc
            for w in range(self.nw):
                sems = dict(send_sem=send_sems.at[w, k - 1], recv_sem=recv_sems.at[w, k - 1],
                            device_id=(px, py, pc), device_id_type=MESH)
                if want == "send":
                    out.append(pltpu.make_async_remote_copy(
                        src_ref=g_refs[w].at[pl.ds(base[w] + peer * stride[w], rows[w])], dst_ref=r_refs[w].at[me],
                        **sems))
                else:
                    out.append(pltpu.make_async_remote_copy(
                        src_ref=g_refs[w].at[pl.ds(0, rows[w])], dst_ref=r_refs[w].at[peer], **sems))
        return out

    def start(self, ins, outs, scr):
        for cp in self._copies(ins, outs, scr, "local") + self._copies(ins, outs, scr, "send"):
            cp.start()

    def finish(self, ins, outs, scr):
        for cp in self._copies(ins, outs, scr, "recv"):
            cp.wait_recv()
        for cp in self._copies(ins, outs, scr, "send"):
            cp.wait_send()
        for cp in self._copies(ins, outs, scr, "local"):
            cp.wait()


def _mesh_barrier():
    barrier = pltpu.get_barrier_semaphore()
    x, y, c = lax.axis_index("x"), lax.axis_index("y"), lax.axis_index("c")
    for k in range(1, NDEV):
        pl.semaphore_signal(barrier, inc=1, device_id=(x ^ (k >> 2), y ^ ((k >> 1) & 1), c ^ (k & 1)),
                            device_id_type=MESH)
    pl.semaphore_wait(barrier, NDEV - 1)


def _call(body, *, name, grid, ins, in_specs, out_shape, out_specs, scratch=(), vmem=None, carries=()):
    n_in, n_out, n_scr = len(ins), len(out_shape), len(scratch)
    ng = len(grid)
    own_barrier = any(getattr(c, "waits_for_all", False) for c in carries)

    def split(refs):
        pos = [0]

        def take(k):
            part = refs[pos[0]:pos[0] + k]
            pos[0] += k
            return part

        i_refs = take(n_in)
        c_in = [take(len(c.ins)) for c in carries]
        o_refs = take(n_out)
        c_out = [take(len(c.out_shape)) for c in carries]
        s_refs = take(n_scr)
        c_scr = [take(len(c.scratch)) for c in carries]
        return i_refs, o_refs, s_refs, list(zip(carries, c_in, c_out, c_scr))

    def full(*refs):
        i_refs, o_refs, s_refs, cparts = split(refs)
        relays = [part for part in cparts if hasattr(part[0], "pass_on")]
        if ng == 0:
            if own_barrier:
                _mesh_barrier()
            for c, a, b, s in cparts:
                c.start(a, b, s)
            for c, a, b, s in relays:
                c.pass_on(a, b, s)
            for c, a, b, s in cparts:
                c.finish(a, b, s)
            return
        ids = [pl.program_id(a) for a in range(ng)]
        ahead = grid[-1] >= 2
        if cparts:
            @pl.when(functools.reduce(operator.and_, [i == 0 for i in ids]))
            def _():
                if own_barrier:
                    _mesh_barrier()
                for c, a, b, s in cparts:
                    c.start(a, b, s)

        if relays and ahead:
            @pl.when(functools.reduce(operator.and_, [i == g - 1 for i, g in zip(ids[:-1], grid[:-1])]
                                      + [ids[-1] == grid[-1] - 2]))
            def _():
                for c, a, b, s in relays:
                    c.pass_on(a, b, s)

        body(*i_refs, *o_refs, *s_refs)
        if cparts:
            @pl.when(functools.reduce(operator.and_, [i == g - 1 for i, g in zip(ids, grid)]))
            def _():
                if not ahead:
                    for c, a, b, s in relays:
                        c.pass_on(a, b, s)
                for c, a, b, s in cparts:
                    c.finish(a, b, s)

    all_ins = list(ins) + [a for c in carries for a in c.ins]
    all_in_specs = list(in_specs) + [ANY for c in carries for _ in c.ins]
    all_out_shape = list(out_shape) + [s for c in carries for s in c.out_shape]
    all_out_specs = list(out_specs) + [ANY for c in carries for _ in c.out_shape]
    all_scratch = list(scratch) + [s for c in carries for s in c.scratch]
    kwargs = dict(grid=grid) if ng else {}
    outs = pl.pallas_call(
        full, name=name, in_specs=all_in_specs, out_specs=all_out_specs, out_shape=all_out_shape,
        scratch_shapes=all_scratch,
        compiler_params=pltpu.CompilerParams(dimension_semantics=("arbitrary",) * ng if ng else None,
                                             vmem_limit_bytes=vmem, collective_id=0 if own_barrier else None),
        **kwargs)(*all_ins)
    outs = list(outs)
    res, pos = outs[:n_out], n_out
    carried = []
    for c in carries:
        carried.append(outs[pos:pos + len(c.out_shape)])
        pos += len(c.out_shape)
    return res, carried


def _pair_sum(g, items):
    rows = items[0][1]
    n = len(items) * (NDEV // 2)
    width = g.shape[1]

    def body(g_ref, out_ref, got_ref, mine_v, got_v, send_sems, recv_sems, ld_sems, st_sems):
        x, y, c = lax.axis_index("x"), lax.axis_index("y"), lax.axis_index("c")
        keep = [base + (2 * ch + c) * rows for base, _ in items for ch in range(NDEV // 2)]
        give = [base + (2 * ch + 1 - c) * rows for base, _ in items for ch in range(NDEV // 2)]

        def remote(j, r):
            return pltpu.make_async_remote_copy(
                src_ref=g_ref.at[pl.ds(r, rows)], dst_ref=got_ref.at[pl.ds(r, rows)], send_sem=send_sems.at[j],
                recv_sem=recv_sems.at[j], device_id=(x, y, 1 - c), device_id_type=MESH)

        sends = [remote(j, r) for j, r in enumerate(give)]
        for cp in sends:
            cp.start()
        loads = [pltpu.make_async_copy(g_ref.at[pl.ds(r, rows)], mine_v.at[j], ld_sems.at[0, j])
                 for j, r in enumerate(keep)]
        for cp in loads:
            cp.start()
        stores = []
        for j, r in enumerate(keep):
            remote(j, r).wait_recv()
            cp = pltpu.make_async_copy(got_ref.at[pl.ds(r, rows)], got_v.at[j], ld_sems.at[1, j])
            cp.start()
            loads[j].wait()
            cp.wait()
            mine_v[j] = (mine_v[j].astype(F32) + got_v[j].astype(F32)).astype(BF16)
            st = pltpu.make_async_copy(mine_v.at[j], out_ref.at[pl.ds(r, rows)], st_sems.at[j])
            st.start()
            stores.append(st)
        for cp in sends:
            cp.wait_send()
        for cp in stores:
            cp.wait()

    out, _ = pl.pallas_call(
        body, name="pair_sum", in_specs=[ANY], out_specs=[ANY, ANY],
        out_shape=[jax.ShapeDtypeStruct(g.shape, g.dtype), jax.ShapeDtypeStruct(g.shape, g.dtype)],
        scratch_shapes=[pltpu.VMEM((n, rows, width), g.dtype), pltpu.VMEM((n, rows, width), g.dtype),
                        DMA((n,)), DMA((n,)), DMA((2, n)), DMA((n,))],
        compiler_params=pltpu.CompilerParams(vmem_limit_bytes=40 * 2 ** 20))(g)
    return out


def _groups_of(names):
    out = []
    for n in names:
        if LOC[n][0] not in out:
            out.append(LOC[n][0])
    return out


def _weight_pieces(name):
    g, off, rows = LOC[name]
    return [(d * GROUP_ROWS[g] + off, d * rows, rows) for d in range(NDEV)]


def _win_pieces():
    kv = N_KV * HEAD_DIM
    pieces = [(0, 0, D)]
    for g0, d0 in ((D, D), (D + kv, D + N_KV * LANES)):
        for g in range(N_KV):
            for half in range(2):
                pieces.append((g0 + g * HEAD_DIM, d0 + g * LANES + half * HEAD_DIM, HEAD_DIM))
    pieces.append((D + 2 * kv, D + 2 * N_KV * LANES, D))
    pieces.append((D + 2 * kv + D, D + 2 * N_KV * LANES + D, D))
    grp, off, rows = LOC["w_in"]
    out = []
    for g0, d0, n in pieces:
        while n > 0:
            dev, loc = divmod(g0, rows)
            m = min(n, rows - loc)
            out.append((dev * GROUP_ROWS[grp] + off + loc, d0, m))
            g0, d0, n = g0 + m, d0 + m, n - m
    return out


def _start_loads(src_ref, dst_ref, pieces, sems, base):
    cps = []
    for j, (s, d, n) in enumerate(pieces):
        cp = pltpu.make_async_copy(src_ref.at[pl.ds(s, n)], dst_ref.at[pl.ds(d, n)], sems.at[base + j])
        cp.start()
        cps.append(cp)
    return cps


def _load_weights(wrefs, targets, sems):
    cps, base = [], 0
    for name, dst in targets:
        pieces = _win_pieces() if name == "w_in" else _weight_pieces(name)
        cps += _start_loads(wrefs[LOC[name][0]], dst, pieces, sems, base)
        base += len(pieces)
    for cp in cps:
        cp.wait()


def _n_pieces(names):
    return sum(len(_win_pieces()) if n == "w_in" else NDEV for n in names)


def _dw(lhs, rhs, chunk, name, carries=()):
    nq = 4
    if lhs.ndim == 3:
        nch, t_tok, chunk = lhs.shape
        c = nch * chunk
        tq = t_tok // nq
        lhs_specs = [pl.BlockSpec((None, tq, chunk), lambda i, k=k: (i, k, 0)) for k in range(nq)]
    else:
        t_tok, c = lhs.shape
        tq = t_tok // nq
        lhs_specs = [pl.BlockSpec((tq, chunk), lambda i, k=k: (k, i)) for k in range(nq)]

    def body(*refs):
        lhs_refs, (rhs_ref, out_ref, rhs_s, sems) = refs[:nq], refs[nq:]
        cps = [pltpu.make_async_copy(rhs_ref.at[pl.ds(k * tq, tq)], rhs_s.at[pl.ds(k * tq, tq)], sems.at[k])
               for k in range(nq)]
        first = pl.program_id(0) == 0

        @pl.when(first)
        def _():
            for cp in cps:
                cp.start()

        for o in range(0, chunk, D // 2):
            n = min(D // 2, chunk - o)
            acc = None
            for k in range(nq):
                if o == 0:
                    @pl.when(first)
                    def _():
                        cps[k].wait()

                part = _tn(lhs_refs[k][:, pl.ds(o, n)], rhs_s[pl.ds(k * tq, tq), :])
                acc = part if acc is None else acc + part
            out_ref[pl.ds(o, n), :] = acc.astype(BF16)

    (out,), carried = _call(
        body, name=name, grid=(c // chunk,), ins=[lhs] * nq + [rhs],
        in_specs=lhs_specs + [ANY],
        out_specs=[pl.BlockSpec((chunk, D), lambda i: (i, 0))],
        out_shape=[jax.ShapeDtypeStruct((c, D), BF16)],
        scratch=[pltpu.VMEM((t_tok, D), BF16), DMA((nq,))], vmem=VMEM_BIG, carries=carries)
    return out, carried


def _silu_parts(a):
    sig = jax.nn.sigmoid(a)
    return sig, a * sig


FC = 256


def _ffn_fwd(h, gpre, gpost, wg, names, target=None, tm=512, carries=()):
    t_tok = h.shape[0]
    nt = t_tok // tm
    with_loss = target is not None
    groups = _groups_of(names)

    def body(*refs):
        refs = list(refs)
        h_ref, gpre_ref, gpost_ref = refs[:3]
        del refs[:3]
        tgt_ref = refs.pop(0) if with_loss else None
        wrefs = dict(zip(groups, refs[:len(groups)]))
        del refs[:len(groups)]
        hout_ref, a_ref, b_ref, f_ref, nb_ref = refs[:5]
        del refs[:5]
        if with_loss:
            dy_ref, loss_ref = refs[:2]
            del refs[:2]
        w1_s, w3_s, w2_s, sems = refs
        i = pl.program_id(0)

        @pl.when(i == 0)
        def _():
            _load_weights(wrefs, list(zip(names, (w1_s, w3_s, w2_s))), sems)
            if with_loss:
                loss_ref[...] = jnp.zeros_like(loss_ref)

        x = h_ref[...]
        n, _, _ = _rms_fwd(x, gpre_ref[...])
        nb = n.astype(BF16)
        nb_ref[...] = nb
        f = jnp.zeros((tm, D), F32)
        for c0 in range(0, FF, FC):
            a = _nt(nb, w1_s[pl.ds(c0, FC), :])
            b = _nt(nb, w3_s[pl.ds(c0, FC), :])
            _, sl = _silu_parts(a)
            a_ref[:, pl.ds(c0, FC)] = a.astype(BF16)
            b_ref[:, pl.ds(c0, FC)] = b.astype(BF16)
            f = f + _nn((sl * b).astype(BF16), w2_s[pl.ds(c0, FC), :])
        f_ref[...] = f
        fn, _, _ = _rms_fwd(f, gpost_ref[...])
        y = x + 0.5 * fn
        hout_ref[...] = y
        if with_loss:
            err = y - tgt_ref[...]
            dy_ref[...] = err * (1.0 / D)
            loss_ref[...] += jnp.sum(jnp.sum(err * err, axis=-1, keepdims=True), axis=0, keepdims=True) * (0.5 / D)

    ins = [h, gpre, gpost] + ([target] if with_loss else []) + [wg[g] for g in groups]
    in_specs = [_row_spec(tm, D), _const_spec((1, D)), _const_spec((1, D))]
    in_specs += ([_row_spec(tm, D)] if with_loss else []) + [ANY] * len(groups)
    out_shape = [jax.ShapeDtypeStruct((t_tok, D), F32), jax.ShapeDtypeStruct((t_tok, FF), BF16),
                 jax.ShapeDtypeStruct((t_tok, FF), BF16), jax.ShapeDtypeStruct((t_tok, D), F32),
                 jax.ShapeDtypeStruct((t_tok, D), BF16)]
    out_specs = [_row_spec(tm, D), _row_spec(tm, FF), _row_spec(tm, FF), _row_spec(tm, D), _row_spec(tm, D)]
    if with_loss:
        out_shape += [jax.ShapeDtypeStruct((t_tok, D), F32), jax.ShapeDtypeStruct((1, LANES), F32)]
        out_specs += [_row_spec(tm, D), _const_spec((1, LANES))]
    return _call(body, name="ffn_fwd_" + names[0][:4], grid=(nt,), ins=ins, in_specs=in_specs,
                 out_shape=out_shape, out_specs=out_specs,
                 scratch=[pltpu.VMEM((FF, D), BF16)] * 3 + [DMA((3 * NDEV,))], vmem=VMEM_BIG, carries=carries)


FH = FF // 2
HALF_PIECES = ((0, 256), (256, 256), (512, 256), (768, 256), (1024, 256), (1280, 128))


def _ffn_bwd_a(dh, f, a, b, gpost, wg, name_w2, tm=512, carries=()):
    t_tok = dh.shape[0]
    nt = t_tok // tm
    groups = _groups_of([name_w2])

    def body(dh_ref, f_ref, a_ref, b_ref, gpost_ref, wg_ref, dab_ref, s_ref, df_ref, dgp_ref, w2_s, sems):
        i = pl.program_id(0)

        @pl.when(i == 0)
        def _():
            _load_weights({groups[0]: wg_ref}, [(name_w2, w2_s)], sems)
            dgp_ref[...] = jnp.zeros_like(dgp_ref)

        fv = f_ref[...]
        _, fh, r = _rms_fwd(fv, gpost_ref[...])
        df, dg = _rms_bwd(0.5 * dh_ref[...], fh, r, gpost_ref[...])
        dgp_ref[...] += dg
        dfb = df.astype(BF16)
        df_ref[...] = dfb
        for half in range(2):
            for o, n in HALF_PIECES:
                c0 = half * FH + o
                ds = _nt(dfb, w2_s[pl.ds(c0, n), :])
                av = a_ref[:, pl.ds(c0, n)].astype(F32)
                bv = b_ref[:, pl.ds(c0, n)].astype(F32)
                sig, sl = _silu_parts(av)
                dab_ref[half, :, pl.ds(o, n)] = (ds * bv * (sig * (1.0 + av * (1.0 - sig)))).astype(BF16)
                dab_ref[2 + half, :, pl.ds(o, n)] = (ds * sl).astype(BF16)
                s_ref[half, :, pl.ds(o, n)] = (sl * bv).astype(BF16)

    return _call(
        body, name="ffn_bwd_a_" + name_w2[:4], grid=(nt,), ins=[dh, f, a, b, gpost, wg[groups[0]]],
        in_specs=[_row_spec(tm, D), _row_spec(tm, D), _row_spec(tm, FF), _row_spec(tm, FF), _const_spec((1, D)), ANY],
        out_specs=[pl.BlockSpec((4, tm, FH), lambda i: (0, i, 0)), pl.BlockSpec((2, tm, FH), lambda i: (0, i, 0)),
                   _row_spec(tm, D), _const_spec((1, D))],
        out_shape=[jax.ShapeDtypeStruct((4, t_tok, FH), BF16), jax.ShapeDtypeStruct((2, t_tok, FH), BF16),
                   jax.ShapeDtypeStruct((t_tok, D), BF16), jax.ShapeDtypeStruct((1, D), F32)],
        scratch=[pltpu.VMEM((FF, D), BF16), DMA((NDEV,))],
        vmem=VMEM_BIG, carries=carries)


def _ffn_bwd_dx(dab, h, dh, gpre, wg, name_w1, name_w3, tm=512, carries=()):
    t_tok = dh.shape[0]
    nt = t_tok // tm
    groups = _groups_of([name_w1, name_w3])

    def body(*refs):
        dab_refs, (h_ref, dh_ref, gpre_ref) = refs[:4], refs[4:7]
        wrefs = dict(zip(groups, refs[7:7 + len(groups)]))
        dhin_ref, dgp_ref, w13_s, sems = refs[7 + len(groups):]
        i = pl.program_id(0)

        @pl.when(i == 0)
        def _():
            _load_weights(wrefs, [(name_w1, w13_s.at[pl.ds(0, FF)]), (name_w3, w13_s.at[pl.ds(FF, FF)])], sems)
            dgp_ref[...] = jnp.zeros_like(dgp_ref)

        g = gpre_ref[...]
        _, xh, r = _rms_fwd(h_ref[...], g)
        dn = _nn(dab_refs[0][...], w13_s[pl.ds(0, FH), :])
        for k in range(1, 4):
            dn = dn + _nn(dab_refs[k][...], w13_s[pl.ds(k * FH, FH), :])
        dx, dg = _rms_bwd(dn, xh, r, g)
        dgp_ref[...] += dg
        dhin_ref[...] = dh_ref[...] + dx

    return _call(
        body, name="ffn_bwd_dx_" + name_w1[:4], grid=(nt,), ins=[dab] * 4 + [h, dh, gpre] + [wg[g] for g in groups],
        in_specs=[pl.BlockSpec((None, tm, FH), lambda i, k=k: (k, i, 0)) for k in range(4)]
        + [_row_spec(tm, D), _row_spec(tm, D), _const_spec((1, D))] + [ANY] * len(groups),
        out_specs=[_row_spec(tm, D), _const_spec((1, D))],
        out_shape=[jax.ShapeDtypeStruct((t_tok, D), F32), jax.ShapeDtypeStruct((1, D), F32)],
        scratch=[pltpu.VMEM((2 * FF, D), BF16), DMA((2 * NDEV,))],
        vmem=VMEM_BIG, carries=carries)


Q0, K0, V0, XR0, XG0 = 0, D, D + N_KV * LANES, D + 2 * N_KV * LANES, 2 * D + 2 * N_KV * LANES


def _mix_proj_fwd(h, g, bgate, wg, tm=512, carries=()):
    t_tok = h.shape[0]
    nt = t_tok // tm
    names = ("w_in", "w_gate")
    groups = _groups_of(names)

    def body(h_ref, g_ref, bg_ref, wg_ref, q_ref, k_ref, v_ref, xr_ref, xg_ref, gs_ref, ub_ref, win_s, wgt_s, sems):
        i = pl.program_id(0)

        @pl.when(i == 0)
        def _():
            _load_weights({groups[0]: wg_ref}, [("w_in", win_s), ("w_gate", wgt_s)], sems)

        n, _, _ = _rms_fwd(h_ref[...], g_ref[...])
        nb = n.astype(BF16)
        ub_ref[...] = nb
        q_ref[...] = _nt(nb, win_s[pl.ds(Q0, D), :]).astype(BF16)
        k_ref[...] = _nt(nb, win_s[pl.ds(K0, N_KV * LANES), :]).astype(BF16)
        v_ref[...] = _nt(nb, win_s[pl.ds(V0, N_KV * LANES), :]).astype(BF16)
        xr_ref[...] = _nt(nb, win_s[pl.ds(XR0, D), :])
        xg_ref[...] = _nt(nb, win_s[pl.ds(XG0, D), :])
        gs_ref[...] = jax.nn.sigmoid(_nt(nb, wgt_s[...]) + bg_ref[...]).astype(BF16)

    kvw = N_KV * LANES
    return _call(
        body, name="mix_proj_fwd", grid=(nt,), ins=[h, g, bgate, wg[groups[0]]],
        in_specs=[_row_spec(tm, D), _const_spec((1, D)), _const_spec((1, 2 * D)), ANY],
        out_specs=[_row_spec(tm, D), _row_spec(tm, kvw), _row_spec(tm, kvw), _row_spec(tm, D), _row_spec(tm, D),
                   _row_spec(tm, 2 * D), _row_spec(tm, D)],
        out_shape=[jax.ShapeDtypeStruct((t_tok, D), BF16), jax.ShapeDtypeStruct((t_tok, kvw), BF16),
                   jax.ShapeDtypeStruct((t_tok, kvw), BF16), jax.ShapeDtypeStruct((t_tok, D), F32),
                   jax.ShapeDtypeStruct((t_tok, D), F32), jax.ShapeDtypeStruct((t_tok, 2 * D), BF16),
                   jax.ShapeDtypeStruct((t_tok, D), BF16)],
        scratch=[pltpu.VMEM((INP_W, D), BF16), pltpu.VMEM((2 * D, D), BF16), DMA((_n_pieces(names),))],
        vmem=VMEM_BIG, carries=carries)


def _mix_proj_bwd(dqkv, dxr, dxg, dgpre, h, dh, g, wg, tm=512, carries=()):
    t_tok = h.shape[0]
    nt = t_tok // tm
    names = ("w_in", "w_gate")
    groups = _groups_of(names)

    def body(dqkv_ref, dxr_ref, dxg_ref, dgp_ref, h_ref, dh_ref, g_ref, wg_ref, dhin_ref, dg_ref, win_s, wgt_s, sems):
        i = pl.program_id(0)

        @pl.when(i == 0)
        def _():
            _load_weights({groups[0]: wg_ref}, [("w_in", win_s), ("w_gate", wgt_s)], sems)
            dg_ref[...] = jnp.zeros_like(dg_ref)

        gv = g_ref[...]
        _, xh, r = _rms_fwd(h_ref[...], gv)
        du = _nn(dgp_ref[...], wgt_s[...])
        du = du + _nn(dqkv_ref[...], win_s[pl.ds(Q0, XR0), :])
        du = du + _nn(dxr_ref[...], win_s[pl.ds(XR0, D), :])
        du = du + _nn(dxg_ref[...], win_s[pl.ds(XG0, D), :])
        dx, dg = _rms_bwd(du, xh, r, gv)
        dg_ref[...] += dg
        dhin_ref[...] = dh_ref[...] + dx

    return _call(
        body, name="mix_proj_bwd", grid=(nt,), ins=[dqkv, dxr, dxg, dgpre, h, dh, g, wg[groups[0]]],
        in_specs=[_row_spec(tm, XR0), _row_spec(tm, D), _row_spec(tm, D), _row_spec(tm, 2 * D), _row_spec(tm, D),
                  _row_spec(tm, D), _const_spec((1, D)), ANY],
        out_specs=[_row_spec(tm, D), _const_spec((1, D))],
        out_shape=[jax.ShapeDtypeStruct((t_tok, D), F32), jax.ShapeDtypeStruct((1, D), F32)],
        scratch=[pltpu.VMEM((INP_W, D), BF16), pltpu.VMEM((2 * D, D), BF16), DMA((_n_pieces(names),))],
        vmem=VMEM_BIG, carries=carries)


def _shift_down(x, d, fill):
    row = lax.broadcasted_iota(jnp.int32, x.shape, 0)
    return jnp.where(row >= d, pltpu.roll(x, d, 0), fill)


def _shift_up(x, d, fill):
    rows = x.shape[0]
    row = lax.broadcasted_iota(jnp.int32, x.shape, 0)
    return jnp.where(row < rows - d, pltpu.roll(x, rows - d, 0), fill)


def _scan_rows(a, b, reverse):
    rows = a.shape[0]
    d = 1
    while d < rows:
        if d < 8:
            shift = _shift_up if reverse else _shift_down
            b = a * shift(b, d, 0.0) + b
            a = a * shift(a, d, 1.0)
        elif reverse:
            b = jnp.concatenate([a[:rows - d] * b[d:] + b[:rows - d], b[rows - d:]], axis=0)
            a = jnp.concatenate([a[:rows - d] * a[d:], a[rows - d:]], axis=0)
        else:
            b = jnp.concatenate([b[:d], a[d:] * b[:rows - d] + b[d:]], axis=0)
            a = jnp.concatenate([a[:d], a[d:] * a[:rows - d]], axis=0)
        d *= 2
    return a, b


def _softplus(x):
    return jnp.maximum(x, 0.0) + jnp.log(1.0 + jnp.exp(-jnp.abs(x)))


_GELU_C = math.sqrt(2.0 / math.pi)


def _gelu_parts(x):
    th = jnp.tanh(_GELU_C * (x + 0.044715 * x * x * x))
    val = 0.5 * x * (1.0 + th)
    grad = 0.5 * (1.0 + th) + 0.5 * x * (1.0 - th * th) * _GELU_C * (1.0 + 3.0 * 0.044715 * x * x)
    return val, grad


def _lru_pre(x, halo, cw_ref, cb_ref, wa_ref, wx_ref, ba_ref, bx_ref, lam_ref):
    ext = jnp.concatenate([halo, x], axis=0)
    shifted = [x] + [pltpu.roll(ext, k, 0)[8:] for k in (1, 2, 3)]
    xc = cb_ref[...] + cw_ref[pl.ds(CONV_WIDTH - 1, 1), :] * x
    for k in (1, 2, 3):
        xc = xc + cw_ref[pl.ds(CONV_WIDTH - 1 - k, 1), :] * shifted[k]
    xcb = xc.astype(BF16)
    r = jax.nn.sigmoid(_nn(xcb, wa_ref[...]) + ba_ref[...])
    ig = jax.nn.sigmoid(_nn(xcb, wx_ref[...]) + bx_ref[...])
    sp = _softplus(-lam_ref[...])
    log_a = -LRU_C * r * sp
    a = jnp.exp(log_a)
    th = jnp.tanh(log_a)
    mult = jnp.sqrt(-2.0 * th / (1.0 - th))
    return shifted, xc, xcb, r, ig, sp, a, mult


def _lru_specs(nt, rows, reverse):
    def tt(t):
        return nt - 1 - t if reverse else t
    tile = pl.BlockSpec((rows, LANES), lambda cb, t: (tt(t), cb))
    halo = pl.BlockSpec((8, LANES), lambda cb, t: (jnp.maximum(tt(t) * (rows // 8) - 1, 0), cb))
    vec = pl.BlockSpec((1, LANES), lambda cb, t: (0, cb))
    cw = pl.BlockSpec((CONV_WIDTH, LANES), lambda cb, t: (0, cb))
    mat = pl.BlockSpec((None, LANES, LANES), lambda cb, t: (cb, 0, 0))
    return tile, halo, vec, cw, mat


def _lru_fwd(xr, xg, cw, cb, wa, wx, ba, bx, lam, carries=()):
    t_tok = xr.shape[0]
    rows = min(LRU_ROWS, t_tok)
    nt = t_tok // rows

    def body(xr_ref, xg_ref, cw_ref, cb_ref, wa_ref, wx_ref, ba_ref, bx_ref, lam_ref, y_ref, h_ref, tail_s, hc_s):
        t = pl.program_id(1)

        @pl.when(t == 0)
        def _():
            tail_s[...] = jnp.zeros_like(tail_s)
            hc_s[...] = jnp.zeros_like(hc_s)

        x = xr_ref[...]
        _, xc, _, _, ig, _, a, mult = _lru_pre(x, tail_s[...], cw_ref, cb_ref, wa_ref, wx_ref, ba_ref, bx_ref, lam_ref)
        tail_s[...] = xr_ref[pl.ds(rows - 8, 8), :]
        acc_a, acc_b = _scan_rows(a, mult * (ig * xc), False)
        hv = acc_b + acc_a * hc_s[...]
        h_ref[...] = hv
        hc_s[...] = h_ref[pl.ds(rows - 1, 1), :]
        gl, _ = _gelu_parts(xg_ref[...])
        y_ref[...] = (hv * gl).astype(BF16)

    tile, _, vec, cws, mat = _lru_specs(nt, rows, False)
    return _call(
        body, name="lru_fwd", grid=(D // LANES, nt), ins=[xr, xg, cw, cb, wa, wx, ba, bx, lam],
        in_specs=[tile, tile, cws, vec, mat, mat, vec, vec, vec],
        out_specs=[tile, tile],
        out_shape=[jax.ShapeDtypeStruct((t_tok, D), BF16), jax.ShapeDtypeStruct((t_tok, D), F32)],
        scratch=[pltpu.VMEM((8, LANES), F32), pltpu.VMEM((1, LANES), F32)], carries=carries)


def _lru_bwd(dy, xr, xg, hseq, cw, cb, wa, wx, ba, bx, lam, carries=()):
    t_tok = xr.shape[0]
    rows = min(LRU_ROWS // 2, t_tok)
    nt = t_tok // rows

    def body(dy_ref, xr_ref, xrh_ref, xg_ref, h_ref, hh_ref, cw_ref, cb_ref, wa_ref, wx_ref, ba_ref, bx_ref, lam_ref,
             dxr_ref, dxg_ref, dvec_ref, dwa_ref, dwx_ref, gcar_s, acar_s, head_s, tmp_s):
        t = pl.program_id(1)
        first_tile = t == nt - 1

        @pl.when(t == 0)
        def _():
            gcar_s[...] = jnp.zeros_like(gcar_s)
            acar_s[...] = jnp.zeros_like(acar_s)
            head_s[...] = jnp.zeros_like(head_s)
            dvec_ref[...] = jnp.zeros_like(dvec_ref)
            dwa_ref[...] = jnp.zeros_like(dwa_ref)
            dwx_ref[...] = jnp.zeros_like(dwx_ref)

        x = xr_ref[...]
        halo = jnp.where(first_tile, 0.0, xrh_ref[...])
        shifted, xc, xcb, r, ig, sp, a, mult = _lru_pre(x, halo, cw_ref, cb_ref, wa_ref, wx_ref, ba_ref, bx_ref, lam_ref)
        hv = h_ref[...]
        dyv = dy_ref[...]
        gl, glg = _gelu_parts(xg_ref[...])
        dxg_ref[...] = (dyv * hv * glg).astype(BF16)
        acc_a, acc_b = _scan_rows(_shift_up(a, 1, acar_s[...]), dyv * gl, True)
        g = acc_b + acc_a * gcar_s[...]
        hhalo = jnp.where(first_tile, 0.0, hh_ref[...])
        hprev = pltpu.roll(jnp.concatenate([hhalo, hv], axis=0), 1, 0)[8:]
        dmult = g * ig * xc
        dlog_a = a * (g * hprev) - dmult * a * a / mult
        dig = g * mult * xc
        dxc = g * mult * ig
        dzr = (dlog_a * (-LRU_C * sp)) * r * (1.0 - r)
        dzx = dig * ig * (1.0 - ig)
        dzrb, dzxb = dzr.astype(BF16), dzx.astype(BF16)
        dxc = dxc + _nt(dzrb, wa_ref[...]) + _nt(dzxb, wx_ref[...])
        dwa_ref[...] += _tn(xcb, dzrb)
        dwx_ref[...] += _tn(xcb, dzxb)
        dsp = jnp.sum(dlog_a * (-LRU_C * r), axis=0, keepdims=True)
        dlam = dsp * (-jax.nn.sigmoid(-lam_ref[...]))
        vrow = lax.broadcasted_iota(jnp.int32, (8, LANES), 0)
        upd = jnp.where(vrow == 4, jnp.sum(dxc, axis=0, keepdims=True), 0.0)
        upd = jnp.where(vrow == 5, jnp.sum(dzr, axis=0, keepdims=True), upd)
        upd = jnp.where(vrow == 6, jnp.sum(dzx, axis=0, keepdims=True), upd)
        upd = jnp.where(vrow == 7, dlam, upd)
        for k in range(CONV_WIDTH):
            upd = jnp.where(vrow == CONV_WIDTH - 1 - k, jnp.sum(dxc * shifted[k], axis=0, keepdims=True), upd)
        dvec_ref[...] += upd
        ext = jnp.concatenate([dxc, head_s[...]], axis=0)
        dxr = cw_ref[pl.ds(CONV_WIDTH - 1, 1), :] * dxc
        for k in (1, 2, 3):
            dxr = dxr + cw_ref[pl.ds(CONV_WIDTH - 1 - k, 1), :] * pltpu.roll(ext, rows + 8 - k, 0)[:rows]
        dxr_ref[...] = dxr.astype(BF16)
        tmp_s[...] = g
        gcar_s[...] = tmp_s[pl.ds(0, 1), :]
        tmp_s[...] = a
        acar_s[...] = tmp_s[pl.ds(0, 1), :]
        tmp_s[...] = dxc
        head_s[...] = tmp_s[pl.ds(0, 8), :]

    tile, halo, vec, cws, mat = _lru_specs(nt, rows, True)
    return _call(
        body, name="lru_bwd", grid=(D // LANES, nt), ins=[dy, xr, xr, xg, hseq, hseq, cw, cb, wa, wx, ba, bx, lam],
        in_specs=[tile, tile, halo, tile, tile, halo, cws, vec, mat, mat, vec, vec, vec],
        out_specs=[tile, tile, pl.BlockSpec((8, LANES), lambda cb, t: (0, cb)), mat, mat],
        out_shape=[jax.ShapeDtypeStruct((t_tok, D), BF16), jax.ShapeDtypeStruct((t_tok, D), BF16),
                   jax.ShapeDtypeStruct((8, D), F32), jax.ShapeDtypeStruct((D // LANES, LANES, LANES), F32),
                   jax.ShapeDtypeStruct((D // LANES, LANES, LANES), F32)],
        scratch=[pltpu.VMEM((1, LANES), F32), pltpu.VMEM((1, LANES), F32), pltpu.VMEM((8, LANES), F32),
                 pltpu.VMEM((rows, LANES), F32)], carries=carries)


def _t5_bucket_np(rel):
    nb = N_BUCKETS // 2
    max_exact = nb // 2
    ret = np.where(rel > 0, nb, 0)
    n = np.abs(rel)
    nf = np.maximum(n, 1).astype(np.float32)
    large = max_exact + (np.log(nf / np.float32(max_exact)) / np.float32(math.log(MAX_DISTANCE / max_exact))
                         * np.float32(nb - max_exact)).astype(np.int32)
    large = np.minimum(large, nb - 1)
    return ret + np.where(n < max_exact, n, large)


def _bucket_map():
    r = np.arange(QT)[:, None]
    c = np.arange(KW)[None, :]
    j = c - (r // CHUNK) * CHUNK
    band = (j >= 0) & (j < WINDOW + CHUNK)
    return np.where(band, _t5_bucket_np(c - r - WINDOW), -1).astype(np.int32)


def _attn_specs(nt, reverse):
    def tt(i):
        return nt - 1 - i if reverse else i
    kvw = N_KV * LANES
    qs = pl.BlockSpec((QT, D), lambda i: (tt(i), 0))
    cur = pl.BlockSpec((QT, kvw), lambda i: (tt(i), 0))
    prev = pl.BlockSpec((WINDOW, kvw), lambda i: (jnp.maximum(tt(i) * (QT // WINDOW) - 1, 0), 0))
    lse = pl.BlockSpec((QT, LANES), lambda i: (tt(i), 0))
    return qs, cur, prev, lse


REP = N_HEADS // N_KV
SCALE = HEAD_DIM ** -0.5


def _stack_heads(x_ref, g, lo, scale=None):
    parts = []
    for hl in range(REP):
        xs = x_ref[:, pl.ds((2 * g + hl // 2) * LANES, LANES)]
        xs = jnp.where(lo if hl % 2 == 0 else jnp.logical_not(lo), xs, jnp.zeros_like(xs))
        parts.append(xs if scale is None else xs * jnp.asarray(scale, xs.dtype))
    return jnp.concatenate(parts, axis=0)


def _stack_sinks(sink_ref, g, srow):
    sk = jnp.full(srow.shape, sink_ref[REP * g + REP - 1], F32)
    for hl in range(REP - 2, -1, -1):
        sk = jnp.where(srow < (hl + 1) * QT, sink_ref[REP * g + hl], sk)
    return sk


def _attn_fwd(q, kd, vd, bias, sinks, carries=()):
    t_tok = q.shape[0]
    nt = t_tok // QT

    def body(q_ref, kp_ref, kc_ref, vp_ref, vc_ref, bias_ref, sink_ref, o_ref, lse_ref):
        i = pl.program_id(0)
        col = lax.broadcasted_iota(jnp.int32, (1, KW), 1)
        first = jnp.where((i == 0) & (col < WINDOW), NEG_INF, 0.0)
        lane = lax.broadcasted_iota(jnp.int32, (QT, LANES), 1)
        lo = lane < HEAD_DIM
        srow = lax.broadcasted_iota(jnp.int32, (REP * QT, 1), 0)
        lse_t = jnp.zeros((QT, LANES), F32)
        for g in range(N_KV):
            kwin = jnp.concatenate([kp_ref[:, pl.ds(g * LANES, LANES)], kc_ref[:, pl.ds(g * LANES, LANES)]], axis=0)
            vwin = jnp.concatenate([vp_ref[:, pl.ds(g * LANES, LANES)], vc_ref[:, pl.ds(g * LANES, LANES)]], axis=0)
            qst = _stack_heads(q_ref, g, lo, SCALE)
            s = _nt(qst, kwin) + (bias_ref[g] + first)
            sk = _stack_sinks(sink_ref, g, srow)
            m = jnp.maximum(jnp.max(s, axis=-1, keepdims=True), sk)
            e = jnp.exp(s - m)
            l = jnp.sum(e, axis=-1, keepdims=True) + jnp.exp(sk - m)
            p = e / l
            ost = _nn(p.astype(BF16), vwin)
            lse_s = m + jnp.log(l)
            for sl in range(2):
                o_ref[:, pl.ds((2 * g + sl) * LANES, LANES)] = jnp.where(
                    lo, ost[2 * sl * QT:(2 * sl + 1) * QT], ost[(2 * sl + 1) * QT:(2 * sl + 2) * QT]).astype(BF16)
            for hl in range(REP):
                lse_t = jnp.where(lane == REP * g + hl, lse_s[hl * QT:(hl + 1) * QT], lse_t)
        lse_ref[...] = lse_t

    qs, cur, prev, lse = _attn_specs(nt, False)
    return _call(
        body, name="attn_fwd", grid=(nt,), ins=[q, kd, kd, vd, vd, bias, sinks],
        in_specs=[qs, prev, cur, prev, cur, _const_spec((N_KV, REP * QT, KW)), pl.BlockSpec(memory_space=pltpu.SMEM)],
        out_specs=[qs, lse],
        out_shape=[jax.ShapeDtypeStruct((t_tok, D), BF16), jax.ShapeDtypeStruct((t_tok, LANES), F32)],
        vmem=48 * 2 ** 20, carries=carries)


def _attn_bwd(q, kd, vd, o, do, lse, bias, sinks, carries=()):
    t_tok = q.shape[0]
    nt = t_tok // QT
    kvw = N_KV * LANES

    def body(q_ref, kp_ref, kc_ref, vp_ref, vc_ref, o_ref, do_ref, lse_ref, bias_ref, sink_ref,
             dqkv_ref, ds_ref, dsink_ref, kcar_s, vcar_s):
        i = pl.program_id(0)
        tile = nt - 1 - i

        @pl.when(i == 0)
        def _():
            kcar_s[...] = jnp.zeros_like(kcar_s)
            vcar_s[...] = jnp.zeros_like(vcar_s)
            ds_ref[...] = jnp.zeros_like(ds_ref)
            dsink_ref[...] = jnp.zeros_like(dsink_ref)

        col = lax.broadcasted_iota(jnp.int32, (1, KW), 1)
        first = jnp.where((tile == 0) & (col < WINDOW), NEG_INF, 0.0)
        lane = lax.broadcasted_iota(jnp.int32, (QT, LANES), 1)
        lo = lane < HEAD_DIM
        lane_k = lax.broadcasted_iota(jnp.int32, (KW, LANES), 1)
        lane_1 = lax.broadcasted_iota(jnp.int32, (1, LANES), 1)
        srow = lax.broadcasted_iota(jnp.int32, (REP * QT, 1), 0)
        lse_t = lse_ref[...]
        dsink = jnp.zeros((1, LANES), F32)
        for g in range(N_KV):
            kwin = jnp.concatenate([kp_ref[:, pl.ds(g * LANES, LANES)], kc_ref[:, pl.ds(g * LANES, LANES)]], axis=0)
            vwin = jnp.concatenate([vp_ref[:, pl.ds(g * LANES, LANES)], vc_ref[:, pl.ds(g * LANES, LANES)]], axis=0)
            qst = _stack_heads(q_ref, g, lo, SCALE)
            dost = _stack_heads(do_ref, g, lo)
            od = [do_ref[:, pl.ds((2 * g + sl) * LANES, LANES)].astype(F32)
                  * o_ref[:, pl.ds((2 * g + sl) * LANES, LANES)].astype(F32) for sl in range(2)]
            drow = jnp.concatenate([jnp.sum(jnp.where(lo if hl % 2 == 0 else jnp.logical_not(lo), od[hl // 2], 0.0),
                                            axis=-1, keepdims=True) for hl in range(REP)], axis=0)
            lse_s = jnp.concatenate([jnp.sum(jnp.where(lane == REP * g + hl, lse_t, 0.0), axis=-1, keepdims=True)
                                     for hl in range(REP)], axis=0)
            s = _nt(qst, kwin) + (bias_ref[g] + first)
            p = jnp.exp(s - lse_s)
            ds = p * (_nt(dost, vwin) - drow)
            ds_ref[g] += ds
            tsink = -(jnp.exp(_stack_sinks(sink_ref, g, srow) - lse_s) * drow)
            for hl in range(REP):
                dsink = dsink + jnp.where(lane_1 == REP * g + hl,
                                          jnp.sum(tsink[hl * QT:(hl + 1) * QT], axis=0, keepdims=True), 0.0)
            dsb = ds.astype(BF16)
            dqst = _nn(dsb, kwin) * SCALE
            for sl in range(2):
                dqkv_ref[:, pl.ds((2 * g + sl) * LANES, LANES)] = jnp.where(
                    lo, dqst[2 * sl * QT:(2 * sl + 1) * QT], dqst[(2 * sl + 1) * QT:(2 * sl + 2) * QT]).astype(BF16)
            dk_acc = _tn(dsb, qst)
            dv_acc = _tn(p.astype(BF16), dost)
            dk_f = jnp.where(lane_k < HEAD_DIM, dk_acc + pltpu.roll(dk_acc, HEAD_DIM, 1), 0.0)
            dv_f = jnp.where(lane_k < HEAD_DIM, dv_acc + pltpu.roll(dv_acc, HEAD_DIM, 1), 0.0)
            for acc, col0, car in ((dk_f, K0, kcar_s), (dv_f, V0, vcar_s)):
                cs = pl.ds(g * LANES, LANES)
                co = pl.ds(col0 + g * LANES, LANES)
                if QT > WINDOW:
                    dqkv_ref[pl.ds(0, QT - WINDOW), co] = acc[WINDOW:QT].astype(BF16)
                dqkv_ref[pl.ds(QT - WINDOW, WINDOW), co] = (acc[QT:KW] + car[:, cs]).astype(BF16)
                car[:, cs] = acc[0:WINDOW]
        dsink_ref[...] += dsink

    qs, cur, prev, lse_s = _attn_specs(nt, True)
    return _call(
        body, name="attn_bwd", grid=(nt,), ins=[q, kd, kd, vd, vd, o, do, lse, bias, sinks],
        in_specs=[qs, prev, cur, prev, cur, qs, qs, lse_s, _const_spec((N_KV, REP * QT, KW)),
                  pl.BlockSpec(memory_space=pltpu.SMEM)],
        out_specs=[pl.BlockSpec((QT, XR0), lambda i: (nt - 1 - i, 0)), _const_spec((N_KV, REP * QT, KW)),
                   _const_spec((1, LANES))],
        out_shape=[jax.ShapeDtypeStruct((t_tok, XR0), BF16), jax.ShapeDtypeStruct((N_KV, REP * QT, KW), F32),
                   jax.ShapeDtypeStruct((1, LANES), F32)],
        scratch=[pltpu.VMEM((WINDOW, kvw), F32), pltpu.VMEM((WINDOW, kvw), F32)],
        vmem=VMEM_BIG, carries=carries)


def _bias_tile(table, bmap):
    def body(tab_ref, bm_ref, out_ref):
        bm = bm_ref[...]

        def per_head(hd, carry):
            acc = jnp.full((QT, KW), NEG_INF, F32)
            for b in range(N_BUCKETS):
                acc = jnp.where(bm == b, tab_ref[b, hd], acc)
            out_ref[hd] = acc
            return carry

        lax.fori_loop(0, N_HEADS, per_head, 0)

    return pl.pallas_call(
        body, name="bias_tile", out_shape=jax.ShapeDtypeStruct((N_HEADS, QT, KW), F32),
        in_specs=[pl.BlockSpec(memory_space=pltpu.SMEM), pl.BlockSpec(memory_space=pltpu.VMEM)],
        out_specs=pl.BlockSpec(memory_space=pltpu.VMEM))(table, bmap)


def _bias_grad(ds_acc, bmap):
    def body(ds_ref, bm_ref, out_ref):
        row = lax.broadcasted_iota(jnp.int32, (N_BUCKETS, LANES), 0)
        lane = lax.broadcasted_iota(jnp.int32, (N_BUCKETS, LANES), 1)
        bm = bm_ref[...]

        def per_head(hd, res):
            dsv = ds_ref[hd]
            for b in range(N_BUCKETS):
                val = jnp.sum(jnp.sum(jnp.where(bm == b, dsv, 0.0), axis=0, keepdims=True), axis=1, keepdims=True)
                res = jnp.where((row == b) & (lane == hd), val, res)
            return res

        out_ref[...] = lax.fori_loop(0, N_HEADS, per_head, jnp.zeros((N_BUCKETS, LANES), F32))

    return pl.pallas_call(
        body, name="bias_grad", out_shape=jax.ShapeDtypeStruct((N_BUCKETS, LANES), F32),
        in_specs=[pl.BlockSpec(memory_space=pltpu.VMEM), pl.BlockSpec(memory_space=pltpu.VMEM)],
        out_specs=pl.BlockSpec(memory_space=pltpu.VMEM))(ds_acc, bmap)


MIXOUT = ("w_lru_out", "w_attn_out", "w_o")


def _mix_out_fwd(ya_in, o, gs, h, g, wg, tm=512, carries=()):
    t_tok = h.shape[0]
    nt = t_tok // tm
    groups = _groups_of(MIXOUT)

    def body(ya_ref, o_ref, gs_ref, h_ref, g_ref, wg_ref, hout_ref, yao_ref, ybo_ref, z_ref, wa_s, wb_s, wo_s, sems):
        i = pl.program_id(0)

        @pl.when(i == 0)
        def _():
            _load_weights({groups[0]: wg_ref}, list(zip(MIXOUT, (wa_s, wb_s, wo_s))), sems)

        ya = _nn(ya_ref[...], wa_s[...])
        yb = _nn(o_ref[...], wb_s[...])
        yao_ref[...] = ya.astype(BF16)
        ybo_ref[...] = yb.astype(BF16)
        merged = gs_ref[:, pl.ds(0, D)].astype(F32) * ya + gs_ref[:, pl.ds(D, D)].astype(F32) * yb
        z = _nn(merged.astype(BF16), wo_s[...])
        z_ref[...] = z
        zn, _, _ = _rms_fwd(z, g_ref[...])
        hout_ref[...] = h_ref[...] + zn

    return _call(
        body, name="mix_out_fwd", grid=(nt,), ins=[ya_in, o, gs, h, g, wg[groups[0]]],
        in_specs=[_row_spec(tm, D), _row_spec(tm, D), _row_spec(tm, 2 * D), _row_spec(tm, D), _const_spec((1, D)), ANY],
        out_specs=[_row_spec(tm, D)] * 4,
        out_shape=[jax.ShapeDtypeStruct((t_tok, D), F32), jax.ShapeDtypeStruct((t_tok, D), BF16),
                   jax.ShapeDtypeStruct((t_tok, D), BF16), jax.ShapeDtypeStruct((t_tok, D), F32)],
        scratch=[pltpu.VMEM((D, D), BF16)] * 3 + [DMA((3 * NDEV,))],
        vmem=48 * 2 ** 20, carries=carries)


def _mix_out_bwd(dh, z, ya, yb, gs, g, wg, tm=512, carries=()):
    t_tok = dh.shape[0]
    nt = t_tok // tm
    groups = _groups_of(MIXOUT)

    def body(dh_ref, z_ref, ya_ref, yb_ref, gs_ref, g_ref, wg_ref,
             dyain_ref, do_ref, dgpre_ref, dya_ref, dyb_ref, mg_ref, dz_ref, dg_ref, dbg_ref,
             wa_s, wb_s, wo_s, sems):
        i = pl.program_id(0)

        @pl.when(i == 0)
        def _():
            _load_weights({groups[0]: wg_ref}, list(zip(MIXOUT, (wa_s, wb_s, wo_s))), sems)
            dg_ref[...] = jnp.zeros_like(dg_ref)
            dbg_ref[...] = jnp.zeros_like(dbg_ref)

        gv = g_ref[...]
        _, zh, r = _rms_fwd(z_ref[...], gv)
        dz, dg = _rms_bwd(dh_ref[...], zh, r, gv)
        dg_ref[...] += dg
        dzb = dz.astype(BF16)
        dz_ref[...] = dzb
        ga, gb = gs_ref[:, pl.ds(0, D)].astype(F32), gs_ref[:, pl.ds(D, D)].astype(F32)
        ya_v, yb_v = ya_ref[...].astype(F32), yb_ref[...].astype(F32)
        mg_ref[...] = (ga * ya_v + gb * yb_v).astype(BF16)
        dm = _nt(dzb, wo_s[...])
        dga = dm * ya_v * ga * (1.0 - ga)
        dgb = dm * yb_v * gb * (1.0 - gb)
        dgpre_ref[:, pl.ds(0, D)] = dga.astype(BF16)
        dgpre_ref[:, pl.ds(D, D)] = dgb.astype(BF16)
        dbg_ref[:, pl.ds(0, D)] += jnp.sum(dga, axis=0, keepdims=True)
        dbg_ref[:, pl.ds(D, D)] += jnp.sum(dgb, axis=0, keepdims=True)
        dya = (dm * ga).astype(BF16)
        dyb = (dm * gb).astype(BF16)
        dya_ref[...] = dya
        dyb_ref[...] = dyb
        dyain_ref[...] = _nt(dya, wa_s[...])
        do_ref[...] = _nt(dyb, wb_s[...]).astype(BF16)

    bf = jax.ShapeDtypeStruct((t_tok, D), BF16)
    return _call(
        body, name="mix_out_bwd", grid=(nt,), ins=[dh, z, ya, yb, gs, g, wg[groups[0]]],
        in_specs=[_row_spec(tm, D)] * 4 + [_row_spec(tm, 2 * D), _const_spec((1, D)), ANY],
        out_specs=[_row_spec(tm, D), _row_spec(tm, D), _row_spec(tm, 2 * D)] + [_row_spec(tm, D)] * 4
        + [_const_spec((1, D)), _const_spec((1, 2 * D))],
        out_shape=[jax.ShapeDtypeStruct((t_tok, D), F32), bf, jax.ShapeDtypeStruct((t_tok, 2 * D), BF16), bf, bf, bf, bf,
                   jax.ShapeDtypeStruct((1, D), F32), jax.ShapeDtypeStruct((1, 2 * D), F32)],
        scratch=[pltpu.VMEM((D, D), BF16)] * 3 + [DMA((3 * NDEV,))],
        vmem=VMEM_BIG, carries=carries)


def _sum_parts(parts_list):
    n = len(parts_list)
    _, r, c = parts_list[0].shape
    tc = 256

    def body(*refs):
        for p_ref, o_ref in zip(refs[:n], refs[n:]):
            acc = p_ref[0].astype(F32)
            for s in range(1, NDEV):
                acc = acc + p_ref[s].astype(F32)
            o_ref[...] = acc

    return pl.pallas_call(
        body, name=f"sum_parts_{r}", grid=(c // tc,),
        in_specs=[pl.BlockSpec((NDEV, r, tc), lambda i: (0, 0, i))] * n,
        out_specs=[pl.BlockSpec((r, tc), lambda i: (0, i))] * n,
        out_shape=[jax.ShapeDtypeStruct((r, c), F32)] * n,
        compiler_params=pltpu.CompilerParams(dimension_semantics=("arbitrary",), vmem_limit_bytes=48 * 2 ** 20),
    )(*parts_list)


def _adamw_math(w, g, m, v):
    m = ADAM_B1 * m + (1.0 - ADAM_B1) * g
    v = ADAM_B2 * v + (1.0 - ADAM_B2) * (g * g)
    m_hat = m / (1.0 - ADAM_B1 ** ADAM_STEP)
    v_hat = v / (1.0 - ADAM_B2 ** ADAM_STEP)
    delta = -ADAM_LR * (m_hat / (jnp.sqrt(v_hat) + ADAM_EPS) + ADAM_WD * w)
    return delta, m, v


def _sum_ready(parts_list, name, carries=(), chip_sums=()):
    n = len(parts_list)
    c = parts_list[0].shape[2]
    tc = 2 * LANES

    def body(*refs):
        for k, (p_ref, o_ref) in enumerate(zip(refs[:n], refs[n:])):
            if k in chip_sums:
                core = lax.axis_index("c")
                slots = [2 * ch + core for ch in range(NDEV // 2)]
            else:
                slots = list(range(NDEV))
            g = p_ref[slots[0]].astype(F32)
            for s in slots[1:]:
                g = g + p_ref[s].astype(F32)
            o_ref[...] = g

    return _call(
        body, name=name, grid=(c // tc,), ins=list(parts_list),
        in_specs=[pl.BlockSpec((NDEV, p.shape[1], tc), lambda i: (0, 0, i)) for p in parts_list],
        out_shape=[jax.ShapeDtypeStruct((p.shape[1], c), F32) for p in parts_list],
        out_specs=[pl.BlockSpec((p.shape[1], tc), lambda i: (0, i)) for p in parts_list],
        vmem=48 * 2 ** 20, carries=carries)


def _adamw_cols(items, name):
    n = len(items)
    c = items[0][1].shape[1]
    tc = LANES

    def body(*refs):
        for k in range(n):
            g_ref, w_ref, m_ref, v_ref = refs[4 * k:4 * k + 4]
            go_ref, d_ref, nm_ref, nv_ref = refs[4 * n + 4 * k:4 * n + 4 * k + 4]
            g = g_ref[...]
            d, m, v = _adamw_math(w_ref[...], g, m_ref[...], v_ref[...])
            go_ref[...] = g
            d_ref[...] = d
            nm_ref[...] = m
            nv_ref[...] = v

    ins, specs, out_shape = [], [], []
    for g, w, m, v in items:
        r = w.shape[0]
        ins += [g, w, m, v]
        specs += [pl.BlockSpec((r, tc), lambda i: (0, i))] * 4
        out_shape += [jax.ShapeDtypeStruct((r, c), F32)] * 4
    outs, _ = _call(body, name=name, grid=(c // tc,), ins=ins, in_specs=specs, out_shape=out_shape,
                    out_specs=specs, vmem=48 * 2 ** 20)
    return [tuple(outs[4 * k:4 * k + 4]) for k in range(n)]


def _adamw_body(n):
    def body(*refs):
        for k in range(n):
            g_ref, w_ref, m_ref, v_ref = refs[4 * k:4 * k + 4]
            d_ref, nm_ref, nv_ref = refs[4 * n + 3 * k:4 * n + 3 * k + 3]
            d, m, v = _adamw_math(w_ref[...], g_ref[...], m_ref[...], v_ref[...])
            d_ref[...] = d
            nm_ref[...] = m
            nv_ref[...] = v
    return body


def _adamw(items):
    n = len(items)
    r, c = items[0][0].shape
    tr = r if r * c <= 2 ** 18 else max(t for t in range(8, 65, 8) if r % t == 0)
    spec = pl.BlockSpec((tr, c), lambda i: (i, 0))
    outs = pl.pallas_call(
        _adamw_body(n), name=f"adamw_{r}x{c}", grid=(r // tr,),
        in_specs=[spec] * (4 * n), out_specs=[spec] * (3 * n),
        out_shape=[jax.ShapeDtypeStruct((r, c), F32)] * (3 * n),
        compiler_params=pltpu.CompilerParams(dimension_semantics=("arbitrary",), vmem_limit_bytes=40 * 2 ** 20),
    )(*[a for it in items for a in it])
    return [tuple(outs[3 * k:3 * k + 3]) for k in range(n)]


def _adamw_small(items):
    n = len(items)
    vm = pl.BlockSpec(memory_space=pltpu.VMEM)
    outs = pl.pallas_call(
        _adamw_body(n), name="adamw_small", in_specs=[vm] * (4 * n), out_specs=[vm] * (3 * n),
        out_shape=[jax.ShapeDtypeStruct(it[1].shape, F32) for it in items for _ in range(3)],
    )(*[a for it in items for a in it])
    return [tuple(outs[3 * k:3 * k + 3]) for k in range(n)]


def _pack_small(arrs):
    rows, offs = [], []
    total = 0
    for a in arrs:
        flat = a.reshape(-1).astype(F32)
        nr = -(-flat.shape[0] // LANES)
        flat = jnp.pad(flat, (0, nr * LANES - flat.shape[0]))
        rows.append(flat.reshape(nr, LANES))
        offs.append((total, nr))
        total += nr
    pad = -total % 8
    if pad:
        rows.append(jnp.zeros((pad, LANES), F32))
    return jnp.concatenate(rows, axis=0), offs


def _unpack_small(pack, offs, shapes):
    out = []
    for (o, nr), shp in zip(offs, shapes):
        size = int(np.prod(shp))
        out.append(pack[o:o + nr].reshape(-1)[:size].reshape(shp))
    return out


def _sum_small(gathered, rows):
    def body(p_ref, o_ref):
        acc = p_ref[pl.ds(0, rows), :]
        for s in range(1, NDEV):
            acc = acc + p_ref[pl.ds(s * rows, rows), :]
        o_ref[...] = acc

    return pl.pallas_call(
        body, name="sum_small", out_shape=jax.ShapeDtypeStruct((rows, LANES), F32),
        in_specs=[pl.BlockSpec(memory_space=pltpu.VMEM)], out_specs=pl.BlockSpec(memory_space=pltpu.VMEM))(gathered)


def _block_diag(w):
    w = w.reshape(D // LANES, 2, LRU_BLOCK, LRU_BLOCK)
    z = jnp.zeros((D // LANES, LRU_BLOCK, LRU_BLOCK), w.dtype)
    top = jnp.concatenate([w[:, 0], z], axis=2)
    bot = jnp.concatenate([z, w[:, 1]], axis=2)
    return jnp.concatenate([top, bot], axis=1)


def _block_diag_grad(dw):
    a = dw[:, :LRU_BLOCK, :LRU_BLOCK]
    b = dw[:, LRU_BLOCK:, LRU_BLOCK:]
    return jnp.stack([a, b], axis=1).reshape(D // LRU_BLOCK, LRU_BLOCK, LRU_BLOCK)


def kernel(x, ffn1_pre_g, ffn1_w1, ffn1_w3, ffn1_w2, ffn1_post_g, mix_pre_g, w_in, conv_w, conv_b, rg_a_w, rg_a_b, rg_x_w, rg_x_b, lru_lambda, w_lru_out, attn_sinks, rel_bias, w_attn_out, w_gate, b_gate, w_o, mix_post_g, ffn2_pre_g, ffn2_w1, ffn2_w3, ffn2_w2, ffn2_post_g, loss_target, m_ffn1_pre_g, m_ffn1_w1, m_ffn1_w3, m_ffn1_w2, m_ffn1_post_g, m_mix_pre_g, m_w_in, m_conv_w, m_conv_b, m_rg_a_w, m_rg_a_b, m_rg_x_w, m_rg_x_b, m_lru_lambda, m_w_lru_out, m_attn_sinks, m_rel_bias, m_w_attn_out, m_w_gate, m_b_gate, m_w_o, m_mix_post_g, m_ffn2_pre_g, m_ffn2_w1, m_ffn2_w3, m_ffn2_w2, m_ffn2_post_g, v_ffn1_pre_g, v_ffn1_w1, v_ffn1_w3, v_ffn1_w2, v_ffn1_post_g, v_mix_pre_g, v_w_in, v_conv_w, v_conv_b, v_rg_a_w, v_rg_a_b, v_rg_x_w, v_rg_x_b, v_lru_lambda, v_w_lru_out, v_attn_sinks, v_rel_bias, v_w_attn_out, v_w_gate, v_b_gate, v_w_o, v_mix_post_g, v_ffn2_pre_g, v_ffn2_w1, v_ffn2_w3, v_ffn2_w2, v_ffn2_post_g):
    names = ["ffn1_pre_g", "ffn1_w1", "ffn1_w3", "ffn1_w2", "ffn1_post_g", "mix_pre_g", "w_in", "conv_w", "conv_b",
             "rg_a_w", "rg_a_b", "rg_x_w", "rg_x_b", "lru_lambda", "w_lru_out", "attn_sinks", "rel_bias", "w_attn_out",
             "w_gate", "b_gate", "w_o", "mix_post_g", "ffn2_pre_g", "ffn2_w1", "ffn2_w3", "ffn2_w2", "ffn2_post_g"]
    ws = dict(zip(names, (ffn1_pre_g, ffn1_w1, ffn1_w3, ffn1_w2, ffn1_post_g, mix_pre_g, w_in, conv_w, conv_b, rg_a_w,
                          rg_a_b, rg_x_w, rg_x_b, lru_lambda, w_lru_out, attn_sinks, rel_bias, w_attn_out, w_gate,
                          b_gate, w_o, mix_post_g, ffn2_pre_g, ffn2_w1, ffn2_w3, ffn2_w2, ffn2_post_g)))
    ms = dict(zip(names, (m_ffn1_pre_g, m_ffn1_w1, m_ffn1_w3, m_ffn1_w2, m_ffn1_post_g, m_mix_pre_g, m_w_in, m_conv_w,
                          m_conv_b, m_rg_a_w, m_rg_a_b, m_rg_x_w, m_rg_x_b, m_lru_lambda, m_w_lru_out, m_attn_sinks,
                          m_rel_bias, m_w_attn_out, m_w_gate, m_b_gate, m_w_o, m_mix_post_g, m_ffn2_pre_g, m_ffn2_w1,
                          m_ffn2_w3, m_ffn2_w2, m_ffn2_post_g)))
    vs = dict(zip(names, (v_ffn1_pre_g, v_ffn1_w1, v_ffn1_w3, v_ffn1_w2, v_ffn1_post_g, v_mix_pre_g, v_w_in, v_conv_w,
                          v_conv_b, v_rg_a_w, v_rg_a_b, v_rg_x_w, v_rg_x_b, v_lru_lambda, v_w_lru_out, v_attn_sinks,
                          v_rel_bias, v_w_attn_out, v_w_gate, v_b_gate, v_w_o, v_mix_post_g, v_ffn2_pre_g, v_ffn2_w1,
                          v_ffn2_w3, v_ffn2_w2, v_ffn2_post_g)))
    me = 4 * lax.axis_index("x") + 2 * lax.axis_index("y") + lax.axis_index("c")
    vec = lambda n: ws[n].reshape(1, -1)

    def shard2d(name):
        if name == "conv":
            row = lax.bitcast_convert_type(conv_w.reshape(CONV_WIDTH, LANES), BF16).reshape(1, D)
            return jnp.concatenate([row, jnp.zeros((LOC["conv"][2] - 1, D), BF16)], axis=0)
        a = ws[name].reshape(ws[name].shape[-2], ws[name].shape[-1])
        return (a.T if name in COL_SHARDED else a).astype(BF16)

    packs = {g: jnp.concatenate([shard2d(n) for n, _ in members], axis=0) for g, members in GROUPS}

    wg = {}
    _, ((wg["ffn1"],),) = _call(None, name="allgather_ffn1", grid=(), ins=[], in_specs=[], out_shape=[], out_specs=[],
                                carries=[_AllGatherTree(packs["ffn1"])])
    conv_rows = wg["ffn1"].reshape(NDEV, GROUP_ROWS["ffn1"], D)[:, LOC["conv"][1]]
    cw = jnp.transpose(lax.bitcast_convert_type(conv_rows.reshape(NDEV, CONV_WIDTH, LANES, 2), F32),
                       (1, 0, 2)).reshape(CONV_WIDTH, D)
    bmap = jnp.asarray(_bucket_map())
    bias = _bias_tile(rel_bias, bmap).reshape(N_KV, REP * QT, KW)
    sinks = attn_sinks.reshape(N_HEADS)
    wa_bd = _block_diag(rg_a_w.reshape(D // LRU_BLOCK, LRU_BLOCK, LRU_BLOCK)).astype(BF16)
    wx_bd = _block_diag(rg_x_w.reshape(D // LRU_BLOCK, LRU_BLOCK, LRU_BLOCK)).astype(BF16)
    lru_args = (cw, vec("conv_b"), wa_bd, wx_bd, vec("rg_a_b"), vec("rg_x_b"), vec("lru_lambda"))
    x2, tgt = x[0], loss_target[0]

    (h1, a1, b1, f1, nb1), ((wg["mixin"],),) = _ffn_fwd(
        x2, vec("ffn1_pre_g"), vec("ffn1_post_g"), wg, ("ffn1_w1", "ffn1_w3", "ffn1_w2"),
        carries=[_AllGather(packs["mixin"])])
    (q, kd, vd, xr, xg, gs, ub), ((wg["mixout"],),) = _mix_proj_fwd(
        h1, vec("mix_pre_g"), vec("b_gate"), wg, carries=[_AllGather(packs["mixout"])])
    (ya_in, hseq), ((wg["ffn2a"],),) = _lru_fwd(xr, xg, *lru_args, carries=[_AllGather(packs["ffn2a"])])
    (o, lse), ((wg["ffn2b"],),) = _attn_fwd(q, kd, vd, bias, sinks, carries=[_AllGather(packs["ffn2b"])])
    (h2, ya, yb, z), _ = _mix_out_fwd(ya_in, o, gs, h1, vec("mix_post_g"), wg)
    (_, a2, b2, f2, nb2, dy, loss_part), _ = _ffn_fwd(h2, vec("ffn2_pre_g"), vec("ffn2_post_g"), wg,
                                                     ("ffn2_w1", "ffn2_w3", "ffn2_w2"), target=tgt)

    gsm, parts = {}, {}
    rs = lambda *grads_: [_ReduceScatterSend(list(grads_))]
    ffr = FF // NDEV
    (dab, s_act, dfb, gsm["ffn2_post_g"]), _ = _ffn_bwd_a(dy, f2, a2, b2, vec("ffn2_post_g"), wg, "ffn2_w2")
    g_w2, _ = _dw(s_act, dfb, FF // 2, "dw_ffn2_w2")
    g_w13, ((parts["ffn2_w2"],),) = _dw(dab, nb2, D // 2, "dw_ffn2_w13", carries=rs(g_w2))
    (dh2, gsm["ffn2_pre_g"]), ((parts["ffn2_w1"],),) = _ffn_bwd_dx(
        dab, h2, dy, vec("ffn2_pre_g"), wg, "ffn2_w1", "ffn2_w3", carries=rs((g_w13, 0, ffr, ffr)))
    (dya_in, do, dgpre, dya, dyb, mg, dzb, gsm["mix_post_g"], gsm["b_gate"]), ((parts["ffn2_w3"],),) = _mix_out_bwd(
        dh2, z, ya, yb, gs, vec("mix_post_g"), wg, carries=rs((g_w13, FF, ffr, ffr)))
    g_wa, _ = _dw(ya_in, dya, D // 2, "dw_w_lru_out")
    g_wb, _ = _dw(o, dyb, D // 2, "dw_w_attn_out")
    g_wo, _ = _dw(mg, dzb, D // 2, "dw_w_o")
    g_wgate, _ = _dw(dgpre, ub, D // 2, "dw_w_gate")
    (dqkv, ds_acc, dsink), (mixout_parts,) = _attn_bwd(
        q, kd, vd, o, do, lse, bias, sinks, carries=rs(g_wa, g_wb, g_wo, g_wgate))
    parts["w_lru_out"], parts["w_attn_out"], parts["w_o"], parts["w_gate"] = mixout_parts
    (dxr, dxg, dvec, dwa, dwx), _ = _lru_bwd(dya_in, xr, xg, hseq, *lru_args)
    g_qkv, _ = _dw(dqkv, ub, D // 2, "dw_w_in_qkv")
    g_xr, _ = _dw(dxr, ub, D // 2, "dw_w_in_xr")
    g_xg, _ = _dw(dxg, ub, D // 2, "dw_w_in_xg")
    g_win = jnp.concatenate(
        [g_qkv[:D]] + [g_qkv[c0 + gi * LANES:c0 + gi * LANES + HEAD_DIM] for c0 in (K0, V0) for gi in range(N_KV)]
        + [g_xr, g_xg], axis=0)
    wir, wir_a = IN_W // NDEV, 336
    (dh1, gsm["mix_pre_g"]), ((win_a,),) = _mix_proj_bwd(
        dqkv, dxr, dxg, dgpre, h1, dh2, vec("mix_pre_g"), wg, carries=rs((g_win, 0, wir, wir_a)))
    gsm["conv_w"] = dvec[0:CONV_WIDTH]
    gsm["conv_b"], gsm["rg_a_b"], gsm["rg_x_b"], gsm["lru_lambda"] = dvec[4], dvec[5], dvec[6], dvec[7]
    gsm["rg_a_w"] = _block_diag_grad(dwa)
    gsm["rg_x_w"] = _block_diag_grad(dwx)
    gsm["attn_sinks"] = dsink[0, :N_HEADS]
    gsm["rel_bias"] = _bias_grad(ds_acc.reshape(N_HEADS, QT, KW), bmap)[:, :N_HEADS]
    late = ("ffn1_post_g", "ffn1_pre_g")
    early = tuple(n for n in SMALL if n not in late)
    early_pack, early_offs = _pack_small([gsm[n] for n in early])
    (dab, s_act, dfb, gsm["ffn1_post_g"]), ((win_b,), (early_all,)) = _ffn_bwd_a(
        dh1, f1, a1, b1, vec("ffn1_post_g"), wg, "ffn1_w2",
        carries=rs((g_win, wir_a, wir, wir - wir_a)) + [_AllGather(early_pack)])
    parts["w_in"] = jnp.concatenate([win_a, win_b], axis=1)
    g_w2, _ = _dw(s_act, dfb, FF // 2, "dw_ffn1_w2")
    g_w13, ((parts["ffn1_w2"],),) = _dw(dab, nb1, D // 2, "dw_ffn1_w13", carries=rs(g_w2))
    chip_w13 = _pair_sum(g_w13, [(0, ffr), (FF, ffr)])
    (grad_x, gsm["ffn1_pre_g"]), ((parts["ffn1_w1"], parts["ffn1_w3"]),) = _ffn_bwd_dx(
        dab, x2, dh1, vec("ffn1_pre_g"), wg, "ffn1_w1", "ffn1_w3",
        carries=[_ReduceScatterSend([(chip_w13, 0, ffr, ffr), (chip_w13, FF, ffr, ffr)], chip_sums=True)])
    late_pack, late_offs = _pack_small([gsm[n] for n in late] + [loss_part])

    grads, delta, new_m, new_v = {}, {}, {}, {}
    t_form = [n for n in COL_SHARDED if ws[n].shape[-1] % LANES]
    view = {n: (lambda a: a.reshape(a.shape[-2], a.shape[-1]).T) if n in t_form
            else (lambda a: a.reshape(a.shape[-2], a.shape[-1])) for n in BIG}
    unview = {n: (lambda a: a.T) if n in t_form else (lambda a: a) for n in BIG}

    group = [n for n in BIG if n != "w_gate"]
    sums, ((late_all,),) = _sum_ready([parts[n] for n in group], "sum_ready", [_AllGather(late_pack)],
                                      chip_sums=(group.index("ffn1_w1"), group.index("ffn1_w3")))
    res = _adamw_cols([(g, view[n](ws[n]), view[n](ms[n]), view[n](vs[n])) for n, g in zip(group, sums)], "adamw_big")
    for n, quad in zip(group, res):
        grads[n], delta[n], new_m[n], new_v[n] = (unview[n](a) for a in quad)
    (g_gate_t,) = _sum_parts([parts["w_gate"]])
    grads["w_gate"] = g_gate_t.T
    ((delta["w_gate"], new_m["w_gate"], new_v["w_gate"]),) = _adamw(
        [(grads["w_gate"], view["w_gate"](ws["w_gate"]), view["w_gate"](ms["w_gate"]), view["w_gate"](vs["w_gate"]))])
    for group, gathered, offs in ((early, early_all, early_offs), (late, late_all, late_offs)):
        total = _sum_small(gathered, gathered.shape[0] // NDEV)
        shapes = [(CONV_WIDTH, D) if n == "conv_w" else ws[n].shape for n in group]
        if group is late:
            shapes = shapes + [(1, LANES)]
        unpacked = _unpack_small(total, offs, shapes)
        if group is late:
            loss = unpacked.pop()[0, 0]
        for n, g in zip(group, unpacked):
            grads[n] = g
    grads["conv_w"] = lax.dynamic_slice(grads["conv_w"], (0, me * LANES), (CONV_WIDTH, LANES)).reshape(conv_w.shape)

    flat2d = lambda a: a.reshape(-1, a.shape[-1])
    res = _adamw_small([(flat2d(grads[n].reshape(ws[n].shape)), flat2d(ws[n]), flat2d(ms[n]), flat2d(vs[n]))
                        for n in SMALL])
    for n, (d_, m_, v_) in zip(SMALL, res):
        delta[n], new_m[n], new_v[n] = d_, m_, v_

    outs = [loss, grad_x.reshape(x.shape)]
    for src in (grads, delta, new_m, new_v):
        outs += [src[n].reshape(ws[n].shape) for n in names]
    return tuple(outs)
```

```python
import functools
import math
import operator

import numpy as np
import jax
import jax.numpy as jnp
from jax import lax
from jax.experimental import pallas as pl
from jax.experimental.pallas import tpu as pltpu

F32, BF16 = jnp.float32, jnp.bfloat16

NDEV = 8
D = 1024
FF = 2816
N_HEADS, N_KV, HEAD_DIM = 16, 4, 64
CHUNK, WINDOW = 64, 128
N_BUCKETS, MAX_DISTANCE = 32, 128
LRU_BLOCK = 64
CONV_WIDTH = 4
LRU_C = 8.0
RMS_EPS = 1e-6
NEG_INF = -1e30
LANES = 128
QT = 128
KW = QT + WINDOW
LRU_ROWS = 1024
IN_W = D + 2 * N_KV * HEAD_DIM + 2 * D
INP_W = D + 2 * N_KV * LANES + 2 * D
VMEM_BIG = 58 * 2 ** 20

ADAM_LR, ADAM_B1, ADAM_B2, ADAM_EPS, ADAM_WD, ADAM_STEP = 0.001, 0.9, 0.999, 1e-08, 0.01, 10

GROUPS = (("ffn1", (("ffn1_w1", FF // NDEV), ("ffn1_w3", FF // NDEV), ("ffn1_w2", FF // NDEV), ("conv", 16))),
          ("mixin", (("w_in", IN_W // NDEV), ("w_gate", 2 * D // NDEV))),
          ("mixout", (("w_lru_out", D // NDEV), ("w_attn_out", D // NDEV), ("w_o", D // NDEV))),
          ("ffn2a", (("ffn2_w1", FF // NDEV), ("ffn2_w3", FF // NDEV))),
          ("ffn2b", (("ffn2_w2", FF // NDEV),)))
LOC, GROUP_ROWS = {}, {}
for _g, _members in GROUPS:
    _o = 0
    for _n, _r in _members:
        LOC[_n] = (_g, _o, _r)
        _o += _r
    GROUP_ROWS[_g] = _o
BIG = tuple(n for _, members in GROUPS for n, _ in members if n != "conv")
COL_SHARDED = ("ffn1_w1", "ffn1_w3", "w_in", "w_gate", "ffn2_w1", "ffn2_w3")

SMALL = ("ffn1_pre_g", "ffn1_post_g", "mix_pre_g", "conv_w", "conv_b", "rg_a_w", "rg_a_b", "rg_x_w", "rg_x_b",
         "lru_lambda", "attn_sinks", "rel_bias", "b_gate", "mix_post_g", "ffn2_pre_g", "ffn2_post_g")

MESH = pl.DeviceIdType.MESH
ANY = pl.BlockSpec(memory_space=pl.ANY)
DMA = pltpu.SemaphoreType.DMA


def _nn(a, b):
    return lax.dot_general(a, b, (((1,), (0,)), ((), ())), preferred_element_type=F32)


def _nt(a, b):
    return lax.dot_general(a, b, (((1,), (1,)), ((), ())), preferred_element_type=F32)


def _tn(a, b):
    return lax.dot_general(a, b, (((0,), (0,)), ((), ())), preferred_element_type=F32)


def _rms_fwd(x, g):
    r = lax.rsqrt(jnp.mean(x * x, axis=-1, keepdims=True) + RMS_EPS)
    xh = x * r
    return xh * g, xh, r


def _rms_bwd(dn, xh, r, g):
    dxh = dn * g
    dx = r * (dxh - xh * jnp.mean(dxh * xh, axis=-1, keepdims=True))
    return dx, jnp.sum(dn * xh, axis=0, keepdims=True)


def _row_spec(tm, c):
    return pl.BlockSpec((tm, c), lambda i: (i, 0))


def _const_spec(shape):
    nd = len(shape)
    return pl.BlockSpec(shape, lambda i: (0,) * nd)


class _AllGather:
    waits_for_all = True

    def __init__(self, shard):
        self.m, n = shard.shape
        self.ins = [shard]
        self.out_shape = [jax.ShapeDtypeStruct((NDEV * self.m, n), shard.dtype)]
        self.scratch = [DMA((7,)), DMA((7,)), DMA]

    def _copies(self, ins, outs, scr, all_of_them):
        x_ref, out_ref = ins[0], outs[0]
        send_sems, recv_sems, local_sem = scr
        x, y, c = lax.axis_index("x"), lax.axis_index("y"), lax.axis_index("c")
        me, sibling = (x, y, c), (x, y, 1 - c)
        chips = [(1 - x, y), (x, 1 - y), (1 - x, 1 - y)]
        m = self.m

        def rows(px, py, pc):
            return out_ref.at[pl.ds((4 * px + 2 * py + pc) * m, m), :]

        def copy(k, block, to, src=None):
            return pltpu.make_async_remote_copy(
                src_ref=rows(*block) if src is None else src, dst_ref=rows(*block),
                send_sem=send_sems.at[k], recv_sem=recv_sems.at[k], device_id=to, device_id_type=MESH)

        mine = pltpu.make_async_copy(x_ref, rows(*me), local_sem)
        first = [copy(0, me, sibling, src=x_ref)] + [copy(1 + j, me, (*chip, c), src=x_ref)
                                                     for j, chip in enumerate(chips)]
        if not all_of_them:
            return mine, first
        passed = [copy(4 + j, (*chip, c), sibling) for j, chip in enumerate(chips)]
        landed = [copy(1 + j, (*chip, c), me) for j, chip in enumerate(chips)]
        from_sibling = [copy(0, sibling, me)] + [copy(4 + j, (*chip, 1 - c), me) for j, chip in enumerate(chips)]
        return mine, first, passed, landed, from_sibling

    def start(self, ins, outs, scr):
        mine, first = self._copies(ins, outs, scr, False)
        mine.start()
        for cp in first:
            cp.start()

    def pass_on(self, ins, outs, scr):
        _, _, passed, landed, _ = self._copies(ins, outs, scr, True)
        for cp_in, cp_on in zip(landed, passed):
            cp_in.wait_recv()
            cp_on.start()

    def finish(self, ins, outs, scr):
        mine, first, passed, landed, from_sibling = self._copies(ins, outs, scr, True)
        for cp in from_sibling:
            cp.wait_recv()
        for cp in first + passed:
            cp.wait_send()
        mine.wait()


class _AllGatherTree:
    waits_for_all = True

    def __init__(self, shard):
        self.m, n = shard.shape
        self.half = self.m // 32 * 16
        self.ins = [shard]
        self.out_shape = [jax.ShapeDtypeStruct((NDEV * self.m, n), shard.dtype)]
        self.scratch = [DMA((9,)), DMA((9,)), DMA]

    def start(self, ins, outs, scr):
        x_ref, out_ref = ins[0], outs[0]
        send_sems, recv_sems, local_sem = scr
        x, y, c = lax.axis_index("x"), lax.axis_index("y"), lax.axis_index("c")
        m, half = self.m, self.half
        me, sibling, xn, yn, diag = (x, y, c), (x, y, 1 - c), (1 - x, y, c), (x, 1 - y, c), (1 - x, 1 - y, c)
        other = lambda dev: (dev[0], dev[1], 1 - c)

        def rows(dev, lo=0, n=m):
            return out_ref.at[pl.ds((4 * dev[0] + 2 * dev[1] + dev[2]) * m + lo, n), :]

        def copy(k, dev, to, lo=0, n=m, src=None):
            return pltpu.make_async_remote_copy(
                src_ref=rows(dev, lo, n) if src is None else src, dst_ref=rows(dev, lo, n),
                send_sem=send_sems.at[k], recv_sem=recv_sems.at[k], device_id=to, device_id_type=MESH)

        mine = pltpu.make_async_copy(x_ref, rows(me), local_sem)
        own = [copy(0, me, sibling, src=x_ref), copy(1, me, xn, src=x_ref), copy(2, me, yn, src=x_ref)]
        mine.start()
        for cp in own:
            cp.start()
        steps = [(copy(1, xn, me), [copy(3, xn, yn, 0, half), copy(5, xn, sibling)]),
                 (copy(2, yn, me), [copy(4, yn, xn, half, m - half), copy(6, yn, sibling)]),
                 (copy(3, diag, me, 0, half), [copy(7, diag, sibling, 0, half)]),
                 (copy(4, diag, me, half, m - half), [copy(8, diag, sibling, half, m - half)])]
        from_sibling = [copy(0, sibling, me), copy(5, other(xn), me), copy(6, other(yn), me),
                        copy(7, other(diag), me, 0, half), copy(8, other(diag), me, half, m - half)]
        passed = []
        for landing, onward in steps:
            landing.wait_recv()
            for cp in onward:
                cp.start()
            passed += onward
        for cp in from_sibling:
            cp.wait_recv()
        for cp in own + passed:
            cp.wait_send()
        mine.wait()

    def finish(self, ins, outs, scr):
        pass


class _ReduceScatterSend:
    def __init__(self, grads, chip_sums=False):
        grads = [g if isinstance(g, tuple) else (g, 0, g.shape[0] // NDEV, g.shape[0] // NDEV) for g in grads]
        self.peers = (2, 4, 6) if chip_sums else tuple(range(1, NDEV))
        self.waits_for_all = not chip_sums
        self.nw = len(grads)
        self.base = [b for _, b, _, _ in grads]
        self.stride = [s for _, _, s, _ in grads]
        self.rows = [r for _, _, _, r in grads]
        self.ins = [g for g, _, _, _ in grads]
        self.out_shape = [jax.ShapeDtypeStruct((NDEV, r, g.shape[1]), g.dtype) for g, _, _, r in grads]
        self.scratch = [DMA((self.nw, NDEV - 1)), DMA((self.nw, NDEV - 1)), DMA((self.nw,))]

    def _copies(self, g_refs, r_refs, scr, want):
        send_sems, recv_sems, local_sems = scr
        x, y, c = lax.axis_index("x"), lax.axis_index("y"), lax.axis_index("c")
        me = 4 * x + 2 * y + c
        rows, base, stride = self.rows, self.base, self.stride
        out = []
        if want == "local":
            for w in range(self.nw):
                out.append(pltpu.make_async_copy(g_refs[w].at[pl.ds(base[w] + me * stride[w], rows[w])],
                                                 r_refs[w].at[me], local_sems.at[w]))
            return out
        for k in self.peers:
            px, py, pc = x ^ (k >> 2), y ^ ((k >> 1) & 1), c ^ (k & 1)
            peer = 4 * px + 2 * py + pc
            for w in range(self.nw):
                sems = dict(send_sem=send_sems.at[w, k - 1], recv_sem=recv_sems.at[w, k - 1],
                            device_id=(px, py, pc), device_id_type=MESH)
                if want == "send":
                    out.append(pltpu.make_async_remote_copy(
                        src_ref=g_refs[w].at[pl.ds(base[w] + peer * stride[w], rows[w])], dst_ref=r_refs[w].at[me],
                        **sems))
                else:
                    out.append(pltpu.make_async_remote_copy(
                        src_ref=g_refs[w].at[pl.ds(0, rows[w])], dst_ref=r_refs[w].at[peer], **sems))
        return out

    def start(self, ins, outs, scr):
        for cp in self._copies(ins, outs, scr, "local") + self._copies(ins, outs, scr, "send"):
            cp.start()

    def finish(self, ins, outs, scr):
        for cp in self._copies(ins, outs, scr, "recv"):
            cp.wait_recv()
        for cp in self._copies(ins, outs, scr, "send"):
            cp.wait_send()
        for cp in self._copies(ins, outs, scr, "local"):
            cp.wait()


def _mesh_barrier():
    barrier = pltpu.get_barrier_semaphore()
    x, y, c = lax.axis_index("x"), lax.axis_index("y"), lax.axis_index("c")
    for k in range(1, NDEV):
        pl.semaphore_signal(barrier, inc=1, device_id=(x ^ (k >> 2), y ^ ((k >> 1) & 1), c ^ (k & 1)),
                            device_id_type=MESH)
    pl.semaphore_wait(barrier, NDEV - 1)


def _call(body, *, name, grid, ins, in_specs, out_shape, out_specs, scratch=(), vmem=None, carries=()):
    n_in, n_out, n_scr = len(ins), len(out_shape), len(scratch)
    ng = len(grid)
    own_barrier = any(getattr(c, "waits_for_all", False) for c in carries)

    def split(refs):
        pos = [0]

        def take(k):
            part = refs[pos[0]:pos[0] + k]
            pos[0] += k
            return part

        i_refs = take(n_in)
        c_in = [take(len(c.ins)) for c in carries]
        o_refs = take(n_out)
        c_out = [take(len(c.out_shape)) for c in carries]
        s_refs = take(n_scr)
        c_scr = [take(len(c.scratch)) for c in carries]
        return i_refs, o_refs, s_refs, list(zip(carries, c_in, c_out, c_scr))

    def full(*refs):
        i_refs, o_refs, s_refs, cparts = split(refs)
        relays = [part for part in cparts if hasattr(part[0], "pass_on")]
        if ng == 0:
            if own_barrier:
                _mesh_barrier()
            for c, a, b, s in cparts:
                c.start(a, b, s)
            for c, a, b, s in relays:
                c.pass_on(a, b, s)
            for c, a, b, s in cparts:
                c.finish(a, b, s)
            return
        ids = [pl.program_id(a) for a in range(ng)]
        ahead = grid[-1] >= 2
        if cparts:
            @pl.when(functools.reduce(operator.and_, [i == 0 for i in ids]))
            def _():
                if own_barrier:
                    _mesh_barrier()
                for c, a, b, s in cparts:
                    c.start(a, b, s)

        if relays and ahead:
            @pl.when(functools.reduce(operator.and_, [i == g - 1 for i, g in zip(ids[:-1], grid[:-1])]
                                      + [ids[-1] == grid[-1] - 2]))
            def _():
                for c, a, b, s in relays:
                    c.pass_on(a, b, s)

        body(*i_refs, *o_refs, *s_refs)
        if cparts:
            @pl.when(functools.reduce(operator.and_, [i == g - 1 for i, g in zip(ids, grid)]))
            def _():
                if not ahead:
                    for c, a, b, s in relays:
                        c.pass_on(a, b, s)
                for c, a, b, s in cparts:
                    c.finish(a, b, s)

    all_ins = list(ins) + [a for c in carries for a in c.ins]
    all_in_specs = list(in_specs) + [ANY for c in carries for _ in c.ins]
    all_out_shape = list(out_shape) + [s for c in carries for s in c.out_shape]
    all_out_specs = list(out_specs) + [ANY for c in carries for _ in c.out_shape]
    all_scratch = list(scratch) + [s for c in carries for s in c.scratch]
    kwargs = dict(grid=grid) if ng else {}
    outs = pl.pallas_call(
        full, name=name, in_specs=all_in_specs, out_specs=all_out_specs, out_shape=all_out_shape,
        scratch_shapes=all_scratch,
        compiler_params=pltpu.CompilerParams(dimension_semantics=("arbitrary",) * ng if ng else None,
                                             vmem_limit_bytes=vmem, collective_id=0 if own_barrier else None),
        **kwargs)(*all_ins)
    outs = list(outs)
    res, pos = outs[:n_out], n_out
    carried = []
    for c in carries:
        carried.append(outs[pos:pos + len(c.out_shape)])
        pos += len(c.out_shape)
    return res, carried


def _pair_sum(g, items):
    rows = items[0][1]
    n = len(items) * (NDEV // 2)
    width = g.shape[1]

    def body(g_ref, out_ref, got_ref, mine_v, got_v, send_sems, recv_sems, ld_sems, st_sems):
        x, y, c = lax.axis_index("x"), lax.axis_index("y"), lax.axis_index("c")
        keep = [base + (2 * ch + c) * rows for base, _ in items for ch in range(NDEV // 2)]
        give = [base + (2 * ch + 1 - c) * rows for base, _ in items for ch in range(NDEV // 2)]

        def remote(j, r):
            return pltpu.make_async_remote_copy(
                src_ref=g_ref.at[pl.ds(r, rows)], dst_ref=got_ref.at[pl.ds(r, rows)], send_sem=send_sems.at[j],
                recv_sem=recv_sems.at[j], device_id=(x, y, 1 - c), device_id_type=MESH)

        sends = [remote(j, r) for j, r in enumerate(give)]
        for cp in sends:
            cp.start()
        loads = [pltpu.make_async_copy(g_ref.at[pl.ds(r, rows)], mine_v.at[j], ld_sems.at[0, j])
                 for j, r in enumerate(keep)]
        for cp in loads:
            cp.start()
        stores = []
        for j, r in enumerate(keep):
            remote(j, r).wait_recv()
            cp = pltpu.make_async_copy(got_ref.at[pl.ds(r, rows)], got_v.at[j], ld_sems.at[1, j])
            cp.start()
            loads[j].wait()
            cp.wait()
            mine_v[j] = (mine_v[j].astype(F32) + got_v[j].astype(F32)).astype(BF16)
            st = pltpu.make_async_copy(mine_v.at[j], out_ref.at[pl.ds(r, rows)], st_sems.at[j])
            st.start()
            stores.append(st)
        for cp in sends:
            cp.wait_send()
        for cp in stores:
            cp.wait()

    out, _ = pl.pallas_call(
        body, name="pair_sum", in_specs=[ANY], out_specs=[ANY, ANY],
        out_shape=[jax.ShapeDtypeStruct(g.shape, g.dtype), jax.ShapeDtypeStruct(g.shape, g.dtype)],
        scratch_shapes=[pltpu.VMEM((n, rows, width), g.dtype), pltpu.VMEM((n, rows, width), g.dtype),
                        DMA((n,)), DMA((n,)), DMA((2, n)), DMA((n,))],
        compiler_params=pltpu.CompilerParams(vmem_limit_bytes=40 * 2 ** 20))(g)
    return out


def _groups_of(names):
    out = []
    for n in names:
        if LOC[n][0] not in out:
            out.append(LOC[n][0])
    return out


def _weight_pieces(name):
    g, off, rows = LOC[name]
    return [(d * GROUP_ROWS[g] + off, d * rows, rows) for d in range(NDEV)]


def _win_pieces():
    kv = N_KV * HEAD_DIM
    pieces = [(0, 0, D)]
    for g0, d0 in ((D, D), (D + kv, D + N_KV * LANES)):
        for g in range(N_KV):
            for half in range(2):
                pieces.append((g0 + g * HEAD_DIM, d0 + g * LANES + half * HEAD_DIM, HEAD_DIM))
    pieces.append((D + 2 * kv, D + 2 * N_KV * LANES, D))
    pieces.append((D + 2 * kv + D, D + 2 * N_KV * LANES + D, D))
    grp, off, rows = LOC["w_in"]
    out = []
    for g0, d0, n in pieces:
        while n > 0:
            dev, loc = divmod(g0, rows)
            m = min(n, rows - loc)
            out.append((dev * GROUP_ROWS[grp] + off + loc, d0, m))
            g0, d0, n = g0 + m, d0 + m, n - m
    return out


def _start_loads(src_ref, dst_ref, pieces, sems, base):
    cps = []
    for j, (s, d, n) in enumerate(pieces):
        cp = pltpu.make_async_copy(src_ref.at[pl.ds(s, n)], dst_ref.at[pl.ds(d, n)], sems.at[base + j])
        cp.start()
        cps.append(cp)
    return cps


def _load_weights(wrefs, targets, sems):
    cps, base = [], 0
    for name, dst in targets:
        pieces = _win_pieces() if name == "w_in" else _weight_pieces(name)
        cps += _start_loads(wrefs[LOC[name][0]], dst, pieces, sems, base)
        base += len(pieces)
    for cp in cps:
        cp.wait()


def _n_pieces(names):
    return sum(len(_win_pieces()) if n == "w_in" else NDEV for n in names)


def _dw(lhs, rhs, chunk, name, carries=()):
    nq = 4
    if lhs.ndim == 3:
        nch, t_tok, chunk = lhs.shape
        c = nch * chunk
        tq = t_tok // nq
        lhs_specs = [pl.BlockSpec((None, tq, chunk), lambda i, k=k: (i, k, 0)) for k in range(nq)]
    else:
        t_tok, c = lhs.shape
        tq = t_tok // nq
        lhs_specs = [pl.BlockSpec((tq, chunk), lambda i, k=k: (k, i)) for k in range(nq)]

    def body(*refs):
        lhs_refs, (rhs_ref, out_ref, rhs_s, sems) = refs[:nq], refs[nq:]
        cps = [pltpu.make_async_copy(rhs_ref.at[pl.ds(k * tq, tq)], rhs_s.at[pl.ds(k * tq, tq)], sems.at[k])
               for k in range(nq)]
        first = pl.program_id(0) == 0

        @pl.when(first)
        def _():
            for cp in cps:
                cp.start()

        for o in range(0, chunk, D // 2):
            n = min(D // 2, chunk - o)
            acc = None
            for k in range(nq):
                if o == 0:
                    @pl.when(first)
                    def _():
                        cps[k].wait()

                part = _tn(lhs_refs[k][:, pl.ds(o, n)], rhs_s[pl.ds(k * tq, tq), :])
                acc = part if acc is None else acc + part
            out_ref[pl.ds(o, n), :] = acc.astype(BF16)

    (out,), carried = _call(
        body, name=name, grid=(c // chunk,), ins=[lhs] * nq + [rhs],
        in_specs=lhs_specs + [ANY],
        out_specs=[pl.BlockSpec((chunk, D), lambda i: (i, 0))],
        out_shape=[jax.ShapeDtypeStruct((c, D), BF16)],
        scratch=[pltpu.VMEM((t_tok, D), BF16), DMA((nq,))], vmem=VMEM_BIG, carries=carries)
    return out, carried


def _silu_parts(a):
    sig = jax.nn.sigmoid(a)
    return sig, a * sig


FC = 256


def _ffn_fwd(h, gpre, gpost, wg, names, target=None, tm=512, carries=()):
    t_tok = h.shape[0]
    nt = t_tok // tm
    with_loss = target is not None
    groups = _groups_of(names)

    def body(*refs):
        refs = list(refs)
        h_ref, gpre_ref, gpost_ref = refs[:3]
        del refs[:3]
        tgt_ref = refs.pop(0) if with_loss else None
        wrefs = dict(zip(groups, refs[:len(groups)]))
        del refs[:len(groups)]
        hout_ref, a_ref, b_ref, f_ref, nb_ref = refs[:5]
        del refs[:5]
        if with_loss:
            dy_ref, loss_ref = refs[:2]
            del refs[:2]
        w1_s, w3_s, w2_s, sems = refs
        i = pl.program_id(0)

        @pl.when(i == 0)
        def _():
            _load_weights(wrefs, list(zip(names, (w1_s, w3_s, w2_s))), sems)
            if with_loss:
                loss_ref[...] = jnp.zeros_like(loss_ref)

        x = h_ref[...]
        n, _, _ = _rms_fwd(x, gpre_ref[...])
        nb = n.astype(BF16)
        nb_ref[...] = nb
        f = jnp.zeros((tm, D), F32)
        for c0 in range(0, FF, FC):
            a = _nt(nb, w1_s[pl.ds(c0, FC), :])
            b = _nt(nb, w3_s[pl.ds(c0, FC), :])
            _, sl = _silu_parts(a)
            a_ref[:, pl.ds(c0, FC)] = a.astype(BF16)
            b_ref[:, pl.ds(c0, FC)] = b.astype(BF16)
            f = f + _nn((sl * b).astype(BF16), w2_s[pl.ds(c0, FC), :])
        f_ref[...] = f
        fn, _, _ = _rms_fwd(f, gpost_ref[...])
        y = x + 0.5 * fn
        hout_ref[...] = y
        if with_loss:
            err = y - tgt_ref[...]
            dy_ref[...] = err * (1.0 / D)
            loss_ref[...] += jnp.sum(jnp.sum(err * err, axis=-1, keepdims=True), axis=0, keepdims=True) * (0.5 / D)

    ins = [h, gpre, gpost] + ([target] if with_loss else []) + [wg[g] for g in groups]
    in_specs = [_row_spec(tm, D), _const_spec((1, D)), _const_spec((1, D))]
    in_specs += ([_row_spec(tm, D)] if with_loss else []) + [ANY] * len(groups)
    out_shape = [jax.ShapeDtypeStruct((t_tok, D), F32), jax.ShapeDtypeStruct((t_tok, FF), BF16),
                 jax.ShapeDtypeStruct((t_tok, FF), BF16), jax.ShapeDtypeStruct((t_tok, D), F32),
                 jax.ShapeDtypeStruct((t_tok, D), BF16)]
    out_specs = [_row_spec(tm, D), _row_spec(tm, FF), _row_spec(tm, FF), _row_spec(tm, D), _row_spec(tm, D)]
    if with_loss:
        out_shape += [jax.ShapeDtypeStruct((t_tok, D), F32), jax.ShapeDtypeStruct((1, LANES), F32)]
        out_specs += [_row_spec(tm, D), _const_spec((1, LANES))]
    return _call(body, name="ffn_fwd_" + names[0][:4], grid=(nt,), ins=ins, in_specs=in_specs,
                 out_shape=out_shape, out_specs=out_specs,
                 scratch=[pltpu.VMEM((FF, D), BF16)] * 3 + [DMA((3 * NDEV,))], vmem=VMEM_BIG, carries=carries)


FH = FF // 2
HALF_PIECES = ((0, 256), (256, 256), (512, 256), (768, 256), (1024, 256), (1280, 128))


def _ffn_bwd_a(dh, f, a, b, gpost, wg, name_w2, tm=512, carries=()):
    t_tok = dh.shape[0]
    nt = t_tok // tm
    groups = _groups_of([name_w2])

    def body(dh_ref, f_ref, a_ref, b_ref, gpost_ref, wg_ref, dab_ref, s_ref, df_ref, dgp_ref, w2_s, sems):
        i = pl.program_id(0)

        @pl.when(i == 0)
        def _():
            _load_weights({groups[0]: wg_ref}, [(name_w2, w2_s)], sems)
            dgp_ref[...] = jnp.zeros_like(dgp_ref)

        fv = f_ref[...]
        _, fh, r = _rms_fwd(fv, gpost_ref[...])
        df, dg = _rms_bwd(0.5 * dh_ref[...], fh, r, gpost_ref[...])
        dgp_ref[...] += dg
        dfb = df.astype(BF16)
        df_ref[...] = dfb
        for half in range(2):
            for o, n in HALF_PIECES:
                c0 = half * FH + o
                ds = _nt(dfb, w2_s[pl.ds(c0, n), :])
                av = a_ref[:, pl.ds(c0, n)].astype(F32)
                bv = b_ref[:, pl.ds(c0, n)].astype(F32)
                sig, sl = _silu_parts(av)
                dab_ref[half, :, pl.ds(o, n)] = (ds * bv * (sig * (1.0 + av * (1.0 - sig)))).astype(BF16)
                dab_ref[2 + half, :, pl.ds(o, n)] = (ds * sl).astype(BF16)
                s_ref[half, :, pl.ds(o, n)] = (sl * bv).astype(BF16)

    return _call(
        body, name="ffn_bwd_a_" + name_w2[:4], grid=(nt,), ins=[dh, f, a, b, gpost, wg[groups[0]]],
        in_specs=[_row_spec(tm, D), _row_spec(tm, D), _row_spec(tm, FF), _row_spec(tm, FF), _const_spec((1, D)), ANY],
        out_specs=[pl.BlockSpec((4, tm, FH), lambda i: (0, i, 0)), pl.BlockSpec((2, tm, FH), lambda i: (0, i, 0)),
                   _row_spec(tm, D), _const_spec((1, D))],
        out_shape=[jax.ShapeDtypeStruct((4, t_tok, FH), BF16), jax.ShapeDtypeStruct((2, t_tok, FH), BF16),
                   jax.ShapeDtypeStruct((t_tok, D), BF16), jax.ShapeDtypeStruct((1, D), F32)],
        scratch=[pltpu.VMEM((FF, D), BF16), DMA((NDEV,))],
        vmem=VMEM_BIG, carries=carries)


def _ffn_bwd_dx(dab, h, dh, gpre, wg, name_w1, name_w3, tm=512, carries=()):
    t_tok = dh.shape[0]
    nt = t_tok // tm
    groups = _groups_of([name_w1, name_w3])

    def body(*refs):
        dab_refs, (h_ref, dh_ref, gpre_ref) = refs[:4], refs[4:7]
        wrefs = dict(zip(groups, refs[7:7 + len(groups)]))
        dhin_ref, dgp_ref, w13_s, sems = refs[7 + len(groups):]
        i = pl.program_id(0)

        @pl.when(i == 0)
        def _():
            _load_weights(wrefs, [(name_w1, w13_s.at[pl.ds(0, FF)]), (name_w3, w13_s.at[pl.ds(FF, FF)])], sems)
            dgp_ref[...] = jnp.zeros_like(dgp_ref)

        g = gpre_ref[...]
        _, xh, r = _rms_fwd(h_ref[...], g)
        dn = _nn(dab_refs[0][...], w13_s[pl.ds(0, FH), :])
        for k in range(1, 4):
            dn = dn + _nn(dab_refs[k][...], w13_s[pl.ds(k * FH, FH), :])
        dx, dg = _rms_bwd(dn, xh, r, g)
        dgp_ref[...] += dg
        dhin_ref[...] = dh_ref[...] + dx

    return _call(
        body, name="ffn_bwd_dx_" + name_w1[:4], grid=(nt,), ins=[dab] * 4 + [h, dh, gpre] + [wg[g] for g in groups],
        in_specs=[pl.BlockSpec((None, tm, FH), lambda i, k=k: (k, i, 0)) for k in range(4)]
        + [_row_spec(tm, D), _row_spec(tm, D), _const_spec((1, D))] + [ANY] * len(groups),
        out_specs=[_row_spec(tm, D), _const_spec((1, D))],
        out_shape=[jax.ShapeDtypeStruct((t_tok, D), F32), jax.ShapeDtypeStruct((1, D), F32)],
        scratch=[pltpu.VMEM((2 * FF, D), BF16), DMA((2 * NDEV,))],
        vmem=VMEM_BIG, carries=carries)


Q0, K0, V0, XR0, XG0 = 0, D, D + N_KV * LANES, D + 2 * N_KV * LANES, 2 * D + 2 * N_KV * LANES


def _mix_proj_fwd(h, g, bgate, wg, tm=512, carries=()):
    t_tok = h.shape[0]
    nt = t_tok // tm
    names = ("w_in", "w_gate")
    groups = _groups_of(names)

    def body(h_ref, g_ref, bg_ref, wg_ref, q_ref, k_ref, v_ref, xr_ref, xg_ref, gs_ref, ub_ref, win_s, wgt_s, sems):
        i = pl.program_id(0)

        @pl.when(i == 0)
        def _():
            _load_weights({groups[0]: wg_ref}, [("w_in", win_s), ("w_gate", wgt_s)], sems)

        n, _, _ = _rms_fwd(h_ref[...], g_ref[...])
        nb = n.astype(BF16)
        ub_ref[...] = nb
        q_ref[...] = _nt(nb, win_s[pl.ds(Q0, D), :]).astype(BF16)
        k_ref[...] = _nt(nb, win_s[pl.ds(K0, N_KV * LANES), :]).astype(BF16)
        v_ref[...] = _nt(nb, win_s[pl.ds(V0, N_KV * LANES), :]).astype(BF16)
        xr_ref[...] = _nt(nb, win_s[pl.ds(XR0, D), :])
        xg_ref[...] = _nt(nb, win_s[pl.ds(XG0, D), :])
        gs_ref[...] = jax.nn.sigmoid(_nt(nb, wgt_s[...]) + bg_ref[...]).astype(BF16)

    kvw = N_KV * LANES
    return _call(
        body, name="mix_proj_fwd", grid=(nt,), ins=[h, g, bgate, wg[groups[0]]],
        in_specs=[_row_spec(tm, D), _const_spec((1, D)), _const_spec((1, 2 * D)), ANY],
        out_specs=[_row_spec(tm, D), _row_spec(tm, kvw), _row_spec(tm, kvw), _row_spec(tm, D), _row_spec(tm, D),
                   _row_spec(tm, 2 * D), _row_spec(tm, D)],
        out_shape=[jax.ShapeDtypeStruct((t_tok, D), BF16), jax.ShapeDtypeStruct((t_tok, kvw), BF16),
                   jax.ShapeDtypeStruct((t_tok, kvw), BF16), jax.ShapeDtypeStruct((t_tok, D), F32),
                   jax.ShapeDtypeStruct((t_tok, D), F32), jax.ShapeDtypeStruct((t_tok, 2 * D), BF16),
                   jax.ShapeDtypeStruct((t_tok, D), BF16)],
        scratch=[pltpu.VMEM((INP_W, D), BF16), pltpu.VMEM((2 * D, D), BF16), DMA((_n_pieces(names),))],
        vmem=VMEM_BIG, carries=carries)


def _mix_proj_bwd(dqkv, dxr, dxg, dgpre, h, dh, g, wg, tm=512, carries=()):
    t_tok = h.shape[0]
    nt = t_tok // tm
    names = ("w_in", "w_gate")
    groups = _groups_of(names)

    def body(dqkv_ref, dxr_ref, dxg_ref, dgp_ref, h_ref, dh_ref, g_ref, wg_ref, dhin_ref, dg_ref, win_s, wgt_s, sems):
        i = pl.program_id(0)

        @pl.when(i == 0)
        def _():
            _load_weights({groups[0]: wg_ref}, [("w_in", win_s), ("w_gate", wgt_s)], sems)
            dg_ref[...] = jnp.zeros_like(dg_ref)

        gv = g_ref[...]
        _, xh, r = _rms_fwd(h_ref[...], gv)
        du = _nn(dgp_ref[...], wgt_s[...])
        du = du + _nn(dqkv_ref[...], win_s[pl.ds(Q0, XR0), :])
        du = du + _nn(dxr_ref[...], win_s[pl.ds(XR0, D), :])
        du = du + _nn(dxg_ref[...], win_s[pl.ds(XG0, D), :])
        dx, dg = _rms_bwd(du, xh, r, gv)
        dg_ref[...] += dg
        dhin_ref[...] = dh_ref[...] + dx

    return _call(
        body, name="mix_proj_bwd", grid=(nt,), ins=[dqkv, dxr, dxg, dgpre, h, dh, g, wg[groups[0]]],
        in_specs=[_row_spec(tm, XR0), _row_spec(tm, D), _row_spec(tm, D), _row_spec(tm, 2 * D), _row_spec(tm, D),
                  _row_spec(tm, D), _const_spec((1, D)), ANY],
        out_specs=[_row_spec(tm, D), _const_spec((1, D))],
        out_shape=[jax.ShapeDtypeStruct((t_tok, D), F32), jax.ShapeDtypeStruct((1, D), F32)],
        scratch=[pltpu.VMEM((INP_W, D), BF16), pltpu.VMEM((2 * D, D), BF16), DMA((_n_pieces(names),))],
        vmem=VMEM_BIG, carries=carries)


def _shift_down(x, d, fill):
    row = lax.broadcasted_iota(jnp.int32, x.shape, 0)
    return jnp.where(row >= d, pltpu.roll(x, d, 0), fill)


def _shift_up(x, d, fill):
    rows = x.shape[0]
    row = lax.broadcasted_iota(jnp.int32, x.shape, 0)
    return jnp.where(row < rows - d, pltpu.roll(x, rows - d, 0), fill)


def _scan_rows(a, b, reverse):
    rows = a.shape[0]
    d = 1
    while d < rows:
        if d < 8:
            shift = _shift_up if reverse else _shift_down
            b = a * shift(b, d, 0.0) + b
            a = a * shift(a, d, 1.0)
        elif reverse:
            b = jnp.concatenate([a[:rows - d] * b[d:] + b[:rows - d], b[rows - d:]], axis=0)
            a = jnp.concatenate([a[:rows - d] * a[d:], a[rows - d:]], axis=0)
        else:
            b = jnp.concatenate([b[:d], a[d:] * b[:rows - d] + b[d:]], axis=0)
            a = jnp.concatenate([a[:d], a[d:] * a[:rows - d]], axis=0)
        d *= 2
    return a, b


def _softplus(x):
    return jnp.maximum(x, 0.0) + jnp.log(1.0 + jnp.exp(-jnp.abs(x)))


_GELU_C = math.sqrt(2.0 / math.pi)


def _gelu_parts(x):
    th = jnp.tanh(_GELU_C * (x + 0.044715 * x * x * x))
    val = 0.5 * x * (1.0 + th)
    grad = 0.5 * (1.0 + th) + 0.5 * x * (1.0 - th * th) * _GELU_C * (1.0 + 3.0 * 0.044715 * x * x)
    return val, grad


def _lru_pre(x, halo, cw_ref, cb_ref, wa_ref, wx_ref, ba_ref, bx_ref, lam_ref):
    ext = jnp.concatenate([halo, x], axis=0)
    shifted = [x] + [pltpu.roll(ext, k, 0)[8:] for k in (1, 2, 3)]
    xc = cb_ref[...] + cw_ref[pl.ds(CONV_WIDTH - 1, 1), :] * x
    for k in (1, 2, 3):
        xc = xc + cw_ref[pl.ds(CONV_WIDTH - 1 - k, 1), :] * shifted[k]
    xcb = xc.astype(BF16)
    r = jax.nn.sigmoid(_nn(xcb, wa_ref[...]) + ba_ref[...])
    ig = jax.nn.sigmoid(_nn(xcb, wx_ref[...]) + bx_ref[...])
    sp = _softplus(-lam_ref[...])
    log_a = -LRU_C * r * sp
    a = jnp.exp(log_a)
    th = jnp.tanh(log_a)
    mult = jnp.sqrt(-2.0 * th / (1.0 - th))
    return shifted, xc, xcb, r, ig, sp, a, mult


def _lru_specs(nt, rows, reverse):
    def tt(t):
        return nt - 1 - t if reverse else t
    tile = pl.BlockSpec((rows, LANES), lambda cb, t: (tt(t), cb))
    halo = pl.BlockSpec((8, LANES), lambda cb, t: (jnp.maximum(tt(t) * (rows // 8) - 1, 0), cb))
    vec = pl.BlockSpec((1, LANES), lambda cb, t: (0, cb))
    cw = pl.BlockSpec((CONV_WIDTH, LANES), lambda cb, t: (0, cb))
    mat = pl.BlockSpec((None, LANES, LANES), lambda cb, t: (cb, 0, 0))
    return tile, halo, vec, cw, mat


def _lru_fwd(xr, xg, cw, cb, wa, wx, ba, bx, lam, carries=()):
    t_tok = xr.shape[0]
    rows = min(LRU_ROWS, t_tok)
    nt = t_tok // rows

    def body(xr_ref, xg_ref, cw_ref, cb_ref, wa_ref, wx_ref, ba_ref, bx_ref, lam_ref, y_ref, h_ref, tail_s, hc_s):
        t = pl.program_id(1)

        @pl.when(t == 0)
        def _():
            tail_s[...] = jnp.zeros_like(tail_s)
            hc_s[...] = jnp.zeros_like(hc_s)

        x = xr_ref[...]
        _, xc, _, _, ig, _, a, mult = _lru_pre(x, tail_s[...], cw_ref, cb_ref, wa_ref, wx_ref, ba_ref, bx_ref, lam_ref)
        tail_s[...] = xr_ref[pl.ds(rows - 8, 8), :]
        acc_a, acc_b = _scan_rows(a, mult * (ig * xc), False)
        hv = acc_b + acc_a * hc_s[...]
        h_ref[...] = hv
        hc_s[...] = h_ref[pl.ds(rows - 1, 1), :]
        gl, _ = _gelu_parts(xg_ref[...])
        y_ref[...] = (hv * gl).astype(BF16)

    tile, _, vec, cws, mat = _lru_specs(nt, rows, False)
    return _call(
        body, name="lru_fwd", grid=(D // LANES, nt), ins=[xr, xg, cw, cb, wa, wx, ba, bx, lam],
        in_specs=[tile, tile, cws, vec, mat, mat, vec, vec, vec],
        out_specs=[tile, tile],
        out_shape=[jax.ShapeDtypeStruct((t_tok, D), BF16), jax.ShapeDtypeStruct((t_tok, D), F32)],
        scratch=[pltpu.VMEM((8, LANES), F32), pltpu.VMEM((1, LANES), F32)], carries=carries)


def _lru_bwd(dy, xr, xg, hseq, cw, cb, wa, wx, ba, bx, lam, carries=()):
    t_tok = xr.shape[0]
    rows = min(LRU_ROWS // 2, t_tok)
    nt = t_tok // rows

    def body(dy_ref, xr_ref, xrh_ref, xg_ref, h_ref, hh_ref, cw_ref, cb_ref, wa_ref, wx_ref, ba_ref, bx_ref, lam_ref,
             dxr_ref, dxg_ref, dvec_ref, dwa_ref, dwx_ref, gcar_s, acar_s, head_s, tmp_s):
        t = pl.program_id(1)
        first_tile = t == nt - 1

        @pl.when(t == 0)
        def _():
            gcar_s[...] = jnp.zeros_like(gcar_s)
            acar_s[...] = jnp.zeros_like(acar_s)
            head_s[...] = jnp.zeros_like(head_s)
            dvec_ref[...] = jnp.zeros_like(dvec_ref)
            dwa_ref[...] = jnp.zeros_like(dwa_ref)
            dwx_ref[...] = jnp.zeros_like(dwx_ref)

        x = xr_ref[...]
        halo = jnp.where(first_tile, 0.0, xrh_ref[...])
        shifted, xc, xcb, r, ig, sp, a, mult = _lru_pre(x, halo, cw_ref, cb_ref, wa_ref, wx_ref, ba_ref, bx_ref, lam_ref)
        hv = h_ref[...]
        dyv = dy_ref[...]
        gl, glg = _gelu_parts(xg_ref[...])
        dxg_ref[...] = (dyv * hv * glg).astype(BF16)
        acc_a, acc_b = _scan_rows(_shift_up(a, 1, acar_s[...]), dyv * gl, True)
        g = acc_b + acc_a * gcar_s[...]
        hhalo = jnp.where(first_tile, 0.0, hh_ref[...])
        hprev = pltpu.roll(jnp.concatenate([hhalo, hv], axis=0), 1, 0)[8:]
        dmult = g * ig * xc
        dlog_a = a * (g * hprev) - dmult * a * a / mult
        dig = g * mult * xc
        dxc = g * mult * ig
        dzr = (dlog_a * (-LRU_C * sp)) * r * (1.0 - r)
        dzx = dig * ig * (1.0 - ig)
        dzrb, dzxb = dzr.astype(BF16), dzx.astype(BF16)
        dxc = dxc + _nt(dzrb, wa_ref[...]) + _nt(dzxb, wx_ref[...])
        dwa_ref[...] += _tn(xcb, dzrb)
        dwx_ref[...] += _tn(xcb, dzxb)
        dsp = jnp.sum(dlog_a * (-LRU_C * r), axis=0, keepdims=True)
        dlam = dsp * (-jax.nn.sigmoid(-lam_ref[...]))
        vrow = lax.broadcasted_iota(jnp.int32, (8, LANES), 0)
        upd = jnp.where(vrow == 4, jnp.sum(dxc, axis=0, keepdims=True), 0.0)
        upd = jnp.where(vrow == 5, jnp.sum(dzr, axis=0, keepdims=True), upd)
        upd = jnp.where(vrow == 6, jnp.sum(dzx, axis=0, keepdims=True), upd)
        upd = jnp.where(vrow == 7, dlam, upd)
        for k in range(CONV_WIDTH):
            upd = jnp.where(vrow == CONV_WIDTH - 1 - k, jnp.sum(dxc * shifted[k], axis=0, keepdims=True), upd)
        dvec_ref[...] += upd
        ext = jnp.concatenate([dxc, head_s[...]], axis=0)
        dxr = cw_ref[pl.ds(CONV_WIDTH - 1, 1), :] * dxc
        for k in (1, 2, 3):
            dxr = dxr + cw_ref[pl.ds(CONV_WIDTH - 1 - k, 1), :] * pltpu.roll(ext, rows + 8 - k, 0)[:rows]
        dxr_ref[...] = dxr.astype(BF16)
        tmp_s[...] = g
        gcar_s[...] = tmp_s[pl.ds(0, 1), :]
        tmp_s[...] = a
        acar_s[...] = tmp_s[pl.ds(0, 1), :]
        tmp_s[...] = dxc
        head_s[...] = tmp_s[pl.ds(0, 8), :]

    tile, halo, vec, cws, mat = _lru_specs(nt, rows, True)
    return _call(
        body, name="lru_bwd", grid=(D // LANES, nt), ins=[dy, xr, xr, xg, hseq, hseq, cw, cb, wa, wx, ba, bx, lam],
        in_specs=[tile, tile, halo, tile, tile, halo, cws, vec, mat, mat, vec, vec, vec],
        out_specs=[tile, tile, pl.BlockSpec((8, LANES), lambda cb, t: (0, cb)), mat, mat],
        out_shape=[jax.ShapeDtypeStruct((t_tok, D), BF16), jax.ShapeDtypeStruct((t_tok, D), BF16),
                   jax.ShapeDtypeStruct((8, D), F32), jax.ShapeDtypeStruct((D // LANES, LANES, LANES), F32),
                   jax.ShapeDtypeStruct((D // LANES, LANES, LANES), F32)],
        scratch=[pltpu.VMEM((1, LANES), F32), pltpu.VMEM((1, LANES), F32), pltpu.VMEM((8, LANES), F32),
                 pltpu.VMEM((rows, LANES), F32)], carries=carries)


def _t5_bucket_np(rel):
    nb = N_BUCKETS // 2
    max_exact = nb // 2
    ret = np.where(rel > 0, nb, 0)
    n = np.abs(rel)
    nf = np.maximum(n, 1).astype(np.float32)
    large = max_exact + (np.log(nf / np.float32(max_exact)) / np.float32(math.log(MAX_DISTANCE / max_exact))
                         * np.float32(nb - max_exact)).astype(np.int32)
    large = np.minimum(large, nb - 1)
    return ret + np.where(n < max_exact, n, large)


def _bucket_map():
    r = np.arange(QT)[:, None]
    c = np.arange(KW)[None, :]
    j = c - (r // CHUNK) * CHUNK
    band = (j >= 0) & (j < WINDOW + CHUNK)
    return np.where(band, _t5_bucket_np(c - r - WINDOW), -1).astype(np.int32)


def _attn_specs(nt, reverse):
    def tt(i):
        return nt - 1 - i if reverse else i
    kvw = N_KV * LANES
    qs = pl.BlockSpec((QT, D), lambda i: (tt(i), 0))
    cur = pl.BlockSpec((QT, kvw), lambda i: (tt(i), 0))
    prev = pl.BlockSpec((WINDOW, kvw), lambda i: (jnp.maximum(tt(i) * (QT // WINDOW) - 1, 0), 0))
    lse = pl.BlockSpec((QT, LANES), lambda i: (tt(i), 0))
    return qs, cur, prev, lse


REP = N_HEADS // N_KV
SCALE = HEAD_DIM ** -0.5


def _stack_heads(x_ref, g, lo, scale=None):
    parts = []
    for hl in range(REP):
        xs = x_ref[:, pl.ds((2 * g + hl // 2) * LANES, LANES)]
        xs = jnp.where(lo if hl % 2 == 0 else jnp.logical_not(lo), xs, jnp.zeros_like(xs))
        parts.append(xs if scale is None else xs * jnp.asarray(scale, xs.dtype))
    return jnp.concatenate(parts, axis=0)


def _stack_sinks(sink_ref, g, srow):
    sk = jnp.full(srow.shape, sink_ref[REP * g + REP - 1], F32)
    for hl in range(REP - 2, -1, -1):
        sk = jnp.where(srow < (hl + 1) * QT, sink_ref[REP * g + hl], sk)
    return sk


def _attn_fwd(q, kd, vd, bias, sinks, carries=()):
    t_tok = q.shape[0]
    nt = t_tok // QT

    def body(q_ref, kp_ref, kc_ref, vp_ref, vc_ref, bias_ref, sink_ref, o_ref, lse_ref):
        i = pl.program_id(0)
        col = lax.broadcasted_iota(jnp.int32, (1, KW), 1)
        first = jnp.where((i == 0) & (col < WINDOW), NEG_INF, 0.0)
        lane = lax.broadcasted_iota(jnp.int32, (QT, LANES), 1)
        lo = lane < HEAD_DIM
        srow = lax.broadcasted_iota(jnp.int32, (REP * QT, 1), 0)
        lse_t = jnp.zeros((QT, LANES), F32)
        for g in range(N_KV):
            kwin = jnp.concatenate([kp_ref[:, pl.ds(g * LANES, LANES)], kc_ref[:, pl.ds(g * LANES, LANES)]], axis=0)
            vwin = jnp.concatenate([vp_ref[:, pl.ds(g * LANES, LANES)], vc_ref[:, pl.ds(g * LANES, LANES)]], axis=0)
            qst = _stack_heads(q_ref, g, lo, SCALE)
            s = _nt(qst, kwin) + (bias_ref[g] + first)
            sk = _stack_sinks(sink_ref, g, srow)
            m = jnp.maximum(jnp.max(s, axis=-1, keepdims=True), sk)
            e = jnp.exp(s - m)
            l = jnp.sum(e, axis=-1, keepdims=True) + jnp.exp(sk - m)
            p = e / l
            ost = _nn(p.astype(BF16), vwin)
            lse_s = m + jnp.log(l)
            for sl in range(2):
                o_ref[:, pl.ds((2 * g + sl) * LANES, LANES)] = jnp.where(
                    lo, ost[2 * sl * QT:(2 * sl + 1) * QT], ost[(2 * sl + 1) * QT:(2 * sl + 2) * QT]).astype(BF16)
            for hl in range(REP):
                lse_t = jnp.where(lane == REP * g + hl, lse_s[hl * QT:(hl + 1) * QT], lse_t)
        lse_ref[...] = lse_t

    qs, cur, prev, lse = _attn_specs(nt, False)
    return _call(
        body, name="attn_fwd", grid=(nt,), ins=[q, kd, kd, vd, vd, bias, sinks],
        in_specs=[qs, prev, cur, prev, cur, _const_spec((N_KV, REP * QT, KW)), pl.BlockSpec(memory_space=pltpu.SMEM)],
        out_specs=[qs, lse],
        out_shape=[jax.ShapeDtypeStruct((t_tok, D), BF16), jax.ShapeDtypeStruct((t_tok, LANES), F32)],
        vmem=48 * 2 ** 20, carries=carries)


def _attn_bwd(q, kd, vd, o, do, lse, bias, sinks, carries=()):
    t_tok = q.shape[0]
    nt = t_tok // QT
    kvw = N_KV * LANES

    def body(q_ref, kp_ref, kc_ref, vp_ref, vc_ref, o_ref, do_ref, lse_ref, bias_ref, sink_ref,
             dqkv_ref, ds_ref, dsink_ref, kcar_s, vcar_s):
        i = pl.program_id(0)
        tile = nt - 1 - i

        @pl.when(i == 0)
        def _():
            kcar_s[...] = jnp.zeros_like(kcar_s)
            vcar_s[...] = jnp.zeros_like(vcar_s)
            ds_ref[...] = jnp.zeros_like(ds_ref)
            dsink_ref[...] = jnp.zeros_like(dsink_ref)

        col = lax.broadcasted_iota(jnp.int32, (1, KW), 1)
        first = jnp.where((tile == 0) & (col < WINDOW), NEG_INF, 0.0)
        lane = lax.broadcasted_iota(jnp.int32, (QT, LANES), 1)
        lo = lane < HEAD_DIM
        lane_k = lax.broadcasted_iota(jnp.int32, (KW, LANES), 1)
        lane_1 = lax.broadcasted_iota(jnp.int32, (1, LANES), 1)
        srow = lax.broadcasted_iota(jnp.int32, (REP * QT, 1), 0)
        lse_t = lse_ref[...]
        dsink = jnp.zeros((1, LANES), F32)
        for g in range(N_KV):
            kwin = jnp.concatenate([kp_ref[:, pl.ds(g * LANES, LANES)], kc_ref[:, pl.ds(g * LANES, LANES)]], axis=0)
            vwin = jnp.concatenate([vp_ref[:, pl.ds(g * LANES, LANES)], vc_ref[:, pl.ds(g * LANES, LANES)]], axis=0)
            qst = _stack_heads(q_ref, g, lo, SCALE)
            dost = _stack_heads(do_ref, g, lo)
            od = [do_ref[:, pl.ds((2 * g + sl) * LANES, LANES)].astype(F32)
                  * o_ref[:, pl.ds((2 * g + sl) * LANES, LANES)].astype(F32) for sl in range(2)]
            drow = jnp.concatenate([jnp.sum(jnp.where(lo if hl % 2 == 0 else jnp.logical_not(lo), od[hl // 2], 0.0),
                                            axis=-1, keepdims=True) for hl in range(REP)], axis=0)
            lse_s = jnp.concatenate([jnp.sum(jnp.where(lane == REP * g + hl, lse_t, 0.0), axis=-1, keepdims=True)
                                     for hl in range(REP)], axis=0)
            s = _nt(qst, kwin) + (bias_ref[g] + first)
            p = jnp.exp(s - lse_s)
            ds = p * (_nt(dost, vwin) - drow)
            ds_ref[g] += ds
            tsink = -(jnp.exp(_stack_sinks(sink_ref, g, srow) - lse_s) * drow)
            for hl in range(REP):
                dsink = dsink + jnp.where(lane_1 == REP * g + hl,
                                          jnp.sum(tsink[hl * QT:(hl + 1) * QT], axis=0, keepdims=True), 0.0)
            dsb = ds.astype(BF16)
            dqst = _nn(dsb, kwin) * SCALE
            for sl in range(2):
                dqkv_ref[:, pl.ds((2 * g + sl) * LANES, LANES)] = jnp.where(
                    lo, dqst[2 * sl * QT:(2 * sl + 1) * QT], dqst[(2 * sl + 1) * QT:(2 * sl + 2) * QT]).astype(BF16)
            dk_acc = _tn(dsb, qst)
            dv_acc = _tn(p.astype(BF16), dost)
            dk_f = jnp.where(lane_k < HEAD_DIM, dk_acc + pltpu.roll(dk_acc, HEAD_DIM, 1), 0.0)
            dv_f = jnp.where(lane_k < HEAD_DIM, dv_acc + pltpu.roll(dv_acc, HEAD_DIM, 1), 0.0)
            for acc, col0, car in ((dk_f, K0, kcar_s), (dv_f, V0, vcar_s)):
                cs = pl.ds(g * LANES, LANES)
                co = pl.ds(col0 + g * LANES, LANES)
                if QT > WINDOW:
                    dqkv_ref[pl.ds(0, QT - WINDOW), co] = acc[WINDOW:QT].astype(BF16)
                dqkv_ref[pl.ds(QT - WINDOW, WINDOW), co] = (acc[QT:KW] + car[:, cs]).astype(BF16)
                car[:, cs] = acc[0:WINDOW]
        dsink_ref[...] += dsink

    qs, cur, prev, lse_s = _attn_specs(nt, True)
    return _call(
        body, name="attn_bwd", grid=(nt,), ins=[q, kd, kd, vd, vd, o, do, lse, bias, sinks],
        in_specs=[qs, prev, cur, prev, cur, qs, qs, lse_s, _const_spec((N_KV, REP * QT, KW)),
                  pl.BlockSpec(memory_space=pltpu.SMEM)],
        out_specs=[pl.BlockSpec((QT, XR0), lambda i: (nt - 1 - i, 0)), _const_spec((N_KV, REP * QT, KW)),
                   _const_spec((1, LANES))],
        out_shape=[jax.ShapeDtypeStruct((t_tok, XR0), BF16), jax.ShapeDtypeStruct((N_KV, REP * QT, KW), F32),
                   jax.ShapeDtypeStruct((1, LANES), F32)],
        scratch=[pltpu.VMEM((WINDOW, kvw), F32), pltpu.VMEM((WINDOW, kvw), F32)],
        vmem=VMEM_BIG, carries=carries)


def _bias_tile(table, bmap):
    def body(tab_ref, bm_ref, out_ref):
        bm = bm_ref[...]

        def per_head(hd, carry):
            acc = jnp.full((QT, KW), NEG_INF, F32)
            for b in range(N_BUCKETS):
                acc = jnp.where(bm == b, tab_ref[b, hd], acc)
            out_ref[hd] = acc
            return carry

        lax.fori_loop(0, N_HEADS, per_head, 0)

    return pl.pallas_call(
        body, name="bias_tile", out_shape=jax.ShapeDtypeStruct((N_HEADS, QT, KW), F32),
        in_specs=[pl.BlockSpec(memory_space=pltpu.SMEM), pl.BlockSpec(memory_space=pltpu.VMEM)],
        out_specs=pl.BlockSpec(memory_space=pltpu.VMEM))(table, bmap)


def _bias_grad(ds_acc, bmap):
    def body(ds_ref, bm_ref, out_ref):
        row = lax.broadcasted_iota(jnp.int32, (N_BUCKETS, LANES), 0)
        lane = lax.broadcasted_iota(jnp.int32, (N_BUCKETS, LANES), 1)
        bm = bm_ref[...]

        def per_head(hd, res):
            dsv = ds_ref[hd]
            for b in range(N_BUCKETS):
                val = jnp.sum(jnp.sum(jnp.where(bm == b, dsv, 0.0), axis=0, keepdims=True), axis=1, keepdims=True)
                res = jnp.where((row == b) & (lane == hd), val, res)
            return res

        out_ref[...] = lax.fori_loop(0, N_HEADS, per_head, jnp.zeros((N_BUCKETS, LANES), F32))

    return pl.pallas_call(
        body, name="bias_grad", out_shape=jax.ShapeDtypeStruct((N_BUCKETS, LANES), F32),
        in_specs=[pl.BlockSpec(memory_space=pltpu.VMEM), pl.BlockSpec(memory_space=pltpu.VMEM)],
        out_specs=pl.BlockSpec(memory_space=pltpu.VMEM))(ds_acc, bmap)


MIXOUT = ("w_lru_out", "w_attn_out", "w_o")


def _mix_out_fwd(ya_in, o, gs, h, g, wg, tm=512, carries=()):
    t_tok = h.shape[0]
    nt = t_tok // tm
    groups = _groups_of(MIXOUT)

    def body(ya_ref, o_ref, gs_ref, h_ref, g_ref, wg_ref, hout_ref, yao_ref, ybo_ref, z_ref, wa_s, wb_s, wo_s, sems):
        i = pl.program_id(0)

        @pl.when(i == 0)
        def _():
            _load_weights({groups[0]: wg_ref}, list(zip(MIXOUT, (wa_s, wb_s, wo_s))), sems)

        ya = _nn(ya_ref[...], wa_s[...])
        yb = _nn(o_ref[...], wb_s[...])
        yao_ref[...] = ya.astype(BF16)
        ybo_ref[...] = yb.astype(BF16)
        merged = gs_ref[:, pl.ds(0, D)].astype(F32) * ya + gs_ref[:, pl.ds(D, D)].astype(F32) * yb
        z = _nn(merged.astype(BF16), wo_s[...])
        z_ref[...] = z
        zn, _, _ = _rms_fwd(z, g_ref[...])
        hout_ref[...] = h_ref[...] + zn

    return _call(
        body, name="mix_out_fwd", grid=(nt,), ins=[ya_in, o, gs, h, g, wg[groups[0]]],
        in_specs=[_row_spec(tm, D), _row_spec(tm, D), _row_spec(tm, 2 * D), _row_spec(tm, D), _const_spec((1, D)), ANY],
        out_specs=[_row_spec(tm, D)] * 4,
        out_shape=[jax.ShapeDtypeStruct((t_tok, D), F32), jax.ShapeDtypeStruct((t_tok, D), BF16),
                   jax.ShapeDtypeStruct((t_tok, D), BF16), jax.ShapeDtypeStruct((t_tok, D), F32)],
        scratch=[pltpu.VMEM((D, D), BF16)] * 3 + [DMA((3 * NDEV,))],
        vmem=48 * 2 ** 20, carries=carries)


def _mix_out_bwd(dh, z, ya, yb, gs, g, wg, tm=512, carries=()):
    t_tok = dh.shape[0]
    nt = t_tok // tm
    groups = _groups_of(MIXOUT)

    def body(dh_ref, z_ref, ya_ref, yb_ref, gs_ref, g_ref, wg_ref,
             dyain_ref, do_ref, dgpre_ref, dya_ref, dyb_ref, mg_ref, dz_ref, dg_ref, dbg_ref,
             wa_s, wb_s, wo_s, sems):
        i = pl.program_id(0)

        @pl.when(i == 0)
        def _():
            _load_weights({groups[0]: wg_ref}, list(zip(MIXOUT, (wa_s, wb_s, wo_s))), sems)
            dg_ref[...] = jnp.zeros_like(dg_ref)
            dbg_ref[...] = jnp.zeros_like(dbg_ref)

        gv = g_ref[...]
        _, zh, r = _rms_fwd(z_ref[...], gv)
        dz, dg = _rms_bwd(dh_ref[...], zh, r, gv)
        dg_ref[...] += dg
        dzb = dz.astype(BF16)
        dz_ref[...] = dzb
        ga, gb = gs_ref[:, pl.ds(0, D)].astype(F32), gs_ref[:, pl.ds(D, D)].astype(F32)
        ya_v, yb_v = ya_ref[...].astype(F32), yb_ref[...].astype(F32)
        mg_ref[...] = (ga * ya_v + gb * yb_v).astype(BF16)
        dm = _nt(dzb, wo_s[...])
        dga = dm * ya_v * ga * (1.0 - ga)
        dgb = dm * yb_v * gb * (1.0 - gb)
        dgpre_ref[:, pl.ds(0, D)] = dga.astype(BF16)
        dgpre_ref[:, pl.ds(D, D)] = dgb.astype(BF16)
        dbg_ref[:, pl.ds(0, D)] += jnp.sum(dga, axis=0, keepdims=True)
        dbg_ref[:, pl.ds(D, D)] += jnp.sum(dgb, axis=0, keepdims=True)
        dya = (dm * ga).astype(BF16)
        dyb = (dm * gb).astype(BF16)
        dya_ref[...] = dya
        dyb_ref[...] = dyb
        dyain_ref[...] = _nt(dya, wa_s[...])
        do_ref[...] = _nt(dyb, wb_s[...]).astype(BF16)

    bf = jax.ShapeDtypeStruct((t_tok, D), BF16)
    return _call(
        body, name="mix_out_bwd", grid=(nt,), ins=[dh, z, ya, yb, gs, g, wg[groups[0]]],
        in_specs=[_row_spec(tm, D)] * 4 + [_row_spec(tm, 2 * D), _const_spec((1, D)), ANY],
        out_specs=[_row_spec(tm, D), _row_spec(tm, D), _row_spec(tm, 2 * D)] + [_row_spec(tm, D)] * 4
        + [_const_spec((1, D)), _const_spec((1, 2 * D))],
        out_shape=[jax.ShapeDtypeStruct((t_tok, D), F32), bf, jax.ShapeDtypeStruct((t_tok, 2 * D), BF16), bf, bf, bf, bf,
                   jax.ShapeDtypeStruct((1, D), F32), jax.ShapeDtypeStruct((1, 2 * D), F32)],
        scratch=[pltpu.VMEM((D, D), BF16)] * 3 + [DMA((3 * NDEV,))],
        vmem=VMEM_BIG, carries=carries)


def _sum_parts(parts_list):
    n = len(parts_list)
    _, r, c = parts_list[0].shape
    tc = 256

    def body(*refs):
        for p_ref, o_ref in zip(refs[:n], refs[n:]):
            acc = p_ref[0].astype(F32)
            for s in range(1, NDEV):
                acc = acc + p_ref[s].astype(F32)
            o_ref[...] = acc

    return pl.pallas_call(
        body, name=f"sum_parts_{r}", grid=(c // tc,),
        in_specs=[pl.BlockSpec((NDEV, r, tc), lambda i: (0, 0, i))] * n,
        out_specs=[pl.BlockSpec((r, tc), lambda i: (0, i))] * n,
        out_shape=[jax.ShapeDtypeStruct((r, c), F32)] * n,
        compiler_params=pltpu.CompilerParams(dimension_semantics=("arbitrary",), vmem_limit_bytes=48 * 2 ** 20),
    )(*parts_list)


def _adamw_math(w, g, m, v):
    m = ADAM_B1 * m + (1.0 - ADAM_B1) * g
    v = ADAM_B2 * v + (1.0 - ADAM_B2) * (g * g)
    m_hat = m / (1.0 - ADAM_B1 ** ADAM_STEP)
    v_hat = v / (1.0 - ADAM_B2 ** ADAM_STEP)
    delta = -ADAM_LR * (m_hat / (jnp.sqrt(v_hat) + ADAM_EPS) + ADAM_WD * w)
    return delta, m, v


def _sum_ready(parts_list, name, carries=(), chip_sums=()):
    n = len(parts_list)
    c = parts_list[0].shape[2]
    tc = 2 * LANES

    def body(*refs):
        for k, (p_ref, o_ref) in enumerate(zip(refs[:n], refs[n:])):
            if k in chip_sums:
                core = lax.axis_index("c")
                slots = [2 * ch + core for ch in range(NDEV // 2)]
            else:
                slots = list(range(NDEV))
            g = p_ref[slots[0]].astype(F32)
            for s in slots[1:]:
                g = g + p_ref[s].astype(F32)
            o_ref[...] = g

    return _call(
        body, name=name, grid=(c // tc,), ins=list(parts_list),
        in_specs=[pl.BlockSpec((NDEV, p.shape[1], tc), lambda i: (0, 0, i)) for p in parts_list],
        out_shape=[jax.ShapeDtypeStruct((p.shape[1], c), F32) for p in parts_list],
        out_specs=[pl.BlockSpec((p.shape[1], tc), lambda i: (0, i)) for p in parts_list],
        vmem=48 * 2 ** 20, carries=carries)


def _adamw_cols(items, name):
    n = len(items)
    c = items[0][1].shape[1]
    tc = LANES

    def body(*refs):
        for k in range(n):
            g_ref, w_ref, m_ref, v_ref = refs[4 * k:4 * k + 4]
            go_ref, d_ref, nm_ref, nv_ref = refs[4 * n + 4 * k:4 * n + 4 * k + 4]
            g = g_ref[...]
            d, m, v = _adamw_math(w_ref[...], g, m_ref[...], v_ref[...])
            go_ref[...] = g
            d_ref[...] = d
            nm_ref[...] = m
            nv_ref[...] = v

    ins, specs, out_shape = [], [], []
    for g, w, m, v in items:
        r = w.shape[0]
        ins += [g, w, m, v]
        specs += [pl.BlockSpec((r, tc), lambda i: (0, i))] * 4
        out_shape += [jax.ShapeDtypeStruct((r, c), F32)] * 4
    outs, _ = _call(body, name=name, grid=(c // tc,), ins=ins, in_specs=specs, out_shape=out_shape,
                    out_specs=specs, vmem=48 * 2 ** 20)
    return [tuple(outs[4 * k:4 * k + 4]) for k in range(n)]


def _adamw_body(n):
    def body(*refs):
        for k in range(n):
            g_ref, w_ref, m_ref, v_ref = refs[4 * k:4 * k + 4]
            d_ref, nm_ref, nv_ref = refs[4 * n + 3 * k:4 * n + 3 * k + 3]
            d, m, v = _adamw_math(w_ref[...], g_ref[...], m_ref[...], v_ref[...])
            d_ref[...] = d
            nm_ref[...] = m
            nv_ref[...] = v
    return body


def _adamw(items):
    n = len(items)
    r, c = items[0][0].shape
    tr = r if r * c <= 2 ** 18 else max(t for t in range(8, 65, 8) if r % t == 0)
    spec = pl.BlockSpec((tr, c), lambda i: (i, 0))
    outs = pl.pallas_call(
        _adamw_body(n), name=f"adamw_{r}x{c}", grid=(r // tr,),
        in_specs=[spec] * (4 * n), out_specs=[spec] * (3 * n),
        out_shape=[jax.ShapeDtypeStruct((r, c), F32)] * (3 * n),
        compiler_params=pltpu.CompilerParams(dimension_semantics=("arbitrary",), vmem_limit_bytes=40 * 2 ** 20),
    )(*[a for it in items for a in it])
    return [tuple(outs[3 * k:3 * k + 3]) for k in range(n)]


def _adamw_small(items):
    n = len(items)
    vm = pl.BlockSpec(memory_space=pltpu.VMEM)
    outs = pl.pallas_call(
        _adamw_body(n), name="adamw_small", in_specs=[vm] * (4 * n), out_specs=[vm] * (3 * n),
        out_shape=[jax.ShapeDtypeStruct(it[1].shape, F32) for it in items for _ in range(3)],
    )(*[a for it in items for a in it])
    return [tuple(outs[3 * k:3 * k + 3]) for k in range(n)]


def _pack_small(arrs):
    rows, offs = [], []
    total = 0
    for a in arrs:
        flat = a.reshape(-1).astype(F32)
        nr = -(-flat.shape[0] // LANES)
        flat = jnp.pad(flat, (0, nr * LANES - flat.shape[0]))
        rows.append(flat.reshape(nr, LANES))
        offs.append((total, nr))
        total += nr
    pad = -total % 8
    if pad:
        rows.append(jnp.zeros((pad, LANES), F32))
    return jnp.concatenate(rows, axis=0), offs


def _unpack_small(pack, offs, shapes):
    out = []
    for (o, nr), shp in zip(offs, shapes):
        size = int(np.prod(shp))
        out.append(pack[o:o + nr].reshape(-1)[:size].reshape(shp))
    return out


def _sum_small(gathered, rows):
    def body(p_ref, o_ref):
        acc = p_ref[pl.ds(0, rows), :]
        for s in range(1, NDEV):
            acc = acc + p_ref[pl.ds(s * rows, rows), :]
        o_ref[...] = acc

    return pl.pallas_call(
        body, name="sum_small", out_shape=jax.ShapeDtypeStruct((rows, LANES), F32),
        in_specs=[pl.BlockSpec(memory_space=pltpu.VMEM)], out_specs=pl.BlockSpec(memory_space=pltpu.VMEM))(gathered)


def _block_diag(w):
    w = w.reshape(D // LANES, 2, LRU_BLOCK, LRU_BLOCK)
    z = jnp.zeros((D // LANES, LRU_BLOCK, LRU_BLOCK), w.dtype)
    top = jnp.concatenate([w[:, 0], z], axis=2)
    bot = jnp.concatenate([z, w[:, 1]], axis=2)
    return jnp.concatenate([top, bot], axis=1)


def _block_diag_grad(dw):
    a = dw[:, :LRU_BLOCK, :LRU_BLOCK]
    b = dw[:, LRU_BLOCK:, LRU_BLOCK:]
    return jnp.stack([a, b], axis=1).reshape(D // LRU_BLOCK, LRU_BLOCK, LRU_BLOCK)


def kernel(x, ffn1_pre_g, ffn1_w1, ffn1_w3, ffn1_w2, ffn1_post_g, mix_pre_g, w_in, conv_w, conv_b, rg_a_w, rg_a_b, rg_x_w, rg_x_b, lru_lambda, w_lru_out, attn_sinks, rel_bias, w_attn_out, w_gate, b_gate, w_o, mix_post_g, ffn2_pre_g, ffn2_w1, ffn2_w3, ffn2_w2, ffn2_post_g, loss_target, m_ffn1_pre_g, m_ffn1_w1, m_ffn1_w3, m_ffn1_w2, m_ffn1_post_g, m_mix_pre_g, m_w_in, m_conv_w, m_conv_b, m_rg_a_w, m_rg_a_b, m_rg_x_w, m_rg_x_b, m_lru_lambda, m_w_lru_out, m_attn_sinks, m_rel_bias, m_w_attn_out, m_w_gate, m_b_gate, m_w_o, m_mix_post_g, m_ffn2_pre_g, m_ffn2_w1, m_ffn2_w3, m_ffn2_w2, m_ffn2_post_g, v_ffn1_pre_g, v_ffn1_w1, v_ffn1_w3, v_ffn1_w2, v_ffn1_post_g, v_mix_pre_g, v_w_in, v_conv_w, v_conv_b, v_rg_a_w, v_rg_a_b, v_rg_x_w, v_rg_x_b, v_lru_lambda, v_w_lru_out, v_attn_sinks, v_rel_bias, v_w_attn_out, v_w_gate, v_b_gate, v_w_o, v_mix_post_g, v_ffn2_pre_g, v_ffn2_w1, v_ffn2_w3, v_ffn2_w2, v_ffn2_post_g):
    names = ["ffn1_pre_g", "ffn1_w1", "ffn1_w3", "ffn1_w2", "ffn1_post_g", "mix_pre_g", "w_in", "conv_w", "conv_b",
             "rg_a_w", "rg_a_b", "rg_x_w", "rg_x_b", "lru_lambda", "w_lru_out", "attn_sinks", "rel_bias", "w_attn_out",
             "w_gate", "b_gate", "w_o", "mix_post_g", "ffn2_pre_g", "ffn2_w1", "ffn2_w3", "ffn2_w2", "ffn2_post_g"]
    ws = dict(zip(names, (ffn1_pre_g, ffn1_w1, ffn1_w3, ffn1_w2, ffn1_post_g, mix_pre_g, w_in, conv_w, conv_b, rg_a_w,
                          rg_a_b, rg_x_w, rg_x_b, lru_lambda, w_lru_out, attn_sinks, rel_bias, w_attn_out, w_gate,
                          b_gate, w_o, mix_post_g, ffn2_pre_g, ffn2_w1, ffn2_w3, ffn2_w2, ffn2_post_g)))
    ms = dict(zip(names, (m_ffn1_pre_g, m_ffn1_w1, m_ffn1_w3, m_ffn1_w2, m_ffn1_post_g, m_mix_pre_g, m_w_in, m_conv_w,
                          m_conv_b, m_rg_a_w, m_rg_a_b, m_rg_x_w, m_rg_x_b, m_lru_lambda, m_w_lru_out, m_attn_sinks,
                          m_rel_bias, m_w_attn_out, m_w_gate, m_b_gate, m_w_o, m_mix_post_g, m_ffn2_pre_g, m_ffn2_w1,
                          m_ffn2_w3, m_ffn2_w2, m_ffn2_post_g)))
    vs = dict(zip(names, (v_ffn1_pre_g, v_ffn1_w1, v_ffn1_w3, v_ffn1_w2, v_ffn1_post_g, v_mix_pre_g, v_w_in, v_conv_w,
                          v_conv_b, v_rg_a_w, v_rg_a_b, v_rg_x_w, v_rg_x_b, v_lru_lambda, v_w_lru_out, v_attn_sinks,
                          v_rel_bias, v_w_attn_out, v_w_gate, v_b_gate, v_w_o, v_mix_post_g, v_ffn2_pre_g, v_ffn2_w1,
                          v_ffn2_w3, v_ffn2_w2, v_ffn2_post_g)))
    me = 4 * lax.axis_index("x") + 2 * lax.axis_index("y") + lax.axis_index("c")
    vec = lambda n: ws[n].reshape(1, -1)

    def shard2d(name):
        if name == "conv":
            row = lax.bitcast_convert_type(conv_w.reshape(CONV_WIDTH, LANES), BF16).reshape(1, D)
            return jnp.concatenate([row, jnp.zeros((LOC["conv"][2] - 1, D), BF16)], axis=0)
        a = ws[name].reshape(ws[name].shape[-2], ws[name].shape[-1])
        return (a.T if name in COL_SHARDED else a).astype(BF16)

    packs = {g: jnp.concatenate([shard2d(n) for n, _ in members], axis=0) for g, members in GROUPS}

    wg = {}
    _, ((wg["ffn1"],),) = _call(None, name="allgather_ffn1", grid=(), ins=[], in_specs=[], out_shape=[], out_specs=[],
                                carries=[_AllGatherTree(packs["ffn1"])])
    conv_rows = wg["ffn1"].reshape(NDEV, GROUP_ROWS["ffn1"], D)[:, LOC["conv"][1]]
    cw = jnp.transpose(lax.bitcast_convert_type(conv_rows.reshape(NDEV, CONV_WIDTH, LANES, 2), F32),
                       (1, 0, 2)).reshape(CONV_WIDTH, D)
    bmap = jnp.asarray(_bucket_map())
    bias = _bias_tile(rel_bias, bmap).reshape(N_KV, REP * QT, KW)
    sinks = attn_sinks.reshape(N_HEADS)
    wa_bd = _block_diag(rg_a_w.reshape(D // LRU_BLOCK, LRU_BLOCK, LRU_BLOCK)).astype(BF16)
    wx_bd = _block_diag(rg_x_w.reshape(D // LRU_BLOCK, LRU_BLOCK, LRU_BLOCK)).astype(BF16)
    lru_args = (cw, vec("conv_b"), wa_bd, wx_bd, vec("rg_a_b"), vec("rg_x_b"), vec("lru_lambda"))
    x2, tgt = x[0], loss_target[0]

    (h1, a1, b1, f1, nb1), ((wg["mixin"],),) = _ffn_fwd(
        x2, vec("ffn1_pre_g"), vec("ffn1_post_g"), wg, ("ffn1_w1", "ffn1_w3", "ffn1_w2"),
        carries=[_AllGather(packs["mixin"])])
    (q, kd, vd, xr, xg, gs, ub), ((wg["mixout"],),) = _mix_proj_fwd(
        h1, vec("mix_pre_g"), vec("b_gate"), wg, carries=[_AllGather(packs["mixout"])])
    (ya_in, hseq), ((wg["ffn2a"],),) = _lru_fwd(xr, xg, *lru_args, carries=[_AllGather(packs["ffn2a"])])
    (o, lse), ((wg["ffn2b"],),) = _attn_fwd(q, kd, vd, bias, sinks, carries=[_AllGather(packs["ffn2b"])])
    (h2, ya, yb, z), _ = _mix_out_fwd(ya_in, o, gs, h1, vec("mix_post_g"), wg)
    (_, a2, b2, f2, nb2, dy, loss_part), _ = _ffn_fwd(h2, vec("ffn2_pre_g"), vec("ffn2_post_g"), wg,
                                                     ("ffn2_w1", "ffn2_w3", "ffn2_w2"), target=tgt)

    gsm, parts = {}, {}
    rs = lambda *grads_: [_ReduceScatterSend(list(grads_))]
    ffr = FF // NDEV
    (dab, s_act, dfb, gsm["ffn2_post_g"]), _ = _ffn_bwd_a(dy, f2, a2, b2, vec("ffn2_post_g"), wg, "ffn2_w2")
    g_w2, _ = _dw(s_act, dfb, FF // 2, "dw_ffn2_w2")
    g_w13, ((parts["ffn2_w2"],),) = _dw(dab, nb2, D // 2, "dw_ffn2_w13", carries=rs(g_w2))
    (dh2, gsm["ffn2_pre_g"]), ((parts["ffn2_w1"],),) = _ffn_bwd_dx(
        dab, h2, dy, vec("ffn2_pre_g"), wg, "ffn2_w1", "ffn2_w3", carries=rs((g_w13, 0, ffr, ffr)))
    (dya_in, do, dgpre, dya, dyb, mg, dzb, gsm["mix_post_g"], gsm["b_gate"]), ((parts["ffn2_w3"],),) = _mix_out_bwd(
        dh2, z, ya, yb, gs, vec("mix_post_g"), wg, carries=rs((g_w13, FF, ffr, ffr)))
    g_wa, _ = _dw(ya_in, dya, D // 2, "dw_w_lru_out")
    g_wb, _ = _dw(o, dyb, D // 2, "dw_w_attn_out")
    g_wo, _ = _dw(mg, dzb, D // 2, "dw_w_o")
    g_wgate, _ = _dw(dgpre, ub, D // 2, "dw_w_gate")
    (dqkv, ds_acc, dsink), (mixout_parts,) = _attn_bwd(
        q, kd, vd, o, do, lse, bias, sinks, carries=rs(g_wa, g_wb, g_wo, g_wgate))
    parts["w_lru_out"], parts["w_attn_out"], parts["w_o"], parts["w_gate"] = mixout_parts
    (dxr, dxg, dvec, dwa, dwx), _ = _lru_bwd(dya_in, xr, xg, hseq, *lru_args)
    g_qkv, _ = _dw(dqkv, ub, D // 2, "dw_w_in_qkv")
    g_xr, _ = _dw(dxr, ub, D // 2, "dw_w_in_xr")
    g_xg, _ = _dw(dxg, ub, D // 2, "dw_w_in_xg")
    g_win = jnp.concatenate(
        [g_qkv[:D]] + [g_qkv[c0 + gi * LANES:c0 + gi * LANES + HEAD_DIM] for c0 in (K0, V0) for gi in range(N_KV)]
        + [g_xr, g_xg], axis=0)
    wir, wir_a = IN_W // NDEV, 288
    (dh1, gsm["mix_pre_g"]), ((win_a,),) = _mix_proj_bwd(
        dqkv, dxr, dxg, dgpre, h1, dh2, vec("mix_pre_g"), wg, carries=rs((g_win, 0, wir, wir_a)))
    gsm["conv_w"] = dvec[0:CONV_WIDTH]
    gsm["conv_b"], gsm["rg_a_b"], gsm["rg_x_b"], gsm["lru_lambda"] = dvec[4], dvec[5], dvec[6], dvec[7]
    gsm["rg_a_w"] = _block_diag_grad(dwa)
    gsm["rg_x_w"] = _block_diag_grad(dwx)
    gsm["attn_sinks"] = dsink[0, :N_HEADS]
    gsm["rel_bias"] = _bias_grad(ds_acc.reshape(N_HEADS, QT, KW), bmap)[:, :N_HEADS]
    late = ("ffn1_post_g", "ffn1_pre_g")
    early = tuple(n for n in SMALL if n not in late)
    early_pack, early_offs = _pack_small([gsm[n] for n in early])
    (dab, s_act, dfb, gsm["ffn1_post_g"]), ((early_all,),) = _ffn_bwd_a(
        dh1, f1, a1, b1, vec("ffn1_post_g"), wg, "ffn1_w2", carries=[_AllGather(early_pack)])
    g_w2, ((win_b,),) = _dw(s_act, dfb, FF // 2, "dw_ffn1_w2", carries=rs((g_win, wir_a, wir, wir - wir_a)))
    parts["w_in"] = jnp.concatenate([win_a, win_b], axis=1)
    g_w13, ((parts["ffn1_w2"],),) = _dw(dab, nb1, D // 2, "dw_ffn1_w13", carries=rs(g_w2))
    chip_w13 = _pair_sum(g_w13, [(0, ffr), (FF, ffr)])
    (grad_x, gsm["ffn1_pre_g"]), ((parts["ffn1_w1"], parts["ffn1_w3"]),) = _ffn_bwd_dx(
        dab, x2, dh1, vec("ffn1_pre_g"), wg, "ffn1_w1", "ffn1_w3",
        carries=[_ReduceScatterSend([(chip_w13, 0, ffr, ffr), (chip_w13, FF, ffr, ffr)], chip_sums=True)])
    late_pack, late_offs = _pack_small([gsm[n] for n in late] + [loss_part])

    grads, delta, new_m, new_v = {}, {}, {}, {}
    t_form = [n for n in COL_SHARDED if ws[n].shape[-1] % LANES]
    view = {n: (lambda a: a.reshape(a.shape[-2], a.shape[-1]).T) if n in t_form
            else (lambda a: a.reshape(a.shape[-2], a.shape[-1])) for n in BIG}
    unview = {n: (lambda a: a.T) if n in t_form else (lambda a: a) for n in BIG}

    group = [n for n in BIG if n != "w_gate"]
    sums, ((late_all,),) = _sum_ready([parts[n] for n in group], "sum_ready", [_AllGather(late_pack)],
                                      chip_sums=(group.index("ffn1_w1"), group.index("ffn1_w3")))
    res = _adamw_cols([(g, view[n](ws[n]), view[n](ms[n]), view[n](vs[n])) for n, g in zip(group, sums)], "adamw_big")
    for n, quad in zip(group, res):
        grads[n], delta[n], new_m[n], new_v[n] = (unview[n](a) for a in quad)
    (g_gate_t,) = _sum_parts([parts["w_gate"]])
    grads["w_gate"] = g_gate_t.T
    ((delta["w_gate"], new_m["w_gate"], new_v["w_gate"]),) = _adamw(
        [(grads["w_gate"], view["w_gate"](ws["w_gate"]), view["w_gate"](ms["w_gate"]), view["w_gate"](vs["w_gate"]))])
    for group, gathered, offs in ((early, early_all, early_offs), (late, late_all, late_offs)):
        total = _sum_small(gathered, gathered.shape[0] // NDEV)
        shapes = [(CONV_WIDTH, D) if n == "conv_w" else ws[n].shape for n in group]
        if group is late:
            shapes = shapes + [(1, LANES)]
        unpacked = _unpack_small(total, offs, shapes)
        if group is late:
            loss = unpacked.pop()[0, 0]
        for n, g in zip(group, unpacked):
            grads[n] = g
    grads["conv_w"] = lax.dynamic_slice(grads["conv_w"], (0, me * LANES), (CONV_WIDTH, LANES)).reshape(conv_w.shape)

    flat2d = lambda a: a.reshape(-1, a.shape[-1])
    res = _adamw_small([(flat2d(grads[n].reshape(ws[n].shape)), flat2d(ws[n]), flat2d(ms[n]), flat2d(vs[n]))
                        for n in SMALL])
    for n, (d_, m_, v_) in zip(SMALL, res):
        delta[n], new_m[n], new_v[n] = d_, m_, v_

    outs = [loss, grad_x.reshape(x.shape)]
    for src in (grads, delta, new_m, new_v):
        outs += [src[n].reshape(ws[n].shape) for n in names]
    return tuple(outs)
```

```python
import functools
import math
import operator

import numpy as np
import jax
import jax.numpy as jnp
from jax import lax
from jax.experimental import pallas as pl
from jax.experimental.pallas import tpu as pltpu

F32, BF16 = jnp.float32, jnp.bfloat16

NDEV = 8
D = 1024
FF = 2816
N_HEADS, N_KV, HEAD_DIM = 16, 4, 64
CHUNK, WINDOW = 64, 128
N_BUCKETS, MAX_DISTANCE = 32, 128
LRU_BLOCK = 64
CONV_WIDTH = 4
LRU_C = 8.0
RMS_EPS = 1e-6
NEG_INF = -1e30
LANES = 128
QT = 128
KW = QT + WINDOW
LRU_ROWS = 1024
IN_W = D + 2 * N_KV * HEAD_DIM + 2 * D
INP_W = D + 2 * N_KV * LANES + 2 * D
VMEM_BIG = 58 * 2 ** 20

ADAM_LR, ADAM_B1, ADAM_B2, ADAM_EPS, ADAM_WD, ADAM_STEP = 0.001, 0.9, 0.999, 1e-08, 0.01, 10

GROUPS = (("ffn1", (("ffn1_w1", FF // NDEV), ("ffn1_w3", FF // NDEV), ("ffn1_w2", FF // NDEV), ("conv", 16))),
          ("mixin", (("w_in", IN_W // NDEV), ("w_gate", 2 * D // NDEV))),
          ("mixout", (("w_lru_out", D // NDEV), ("w_attn_out", D // NDEV), ("w_o", D // NDEV))),
          ("ffn2a", (("ffn2_w1", FF // NDEV), ("ffn2_w3", FF // NDEV))),
          ("ffn2b", (("ffn2_w2", FF // NDEV),)))
LOC, GROUP_ROWS = {}, {}
for _g, _members in GROUPS:
    _o = 0
    for _n, _r in _members:
        LOC[_n] = (_g, _o, _r)
        _o += _r
    GROUP_ROWS[_g] = _o
BIG = tuple(n for _, members in GROUPS for n, _ in members if n != "conv")
COL_SHARDED = ("ffn1_w1", "ffn1_w3", "w_in", "w_gate", "ffn2_w1", "ffn2_w3")

SMALL = ("ffn1_pre_g", "ffn1_post_g", "mix_pre_g", "conv_w", "conv_b", "rg_a_w", "rg_a_b", "rg_x_w", "rg_x_b",
         "lru_lambda", "attn_sinks", "rel_bias", "b_gate", "mix_post_g", "ffn2_pre_g", "ffn2_post_g")

MESH = pl.DeviceIdType.MESH
ANY = pl.BlockSpec(memory_space=pl.ANY)
DMA = pltpu.SemaphoreType.DMA


def _nn(a, b):
    return lax.dot_general(a, b, (((1,), (0,)), ((), ())), preferred_element_type=F32)


def _nt(a, b):
    return lax.dot_general(a, b, (((1,), (1,)), ((), ())), preferred_element_type=F32)


def _tn(a, b):
    return lax.dot_general(a, b, (((0,), (0,)), ((), ())), preferred_element_type=F32)


def _rms_fwd(x, g):
    r = lax.rsqrt(jnp.mean(x * x, axis=-1, keepdims=True) + RMS_EPS)
    xh = x * r
    return xh * g, xh, r


def _rms_bwd(dn, xh, r, g):
    dxh = dn * g
    dx = r * (dxh - xh * jnp.mean(dxh * xh, axis=-1, keepdims=True))
    return dx, jnp.sum(dn * xh, axis=0, keepdims=True)


def _row_spec(tm, c):
    return pl.BlockSpec((tm, c), lambda i: (i, 0))


def _const_spec(shape):
    nd = len(shape)
    return pl.BlockSpec(shape, lambda i: (0,) * nd)


class _AllGather:
    waits_for_all = True

    def __init__(self, shard):
        self.m, n = shard.shape
        self.ins = [shard]
        self.out_shape = [jax.ShapeDtypeStruct((NDEV * self.m, n), shard.dtype)]
        self.scratch = [DMA((7,)), DMA((7,)), DMA]

    def _copies(self, ins, outs, scr, all_of_them):
        x_ref, out_ref = ins[0], outs[0]
        send_sems, recv_sems, local_sem = scr
        x, y, c = lax.axis_index("x"), lax.axis_index("y"), lax.axis_index("c")
        me, sibling = (x, y, c), (x, y, 1 - c)
        chips = [(1 - x, y), (x, 1 - y), (1 - x, 1 - y)]
        m = self.m

        def rows(px, py, pc):
            return out_ref.at[pl.ds((4 * px + 2 * py + pc) * m, m), :]

        def copy(k, block, to, src=None):
            return pltpu.make_async_remote_copy(
                src_ref=rows(*block) if src is None else src, dst_ref=rows(*block),
                send_sem=send_sems.at[k], recv_sem=recv_sems.at[k], device_id=to, device_id_type=MESH)

        mine = pltpu.make_async_copy(x_ref, rows(*me), local_sem)
        first = [copy(0, me, sibling, src=x_ref)] + [copy(1 + j, me, (*chip, c), src=x_ref)
                                                     for j, chip in enumerate(chips)]
        if not all_of_them:
            return mine, first
        passed = [copy(4 + j, (*chip, c), sibling) for j, chip in enumerate(chips)]
        landed = [copy(1 + j, (*chip, c), me) for j, chip in enumerate(chips)]
        from_sibling = [copy(0, sibling, me)] + [copy(4 + j, (*chip, 1 - c), me) for j, chip in enumerate(chips)]
        return mine, first, passed, landed, from_sibling

    def start(self, ins, outs, scr):
        mine, first = self._copies(ins, outs, scr, False)
        mine.start()
        for cp in first:
            cp.start()

    def pass_on(self, ins, outs, scr):
        _, _, passed, landed, _ = self._copies(ins, outs, scr, True)
        for cp_in, cp_on in zip(landed, passed):
            cp_in.wait_recv()
            cp_on.start()

    def finish(self, ins, outs, scr):
        mine, first, passed, landed, from_sibling = self._copies(ins, outs, scr, True)
        for cp in from_sibling:
            cp.wait_recv()
        for cp in first + passed:
            cp.wait_send()
        mine.wait()


class _AllGatherTree:
    waits_for_all = True

    def __init__(self, shard):
        self.m, n = shard.shape
        self.half = self.m // 32 * 16
        self.ins = [shard]
        self.out_shape = [jax.ShapeDtypeStruct((NDEV * self.m, n), shard.dtype)]
        self.scratch = [DMA((9,)), DMA((9,)), DMA]

    def start(self, ins, outs, scr):
        x_ref, out_ref = ins[0], outs[0]
        send_sems, recv_sems, local_sem = scr
        x, y, c = lax.axis_index("x"), lax.axis_index("y"), lax.axis_index("c")
        m, half = self.m, self.half
        me, sibling, xn, yn, diag = (x, y, c), (x, y, 1 - c), (1 - x, y, c), (x, 1 - y, c), (1 - x, 1 - y, c)
        other = lambda dev: (dev[0], dev[1], 1 - c)

        def rows(dev, lo=0, n=m):
            return out_ref.at[pl.ds((4 * dev[0] + 2 * dev[1] + dev[2]) * m + lo, n), :]

        def copy(k, dev, to, lo=0, n=m, src=None):
            return pltpu.make_async_remote_copy(
                src_ref=rows(dev, lo, n) if src is None else src, dst_ref=rows(dev, lo, n),
                send_sem=send_sems.at[k], recv_sem=recv_sems.at[k], device_id=to, device_id_type=MESH)

        mine = pltpu.make_async_copy(x_ref, rows(me), local_sem)
        own = [copy(0, me, sibling, src=x_ref), copy(1, me, xn, src=x_ref), copy(2, me, yn, src=x_ref)]
        mine.start()
        for cp in own:
            cp.start()
        steps = [(copy(1, xn, me), [copy(3, xn, yn, 0, half), copy(5, xn, sibling)]),
                 (copy(2, yn, me), [copy(4, yn, xn, half, m - half), copy(6, yn, sibling)]),
                 (copy(3, diag, me, 0, half), [copy(7, diag, sibling, 0, half)]),
                 (copy(4, diag, me, half, m - half), [copy(8, diag, sibling, half, m - half)])]
        from_sibling = [copy(0, sibling, me), copy(5, other(xn), me), copy(6, other(yn), me),
                        copy(7, other(diag), me, 0, half), copy(8, other(diag), me, half, m - half)]
        passed = []
        for landing, onward in steps:
            landing.wait_recv()
            for cp in onward:
                cp.start()
            passed += onward
        for cp in from_sibling:
            cp.wait_recv()
        for cp in own + passed:
            cp.wait_send()
        mine.wait()

    def finish(self, ins, outs, scr):
        pass


class _ReduceScatterSend:
    def __init__(self, grads, chip_sums=False):
        grads = [g if isinstance(g, tuple) else (g, 0, g.shape[0] // NDEV, g.shape[0] // NDEV) for g in grads]
        self.peers = (2, 4, 6) if chip_sums else tuple(range(1, NDEV))
        self.waits_for_all = not chip_sums
        self.nw = len(grads)
        self.base = [b for _, b, _, _ in grads]
        self.stride = [s for _, _, s, _ in grads]
        self.rows = [r for _, _, _, r in grads]
        self.ins = [g for g, _, _, _ in grads]
        self.out_shape = [jax.ShapeDtypeStruct((NDEV, r, g.shape[1]), g.dtype) for g, _, _, r in grads]
        self.scratch = [DMA((self.nw, NDEV - 1)), DMA((self.nw, NDEV - 1)), DMA((self.nw,))]

    def _copies(self, g_refs, r_refs, scr, want):
        send_sems, recv_sems, local_sems = scr
        x, y, c = lax.axis_index("x"), lax.axis_index("y"), lax.axis_index("c")
        me = 4 * x + 2 * y + c
        rows, base, stride = self.rows, self.base, self.stride
        out = []
        if want == "local":
            for w in range(self.nw):
                out.append(pltpu.make_async_copy(g_refs[w].at[pl.ds(base[w] + me * stride[w], rows[w])],
                                                 r_refs[w].at[me], local_sems.at[w]))
            return out
        for k in self.peers:
            px, py, pc = x ^ (k >> 2), y ^ ((k >> 1) & 1), c ^ (k & 1)
            peer = 4 * px + 2 * py + pc
            for w in range(self.nw):
                sems = dict(send_sem=send_sems.at[w, k - 1], recv_sem=recv_sems.at[w, k - 1],
                            device_id=(px, py, pc), device_id_type=MESH)
                if want == "send":
                    out.append(pltpu.make_async_remote_copy(
                        src_ref=g_refs[w].at[pl.ds(base[w] + peer * stride[w], rows[w])], dst_ref=r_refs[w].at[me],
                        **sems))
                else:
                    out.append(pltpu.make_async_remote_copy(
                        src_ref=g_refs[w].at[pl.ds(0, rows[w])], dst_ref=r_refs[w].at[peer], **sems))
        return out

    def start(self, ins, outs, scr):
        for cp in self._copies(ins, outs, scr, "local") + self._copies(ins, outs, scr, "send"):
            cp.start()

    def finish(self, ins, outs, scr):
        for cp in self._copies(ins, outs, scr, "recv"):
            cp.wait_recv()
        for cp in self._copies(ins, outs, scr, "send"):
            cp.wait_send()
        for cp in self._copies(ins, outs, scr, "local"):
            cp.wait()


def _mesh_barrier():
    barrier = pltpu.get_barrier_semaphore()
    x, y, c = lax.axis_index("x"), lax.axis_index("y"), lax.axis_index("c")
    for k in range(1, NDEV):
        pl.semaphore_signal(barrier, inc=1, device_id=(x ^ (k >> 2), y ^ ((k >> 1) & 1), c ^ (k & 1)),
                            device_id_type=MESH)
    pl.semaphore_wait(barrier, NDEV - 1)


def _call(body, *, name, grid, ins, in_specs, out_shape, out_specs, scratch=(), vmem=None, carries=()):
    n_in, n_out, n_scr = len(ins), len(out_shape), len(scratch)
    ng = len(grid)
    own_barrier = any(getattr(c, "waits_for_all", False) for c in carries)

    def split(refs):
        pos = [0]

        def take(k):
            part = refs[pos[0]:pos[0] + k]
            pos[0] += k
            return part

        i_refs = take(n_in)
        c_in = [take(len(c.ins)) for c in carries]
        o_refs = take(n_out)
        c_out = [take(len(c.out_shape)) for c in carries]
        s_refs = take(n_scr)
        c_scr = [take(len(c.scratch)) for c in carries]
        return i_refs, o_refs, s_refs, list(zip(carries, c_in, c_out, c_scr))

    def full(*refs):
        i_refs, o_refs, s_refs, cparts = split(refs)
        relays = [part for part in cparts if hasattr(part[0], "pass_on")]
        if ng == 0:
            if own_barrier:
                _mesh_barrier()
            for c, a, b, s in cparts:
                c.start(a, b, s)
            for c, a, b, s in relays:
                c.pass_on(a, b, s)
            for c, a, b, s in cparts:
                c.finish(a, b, s)
            return
        ids = [pl.program_id(a) for a in range(ng)]
        ahead = grid[-1] >= 2
        if cparts:
            @pl.when(functools.reduce(operator.and_, [i == 0 for i in ids]))
            def _():
                if own_barrier:
                    _mesh_barrier()
                for c, a, b, s in cparts:
                    c.start(a, b, s)

        if relays and ahead:
            @pl.when(functools.reduce(operator.and_, [i == g - 1 for i, g in zip(ids[:-1], grid[:-1])]
                                      + [ids[-1] == grid[-1] - 2]))
            def _():
                for c, a, b, s in relays:
                    c.pass_on(a, b, s)

        body(*i_refs, *o_refs, *s_refs)
        if cparts:
            @pl.when(functools.reduce(operator.and_, [i == g - 1 for i, g in zip(ids, grid)]))
            def _():
                if not ahead:
                    for c, a, b, s in relays:
                        c.pass_on(a, b, s)
                for c, a, b, s in cparts:
                    c.finish(a, b, s)

    all_ins = list(ins) + [a for c in carries for a in c.ins]
    all_in_specs = list(in_specs) + [ANY for c in carries for _ in c.ins]
    all_out_shape = list(out_shape) + [s for c in carries for s in c.out_shape]
    all_out_specs = list(out_specs) + [ANY for c in carries for _ in c.out_shape]
    all_scratch = list(scratch) + [s for c in carries for s in c.scratch]
    kwargs = dict(grid=grid) if ng else {}
    outs = pl.pallas_call(
        full, name=name, in_specs=all_in_specs, out_specs=all_out_specs, out_shape=all_out_shape,
        scratch_shapes=all_scratch,
        compiler_params=pltpu.CompilerParams(dimension_semantics=("arbitrary",) * ng if ng else None,
                                             vmem_limit_bytes=vmem, collective_id=0 if own_barrier else None),
        **kwargs)(*all_ins)
    outs = list(outs)
    res, pos = outs[:n_out], n_out
    carried = []
    for c in carries:
        carried.append(outs[pos:pos + len(c.out_shape)])
        pos += len(c.out_shape)
    return res, carried


def _pair_sum(g, items):
    rows = items[0][1]
    n = len(items) * (NDEV // 2)
    width = g.shape[1]

    def body(g_ref, out_ref, got_ref, mine_v, got_v, send_sems, recv_sems, ld_sems, st_sems):
        x, y, c = lax.axis_index("x"), lax.axis_index("y"), lax.axis_index("c")
        keep = [base + (2 * ch + c) * rows for base, _ in items for ch in range(NDEV // 2)]
        give = [base + (2 * ch + 1 - c) * rows for base, _ in items for ch in range(NDEV // 2)]

        def remote(j, r):
            return pltpu.make_async_remote_copy(
                src_ref=g_ref.at[pl.ds(r, rows)], dst_ref=got_ref.at[pl.ds(r, rows)], send_sem=send_sems.at[j],
                recv_sem=recv_sems.at[j], device_id=(x, y, 1 - c), device_id_type=MESH)

        sends = [remote(j, r) for j, r in enumerate(give)]
        for cp in sends:
            cp.start()
        loads = [pltpu.make_async_copy(g_ref.at[pl.ds(r, rows)], mine_v.at[j], ld_sems.at[0, j])
                 for j, r in enumerate(keep)]
        for cp in loads:
            cp.start()
        stores = []
        for j, r in enumerate(keep):
            remote(j, r).wait_recv()
            cp = pltpu.make_async_copy(got_ref.at[pl.ds(r, rows)], got_v.at[j], ld_sems.at[1, j])
            cp.start()
            loads[j].wait()
            cp.wait()
            mine_v[j] = (mine_v[j].astype(F32) + got_v[j].astype(F32)).astype(BF16)
            st = pltpu.make_async_copy(mine_v.at[j], out_ref.at[pl.ds(r, rows)], st_sems.at[j])
            st.start()
            stores.append(st)
        for cp in sends:
            cp.wait_send()
        for cp in stores:
            cp.wait()

    out, _ = pl.pallas_call(
        body, name="pair_sum", in_specs=[ANY], out_specs=[ANY, ANY],
        out_shape=[jax.ShapeDtypeStruct(g.shape, g.dtype), jax.ShapeDtypeStruct(g.shape, g.dtype)],
        scratch_shapes=[pltpu.VMEM((n, rows, width), g.dtype), pltpu.VMEM((n, rows, width), g.dtype),
                        DMA((n,)), DMA((n,)), DMA((2, n)), DMA((n,))],
        compiler_params=pltpu.CompilerParams(vmem_limit_bytes=40 * 2 ** 20))(g)
    return out


def _groups_of(names):
    out = []
    for n in names:
        if LOC[n][0] not in out:
            out.append(LOC[n][0])
    return out


def _weight_pieces(name):
    g, off, rows = LOC[name]
    return [(d * GROUP_ROWS[g] + off, d * rows, rows) for d in range(NDEV)]


def _win_pieces():
    kv = N_KV * HEAD_DIM
    pieces = [(0, 0, D)]
    for g0, d0 in ((D, D), (D + kv, D + N_KV * LANES)):
        for g in range(N_KV):
            for half in range(2):
                pieces.append((g0 + g * HEAD_DIM, d0 + g * LANES + half * HEAD_DIM, HEAD_DIM))
    pieces.append((D + 2 * kv, D + 2 * N_KV * LANES, D))
    pieces.append((D + 2 * kv + D, D + 2 * N_KV * LANES + D, D))
    grp, off, rows = LOC["w_in"]
    out = []
    for g0, d0, n in pieces:
        while n > 0:
            dev, loc = divmod(g0, rows)
            m = min(n, rows - loc)
            out.append((dev * GROUP_ROWS[grp] + off + loc, d0, m))
            g0, d0, n = g0 + m, d0 + m, n - m
    return out


def _start_loads(src_ref, dst_ref, pieces, sems, base):
    cps = []
    for j, (s, d, n) in enumerate(pieces):
        cp = pltpu.make_async_copy(src_ref.at[pl.ds(s, n)], dst_ref.at[pl.ds(d, n)], sems.at[base + j])
        cp.start()
        cps.append(cp)
    return cps


def _load_weights(wrefs, targets, sems):
    cps, base = [], 0
    for name, dst in targets:
        pieces = _win_pieces() if name == "w_in" else _weight_pieces(name)
        cps += _start_loads(wrefs[LOC[name][0]], dst, pieces, sems, base)
        base += len(pieces)
    for cp in cps:
        cp.wait()


def _n_pieces(names):
    return sum(len(_win_pieces()) if n == "w_in" else NDEV for n in names)


def _dw(lhs, rhs, chunk, name, carries=()):
    nq = 4
    if lhs.ndim == 3:
        nch, t_tok, chunk = lhs.shape
        c = nch * chunk
        tq = t_tok // nq
        lhs_specs = [pl.BlockSpec((None, tq, chunk), lambda i, k=k: (i, k, 0)) for k in range(nq)]
    else:
        t_tok, c = lhs.shape
        tq = t_tok // nq
        lhs_specs = [pl.BlockSpec((tq, chunk), lambda i, k=k: (k, i)) for k in range(nq)]

    def body(*refs):
        lhs_refs, (rhs_ref, out_ref, rhs_s, sems) = refs[:nq], refs[nq:]
        cps = [pltpu.make_async_copy(rhs_ref.at[pl.ds(k * tq, tq)], rhs_s.at[pl.ds(k * tq, tq)], sems.at[k])
               for k in range(nq)]
        first = pl.program_id(0) == 0

        @pl.when(first)
        def _():
            for cp in cps:
                cp.start()

        for o in range(0, chunk, D // 2):
            n = min(D // 2, chunk - o)
            acc = None
            for k in range(nq):
                if o == 0:
                    @pl.when(first)
                    def _():
                        cps[k].wait()

                part = _tn(lhs_refs[k][:, pl.ds(o, n)], rhs_s[pl.ds(k * tq, tq), :])
                acc = part if acc is None else acc + part
            out_ref[pl.ds(o, n), :] = acc.astype(BF16)

    (out,), carried = _call(
        body, name=name, grid=(c // chunk,), ins=[lhs] * nq + [rhs],
        in_specs=lhs_specs + [ANY],
        out_specs=[pl.BlockSpec((chunk, D), lambda i: (i, 0))],
        out_shape=[jax.ShapeDtypeStruct((c, D), BF16)],
        scratch=[pltpu.VMEM((t_tok, D), BF16), DMA((nq,))], vmem=VMEM_BIG, carries=carries)
    return out, carried


def _silu_parts(a):
    sig = jax.nn.sigmoid(a)
    return sig, a * sig


FC = 256


def _ffn_fwd(h, gpre, gpost, wg, names, target=None, tm=512, carries=()):
    t_tok = h.shape[0]
    nt = t_tok // tm
    with_loss = target is not None
    groups = _groups_of(names)

    def body(*refs):
        refs = list(refs)
        h_ref, gpre_ref, gpost_ref = refs[:3]
        del refs[:3]
        tgt_ref = refs.pop(0) if with_loss else None
        wrefs = dict(zip(groups, refs[:len(groups)]))
        del refs[:len(groups)]
        hout_ref, a_ref, b_ref, f_ref, nb_ref = refs[:5]
        del refs[:5]
        if with_loss:
            dy_ref, loss_ref = refs[:2]
            del refs[:2]
        w1_s, w3_s, w2_s, sems = refs
        i = pl.program_id(0)

        @pl.when(i == 0)
        def _():
            _load_weights(wrefs, list(zip(names, (w1_s, w3_s, w2_s))), sems)
            if with_loss:
                loss_ref[...] = jnp.zeros_like(loss_ref)

        x = h_ref[...]
        n, _, _ = _rms_fwd(x, gpre_ref[...])
        nb = n.astype(BF16)
        nb_ref[...] = nb
        f = jnp.zeros((tm, D), F32)
        for c0 in range(0, FF, FC):
            a = _nt(nb, w1_s[pl.ds(c0, FC), :])
            b = _nt(nb, w3_s[pl.ds(c0, FC), :])
            _, sl = _silu_parts(a)
            a_ref[:, pl.ds(c0, FC)] = a.astype(BF16)
            b_ref[:, pl.ds(c0, FC)] = b.astype(BF16)
            f = f + _nn((sl * b).astype(BF16), w2_s[pl.ds(c0, FC), :])
        f_ref[...] = f
        fn, _, _ = _rms_fwd(f, gpost_ref[...])
        y = x + 0.5 * fn
        hout_ref[...] = y
        if with_loss:
            err = y - tgt_ref[...]
            dy_ref[...] = err * (1.0 / D)
            loss_ref[...] += jnp.sum(jnp.sum(err * err, axis=-1, keepdims=True), axis=0, keepdims=True) * (0.5 / D)

    ins = [h, gpre, gpost] + ([target] if with_loss else []) + [wg[g] for g in groups]
    in_specs = [_row_spec(tm, D), _const_spec((1, D)), _const_spec((1, D))]
    in_specs += ([_row_spec(tm, D)] if with_loss else []) + [ANY] * len(groups)
    out_shape = [jax.ShapeDtypeStruct((t_tok, D), F32), jax.ShapeDtypeStruct((t_tok, FF), BF16),
                 jax.ShapeDtypeStruct((t_tok, FF), BF16), jax.ShapeDtypeStruct((t_tok, D), F32),
                 jax.ShapeDtypeStruct((t_tok, D), BF16)]
    out_specs = [_row_spec(tm, D), _row_spec(tm, FF), _row_spec(tm, FF), _row_spec(tm, D), _row_spec(tm, D)]
    if with_loss:
        out_shape += [jax.ShapeDtypeStruct((t_tok, D), F32), jax.ShapeDtypeStruct((1, LANES), F32)]
        out_specs += [_row_spec(tm, D), _const_spec((1, LANES))]
    return _call(body, name="ffn_fwd_" + names[0][:4], grid=(nt,), ins=ins, in_specs=in_specs,
                 out_shape=out_shape, out_specs=out_specs,
                 scratch=[pltpu.VMEM((FF, D), BF16)] * 3 + [DMA((3 * NDEV,))], vmem=VMEM_BIG, carries=carries)


FH = FF // 2
HALF_PIECES = ((0, 256), (256, 256), (512, 256), (768, 256), (1024, 256), (1280, 128))


def _ffn_bwd_a(dh, f, a, b, gpost, wg, name_w2, tm=512, carries=()):
    t_tok = dh.shape[0]
    nt = t_tok // tm
    groups = _groups_of([name_w2])

    def body(dh_ref, f_ref, a_ref, b_ref, gpost_ref, wg_ref, dab_ref, s_ref, df_ref, dgp_ref, w2_s, sems):
        i = pl.program_id(0)

        @pl.when(i == 0)
        def _():
            _load_weights({groups[0]: wg_ref}, [(name_w2, w2_s)], sems)
            dgp_ref[...] = jnp.zeros_like(dgp_ref)

        fv = f_ref[...]
        _, fh, r = _rms_fwd(fv, gpost_ref[...])
        df, dg = _rms_bwd(0.5 * dh_ref[...], fh, r, gpost_ref[...])
        dgp_ref[...] += dg
        dfb = df.astype(BF16)
        df_ref[...] = dfb
        for half in range(2):
            for o, n in HALF_PIECES:
                c0 = half * FH + o
                ds = _nt(dfb, w2_s[pl.ds(c0, n), :])
                av = a_ref[:, pl.ds(c0, n)].astype(F32)
                bv = b_ref[:, pl.ds(c0, n)].astype(F32)
                sig, sl = _silu_parts(av)
                dab_ref[half, :, pl.ds(o, n)] = (ds * bv * (sig * (1.0 + av * (1.0 - sig)))).astype(BF16)
                dab_ref[2 + half, :, pl.ds(o, n)] = (ds * sl).astype(BF16)
                s_ref[half, :, pl.ds(o, n)] = (sl * bv).astype(BF16)

    return _call(
        body, name="ffn_bwd_a_" + name_w2[:4], grid=(nt,), ins=[dh, f, a, b, gpost, wg[groups[0]]],
        in_specs=[_row_spec(tm, D), _row_spec(tm, D), _row_spec(tm, FF), _row_spec(tm, FF), _const_spec((1, D)), ANY],
        out_specs=[pl.BlockSpec((4, tm, FH), lambda i: (0, i, 0)), pl.BlockSpec((2, tm, FH), lambda i: (0, i, 0)),
                   _row_spec(tm, D), _const_spec((1, D))],
        out_shape=[jax.ShapeDtypeStruct((4, t_tok, FH), BF16), jax.ShapeDtypeStruct((2, t_tok, FH), BF16),
                   jax.ShapeDtypeStruct((t_tok, D), BF16), jax.ShapeDtypeStruct((1, D), F32)],
        scratch=[pltpu.VMEM((FF, D), BF16), DMA((NDEV,))],
        vmem=VMEM_BIG, carries=carries)


def _ffn_bwd_dx(dab, h, dh, gpre, wg, name_w1, name_w3, tm=512, carries=()):
    t_tok = dh.shape[0]
    nt = t_tok // tm
    groups = _groups_of([name_w1, name_w3])

    def body(*refs):
        dab_refs, (h_ref, dh_ref, gpre_ref) = refs[:4], refs[4:7]
        wrefs = dict(zip(groups, refs[7:7 + len(groups)]))
        dhin_ref, dgp_ref, w13_s, sems = refs[7 + len(groups):]
        i = pl.program_id(0)

        @pl.when(i == 0)
        def _():
            _load_weights(wrefs, [(name_w1, w13_s.at[pl.ds(0, FF)]), (name_w3, w13_s.at[pl.ds(FF, FF)])], sems)
            dgp_ref[...] = jnp.zeros_like(dgp_ref)

        g = gpre_ref[...]
        _, xh, r = _rms_fwd(h_ref[...], g)
        dn = _nn(dab_refs[0][...], w13_s[pl.ds(0, FH), :])
        for k in range(1, 4):
            dn = dn + _nn(dab_refs[k][...], w13_s[pl.ds(k * FH, FH), :])
        dx, dg = _rms_bwd(dn, xh, r, g)
        dgp_ref[...] += dg
        dhin_ref[...] = dh_ref[...] + dx

    return _call(
        body, name="ffn_bwd_dx_" + name_w1[:4], grid=(nt,), ins=[dab] * 4 + [h, dh, gpre] + [wg[g] for g in groups],
        in_specs=[pl.BlockSpec((None, tm, FH), lambda i, k=k: (k, i, 0)) for k in range(4)]
        + [_row_spec(tm, D), _row_spec(tm, D), _const_spec((1, D))] + [ANY] * len(groups),
        out_specs=[_row_spec(tm, D), _const_spec((1, D))],
        out_shape=[jax.ShapeDtypeStruct((t_tok, D), F32), jax.ShapeDtypeStruct((1, D), F32)],
        scratch=[pltpu.VMEM((2 * FF, D), BF16), DMA((2 * NDEV,))],
        vmem=VMEM_BIG, carries=carries)


Q0, K0, V0, XR0, XG0 = 0, D, D + N_KV * LANES, D + 2 * N_KV * LANES, 2 * D + 2 * N_KV * LANES


def _mix_proj_fwd(h, g, bgate, wg, tm=512, carries=()):
    t_tok = h.shape[0]
    nt = t_tok // tm
    names = ("w_in", "w_gate")
    groups = _groups_of(names)

    def body(h_ref, g_ref, bg_ref, wg_ref, q_ref, k_ref, v_ref, xr_ref, xg_ref, gs_ref, ub_ref, win_s, wgt_s, sems):
        i = pl.program_id(0)

        @pl.when(i == 0)
        def _():
            _load_weights({groups[0]: wg_ref}, [("w_in", win_s), ("w_gate", wgt_s)], sems)

        n, _, _ = _rms_fwd(h_ref[...], g_ref[...])
        nb = n.astype(BF16)
        ub_ref[...] = nb
        q_ref[...] = _nt(nb, win_s[pl.ds(Q0, D), :]).astype(BF16)
        k_ref[...] = _nt(nb, win_s[pl.ds(K0, N_KV * LANES), :]).astype(BF16)
        v_ref[...] = _nt(nb, win_s[pl.ds(V0, N_KV * LANES), :]).astype(BF16)
        xr_ref[...] = _nt(nb, win_s[pl.ds(XR0, D), :])
        xg_ref[...] = _nt(nb, win_s[pl.ds(XG0, D), :])
        gs_ref[...] = jax.nn.sigmoid(_nt(nb, wgt_s[...]) + bg_ref[...]).astype(BF16)

    kvw = N_KV * LANES
    return _call(
        body, name="mix_proj_fwd", grid=(nt,), ins=[h, g, bgate, wg[groups[0]]],
        in_specs=[_row_spec(tm, D), _const_spec((1, D)), _const_spec((1, 2 * D)), ANY],
        out_specs=[_row_spec(tm, D), _row_spec(tm, kvw), _row_spec(tm, kvw), _row_spec(tm, D), _row_spec(tm, D),
                   _row_spec(tm, 2 * D), _row_spec(tm, D)],
        out_shape=[jax.ShapeDtypeStruct((t_tok, D), BF16), jax.ShapeDtypeStruct((t_tok, kvw), BF16),
                   jax.ShapeDtypeStruct((t_tok, kvw), BF16), jax.ShapeDtypeStruct((t_tok, D), F32),
                   jax.ShapeDtypeStruct((t_tok, D), F32), jax.ShapeDtypeStruct((t_tok, 2 * D), BF16),
                   jax.ShapeDtypeStruct((t_tok, D), BF16)],
        scratch=[pltpu.VMEM((INP_W, D), BF16), pltpu.VMEM((2 * D, D), BF16), DMA((_n_pieces(names),))],
        vmem=VMEM_BIG, carries=carries)


def _mix_proj_bwd(dqkv, dxr, dxg, dgpre, h, dh, g, wg, tm=512, carries=()):
    t_tok = h.shape[0]
    nt = t_tok // tm
    names = ("w_in", "w_gate")
    groups = _groups_of(names)

    def body(dqkv_ref, dxr_ref, dxg_ref, dgp_ref, h_ref, dh_ref, g_ref, wg_ref, dhin_ref, dg_ref, win_s, wgt_s, sems):
        i = pl.program_id(0)

        @pl.when(i == 0)
        def _():
            _load_weights({groups[0]: wg_ref}, [("w_in", win_s), ("w_gate", wgt_s)], sems)
            dg_ref[...] = jnp.zeros_like(dg_ref)

        gv = g_ref[...]
        _, xh, r = _rms_fwd(h_ref[...], gv)
        du = _nn(dgp_ref[...], wgt_s[...])
        du = du + _nn(dqkv_ref[...], win_s[pl.ds(Q0, XR0), :])
        du = du + _nn(dxr_ref[...], win_s[pl.ds(XR0, D), :])
        du = du + _nn(dxg_ref[...], win_s[pl.ds(XG0, D), :])
        dx, dg = _rms_bwd(du, xh, r, gv)
        dg_ref[...] += dg
        dhin_ref[...] = dh_ref[...] + dx

    return _call(
        body, name="mix_proj_bwd", grid=(nt,), ins=[dqkv, dxr, dxg, dgpre, h, dh, g, wg[groups[0]]],
        in_specs=[_row_spec(tm, XR0), _row_spec(tm, D), _row_spec(tm, D), _row_spec(tm, 2 * D), _row_spec(tm, D),
                  _row_spec(tm, D), _const_spec((1, D)), ANY],
        out_specs=[_row_spec(tm, D), _const_spec((1, D))],
        out_shape=[jax.ShapeDtypeStruct((t_tok, D), F32), jax.ShapeDtypeStruct((1, D), F32)],
        scratch=[pltpu.VMEM((INP_W, D), BF16), pltpu.VMEM((2 * D, D), BF16), DMA((_n_pieces(names),))],
        vmem=VMEM_BIG, carries=carries)


def _shift_down(x, d, fill):
    row = lax.broadcasted_iota(jnp.int32, x.shape, 0)
    return jnp.where(row >= d, pltpu.roll(x, d, 0), fill)


def _shift_up(x, d, fill):
    rows = x.shape[0]
    row = lax.broadcasted_iota(jnp.int32, x.shape, 0)
    return jnp.where(row < rows - d, pltpu.roll(x, rows - d, 0), fill)


def _scan_rows(a, b, reverse):
    rows = a.shape[0]
    d = 1
    while d < rows:
        if d < 8:
            shift = _shift_up if reverse else _shift_down
            b = a * shift(b, d, 0.0) + b
            a = a * shift(a, d, 1.0)
        elif reverse:
            b = jnp.concatenate([a[:rows - d] * b[d:] + b[:rows - d], b[rows - d:]], axis=0)
            a = jnp.concatenate([a[:rows - d] * a[d:], a[rows - d:]], axis=0)
        else:
            b = jnp.concatenate([b[:d], a[d:] * b[:rows - d] + b[d:]], axis=0)
            a = jnp.concatenate([a[:d], a[d:] * a[:rows - d]], axis=0)
        d *= 2
    return a, b


def _softplus(x):
    return jnp.maximum(x, 0.0) + jnp.log(1.0 + jnp.exp(-jnp.abs(x)))


_GELU_C = math.sqrt(2.0 / math.pi)


def _gelu_parts(x):
    th = jnp.tanh(_GELU_C * (x + 0.044715 * x * x * x))
    val = 0.5 * x * (1.0 + th)
    grad = 0.5 * (1.0 + th) + 0.5 * x * (1.0 - th * th) * _GELU_C * (1.0 + 3.0 * 0.044715 * x * x)
    return val, grad


def _lru_pre(x, halo, cw_ref, cb_ref, wa_ref, wx_ref, ba_ref, bx_ref, lam_ref):
    ext = jnp.concatenate([halo, x], axis=0)
    shifted = [x] + [pltpu.roll(ext, k, 0)[8:] for k in (1, 2, 3)]
    xc = cb_ref[...] + cw_ref[pl.ds(CONV_WIDTH - 1, 1), :] * x
    for k in (1, 2, 3):
        xc = xc + cw_ref[pl.ds(CONV_WIDTH - 1 - k, 1), :] * shifted[k]
    xcb = xc.astype(BF16)
    r = jax.nn.sigmoid(_nn(xcb, wa_ref[...]) + ba_ref[...])
    ig = jax.nn.sigmoid(_nn(xcb, wx_ref[...]) + bx_ref[...])
    sp = _softplus(-lam_ref[...])
    log_a = -LRU_C * r * sp
    a = jnp.exp(log_a)
    th = jnp.tanh(log_a)
    mult = jnp.sqrt(-2.0 * th / (1.0 - th))
    return shifted, xc, xcb, r, ig, sp, a, mult


def _lru_specs(nt, rows, reverse):
    def tt(t):
        return nt - 1 - t if reverse else t
    tile = pl.BlockSpec((rows, LANES), lambda cb, t: (tt(t), cb))
    halo = pl.BlockSpec((8, LANES), lambda cb, t: (jnp.maximum(tt(t) * (rows // 8) - 1, 0), cb))
    vec = pl.BlockSpec((1, LANES), lambda cb, t: (0, cb))
    cw = pl.BlockSpec((CONV_WIDTH, LANES), lambda cb, t: (0, cb))
    mat = pl.BlockSpec((None, LANES, LANES), lambda cb, t: (cb, 0, 0))
    return tile, halo, vec, cw, mat


def _lru_fwd(xr, xg, cw, cb, wa, wx, ba, bx, lam, carries=()):
    t_tok = xr.shape[0]
    rows = min(LRU_ROWS, t_tok)
    nt = t_tok // rows

    def body(xr_ref, xg_ref, cw_ref, cb_ref, wa_ref, wx_ref, ba_ref, bx_ref, lam_ref, y_ref, h_ref,
             xc_ref, r_ref, ig_ref, a_ref, mult_ref, tail_s, hc_s):
        t = pl.program_id(1)

        @pl.when(t == 0)
        def _():
            tail_s[...] = jnp.zeros_like(tail_s)
            hc_s[...] = jnp.zeros_like(hc_s)

        x = xr_ref[...]
        _, xc, xcb, r, ig, _, a, mult = _lru_pre(x, tail_s[...], cw_ref, cb_ref, wa_ref, wx_ref, ba_ref, bx_ref, lam_ref)
        xc_ref[...] = xcb
        r_ref[...] = r.astype(BF16)
        ig_ref[...] = ig.astype(BF16)
        a_ref[...] = a
        mult_ref[...] = mult
        tail_s[...] = xr_ref[pl.ds(rows - 8, 8), :]
        acc_a, acc_b = _scan_rows(a, mult * (ig * xc), False)
        hv = acc_b + acc_a * hc_s[...]
        h_ref[...] = hv
        hc_s[...] = h_ref[pl.ds(rows - 1, 1), :]
        gl, _ = _gelu_parts(xg_ref[...])
        y_ref[...] = (hv * gl).astype(BF16)

    tile, _, vec, cws, mat = _lru_specs(nt, rows, False)
    return _call(
        body, name="lru_fwd", grid=(D // LANES, nt), ins=[xr, xg, cw, cb, wa, wx, ba, bx, lam],
        in_specs=[tile, tile, cws, vec, mat, mat, vec, vec, vec],
        out_specs=[tile] * 7,
        out_shape=[jax.ShapeDtypeStruct((t_tok, D), dt) for dt in (BF16, F32, BF16, BF16, BF16, F32, F32)],
        scratch=[pltpu.VMEM((8, LANES), F32), pltpu.VMEM((1, LANES), F32)], carries=carries)


def _lru_bwd(dy, xr, xg, hseq, gates, cw, cb, wa, wx, ba, bx, lam, carries=()):
    t_tok = xr.shape[0]
    rows = min(LRU_ROWS // 2, t_tok)
    nt = t_tok // rows

    def body(dy_ref, xr_ref, xrh_ref, xg_ref, h_ref, hh_ref, xc_ref, r_ref, ig_ref, a_ref, mult_ref,
             cw_ref, cb_ref, wa_ref, wx_ref, ba_ref, bx_ref, lam_ref,
             dxr_ref, dxg_ref, dvec_ref, dwa_ref, dwx_ref, gcar_s, acar_s, head_s, tmp_s):
        t = pl.program_id(1)
        first_tile = t == nt - 1

        @pl.when(t == 0)
        def _():
            gcar_s[...] = jnp.zeros_like(gcar_s)
            acar_s[...] = jnp.zeros_like(acar_s)
            head_s[...] = jnp.zeros_like(head_s)
            dvec_ref[...] = jnp.zeros_like(dvec_ref)
            dwa_ref[...] = jnp.zeros_like(dwa_ref)
            dwx_ref[...] = jnp.zeros_like(dwx_ref)

        x = xr_ref[...]
        halo = jnp.where(first_tile, 0.0, xrh_ref[...])
        ext = jnp.concatenate([halo, x], axis=0)
        shifted = [x] + [pltpu.roll(ext, k, 0)[8:] for k in (1, 2, 3)]
        xcb = xc_ref[...]
        xc, r, ig = xcb.astype(F32), r_ref[...].astype(F32), ig_ref[...].astype(F32)
        a, mult = a_ref[...], mult_ref[...]
        sp = _softplus(-lam_ref[...])
        hv = h_ref[...]
        dyv = dy_ref[...]
        gl, glg = _gelu_parts(xg_ref[...])
        dxg_ref[...] = (dyv * hv * glg).astype(BF16)
        acc_a, acc_b = _scan_rows(_shift_up(a, 1, acar_s[...]), dyv * gl, True)
        g = acc_b + acc_a * gcar_s[...]
        hhalo = jnp.where(first_tile, 0.0, hh_ref[...])
        hprev = pltpu.roll(jnp.concatenate([hhalo, hv], axis=0), 1, 0)[8:]
        dmult = g * ig * xc
        dlog_a = a * (g * hprev) - dmult * a * a / mult
        dig = g * mult * xc
        dxc = g * mult * ig
        dzr = (dlog_a * (-LRU_C * sp)) * r * (1.0 - r)
        dzx = dig * ig * (1.0 - ig)
        dzrb, dzxb = dzr.astype(BF16), dzx.astype(BF16)
        dxc = dxc + _nt(dzrb, wa_ref[...]) + _nt(dzxb, wx_ref[...])
        dwa_ref[...] += _tn(xcb, dzrb)
        dwx_ref[...] += _tn(xcb, dzxb)
        dsp = jnp.sum(dlog_a * (-LRU_C * r), axis=0, keepdims=True)
        dlam = dsp * (-jax.nn.sigmoid(-lam_ref[...]))
        vrow = lax.broadcasted_iota(jnp.int32, (8, LANES), 0)
        upd = jnp.where(vrow == 4, jnp.sum(dxc, axis=0, keepdims=True), 0.0)
        upd = jnp.where(vrow == 5, jnp.sum(dzr, axis=0, keepdims=True), upd)
        upd = jnp.where(vrow == 6, jnp.sum(dzx, axis=0, keepdims=True), upd)
        upd = jnp.where(vrow == 7, dlam, upd)
        for k in range(CONV_WIDTH):
            upd = jnp.where(vrow == CONV_WIDTH - 1 - k, jnp.sum(dxc * shifted[k], axis=0, keepdims=True), upd)
        dvec_ref[...] += upd
        ext = jnp.concatenate([dxc, head_s[...]], axis=0)
        dxr = cw_ref[pl.ds(CONV_WIDTH - 1, 1), :] * dxc
        for k in (1, 2, 3):
            dxr = dxr + cw_ref[pl.ds(CONV_WIDTH - 1 - k, 1), :] * pltpu.roll(ext, rows + 8 - k, 0)[:rows]
        dxr_ref[...] = dxr.astype(BF16)
        tmp_s[...] = g
        gcar_s[...] = tmp_s[pl.ds(0, 1), :]
        tmp_s[...] = a
        acar_s[...] = tmp_s[pl.ds(0, 1), :]
        tmp_s[...] = dxc
        head_s[...] = tmp_s[pl.ds(0, 8), :]

    tile, halo, vec, cws, mat = _lru_specs(nt, rows, True)
    return _call(
        body, name="lru_bwd", grid=(D // LANES, nt),
        ins=[dy, xr, xr, xg, hseq, hseq, *gates, cw, cb, wa, wx, ba, bx, lam],
        in_specs=[tile, tile, halo, tile, tile, halo] + [tile] * 5 + [cws, vec, mat, mat, vec, vec, vec],
        out_specs=[tile, tile, pl.BlockSpec((8, LANES), lambda cb, t: (0, cb)), mat, mat],
        out_shape=[jax.ShapeDtypeStruct((t_tok, D), BF16), jax.ShapeDtypeStruct((t_tok, D), BF16),
                   jax.ShapeDtypeStruct((8, D), F32), jax.ShapeDtypeStruct((D // LANES, LANES, LANES), F32),
                   jax.ShapeDtypeStruct((D // LANES, LANES, LANES), F32)],
        scratch=[pltpu.VMEM((1, LANES), F32), pltpu.VMEM((1, LANES), F32), pltpu.VMEM((8, LANES), F32),
                 pltpu.VMEM((rows, LANES), F32)], carries=carries)


def _t5_bucket_np(rel):
    nb = N_BUCKETS // 2
    max_exact = nb // 2
    ret = np.where(rel > 0, nb, 0)
    n = np.abs(rel)
    nf = np.maximum(n, 1).astype(np.float32)
    large = max_exact + (np.log(nf / np.float32(max_exact)) / np.float32(math.log(MAX_DISTANCE / max_exact))
                         * np.float32(nb - max_exact)).astype(np.int32)
    large = np.minimum(large, nb - 1)
    return ret + np.where(n < max_exact, n, large)


def _bucket_map():
    r = np.arange(QT)[:, None]
    c = np.arange(KW)[None, :]
    j = c - (r // CHUNK) * CHUNK
    band = (j >= 0) & (j < WINDOW + CHUNK)
    return np.where(band, _t5_bucket_np(c - r - WINDOW), -1).astype(np.int32)


def _attn_specs(nt, reverse):
    def tt(i):
        return nt - 1 - i if reverse else i
    kvw = N_KV * LANES
    qs = pl.BlockSpec((QT, D), lambda i: (tt(i), 0))
    cur = pl.BlockSpec((QT, kvw), lambda i: (tt(i), 0))
    prev = pl.BlockSpec((WINDOW, kvw), lambda i: (jnp.maximum(tt(i) * (QT // WINDOW) - 1, 0), 0))
    lse = pl.BlockSpec((QT, LANES), lambda i: (tt(i), 0))
    return qs, cur, prev, lse


REP = N_HEADS // N_KV
SCALE = HEAD_DIM ** -0.5


def _stack_heads(x_ref, g, lo, scale=None):
    parts = []
    for hl in range(REP):
        xs = x_ref[:, pl.ds((2 * g + hl // 2) * LANES, LANES)]
        xs = jnp.where(lo if hl % 2 == 0 else jnp.logical_not(lo), xs, jnp.zeros_like(xs))
        parts.append(xs if scale is None else xs * jnp.asarray(scale, xs.dtype))
    return jnp.concatenate(parts, axis=0)


def _stack_sinks(sink_ref, g, srow):
    sk = jnp.full(srow.shape, sink_ref[REP * g + REP - 1], F32)
    for hl in range(REP - 2, -1, -1):
        sk = jnp.where(srow < (hl + 1) * QT, sink_ref[REP * g + hl], sk)
    return sk


def _attn_fwd(q, kd, vd, bias, sinks, carries=()):
    t_tok = q.shape[0]
    nt = t_tok // QT

    def body(q_ref, kp_ref, kc_ref, vp_ref, vc_ref, bias_ref, sink_ref, o_ref, lse_ref):
        i = pl.program_id(0)
        col = lax.broadcasted_iota(jnp.int32, (1, KW), 1)
        first = jnp.where((i == 0) & (col < WINDOW), NEG_INF, 0.0)
        lane = lax.broadcasted_iota(jnp.int32, (QT, LANES), 1)
        lo = lane < HEAD_DIM
        srow = lax.broadcasted_iota(jnp.int32, (REP * QT, 1), 0)
        lse_t = jnp.zeros((QT, LANES), F32)
        for g in range(N_KV):
            kwin = jnp.concatenate([kp_ref[:, pl.ds(g * LANES, LANES)], kc_ref[:, pl.ds(g * LANES, LANES)]], axis=0)
            vwin = jnp.concatenate([vp_ref[:, pl.ds(g * LANES, LANES)], vc_ref[:, pl.ds(g * LANES, LANES)]], axis=0)
            qst = _stack_heads(q_ref, g, lo, SCALE)
            s = _nt(qst, kwin) + (bias_ref[g] + first)
            sk = _stack_sinks(sink_ref, g, srow)
            m = jnp.maximum(jnp.max(s, axis=-1, keepdims=True), sk)
            e = jnp.exp(s - m)
            l = jnp.sum(e, axis=-1, keepdims=True) + jnp.exp(sk - m)
            p = e / l
            ost = _nn(p.astype(BF16), vwin)
            lse_s = m + jnp.log(l)
            for sl in range(2):
                o_ref[:, pl.ds((2 * g + sl) * LANES, LANES)] = jnp.where(
                    lo, ost[2 * sl * QT:(2 * sl + 1) * QT], ost[(2 * sl + 1) * QT:(2 * sl + 2) * QT]).astype(BF16)
            for hl in range(REP):
                lse_t = jnp.where(lane == REP * g + hl, lse_s[hl * QT:(hl + 1) * QT], lse_t)
        lse_ref[...] = lse_t

    qs, cur, prev, lse = _attn_specs(nt, False)
    return _call(
        body, name="attn_fwd", grid=(nt,), ins=[q, kd, kd, vd, vd, bias, sinks],
        in_specs=[qs, prev, cur, prev, cur, _const_spec((N_KV, REP * QT, KW)), pl.BlockSpec(memory_space=pltpu.SMEM)],
        out_specs=[qs, lse],
        out_shape=[jax.ShapeDtypeStruct((t_tok, D), BF16), jax.ShapeDtypeStruct((t_tok, LANES), F32)],
        vmem=48 * 2 ** 20, carries=carries)


def _attn_bwd(q, kd, vd, o, do, lse, bias, sinks, carries=()):
    t_tok = q.shape[0]
    nt = t_tok // QT
    kvw = N_KV * LANES

    def body(q_ref, kp_ref, kc_ref, vp_ref, vc_ref, o_ref, do_ref, lse_ref, bias_ref, sink_ref,
             dqkv_ref, ds_ref, dsink_ref, kcar_s, vcar_s):
        i = pl.program_id(0)
        tile = nt - 1 - i

        @pl.when(i == 0)
        def _():
            kcar_s[...] = jnp.zeros_like(kcar_s)
            vcar_s[...] = jnp.zeros_like(vcar_s)
            ds_ref[...] = jnp.zeros_like(ds_ref)
            dsink_ref[...] = jnp.zeros_like(dsink_ref)

        col = lax.broadcasted_iota(jnp.int32, (1, KW), 1)
        first = jnp.where((tile == 0) & (col < WINDOW), NEG_INF, 0.0)
        lane = lax.broadcasted_iota(jnp.int32, (QT, LANES), 1)
        lo = lane < HEAD_DIM
        lane_k = lax.broadcasted_iota(jnp.int32, (KW, LANES), 1)
        lane_1 = lax.broadcasted_iota(jnp.int32, (1, LANES), 1)
        srow = lax.broadcasted_iota(jnp.int32, (REP * QT, 1), 0)
        lse_t = lse_ref[...]
        dsink = jnp.zeros((1, LANES), F32)
        for g in range(N_KV):
            kwin = jnp.concatenate([kp_ref[:, pl.ds(g * LANES, LANES)], kc_ref[:, pl.ds(g * LANES, LANES)]], axis=0)
            vwin = jnp.concatenate([vp_ref[:, pl.ds(g * LANES, LANES)], vc_ref[:, pl.ds(g * LANES, LANES)]], axis=0)
            qst = _stack_heads(q_ref, g, lo, SCALE)
            dost = _stack_heads(do_ref, g, lo)
            od = [do_ref[:, pl.ds((2 * g + sl) * LANES, LANES)].astype(F32)
                  * o_ref[:, pl.ds((2 * g + sl) * LANES, LANES)].astype(F32) for sl in range(2)]
            drow = jnp.concatenate([jnp.sum(jnp.where(lo if hl % 2 == 0 else jnp.logical_not(lo), od[hl // 2], 0.0),
                                            axis=-1, keepdims=True) for hl in range(REP)], axis=0)
            lse_s = jnp.concatenate([jnp.sum(jnp.where(lane == REP * g + hl, lse_t, 0.0), axis=-1, keepdims=True)
                                     for hl in range(REP)], axis=0)
            s = _nt(qst, kwin) + (bias_ref[g] + first)
            p = jnp.exp(s - lse_s)
            ds = p * (_nt(dost, vwin) - drow)
            ds_ref[g] += ds
            tsink = -(jnp.exp(_stack_sinks(sink_ref, g, srow) - lse_s) * drow)
            for hl in range(REP):
                dsink = dsink + jnp.where(lane_1 == REP * g + hl,
                                          jnp.sum(tsink[hl * QT:(hl + 1) * QT], axis=0, keepdims=True), 0.0)
            dsb = ds.astype(BF16)
            dqst = _nn(dsb, kwin) * SCALE
            for sl in range(2):
                dqkv_ref[:, pl.ds((2 * g + sl) * LANES, LANES)] = jnp.where(
                    lo, dqst[2 * sl * QT:(2 * sl + 1) * QT], dqst[(2 * sl + 1) * QT:(2 * sl + 2) * QT]).astype(BF16)
            dk_acc = _tn(dsb, qst)
            dv_acc = _tn(p.astype(BF16), dost)
            dk_f = jnp.where(lane_k < HEAD_DIM, dk_acc + pltpu.roll(dk_acc, HEAD_DIM, 1), 0.0)
            dv_f = jnp.where(lane_k < HEAD_DIM, dv_acc + pltpu.roll(dv_acc, HEAD_DIM, 1), 0.0)
            for acc, col0, car in ((dk_f, K0, kcar_s), (dv_f, V0, vcar_s)):
                cs = pl.ds(g * LANES, LANES)
                co = pl.ds(col0 + g * LANES, LANES)
                if QT > WINDOW:
                    dqkv_ref[pl.ds(0, QT - WINDOW), co] = acc[WINDOW:QT].astype(BF16)
                dqkv_ref[pl.ds(QT - WINDOW, WINDOW), co] = (acc[QT:KW] + car[:, cs]).astype(BF16)
                car[:, cs] = acc[0:WINDOW]
        dsink_ref[...] += dsink

    qs, cur, prev, lse_s = _attn_specs(nt, True)
    return _call(
        body, name="attn_bwd", grid=(nt,), ins=[q, kd, kd, vd, vd, o, do, lse, bias, sinks],
        in_specs=[qs, prev, cur, prev, cur, qs, qs, lse_s, _const_spec((N_KV, REP * QT, KW)),
                  pl.BlockSpec(memory_space=pltpu.SMEM)],
        out_specs=[pl.BlockSpec((QT, XR0), lambda i: (nt - 1 - i, 0)), _const_spec((N_KV, REP * QT, KW)),
                   _const_spec((1, LANES))],
        out_shape=[jax.ShapeDtypeStruct((t_tok, XR0), BF16), jax.ShapeDtypeStruct((N_KV, REP * QT, KW), F32),
                   jax.ShapeDtypeStruct((1, LANES), F32)],
        scratch=[pltpu.VMEM((WINDOW, kvw), F32), pltpu.VMEM((WINDOW, kvw), F32)],
        vmem=VMEM_BIG, carries=carries)


def _bias_tile(table, bmap):
    def body(tab_ref, bm_ref, out_ref):
        bm = bm_ref[...]

        def per_head(hd, carry):
            acc = jnp.full((QT, KW), NEG_INF, F32)
            for b in range(N_BUCKETS):
                acc = jnp.where(bm == b, tab_ref[b, hd], acc)
            out_ref[hd] = acc
            return carry

        lax.fori_loop(0, N_HEADS, per_head, 0)

    return pl.pallas_call(
        body, name="bias_tile", out_shape=jax.ShapeDtypeStruct((N_HEADS, QT, KW), F32),
        in_specs=[pl.BlockSpec(memory_space=pltpu.SMEM), pl.BlockSpec(memory_space=pltpu.VMEM)],
        out_specs=pl.BlockSpec(memory_space=pltpu.VMEM))(table, bmap)


def _bias_grad(ds_acc, bmap):
    def body(ds_ref, bm_ref, out_ref):
        row = lax.broadcasted_iota(jnp.int32, (N_BUCKETS, LANES), 0)
        lane = lax.broadcasted_iota(jnp.int32, (N_BUCKETS, LANES), 1)
        bm = bm_ref[...]

        def per_head(hd, res):
            dsv = ds_ref[hd]
            for b in range(N_BUCKETS):
                val = jnp.sum(jnp.sum(jnp.where(bm == b, dsv, 0.0), axis=0, keepdims=True), axis=1, keepdims=True)
                res = jnp.where((row == b) & (lane == hd), val, res)
            return res

        out_ref[...] = lax.fori_loop(0, N_HEADS, per_head, jnp.zeros((N_BUCKETS, LANES), F32))

    return pl.pallas_call(
        body, name="bias_grad", out_shape=jax.ShapeDtypeStruct((N_BUCKETS, LANES), F32),
        in_specs=[pl.BlockSpec(memory_space=pltpu.VMEM), pl.BlockSpec(memory_space=pltpu.VMEM)],
        out_specs=pl.BlockSpec(memory_space=pltpu.VMEM))(ds_acc, bmap)


MIXOUT = ("w_lru_out", "w_attn_out", "w_o")


def _mix_out_fwd(ya_in, o, gs, h, g, wg, tm=512, carries=()):
    t_tok = h.shape[0]
    nt = t_tok // tm
    groups = _groups_of(MIXOUT)

    def body(ya_ref, o_ref, gs_ref, h_ref, g_ref, wg_ref, hout_ref, yao_ref, ybo_ref, z_ref, wa_s, wb_s, wo_s, sems):
        i = pl.program_id(0)

        @pl.when(i == 0)
        def _():
            _load_weights({groups[0]: wg_ref}, list(zip(MIXOUT, (wa_s, wb_s, wo_s))), sems)

        ya = _nn(ya_ref[...], wa_s[...])
        yb = _nn(o_ref[...], wb_s[...])
        yao_ref[...] = ya.astype(BF16)
        ybo_ref[...] = yb.astype(BF16)
        merged = gs_ref[:, pl.ds(0, D)].astype(F32) * ya + gs_ref[:, pl.ds(D, D)].astype(F32) * yb
        z = _nn(merged.astype(BF16), wo_s[...])
        z_ref[...] = z
        zn, _, _ = _rms_fwd(z, g_ref[...])
        hout_ref[...] = h_ref[...] + zn

    return _call(
        body, name="mix_out_fwd", grid=(nt,), ins=[ya_in, o, gs, h, g, wg[groups[0]]],
        in_specs=[_row_spec(tm, D), _row_spec(tm, D), _row_spec(tm, 2 * D), _row_spec(tm, D), _const_spec((1, D)), ANY],
        out_specs=[_row_spec(tm, D)] * 4,
        out_shape=[jax.ShapeDtypeStruct((t_tok, D), F32), jax.ShapeDtypeStruct((t_tok, D), BF16),
                   jax.ShapeDtypeStruct((t_tok, D), BF16), jax.ShapeDtypeStruct((t_tok, D), F32)],
        scratch=[pltpu.VMEM((D, D), BF16)] * 3 + [DMA((3 * NDEV,))],
        vmem=48 * 2 ** 20, carries=carries)


def _mix_out_bwd(dh, z, ya, yb, gs, g, wg, tm=512, carries=()):
    t_tok = dh.shape[0]
    nt = t_tok // tm
    groups = _groups_of(MIXOUT)

    def body(dh_ref, z_ref, ya_ref, yb_ref, gs_ref, g_ref, wg_ref,
             dyain_ref, do_ref, dgpre_ref, dya_ref, dyb_ref, mg_ref, dz_ref, dg_ref, dbg_ref,
             wa_s, wb_s, wo_s, sems):
        i = pl.program_id(0)

        @pl.when(i == 0)
        def _():
            _load_weights({groups[0]: wg_ref}, list(zip(MIXOUT, (wa_s, wb_s, wo_s))), sems)
            dg_ref[...] = jnp.zeros_like(dg_ref)
            dbg_ref[...] = jnp.zeros_like(dbg_ref)

        gv = g_ref[...]
        _, zh, r = _rms_fwd(z_ref[...], gv)
        dz, dg = _rms_bwd(dh_ref[...], zh, r, gv)
        dg_ref[...] += dg
        dzb = dz.astype(BF16)
        dz_ref[...] = dzb
        ga, gb = gs_ref[:, pl.ds(0, D)].astype(F32), gs_ref[:, pl.ds(D, D)].astype(F32)
        ya_v, yb_v = ya_ref[...].astype(F32), yb_ref[...].astype(F32)
        mg_ref[...] = (ga * ya_v + gb * yb_v).astype(BF16)
        dm = _nt(dzb, wo_s[...])
        dga = dm * ya_v * ga * (1.0 - ga)
        dgb = dm * yb_v * gb * (1.0 - gb)
        dgpre_ref[:, pl.ds(0, D)] = dga.astype(BF16)
        dgpre_ref[:, pl.ds(D, D)] = dgb.astype(BF16)
        dbg_ref[:, pl.ds(0, D)] += jnp.sum(dga, axis=0, keepdims=True)
        dbg_ref[:, pl.ds(D, D)] += jnp.sum(dgb, axis=0, keepdims=True)
        dya = (dm * ga).astype(BF16)
        dyb = (dm * gb).astype(BF16)
        dya_ref[...] = dya
        dyb_ref[...] = dyb
        dyain_ref[...] = _nt(dya, wa_s[...])
        do_ref[...] = _nt(dyb, wb_s[...]).astype(BF16)

    bf = jax.ShapeDtypeStruct((t_tok, D), BF16)
    return _call(
        body, name="mix_out_bwd", grid=(nt,), ins=[dh, z, ya, yb, gs, g, wg[groups[0]]],
        in_specs=[_row_spec(tm, D)] * 4 + [_row_spec(tm, 2 * D), _const_spec((1, D)), ANY],
        out_specs=[_row_spec(tm, D), _row_spec(tm, D), _row_spec(tm, 2 * D)] + [_row_spec(tm, D)] * 4
        + [_const_spec((1, D)), _const_spec((1, 2 * D))],
        out_shape=[jax.ShapeDtypeStruct((t_tok, D), F32), bf, jax.ShapeDtypeStruct((t_tok, 2 * D), BF16), bf, bf, bf, bf,
                   jax.ShapeDtypeStruct((1, D), F32), jax.ShapeDtypeStruct((1, 2 * D), F32)],
        scratch=[pltpu.VMEM((D, D), BF16)] * 3 + [DMA((3 * NDEV,))],
        vmem=VMEM_BIG, carries=carries)


def _sum_parts(parts_list):
    n = len(parts_list)
    _, r, c = parts_list[0].shape
    tc = 256

    def body(*refs):
        for p_ref, o_ref in zip(refs[:n], refs[n:]):
            acc = p_ref[0].astype(F32)
            for s in range(1, NDEV):
                acc = acc + p_ref[s].astype(F32)
            o_ref[...] = acc

    return pl.pallas_call(
        body, name=f"sum_parts_{r}", grid=(c // tc,),
        in_specs=[pl.BlockSpec((NDEV, r, tc), lambda i: (0, 0, i))] * n,
        out_specs=[pl.BlockSpec((r, tc), lambda i: (0, i))] * n,
        out_shape=[jax.ShapeDtypeStruct((r, c), F32)] * n,
        compiler_params=pltpu.CompilerParams(dimension_semantics=("arbitrary",), vmem_limit_bytes=48 * 2 ** 20),
    )(*parts_list)


def _adamw_math(w, g, m, v):
    m = ADAM_B1 * m + (1.0 - ADAM_B1) * g
    v = ADAM_B2 * v + (1.0 - ADAM_B2) * (g * g)
    m_hat = m / (1.0 - ADAM_B1 ** ADAM_STEP)
    v_hat = v / (1.0 - ADAM_B2 ** ADAM_STEP)
    delta = -ADAM_LR * (m_hat / (jnp.sqrt(v_hat) + ADAM_EPS) + ADAM_WD * w)
    return delta, m, v


def _sum_ready(parts_list, name, carries=(), chip_sums=()):
    n = len(parts_list)
    c = parts_list[0].shape[2]
    tc = 2 * LANES

    def body(*refs):
        for k, (p_ref, o_ref) in enumerate(zip(refs[:n], refs[n:])):
            if k in chip_sums:
                core = lax.axis_index("c")
                slots = [2 * ch + core for ch in range(NDEV // 2)]
            else:
                slots = list(range(NDEV))
            g = p_ref[slots[0]].astype(F32)
            for s in slots[1:]:
                g = g + p_ref[s].astype(F32)
            o_ref[...] = g

    return _call(
        body, name=name, grid=(c // tc,), ins=list(parts_list),
        in_specs=[pl.BlockSpec((NDEV, p.shape[1], tc), lambda i: (0, 0, i)) for p in parts_list],
        out_shape=[jax.ShapeDtypeStruct((p.shape[1], c), F32) for p in parts_list],
        out_specs=[pl.BlockSpec((p.shape[1], tc), lambda i: (0, i)) for p in parts_list],
        vmem=48 * 2 ** 20, carries=carries)


def _adamw_cols(items, name):
    n = len(items)
    c = items[0][1].shape[1]
    tc = LANES

    def body(*refs):
        for k in range(n):
            g_ref, w_ref, m_ref, v_ref = refs[4 * k:4 * k + 4]
            go_ref, d_ref, nm_ref, nv_ref = refs[4 * n + 4 * k:4 * n + 4 * k + 4]
            g = g_ref[...]
            d, m, v = _adamw_math(w_ref[...], g, m_ref[...], v_ref[...])
            go_ref[...] = g
            d_ref[...] = d
            nm_ref[...] = m
            nv_ref[...] = v

    ins, specs, out_shape = [], [], []
    for g, w, m, v in items:
        r = w.shape[0]
        ins += [g, w, m, v]
        specs += [pl.BlockSpec((r, tc), lambda i: (0, i))] * 4
        out_shape += [jax.ShapeDtypeStruct((r, c), F32)] * 4
    outs, _ = _call(body, name=name, grid=(c // tc,), ins=ins, in_specs=specs, out_shape=out_shape,
                    out_specs=specs, vmem=48 * 2 ** 20)
    return [tuple(outs[4 * k:4 * k + 4]) for k in range(n)]


def _adamw_body(n):
    def body(*refs):
        for k in range(n):
            g_ref, w_ref, m_ref, v_ref = refs[4 * k:4 * k + 4]
            d_ref, nm_ref, nv_ref = refs[4 * n + 3 * k:4 * n + 3 * k + 3]
            d, m, v = _adamw_math(w_ref[...], g_ref[...], m_ref[...], v_ref[...])
            d_ref[...] = d
            nm_ref[...] = m
            nv_ref[...] = v
    return body


def _adamw(items):
    n = len(items)
    r, c = items[0][0].shape
    tr = r if r * c <= 2 ** 18 else max(t for t in range(8, 65, 8) if r % t == 0)
    spec = pl.BlockSpec((tr, c), lambda i: (i, 0))
    outs = pl.pallas_call(
        _adamw_body(n), name=f"adamw_{r}x{c}", grid=(r // tr,),
        in_specs=[spec] * (4 * n), out_specs=[spec] * (3 * n),
        out_shape=[jax.ShapeDtypeStruct((r, c), F32)] * (3 * n),
        compiler_params=pltpu.CompilerParams(dimension_semantics=("arbitrary",), vmem_limit_bytes=40 * 2 ** 20),
    )(*[a for it in items for a in it])
    return [tuple(outs[3 * k:3 * k + 3]) for k in range(n)]


def _adamw_small(items):
    n = len(items)
    vm = pl.BlockSpec(memory_space=pltpu.VMEM)
    outs = pl.pallas_call(
        _adamw_body(n), name="adamw_small", in_specs=[vm] * (4 * n), out_specs=[vm] * (3 * n),
        out_shape=[jax.ShapeDtypeStruct(it[1].shape, F32) for it in items for _ in range(3)],
    )(*[a for it in items for a in it])
    return [tuple(outs[3 * k:3 * k + 3]) for k in range(n)]


def _pack_small(arrs):
    rows, offs = [], []
    total = 0
    for a in arrs:
        flat = a.reshape(-1).astype(F32)
        nr = -(-flat.shape[0] // LANES)
        flat = jnp.pad(flat, (0, nr * LANES - flat.shape[0]))
        rows.append(flat.reshape(nr, LANES))
        offs.append((total, nr))
        total += nr
    pad = -total % 8
    if pad:
        rows.append(jnp.zeros((pad, LANES), F32))
    return jnp.concatenate(rows, axis=0), offs


def _unpack_small(pack, offs, shapes):
    out = []
    for (o, nr), shp in zip(offs, shapes):
        size = int(np.prod(shp))
        out.append(pack[o:o + nr].reshape(-1)[:size].reshape(shp))
    return out


def _sum_small(gathered, rows):
    def body(p_ref, o_ref):
        acc = p_ref[pl.ds(0, rows), :]
        for s in range(1, NDEV):
            acc = acc + p_ref[pl.ds(s * rows, rows), :]
        o_ref[...] = acc

    return pl.pallas_call(
        body, name="sum_small", out_shape=jax.ShapeDtypeStruct((rows, LANES), F32),
        in_specs=[pl.BlockSpec(memory_space=pltpu.VMEM)], out_specs=pl.BlockSpec(memory_space=pltpu.VMEM))(gathered)


def _block_diag(w):
    w = w.reshape(D // LANES, 2, LRU_BLOCK, LRU_BLOCK)
    z = jnp.zeros((D // LANES, LRU_BLOCK, LRU_BLOCK), w.dtype)
    top = jnp.concatenate([w[:, 0], z], axis=2)
    bot = jnp.concatenate([z, w[:, 1]], axis=2)
    return jnp.concatenate([top, bot], axis=1)


def _block_diag_grad(dw):
    a = dw[:, :LRU_BLOCK, :LRU_BLOCK]
    b = dw[:, LRU_BLOCK:, LRU_BLOCK:]
    return jnp.stack([a, b], axis=1).reshape(D // LRU_BLOCK, LRU_BLOCK, LRU_BLOCK)


def kernel(x, ffn1_pre_g, ffn1_w1, ffn1_w3, ffn1_w2, ffn1_post_g, mix_pre_g, w_in, conv_w, conv_b, rg_a_w, rg_a_b, rg_x_w, rg_x_b, lru_lambda, w_lru_out, attn_sinks, rel_bias, w_attn_out, w_gate, b_gate, w_o, mix_post_g, ffn2_pre_g, ffn2_w1, ffn2_w3, ffn2_w2, ffn2_post_g, loss_target, m_ffn1_pre_g, m_ffn1_w1, m_ffn1_w3, m_ffn1_w2, m_ffn1_post_g, m_mix_pre_g, m_w_in, m_conv_w, m_conv_b, m_rg_a_w, m_rg_a_b, m_rg_x_w, m_rg_x_b, m_lru_lambda, m_w_lru_out, m_attn_sinks, m_rel_bias, m_w_attn_out, m_w_gate, m_b_gate, m_w_o, m_mix_post_g, m_ffn2_pre_g, m_ffn2_w1, m_ffn2_w3, m_ffn2_w2, m_ffn2_post_g, v_ffn1_pre_g, v_ffn1_w1, v_ffn1_w3, v_ffn1_w2, v_ffn1_post_g, v_mix_pre_g, v_w_in, v_conv_w, v_conv_b, v_rg_a_w, v_rg_a_b, v_rg_x_w, v_rg_x_b, v_lru_lambda, v_w_lru_out, v_attn_sinks, v_rel_bias, v_w_attn_out, v_w_gate, v_b_gate, v_w_o, v_mix_post_g, v_ffn2_pre_g, v_ffn2_w1, v_ffn2_w3, v_ffn2_w2, v_ffn2_post_g):
    names = ["ffn1_pre_g", "ffn1_w1", "ffn1_w3", "ffn1_w2", "ffn1_post_g", "mix_pre_g", "w_in", "conv_w", "conv_b",
             "rg_a_w", "rg_a_b", "rg_x_w", "rg_x_b", "lru_lambda", "w_lru_out", "attn_sinks", "rel_bias", "w_attn_out",
             "w_gate", "b_gate", "w_o", "mix_post_g", "ffn2_pre_g", "ffn2_w1", "ffn2_w3", "ffn2_w2", "ffn2_post_g"]
    ws = dict(zip(names, (ffn1_pre_g, ffn1_w1, ffn1_w3, ffn1_w2, ffn1_post_g, mix_pre_g, w_in, conv_w, conv_b, rg_a_w,
                          rg_a_b, rg_x_w, rg_x_b, lru_lambda, w_lru_out, attn_sinks, rel_bias, w_attn_out, w_gate,
                          b_gate, w_o, mix_post_g, ffn2_pre_g, ffn2_w1, ffn2_w3, ffn2_w2, ffn2_post_g)))
    ms = dict(zip(names, (m_ffn1_pre_g, m_ffn1_w1, m_ffn1_w3, m_ffn1_w2, m_ffn1_post_g, m_mix_pre_g, m_w_in, m_conv_w,
                          m_conv_b, m_rg_a_w, m_rg_a_b, m_rg_x_w, m_rg_x_b, m_lru_lambda, m_w_lru_out, m_attn_sinks,
                          m_rel_bias, m_w_attn_out, m_w_gate, m_b_gate, m_w_o, m_mix_post_g, m_ffn2_pre_g, m_ffn2_w1,
                          m_ffn2_w3, m_ffn2_w2, m_ffn2_post_g)))
    vs = dict(zip(names, (v_ffn1_pre_g, v_ffn1_w1, v_ffn1_w3, v_ffn1_w2, v_ffn1_post_g, v_mix_pre_g, v_w_in, v_conv_w,
                          v_conv_b, v_rg_a_w, v_rg_a_b, v_rg_x_w, v_rg_x_b, v_lru_lambda, v_w_lru_out, v_attn_sinks,
                          v_rel_bias, v_w_attn_out, v_w_gate, v_b_gate, v_w_o, v_mix_post_g, v_ffn2_pre_g, v_ffn2_w1,
                          v_ffn2_w3, v_ffn2_w2, v_ffn2_post_g)))
    me = 4 * lax.axis_index("x") + 2 * lax.axis_index("y") + lax.axis_index("c")
    vec = lambda n: ws[n].reshape(1, -1)

    def shard2d(name):
        if name == "conv":
            row = lax.bitcast_convert_type(conv_w.reshape(CONV_WIDTH, LANES), BF16).reshape(1, D)
            return jnp.concatenate([row, jnp.zeros((LOC["conv"][2] - 1, D), BF16)], axis=0)
        a = ws[name].reshape(ws[name].shape[-2], ws[name].shape[-1])
        return (a.T if name in COL_SHARDED else a).astype(BF16)

    packs = {g: jnp.concatenate([shard2d(n) for n, _ in members], axis=0) for g, members in GROUPS}

    wg = {}
    _, ((wg["ffn1"],),) = _call(None, name="allgather_ffn1", grid=(), ins=[], in_specs=[], out_shape=[], out_specs=[],
                                carries=[_AllGatherTree(packs["ffn1"])])
    conv_rows = wg["ffn1"].reshape(NDEV, GROUP_ROWS["ffn1"], D)[:, LOC["conv"][1]]
    cw = jnp.transpose(lax.bitcast_convert_type(conv_rows.reshape(NDEV, CONV_WIDTH, LANES, 2), F32),
                       (1, 0, 2)).reshape(CONV_WIDTH, D)
    bmap = jnp.asarray(_bucket_map())
    bias = _bias_tile(rel_bias, bmap).reshape(N_KV, REP * QT, KW)
    sinks = attn_sinks.reshape(N_HEADS)
    wa_bd = _block_diag(rg_a_w.reshape(D // LRU_BLOCK, LRU_BLOCK, LRU_BLOCK)).astype(BF16)
    wx_bd = _block_diag(rg_x_w.reshape(D // LRU_BLOCK, LRU_BLOCK, LRU_BLOCK)).astype(BF16)
    lru_args = (cw, vec("conv_b"), wa_bd, wx_bd, vec("rg_a_b"), vec("rg_x_b"), vec("lru_lambda"))
    x2, tgt = x[0], loss_target[0]

    (h1, a1, b1, f1, nb1), ((wg["mixin"],),) = _ffn_fwd(
        x2, vec("ffn1_pre_g"), vec("ffn1_post_g"), wg, ("ffn1_w1", "ffn1_w3", "ffn1_w2"),
        carries=[_AllGather(packs["mixin"])])
    (q, kd, vd, xr, xg, gs, ub), ((wg["mixout"],),) = _mix_proj_fwd(
        h1, vec("mix_pre_g"), vec("b_gate"), wg, carries=[_AllGather(packs["mixout"])])
    (ya_in, hseq, *gates), ((wg["ffn2a"],),) = _lru_fwd(xr, xg, *lru_args, carries=[_AllGather(packs["ffn2a"])])
    (o, lse), ((wg["ffn2b"],),) = _attn_fwd(q, kd, vd, bias, sinks, carries=[_AllGather(packs["ffn2b"])])
    (h2, ya, yb, z), _ = _mix_out_fwd(ya_in, o, gs, h1, vec("mix_post_g"), wg)
    (_, a2, b2, f2, nb2, dy, loss_part), _ = _ffn_fwd(h2, vec("ffn2_pre_g"), vec("ffn2_post_g"), wg,
                                                     ("ffn2_w1", "ffn2_w3", "ffn2_w2"), target=tgt)

    gsm, parts = {}, {}
    rs = lambda *grads_: [_ReduceScatterSend(list(grads_))]
    ffr = FF // NDEV
    (dab, s_act, dfb, gsm["ffn2_post_g"]), _ = _ffn_bwd_a(dy, f2, a2, b2, vec("ffn2_post_g"), wg, "ffn2_w2")
    g_w2, _ = _dw(s_act, dfb, FF // 2, "dw_ffn2_w2")
    g_w13, ((parts["ffn2_w2"],),) = _dw(dab, nb2, D // 2, "dw_ffn2_w13", carries=rs(g_w2))
    (dh2, gsm["ffn2_pre_g"]), ((parts["ffn2_w1"],),) = _ffn_bwd_dx(
        dab, h2, dy, vec("ffn2_pre_g"), wg, "ffn2_w1", "ffn2_w3", carries=rs((g_w13, 0, ffr, ffr)))
    (dya_in, do, dgpre, dya, dyb, mg, dzb, gsm["mix_post_g"], gsm["b_gate"]), ((parts["ffn2_w3"],),) = _mix_out_bwd(
        dh2, z, ya, yb, gs, vec("mix_post_g"), wg, carries=rs((g_w13, FF, ffr, ffr)))
    g_wa, _ = _dw(ya_in, dya, D // 2, "dw_w_lru_out")
    g_wb, _ = _dw(o, dyb, D // 2, "dw_w_attn_out")
    g_wo, _ = _dw(mg, dzb, D // 2, "dw_w_o")
    g_wgate, _ = _dw(dgpre, ub, D // 2, "dw_w_gate")
    (dqkv, ds_acc, dsink), (mixout_parts,) = _attn_bwd(
        q, kd, vd, o, do, lse, bias, sinks, carries=rs(g_wa, g_wb, g_wo, g_wgate))
    parts["w_lru_out"], parts["w_attn_out"], parts["w_o"], parts["w_gate"] = mixout_parts
    (dxr, dxg, dvec, dwa, dwx), _ = _lru_bwd(dya_in, xr, xg, hseq, gates, *lru_args)
    g_qkv, _ = _dw(dqkv, ub, D // 2, "dw_w_in_qkv")
    g_xr, _ = _dw(dxr, ub, D // 2, "dw_w_in_xr")
    g_xg, _ = _dw(dxg, ub, D // 2, "dw_w_in_xg")
    g_win = jnp.concatenate(
        [g_qkv[:D]] + [g_qkv[c0 + gi * LANES:c0 + gi * LANES + HEAD_DIM] for c0 in (K0, V0) for gi in range(N_KV)]
        + [g_xr, g_xg], axis=0)
    wir, wir_a = IN_W // NDEV, 336
    (dh1, gsm["mix_pre_g"]), ((win_a,),) = _mix_proj_bwd(
        dqkv, dxr, dxg, dgpre, h1, dh2, vec("mix_pre_g"), wg, carries=rs((g_win, 0, wir, wir_a)))
    gsm["conv_w"] = dvec[0:CONV_WIDTH]
    gsm["conv_b"], gsm["rg_a_b"], gsm["rg_x_b"], gsm["lru_lambda"] = dvec[4], dvec[5], dvec[6], dvec[7]
    gsm["rg_a_w"] = _block_diag_grad(dwa)
    gsm["rg_x_w"] = _block_diag_grad(dwx)
    gsm["attn_sinks"] = dsink[0, :N_HEADS]
    gsm["rel_bias"] = _bias_grad(ds_acc.reshape(N_HEADS, QT, KW), bmap)[:, :N_HEADS]
    late = ("ffn1_post_g", "ffn1_pre_g")
    early = tuple(n for n in SMALL if n not in late)
    early_pack, early_offs = _pack_small([gsm[n] for n in early])
    (dab, s_act, dfb, gsm["ffn1_post_g"]), ((win_b,), (early_all,)) = _ffn_bwd_a(
        dh1, f1, a1, b1, vec("ffn1_post_g"), wg, "ffn1_w2",
        carries=rs((g_win, wir_a, wir, wir - wir_a)) + [_AllGather(early_pack)])
    parts["w_in"] = jnp.concatenate([win_a, win_b], axis=1)
    g_w2, _ = _dw(s_act, dfb, FF // 2, "dw_ffn1_w2")
    g_w13, ((parts["ffn1_w2"],),) = _dw(dab, nb1, D // 2, "dw_ffn1_w13", carries=rs(g_w2))
    chip_w13 = _pair_sum(g_w13, [(0, ffr), (FF, ffr)])
    (grad_x, gsm["ffn1_pre_g"]), ((parts["ffn1_w1"], parts["ffn1_w3"]),) = _ffn_bwd_dx(
        dab, x2, dh1, vec("ffn1_pre_g"), wg, "ffn1_w1", "ffn1_w3",
        carries=[_ReduceScatterSend([(chip_w13, 0, ffr, ffr), (chip_w13, FF, ffr, ffr)], chip_sums=True)])
    late_pack, late_offs = _pack_small([gsm[n] for n in late] + [loss_part])

    grads, delta, new_m, new_v = {}, {}, {}, {}
    t_form = [n for n in COL_SHARDED if ws[n].shape[-1] % LANES]
    view = {n: (lambda a: a.reshape(a.shape[-2], a.shape[-1]).T) if n in t_form
            else (lambda a: a.reshape(a.shape[-2], a.shape[-1])) for n in BIG}
    unview = {n: (lambda a: a.T) if n in t_form else (lambda a: a) for n in BIG}

    group = [n for n in BIG if n != "w_gate"]
    sums, ((late_all,),) = _sum_ready([parts[n] for n in group], "sum_ready", [_AllGather(late_pack)],
                                      chip_sums=(group.index("ffn1_w1"), group.index("ffn1_w3")))
    res = _adamw_cols([(g, view[n](ws[n]), view[n](ms[n]), view[n](vs[n])) for n, g in zip(group, sums)], "adamw_big")
    for n, quad in zip(group, res):
        grads[n], delta[n], new_m[n], new_v[n] = (unview[n](a) for a in quad)
    (g_gate_t,) = _sum_parts([parts["w_gate"]])
    grads["w_gate"] = g_gate_t.T
    ((delta["w_gate"], new_m["w_gate"], new_v["w_gate"]),) = _adamw(
        [(grads["w_gate"], view["w_gate"](ws["w_gate"]), view["w_gate"](ms["w_gate"]), view["w_gate"](vs["w_gate"]))])
    for group, gathered, offs in ((early, early_all, early_offs), (late, late_all, late_offs)):
        total = _sum_small(gathered, gathered.shape[0] // NDEV)
        shapes = [(CONV_WIDTH, D) if n == "conv_w" else ws[n].shape for n in group]
        if group is late:
            shapes = shapes + [(1, LANES)]
        unpacked = _unpack_small(total, offs, shapes)
        if group is late:
            loss = unpacked.pop()[0, 0]
        for n, g in zip(group, unpacked):
            grads[n] = g
    grads["conv_w"] = lax.dynamic_slice(grads["conv_w"], (0, me * LANES), (CONV_WIDTH, LANES)).reshape(conv_w.shape)

    flat2d = lambda a: a.reshape(-1, a.shape[-1])
    res = _adamw_small([(flat2d(grads[n].reshape(ws[n].shape)), flat2d(ws[n]), flat2d(ms[n]), flat2d(vs[n]))
                        for n in SMALL])
    for n, (d_, m_, v_) in zip(SMALL, res):
        delta[n], new_m[n], new_v[n] = d_, m_, v_

    outs = [loss, grad_x.reshape(x.shape)]
    for src in (grads, delta, new_m, new_v):
        outs += [src[n].reshape(ws[n].shape) for n in names]
    return tuple(outs)
```

```python
import functools
import math
import operator

import numpy as np
import jax
import jax.numpy as jnp
from jax import lax
from jax.experimental import pallas as pl
from jax.experimental.pallas import tpu as pltpu

F32, BF16 = jnp.float32, jnp.bfloat16

NDEV = 8
D = 1024
FF = 2816
N_HEADS, N_KV, HEAD_DIM = 16, 4, 64
CHUNK, WINDOW = 64, 128
N_BUCKETS, MAX_DISTANCE = 32, 128
LRU_BLOCK = 64
CONV_WIDTH = 4
LRU_C = 8.0
RMS_EPS = 1e-6
NEG_INF = -1e30
LANES = 128
QT = 128
KW = QT + WINDOW
LRU_ROWS = 1024
IN_W = D + 2 * N_KV * HEAD_DIM + 2 * D
INP_W = D + 2 * N_KV * LANES + 2 * D
VMEM_BIG = 58 * 2 ** 20

ADAM_LR, ADAM_B1, ADAM_B2, ADAM_EPS, ADAM_WD, ADAM_STEP = 0.001, 0.9, 0.999, 1e-08, 0.01, 10

GROUPS = (("ffn1", (("ffn1_w1", FF // NDEV), ("ffn1_w3", FF // NDEV), ("ffn1_w2", FF // NDEV), ("conv", 16))),
          ("mixin", (("w_in", IN_W // NDEV), ("w_gate", 2 * D // NDEV))),
          ("mixout", (("w_lru_out", D // NDEV), ("w_attn_out", D // NDEV), ("w_o", D // NDEV))),
          ("ffn2a", (("ffn2_w1", FF // NDEV), ("ffn2_w3", FF // NDEV))),
          ("ffn2b", (("ffn2_w2", FF // NDEV),)))
LOC, GROUP_ROWS = {}, {}
for _g, _members in GROUPS:
    _o = 0
    for _n, _r in _members:
        LOC[_n] = (_g, _o, _r)
        _o += _r
    GROUP_ROWS[_g] = _o
BIG = tuple(n for _, members in GROUPS for n, _ in members if n != "conv")
COL_SHARDED = ("ffn1_w1", "ffn1_w3", "w_in", "w_gate", "ffn2_w1", "ffn2_w3")

SMALL = ("ffn1_pre_g", "ffn1_post_g", "mix_pre_g", "conv_w", "conv_b", "rg_a_w", "rg_a_b", "rg_x_w", "rg_x_b",
         "lru_lambda", "attn_sinks", "rel_bias", "b_gate", "mix_post_g", "ffn2_pre_g", "ffn2_post_g")

MESH = pl.DeviceIdType.MESH
ANY = pl.BlockSpec(memory_space=pl.ANY)
DMA = pltpu.SemaphoreType.DMA


def _nn(a, b):
    return lax.dot_general(a, b, (((1,), (0,)), ((), ())), preferred_element_type=F32)


def _nt(a, b):
    return lax.dot_general(a, b, (((1,), (1,)), ((), ())), preferred_element_type=F32)


def _tn(a, b):
    return lax.dot_general(a, b, (((0,), (0,)), ((), ())), preferred_element_type=F32)


def _rms_fwd(x, g):
    r = lax.rsqrt(jnp.mean(x * x, axis=-1, keepdims=True) + RMS_EPS)
    xh = x * r
    return xh * g, xh, r


def _rms_bwd(dn, xh, r, g):
    dxh = dn * g
    dx = r * (dxh - xh * jnp.mean(dxh * xh, axis=-1, keepdims=True))
    return dx, jnp.sum(dn * xh, axis=0, keepdims=True)


def _row_spec(tm, c):
    return pl.BlockSpec((tm, c), lambda i: (i, 0))


def _const_spec(shape):
    nd = len(shape)
    return pl.BlockSpec(shape, lambda i: (0,) * nd)


class _AllGather:
    waits_for_all = True

    def __init__(self, shard):
        self.m, n = shard.shape
        self.ins = [shard]
        self.out_shape = [jax.ShapeDtypeStruct((NDEV * self.m, n), shard.dtype)]
        self.scratch = [DMA((7,)), DMA((7,)), DMA]

    def _copies(self, ins, outs, scr, all_of_them):
        x_ref, out_ref = ins[0], outs[0]
        send_sems, recv_sems, local_sem = scr
        x, y, c = lax.axis_index("x"), lax.axis_index("y"), lax.axis_index("c")
        me, sibling = (x, y, c), (x, y, 1 - c)
        chips = [(1 - x, y), (x, 1 - y), (1 - x, 1 - y)]
        m = self.m

        def rows(px, py, pc):
            return out_ref.at[pl.ds((4 * px + 2 * py + pc) * m, m), :]

        def copy(k, block, to, src=None):
            return pltpu.make_async_remote_copy(
                src_ref=rows(*block) if src is None else src, dst_ref=rows(*block),
                send_sem=send_sems.at[k], recv_sem=recv_sems.at[k], device_id=to, device_id_type=MESH)

        mine = pltpu.make_async_copy(x_ref, rows(*me), local_sem)
        first = [copy(0, me, sibling, src=x_ref)] + [copy(1 + j, me, (*chip, c), src=x_ref)
                                                     for j, chip in enumerate(chips)]
        if not all_of_them:
            return mine, first
        passed = [copy(4 + j, (*chip, c), sibling) for j, chip in enumerate(chips)]
        landed = [copy(1 + j, (*chip, c), me) for j, chip in enumerate(chips)]
        from_sibling = [copy(0, sibling, me)] + [copy(4 + j, (*chip, 1 - c), me) for j, chip in enumerate(chips)]
        return mine, first, passed, landed, from_sibling

    def start(self, ins, outs, scr):
        mine, first = self._copies(ins, outs, scr, False)
        mine.start()
        for cp in first:
            cp.start()

    def pass_on(self, ins, outs, scr):
        _, _, passed, landed, _ = self._copies(ins, outs, scr, True)
        for cp_in, cp_on in zip(landed, passed):
            cp_in.wait_recv()
            cp_on.start()

    def finish(self, ins, outs, scr):
        mine, first, passed, landed, from_sibling = self._copies(ins, outs, scr, True)
        for cp in from_sibling:
            cp.wait_recv()
        for cp in first + passed:
            cp.wait_send()
        mine.wait()


class _AllGatherTree:
    waits_for_all = True

    def __init__(self, shard):
        self.m, n = shard.shape
        self.half = self.m // 32 * 16
        self.ins = [shard]
        self.out_shape = [jax.ShapeDtypeStruct((NDEV * self.m, n), shard.dtype)]
        self.scratch = [DMA((9,)), DMA((9,)), DMA]

    def start(self, ins, outs, scr):
        x_ref, out_ref = ins[0], outs[0]
        send_sems, recv_sems, local_sem = scr
        x, y, c = lax.axis_index("x"), lax.axis_index("y"), lax.axis_index("c")
        m, half = self.m, self.half
        me, sibling, xn, yn, diag = (x, y, c), (x, y, 1 - c), (1 - x, y, c), (x, 1 - y, c), (1 - x, 1 - y, c)
        other = lambda dev: (dev[0], dev[1], 1 - c)

        def rows(dev, lo=0, n=m):
            return out_ref.at[pl.ds((4 * dev[0] + 2 * dev[1] + dev[2]) * m + lo, n), :]

        def copy(k, dev, to, lo=0, n=m, src=None):
            return pltpu.make_async_remote_copy(
                src_ref=rows(dev, lo, n) if src is None else src, dst_ref=rows(dev, lo, n),
                send_sem=send_sems.at[k], recv_sem=recv_sems.at[k], device_id=to, device_id_type=MESH)

        mine = pltpu.make_async_copy(x_ref, rows(me), local_sem)
        own = [copy(0, me, sibling, src=x_ref), copy(1, me, xn, src=x_ref), copy(2, me, yn, src=x_ref)]
        mine.start()
        for cp in own:
            cp.start()
        steps = [(copy(1, xn, me), [copy(3, xn, yn, 0, half), copy(5, xn, sibling)]),
                 (copy(2, yn, me), [copy(4, yn, xn, half, m - half), copy(6, yn, sibling)]),
                 (copy(3, diag, me, 0, half), [copy(7, diag, sibling, 0, half)]),
                 (copy(4, diag, me, half, m - half), [copy(8, diag, sibling, half, m - half)])]
        from_sibling = [copy(0, sibling, me), copy(5, other(xn), me), copy(6, other(yn), me),
                        copy(7, other(diag), me, 0, half), copy(8, other(diag), me, half, m - half)]
        passed = []
        for landing, onward in steps:
            landing.wait_recv()
            for cp in onward:
                cp.start()
            passed += onward
        for cp in from_sibling:
            cp.wait_recv()
        for cp in own + passed:
            cp.wait_send()
        mine.wait()

    def finish(self, ins, outs, scr):
        pass


class _ReduceScatterSend:
    def __init__(self, grads, chip_sums=False):
        grads = [g if isinstance(g, tuple) else (g, 0, g.shape[0] // NDEV, g.shape[0] // NDEV) for g in grads]
        self.peers = (2, 4, 6) if chip_sums else tuple(range(1, NDEV))
        self.waits_for_all = not chip_sums
        self.nw = len(grads)
        self.base = [b for _, b, _, _ in grads]
        self.stride = [s for _, _, s, _ in grads]
        self.rows = [r for _, _, _, r in grads]
        self.ins = [g for g, _, _, _ in grads]
        self.out_shape = [jax.ShapeDtypeStruct((NDEV, r, g.shape[1]), g.dtype) for g, _, _, r in grads]
        self.scratch = [DMA((self.nw, NDEV - 1)), DMA((self.nw, NDEV - 1)), DMA((self.nw,))]

    def _copies(self, g_refs, r_refs, scr, want):
        send_sems, recv_sems, local_sems = scr
        x, y, c = lax.axis_index("x"), lax.axis_index("y"), lax.axis_index("c")
        me = 4 * x + 2 * y + c
        rows, base, stride = self.rows, self.base, self.stride
        out = []
        if want == "local":
            for w in range(self.nw):
                out.append(pltpu.make_async_copy(g_refs[w].at[pl.ds(base[w] + me * stride[w], rows[w])],
                                                 r_refs[w].at[me], local_sems.at[w]))
            return out
        for k in self.peers:
            px, py, pc = x ^ (k >> 2), y ^ ((k >> 1) & 1), c ^ (k & 1)
            peer = 4 * px + 2 * py + pc
            for w in range(self.nw):
                sems = dict(send_sem=send_sems.at[w, k - 1], recv_sem=recv_sems.at[w, k - 1],
                            device_id=(px, py, pc), device_id_type=MESH)
                if want == "send":
                    out.append(pltpu.make_async_remote_copy(
                        src_ref=g_refs[w].at[pl.ds(base[w] + peer * stride[w], rows[w])], dst_ref=r_refs[w].at[me],
                        **sems))
                else:
                    out.append(pltpu.make_async_remote_copy(
                        src_ref=g_refs[w].at[pl.ds(0, rows[w])], dst_ref=r_refs[w].at[peer], **sems))
        return out

    def start(self, ins, outs, scr):
        for cp in self._copies(ins, outs, scr, "local") + self._copies(ins, outs, scr, "send"):
            cp.start()

    def finish(self, ins, outs, scr):
        for cp in self._copies(ins, outs, scr, "recv"):
            cp.wait_recv()
        for cp in self._copies(ins, outs, scr, "send"):
            cp.wait_send()
        for cp in self._copies(ins, outs, scr, "local"):
            cp.wait()


def _mesh_barrier():
    barrier = pltpu.get_barrier_semaphore()
    x, y, c = lax.axis_index("x"), lax.axis_index("y"), lax.axis_index("c")
    for k in range(1, NDEV):
        pl.semaphore_signal(barrier, inc=1, device_id=(x ^ (k >> 2), y ^ ((k >> 1) & 1), c ^ (k & 1)),
                            device_id_type=MESH)
    pl.semaphore_wait(barrier, NDEV - 1)


def _call(body, *, name, grid, ins, in_specs, out_shape, out_specs, scratch=(), vmem=None, carries=()):
    n_in, n_out, n_scr = len(ins), len(out_shape), len(scratch)
    ng = len(grid)
    own_barrier = any(getattr(c, "waits_for_all", False) for c in carries)

    def split(refs):
        pos = [0]

        def take(k):
            part = refs[pos[0]:pos[0] + k]
            pos[0] += k
            return part

        i_refs = take(n_in)
        c_in = [take(len(c.ins)) for c in carries]
        o_refs = take(n_out)
        c_out = [take(len(c.out_shape)) for c in carries]
        s_refs = take(n_scr)
        c_scr = [take(len(c.scratch)) for c in carries]
        return i_refs, o_refs, s_refs, list(zip(carries, c_in, c_out, c_scr))

    def full(*refs):
        i_refs, o_refs, s_refs, cparts = split(refs)
        relays = [part for part in cparts if hasattr(part[0], "pass_on")]
        if ng == 0:
            if own_barrier:
                _mesh_barrier()
            for c, a, b, s in cparts:
                c.start(a, b, s)
            for c, a, b, s in relays:
                c.pass_on(a, b, s)
            for c, a, b, s in cparts:
                c.finish(a, b, s)
            return
        ids = [pl.program_id(a) for a in range(ng)]
        ahead = grid[-1] >= 2
        if cparts:
            @pl.when(functools.reduce(operator.and_, [i == 0 for i in ids]))
            def _():
                if own_barrier:
                    _mesh_barrier()
                for c, a, b, s in cparts:
                    c.start(a, b, s)

        if relays and ahead:
            @pl.when(functools.reduce(operator.and_, [i == g - 1 for i, g in zip(ids[:-1], grid[:-1])]
                                      + [ids[-1] == grid[-1] - 2]))
            def _():
                for c, a, b, s in relays:
                    c.pass_on(a, b, s)

        body(*i_refs, *o_refs, *s_refs)
        if cparts:
            @pl.when(functools.reduce(operator.and_, [i == g - 1 for i, g in zip(ids, grid)]))
            def _():
                if not ahead:
                    for c, a, b, s in relays:
                        c.pass_on(a, b, s)
                for c, a, b, s in cparts:
                    c.finish(a, b, s)

    all_ins = list(ins) + [a for c in carries for a in c.ins]
    all_in_specs = list(in_specs) + [ANY for c in carries for _ in c.ins]
    all_out_shape = list(out_shape) + [s for c in carries for s in c.out_shape]
    all_out_specs = list(out_specs) + [ANY for c in carries for _ in c.out_shape]
    all_scratch = list(scratch) + [s for c in carries for s in c.scratch]
    kwargs = dict(grid=grid) if ng else {}
    outs = pl.pallas_call(
        full, name=name, in_specs=all_in_specs, out_specs=all_out_specs, out_shape=all_out_shape,
        scratch_shapes=all_scratch,
        compiler_params=pltpu.CompilerParams(dimension_semantics=("arbitrary",) * ng if ng else None,
                                             vmem_limit_bytes=vmem, collective_id=0 if own_barrier else None),
        **kwargs)(*all_ins)
    outs = list(outs)
    res, pos = outs[:n_out], n_out
    carried = []
    for c in carries:
        carried.append(outs[pos:pos + len(c.out_shape)])
        pos += len(c.out_shape)
    return res, carried


def _pair_sum(g, items):
    rows = items[0][1]
    n = len(items) * (NDEV // 2)
    width = g.shape[1]

    def body(g_ref, out_ref, got_ref, mine_v, got_v, send_sems, recv_sems, ld_sems, st_sems):
        x, y, c = lax.axis_index("x"), lax.axis_index("y"), lax.axis_index("c")
        keep = [base + (2 * ch + c) * rows for base, _ in items for ch in range(NDEV // 2)]
        give = [base + (2 * ch + 1 - c) * rows for base, _ in items for ch in range(NDEV // 2)]

        def remote(j, r):
            return pltpu.make_async_remote_copy(
                src_ref=g_ref.at[pl.ds(r, rows)], dst_ref=got_ref.at[pl.ds(r, rows)], send_sem=send_sems.at[j],
                recv_sem=recv_sems.at[j], device_id=(x, y, 1 - c), device_id_type=MESH)

        sends = [remote(j, r) for j, r in enumerate(give)]
        for cp in sends:
            cp.start()
        loads = [pltpu.make_async_copy(g_ref.at[pl.ds(r, rows)], mine_v.at[j], ld_sems.at[0, j])
                 for j, r in enumerate(keep)]
        for cp in loads:
            cp.start()
        stores = []
        for j, r in enumerate(keep):
            remote(j, r).wait_recv()
            cp = pltpu.make_async_copy(got_ref.at[pl.ds(r, rows)], got_v.at[j], ld_sems.at[1, j])
            cp.start()
            loads[j].wait()
            cp.wait()
            mine_v[j] = (mine_v[j].astype(F32) + got_v[j].astype(F32)).astype(BF16)
            st = pltpu.make_async_copy(mine_v.at[j], out_ref.at[pl.ds(r, rows)], st_sems.at[j])
            st.start()
            stores.append(st)
        for cp in sends:
            cp.wait_send()
        for cp in stores:
            cp.wait()

    out, _ = pl.pallas_call(
        body, name="pair_sum", in_specs=[ANY], out_specs=[ANY, ANY],
        out_shape=[jax.ShapeDtypeStruct(g.shape, g.dtype), jax.ShapeDtypeStruct(g.shape, g.dtype)],
        scratch_shapes=[pltpu.VMEM((n, rows, width), g.dtype), pltpu.VMEM((n, rows, width), g.dtype),
                        DMA((n,)), DMA((n,)), DMA((2, n)), DMA((n,))],
        compiler_params=pltpu.CompilerParams(vmem_limit_bytes=40 * 2 ** 20))(g)
    return out


def _groups_of(names):
    out = []
    for n in names:
        if LOC[n][0] not in out:
            out.append(LOC[n][0])
    return out


def _weight_pieces(name):
    g, off, rows = LOC[name]
    return [(d * GROUP_ROWS[g] + off, d * rows, rows) for d in range(NDEV)]


def _win_pieces():
    kv = N_KV * HEAD_DIM
    pieces = [(0, 0, D)]
    for g0, d0 in ((D, D), (D + kv, D + N_KV * LANES)):
        for g in range(N_KV):
            for half in range(2):
                pieces.append((g0 + g * HEAD_DIM, d0 + g * LANES + half * HEAD_DIM, HEAD_DIM))
    pieces.append((D + 2 * kv, D + 2 * N_KV * LANES, D))
    pieces.append((D + 2 * kv + D, D + 2 * N_KV * LANES + D, D))
    grp, off, rows = LOC["w_in"]
    out = []
    for g0, d0, n in pieces:
        while n > 0:
            dev, loc = divmod(g0, rows)
            m = min(n, rows - loc)
            out.append((dev * GROUP_ROWS[grp] + off + loc, d0, m))
            g0, d0, n = g0 + m, d0 + m, n - m
    return out


def _start_loads(src_ref, dst_ref, pieces, sems, base):
    cps = []
    for j, (s, d, n) in enumerate(pieces):
        cp = pltpu.make_async_copy(src_ref.at[pl.ds(s, n)], dst_ref.at[pl.ds(d, n)], sems.at[base + j])
        cp.start()
        cps.append(cp)
    return cps


def _load_weights(wrefs, targets, sems):
    cps, base = [], 0
    for name, dst in targets:
        pieces = _win_pieces() if name == "w_in" else _weight_pieces(name)
        cps += _start_loads(wrefs[LOC[name][0]], dst, pieces, sems, base)
        base += len(pieces)
    for cp in cps:
        cp.wait()


def _n_pieces(names):
    return sum(len(_win_pieces()) if n == "w_in" else NDEV for n in names)


def _dw(lhs, rhs, chunk, name, carries=()):
    nq = 4
    if lhs.ndim == 3:
        nch, t_tok, chunk = lhs.shape
        c = nch * chunk
        tq = t_tok // nq
        lhs_specs = [pl.BlockSpec((None, tq, chunk), lambda i, k=k: (i, k, 0)) for k in range(nq)]
    else:
        t_tok, c = lhs.shape
        tq = t_tok // nq
        lhs_specs = [pl.BlockSpec((tq, chunk), lambda i, k=k: (k, i)) for k in range(nq)]

    def body(*refs):
        lhs_refs, (rhs_ref, out_ref, rhs_s, sems) = refs[:nq], refs[nq:]
        cps = [pltpu.make_async_copy(rhs_ref.at[pl.ds(k * tq, tq)], rhs_s.at[pl.ds(k * tq, tq)], sems.at[k])
               for k in range(nq)]
        first = pl.program_id(0) == 0

        @pl.when(first)
        def _():
            for cp in cps:
                cp.start()

        for o in range(0, chunk, D // 2):
            n = min(D // 2, chunk - o)
            acc = None
            for k in range(nq):
                if o == 0:
                    @pl.when(first)
                    def _():
                        cps[k].wait()

                part = _tn(lhs_refs[k][:, pl.ds(o, n)], rhs_s[pl.ds(k * tq, tq), :])
                acc = part if acc is None else acc + part
            out_ref[pl.ds(o, n), :] = acc.astype(BF16)

    (out,), carried = _call(
        body, name=name, grid=(c // chunk,), ins=[lhs] * nq + [rhs],
        in_specs=lhs_specs + [ANY],
        out_specs=[pl.BlockSpec((chunk, D), lambda i: (i, 0))],
        out_shape=[jax.ShapeDtypeStruct((c, D), BF16)],
        scratch=[pltpu.VMEM((t_tok, D), BF16), DMA((nq,))], vmem=VMEM_BIG, carries=carries)
    return out, carried


def _silu_parts(a):
    sig = jax.nn.sigmoid(a)
    return sig, a * sig


FC = 256


def _ffn_fwd(h, gpre, gpost, wg, names, target=None, tm=512, carries=()):
    t_tok = h.shape[0]
    nt = t_tok // tm
    with_loss = target is not None
    groups = _groups_of(names)

    def body(*refs):
        refs = list(refs)
        h_ref, gpre_ref, gpost_ref = refs[:3]
        del refs[:3]
        tgt_ref = refs.pop(0) if with_loss else None
        wrefs = dict(zip(groups, refs[:len(groups)]))
        del refs[:len(groups)]
        hout_ref, a_ref, b_ref, f_ref, nb_ref = refs[:5]
        del refs[:5]
        if with_loss:
            dy_ref, loss_ref = refs[:2]
            del refs[:2]
        w1_s, w3_s, w2_s, sems = refs
        i = pl.program_id(0)

        @pl.when(i == 0)
        def _():
            _load_weights(wrefs, list(zip(names, (w1_s, w3_s, w2_s))), sems)
            if with_loss:
                loss_ref[...] = jnp.zeros_like(loss_ref)

        x = h_ref[...]
        n, _, _ = _rms_fwd(x, gpre_ref[...])
        nb = n.astype(BF16)
        nb_ref[...] = nb
        f = jnp.zeros((tm, D), F32)
        for c0 in range(0, FF, FC):
            a = _nt(nb, w1_s[pl.ds(c0, FC), :])
            b = _nt(nb, w3_s[pl.ds(c0, FC), :])
            _, sl = _silu_parts(a)
            a_ref[:, pl.ds(c0, FC)] = a.astype(BF16)
            b_ref[:, pl.ds(c0, FC)] = b.astype(BF16)
            f = f + _nn((sl * b).astype(BF16), w2_s[pl.ds(c0, FC), :])
        f_ref[...] = f
        fn, _, _ = _rms_fwd(f, gpost_ref[...])
        y = x + 0.5 * fn
        hout_ref[...] = y
        if with_loss:
            err = y - tgt_ref[...]
            dy_ref[...] = err * (1.0 / D)
            loss_ref[...] += jnp.sum(jnp.sum(err * err, axis=-1, keepdims=True), axis=0, keepdims=True) * (0.5 / D)

    ins = [h, gpre, gpost] + ([target] if with_loss else []) + [wg[g] for g in groups]
    in_specs = [_row_spec(tm, D), _const_spec((1, D)), _const_spec((1, D))]
    in_specs += ([_row_spec(tm, D)] if with_loss else []) + [ANY] * len(groups)
    out_shape = [jax.ShapeDtypeStruct((t_tok, D), F32), jax.ShapeDtypeStruct((t_tok, FF), BF16),
                 jax.ShapeDtypeStruct((t_tok, FF), BF16), jax.ShapeDtypeStruct((t_tok, D), F32),
                 jax.ShapeDtypeStruct((t_tok, D), BF16)]
    out_specs = [_row_spec(tm, D), _row_spec(tm, FF), _row_spec(tm, FF), _row_spec(tm, D), _row_spec(tm, D)]
    if with_loss:
        out_shape += [jax.ShapeDtypeStruct((t_tok, D), F32), jax.ShapeDtypeStruct((1, LANES), F32)]
        out_specs += [_row_spec(tm, D), _const_spec((1, LANES))]
    return _call(body, name="ffn_fwd_" + names[0][:4], grid=(nt,), ins=ins, in_specs=in_specs,
                 out_shape=out_shape, out_specs=out_specs,
                 scratch=[pltpu.VMEM((FF, D), BF16)] * 3 + [DMA((3 * NDEV,))], vmem=VMEM_BIG, carries=carries)


FH = FF // 2
HALF_PIECES = ((0, 256), (256, 256), (512, 256), (768, 256), (1024, 256), (1280, 128))


def _ffn_bwd_a(dh, f, a, b, gpost, wg, name_w2, tm=512, carries=()):
    t_tok = dh.shape[0]
    nt = t_tok // tm
    groups = _groups_of([name_w2])

    def body(dh_ref, f_ref, a_ref, b_ref, gpost_ref, wg_ref, dab_ref, s_ref, df_ref, dgp_ref, w2_s, sems):
        i = pl.program_id(0)

        @pl.when(i == 0)
        def _():
            _load_weights({groups[0]: wg_ref}, [(name_w2, w2_s)], sems)
            dgp_ref[...] = jnp.zeros_like(dgp_ref)

        fv = f_ref[...]
        _, fh, r = _rms_fwd(fv, gpost_ref[...])
        df, dg = _rms_bwd(0.5 * dh_ref[...], fh, r, gpost_ref[...])
        dgp_ref[...] += dg
        dfb = df.astype(BF16)
        df_ref[...] = dfb
        for half in range(2):
            for o, n in HALF_PIECES:
                c0 = half * FH + o
                ds = _nt(dfb, w2_s[pl.ds(c0, n), :])
                av = a_ref[:, pl.ds(c0, n)].astype(F32)
                bv = b_ref[:, pl.ds(c0, n)].astype(F32)
                sig, sl = _silu_parts(av)
                dab_ref[half, :, pl.ds(o, n)] = (ds * bv * (sig * (1.0 + av * (1.0 - sig)))).astype(BF16)
                dab_ref[2 + half, :, pl.ds(o, n)] = (ds * sl).astype(BF16)
                s_ref[half, :, pl.ds(o, n)] = (sl * bv).astype(BF16)

    return _call(
        body, name="ffn_bwd_a_" + name_w2[:4], grid=(nt,), ins=[dh, f, a, b, gpost, wg[groups[0]]],
        in_specs=[_row_spec(tm, D), _row_spec(tm, D), _row_spec(tm, FF), _row_spec(tm, FF), _const_spec((1, D)), ANY],
        out_specs=[pl.BlockSpec((4, tm, FH), lambda i: (0, i, 0)), pl.BlockSpec((2, tm, FH), lambda i: (0, i, 0)),
                   _row_spec(tm, D), _const_spec((1, D))],
        out_shape=[jax.ShapeDtypeStruct((4, t_tok, FH), BF16), jax.ShapeDtypeStruct((2, t_tok, FH), BF16),
                   jax.ShapeDtypeStruct((t_tok, D), BF16), jax.ShapeDtypeStruct((1, D), F32)],
        scratch=[pltpu.VMEM((FF, D), BF16), DMA((NDEV,))],
        vmem=VMEM_BIG, carries=carries)


def _ffn_bwd_dx(dab, h, dh, gpre, wg, name_w1, name_w3, tm=512, carries=()):
    t_tok = dh.shape[0]
    nt = t_tok // tm
    groups = _groups_of([name_w1, name_w3])

    def body(*refs):
        dab_refs, (h_ref, dh_ref, gpre_ref) = refs[:4], refs[4:7]
        wrefs = dict(zip(groups, refs[7:7 + len(groups)]))
        dhin_ref, dgp_ref, w13_s, sems = refs[7 + len(groups):]
        i = pl.program_id(0)

        @pl.when(i == 0)
        def _():
            _load_weights(wrefs, [(name_w1, w13_s.at[pl.ds(0, FF)]), (name_w3, w13_s.at[pl.ds(FF, FF)])], sems)
            dgp_ref[...] = jnp.zeros_like(dgp_ref)

        g = gpre_ref[...]
        _, xh, r = _rms_fwd(h_ref[...], g)
        dn = _nn(dab_refs[0][...], w13_s[pl.ds(0, FH), :])
        for k in range(1, 4):
            dn = dn + _nn(dab_refs[k][...], w13_s[pl.ds(k * FH, FH), :])
        dx, dg = _rms_bwd(dn, xh, r, g)
        dgp_ref[...] += dg
        dhin_ref[...] = dh_ref[...] + dx

    return _call(
        body, name="ffn_bwd_dx_" + name_w1[:4], grid=(nt,), ins=[dab] * 4 + [h, dh, gpre] + [wg[g] for g in groups],
        in_specs=[pl.BlockSpec((None, tm, FH), lambda i, k=k: (k, i, 0)) for k in range(4)]
        + [_row_spec(tm, D), _row_spec(tm, D), _const_spec((1, D))] + [ANY] * len(groups),
        out_specs=[_row_spec(tm, D), _const_spec((1, D))],
        out_shape=[jax.ShapeDtypeStruct((t_tok, D), F32), jax.ShapeDtypeStruct((1, D), F32)],
        scratch=[pltpu.VMEM((2 * FF, D), BF16), DMA((2 * NDEV,))],
        vmem=VMEM_BIG, carries=carries)


Q0, K0, V0, XR0, XG0 = 0, D, D + N_KV * LANES, D + 2 * N_KV * LANES, 2 * D + 2 * N_KV * LANES


def _mix_proj_fwd(h, g, bgate, wg, tm=512, carries=()):
    t_tok = h.shape[0]
    nt = t_tok // tm
    names = ("w_in", "w_gate")
    groups = _groups_of(names)

    def body(h_ref, g_ref, bg_ref, wg_ref, q_ref, k_ref, v_ref, xr_ref, xg_ref, gs_ref, ub_ref, win_s, wgt_s, sems):
        i = pl.program_id(0)

        @pl.when(i == 0)
        def _():
            _load_weights({groups[0]: wg_ref}, [("w_in", win_s), ("w_gate", wgt_s)], sems)

        n, _, _ = _rms_fwd(h_ref[...], g_ref[...])
        nb = n.astype(BF16)
        ub_ref[...] = nb
        q_ref[...] = _nt(nb, win_s[pl.ds(Q0, D), :]).astype(BF16)
        k_ref[...] = _nt(nb, win_s[pl.ds(K0, N_KV * LANES), :]).astype(BF16)
        v_ref[...] = _nt(nb, win_s[pl.ds(V0, N_KV * LANES), :]).astype(BF16)
        xr_ref[...] = _nt(nb, win_s[pl.ds(XR0, D), :])
        xg_ref[...] = _nt(nb, win_s[pl.ds(XG0, D), :])
        gs_ref[...] = jax.nn.sigmoid(_nt(nb, wgt_s[...]) + bg_ref[...]).astype(BF16)

    kvw = N_KV * LANES
    return _call(
        body, name="mix_proj_fwd", grid=(nt,), ins=[h, g, bgate, wg[groups[0]]],
        in_specs=[_row_spec(tm, D), _const_spec((1, D)), _const_spec((1, 2 * D)), ANY],
        out_specs=[_row_spec(tm, D), _row_spec(tm, kvw), _row_spec(tm, kvw), _row_spec(tm, D), _row_spec(tm, D),
                   _row_spec(tm, 2 * D), _row_spec(tm, D)],
        out_shape=[jax.ShapeDtypeStruct((t_tok, D), BF16), jax.ShapeDtypeStruct((t_tok, kvw), BF16),
                   jax.ShapeDtypeStruct((t_tok, kvw), BF16), jax.ShapeDtypeStruct((t_tok, D), F32),
                   jax.ShapeDtypeStruct((t_tok, D), F32), jax.ShapeDtypeStruct((t_tok, 2 * D), BF16),
                   jax.ShapeDtypeStruct((t_tok, D), BF16)],
        scratch=[pltpu.VMEM((INP_W, D), BF16), pltpu.VMEM((2 * D, D), BF16), DMA((_n_pieces(names),))],
        vmem=VMEM_BIG, carries=carries)


def _mix_proj_bwd(dqkv, dxr, dxg, dgpre, h, dh, g, wg, tm=512, carries=()):
    t_tok = h.shape[0]
    nt = t_tok // tm
    names = ("w_in", "w_gate")
    groups = _groups_of(names)

    def body(dqkv_ref, dxr_ref, dxg_ref, dgp_ref, h_ref, dh_ref, g_ref, wg_ref, dhin_ref, dg_ref, win_s, wgt_s, sems):
        i = pl.program_id(0)

        @pl.when(i == 0)
        def _():
            _load_weights({groups[0]: wg_ref}, [("w_in", win_s), ("w_gate", wgt_s)], sems)
            dg_ref[...] = jnp.zeros_like(dg_ref)

        gv = g_ref[...]
        _, xh, r = _rms_fwd(h_ref[...], gv)
        du = _nn(dgp_ref[...], wgt_s[...])
        du = du + _nn(dqkv_ref[...], win_s[pl.ds(Q0, XR0), :])
        du = du + _nn(dxr_ref[...], win_s[pl.ds(XR0, D), :])
        du = du + _nn(dxg_ref[...], win_s[pl.ds(XG0, D), :])
        dx, dg = _rms_bwd(du, xh, r, gv)
        dg_ref[...] += dg
        dhin_ref[...] = dh_ref[...] + dx

    return _call(
        body, name="mix_proj_bwd", grid=(nt,), ins=[dqkv, dxr, dxg, dgpre, h, dh, g, wg[groups[0]]],
        in_specs=[_row_spec(tm, XR0), _row_spec(tm, D), _row_spec(tm, D), _row_spec(tm, 2 * D), _row_spec(tm, D),
                  _row_spec(tm, D), _const_spec((1, D)), ANY],
        out_specs=[_row_spec(tm, D), _const_spec((1, D))],
        out_shape=[jax.ShapeDtypeStruct((t_tok, D), F32), jax.ShapeDtypeStruct((1, D), F32)],
        scratch=[pltpu.VMEM((INP_W, D), BF16), pltpu.VMEM((2 * D, D), BF16), DMA((_n_pieces(names),))],
        vmem=VMEM_BIG, carries=carries)


def _shift_down(x, d, fill):
    row = lax.broadcasted_iota(jnp.int32, x.shape, 0)
    return jnp.where(row >= d, pltpu.roll(x, d, 0), fill)


def _shift_up(x, d, fill):
    rows = x.shape[0]
    row = lax.broadcasted_iota(jnp.int32, x.shape, 0)
    return jnp.where(row < rows - d, pltpu.roll(x, rows - d, 0), fill)


def _scan_rows(a, b, reverse):
    rows = a.shape[0]
    d = 1
    while d < rows:
        if d < 8:
            shift = _shift_up if reverse else _shift_down
            b = a * shift(b, d, 0.0) + b
            a = a * shift(a, d, 1.0)
        elif reverse:
            b = jnp.concatenate([a[:rows - d] * b[d:] + b[:rows - d], b[rows - d:]], axis=0)
            a = jnp.concatenate([a[:rows - d] * a[d:], a[rows - d:]], axis=0)
        else:
            b = jnp.concatenate([b[:d], a[d:] * b[:rows - d] + b[d:]], axis=0)
            a = jnp.concatenate([a[:d], a[d:] * a[:rows - d]], axis=0)
        d *= 2
    return a, b


def _softplus(x):
    return jnp.maximum(x, 0.0) + jnp.log(1.0 + jnp.exp(-jnp.abs(x)))


_GELU_C = math.sqrt(2.0 / math.pi)


def _gelu_parts(x):
    th = jnp.tanh(_GELU_C * (x + 0.044715 * x * x * x))
    val = 0.5 * x * (1.0 + th)
    grad = 0.5 * (1.0 + th) + 0.5 * x * (1.0 - th * th) * _GELU_C * (1.0 + 3.0 * 0.044715 * x * x)
    return val, grad


def _lru_pre(x, halo, cw_ref, cb_ref, wa_ref, wx_ref, ba_ref, bx_ref, lam_ref):
    ext = jnp.concatenate([halo, x], axis=0)
    shifted = [x] + [pltpu.roll(ext, k, 0)[8:] for k in (1, 2, 3)]
    xc = cb_ref[...] + cw_ref[pl.ds(CONV_WIDTH - 1, 1), :] * x
    for k in (1, 2, 3):
        xc = xc + cw_ref[pl.ds(CONV_WIDTH - 1 - k, 1), :] * shifted[k]
    xcb = xc.astype(BF16)
    r = jax.nn.sigmoid(_nn(xcb, wa_ref[...]) + ba_ref[...])
    ig = jax.nn.sigmoid(_nn(xcb, wx_ref[...]) + bx_ref[...])
    sp = _softplus(-lam_ref[...])
    log_a = -LRU_C * r * sp
    a = jnp.exp(log_a)
    th = jnp.tanh(log_a)
    mult = jnp.sqrt(-2.0 * th / (1.0 - th))
    return shifted, xc, xcb, r, ig, sp, a, mult


def _lru_specs(nt, rows, reverse):
    def tt(t):
        return nt - 1 - t if reverse else t
    tile = pl.BlockSpec((rows, LANES), lambda cb, t: (tt(t), cb))
    halo = pl.BlockSpec((8, LANES), lambda cb, t: (jnp.maximum(tt(t) * (rows // 8) - 1, 0), cb))
    vec = pl.BlockSpec((1, LANES), lambda cb, t: (0, cb))
    cw = pl.BlockSpec((CONV_WIDTH, LANES), lambda cb, t: (0, cb))
    mat = pl.BlockSpec((None, LANES, LANES), lambda cb, t: (cb, 0, 0))
    return tile, halo, vec, cw, mat


def _lru_fwd(xr, xg, cw, cb, wa, wx, ba, bx, lam, carries=()):
    t_tok = xr.shape[0]
    rows = min(LRU_ROWS, t_tok)
    nt = t_tok // rows

    def body(xr_ref, xg_ref, cw_ref, cb_ref, wa_ref, wx_ref, ba_ref, bx_ref, lam_ref, y_ref, h_ref,
             xc_ref, r_ref, ig_ref, a_ref, mult_ref, tail_s, hc_s):
        t = pl.program_id(1)

        @pl.when(t == 0)
        def _():
            tail_s[...] = jnp.zeros_like(tail_s)
            hc_s[...] = jnp.zeros_like(hc_s)

        x = xr_ref[...]
        _, xc, xcb, r, ig, _, a, mult = _lru_pre(x, tail_s[...], cw_ref, cb_ref, wa_ref, wx_ref, ba_ref, bx_ref, lam_ref)
        xc_ref[...] = xcb
        r_ref[...] = r.astype(BF16)
        ig_ref[...] = ig.astype(BF16)
        a_ref[...] = a
        mult_ref[...] = mult
        tail_s[...] = xr_ref[pl.ds(rows - 8, 8), :]
        acc_a, acc_b = _scan_rows(a, mult * (ig * xc), False)
        hv = acc_b + acc_a * hc_s[...]
        h_ref[...] = hv
        hc_s[...] = h_ref[pl.ds(rows - 1, 1), :]
        gl, _ = _gelu_parts(xg_ref[...])
        y_ref[...] = (hv * gl).astype(BF16)

    tile, _, vec, cws, mat = _lru_specs(nt, rows, False)
    return _call(
        body, name="lru_fwd", grid=(D // LANES, nt), ins=[xr, xg, cw, cb, wa, wx, ba, bx, lam],
        in_specs=[tile, tile, cws, vec, mat, mat, vec, vec, vec],
        out_specs=[tile] * 7,
        out_shape=[jax.ShapeDtypeStruct((t_tok, D), dt) for dt in (BF16, F32, BF16, BF16, BF16, F32, F32)],
        scratch=[pltpu.VMEM((8, LANES), F32), pltpu.VMEM((1, LANES), F32)], carries=carries)


def _lru_bwd(dy, xr, xg, hseq, gates, cw, cb, wa, wx, ba, bx, lam, carries=()):
    t_tok = xr.shape[0]
    rows = min(LRU_ROWS // 2, t_tok)
    nt = t_tok // rows

    def body(dy_ref, xr_ref, xrh_ref, xg_ref, h_ref, hh_ref, xc_ref, r_ref, ig_ref, a_ref, mult_ref,
             cw_ref, cb_ref, wa_ref, wx_ref, ba_ref, bx_ref, lam_ref,
             dxr_ref, dxg_ref, dvec_ref, dwa_ref, dwx_ref, gcar_s, acar_s, head_s, tmp_s):
        t = pl.program_id(1)
        first_tile = t == nt - 1

        @pl.when(t == 0)
        def _():
            gcar_s[...] = jnp.zeros_like(gcar_s)
            acar_s[...] = jnp.zeros_like(acar_s)
            head_s[...] = jnp.zeros_like(head_s)
            dvec_ref[...] = jnp.zeros_like(dvec_ref)
            dwa_ref[...] = jnp.zeros_like(dwa_ref)
            dwx_ref[...] = jnp.zeros_like(dwx_ref)

        x = xr_ref[...]
        halo = jnp.where(first_tile, 0.0, xrh_ref[...])
        ext = jnp.concatenate([halo, x], axis=0)
        shifted = [x] + [pltpu.roll(ext, k, 0)[8:] for k in (1, 2, 3)]
        xcb = xc_ref[...]
        xc, r, ig = xcb.astype(F32), r_ref[...].astype(F32), ig_ref[...].astype(F32)
        a, mult = a_ref[...], mult_ref[...]
        sp = _softplus(-lam_ref[...])
        hv = h_ref[...]
        dyv = dy_ref[...]
        gl, glg = _gelu_parts(xg_ref[...])
        dxg_ref[...] = (dyv * hv * glg).astype(BF16)
        acc_a, acc_b = _scan_rows(_shift_up(a, 1, acar_s[...]), dyv * gl, True)
        g = acc_b + acc_a * gcar_s[...]
        hhalo = jnp.where(first_tile, 0.0, hh_ref[...])
        hprev = pltpu.roll(jnp.concatenate([hhalo, hv], axis=0), 1, 0)[8:]
        dmult = g * ig * xc
        dlog_a = a * (g * hprev) - dmult * a * a / mult
        dig = g * mult * xc
        dxc = g * mult * ig
        dzr = (dlog_a * (-LRU_C * sp)) * r * (1.0 - r)
        dzx = dig * ig * (1.0 - ig)
        dzrb, dzxb = dzr.astype(BF16), dzx.astype(BF16)
        dxc = dxc + _nt(dzrb, wa_ref[...]) + _nt(dzxb, wx_ref[...])
        dwa_ref[...] += _tn(xcb, dzrb)
        dwx_ref[...] += _tn(xcb, dzxb)
        dsp = jnp.sum(dlog_a * (-LRU_C * r), axis=0, keepdims=True)
        dlam = dsp * (-jax.nn.sigmoid(-lam_ref[...]))
        vrow = lax.broadcasted_iota(jnp.int32, (8, LANES), 0)
        upd = jnp.where(vrow == 4, jnp.sum(dxc, axis=0, keepdims=True), 0.0)
        upd = jnp.where(vrow == 5, jnp.sum(dzr, axis=0, keepdims=True), upd)
        upd = jnp.where(vrow == 6, jnp.sum(dzx, axis=0, keepdims=True), upd)
        upd = jnp.where(vrow == 7, dlam, upd)
        for k in range(CONV_WIDTH):
            upd = jnp.where(vrow == CONV_WIDTH - 1 - k, jnp.sum(dxc * shifted[k], axis=0, keepdims=True), upd)
        dvec_ref[...] += upd
        ext = jnp.concatenate([dxc, head_s[...]], axis=0)
        dxr = cw_ref[pl.ds(CONV_WIDTH - 1, 1), :] * dxc
        for k in (1, 2, 3):
            dxr = dxr + cw_ref[pl.ds(CONV_WIDTH - 1 - k, 1), :] * pltpu.roll(ext, rows + 8 - k, 0)[:rows]
        dxr_ref[...] = dxr.astype(BF16)
        tmp_s[...] = g
        gcar_s[...] = tmp_s[pl.ds(0, 1), :]
        tmp_s[...] = a
        acar_s[...] = tmp_s[pl.ds(0, 1), :]
        tmp_s[...] = dxc
        head_s[...] = tmp_s[pl.ds(0, 8), :]

    tile, halo, vec, cws, mat = _lru_specs(nt, rows, True)
    return _call(
        body, name="lru_bwd", grid=(D // LANES, nt),
        ins=[dy, xr, xr, xg, hseq, hseq, *gates, cw, cb, wa, wx, ba, bx, lam],
        in_specs=[tile, tile, halo, tile, tile, halo] + [tile] * 5 + [cws, vec, mat, mat, vec, vec, vec],
        out_specs=[tile, tile, pl.BlockSpec((8, LANES), lambda cb, t: (0, cb)), mat, mat],
        out_shape=[jax.ShapeDtypeStruct((t_tok, D), BF16), jax.ShapeDtypeStruct((t_tok, D), BF16),
                   jax.ShapeDtypeStruct((8, D), F32), jax.ShapeDtypeStruct((D // LANES, LANES, LANES), F32),
                   jax.ShapeDtypeStruct((D // LANES, LANES, LANES), F32)],
        scratch=[pltpu.VMEM((1, LANES), F32), pltpu.VMEM((1, LANES), F32), pltpu.VMEM((8, LANES), F32),
                 pltpu.VMEM((rows, LANES), F32)], carries=carries)


def _t5_bucket_np(rel):
    nb = N_BUCKETS // 2
    max_exact = nb // 2
    ret = np.where(rel > 0, nb, 0)
    n = np.abs(rel)
    nf = np.maximum(n, 1).astype(np.float32)
    large = max_exact + (np.log(nf / np.float32(max_exact)) / np.float32(math.log(MAX_DISTANCE / max_exact))
                         * np.float32(nb - max_exact)).astype(np.int32)
    large = np.minimum(large, nb - 1)
    return ret + np.where(n < max_exact, n, large)


def _bucket_map():
    r = np.arange(QT)[:, None]
    c = np.arange(KW)[None, :]
    j = c - (r // CHUNK) * CHUNK
    band = (j >= 0) & (j < WINDOW + CHUNK)
    return np.where(band, _t5_bucket_np(c - r - WINDOW), -1).astype(np.int32)


def _attn_specs(nt, reverse):
    def tt(i):
        return nt - 1 - i if reverse else i
    kvw = N_KV * LANES
    qs = pl.BlockSpec((QT, D), lambda i: (tt(i), 0))
    cur = pl.BlockSpec((QT, kvw), lambda i: (tt(i), 0))
    prev = pl.BlockSpec((WINDOW, kvw), lambda i: (jnp.maximum(tt(i) * (QT // WINDOW) - 1, 0), 0))
    lse = pl.BlockSpec((QT, LANES), lambda i: (tt(i), 0))
    return qs, cur, prev, lse


REP = N_HEADS // N_KV
SCALE = HEAD_DIM ** -0.5


def _stack_heads(x_ref, g, lo, scale=None):
    parts = []
    for hl in range(REP):
        xs = x_ref[:, pl.ds((2 * g + hl // 2) * LANES, LANES)]
        xs = jnp.where(lo if hl % 2 == 0 else jnp.logical_not(lo), xs, jnp.zeros_like(xs))
        parts.append(xs if scale is None else xs * jnp.asarray(scale, xs.dtype))
    return jnp.concatenate(parts, axis=0)


def _stack_sinks(sink_ref, g, srow):
    sk = jnp.full(srow.shape, sink_ref[REP * g + REP - 1], F32)
    for hl in range(REP - 2, -1, -1):
        sk = jnp.where(srow < (hl + 1) * QT, sink_ref[REP * g + hl], sk)
    return sk


def _attn_fwd(q, kd, vd, bias, sinks, carries=()):
    t_tok = q.shape[0]
    nt = t_tok // QT

    def body(q_ref, kp_ref, kc_ref, vp_ref, vc_ref, bias_ref, sink_ref, o_ref, lse_ref, p_ref):
        i = pl.program_id(0)
        col = lax.broadcasted_iota(jnp.int32, (1, KW), 1)
        first = jnp.where((i == 0) & (col < WINDOW), NEG_INF, 0.0)
        lane = lax.broadcasted_iota(jnp.int32, (QT, LANES), 1)
        lo = lane < HEAD_DIM
        srow = lax.broadcasted_iota(jnp.int32, (REP * QT, 1), 0)
        lse_t = jnp.zeros((QT, LANES), F32)
        for g in range(N_KV):
            kwin = jnp.concatenate([kp_ref[:, pl.ds(g * LANES, LANES)], kc_ref[:, pl.ds(g * LANES, LANES)]], axis=0)
            vwin = jnp.concatenate([vp_ref[:, pl.ds(g * LANES, LANES)], vc_ref[:, pl.ds(g * LANES, LANES)]], axis=0)
            qst = _stack_heads(q_ref, g, lo, SCALE)
            s = _nt(qst, kwin) + (bias_ref[g] + first)
            sk = _stack_sinks(sink_ref, g, srow)
            m = jnp.maximum(jnp.max(s, axis=-1, keepdims=True), sk)
            e = jnp.exp(s - m)
            l = jnp.sum(e, axis=-1, keepdims=True) + jnp.exp(sk - m)
            pb = (e / l).astype(BF16)
            p_ref[g] = pb
            ost = _nn(pb, vwin)
            lse_s = m + jnp.log(l)
            for sl in range(2):
                o_ref[:, pl.ds((2 * g + sl) * LANES, LANES)] = jnp.where(
                    lo, ost[2 * sl * QT:(2 * sl + 1) * QT], ost[(2 * sl + 1) * QT:(2 * sl + 2) * QT]).astype(BF16)
            for hl in range(REP):
                lse_t = jnp.where(lane == REP * g + hl, lse_s[hl * QT:(hl + 1) * QT], lse_t)
        lse_ref[...] = lse_t

    qs, cur, prev, lse = _attn_specs(nt, False)
    return _call(
        body, name="attn_fwd", grid=(nt,), ins=[q, kd, kd, vd, vd, bias, sinks],
        in_specs=[qs, prev, cur, prev, cur, _const_spec((N_KV, REP * QT, KW)), pl.BlockSpec(memory_space=pltpu.SMEM)],
        out_specs=[qs, lse, pl.BlockSpec((None, N_KV, REP * QT, KW), lambda i: (i, 0, 0, 0))],
        out_shape=[jax.ShapeDtypeStruct((t_tok, D), BF16), jax.ShapeDtypeStruct((t_tok, LANES), F32),
                   jax.ShapeDtypeStruct((nt, N_KV, REP * QT, KW), BF16)],
        vmem=48 * 2 ** 20, carries=carries)


def _attn_bwd(q, kd, vd, o, do, lse, probs, sinks, carries=()):
    t_tok = q.shape[0]
    nt = t_tok // QT
    kvw = N_KV * LANES

    def body(q_ref, kp_ref, kc_ref, vp_ref, vc_ref, o_ref, do_ref, lse_ref, p_ref, sink_ref,
             dqkv_ref, ds_ref, dsink_ref, kcar_s, vcar_s):
        i = pl.program_id(0)

        @pl.when(i == 0)
        def _():
            kcar_s[...] = jnp.zeros_like(kcar_s)
            vcar_s[...] = jnp.zeros_like(vcar_s)
            ds_ref[...] = jnp.zeros_like(ds_ref)
            dsink_ref[...] = jnp.zeros_like(dsink_ref)

        lane = lax.broadcasted_iota(jnp.int32, (QT, LANES), 1)
        lo = lane < HEAD_DIM
        lane_k = lax.broadcasted_iota(jnp.int32, (KW, LANES), 1)
        lane_1 = lax.broadcasted_iota(jnp.int32, (1, LANES), 1)
        srow = lax.broadcasted_iota(jnp.int32, (REP * QT, 1), 0)
        lse_t = lse_ref[...]
        dsink = jnp.zeros((1, LANES), F32)
        for g in range(N_KV):
            kwin = jnp.concatenate([kp_ref[:, pl.ds(g * LANES, LANES)], kc_ref[:, pl.ds(g * LANES, LANES)]], axis=0)
            vwin = jnp.concatenate([vp_ref[:, pl.ds(g * LANES, LANES)], vc_ref[:, pl.ds(g * LANES, LANES)]], axis=0)
            qst = _stack_heads(q_ref, g, lo, SCALE)
            dost = _stack_heads(do_ref, g, lo)
            od = [do_ref[:, pl.ds((2 * g + sl) * LANES, LANES)].astype(F32)
                  * o_ref[:, pl.ds((2 * g + sl) * LANES, LANES)].astype(F32) for sl in range(2)]
            drow = jnp.concatenate([jnp.sum(jnp.where(lo if hl % 2 == 0 else jnp.logical_not(lo), od[hl // 2], 0.0),
                                            axis=-1, keepdims=True) for hl in range(REP)], axis=0)
            lse_s = jnp.concatenate([jnp.sum(jnp.where(lane == REP * g + hl, lse_t, 0.0), axis=-1, keepdims=True)
                                     for hl in range(REP)], axis=0)
            pb = p_ref[g]
            p = pb.astype(F32)
            ds = p * (_nt(dost, vwin) - drow)
            ds_ref[g] += ds
            tsink = -(jnp.exp(_stack_sinks(sink_ref, g, srow) - lse_s) * drow)
            for hl in range(REP):
                dsink = dsink + jnp.where(lane_1 == REP * g + hl,
                                          jnp.sum(tsink[hl * QT:(hl + 1) * QT], axis=0, keepdims=True), 0.0)
            dsb = ds.astype(BF16)
            dqst = _nn(dsb, kwin) * SCALE
            for sl in range(2):
                dqkv_ref[:, pl.ds((2 * g + sl) * LANES, LANES)] = jnp.where(
                    lo, dqst[2 * sl * QT:(2 * sl + 1) * QT], dqst[(2 * sl + 1) * QT:(2 * sl + 2) * QT]).astype(BF16)
            dk_acc = _tn(dsb, qst)
            dv_acc = _tn(pb, dost)
            dk_f = jnp.where(lane_k < HEAD_DIM, dk_acc + pltpu.roll(dk_acc, HEAD_DIM, 1), 0.0)
            dv_f = jnp.where(lane_k < HEAD_DIM, dv_acc + pltpu.roll(dv_acc, HEAD_DIM, 1), 0.0)
            for acc, col0, car in ((dk_f, K0, kcar_s), (dv_f, V0, vcar_s)):
                cs = pl.ds(g * LANES, LANES)
                co = pl.ds(col0 + g * LANES, LANES)
                if QT > WINDOW:
                    dqkv_ref[pl.ds(0, QT - WINDOW), co] = acc[WINDOW:QT].astype(BF16)
                dqkv_ref[pl.ds(QT - WINDOW, WINDOW), co] = (acc[QT:KW] + car[:, cs]).astype(BF16)
                car[:, cs] = acc[0:WINDOW]
        dsink_ref[...] += dsink

    qs, cur, prev, lse_s = _attn_specs(nt, True)
    return _call(
        body, name="attn_bwd", grid=(nt,), ins=[q, kd, kd, vd, vd, o, do, lse, probs, sinks],
        in_specs=[qs, prev, cur, prev, cur, qs, qs, lse_s,
                  pl.BlockSpec((None, N_KV, REP * QT, KW), lambda i: (nt - 1 - i, 0, 0, 0)),
                  pl.BlockSpec(memory_space=pltpu.SMEM)],
        out_specs=[pl.BlockSpec((QT, XR0), lambda i: (nt - 1 - i, 0)), _const_spec((N_KV, REP * QT, KW)),
                   _const_spec((1, LANES))],
        out_shape=[jax.ShapeDtypeStruct((t_tok, XR0), BF16), jax.ShapeDtypeStruct((N_KV, REP * QT, KW), F32),
                   jax.ShapeDtypeStruct((1, LANES), F32)],
        scratch=[pltpu.VMEM((WINDOW, kvw), F32), pltpu.VMEM((WINDOW, kvw), F32)],
        vmem=VMEM_BIG, carries=carries)


def _bias_tile(table, bmap):
    def body(tab_ref, bm_ref, out_ref):
        bm = bm_ref[...]

        def per_head(hd, carry):
            acc = jnp.full((QT, KW), NEG_INF, F32)
            for b in range(N_BUCKETS):
                acc = jnp.where(bm == b, tab_ref[b, hd], acc)
            out_ref[hd] = acc
            return carry

        lax.fori_loop(0, N_HEADS, per_head, 0)

    return pl.pallas_call(
        body, name="bias_tile", out_shape=jax.ShapeDtypeStruct((N_HEADS, QT, KW), F32),
        in_specs=[pl.BlockSpec(memory_space=pltpu.SMEM), pl.BlockSpec(memory_space=pltpu.VMEM)],
        out_specs=pl.BlockSpec(memory_space=pltpu.VMEM))(table, bmap)


def _bias_grad(ds_acc, bmap):
    def body(ds_ref, bm_ref, out_ref):
        row = lax.broadcasted_iota(jnp.int32, (N_BUCKETS, LANES), 0)
        lane = lax.broadcasted_iota(jnp.int32, (N_BUCKETS, LANES), 1)
        bm = bm_ref[...]

        def per_head(hd, res):
            dsv = ds_ref[hd]
            for b in range(N_BUCKETS):
                val = jnp.sum(jnp.sum(jnp.where(bm == b, dsv, 0.0), axis=0, keepdims=True), axis=1, keepdims=True)
                res = jnp.where((row == b) & (lane == hd), val, res)
            return res

        out_ref[...] = lax.fori_loop(0, N_HEADS, per_head, jnp.zeros((N_BUCKETS, LANES), F32))

    return pl.pallas_call(
        body, name="bias_grad", out_shape=jax.ShapeDtypeStruct((N_BUCKETS, LANES), F32),
        in_specs=[pl.BlockSpec(memory_space=pltpu.VMEM), pl.BlockSpec(memory_space=pltpu.VMEM)],
        out_specs=pl.BlockSpec(memory_space=pltpu.VMEM))(ds_acc, bmap)


MIXOUT = ("w_lru_out", "w_attn_out", "w_o")


def _mix_out_fwd(ya_in, o, gs, h, g, wg, tm=512, carries=()):
    t_tok = h.shape[0]
    nt = t_tok // tm
    groups = _groups_of(MIXOUT)

    def body(ya_ref, o_ref, gs_ref, h_ref, g_ref, wg_ref, hout_ref, yao_ref, ybo_ref, z_ref, wa_s, wb_s, wo_s, sems):
        i = pl.program_id(0)

        @pl.when(i == 0)
        def _():
            _load_weights({groups[0]: wg_ref}, list(zip(MIXOUT, (wa_s, wb_s, wo_s))), sems)

        ya = _nn(ya_ref[...], wa_s[...])
        yb = _nn(o_ref[...], wb_s[...])
        yao_ref[...] = ya.astype(BF16)
        ybo_ref[...] = yb.astype(BF16)
        merged = gs_ref[:, pl.ds(0, D)].astype(F32) * ya + gs_ref[:, pl.ds(D, D)].astype(F32) * yb
        z = _nn(merged.astype(BF16), wo_s[...])
        z_ref[...] = z
        zn, _, _ = _rms_fwd(z, g_ref[...])
        hout_ref[...] = h_ref[...] + zn

    return _call(
        body, name="mix_out_fwd", grid=(nt,), ins=[ya_in, o, gs, h, g, wg[groups[0]]],
        in_specs=[_row_spec(tm, D), _row_spec(tm, D), _row_spec(tm, 2 * D), _row_spec(tm, D), _const_spec((1, D)), ANY],
        out_specs=[_row_spec(tm, D)] * 4,
        out_shape=[jax.ShapeDtypeStruct((t_tok, D), F32), jax.ShapeDtypeStruct((t_tok, D), BF16),
                   jax.ShapeDtypeStruct((t_tok, D), BF16), jax.ShapeDtypeStruct((t_tok, D), F32)],
        scratch=[pltpu.VMEM((D, D), BF16)] * 3 + [DMA((3 * NDEV,))],
        vmem=48 * 2 ** 20, carries=carries)


def _mix_out_bwd(dh, z, ya, yb, gs, g, wg, tm=512, carries=()):
    t_tok = dh.shape[0]
    nt = t_tok // tm
    groups = _groups_of(MIXOUT)

    def body(dh_ref, z_ref, ya_ref, yb_ref, gs_ref, g_ref, wg_ref,
             dyain_ref, do_ref, dgpre_ref, dya_ref, dyb_ref, mg_ref, dz_ref, dg_ref, dbg_ref,
             wa_s, wb_s, wo_s, sems):
        i = pl.program_id(0)

        @pl.when(i == 0)
        def _():
            _load_weights({groups[0]: wg_ref}, list(zip(MIXOUT, (wa_s, wb_s, wo_s))), sems)
            dg_ref[...] = jnp.zeros_like(dg_ref)
            dbg_ref[...] = jnp.zeros_like(dbg_ref)

        gv = g_ref[...]
        _, zh, r = _rms_fwd(z_ref[...], gv)
        dz, dg = _rms_bwd(dh_ref[...], zh, r, gv)
        dg_ref[...] += dg
        dzb = dz.astype(BF16)
        dz_ref[...] = dzb
        ga, gb = gs_ref[:, pl.ds(0, D)].astype(F32), gs_ref[:, pl.ds(D, D)].astype(F32)
        ya_v, yb_v = ya_ref[...].astype(F32), yb_ref[...].astype(F32)
        mg_ref[...] = (ga * ya_v + gb * yb_v).astype(BF16)
        dm = _nt(dzb, wo_s[...])
        dga = dm * ya_v * ga * (1.0 - ga)
        dgb = dm * yb_v * gb * (1.0 - gb)
        dgpre_ref[:, pl.ds(0, D)] = dga.astype(BF16)
        dgpre_ref[:, pl.ds(D, D)] = dgb.astype(BF16)
        dbg_ref[:, pl.ds(0, D)] += jnp.sum(dga, axis=0, keepdims=True)
        dbg_ref[:, pl.ds(D, D)] += jnp.sum(dgb, axis=0, keepdims=True)
        dya = (dm * ga).astype(BF16)
        dyb = (dm * gb).astype(BF16)
        dya_ref[...] = dya
        dyb_ref[...] = dyb
        dyain_ref[...] = _nt(dya, wa_s[...])
        do_ref[...] = _nt(dyb, wb_s[...]).astype(BF16)

    bf = jax.ShapeDtypeStruct((t_tok, D), BF16)
    return _call(
        body, name="mix_out_bwd", grid=(nt,), ins=[dh, z, ya, yb, gs, g, wg[groups[0]]],
        in_specs=[_row_spec(tm, D)] * 4 + [_row_spec(tm, 2 * D), _const_spec((1, D)), ANY],
        out_specs=[_row_spec(tm, D), _row_spec(tm, D), _row_spec(tm, 2 * D)] + [_row_spec(tm, D)] * 4
        + [_const_spec((1, D)), _const_spec((1, 2 * D))],
        out_shape=[jax.ShapeDtypeStruct((t_tok, D), F32), bf, jax.ShapeDtypeStruct((t_tok, 2 * D), BF16), bf, bf, bf, bf,
                   jax.ShapeDtypeStruct((1, D), F32), jax.ShapeDtypeStruct((1, 2 * D), F32)],
        scratch=[pltpu.VMEM((D, D), BF16)] * 3 + [DMA((3 * NDEV,))],
        vmem=VMEM_BIG, carries=carries)


def _sum_parts(parts_list):
    n = len(parts_list)
    _, r, c = parts_list[0].shape
    tc = 256

    def body(*refs):
        for p_ref, o_ref in zip(refs[:n], refs[n:]):
            acc = p_ref[0].astype(F32)
            for s in range(1, NDEV):
                acc = acc + p_ref[s].astype(F32)
            o_ref[...] = acc

    return pl.pallas_call(
        body, name=f"sum_parts_{r}", grid=(c // tc,),
        in_specs=[pl.BlockSpec((NDEV, r, tc), lambda i: (0, 0, i))] * n,
        out_specs=[pl.BlockSpec((r, tc), lambda i: (0, i))] * n,
        out_shape=[jax.ShapeDtypeStruct((r, c), F32)] * n,
        compiler_params=pltpu.CompilerParams(dimension_semantics=("arbitrary",), vmem_limit_bytes=48 * 2 ** 20),
    )(*parts_list)


def _adamw_math(w, g, m, v):
    m = ADAM_B1 * m + (1.0 - ADAM_B1) * g
    v = ADAM_B2 * v + (1.0 - ADAM_B2) * (g * g)
    m_hat = m / (1.0 - ADAM_B1 ** ADAM_STEP)
    v_hat = v / (1.0 - ADAM_B2 ** ADAM_STEP)
    delta = -ADAM_LR * (m_hat / (jnp.sqrt(v_hat) + ADAM_EPS) + ADAM_WD * w)
    return delta, m, v


def _sum_ready(parts_list, name, carries=(), chip_sums=()):
    n = len(parts_list)
    c = parts_list[0].shape[2]
    tc = 2 * LANES

    def body(*refs):
        for k, (p_ref, o_ref) in enumerate(zip(refs[:n], refs[n:])):
            if k in chip_sums:
                core = lax.axis_index("c")
                slots = [2 * ch + core for ch in range(NDEV // 2)]
            else:
                slots = list(range(NDEV))
            g = p_ref[slots[0]].astype(F32)
            for s in slots[1:]:
                g = g + p_ref[s].astype(F32)
            o_ref[...] = g

    return _call(
        body, name=name, grid=(c // tc,), ins=list(parts_list),
        in_specs=[pl.BlockSpec((NDEV, p.shape[1], tc), lambda i: (0, 0, i)) for p in parts_list],
        out_shape=[jax.ShapeDtypeStruct((p.shape[1], c), F32) for p in parts_list],
        out_specs=[pl.BlockSpec((p.shape[1], tc), lambda i: (0, i)) for p in parts_list],
        vmem=48 * 2 ** 20, carries=carries)


def _adamw_cols(items, name):
    n = len(items)
    c = items[0][1].shape[1]
    tc = LANES

    def body(*refs):
        for k in range(n):
            g_ref, w_ref, m_ref, v_ref = refs[4 * k:4 * k + 4]
            go_ref, d_ref, nm_ref, nv_ref = refs[4 * n + 4 * k:4 * n + 4 * k + 4]
            g = g_ref[...]
            d, m, v = _adamw_math(w_ref[...], g, m_ref[...], v_ref[...])
            go_ref[...] = g
            d_ref[...] = d
            nm_ref[...] = m
            nv_ref[...] = v

    ins, specs, out_shape = [], [], []
    for g, w, m, v in items:
        r = w.shape[0]
        ins += [g, w, m, v]
        specs += [pl.BlockSpec((r, tc), lambda i: (0, i))] * 4
        out_shape += [jax.ShapeDtypeStruct((r, c), F32)] * 4
    outs, _ = _call(body, name=name, grid=(c // tc,), ins=ins, in_specs=specs, out_shape=out_shape,
                    out_specs=specs, vmem=48 * 2 ** 20)
    return [tuple(outs[4 * k:4 * k + 4]) for k in range(n)]


def _adamw_body(n):
    def body(*refs):
        for k in range(n):
            g_ref, w_ref, m_ref, v_ref = refs[4 * k:4 * k + 4]
            d_ref, nm_ref, nv_ref = refs[4 * n + 3 * k:4 * n + 3 * k + 3]
            d, m, v = _adamw_math(w_ref[...], g_ref[...], m_ref[...], v_ref[...])
            d_ref[...] = d
            nm_ref[...] = m
            nv_ref[...] = v
    return body


def _adamw(items):
    n = len(items)
    r, c = items[0][0].shape
    tr = r if r * c <= 2 ** 18 else max(t for t in range(8, 65, 8) if r % t == 0)
    spec = pl.BlockSpec((tr, c), lambda i: (i, 0))
    outs = pl.pallas_call(
        _adamw_body(n), name=f"adamw_{r}x{c}", grid=(r // tr,),
        in_specs=[spec] * (4 * n), out_specs=[spec] * (3 * n),
        out_shape=[jax.ShapeDtypeStruct((r, c), F32)] * (3 * n),
        compiler_params=pltpu.CompilerParams(dimension_semantics=("arbitrary",), vmem_limit_bytes=40 * 2 ** 20),
    )(*[a for it in items for a in it])
    return [tuple(outs[3 * k:3 * k + 3]) for k in range(n)]


def _adamw_small(items):
    n = len(items)
    vm = pl.BlockSpec(memory_space=pltpu.VMEM)
    outs = pl.pallas_call(
        _adamw_body(n), name="adamw_small", in_specs=[vm] * (4 * n), out_specs=[vm] * (3 * n),
        out_shape=[jax.ShapeDtypeStruct(it[1].shape, F32) for it in items for _ in range(3)],
    )(*[a for it in items for a in it])
    return [tuple(outs[3 * k:3 * k + 3]) for k in range(n)]


def _pack_small(arrs):
    rows, offs = [], []
    total = 0
    for a in arrs:
        flat = a.reshape(-1).astype(F32)
        nr = -(-flat.shape[0] // LANES)
        flat = jnp.pad(flat, (0, nr * LANES - flat.shape[0]))
        rows.append(flat.reshape(nr, LANES))
        offs.append((total, nr))
        total += nr
    pad = -total % 8
    if pad:
        rows.append(jnp.zeros((pad, LANES), F32))
    return jnp.concatenate(rows, axis=0), offs


def _unpack_small(pack, offs, shapes):
    out = []
    for (o, nr), shp in zip(offs, shapes):
        size = int(np.prod(shp))
        out.append(pack[o:o + nr].reshape(-1)[:size].reshape(shp))
    return out


def _sum_small(gathered, rows):
    def body(p_ref, o_ref):
        acc = p_ref[pl.ds(0, rows), :]
        for s in range(1, NDEV):
            acc = acc + p_ref[pl.ds(s * rows, rows), :]
        o_ref[...] = acc

    return pl.pallas_call(
        body, name="sum_small", out_shape=jax.ShapeDtypeStruct((rows, LANES), F32),
        in_specs=[pl.BlockSpec(memory_space=pltpu.VMEM)], out_specs=pl.BlockSpec(memory_space=pltpu.VMEM))(gathered)


def _block_diag(w):
    w = w.reshape(D // LANES, 2, LRU_BLOCK, LRU_BLOCK)
    z = jnp.zeros((D // LANES, LRU_BLOCK, LRU_BLOCK), w.dtype)
    top = jnp.concatenate([w[:, 0], z], axis=2)
    bot = jnp.concatenate([z, w[:, 1]], axis=2)
    return jnp.concatenate([top, bot], axis=1)


def _block_diag_grad(dw):
    a = dw[:, :LRU_BLOCK, :LRU_BLOCK]
    b = dw[:, LRU_BLOCK:, LRU_BLOCK:]
    return jnp.stack([a, b], axis=1).reshape(D // LRU_BLOCK, LRU_BLOCK, LRU_BLOCK)


def kernel(x, ffn1_pre_g, ffn1_w1, ffn1_w3, ffn1_w2, ffn1_post_g, mix_pre_g, w_in, conv_w, conv_b, rg_a_w, rg_a_b, rg_x_w, rg_x_b, lru_lambda, w_lru_out, attn_sinks, rel_bias, w_attn_out, w_gate, b_gate, w_o, mix_post_g, ffn2_pre_g, ffn2_w1, ffn2_w3, ffn2_w2, ffn2_post_g, loss_target, m_ffn1_pre_g, m_ffn1_w1, m_ffn1_w3, m_ffn1_w2, m_ffn1_post_g, m_mix_pre_g, m_w_in, m_conv_w, m_conv_b, m_rg_a_w, m_rg_a_b, m_rg_x_w, m_rg_x_b, m_lru_lambda, m_w_lru_out, m_attn_sinks, m_rel_bias, m_w_attn_out, m_w_gate, m_b_gate, m_w_o, m_mix_post_g, m_ffn2_pre_g, m_ffn2_w1, m_ffn2_w3, m_ffn2_w2, m_ffn2_post_g, v_ffn1_pre_g, v_ffn1_w1, v_ffn1_w3, v_ffn1_w2, v_ffn1_post_g, v_mix_pre_g, v_w_in, v_conv_w, v_conv_b, v_rg_a_w, v_rg_a_b, v_rg_x_w, v_rg_x_b, v_lru_lambda, v_w_lru_out, v_attn_sinks, v_rel_bias, v_w_attn_out, v_w_gate, v_b_gate, v_w_o, v_mix_post_g, v_ffn2_pre_g, v_ffn2_w1, v_ffn2_w3, v_ffn2_w2, v_ffn2_post_g):
    names = ["ffn1_pre_g", "ffn1_w1", "ffn1_w3", "ffn1_w2", "ffn1_post_g", "mix_pre_g", "w_in", "conv_w", "conv_b",
             "rg_a_w", "rg_a_b", "rg_x_w", "rg_x_b", "lru_lambda", "w_lru_out", "attn_sinks", "rel_bias", "w_attn_out",
             "w_gate", "b_gate", "w_o", "mix_post_g", "ffn2_pre_g", "ffn2_w1", "ffn2_w3", "ffn2_w2", "ffn2_post_g"]
    ws = dict(zip(names, (ffn1_pre_g, ffn1_w1, ffn1_w3, ffn1_w2, ffn1_post_g, mix_pre_g, w_in, conv_w, conv_b, rg_a_w,
                          rg_a_b, rg_x_w, rg_x_b, lru_lambda, w_lru_out, attn_sinks, rel_bias, w_attn_out, w_gate,
                          b_gate, w_o, mix_post_g, ffn2_pre_g, ffn2_w1, ffn2_w3, ffn2_w2, ffn2_post_g)))
    ms = dict(zip(names, (m_ffn1_pre_g, m_ffn1_w1, m_ffn1_w3, m_ffn1_w2, m_ffn1_post_g, m_mix_pre_g, m_w_in, m_conv_w,
                          m_conv_b, m_rg_a_w, m_rg_a_b, m_rg_x_w, m_rg_x_b, m_lru_lambda, m_w_lru_out, m_attn_sinks,
                          m_rel_bias, m_w_attn_out, m_w_gate, m_b_gate, m_w_o, m_mix_post_g, m_ffn2_pre_g, m_ffn2_w1,
                          m_ffn2_w3, m_ffn2_w2, m_ffn2_post_g)))
    vs = dict(zip(names, (v_ffn1_pre_g, v_ffn1_w1, v_ffn1_w3, v_ffn1_w2, v_ffn1_post_g, v_mix_pre_g, v_w_in, v_conv_w,
                          v_conv_b, v_rg_a_w, v_rg_a_b, v_rg_x_w, v_rg_x_b, v_lru_lambda, v_w_lru_out, v_attn_sinks,
                          v_rel_bias, v_w_attn_out, v_w_gate, v_b_gate, v_w_o, v_mix_post_g, v_ffn2_pre_g, v_ffn2_w1,
                          v_ffn2_w3, v_ffn2_w2, v_ffn2_post_g)))
    me = 4 * lax.axis_index("x") + 2 * lax.axis_index("y") + lax.axis_index("c")
    vec = lambda n: ws[n].reshape(1, -1)

    def shard2d(name):
        if name == "conv":
            row = lax.bitcast_convert_type(conv_w.reshape(CONV_WIDTH, LANES), BF16).reshape(1, D)
            return jnp.concatenate([row, jnp.zeros((LOC["conv"][2] - 1, D), BF16)], axis=0)
        a = ws[name].reshape(ws[name].shape[-2], ws[name].shape[-1])
        return (a.T if name in COL_SHARDED else a).astype(BF16)

    packs = {g: jnp.concatenate([shard2d(n) for n, _ in members], axis=0) for g, members in GROUPS}

    wg = {}
    _, ((wg["ffn1"],),) = _call(None, name="allgather_ffn1", grid=(), ins=[], in_specs=[], out_shape=[], out_specs=[],
                                carries=[_AllGatherTree(packs["ffn1"])])
    conv_rows = wg["ffn1"].reshape(NDEV, GROUP_ROWS["ffn1"], D)[:, LOC["conv"][1]]
    cw = jnp.transpose(lax.bitcast_convert_type(conv_rows.reshape(NDEV, CONV_WIDTH, LANES, 2), F32),
                       (1, 0, 2)).reshape(CONV_WIDTH, D)
    bmap = jnp.asarray(_bucket_map())
    bias = _bias_tile(rel_bias, bmap).reshape(N_KV, REP * QT, KW)
    sinks = attn_sinks.reshape(N_HEADS)
    wa_bd = _block_diag(rg_a_w.reshape(D // LRU_BLOCK, LRU_BLOCK, LRU_BLOCK)).astype(BF16)
    wx_bd = _block_diag(rg_x_w.reshape(D // LRU_BLOCK, LRU_BLOCK, LRU_BLOCK)).astype(BF16)
    lru_args = (cw, vec("conv_b"), wa_bd, wx_bd, vec("rg_a_b"), vec("rg_x_b"), vec("lru_lambda"))
    x2, tgt = x[0], loss_target[0]

    (h1, a1, b1, f1, nb1), ((wg["mixin"],),) = _ffn_fwd(
        x2, vec("ffn1_pre_g"), vec("ffn1_post_g"), wg, ("ffn1_w1", "ffn1_w3", "ffn1_w2"),
        carries=[_AllGather(packs["mixin"])])
    (q, kd, vd, xr, xg, gs, ub), ((wg["mixout"],),) = _mix_proj_fwd(
        h1, vec("mix_pre_g"), vec("b_gate"), wg, carries=[_AllGather(packs["mixout"])])
    (ya_in, hseq, *gates), ((wg["ffn2a"],),) = _lru_fwd(xr, xg, *lru_args, carries=[_AllGather(packs["ffn2a"])])
    (o, lse, probs), ((wg["ffn2b"],),) = _attn_fwd(q, kd, vd, bias, sinks, carries=[_AllGather(packs["ffn2b"])])
    (h2, ya, yb, z), _ = _mix_out_fwd(ya_in, o, gs, h1, vec("mix_post_g"), wg)
    (_, a2, b2, f2, nb2, dy, loss_part), _ = _ffn_fwd(h2, vec("ffn2_pre_g"), vec("ffn2_post_g"), wg,
                                                     ("ffn2_w1", "ffn2_w3", "ffn2_w2"), target=tgt)

    gsm, parts = {}, {}
    rs = lambda *grads_: [_ReduceScatterSend(list(grads_))]
    ffr = FF // NDEV
    (dab, s_act, dfb, gsm["ffn2_post_g"]), _ = _ffn_bwd_a(dy, f2, a2, b2, vec("ffn2_post_g"), wg, "ffn2_w2")
    g_w2, _ = _dw(s_act, dfb, FF // 2, "dw_ffn2_w2")
    g_w13, ((parts["ffn2_w2"],),) = _dw(dab, nb2, D // 2, "dw_ffn2_w13", carries=rs(g_w2))
    (dh2, gsm["ffn2_pre_g"]), ((parts["ffn2_w1"],),) = _ffn_bwd_dx(
        dab, h2, dy, vec("ffn2_pre_g"), wg, "ffn2_w1", "ffn2_w3", carries=rs((g_w13, 0, ffr, ffr)))
    (dya_in, do, dgpre, dya, dyb, mg, dzb, gsm["mix_post_g"], gsm["b_gate"]), ((parts["ffn2_w3"],),) = _mix_out_bwd(
        dh2, z, ya, yb, gs, vec("mix_post_g"), wg, carries=rs((g_w13, FF, ffr, ffr)))
    g_wa, _ = _dw(ya_in, dya, D // 2, "dw_w_lru_out")
    g_wb, _ = _dw(o, dyb, D // 2, "dw_w_attn_out")
    g_wo, _ = _dw(mg, dzb, D // 2, "dw_w_o")
    g_wgate, _ = _dw(dgpre, ub, D // 2, "dw_w_gate")
    (dqkv, ds_acc, dsink), (mixout_parts,) = _attn_bwd(
        q, kd, vd, o, do, lse, probs, sinks, carries=rs(g_wa, g_wb, g_wo, g_wgate))
    parts["w_lru_out"], parts["w_attn_out"], parts["w_o"], parts["w_gate"] = mixout_parts
    (dxr, dxg, dvec, dwa, dwx), _ = _lru_bwd(dya_in, xr, xg, hseq, gates, *lru_args)
    g_qkv, _ = _dw(dqkv, ub, D // 2, "dw_w_in_qkv")
    g_xr, _ = _dw(dxr, ub, D // 2, "dw_w_in_xr")
    g_xg, _ = _dw(dxg, ub, D // 2, "dw_w_in_xg")
    g_win = jnp.concatenate(
        [g_qkv[:D]] + [g_qkv[c0 + gi * LANES:c0 + gi * LANES + HEAD_DIM] for c0 in (K0, V0) for gi in range(N_KV)]
        + [g_xr, g_xg], axis=0)
    wir, wir_a = IN_W // NDEV, 336
    (dh1, gsm["mix_pre_g"]), ((win_a,),) = _mix_proj_bwd(
        dqkv, dxr, dxg, dgpre, h1, dh2, vec("mix_pre_g"), wg, carries=rs((g_win, 0, wir, wir_a)))
    gsm["conv_w"] = dvec[0:CONV_WIDTH]
    gsm["conv_b"], gsm["rg_a_b"], gsm["rg_x_b"], gsm["lru_lambda"] = dvec[4], dvec[5], dvec[6], dvec[7]
    gsm["rg_a_w"] = _block_diag_grad(dwa)
    gsm["rg_x_w"] = _block_diag_grad(dwx)
    gsm["attn_sinks"] = dsink[0, :N_HEADS]
    gsm["rel_bias"] = _bias_grad(ds_acc.reshape(N_HEADS, QT, KW), bmap)[:, :N_HEADS]
    late = ("ffn1_post_g", "ffn1_pre_g")
    early = tuple(n for n in SMALL if n not in late)
    early_pack, early_offs = _pack_small([gsm[n] for n in early])
    (dab, s_act, dfb, gsm["ffn1_post_g"]), ((win_b,), (early_all,)) = _ffn_bwd_a(
        dh1, f1, a1, b1, vec("ffn1_post_g"), wg, "ffn1_w2",
        carries=rs((g_win, wir_a, wir, wir - wir_a)) + [_AllGather(early_pack)])
    parts["w_in"] = jnp.concatenate([win_a, win_b], axis=1)
    g_w2, _ = _dw(s_act, dfb, FF // 2, "dw_ffn1_w2")
    g_w13, ((parts["ffn1_w2"],),) = _dw(dab, nb1, D // 2, "dw_ffn1_w13", carries=rs(g_w2))
    chip_w13 = _pair_sum(g_w13, [(0, ffr), (FF, ffr)])
    (grad_x, gsm["ffn1_pre_g"]), ((parts["ffn1_w1"], parts["ffn1_w3"]),) = _ffn_bwd_dx(
        dab, x2, dh1, vec("ffn1_pre_g"), wg, "ffn1_w1", "ffn1_w3",
        carries=[_ReduceScatterSend([(chip_w13, 0, ffr, ffr), (chip_w13, FF, ffr, ffr)], chip_sums=True)])
    late_pack, late_offs = _pack_small([gsm[n] for n in late] + [loss_part])

    grads, delta, new_m, new_v = {}, {}, {}, {}
    t_form = [n for n in COL_SHARDED if ws[n].shape[-1] % LANES]
    view = {n: (lambda a: a.reshape(a.shape[-2], a.shape[-1]).T) if n in t_form
            else (lambda a: a.reshape(a.shape[-2], a.shape[-1])) for n in BIG}
    unview = {n: (lambda a: a.T) if n in t_form else (lambda a: a) for n in BIG}

    group = [n for n in BIG if n != "w_gate"]
    sums, ((late_all,),) = _sum_ready([parts[n] for n in group], "sum_ready", [_AllGather(late_pack)],
                                      chip_sums=(group.index("ffn1_w1"), group.index("ffn1_w3")))
    res = _adamw_cols([(g, view[n](ws[n]), view[n](ms[n]), view[n](vs[n])) for n, g in zip(group, sums)], "adamw_big")
    for n, quad in zip(group, res):
        grads[n], delta[n], new_m[n], new_v[n] = (unview[n](a) for a in quad)
    (g_gate_t,) = _sum_parts([parts["w_gate"]])
    grads["w_gate"] = g_gate_t.T
    ((delta["w_gate"], new_m["w_gate"], new_v["w_gate"]),) = _adamw(
        [(grads["w_gate"], view["w_gate"](ws["w_gate"]), view["w_gate"](ms["w_gate"]), view["w_gate"](vs["w_gate"]))])
    for group, gathered, offs in ((early, early_all, early_offs), (late, late_all, late_offs)):
        total = _sum_small(gathered, gathered.shape[0] // NDEV)
        shapes = [(CONV_WIDTH, D) if n == "conv_w" else ws[n].shape for n in group]
        if group is late:
            shapes = shapes + [(1, LANES)]
        unpacked = _unpack_small(total, offs, shapes)
        if group is late:
            loss = unpacked.pop()[0, 0]
        for n, g in zip(group, unpacked):
            grads[n] = g
    grads["conv_w"] = lax.dynamic_slice(grads["conv_w"], (0, me * LANES), (CONV_WIDTH, LANES)).reshape(conv_w.shape)

    flat2d = lambda a: a.reshape(-1, a.shape[-1])
    res = _adamw_small([(flat2d(grads[n].reshape(ws[n].shape)), flat2d(ws[n]), flat2d(ms[n]), flat2d(vs[n]))
                        for n in SMALL])
    for n, (d_, m_, v_) in zip(SMALL, res):
        delta[n], new_m[n], new_v[n] = d_, m_, v_

    outs = [loss, grad_x.reshape(x.shape)]
    for src in (grads, delta, new_m, new_v):
        outs += [src[n].reshape(ws[n].shape) for n in names]
    return tuple(outs)
```
